```python
import jax, jax.numpy as jnp
from jax import lax
import numpy as np

D_MODEL = 1024
BATCH = 8
SEQ = 4096
DEPTH = 2

GRID_W = 64
CTX_LEN = 256
HEAD_DIM = 64
N_Q_HEADS = 8
N_KV_HEADS = 2
Q_GROUP = N_Q_HEADS // N_KV_HEADS
ATTN_WIDTH = N_Q_HEADS * HEAD_DIM
KV_WIDTH = N_KV_HEADS * HEAD_DIM
CHUNK = 128
N_SG_GROUPS = 4
SG_WIDTH = D_MODEL - ATTN_WIDTH
SG_GROUP_DIM = SG_WIDTH // N_SG_GROUPS
IN_WIDTH = ATTN_WIDTH + 2 * KV_WIDTH + 2 * SG_WIDTH
Q_BLOCK = 128
ROPE_THETA = 10000.0
AXIS_ROPE_DIM = HEAD_DIM // 2
CONV_WIDTH = 31
D_FF = ((8 * D_MODEL // 3 + 255) // 256) * 256
N_EVEN = (DEPTH + 1) // 2
N_ODD = DEPTH // 2
EPS = 1e-6

kernel_name = "hybrid_attn_sgmlp_conformer_dit"


def rms_norm(x, g):
    xf = x.astype(jnp.float32)
    y = xf * lax.rsqrt(jnp.mean(xf * xf, axis=-1, keepdims=True) + EPS)
    return (y * g.astype(jnp.float32)).astype(x.dtype)


def layer_norm(x, g=None, b=None):
    xf = x.astype(jnp.float32)
    mu = jnp.mean(xf, axis=-1, keepdims=True)
    var = jnp.mean(jnp.square(xf - mu), axis=-1, keepdims=True)
    y = (xf - mu) * lax.rsqrt(var + EPS)
    if g is not None:
        y = y * g.astype(jnp.float32) + b.astype(jnp.float32)
    return y.astype(x.dtype)


def axial_rope_tables(n):
    rows = n // GRID_W
    row = jnp.broadcast_to(jnp.arange(rows)[:, None], (rows, GRID_W)).reshape(-1).astype(jnp.float32)
    col = jnp.broadcast_to(jnp.arange(GRID_W)[None, :], (rows, GRID_W)).reshape(-1).astype(jnp.float32)
    inv = ROPE_THETA ** (-jnp.arange(0, AXIS_ROPE_DIM, 2, dtype=jnp.float32) / AXIS_ROPE_DIM)
    ang_r = row[:, None] * inv[None, :]
    ang_c = col[:, None] * inv[None, :]
    return (jnp.cos(ang_r), jnp.sin(ang_r), jnp.cos(ang_c), jnp.sin(ang_c))


def rotate_half(x, cos, sin):
    x1, x2 = jnp.split(x, 2, axis=-1)
    cos = cos[None, :, None, :]
    sin = sin[None, :, None, :]
    return jnp.concatenate([x1 * cos - x2 * sin, x2 * cos + x1 * sin], axis=-1)


def apply_axial_rope(x, tables):
    cos_r, sin_r, cos_c, sin_c = tables
    xf = x.astype(jnp.float32)
    xr, xc = jnp.split(xf, 2, axis=-1)
    out = jnp.concatenate([rotate_half(xr, cos_r, sin_r), rotate_half(xc, cos_c, sin_c)], axis=-1)
    return out.astype(x.dtype)


def block_attention(q, k, v):
    b, n = q.shape[0], q.shape[1]
    nb = n // Q_BLOCK
    qb = (q * HEAD_DIM ** -0.5).reshape(b, nb, Q_BLOCK, N_KV_HEADS, Q_GROUP, HEAD_DIM)
    qb = qb.transpose(1, 0, 2, 3, 4, 5)

    def one_block(q_blk):
        s = jnp.einsum('bqkgd,bskd->bkgqs', q_blk, k).astype(jnp.float32)
        p = jax.nn.softmax(s, axis=-1).astype(v.dtype)
        return jnp.einsum('bkgqs,bskd->bqkgd', p, v)

    o = lax.map(one_block, qb)
    return o.transpose(1, 0, 2, 3, 4, 5).reshape(b, n, ATTN_WIDTH)


def spatial_gating(u, v, w_sp, b_sp):
    b, n = u.shape[0], u.shape[1]
    shape = (b, n // CHUNK, CHUNK, N_SG_GROUPS, SG_GROUP_DIM)
    vn = layer_norm(v.reshape(shape))
    mixed = jnp.einsum('gpq,bmqgc->bmpgc', w_sp, vn) + b_sp.T[:, :, None]
    return (u.reshape(shape) * mixed).reshape(b, n, SG_WIDTH)


def split_in_proj(p):
    o1 = ATTN_WIDTH
    o2 = o1 + KV_WIDTH
    o3 = o2 + KV_WIDTH
    o4 = o3 + SG_WIDTH
    return jnp.split(p, [o1, o2, o3, o4], axis=-1)


def even_mixer(xm, xc, w_in, q_gain, k_gain, w_sp, b_sp, w_out, rope, ctx_out):
    b, n, _ = xm.shape
    bc, m, _ = xc.shape
    q, k, v, su, sv = split_in_proj(xm @ w_in)
    q = apply_axial_rope(rms_norm(q.reshape(b, n, N_Q_HEADS, HEAD_DIM), q_gain), rope)
    k = apply_axial_rope(rms_norm(k.reshape(b, n, N_KV_HEADS, HEAD_DIM), k_gain), rope)
    v = v.reshape(b, n, N_KV_HEADS, HEAD_DIM)
    if ctx_out:
        qc, kc, vc, suc, svc = split_in_proj(xc @ w_in)
    else:
        kc, vc = jnp.split(xc @ w_in[:, ATTN_WIDTH:ATTN_WIDTH + 2 * KV_WIDTH], 2, axis=-1)
    kc = rms_norm(kc.reshape(bc, m, N_KV_HEADS, HEAD_DIM), k_gain)
    vc = vc.reshape(bc, m, N_KV_HEADS, HEAD_DIM)
    attn = block_attention(q, jnp.concatenate([kc, k], axis=1), jnp.concatenate([vc, v], axis=1))
    sg = spatial_gating(jax.nn.gelu(su), jax.nn.gelu(sv), w_sp, b_sp)
    y = jnp.concatenate([attn, sg], axis=-1) @ w_out
    yc = None
    if ctx_out:
        qc = rms_norm(qc.reshape(bc, m, N_Q_HEADS, HEAD_DIM), q_gain)
        attn_c = block_attention(qc, kc, vc)
        sg_c = spatial_gating(jax.nn.gelu(suc), jax.nn.gelu(svc), w_sp, b_sp)
        yc = jnp.concatenate([attn_c, sg_c], axis=-1) @ w_out
    return y, yc


def conformer_conv(x, w_pw1, b_pw1, w_dw, b_dw, ln_g, ln_b, w_pw2, b_pw2):
    a, gate = jnp.split(x @ w_pw1 + b_pw1, 2, axis=-1)
    h = a * jax.nn.sigmoid(gate)
    h = lax.conv_general_dilated(h, w_dw[:, None, :].astype(h.dtype), window_strides=(1,),
                                 padding=[(CONV_WIDTH // 2, CONV_WIDTH // 2)],
                                 dimension_numbers=('NWC', 'WIO', 'NWC'),
                                 feature_group_count=D_MODEL) + b_dw
    h = jax.nn.silu(layer_norm(h, ln_g, ln_b))
    return h @ w_pw2 + b_pw2


def swiglu_ffn(x, w_in, w_out):
    g, u = jnp.split(x @ w_in, 2, axis=-1)
    return (jax.nn.silu(g) * u) @ w_out


def adaln(cvec, w_mod, b_mod):
    return jnp.split(jax.nn.silu(cvec) @ w_mod + b_mod, 6, axis=-1)


def _fwd_setup_inputs(seed: int = 0) -> dict:
    key = jax.random.key(seed)
    ks = iter(jax.random.split(key, 32))
    f32 = jnp.float32

    def nrm(shape, scale):
        return jax.random.normal(next(ks), shape, f32) * scale

    D = D_MODEL
    return {
        'x': nrm((BATCH, SEQ, D), 1.0),
        'c': nrm((BATCH, D), 1.0),
        'ctx': nrm((BATCH, CTX_LEN, D), 1.0),
        'c_ctx': nrm((D,), 1.0),
        'w_mod': nrm((DEPTH, D, 6 * D), 0.5 * D ** -0.5),
        'b_mod': nrm((DEPTH, 6 * D), 0.01),
        'g_mix': 1.0 + nrm((DEPTH, D), 0.01),
        'g_ffn': 1.0 + nrm((DEPTH, D), 0.01),
        'w_ffn_in': nrm((DEPTH, D, 2 * D_FF), D ** -0.5),
        'w_ffn_out': nrm((DEPTH, D_FF, D), D_FF ** -0.5),
        'w_in': nrm((N_EVEN, D, IN_WIDTH), D ** -0.5),
        'q_gain': 1.0 + nrm((N_EVEN, HEAD_DIM), 0.01),
        'k_gain': 1.0 + nrm((N_EVEN, HEAD_DIM), 0.01),
        'w_sp': nrm((N_EVEN, N_SG_GROUPS, CHUNK, CHUNK), CHUNK ** -0.5),
        'b_sp': 1.0 + nrm((N_EVEN, N_SG_GROUPS, CHUNK), 0.01),
        'w_out': nrm((N_EVEN, ATTN_WIDTH + SG_WIDTH, D), (ATTN_WIDTH + SG_WIDTH) ** -0.5),
        'w_pw1': nrm((N_ODD, D, 2 * D), D ** -0.5),
        'b_pw1': nrm((N_ODD, 2 * D), 0.01),
        'w_dw': nrm((N_ODD, CONV_WIDTH, D), CONV_WIDTH ** -0.5),
        'b_dw': nrm((N_ODD, D), 0.01),
        'ln_g': 1.0 + nrm((N_ODD, D), 0.01),
        'ln_b': nrm((N_ODD, D), 0.01),
        'w_pw2': nrm((N_ODD, D, D), D ** -0.5),
        'b_pw2': nrm((N_ODD, D), 0.01),
        'g_final': 1.0 + nrm((D,), 0.01),
    }


def _fwd_reference(x, c, ctx, c_ctx, w_mod, b_mod, g_mix, g_ffn, w_ffn_in, w_ffn_out,
              w_in, q_gain, k_gain, w_sp, b_sp, w_out,
              w_pw1, b_pw1, w_dw, b_dw, ln_g, ln_b, w_pw2, b_pw2, g_final):
    n = x.shape[1]
    rope = axial_rope_tables(n)
    h, hc = x, ctx
    for l in range(DEPTH):
        even = (l % 2 == 0)
        i = l // 2
        ctx_after = any(j % 2 == 0 for j in range(l + 1, DEPTH))
        sh1, sc1, gt1, sh2, sc2, gt2 = [t[:, None, :] for t in adaln(c, w_mod[l], b_mod[l])]
        xm = rms_norm(h, g_mix[l]) * (1.0 + sc1) + sh1
        if even or ctx_after:
            csh1, csc1, cgt1, csh2, csc2, cgt2 = adaln(c_ctx, w_mod[l], b_mod[l])
            xc = rms_norm(hc, g_mix[l]) * (1.0 + csc1) + csh1
        if even:
            y, yc = even_mixer(xm, xc, w_in[i], q_gain[i], k_gain[i], w_sp[i], b_sp[i], w_out[i],
                               rope, ctx_after)
        else:
            conv_p = (w_pw1[i], b_pw1[i], w_dw[i], b_dw[i], ln_g[i], ln_b[i], w_pw2[i], b_pw2[i])
            y = conformer_conv(xm, *conv_p)
            yc = conformer_conv(xc, *conv_p) if ctx_after else None
        h = h + gt1 * y
        h = h + gt2 * swiglu_ffn(rms_norm(h, g_ffn[l]) * (1.0 + sc2) + sh2, w_ffn_in[l], w_ffn_out[l])
        if ctx_after:
            hc = hc + cgt1 * yc
            hc = hc + cgt2 * swiglu_ffn(rms_norm(hc, g_ffn[l]) * (1.0 + csc2) + csh2,
                                        w_ffn_in[l], w_ffn_out[l])
    return rms_norm(h, g_final)


import jax as _jax
import jax.numpy as _jnp

TWIN_FORMAT = 'train_step'
FWD_PARAMS = ['x', 'c', 'ctx', 'c_ctx', 'w_mod', 'b_mod', 'g_mix', 'g_ffn', 'w_ffn_in', 'w_ffn_out', 'w_in', 'q_gain', 'k_gain', 'w_sp', 'b_sp', 'w_out', 'w_pw1', 'b_pw1', 'w_dw', 'b_dw', 'ln_g', 'ln_b', 'w_pw2', 'b_pw2', 'g_final']
TWIN_WEIGHTS = ['c_ctx', 'w_mod', 'b_mod', 'g_mix', 'g_ffn', 'w_ffn_in', 'w_ffn_out', 'w_in', 'q_gain', 'k_gain', 'w_sp', 'b_sp', 'w_out', 'w_pw1', 'b_pw1', 'w_dw', 'b_dw', 'ln_g', 'ln_b', 'w_pw2', 'b_pw2', 'g_final']
TWIN_DIFF_INPUT = 'x'
TWIN_INPUTS = ['x', 'c', 'ctx', 'c_ctx', 'w_mod', 'b_mod', 'g_mix', 'g_ffn', 'w_ffn_in', 'w_ffn_out', 'w_in', 'q_gain', 'k_gain', 'w_sp', 'b_sp', 'w_out', 'w_pw1', 'b_pw1', 'w_dw', 'b_dw', 'ln_g', 'ln_b', 'w_pw2', 'b_pw2', 'g_final', 'loss_target', 'm_c_ctx', 'm_w_mod', 'm_b_mod', 'm_g_mix', 'm_g_ffn', 'm_w_ffn_in', 'm_w_ffn_out', 'm_w_in', 'm_q_gain', 'm_k_gain', 'm_w_sp', 'm_b_sp', 'm_w_out', 'm_w_pw1', 'm_b_pw1', 'm_w_dw', 'm_b_dw', 'm_ln_g', 'm_ln_b', 'm_w_pw2', 'm_b_pw2', 'm_g_final', 'v_c_ctx', 'v_w_mod', 'v_b_mod', 'v_g_mix', 'v_g_ffn', 'v_w_ffn_in', 'v_w_ffn_out', 'v_w_in', 'v_q_gain', 'v_k_gain', 'v_w_sp', 'v_b_sp', 'v_w_out', 'v_w_pw1', 'v_b_pw1', 'v_w_dw', 'v_b_dw', 'v_ln_g', 'v_ln_b', 'v_w_pw2', 'v_b_pw2', 'v_g_final']
TWIN_OUTPUTS = ['loss', 'grad_x', 'grad_c_ctx', 'grad_w_mod', 'grad_b_mod', 'grad_g_mix', 'grad_g_ffn', 'grad_w_ffn_in', 'grad_w_ffn_out', 'grad_w_in', 'grad_q_gain', 'grad_k_gain', 'grad_w_sp', 'grad_b_sp', 'grad_w_out', 'grad_w_pw1', 'grad_b_pw1', 'grad_w_dw', 'grad_b_dw', 'grad_ln_g', 'grad_ln_b', 'grad_w_pw2', 'grad_b_pw2', 'grad_g_final', 'delta_c_ctx', 'delta_w_mod', 'delta_b_mod', 'delta_g_mix', 'delta_g_ffn', 'delta_w_ffn_in', 'delta_w_ffn_out', 'delta_w_in', 'delta_q_gain', 'delta_k_gain', 'delta_w_sp', 'delta_b_sp', 'delta_w_out', 'delta_w_pw1', 'delta_b_pw1', 'delta_w_dw', 'delta_b_dw', 'delta_ln_g', 'delta_ln_b', 'delta_w_pw2', 'delta_b_pw2', 'delta_g_final', 'new_m_c_ctx', 'new_m_w_mod', 'new_m_b_mod', 'new_m_g_mix', 'new_m_g_ffn', 'new_m_w_ffn_in', 'new_m_w_ffn_out', 'new_m_w_in', 'new_m_q_gain', 'new_m_k_gain', 'new_m_w_sp', 'new_m_b_sp', 'new_m_w_out', 'new_m_w_pw1', 'new_m_b_pw1', 'new_m_w_dw', 'new_m_b_dw', 'new_m_ln_g', 'new_m_ln_b', 'new_m_w_pw2', 'new_m_b_pw2', 'new_m_g_final', 'new_v_c_ctx', 'new_v_w_mod', 'new_v_b_mod', 'new_v_g_mix', 'new_v_g_ffn', 'new_v_w_ffn_in', 'new_v_w_ffn_out', 'new_v_w_in', 'new_v_q_gain', 'new_v_k_gain', 'new_v_w_sp', 'new_v_b_sp', 'new_v_w_out', 'new_v_w_pw1', 'new_v_b_pw1', 'new_v_w_dw', 'new_v_b_dw', 'new_v_ln_g', 'new_v_ln_b', 'new_v_w_pw2', 'new_v_b_pw2', 'new_v_g_final']
TWIN_LEAF_KINDS = {'loss': 'loss', 'grad_x': 'grad_x', 'grad_c_ctx': 'grad_w', 'grad_w_mod': 'grad_w', 'grad_b_mod': 'grad_w', 'grad_g_mix': 'grad_w', 'grad_g_ffn': 'grad_w', 'grad_w_ffn_in': 'grad_w', 'grad_w_ffn_out': 'grad_w', 'grad_w_in': 'grad_w', 'grad_q_gain': 'grad_w', 'grad_k_gain': 'grad_w', 'grad_w_sp': 'grad_w', 'grad_b_sp': 'grad_w', 'grad_w_out': 'grad_w', 'grad_w_pw1': 'grad_w', 'grad_b_pw1': 'grad_w', 'grad_w_dw': 'grad_w', 'grad_b_dw': 'grad_w', 'grad_ln_g': 'grad_w', 'grad_ln_b': 'grad_w', 'grad_w_pw2': 'grad_w', 'grad_b_pw2': 'grad_w', 'grad_g_final': 'grad_w', 'delta_c_ctx': 'delta_w', 'delta_w_mod': 'delta_w', 'delta_b_mod': 'delta_w', 'delta_g_mix': 'delta_w', 'delta_g_ffn': 'delta_w', 'delta_w_ffn_in': 'delta_w', 'delta_w_ffn_out': 'delta_w', 'delta_w_in': 'delta_w', 'delta_q_gain': 'delta_w', 'delta_k_gain': 'delta_w', 'delta_w_sp': 'delta_w', 'delta_b_sp': 'delta_w', 'delta_w_out': 'delta_w', 'delta_w_pw1': 'delta_w', 'delta_b_pw1': 'delta_w', 'delta_w_dw': 'delta_w', 'delta_b_dw': 'delta_w', 'delta_ln_g': 'delta_w', 'delta_ln_b': 'delta_w', 'delta_w_pw2': 'delta_w', 'delta_b_pw2': 'delta_w', 'delta_g_final': 'delta_w', 'new_m_c_ctx': 'new_m', 'new_m_w_mod': 'new_m', 'new_m_b_mod': 'new_m', 'new_m_g_mix': 'new_m', 'new_m_g_ffn': 'new_m', 'new_m_w_ffn_in': 'new_m', 'new_m_w_ffn_out': 'new_m', 'new_m_w_in': 'new_m', 'new_m_q_gain': 'new_m', 'new_m_k_gain': 'new_m', 'new_m_w_sp': 'new_m', 'new_m_b_sp': 'new_m', 'new_m_w_out': 'new_m', 'new_m_w_pw1': 'new_m', 'new_m_b_pw1': 'new_m', 'new_m_w_dw': 'new_m', 'new_m_b_dw': 'new_m', 'new_m_ln_g': 'new_m', 'new_m_ln_b': 'new_m', 'new_m_w_pw2': 'new_m', 'new_m_b_pw2': 'new_m', 'new_m_g_final': 'new_m', 'new_v_c_ctx': 'new_v', 'new_v_w_mod': 'new_v', 'new_v_b_mod': 'new_v', 'new_v_g_mix': 'new_v', 'new_v_g_ffn': 'new_v', 'new_v_w_ffn_in': 'new_v', 'new_v_w_ffn_out': 'new_v', 'new_v_w_in': 'new_v', 'new_v_q_gain': 'new_v', 'new_v_k_gain': 'new_v', 'new_v_w_sp': 'new_v', 'new_v_b_sp': 'new_v', 'new_v_w_out': 'new_v', 'new_v_w_pw1': 'new_v', 'new_v_b_pw1': 'new_v', 'new_v_w_dw': 'new_v', 'new_v_b_dw': 'new_v', 'new_v_ln_g': 'new_v', 'new_v_ln_b': 'new_v', 'new_v_w_pw2': 'new_v', 'new_v_b_pw2': 'new_v', 'new_v_g_final': 'new_v'}


def _forward(args):
    return _fwd_reference(*[args[k] for k in FWD_PARAMS])


def _output_shape():
    out = _jax.eval_shape(lambda: _forward(_fwd_setup_inputs(0)))
    return out.shape, out.dtype

N_MICROBATCH = 1
ADAM_LR = 0.001
ADAM_B1 = 0.9
ADAM_B2 = 0.999
ADAM_EPS = 1e-08
ADAM_WD = 0.01
ADAM_STEP = 10
PER_EXAMPLE_BATCH_AXIS = {'x': 0, 'c': 0, 'ctx': 0, 'loss_target': 0}
SHARED_INPUTS = []
_WEIGHT_DTYPES = {'c_ctx': _jnp.float32, 'w_mod': _jnp.float32, 'b_mod': _jnp.float32, 'g_mix': _jnp.float32, 'g_ffn': _jnp.float32, 'w_ffn_in': _jnp.float32, 'w_ffn_out': _jnp.float32, 'w_in': _jnp.float32, 'q_gain': _jnp.float32, 'k_gain': _jnp.float32, 'w_sp': _jnp.float32, 'b_sp': _jnp.float32, 'w_out': _jnp.float32, 'w_pw1': _jnp.float32, 'b_pw1': _jnp.float32, 'w_dw': _jnp.float32, 'b_dw': _jnp.float32, 'ln_g': _jnp.float32, 'ln_b': _jnp.float32, 'w_pw2': _jnp.float32, 'b_pw2': _jnp.float32, 'g_final': _jnp.float32}
MOMENT_SCALE = {'c_ctx': 6.902832e-03, 'w_mod': 5.044975e-02, 'b_mod': 8.483850e-02, 'g_mix': 4.345703e-02, 'g_ffn': 5.436987e-02, 'w_ffn_in': 2.313693e-02, 'w_ffn_out': 3.776482e-02, 'w_in': 4.125440e-02, 'q_gain': 1.328430e-02, 'k_gain': 1.366246e-02, 'w_sp': 4.067269e-02, 'b_sp': 4.154194e-02, 'w_out': 4.170216e-02, 'w_pw1': 2.479555e-02, 'b_pw1': 2.463972e-02, 'w_dw': 3.240562e-02, 'b_dw': 5.546279e-02, 'ln_g': 3.855990e-02, 'ln_b': 3.209889e-02, 'w_pw2': 3.146977e-02, 'b_pw2': 5.698505e-02, 'g_final': 3.205569e+01}


def _to_microbatches(a, axis):
    t = _jnp.moveaxis(a, axis, 0)
    t = t.reshape((N_MICROBATCH, t.shape[0] // N_MICROBATCH) + t.shape[1:])
    return _jnp.moveaxis(t, 1, axis + 1)


def setup_inputs(seed: int = 0) -> dict:
    inp = _fwd_setup_inputs(seed)
    key = _jax.random.fold_in(_jax.random.key(seed), 7919)
    shape, _ = _output_shape()
    out = dict(inp)
    out["loss_target"] = _jax.random.normal(_jax.random.fold_in(key, 0), shape, _jnp.float32)
    for i, name in enumerate(TWIN_WEIGHTS):
        w = inp[name].astype(_jnp.float32)
        if MOMENT_SCALE is None:
            s = _jnp.sqrt(_jnp.mean(_jnp.square(w)) + 1e-30)
        else:
            s = MOMENT_SCALE[name]
        km, kv = _jax.random.split(_jax.random.fold_in(key, i + 1))
        out[name] = w
        out["m_" + name] = s * _jax.random.normal(km, w.shape, _jnp.float32)
        out["v_" + name] = (s * s) * _jax.random.uniform(kv, w.shape, _jnp.float32, 0.5, 1.5)
    if N_MICROBATCH > 1:
        for name, axis in PER_EXAMPLE_BATCH_AXIS.items():
            out[name] = _to_microbatches(out[name], axis)
    return {'x': out['x'], 'c': out['c'], 'ctx': out['ctx'], 'c_ctx': out['c_ctx'], 'w_mod': out['w_mod'], 'b_mod': out['b_mod'], 'g_mix': out['g_mix'], 'g_ffn': out['g_ffn'], 'w_ffn_in': out['w_ffn_in'], 'w_ffn_out': out['w_ffn_out'], 'w_in': out['w_in'], 'q_gain': out['q_gain'], 'k_gain': out['k_gain'], 'w_sp': out['w_sp'], 'b_sp': out['b_sp'], 'w_out': out['w_out'], 'w_pw1': out['w_pw1'], 'b_pw1': out['b_pw1'], 'w_dw': out['w_dw'], 'b_dw': out['b_dw'], 'ln_g': out['ln_g'], 'ln_b': out['ln_b'], 'w_pw2': out['w_pw2'], 'b_pw2': out['b_pw2'], 'g_final': out['g_final'], 'loss_target': out['loss_target'], 'm_c_ctx': out['m_c_ctx'], 'm_w_mod': out['m_w_mod'], 'm_b_mod': out['m_b_mod'], 'm_g_mix': out['m_g_mix'], 'm_g_ffn': out['m_g_ffn'], 'm_w_ffn_in': out['m_w_ffn_in'], 'm_w_ffn_out': out['m_w_ffn_out'], 'm_w_in': out['m_w_in'], 'm_q_gain': out['m_q_gain'], 'm_k_gain': out['m_k_gain'], 'm_w_sp': out['m_w_sp'], 'm_b_sp': out['m_b_sp'], 'm_w_out': out['m_w_out'], 'm_w_pw1': out['m_w_pw1'], 'm_b_pw1': out['m_b_pw1'], 'm_w_dw': out['m_w_dw'], 'm_b_dw': out['m_b_dw'], 'm_ln_g': out['m_ln_g'], 'm_ln_b': out['m_ln_b'], 'm_w_pw2': out['m_w_pw2'], 'm_b_pw2': out['m_b_pw2'], 'm_g_final': out['m_g_final'], 'v_c_ctx': out['v_c_ctx'], 'v_w_mod': out['v_w_mod'], 'v_b_mod': out['v_b_mod'], 'v_g_mix': out['v_g_mix'], 'v_g_ffn': out['v_g_ffn'], 'v_w_ffn_in': out['v_w_ffn_in'], 'v_w_ffn_out': out['v_w_ffn_out'], 'v_w_in': out['v_w_in'], 'v_q_gain': out['v_q_gain'], 'v_k_gain': out['v_k_gain'], 'v_w_sp': out['v_w_sp'], 'v_b_sp': out['v_b_sp'], 'v_w_out': out['v_w_out'], 'v_w_pw1': out['v_w_pw1'], 'v_b_pw1': out['v_b_pw1'], 'v_w_dw': out['v_w_dw'], 'v_b_dw': out['v_b_dw'], 'v_ln_g': out['v_ln_g'], 'v_ln_b': out['v_ln_b'], 'v_w_pw2': out['v_w_pw2'], 'v_b_pw2': out['v_b_pw2'], 'v_g_final': out['v_g_final']}


def _loss(weights, diff, rest, loss_target):
    with _jax.named_scope("forward"):
        args = {**rest, TWIN_DIFF_INPUT: diff, **{k: w.astype(_WEIGHT_DTYPES[k]) for k, w in weights.items()}}
        y = _forward(args)
    with _jax.named_scope("loss_head"):
        err = _jnp.square(y.astype(_jnp.float32) - loss_target)
        return 0.5 * _jnp.sum(_jnp.mean(err, axis=-1)) if err.ndim else 0.5 * err


def _adamw(w, g, m, v):
    m = ADAM_B1 * m + (1.0 - ADAM_B1) * g
    v = ADAM_B2 * v + (1.0 - ADAM_B2) * _jnp.square(g)
    m_hat = m / (1.0 - ADAM_B1 ** ADAM_STEP)
    v_hat = v / (1.0 - ADAM_B2 ** ADAM_STEP)
    delta = -ADAM_LR * (m_hat / (_jnp.sqrt(v_hat) + ADAM_EPS) + ADAM_WD * w)
    return delta, m, v


def reference(x, c, ctx, c_ctx, w_mod, b_mod, g_mix, g_ffn, w_ffn_in, w_ffn_out, w_in, q_gain, k_gain, w_sp, b_sp, w_out, w_pw1, b_pw1, w_dw, b_dw, ln_g, ln_b, w_pw2, b_pw2, g_final, loss_target, m_c_ctx, m_w_mod, m_b_mod, m_g_mix, m_g_ffn, m_w_ffn_in, m_w_ffn_out, m_w_in, m_q_gain, m_k_gain, m_w_sp, m_b_sp, m_w_out, m_w_pw1, m_b_pw1, m_w_dw, m_b_dw, m_ln_g, m_ln_b, m_w_pw2, m_b_pw2, m_g_final, v_c_ctx, v_w_mod, v_b_mod, v_g_mix, v_g_ffn, v_w_ffn_in, v_w_ffn_out, v_w_in, v_q_gain, v_k_gain, v_w_sp, v_b_sp, v_w_out, v_w_pw1, v_b_pw1, v_w_dw, v_b_dw, v_ln_g, v_ln_b, v_w_pw2, v_b_pw2, v_g_final):
    given = dict(x=x, c=c, ctx=ctx, c_ctx=c_ctx, w_mod=w_mod, b_mod=b_mod, g_mix=g_mix, g_ffn=g_ffn, w_ffn_in=w_ffn_in, w_ffn_out=w_ffn_out, w_in=w_in, q_gain=q_gain, k_gain=k_gain, w_sp=w_sp, b_sp=b_sp, w_out=w_out, w_pw1=w_pw1, b_pw1=b_pw1, w_dw=w_dw, b_dw=b_dw, ln_g=ln_g, ln_b=ln_b, w_pw2=w_pw2, b_pw2=b_pw2, g_final=g_final, loss_target=loss_target, m_c_ctx=m_c_ctx, m_w_mod=m_w_mod, m_b_mod=m_b_mod, m_g_mix=m_g_mix, m_g_ffn=m_g_ffn, m_w_ffn_in=m_w_ffn_in, m_w_ffn_out=m_w_ffn_out, m_w_in=m_w_in, m_q_gain=m_q_gain, m_k_gain=m_k_gain, m_w_sp=m_w_sp, m_b_sp=m_b_sp, m_w_out=m_w_out, m_w_pw1=m_w_pw1, m_b_pw1=m_b_pw1, m_w_dw=m_w_dw, m_b_dw=m_b_dw, m_ln_g=m_ln_g, m_ln_b=m_ln_b, m_w_pw2=m_w_pw2, m_b_pw2=m_b_pw2, m_g_final=m_g_final, v_c_ctx=v_c_ctx, v_w_mod=v_w_mod, v_b_mod=v_b_mod, v_g_mix=v_g_mix, v_g_ffn=v_g_ffn, v_w_ffn_in=v_w_ffn_in, v_w_ffn_out=v_w_ffn_out, v_w_in=v_w_in, v_q_gain=v_q_gain, v_k_gain=v_k_gain, v_w_sp=v_w_sp, v_b_sp=v_b_sp, v_w_out=v_w_out, v_w_pw1=v_w_pw1, v_b_pw1=v_b_pw1, v_w_dw=v_w_dw, v_b_dw=v_b_dw, v_ln_g=v_ln_g, v_ln_b=v_ln_b, v_w_pw2=v_w_pw2, v_b_pw2=v_b_pw2, v_g_final=v_g_final)
    weights = {n: given[n] for n in TWIN_WEIGHTS}
    shared = {n: given[n] for n in SHARED_INPUTS}
    per_example = {n: given[n] for n in ['x', 'c', 'ctx']}
    grad_fn = _jax.value_and_grad(_loss, argnums=(0, 1))

    def one_microbatch(ex, loss_target):
        ex = dict(ex)
        diff = ex.pop(TWIN_DIFF_INPUT)
        return grad_fn(weights, diff, {**shared, **ex}, loss_target)

    if N_MICROBATCH == 1:
        loss, (grad_w, grad_x) = one_microbatch(per_example, given["loss_target"])
    else:
        def body(carry, xs):
            loss_sum, grad_sum = carry
            l_k, (gw_k, gx_k) = one_microbatch(xs[0], xs[1])
            with _jax.named_scope("update"):
                return (loss_sum + l_k, _jax.tree.map(_jnp.add, grad_sum, gw_k)), gx_k

        init = (_jnp.zeros((), _jnp.float32), _jax.tree.map(_jnp.zeros_like, weights))
        (loss, grad_w), grad_x = _jax.lax.scan(body, init, (per_example, given["loss_target"]))
    with _jax.named_scope("update"):
        delta_w, new_m, new_v = {}, {}, {}
        for n in TWIN_WEIGHTS:
            delta_w[n], new_m[n], new_v[n] = _adamw(weights[n], grad_w[n], given["m_" + n], given["v_" + n])
    return (loss, grad_x, *[grad_w[n] for n in TWIN_WEIGHTS], *[delta_w[n] for n in TWIN_WEIGHTS],
            *[new_m[n] for n in TWIN_WEIGHTS], *[new_v[n] for n in TWIN_WEIGHTS])
```

```python
import math

import jax
import jax.numpy as jnp
from jax import lax
from jax.experimental import pallas as pl
from jax.experimental.pallas import tpu as pltpu

F32 = jnp.float32
BF16 = jnp.bfloat16
MESH = pl.DeviceIdType.MESH

N_DEV = 8
D_MODEL = 1024
EPS = 1e-6
HEAD_DIM = 64
ATTN_WIDTH = 512
KV_WIDTH = 128
SG_WIDTH = 512
N_SG_GROUPS = 4
CHUNK = 128
IN_WIDTH = 1792
D_FF = 2816
FF_SHARD = 2 * D_FF // N_DEV
CONV_WIDTH = 31
CONV_HALO = 16
GRID_W = 64
ROPE_THETA = 10000.0
LANES = 128
ROW_BLOCK = 256
ADAM_LR, ADAM_B1, ADAM_B2, ADAM_EPS, ADAM_WD, ADAM_STEP = 0.001, 0.9, 0.999, 1e-08, 0.01, 10


def _tile(n, target, mult=LANES):
    best = None
    for t in range(mult, min(n, target) + 1, mult):
        if n % t == 0:
            best = t
    return best if best is not None else n


def _sigmoid(x):
    return 1.0 / (1.0 + jnp.exp(-x))


def _silu(x):
    return x * _sigmoid(x)


def _dsilu(x):
    s = _sigmoid(x)
    return s * (1.0 + x * (1.0 - s))


_GELU_K = math.sqrt(2.0 / math.pi)


def _gelu(x):
    return 0.5 * x * (1.0 + jnp.tanh(_GELU_K * (x + 0.044715 * x * x * x)))


def _dgelu(x):
    t = jnp.tanh(_GELU_K * (x + 0.044715 * x * x * x))
    return 0.5 * (1.0 + t) + 0.5 * x * (1.0 - t * t) * _GELU_K * (1.0 + 3.0 * 0.044715 * x * x)


def _split_bf16(x):
    hi = x.astype(BF16)
    lo = (x - hi.astype(F32)).astype(BF16)
    return hi, lo


def _dot(a, b, dims):
    return lax.dot_general(a, b, (dims, ((), ())), preferred_element_type=F32)


def _dot3(a, b, dims):
    ah, al = _split_bf16(a)
    bh, bl = _split_bf16(b)
    return _dot(ah, bh, dims) + _dot(ah, bl, dims) + _dot(al, bh, dims)


NN = ((1,), (0,))
NT = ((1,), (1,))
TN = ((0,), (0,))


def _all_gather(xs, name, in_vmem):
    n_arr = len(xs)

    def body(*refs):
        x_refs, out_refs = refs[:n_arr], refs[n_arr:2 * n_arr]
        send_sems, recv_sems, local_sems = refs[2 * n_arr:]
        x, y, c = lax.axis_index("x"), lax.axis_index("y"), lax.axis_index("c")
        me, sibling = (x, y, c), (x, y, 1 - c)
        chips = [(1 - x, y), (x, 1 - y), (1 - x, 1 - y)]

        def rows(a, px, py, pc):
            m_per = xs[a].shape[0]
            return out_refs[a].at[pl.ds((4 * px + 2 * py + pc) * m_per, m_per), :]

        def copy(a, k, block, to, src=None):
            return pltpu.make_async_remote_copy(
                src_ref=rows(a, *block) if src is None else src,
                dst_ref=rows(a, *block),
                send_sem=send_sems.at[7 * a + k],
                recv_sem=recv_sems.at[7 * a + k],
                device_id=to,
                device_id_type=MESH,
            )

        mine, first, passed = [], [], []
        for a in range(n_arr):
            mine.append(pltpu.make_async_copy(x_refs[a], rows(a, *me), local_sems.at[a]))
            mine[-1].start()
            first.append(copy(a, 0, me, sibling, src=x_refs[a]))
            first += [copy(a, 1 + j, me, (*chip, c), src=x_refs[a]) for j, chip in enumerate(chips)]
        for cp in first:
            cp.start()
        for a in range(n_arr):
            for j, chip in enumerate(chips):
                copy(a, 1 + j, (*chip, c), me).wait_recv()
                passed.append(copy(a, 4 + j, (*chip, c), sibling))
                passed[-1].start()
        for a in range(n_arr):
            copy(a, 0, sibling, me).wait_recv()
            for j, chip in enumerate(chips):
                copy(a, 4 + j, (*chip, 1 - c), me).wait_recv()
        for cp in first + passed:
            cp.wait_send()
        for cp in mine:
            cp.wait()

    space = pltpu.VMEM if in_vmem else pl.ANY
    return pl.pallas_call(
        body,
        name=name,
        out_shape=[jax.ShapeDtypeStruct((N_DEV * t.shape[0], t.shape[1]), t.dtype) for t in xs],
        in_specs=[pl.BlockSpec(memory_space=space)] * n_arr,
        out_specs=[pl.BlockSpec(memory_space=space)] * n_arr,
        scratch_shapes=[
            pltpu.SemaphoreType.DMA((7 * n_arr,)),
            pltpu.SemaphoreType.DMA((7 * n_arr,)),
            pltpu.SemaphoreType.DMA((n_arr,)),
        ],
    )(*xs)


def _shard_exchange(gs, name):
    n_arr = len(gs)

    def body(*refs):
        g_refs, r_refs = refs[:n_arr], refs[n_arr:2 * n_arr]
        send_sems, recv_sems, local_sems = refs[2 * n_arr:]
        x, y, c = lax.axis_index("x"), lax.axis_index("y"), lax.axis_index("c")
        me = 4 * x + 2 * y + c

        def rows(ref, a, idx):
            m_per = gs[a].shape[0] // N_DEV
            return ref.at[pl.ds(idx * m_per, m_per), :]

        mine, sends, recvs = [], [], []
        for a in range(n_arr):
            mine.append(pltpu.make_async_copy(rows(g_refs[a], a, me), rows(r_refs[a], a, me), local_sems.at[a]))
            mine[-1].start()
            for k in range(1, N_DEV):
                px = 1 - x if (k >> 2) & 1 else x
                py = 1 - y if (k >> 1) & 1 else y
                pc = 1 - c if k & 1 else c
                peer = 4 * px + 2 * py + pc
                sem = 7 * a + k - 1
                sends.append(pltpu.make_async_remote_copy(
                    src_ref=rows(g_refs[a], a, peer), dst_ref=rows(r_refs[a], a, me),
                    send_sem=send_sems.at[sem], recv_sem=recv_sems.at[sem],
                    device_id=(px, py, pc), device_id_type=MESH))
                recvs.append(pltpu.make_async_remote_copy(
                    src_ref=rows(g_refs[a], a, me), dst_ref=rows(r_refs[a], a, peer),
                    send_sem=send_sems.at[sem], recv_sem=recv_sems.at[sem],
                    device_id=(px, py, pc), device_id_type=MESH))
        for cp in sends:
            cp.start()
        for cp in recvs:
            cp.wait_recv()
        for cp in sends:
            cp.wait_send()
        for cp in mine:
            cp.wait()

    return pl.pallas_call(
        body,
        name=name,
        out_shape=[jax.ShapeDtypeStruct(t.shape, t.dtype) for t in gs],
        in_specs=[pl.BlockSpec(memory_space=pl.ANY)] * n_arr,
        out_specs=[pl.BlockSpec(memory_space=pl.ANY)] * n_arr,
        scratch_shapes=[
            pltpu.SemaphoreType.DMA((7 * n_arr,)),
            pltpu.SemaphoreType.DMA((7 * n_arr,)),
            pltpu.SemaphoreType.DMA((n_arr,)),
        ],
    )(*gs)


def _sum_devices(r, name, rows_per_step=ROW_BLOCK):
    _, m, n = r.shape
    tm = _tile(m, rows_per_step, 8)

    def body(r_ref, o_ref):
        acc = r_ref[0].astype(F32)
        for s in range(1, N_DEV):
            acc = acc + r_ref[s].astype(F32)
        o_ref[...] = acc

    return pl.pallas_call(
        body,
        name=name,
        grid=(m // tm,),
        out_shape=jax.ShapeDtypeStruct((m, n), F32),
        in_specs=[pl.BlockSpec((N_DEV, tm, n), lambda i: (0, i, 0))],
        out_specs=pl.BlockSpec((tm, n), lambda i: (i, 0)),
        compiler_params=pltpu.CompilerParams(dimension_semantics=("parallel",)),
    )(r)


def _get(ref):
    return ref[0] if len(ref.shape) == 3 else ref[...]


def _put(ref, val):
    if len(ref.shape) == 3:
        ref[0] = val
    else:
        ref[...] = val


def _mm_call(name, a, b, a_spec, b_spec, out_sds, o_spec, grid, dims, acc_shape, bias=None,
             res=None, gate=None, raw_out=False, vec_spec=None):
    nk = grid[2]
    operands, in_specs = [a, b], [a_spec, b_spec]
    if bias is not None:
        operands.append(bias)
        in_specs.append(vec_spec)
    if res is not None:
        operands += [res, gate]
        in_specs += [o_spec, vec_spec]
    out_shape, out_specs = [out_sds], [o_spec]
    if raw_out:
        out_shape.append(jax.ShapeDtypeStruct(out_sds.shape, F32))
        out_specs.append(o_spec)

    def body(*refs):
        it = iter(refs)
        a_ref, b_ref = next(it), next(it)
        bias_ref = next(it) if bias is not None else None
        res_ref, gate_ref = (next(it), next(it)) if res is not None else (None, None)
        o_ref = next(it)
        raw_ref = next(it) if raw_out else None
        acc = next(it)
        k = pl.program_id(2)

        @pl.when(k == 0)
        def _():
            acc[...] = jnp.zeros_like(acc)

        acc[...] += _dot(_get(a_ref).astype(BF16), _get(b_ref).astype(BF16), dims)

        @pl.when(k == nk - 1)
        def _():
            y = acc[...]
            if bias_ref is not None:
                y = y + bias_ref[...]
            if raw_ref is not None:
                raw_ref[...] = y
            if res_ref is not None:
                y = res_ref[...] + gate_ref[...] * y
            _put(o_ref, y.astype(out_sds.dtype))

    outs = pl.pallas_call(
        body,
        name=name,
        grid=grid,
        out_shape=out_shape,
        in_specs=in_specs,
        out_specs=out_specs,
        scratch_shapes=[pltpu.VMEM(acc_shape, F32)],
        compiler_params=pltpu.CompilerParams(dimension_semantics=("parallel", "parallel", "arbitrary")),
    )(*operands)
    return outs if raw_out else outs[0]


def _mm(a, b, mode, name, out_dtype=F32, bias=None, res=None, gate=None, raw_out=False,
        tm=512, tn=1024, tk=1024, a_row_off=0):
    if mode == "nn":
        K, N = b.shape
        M = a.shape[0] - a_row_off
    elif mode == "nt":
        N, K = b.shape
        M = a.shape[0] - a_row_off
    else:
        (K, M), N = a.shape, b.shape[1]
    tm, tn, tk = _tile(M, tm), _tile(N, tn), _tile(K, tk)
    off = a_row_off // tm
    dims = {"nn": NN, "nt": NT, "tn": TN}[mode]
    a_spec = (pl.BlockSpec((tk, tm), lambda i, j, k: (k, i)) if mode == "tn"
              else pl.BlockSpec((tm, tk), lambda i, j, k: (i + off, k)))
    b_spec = (pl.BlockSpec((tn, tk), lambda i, j, k: (j, k)) if mode == "nt"
              else pl.BlockSpec((tk, tn), lambda i, j, k: (k, j)))
    return _mm_call(name, a, b, a_spec, b_spec, jax.ShapeDtypeStruct((M, N), out_dtype),
                    pl.BlockSpec((tm, tn), lambda i, j, k: (i, j)), (M // tm, N // tn, K // tk), dims,
                    (tm, tn), bias, res, gate, raw_out, pl.BlockSpec((1, tn), lambda i, j, k: (0, j)))


def _mm_to_shards(a, b3, mode, name, out_dtype, tm=512):
    M, K = a.shape
    S = b3.shape[0]
    n = b3.shape[2] if mode == "nn" else b3.shape[1]
    tm = _tile(M, tm)
    return _mm_call(name, a, b3, pl.BlockSpec((tm, K), lambda i, j, k: (i, 0)),
                    pl.BlockSpec((1,) + b3.shape[1:], lambda i, j, k: (j, 0, 0)),
                    jax.ShapeDtypeStruct((S, M, n), out_dtype),
                    pl.BlockSpec((1, tm, n), lambda i, j, k: (j, i, 0)), (M // tm, S, 1),
                    NN if mode == "nn" else NT, (tm, n))


def _mm_over_shards(a3, b3, mode, name, out_dtype=F32, res=None, gate=None, raw_out=False, tm=512, tn=1024):
    S, M, kk = a3.shape
    N = b3.shape[2] if mode == "nn" else b3.shape[1]
    tm, tn = _tile(M, tm), _tile(N, tn)
    b_spec = (pl.BlockSpec((1, kk, tn), lambda i, j, k: (k, 0, j)) if mode == "nn"
              else pl.BlockSpec((1, tn, kk), lambda i, j, k: (k, j, 0)))
    return _mm_call(name, a3, b3, pl.BlockSpec((1, tm, kk), lambda i, j, k: (k, i, 0)), b_spec,
                    jax.ShapeDtypeStruct((M, N), out_dtype), pl.BlockSpec((tm, tn), lambda i, j, k: (i, j)),
                    (M // tm, N // tn, S), NN if mode == "nn" else NT, (tm, tn), None, res, gate, raw_out,
                    pl.BlockSpec((1, tn), lambda i, j, k: (0, j)))


def _mm_tn_shard_rows(a3, b, name, out_dtype, tn=1024, tk=512):
    S, T, m = a3.shape
    N = b.shape[1]
    tn, tk = _tile(N, tn), _tile(T, tk)
    return _mm_call(name, a3, b, pl.BlockSpec((1, tk, m), lambda i, j, k: (i, k, 0)),
                    pl.BlockSpec((tk, tn), lambda i, j, k: (k, j)), jax.ShapeDtypeStruct((S, m, N), out_dtype),
                    pl.BlockSpec((1, m, tn), lambda i, j, k: (i, 0, j)), (S, N // tn, T // tk), TN, (m, tn))


def _mm_tn_shard_cols(a, b3, name, out_dtype, tm=1024, tk=512):
    T, M = a.shape
    S, _, n = b3.shape
    tm, tk = _tile(M, tm), _tile(T, tk)
    return _mm_call(name, a, b3, pl.BlockSpec((tk, tm), lambda i, j, k: (k, i)),
                    pl.BlockSpec((1, tk, n), lambda i, j, k: (j, k, 0)), jax.ShapeDtypeStruct((S, M, n), out_dtype),
                    pl.BlockSpec((1, tm, n), lambda i, j, k: (j, i, 0)), (M // tm, S, T // tk), TN, (tm, n))


def _row_spec(tm, width, off=0):
    return pl.BlockSpec((tm, width), lambda i: (i + off, 0))


def _vec_spec(width):
    return pl.BlockSpec((1, width), lambda i: (0, 0))


def _norm_mod_fwd(h, g, sc, sh, name):
    R, Dm = h.shape
    tm = _tile(R, ROW_BLOCK, 8)

    def body(h_ref, g_ref, sc_ref, sh_ref, o_ref):
        hv = h_ref[...]
        r = lax.rsqrt(jnp.mean(hv * hv, axis=-1, keepdims=True) + EPS)
        o_ref[...] = ((hv * r) * g_ref[...] * (1.0 + sc_ref[...]) + sh_ref[...]).astype(BF16)

    return pl.pallas_call(
        body, name=name, grid=(R // tm,),
        out_shape=jax.ShapeDtypeStruct((R, Dm), BF16),
        in_specs=[_row_spec(tm, Dm), _vec_spec(Dm), _vec_spec(Dm), _vec_spec(Dm)],
        out_specs=_row_spec(tm, Dm),
        compiler_params=pltpu.CompilerParams(dimension_semantics=("parallel",)),
    )(h, g, sc, sh)


def _norm_mod_fwd_cat(hc, h, g, csc, csh, sc, sh, name):
    (C, Dm), T = hc.shape, h.shape[0]
    tm = _tile(math.gcd(C, T), ROW_BLOCK, 8)
    off = C // tm

    def body(hc_ref, h_ref, g_ref, csc_ref, csh_ref, sc_ref, sh_ref, o_ref):
        is_ctx = pl.program_id(0) < off
        hv = jnp.where(is_ctx, hc_ref[...], h_ref[...])
        scv = jnp.where(is_ctx, csc_ref[...], sc_ref[...])
        shv = jnp.where(is_ctx, csh_ref[...], sh_ref[...])
        r = lax.rsqrt(jnp.mean(hv * hv, axis=-1, keepdims=True) + EPS)
        o_ref[...] = ((hv * r) * g_ref[...] * (1.0 + scv) + shv).astype(BF16)

    return pl.pallas_call(
        body, name=name, grid=((C + T) // tm,),
        out_shape=jax.ShapeDtypeStruct((C + T, Dm), BF16),
        in_specs=[pl.BlockSpec((tm, Dm), lambda i: (jnp.minimum(i, off - 1), 0)),
                  pl.BlockSpec((tm, Dm), lambda i: (jnp.maximum(i - off, 0), 0))] + [_vec_spec(Dm)] * 5,
        out_specs=_row_spec(tm, Dm),
        compiler_params=pltpu.CompilerParams(dimension_semantics=("parallel",)),
    )(hc, h, g, csc, csh, sc, sh)


def _norm_mod_bwd(h, g, sc, dxm, dres, name, dxm_row_off=0):
    R, Dm = h.shape
    tm = _tile(R, ROW_BLOCK, 8)
    off = dxm_row_off // tm
    has_res = dres is not None

    def body(*refs):
        it = iter(refs)
        h_ref, g_ref, sc_ref, dx_ref = next(it), next(it), next(it), next(it)
        dres_ref = next(it) if has_res else None
        dh_ref, da_ref, dsh_ref = next(it), next(it), next(it)
        i = pl.program_id(0)

        @pl.when(i == 0)
        def _():
            da_ref[...] = jnp.zeros_like(da_ref)
            dsh_ref[...] = jnp.zeros_like(dsh_ref)

        hv = h_ref[...]
        dx = dx_ref[...].astype(F32)
        r = lax.rsqrt(jnp.mean(hv * hv, axis=-1, keepdims=True) + EPS)
        n = hv * r
        da_ref[...] += jnp.sum(dx * n, axis=0, keepdims=True)
        dsh_ref[...] += jnp.sum(dx, axis=0, keepdims=True)
        dn = dx * (g_ref[...] * (1.0 + sc_ref[...]))
        dh = r * (dn - n * jnp.mean(dn * n, axis=-1, keepdims=True))
        if has_res:
            dh = dh + dres_ref[...]
        dh_ref[...] = dh

    operands = [h, g, sc, dxm] + ([dres] if has_res else [])
    in_specs = [_row_spec(tm, Dm), _vec_spec(Dm), _vec_spec(Dm), _row_spec(tm, Dm, off)]
    in_specs += [_row_spec(tm, Dm)] if has_res else []
    return pl.pallas_call(
        body, name=name, grid=(R // tm,),
        out_shape=[jax.ShapeDtypeStruct((R, Dm), F32), jax.ShapeDtypeStruct((1, Dm), F32),
                   jax.ShapeDtypeStruct((1, Dm), F32)],
        in_specs=in_specs,
        out_specs=[_row_spec(tm, Dm), _vec_spec(Dm), _vec_spec(Dm)],
        compiler_params=pltpu.CompilerParams(dimension_semantics=("arbitrary",)),
    )(*operands)


def _gate_bwd(dh, y, gt, name):
    R, Dm = dh.shape
    tm = _tile(R, ROW_BLOCK, 8)

    def body(dh_ref, y_ref, gt_ref, dy_ref, dgt_ref, dsum_ref):
        i = pl.program_id(0)

        @pl.when(i == 0)
        def _():
            dgt_ref[...] = jnp.zeros_like(dgt_ref)
            dsum_ref[...] = jnp.zeros_like(dsum_ref)

        dhv = dh_ref[...]
        dy = dhv * gt_ref[...]
        dgt_ref[...] += jnp.sum(dhv * y_ref[...], axis=0, keepdims=True)
        dsum_ref[...] += jnp.sum(dy, axis=0, keepdims=True)
        dy_ref[...] = dy.astype(BF16)

    return pl.pallas_call(
        body, name=name, grid=(R // tm,),
        out_shape=[jax.ShapeDtypeStruct((R, Dm), BF16), jax.ShapeDtypeStruct((1, Dm), F32),
                   jax.ShapeDtypeStruct((1, Dm), F32)],
        in_specs=[_row_spec(tm, Dm), _row_spec(tm, Dm), _vec_spec(Dm)],
        out_specs=[_row_spec(tm, Dm), _vec_spec(Dm), _vec_spec(Dm)],
        compiler_params=pltpu.CompilerParams(dimension_semantics=("arbitrary",)),
    )(dh, y, gt)


def _swiglu_fwd(gu, name):
    S, T, n = gu.shape
    half = S // 2
    tm = _tile(T, ROW_BLOCK, 8)

    def body(g_ref, u_ref, o_ref):
        o_ref[0] = (_silu(g_ref[0]) * u_ref[0]).astype(BF16)

    return pl.pallas_call(
        body, name=name, grid=(half, T // tm),
        out_shape=jax.ShapeDtypeStruct((half, T, n), BF16),
        in_specs=[pl.BlockSpec((1, tm, n), lambda j, i: (j, i, 0)),
                  pl.BlockSpec((1, tm, n), lambda j, i: (j + half, i, 0))],
        out_specs=pl.BlockSpec((1, tm, n), lambda j, i: (j, i, 0)),
        compiler_params=pltpu.CompilerParams(dimension_semantics=("parallel", "parallel")),
    )(gu, gu)


def _swiglu_bwd(gu, dact, name):
    S, T, n = gu.shape
    half = S // 2
    tm = _tile(T, ROW_BLOCK, 8)

    def body(g_ref, u_ref, da_ref, o_ref):
        g = g_ref[0]
        da = da_ref[0].astype(F32)
        is_gate = pl.program_id(0) < half
        o_ref[0] = jnp.where(is_gate, da * u_ref[0] * _dsilu(g), da * _silu(g)).astype(BF16)

    return pl.pallas_call(
        body, name=name, grid=(S, T // tm),
        out_shape=jax.ShapeDtypeStruct((S, T, n), BF16),
        in_specs=[pl.BlockSpec((1, tm, n), lambda j, i: (j % half, i, 0)),
                  pl.BlockSpec((1, tm, n), lambda j, i: (j % half + half, i, 0)),
                  pl.BlockSpec((1, tm, n), lambda j, i: (j % half, i, 0))],
        out_specs=pl.BlockSpec((1, tm, n), lambda j, i: (j, i, 0)),
        compiler_params=pltpu.CompilerParams(dimension_semantics=("parallel", "parallel")),
    )(gu, gu, dact)


def _glu_fwd(ag, name):
    R = ag.shape[0]
    tm = _tile(R, ROW_BLOCK, 8)

    def body(ag_ref, o_ref):
        o_ref[...] = ag_ref[:, :D_MODEL] * _sigmoid(ag_ref[:, D_MODEL:])

    return pl.pallas_call(
        body, name=name, grid=(R // tm,),
        out_shape=jax.ShapeDtypeStruct((R, D_MODEL), F32),
        in_specs=[_row_spec(tm, 2 * D_MODEL)],
        out_specs=_row_spec(tm, D_MODEL),
        compiler_params=pltpu.CompilerParams(dimension_semantics=("parallel",)),
    )(ag)


def _glu_bwd(ag, dhg, name):
    R = ag.shape[0]
    tm = _tile(R, ROW_BLOCK, 8)

    def body(ag_ref, dh_ref, o_ref, s_ref):
        i = pl.program_id(0)

        @pl.when(i == 0)
        def _():
            s_ref[...] = jnp.zeros_like(s_ref)

        a = ag_ref[:, :D_MODEL]
        s = _sigmoid(ag_ref[:, D_MODEL:])
        dh = dh_ref[...]
        da = dh * s
        dg = dh * a * s * (1.0 - s)
        o_ref[:, :D_MODEL] = da.astype(BF16)
        o_ref[:, D_MODEL:] = dg.astype(BF16)
        s_ref[:, :D_MODEL] += jnp.sum(da, axis=0, keepdims=True)
        s_ref[:, D_MODEL:] += jnp.sum(dg, axis=0, keepdims=True)

    return pl.pallas_call(
        body, name=name, grid=(R // tm,),
        out_shape=[jax.ShapeDtypeStruct((R, 2 * D_MODEL), BF16), jax.ShapeDtypeStruct((1, 2 * D_MODEL), F32)],
        in_specs=[_row_spec(tm, 2 * D_MODEL), _row_spec(tm, D_MODEL)],
        out_specs=[_row_spec(tm, 2 * D_MODEL), _vec_spec(2 * D_MODEL)],
        compiler_params=pltpu.CompilerParams(dimension_semantics=("arbitrary",)),
    )(ag, dhg)


def _halo_specs(tm, nblk, width):
    per = tm // CONV_HALO
    prev = pl.BlockSpec((CONV_HALO, width), lambda i: (jnp.maximum(i * per - 1, 0), 0))
    nxt = pl.BlockSpec((CONV_HALO, width), lambda i: (jnp.minimum((i + 1) * per, nblk * per - 1), 0))
    return prev, nxt


def _fill_halo(scr, prev_ref, cur_ref, next_ref, i, nblk, tm):
    scr[0:CONV_HALO, :] = jnp.where(i > 0, prev_ref[...], 0.0)
    scr[CONV_HALO:CONV_HALO + tm, :] = cur_ref[...]
    scr[CONV_HALO + tm:2 * CONV_HALO + tm, :] = jnp.where(i < nblk - 1, next_ref[...], 0.0)


def _conv_ln_fwd(hg, w_dw, b_dw, ln_g, ln_b, name):
    R, Dm = hg.shape
    tm = _tile(R, ROW_BLOCK, CONV_HALO)
    nblk = R // tm
    prev_spec, next_spec = _halo_specs(tm, nblk, Dm)

    def body(prev_ref, cur_ref, next_ref, w_ref, bdw_ref, g_ref, b_ref, hd_ref, hs_ref, scr):
        i = pl.program_id(0)
        _fill_halo(scr, prev_ref, cur_ref, next_ref, i, nblk, tm)
        acc = jnp.zeros((tm, Dm), F32)
        for j in range(CONV_WIDTH):
            acc = acc + w_ref[j:j + 1, :] * scr[pl.ds(CONV_HALO - CONV_WIDTH // 2 + j, tm), :]
        hd = acc + bdw_ref[...]
        hd_ref[...] = hd
        mu = jnp.mean(hd, axis=-1, keepdims=True)
        xc = hd - mu
        rs = lax.rsqrt(jnp.mean(xc * xc, axis=-1, keepdims=True) + EPS)
        hl = xc * rs * g_ref[...] + b_ref[...]
        hs_ref[...] = _silu(hl).astype(BF16)

    return pl.pallas_call(
        body, name=name, grid=(nblk,),
        out_shape=[jax.ShapeDtypeStruct((R, Dm), F32), jax.ShapeDtypeStruct((R, Dm), BF16)],
        in_specs=[prev_spec, _row_spec(tm, Dm), next_spec,
                  pl.BlockSpec((CONV_WIDTH, Dm), lambda i: (0, 0)),
                  _vec_spec(Dm), _vec_spec(Dm), _vec_spec(Dm)],
        out_specs=[_row_spec(tm, Dm), _row_spec(tm, Dm)],
        scratch_shapes=[pltpu.VMEM((tm + 2 * CONV_HALO, Dm), F32)],
        compiler_params=pltpu.CompilerParams(dimension_semantics=("parallel",)),
    )(hg, hg, hg, w_dw, b_dw, ln_g, ln_b)


def _ln_silu_bwd(dhs, hd, ln_g, ln_b, name):
    R, Dm = hd.shape
    tm = _tile(R, ROW_BLOCK, 8)

    def body(dhs_ref, hd_ref, g_ref, b_ref, dhd_ref, dg_ref, db_ref, dsum_ref):
        i = pl.program_id(0)

        @pl.when(i == 0)
        def _():
            dg_ref[...] = jnp.zeros_like(dg_ref)
            db_ref[...] = jnp.zeros_like(db_ref)
            dsum_ref[...] = jnp.zeros_like(dsum_ref)

        hd = hd_ref[...]
        mu = jnp.mean(hd, axis=-1, keepdims=True)
        xc = hd - mu
        rs = lax.rsqrt(jnp.mean(xc * xc, axis=-1, keepdims=True) + EPS)
        z = xc * rs
        hl = z * g_ref[...] + b_ref[...]
        dhl = dhs_ref[...] * _dsilu(hl)
        dg_ref[...] += jnp.sum(dhl * z, axis=0, keepdims=True)
        db_ref[...] += jnp.sum(dhl, axis=0, keepdims=True)
        dz = dhl * g_ref[...]
        dhd = rs * (dz - jnp.mean(dz, axis=-1, keepdims=True) - z * jnp.mean(dz * z, axis=-1, keepdims=True))
        dsum_ref[...] += jnp.sum(dhd, axis=0, keepdims=True)
        dhd_ref[...] = dhd

    return pl.pallas_call(
        body, name=name, grid=(R // tm,),
        out_shape=[jax.ShapeDtypeStruct((R, Dm), F32)] + [jax.ShapeDtypeStruct((1, Dm), F32)] * 3,
        in_specs=[_row_spec(tm, Dm), _row_spec(tm, Dm), _vec_spec(Dm), _vec_spec(Dm)],
        out_specs=[_row_spec(tm, Dm), _vec_spec(Dm), _vec_spec(Dm), _vec_spec(Dm)],
        compiler_params=pltpu.CompilerParams(dimension_semantics=("arbitrary",)),
    )(dhs, hd, ln_g, ln_b)


def _conv_bwd(dhd, hg, w_dw, name):
    R, Dm = hg.shape
    tm = _tile(R, ROW_BLOCK, CONV_HALO)
    nblk = R // tm
    prev_spec, next_spec = _halo_specs(tm, nblk, Dm)
    half = CONV_WIDTH // 2

    def body(dprev, dcur, dnext, gprev, gcur, gnext, w_ref, dhg_ref, dw_ref, dscr, gscr):
        i = pl.program_id(0)

        @pl.when(i == 0)
        def _():
            dw_ref[...] = jnp.zeros_like(dw_ref)

        _fill_halo(dscr, dprev, dcur, dnext, i, nblk, tm)
        _fill_halo(gscr, gprev, gcur, gnext, i, nblk, tm)
        d_here = dcur[...]
        acc = jnp.zeros((tm, Dm), F32)
        for j in range(CONV_WIDTH):
            acc = acc + w_ref[j:j + 1, :] * dscr[pl.ds(CONV_HALO + half - j, tm), :]
            dw_ref[j:j + 1, :] += jnp.sum(d_here * gscr[pl.ds(CONV_HALO - half + j, tm), :], axis=0, keepdims=True)
        dhg_ref[...] = acc

    return pl.pallas_call(
        body, name=name, grid=(nblk,),
        out_shape=[jax.ShapeDtypeStruct((R, Dm), F32), jax.ShapeDtypeStruct((CONV_WIDTH, Dm), F32)],
        in_specs=[prev_spec, _row_spec(tm, Dm), next_spec, prev_spec, _row_spec(tm, Dm), next_spec,
                  pl.BlockSpec((CONV_WIDTH, Dm), lambda i: (0, 0))],
        out_specs=[_row_spec(tm, Dm), pl.BlockSpec((CONV_WIDTH, Dm), lambda i: (0, 0))],
        scratch_shapes=[pltpu.VMEM((tm + 2 * CONV_HALO, Dm), F32)] * 2,
        compiler_params=pltpu.CompilerParams(dimension_semantics=("arbitrary",)),
    )(dhd, dhd, dhd, hg, hg, hg, w_dw)


def _swap16(y, lane):
    return jnp.where((lane & 16) == 0, pltpu.roll(y, LANES - 16, 1), pltpu.roll(y, 16, 1))


def _head_mean(v, bd):
    hi, lo = _split_bf16(v)
    return (_dot(hi, bd, NN) + _dot(lo, bd, NN)) * (1.0 / HEAD_DIM)


Q_COLS = (0, ATTN_WIDTH)
K_COLS = (ATTN_WIDTH, ATTN_WIDTH + HEAD_DIM * 2)
V_COLS = (K_COLS[1], K_COLS[1] + HEAD_DIM * 2)
SU_COLS = (V_COLS[1], V_COLS[1] + SG_WIDTH)
SV_COLS = (SU_COLS[1], SU_COLS[1] + SG_WIDTH)


def _mix_prep_fwd(p, ctx_rows, cos, sin, qg, kg, bd, w_sp, b_spt, name):
    TT = p.shape[0]
    off = ctx_rows // CHUNK
    q_scale = HEAD_DIM ** -0.5

    def body(p_ref, cos_ref, sin_ref, qg_ref, kg_ref, bd_ref, w_ref, b_ref,
             q_ref, kp_ref, vp_ref, kt_ref, sg_ref):
        lane = lax.broadcasted_iota(jnp.int32, (CHUNK, LANES), 1)
        low = lane < HEAD_DIM
        cs, sn, bdv = cos_ref[...], sin_ref[...], bd_ref[...]

        def norm_rope(xv, gain):
            r = lax.rsqrt(_head_mean(xv * xv, bdv) + EPS)
            yv = xv * r * gain
            return yv * cs + _swap16(yv, lane) * sn

        def pad_heads(ref, t):
            tr = pltpu.roll(t, HEAD_DIM, 1)
            ref[0, 0] = jnp.where(low, t, 0.0).astype(BF16)
            ref[0, 1] = jnp.where(low, 0.0, tr).astype(BF16)
            ref[1, 0] = jnp.where(low, tr, 0.0).astype(BF16)
            ref[1, 1] = jnp.where(low, 0.0, t).astype(BF16)

        for a in range(ATTN_WIDTH // LANES):
            xv = p_ref[:, a * LANES:(a + 1) * LANES]
            q_ref[:, a * LANES:(a + 1) * LANES] = (norm_rope(xv, qg_ref[...]) * q_scale).astype(BF16)
        kh = norm_rope(p_ref[:, K_COLS[0]:K_COLS[1]], kg_ref[...])
        pad_heads(kp_ref, kh)
        pad_heads(vp_ref, p_ref[:, V_COLS[0]:V_COLS[1]])
        kht = kh.T
        kt_ref[0] = kht[:HEAD_DIM].astype(BF16)
        kt_ref[1] = kht[HEAD_DIM:].astype(BF16)
        for g in range(N_SG_GROUPS):
            u = _gelu(p_ref[:, SU_COLS[0] + g * LANES:SU_COLS[0] + (g + 1) * LANES])
            vg = _gelu(p_ref[:, SV_COLS[0] + g * LANES:SV_COLS[0] + (g + 1) * LANES])
            xc = vg - jnp.mean(vg, axis=-1, keepdims=True)
            vn = xc * lax.rsqrt(jnp.mean(xc * xc, axis=-1, keepdims=True) + EPS)
            mixed = _dot(w_ref[g].astype(BF16), vn.astype(BF16), NN) + b_ref[:, g:g + 1]
            sg_ref[:, g * LANES:(g + 1) * LANES] = (u * mixed).astype(BF16)

    def row(width):
        return pl.BlockSpec((CHUNK, width), lambda i: (i, 0))

    def whole(shape):
        return pl.BlockSpec(shape, lambda i: (0,) * len(shape))

    pad_spec = pl.BlockSpec((2, 2, CHUNK, LANES), lambda i: (0, 0, i, 0))
    return pl.pallas_call(
        body, name=name, grid=(TT // CHUNK,),
        out_shape=[jax.ShapeDtypeStruct((TT, ATTN_WIDTH), BF16),
                   jax.ShapeDtypeStruct((2, 2, TT, LANES), BF16), jax.ShapeDtypeStruct((2, 2, TT, LANES), BF16),
                   jax.ShapeDtypeStruct((2, HEAD_DIM, TT), BF16),
                   jax.ShapeDtypeStruct((TT - ctx_rows, ATTN_WIDTH + SG_WIDTH), BF16)],
        in_specs=[row(IN_WIDTH), row(LANES), row(LANES), whole((1, LANES)), whole((1, LANES)),
                  whole((LANES, LANES)), whole((N_SG_GROUPS, CHUNK, CHUNK)), whole((CHUNK, N_SG_GROUPS))],
        out_specs=[row(ATTN_WIDTH), pad_spec, pad_spec,
                   pl.BlockSpec((2, HEAD_DIM, CHUNK), lambda i: (0, 0, i)),
                   pl.BlockSpec((CHUNK, SG_WIDTH), lambda i: (jnp.maximum(i - off, 0), 1))],
        compiler_params=pltpu.CompilerParams(dimension_semantics=("arbitrary",)),
    )(p, cos, sin, qg, kg, bd, w_sp, b_spt)


def _mix_prep_bwd(p, dq, f, dao, ctx_rows, cos, sin, qg, kg, bd, w_sp, w_spt, b_spt, name):
    TT = p.shape[0]
    off = ctx_rows // CHUNK
    q_scale = HEAD_DIM ** -0.5

    def body(p_ref, dq_ref, f_ref, dsg_ref, cos_ref, sin_ref, qg_ref, kg_ref, bd_ref, w_ref, wt_ref,
             b_ref, dp_ref, dqg_ref, dkg_ref, dw_ref, db_ref):
        i = pl.program_id(0)

        @pl.when(i == 0)
        def _():
            dqg_ref[...] = jnp.zeros_like(dqg_ref)
            dkg_ref[...] = jnp.zeros_like(dkg_ref)
            dw_ref[...] = jnp.zeros_like(dw_ref)
            db_ref[...] = jnp.zeros_like(db_ref)

        latent = (i >= off).astype(F32)
        lane = lax.broadcasted_iota(jnp.int32, (CHUNK, LANES), 1)
        low = lane < HEAD_DIM
        cs, sn, bdv = cos_ref[...], sin_ref[...], bd_ref[...]

        def fold(b0):
            return jnp.where(low, f_ref[0, b0] + pltpu.roll(f_ref[0, b0 + 1], HEAD_DIM, 1),
                             pltpu.roll(f_ref[1, b0], HEAD_DIM, 1) + f_ref[1, b0 + 1])

        def norm_rope_bwd(xv, dout, gain):
            r = lax.rsqrt(_head_mean(xv * xv, bdv) + EPS)
            n = xv * r
            dy = dout * cs + _swap16(dout * sn, lane)
            dn = dy * gain
            dx = r * (dn - n * _head_mean(dn * n, bdv))
            return dx, jnp.sum(dy * n, axis=0, keepdims=True)

        for a in range(ATTN_WIDTH // LANES):
            cols = slice(a * LANES, (a + 1) * LANES)
            dx, dg = norm_rope_bwd(p_ref[:, cols], dq_ref[:, cols] * (latent * q_scale), qg_ref[...])
            dp_ref[:, cols] = dx.astype(BF16)
            dqg_ref[...] += dg
        dx, dg = norm_rope_bwd(p_ref[:, K_COLS[0]:K_COLS[1]], fold(0), kg_ref[...])
        dp_ref[:, K_COLS[0]:K_COLS[1]] = dx.astype(BF16)
        dkg_ref[...] += dg
        dp_ref[:, V_COLS[0]:V_COLS[1]] = fold(2).astype(BF16)
        for g in range(N_SG_GROUPS):
            su = p_ref[:, SU_COLS[0] + g * LANES:SU_COLS[0] + (g + 1) * LANES]
            sv = p_ref[:, SV_COLS[0] + g * LANES:SV_COLS[0] + (g + 1) * LANES]
            u, vg = _gelu(su), _gelu(sv)
            xc = vg - jnp.mean(vg, axis=-1, keepdims=True)
            rs = lax.rsqrt(jnp.mean(xc * xc, axis=-1, keepdims=True) + EPS)
            vn = xc * rs
            vnb = vn.astype(BF16)
            mixed = _dot(w_ref[g].astype(BF16), vnb, NN) + b_ref[:, g:g + 1]
            dsg = dsg_ref[:, g * LANES:(g + 1) * LANES] * latent
            du = dsg * mixed
            dmix = dsg * u
            dmb = dmix.astype(BF16)
            db_ref[:, g:g + 1] += jnp.sum(dmix, axis=-1, keepdims=True)
            dw_ref[g] += _dot(dmb, vnb, NT)
            dvn = _dot(wt_ref[g].astype(BF16), dmb, NN)
            dvg = rs * (dvn - jnp.mean(dvn, axis=-1, keepdims=True)
                        - vn * jnp.mean(dvn * vn, axis=-1, keepdims=True))
            dp_ref[:, SU_COLS[0] + g * LANES:SU_COLS[0] + (g + 1) * LANES] = (du * _dgelu(su)).astype(BF16)
            dp_ref[:, SV_COLS[0] + g * LANES:SV_COLS[0] + (g + 1) * LANES] = (dvg * _dgelu(sv)).astype(BF16)

    def row(width):
        return pl.BlockSpec((CHUNK, width), lambda i: (i, 0))

    def latent_row(width, col_block):
        return pl.BlockSpec((CHUNK, width), lambda i: (jnp.maximum(i - off, 0), col_block))

    def whole(shape):
        return pl.BlockSpec(shape, lambda i: (0,) * len(shape))

    return pl.pallas_call(
        body, name=name, grid=(TT // CHUNK,),
        out_shape=[jax.ShapeDtypeStruct((TT, IN_WIDTH), BF16), jax.ShapeDtypeStruct((1, LANES), F32),
                   jax.ShapeDtypeStruct((1, LANES), F32),
                   jax.ShapeDtypeStruct((N_SG_GROUPS, CHUNK, CHUNK), F32),
                   jax.ShapeDtypeStruct((CHUNK, N_SG_GROUPS), F32)],
        in_specs=[row(IN_WIDTH), latent_row(ATTN_WIDTH, 0),
                  pl.BlockSpec((2, 4, CHUNK, LANES), lambda i: (0, 0, i, 0)),
                  latent_row(SG_WIDTH, 1), row(LANES), row(LANES), whole((1, LANES)), whole((1, LANES)),
                  whole((LANES, LANES)), whole((N_SG_GROUPS, CHUNK, CHUNK)),
                  whole((N_SG_GROUPS, CHUNK, CHUNK)), whole((CHUNK, N_SG_GROUPS))],
        out_specs=[row(IN_WIDTH), whole((1, LANES)), whole((1, LANES)),
                   whole((N_SG_GROUPS, CHUNK, CHUNK)), whole((CHUNK, N_SG_GROUPS))],
        compiler_params=pltpu.CompilerParams(dimension_semantics=("arbitrary",)),
    )(p, dq, f, dao, cos, sin, qg, kg, bd, w_sp, w_spt, b_spt)


def _attn_fwd(q, kpad, vpad, ao, ctx_rows, name, tq=256):
    TT = q.shape[0]
    T = TT - ctx_rows
    tq = _tile(T, tq)
    off = ctx_rows // tq
    group = 2 * LANES

    def body(q_ref, k_ref, v_ref, ao_in, o_ref):
        del ao_in
        for a in range(2):
            acc = jnp.zeros((tq, LANES), F32)
            qa = q_ref[:, a * LANES:(a + 1) * LANES]
            for b in range(2):
                s = _dot(qa, k_ref[0, b], NT)
                e = jnp.exp(s - jnp.max(s, axis=-1, keepdims=True))
                inv = 1.0 / jnp.sum(e, axis=-1, keepdims=True)
                acc = acc + _dot(e.astype(BF16), v_ref[0, b], NN) * inv
            o_ref[:, a * LANES:(a + 1) * LANES] = acc.astype(BF16)

    kv_spec = pl.BlockSpec((1, 2, TT, LANES), lambda j, i: (j, 0, 0, 0))
    return pl.pallas_call(
        body, name=name, grid=(2, T // tq),
        out_shape=jax.ShapeDtypeStruct(ao.shape, BF16),
        in_specs=[pl.BlockSpec((tq, group), lambda j, i: (i + off, j)), kv_spec, kv_spec,
                  pl.BlockSpec(memory_space=pl.ANY)],
        out_specs=pl.BlockSpec((tq, group), lambda j, i: (i, j)),
        input_output_aliases={3: 0},
        compiler_params=pltpu.CompilerParams(dimension_semantics=("parallel", "parallel")),
    )(q, kpad, vpad, ao)


def _attn_bwd(q, dao, kpad, vpad, kt, ctx_rows, name, tq=128):
    TT = q.shape[0]
    T = TT - ctx_rows
    tq = _tile(T, tq)
    off = ctx_rows // tq
    group = 2 * LANES

    def body(q_ref, do_ref, k_ref, v_ref, kt_ref, dq_ref, f_ref):
        i = pl.program_id(1)

        @pl.when(i == 0)
        def _():
            f_ref[...] = jnp.zeros_like(f_ref)

        ktv = kt_ref[0]
        for a in range(2):
            qa = q_ref[:, a * LANES:(a + 1) * LANES]
            doa = do_ref[:, a * LANES:(a + 1) * LANES].astype(BF16)
            halves = []
            for b in range(2):
                st = _dot(k_ref[0, b], qa, NT)
                e = jnp.exp(st - jnp.max(st, axis=0, keepdims=True))
                pt = e * (1.0 / jnp.sum(e, axis=0, keepdims=True))
                dpt = _dot(v_ref[0, b], doa, NT)
                delta = jnp.sum(dpt * pt, axis=0, keepdims=True)
                dst = (pt * (dpt - delta)).astype(BF16)
                f_ref[0, b] += _dot(dst, qa, NN)
                f_ref[0, 2 + b] += _dot(pt.astype(BF16), doa, NN)
                halves.append(_dot(ktv, dst, NN))
            dq_ref[:, a * LANES:(a + 1) * LANES] = jnp.concatenate(halves, axis=0).T

    kv_spec = pl.BlockSpec((1, 2, TT, LANES), lambda j, i: (j, 0, 0, 0))
    return pl.pallas_call(
        body, name=name, grid=(2, T // tq),
        out_shape=[jax.ShapeDtypeStruct((T, ATTN_WIDTH), F32), jax.ShapeDtypeStruct((2, 4, TT, LANES), F32)],
        in_specs=[pl.BlockSpec((tq, group), lambda j, i: (i + off, j)),
                  pl.BlockSpec((tq, group), lambda j, i: (i, j)),
                  kv_spec, kv_spec, pl.BlockSpec((1, HEAD_DIM, TT), lambda j, i: (j, 0, 0))],
        out_specs=[pl.BlockSpec((tq, group), lambda j, i: (i, j)),
                   pl.BlockSpec((1, 4, TT, LANES), lambda j, i: (j, 0, 0, 0))],
        compiler_params=pltpu.CompilerParams(dimension_semantics=("parallel", "arbitrary")),
    )(q, dao, kpad, vpad, kt)


def _final_fwd_bwd(h, g, target, name):
    R, Dm = h.shape
    tm = _tile(R, ROW_BLOCK, 8)

    def body(h_ref, g_ref, t_ref, dh_ref, loss_ref, dg_ref):
        i = pl.program_id(0)

        @pl.when(i == 0)
        def _():
            loss_ref[...] = jnp.zeros_like(loss_ref)
            dg_ref[...] = jnp.zeros_like(dg_ref)

        hv = h_ref[...]
        r = lax.rsqrt(jnp.mean(hv * hv, axis=-1, keepdims=True) + EPS)
        n = hv * r
        diff = n * g_ref[...] - t_ref[...]
        loss_ref[...] += jnp.sum(diff * diff)
        dout = diff * (1.0 / Dm)
        dg_ref[...] += jnp.sum(dout * n, axis=0, keepdims=True)
        dn = dout * g_ref[...]
        dh_ref[...] = r * (dn - n * jnp.mean(dn * n, axis=-1, keepdims=True))

    return pl.pallas_call(
        body, name=name, grid=(R // tm,),
        out_shape=[jax.ShapeDtypeStruct((R, Dm), F32), jax.ShapeDtypeStruct((1, LANES), F32),
                   jax.ShapeDtypeStruct((1, Dm), F32)],
        in_specs=[_row_spec(tm, Dm), _vec_spec(Dm), _row_spec(tm, Dm)],
        out_specs=[_row_spec(tm, Dm), _vec_spec(LANES), _vec_spec(Dm)],
        compiler_params=pltpu.CompilerParams(dimension_semantics=("arbitrary",)),
    )(h, g, target)


MOD_ROWS = 16


def _mod_fwd(c_rows, w_mod, name):
    L, Dm, n = w_mod.shape

    def body(c_ref, w_ref, o_ref):
        o_ref[0] = _dot3(_silu(c_ref[...]), w_ref[0], NN)

    return pl.pallas_call(
        body, name=name, grid=(L,),
        out_shape=jax.ShapeDtypeStruct((L, MOD_ROWS, n), F32),
        in_specs=[pl.BlockSpec((MOD_ROWS, Dm), lambda l: (0, 0)), pl.BlockSpec((1, Dm, n), lambda l: (l, 0, 0))],
        out_specs=pl.BlockSpec((1, MOD_ROWS, n), lambda l: (l, 0, 0)),
        compiler_params=pltpu.CompilerParams(dimension_semantics=("parallel",)),
    )(c_rows, w_mod)


def _mod_bwd(c_rows_t, dmod, w_mod, name):
    L, Dm, n = w_mod.shape

    def body(ct_ref, d_ref, w_ref, gw_ref, ds_ref):
        dm = d_ref[0]
        gw_ref[0] = _dot3(_silu(ct_ref[...]), dm, NN)
        ds_ref[0] = _dot3(dm[:MOD_ROWS], w_ref[0], NT)

    return pl.pallas_call(
        body, name=name, grid=(L,),
        out_shape=[jax.ShapeDtypeStruct((L, Dm, n), F32), jax.ShapeDtypeStruct((L, MOD_ROWS, Dm), F32)],
        in_specs=[pl.BlockSpec((Dm, LANES), lambda l: (0, 0)), pl.BlockSpec((1, LANES, n), lambda l: (l, 0, 0)),
                  pl.BlockSpec((1, Dm, n), lambda l: (l, 0, 0))],
        out_specs=[pl.BlockSpec((1, Dm, n), lambda l: (l, 0, 0)),
                   pl.BlockSpec((1, MOD_ROWS, Dm), lambda l: (l, 0, 0))],
        compiler_params=pltpu.CompilerParams(dimension_semantics=("parallel",)),
    )(c_rows_t, dmod, w_mod)


def _adam_update(w, g, m, v):
    c1 = 1.0 - ADAM_B1 ** ADAM_STEP
    c2 = 1.0 - ADAM_B2 ** ADAM_STEP
    mn = ADAM_B1 * m + (1.0 - ADAM_B1) * g
    vn = ADAM_B2 * v + (1.0 - ADAM_B2) * (g * g)
    return -ADAM_LR * ((mn / c1) / (jnp.sqrt(vn / c2) + ADAM_EPS) + ADAM_WD * w), mn, vn


def _adamw(w, g, m, v, name):
    R, Cw = w.shape
    tm = _tile(R, ROW_BLOCK, 8)

    def body(w_ref, g_ref, m_ref, v_ref, d_ref, mo_ref, vo_ref):
        d_ref[...], mo_ref[...], vo_ref[...] = _adam_update(w_ref[...], g_ref[...], m_ref[...], v_ref[...])

    spec = pl.BlockSpec((tm, Cw), lambda i: (i, 0))
    return pl.pallas_call(
        body, name=name, grid=(R // tm,),
        out_shape=[jax.ShapeDtypeStruct((R, Cw), F32)] * 3,
        in_specs=[spec] * 4, out_specs=[spec] * 3,
        compiler_params=pltpu.CompilerParams(dimension_semantics=("parallel",)),
    )(w, g, m, v)


def _adamw_recv(w, m, v, recvs, name):
    L, R, n = w.shape
    tm = _tile(R, ROW_BLOCK, 8)
    nblk = R // tm
    parts = [r.reshape(N_DEV, R, n) for r in recvs]

    def body(*refs):
        w_ref, m_ref, v_ref = refs[:3]
        part_refs = refs[3:3 + L]
        g_ref, d_ref, mo_ref, vo_ref, gsum = refs[3 + L:]
        l = pl.program_id(0)
        for ll in range(L):
            @pl.when(l == ll)
            def _(ll=ll):
                acc = part_refs[ll][0].astype(F32)
                for s in range(1, N_DEV):
                    acc = acc + part_refs[ll][s].astype(F32)
                gsum[...] = acc
        g = gsum[...]
        g_ref[0] = g
        d_ref[0], mo_ref[0], vo_ref[0] = _adam_update(w_ref[0], g, m_ref[0], v_ref[0])

    def part_spec(ll):
        return pl.BlockSpec((N_DEV, tm, n), lambda l, i: (0, jnp.where(l == ll, i, jnp.where(l < ll, 0, nblk - 1)), 0))

    spec = pl.BlockSpec((1, tm, n), lambda l, i: (l, i, 0))
    return pl.pallas_call(
        body, name=name, grid=(L, nblk),
        out_shape=[jax.ShapeDtypeStruct((L, R, n), F32)] * 4,
        in_specs=[spec] * 3 + [part_spec(ll) for ll in range(L)], out_specs=[spec] * 4,
        scratch_shapes=[pltpu.VMEM((tm, n), F32)],
        compiler_params=pltpu.CompilerParams(dimension_semantics=("parallel", "parallel")),
    )(w, m, v, *parts)


def _pack(parts, row_mult=8):
    flat, offs, pos = [], [], 0
    for t in parts:
        t = t.reshape(-1).astype(F32)
        size = -(-t.shape[0] // LANES) * LANES
        flat.append(jnp.pad(t, (0, size - t.shape[0])))
        offs.append(pos)
        pos += size
    total = -(-pos // (LANES * row_mult)) * (LANES * row_mult)
    if total > pos:
        flat.append(jnp.zeros((total - pos,), F32))
    return jnp.concatenate(flat).reshape(-1, LANES), offs


def _take(buf, off, shape):
    size = math.prod(shape)
    return buf[..., off:off + size].reshape(buf.shape[:-1] + tuple(shape))


def _rope_tables(T, ctx_rows):
    pos = jnp.arange(T)
    row = (pos // GRID_W).astype(F32)
    col = (pos % GRID_W).astype(F32)
    half = HEAD_DIM // 4
    inv = ROPE_THETA ** (-jnp.arange(0, 2 * half, 2, dtype=F32) / (2 * half))
    ang_r, ang_c = row[:, None] * inv[None, :], col[:, None] * inv[None, :]
    cos = jnp.concatenate([jnp.cos(ang_r)] * 2 + [jnp.cos(ang_c)] * 2, axis=1)
    sin = jnp.concatenate([-jnp.sin(ang_r), jnp.sin(ang_r), -jnp.sin(ang_c), jnp.sin(ang_c)], axis=1)
    cos = jnp.concatenate([jnp.ones((ctx_rows, HEAD_DIM), F32), cos], axis=0)
    sin = jnp.concatenate([jnp.zeros((ctx_rows, HEAD_DIM), F32), sin], axis=0)
    return jnp.tile(cos, (1, 2)), jnp.tile(sin, (1, 2))


def kernel(x, c, ctx, c_ctx, w_mod, b_mod, g_mix, g_ffn, w_ffn_in, w_ffn_out, w_in, q_gain, k_gain, w_sp, b_sp, w_out, w_pw1, b_pw1, w_dw, b_dw, ln_g, ln_b, w_pw2, b_pw2, g_final, loss_target, m_c_ctx, m_w_mod, m_b_mod, m_g_mix, m_g_ffn, m_w_ffn_in, m_w_ffn_out, m_w_in, m_q_gain, m_k_gain, m_w_sp, m_b_sp, m_w_out, m_w_pw1, m_b_pw1, m_w_dw, m_b_dw, m_ln_g, m_ln_b, m_w_pw2, m_b_pw2, m_g_final, v_c_ctx, v_w_mod, v_b_mod, v_g_mix, v_g_ffn, v_w_ffn_in, v_w_ffn_out, v_w_in, v_q_gain, v_k_gain, v_w_sp, v_b_sp, v_w_out, v_w_pw1, v_b_pw1, v_w_dw, v_b_dw, v_ln_g, v_ln_b, v_w_pw2, v_b_pw2, v_g_final):
    weights = dict(c_ctx=c_ctx, w_mod=w_mod, b_mod=b_mod, g_mix=g_mix, g_ffn=g_ffn, w_ffn_in=w_ffn_in,
                   w_ffn_out=w_ffn_out, w_in=w_in, q_gain=q_gain, k_gain=k_gain, w_sp=w_sp, b_sp=b_sp,
                   w_out=w_out, w_pw1=w_pw1, b_pw1=b_pw1, w_dw=w_dw, b_dw=b_dw, ln_g=ln_g, ln_b=ln_b,
                   w_pw2=w_pw2, b_pw2=b_pw2, g_final=g_final)
    moments_m = dict(c_ctx=m_c_ctx, w_mod=m_w_mod, b_mod=m_b_mod, g_mix=m_g_mix, g_ffn=m_g_ffn,
                     w_ffn_in=m_w_ffn_in, w_ffn_out=m_w_ffn_out, w_in=m_w_in, q_gain=m_q_gain,
                     k_gain=m_k_gain, w_sp=m_w_sp, b_sp=m_b_sp, w_out=m_w_out, w_pw1=m_w_pw1,
                     b_pw1=m_b_pw1, w_dw=m_w_dw, b_dw=m_b_dw, ln_g=m_ln_g, ln_b=m_ln_b, w_pw2=m_w_pw2,
                     b_pw2=m_b_pw2, g_final=m_g_final)
    moments_v = dict(c_ctx=v_c_ctx, w_mod=v_w_mod, b_mod=v_b_mod, g_mix=v_g_mix, g_ffn=v_g_ffn,
                     w_ffn_in=v_w_ffn_in, w_ffn_out=v_w_ffn_out, w_in=v_w_in, q_gain=v_q_gain,
                     k_gain=v_k_gain, w_sp=v_w_sp, b_sp=v_b_sp, w_out=v_w_out, w_pw1=v_w_pw1,
                     b_pw1=v_b_pw1, w_dw=v_w_dw, b_dw=v_b_dw, ln_g=v_ln_g, ln_b=v_ln_b, w_pw2=v_w_pw2,
                     b_pw2=v_b_pw2, g_final=v_g_final)
    names = list(weights)

    T, C = x.shape[1], ctx.shape[1]
    Dm = D_MODEL
    me = 4 * lax.axis_index("x") + 2 * lax.axis_index("y") + lax.axis_index("c")
    h0 = x[0]
    ctx2 = ctx[0]
    target = loss_target[0]

    small_sharded = (("w_dw", w_dw[0]), ("b_pw1", b_pw1), ("b_dw", b_dw), ("ln_g", ln_g), ("ln_b", ln_b),
                     ("b_pw2", b_pw2))
    buf1, offs1 = _pack([c] + [t for _, t in small_sharded])
    got1 = _all_gather([buf1], "gather_cond", True)[0].reshape(N_DEV, -1)
    c_all = _take(got1, offs1[0], (Dm,))
    full_small = {}
    for (nm, t), off in zip(small_sharded, offs1[1:]):
        seg = _take(got1, off, t.shape)
        full_small[nm] = jnp.moveaxis(seg, 0, -2).reshape(t.shape[:-1] + (N_DEV * t.shape[-1],))
    w_dw_f, b_pw1_f = full_small["w_dw"], full_small["b_pw1"]
    b_dw_f, ln_g_f, ln_b_f, b_pw2_f = (full_small[k] for k in ("b_dw", "ln_g", "ln_b", "b_pw2"))

    c_rows = jnp.concatenate([c_all, c_ctx[None, :], jnp.zeros((MOD_ROWS - N_DEV - 1, Dm), F32)], axis=0)
    mod_part = _mod_fwd(c_rows, w_mod, "mod_fwd")
    n_mod = w_mod.shape[2]
    got2 = _all_gather([mod_part.reshape(-1, LANES)], "gather_mod", True)[0]
    mod_all = got2.reshape(N_DEV, 2, MOD_ROWS, n_mod).transpose(1, 2, 0, 3).reshape(2, MOD_ROWS, N_DEV * n_mod)
    mod_all = mod_all + b_mod[:, None, :]
    my_mod = lax.dynamic_index_in_dim(mod_all, me, axis=1, keepdims=False)
    sh1, sc1, gt1, sh2, sc2, gt2 = ([my_mod[l:l + 1, k * Dm:(k + 1) * Dm] for l in range(2)] for k in range(6))
    csh1 = mod_all[0, N_DEV:N_DEV + 1, 0:Dm]
    csc1 = mod_all[0, N_DEV:N_DEV + 1, Dm:2 * Dm]

    shards = [w_ffn_in[0], w_ffn_in[1], w_ffn_out[0], w_ffn_out[1], w_in[0], w_out[0], w_pw1[0], w_pw2[0]]
    got = _all_gather([t.astype(BF16) for t in shards], "gather_weights", False)
    W_ffi = [got[l].reshape(N_DEV, Dm, FF_SHARD) for l in range(2)]
    W_ffo = [got[2 + l].reshape(N_DEV // 2, FF_SHARD, Dm) for l in range(2)]
    W_in = got[4].reshape(N_DEV, Dm, IN_WIDTH // N_DEV).transpose(1, 0, 2).reshape(Dm, IN_WIDTH)
    W_out = got[5]
    W_pw1 = got[6].reshape(N_DEV, Dm, 2 * Dm // N_DEV).transpose(1, 0, 2).reshape(Dm, 2 * Dm)
    W_pw2 = got[7]

    g_mix_r = [g_mix[l:l + 1] for l in range(2)]
    g_ffn_r = [g_ffn[l:l + 1] for l in range(2)]
    g_fin = g_final[None, :]

    cos, sin = _rope_tables(T, C)
    qg = jnp.tile(q_gain, (1, 2))
    kg = jnp.tile(k_gain, (1, 2))
    lane_head = jnp.arange(LANES) // HEAD_DIM
    bd = (lane_head[:, None] == lane_head[None, :]).astype(BF16)
    w_sp0 = w_sp[0]
    w_spt0 = w_sp0.transpose(0, 2, 1)
    b_spt0 = b_sp[0].T

    XM = _norm_mod_fwd_cat(ctx2, h0, g_mix_r[0], csc1, csh1, sc1[0], sh1[0], "norm_mix0")
    P = _mm(XM, W_in, "nn", "in_proj", tn=896)
    qh, kpad, vpad, kt, ao = _mix_prep_fwd(P, C, cos, sin, qg, kg, bd, w_sp0, b_spt0, "mix_prep")
    ao = _attn_fwd(qh, kpad, vpad, ao, C, "attn_fwd")
    h1, y0 = _mm(ao, W_out, "nn", "out_proj", res=h0, gate=gt1[0], raw_out=True)

    def ffn_fwd(h_in, l):
        xf = _norm_mod_fwd(h_in, g_ffn_r[l], sc2[l], sh2[l], f"norm_ffn{l}")
        gu = _mm_to_shards(xf, W_ffi[l], "nn", f"ffn_in{l}", F32)
        act = _swiglu_fwd(gu, f"swiglu{l}")
        h_out, f = _mm_over_shards(act, W_ffo[l], "nn", f"ffn_out{l}", res=h_in, gate=gt2[l], raw_out=True)
        return h_out, (xf, gu, act, f)

    h2, saved_ffn0 = ffn_fwd(h1, 0)

    xm1 = _norm_mod_fwd(h2, g_mix_r[1], sc1[1], sh1[1], "norm_mix1")
    ag = _mm(xm1, W_pw1, "nn", "pw1", bias=b_pw1_f)
    hg = _glu_fwd(ag, "glu")
    hd, hs = _conv_ln_fwd(hg, w_dw_f, b_dw_f, ln_g_f, ln_b_f, "conv_ln")
    h3, y1 = _mm(hs, W_pw2, "nn", "pw2", bias=b_pw2_f, res=h2, gate=gt1[1], raw_out=True)
    h4, saved_ffn1 = ffn_fwd(h3, 1)

    dh4, sq_err, dg_final = _final_fwd_bwd(h4, g_fin, target, "loss_head")
    loss_local = (0.5 / Dm) * sq_err[0, 0:1]

    def ffn_bwd(dh_out, h_in, saved, l):
        xf, gu, act, f = saved
        df, dgt, _ = _gate_bwd(dh_out, f, gt2[l], f"gate_ffn_bwd{l}")
        dw_out = _mm_tn_shard_rows(act, df, f"ffn_out_dw{l}", BF16)
        dact = _mm_to_shards(df, W_ffo[l], "nt", f"ffn_out_dx{l}", F32)
        dgu = _swiglu_bwd(gu, dact, f"swiglu_bwd{l}")
        dw_in = _mm_tn_shard_cols(xf, dgu, f"ffn_in_dw{l}", BF16)
        dxf = _mm_over_shards(dgu, W_ffi[l], "nt", f"ffn_in_dx{l}")
        dh_in, da, dsh = _norm_mod_bwd(h_in, g_ffn_r[l], sc2[l], dxf, dh_out, f"norm_ffn_bwd{l}")
        return dh_in, dw_in, dw_out, (dsh, da * g_ffn_r[l], dgt), da * (1.0 + sc2[l])

    dh3, dW_ffi1, dW_ffo1, dmod_ffn1, dg_ffn1 = ffn_bwd(dh4, h3, saved_ffn1, 1)

    dy1, dgt1_1, db_pw2 = _gate_bwd(dh3, y1, gt1[1], "gate_conv_bwd")
    dW_pw2 = _mm(hs, dy1, "tn", "pw2_dw", BF16, tk=512)
    dhs = _mm(dy1, W_pw2, "nt", "pw2_dx")
    dhd, dln_g, dln_b, db_dw = _ln_silu_bwd(dhs, hd, ln_g_f, ln_b_f, "ln_silu_bwd")
    dhg, dw_dw = _conv_bwd(dhd, hg, w_dw_f, "conv_bwd")
    dag, db_pw1 = _glu_bwd(ag, dhg, "glu_bwd")
    dW_pw1 = _mm(xm1, dag, "tn", "pw1_dw", BF16, tk=512)
    dxm1 = _mm(dag, W_pw1, "nt", "pw1_dx")
    dh2, da, dsh = _norm_mod_bwd(h2, g_mix_r[1], sc1[1], dxm1, dh3, "norm_mix1_bwd")
    dmod_mix1 = (dsh, da * g_mix_r[1], dgt1_1)
    dg_mix1 = da * (1.0 + sc1[1])

    dh1, dW_ffi0, dW_ffo0, dmod_ffn0, dg_ffn0 = ffn_bwd(dh2, h1, saved_ffn0, 0)

    dy0, dgt1_0, _ = _gate_bwd(dh1, y0, gt1[0], "gate_mix_bwd")
    dW_out = _mm(ao, dy0, "tn", "out_proj_dw", BF16, tk=512)
    dao = _mm(dy0, W_out, "nt", "out_proj_dx")
    dq, f_acc = _attn_bwd(qh, dao, kpad, vpad, kt, C, "attn_bwd")
    dP, dqg, dkg, dw_sp0, db_spt0 = _mix_prep_bwd(P, dq, f_acc, dao, C, cos, sin, qg, kg, bd, w_sp0, w_spt0,
                                                  b_spt0, "mix_prep_bwd")
    dW_in = _mm(XM, dP, "tn", "in_proj_dw", BF16, tn=896, tk=512)
    dXM = _mm(dP, W_in, "nt", "in_proj_dx", tk=896)
    dh0, da, dsh = _norm_mod_bwd(h0, g_mix_r[0], sc1[0], dXM, dh1, "norm_mix0_bwd", dxm_row_off=C)
    _, dac, dcsh = _norm_mod_bwd(ctx2, g_mix_r[0], csc1, dXM, None, "norm_ctx_bwd")
    dmod_mix0 = (dsh, da * g_mix_r[0], dgt1_0)
    dg_mix0 = da * (1.0 + sc1[0]) + dac * (1.0 + csc1)
    dcmod = jnp.concatenate([dcsh, dac * g_mix_r[0]], axis=1)

    dmod_mine = jnp.stack([jnp.concatenate(dmod_mix0 + dmod_ffn0, axis=1)[0],
                           jnp.concatenate(dmod_mix1 + dmod_ffn1, axis=1)[0]])

    small_grads = [
        ("loss", loss_local), ("g_final", dg_final), ("g_mix", jnp.concatenate([dg_mix0, dg_mix1])),
        ("g_ffn", jnp.concatenate([dg_ffn0, dg_ffn1])),
        ("q_gain", dqg[:, :HEAD_DIM] + dqg[:, HEAD_DIM:]), ("k_gain", dkg[:, :HEAD_DIM] + dkg[:, HEAD_DIM:]),
        ("w_sp", dw_sp0[None]), ("b_sp", db_spt0.T[None]), ("b_pw1", db_pw1), ("w_dw", dw_dw[None]),
        ("b_dw", db_dw), ("ln_g", dln_g), ("ln_b", dln_b), ("b_pw2", db_pw2), ("dcmod", dcmod),
        ("dmod", dmod_mine),
    ]
    buf3, offs3 = _pack([t for _, t in small_grads])
    got3 = _all_gather([buf3], "gather_small_grads", True)[0].reshape(N_DEV, buf3.shape[0], LANES)
    sum3 = _sum_devices(got3, "sum_small_grads").reshape(-1)
    off3 = {nm: off for (nm, _), off in zip(small_grads, offs3)}
    shape3 = {nm: t.shape for nm, t in small_grads}

    def summed(nm):
        return _take(sum3, off3[nm], shape3[nm])

    loss = summed("loss")[0]
    dcmod_sum = summed("dcmod")
    dmod_rows = _take(got3.reshape(N_DEV, -1), off3["dmod"], (2, 6 * Dm)).transpose(1, 0, 2)
    ctx_row = jnp.concatenate([jnp.pad(dcmod_sum, ((0, 0), (0, 4 * Dm))), jnp.zeros((1, 6 * Dm), F32)])
    dmod_all = jnp.concatenate([dmod_rows, ctx_row[:, None, :],
                                jnp.zeros((2, LANES - N_DEV - 1, 6 * Dm), F32)], axis=1)
    grads = {}
    grads["b_mod"] = summed("dmod") + ctx_row
    dmod_shard = lax.dynamic_slice_in_dim(dmod_all, me * n_mod, n_mod, axis=2)
    c_rows_t = jnp.pad(c_rows.T, ((0, 0), (0, LANES - MOD_ROWS)))
    grads["w_mod"], ds_part = _mod_bwd(c_rows_t, dmod_shard, w_mod, "mod_bwd")

    buf4, _ = _pack([ds_part[0, N_DEV]])
    got4 = _all_gather([buf4], "gather_c_ctx_grad", True)[0].reshape(N_DEV, buf4.shape[0], LANES)
    ds_ctx = _sum_devices(got4, "sum_c_ctx_grad").reshape(-1)[:Dm]
    grads["c_ctx"] = ds_ctx * _dsilu(c_ctx)

    for nm in ("g_final", "g_mix", "g_ffn", "q_gain", "k_gain", "w_sp", "b_sp"):
        grads[nm] = summed(nm).reshape(weights[nm].shape)
    for nm in ("b_pw1", "w_dw", "b_dw", "ln_g", "ln_b", "b_pw2"):
        n_loc = weights[nm].shape[-1]
        grads[nm] = lax.dynamic_slice_in_dim(summed(nm), me * n_loc, n_loc, axis=-1).reshape(weights[nm].shape)

    def col_shards(g, n):
        return g.reshape(Dm, N_DEV, n).transpose(1, 0, 2).reshape(N_DEV * Dm, n)

    send = [dW_ffi0.reshape(N_DEV * Dm, FF_SHARD), dW_ffi1.reshape(N_DEV * Dm, FF_SHARD),
            dW_ffo0.reshape(D_FF, Dm), dW_ffo1.reshape(D_FF, Dm),
            col_shards(dW_in, IN_WIDTH // N_DEV), dW_out, col_shards(dW_pw1, 2 * Dm // N_DEV), dW_pw2]
    recv = _shard_exchange(send, "exchange_weight_grads")

    delta, new_m, new_v = {}, {}, {}
    for nm, parts in (("w_ffn_in", recv[0:2]), ("w_ffn_out", recv[2:4]), ("w_in", recv[4:5]),
                      ("w_out", recv[5:6]), ("w_pw1", recv[6:7]), ("w_pw2", recv[7:8])):
        grads[nm], delta[nm], new_m[nm], new_v[nm] = _adamw_recv(
            weights[nm], moments_m[nm], moments_v[nm], parts, f"adamw_{nm}")
    shp = w_mod.shape
    outs = _adamw(w_mod.reshape(-1, shp[-1]), grads["w_mod"].reshape(-1, shp[-1]),
                  m_w_mod.reshape(-1, shp[-1]), v_w_mod.reshape(-1, shp[-1]), "adamw_w_mod")
    delta["w_mod"], new_m["w_mod"], new_v["w_mod"] = (o.reshape(shp) for o in outs)
    small_names = [nm for nm in names if nm not in delta]
    packs = [_pack([src[nm] for nm in small_names]) for src in (weights, grads, moments_m, moments_v)]
    offs_s = packs[0][1]
    outs = _adamw(*[pk[0] for pk in packs], "adamw_small")
    for o, dst in zip(outs, (delta, new_m, new_v)):
        o = o.reshape(-1)
        for nm, off in zip(small_names, offs_s):
            dst[nm] = _take(o, off, weights[nm].shape)

    return (loss, dh0[None], *[grads[n] for n in names], *[delta[n] for n in names],
            *[new_m[n] for n in names], *[new_v[n] for n in names])
```

```python
import math

import jax
import jax.numpy as jnp
from jax import lax
from jax.experimental import pallas as pl
from jax.experimental.pallas import tpu as pltpu

F32 = jnp.float32
BF16 = jnp.bfloat16
MESH = pl.DeviceIdType.MESH

N_DEV = 8
D_MODEL = 1024
EPS = 1e-6
HEAD_DIM = 64
ATTN_WIDTH = 512
KV_WIDTH = 128
SG_WIDTH = 512
N_SG_GROUPS = 4
CHUNK = 128
IN_WIDTH = 1792
D_FF = 2816
FF_SHARD = 2 * D_FF // N_DEV
CONV_WIDTH = 31
CONV_HALO = 16
GRID_W = 64
ROPE_THETA = 10000.0
LANES = 128
ROW_BLOCK = 256
ADAM_LR, ADAM_B1, ADAM_B2, ADAM_EPS, ADAM_WD, ADAM_STEP = 0.001, 0.9, 0.999, 1e-08, 0.01, 10


def _tile(n, target, mult=LANES):
    best = None
    for t in range(mult, min(n, target) + 1, mult):
        if n % t == 0:
            best = t
    return best if best is not None else n


def _sigmoid(x):
    return 1.0 / (1.0 + jnp.exp(-x))


def _silu(x):
    return x * _sigmoid(x)


def _dsilu(x):
    s = _sigmoid(x)
    return s * (1.0 + x * (1.0 - s))


_GELU_K = math.sqrt(2.0 / math.pi)


def _gelu(x):
    return 0.5 * x * (1.0 + jnp.tanh(_GELU_K * (x + 0.044715 * x * x * x)))


def _dgelu(x):
    t = jnp.tanh(_GELU_K * (x + 0.044715 * x * x * x))
    return 0.5 * (1.0 + t) + 0.5 * x * (1.0 - t * t) * _GELU_K * (1.0 + 3.0 * 0.044715 * x * x)


def _split_bf16(x):
    hi = x.astype(BF16)
    lo = (x - hi.astype(F32)).astype(BF16)
    return hi, lo


def _dot(a, b, dims):
    return lax.dot_general(a, b, (dims, ((), ())), preferred_element_type=F32)


def _dot3(a, b, dims):
    ah, al = _split_bf16(a)
    bh, bl = _split_bf16(b)
    return _dot(ah, bh, dims) + _dot(ah, bl, dims) + _dot(al, bh, dims)


NN = ((1,), (0,))
NT = ((1,), (1,))
TN = ((0,), (0,))


def _all_gather(xs, name, in_vmem):
    n_arr = len(xs)

    def body(*refs):
        x_refs, out_refs = refs[:n_arr], refs[n_arr:2 * n_arr]
        send_sems, recv_sems, local_sems = refs[2 * n_arr:]
        x, y, c = lax.axis_index("x"), lax.axis_index("y"), lax.axis_index("c")
        me, sibling = (x, y, c), (x, y, 1 - c)
        chips = [(1 - x, y), (x, 1 - y), (1 - x, 1 - y)]

        def rows(a, px, py, pc):
            m_per = xs[a].shape[0]
            return out_refs[a].at[pl.ds((4 * px + 2 * py + pc) * m_per, m_per), :]

        def copy(a, k, block, to, src=None):
            return pltpu.make_async_remote_copy(
                src_ref=rows(a, *block) if src is None else src,
                dst_ref=rows(a, *block),
                send_sem=send_sems.at[7 * a + k],
                recv_sem=recv_sems.at[7 * a + k],
                device_id=to,
                device_id_type=MESH,
            )

        mine, first, passed = [], [], []
        for a in range(n_arr):
            mine.append(pltpu.make_async_copy(x_refs[a], rows(a, *me), local_sems.at[a]))
            mine[-1].start()
            first.append(copy(a, 0, me, sibling, src=x_refs[a]))
            first += [copy(a, 1 + j, me, (*chip, c), src=x_refs[a]) for j, chip in enumerate(chips)]
        for cp in first:
            cp.start()
        for a in range(n_arr):
            for j, chip in enumerate(chips):
                copy(a, 1 + j, (*chip, c), me).wait_recv()
                passed.append(copy(a, 4 + j, (*chip, c), sibling))
                passed[-1].start()
        for a in range(n_arr):
            copy(a, 0, sibling, me).wait_recv()
            for j, chip in enumerate(chips):
                copy(a, 4 + j, (*chip, 1 - c), me).wait_recv()
        for cp in first + passed:
            cp.wait_send()
        for cp in mine:
            cp.wait()

    space = pltpu.VMEM if in_vmem else pl.ANY
    return pl.pallas_call(
        body,
        name=name,
        out_shape=[jax.ShapeDtypeStruct((N_DEV * t.shape[0], t.shape[1]), t.dtype) for t in xs],
        in_specs=[pl.BlockSpec(memory_space=space)] * n_arr,
        out_specs=[pl.BlockSpec(memory_space=space)] * n_arr,
        scratch_shapes=[
            pltpu.SemaphoreType.DMA((7 * n_arr,)),
            pltpu.SemaphoreType.DMA((7 * n_arr,)),
            pltpu.SemaphoreType.DMA((n_arr,)),
        ],
    )(*xs)


def _shard_exchange(gs, name):
    n_arr = len(gs)

    def body(*refs):
        g_refs, r_refs = refs[:n_arr], refs[n_arr:2 * n_arr]
        send_sems, recv_sems, local_sems = refs[2 * n_arr:]
        x, y, c = lax.axis_index("x"), lax.axis_index("y"), lax.axis_index("c")
        me = 4 * x + 2 * y + c

        def rows(ref, a, idx):
            m_per = gs[a].shape[0] // N_DEV
            return ref.at[pl.ds(idx * m_per, m_per), :]

        mine, sends, recvs = [], [], []
        for a in range(n_arr):
            mine.append(pltpu.make_async_copy(rows(g_refs[a], a, me), rows(r_refs[a], a, me), local_sems.at[a]))
            mine[-1].start()
            for k in range(1, N_DEV):
                px = 1 - x if (k >> 2) & 1 else x
                py = 1 - y if (k >> 1) & 1 else y
                pc = 1 - c if k & 1 else c
                peer = 4 * px + 2 * py + pc
                sem = 7 * a + k - 1
                sends.append(pltpu.make_async_remote_copy(
                    src_ref=rows(g_refs[a], a, peer), dst_ref=rows(r_refs[a], a, me),
                    send_sem=send_sems.at[sem], recv_sem=recv_sems.at[sem],
                    device_id=(px, py, pc), device_id_type=MESH))
                recvs.append(pltpu.make_async_remote_copy(
                    src_ref=rows(g_refs[a], a, me), dst_ref=rows(r_refs[a], a, peer),
                    send_sem=send_sems.at[sem], recv_sem=recv_sems.at[sem],
                    device_id=(px, py, pc), device_id_type=MESH))
        for cp in sends:
            cp.start()
        for cp in recvs:
            cp.wait_recv()
        for cp in sends:
            cp.wait_send()
        for cp in mine:
            cp.wait()

    return pl.pallas_call(
        body,
        name=name,
        out_shape=[jax.ShapeDtypeStruct(t.shape, t.dtype) for t in gs],
        in_specs=[pl.BlockSpec(memory_space=pl.ANY)] * n_arr,
        out_specs=[pl.BlockSpec(memory_space=pl.ANY)] * n_arr,
        scratch_shapes=[
            pltpu.SemaphoreType.DMA((7 * n_arr,)),
            pltpu.SemaphoreType.DMA((7 * n_arr,)),
            pltpu.SemaphoreType.DMA((n_arr,)),
        ],
    )(*gs)


def _sum_devices(r, name, rows_per_step=ROW_BLOCK):
    _, m, n = r.shape
    tm = _tile(m, rows_per_step, 8)

    def body(r_ref, o_ref):
        acc = r_ref[0].astype(F32)
        for s in range(1, N_DEV):
            acc = acc + r_ref[s].astype(F32)
        o_ref[...] = acc

    return pl.pallas_call(
        body,
        name=name,
        grid=(m // tm,),
        out_shape=jax.ShapeDtypeStruct((m, n), F32),
        in_specs=[pl.BlockSpec((N_DEV, tm, n), lambda i: (0, i, 0))],
        out_specs=pl.BlockSpec((tm, n), lambda i: (i, 0)),
        compiler_params=pltpu.CompilerParams(dimension_semantics=("parallel",)),
    )(r)


def _get(ref):
    return ref[0] if len(ref.shape) == 3 else ref[...]


def _put(ref, val):
    if len(ref.shape) == 3:
        ref[0] = val
    else:
        ref[...] = val


def _mm_call(name, a, b, a_spec, b_spec, out_sds, o_spec, grid, dims, acc_shape, bias=None,
             res=None, gate=None, raw_out=False, vec_spec=None):
    nk = grid[2]
    operands, in_specs = [a, b], [a_spec, b_spec]
    if bias is not None:
        operands.append(bias)
        in_specs.append(vec_spec)
    if res is not None:
        operands += [res, gate]
        in_specs += [o_spec, vec_spec]
    out_shape, out_specs = [out_sds], [o_spec]
    if raw_out:
        out_shape.append(jax.ShapeDtypeStruct(out_sds.shape, F32))
        out_specs.append(o_spec)

    def body(*refs):
        it = iter(refs)
        a_ref, b_ref = next(it), next(it)
        bias_ref = next(it) if bias is not None else None
        res_ref, gate_ref = (next(it), next(it)) if res is not None else (None, None)
        o_ref = next(it)
        raw_ref = next(it) if raw_out else None
        acc = next(it) if nk > 1 else None
        k = pl.program_id(2)
        part = _dot(_get(a_ref).astype(BF16), _get(b_ref).astype(BF16), dims)

        def finish(y):
            if bias_ref is not None:
                y = y + bias_ref[...]
            if raw_ref is not None:
                raw_ref[...] = y
            if res_ref is not None:
                y = res_ref[...] + gate_ref[...] * y
            _put(o_ref, y.astype(out_sds.dtype))

        if nk == 1:
            finish(part)
        else:
            @pl.when(k == 0)
            def _():
                acc[...] = part

            @pl.when(k > 0)
            def _():
                acc[...] += part

            @pl.when(k == nk - 1)
            def _():
                finish(acc[...])

    outs = pl.pallas_call(
        body,
        name=name,
        grid=grid,
        out_shape=out_shape,
        in_specs=in_specs,
        out_specs=out_specs,
        scratch_shapes=[pltpu.VMEM(acc_shape, F32)] if nk > 1 else [],
        compiler_params=pltpu.CompilerParams(dimension_semantics=("parallel", "parallel", "arbitrary")),
    )(*operands)
    return outs if raw_out else outs[0]


def _mm(a, b, mode, name, out_dtype=F32, bias=None, res=None, gate=None, raw_out=False,
        tm=512, tn=1024, tk=1024, a_row_off=0):
    if mode == "nn":
        K, N = b.shape
        M = a.shape[0] - a_row_off
    elif mode == "nt":
        N, K = b.shape
        M = a.shape[0] - a_row_off
    else:
        (K, M), N = a.shape, b.shape[1]
    tm, tn, tk = _tile(M, tm), _tile(N, tn), _tile(K, tk)
    off = a_row_off // tm
    dims = {"nn": NN, "nt": NT, "tn": TN}[mode]
    a_spec = (pl.BlockSpec((tk, tm), lambda i, j, k: (k, i)) if mode == "tn"
              else pl.BlockSpec((tm, tk), lambda i, j, k: (i + off, k)))
    b_spec = (pl.BlockSpec((tn, tk), lambda i, j, k: (j, k)) if mode == "nt"
              else pl.BlockSpec((tk, tn), lambda i, j, k: (k, j)))
    return _mm_call(name, a, b, a_spec, b_spec, jax.ShapeDtypeStruct((M, N), out_dtype),
                    pl.BlockSpec((tm, tn), lambda i, j, k: (i, j)), (M // tm, N // tn, K // tk), dims,
                    (tm, tn), bias, res, gate, raw_out, pl.BlockSpec((1, tn), lambda i, j, k: (0, j)))


def _mm_to_shards(a, b3, mode, name, out_dtype, tm=512):
    M, K = a.shape
    S = b3.shape[0]
    n = b3.shape[2] if mode == "nn" else b3.shape[1]
    tm = _tile(M, tm)
    return _mm_call(name, a, b3, pl.BlockSpec((tm, K), lambda i, j, k: (i, 0)),
                    pl.BlockSpec((1,) + b3.shape[1:], lambda i, j, k: (j, 0, 0)),
                    jax.ShapeDtypeStruct((S, M, n), out_dtype),
                    pl.BlockSpec((1, tm, n), lambda i, j, k: (j, i, 0)), (M // tm, S, 1),
                    NN if mode == "nn" else NT, (tm, n))


def _mm_over_shards(a3, b3, mode, name, out_dtype=F32, res=None, gate=None, raw_out=False, tm=512, tn=1024):
    S, M, kk = a3.shape
    N = b3.shape[2] if mode == "nn" else b3.shape[1]
    tm, tn = _tile(M, tm), _tile(N, tn)
    b_spec = (pl.BlockSpec((1, kk, tn), lambda i, j, k: (k, 0, j)) if mode == "nn"
              else pl.BlockSpec((1, tn, kk), lambda i, j, k: (k, j, 0)))
    return _mm_call(name, a3, b3, pl.BlockSpec((1, tm, kk), lambda i, j, k: (k, i, 0)), b_spec,
                    jax.ShapeDtypeStruct((M, N), out_dtype), pl.BlockSpec((tm, tn), lambda i, j, k: (i, j)),
                    (M // tm, N // tn, S), NN if mode == "nn" else NT, (tm, tn), None, res, gate, raw_out,
                    pl.BlockSpec((1, tn), lambda i, j, k: (0, j)))


def _mm_tn_shard_rows(a3, b, name, out_dtype, tn=1024, tk=512):
    S, T, m = a3.shape
    N = b.shape[1]
    tn, tk = _tile(N, tn), _tile(T, tk)
    return _mm_call(name, a3, b, pl.BlockSpec((1, tk, m), lambda i, j, k: (i, k, 0)),
                    pl.BlockSpec((tk, tn), lambda i, j, k: (k, j)), jax.ShapeDtypeStruct((S, m, N), out_dtype),
                    pl.BlockSpec((1, m, tn), lambda i, j, k: (i, 0, j)), (S, N // tn, T // tk), TN, (m, tn))


def _mm_tn_shard_cols(a, b3, name, out_dtype, tm=1024, tk=512):
    T, M = a.shape
    S, _, n = b3.shape
    tm, tk = _tile(M, tm), _tile(T, tk)
    return _mm_call(name, a, b3, pl.BlockSpec((tk, tm), lambda i, j, k: (k, i)),
                    pl.BlockSpec((1, tk, n), lambda i, j, k: (j, k, 0)), jax.ShapeDtypeStruct((S, M, n), out_dtype),
                    pl.BlockSpec((1, tm, n), lambda i, j, k: (j, i, 0)), (M // tm, S, T // tk), TN, (tm, n))


def _row_spec(tm, width, off=0):
    return pl.BlockSpec((tm, width), lambda i: (i + off, 0))


def _vec_spec(width):
    return pl.BlockSpec((1, width), lambda i: (0, 0))


def _norm_mod_fwd(h, g, sc, sh, name):
    R, Dm = h.shape
    tm = _tile(R, ROW_BLOCK, 8)

    def body(h_ref, g_ref, sc_ref, sh_ref, o_ref):
        hv = h_ref[...]
        r = lax.rsqrt(jnp.mean(hv * hv, axis=-1, keepdims=True) + EPS)
        o_ref[...] = ((hv * r) * g_ref[...] * (1.0 + sc_ref[...]) + sh_ref[...]).astype(BF16)

    return pl.pallas_call(
        body, name=name, grid=(R // tm,),
        out_shape=jax.ShapeDtypeStruct((R, Dm), BF16),
        in_specs=[_row_spec(tm, Dm), _vec_spec(Dm), _vec_spec(Dm), _vec_spec(Dm)],
        out_specs=_row_spec(tm, Dm),
        compiler_params=pltpu.CompilerParams(dimension_semantics=("parallel",)),
    )(h, g, sc, sh)


def _norm_mod_fwd_cat(hc, h, g, csc, csh, sc, sh, name):
    (C, Dm), T = hc.shape, h.shape[0]
    tm = _tile(math.gcd(C, T), ROW_BLOCK, 8)
    off = C // tm

    def body(hc_ref, h_ref, g_ref, csc_ref, csh_ref, sc_ref, sh_ref, o_ref):
        is_ctx = pl.program_id(0) < off
        hv = jnp.where(is_ctx, hc_ref[...], h_ref[...])
        scv = jnp.where(is_ctx, csc_ref[...], sc_ref[...])
        shv = jnp.where(is_ctx, csh_ref[...], sh_ref[...])
        r = lax.rsqrt(jnp.mean(hv * hv, axis=-1, keepdims=True) + EPS)
        o_ref[...] = ((hv * r) * g_ref[...] * (1.0 + scv) + shv).astype(BF16)

    return pl.pallas_call(
        body, name=name, grid=((C + T) // tm,),
        out_shape=jax.ShapeDtypeStruct((C + T, Dm), BF16),
        in_specs=[pl.BlockSpec((tm, Dm), lambda i: (jnp.minimum(i, off - 1), 0)),
                  pl.BlockSpec((tm, Dm), lambda i: (jnp.maximum(i - off, 0), 0))] + [_vec_spec(Dm)] * 5,
        out_specs=_row_spec(tm, Dm),
        compiler_params=pltpu.CompilerParams(dimension_semantics=("parallel",)),
    )(hc, h, g, csc, csh, sc, sh)


def _norm_mod_bwd(h, g, sc, dxm, dres, name, dxm_row_off=0):
    R, Dm = h.shape
    tm = _tile(R, ROW_BLOCK, 8)
    off = dxm_row_off // tm
    has_res = dres is not None

    def body(*refs):
        it = iter(refs)
        h_ref, g_ref, sc_ref, dx_ref = next(it), next(it), next(it), next(it)
        dres_ref = next(it) if has_res else None
        dh_ref, da_ref, dsh_ref = next(it), next(it), next(it)
        i = pl.program_id(0)

        @pl.when(i == 0)
        def _():
            da_ref[...] = jnp.zeros_like(da_ref)
            dsh_ref[...] = jnp.zeros_like(dsh_ref)

        hv = h_ref[...]
        dx = dx_ref[...].astype(F32)
        r = lax.rsqrt(jnp.mean(hv * hv, axis=-1, keepdims=True) + EPS)
        n = hv * r
        da_ref[...] += jnp.sum(dx * n, axis=0, keepdims=True)
        dsh_ref[...] += jnp.sum(dx, axis=0, keepdims=True)
        dn = dx * (g_ref[...] * (1.0 + sc_ref[...]))
        dh = r * (dn - n * jnp.mean(dn * n, axis=-1, keepdims=True))
        if has_res:
            dh = dh + dres_ref[...]
        dh_ref[...] = dh

    operands = [h, g, sc, dxm] + ([dres] if has_res else [])
    in_specs = [_row_spec(tm, Dm), _vec_spec(Dm), _vec_spec(Dm), _row_spec(tm, Dm, off)]
    in_specs += [_row_spec(tm, Dm)] if has_res else []
    return pl.pallas_call(
        body, name=name, grid=(R // tm,),
        out_shape=[jax.ShapeDtypeStruct((R, Dm), F32), jax.ShapeDtypeStruct((1, Dm), F32),
                   jax.ShapeDtypeStruct((1, Dm), F32)],
        in_specs=in_specs,
        out_specs=[_row_spec(tm, Dm), _vec_spec(Dm), _vec_spec(Dm)],
        compiler_params=pltpu.CompilerParams(dimension_semantics=("arbitrary",)),
    )(*operands)


def _gate_bwd(dh, y, gt, name):
    R, Dm = dh.shape
    tm = _tile(R, ROW_BLOCK, 8)

    def body(dh_ref, y_ref, gt_ref, dy_ref, dgt_ref, dsum_ref):
        i = pl.program_id(0)

        @pl.when(i == 0)
        def _():
            dgt_ref[...] = jnp.zeros_like(dgt_ref)
            dsum_ref[...] = jnp.zeros_like(dsum_ref)

        dhv = dh_ref[...]
        dy = dhv * gt_ref[...]
        dgt_ref[...] += jnp.sum(dhv * y_ref[...], axis=0, keepdims=True)
        dsum_ref[...] += jnp.sum(dy, axis=0, keepdims=True)
        dy_ref[...] = dy.astype(BF16)

    return pl.pallas_call(
        body, name=name, grid=(R // tm,),
        out_shape=[jax.ShapeDtypeStruct((R, Dm), BF16), jax.ShapeDtypeStruct((1, Dm), F32),
                   jax.ShapeDtypeStruct((1, Dm), F32)],
        in_specs=[_row_spec(tm, Dm), _row_spec(tm, Dm), _vec_spec(Dm)],
        out_specs=[_row_spec(tm, Dm), _vec_spec(Dm), _vec_spec(Dm)],
        compiler_params=pltpu.CompilerParams(dimension_semantics=("arbitrary",)),
    )(dh, y, gt)


def _ffn_in_swiglu(xf, w3, name, tm=1024):
    T, K = xf.shape
    S, _, n = w3.shape
    half = S // 2
    tm = _tile(T, tm)

    def body(a_ref, wg_ref, wu_ref, gu_ref, act_ref):
        a = a_ref[...]
        g = _dot(a, wg_ref[0], NN)
        u = _dot(a, wu_ref[0], NN)
        gu_ref[0, 0] = g.astype(BF16)
        gu_ref[1, 0] = u.astype(BF16)
        act_ref[0] = (_silu(g) * u).astype(BF16)

    return pl.pallas_call(
        body, name=name, grid=(T // tm, half),
        out_shape=[jax.ShapeDtypeStruct((2, half, T, n), BF16), jax.ShapeDtypeStruct((half, T, n), BF16)],
        in_specs=[pl.BlockSpec((tm, K), lambda i, j: (i, 0)),
                  pl.BlockSpec((1, K, n), lambda i, j: (j, 0, 0)),
                  pl.BlockSpec((1, K, n), lambda i, j: (j + half, 0, 0))],
        out_specs=[pl.BlockSpec((2, 1, tm, n), lambda i, j: (0, j, i, 0)),
                   pl.BlockSpec((1, tm, n), lambda i, j: (j, i, 0))],
        compiler_params=pltpu.CompilerParams(dimension_semantics=("parallel", "parallel")),
    )(xf, w3, w3)


def _ffn_out_dx_swiglu(df, wo, gu, name, tm=1024):
    T, Dm = df.shape
    half, n, _ = wo.shape
    tm = _tile(T, tm)

    def body(df_ref, w_ref, gu_ref, o_ref):
        da = _dot(df_ref[...], w_ref[0], NT)
        g = gu_ref[0, 0].astype(F32)
        u = gu_ref[1, 0].astype(F32)
        s = _sigmoid(g)
        o_ref[0, 0] = (da * u * (s * (1.0 + g * (1.0 - s)))).astype(BF16)
        o_ref[1, 0] = (da * (g * s)).astype(BF16)

    gu_spec = pl.BlockSpec((2, 1, tm, n), lambda i, j: (0, j, i, 0))
    return pl.pallas_call(
        body, name=name, grid=(T // tm, half),
        out_shape=jax.ShapeDtypeStruct(gu.shape, BF16),
        in_specs=[pl.BlockSpec((tm, Dm), lambda i, j: (i, 0)),
                  pl.BlockSpec((1, n, Dm), lambda i, j: (j, 0, 0)), gu_spec],
        out_specs=gu_spec,
        compiler_params=pltpu.CompilerParams(dimension_semantics=("parallel", "parallel")),
    )(df, wo, gu)


def _glu_fwd(ag, name):
    R = ag.shape[0]
    tm = _tile(R, ROW_BLOCK, 8)

    def body(ag_ref, o_ref):
        o_ref[...] = ag_ref[:, :D_MODEL] * _sigmoid(ag_ref[:, D_MODEL:])

    return pl.pallas_call(
        body, name=name, grid=(R // tm,),
        out_shape=jax.ShapeDtypeStruct((R, D_MODEL), F32),
        in_specs=[_row_spec(tm, 2 * D_MODEL)],
        out_specs=_row_spec(tm, D_MODEL),
        compiler_params=pltpu.CompilerParams(dimension_semantics=("parallel",)),
    )(ag)


def _glu_bwd(ag, dhg, name):
    R = ag.shape[0]
    tm = _tile(R, ROW_BLOCK, 8)

    def body(ag_ref, dh_ref, o_ref, s_ref):
        i = pl.program_id(0)

        @pl.when(i == 0)
        def _():
            s_ref[...] = jnp.zeros_like(s_ref)

        a = ag_ref[:, :D_MODEL]
        s = _sigmoid(ag_ref[:, D_MODEL:])
        dh = dh_ref[...]
        da = dh * s
        dg = dh * a * s * (1.0 - s)
        o_ref[:, :D_MODEL] = da.astype(BF16)
        o_ref[:, D_MODEL:] = dg.astype(BF16)
        s_ref[:, :D_MODEL] += jnp.sum(da, axis=0, keepdims=True)
        s_ref[:, D_MODEL:] += jnp.sum(dg, axis=0, keepdims=True)

    return pl.pallas_call(
        body, name=name, grid=(R // tm,),
        out_shape=[jax.ShapeDtypeStruct((R, 2 * D_MODEL), BF16), jax.ShapeDtypeStruct((1, 2 * D_MODEL), F32)],
        in_specs=[_row_spec(tm, 2 * D_MODEL), _row_spec(tm, D_MODEL)],
        out_specs=[_row_spec(tm, 2 * D_MODEL), _vec_spec(2 * D_MODEL)],
        compiler_params=pltpu.CompilerParams(dimension_semantics=("arbitrary",)),
    )(ag, dhg)


def _halo_specs(tm, nblk, width):
    per = tm // CONV_HALO
    prev = pl.BlockSpec((CONV_HALO, width), lambda i: (jnp.maximum(i * per - 1, 0), 0))
    nxt = pl.BlockSpec((CONV_HALO, width), lambda i: (jnp.minimum((i + 1) * per, nblk * per - 1), 0))
    return prev, nxt


def _fill_halo(scr, prev_ref, cur_ref, next_ref, i, nblk, tm):
    scr[0:CONV_HALO, :] = jnp.where(i > 0, prev_ref[...], 0.0)
    scr[CONV_HALO:CONV_HALO + tm, :] = cur_ref[...]
    scr[CONV_HALO + tm:2 * CONV_HALO + tm, :] = jnp.where(i < nblk - 1, next_ref[...], 0.0)


CONV_ROWS = 128
SUBLANES = 8


def _windows(scr, cols, tm):
    reach = (CONV_WIDTH // SUBLANES) * SUBLANES
    for r in range(SUBLANES):
        base = scr[pl.ds(r, tm + reach), cols]
        for a in range(reach // SUBLANES + 1):
            off = SUBLANES * a + r
            if 1 <= off <= CONV_WIDTH:
                yield off, base[SUBLANES * a:SUBLANES * a + tm]


def _conv_fwd(hg, w_dw, b_dw, name):
    R, Dm = hg.shape
    tm = _tile(R, CONV_ROWS, CONV_HALO)
    nblk = R // tm
    prev_spec, next_spec = _halo_specs(tm, nblk, Dm)

    def body(prev_ref, cur_ref, next_ref, w_ref, bdw_ref, hd_ref, scr):
        _fill_halo(scr, prev_ref, cur_ref, next_ref, pl.program_id(0), nblk, tm)
        for cb in range(Dm // LANES):
            cols = slice(cb * LANES, (cb + 1) * LANES)
            acc = jnp.zeros((tm, LANES), F32) + bdw_ref[:, cols]
            for off, win in _windows(scr, cols, tm):
                acc = acc + w_ref[off - 1:off, cols] * win
            hd_ref[:, cols] = acc

    return pl.pallas_call(
        body, name=name, grid=(nblk,),
        out_shape=jax.ShapeDtypeStruct((R, Dm), F32),
        in_specs=[prev_spec, _row_spec(tm, Dm), next_spec,
                  pl.BlockSpec((CONV_WIDTH, Dm), lambda i: (0, 0)), _vec_spec(Dm)],
        out_specs=_row_spec(tm, Dm),
        scratch_shapes=[pltpu.VMEM((tm + 2 * CONV_HALO, Dm), F32)],
        compiler_params=pltpu.CompilerParams(dimension_semantics=("parallel",)),
    )(hg, hg, hg, w_dw, b_dw)


def _ln_silu_fwd(hd, ln_g, ln_b, name):
    R, Dm = hd.shape
    tm = _tile(R, ROW_BLOCK, 8)

    def body(hd_ref, g_ref, b_ref, hs_ref):
        hd = hd_ref[...]
        xc = hd - jnp.mean(hd, axis=-1, keepdims=True)
        rs = lax.rsqrt(jnp.mean(xc * xc, axis=-1, keepdims=True) + EPS)
        hs_ref[...] = _silu(xc * rs * g_ref[...] + b_ref[...]).astype(BF16)

    return pl.pallas_call(
        body, name=name, grid=(R // tm,),
        out_shape=jax.ShapeDtypeStruct((R, Dm), BF16),
        in_specs=[_row_spec(tm, Dm), _vec_spec(Dm), _vec_spec(Dm)],
        out_specs=_row_spec(tm, Dm),
        compiler_params=pltpu.CompilerParams(dimension_semantics=("parallel",)),
    )(hd, ln_g, ln_b)


def _ln_silu_bwd(dhs, hd, ln_g, ln_b, name):
    R, Dm = hd.shape
    tm = _tile(R, ROW_BLOCK, 8)

    def body(dhs_ref, hd_ref, g_ref, b_ref, dhd_ref, dg_ref, db_ref, dsum_ref):
        i = pl.program_id(0)

        @pl.when(i == 0)
        def _():
            dg_ref[...] = jnp.zeros_like(dg_ref)
            db_ref[...] = jnp.zeros_like(db_ref)
            dsum_ref[...] = jnp.zeros_like(dsum_ref)

        hd = hd_ref[...]
        mu = jnp.mean(hd, axis=-1, keepdims=True)
        xc = hd - mu
        rs = lax.rsqrt(jnp.mean(xc * xc, axis=-1, keepdims=True) + EPS)
        z = xc * rs
        hl = z * g_ref[...] + b_ref[...]
        dhl = dhs_ref[...] * _dsilu(hl)
        dg_ref[...] += jnp.sum(dhl * z, axis=0, keepdims=True)
        db_ref[...] += jnp.sum(dhl, axis=0, keepdims=True)
        dz = dhl * g_ref[...]
        dhd = rs * (dz - jnp.mean(dz, axis=-1, keepdims=True) - z * jnp.mean(dz * z, axis=-1, keepdims=True))
        dsum_ref[...] += jnp.sum(dhd, axis=0, keepdims=True)
        dhd_ref[...] = dhd

    return pl.pallas_call(
        body, name=name, grid=(R // tm,),
        out_shape=[jax.ShapeDtypeStruct((R, Dm), F32)] + [jax.ShapeDtypeStruct((1, Dm), F32)] * 3,
        in_specs=[_row_spec(tm, Dm), _row_spec(tm, Dm), _vec_spec(Dm), _vec_spec(Dm)],
        out_specs=[_row_spec(tm, Dm), _vec_spec(Dm), _vec_spec(Dm), _vec_spec(Dm)],
        compiler_params=pltpu.CompilerParams(dimension_semantics=("arbitrary",)),
    )(dhs, hd, ln_g, ln_b)


def _conv_bwd(dhd, hg, w_dw, name):
    R, Dm = hg.shape
    tm = _tile(R, CONV_ROWS, CONV_HALO)
    nblk = R // tm
    prev_spec, next_spec = _halo_specs(tm, nblk, Dm)

    def body(dprev, dcur, dnext, gprev, gcur, gnext, w_ref, dhg_ref, dw_ref, dscr, gscr, dwp):
        i = pl.program_id(0)

        @pl.when(i == 0)
        def _():
            dwp[...] = jnp.zeros_like(dwp)

        _fill_halo(dscr, dprev, dcur, dnext, i, nblk, tm)
        _fill_halo(gscr, gprev, gcur, gnext, i, nblk, tm)
        for cb in range(Dm // LANES):
            cols = slice(cb * LANES, (cb + 1) * LANES)
            acc = jnp.zeros((tm, LANES), F32)
            for off, win in _windows(dscr, cols, tm):
                j = CONV_WIDTH - off
                acc = acc + w_ref[j:j + 1, cols] * win
            dhg_ref[:, cols] = acc
            d_here = dcur[:, cols]
            for off, win in _windows(gscr, cols, tm):
                j = off - 1
                prod = d_here * win
                part = prod[0:SUBLANES]
                for k in range(1, tm // SUBLANES):
                    part = part + prod[k * SUBLANES:(k + 1) * SUBLANES]
                dwp[j * SUBLANES:(j + 1) * SUBLANES, cols] += part

        @pl.when(i == nblk - 1)
        def _():
            for j in range(CONV_WIDTH):
                dw_ref[j:j + 1, :] = jnp.sum(dwp[j * SUBLANES:(j + 1) * SUBLANES, :], axis=0, keepdims=True)

    return pl.pallas_call(
        body, name=name, grid=(nblk,),
        out_shape=[jax.ShapeDtypeStruct((R, Dm), F32), jax.ShapeDtypeStruct((CONV_WIDTH, Dm), F32)],
        in_specs=[prev_spec, _row_spec(tm, Dm), next_spec, prev_spec, _row_spec(tm, Dm), next_spec,
                  pl.BlockSpec((CONV_WIDTH, Dm), lambda i: (0, 0))],
        out_specs=[_row_spec(tm, Dm), pl.BlockSpec((CONV_WIDTH, Dm), lambda i: (0, 0))],
        scratch_shapes=[pltpu.VMEM((tm + 2 * CONV_HALO, Dm), F32)] * 2
        + [pltpu.VMEM((CONV_WIDTH * SUBLANES, Dm), F32)],
        compiler_params=pltpu.CompilerParams(dimension_semantics=("arbitrary",)),
    )(dhd, dhd, dhd, hg, hg, hg, w_dw)


def _swap16(y, lane):
    return jnp.where((lane & 16) == 0, pltpu.roll(y, LANES - 16, 1), pltpu.roll(y, 16, 1))


def _head_mean(v, bd):
    hi, lo = _split_bf16(v)
    return (_dot(hi, bd, NN) + _dot(lo, bd, NN)) * (1.0 / HEAD_DIM)


Q_COLS = (0, ATTN_WIDTH)
K_COLS = (ATTN_WIDTH, ATTN_WIDTH + HEAD_DIM * 2)
V_COLS = (K_COLS[1], K_COLS[1] + HEAD_DIM * 2)
SU_COLS = (V_COLS[1], V_COLS[1] + SG_WIDTH)
SV_COLS = (SU_COLS[1], SU_COLS[1] + SG_WIDTH)


def _mix_prep_fwd(p, ctx_rows, cos, sin, qg, kg, bd, w_sp, b_spt, name):
    TT = p.shape[0]
    off = ctx_rows // CHUNK
    q_scale = HEAD_DIM ** -0.5

    def body(p_ref, cos_ref, sin_ref, qg_ref, kg_ref, bd_ref, w_ref, b_ref,
             q_ref, kp_ref, vp_ref, kt_ref, vt_ref, sg_ref):
        lane = lax.broadcasted_iota(jnp.int32, (CHUNK, LANES), 1)
        low = lane < HEAD_DIM
        cs, sn, bdv = cos_ref[...], sin_ref[...], bd_ref[...]

        def norm_rope(xv, gain):
            r = lax.rsqrt(_head_mean(xv * xv, bdv) + EPS)
            yv = xv * r * gain
            return yv * cs + _swap16(yv, lane) * sn

        def pad_heads(ref, t):
            tr = pltpu.roll(t, HEAD_DIM, 1)
            ref[0, 0] = jnp.where(low, t, 0.0).astype(BF16)
            ref[0, 1] = jnp.where(low, 0.0, tr).astype(BF16)
            ref[1, 0] = jnp.where(low, tr, 0.0).astype(BF16)
            ref[1, 1] = jnp.where(low, 0.0, t).astype(BF16)

        for a in range(ATTN_WIDTH // LANES):
            xv = p_ref[:, a * LANES:(a + 1) * LANES]
            q_ref[:, a * LANES:(a + 1) * LANES] = (norm_rope(xv, qg_ref[...]) * q_scale).astype(BF16)
        kh = norm_rope(p_ref[:, K_COLS[0]:K_COLS[1]], kg_ref[...])
        pad_heads(kp_ref, kh)
        vh = p_ref[:, V_COLS[0]:V_COLS[1]]
        pad_heads(vp_ref, vh)
        for t_ref, t in ((kt_ref, kh.T), (vt_ref, vh.T)):
            t_ref[0] = t[:HEAD_DIM].astype(BF16)
            t_ref[1] = t[HEAD_DIM:].astype(BF16)
        for g in range(N_SG_GROUPS):
            u = _gelu(p_ref[:, SU_COLS[0] + g * LANES:SU_COLS[0] + (g + 1) * LANES])
            vg = _gelu(p_ref[:, SV_COLS[0] + g * LANES:SV_COLS[0] + (g + 1) * LANES])
            xc = vg - jnp.mean(vg, axis=-1, keepdims=True)
            vn = xc * lax.rsqrt(jnp.mean(xc * xc, axis=-1, keepdims=True) + EPS)
            mixed = _dot(w_ref[g].astype(BF16), vn.astype(BF16), NN) + b_ref[:, g:g + 1]
            sg_ref[:, g * LANES:(g + 1) * LANES] = (u * mixed).astype(BF16)

    def row(width):
        return pl.BlockSpec((CHUNK, width), lambda i: (i, 0))

    def whole(shape):
        return pl.BlockSpec(shape, lambda i: (0,) * len(shape))

    pad_spec = pl.BlockSpec((2, 2, CHUNK, LANES), lambda i: (0, 0, i, 0))
    return pl.pallas_call(
        body, name=name, grid=(TT // CHUNK,),
        out_shape=[jax.ShapeDtypeStruct((TT, ATTN_WIDTH), BF16),
                   jax.ShapeDtypeStruct((2, 2, TT, LANES), BF16), jax.ShapeDtypeStruct((2, 2, TT, LANES), BF16),
                   jax.ShapeDtypeStruct((2, HEAD_DIM, TT), BF16), jax.ShapeDtypeStruct((2, HEAD_DIM, TT), BF16),
                   jax.ShapeDtypeStruct((TT - ctx_rows, ATTN_WIDTH + SG_WIDTH), BF16)],
        in_specs=[row(IN_WIDTH), row(LANES), row(LANES), whole((1, LANES)), whole((1, LANES)),
                  whole((LANES, LANES)), whole((N_SG_GROUPS, CHUNK, CHUNK)), whole((CHUNK, N_SG_GROUPS))],
        out_specs=[row(ATTN_WIDTH), pad_spec, pad_spec,
                   pl.BlockSpec((2, HEAD_DIM, CHUNK), lambda i: (0, 0, i)),
                   pl.BlockSpec((2, HEAD_DIM, CHUNK), lambda i: (0, 0, i)),
                   pl.BlockSpec((CHUNK, SG_WIDTH), lambda i: (jnp.maximum(i - off, 0), 1))],
        compiler_params=pltpu.CompilerParams(dimension_semantics=("arbitrary",)),
    )(p, cos, sin, qg, kg, bd, w_sp, b_spt)


def _mix_prep_bwd(p, dq, f, dao, ctx_rows, cos, sin, qg, kg, bd, w_sp, w_spt, b_spt, name):
    TT = p.shape[0]
    off = ctx_rows // CHUNK
    q_scale = HEAD_DIM ** -0.5

    def body(p_ref, dq_ref, f_ref, dsg_ref, cos_ref, sin_ref, qg_ref, kg_ref, bd_ref, w_ref, wt_ref,
             b_ref, dp_ref, dqg_ref, dkg_ref, dw_ref, db_ref):
        i = pl.program_id(0)

        @pl.when(i == 0)
        def _():
            dqg_ref[...] = jnp.zeros_like(dqg_ref)
            dkg_ref[...] = jnp.zeros_like(dkg_ref)
            dw_ref[...] = jnp.zeros_like(dw_ref)
            db_ref[...] = jnp.zeros_like(db_ref)

        latent = (i >= off).astype(F32)
        lane = lax.broadcasted_iota(jnp.int32, (CHUNK, LANES), 1)
        low = lane < HEAD_DIM
        cs, sn, bdv = cos_ref[...], sin_ref[...], bd_ref[...]

        def fold(b0):
            return jnp.where(low, f_ref[0, b0] + pltpu.roll(f_ref[0, b0 + 1], HEAD_DIM, 1),
                             pltpu.roll(f_ref[1, b0], HEAD_DIM, 1) + f_ref[1, b0 + 1])

        def norm_rope_bwd(xv, dout, gain):
            r = lax.rsqrt(_head_mean(xv * xv, bdv) + EPS)
            n = xv * r
            dy = dout * cs + _swap16(dout * sn, lane)
            dn = dy * gain
            dx = r * (dn - n * _head_mean(dn * n, bdv))
            return dx, jnp.sum(dy * n, axis=0, keepdims=True)

        for a in range(ATTN_WIDTH // LANES):
            cols = slice(a * LANES, (a + 1) * LANES)
            dx, dg = norm_rope_bwd(p_ref[:, cols], dq_ref[:, cols] * (latent * q_scale), qg_ref[...])
            dp_ref[:, cols] = dx.astype(BF16)
            dqg_ref[...] += dg
        dx, dg = norm_rope_bwd(p_ref[:, K_COLS[0]:K_COLS[1]], fold(0), kg_ref[...])
        dp_ref[:, K_COLS[0]:K_COLS[1]] = dx.astype(BF16)
        dkg_ref[...] += dg
        dp_ref[:, V_COLS[0]:V_COLS[1]] = fold(2).astype(BF16)
        for g in range(N_SG_GROUPS):
            su = p_ref[:, SU_COLS[0] + g * LANES:SU_COLS[0] + (g + 1) * LANES]
            sv = p_ref[:, SV_COLS[0] + g * LANES:SV_COLS[0] + (g + 1) * LANES]
            u, vg = _gelu(su), _gelu(sv)
            xc = vg - jnp.mean(vg, axis=-1, keepdims=True)
            rs = lax.rsqrt(jnp.mean(xc * xc, axis=-1, keepdims=True) + EPS)
            vn = xc * rs
            vnb = vn.astype(BF16)
            mixed = _dot(w_ref[g].astype(BF16), vnb, NN) + b_ref[:, g:g + 1]
            dsg = dsg_ref[:, g * LANES:(g + 1) * LANES] * latent
            du = dsg * mixed
            dmix = dsg * u
            dmb = dmix.astype(BF16)
            db_ref[:, g:g + 1] += jnp.sum(dmix, axis=-1, keepdims=True)
            dw_ref[g] += _dot(dmb, vnb, NT)
            dvn = _dot(wt_ref[g].astype(BF16), dmb, NN)
            dvg = rs * (dvn - jnp.mean(dvn, axis=-1, keepdims=True)
                        - vn * jnp.mean(dvn * vn, axis=-1, keepdims=True))
            dp_ref[:, SU_COLS[0] + g * LANES:SU_COLS[0] + (g + 1) * LANES] = (du * _dgelu(su)).astype(BF16)
            dp_ref[:, SV_COLS[0] + g * LANES:SV_COLS[0] + (g + 1) * LANES] = (dvg * _dgelu(sv)).astype(BF16)

    def row(width):
        return pl.BlockSpec((CHUNK, width), lambda i: (i, 0))

    def latent_row(width, col_block):
        return pl.BlockSpec((CHUNK, width), lambda i: (jnp.maximum(i - off, 0), col_block))

    def whole(shape):
        return pl.BlockSpec(shape, lambda i: (0,) * len(shape))

    return pl.pallas_call(
        body, name=name, grid=(TT // CHUNK,),
        out_shape=[jax.ShapeDtypeStruct((TT, IN_WIDTH), BF16), jax.ShapeDtypeStruct((1, LANES), F32),
                   jax.ShapeDtypeStruct((1, LANES), F32),
                   jax.ShapeDtypeStruct((N_SG_GROUPS, CHUNK, CHUNK), F32),
                   jax.ShapeDtypeStruct((CHUNK, N_SG_GROUPS), F32)],
        in_specs=[row(IN_WIDTH), latent_row(ATTN_WIDTH, 0),
                  pl.BlockSpec((2, 4, CHUNK, LANES), lambda i: (0, 0, i, 0)),
                  latent_row(SG_WIDTH, 1), row(LANES), row(LANES), whole((1, LANES)), whole((1, LANES)),
                  whole((LANES, LANES)), whole((N_SG_GROUPS, CHUNK, CHUNK)),
                  whole((N_SG_GROUPS, CHUNK, CHUNK)), whole((CHUNK, N_SG_GROUPS))],
        out_specs=[row(IN_WIDTH), whole((1, LANES)), whole((1, LANES)),
                   whole((N_SG_GROUPS, CHUNK, CHUNK)), whole((CHUNK, N_SG_GROUPS))],
        compiler_params=pltpu.CompilerParams(dimension_semantics=("arbitrary",)),
    )(p, dq, f, dao, cos, sin, qg, kg, bd, w_sp, w_spt, b_spt)


def _attn_fwd(q, kpad, vt, ao, ctx_rows, name, tq=256):
    TT = q.shape[0]
    T = TT - ctx_rows
    tq = _tile(T, tq)
    off = ctx_rows // tq
    group = 2 * LANES

    def body(q_ref, k_ref, vt_ref, ao_in, o_ref, lse_ref):
        del ao_in
        vtv = vt_ref[0]
        for a in range(2):
            qa = q_ref[:, a * LANES:(a + 1) * LANES]
            halves = []
            for b in range(2):
                st = _dot(k_ref[0, b], qa, NT)
                m = jnp.max(st, axis=0, keepdims=True)
                e = jnp.exp(st - m)
                l = jnp.sum(e, axis=0, keepdims=True)
                lse_ref[0, 2 * a + b:2 * a + b + 1, :] = m + jnp.log(l)
                halves.append(_dot(vtv, e.astype(BF16), NN) * (1.0 / l))
            o_ref[:, a * LANES:(a + 1) * LANES] = jnp.concatenate(halves, axis=0).T.astype(BF16)

    return pl.pallas_call(
        body, name=name, grid=(2, T // tq),
        out_shape=[jax.ShapeDtypeStruct(ao.shape, BF16), jax.ShapeDtypeStruct((2, 4, T), F32)],
        in_specs=[pl.BlockSpec((tq, group), lambda j, i: (i + off, j)),
                  pl.BlockSpec((1, 2, TT, LANES), lambda j, i: (j, 0, 0, 0)),
                  pl.BlockSpec((1, HEAD_DIM, TT), lambda j, i: (j, 0, 0)),
                  pl.BlockSpec(memory_space=pl.ANY)],
        out_specs=[pl.BlockSpec((tq, group), lambda j, i: (i, j)),
                   pl.BlockSpec((1, 4, tq), lambda j, i: (j, 0, i))],
        input_output_aliases={3: 0},
        compiler_params=pltpu.CompilerParams(dimension_semantics=("parallel", "parallel")),
    )(q, kpad, vt, ao)


def _attn_bwd(q, dao, ao, lse, kpad, vpad, kt, ctx_rows, name, tq=128):
    TT = q.shape[0]
    T = TT - ctx_rows
    tq = _tile(T, tq)
    off = ctx_rows // tq
    group = 2 * LANES

    def body(q_ref, do_ref, o_ref, lse_ref, k_ref, v_ref, kt_ref, dq_ref, f_ref):
        i = pl.program_id(1)

        @pl.when(i == 0)
        def _():
            f_ref[...] = jnp.zeros_like(f_ref)

        ktv = kt_ref[0]
        row = lax.broadcasted_iota(jnp.int32, (SUBLANES, LANES), 0)
        lane = lax.broadcasted_iota(jnp.int32, (SUBLANES, LANES), 1)
        half_ones = (jnp.where(lane < HEAD_DIM, 0, 1) == row).astype(BF16)
        for a in range(2):
            cols = slice(a * LANES, (a + 1) * LANES)
            qa = q_ref[:, cols]
            do32 = do_ref[:, cols]
            doa = do32.astype(BF16)
            hi, lo = _split_bf16(do32 * o_ref[:, cols].astype(F32))
            deltas = _dot(half_ones, hi, NT) + _dot(half_ones, lo, NT)
            halves = []
            for b in range(2):
                h = 2 * a + b
                st = _dot(k_ref[0, b], qa, NT)
                pt = jnp.exp(st - lse_ref[0, h:h + 1, :])
                dpt = _dot(v_ref[0, b], doa, NT)
                dst = (pt * (dpt - deltas[b:b + 1, :])).astype(BF16)
                f_ref[0, b] += _dot(dst, qa, NN)
                f_ref[0, 2 + b] += _dot(pt.astype(BF16), doa, NN)
                halves.append(_dot(ktv, dst, NN))
            dq_ref[:, cols] = jnp.concatenate(halves, axis=0).T

    kv_spec = pl.BlockSpec((1, 2, TT, LANES), lambda j, i: (j, 0, 0, 0))
    out_cols = pl.BlockSpec((tq, group), lambda j, i: (i, j))
    return pl.pallas_call(
        body, name=name, grid=(2, T // tq),
        out_shape=[jax.ShapeDtypeStruct((T, ATTN_WIDTH), F32), jax.ShapeDtypeStruct((2, 4, TT, LANES), F32)],
        in_specs=[pl.BlockSpec((tq, group), lambda j, i: (i + off, j)), out_cols, out_cols,
                  pl.BlockSpec((1, 4, tq), lambda j, i: (j, 0, i)),
                  kv_spec, kv_spec, pl.BlockSpec((1, HEAD_DIM, TT), lambda j, i: (j, 0, 0))],
        out_specs=[out_cols, pl.BlockSpec((1, 4, TT, LANES), lambda j, i: (j, 0, 0, 0))],
        compiler_params=pltpu.CompilerParams(dimension_semantics=("parallel", "arbitrary")),
    )(q, dao, ao, lse, kpad, vpad, kt)


def _final_fwd_bwd(h, g, target, name):
    R, Dm = h.shape
    tm = _tile(R, ROW_BLOCK, 8)

    def body(h_ref, g_ref, t_ref, dh_ref, loss_ref, dg_ref):
        i = pl.program_id(0)

        @pl.when(i == 0)
        def _():
            loss_ref[...] = jnp.zeros_like(loss_ref)
            dg_ref[...] = jnp.zeros_like(dg_ref)

        hv = h_ref[...]
        r = lax.rsqrt(jnp.mean(hv * hv, axis=-1, keepdims=True) + EPS)
        n = hv * r
        diff = n * g_ref[...] - t_ref[...]
        loss_ref[...] += jnp.sum(diff * diff)
        dout = diff * (1.0 / Dm)
        dg_ref[...] += jnp.sum(dout * n, axis=0, keepdims=True)
        dn = dout * g_ref[...]
        dh_ref[...] = r * (dn - n * jnp.mean(dn * n, axis=-1, keepdims=True))

    return pl.pallas_call(
        body, name=name, grid=(R // tm,),
        out_shape=[jax.ShapeDtypeStruct((R, Dm), F32), jax.ShapeDtypeStruct((1, LANES), F32),
                   jax.ShapeDtypeStruct((1, Dm), F32)],
        in_specs=[_row_spec(tm, Dm), _vec_spec(Dm), _row_spec(tm, Dm)],
        out_specs=[_row_spec(tm, Dm), _vec_spec(LANES), _vec_spec(Dm)],
        compiler_params=pltpu.CompilerParams(dimension_semantics=("arbitrary",)),
    )(h, g, target)


MOD_ROWS = 16


def _mod_fwd(c_rows, w_mod, name):
    L, Dm, n = w_mod.shape

    def body(c_ref, w_ref, o_ref):
        o_ref[0] = _dot3(_silu(c_ref[...]), w_ref[0], NN)

    return pl.pallas_call(
        body, name=name, grid=(L,),
        out_shape=jax.ShapeDtypeStruct((L, MOD_ROWS, n), F32),
        in_specs=[pl.BlockSpec((MOD_ROWS, Dm), lambda l: (0, 0)), pl.BlockSpec((1, Dm, n), lambda l: (l, 0, 0))],
        out_specs=pl.BlockSpec((1, MOD_ROWS, n), lambda l: (l, 0, 0)),
        compiler_params=pltpu.CompilerParams(dimension_semantics=("parallel",)),
    )(c_rows, w_mod)


def _mod_bwd(c_rows_t, dmod, w_mod, name):
    L, Dm, n = w_mod.shape

    def body(ct_ref, d_ref, w_ref, gw_ref, ds_ref):
        dm = d_ref[0]
        gw_ref[0] = _dot3(_silu(ct_ref[...]), dm, NN)
        ds_ref[0] = _dot3(dm[:MOD_ROWS], w_ref[0], NT)

    return pl.pallas_call(
        body, name=name, grid=(L,),
        out_shape=[jax.ShapeDtypeStruct((L, Dm, n), F32), jax.ShapeDtypeStruct((L, MOD_ROWS, Dm), F32)],
        in_specs=[pl.BlockSpec((Dm, LANES), lambda l: (0, 0)), pl.BlockSpec((1, LANES, n), lambda l: (l, 0, 0)),
                  pl.BlockSpec((1, Dm, n), lambda l: (l, 0, 0))],
        out_specs=[pl.BlockSpec((1, Dm, n), lambda l: (l, 0, 0)),
                   pl.BlockSpec((1, MOD_ROWS, Dm), lambda l: (l, 0, 0))],
        compiler_params=pltpu.CompilerParams(dimension_semantics=("parallel",)),
    )(c_rows_t, dmod, w_mod)


def _adam_update(w, g, m, v):
    c1 = 1.0 - ADAM_B1 ** ADAM_STEP
    c2 = 1.0 - ADAM_B2 ** ADAM_STEP
    mn = ADAM_B1 * m + (1.0 - ADAM_B1) * g
    vn = ADAM_B2 * v + (1.0 - ADAM_B2) * (g * g)
    return -ADAM_LR * ((mn / c1) / (jnp.sqrt(vn / c2) + ADAM_EPS) + ADAM_WD * w), mn, vn


def _adamw(w, g, m, v, name):
    R, Cw = w.shape
    tm = _tile(R, ROW_BLOCK, 8)

    def body(w_ref, g_ref, m_ref, v_ref, d_ref, mo_ref, vo_ref):
        d_ref[...], mo_ref[...], vo_ref[...] = _adam_update(w_ref[...], g_ref[...], m_ref[...], v_ref[...])

    spec = pl.BlockSpec((tm, Cw), lambda i: (i, 0))
    return pl.pallas_call(
        body, name=name, grid=(R // tm,),
        out_shape=[jax.ShapeDtypeStruct((R, Cw), F32)] * 3,
        in_specs=[spec] * 4, out_specs=[spec] * 3,
        compiler_params=pltpu.CompilerParams(dimension_semantics=("parallel",)),
    )(w, g, m, v)


def _adamw_recv(w, m, v, recvs, name):
    L, R, n = w.shape
    tm = _tile(R, ROW_BLOCK, 8)
    nblk = R // tm
    parts = [r.reshape(N_DEV, R, n) for r in recvs]

    def body(*refs):
        w_ref, m_ref, v_ref = refs[:3]
        part_refs = refs[3:3 + L]
        g_ref, d_ref, mo_ref, vo_ref, gsum = refs[3 + L:]
        l = pl.program_id(0)
        for ll in range(L):
            @pl.when(l == ll)
            def _(ll=ll):
                acc = part_refs[ll][0].astype(F32)
                for s in range(1, N_DEV):
                    acc = acc + part_refs[ll][s].astype(F32)
                gsum[...] = acc
        g = gsum[...]
        g_ref[0] = g
        d_ref[0], mo_ref[0], vo_ref[0] = _adam_update(w_ref[0], g, m_ref[0], v_ref[0])

    def part_spec(ll):
        return pl.BlockSpec((N_DEV, tm, n), lambda l, i: (0, jnp.where(l == ll, i, jnp.where(l < ll, 0, nblk - 1)), 0))

    spec = pl.BlockSpec((1, tm, n), lambda l, i: (l, i, 0))
    return pl.pallas_call(
        body, name=name, grid=(L, nblk),
        out_shape=[jax.ShapeDtypeStruct((L, R, n), F32)] * 4,
        in_specs=[spec] * 3 + [part_spec(ll) for ll in range(L)], out_specs=[spec] * 4,
        scratch_shapes=[pltpu.VMEM((tm, n), F32)],
        compiler_params=pltpu.CompilerParams(dimension_semantics=("parallel", "parallel")),
    )(w, m, v, *parts)


def _pack(parts, row_mult=8):
    flat, offs, pos = [], [], 0
    for t in parts:
        t = t.reshape(-1).astype(F32)
        size = -(-t.shape[0] // LANES) * LANES
        flat.append(jnp.pad(t, (0, size - t.shape[0])))
        offs.append(pos)
        pos += size
    total = -(-pos // (LANES * row_mult)) * (LANES * row_mult)
    if total > pos:
        flat.append(jnp.zeros((total - pos,), F32))
    return jnp.concatenate(flat).reshape(-1, LANES), offs


def _take(buf, off, shape):
    size = math.prod(shape)
    return buf[..., off:off + size].reshape(buf.shape[:-1] + tuple(shape))


def _rope_tables(T, ctx_rows):
    pos = jnp.arange(T)
    row = (pos // GRID_W).astype(F32)
    col = (pos % GRID_W).astype(F32)
    half = HEAD_DIM // 4
    inv = ROPE_THETA ** (-jnp.arange(0, 2 * half, 2, dtype=F32) / (2 * half))
    ang_r, ang_c = row[:, None] * inv[None, :], col[:, None] * inv[None, :]
    cos = jnp.concatenate([jnp.cos(ang_r)] * 2 + [jnp.cos(ang_c)] * 2, axis=1)
    sin = jnp.concatenate([-jnp.sin(ang_r), jnp.sin(ang_r), -jnp.sin(ang_c), jnp.sin(ang_c)], axis=1)
    cos = jnp.concatenate([jnp.ones((ctx_rows, HEAD_DIM), F32), cos], axis=0)
    sin = jnp.concatenate([jnp.zeros((ctx_rows, HEAD_DIM), F32), sin], axis=0)
    return jnp.tile(cos, (1, 2)), jnp.tile(sin, (1, 2))


def kernel(x, c, ctx, c_ctx, w_mod, b_mod, g_mix, g_ffn, w_ffn_in, w_ffn_out, w_in, q_gain, k_gain, w_sp, b_sp, w_out, w_pw1, b_pw1, w_dw, b_dw, ln_g, ln_b, w_pw2, b_pw2, g_final, loss_target, m_c_ctx, m_w_mod, m_b_mod, m_g_mix, m_g_ffn, m_w_ffn_in, m_w_ffn_out, m_w_in, m_q_gain, m_k_gain, m_w_sp, m_b_sp, m_w_out, m_w_pw1, m_b_pw1, m_w_dw, m_b_dw, m_ln_g, m_ln_b, m_w_pw2, m_b_pw2, m_g_final, v_c_ctx, v_w_mod, v_b_mod, v_g_mix, v_g_ffn, v_w_ffn_in, v_w_ffn_out, v_w_in, v_q_gain, v_k_gain, v_w_sp, v_b_sp, v_w_out, v_w_pw1, v_b_pw1, v_w_dw, v_b_dw, v_ln_g, v_ln_b, v_w_pw2, v_b_pw2, v_g_final):
    weights = dict(c_ctx=c_ctx, w_mod=w_mod, b_mod=b_mod, g_mix=g_mix, g_ffn=g_ffn, w_ffn_in=w_ffn_in,
                   w_ffn_out=w_ffn_out, w_in=w_in, q_gain=q_gain, k_gain=k_gain, w_sp=w_sp, b_sp=b_sp,
                   w_out=w_out, w_pw1=w_pw1, b_pw1=b_pw1, w_dw=w_dw, b_dw=b_dw, ln_g=ln_g, ln_b=ln_b,
                   w_pw2=w_pw2, b_pw2=b_pw2, g_final=g_final)
    moments_m = dict(c_ctx=m_c_ctx, w_mod=m_w_mod, b_mod=m_b_mod, g_mix=m_g_mix, g_ffn=m_g_ffn,
                     w_ffn_in=m_w_ffn_in, w_ffn_out=m_w_ffn_out, w_in=m_w_in, q_gain=m_q_gain,
                     k_gain=m_k_gain, w_sp=m_w_sp, b_sp=m_b_sp, w_out=m_w_out, w_pw1=m_w_pw1,
                     b_pw1=m_b_pw1, w_dw=m_w_dw, b_dw=m_b_dw, ln_g=m_ln_g, ln_b=m_ln_b, w_pw2=m_w_pw2,
                     b_pw2=m_b_pw2, g_final=m_g_final)
    moments_v = dict(c_ctx=v_c_ctx, w_mod=v_w_mod, b_mod=v_b_mod, g_mix=v_g_mix, g_ffn=v_g_ffn,
                     w_ffn_in=v_w_ffn_in, w_ffn_out=v_w_ffn_out, w_in=v_w_in, q_gain=v_q_gain,
                     k_gain=v_k_gain, w_sp=v_w_sp, b_sp=v_b_sp, w_out=v_w_out, w_pw1=v_w_pw1,
                     b_pw1=v_b_pw1, w_dw=v_w_dw, b_dw=v_b_dw, ln_g=v_ln_g, ln_b=v_ln_b, w_pw2=v_w_pw2,
                     b_pw2=v_b_pw2, g_final=v_g_final)
    names = list(weights)

    T, C = x.shape[1], ctx.shape[1]
    Dm = D_MODEL
    me = 4 * lax.axis_index("x") + 2 * lax.axis_index("y") + lax.axis_index("c")
    h0 = x[0]
    ctx2 = ctx[0]
    target = loss_target[0]

    small_sharded = (("w_dw", w_dw[0]), ("b_pw1", b_pw1), ("b_dw", b_dw), ("ln_g", ln_g), ("ln_b", ln_b),
                     ("b_pw2", b_pw2))
    buf1, offs1 = _pack([c] + [t for _, t in small_sharded])
    got1 = _all_gather([buf1], "gather_cond", True)[0].reshape(N_DEV, -1)
    c_all = _take(got1, offs1[0], (Dm,))
    full_small = {}
    for (nm, t), off in zip(small_sharded, offs1[1:]):
        seg = _take(got1, off, t.shape)
        full_small[nm] = jnp.moveaxis(seg, 0, -2).reshape(t.shape[:-1] + (N_DEV * t.shape[-1],))
    w_dw_f, b_pw1_f = full_small["w_dw"], full_small["b_pw1"]
    b_dw_f, ln_g_f, ln_b_f, b_pw2_f = (full_small[k] for k in ("b_dw", "ln_g", "ln_b", "b_pw2"))

    c_rows = jnp.concatenate([c_all, c_ctx[None, :], jnp.zeros((MOD_ROWS - N_DEV - 1, Dm), F32)], axis=0)
    mod_part = _mod_fwd(c_rows, w_mod, "mod_fwd")
    n_mod = w_mod.shape[2]
    got2 = _all_gather([mod_part.reshape(-1, LANES)], "gather_mod", True)[0]
    mod_all = got2.reshape(N_DEV, 2, MOD_ROWS, n_mod).transpose(1, 2, 0, 3).reshape(2, MOD_ROWS, N_DEV * n_mod)
    mod_all = mod_all + b_mod[:, None, :]
    my_mod = lax.dynamic_index_in_dim(mod_all, me, axis=1, keepdims=False)
    sh1, sc1, gt1, sh2, sc2, gt2 = ([my_mod[l:l + 1, k * Dm:(k + 1) * Dm] for l in range(2)] for k in range(6))
    csh1 = mod_all[0, N_DEV:N_DEV + 1, 0:Dm]
    csc1 = mod_all[0, N_DEV:N_DEV + 1, Dm:2 * Dm]

    shards = [w_ffn_in[0], w_ffn_in[1], w_ffn_out[0], w_ffn_out[1], w_in[0], w_out[0], w_pw1[0], w_pw2[0]]
    got = _all_gather([t.astype(BF16) for t in shards], "gather_weights", False)
    W_ffi = [got[l].reshape(N_DEV, Dm, FF_SHARD) for l in range(2)]
    W_ffo = [got[2 + l].reshape(N_DEV // 2, FF_SHARD, Dm) for l in range(2)]
    W_in = got[4].reshape(N_DEV, Dm, IN_WIDTH // N_DEV).transpose(1, 0, 2).reshape(Dm, IN_WIDTH)
    W_out = got[5]
    W_pw1 = got[6].reshape(N_DEV, Dm, 2 * Dm // N_DEV).transpose(1, 0, 2).reshape(Dm, 2 * Dm)
    W_pw2 = got[7]

    g_mix_r = [g_mix[l:l + 1] for l in range(2)]
    g_ffn_r = [g_ffn[l:l + 1] for l in range(2)]
    g_fin = g_final[None, :]

    cos, sin = _rope_tables(T, C)
    qg = jnp.tile(q_gain, (1, 2))
    kg = jnp.tile(k_gain, (1, 2))
    lane_head = jnp.arange(LANES) // HEAD_DIM
    bd = (lane_head[:, None] == lane_head[None, :]).astype(BF16)
    w_sp0 = w_sp[0]
    w_spt0 = w_sp0.transpose(0, 2, 1)
    b_spt0 = b_sp[0].T

    XM = _norm_mod_fwd_cat(ctx2, h0, g_mix_r[0], csc1, csh1, sc1[0], sh1[0], "norm_mix0")
    P = _mm(XM, W_in, "nn", "in_proj", tn=896)
    qh, kpad, vpad, kt, vt, ao = _mix_prep_fwd(P, C, cos, sin, qg, kg, bd, w_sp0, b_spt0, "mix_prep")
    ao, lse = _attn_fwd(qh, kpad, vt, ao, C, "attn_fwd")
    h1, y0 = _mm(ao, W_out, "nn", "out_proj", res=h0, gate=gt1[0], raw_out=True)

    def ffn_fwd(h_in, l):
        xf = _norm_mod_fwd(h_in, g_ffn_r[l], sc2[l], sh2[l], f"norm_ffn{l}")
        gu, act = _ffn_in_swiglu(xf, W_ffi[l], f"ffn_in{l}")
        h_out, f = _mm_over_shards(act, W_ffo[l], "nn", f"ffn_out{l}", res=h_in, gate=gt2[l], raw_out=True)
        return h_out, (xf, gu, act, f)

    h2, saved_ffn0 = ffn_fwd(h1, 0)

    xm1 = _norm_mod_fwd(h2, g_mix_r[1], sc1[1], sh1[1], "norm_mix1")
    ag = _mm(xm1, W_pw1, "nn", "pw1", bias=b_pw1_f)
    hg = _glu_fwd(ag, "glu")
    hd = _conv_fwd(hg, w_dw_f, b_dw_f, "conv")
    hs = _ln_silu_fwd(hd, ln_g_f, ln_b_f, "ln_silu")
    h3, y1 = _mm(hs, W_pw2, "nn", "pw2", bias=b_pw2_f, res=h2, gate=gt1[1], raw_out=True)
    h4, saved_ffn1 = ffn_fwd(h3, 1)

    dh4, sq_err, dg_final = _final_fwd_bwd(h4, g_fin, target, "loss_head")
    loss_local = (0.5 / Dm) * sq_err[0, 0:1]

    def ffn_bwd(dh_out, h_in, saved, l):
        xf, gu, act, f = saved
        df, dgt, _ = _gate_bwd(dh_out, f, gt2[l], f"gate_ffn_bwd{l}")
        dw_out = _mm_tn_shard_rows(act, df, f"ffn_out_dw{l}", BF16)
        dgu = _ffn_out_dx_swiglu(df, W_ffo[l], gu, f"ffn_out_dx{l}").reshape(N_DEV, T, FF_SHARD)
        dw_in = _mm_tn_shard_cols(xf, dgu, f"ffn_in_dw{l}", BF16)
        dxf = _mm_over_shards(dgu, W_ffi[l], "nt", f"ffn_in_dx{l}")
        dh_in, da, dsh = _norm_mod_bwd(h_in, g_ffn_r[l], sc2[l], dxf, dh_out, f"norm_ffn_bwd{l}")
        return dh_in, dw_in, dw_out, (dsh, da * g_ffn_r[l], dgt), da * (1.0 + sc2[l])

    dh3, dW_ffi1, dW_ffo1, dmod_ffn1, dg_ffn1 = ffn_bwd(dh4, h3, saved_ffn1, 1)

    dy1, dgt1_1, db_pw2 = _gate_bwd(dh3, y1, gt1[1], "gate_conv_bwd")
    dW_pw2 = _mm(hs, dy1, "tn", "pw2_dw", BF16, tk=512)
    dhs = _mm(dy1, W_pw2, "nt", "pw2_dx")
    dhd, dln_g, dln_b, db_dw = _ln_silu_bwd(dhs, hd, ln_g_f, ln_b_f, "ln_silu_bwd")
    dhg, dw_dw = _conv_bwd(dhd, hg, w_dw_f, "conv_bwd")
    dag, db_pw1 = _glu_bwd(ag, dhg, "glu_bwd")
    dW_pw1 = _mm(xm1, dag, "tn", "pw1_dw", BF16, tk=512)
    dxm1 = _mm(dag, W_pw1, "nt", "pw1_dx")
    dh2, da, dsh = _norm_mod_bwd(h2, g_mix_r[1], sc1[1], dxm1, dh3, "norm_mix1_bwd")
    dmod_mix1 = (dsh, da * g_mix_r[1], dgt1_1)
    dg_mix1 = da * (1.0 + sc1[1])

    dh1, dW_ffi0, dW_ffo0, dmod_ffn0, dg_ffn0 = ffn_bwd(dh2, h1, saved_ffn0, 0)

    dy0, dgt1_0, _ = _gate_bwd(dh1, y0, gt1[0], "gate_mix_bwd")
    dW_out = _mm(ao, dy0, "tn", "out_proj_dw", BF16, tk=512)
    dao = _mm(dy0, W_out, "nt", "out_proj_dx")
    dq, f_acc = _attn_bwd(qh, dao, ao, lse, kpad, vpad, kt, C, "attn_bwd")
    dP, dqg, dkg, dw_sp0, db_spt0 = _mix_prep_bwd(P, dq, f_acc, dao, C, cos, sin, qg, kg, bd, w_sp0, w_spt0,
                                                  b_spt0, "mix_prep_bwd")
    dW_in = _mm(XM, dP, "tn", "in_proj_dw", BF16, tn=896, tk=512)
    dXM = _mm(dP, W_in, "nt", "in_proj_dx", tk=896)
    dh0, da, dsh = _norm_mod_bwd(h0, g_mix_r[0], sc1[0], dXM, dh1, "norm_mix0_bwd", dxm_row_off=C)
    _, dac, dcsh = _norm_mod_bwd(ctx2, g_mix_r[0], csc1, dXM, None, "norm_ctx_bwd")
    dmod_mix0 = (dsh, da * g_mix_r[0], dgt1_0)
    dg_mix0 = da * (1.0 + sc1[0]) + dac * (1.0 + csc1)
    dcmod = jnp.concatenate([dcsh, dac * g_mix_r[0]], axis=1)

    dmod_mine = jnp.stack([jnp.concatenate(dmod_mix0 + dmod_ffn0, axis=1)[0],
                           jnp.concatenate(dmod_mix1 + dmod_ffn1, axis=1)[0]])

    small_grads = [
        ("loss", loss_local), ("g_final", dg_final), ("g_mix", jnp.concatenate([dg_mix0, dg_mix1])),
        ("g_ffn", jnp.concatenate([dg_ffn0, dg_ffn1])),
        ("q_gain", dqg[:, :HEAD_DIM] + dqg[:, HEAD_DIM:]), ("k_gain", dkg[:, :HEAD_DIM] + dkg[:, HEAD_DIM:]),
        ("w_sp", dw_sp0[None]), ("b_sp", db_spt0.T[None]), ("b_pw1", db_pw1), ("w_dw", dw_dw[None]),
        ("b_dw", db_dw), ("ln_g", dln_g), ("ln_b", dln_b), ("b_pw2", db_pw2), ("dcmod", dcmod),
        ("dmod", dmod_mine),
    ]
    buf3, offs3 = _pack([t for _, t in small_grads])
    got3 = _all_gather([buf3], "gather_small_grads", True)[0].reshape(N_DEV, buf3.shape[0], LANES)
    sum3 = _sum_devices(got3, "sum_small_grads").reshape(-1)
    off3 = {nm: off for (nm, _), off in zip(small_grads, offs3)}
    shape3 = {nm: t.shape for nm, t in small_grads}

    def summed(nm):
        return _take(sum3, off3[nm], shape3[nm])

    loss = summed("loss")[0]
    dcmod_sum = summed("dcmod")
    dmod_rows = _take(got3.reshape(N_DEV, -1), off3["dmod"], (2, 6 * Dm)).transpose(1, 0, 2)
    ctx_row = jnp.concatenate([jnp.pad(dcmod_sum, ((0, 0), (0, 4 * Dm))), jnp.zeros((1, 6 * Dm), F32)])
    dmod_all = jnp.concatenate([dmod_rows, ctx_row[:, None, :],
                                jnp.zeros((2, LANES - N_DEV - 1, 6 * Dm), F32)], axis=1)
    grads = {}
    grads["b_mod"] = summed("dmod") + ctx_row
    dmod_shard = lax.dynamic_slice_in_dim(dmod_all, me * n_mod, n_mod, axis=2)
    c_rows_t = jnp.pad(c_rows.T, ((0, 0), (0, LANES - MOD_ROWS)))
    grads["w_mod"], ds_part = _mod_bwd(c_rows_t, dmod_shard, w_mod, "mod_bwd")

    buf4, _ = _pack([ds_part[0, N_DEV]])
    got4 = _all_gather([buf4], "gather_c_ctx_grad", True)[0].reshape(N_DEV, buf4.shape[0], LANES)
    ds_ctx = _sum_devices(got4, "sum_c_ctx_grad").reshape(-1)[:Dm]
    grads["c_ctx"] = ds_ctx * _dsilu(c_ctx)

    for nm in ("g_final", "g_mix", "g_ffn", "q_gain", "k_gain", "w_sp", "b_sp"):
        grads[nm] = summed(nm).reshape(weights[nm].shape)
    for nm in ("b_pw1", "w_dw", "b_dw", "ln_g", "ln_b", "b_pw2"):
        n_loc = weights[nm].shape[-1]
        grads[nm] = lax.dynamic_slice_in_dim(summed(nm), me * n_loc, n_loc, axis=-1).reshape(weights[nm].shape)

    def col_shards(g, n):
        return g.reshape(Dm, N_DEV, n).transpose(1, 0, 2).reshape(N_DEV * Dm, n)

    send = [dW_ffi0.reshape(N_DEV * Dm, FF_SHARD), dW_ffi1.reshape(N_DEV * Dm, FF_SHARD),
            dW_ffo0.reshape(D_FF, Dm), dW_ffo1.reshape(D_FF, Dm),
            col_shards(dW_in, IN_WIDTH // N_DEV), dW_out, col_shards(dW_pw1, 2 * Dm // N_DEV), dW_pw2]
    recv = _shard_exchange(send, "exchange_weight_grads")

    delta, new_m, new_v = {}, {}, {}
    for nm, parts in (("w_ffn_in", recv[0:2]), ("w_ffn_out", recv[2:4]), ("w_in", recv[4:5]),
                      ("w_out", recv[5:6]), ("w_pw1", recv[6:7]), ("w_pw2", recv[7:8])):
        grads[nm], delta[nm], new_m[nm], new_v[nm] = _adamw_recv(
            weights[nm], moments_m[nm], moments_v[nm], parts, f"adamw_{nm}")
    shp = w_mod.shape
    outs = _adamw(w_mod.reshape(-1, shp[-1]), grads["w_mod"].reshape(-1, shp[-1]),
                  m_w_mod.reshape(-1, shp[-1]), v_w_mod.reshape(-1, shp[-1]), "adamw_w_mod")
    delta["w_mod"], new_m["w_mod"], new_v["w_mod"] = (o.reshape(shp) for o in outs)
    small_names = [nm for nm in names if nm not in delta]
    packs = [_pack([src[nm] for nm in small_names]) for src in (weights, grads, moments_m, moments_v)]
    offs_s = packs[0][1]
    outs = _adamw(*[pk[0] for pk in packs], "adamw_small")
    for o, dst in zip(outs, (delta, new_m, new_v)):
        o = o.reshape(-1)
        for nm, off in zip(small_names, offs_s):
            dst[nm] = _take(o, off, weights[nm].shape)

    return (loss, dh0[None], *[grads[n] for n in names], *[delta[n] for n in names],
            *[new_m[n] for n in names], *[new_v[n] for n in names])
```

```python
import math

import jax
import jax.numpy as jnp
from jax import lax
from jax.experimental import pallas as pl
from jax.experimental.pallas import tpu as pltpu

F32 = jnp.float32
BF16 = jnp.bfloat16
MESH = pl.DeviceIdType.MESH

N_DEV = 8
D_MODEL = 1024
EPS = 1e-6
HEAD_DIM = 64
ATTN_WIDTH = 512
KV_WIDTH = 128
SG_WIDTH = 512
N_SG_GROUPS = 4
CHUNK = 128
IN_WIDTH = 1792
D_FF = 2816
FF_SHARD = 2 * D_FF // N_DEV
CONV_WIDTH = 31
CONV_HALO = 16
GRID_W = 64
ROPE_THETA = 10000.0
LANES = 128
SUBLANES = 8
ROW_BLOCK = 256
ADAM_LR, ADAM_B1, ADAM_B2, ADAM_EPS, ADAM_WD, ADAM_STEP = 0.001, 0.9, 0.999, 1e-08, 0.01, 10


def _tile(n, target, mult=LANES):
    best = None
    for t in range(mult, min(n, target) + 1, mult):
        if n % t == 0:
            best = t
    return best if best is not None else n


def _sigmoid(x):
    return 1.0 / (1.0 + jnp.exp(-x))


def _silu(x):
    return x * _sigmoid(x)


def _dsilu(x):
    s = _sigmoid(x)
    return s * (1.0 + x * (1.0 - s))


_GELU_K = math.sqrt(2.0 / math.pi)


def _gelu(x):
    return 0.5 * x * (1.0 + jnp.tanh(_GELU_K * (x + 0.044715 * x * x * x)))


def _dgelu(x):
    t = jnp.tanh(_GELU_K * (x + 0.044715 * x * x * x))
    return 0.5 * (1.0 + t) + 0.5 * x * (1.0 - t * t) * _GELU_K * (1.0 + 3.0 * 0.044715 * x * x)


def _split_bf16(x):
    hi = x.astype(BF16)
    lo = (x - hi.astype(F32)).astype(BF16)
    return hi, lo


def _dot(a, b, dims):
    return lax.dot_general(a, b, (dims, ((), ())), preferred_element_type=F32)


def _dot3(a, b, dims):
    ah, al = _split_bf16(a)
    bh, bl = _split_bf16(b)
    return _dot(ah, bh, dims) + _dot(ah, bl, dims) + _dot(al, bh, dims)


NN = ((1,), (0,))
NT = ((1,), (1,))
TN = ((0,), (0,))


def _all_gather(xs, name, in_vmem):
    n_arr = len(xs)

    def body(*refs):
        x_refs, out_refs = refs[:n_arr], refs[n_arr:2 * n_arr]
        send_sems, recv_sems, local_sems = refs[2 * n_arr:]
        x, y, c = lax.axis_index("x"), lax.axis_index("y"), lax.axis_index("c")
        me, sibling = (x, y, c), (x, y, 1 - c)
        chips = [(1 - x, y), (x, 1 - y), (1 - x, 1 - y)]

        def rows(a, px, py, pc):
            m_per = xs[a].shape[0]
            return out_refs[a].at[pl.ds((4 * px + 2 * py + pc) * m_per, m_per), :]

        def copy(a, k, block, to, src=None):
            return pltpu.make_async_remote_copy(
                src_ref=rows(a, *block) if src is None else src,
                dst_ref=rows(a, *block),
                send_sem=send_sems.at[7 * a + k],
                recv_sem=recv_sems.at[7 * a + k],
                device_id=to,
                device_id_type=MESH,
            )

        mine, first, passed = [], [], []
        for a in range(n_arr):
            mine.append(pltpu.make_async_copy(x_refs[a], rows(a, *me), local_sems.at[a]))
            mine[-1].start()
            first.append(copy(a, 0, me, sibling, src=x_refs[a]))
            first += [copy(a, 1 + j, me, (*chip, c), src=x_refs[a]) for j, chip in enumerate(chips)]
        for cp in first:
            cp.start()
        for a in range(n_arr):
            for j, chip in enumerate(chips):
                copy(a, 1 + j, (*chip, c), me).wait_recv()
                passed.append(copy(a, 4 + j, (*chip, c), sibling))
                passed[-1].start()
        for a in range(n_arr):
            copy(a, 0, sibling, me).wait_recv()
            for j, chip in enumerate(chips):
                copy(a, 4 + j, (*chip, 1 - c), me).wait_recv()
        for cp in first + passed:
            cp.wait_send()
        for cp in mine:
            cp.wait()

    space = pltpu.VMEM if in_vmem else pl.ANY
    return pl.pallas_call(
        body,
        name=name,
        out_shape=[jax.ShapeDtypeStruct((N_DEV * t.shape[0], t.shape[1]), t.dtype) for t in xs],
        in_specs=[pl.BlockSpec(memory_space=space)] * n_arr,
        out_specs=[pl.BlockSpec(memory_space=space)] * n_arr,
        scratch_shapes=[
            pltpu.SemaphoreType.DMA((7 * n_arr,)),
            pltpu.SemaphoreType.DMA((7 * n_arr,)),
            pltpu.SemaphoreType.DMA((n_arr,)),
        ],
    )(*xs)


def _shard_exchange(gs, name):
    n_arr = len(gs)

    def body(*refs):
        g_refs, r_refs = refs[:n_arr], refs[n_arr:2 * n_arr]
        send_sems, recv_sems, local_sems = refs[2 * n_arr:]
        x, y, c = lax.axis_index("x"), lax.axis_index("y"), lax.axis_index("c")
        me = 4 * x + 2 * y + c

        def rows(ref, a, idx):
            m_per = gs[a].shape[0] // N_DEV
            return ref.at[pl.ds(idx * m_per, m_per), :]

        mine, sends, recvs = [], [], []
        for a in range(n_arr):
            mine.append(pltpu.make_async_copy(rows(g_refs[a], a, me), rows(r_refs[a], a, me), local_sems.at[a]))
            mine[-1].start()
            for k in range(1, N_DEV):
                px = 1 - x if (k >> 2) & 1 else x
                py = 1 - y if (k >> 1) & 1 else y
                pc = 1 - c if k & 1 else c
                peer = 4 * px + 2 * py + pc
                sem = 7 * a + k - 1
                sends.append(pltpu.make_async_remote_copy(
                    src_ref=rows(g_refs[a], a, peer), dst_ref=rows(r_refs[a], a, me),
                    send_sem=send_sems.at[sem], recv_sem=recv_sems.at[sem],
                    device_id=(px, py, pc), device_id_type=MESH))
                recvs.append(pltpu.make_async_remote_copy(
                    src_ref=rows(g_refs[a], a, me), dst_ref=rows(r_refs[a], a, peer),
                    send_sem=send_sems.at[sem], recv_sem=recv_sems.at[sem],
                    device_id=(px, py, pc), device_id_type=MESH))
        for cp in sends:
            cp.start()
        for cp in recvs:
            cp.wait_recv()
        for cp in sends:
            cp.wait_send()
        for cp in mine:
            cp.wait()

    return pl.pallas_call(
        body,
        name=name,
        out_shape=[jax.ShapeDtypeStruct(t.shape, t.dtype) for t in gs],
        in_specs=[pl.BlockSpec(memory_space=pl.ANY)] * n_arr,
        out_specs=[pl.BlockSpec(memory_space=pl.ANY)] * n_arr,
        scratch_shapes=[
            pltpu.SemaphoreType.DMA((7 * n_arr,)),
            pltpu.SemaphoreType.DMA((7 * n_arr,)),
            pltpu.SemaphoreType.DMA((n_arr,)),
        ],
    )(*gs)


HBM_SPEC = pl.BlockSpec(memory_space=pltpu.HBM)
SEM_SPEC = pl.BlockSpec(memory_space=pltpu.SEMAPHORE)
DATAFLOW_EFFECT = pltpu.SideEffectType.DATAFLOW_SIDE_EFFECTING


def _peers(x, y, c):
    for k in range(1, N_DEV):
        px = 1 - x if (k >> 2) & 1 else x
        py = 1 - y if (k >> 1) & 1 else y
        pc = 1 - c if k & 1 else c
        yield k - 1, (px, py, pc), 4 * px + 2 * py + pc


def _push_copies(src_refs, land_refs, send_sems, recv_sems, shapes, whole_src):
    x, y, c = lax.axis_index("x"), lax.axis_index("y"), lax.axis_index("c")
    me = 4 * x + 2 * y + c
    for a, (m_per, _) in enumerate(shapes):
        def block(ref, idx, m_per=m_per):
            return ref.at[pl.ds(idx * m_per, m_per), :]

        for k, peer, pidx in _peers(x, y, c):
            src = src_refs[a] if whole_src else block(src_refs[a], pidx)
            sems = dict(send_sem=send_sems.at[7 * a + k], recv_sem=recv_sems.at[7 * a + k],
                        device_id=peer, device_id_type=MESH)
            yield (pltpu.make_async_remote_copy(src_ref=src, dst_ref=block(land_refs[a], me), **sems),
                   pltpu.make_async_remote_copy(src_ref=src, dst_ref=block(land_refs[a], pidx), **sems))


def _push_begin(srcs, whole_src, name):
    n_arr = len(srcs)
    shapes = [(t.shape[0] if whole_src else t.shape[0] // N_DEV, t.shape[1]) for t in srcs]
    lands = [lax.empty((N_DEV * m, n), t.dtype) for (m, n), t in zip(shapes, srcs)]

    def body(*refs):
        src_refs, land_refs = refs[:n_arr], refs[n_arr:2 * n_arr]
        send_sems, recv_sems = refs[2 * n_arr], refs[2 * n_arr + 1]
        token = refs[-1]
        for outgoing, _ in _push_copies(src_refs, land_refs, send_sems, recv_sems, shapes, whole_src):
            outgoing.start()
        token[...] = jnp.zeros_like(token)

    operands = [pltpu.with_memory_space_constraint(t, pltpu.HBM) for t in list(srcs) + lands]
    outs = pl.pallas_call(
        body, name=name,
        out_shape=(pltpu.SemaphoreType.DMA((7 * n_arr,)), pltpu.SemaphoreType.DMA((7 * n_arr,)),
                   *[pltpu.HBM(t.shape, t.dtype) for t in operands],
                   jax.ShapeDtypeStruct((SUBLANES, LANES), F32)),
        in_specs=[HBM_SPEC] * (2 * n_arr),
        out_specs=(SEM_SPEC, SEM_SPEC, *[HBM_SPEC] * (2 * n_arr), pl.BlockSpec(memory_space=pltpu.VMEM)),
        input_output_aliases={i: 2 + i for i in range(2 * n_arr)},
        compiler_params=pltpu.CompilerParams(has_side_effects=DATAFLOW_EFFECT),
    )(*operands)
    return outs[0], outs[1], list(outs[2:2 + n_arr]), list(outs[2 + n_arr:2 + 2 * n_arr]), outs[-1], whole_src


def _push_end(handle, after, name):
    send_sems, recv_sems, srcs, lands, _, whole_src = handle
    n_arr = len(srcs)
    shapes = [(t.shape[0] // N_DEV, t.shape[1]) for t in lands]

    def body(*refs):
        src_refs, land_refs = refs[:n_arr], refs[n_arr:2 * n_arr]
        send_sems_ref, recv_sems_ref = refs[2 * n_arr], refs[2 * n_arr + 1]
        for outgoing, incoming in _push_copies(src_refs, land_refs, send_sems_ref, recv_sems_ref, shapes, whole_src):
            outgoing.wait_send()
            incoming.wait_recv()

    outs = pl.pallas_call(
        body, name=name,
        out_shape=tuple(pltpu.HBM(t.shape, t.dtype) for t in srcs + lands),
        in_specs=[HBM_SPEC] * (2 * n_arr) + [SEM_SPEC, SEM_SPEC, pl.BlockSpec(memory_space=pl.ANY)],
        out_specs=tuple([HBM_SPEC] * (2 * n_arr)),
        input_output_aliases={i: i for i in range(2 * n_arr)},
        compiler_params=pltpu.CompilerParams(has_side_effects=DATAFLOW_EFFECT),
    )(*srcs, *lands, send_sems, recv_sems, after)
    return list(outs[:n_arr]), list(outs[n_arr:])


def _sum_devices(r, name, rows_per_step=ROW_BLOCK):
    _, m, n = r.shape
    tm = _tile(m, rows_per_step, 8)

    def body(r_ref, o_ref):
        acc = r_ref[0].astype(F32)
        for s in range(1, N_DEV):
            acc = acc + r_ref[s].astype(F32)
        o_ref[...] = acc

    return pl.pallas_call(
        body,
        name=name,
        grid=(m // tm,),
        out_shape=jax.ShapeDtypeStruct((m, n), F32),
        in_specs=[pl.BlockSpec((N_DEV, tm, n), lambda i: (0, i, 0))],
        out_specs=pl.BlockSpec((tm, n), lambda i: (i, 0)),
        compiler_params=pltpu.CompilerParams(dimension_semantics=("parallel",)),
    )(r)


def _get(ref):
    return ref[0] if len(ref.shape) == 3 else ref[...]


def _put(ref, val):
    if len(ref.shape) == 3:
        ref[0] = val
    else:
        ref[...] = val


def _mm_call(name, a, b, a_spec, b_spec, out_sds, o_spec, grid, dims, acc_shape, bias=None,
             res=None, gate=None, raw_out=False, vec_spec=None):
    nk = grid[2]
    operands, in_specs = [a, b], [a_spec, b_spec]
    if bias is not None:
        operands.append(bias)
        in_specs.append(vec_spec)
    if res is not None:
        operands += [res, gate]
        in_specs += [o_spec, vec_spec]
    out_shape, out_specs = [out_sds], [o_spec]
    if raw_out:
        out_shape.append(jax.ShapeDtypeStruct(out_sds.shape, F32))
        out_specs.append(o_spec)

    def body(*refs):
        it = iter(refs)
        a_ref, b_ref = next(it), next(it)
        bias_ref = next(it) if bias is not None else None
        res_ref, gate_ref = (next(it), next(it)) if res is not None else (None, None)
        o_ref = next(it)
        raw_ref = next(it) if raw_out else None
        acc = next(it) if nk > 1 else None
        k = pl.program_id(2)
        part = _dot(_get(a_ref).astype(BF16), _get(b_ref).astype(BF16), dims)

        def finish(y):
            if bias_ref is not None:
                y = y + bias_ref[...]
            if raw_ref is not None:
                raw_ref[...] = y
            if res_ref is not None:
                y = res_ref[...] + gate_ref[...] * y
            _put(o_ref, y.astype(out_sds.dtype))

        if nk == 1:
            finish(part)
        else:
            @pl.when(k == 0)
            def _():
                acc[...] = part

            @pl.when(k > 0)
            def _():
                acc[...] += part

            @pl.when(k == nk - 1)
            def _():
                finish(acc[...])

    outs = pl.pallas_call(
        body,
        name=name,
        grid=grid,
        out_shape=out_shape,
        in_specs=in_specs,
        out_specs=out_specs,
        scratch_shapes=[pltpu.VMEM(acc_shape, F32)] if nk > 1 else [],
        compiler_params=pltpu.CompilerParams(dimension_semantics=("parallel", "parallel", "arbitrary")),
    )(*operands)
    return outs if raw_out else outs[0]


def _mm(a, b, mode, name, out_dtype=F32, bias=None, res=None, gate=None, raw_out=False,
        tm=512, tn=1024, tk=1024, a_row_off=0):
    if mode == "nn":
        K, N = b.shape
        M = a.shape[0] - a_row_off
    elif mode == "nt":
        N, K = b.shape
        M = a.shape[0] - a_row_off
    else:
        (K, M), N = a.shape, b.shape[1]
    tm, tn, tk = _tile(M, tm), _tile(N, tn), _tile(K, tk)
    off = a_row_off // tm
    dims = {"nn": NN, "nt": NT, "tn": TN}[mode]
    a_spec = (pl.BlockSpec((tk, tm), lambda i, j, k: (k, i)) if mode == "tn"
              else pl.BlockSpec((tm, tk), lambda i, j, k: (i + off, k)))
    b_spec = (pl.BlockSpec((tn, tk), lambda i, j, k: (j, k)) if mode == "nt"
              else pl.BlockSpec((tk, tn), lambda i, j, k: (k, j)))
    return _mm_call(name, a, b, a_spec, b_spec, jax.ShapeDtypeStruct((M, N), out_dtype),
                    pl.BlockSpec((tm, tn), lambda i, j, k: (i, j)), (M // tm, N // tn, K // tk), dims,
                    (tm, tn), bias, res, gate, raw_out, pl.BlockSpec((1, tn), lambda i, j, k: (0, j)))


def _mm_to_shards(a, b3, mode, name, out_dtype, tm=512):
    M, K = a.shape
    S = b3.shape[0]
    n = b3.shape[2] if mode == "nn" else b3.shape[1]
    tm = _tile(M, tm)
    return _mm_call(name, a, b3, pl.BlockSpec((tm, K), lambda i, j, k: (i, 0)),
                    pl.BlockSpec((1,) + b3.shape[1:], lambda i, j, k: (j, 0, 0)),
                    jax.ShapeDtypeStruct((S, M, n), out_dtype),
                    pl.BlockSpec((1, tm, n), lambda i, j, k: (j, i, 0)), (M // tm, S, 1),
                    NN if mode == "nn" else NT, (tm, n))


def _mm_over_shards(a3, b3, mode, name, out_dtype=F32, res=None, gate=None, raw_out=False, tm=512, tn=1024):
    S, M, kk = a3.shape
    N = b3.shape[2] if mode == "nn" else b3.shape[1]
    tm, tn = _tile(M, tm), _tile(N, tn)
    b_spec = (pl.BlockSpec((1, kk, tn), lambda i, j, k: (k, 0, j)) if mode == "nn"
              else pl.BlockSpec((1, tn, kk), lambda i, j, k: (k, j, 0)))
    return _mm_call(name, a3, b3, pl.BlockSpec((1, tm, kk), lambda i, j, k: (k, i, 0)), b_spec,
                    jax.ShapeDtypeStruct((M, N), out_dtype), pl.BlockSpec((tm, tn), lambda i, j, k: (i, j)),
                    (M // tm, N // tn, S), NN if mode == "nn" else NT, (tm, tn), None, res, gate, raw_out,
                    pl.BlockSpec((1, tn), lambda i, j, k: (0, j)))


def _mm_tn_shard_rows(a3, b, name, out_dtype, tn=1024, tk=512):
    S, T, m = a3.shape
    N = b.shape[1]
    tn, tk = _tile(N, tn), _tile(T, tk)
    return _mm_call(name, a3, b, pl.BlockSpec((1, tk, m), lambda i, j, k: (i, k, 0)),
                    pl.BlockSpec((tk, tn), lambda i, j, k: (k, j)), jax.ShapeDtypeStruct((S, m, N), out_dtype),
                    pl.BlockSpec((1, m, tn), lambda i, j, k: (i, 0, j)), (S, N // tn, T // tk), TN, (m, tn))


def _mm_tn_shard_cols(a, b3, name, out_dtype, tm=1024, tk=512):
    T, M = a.shape
    S, _, n = b3.shape
    tm, tk = _tile(M, tm), _tile(T, tk)
    return _mm_call(name, a, b3, pl.BlockSpec((tk, tm), lambda i, j, k: (k, i)),
                    pl.BlockSpec((1, tk, n), lambda i, j, k: (j, k, 0)), jax.ShapeDtypeStruct((S, M, n), out_dtype),
                    pl.BlockSpec((1, tm, n), lambda i, j, k: (j, i, 0)), (M // tm, S, T // tk), TN, (tm, n))


def _row_spec(tm, width, off=0):
    return pl.BlockSpec((tm, width), lambda i: (i + off, 0))


def _vec_spec(width):
    return pl.BlockSpec((1, width), lambda i: (0, 0))


def _norm_mod_fwd(h, g, sc, sh, name):
    R, Dm = h.shape
    tm = _tile(R, ROW_BLOCK, 8)

    def body(h_ref, g_ref, sc_ref, sh_ref, o_ref):
        hv = h_ref[...]
        r = lax.rsqrt(jnp.mean(hv * hv, axis=-1, keepdims=True) + EPS)
        o_ref[...] = ((hv * r) * g_ref[...] * (1.0 + sc_ref[...]) + sh_ref[...]).astype(BF16)

    return pl.pallas_call(
        body, name=name, grid=(R // tm,),
        out_shape=jax.ShapeDtypeStruct((R, Dm), BF16),
        in_specs=[_row_spec(tm, Dm), _vec_spec(Dm), _vec_spec(Dm), _vec_spec(Dm)],
        out_specs=_row_spec(tm, Dm),
        compiler_params=pltpu.CompilerParams(dimension_semantics=("parallel",)),
    )(h, g, sc, sh)


def _norm_mod_fwd_cat(hc, h, g, csc, csh, sc, sh, name):
    (C, Dm), T = hc.shape, h.shape[0]
    tm = _tile(math.gcd(C, T), ROW_BLOCK, 8)
    off = C // tm

    def body(hc_ref, h_ref, g_ref, csc_ref, csh_ref, sc_ref, sh_ref, o_ref):
        is_ctx = pl.program_id(0) < off
        hv = jnp.where(is_ctx, hc_ref[...], h_ref[...])
        scv = jnp.where(is_ctx, csc_ref[...], sc_ref[...])
        shv = jnp.where(is_ctx, csh_ref[...], sh_ref[...])
        r = lax.rsqrt(jnp.mean(hv * hv, axis=-1, keepdims=True) + EPS)
        o_ref[...] = ((hv * r) * g_ref[...] * (1.0 + scv) + shv).astype(BF16)

    return pl.pallas_call(
        body, name=name, grid=((C + T) // tm,),
        out_shape=jax.ShapeDtypeStruct((C + T, Dm), BF16),
        in_specs=[pl.BlockSpec((tm, Dm), lambda i: (jnp.minimum(i, off - 1), 0)),
                  pl.BlockSpec((tm, Dm), lambda i: (jnp.maximum(i - off, 0), 0))] + [_vec_spec(Dm)] * 5,
        out_specs=_row_spec(tm, Dm),
        compiler_params=pltpu.CompilerParams(dimension_semantics=("parallel",)),
    )(hc, h, g, csc, csh, sc, sh)


def _norm_mod_bwd(h, g, sc, dxm, dres, name, dxm_row_off=0):
    R, Dm = h.shape
    tm = _tile(R, ROW_BLOCK, 8)
    off = dxm_row_off // tm
    has_res = dres is not None

    def body(*refs):
        it = iter(refs)
        h_ref, g_ref, sc_ref, dx_ref = next(it), next(it), next(it), next(it)
        dres_ref = next(it) if has_res else None
        dh_ref, da_ref, dsh_ref = next(it), next(it), next(it)
        i = pl.program_id(0)

        @pl.when(i == 0)
        def _():
            da_ref[...] = jnp.zeros_like(da_ref)
            dsh_ref[...] = jnp.zeros_like(dsh_ref)

        hv = h_ref[...]
        dx = dx_ref[...].astype(F32)
        r = lax.rsqrt(jnp.mean(hv * hv, axis=-1, keepdims=True) + EPS)
        n = hv * r
        da_ref[...] += jnp.sum(dx * n, axis=0, keepdims=True)
        dsh_ref[...] += jnp.sum(dx, axis=0, keepdims=True)
        dn = dx * (g_ref[...] * (1.0 + sc_ref[...]))
        dh = r * (dn - n * jnp.mean(dn * n, axis=-1, keepdims=True))
        if has_res:
            dh = dh + dres_ref[...]
        dh_ref[...] = dh

    operands = [h, g, sc, dxm] + ([dres] if has_res else [])
    in_specs = [_row_spec(tm, Dm), _vec_spec(Dm), _vec_spec(Dm), _row_spec(tm, Dm, off)]
    in_specs += [_row_spec(tm, Dm)] if has_res else []
    return pl.pallas_call(
        body, name=name, grid=(R // tm,),
        out_shape=[jax.ShapeDtypeStruct((R, Dm), F32), jax.ShapeDtypeStruct((1, Dm), F32),
                   jax.ShapeDtypeStruct((1, Dm), F32)],
        in_specs=in_specs,
        out_specs=[_row_spec(tm, Dm), _vec_spec(Dm), _vec_spec(Dm)],
        compiler_params=pltpu.CompilerParams(dimension_semantics=("arbitrary",)),
    )(*operands)


def _gate_bwd(dh, y, gt, name):
    R, Dm = dh.shape
    tm = _tile(R, ROW_BLOCK, 8)

    def body(dh_ref, y_ref, gt_ref, dy_ref, dgt_ref, dsum_ref):
        i = pl.program_id(0)

        @pl.when(i == 0)
        def _():
            dgt_ref[...] = jnp.zeros_like(dgt_ref)
            dsum_ref[...] = jnp.zeros_like(dsum_ref)

        dhv = dh_ref[...]
        dy = dhv * gt_ref[...]
        dgt_ref[...] += jnp.sum(dhv * y_ref[...], axis=0, keepdims=True)
        dsum_ref[...] += jnp.sum(dy, axis=0, keepdims=True)
        dy_ref[...] = dy.astype(BF16)

    return pl.pallas_call(
        body, name=name, grid=(R // tm,),
        out_shape=[jax.ShapeDtypeStruct((R, Dm), BF16), jax.ShapeDtypeStruct((1, Dm), F32),
                   jax.ShapeDtypeStruct((1, Dm), F32)],
        in_specs=[_row_spec(tm, Dm), _row_spec(tm, Dm), _vec_spec(Dm)],
        out_specs=[_row_spec(tm, Dm), _vec_spec(Dm), _vec_spec(Dm)],
        compiler_params=pltpu.CompilerParams(dimension_semantics=("arbitrary",)),
    )(dh, y, gt)


def _ffn_in_swiglu(xf, w3, name, tm=1024):
    T, K = xf.shape
    S, _, n = w3.shape
    half = S // 2
    tm = _tile(T, tm)

    def body(a_ref, wg_ref, wu_ref, gu_ref, act_ref):
        a = a_ref[...]
        g = _dot(a, wg_ref[0], NN)
        u = _dot(a, wu_ref[0], NN)
        gu_ref[0, 0] = g.astype(BF16)
        gu_ref[1, 0] = u.astype(BF16)
        act_ref[0] = (_silu(g) * u).astype(BF16)

    return pl.pallas_call(
        body, name=name, grid=(T // tm, half),
        out_shape=[jax.ShapeDtypeStruct((2, half, T, n), BF16), jax.ShapeDtypeStruct((half, T, n), BF16)],
        in_specs=[pl.BlockSpec((tm, K), lambda i, j: (i, 0)),
                  pl.BlockSpec((1, K, n), lambda i, j: (j, 0, 0)),
                  pl.BlockSpec((1, K, n), lambda i, j: (j + half, 0, 0))],
        out_specs=[pl.BlockSpec((2, 1, tm, n), lambda i, j: (0, j, i, 0)),
                   pl.BlockSpec((1, tm, n), lambda i, j: (j, i, 0))],
        compiler_params=pltpu.CompilerParams(dimension_semantics=("parallel", "parallel")),
    )(xf, w3, w3)


def _ffn_out_dx_swiglu(df, wo, gu, name, tm=1024):
    T, Dm = df.shape
    half, n, _ = wo.shape
    tm = _tile(T, tm)

    def body(df_ref, w_ref, gu_ref, o_ref):
        da = _dot(df_ref[...], w_ref[0], NT)
        g = gu_ref[0, 0].astype(F32)
        u = gu_ref[1, 0].astype(F32)
        s = _sigmoid(g)
        o_ref[0, 0] = (da * u * (s * (1.0 + g * (1.0 - s)))).astype(BF16)
        o_ref[1, 0] = (da * (g * s)).astype(BF16)

    gu_spec = pl.BlockSpec((2, 1, tm, n), lambda i, j: (0, j, i, 0))
    return pl.pallas_call(
        body, name=name, grid=(T // tm, half),
        out_shape=jax.ShapeDtypeStruct(gu.shape, BF16),
        in_specs=[pl.BlockSpec((tm, Dm), lambda i, j: (i, 0)),
                  pl.BlockSpec((1, n, Dm), lambda i, j: (j, 0, 0)), gu_spec],
        out_specs=gu_spec,
        compiler_params=pltpu.CompilerParams(dimension_semantics=("parallel", "parallel")),
    )(df, wo, gu)


def _glu_fwd(ag, name):
    R = ag.shape[0]
    tm = _tile(R, ROW_BLOCK, 8)

    def body(ag_ref, o_ref):
        o_ref[...] = ag_ref[:, :D_MODEL] * _sigmoid(ag_ref[:, D_MODEL:])

    return pl.pallas_call(
        body, name=name, grid=(R // tm,),
        out_shape=jax.ShapeDtypeStruct((R, D_MODEL), F32),
        in_specs=[_row_spec(tm, 2 * D_MODEL)],
        out_specs=_row_spec(tm, D_MODEL),
        compiler_params=pltpu.CompilerParams(dimension_semantics=("parallel",)),
    )(ag)


def _glu_bwd(ag, dhg, name):
    R = ag.shape[0]
    tm = _tile(R, ROW_BLOCK, 8)

    def body(ag_ref, dh_ref, o_ref, s_ref):
        i = pl.program_id(0)

        @pl.when(i == 0)
        def _():
            s_ref[...] = jnp.zeros_like(s_ref)

        a = ag_ref[:, :D_MODEL]
        s = _sigmoid(ag_ref[:, D_MODEL:])
        dh = dh_ref[...]
        da = dh * s
        dg = dh * a * s * (1.0 - s)
        o_ref[:, :D_MODEL] = da.astype(BF16)
        o_ref[:, D_MODEL:] = dg.astype(BF16)
        s_ref[:, :D_MODEL] += jnp.sum(da, axis=0, keepdims=True)
        s_ref[:, D_MODEL:] += jnp.sum(dg, axis=0, keepdims=True)

    return pl.pallas_call(
        body, name=name, grid=(R // tm,),
        out_shape=[jax.ShapeDtypeStruct((R, 2 * D_MODEL), BF16), jax.ShapeDtypeStruct((1, 2 * D_MODEL), F32)],
        in_specs=[_row_spec(tm, 2 * D_MODEL), _row_spec(tm, D_MODEL)],
        out_specs=[_row_spec(tm, 2 * D_MODEL), _vec_spec(2 * D_MODEL)],
        compiler_params=pltpu.CompilerParams(dimension_semantics=("arbitrary",)),
    )(ag, dhg)


def _halo_specs(tm, nblk, width):
    per = tm // CONV_HALO
    prev = pl.BlockSpec((CONV_HALO, width), lambda i: (jnp.maximum(i * per - 1, 0), 0))
    nxt = pl.BlockSpec((CONV_HALO, width), lambda i: (jnp.minimum((i + 1) * per, nblk * per - 1), 0))
    return prev, nxt


def _fill_halo(scr, prev_ref, cur_ref, next_ref, i, nblk, tm):
    scr[0:CONV_HALO, :] = jnp.where(i > 0, prev_ref[...], 0.0)
    scr[CONV_HALO:CONV_HALO + tm, :] = cur_ref[...]
    scr[CONV_HALO + tm:2 * CONV_HALO + tm, :] = jnp.where(i < nblk - 1, next_ref[...], 0.0)


CONV_ROWS = 128


def _windows(scr, cols, tm):
    reach = (CONV_WIDTH // SUBLANES) * SUBLANES
    for r in range(SUBLANES):
        base = scr[pl.ds(r, tm + reach), cols]
        for a in range(reach // SUBLANES + 1):
            off = SUBLANES * a + r
            if 1 <= off <= CONV_WIDTH:
                yield off, base[SUBLANES * a:SUBLANES * a + tm]


def _conv_fwd(hg, w_dw, b_dw, name):
    R, Dm = hg.shape
    tm = _tile(R, CONV_ROWS, CONV_HALO)
    nblk = R // tm
    prev_spec, next_spec = _halo_specs(tm, nblk, Dm)

    def body(prev_ref, cur_ref, next_ref, w_ref, bdw_ref, hd_ref, scr):
        _fill_halo(scr, prev_ref, cur_ref, next_ref, pl.program_id(0), nblk, tm)
        for cb in range(Dm // LANES):
            cols = slice(cb * LANES, (cb + 1) * LANES)
            acc = jnp.zeros((tm, LANES), F32) + bdw_ref[:, cols]
            for off, win in _windows(scr, cols, tm):
                acc = acc + w_ref[off - 1:off, cols] * win
            hd_ref[:, cols] = acc

    return pl.pallas_call(
        body, name=name, grid=(nblk,),
        out_shape=jax.ShapeDtypeStruct((R, Dm), F32),
        in_specs=[prev_spec, _row_spec(tm, Dm), next_spec,
                  pl.BlockSpec((CONV_WIDTH, Dm), lambda i: (0, 0)), _vec_spec(Dm)],
        out_specs=_row_spec(tm, Dm),
        scratch_shapes=[pltpu.VMEM((tm + 2 * CONV_HALO, Dm), F32)],
        compiler_params=pltpu.CompilerParams(dimension_semantics=("parallel",)),
    )(hg, hg, hg, w_dw, b_dw)


def _ln_silu_fwd(hd, ln_g, ln_b, name):
    R, Dm = hd.shape
    tm = _tile(R, ROW_BLOCK, 8)

    def body(hd_ref, g_ref, b_ref, hs_ref):
        hd = hd_ref[...]
        xc = hd - jnp.mean(hd, axis=-1, keepdims=True)
        rs = lax.rsqrt(jnp.mean(xc * xc, axis=-1, keepdims=True) + EPS)
        hs_ref[...] = _silu(xc * rs * g_ref[...] + b_ref[...]).astype(BF16)

    return pl.pallas_call(
        body, name=name, grid=(R // tm,),
        out_shape=jax.ShapeDtypeStruct((R, Dm), BF16),
        in_specs=[_row_spec(tm, Dm), _vec_spec(Dm), _vec_spec(Dm)],
        out_specs=_row_spec(tm, Dm),
        compiler_params=pltpu.CompilerParams(dimension_semantics=("parallel",)),
    )(hd, ln_g, ln_b)


def _ln_silu_bwd(dhs, hd, ln_g, ln_b, name):
    R, Dm = hd.shape
    tm = _tile(R, ROW_BLOCK, 8)

    def body(dhs_ref, hd_ref, g_ref, b_ref, dhd_ref, dg_ref, db_ref, dsum_ref):
        i = pl.program_id(0)

        @pl.when(i == 0)
        def _():
            dg_ref[...] = jnp.zeros_like(dg_ref)
            db_ref[...] = jnp.zeros_like(db_ref)
            dsum_ref[...] = jnp.zeros_like(dsum_ref)

        hd = hd_ref[...]
        mu = jnp.mean(hd, axis=-1, keepdims=True)
        xc = hd - mu
        rs = lax.rsqrt(jnp.mean(xc * xc, axis=-1, keepdims=True) + EPS)
        z = xc * rs
        hl = z * g_ref[...] + b_ref[...]
        dhl = dhs_ref[...] * _dsilu(hl)
        dg_ref[...] += jnp.sum(dhl * z, axis=0, keepdims=True)
        db_ref[...] += jnp.sum(dhl, axis=0, keepdims=True)
        dz = dhl * g_ref[...]
        dhd = rs * (dz - jnp.mean(dz, axis=-1, keepdims=True) - z * jnp.mean(dz * z, axis=-1, keepdims=True))
        dsum_ref[...] += jnp.sum(dhd, axis=0, keepdims=True)
        dhd_ref[...] = dhd

    return pl.pallas_call(
        body, name=name, grid=(R // tm,),
        out_shape=[jax.ShapeDtypeStruct((R, Dm), F32)] + [jax.ShapeDtypeStruct((1, Dm), F32)] * 3,
        in_specs=[_row_spec(tm, Dm), _row_spec(tm, Dm), _vec_spec(Dm), _vec_spec(Dm)],
        out_specs=[_row_spec(tm, Dm), _vec_spec(Dm), _vec_spec(Dm), _vec_spec(Dm)],
        compiler_params=pltpu.CompilerParams(dimension_semantics=("arbitrary",)),
    )(dhs, hd, ln_g, ln_b)


def _conv_bwd(dhd, hg, w_dw, name):
    R, Dm = hg.shape
    tm = _tile(R, CONV_ROWS, CONV_HALO)
    nblk = R // tm
    prev_spec, next_spec = _halo_specs(tm, nblk, Dm)

    def body(dprev, dcur, dnext, gprev, gcur, gnext, w_ref, dhg_ref, dw_ref, dscr, gscr, dwp):
        i = pl.program_id(0)

        @pl.when(i == 0)
        def _():
            dwp[...] = jnp.zeros_like(dwp)

        _fill_halo(dscr, dprev, dcur, dnext, i, nblk, tm)
        _fill_halo(gscr, gprev, gcur, gnext, i, nblk, tm)
        for cb in range(Dm // LANES):
            cols = slice(cb * LANES, (cb + 1) * LANES)
            acc = jnp.zeros((tm, LANES), F32)
            for off, win in _windows(dscr, cols, tm):
                j = CONV_WIDTH - off
                acc = acc + w_ref[j:j + 1, cols] * win
            dhg_ref[:, cols] = acc
            d_here = dcur[:, cols]
            for off, win in _windows(gscr, cols, tm):
                j = off - 1
                prod = d_here * win
                part = prod[0:SUBLANES]
                for k in range(1, tm // SUBLANES):
                    part = part + prod[k * SUBLANES:(k + 1) * SUBLANES]
                dwp[j * SUBLANES:(j + 1) * SUBLANES, cols] += part

        @pl.when(i == nblk - 1)
        def _():
            for j in range(CONV_WIDTH):
                dw_ref[j:j + 1, :] = jnp.sum(dwp[j * SUBLANES:(j + 1) * SUBLANES, :], axis=0, keepdims=True)

    return pl.pallas_call(
        body, name=name, grid=(nblk,),
        out_shape=[jax.ShapeDtypeStruct((R, Dm), F32), jax.ShapeDtypeStruct((CONV_WIDTH, Dm), F32)],
        in_specs=[prev_spec, _row_spec(tm, Dm), next_spec, prev_spec, _row_spec(tm, Dm), next_spec,
                  pl.BlockSpec((CONV_WIDTH, Dm), lambda i: (0, 0))],
        out_specs=[_row_spec(tm, Dm), pl.BlockSpec((CONV_WIDTH, Dm), lambda i: (0, 0))],
        scratch_shapes=[pltpu.VMEM((tm + 2 * CONV_HALO, Dm), F32)] * 2
        + [pltpu.VMEM((CONV_WIDTH * SUBLANES, Dm), F32)],
        compiler_params=pltpu.CompilerParams(dimension_semantics=("arbitrary",)),
    )(dhd, dhd, dhd, hg, hg, hg, w_dw)


def _swap16(y, lane):
    return jnp.where((lane & 16) == 0, pltpu.roll(y, LANES - 16, 1), pltpu.roll(y, 16, 1))


def _head_mean(v, bd):
    hi, lo = _split_bf16(v)
    return (_dot(hi, bd, NN) + _dot(lo, bd, NN)) * (1.0 / HEAD_DIM)


Q_COLS = (0, ATTN_WIDTH)
K_COLS = (ATTN_WIDTH, ATTN_WIDTH + HEAD_DIM * 2)
V_COLS = (K_COLS[1], K_COLS[1] + HEAD_DIM * 2)
SU_COLS = (V_COLS[1], V_COLS[1] + SG_WIDTH)
SV_COLS = (SU_COLS[1], SU_COLS[1] + SG_WIDTH)


def _mix_prep_fwd(p, ctx_rows, cos, sin, qg, kg, bd, w_sp, b_spt, name):
    TT = p.shape[0]
    off = ctx_rows // CHUNK
    q_scale = HEAD_DIM ** -0.5

    def body(p_ref, cos_ref, sin_ref, qg_ref, kg_ref, bd_ref, w_ref, b_ref,
             q_ref, kp_ref, vp_ref, kt_ref, vt_ref, sg_ref):
        lane = lax.broadcasted_iota(jnp.int32, (CHUNK, LANES), 1)
        low = lane < HEAD_DIM
        cs, sn, bdv = cos_ref[...], sin_ref[...], bd_ref[...]

        def norm_rope(xv, gain):
            r = lax.rsqrt(_head_mean(xv * xv, bdv) + EPS)
            yv = xv * r * gain
            return yv * cs + _swap16(yv, lane) * sn

        def pad_heads(ref, t):
            tr = pltpu.roll(t, HEAD_DIM, 1)
            ref[0, 0] = jnp.where(low, t, 0.0).astype(BF16)
            ref[0, 1] = jnp.where(low, 0.0, tr).astype(BF16)
            ref[1, 0] = jnp.where(low, tr, 0.0).astype(BF16)
            ref[1, 1] = jnp.where(low, 0.0, t).astype(BF16)

        for a in range(ATTN_WIDTH // LANES):
            xv = p_ref[:, a * LANES:(a + 1) * LANES]
            q_ref[:, a * LANES:(a + 1) * LANES] = (norm_rope(xv, qg_ref[...]) * q_scale).astype(BF16)
        kh = norm_rope(p_ref[:, K_COLS[0]:K_COLS[1]], kg_ref[...])
        pad_heads(kp_ref, kh)
        vh = p_ref[:, V_COLS[0]:V_COLS[1]]
        pad_heads(vp_ref, vh)
        for t_ref, t in ((kt_ref, kh.T), (vt_ref, vh.T)):
            t_ref[0] = t[:HEAD_DIM].astype(BF16)
            t_ref[1] = t[HEAD_DIM:].astype(BF16)
        for g in range(N_SG_GROUPS):
            u = _gelu(p_ref[:, SU_COLS[0] + g * LANES:SU_COLS[0] + (g + 1) * LANES])
            vg = _gelu(p_ref[:, SV_COLS[0] + g * LANES:SV_COLS[0] + (g + 1) * LANES])
            xc = vg - jnp.mean(vg, axis=-1, keepdims=True)
            vn = xc * lax.rsqrt(jnp.mean(xc * xc, axis=-1, keepdims=True) + EPS)
            mixed = _dot(w_ref[g].astype(BF16), vn.astype(BF16), NN) + b_ref[:, g:g + 1]
            sg_ref[:, g * LANES:(g + 1) * LANES] = (u * mixed).astype(BF16)

    def row(width):
        return pl.BlockSpec((CHUNK, width), lambda i: (i, 0))

    def whole(shape):
        return pl.BlockSpec(shape, lambda i: (0,) * len(shape))

    pad_spec = pl.BlockSpec((2, 2, CHUNK, LANES), lambda i: (0, 0, i, 0))
    return pl.pallas_call(
        body, name=name, grid=(TT // CHUNK,),
        out_shape=[jax.ShapeDtypeStruct((TT, ATTN_WIDTH), BF16),
                   jax.ShapeDtypeStruct((2, 2, TT, LANES), BF16), jax.ShapeDtypeStruct((2, 2, TT, LANES), BF16),
                   jax.ShapeDtypeStruct((2, HEAD_DIM, TT), BF16), jax.ShapeDtypeStruct((2, HEAD_DIM, TT), BF16),
                   jax.ShapeDtypeStruct((TT - ctx_rows, ATTN_WIDTH + SG_WIDTH), BF16)],
        in_specs=[row(IN_WIDTH), row(LANES), row(LANES), whole((1, LANES)), whole((1, LANES)),
                  whole((LANES, LANES)), whole((N_SG_GROUPS, CHUNK, CHUNK)), whole((CHUNK, N_SG_GROUPS))],
        out_specs=[row(ATTN_WIDTH), pad_spec, pad_spec,
                   pl.BlockSpec((2, HEAD_DIM, CHUNK), lambda i: (0, 0, i)),
                   pl.BlockSpec((2, HEAD_DIM, CHUNK), lambda i: (0, 0, i)),
                   pl.BlockSpec((CHUNK, SG_WIDTH), lambda i: (jnp.maximum(i - off, 0), 1))],
        compiler_params=pltpu.CompilerParams(dimension_semantics=("arbitrary",)),
    )(p, cos, sin, qg, kg, bd, w_sp, b_spt)


def _mix_prep_bwd(p, dq, f, dao, ctx_rows, cos, sin, qg, kg, bd, w_sp, w_spt, b_spt, name):
    TT = p.shape[0]
    off = ctx_rows // CHUNK
    q_scale = HEAD_DIM ** -0.5

    def body(p_ref, dq_ref, f_ref, dsg_ref, cos_ref, sin_ref, qg_ref, kg_ref, bd_ref, w_ref, wt_ref,
             b_ref, dp_ref, dqg_ref, dkg_ref, dw_ref, db_ref):
        i = pl.program_id(0)

        @pl.when(i == 0)
        def _():
            dqg_ref[...] = jnp.zeros_like(dqg_ref)
            dkg_ref[...] = jnp.zeros_like(dkg_ref)
            dw_ref[...] = jnp.zeros_like(dw_ref)
            db_ref[...] = jnp.zeros_like(db_ref)

        latent = (i >= off).astype(F32)
        lane = lax.broadcasted_iota(jnp.int32, (CHUNK, LANES), 1)
        low = lane < HEAD_DIM
        cs, sn, bdv = cos_ref[...], sin_ref[...], bd_ref[...]

        def fold(b0):
            return jnp.where(low, f_ref[0, b0] + pltpu.roll(f_ref[0, b0 + 1], HEAD_DIM, 1),
                             pltpu.roll(f_ref[1, b0], HEAD_DIM, 1) + f_ref[1, b0 + 1])

        def norm_rope_bwd(xv, dout, gain):
            r = lax.rsqrt(_head_mean(xv * xv, bdv) + EPS)
            n = xv * r
            dy = dout * cs + _swap16(dout * sn, lane)
            dn = dy * gain
            dx = r * (dn - n * _head_mean(dn * n, bdv))
            return dx, jnp.sum(dy * n, axis=0, keepdims=True)

        for a in range(ATTN_WIDTH // LANES):
            cols = slice(a * LANES, (a + 1) * LANES)
            dx, dg = norm_rope_bwd(p_ref[:, cols], dq_ref[:, cols] * (latent * q_scale), qg_ref[...])
            dp_ref[:, cols] = dx.astype(BF16)
            dqg_ref[...] += dg
        dx, dg = norm_rope_bwd(p_ref[:, K_COLS[0]:K_COLS[1]], fold(0), kg_ref[...])
        dp_ref[:, K_COLS[0]:K_COLS[1]] = dx.astype(BF16)
        dkg_ref[...] += dg
        dp_ref[:, V_COLS[0]:V_COLS[1]] = fold(2).astype(BF16)
        for g in range(N_SG_GROUPS):
            su = p_ref[:, SU_COLS[0] + g * LANES:SU_COLS[0] + (g + 1) * LANES]
            sv = p_ref[:, SV_COLS[0] + g * LANES:SV_COLS[0] + (g + 1) * LANES]
            u, vg = _gelu(su), _gelu(sv)
            xc = vg - jnp.mean(vg, axis=-1, keepdims=True)
            rs = lax.rsqrt(jnp.mean(xc * xc, axis=-1, keepdims=True) + EPS)
            vn = xc * rs
            vnb = vn.astype(BF16)
            mixed = _dot(w_ref[g].astype(BF16), vnb, NN) + b_ref[:, g:g + 1]
            dsg = dsg_ref[:, g * LANES:(g + 1) * LANES] * latent
            du = dsg * mixed
            dmix = dsg * u
            dmb = dmix.astype(BF16)
            db_ref[:, g:g + 1] += jnp.sum(dmix, axis=-1, keepdims=True)
            dw_ref[g] += _dot(dmb, vnb, NT)
            dvn = _dot(wt_ref[g].astype(BF16), dmb, NN)
            dvg = rs * (dvn - jnp.mean(dvn, axis=-1, keepdims=True)
                        - vn * jnp.mean(dvn * vn, axis=-1, keepdims=True))
            dp_ref[:, SU_COLS[0] + g * LANES:SU_COLS[0] + (g + 1) * LANES] = (du * _dgelu(su)).astype(BF16)
            dp_ref[:, SV_COLS[0] + g * LANES:SV_COLS[0] + (g + 1) * LANES] = (dvg * _dgelu(sv)).astype(BF16)

    def row(width):
        return pl.BlockSpec((CHUNK, width), lambda i: (i, 0))

    def latent_row(width, col_block):
        return pl.BlockSpec((CHUNK, width), lambda i: (jnp.maximum(i - off, 0), col_block))

    def whole(shape):
        return pl.BlockSpec(shape, lambda i: (0,) * len(shape))

    return pl.pallas_call(
        body, name=name, grid=(TT // CHUNK,),
        out_shape=[jax.ShapeDtypeStruct((TT, IN_WIDTH), BF16), jax.ShapeDtypeStruct((1, LANES), F32),
                   jax.ShapeDtypeStruct((1, LANES), F32),
                   jax.ShapeDtypeStruct((N_SG_GROUPS, CHUNK, CHUNK), F32),
                   jax.ShapeDtypeStruct((CHUNK, N_SG_GROUPS), F32)],
        in_specs=[row(IN_WIDTH), latent_row(ATTN_WIDTH, 0),
                  pl.BlockSpec((2, 4, CHUNK, LANES), lambda i: (0, 0, i, 0)),
                  latent_row(SG_WIDTH, 1), row(LANES), row(LANES), whole((1, LANES)), whole((1, LANES)),
                  whole((LANES, LANES)), whole((N_SG_GROUPS, CHUNK, CHUNK)),
                  whole((N_SG_GROUPS, CHUNK, CHUNK)), whole((CHUNK, N_SG_GROUPS))],
        out_specs=[row(IN_WIDTH), whole((1, LANES)), whole((1, LANES)),
                   whole((N_SG_GROUPS, CHUNK, CHUNK)), whole((CHUNK, N_SG_GROUPS))],
        compiler_params=pltpu.CompilerParams(dimension_semantics=("arbitrary",)),
    )(p, dq, f, dao, cos, sin, qg, kg, bd, w_sp, w_spt, b_spt)


def _col_reduce(t, pair_op, reduce_op, slab=256):
    R = t.shape[0]
    slab = _tile(R, slab, SUBLANES)
    part = t[0:slab]
    for k in range(1, R // slab):
        part = pair_op(part, t[k * slab:(k + 1) * slab])
    return reduce_op(part, axis=0, keepdims=True)


def _attn_fwd(q, kpad, vt, ao, ctx_rows, name, tq=256):
    TT = q.shape[0]
    T = TT - ctx_rows
    tq = _tile(T, tq)
    off = ctx_rows // tq
    group = 2 * LANES

    def body(q_ref, k_ref, vt_ref, ao_in, o_ref, lse_ref):
        del ao_in
        vtv = vt_ref[0]
        for a in range(2):
            qa = q_ref[:, a * LANES:(a + 1) * LANES]
            halves = []
            for b in range(2):
                st = _dot(k_ref[0, b], qa, NT)
                m = _col_reduce(st, jnp.maximum, jnp.max)
                e = jnp.exp(st - m)
                l = _col_reduce(e, jnp.add, jnp.sum)
                lse_ref[0, 2 * a + b:2 * a + b + 1, :] = m + jnp.log(l)
                halves.append(_dot(vtv, e.astype(BF16), NN) * (1.0 / l))
            o_ref[:, a * LANES:(a + 1) * LANES] = jnp.concatenate(halves, axis=0).T.astype(BF16)

    return pl.pallas_call(
        body, name=name, grid=(2, T // tq),
        out_shape=[jax.ShapeDtypeStruct(ao.shape, BF16), jax.ShapeDtypeStruct((2, 4, T), F32)],
        in_specs=[pl.BlockSpec((tq, group), lambda j, i: (i + off, j)),
                  pl.BlockSpec((1, 2, TT, LANES), lambda j, i: (j, 0, 0, 0)),
                  pl.BlockSpec((1, HEAD_DIM, TT), lambda j, i: (j, 0, 0)),
                  pl.BlockSpec(memory_space=pl.ANY)],
        out_specs=[pl.BlockSpec((tq, group), lambda j, i: (i, j)),
                   pl.BlockSpec((1, 4, tq), lambda j, i: (j, 0, i))],
        input_output_aliases={3: 0},
        compiler_params=pltpu.CompilerParams(dimension_semantics=("parallel", "parallel")),
    )(q, kpad, vt, ao)


def _attn_bwd(q, dao, ao, lse, kpad, vpad, kt, ctx_rows, name, tq=128):
    TT = q.shape[0]
    T = TT - ctx_rows
    tq = _tile(T, tq)
    off = ctx_rows // tq
    group = 2 * LANES

    def body(q_ref, do_ref, o_ref, lse_ref, k_ref, v_ref, kt_ref, dq_ref, f_ref):
        i = pl.program_id(1)

        @pl.when(i == 0)
        def _():
            f_ref[...] = jnp.zeros_like(f_ref)

        ktv = kt_ref[0]
        row = lax.broadcasted_iota(jnp.int32, (SUBLANES, LANES), 0)
        lane = lax.broadcasted_iota(jnp.int32, (SUBLANES, LANES), 1)
        half_ones = (jnp.where(lane < HEAD_DIM, 0, 1) == row).astype(BF16)
        for a in range(2):
            cols = slice(a * LANES, (a + 1) * LANES)
            qa = q_ref[:, cols]
            do32 = do_ref[:, cols]
            doa = do32.astype(BF16)
            hi, lo = _split_bf16(do32 * o_ref[:, cols].astype(F32))
            deltas = _dot(half_ones, hi, NT) + _dot(half_ones, lo, NT)
            halves = []
            for b in range(2):
                h = 2 * a + b
                st = _dot(k_ref[0, b], qa, NT)
                pt = jnp.exp(st - lse_ref[0, h:h + 1, :])
                dpt = _dot(v_ref[0, b], doa, NT)
                dst = (pt * (dpt - deltas[b:b + 1, :])).astype(BF16)
                f_ref[0, b] += _dot(dst, qa, NN)
                f_ref[0, 2 + b] += _dot(pt.astype(BF16), doa, NN)
                halves.append(_dot(ktv, dst, NN))
            dq_ref[:, cols] = jnp.concatenate(halves, axis=0).T

    kv_spec = pl.BlockSpec((1, 2, TT, LANES), lambda j, i: (j, 0, 0, 0))
    out_cols = pl.BlockSpec((tq, group), lambda j, i: (i, j))
    return pl.pallas_call(
        body, name=name, grid=(2, T // tq),
        out_shape=[jax.ShapeDtypeStruct((T, ATTN_WIDTH), F32), jax.ShapeDtypeStruct((2, 4, TT, LANES), F32)],
        in_specs=[pl.BlockSpec((tq, group), lambda j, i: (i + off, j)), out_cols, out_cols,
                  pl.BlockSpec((1, 4, tq), lambda j, i: (j, 0, i)),
                  kv_spec, kv_spec, pl.BlockSpec((1, HEAD_DIM, TT), lambda j, i: (j, 0, 0))],
        out_specs=[out_cols, pl.BlockSpec((1, 4, TT, LANES), lambda j, i: (j, 0, 0, 0))],
        compiler_params=pltpu.CompilerParams(dimension_semantics=("parallel", "arbitrary")),
    )(q, dao, ao, lse, kpad, vpad, kt)


def _final_fwd_bwd(h, g, target, name):
    R, Dm = h.shape
    tm = _tile(R, ROW_BLOCK, 8)

    def body(h_ref, g_ref, t_ref, dh_ref, loss_ref, dg_ref):
        i = pl.program_id(0)

        @pl.when(i == 0)
        def _():
            loss_ref[...] = jnp.zeros_like(loss_ref)
            dg_ref[...] = jnp.zeros_like(dg_ref)

        hv = h_ref[...]
        r = lax.rsqrt(jnp.mean(hv * hv, axis=-1, keepdims=True) + EPS)
        n = hv * r
        diff = n * g_ref[...] - t_ref[...]
        loss_ref[...] += jnp.sum(diff * diff)
        dout = diff * (1.0 / Dm)
        dg_ref[...] += jnp.sum(dout * n, axis=0, keepdims=True)
        dn = dout * g_ref[...]
        dh_ref[...] = r * (dn - n * jnp.mean(dn * n, axis=-1, keepdims=True))

    return pl.pallas_call(
        body, name=name, grid=(R // tm,),
        out_shape=[jax.ShapeDtypeStruct((R, Dm), F32), jax.ShapeDtypeStruct((1, LANES), F32),
                   jax.ShapeDtypeStruct((1, Dm), F32)],
        in_specs=[_row_spec(tm, Dm), _vec_spec(Dm), _row_spec(tm, Dm)],
        out_specs=[_row_spec(tm, Dm), _vec_spec(LANES), _vec_spec(Dm)],
        compiler_params=pltpu.CompilerParams(dimension_semantics=("arbitrary",)),
    )(h, g, target)


MOD_ROWS = 16


def _mod_fwd(c_rows, w_mod, name):
    L, Dm, n = w_mod.shape

    def body(c_ref, w_ref, o_ref):
        o_ref[0] = _dot3(_silu(c_ref[...]), w_ref[0], NN)

    return pl.pallas_call(
        body, name=name, grid=(L,),
        out_shape=jax.ShapeDtypeStruct((L, MOD_ROWS, n), F32),
        in_specs=[pl.BlockSpec((MOD_ROWS, Dm), lambda l: (0, 0)), pl.BlockSpec((1, Dm, n), lambda l: (l, 0, 0))],
        out_specs=pl.BlockSpec((1, MOD_ROWS, n), lambda l: (l, 0, 0)),
        compiler_params=pltpu.CompilerParams(dimension_semantics=("parallel",)),
    )(c_rows, w_mod)


def _mod_bwd(c_rows_t, dmod, w_mod, name):
    L, Dm, n = w_mod.shape

    def body(ct_ref, d_ref, w_ref, gw_ref, ds_ref):
        dm = d_ref[0]
        gw_ref[0] = _dot3(_silu(ct_ref[...]), dm, NN)
        ds_ref[0] = _dot3(dm[:MOD_ROWS], w_ref[0], NT)

    return pl.pallas_call(
        body, name=name, grid=(L,),
        out_shape=[jax.ShapeDtypeStruct((L, Dm, n), F32), jax.ShapeDtypeStruct((L, MOD_ROWS, Dm), F32)],
        in_specs=[pl.BlockSpec((Dm, LANES), lambda l: (0, 0)), pl.BlockSpec((1, LANES, n), lambda l: (l, 0, 0)),
                  pl.BlockSpec((1, Dm, n), lambda l: (l, 0, 0))],
        out_specs=[pl.BlockSpec((1, Dm, n), lambda l: (l, 0, 0)),
                   pl.BlockSpec((1, MOD_ROWS, Dm), lambda l: (l, 0, 0))],
        compiler_params=pltpu.CompilerParams(dimension_semantics=("parallel",)),
    )(c_rows_t, dmod, w_mod)


def _adam_update(w, g, m, v):
    c1 = 1.0 - ADAM_B1 ** ADAM_STEP
    c2 = 1.0 - ADAM_B2 ** ADAM_STEP
    mn = ADAM_B1 * m + (1.0 - ADAM_B1) * g
    vn = ADAM_B2 * v + (1.0 - ADAM_B2) * (g * g)
    return -ADAM_LR * ((mn / c1) / (jnp.sqrt(vn / c2) + ADAM_EPS) + ADAM_WD * w), mn, vn


def _adamw(w, g, m, v, name):
    R, Cw = w.shape
    tm = _tile(R, ROW_BLOCK, 8)

    def body(w_ref, g_ref, m_ref, v_ref, d_ref, mo_ref, vo_ref):
        d_ref[...], mo_ref[...], vo_ref[...] = _adam_update(w_ref[...], g_ref[...], m_ref[...], v_ref[...])

    spec = pl.BlockSpec((tm, Cw), lambda i: (i, 0))
    return pl.pallas_call(
        body, name=name, grid=(R // tm,),
        out_shape=[jax.ShapeDtypeStruct((R, Cw), F32)] * 3,
        in_specs=[spec] * 4, out_specs=[spec] * 3,
        compiler_params=pltpu.CompilerParams(dimension_semantics=("parallel",)),
    )(w, g, m, v)


def _adamw_recv(w, m, v, recvs, name):
    L, R, n = w.shape
    tm = _tile(R, ROW_BLOCK, 8)
    nblk = R // tm
    parts = [r.reshape(N_DEV, R, n) for r in recvs]

    def body(*refs):
        w_ref, m_ref, v_ref = refs[:3]
        part_refs = refs[3:3 + L]
        g_ref, d_ref, mo_ref, vo_ref, gsum = refs[3 + L:]
        l = pl.program_id(0)
        for ll in range(L):
            @pl.when(l == ll)
            def _(ll=ll):
                acc = part_refs[ll][0].astype(F32)
                for s in range(1, N_DEV):
                    acc = acc + part_refs[ll][s].astype(F32)
                gsum[...] = acc
        g = gsum[...]
        g_ref[0] = g
        d_ref[0], mo_ref[0], vo_ref[0] = _adam_update(w_ref[0], g, m_ref[0], v_ref[0])

    def part_spec(ll):
        return pl.BlockSpec((N_DEV, tm, n), lambda l, i: (0, jnp.where(l == ll, i, jnp.where(l < ll, 0, nblk - 1)), 0))

    spec = pl.BlockSpec((1, tm, n), lambda l, i: (l, i, 0))
    return pl.pallas_call(
        body, name=name, grid=(L, nblk),
        out_shape=[jax.ShapeDtypeStruct((L, R, n), F32)] * 4,
        in_specs=[spec] * 3 + [part_spec(ll) for ll in range(L)], out_specs=[spec] * 4,
        scratch_shapes=[pltpu.VMEM((tm, n), F32)],
        compiler_params=pltpu.CompilerParams(dimension_semantics=("parallel", "parallel")),
    )(w, m, v, *parts)


def _pack(parts, row_mult=8):
    flat, offs, pos = [], [], 0
    for t in parts:
        t = t.reshape(-1).astype(F32)
        size = -(-t.shape[0] // LANES) * LANES
        flat.append(jnp.pad(t, (0, size - t.shape[0])))
        offs.append(pos)
        pos += size
    total = -(-pos // (LANES * row_mult)) * (LANES * row_mult)
    if total > pos:
        flat.append(jnp.zeros((total - pos,), F32))
    return jnp.concatenate(flat).reshape(-1, LANES), offs


def _take(buf, off, shape):
    size = math.prod(shape)
    return buf[..., off:off + size].reshape(buf.shape[:-1] + tuple(shape))


def _rope_tables(T, ctx_rows):
    pos = jnp.arange(T)
    row = (pos // GRID_W).astype(F32)
    col = (pos % GRID_W).astype(F32)
    half = HEAD_DIM // 4
    inv = ROPE_THETA ** (-jnp.arange(0, 2 * half, 2, dtype=F32) / (2 * half))
    ang_r, ang_c = row[:, None] * inv[None, :], col[:, None] * inv[None, :]
    cos = jnp.concatenate([jnp.cos(ang_r)] * 2 + [jnp.cos(ang_c)] * 2, axis=1)
    sin = jnp.concatenate([-jnp.sin(ang_r), jnp.sin(ang_r), -jnp.sin(ang_c), jnp.sin(ang_c)], axis=1)
    cos = jnp.concatenate([jnp.ones((ctx_rows, HEAD_DIM), F32), cos], axis=0)
    sin = jnp.concatenate([jnp.zeros((ctx_rows, HEAD_DIM), F32), sin], axis=0)
    return jnp.tile(cos, (1, 2)), jnp.tile(sin, (1, 2))


def kernel(x, c, ctx, c_ctx, w_mod, b_mod, g_mix, g_ffn, w_ffn_in, w_ffn_out, w_in, q_gain, k_gain, w_sp, b_sp, w_out, w_pw1, b_pw1, w_dw, b_dw, ln_g, ln_b, w_pw2, b_pw2, g_final, loss_target, m_c_ctx, m_w_mod, m_b_mod, m_g_mix, m_g_ffn, m_w_ffn_in, m_w_ffn_out, m_w_in, m_q_gain, m_k_gain, m_w_sp, m_b_sp, m_w_out, m_w_pw1, m_b_pw1, m_w_dw, m_b_dw, m_ln_g, m_ln_b, m_w_pw2, m_b_pw2, m_g_final, v_c_ctx, v_w_mod, v_b_mod, v_g_mix, v_g_ffn, v_w_ffn_in, v_w_ffn_out, v_w_in, v_q_gain, v_k_gain, v_w_sp, v_b_sp, v_w_out, v_w_pw1, v_b_pw1, v_w_dw, v_b_dw, v_ln_g, v_ln_b, v_w_pw2, v_b_pw2, v_g_final):
    weights = dict(c_ctx=c_ctx, w_mod=w_mod, b_mod=b_mod, g_mix=g_mix, g_ffn=g_ffn, w_ffn_in=w_ffn_in,
                   w_ffn_out=w_ffn_out, w_in=w_in, q_gain=q_gain, k_gain=k_gain, w_sp=w_sp, b_sp=b_sp,
                   w_out=w_out, w_pw1=w_pw1, b_pw1=b_pw1, w_dw=w_dw, b_dw=b_dw, ln_g=ln_g, ln_b=ln_b,
                   w_pw2=w_pw2, b_pw2=b_pw2, g_final=g_final)
    moments_m = dict(c_ctx=m_c_ctx, w_mod=m_w_mod, b_mod=m_b_mod, g_mix=m_g_mix, g_ffn=m_g_ffn,
                     w_ffn_in=m_w_ffn_in, w_ffn_out=m_w_ffn_out, w_in=m_w_in, q_gain=m_q_gain,
                     k_gain=m_k_gain, w_sp=m_w_sp, b_sp=m_b_sp, w_out=m_w_out, w_pw1=m_w_pw1,
                     b_pw1=m_b_pw1, w_dw=m_w_dw, b_dw=m_b_dw, ln_g=m_ln_g, ln_b=m_ln_b, w_pw2=m_w_pw2,
                     b_pw2=m_b_pw2, g_final=m_g_final)
    moments_v = dict(c_ctx=v_c_ctx, w_mod=v_w_mod, b_mod=v_b_mod, g_mix=v_g_mix, g_ffn=v_g_ffn,
                     w_ffn_in=v_w_ffn_in, w_ffn_out=v_w_ffn_out, w_in=v_w_in, q_gain=v_q_gain,
                     k_gain=v_k_gain, w_sp=v_w_sp, b_sp=v_b_sp, w_out=v_w_out, w_pw1=v_w_pw1,
                     b_pw1=v_b_pw1, w_dw=v_w_dw, b_dw=v_b_dw, ln_g=v_ln_g, ln_b=v_ln_b, w_pw2=v_w_pw2,
                     b_pw2=v_b_pw2, g_final=v_g_final)
    names = list(weights)

    T, C = x.shape[1], ctx.shape[1]
    Dm = D_MODEL
    me = 4 * lax.axis_index("x") + 2 * lax.axis_index("y") + lax.axis_index("c")
    h0 = x[0]
    ctx2 = ctx[0]
    target = loss_target[0]

    gather_groups = [[w_in[0], w_out[0]], [w_ffn_in[0], w_ffn_out[0]], [w_pw1[0], w_pw2[0]],
                     [w_ffn_in[1], w_ffn_out[1]]]
    gathers = [_push_begin([t.astype(BF16) for t in grp], True, f"gather_start{k}")
               for k, grp in enumerate(gather_groups)]
    started = sum(h[4][0:1, 0:1] for h in gathers)

    def gathered(k, after):
        own, lands = _push_end(gathers[k], after, f"gather_wait{k}")
        return [lax.dynamic_update_slice(land, mine, (me * mine.shape[0], 0)) for land, mine in zip(lands, own)]

    small_sharded = (("w_dw", w_dw[0]), ("b_pw1", b_pw1), ("b_dw", b_dw), ("ln_g", ln_g), ("ln_b", ln_b),
                     ("b_pw2", b_pw2))
    buf1, offs1 = _pack([c + started] + [t for _, t in small_sharded])
    got1 = _all_gather([buf1], "gather_cond", True)[0].reshape(N_DEV, -1)
    c_all = _take(got1, offs1[0], (Dm,))
    full_small = {}
    for (nm, t), off in zip(small_sharded, offs1[1:]):
        seg = _take(got1, off, t.shape)
        full_small[nm] = jnp.moveaxis(seg, 0, -2).reshape(t.shape[:-1] + (N_DEV * t.shape[-1],))
    w_dw_f, b_pw1_f = full_small["w_dw"], full_small["b_pw1"]
    b_dw_f, ln_g_f, ln_b_f, b_pw2_f = (full_small[k] for k in ("b_dw", "ln_g", "ln_b", "b_pw2"))

    c_rows = jnp.concatenate([c_all, c_ctx[None, :], jnp.zeros((MOD_ROWS - N_DEV - 1, Dm), F32)], axis=0)
    mod_part = _mod_fwd(c_rows, w_mod, "mod_fwd")
    n_mod = w_mod.shape[2]
    got2 = _all_gather([mod_part.reshape(-1, LANES)], "gather_mod", True)[0]
    mod_all = got2.reshape(N_DEV, 2, MOD_ROWS, n_mod).transpose(1, 2, 0, 3).reshape(2, MOD_ROWS, N_DEV * n_mod)
    mod_all = mod_all + b_mod[:, None, :]
    my_mod = lax.dynamic_index_in_dim(mod_all, me, axis=1, keepdims=False)
    sh1, sc1, gt1, sh2, sc2, gt2 = ([my_mod[l:l + 1, k * Dm:(k + 1) * Dm] for l in range(2)] for k in range(6))
    csh1 = mod_all[0, N_DEV:N_DEV + 1, 0:Dm]
    csc1 = mod_all[0, N_DEV:N_DEV + 1, Dm:2 * Dm]

    def ffn_weights(k, after):
        wi, wo = gathered(k, after)
        return wi.reshape(N_DEV, Dm, FF_SHARD), wo.reshape(N_DEV // 2, FF_SHARD, Dm)

    def col_gathered(t, n):
        return t.reshape(N_DEV, Dm, n).transpose(1, 0, 2).reshape(Dm, N_DEV * n)

    W_ffi, W_ffo = [None, None], [None, None]

    g_mix_r = [g_mix[l:l + 1] for l in range(2)]
    g_ffn_r = [g_ffn[l:l + 1] for l in range(2)]
    g_fin = g_final[None, :]

    cos, sin = _rope_tables(T, C)
    qg = jnp.tile(q_gain, (1, 2))
    kg = jnp.tile(k_gain, (1, 2))
    lane_head = jnp.arange(LANES) // HEAD_DIM
    bd = (lane_head[:, None] == lane_head[None, :]).astype(BF16)
    w_sp0 = w_sp[0]
    w_spt0 = w_sp0.transpose(0, 2, 1)
    b_spt0 = b_sp[0].T

    XM = _norm_mod_fwd_cat(ctx2, h0, g_mix_r[0], csc1, csh1, sc1[0], sh1[0], "norm_mix0")
    W_in, W_out = gathered(0, XM)
    W_in = col_gathered(W_in, IN_WIDTH // N_DEV)
    P = _mm(XM, W_in, "nn", "in_proj", tn=896)
    qh, kpad, vpad, kt, vt, ao = _mix_prep_fwd(P, C, cos, sin, qg, kg, bd, w_sp0, b_spt0, "mix_prep")
    ao, lse = _attn_fwd(qh, kpad, vt, ao, C, "attn_fwd")
    h1, y0 = _mm(ao, W_out, "nn", "out_proj", res=h0, gate=gt1[0], raw_out=True)

    def ffn_fwd(h_in, l):
        xf = _norm_mod_fwd(h_in, g_ffn_r[l], sc2[l], sh2[l], f"norm_ffn{l}")
        W_ffi[l], W_ffo[l] = ffn_weights(1 + 2 * l, xf)
        gu, act = _ffn_in_swiglu(xf, W_ffi[l], f"ffn_in{l}")
        h_out, f = _mm_over_shards(act, W_ffo[l], "nn", f"ffn_out{l}", res=h_in, gate=gt2[l], raw_out=True)
        return h_out, (xf, gu, act, f)

    h2, saved_ffn0 = ffn_fwd(h1, 0)

    xm1 = _norm_mod_fwd(h2, g_mix_r[1], sc1[1], sh1[1], "norm_mix1")
    W_pw1, W_pw2 = gathered(2, xm1)
    W_pw1 = col_gathered(W_pw1, 2 * Dm // N_DEV)
    ag = _mm(xm1, W_pw1, "nn", "pw1", bias=b_pw1_f)
    hg = _glu_fwd(ag, "glu")
    hd = _conv_fwd(hg, w_dw_f, b_dw_f, "conv")
    hs = _ln_silu_fwd(hd, ln_g_f, ln_b_f, "ln_silu")
    h3, y1 = _mm(hs, W_pw2, "nn", "pw2", bias=b_pw2_f, res=h2, gate=gt1[1], raw_out=True)
    h4, saved_ffn1 = ffn_fwd(h3, 1)

    dh4, sq_err, dg_final = _final_fwd_bwd(h4, g_fin, target, "loss_head")
    loss_local = (0.5 / Dm) * sq_err[0, 0:1]

    def col_shards(g, n):
        return g.reshape(Dm, N_DEV, n).transpose(1, 0, 2).reshape(N_DEV * Dm, n)

    def exchange_begin(k, parts):
        return _push_begin(parts, False, f"exchange_start{k}")

    def zero_of(handle):
        return handle[4][0:1, 0:1]

    def ffn_bwd(dh_out, h_in, saved, l, zero):
        xf, gu, act, f = saved
        df, dgt, _ = _gate_bwd(dh_out, f, gt2[l] + zero, f"gate_ffn_bwd{l}")
        dw_out = _mm_tn_shard_rows(act, df, f"ffn_out_dw{l}", BF16)
        dgu = _ffn_out_dx_swiglu(df, W_ffo[l], gu, f"ffn_out_dx{l}").reshape(N_DEV, T, FF_SHARD)
        dw_in = _mm_tn_shard_cols(xf, dgu, f"ffn_in_dw{l}", BF16)
        dxf = _mm_over_shards(dgu, W_ffi[l], "nt", f"ffn_in_dx{l}")
        dh_in, da, dsh = _norm_mod_bwd(h_in, g_ffn_r[l], sc2[l], dxf, dh_out, f"norm_ffn_bwd{l}")
        return dh_in, dw_in, dw_out, (dsh, da * g_ffn_r[l], dgt), da * (1.0 + sc2[l])

    dh3, dW_ffi1, dW_ffo1, dmod_ffn1, dg_ffn1 = ffn_bwd(dh4, h3, saved_ffn1, 1, 0.0)
    ex0 = exchange_begin(0, [dW_ffi1.reshape(N_DEV * Dm, FF_SHARD), dW_ffo1.reshape(D_FF, Dm)])

    dy1, dgt1_1, db_pw2 = _gate_bwd(dh3, y1, gt1[1] + zero_of(ex0), "gate_conv_bwd")
    dW_pw2 = _mm(hs, dy1, "tn", "pw2_dw", BF16, tk=512)
    dhs = _mm(dy1, W_pw2, "nt", "pw2_dx")
    dhd, dln_g, dln_b, db_dw = _ln_silu_bwd(dhs, hd, ln_g_f, ln_b_f, "ln_silu_bwd")
    dhg, dw_dw = _conv_bwd(dhd, hg, w_dw_f, "conv_bwd")
    dag, db_pw1 = _glu_bwd(ag, dhg, "glu_bwd")
    dW_pw1 = _mm(xm1, dag, "tn", "pw1_dw", BF16, tk=512)
    dxm1 = _mm(dag, W_pw1, "nt", "pw1_dx")
    dh2, da, dsh = _norm_mod_bwd(h2, g_mix_r[1], sc1[1], dxm1, dh3, "norm_mix1_bwd")
    dmod_mix1 = (dsh, da * g_mix_r[1], dgt1_1)
    dg_mix1 = da * (1.0 + sc1[1])

    ex1 = exchange_begin(1, [col_shards(dW_pw1, 2 * Dm // N_DEV), dW_pw2])
    dh1, dW_ffi0, dW_ffo0, dmod_ffn0, dg_ffn0 = ffn_bwd(dh2, h1, saved_ffn0, 0, zero_of(ex1))
    ex2 = exchange_begin(2, [dW_ffi0.reshape(N_DEV * Dm, FF_SHARD), dW_ffo0.reshape(D_FF, Dm)])

    dy0, dgt1_0, _ = _gate_bwd(dh1, y0, gt1[0] + zero_of(ex2), "gate_mix_bwd")
    dW_out = _mm(ao, dy0, "tn", "out_proj_dw", BF16, tk=512)
    dao = _mm(dy0, W_out, "nt", "out_proj_dx")
    dq, f_acc = _attn_bwd(qh, dao, ao, lse, kpad, vpad, kt, C, "attn_bwd")
    dP, dqg, dkg, dw_sp0, db_spt0 = _mix_prep_bwd(P, dq, f_acc, dao, C, cos, sin, qg, kg, bd, w_sp0, w_spt0,
                                                  b_spt0, "mix_prep_bwd")
    dW_in = _mm(XM, dP, "tn", "in_proj_dw", BF16, tn=896, tk=512)
    ex3 = exchange_begin(3, [col_shards(dW_in, IN_WIDTH // N_DEV), dW_out])
    dXM = _mm(dP, W_in, "nt", "in_proj_dx", tk=896)
    dh0, da, dsh = _norm_mod_bwd(h0, g_mix_r[0], sc1[0] + zero_of(ex3), dXM, dh1, "norm_mix0_bwd", dxm_row_off=C)
    _, dac, dcsh = _norm_mod_bwd(ctx2, g_mix_r[0], csc1, dXM, None, "norm_ctx_bwd")
    dmod_mix0 = (dsh, da * g_mix_r[0], dgt1_0)
    dg_mix0 = da * (1.0 + sc1[0]) + dac * (1.0 + csc1)
    dcmod = jnp.concatenate([dcsh, dac * g_mix_r[0]], axis=1)

    dmod_mine = jnp.stack([jnp.concatenate(dmod_mix0 + dmod_ffn0, axis=1)[0],
                           jnp.concatenate(dmod_mix1 + dmod_ffn1, axis=1)[0]])

    small_grads = [
        ("loss", loss_local), ("g_final", dg_final), ("g_mix", jnp.concatenate([dg_mix0, dg_mix1])),
        ("g_ffn", jnp.concatenate([dg_ffn0, dg_ffn1])),
        ("q_gain", dqg[:, :HEAD_DIM] + dqg[:, HEAD_DIM:]), ("k_gain", dkg[:, :HEAD_DIM] + dkg[:, HEAD_DIM:]),
        ("w_sp", dw_sp0[None]), ("b_sp", db_spt0.T[None]), ("b_pw1", db_pw1), ("w_dw", dw_dw[None]),
        ("b_dw", db_dw), ("ln_g", dln_g), ("ln_b", dln_b), ("b_pw2", db_pw2), ("dcmod", dcmod),
        ("dmod", dmod_mine),
    ]
    buf3, offs3 = _pack([t for _, t in small_grads])
    got3 = _all_gather([buf3], "gather_small_grads", True)[0].reshape(N_DEV, buf3.shape[0], LANES)
    sum3 = _sum_devices(got3, "sum_small_grads").reshape(-1)
    off3 = {nm: off for (nm, _), off in zip(small_grads, offs3)}
    shape3 = {nm: t.shape for nm, t in small_grads}

    def summed(nm):
        return _take(sum3, off3[nm], shape3[nm])

    loss = summed("loss")[0]
    dcmod_sum = summed("dcmod")
    dmod_rows = _take(got3.reshape(N_DEV, -1), off3["dmod"], (2, 6 * Dm)).transpose(1, 0, 2)
    ctx_row = jnp.concatenate([jnp.pad(dcmod_sum, ((0, 0), (0, 4 * Dm))), jnp.zeros((1, 6 * Dm), F32)])
    dmod_all = jnp.concatenate([dmod_rows, ctx_row[:, None, :],
                                jnp.zeros((2, LANES - N_DEV - 1, 6 * Dm), F32)], axis=1)
    grads = {}
    grads["b_mod"] = summed("dmod") + ctx_row
    dmod_shard = lax.dynamic_slice_in_dim(dmod_all, me * n_mod, n_mod, axis=2)
    c_rows_t = jnp.pad(c_rows.T, ((0, 0), (0, LANES - MOD_ROWS)))
    grads["w_mod"], ds_part = _mod_bwd(c_rows_t, dmod_shard, w_mod, "mod_bwd")

    buf4, _ = _pack([ds_part[0, N_DEV]])
    got4 = _all_gather([buf4], "gather_c_ctx_grad", True)[0].reshape(N_DEV, buf4.shape[0], LANES)
    ds_ctx = _sum_devices(got4, "sum_c_ctx_grad").reshape(-1)[:Dm]
    grads["c_ctx"] = ds_ctx * _dsilu(c_ctx)

    for nm in ("g_final", "g_mix", "g_ffn", "q_gain", "k_gain", "w_sp", "b_sp"):
        grads[nm] = summed(nm).reshape(weights[nm].shape)
    for nm in ("b_pw1", "w_dw", "b_dw", "ln_g", "ln_b", "b_pw2"):
        n_loc = weights[nm].shape[-1]
        grads[nm] = lax.dynamic_slice_in_dim(summed(nm), me * n_loc, n_loc, axis=-1).reshape(weights[nm].shape)

    delta, new_m, new_v = {}, {}, {}
    shp = w_mod.shape
    outs = _adamw(w_mod.reshape(-1, shp[-1]), grads["w_mod"].reshape(-1, shp[-1]),
                  m_w_mod.reshape(-1, shp[-1]), v_w_mod.reshape(-1, shp[-1]), "adamw_w_mod")
    delta["w_mod"], new_m["w_mod"], new_v["w_mod"] = (o.reshape(shp) for o in outs)
    big_names = ("w_mod", "w_ffn_in", "w_ffn_out", "w_in", "w_out", "w_pw1", "w_pw2")
    small_names = [nm for nm in names if nm not in big_names]
    packs = [_pack([src[nm] for nm in small_names]) for src in (weights, grads, moments_m, moments_v)]
    offs_s = packs[0][1]
    outs = _adamw(*[pk[0] for pk in packs], "adamw_small")
    for o, dst in zip(outs, (delta, new_m, new_v)):
        o = o.reshape(-1)
        for nm, off in zip(small_names, offs_s):
            dst[nm] = _take(o, off, weights[nm].shape)

    def exchanged(k, handle):
        sent, lands = _push_end(handle, outs[0], f"exchange_wait{k}")
        done = []
        for land, mine in zip(lands, sent):
            m_per = land.shape[0] // N_DEV
            own = lax.dynamic_slice_in_dim(mine, me * m_per, m_per, axis=0)
            done.append(lax.dynamic_update_slice(land, own, (me * m_per, 0)))
        return done

    r_ffi1, r_ffo1 = exchanged(0, ex0)
    r_pw1, r_pw2 = exchanged(1, ex1)
    r_ffi0, r_ffo0 = exchanged(2, ex2)
    r_in, r_out = exchanged(3, ex3)
    for nm, parts in (("w_ffn_in", [r_ffi0, r_ffi1]), ("w_ffn_out", [r_ffo0, r_ffo1]), ("w_pw1", [r_pw1]),
                      ("w_pw2", [r_pw2]), ("w_in", [r_in]), ("w_out", [r_out])):
        grads[nm], delta[nm], new_m[nm], new_v[nm] = _adamw_recv(
            weights[nm], moments_m[nm], moments_v[nm], parts, f"adamw_{nm}")

    return (loss, dh0[None], *[grads[n] for n in names], *[delta[n] for n in names],
            *[new_m[n] for n in names], *[new_v[n] for n in names])
```

```python
import math

import jax
import jax.numpy as jnp
from jax import lax
from jax.experimental import pallas as pl
from jax.experimental.pallas import tpu as pltpu

F32 = jnp.float32
BF16 = jnp.bfloat16
MESH = pl.DeviceIdType.MESH

N_DEV = 8
D_MODEL = 1024
EPS = 1e-6
HEAD_DIM = 64
ATTN_WIDTH = 512
KV_WIDTH = 128
SG_WIDTH = 512
N_SG_GROUPS = 4
CHUNK = 128
IN_WIDTH = 1792
D_FF = 2816
FF_SHARD = 2 * D_FF // N_DEV
CONV_WIDTH = 31
CONV_HALO = 16
GRID_W = 64
ROPE_THETA = 10000.0
LANES = 128
SUBLANES = 8
ROW_BLOCK = 256
ADAM_LR, ADAM_B1, ADAM_B2, ADAM_EPS, ADAM_WD, ADAM_STEP = 0.001, 0.9, 0.999, 1e-08, 0.01, 10


def _tile(n, target, mult=LANES):
    best = None
    for t in range(mult, min(n, target) + 1, mult):
        if n % t == 0:
            best = t
    return best if best is not None else n


def _sigmoid(x):
    return 1.0 / (1.0 + jnp.exp(-x))


def _silu(x):
    return x * _sigmoid(x)


def _dsilu(x):
    s = _sigmoid(x)
    return s * (1.0 + x * (1.0 - s))


_GELU_K = math.sqrt(2.0 / math.pi)


def _gelu(x):
    return 0.5 * x * (1.0 + jnp.tanh(_GELU_K * (x + 0.044715 * x * x * x)))


def _dgelu(x):
    t = jnp.tanh(_GELU_K * (x + 0.044715 * x * x * x))
    return 0.5 * (1.0 + t) + 0.5 * x * (1.0 - t * t) * _GELU_K * (1.0 + 3.0 * 0.044715 * x * x)


def _split_bf16(x):
    hi = x.astype(BF16)
    lo = (x - hi.astype(F32)).astype(BF16)
    return hi, lo


def _dot(a, b, dims):
    return lax.dot_general(a, b, (dims, ((), ())), preferred_element_type=F32)


def _dot3(a, b, dims):
    ah, al = _split_bf16(a)
    bh, bl = _split_bf16(b)
    return _dot(ah, bh, dims) + _dot(ah, bl, dims) + _dot(al, bh, dims)


NN = ((1,), (0,))
NT = ((1,), (1,))
TN = ((0,), (0,))


def _all_gather(xs, name, in_vmem):
    n_arr = len(xs)

    def body(*refs):
        x_refs, out_refs = refs[:n_arr], refs[n_arr:2 * n_arr]
        send_sems, recv_sems, local_sems = refs[2 * n_arr:]
        x, y, c = lax.axis_index("x"), lax.axis_index("y"), lax.axis_index("c")
        me, sibling = (x, y, c), (x, y, 1 - c)
        chips = [(1 - x, y), (x, 1 - y), (1 - x, 1 - y)]

        def rows(a, px, py, pc):
            m_per = xs[a].shape[0]
            return out_refs[a].at[pl.ds((4 * px + 2 * py + pc) * m_per, m_per), :]

        def copy(a, k, block, to, src=None):
            return pltpu.make_async_remote_copy(
                src_ref=rows(a, *block) if src is None else src,
                dst_ref=rows(a, *block),
                send_sem=send_sems.at[7 * a + k],
                recv_sem=recv_sems.at[7 * a + k],
                device_id=to,
                device_id_type=MESH,
            )

        mine, first, passed = [], [], []
        for a in range(n_arr):
            mine.append(pltpu.make_async_copy(x_refs[a], rows(a, *me), local_sems.at[a]))
            mine[-1].start()
            first.append(copy(a, 0, me, sibling, src=x_refs[a]))
            first += [copy(a, 1 + j, me, (*chip, c), src=x_refs[a]) for j, chip in enumerate(chips)]
        for cp in first:
            cp.start()
        for a in range(n_arr):
            for j, chip in enumerate(chips):
                copy(a, 1 + j, (*chip, c), me).wait_recv()
                passed.append(copy(a, 4 + j, (*chip, c), sibling))
                passed[-1].start()
        for a in range(n_arr):
            copy(a, 0, sibling, me).wait_recv()
            for j, chip in enumerate(chips):
                copy(a, 4 + j, (*chip, 1 - c), me).wait_recv()
        for cp in first + passed:
            cp.wait_send()
        for cp in mine:
            cp.wait()

    space = pltpu.VMEM if in_vmem else pl.ANY
    return pl.pallas_call(
        body,
        name=name,
        out_shape=[jax.ShapeDtypeStruct((N_DEV * t.shape[0], t.shape[1]), t.dtype) for t in xs],
        in_specs=[pl.BlockSpec(memory_space=space)] * n_arr,
        out_specs=[pl.BlockSpec(memory_space=space)] * n_arr,
        scratch_shapes=[
            pltpu.SemaphoreType.DMA((7 * n_arr,)),
            pltpu.SemaphoreType.DMA((7 * n_arr,)),
            pltpu.SemaphoreType.DMA((n_arr,)),
        ],
    )(*xs)


def _shard_exchange(gs, name):
    n_arr = len(gs)

    def body(*refs):
        g_refs, r_refs = refs[:n_arr], refs[n_arr:2 * n_arr]
        send_sems, recv_sems, local_sems = refs[2 * n_arr:]
        x, y, c = lax.axis_index("x"), lax.axis_index("y"), lax.axis_index("c")
        me = 4 * x + 2 * y + c

        def rows(ref, a, idx):
            m_per = gs[a].shape[0] // N_DEV
            return ref.at[pl.ds(idx * m_per, m_per), :]

        mine, sends, recvs = [], [], []
        for a in range(n_arr):
            mine.append(pltpu.make_async_copy(rows(g_refs[a], a, me), rows(r_refs[a], a, me), local_sems.at[a]))
            mine[-1].start()
            for k in range(1, N_DEV):
                px = 1 - x if (k >> 2) & 1 else x
                py = 1 - y if (k >> 1) & 1 else y
                pc = 1 - c if k & 1 else c
                peer = 4 * px + 2 * py + pc
                sem = 7 * a + k - 1
                sends.append(pltpu.make_async_remote_copy(
                    src_ref=rows(g_refs[a], a, peer), dst_ref=rows(r_refs[a], a, me),
                    send_sem=send_sems.at[sem], recv_sem=recv_sems.at[sem],
                    device_id=(px, py, pc), device_id_type=MESH))
                recvs.append(pltpu.make_async_remote_copy(
                    src_ref=rows(g_refs[a], a, me), dst_ref=rows(r_refs[a], a, peer),
                    send_sem=send_sems.at[sem], recv_sem=recv_sems.at[sem],
                    device_id=(px, py, pc), device_id_type=MESH))
        for cp in sends:
            cp.start()
        for cp in recvs:
            cp.wait_recv()
        for cp in sends:
            cp.wait_send()
        for cp in mine:
            cp.wait()

    return pl.pallas_call(
        body,
        name=name,
        out_shape=[jax.ShapeDtypeStruct(t.shape, t.dtype) for t in gs],
        in_specs=[pl.BlockSpec(memory_space=pl.ANY)] * n_arr,
        out_specs=[pl.BlockSpec(memory_space=pl.ANY)] * n_arr,
        scratch_shapes=[
            pltpu.SemaphoreType.DMA((7 * n_arr,)),
            pltpu.SemaphoreType.DMA((7 * n_arr,)),
            pltpu.SemaphoreType.DMA((n_arr,)),
        ],
    )(*gs)


HBM_SPEC = pl.BlockSpec(memory_space=pltpu.HBM)
SEM_SPEC = pl.BlockSpec(memory_space=pltpu.SEMAPHORE)
DATAFLOW_EFFECT = pltpu.SideEffectType.DATAFLOW_SIDE_EFFECTING


def _peers(x, y, c):
    for k in range(1, N_DEV):
        px = 1 - x if (k >> 2) & 1 else x
        py = 1 - y if (k >> 1) & 1 else y
        pc = 1 - c if k & 1 else c
        yield k - 1, (px, py, pc), 4 * px + 2 * py + pc


def _push_copies(src_refs, land_refs, send_sems, recv_sems, shapes, whole_src):
    x, y, c = lax.axis_index("x"), lax.axis_index("y"), lax.axis_index("c")
    me = 4 * x + 2 * y + c
    for a, (m_per, _) in enumerate(shapes):
        def block(ref, idx, m_per=m_per):
            return ref.at[pl.ds(idx * m_per, m_per), :]

        for k, peer, pidx in _peers(x, y, c):
            src = src_refs[a] if whole_src else block(src_refs[a], pidx)
            sems = dict(send_sem=send_sems.at[7 * a + k], recv_sem=recv_sems.at[7 * a + k],
                        device_id=peer, device_id_type=MESH)
            yield (pltpu.make_async_remote_copy(src_ref=src, dst_ref=block(land_refs[a], me), **sems),
                   pltpu.make_async_remote_copy(src_ref=src, dst_ref=block(land_refs[a], pidx), **sems))


def _push_begin(srcs, whole_src, name):
    n_arr = len(srcs)
    shapes = [(t.shape[0] if whole_src else t.shape[0] // N_DEV, t.shape[1]) for t in srcs]
    lands = [lax.empty((N_DEV * m, n), t.dtype) for (m, n), t in zip(shapes, srcs)]

    def body(*refs):
        src_refs, land_refs = refs[:n_arr], refs[n_arr:2 * n_arr]
        send_sems, recv_sems = refs[2 * n_arr], refs[2 * n_arr + 1]
        token = refs[-1]
        for outgoing, _ in _push_copies(src_refs, land_refs, send_sems, recv_sems, shapes, whole_src):
            outgoing.start()
        token[...] = jnp.zeros_like(token)

    operands = [pltpu.with_memory_space_constraint(t, pltpu.HBM) for t in list(srcs) + lands]
    outs = pl.pallas_call(
        body, name=name,
        out_shape=(pltpu.SemaphoreType.DMA((7 * n_arr,)), pltpu.SemaphoreType.DMA((7 * n_arr,)),
                   *[pltpu.HBM(t.shape, t.dtype) for t in operands],
                   jax.ShapeDtypeStruct((SUBLANES, LANES), F32)),
        in_specs=[HBM_SPEC] * (2 * n_arr),
        out_specs=(SEM_SPEC, SEM_SPEC, *[HBM_SPEC] * (2 * n_arr), pl.BlockSpec(memory_space=pltpu.VMEM)),
        input_output_aliases={i: 2 + i for i in range(2 * n_arr)},
        compiler_params=pltpu.CompilerParams(has_side_effects=DATAFLOW_EFFECT),
    )(*operands)
    return outs[0], outs[1], list(outs[2:2 + n_arr]), list(outs[2 + n_arr:2 + 2 * n_arr]), outs[-1], whole_src


def _push_end(handle, after, name):
    send_sems, recv_sems, srcs, lands, _, whole_src = handle
    n_arr = len(srcs)
    shapes = [(t.shape[0] // N_DEV, t.shape[1]) for t in lands]

    def body(*refs):
        src_refs, land_refs = refs[:n_arr], refs[n_arr:2 * n_arr]
        send_sems_ref, recv_sems_ref = refs[2 * n_arr], refs[2 * n_arr + 1]
        for outgoing, incoming in _push_copies(src_refs, land_refs, send_sems_ref, recv_sems_ref, shapes, whole_src):
            outgoing.wait_send()
            incoming.wait_recv()

    outs = pl.pallas_call(
        body, name=name,
        out_shape=tuple(pltpu.HBM(t.shape, t.dtype) for t in srcs + lands),
        in_specs=[HBM_SPEC] * (2 * n_arr) + [SEM_SPEC, SEM_SPEC, pl.BlockSpec(memory_space=pl.ANY)],
        out_specs=tuple([HBM_SPEC] * (2 * n_arr)),
        input_output_aliases={i: i for i in range(2 * n_arr)},
        compiler_params=pltpu.CompilerParams(has_side_effects=DATAFLOW_EFFECT),
    )(*srcs, *lands, send_sems, recv_sems, after)
    return list(outs[:n_arr]), list(outs[n_arr:])


def _sum_devices(r, name, rows_per_step=ROW_BLOCK):
    _, m, n = r.shape
    tm = _tile(m, rows_per_step, 8)

    def body(r_ref, o_ref):
        acc = r_ref[0].astype(F32)
        for s in range(1, N_DEV):
            acc = acc + r_ref[s].astype(F32)
        o_ref[...] = acc

    return pl.pallas_call(
        body,
        name=name,
        grid=(m // tm,),
        out_shape=jax.ShapeDtypeStruct((m, n), F32),
        in_specs=[pl.BlockSpec((N_DEV, tm, n), lambda i: (0, i, 0))],
        out_specs=pl.BlockSpec((tm, n), lambda i: (i, 0)),
        compiler_params=pltpu.CompilerParams(dimension_semantics=("parallel",)),
    )(r)


def _get(ref):
    return ref[0] if len(ref.shape) == 3 else ref[...]


def _put(ref, val):
    if len(ref.shape) == 3:
        ref[0] = val
    else:
        ref[...] = val


def _mm_call(name, a, b, a_spec, b_spec, out_sds, o_spec, grid, dims, acc_shape, bias=None,
             res=None, gate=None, raw_out=False, vec_spec=None):
    nk = grid[2]
    operands, in_specs = [a, b], [a_spec, b_spec]
    if bias is not None:
        operands.append(bias)
        in_specs.append(vec_spec)
    if res is not None:
        operands += [res, gate]
        in_specs += [o_spec, vec_spec]
    out_shape, out_specs = [out_sds], [o_spec]
    if raw_out:
        out_shape.append(jax.ShapeDtypeStruct(out_sds.shape, F32))
        out_specs.append(o_spec)

    def body(*refs):
        it = iter(refs)
        a_ref, b_ref = next(it), next(it)
        bias_ref = next(it) if bias is not None else None
        res_ref, gate_ref = (next(it), next(it)) if res is not None else (None, None)
        o_ref = next(it)
        raw_ref = next(it) if raw_out else None
        acc = next(it) if nk > 1 else None
        k = pl.program_id(2)
        part = _dot(_get(a_ref).astype(BF16), _get(b_ref).astype(BF16), dims)

        def finish(y):
            if bias_ref is not None:
                y = y + bias_ref[...]
            if raw_ref is not None:
                raw_ref[...] = y
            if res_ref is not None:
                y = res_ref[...] + gate_ref[...] * y
            _put(o_ref, y.astype(out_sds.dtype))

        if nk == 1:
            finish(part)
        else:
            @pl.when(k == 0)
            def _():
                acc[...] = part

            @pl.when(k > 0)
            def _():
                acc[...] += part

            @pl.when(k == nk - 1)
            def _():
                finish(acc[...])

    outs = pl.pallas_call(
        body,
        name=name,
        grid=grid,
        out_shape=out_shape,
        in_specs=in_specs,
        out_specs=out_specs,
        scratch_shapes=[pltpu.VMEM(acc_shape, F32)] if nk > 1 else [],
        compiler_params=pltpu.CompilerParams(dimension_semantics=("parallel", "parallel", "arbitrary")),
    )(*operands)
    return outs if raw_out else outs[0]


def _mm(a, b, mode, name, out_dtype=F32, bias=None, res=None, gate=None, raw_out=False,
        tm=512, tn=1024, tk=1024, a_row_off=0):
    if mode == "nn":
        K, N = b.shape
        M = a.shape[0] - a_row_off
    elif mode == "nt":
        N, K = b.shape
        M = a.shape[0] - a_row_off
    else:
        (K, M), N = a.shape, b.shape[1]
    tm, tn, tk = _tile(M, tm), _tile(N, tn), _tile(K, tk)
    off = a_row_off // tm
    dims = {"nn": NN, "nt": NT, "tn": TN}[mode]
    a_spec = (pl.BlockSpec((tk, tm), lambda i, j, k: (k, i)) if mode == "tn"
              else pl.BlockSpec((tm, tk), lambda i, j, k: (i + off, k)))
    b_spec = (pl.BlockSpec((tn, tk), lambda i, j, k: (j, k)) if mode == "nt"
              else pl.BlockSpec((tk, tn), lambda i, j, k: (k, j)))
    return _mm_call(name, a, b, a_spec, b_spec, jax.ShapeDtypeStruct((M, N), out_dtype),
                    pl.BlockSpec((tm, tn), lambda i, j, k: (i, j)), (M // tm, N // tn, K // tk), dims,
                    (tm, tn), bias, res, gate, raw_out, pl.BlockSpec((1, tn), lambda i, j, k: (0, j)))


def _mm_to_shards(a, b3, mode, name, out_dtype, tm=512):
    M, K = a.shape
    S = b3.shape[0]
    n = b3.shape[2] if mode == "nn" else b3.shape[1]
    tm = _tile(M, tm)
    return _mm_call(name, a, b3, pl.BlockSpec((tm, K), lambda i, j, k: (i, 0)),
                    pl.BlockSpec((1,) + b3.shape[1:], lambda i, j, k: (j, 0, 0)),
                    jax.ShapeDtypeStruct((S, M, n), out_dtype),
                    pl.BlockSpec((1, tm, n), lambda i, j, k: (j, i, 0)), (M // tm, S, 1),
                    NN if mode == "nn" else NT, (tm, n))


def _mm_over_shards(a3, b3, mode, name, out_dtype=F32, res=None, gate=None, raw_out=False, tm=512, tn=1024):
    S, M, kk = a3.shape
    N = b3.shape[2] if mode == "nn" else b3.shape[1]
    tm, tn = _tile(M, tm), _tile(N, tn)
    b_spec = (pl.BlockSpec((1, kk, tn), lambda i, j, k: (k, 0, j)) if mode == "nn"
              else pl.BlockSpec((1, tn, kk), lambda i, j, k: (k, j, 0)))
    return _mm_call(name, a3, b3, pl.BlockSpec((1, tm, kk), lambda i, j, k: (k, i, 0)), b_spec,
                    jax.ShapeDtypeStruct((M, N), out_dtype), pl.BlockSpec((tm, tn), lambda i, j, k: (i, j)),
                    (M // tm, N // tn, S), NN if mode == "nn" else NT, (tm, tn), None, res, gate, raw_out,
                    pl.BlockSpec((1, tn), lambda i, j, k: (0, j)))


def _mm_tn_shard_rows(a3, b, name, out_dtype, tn=1024, tk=1024):
    S, T, m = a3.shape
    N = b.shape[1]
    tn, tk = _tile(N, tn), _tile(T, tk)
    return _mm_call(name, a3, b, pl.BlockSpec((1, tk, m), lambda i, j, k: (i, k, 0)),
                    pl.BlockSpec((tk, tn), lambda i, j, k: (k, j)), jax.ShapeDtypeStruct((S, m, N), out_dtype),
                    pl.BlockSpec((1, m, tn), lambda i, j, k: (i, 0, j)), (S, N // tn, T // tk), TN, (m, tn))


def _mm_tn_shard_cols(a, b3, name, out_dtype, tm=1024, tk=1024):
    T, M = a.shape
    S, _, n = b3.shape
    tm, tk = _tile(M, tm), _tile(T, tk)
    return _mm_call(name, a, b3, pl.BlockSpec((tk, tm), lambda i, j, k: (k, i)),
                    pl.BlockSpec((1, tk, n), lambda i, j, k: (j, k, 0)), jax.ShapeDtypeStruct((S, M, n), out_dtype),
                    pl.BlockSpec((1, tm, n), lambda i, j, k: (j, i, 0)), (M // tm, S, T // tk), TN, (tm, n))


def _row_spec(tm, width, off=0):
    return pl.BlockSpec((tm, width), lambda i: (i + off, 0))


def _vec_spec(width):
    return pl.BlockSpec((1, width), lambda i: (0, 0))


def _norm_mod_fwd(h, g, sc, sh, name):
    R, Dm = h.shape
    tm = _tile(R, ROW_BLOCK, 8)

    def body(h_ref, g_ref, sc_ref, sh_ref, o_ref):
        hv = h_ref[...]
        r = lax.rsqrt(jnp.mean(hv * hv, axis=-1, keepdims=True) + EPS)
        o_ref[...] = ((hv * r) * g_ref[...] * (1.0 + sc_ref[...]) + sh_ref[...]).astype(BF16)

    return pl.pallas_call(
        body, name=name, grid=(R // tm,),
        out_shape=jax.ShapeDtypeStruct((R, Dm), BF16),
        in_specs=[_row_spec(tm, Dm), _vec_spec(Dm), _vec_spec(Dm), _vec_spec(Dm)],
        out_specs=_row_spec(tm, Dm),
        compiler_params=pltpu.CompilerParams(dimension_semantics=("parallel",)),
    )(h, g, sc, sh)


def _norm_mod_fwd_cat(hc, h, g, csc, csh, sc, sh, name):
    (C, Dm), T = hc.shape, h.shape[0]
    tm = _tile(math.gcd(C, T), ROW_BLOCK, 8)
    off = C // tm

    def body(hc_ref, h_ref, g_ref, csc_ref, csh_ref, sc_ref, sh_ref, o_ref):
        is_ctx = pl.program_id(0) < off
        hv = jnp.where(is_ctx, hc_ref[...], h_ref[...])
        scv = jnp.where(is_ctx, csc_ref[...], sc_ref[...])
        shv = jnp.where(is_ctx, csh_ref[...], sh_ref[...])
        r = lax.rsqrt(jnp.mean(hv * hv, axis=-1, keepdims=True) + EPS)
        o_ref[...] = ((hv * r) * g_ref[...] * (1.0 + scv) + shv).astype(BF16)

    return pl.pallas_call(
        body, name=name, grid=((C + T) // tm,),
        out_shape=jax.ShapeDtypeStruct((C + T, Dm), BF16),
        in_specs=[pl.BlockSpec((tm, Dm), lambda i: (jnp.minimum(i, off - 1), 0)),
                  pl.BlockSpec((tm, Dm), lambda i: (jnp.maximum(i - off, 0), 0))] + [_vec_spec(Dm)] * 5,
        out_specs=_row_spec(tm, Dm),
        compiler_params=pltpu.CompilerParams(dimension_semantics=("parallel",)),
    )(hc, h, g, csc, csh, sc, sh)


def _norm_mod_bwd(h, g, sc, dxm, dres, name, dxm_row_off=0):
    R, Dm = h.shape
    tm = _tile(R, ROW_BLOCK, 8)
    off = dxm_row_off // tm
    has_res = dres is not None

    def body(*refs):
        it = iter(refs)
        h_ref, g_ref, sc_ref, dx_ref = next(it), next(it), next(it), next(it)
        dres_ref = next(it) if has_res else None
        dh_ref, da_ref, dsh_ref = next(it), next(it), next(it)
        i = pl.program_id(0)

        @pl.when(i == 0)
        def _():
            da_ref[...] = jnp.zeros_like(da_ref)
            dsh_ref[...] = jnp.zeros_like(dsh_ref)

        hv = h_ref[...]
        dx = dx_ref[...].astype(F32)
        r = lax.rsqrt(jnp.mean(hv * hv, axis=-1, keepdims=True) + EPS)
        n = hv * r
        da_ref[...] += jnp.sum(dx * n, axis=0, keepdims=True)
        dsh_ref[...] += jnp.sum(dx, axis=0, keepdims=True)
        dn = dx * (g_ref[...] * (1.0 + sc_ref[...]))
        dh = r * (dn - n * jnp.mean(dn * n, axis=-1, keepdims=True))
        if has_res:
            dh = dh + dres_ref[...]
        dh_ref[...] = dh

    operands = [h, g, sc, dxm] + ([dres] if has_res else [])
    in_specs = [_row_spec(tm, Dm), _vec_spec(Dm), _vec_spec(Dm), _row_spec(tm, Dm, off)]
    in_specs += [_row_spec(tm, Dm)] if has_res else []
    return pl.pallas_call(
        body, name=name, grid=(R // tm,),
        out_shape=[jax.ShapeDtypeStruct((R, Dm), F32), jax.ShapeDtypeStruct((1, Dm), F32),
                   jax.ShapeDtypeStruct((1, Dm), F32)],
        in_specs=in_specs,
        out_specs=[_row_spec(tm, Dm), _vec_spec(Dm), _vec_spec(Dm)],
        compiler_params=pltpu.CompilerParams(dimension_semantics=("arbitrary",)),
    )(*operands)


def _gate_bwd(dh, y, gt, name):
    R, Dm = dh.shape
    tm = _tile(R, ROW_BLOCK, 8)

    def body(dh_ref, y_ref, gt_ref, dy_ref, dgt_ref, dsum_ref):
        i = pl.program_id(0)

        @pl.when(i == 0)
        def _():
            dgt_ref[...] = jnp.zeros_like(dgt_ref)
            dsum_ref[...] = jnp.zeros_like(dsum_ref)

        dhv = dh_ref[...]
        dy = dhv * gt_ref[...]
        dgt_ref[...] += jnp.sum(dhv * y_ref[...], axis=0, keepdims=True)
        dsum_ref[...] += jnp.sum(dy, axis=0, keepdims=True)
        dy_ref[...] = dy.astype(BF16)

    return pl.pallas_call(
        body, name=name, grid=(R // tm,),
        out_shape=[jax.ShapeDtypeStruct((R, Dm), BF16), jax.ShapeDtypeStruct((1, Dm), F32),
                   jax.ShapeDtypeStruct((1, Dm), F32)],
        in_specs=[_row_spec(tm, Dm), _row_spec(tm, Dm), _vec_spec(Dm)],
        out_specs=[_row_spec(tm, Dm), _vec_spec(Dm), _vec_spec(Dm)],
        compiler_params=pltpu.CompilerParams(dimension_semantics=("arbitrary",)),
    )(dh, y, gt)


def _ffn_in_swiglu(xf, w3, name, tm=1024):
    T, K = xf.shape
    S, _, n = w3.shape
    half = S // 2
    tm = _tile(T, tm)

    def body(a_ref, wg_ref, wu_ref, gu_ref, act_ref):
        a = a_ref[...]
        g = _dot(a, wg_ref[0], NN)
        u = _dot(a, wu_ref[0], NN)
        gu_ref[0, 0] = g.astype(BF16)
        gu_ref[1, 0] = u.astype(BF16)
        act_ref[0] = (_silu(g) * u).astype(BF16)

    return pl.pallas_call(
        body, name=name, grid=(T // tm, half),
        out_shape=[jax.ShapeDtypeStruct((2, half, T, n), BF16), jax.ShapeDtypeStruct((half, T, n), BF16)],
        in_specs=[pl.BlockSpec((tm, K), lambda i, j: (i, 0)),
                  pl.BlockSpec((1, K, n), lambda i, j: (j, 0, 0)),
                  pl.BlockSpec((1, K, n), lambda i, j: (j + half, 0, 0))],
        out_specs=[pl.BlockSpec((2, 1, tm, n), lambda i, j: (0, j, i, 0)),
                   pl.BlockSpec((1, tm, n), lambda i, j: (j, i, 0))],
        compiler_params=pltpu.CompilerParams(dimension_semantics=("parallel", "parallel")),
    )(xf, w3, w3)


def _ffn_out_dx_swiglu(df, wo, gu, name, tm=1024):
    T, Dm = df.shape
    half, n, _ = wo.shape
    tm = _tile(T, tm)

    def body(df_ref, w_ref, gu_ref, o_ref):
        da = _dot(df_ref[...], w_ref[0], NT)
        g = gu_ref[0, 0].astype(F32)
        u = gu_ref[1, 0].astype(F32)
        s = _sigmoid(g)
        o_ref[0, 0] = (da * u * (s * (1.0 + g * (1.0 - s)))).astype(BF16)
        o_ref[1, 0] = (da * (g * s)).astype(BF16)

    gu_spec = pl.BlockSpec((2, 1, tm, n), lambda i, j: (0, j, i, 0))
    return pl.pallas_call(
        body, name=name, grid=(T // tm, half),
        out_shape=jax.ShapeDtypeStruct(gu.shape, BF16),
        in_specs=[pl.BlockSpec((tm, Dm), lambda i, j: (i, 0)),
                  pl.BlockSpec((1, n, Dm), lambda i, j: (j, 0, 0)), gu_spec],
        out_specs=gu_spec,
        compiler_params=pltpu.CompilerParams(dimension_semantics=("parallel", "parallel")),
    )(df, wo, gu)


def _glu_fwd(ag, name):
    R = ag.shape[0]
    tm = _tile(R, ROW_BLOCK, 8)

    def body(ag_ref, o_ref):
        o_ref[...] = ag_ref[:, :D_MODEL] * _sigmoid(ag_ref[:, D_MODEL:])

    return pl.pallas_call(
        body, name=name, grid=(R // tm,),
        out_shape=jax.ShapeDtypeStruct((R, D_MODEL), F32),
        in_specs=[_row_spec(tm, 2 * D_MODEL)],
        out_specs=_row_spec(tm, D_MODEL),
        compiler_params=pltpu.CompilerParams(dimension_semantics=("parallel",)),
    )(ag)


def _glu_bwd(ag, dhg, name):
    R = ag.shape[0]
    tm = _tile(R, ROW_BLOCK, 8)

    def body(ag_ref, dh_ref, o_ref, s_ref):
        i = pl.program_id(0)

        @pl.when(i == 0)
        def _():
            s_ref[...] = jnp.zeros_like(s_ref)

        a = ag_ref[:, :D_MODEL]
        s = _sigmoid(ag_ref[:, D_MODEL:])
        dh = dh_ref[...]
        da = dh * s
        dg = dh * a * s * (1.0 - s)
        o_ref[:, :D_MODEL] = da.astype(BF16)
        o_ref[:, D_MODEL:] = dg.astype(BF16)
        s_ref[:, :D_MODEL] += jnp.sum(da, axis=0, keepdims=True)
        s_ref[:, D_MODEL:] += jnp.sum(dg, axis=0, keepdims=True)

    return pl.pallas_call(
        body, name=name, grid=(R // tm,),
        out_shape=[jax.ShapeDtypeStruct((R, 2 * D_MODEL), BF16), jax.ShapeDtypeStruct((1, 2 * D_MODEL), F32)],
        in_specs=[_row_spec(tm, 2 * D_MODEL), _row_spec(tm, D_MODEL)],
        out_specs=[_row_spec(tm, 2 * D_MODEL), _vec_spec(2 * D_MODEL)],
        compiler_params=pltpu.CompilerParams(dimension_semantics=("arbitrary",)),
    )(ag, dhg)


def _halo_specs(tm, nblk, width):
    per = tm // CONV_HALO
    prev = pl.BlockSpec((CONV_HALO, width), lambda i: (jnp.maximum(i * per - 1, 0), 0))
    nxt = pl.BlockSpec((CONV_HALO, width), lambda i: (jnp.minimum((i + 1) * per, nblk * per - 1), 0))
    return prev, nxt


def _fill_halo(scr, prev_ref, cur_ref, next_ref, i, nblk, tm):
    scr[0:CONV_HALO, :] = jnp.where(i > 0, prev_ref[...], 0.0)
    scr[CONV_HALO:CONV_HALO + tm, :] = cur_ref[...]
    scr[CONV_HALO + tm:2 * CONV_HALO + tm, :] = jnp.where(i < nblk - 1, next_ref[...], 0.0)


CONV_ROWS = 128


def _windows(scr, cols, tm):
    reach = (CONV_WIDTH // SUBLANES) * SUBLANES
    for r in range(SUBLANES):
        base = scr[pl.ds(r, tm + reach), cols]
        for a in range(reach // SUBLANES + 1):
            off = SUBLANES * a + r
            if 1 <= off <= CONV_WIDTH:
                yield off, base[SUBLANES * a:SUBLANES * a + tm]


def _conv_fwd(hg, w_dw, b_dw, name):
    R, Dm = hg.shape
    tm = _tile(R, CONV_ROWS, CONV_HALO)
    nblk = R // tm
    prev_spec, next_spec = _halo_specs(tm, nblk, Dm)

    def body(prev_ref, cur_ref, next_ref, w_ref, bdw_ref, hd_ref, scr):
        _fill_halo(scr, prev_ref, cur_ref, next_ref, pl.program_id(0), nblk, tm)
        for cb in range(Dm // LANES):
            cols = slice(cb * LANES, (cb + 1) * LANES)
            acc = jnp.zeros((tm, LANES), F32) + bdw_ref[:, cols]
            for off, win in _windows(scr, cols, tm):
                acc = acc + w_ref[off - 1:off, cols] * win
            hd_ref[:, cols] = acc

    return pl.pallas_call(
        body, name=name, grid=(nblk,),
        out_shape=jax.ShapeDtypeStruct((R, Dm), F32),
        in_specs=[prev_spec, _row_spec(tm, Dm), next_spec,
                  pl.BlockSpec((CONV_WIDTH, Dm), lambda i: (0, 0)), _vec_spec(Dm)],
        out_specs=_row_spec(tm, Dm),
        scratch_shapes=[pltpu.VMEM((tm + 2 * CONV_HALO, Dm), F32)],
        compiler_params=pltpu.CompilerParams(dimension_semantics=("parallel",)),
    )(hg, hg, hg, w_dw, b_dw)


def _ln_silu_fwd(hd, ln_g, ln_b, name):
    R, Dm = hd.shape
    tm = _tile(R, ROW_BLOCK, 8)

    def body(hd_ref, g_ref, b_ref, hs_ref):
        hd = hd_ref[...]
        xc = hd - jnp.mean(hd, axis=-1, keepdims=True)
        rs = lax.rsqrt(jnp.mean(xc * xc, axis=-1, keepdims=True) + EPS)
        hs_ref[...] = _silu(xc * rs * g_ref[...] + b_ref[...]).astype(BF16)

    return pl.pallas_call(
        body, name=name, grid=(R // tm,),
        out_shape=jax.ShapeDtypeStruct((R, Dm), BF16),
        in_specs=[_row_spec(tm, Dm), _vec_spec(Dm), _vec_spec(Dm)],
        out_specs=_row_spec(tm, Dm),
        compiler_params=pltpu.CompilerParams(dimension_semantics=("parallel",)),
    )(hd, ln_g, ln_b)


def _ln_silu_bwd(dhs, hd, ln_g, ln_b, name):
    R, Dm = hd.shape
    tm = _tile(R, ROW_BLOCK, 8)

    def body(dhs_ref, hd_ref, g_ref, b_ref, dhd_ref, dg_ref, db_ref, dsum_ref):
        i = pl.program_id(0)

        @pl.when(i == 0)
        def _():
            dg_ref[...] = jnp.zeros_like(dg_ref)
            db_ref[...] = jnp.zeros_like(db_ref)
            dsum_ref[...] = jnp.zeros_like(dsum_ref)

        hd = hd_ref[...]
        mu = jnp.mean(hd, axis=-1, keepdims=True)
        xc = hd - mu
        rs = lax.rsqrt(jnp.mean(xc * xc, axis=-1, keepdims=True) + EPS)
        z = xc * rs
        hl = z * g_ref[...] + b_ref[...]
        dhl = dhs_ref[...] * _dsilu(hl)
        dg_ref[...] += jnp.sum(dhl * z, axis=0, keepdims=True)
        db_ref[...] += jnp.sum(dhl, axis=0, keepdims=True)
        dz = dhl * g_ref[...]
        dhd = rs * (dz - jnp.mean(dz, axis=-1, keepdims=True) - z * jnp.mean(dz * z, axis=-1, keepdims=True))
        dsum_ref[...] += jnp.sum(dhd, axis=0, keepdims=True)
        dhd_ref[...] = dhd

    return pl.pallas_call(
        body, name=name, grid=(R // tm,),
        out_shape=[jax.ShapeDtypeStruct((R, Dm), F32)] + [jax.ShapeDtypeStruct((1, Dm), F32)] * 3,
        in_specs=[_row_spec(tm, Dm), _row_spec(tm, Dm), _vec_spec(Dm), _vec_spec(Dm)],
        out_specs=[_row_spec(tm, Dm), _vec_spec(Dm), _vec_spec(Dm), _vec_spec(Dm)],
        compiler_params=pltpu.CompilerParams(dimension_semantics=("arbitrary",)),
    )(dhs, hd, ln_g, ln_b)


def _conv_bwd(dhd, hg, w_dw, name):
    R, Dm = hg.shape
    tm = _tile(R, CONV_ROWS, CONV_HALO)
    nblk = R // tm
    prev_spec, next_spec = _halo_specs(tm, nblk, Dm)

    def body(dprev, dcur, dnext, gprev, gcur, gnext, w_ref, dhg_ref, dw_ref, dscr, gscr, dwp):
        i = pl.program_id(0)

        @pl.when(i == 0)
        def _():
            dwp[...] = jnp.zeros_like(dwp)

        _fill_halo(dscr, dprev, dcur, dnext, i, nblk, tm)
        _fill_halo(gscr, gprev, gcur, gnext, i, nblk, tm)
        for cb in range(Dm // LANES):
            cols = slice(cb * LANES, (cb + 1) * LANES)
            acc = jnp.zeros((tm, LANES), F32)
            for off, win in _windows(dscr, cols, tm):
                j = CONV_WIDTH - off
                acc = acc + w_ref[j:j + 1, cols] * win
            dhg_ref[:, cols] = acc
            d_here = dcur[:, cols]
            for off, win in _windows(gscr, cols, tm):
                j = off - 1
                prod = d_here * win
                part = prod[0:SUBLANES]
                for k in range(1, tm // SUBLANES):
                    part = part + prod[k * SUBLANES:(k + 1) * SUBLANES]
                dwp[j * SUBLANES:(j + 1) * SUBLANES, cols] += part

        @pl.when(i == nblk - 1)
        def _():
            for j in range(CONV_WIDTH):
                dw_ref[j:j + 1, :] = jnp.sum(dwp[j * SUBLANES:(j + 1) * SUBLANES, :], axis=0, keepdims=True)

    return pl.pallas_call(
        body, name=name, grid=(nblk,),
        out_shape=[jax.ShapeDtypeStruct((R, Dm), F32), jax.ShapeDtypeStruct((CONV_WIDTH, Dm), F32)],
        in_specs=[prev_spec, _row_spec(tm, Dm), next_spec, prev_spec, _row_spec(tm, Dm), next_spec,
                  pl.BlockSpec((CONV_WIDTH, Dm), lambda i: (0, 0))],
        out_specs=[_row_spec(tm, Dm), pl.BlockSpec((CONV_WIDTH, Dm), lambda i: (0, 0))],
        scratch_shapes=[pltpu.VMEM((tm + 2 * CONV_HALO, Dm), F32)] * 2
        + [pltpu.VMEM((CONV_WIDTH * SUBLANES, Dm), F32)],
        compiler_params=pltpu.CompilerParams(dimension_semantics=("arbitrary",)),
    )(dhd, dhd, dhd, hg, hg, hg, w_dw)


def _swap16(y, lane):
    return jnp.where((lane & 16) == 0, pltpu.roll(y, LANES - 16, 1), pltpu.roll(y, 16, 1))


def _head_mean(v, bd):
    hi, lo = _split_bf16(v)
    return (_dot(hi, bd, NN) + _dot(lo, bd, NN)) * (1.0 / HEAD_DIM)


Q_COLS = (0, ATTN_WIDTH)
K_COLS = (ATTN_WIDTH, ATTN_WIDTH + HEAD_DIM * 2)
V_COLS = (K_COLS[1], K_COLS[1] + HEAD_DIM * 2)
SU_COLS = (V_COLS[1], V_COLS[1] + SG_WIDTH)
SV_COLS = (SU_COLS[1], SU_COLS[1] + SG_WIDTH)


def _mix_prep_fwd(p, ctx_rows, cos, sin, qg, kg, bd, w_sp, b_spt, name):
    TT = p.shape[0]
    off = ctx_rows // CHUNK
    q_scale = HEAD_DIM ** -0.5

    def body(p_ref, cos_ref, sin_ref, qg_ref, kg_ref, bd_ref, w_ref, b_ref,
             q_ref, kp_ref, vp_ref, kt_ref, vt_ref, sg_ref):
        lane = lax.broadcasted_iota(jnp.int32, (CHUNK, LANES), 1)
        low = lane < HEAD_DIM
        cs, sn, bdv = cos_ref[...], sin_ref[...], bd_ref[...]

        def norm_rope(xv, gain):
            r = lax.rsqrt(_head_mean(xv * xv, bdv) + EPS)
            yv = xv * r * gain
            return yv * cs + _swap16(yv, lane) * sn

        def pad_heads(ref, t):
            tr = pltpu.roll(t, HEAD_DIM, 1)
            ref[0, 0] = jnp.where(low, t, 0.0).astype(BF16)
            ref[0, 1] = jnp.where(low, 0.0, tr).astype(BF16)
            ref[1, 0] = jnp.where(low, tr, 0.0).astype(BF16)
            ref[1, 1] = jnp.where(low, 0.0, t).astype(BF16)

        for a in range(ATTN_WIDTH // LANES):
            xv = p_ref[:, a * LANES:(a + 1) * LANES]
            q_ref[:, a * LANES:(a + 1) * LANES] = (norm_rope(xv, qg_ref[...]) * q_scale).astype(BF16)
        kh = norm_rope(p_ref[:, K_COLS[0]:K_COLS[1]], kg_ref[...])
        pad_heads(kp_ref, kh)
        vh = p_ref[:, V_COLS[0]:V_COLS[1]]
        pad_heads(vp_ref, vh)
        for t_ref, t in ((kt_ref, kh.T), (vt_ref, vh.T)):
            t_ref[0] = t[:HEAD_DIM].astype(BF16)
            t_ref[1] = t[HEAD_DIM:].astype(BF16)
        for g in range(N_SG_GROUPS):
            u = _gelu(p_ref[:, SU_COLS[0] + g * LANES:SU_COLS[0] + (g + 1) * LANES])
            vg = _gelu(p_ref[:, SV_COLS[0] + g * LANES:SV_COLS[0] + (g + 1) * LANES])
            xc = vg - jnp.mean(vg, axis=-1, keepdims=True)
            vn = xc * lax.rsqrt(jnp.mean(xc * xc, axis=-1, keepdims=True) + EPS)
            mixed = _dot(w_ref[g].astype(BF16), vn.astype(BF16), NN) + b_ref[:, g:g + 1]
            sg_ref[:, g * LANES:(g + 1) * LANES] = (u * mixed).astype(BF16)

    def row(width):
        return pl.BlockSpec((CHUNK, width), lambda i: (i, 0))

    def whole(shape):
        return pl.BlockSpec(shape, lambda i: (0,) * len(shape))

    pad_spec = pl.BlockSpec((2, 2, CHUNK, LANES), lambda i: (0, 0, i, 0))
    return pl.pallas_call(
        body, name=name, grid=(TT // CHUNK,),
        out_shape=[jax.ShapeDtypeStruct((TT, ATTN_WIDTH), BF16),
                   jax.ShapeDtypeStruct((2, 2, TT, LANES), BF16), jax.ShapeDtypeStruct((2, 2, TT, LANES), BF16),
                   jax.ShapeDtypeStruct((2, HEAD_DIM, TT), BF16), jax.ShapeDtypeStruct((2, HEAD_DIM, TT), BF16),
                   jax.ShapeDtypeStruct((TT - ctx_rows, ATTN_WIDTH + SG_WIDTH), BF16)],
        in_specs=[row(IN_WIDTH), row(LANES), row(LANES), whole((1, LANES)), whole((1, LANES)),
                  whole((LANES, LANES)), whole((N_SG_GROUPS, CHUNK, CHUNK)), whole((CHUNK, N_SG_GROUPS))],
        out_specs=[row(ATTN_WIDTH), pad_spec, pad_spec,
                   pl.BlockSpec((2, HEAD_DIM, CHUNK), lambda i: (0, 0, i)),
                   pl.BlockSpec((2, HEAD_DIM, CHUNK), lambda i: (0, 0, i)),
                   pl.BlockSpec((CHUNK, SG_WIDTH), lambda i: (jnp.maximum(i - off, 0), 1))],
        compiler_params=pltpu.CompilerParams(dimension_semantics=("arbitrary",)),
    )(p, cos, sin, qg, kg, bd, w_sp, b_spt)


def _mix_prep_bwd(p, dq, f, dao, ctx_rows, cos, sin, qg, kg, bd, w_sp, w_spt, b_spt, name):
    TT = p.shape[0]
    off = ctx_rows // CHUNK
    q_scale = HEAD_DIM ** -0.5

    def body(p_ref, dq_ref, f_ref, dsg_ref, cos_ref, sin_ref, qg_ref, kg_ref, bd_ref, w_ref, wt_ref,
             b_ref, dp_ref, dqg_ref, dkg_ref, dw_ref, db_ref):
        i = pl.program_id(0)

        @pl.when(i == 0)
        def _():
            dqg_ref[...] = jnp.zeros_like(dqg_ref)
            dkg_ref[...] = jnp.zeros_like(dkg_ref)
            dw_ref[...] = jnp.zeros_like(dw_ref)
            db_ref[...] = jnp.zeros_like(db_ref)

        latent = (i >= off).astype(F32)
        lane = lax.broadcasted_iota(jnp.int32, (CHUNK, LANES), 1)
        low = lane < HEAD_DIM
        cs, sn, bdv = cos_ref[...], sin_ref[...], bd_ref[...]

        def fold(b0):
            return jnp.where(low, f_ref[0, b0] + pltpu.roll(f_ref[0, b0 + 1], HEAD_DIM, 1),
                             pltpu.roll(f_ref[1, b0], HEAD_DIM, 1) + f_ref[1, b0 + 1])

        def norm_rope_bwd(xv, dout, gain):
            r = lax.rsqrt(_head_mean(xv * xv, bdv) + EPS)
            n = xv * r
            dy = dout * cs + _swap16(dout * sn, lane)
            dn = dy * gain
            dx = r * (dn - n * _head_mean(dn * n, bdv))
            return dx, jnp.sum(dy * n, axis=0, keepdims=True)

        for a in range(ATTN_WIDTH // LANES):
            cols = slice(a * LANES, (a + 1) * LANES)
            dx, dg = norm_rope_bwd(p_ref[:, cols], dq_ref[:, cols] * (latent * q_scale), qg_ref[...])
            dp_ref[:, cols] = dx.astype(BF16)
            dqg_ref[...] += dg
        dx, dg = norm_rope_bwd(p_ref[:, K_COLS[0]:K_COLS[1]], fold(0), kg_ref[...])
        dp_ref[:, K_COLS[0]:K_COLS[1]] = dx.astype(BF16)
        dkg_ref[...] += dg
        dp_ref[:, V_COLS[0]:V_COLS[1]] = fold(2).astype(BF16)
        for g in range(N_SG_GROUPS):
            su = p_ref[:, SU_COLS[0] + g * LANES:SU_COLS[0] + (g + 1) * LANES]
            sv = p_ref[:, SV_COLS[0] + g * LANES:SV_COLS[0] + (g + 1) * LANES]
            u, vg = _gelu(su), _gelu(sv)
            xc = vg - jnp.mean(vg, axis=-1, keepdims=True)
            rs = lax.rsqrt(jnp.mean(xc * xc, axis=-1, keepdims=True) + EPS)
            vn = xc * rs
            vnb = vn.astype(BF16)
            mixed = _dot(w_ref[g].astype(BF16), vnb, NN) + b_ref[:, g:g + 1]
            dsg = dsg_ref[:, g * LANES:(g + 1) * LANES] * latent
            du = dsg * mixed
            dmix = dsg * u
            dmb = dmix.astype(BF16)
            db_ref[:, g:g + 1] += jnp.sum(dmix, axis=-1, keepdims=True)
            dw_ref[g] += _dot(dmb, vnb, NT)
            dvn = _dot(wt_ref[g].astype(BF16), dmb, NN)
            dvg = rs * (dvn - jnp.mean(dvn, axis=-1, keepdims=True)
                        - vn * jnp.mean(dvn * vn, axis=-1, keepdims=True))
            dp_ref[:, SU_COLS[0] + g * LANES:SU_COLS[0] + (g + 1) * LANES] = (du * _dgelu(su)).astype(BF16)
            dp_ref[:, SV_COLS[0] + g * LANES:SV_COLS[0] + (g + 1) * LANES] = (dvg * _dgelu(sv)).astype(BF16)

    def row(width):
        return pl.BlockSpec((CHUNK, width), lambda i: (i, 0))

    def latent_row(width, col_block):
        return pl.BlockSpec((CHUNK, width), lambda i: (jnp.maximum(i - off, 0), col_block))

    def whole(shape):
        return pl.BlockSpec(shape, lambda i: (0,) * len(shape))

    return pl.pallas_call(
        body, name=name, grid=(TT // CHUNK,),
        out_shape=[jax.ShapeDtypeStruct((TT, IN_WIDTH), BF16), jax.ShapeDtypeStruct((1, LANES), F32),
                   jax.ShapeDtypeStruct((1, LANES), F32),
                   jax.ShapeDtypeStruct((N_SG_GROUPS, CHUNK, CHUNK), F32),
                   jax.ShapeDtypeStruct((CHUNK, N_SG_GROUPS), F32)],
        in_specs=[row(IN_WIDTH), latent_row(ATTN_WIDTH, 0),
                  pl.BlockSpec((2, 4, CHUNK, LANES), lambda i: (0, 0, i, 0)),
                  latent_row(SG_WIDTH, 1), row(LANES), row(LANES), whole((1, LANES)), whole((1, LANES)),
                  whole((LANES, LANES)), whole((N_SG_GROUPS, CHUNK, CHUNK)),
                  whole((N_SG_GROUPS, CHUNK, CHUNK)), whole((CHUNK, N_SG_GROUPS))],
        out_specs=[row(IN_WIDTH), whole((1, LANES)), whole((1, LANES)),
                   whole((N_SG_GROUPS, CHUNK, CHUNK)), whole((CHUNK, N_SG_GROUPS))],
        compiler_params=pltpu.CompilerParams(dimension_semantics=("arbitrary",)),
    )(p, dq, f, dao, cos, sin, qg, kg, bd, w_sp, w_spt, b_spt)


def _col_reduce(t, pair_op, reduce_op, slab=256):
    R = t.shape[0]
    slab = _tile(R, slab, SUBLANES)
    part = t[0:slab]
    for k in range(1, R // slab):
        part = pair_op(part, t[k * slab:(k + 1) * slab])
    return reduce_op(part, axis=0, keepdims=True)


def _attn_fwd(q, kpad, vt, ao, ctx_rows, name, tq=256):
    TT = q.shape[0]
    T = TT - ctx_rows
    tq = _tile(T, tq)
    off = ctx_rows // tq
    group = 2 * LANES

    def body(q_ref, k_ref, vt_ref, ao_in, o_ref, lse_ref):
        del ao_in
        vtv = vt_ref[0]
        for a in range(2):
            qa = q_ref[:, a * LANES:(a + 1) * LANES]
            halves = []
            for b in range(2):
                st = _dot(k_ref[0, b], qa, NT)
                m = _col_reduce(st, jnp.maximum, jnp.max)
                e = jnp.exp(st - m)
                l = _col_reduce(e, jnp.add, jnp.sum)
                lse_ref[0, 2 * a + b:2 * a + b + 1, :] = m + jnp.log(l)
                halves.append(_dot(vtv, e.astype(BF16), NN) * (1.0 / l))
            o_ref[:, a * LANES:(a + 1) * LANES] = jnp.concatenate(halves, axis=0).T.astype(BF16)

    return pl.pallas_call(
        body, name=name, grid=(2, T // tq),
        out_shape=[jax.ShapeDtypeStruct(ao.shape, BF16), jax.ShapeDtypeStruct((2, 4, T), F32)],
        in_specs=[pl.BlockSpec((tq, group), lambda j, i: (i + off, j)),
                  pl.BlockSpec((1, 2, TT, LANES), lambda j, i: (j, 0, 0, 0)),
                  pl.BlockSpec((1, HEAD_DIM, TT), lambda j, i: (j, 0, 0)),
                  pl.BlockSpec(memory_space=pl.ANY)],
        out_specs=[pl.BlockSpec((tq, group), lambda j, i: (i, j)),
                   pl.BlockSpec((1, 4, tq), lambda j, i: (j, 0, i))],
        input_output_aliases={3: 0},
        compiler_params=pltpu.CompilerParams(dimension_semantics=("parallel", "parallel")),
    )(q, kpad, vt, ao)


def _attn_bwd(q, dao, ao, lse, kpad, vpad, kt, ctx_rows, name, tq=128):
    TT = q.shape[0]
    T = TT - ctx_rows
    tq = _tile(T, tq)
    off = ctx_rows // tq
    group = 2 * LANES

    def body(q_ref, do_ref, o_ref, lse_ref, k_ref, v_ref, kt_ref, dq_ref, f_ref):
        i = pl.program_id(1)

        @pl.when(i == 0)
        def _():
            f_ref[...] = jnp.zeros_like(f_ref)

        ktv = kt_ref[0]
        row = lax.broadcasted_iota(jnp.int32, (SUBLANES, LANES), 0)
        lane = lax.broadcasted_iota(jnp.int32, (SUBLANES, LANES), 1)
        half_ones = (jnp.where(lane < HEAD_DIM, 0, 1) == row).astype(BF16)
        for a in range(2):
            cols = slice(a * LANES, (a + 1) * LANES)
            qa = q_ref[:, cols]
            do32 = do_ref[:, cols]
            doa = do32.astype(BF16)
            hi, lo = _split_bf16(do32 * o_ref[:, cols].astype(F32))
            deltas = _dot(half_ones, hi, NT) + _dot(half_ones, lo, NT)
            halves = []
            for b in range(2):
                h = 2 * a + b
                st = _dot(k_ref[0, b], qa, NT)
                pt = jnp.exp(st - lse_ref[0, h:h + 1, :])
                dpt = _dot(v_ref[0, b], doa, NT)
                dst = (pt * (dpt - deltas[b:b + 1, :])).astype(BF16)
                f_ref[0, b] += _dot(dst, qa, NN)
                f_ref[0, 2 + b] += _dot(pt.astype(BF16), doa, NN)
                halves.append(_dot(ktv, dst, NN))
            dq_ref[:, cols] = jnp.concatenate(halves, axis=0).T

    kv_spec = pl.BlockSpec((1, 2, TT, LANES), lambda j, i: (j, 0, 0, 0))
    out_cols = pl.BlockSpec((tq, group), lambda j, i: (i, j))
    return pl.pallas_call(
        body, name=name, grid=(2, T // tq),
        out_shape=[jax.ShapeDtypeStruct((T, ATTN_WIDTH), F32), jax.ShapeDtypeStruct((2, 4, TT, LANES), F32)],
        in_specs=[pl.BlockSpec((tq, group), lambda j, i: (i + off, j)), out_cols, out_cols,
                  pl.BlockSpec((1, 4, tq), lambda j, i: (j, 0, i)),
                  kv_spec, kv_spec, pl.BlockSpec((1, HEAD_DIM, TT), lambda j, i: (j, 0, 0))],
        out_specs=[out_cols, pl.BlockSpec((1, 4, TT, LANES), lambda j, i: (j, 0, 0, 0))],
        compiler_params=pltpu.CompilerParams(dimension_semantics=("parallel", "arbitrary")),
    )(q, dao, ao, lse, kpad, vpad, kt)


def _final_fwd_bwd(h, g, target, name):
    R, Dm = h.shape
    tm = _tile(R, ROW_BLOCK, 8)

    def body(h_ref, g_ref, t_ref, dh_ref, loss_ref, dg_ref):
        i = pl.program_id(0)

        @pl.when(i == 0)
        def _():
            loss_ref[...] = jnp.zeros_like(loss_ref)
            dg_ref[...] = jnp.zeros_like(dg_ref)

        hv = h_ref[...]
        r = lax.rsqrt(jnp.mean(hv * hv, axis=-1, keepdims=True) + EPS)
        n = hv * r
        diff = n * g_ref[...] - t_ref[...]
        loss_ref[...] += jnp.sum(diff * diff)
        dout = diff * (1.0 / Dm)
        dg_ref[...] += jnp.sum(dout * n, axis=0, keepdims=True)
        dn = dout * g_ref[...]
        dh_ref[...] = r * (dn - n * jnp.mean(dn * n, axis=-1, keepdims=True))

    return pl.pallas_call(
        body, name=name, grid=(R // tm,),
        out_shape=[jax.ShapeDtypeStruct((R, Dm), F32), jax.ShapeDtypeStruct((1, LANES), F32),
                   jax.ShapeDtypeStruct((1, Dm), F32)],
        in_specs=[_row_spec(tm, Dm), _vec_spec(Dm), _row_spec(tm, Dm)],
        out_specs=[_row_spec(tm, Dm), _vec_spec(LANES), _vec_spec(Dm)],
        compiler_params=pltpu.CompilerParams(dimension_semantics=("arbitrary",)),
    )(h, g, target)


MOD_ROWS = 16


def _mod_fwd(c_rows, w_mod, name):
    L, Dm, n = w_mod.shape

    def body(c_ref, w_ref, o_ref):
        o_ref[0] = _dot3(_silu(c_ref[...]), w_ref[0], NN)

    return pl.pallas_call(
        body, name=name, grid=(L,),
        out_shape=jax.ShapeDtypeStruct((L, MOD_ROWS, n), F32),
        in_specs=[pl.BlockSpec((MOD_ROWS, Dm), lambda l: (0, 0)), pl.BlockSpec((1, Dm, n), lambda l: (l, 0, 0))],
        out_specs=pl.BlockSpec((1, MOD_ROWS, n), lambda l: (l, 0, 0)),
        compiler_params=pltpu.CompilerParams(dimension_semantics=("parallel",)),
    )(c_rows, w_mod)


def _mod_bwd(c_rows_t, dmod, w_mod, name):
    L, Dm, n = w_mod.shape

    def body(ct_ref, d_ref, w_ref, gw_ref, ds_ref):
        dm = d_ref[0]
        gw_ref[0] = _dot3(_silu(ct_ref[...]), dm, NN)
        ds_ref[0] = _dot3(dm[:MOD_ROWS], w_ref[0], NT)

    return pl.pallas_call(
        body, name=name, grid=(L,),
        out_shape=[jax.ShapeDtypeStruct((L, Dm, n), F32), jax.ShapeDtypeStruct((L, MOD_ROWS, Dm), F32)],
        in_specs=[pl.BlockSpec((Dm, LANES), lambda l: (0, 0)), pl.BlockSpec((1, LANES, n), lambda l: (l, 0, 0)),
                  pl.BlockSpec((1, Dm, n), lambda l: (l, 0, 0))],
        out_specs=[pl.BlockSpec((1, Dm, n), lambda l: (l, 0, 0)),
                   pl.BlockSpec((1, MOD_ROWS, Dm), lambda l: (l, 0, 0))],
        compiler_params=pltpu.CompilerParams(dimension_semantics=("parallel",)),
    )(c_rows_t, dmod, w_mod)


def _adam_update(w, g, m, v):
    c1 = 1.0 - ADAM_B1 ** ADAM_STEP
    c2 = 1.0 - ADAM_B2 ** ADAM_STEP
    mn = ADAM_B1 * m + (1.0 - ADAM_B1) * g
    vn = ADAM_B2 * v + (1.0 - ADAM_B2) * (g * g)
    return -ADAM_LR * ((mn / c1) / (jnp.sqrt(vn / c2) + ADAM_EPS) + ADAM_WD * w), mn, vn


def _adamw(w, g, m, v, name):
    R, Cw = w.shape
    tm = _tile(R, ROW_BLOCK, 8)

    def body(w_ref, g_ref, m_ref, v_ref, d_ref, mo_ref, vo_ref):
        d_ref[...], mo_ref[...], vo_ref[...] = _adam_update(w_ref[...], g_ref[...], m_ref[...], v_ref[...])

    spec = pl.BlockSpec((tm, Cw), lambda i: (i, 0))
    return pl.pallas_call(
        body, name=name, grid=(R // tm,),
        out_shape=[jax.ShapeDtypeStruct((R, Cw), F32)] * 3,
        in_specs=[spec] * 4, out_specs=[spec] * 3,
        compiler_params=pltpu.CompilerParams(dimension_semantics=("parallel",)),
    )(w, g, m, v)


def _adamw_recv(w, m, v, recvs, name):
    L, R, n = w.shape
    tm = _tile(R, ROW_BLOCK, 8)
    nblk = R // tm
    parts = [r.reshape(N_DEV, R, n) for r in recvs]

    def body(*refs):
        w_ref, m_ref, v_ref = refs[:3]
        part_refs = refs[3:3 + L]
        g_ref, d_ref, mo_ref, vo_ref, gsum = refs[3 + L:]
        l = pl.program_id(0)
        for ll in range(L):
            @pl.when(l == ll)
            def _(ll=ll):
                acc = part_refs[ll][0].astype(F32)
                for s in range(1, N_DEV):
                    acc = acc + part_refs[ll][s].astype(F32)
                gsum[...] = acc
        g = gsum[...]
        g_ref[0] = g
        d_ref[0], mo_ref[0], vo_ref[0] = _adam_update(w_ref[0], g, m_ref[0], v_ref[0])

    def part_spec(ll):
        return pl.BlockSpec((N_DEV, tm, n), lambda l, i: (0, jnp.where(l == ll, i, jnp.where(l < ll, 0, nblk - 1)), 0))

    spec = pl.BlockSpec((1, tm, n), lambda l, i: (l, i, 0))
    return pl.pallas_call(
        body, name=name, grid=(L, nblk),
        out_shape=[jax.ShapeDtypeStruct((L, R, n), F32)] * 4,
        in_specs=[spec] * 3 + [part_spec(ll) for ll in range(L)], out_specs=[spec] * 4,
        scratch_shapes=[pltpu.VMEM((tm, n), F32)],
        compiler_params=pltpu.CompilerParams(dimension_semantics=("parallel", "parallel")),
    )(w, m, v, *parts)


def _pack(parts, row_mult=8):
    flat, offs, pos = [], [], 0
    for t in parts:
        t = t.reshape(-1).astype(F32)
        size = -(-t.shape[0] // LANES) * LANES
        flat.append(jnp.pad(t, (0, size - t.shape[0])))
        offs.append(pos)
        pos += size
    total = -(-pos // (LANES * row_mult)) * (LANES * row_mult)
    if total > pos:
        flat.append(jnp.zeros((total - pos,), F32))
    return jnp.concatenate(flat).reshape(-1, LANES), offs


def _take(buf, off, shape):
    size = math.prod(shape)
    return buf[..., off:off + size].reshape(buf.shape[:-1] + tuple(shape))


def _rope_tables(T, ctx_rows):
    pos = jnp.arange(T)
    row = (pos // GRID_W).astype(F32)
    col = (pos % GRID_W).astype(F32)
    half = HEAD_DIM // 4
    inv = ROPE_THETA ** (-jnp.arange(0, 2 * half, 2, dtype=F32) / (2 * half))
    ang_r, ang_c = row[:, None] * inv[None, :], col[:, None] * inv[None, :]
    cos = jnp.concatenate([jnp.cos(ang_r)] * 2 + [jnp.cos(ang_c)] * 2, axis=1)
    sin = jnp.concatenate([-jnp.sin(ang_r), jnp.sin(ang_r), -jnp.sin(ang_c), jnp.sin(ang_c)], axis=1)
    cos = jnp.concatenate([jnp.ones((ctx_rows, HEAD_DIM), F32), cos], axis=0)
    sin = jnp.concatenate([jnp.zeros((ctx_rows, HEAD_DIM), F32), sin], axis=0)
    return jnp.tile(cos, (1, 2)), jnp.tile(sin, (1, 2))


def kernel(x, c, ctx, c_ctx, w_mod, b_mod, g_mix, g_ffn, w_ffn_in, w_ffn_out, w_in, q_gain, k_gain, w_sp, b_sp, w_out, w_pw1, b_pw1, w_dw, b_dw, ln_g, ln_b, w_pw2, b_pw2, g_final, loss_target, m_c_ctx, m_w_mod, m_b_mod, m_g_mix, m_g_ffn, m_w_ffn_in, m_w_ffn_out, m_w_in, m_q_gain, m_k_gain, m_w_sp, m_b_sp, m_w_out, m_w_pw1, m_b_pw1, m_w_dw, m_b_dw, m_ln_g, m_ln_b, m_w_pw2, m_b_pw2, m_g_final, v_c_ctx, v_w_mod, v_b_mod, v_g_mix, v_g_ffn, v_w_ffn_in, v_w_ffn_out, v_w_in, v_q_gain, v_k_gain, v_w_sp, v_b_sp, v_w_out, v_w_pw1, v_b_pw1, v_w_dw, v_b_dw, v_ln_g, v_ln_b, v_w_pw2, v_b_pw2, v_g_final):
    weights = dict(c_ctx=c_ctx, w_mod=w_mod, b_mod=b_mod, g_mix=g_mix, g_ffn=g_ffn, w_ffn_in=w_ffn_in,
                   w_ffn_out=w_ffn_out, w_in=w_in, q_gain=q_gain, k_gain=k_gain, w_sp=w_sp, b_sp=b_sp,
                   w_out=w_out, w_pw1=w_pw1, b_pw1=b_pw1, w_dw=w_dw, b_dw=b_dw, ln_g=ln_g, ln_b=ln_b,
                   w_pw2=w_pw2, b_pw2=b_pw2, g_final=g_final)
    moments_m = dict(c_ctx=m_c_ctx, w_mod=m_w_mod, b_mod=m_b_mod, g_mix=m_g_mix, g_ffn=m_g_ffn,
                     w_ffn_in=m_w_ffn_in, w_ffn_out=m_w_ffn_out, w_in=m_w_in, q_gain=m_q_gain,
                     k_gain=m_k_gain, w_sp=m_w_sp, b_sp=m_b_sp, w_out=m_w_out, w_pw1=m_w_pw1,
                     b_pw1=m_b_pw1, w_dw=m_w_dw, b_dw=m_b_dw, ln_g=m_ln_g, ln_b=m_ln_b, w_pw2=m_w_pw2,
                     b_pw2=m_b_pw2, g_final=m_g_final)
    moments_v = dict(c_ctx=v_c_ctx, w_mod=v_w_mod, b_mod=v_b_mod, g_mix=v_g_mix, g_ffn=v_g_ffn,
                     w_ffn_in=v_w_ffn_in, w_ffn_out=v_w_ffn_out, w_in=v_w_in, q_gain=v_q_gain,
                     k_gain=v_k_gain, w_sp=v_w_sp, b_sp=v_b_sp, w_out=v_w_out, w_pw1=v_w_pw1,
                     b_pw1=v_b_pw1, w_dw=v_w_dw, b_dw=v_b_dw, ln_g=v_ln_g, ln_b=v_ln_b, w_pw2=v_w_pw2,
                     b_pw2=v_b_pw2, g_final=v_g_final)
    names = list(weights)

    T, C = x.shape[1], ctx.shape[1]
    Dm = D_MODEL
    me = 4 * lax.axis_index("x") + 2 * lax.axis_index("y") + lax.axis_index("c")
    h0 = x[0]
    ctx2 = ctx[0]
    target = loss_target[0]

    small_sharded = (("w_dw", w_dw[0]), ("b_pw1", b_pw1), ("b_dw", b_dw), ("ln_g", ln_g), ("ln_b", ln_b),
                     ("b_pw2", b_pw2))
    buf1, offs1 = _pack([c] + [t for _, t in small_sharded])
    got1, W_in, W_out = _all_gather([buf1, w_in[0].astype(BF16), w_out[0].astype(BF16)], "gather_cond", False)
    got1 = got1.reshape(N_DEV, -1)
    c_all = _take(got1, offs1[0], (Dm,))
    full_small = {}
    for (nm, t), off in zip(small_sharded, offs1[1:]):
        seg = _take(got1, off, t.shape)
        full_small[nm] = jnp.moveaxis(seg, 0, -2).reshape(t.shape[:-1] + (N_DEV * t.shape[-1],))
    w_dw_f, b_pw1_f = full_small["w_dw"], full_small["b_pw1"]
    b_dw_f, ln_g_f, ln_b_f, b_pw2_f = (full_small[k] for k in ("b_dw", "ln_g", "ln_b", "b_pw2"))

    c_rows = jnp.concatenate([c_all, c_ctx[None, :], jnp.zeros((MOD_ROWS - N_DEV - 1, Dm), F32)], axis=0)
    mod_part = _mod_fwd(c_rows, w_mod, "mod_fwd")
    n_mod = w_mod.shape[2]
    got2 = _all_gather([mod_part.reshape(-1, LANES)], "gather_mod", True)[0]
    mod_all = got2.reshape(N_DEV, 2, MOD_ROWS, n_mod).transpose(1, 2, 0, 3).reshape(2, MOD_ROWS, N_DEV * n_mod)
    mod_all = mod_all + b_mod[:, None, :]
    my_mod = lax.dynamic_index_in_dim(mod_all, me, axis=1, keepdims=False)
    sh1, sc1, gt1, sh2, sc2, gt2 = ([my_mod[l:l + 1, k * Dm:(k + 1) * Dm] for l in range(2)] for k in range(6))
    csh1 = mod_all[0, N_DEV:N_DEV + 1, 0:Dm]
    csc1 = mod_all[0, N_DEV:N_DEV + 1, Dm:2 * Dm]

    behind = got2[0:1, 0:1] * 0.0
    gather_groups = [[w_ffn_in[0], w_ffn_out[0]], [w_pw1[0], w_pw2[0]], [w_ffn_in[1], w_ffn_out[1]]]
    gathers = [_push_begin([(t + behind).astype(BF16) for t in grp], True, f"gather_start{k}")
               for k, grp in enumerate(gather_groups)]
    started = sum(h[4][0:1, 0:1] for h in gathers)

    def gathered(k, after):
        own, lands = _push_end(gathers[k], after, f"gather_wait{k}")
        return [lax.dynamic_update_slice(land, mine, (me * mine.shape[0], 0)) for land, mine in zip(lands, own)]

    def ffn_weights(k, after):
        wi, wo = gathered(k, after)
        return wi.reshape(N_DEV, Dm, FF_SHARD), wo.reshape(N_DEV // 2, FF_SHARD, Dm)

    def col_gathered(t, n):
        return t.reshape(N_DEV, Dm, n).transpose(1, 0, 2).reshape(Dm, N_DEV * n)

    W_ffi, W_ffo = [None, None], [None, None]

    g_mix_r = [g_mix[l:l + 1] for l in range(2)]
    g_ffn_r = [g_ffn[l:l + 1] for l in range(2)]
    g_fin = g_final[None, :]

    cos, sin = _rope_tables(T, C)
    qg = jnp.tile(q_gain, (1, 2))
    kg = jnp.tile(k_gain, (1, 2))
    lane_head = jnp.arange(LANES) // HEAD_DIM
    bd = (lane_head[:, None] == lane_head[None, :]).astype(BF16)
    w_sp0 = w_sp[0]
    w_spt0 = w_sp0.transpose(0, 2, 1)
    b_spt0 = b_sp[0].T

    XM = _norm_mod_fwd_cat(ctx2, h0, g_mix_r[0], csc1, csh1, sc1[0] + started, sh1[0], "norm_mix0")
    W_in = col_gathered(W_in, IN_WIDTH // N_DEV)
    P = _mm(XM, W_in, "nn", "in_proj", tn=896)
    qh, kpad, vpad, kt, vt, ao = _mix_prep_fwd(P, C, cos, sin, qg, kg, bd, w_sp0, b_spt0, "mix_prep")
    ao, lse = _attn_fwd(qh, kpad, vt, ao, C, "attn_fwd")
    h1, y0 = _mm(ao, W_out, "nn", "out_proj", res=h0, gate=gt1[0], raw_out=True)

    def ffn_fwd(h_in, l):
        xf = _norm_mod_fwd(h_in, g_ffn_r[l], sc2[l], sh2[l], f"norm_ffn{l}")
        W_ffi[l], W_ffo[l] = ffn_weights(2 * l, xf)
        gu, act = _ffn_in_swiglu(xf, W_ffi[l], f"ffn_in{l}")
        h_out, f = _mm_over_shards(act, W_ffo[l], "nn", f"ffn_out{l}", res=h_in, gate=gt2[l], raw_out=True)
        return h_out, (xf, gu, act, f)

    h2, saved_ffn0 = ffn_fwd(h1, 0)

    xm1 = _norm_mod_fwd(h2, g_mix_r[1], sc1[1], sh1[1], "norm_mix1")
    W_pw1, W_pw2 = gathered(1, xm1)
    W_pw1 = col_gathered(W_pw1, 2 * Dm // N_DEV)
    ag = _mm(xm1, W_pw1, "nn", "pw1", bias=b_pw1_f)
    hg = _glu_fwd(ag, "glu")
    hd = _conv_fwd(hg, w_dw_f, b_dw_f, "conv")
    hs = _ln_silu_fwd(hd, ln_g_f, ln_b_f, "ln_silu")
    h3, y1 = _mm(hs, W_pw2, "nn", "pw2", bias=b_pw2_f, res=h2, gate=gt1[1], raw_out=True)
    h4, saved_ffn1 = ffn_fwd(h3, 1)

    dh4, sq_err, dg_final = _final_fwd_bwd(h4, g_fin, target, "loss_head")
    loss_local = (0.5 / Dm) * sq_err[0, 0:1]

    def col_shards(g, n):
        return g.reshape(Dm, N_DEV, n).transpose(1, 0, 2).reshape(N_DEV * Dm, n)

    def exchange_begin(k, parts):
        return _push_begin(parts, False, f"exchange_start{k}")

    def zero_of(handle):
        return handle[4][0:1, 0:1]

    def ffn_bwd(dh_out, h_in, saved, l, zero):
        xf, gu, act, f = saved
        df, dgt, _ = _gate_bwd(dh_out, f, gt2[l] + zero, f"gate_ffn_bwd{l}")
        dw_out = _mm_tn_shard_rows(act, df, f"ffn_out_dw{l}", BF16)
        dgu = _ffn_out_dx_swiglu(df, W_ffo[l], gu, f"ffn_out_dx{l}").reshape(N_DEV, T, FF_SHARD)
        dw_in = _mm_tn_shard_cols(xf, dgu, f"ffn_in_dw{l}", BF16)
        dxf = _mm_over_shards(dgu, W_ffi[l], "nt", f"ffn_in_dx{l}", tm=1024)
        dh_in, da, dsh = _norm_mod_bwd(h_in, g_ffn_r[l], sc2[l], dxf, dh_out, f"norm_ffn_bwd{l}")
        return dh_in, dw_in, dw_out, (dsh, da * g_ffn_r[l], dgt), da * (1.0 + sc2[l])

    dh3, dW_ffi1, dW_ffo1, dmod_ffn1, dg_ffn1 = ffn_bwd(dh4, h3, saved_ffn1, 1, 0.0)
    ex0 = exchange_begin(0, [dW_ffi1.reshape(N_DEV * Dm, FF_SHARD), dW_ffo1.reshape(D_FF, Dm)])

    dy1, dgt1_1, db_pw2 = _gate_bwd(dh3, y1, gt1[1] + zero_of(ex0), "gate_conv_bwd")
    dW_pw2 = _mm(hs, dy1, "tn", "pw2_dw", BF16, tk=512)
    dhs = _mm(dy1, W_pw2, "nt", "pw2_dx")
    dhd, dln_g, dln_b, db_dw = _ln_silu_bwd(dhs, hd, ln_g_f, ln_b_f, "ln_silu_bwd")
    dhg, dw_dw = _conv_bwd(dhd, hg, w_dw_f, "conv_bwd")
    dag, db_pw1 = _glu_bwd(ag, dhg, "glu_bwd")
    dW_pw1 = _mm(xm1, dag, "tn", "pw1_dw", BF16, tk=512)
    dxm1 = _mm(dag, W_pw1, "nt", "pw1_dx")
    dh2, da, dsh = _norm_mod_bwd(h2, g_mix_r[1], sc1[1], dxm1, dh3, "norm_mix1_bwd")
    dmod_mix1 = (dsh, da * g_mix_r[1], dgt1_1)
    dg_mix1 = da * (1.0 + sc1[1])

    ex1 = exchange_begin(1, [col_shards(dW_pw1, 2 * Dm // N_DEV), dW_pw2])
    dh1, dW_ffi0, dW_ffo0, dmod_ffn0, dg_ffn0 = ffn_bwd(dh2, h1, saved_ffn0, 0, zero_of(ex1))
    ex2 = exchange_begin(2, [dW_ffi0.reshape(N_DEV * Dm, FF_SHARD), dW_ffo0.reshape(D_FF, Dm)])

    dy0, dgt1_0, _ = _gate_bwd(dh1, y0, gt1[0] + zero_of(ex2), "gate_mix_bwd")
    dW_out = _mm(ao, dy0, "tn", "out_proj_dw", BF16, tk=512)
    dao = _mm(dy0, W_out, "nt", "out_proj_dx")
    dq, f_acc = _attn_bwd(qh, dao, ao, lse, kpad, vpad, kt, C, "attn_bwd")
    dP, dqg, dkg, dw_sp0, db_spt0 = _mix_prep_bwd(P, dq, f_acc, dao, C, cos, sin, qg, kg, bd, w_sp0, w_spt0,
                                                  b_spt0, "mix_prep_bwd")
    dW_in = _mm(XM, dP, "tn", "in_proj_dw", BF16, tn=896, tk=512)
    ex3 = exchange_begin(3, [col_shards(dW_in, IN_WIDTH // N_DEV), dW_out])
    dXM = _mm(dP, W_in, "nt", "in_proj_dx", tk=896)
    dh0, da, dsh = _norm_mod_bwd(h0, g_mix_r[0], sc1[0] + zero_of(ex3), dXM, dh1, "norm_mix0_bwd", dxm_row_off=C)
    _, dac, dcsh = _norm_mod_bwd(ctx2, g_mix_r[0], csc1, dXM, None, "norm_ctx_bwd")
    dmod_mix0 = (dsh, da * g_mix_r[0], dgt1_0)
    dg_mix0 = da * (1.0 + sc1[0]) + dac * (1.0 + csc1)
    dcmod = jnp.concatenate([dcsh, dac * g_mix_r[0]], axis=1)

    dmod_mine = jnp.stack([jnp.concatenate(dmod_mix0 + dmod_ffn0, axis=1)[0],
                           jnp.concatenate(dmod_mix1 + dmod_ffn1, axis=1)[0]])

    small_grads = [
        ("loss", loss_local), ("g_final", dg_final), ("g_mix", jnp.concatenate([dg_mix0, dg_mix1])),
        ("g_ffn", jnp.concatenate([dg_ffn0, dg_ffn1])),
        ("q_gain", dqg[:, :HEAD_DIM] + dqg[:, HEAD_DIM:]), ("k_gain", dkg[:, :HEAD_DIM] + dkg[:, HEAD_DIM:]),
        ("w_sp", dw_sp0[None]), ("b_sp", db_spt0.T[None]), ("b_pw1", db_pw1), ("w_dw", dw_dw[None]),
        ("b_dw", db_dw), ("ln_g", dln_g), ("ln_b", dln_b), ("b_pw2", db_pw2), ("dcmod", dcmod),
        ("dmod", dmod_mine),
    ]
    buf3, offs3 = _pack([t for _, t in small_grads])
    got3 = _all_gather([buf3], "gather_small_grads", True)[0].reshape(N_DEV, buf3.shape[0], LANES)
    sum3 = _sum_devices(got3, "sum_small_grads").reshape(-1)
    off3 = {nm: off for (nm, _), off in zip(small_grads, offs3)}
    shape3 = {nm: t.shape for nm, t in small_grads}

    def summed(nm):
        return _take(sum3, off3[nm], shape3[nm])

    loss = summed("loss")[0]
    dcmod_sum = summed("dcmod")
    dmod_rows = _take(got3.reshape(N_DEV, -1), off3["dmod"], (2, 6 * Dm)).transpose(1, 0, 2)
    ctx_row = jnp.concatenate([jnp.pad(dcmod_sum, ((0, 0), (0, 4 * Dm))), jnp.zeros((1, 6 * Dm), F32)])
    dmod_all = jnp.concatenate([dmod_rows, ctx_row[:, None, :],
                                jnp.zeros((2, LANES - N_DEV - 1, 6 * Dm), F32)], axis=1)
    grads = {}
    grads["b_mod"] = summed("dmod") + ctx_row
    dmod_shard = lax.dynamic_slice_in_dim(dmod_all, me * n_mod, n_mod, axis=2)
    c_rows_t = jnp.pad(c_rows.T, ((0, 0), (0, LANES - MOD_ROWS)))
    grads["w_mod"], ds_part = _mod_bwd(c_rows_t, dmod_shard, w_mod, "mod_bwd")

    buf4, _ = _pack([ds_part[0, N_DEV]])
    got4 = _all_gather([buf4], "gather_c_ctx_grad", True)[0].reshape(N_DEV, buf4.shape[0], LANES)
    ds_ctx = _sum_devices(got4, "sum_c_ctx_grad").reshape(-1)[:Dm]
    grads["c_ctx"] = ds_ctx * _dsilu(c_ctx)

    for nm in ("g_final", "g_mix", "g_ffn", "q_gain", "k_gain", "w_sp", "b_sp"):
        grads[nm] = summed(nm).reshape(weights[nm].shape)
    for nm in ("b_pw1", "w_dw", "b_dw", "ln_g", "ln_b", "b_pw2"):
        n_loc = weights[nm].shape[-1]
        grads[nm] = lax.dynamic_slice_in_dim(summed(nm), me * n_loc, n_loc, axis=-1).reshape(weights[nm].shape)

    delta, new_m, new_v = {}, {}, {}
    shp = w_mod.shape
    outs = _adamw(w_mod.reshape(-1, shp[-1]), grads["w_mod"].reshape(-1, shp[-1]),
                  m_w_mod.reshape(-1, shp[-1]), v_w_mod.reshape(-1, shp[-1]), "adamw_w_mod")
    delta["w_mod"], new_m["w_mod"], new_v["w_mod"] = (o.reshape(shp) for o in outs)
    big_names = ("w_mod", "w_ffn_in", "w_ffn_out", "w_in", "w_out", "w_pw1", "w_pw2")
    small_names = [nm for nm in names if nm not in big_names]
    packs = [_pack([src[nm] for nm in small_names]) for src in (weights, grads, moments_m, moments_v)]
    offs_s = packs[0][1]
    outs = _adamw(*[pk[0] for pk in packs], "adamw_small")
    for o, dst in zip(outs, (delta, new_m, new_v)):
        o = o.reshape(-1)
        for nm, off in zip(small_names, offs_s):
            dst[nm] = _take(o, off, weights[nm].shape)

    def exchanged(k, handle):
        sent, lands = _push_end(handle, outs[0], f"exchange_wait{k}")
        done = []
        for land, mine in zip(lands, sent):
            m_per = land.shape[0] // N_DEV
            own = lax.dynamic_slice_in_dim(mine, me * m_per, m_per, axis=0)
            done.append(lax.dynamic_update_slice(land, own, (me * m_per, 0)))
        return done

    r_ffi1, r_ffo1 = exchanged(0, ex0)
    r_pw1, r_pw2 = exchanged(1, ex1)
    r_ffi0, r_ffo0 = exchanged(2, ex2)
    r_in, r_out = exchanged(3, ex3)
    for nm, parts in (("w_ffn_in", [r_ffi0, r_ffi1]), ("w_ffn_out", [r_ffo0, r_ffo1]), ("w_pw1", [r_pw1]),
                      ("w_pw2", [r_pw2]), ("w_in", [r_in]), ("w_out", [r_out])):
        grads[nm], delta[nm], new_m[nm], new_v[nm] = _adamw_recv(
            weights[nm], moments_m[nm], moments_v[nm], parts, f"adamw_{nm}")

    return (loss, dh0[None], *[grads[n] for n in names], *[delta[n] for n in names],
            *[new_m[n] for n in names], *[new_v[n] for n in names])
```

```python
import math

import jax
import jax.numpy as jnp
from jax import lax
from jax.experimental import pallas as pl
from jax.experimental.pallas import tpu as pltpu

F32 = jnp.float32
BF16 = jnp.bfloat16
MESH = pl.DeviceIdType.MESH

N_DEV = 8
D_MODEL = 1024
EPS = 1e-6
HEAD_DIM = 64
ATTN_WIDTH = 512
KV_WIDTH = 128
SG_WIDTH = 512
N_SG_GROUPS = 4
CHUNK = 128
IN_WIDTH = 1792
D_FF = 2816
FF_SHARD = 2 * D_FF // N_DEV
CONV_WIDTH = 31
CONV_HALO = 16
GRID_W = 64
ROPE_THETA = 10000.0
LANES = 128
SUBLANES = 8
ROW_BLOCK = 256
ADAM_LR, ADAM_B1, ADAM_B2, ADAM_EPS, ADAM_WD, ADAM_STEP = 0.001, 0.9, 0.999, 1e-08, 0.01, 10


def _tile(n, target, mult=LANES):
    best = None
    for t in range(mult, min(n, target) + 1, mult):
        if n % t == 0:
            best = t
    return best if best is not None else n


def _sigmoid(x):
    return 1.0 / (1.0 + jnp.exp(-x))


def _silu(x):
    return x * _sigmoid(x)


def _dsilu(x):
    s = _sigmoid(x)
    return s * (1.0 + x * (1.0 - s))


_GELU_K = math.sqrt(2.0 / math.pi)


def _gelu(x):
    return 0.5 * x * (1.0 + jnp.tanh(_GELU_K * (x + 0.044715 * x * x * x)))


def _dgelu(x):
    t = jnp.tanh(_GELU_K * (x + 0.044715 * x * x * x))
    return 0.5 * (1.0 + t) + 0.5 * x * (1.0 - t * t) * _GELU_K * (1.0 + 3.0 * 0.044715 * x * x)


def _split_bf16(x):
    hi = x.astype(BF16)
    lo = (x - hi.astype(F32)).astype(BF16)
    return hi, lo


def _dot(a, b, dims):
    return lax.dot_general(a, b, (dims, ((), ())), preferred_element_type=F32)


def _dot3(a, b, dims):
    ah, al = _split_bf16(a)
    bh, bl = _split_bf16(b)
    return _dot(ah, bh, dims) + _dot(ah, bl, dims) + _dot(al, bh, dims)


NN = ((1,), (0,))
NT = ((1,), (1,))
TN = ((0,), (0,))


def _all_gather(xs, name, in_vmem):
    n_arr = len(xs)

    def body(*refs):
        x_refs, out_refs = refs[:n_arr], refs[n_arr:2 * n_arr]
        send_sems, recv_sems, local_sems = refs[2 * n_arr:]
        x, y, c = lax.axis_index("x"), lax.axis_index("y"), lax.axis_index("c")
        me, sibling = (x, y, c), (x, y, 1 - c)
        chips = [(1 - x, y), (x, 1 - y), (1 - x, 1 - y)]

        def rows(a, px, py, pc):
            m_per = xs[a].shape[0]
            return out_refs[a].at[pl.ds((4 * px + 2 * py + pc) * m_per, m_per), :]

        def copy(a, k, block, to, src=None):
            return pltpu.make_async_remote_copy(
                src_ref=rows(a, *block) if src is None else src,
                dst_ref=rows(a, *block),
                send_sem=send_sems.at[7 * a + k],
                recv_sem=recv_sems.at[7 * a + k],
                device_id=to,
                device_id_type=MESH,
            )

        mine, first, passed = [], [], []
        for a in range(n_arr):
            mine.append(pltpu.make_async_copy(x_refs[a], rows(a, *me), local_sems.at[a]))
            mine[-1].start()
            first.append(copy(a, 0, me, sibling, src=x_refs[a]))
            first += [copy(a, 1 + j, me, (*chip, c), src=x_refs[a]) for j, chip in enumerate(chips)]
        for cp in first:
            cp.start()
        for a in range(n_arr):
            for j, chip in enumerate(chips):
                copy(a, 1 + j, (*chip, c), me).wait_recv()
                passed.append(copy(a, 4 + j, (*chip, c), sibling))
                passed[-1].start()
        for a in range(n_arr):
            copy(a, 0, sibling, me).wait_recv()
            for j, chip in enumerate(chips):
                copy(a, 4 + j, (*chip, 1 - c), me).wait_recv()
        for cp in first + passed:
            cp.wait_send()
        for cp in mine:
            cp.wait()

    space = pltpu.VMEM if in_vmem else pl.ANY
    return pl.pallas_call(
        body,
        name=name,
        out_shape=[jax.ShapeDtypeStruct((N_DEV * t.shape[0], t.shape[1]), t.dtype) for t in xs],
        in_specs=[pl.BlockSpec(memory_space=space)] * n_arr,
        out_specs=[pl.BlockSpec(memory_space=space)] * n_arr,
        scratch_shapes=[
            pltpu.SemaphoreType.DMA((7 * n_arr,)),
            pltpu.SemaphoreType.DMA((7 * n_arr,)),
            pltpu.SemaphoreType.DMA((n_arr,)),
        ],
    )(*xs)


def _shard_exchange(gs, name):
    n_arr = len(gs)

    def body(*refs):
        g_refs, r_refs = refs[:n_arr], refs[n_arr:2 * n_arr]
        send_sems, recv_sems, local_sems = refs[2 * n_arr:]
        x, y, c = lax.axis_index("x"), lax.axis_index("y"), lax.axis_index("c")
        me = 4 * x + 2 * y + c

        def rows(ref, a, idx):
            m_per = gs[a].shape[0] // N_DEV
            return ref.at[pl.ds(idx * m_per, m_per), :]

        mine, sends, recvs = [], [], []
        for a in range(n_arr):
            mine.append(pltpu.make_async_copy(rows(g_refs[a], a, me), rows(r_refs[a], a, me), local_sems.at[a]))
            mine[-1].start()
            for k in range(1, N_DEV):
                px = 1 - x if (k >> 2) & 1 else x
                py = 1 - y if (k >> 1) & 1 else y
                pc = 1 - c if k & 1 else c
                peer = 4 * px + 2 * py + pc
                sem = 7 * a + k - 1
                sends.append(pltpu.make_async_remote_copy(
                    src_ref=rows(g_refs[a], a, peer), dst_ref=rows(r_refs[a], a, me),
                    send_sem=send_sems.at[sem], recv_sem=recv_sems.at[sem],
                    device_id=(px, py, pc), device_id_type=MESH))
                recvs.append(pltpu.make_async_remote_copy(
                    src_ref=rows(g_refs[a], a, me), dst_ref=rows(r_refs[a], a, peer),
                    send_sem=send_sems.at[sem], recv_sem=recv_sems.at[sem],
                    device_id=(px, py, pc), device_id_type=MESH))
        for cp in sends:
            cp.start()
        for cp in recvs:
            cp.wait_recv()
        for cp in sends:
            cp.wait_send()
        for cp in mine:
            cp.wait()

    return pl.pallas_call(
        body,
        name=name,
        out_shape=[jax.ShapeDtypeStruct(t.shape, t.dtype) for t in gs],
        in_specs=[pl.BlockSpec(memory_space=pl.ANY)] * n_arr,
        out_specs=[pl.BlockSpec(memory_space=pl.ANY)] * n_arr,
        scratch_shapes=[
            pltpu.SemaphoreType.DMA((7 * n_arr,)),
            pltpu.SemaphoreType.DMA((7 * n_arr,)),
            pltpu.SemaphoreType.DMA((n_arr,)),
        ],
    )(*gs)


HBM_SPEC = pl.BlockSpec(memory_space=pltpu.HBM)
SEM_SPEC = pl.BlockSpec(memory_space=pltpu.SEMAPHORE)
DATAFLOW_EFFECT = pltpu.SideEffectType.DATAFLOW_SIDE_EFFECTING


def _peers(x, y, c):
    for k in range(1, N_DEV):
        px = 1 - x if (k >> 2) & 1 else x
        py = 1 - y if (k >> 1) & 1 else y
        pc = 1 - c if k & 1 else c
        yield k - 1, (px, py, pc), 4 * px + 2 * py + pc


def _push_copies(src_refs, land_refs, send_sems, recv_sems, shapes, whole_src):
    x, y, c = lax.axis_index("x"), lax.axis_index("y"), lax.axis_index("c")
    me = 4 * x + 2 * y + c
    for a, (m_per, _) in enumerate(shapes):
        def block(ref, idx, m_per=m_per):
            return ref.at[pl.ds(idx * m_per, m_per), :]

        for k, peer, pidx in _peers(x, y, c):
            src = src_refs[a] if whole_src else block(src_refs[a], pidx)
            sems = dict(send_sem=send_sems.at[7 * a + k], recv_sem=recv_sems.at[7 * a + k],
                        device_id=peer, device_id_type=MESH)
            yield (pltpu.make_async_remote_copy(src_ref=src, dst_ref=block(land_refs[a], me), **sems),
                   pltpu.make_async_remote_copy(src_ref=src, dst_ref=block(land_refs[a], pidx), **sems))


def _push_begin(srcs, whole_src, name):
    n_arr = len(srcs)
    shapes = [(t.shape[0] if whole_src else t.shape[0] // N_DEV, t.shape[1]) for t in srcs]
    lands = [lax.empty((N_DEV * m, n), t.dtype) for (m, n), t in zip(shapes, srcs)]

    def body(*refs):
        src_refs, land_refs = refs[:n_arr], refs[n_arr:2 * n_arr]
        send_sems, recv_sems = refs[2 * n_arr], refs[2 * n_arr + 1]
        token = refs[-1]
        for outgoing, _ in _push_copies(src_refs, land_refs, send_sems, recv_sems, shapes, whole_src):
            outgoing.start()
        token[...] = jnp.zeros_like(token)

    operands = [pltpu.with_memory_space_constraint(t, pltpu.HBM) for t in list(srcs) + lands]
    outs = pl.pallas_call(
        body, name=name,
        out_shape=(pltpu.SemaphoreType.DMA((7 * n_arr,)), pltpu.SemaphoreType.DMA((7 * n_arr,)),
                   *[pltpu.HBM(t.shape, t.dtype) for t in operands],
                   jax.ShapeDtypeStruct((SUBLANES, LANES), F32)),
        in_specs=[HBM_SPEC] * (2 * n_arr),
        out_specs=(SEM_SPEC, SEM_SPEC, *[HBM_SPEC] * (2 * n_arr), pl.BlockSpec(memory_space=pltpu.VMEM)),
        input_output_aliases={i: 2 + i for i in range(2 * n_arr)},
        compiler_params=pltpu.CompilerParams(has_side_effects=DATAFLOW_EFFECT),
    )(*operands)
    return outs[0], outs[1], list(outs[2:2 + n_arr]), list(outs[2 + n_arr:2 + 2 * n_arr]), outs[-1], whole_src


def _push_end(handle, after, name):
    send_sems, recv_sems, srcs, lands, _, whole_src = handle
    n_arr = len(srcs)
    shapes = [(t.shape[0] // N_DEV, t.shape[1]) for t in lands]

    def body(*refs):
        src_refs, land_refs = refs[:n_arr], refs[n_arr:2 * n_arr]
        send_sems_ref, recv_sems_ref = refs[2 * n_arr], refs[2 * n_arr + 1]
        local_sems = refs[-1]
        me = 4 * lax.axis_index("x") + 2 * lax.axis_index("y") + lax.axis_index("c")
        own = []
        for a, (m_per, _) in enumerate(shapes):
            mine = pl.ds(me * m_per, m_per)
            src = src_refs[a] if whole_src else src_refs[a].at[mine, :]
            own.append(pltpu.make_async_copy(src, land_refs[a].at[mine, :], local_sems.at[a]))
            own[-1].start()
        for outgoing, incoming in _push_copies(src_refs, land_refs, send_sems_ref, recv_sems_ref, shapes, whole_src):
            outgoing.wait_send()
            incoming.wait_recv()
        for cp in own:
            cp.wait()

    outs = pl.pallas_call(
        body, name=name,
        out_shape=tuple(pltpu.HBM(t.shape, t.dtype) for t in srcs + lands),
        in_specs=[HBM_SPEC] * (2 * n_arr) + [SEM_SPEC, SEM_SPEC, pl.BlockSpec(memory_space=pl.ANY)],
        out_specs=tuple([HBM_SPEC] * (2 * n_arr)),
        input_output_aliases={i: i for i in range(2 * n_arr)},
        scratch_shapes=[pltpu.SemaphoreType.DMA((n_arr,))],
        compiler_params=pltpu.CompilerParams(has_side_effects=DATAFLOW_EFFECT),
    )(*srcs, *lands, send_sems, recv_sems, after)
    return list(outs[:n_arr]), list(outs[n_arr:])


def _sum_devices(r, name, rows_per_step=ROW_BLOCK):
    _, m, n = r.shape
    tm = _tile(m, rows_per_step, 8)

    def body(r_ref, o_ref):
        acc = r_ref[0].astype(F32)
        for s in range(1, N_DEV):
            acc = acc + r_ref[s].astype(F32)
        o_ref[...] = acc

    return pl.pallas_call(
        body,
        name=name,
        grid=(m // tm,),
        out_shape=jax.ShapeDtypeStruct((m, n), F32),
        in_specs=[pl.BlockSpec((N_DEV, tm, n), lambda i: (0, i, 0))],
        out_specs=pl.BlockSpec((tm, n), lambda i: (i, 0)),
        compiler_params=pltpu.CompilerParams(dimension_semantics=("parallel",)),
    )(r)


def _get(ref):
    return ref[0] if len(ref.shape) == 3 else ref[...]


def _put(ref, val):
    if len(ref.shape) == 3:
        ref[0] = val
    else:
        ref[...] = val


def _mm_call(name, a, b, a_spec, b_spec, out_sds, o_spec, grid, dims, acc_shape, bias=None,
             res=None, gate=None, raw_out=False, vec_spec=None):
    nk = grid[2]
    operands, in_specs = [a, b], [a_spec, b_spec]
    if bias is not None:
        operands.append(bias)
        in_specs.append(vec_spec)
    if res is not None:
        operands += [res, gate]
        in_specs += [o_spec, vec_spec]
    out_shape, out_specs = [out_sds], [o_spec]
    if raw_out:
        out_shape.append(jax.ShapeDtypeStruct(out_sds.shape, F32))
        out_specs.append(o_spec)

    def body(*refs):
        it = iter(refs)
        a_ref, b_ref = next(it), next(it)
        bias_ref = next(it) if bias is not None else None
        res_ref, gate_ref = (next(it), next(it)) if res is not None else (None, None)
        o_ref = next(it)
        raw_ref = next(it) if raw_out else None
        acc = next(it) if nk > 1 else None
        k = pl.program_id(2)
        part = _dot(_get(a_ref).astype(BF16), _get(b_ref).astype(BF16), dims)

        def finish(y):
            if bias_ref is not None:
                y = y + bias_ref[...]
            if raw_ref is not None:
                raw_ref[...] = y
            if res_ref is not None:
                y = res_ref[...] + gate_ref[...] * y
            _put(o_ref, y.astype(out_sds.dtype))

        if nk == 1:
            finish(part)
        else:
            @pl.when(k == 0)
            def _():
                acc[...] = part

            @pl.when(k > 0)
            def _():
                acc[...] += part

            @pl.when(k == nk - 1)
            def _():
                finish(acc[...])

    outs = pl.pallas_call(
        body,
        name=name,
        grid=grid,
        out_shape=out_shape,
        in_specs=in_specs,
        out_specs=out_specs,
        scratch_shapes=[pltpu.VMEM(acc_shape, F32)] if nk > 1 else [],
        compiler_params=pltpu.CompilerParams(dimension_semantics=("parallel", "parallel", "arbitrary")),
    )(*operands)
    return outs if raw_out else outs[0]


def _mm(a, b, mode, name, out_dtype=F32, bias=None, res=None, gate=None, raw_out=False,
        tm=512, tn=1024, tk=1024, a_row_off=0):
    if mode == "nn":
        K, N = b.shape
        M = a.shape[0] - a_row_off
    elif mode == "nt":
        N, K = b.shape
        M = a.shape[0] - a_row_off
    else:
        (K, M), N = a.shape, b.shape[1]
    tm, tn, tk = _tile(M, tm), _tile(N, tn), _tile(K, tk)
    off = a_row_off // tm
    dims = {"nn": NN, "nt": NT, "tn": TN}[mode]
    a_spec = (pl.BlockSpec((tk, tm), lambda i, j, k: (k, i)) if mode == "tn"
              else pl.BlockSpec((tm, tk), lambda i, j, k: (i + off, k)))
    b_spec = (pl.BlockSpec((tn, tk), lambda i, j, k: (j, k)) if mode == "nt"
              else pl.BlockSpec((tk, tn), lambda i, j, k: (k, j)))
    return _mm_call(name, a, b, a_spec, b_spec, jax.ShapeDtypeStruct((M, N), out_dtype),
                    pl.BlockSpec((tm, tn), lambda i, j, k: (i, j)), (M // tm, N // tn, K // tk), dims,
                    (tm, tn), bias, res, gate, raw_out, pl.BlockSpec((1, tn), lambda i, j, k: (0, j)))


def _mm_to_shards(a, b3, mode, name, out_dtype, tm=512):
    M, K = a.shape
    S = b3.shape[0]
    n = b3.shape[2] if mode == "nn" else b3.shape[1]
    tm = _tile(M, tm)
    return _mm_call(name, a, b3, pl.BlockSpec((tm, K), lambda i, j, k: (i, 0)),
                    pl.BlockSpec((1,) + b3.shape[1:], lambda i, j, k: (j, 0, 0)),
                    jax.ShapeDtypeStruct((S, M, n), out_dtype),
                    pl.BlockSpec((1, tm, n), lambda i, j, k: (j, i, 0)), (M // tm, S, 1),
                    NN if mode == "nn" else NT, (tm, n))


def _mm_over_shards(a3, b3, mode, name, out_dtype=F32, res=None, gate=None, raw_out=False, tm=512, tn=1024):
    S, M, kk = a3.shape
    N = b3.shape[2] if mode == "nn" else b3.shape[1]
    tm, tn = _tile(M, tm), _tile(N, tn)
    b_spec = (pl.BlockSpec((1, kk, tn), lambda i, j, k: (k, 0, j)) if mode == "nn"
              else pl.BlockSpec((1, tn, kk), lambda i, j, k: (k, j, 0)))
    return _mm_call(name, a3, b3, pl.BlockSpec((1, tm, kk), lambda i, j, k: (k, i, 0)), b_spec,
                    jax.ShapeDtypeStruct((M, N), out_dtype), pl.BlockSpec((tm, tn), lambda i, j, k: (i, j)),
                    (M // tm, N // tn, S), NN if mode == "nn" else NT, (tm, tn), None, res, gate, raw_out,
                    pl.BlockSpec((1, tn), lambda i, j, k: (0, j)))


def _mm_sum_shards(a3, b3, mode, name, out_dtype=F32, res=None, gate=None, raw_out=False, tm=512):
    S, M, kk = a3.shape
    N = b3.shape[2] if mode == "nn" else b3.shape[1]
    tm = _tile(M, tm)
    dims = NN if mode == "nn" else NT
    has_res = res is not None

    def body(*refs):
        it = iter(refs)
        a_ref, b_ref = next(it), next(it)
        res_ref, gate_ref = (next(it), next(it)) if has_res else (None, None)
        o_ref = next(it)
        raw_ref = next(it) if raw_out else None
        y = _dot(a_ref[0], b_ref[0], dims)
        for s in range(1, S):
            y = y + _dot(a_ref[s], b_ref[s], dims)
        if raw_ref is not None:
            raw_ref[...] = y
        if has_res:
            y = res_ref[...] + gate_ref[...] * y
        o_ref[...] = y.astype(out_dtype)

    tile = pl.BlockSpec((tm, N), lambda i: (i, 0))
    operands = [a3, b3] + ([res, gate] if has_res else [])
    in_specs = [pl.BlockSpec((S, tm, kk), lambda i: (0, i, 0)), pl.BlockSpec(b3.shape, lambda i: (0, 0, 0))]
    in_specs += [tile, _vec_spec(N)] if has_res else []
    out_shape = [jax.ShapeDtypeStruct((M, N), out_dtype)] + ([jax.ShapeDtypeStruct((M, N), F32)] if raw_out else [])
    outs = pl.pallas_call(
        body, name=name, grid=(M // tm,),
        out_shape=out_shape, in_specs=in_specs, out_specs=[tile] * len(out_shape),
        compiler_params=pltpu.CompilerParams(dimension_semantics=("parallel",)),
    )(*operands)
    return outs if raw_out else outs[0]


def _mm_tn_shard_rows(a3, b, name, out_dtype, tn=1024, tk=1024):
    S, T, m = a3.shape
    N = b.shape[1]
    tn, tk = _tile(N, tn), _tile(T, tk)
    return _mm_call(name, a3, b, pl.BlockSpec((1, tk, m), lambda i, j, k: (i, k, 0)),
                    pl.BlockSpec((tk, tn), lambda i, j, k: (k, j)), jax.ShapeDtypeStruct((S, m, N), out_dtype),
                    pl.BlockSpec((1, m, tn), lambda i, j, k: (i, 0, j)), (S, N // tn, T // tk), TN, (m, tn))


def _mm_tn_shard_cols(a, b3, name, out_dtype, tm=1024, tk=1024):
    T, M = a.shape
    S, _, n = b3.shape
    tm, tk = _tile(M, tm), _tile(T, tk)
    return _mm_call(name, a, b3, pl.BlockSpec((tk, tm), lambda i, j, k: (k, i)),
                    pl.BlockSpec((1, tk, n), lambda i, j, k: (j, k, 0)), jax.ShapeDtypeStruct((S, M, n), out_dtype),
                    pl.BlockSpec((1, tm, n), lambda i, j, k: (j, i, 0)), (M // tm, S, T // tk), TN, (tm, n))


def _row_spec(tm, width, off=0):
    return pl.BlockSpec((tm, width), lambda i: (i + off, 0))


def _vec_spec(width):
    return pl.BlockSpec((1, width), lambda i: (0, 0))


def _norm_mod_fwd(h, g, sc, sh, name):
    R, Dm = h.shape
    tm = _tile(R, ROW_BLOCK, 8)

    def body(h_ref, g_ref, sc_ref, sh_ref, o_ref):
        hv = h_ref[...]
        r = lax.rsqrt(jnp.mean(hv * hv, axis=-1, keepdims=True) + EPS)
        o_ref[...] = ((hv * r) * g_ref[...] * (1.0 + sc_ref[...]) + sh_ref[...]).astype(BF16)

    return pl.pallas_call(
        body, name=name, grid=(R // tm,),
        out_shape=jax.ShapeDtypeStruct((R, Dm), BF16),
        in_specs=[_row_spec(tm, Dm), _vec_spec(Dm), _vec_spec(Dm), _vec_spec(Dm)],
        out_specs=_row_spec(tm, Dm),
        compiler_params=pltpu.CompilerParams(dimension_semantics=("parallel",)),
    )(h, g, sc, sh)


def _norm_mod_fwd_cat(hc, h, g, csc, csh, sc, sh, name):
    (C, Dm), T = hc.shape, h.shape[0]
    tm = _tile(math.gcd(C, T), ROW_BLOCK, 8)
    off = C // tm

    def body(hc_ref, h_ref, g_ref, csc_ref, csh_ref, sc_ref, sh_ref, o_ref):
        is_ctx = pl.program_id(0) < off
        hv = jnp.where(is_ctx, hc_ref[...], h_ref[...])
        scv = jnp.where(is_ctx, csc_ref[...], sc_ref[...])
        shv = jnp.where(is_ctx, csh_ref[...], sh_ref[...])
        r = lax.rsqrt(jnp.mean(hv * hv, axis=-1, keepdims=True) + EPS)
        o_ref[...] = ((hv * r) * g_ref[...] * (1.0 + scv) + shv).astype(BF16)

    return pl.pallas_call(
        body, name=name, grid=((C + T) // tm,),
        out_shape=jax.ShapeDtypeStruct((C + T, Dm), BF16),
        in_specs=[pl.BlockSpec((tm, Dm), lambda i: (jnp.minimum(i, off - 1), 0)),
                  pl.BlockSpec((tm, Dm), lambda i: (jnp.maximum(i - off, 0), 0))] + [_vec_spec(Dm)] * 5,
        out_specs=_row_spec(tm, Dm),
        compiler_params=pltpu.CompilerParams(dimension_semantics=("parallel",)),
    )(hc, h, g, csc, csh, sc, sh)


def _norm_mod_bwd(h, g, sc, dxm, dres, name, dxm_row_off=0):
    R, Dm = h.shape
    tm = _tile(R, ROW_BLOCK, 8)
    off = dxm_row_off // tm
    has_res = dres is not None

    def body(*refs):
        it = iter(refs)
        h_ref, g_ref, sc_ref, dx_ref = next(it), next(it), next(it), next(it)
        dres_ref = next(it) if has_res else None
        dh_ref, da_ref, dsh_ref = next(it), next(it), next(it)
        i = pl.program_id(0)

        @pl.when(i == 0)
        def _():
            da_ref[...] = jnp.zeros_like(da_ref)
            dsh_ref[...] = jnp.zeros_like(dsh_ref)

        hv = h_ref[...]
        dx = dx_ref[...].astype(F32)
        r = lax.rsqrt(jnp.mean(hv * hv, axis=-1, keepdims=True) + EPS)
        n = hv * r
        da_ref[...] += jnp.sum(dx * n, axis=0, keepdims=True)
        dsh_ref[...] += jnp.sum(dx, axis=0, keepdims=True)
        dn = dx * (g_ref[...] * (1.0 + sc_ref[...]))
        dh = r * (dn - n * jnp.mean(dn * n, axis=-1, keepdims=True))
        if has_res:
            dh = dh + dres_ref[...]
        dh_ref[...] = dh

    operands = [h, g, sc, dxm] + ([dres] if has_res else [])
    in_specs = [_row_spec(tm, Dm), _vec_spec(Dm), _vec_spec(Dm), _row_spec(tm, Dm, off)]
    in_specs += [_row_spec(tm, Dm)] if has_res else []
    return pl.pallas_call(
        body, name=name, grid=(R // tm,),
        out_shape=[jax.ShapeDtypeStruct((R, Dm), F32), jax.ShapeDtypeStruct((1, Dm), F32),
                   jax.ShapeDtypeStruct((1, Dm), F32)],
        in_specs=in_specs,
        out_specs=[_row_spec(tm, Dm), _vec_spec(Dm), _vec_spec(Dm)],
        compiler_params=pltpu.CompilerParams(dimension_semantics=("arbitrary",)),
    )(*operands)


def _gate_bwd(dh, y, gt, name):
    R, Dm = dh.shape
    tm = _tile(R, ROW_BLOCK, 8)

    def body(dh_ref, y_ref, gt_ref, dy_ref, dgt_ref, dsum_ref):
        i = pl.program_id(0)

        @pl.when(i == 0)
        def _():
            dgt_ref[...] = jnp.zeros_like(dgt_ref)
            dsum_ref[...] = jnp.zeros_like(dsum_ref)

        dhv = dh_ref[...]
        dy = dhv * gt_ref[...]
        dgt_ref[...] += jnp.sum(dhv * y_ref[...], axis=0, keepdims=True)
        dsum_ref[...] += jnp.sum(dy, axis=0, keepdims=True)
        dy_ref[...] = dy.astype(BF16)

    return pl.pallas_call(
        body, name=name, grid=(R // tm,),
        out_shape=[jax.ShapeDtypeStruct((R, Dm), BF16), jax.ShapeDtypeStruct((1, Dm), F32),
                   jax.ShapeDtypeStruct((1, Dm), F32)],
        in_specs=[_row_spec(tm, Dm), _row_spec(tm, Dm), _vec_spec(Dm)],
        out_specs=[_row_spec(tm, Dm), _vec_spec(Dm), _vec_spec(Dm)],
        compiler_params=pltpu.CompilerParams(dimension_semantics=("arbitrary",)),
    )(dh, y, gt)


def _ffn_in_swiglu(xf, w3, name, tm=1024):
    T, K = xf.shape
    S, _, n = w3.shape
    half = S // 2
    tm = _tile(T, tm)

    def body(a_ref, wg_ref, wu_ref, gu_ref, act_ref):
        a = a_ref[...]
        g = _dot(a, wg_ref[0], NN)
        u = _dot(a, wu_ref[0], NN)
        gu_ref[0, 0] = g.astype(BF16)
        gu_ref[1, 0] = u.astype(BF16)
        act_ref[0] = (_silu(g) * u).astype(BF16)

    return pl.pallas_call(
        body, name=name, grid=(T // tm, half),
        out_shape=[jax.ShapeDtypeStruct((2, half, T, n), BF16), jax.ShapeDtypeStruct((half, T, n), BF16)],
        in_specs=[pl.BlockSpec((tm, K), lambda i, j: (i, 0)),
                  pl.BlockSpec((1, K, n), lambda i, j: (j, 0, 0)),
                  pl.BlockSpec((1, K, n), lambda i, j: (j + half, 0, 0))],
        out_specs=[pl.BlockSpec((2, 1, tm, n), lambda i, j: (0, j, i, 0)),
                   pl.BlockSpec((1, tm, n), lambda i, j: (j, i, 0))],
        compiler_params=pltpu.CompilerParams(dimension_semantics=("parallel", "parallel")),
    )(xf, w3, w3)


def _ffn_out_dx_swiglu(df, wo, gu, name, tm=1024):
    T, Dm = df.shape
    half, n, _ = wo.shape
    tm = _tile(T, tm)

    def body(df_ref, w_ref, gu_ref, o_ref):
        da = _dot(df_ref[...], w_ref[0], NT)
        g = gu_ref[0, 0].astype(F32)
        u = gu_ref[1, 0].astype(F32)
        s = _sigmoid(g)
        o_ref[0, 0] = (da * u * (s * (1.0 + g * (1.0 - s)))).astype(BF16)
        o_ref[1, 0] = (da * (g * s)).astype(BF16)

    gu_spec = pl.BlockSpec((2, 1, tm, n), lambda i, j: (0, j, i, 0))
    return pl.pallas_call(
        body, name=name, grid=(T // tm, half),
        out_shape=jax.ShapeDtypeStruct(gu.shape, BF16),
        in_specs=[pl.BlockSpec((tm, Dm), lambda i, j: (i, 0)),
                  pl.BlockSpec((1, n, Dm), lambda i, j: (j, 0, 0)), gu_spec],
        out_specs=gu_spec,
        compiler_params=pltpu.CompilerParams(dimension_semantics=("parallel", "parallel")),
    )(df, wo, gu)


def _glu_fwd(ag, name):
    R = ag.shape[0]
    tm = _tile(R, ROW_BLOCK, 8)

    def body(ag_ref, o_ref):
        o_ref[...] = ag_ref[:, :D_MODEL] * _sigmoid(ag_ref[:, D_MODEL:])

    return pl.pallas_call(
        body, name=name, grid=(R // tm,),
        out_shape=jax.ShapeDtypeStruct((R, D_MODEL), F32),
        in_specs=[_row_spec(tm, 2 * D_MODEL)],
        out_specs=_row_spec(tm, D_MODEL),
        compiler_params=pltpu.CompilerParams(dimension_semantics=("parallel",)),
    )(ag)


def _glu_bwd(ag, dhg, name):
    R = ag.shape[0]
    tm = _tile(R, ROW_BLOCK, 8)

    def body(ag_ref, dh_ref, o_ref, s_ref):
        i = pl.program_id(0)

        @pl.when(i == 0)
        def _():
            s_ref[...] = jnp.zeros_like(s_ref)

        a = ag_ref[:, :D_MODEL]
        s = _sigmoid(ag_ref[:, D_MODEL:])
        dh = dh_ref[...]
        da = dh * s
        dg = dh * a * s * (1.0 - s)
        o_ref[:, :D_MODEL] = da.astype(BF16)
        o_ref[:, D_MODEL:] = dg.astype(BF16)
        s_ref[:, :D_MODEL] += jnp.sum(da, axis=0, keepdims=True)
        s_ref[:, D_MODEL:] += jnp.sum(dg, axis=0, keepdims=True)

    return pl.pallas_call(
        body, name=name, grid=(R // tm,),
        out_shape=[jax.ShapeDtypeStruct((R, 2 * D_MODEL), BF16), jax.ShapeDtypeStruct((1, 2 * D_MODEL), F32)],
        in_specs=[_row_spec(tm, 2 * D_MODEL), _row_spec(tm, D_MODEL)],
        out_specs=[_row_spec(tm, 2 * D_MODEL), _vec_spec(2 * D_MODEL)],
        compiler_params=pltpu.CompilerParams(dimension_semantics=("arbitrary",)),
    )(ag, dhg)


def _halo_specs(tm, nblk, width):
    per = tm // CONV_HALO
    prev = pl.BlockSpec((CONV_HALO, width), lambda i: (jnp.maximum(i * per - 1, 0), 0))
    nxt = pl.BlockSpec((CONV_HALO, width), lambda i: (jnp.minimum((i + 1) * per, nblk * per - 1), 0))
    return prev, nxt


def _fill_halo(scr, prev_ref, cur_ref, next_ref, i, nblk, tm):
    scr[0:CONV_HALO, :] = jnp.where(i > 0, prev_ref[...], 0.0)
    scr[CONV_HALO:CONV_HALO + tm, :] = cur_ref[...]
    scr[CONV_HALO + tm:2 * CONV_HALO + tm, :] = jnp.where(i < nblk - 1, next_ref[...], 0.0)


CONV_ROWS = 128


def _windows(scr, cols, tm):
    reach = (CONV_WIDTH // SUBLANES) * SUBLANES
    for r in range(SUBLANES):
        base = scr[pl.ds(r, tm + reach), cols]
        for a in range(reach // SUBLANES + 1):
            off = SUBLANES * a + r
            if 1 <= off <= CONV_WIDTH:
                yield off, base[SUBLANES * a:SUBLANES * a + tm]


def _conv_fwd(hg, w_dw, b_dw, name):
    R, Dm = hg.shape
    tm = _tile(R, CONV_ROWS, CONV_HALO)
    nblk = R // tm
    prev_spec, next_spec = _halo_specs(tm, nblk, Dm)

    def body(prev_ref, cur_ref, next_ref, w_ref, bdw_ref, hd_ref, scr):
        _fill_halo(scr, prev_ref, cur_ref, next_ref, pl.program_id(0), nblk, tm)
        for cb in range(Dm // LANES):
            cols = slice(cb * LANES, (cb + 1) * LANES)
            acc = jnp.zeros((tm, LANES), F32) + bdw_ref[:, cols]
            for off, win in _windows(scr, cols, tm):
                acc = acc + w_ref[off - 1:off, cols] * win
            hd_ref[:, cols] = acc

    return pl.pallas_call(
        body, name=name, grid=(nblk,),
        out_shape=jax.ShapeDtypeStruct((R, Dm), F32),
        in_specs=[prev_spec, _row_spec(tm, Dm), next_spec,
                  pl.BlockSpec((CONV_WIDTH, Dm), lambda i: (0, 0)), _vec_spec(Dm)],
        out_specs=_row_spec(tm, Dm),
        scratch_shapes=[pltpu.VMEM((tm + 2 * CONV_HALO, Dm), F32)],
        compiler_params=pltpu.CompilerParams(dimension_semantics=("parallel",)),
    )(hg, hg, hg, w_dw, b_dw)


def _ln_silu_fwd(hd, ln_g, ln_b, name):
    R, Dm = hd.shape
    tm = _tile(R, ROW_BLOCK, 8)

    def body(hd_ref, g_ref, b_ref, hs_ref):
        hd = hd_ref[...]
        xc = hd - jnp.mean(hd, axis=-1, keepdims=True)
        rs = lax.rsqrt(jnp.mean(xc * xc, axis=-1, keepdims=True) + EPS)
        hs_ref[...] = _silu(xc * rs * g_ref[...] + b_ref[...]).astype(BF16)

    return pl.pallas_call(
        body, name=name, grid=(R // tm,),
        out_shape=jax.ShapeDtypeStruct((R, Dm), BF16),
        in_specs=[_row_spec(tm, Dm), _vec_spec(Dm), _vec_spec(Dm)],
        out_specs=_row_spec(tm, Dm),
        compiler_params=pltpu.CompilerParams(dimension_semantics=("parallel",)),
    )(hd, ln_g, ln_b)


def _ln_silu_bwd(dhs, hd, ln_g, ln_b, name):
    R, Dm = hd.shape
    tm = _tile(R, ROW_BLOCK, 8)

    def body(dhs_ref, hd_ref, g_ref, b_ref, dhd_ref, dg_ref, db_ref, dsum_ref):
        i = pl.program_id(0)

        @pl.when(i == 0)
        def _():
            dg_ref[...] = jnp.zeros_like(dg_ref)
            db_ref[...] = jnp.zeros_like(db_ref)
            dsum_ref[...] = jnp.zeros_like(dsum_ref)

        hd = hd_ref[...]
        mu = jnp.mean(hd, axis=-1, keepdims=True)
        xc = hd - mu
        rs = lax.rsqrt(jnp.mean(xc * xc, axis=-1, keepdims=True) + EPS)
        z = xc * rs
        hl = z * g_ref[...] + b_ref[...]
        dhl = dhs_ref[...] * _dsilu(hl)
        dg_ref[...] += jnp.sum(dhl * z, axis=0, keepdims=True)
        db_ref[...] += jnp.sum(dhl, axis=0, keepdims=True)
        dz = dhl * g_ref[...]
        dhd = rs * (dz - jnp.mean(dz, axis=-1, keepdims=True) - z * jnp.mean(dz * z, axis=-1, keepdims=True))
        dsum_ref[...] += jnp.sum(dhd, axis=0, keepdims=True)
        dhd_ref[...] = dhd

    return pl.pallas_call(
        body, name=name, grid=(R // tm,),
        out_shape=[jax.ShapeDtypeStruct((R, Dm), F32)] + [jax.ShapeDtypeStruct((1, Dm), F32)] * 3,
        in_specs=[_row_spec(tm, Dm), _row_spec(tm, Dm), _vec_spec(Dm), _vec_spec(Dm)],
        out_specs=[_row_spec(tm, Dm), _vec_spec(Dm), _vec_spec(Dm), _vec_spec(Dm)],
        compiler_params=pltpu.CompilerParams(dimension_semantics=("arbitrary",)),
    )(dhs, hd, ln_g, ln_b)


def _conv_bwd(dhd, hg, w_dw, name):
    R, Dm = hg.shape
    tm = _tile(R, CONV_ROWS, CONV_HALO)
    nblk = R // tm
    prev_spec, next_spec = _halo_specs(tm, nblk, Dm)

    def body(dprev, dcur, dnext, gprev, gcur, gnext, w_ref, dhg_ref, dw_ref, dscr, gscr, dwp):
        i = pl.program_id(0)

        @pl.when(i == 0)
        def _():
            dwp[...] = jnp.zeros_like(dwp)

        _fill_halo(dscr, dprev, dcur, dnext, i, nblk, tm)
        _fill_halo(gscr, gprev, gcur, gnext, i, nblk, tm)
        for cb in range(Dm // LANES):
            cols = slice(cb * LANES, (cb + 1) * LANES)
            acc = jnp.zeros((tm, LANES), F32)
            for off, win in _windows(dscr, cols, tm):
                j = CONV_WIDTH - off
                acc = acc + w_ref[j:j + 1, cols] * win
            dhg_ref[:, cols] = acc
            d_here = dcur[:, cols]
            for off, win in _windows(gscr, cols, tm):
                j = off - 1
                prod = d_here * win
                part = prod[0:SUBLANES]
                for k in range(1, tm // SUBLANES):
                    part = part + prod[k * SUBLANES:(k + 1) * SUBLANES]
                dwp[j * SUBLANES:(j + 1) * SUBLANES, cols] += part

        @pl.when(i == nblk - 1)
        def _():
            for j in range(CONV_WIDTH):
                dw_ref[j:j + 1, :] = jnp.sum(dwp[j * SUBLANES:(j + 1) * SUBLANES, :], axis=0, keepdims=True)

    return pl.pallas_call(
        body, name=name, grid=(nblk,),
        out_shape=[jax.ShapeDtypeStruct((R, Dm), F32), jax.ShapeDtypeStruct((CONV_WIDTH, Dm), F32)],
        in_specs=[prev_spec, _row_spec(tm, Dm), next_spec, prev_spec, _row_spec(tm, Dm), next_spec,
                  pl.BlockSpec((CONV_WIDTH, Dm), lambda i: (0, 0))],
        out_specs=[_row_spec(tm, Dm), pl.BlockSpec((CONV_WIDTH, Dm), lambda i: (0, 0))],
        scratch_shapes=[pltpu.VMEM((tm + 2 * CONV_HALO, Dm), F32)] * 2
        + [pltpu.VMEM((CONV_WIDTH * SUBLANES, Dm), F32)],
        compiler_params=pltpu.CompilerParams(dimension_semantics=("arbitrary",)),
    )(dhd, dhd, dhd, hg, hg, hg, w_dw)


def _swap16(y, lane):
    return jnp.where((lane & 16) == 0, pltpu.roll(y, LANES - 16, 1), pltpu.roll(y, 16, 1))


def _head_mean(v, bd):
    hi, lo = _split_bf16(v)
    return (_dot(hi, bd, NN) + _dot(lo, bd, NN)) * (1.0 / HEAD_DIM)


Q_COLS = (0, ATTN_WIDTH)
K_COLS = (ATTN_WIDTH, ATTN_WIDTH + HEAD_DIM * 2)
V_COLS = (K_COLS[1], K_COLS[1] + HEAD_DIM * 2)
SU_COLS = (V_COLS[1], V_COLS[1] + SG_WIDTH)
SV_COLS = (SU_COLS[1], SU_COLS[1] + SG_WIDTH)


def _mix_prep_fwd(p, ctx_rows, cos, sin, qg, kg, bd, w_sp, b_spt, name):
    TT = p.shape[0]
    off = ctx_rows // CHUNK
    q_scale = HEAD_DIM ** -0.5

    def body(p_ref, cos_ref, sin_ref, qg_ref, kg_ref, bd_ref, w_ref, b_ref,
             q_ref, kp_ref, vp_ref, kt_ref, vt_ref, sg_ref):
        lane = lax.broadcasted_iota(jnp.int32, (CHUNK, LANES), 1)
        low = lane < HEAD_DIM
        cs, sn, bdv = cos_ref[...], sin_ref[...], bd_ref[...]

        def norm_rope(xv, gain):
            r = lax.rsqrt(_head_mean(xv * xv, bdv) + EPS)
            yv = xv * r * gain
            return yv * cs + _swap16(yv, lane) * sn

        def pad_heads(ref, t):
            tr = pltpu.roll(t, HEAD_DIM, 1)
            ref[0, 0] = jnp.where(low, t, 0.0).astype(BF16)
            ref[0, 1] = jnp.where(low, 0.0, tr).astype(BF16)
            ref[1, 0] = jnp.where(low, tr, 0.0).astype(BF16)
            ref[1, 1] = jnp.where(low, 0.0, t).astype(BF16)

        for a in range(ATTN_WIDTH // LANES):
            xv = p_ref[:, a * LANES:(a + 1) * LANES]
            q_ref[:, a * LANES:(a + 1) * LANES] = (norm_rope(xv, qg_ref[...]) * q_scale).astype(BF16)
        kh = norm_rope(p_ref[:, K_COLS[0]:K_COLS[1]], kg_ref[...])
        pad_heads(kp_ref, kh)
        vh = p_ref[:, V_COLS[0]:V_COLS[1]]
        pad_heads(vp_ref, vh)
        for t_ref, t in ((kt_ref, kh.T), (vt_ref, vh.T)):
            t_ref[0] = t[:HEAD_DIM].astype(BF16)
            t_ref[1] = t[HEAD_DIM:].astype(BF16)
        for g in range(N_SG_GROUPS):
            u = _gelu(p_ref[:, SU_COLS[0] + g * LANES:SU_COLS[0] + (g + 1) * LANES])
            vg = _gelu(p_ref[:, SV_COLS[0] + g * LANES:SV_COLS[0] + (g + 1) * LANES])
            xc = vg - jnp.mean(vg, axis=-1, keepdims=True)
            vn = xc * lax.rsqrt(jnp.mean(xc * xc, axis=-1, keepdims=True) + EPS)
            mixed = _dot(w_ref[g].astype(BF16), vn.astype(BF16), NN) + b_ref[:, g:g + 1]
            sg_ref[:, g * LANES:(g + 1) * LANES] = (u * mixed).astype(BF16)

    def row(width):
        return pl.BlockSpec((CHUNK, width), lambda i: (i, 0))

    def whole(shape):
        return pl.BlockSpec(shape, lambda i: (0,) * len(shape))

    pad_spec = pl.BlockSpec((2, 2, CHUNK, LANES), lambda i: (0, 0, i, 0))
    return pl.pallas_call(
        body, name=name, grid=(TT // CHUNK,),
        out_shape=[jax.ShapeDtypeStruct((TT, ATTN_WIDTH), BF16),
                   jax.ShapeDtypeStruct((2, 2, TT, LANES), BF16), jax.ShapeDtypeStruct((2, 2, TT, LANES), BF16),
                   jax.ShapeDtypeStruct((2, HEAD_DIM, TT), BF16), jax.ShapeDtypeStruct((2, HEAD_DIM, TT), BF16),
                   jax.ShapeDtypeStruct((TT - ctx_rows, ATTN_WIDTH + SG_WIDTH), BF16)],
        in_specs=[row(IN_WIDTH), row(LANES), row(LANES), whole((1, LANES)), whole((1, LANES)),
                  whole((LANES, LANES)), whole((N_SG_GROUPS, CHUNK, CHUNK)), whole((CHUNK, N_SG_GROUPS))],
        out_specs=[row(ATTN_WIDTH), pad_spec, pad_spec,
                   pl.BlockSpec((2, HEAD_DIM, CHUNK), lambda i: (0, 0, i)),
                   pl.BlockSpec((2, HEAD_DIM, CHUNK), lambda i: (0, 0, i)),
                   pl.BlockSpec((CHUNK, SG_WIDTH), lambda i: (jnp.maximum(i - off, 0), 1))],
        compiler_params=pltpu.CompilerParams(dimension_semantics=("arbitrary",)),
    )(p, cos, sin, qg, kg, bd, w_sp, b_spt)


def _mix_prep_bwd(p, dq, f, dao, ctx_rows, cos, sin, qg, kg, bd, w_sp, w_spt, b_spt, name):
    TT = p.shape[0]
    off = ctx_rows // CHUNK
    q_scale = HEAD_DIM ** -0.5

    def body(p_ref, dq_ref, f_ref, dsg_ref, cos_ref, sin_ref, qg_ref, kg_ref, bd_ref, w_ref, wt_ref,
             b_ref, dp_ref, dqg_ref, dkg_ref, dw_ref, db_ref):
        i = pl.program_id(0)

        @pl.when(i == 0)
        def _():
            dqg_ref[...] = jnp.zeros_like(dqg_ref)
            dkg_ref[...] = jnp.zeros_like(dkg_ref)
            dw_ref[...] = jnp.zeros_like(dw_ref)
            db_ref[...] = jnp.zeros_like(db_ref)

        latent = (i >= off).astype(F32)
        lane = lax.broadcasted_iota(jnp.int32, (CHUNK, LANES), 1)
        low = lane < HEAD_DIM
        cs, sn, bdv = cos_ref[...], sin_ref[...], bd_ref[...]

        def fold(b0):
            return jnp.where(low, f_ref[0, b0] + pltpu.roll(f_ref[0, b0 + 1], HEAD_DIM, 1),
                             pltpu.roll(f_ref[1, b0], HEAD_DIM, 1) + f_ref[1, b0 + 1])

        def norm_rope_bwd(xv, dout, gain):
            r = lax.rsqrt(_head_mean(xv * xv, bdv) + EPS)
            n = xv * r
            dy = dout * cs + _swap16(dout * sn, lane)
            dn = dy * gain
            dx = r * (dn - n * _head_mean(dn * n, bdv))
            return dx, jnp.sum(dy * n, axis=0, keepdims=True)

        for a in range(ATTN_WIDTH // LANES):
            cols = slice(a * LANES, (a + 1) * LANES)
            dx, dg = norm_rope_bwd(p_ref[:, cols], dq_ref[:, cols] * (latent * q_scale), qg_ref[...])
            dp_ref[:, cols] = dx.astype(BF16)
            dqg_ref[...] += dg
        dx, dg = norm_rope_bwd(p_ref[:, K_COLS[0]:K_COLS[1]], fold(0), kg_ref[...])
        dp_ref[:, K_COLS[0]:K_COLS[1]] = dx.astype(BF16)
        dkg_ref[...] += dg
        dp_ref[:, V_COLS[0]:V_COLS[1]] = fold(2).astype(BF16)
        for g in range(N_SG_GROUPS):
            su = p_ref[:, SU_COLS[0] + g * LANES:SU_COLS[0] + (g + 1) * LANES]
            sv = p_ref[:, SV_COLS[0] + g * LANES:SV_COLS[0] + (g + 1) * LANES]
            u, vg = _gelu(su), _gelu(sv)
            xc = vg - jnp.mean(vg, axis=-1, keepdims=True)
            rs = lax.rsqrt(jnp.mean(xc * xc, axis=-1, keepdims=True) + EPS)
            vn = xc * rs
            vnb = vn.astype(BF16)
            mixed = _dot(w_ref[g].astype(BF16), vnb, NN) + b_ref[:, g:g + 1]
            dsg = dsg_ref[:, g * LANES:(g + 1) * LANES] * latent
            du = dsg * mixed
            dmix = dsg * u
            dmb = dmix.astype(BF16)
            db_ref[:, g:g + 1] += jnp.sum(dmix, axis=-1, keepdims=True)
            dw_ref[g] += _dot(dmb, vnb, NT)
            dvn = _dot(wt_ref[g].astype(BF16), dmb, NN)
            dvg = rs * (dvn - jnp.mean(dvn, axis=-1, keepdims=True)
                        - vn * jnp.mean(dvn * vn, axis=-1, keepdims=True))
            dp_ref[:, SU_COLS[0] + g * LANES:SU_COLS[0] + (g + 1) * LANES] = (du * _dgelu(su)).astype(BF16)
            dp_ref[:, SV_COLS[0] + g * LANES:SV_COLS[0] + (g + 1) * LANES] = (dvg * _dgelu(sv)).astype(BF16)

    def row(width):
        return pl.BlockSpec((CHUNK, width), lambda i: (i, 0))

    def latent_row(width, col_block):
        return pl.BlockSpec((CHUNK, width), lambda i: (jnp.maximum(i - off, 0), col_block))

    def whole(shape):
        return pl.BlockSpec(shape, lambda i: (0,) * len(shape))

    return pl.pallas_call(
        body, name=name, grid=(TT // CHUNK,),
        out_shape=[jax.ShapeDtypeStruct((TT, IN_WIDTH), BF16), jax.ShapeDtypeStruct((1, LANES), F32),
                   jax.ShapeDtypeStruct((1, LANES), F32),
                   jax.ShapeDtypeStruct((N_SG_GROUPS, CHUNK, CHUNK), F32),
                   jax.ShapeDtypeStruct((CHUNK, N_SG_GROUPS), F32)],
        in_specs=[row(IN_WIDTH), latent_row(ATTN_WIDTH, 0),
                  pl.BlockSpec((2, 4, CHUNK, LANES), lambda i: (0, 0, i, 0)),
                  latent_row(SG_WIDTH, 1), row(LANES), row(LANES), whole((1, LANES)), whole((1, LANES)),
                  whole((LANES, LANES)), whole((N_SG_GROUPS, CHUNK, CHUNK)),
                  whole((N_SG_GROUPS, CHUNK, CHUNK)), whole((CHUNK, N_SG_GROUPS))],
        out_specs=[row(IN_WIDTH), whole((1, LANES)), whole((1, LANES)),
                   whole((N_SG_GROUPS, CHUNK, CHUNK)), whole((CHUNK, N_SG_GROUPS))],
        compiler_params=pltpu.CompilerParams(dimension_semantics=("arbitrary",)),
    )(p, dq, f, dao, cos, sin, qg, kg, bd, w_sp, w_spt, b_spt)


def _col_reduce(t, pair_op, reduce_op, slab=256):
    R = t.shape[0]
    slab = _tile(R, slab, SUBLANES)
    part = t[0:slab]
    for k in range(1, R // slab):
        part = pair_op(part, t[k * slab:(k + 1) * slab])
    return reduce_op(part, axis=0, keepdims=True)


def _attn_fwd(q, kpad, vt, ao, ctx_rows, name, tq=256):
    TT = q.shape[0]
    T = TT - ctx_rows
    tq = _tile(T, tq)
    off = ctx_rows // tq
    group = 2 * LANES

    def body(q_ref, k_ref, vt_ref, ao_in, o_ref, lse_ref):
        del ao_in
        vtv = vt_ref[0]
        for a in range(2):
            qa = q_ref[:, a * LANES:(a + 1) * LANES]
            halves = []
            for b in range(2):
                st = _dot(k_ref[0, b], qa, NT)
                m = _col_reduce(st, jnp.maximum, jnp.max)
                e = jnp.exp(st - m)
                l = _col_reduce(e, jnp.add, jnp.sum)
                lse_ref[0, 2 * a + b:2 * a + b + 1, :] = m + jnp.log(l)
                halves.append(_dot(vtv, e.astype(BF16), NN) * (1.0 / l))
            o_ref[:, a * LANES:(a + 1) * LANES] = jnp.concatenate(halves, axis=0).T.astype(BF16)

    return pl.pallas_call(
        body, name=name, grid=(2, T // tq),
        out_shape=[jax.ShapeDtypeStruct(ao.shape, BF16), jax.ShapeDtypeStruct((2, 4, T), F32)],
        in_specs=[pl.BlockSpec((tq, group), lambda j, i: (i + off, j)),
                  pl.BlockSpec((1, 2, TT, LANES), lambda j, i: (j, 0, 0, 0)),
                  pl.BlockSpec((1, HEAD_DIM, TT), lambda j, i: (j, 0, 0)),
                  pl.BlockSpec(memory_space=pl.ANY)],
        out_specs=[pl.BlockSpec((tq, group), lambda j, i: (i, j)),
                   pl.BlockSpec((1, 4, tq), lambda j, i: (j, 0, i))],
        input_output_aliases={3: 0},
        compiler_params=pltpu.CompilerParams(dimension_semantics=("parallel", "parallel")),
    )(q, kpad, vt, ao)


def _attn_bwd(q, dao, ao, lse, kpad, vpad, kt, ctx_rows, name, tq=256):
    TT = q.shape[0]
    T = TT - ctx_rows
    tq = _tile(T, tq)
    off = ctx_rows // tq
    group = 2 * LANES

    def body(q_ref, do_ref, o_ref, lse_ref, k_ref, v_ref, kt_ref, dq_ref, f_ref):
        i = pl.program_id(1)

        @pl.when(i == 0)
        def _():
            f_ref[...] = jnp.zeros_like(f_ref)

        ktv = kt_ref[0]
        row = lax.broadcasted_iota(jnp.int32, (SUBLANES, LANES), 0)
        lane = lax.broadcasted_iota(jnp.int32, (SUBLANES, LANES), 1)
        half_ones = (jnp.where(lane < HEAD_DIM, 0, 1) == row).astype(BF16)
        for a in range(2):
            cols = slice(a * LANES, (a + 1) * LANES)
            qa = q_ref[:, cols]
            do32 = do_ref[:, cols]
            doa = do32.astype(BF16)
            hi, lo = _split_bf16(do32 * o_ref[:, cols].astype(F32))
            deltas = _dot(half_ones, hi, NT) + _dot(half_ones, lo, NT)
            halves = []
            for b in range(2):
                h = 2 * a + b
                st = _dot(k_ref[0, b], qa, NT)
                pt = jnp.exp(st - lse_ref[0, h:h + 1, :])
                dpt = _dot(v_ref[0, b], doa, NT)
                dst = (pt * (dpt - deltas[b:b + 1, :])).astype(BF16)
                f_ref[0, b] += _dot(dst, qa, NN)
                f_ref[0, 2 + b] += _dot(pt.astype(BF16), doa, NN)
                halves.append(_dot(ktv, dst, NN))
            dq_ref[:, cols] = jnp.concatenate(halves, axis=0).T

    kv_spec = pl.BlockSpec((1, 2, TT, LANES), lambda j, i: (j, 0, 0, 0))
    out_cols = pl.BlockSpec((tq, group), lambda j, i: (i, j))
    return pl.pallas_call(
        body, name=name, grid=(2, T // tq),
        out_shape=[jax.ShapeDtypeStruct((T, ATTN_WIDTH), F32), jax.ShapeDtypeStruct((2, 4, TT, LANES), F32)],
        in_specs=[pl.BlockSpec((tq, group), lambda j, i: (i + off, j)), out_cols, out_cols,
                  pl.BlockSpec((1, 4, tq), lambda j, i: (j, 0, i)),
                  kv_spec, kv_spec, pl.BlockSpec((1, HEAD_DIM, TT), lambda j, i: (j, 0, 0))],
        out_specs=[out_cols, pl.BlockSpec((1, 4, TT, LANES), lambda j, i: (j, 0, 0, 0))],
        compiler_params=pltpu.CompilerParams(dimension_semantics=("parallel", "arbitrary")),
    )(q, dao, ao, lse, kpad, vpad, kt)


def _final_fwd_bwd(h, g, target, name):
    R, Dm = h.shape
    tm = _tile(R, ROW_BLOCK, 8)

    def body(h_ref, g_ref, t_ref, dh_ref, loss_ref, dg_ref):
        i = pl.program_id(0)

        @pl.when(i == 0)
        def _():
            loss_ref[...] = jnp.zeros_like(loss_ref)
            dg_ref[...] = jnp.zeros_like(dg_ref)

        hv = h_ref[...]
        r = lax.rsqrt(jnp.mean(hv * hv, axis=-1, keepdims=True) + EPS)
        n = hv * r
        diff = n * g_ref[...] - t_ref[...]
        loss_ref[...] += jnp.sum(diff * diff)
        dout = diff * (1.0 / Dm)
        dg_ref[...] += jnp.sum(dout * n, axis=0, keepdims=True)
        dn = dout * g_ref[...]
        dh_ref[...] = r * (dn - n * jnp.mean(dn * n, axis=-1, keepdims=True))

    return pl.pallas_call(
        body, name=name, grid=(R // tm,),
        out_shape=[jax.ShapeDtypeStruct((R, Dm), F32), jax.ShapeDtypeStruct((1, LANES), F32),
                   jax.ShapeDtypeStruct((1, Dm), F32)],
        in_specs=[_row_spec(tm, Dm), _vec_spec(Dm), _row_spec(tm, Dm)],
        out_specs=[_row_spec(tm, Dm), _vec_spec(LANES), _vec_spec(Dm)],
        compiler_params=pltpu.CompilerParams(dimension_semantics=("arbitrary",)),
    )(h, g, target)


MOD_ROWS = 16


def _mod_fwd(c_rows, w_mod, name):
    L, Dm, n = w_mod.shape

    def body(c_ref, w_ref, o_ref):
        o_ref[0] = _dot3(_silu(c_ref[...]), w_ref[0], NN)

    return pl.pallas_call(
        body, name=name, grid=(L,),
        out_shape=jax.ShapeDtypeStruct((L, MOD_ROWS, n), F32),
        in_specs=[pl.BlockSpec((MOD_ROWS, Dm), lambda l: (0, 0)), pl.BlockSpec((1, Dm, n), lambda l: (l, 0, 0))],
        out_specs=pl.BlockSpec((1, MOD_ROWS, n), lambda l: (l, 0, 0)),
        compiler_params=pltpu.CompilerParams(dimension_semantics=("parallel",)),
    )(c_rows, w_mod)


def _mod_bwd(c_rows_t, dmod, w_mod, name):
    L, Dm, n = w_mod.shape

    def body(ct_ref, d_ref, w_ref, gw_ref, ds_ref):
        dm = d_ref[0]
        gw_ref[0] = _dot3(_silu(ct_ref[...]), dm, NN)
        ds_ref[0] = _dot3(dm[:MOD_ROWS], w_ref[0], NT)

    return pl.pallas_call(
        body, name=name, grid=(L,),
        out_shape=[jax.ShapeDtypeStruct((L, Dm, n), F32), jax.ShapeDtypeStruct((L, MOD_ROWS, Dm), F32)],
        in_specs=[pl.BlockSpec((Dm, LANES), lambda l: (0, 0)), pl.BlockSpec((1, LANES, n), lambda l: (l, 0, 0)),
                  pl.BlockSpec((1, Dm, n), lambda l: (l, 0, 0))],
        out_specs=[pl.BlockSpec((1, Dm, n), lambda l: (l, 0, 0)),
                   pl.BlockSpec((1, MOD_ROWS, Dm), lambda l: (l, 0, 0))],
        compiler_params=pltpu.CompilerParams(dimension_semantics=("parallel",)),
    )(c_rows_t, dmod, w_mod)


def _adam_update(w, g, m, v):
    c1 = 1.0 - ADAM_B1 ** ADAM_STEP
    c2 = 1.0 - ADAM_B2 ** ADAM_STEP
    mn = ADAM_B1 * m + (1.0 - ADAM_B1) * g
    vn = ADAM_B2 * v + (1.0 - ADAM_B2) * (g * g)
    return -ADAM_LR * ((mn / c1) / (jnp.sqrt(vn / c2) + ADAM_EPS) + ADAM_WD * w), mn, vn


def _adamw(w, g, m, v, name):
    R, Cw = w.shape
    tm = _tile(R, ROW_BLOCK, 8)

    def body(w_ref, g_ref, m_ref, v_ref, d_ref, mo_ref, vo_ref):
        d_ref[...], mo_ref[...], vo_ref[...] = _adam_update(w_ref[...], g_ref[...], m_ref[...], v_ref[...])

    spec = pl.BlockSpec((tm, Cw), lambda i: (i, 0))
    return pl.pallas_call(
        body, name=name, grid=(R // tm,),
        out_shape=[jax.ShapeDtypeStruct((R, Cw), F32)] * 3,
        in_specs=[spec] * 4, out_specs=[spec] * 3,
        compiler_params=pltpu.CompilerParams(dimension_semantics=("parallel",)),
    )(w, g, m, v)


def _adamw_recv(w, m, v, recvs, name):
    L, R, n = w.shape
    tm = _tile(R, ROW_BLOCK, 8)
    nblk = R // tm
    parts = [r.reshape(N_DEV, R, n) for r in recvs]

    def body(*refs):
        w_ref, m_ref, v_ref = refs[:3]
        part_refs = refs[3:3 + L]
        g_ref, d_ref, mo_ref, vo_ref, gsum = refs[3 + L:]
        l = pl.program_id(0)
        for ll in range(L):
            @pl.when(l == ll)
            def _(ll=ll):
                acc = part_refs[ll][0].astype(F32)
                for s in range(1, N_DEV):
                    acc = acc + part_refs[ll][s].astype(F32)
                gsum[...] = acc
        g = gsum[...]
        g_ref[0] = g
        d_ref[0], mo_ref[0], vo_ref[0] = _adam_update(w_ref[0], g, m_ref[0], v_ref[0])

    def part_spec(ll):
        return pl.BlockSpec((N_DEV, tm, n), lambda l, i: (0, jnp.where(l == ll, i, jnp.where(l < ll, 0, nblk - 1)), 0))

    spec = pl.BlockSpec((1, tm, n), lambda l, i: (l, i, 0))
    return pl.pallas_call(
        body, name=name, grid=(L, nblk),
        out_shape=[jax.ShapeDtypeStruct((L, R, n), F32)] * 4,
        in_specs=[spec] * 3 + [part_spec(ll) for ll in range(L)], out_specs=[spec] * 4,
        scratch_shapes=[pltpu.VMEM((tm, n), F32)],
        compiler_params=pltpu.CompilerParams(dimension_semantics=("parallel", "parallel")),
    )(w, m, v, *parts)


def _pack(parts, row_mult=8):
    flat, offs, pos = [], [], 0
    for t in parts:
        t = t.reshape(-1).astype(F32)
        size = -(-t.shape[0] // LANES) * LANES
        flat.append(jnp.pad(t, (0, size - t.shape[0])))
        offs.append(pos)
        pos += size
    total = -(-pos // (LANES * row_mult)) * (LANES * row_mult)
    if total > pos:
        flat.append(jnp.zeros((total - pos,), F32))
    return jnp.concatenate(flat).reshape(-1, LANES), offs


def _take(buf, off, shape):
    size = math.prod(shape)
    return buf[..., off:off + size].reshape(buf.shape[:-1] + tuple(shape))


def _rope_tables(T, ctx_rows):
    pos = jnp.arange(T)
    row = (pos // GRID_W).astype(F32)
    col = (pos % GRID_W).astype(F32)
    half = HEAD_DIM // 4
    inv = ROPE_THETA ** (-jnp.arange(0, 2 * half, 2, dtype=F32) / (2 * half))
    ang_r, ang_c = row[:, None] * inv[None, :], col[:, None] * inv[None, :]
    cos = jnp.concatenate([jnp.cos(ang_r)] * 2 + [jnp.cos(ang_c)] * 2, axis=1)
    sin = jnp.concatenate([-jnp.sin(ang_r), jnp.sin(ang_r), -jnp.sin(ang_c), jnp.sin(ang_c)], axis=1)
    cos = jnp.concatenate([jnp.ones((ctx_rows, HEAD_DIM), F32), cos], axis=0)
    sin = jnp.concatenate([jnp.zeros((ctx_rows, HEAD_DIM), F32), sin], axis=0)
    return jnp.tile(cos, (1, 2)), jnp.tile(sin, (1, 2))


def kernel(x, c, ctx, c_ctx, w_mod, b_mod, g_mix, g_ffn, w_ffn_in, w_ffn_out, w_in, q_gain, k_gain, w_sp, b_sp, w_out, w_pw1, b_pw1, w_dw, b_dw, ln_g, ln_b, w_pw2, b_pw2, g_final, loss_target, m_c_ctx, m_w_mod, m_b_mod, m_g_mix, m_g_ffn, m_w_ffn_in, m_w_ffn_out, m_w_in, m_q_gain, m_k_gain, m_w_sp, m_b_sp, m_w_out, m_w_pw1, m_b_pw1, m_w_dw, m_b_dw, m_ln_g, m_ln_b, m_w_pw2, m_b_pw2, m_g_final, v_c_ctx, v_w_mod, v_b_mod, v_g_mix, v_g_ffn, v_w_ffn_in, v_w_ffn_out, v_w_in, v_q_gain, v_k_gain, v_w_sp, v_b_sp, v_w_out, v_w_pw1, v_b_pw1, v_w_dw, v_b_dw, v_ln_g, v_ln_b, v_w_pw2, v_b_pw2, v_g_final):
    weights = dict(c_ctx=c_ctx, w_mod=w_mod, b_mod=b_mod, g_mix=g_mix, g_ffn=g_ffn, w_ffn_in=w_ffn_in,
                   w_ffn_out=w_ffn_out, w_in=w_in, q_gain=q_gain, k_gain=k_gain, w_sp=w_sp, b_sp=b_sp,
                   w_out=w_out, w_pw1=w_pw1, b_pw1=b_pw1, w_dw=w_dw, b_dw=b_dw, ln_g=ln_g, ln_b=ln_b,
                   w_pw2=w_pw2, b_pw2=b_pw2, g_final=g_final)
    moments_m = dict(c_ctx=m_c_ctx, w_mod=m_w_mod, b_mod=m_b_mod, g_mix=m_g_mix, g_ffn=m_g_ffn,
                     w_ffn_in=m_w_ffn_in, w_ffn_out=m_w_ffn_out, w_in=m_w_in, q_gain=m_q_gain,
                     k_gain=m_k_gain, w_sp=m_w_sp, b_sp=m_b_sp, w_out=m_w_out, w_pw1=m_w_pw1,
                     b_pw1=m_b_pw1, w_dw=m_w_dw, b_dw=m_b_dw, ln_g=m_ln_g, ln_b=m_ln_b, w_pw2=m_w_pw2,
                     b_pw2=m_b_pw2, g_final=m_g_final)
    moments_v = dict(c_ctx=v_c_ctx, w_mod=v_w_mod, b_mod=v_b_mod, g_mix=v_g_mix, g_ffn=v_g_ffn,
                     w_ffn_in=v_w_ffn_in, w_ffn_out=v_w_ffn_out, w_in=v_w_in, q_gain=v_q_gain,
                     k_gain=v_k_gain, w_sp=v_w_sp, b_sp=v_b_sp, w_out=v_w_out, w_pw1=v_w_pw1,
                     b_pw1=v_b_pw1, w_dw=v_w_dw, b_dw=v_b_dw, ln_g=v_ln_g, ln_b=v_ln_b, w_pw2=v_w_pw2,
                     b_pw2=v_b_pw2, g_final=v_g_final)
    names = list(weights)

    T, C = x.shape[1], ctx.shape[1]
    Dm = D_MODEL
    me = 4 * lax.axis_index("x") + 2 * lax.axis_index("y") + lax.axis_index("c")
    h0 = x[0]
    ctx2 = ctx[0]
    target = loss_target[0]

    small_sharded = (("w_dw", w_dw[0]), ("b_pw1", b_pw1), ("b_dw", b_dw), ("ln_g", ln_g), ("ln_b", ln_b),
                     ("b_pw2", b_pw2))
    buf1, offs1 = _pack([c] + [t for _, t in small_sharded])
    got1, W_in, W_out = _all_gather([buf1, w_in[0].astype(BF16), w_out[0].astype(BF16)], "gather_cond", False)
    got1 = got1.reshape(N_DEV, -1)
    c_all = _take(got1, offs1[0], (Dm,))
    full_small = {}
    for (nm, t), off in zip(small_sharded, offs1[1:]):
        seg = _take(got1, off, t.shape)
        full_small[nm] = jnp.moveaxis(seg, 0, -2).reshape(t.shape[:-1] + (N_DEV * t.shape[-1],))
    w_dw_f, b_pw1_f = full_small["w_dw"], full_small["b_pw1"]
    b_dw_f, ln_g_f, ln_b_f, b_pw2_f = (full_small[k] for k in ("b_dw", "ln_g", "ln_b", "b_pw2"))

    c_rows = jnp.concatenate([c_all, c_ctx[None, :], jnp.zeros((MOD_ROWS - N_DEV - 1, Dm), F32)], axis=0)
    mod_part = _mod_fwd(c_rows, w_mod, "mod_fwd")
    n_mod = w_mod.shape[2]
    got2 = _all_gather([mod_part.reshape(-1, LANES)], "gather_mod", True)[0]
    mod_all = got2.reshape(N_DEV, 2, MOD_ROWS, n_mod).transpose(1, 2, 0, 3).reshape(2, MOD_ROWS, N_DEV * n_mod)
    mod_all = mod_all + b_mod[:, None, :]
    my_mod = lax.dynamic_index_in_dim(mod_all, me, axis=1, keepdims=False)
    sh1, sc1, gt1, sh2, sc2, gt2 = ([my_mod[l:l + 1, k * Dm:(k + 1) * Dm] for l in range(2)] for k in range(6))
    csh1 = mod_all[0, N_DEV:N_DEV + 1, 0:Dm]
    csc1 = mod_all[0, N_DEV:N_DEV + 1, Dm:2 * Dm]

    behind = got2[0:1, 0:1] * 0.0
    gather_groups = [[w_ffn_in[0], w_ffn_out[0]], [w_pw1[0], w_pw2[0]], [w_ffn_in[1], w_ffn_out[1]]]
    gathers = [_push_begin([(t + behind).astype(BF16) for t in grp], True, f"gather_start{k}")
               for k, grp in enumerate(gather_groups)]
    started = sum(h[4][0:1, 0:1] for h in gathers)

    def gathered(k, after):
        return _push_end(gathers[k], after, f"gather_wait{k}")[1]

    def ffn_weights(k, after):
        wi, wo = gathered(k, after)
        return wi.reshape(N_DEV, Dm, FF_SHARD), wo.reshape(N_DEV // 2, FF_SHARD, Dm)

    def col_gathered(t, n):
        return t.reshape(N_DEV, Dm, n).transpose(1, 0, 2).reshape(Dm, N_DEV * n)

    W_ffi, W_ffo = [None, None], [None, None]

    g_mix_r = [g_mix[l:l + 1] for l in range(2)]
    g_ffn_r = [g_ffn[l:l + 1] for l in range(2)]
    g_fin = g_final[None, :]

    cos, sin = _rope_tables(T, C)
    qg = jnp.tile(q_gain, (1, 2))
    kg = jnp.tile(k_gain, (1, 2))
    lane_head = jnp.arange(LANES) // HEAD_DIM
    bd = (lane_head[:, None] == lane_head[None, :]).astype(BF16)
    w_sp0 = w_sp[0]
    w_spt0 = w_sp0.transpose(0, 2, 1)
    b_spt0 = b_sp[0].T

    XM = _norm_mod_fwd_cat(ctx2, h0, g_mix_r[0], csc1, csh1, sc1[0] + started, sh1[0], "norm_mix0")
    W_in = col_gathered(W_in, IN_WIDTH // N_DEV)
    P = _mm(XM, W_in, "nn", "in_proj", tn=IN_WIDTH)
    qh, kpad, vpad, kt, vt, ao = _mix_prep_fwd(P, C, cos, sin, qg, kg, bd, w_sp0, b_spt0, "mix_prep")
    ao, lse = _attn_fwd(qh, kpad, vt, ao, C, "attn_fwd")
    h1, y0 = _mm(ao, W_out, "nn", "out_proj", res=h0, gate=gt1[0], raw_out=True)

    def ffn_fwd(h_in, l):
        xf = _norm_mod_fwd(h_in, g_ffn_r[l], sc2[l], sh2[l], f"norm_ffn{l}")
        W_ffi[l], W_ffo[l] = ffn_weights(2 * l, xf)
        gu, act = _ffn_in_swiglu(xf, W_ffi[l], f"ffn_in{l}")
        h_out, f = _mm_sum_shards(act, W_ffo[l], "nn", f"ffn_out{l}", res=h_in, gate=gt2[l], raw_out=True)
        return h_out, (xf, gu, act, f)

    h2, saved_ffn0 = ffn_fwd(h1, 0)

    xm1 = _norm_mod_fwd(h2, g_mix_r[1], sc1[1], sh1[1], "norm_mix1")
    W_pw1, W_pw2 = gathered(1, xm1)
    W_pw1 = col_gathered(W_pw1, 2 * Dm // N_DEV)
    ag = _mm(xm1, W_pw1, "nn", "pw1", bias=b_pw1_f)
    hg = _glu_fwd(ag, "glu")
    hd = _conv_fwd(hg, w_dw_f, b_dw_f, "conv")
    hs = _ln_silu_fwd(hd, ln_g_f, ln_b_f, "ln_silu")
    h3, y1 = _mm(hs, W_pw2, "nn", "pw2", bias=b_pw2_f, res=h2, gate=gt1[1], raw_out=True)
    h4, saved_ffn1 = ffn_fwd(h3, 1)

    dh4, sq_err, dg_final = _final_fwd_bwd(h4, g_fin, target, "loss_head")
    loss_local = (0.5 / Dm) * sq_err[0, 0:1]

    def col_shards(g, n):
        return g.reshape(Dm, N_DEV, n).transpose(1, 0, 2).reshape(N_DEV * Dm, n)

    def exchange_begin(k, parts):
        return _push_begin(parts, False, f"exchange_start{k}")

    def zero_of(handle):
        return handle[4][0:1, 0:1]

    def ffn_bwd(dh_out, h_in, saved, l, zero):
        xf, gu, act, f = saved
        df, dgt, _ = _gate_bwd(dh_out, f, gt2[l] + zero, f"gate_ffn_bwd{l}")
        dw_out = _mm_tn_shard_rows(act, df, f"ffn_out_dw{l}", BF16)
        dgu = _ffn_out_dx_swiglu(df, W_ffo[l], gu, f"ffn_out_dx{l}").reshape(N_DEV, T, FF_SHARD)
        dw_in = _mm_tn_shard_cols(xf, dgu, f"ffn_in_dw{l}", BF16)
        dxf = _mm_sum_shards(dgu, W_ffi[l], "nt", f"ffn_in_dx{l}", tm=256)
        dh_in, da, dsh = _norm_mod_bwd(h_in, g_ffn_r[l], sc2[l], dxf, dh_out, f"norm_ffn_bwd{l}")
        return dh_in, dw_in, dw_out, (dsh, da * g_ffn_r[l], dgt), da * (1.0 + sc2[l])

    dh3, dW_ffi1, dW_ffo1, dmod_ffn1, dg_ffn1 = ffn_bwd(dh4, h3, saved_ffn1, 1, 0.0)
    ex0 = exchange_begin(0, [dW_ffi1.reshape(N_DEV * Dm, FF_SHARD), dW_ffo1.reshape(D_FF, Dm)])

    dy1, dgt1_1, db_pw2 = _gate_bwd(dh3, y1, gt1[1] + zero_of(ex0), "gate_conv_bwd")
    dW_pw2 = _mm(hs, dy1, "tn", "pw2_dw", BF16, tk=2048)
    dhs = _mm(dy1, W_pw2, "nt", "pw2_dx")
    dhd, dln_g, dln_b, db_dw = _ln_silu_bwd(dhs, hd, ln_g_f, ln_b_f, "ln_silu_bwd")
    dhg, dw_dw = _conv_bwd(dhd, hg, w_dw_f, "conv_bwd")
    dag, db_pw1 = _glu_bwd(ag, dhg, "glu_bwd")
    dW_pw1 = _mm(xm1, dag, "tn", "pw1_dw", BF16, tk=2048)
    dxm1 = _mm(dag, W_pw1, "nt", "pw1_dx", tk=2048)
    dh2, da, dsh = _norm_mod_bwd(h2, g_mix_r[1], sc1[1], dxm1, dh3, "norm_mix1_bwd")
    dmod_mix1 = (dsh, da * g_mix_r[1], dgt1_1)
    dg_mix1 = da * (1.0 + sc1[1])

    ex1 = exchange_begin(1, [col_shards(dW_pw1, 2 * Dm // N_DEV), dW_pw2])
    dh1, dW_ffi0, dW_ffo0, dmod_ffn0, dg_ffn0 = ffn_bwd(dh2, h1, saved_ffn0, 0, zero_of(ex1))
    ex2 = exchange_begin(2, [dW_ffi0.reshape(N_DEV * Dm, FF_SHARD), dW_ffo0.reshape(D_FF, Dm)])

    dy0, dgt1_0, _ = _gate_bwd(dh1, y0, gt1[0] + zero_of(ex2), "gate_mix_bwd")
    dW_out = _mm(ao, dy0, "tn", "out_proj_dw", BF16, tk=2048)
    dao = _mm(dy0, W_out, "nt", "out_proj_dx")
    dq, f_acc = _attn_bwd(qh, dao, ao, lse, kpad, vpad, kt, C, "attn_bwd")
    dP, dqg, dkg, dw_sp0, db_spt0 = _mix_prep_bwd(P, dq, f_acc, dao, C, cos, sin, qg, kg, bd, w_sp0, w_spt0,
                                                  b_spt0, "mix_prep_bwd")
    dW_in = _mm(XM, dP, "tn", "in_proj_dw", BF16, tn=896, tk=2176)
    ex3 = exchange_begin(3, [col_shards(dW_in, IN_WIDTH // N_DEV), dW_out])
    dXM = _mm(dP, W_in, "nt", "in_proj_dx", tk=IN_WIDTH)
    dh0, da, dsh = _norm_mod_bwd(h0, g_mix_r[0], sc1[0] + zero_of(ex3), dXM, dh1, "norm_mix0_bwd", dxm_row_off=C)
    _, dac, dcsh = _norm_mod_bwd(ctx2, g_mix_r[0], csc1, dXM, None, "norm_ctx_bwd")
    dmod_mix0 = (dsh, da * g_mix_r[0], dgt1_0)
    dg_mix0 = da * (1.0 + sc1[0]) + dac * (1.0 + csc1)
    dcmod = jnp.concatenate([dcsh, dac * g_mix_r[0]], axis=1)

    dmod_mine = jnp.stack([jnp.concatenate(dmod_mix0 + dmod_ffn0, axis=1)[0],
                           jnp.concatenate(dmod_mix1 + dmod_ffn1, axis=1)[0]])

    small_grads = [
        ("loss", loss_local), ("g_final", dg_final), ("g_mix", jnp.concatenate([dg_mix0, dg_mix1])),
        ("g_ffn", jnp.concatenate([dg_ffn0, dg_ffn1])),
        ("q_gain", dqg[:, :HEAD_DIM] + dqg[:, HEAD_DIM:]), ("k_gain", dkg[:, :HEAD_DIM] + dkg[:, HEAD_DIM:]),
        ("w_sp", dw_sp0[None]), ("b_sp", db_spt0.T[None]), ("b_pw1", db_pw1), ("w_dw", dw_dw[None]),
        ("b_dw", db_dw), ("ln_g", dln_g), ("ln_b", dln_b), ("b_pw2", db_pw2), ("dcmod", dcmod),
        ("dmod", dmod_mine),
    ]
    buf3, offs3 = _pack([t for _, t in small_grads])
    got3 = _all_gather([buf3], "gather_small_grads", True)[0].reshape(N_DEV, buf3.shape[0], LANES)
    sum3 = _sum_devices(got3, "sum_small_grads").reshape(-1)
    off3 = {nm: off for (nm, _), off in zip(small_grads, offs3)}
    shape3 = {nm: t.shape for nm, t in small_grads}

    def summed(nm):
        return _take(sum3, off3[nm], shape3[nm])

    loss = summed("loss")[0]
    dcmod_sum = summed("dcmod")
    dmod_rows = _take(got3.reshape(N_DEV, -1), off3["dmod"], (2, 6 * Dm)).transpose(1, 0, 2)
    ctx_row = jnp.concatenate([jnp.pad(dcmod_sum, ((0, 0), (0, 4 * Dm))), jnp.zeros((1, 6 * Dm), F32)])
    dmod_all = jnp.concatenate([dmod_rows, ctx_row[:, None, :],
                                jnp.zeros((2, LANES - N_DEV - 1, 6 * Dm), F32)], axis=1)
    grads = {}
    grads["b_mod"] = summed("dmod") + ctx_row
    dmod_shard = lax.dynamic_slice_in_dim(dmod_all, me * n_mod, n_mod, axis=2)
    c_rows_t = jnp.pad(c_rows.T, ((0, 0), (0, LANES - MOD_ROWS)))
    grads["w_mod"], ds_part = _mod_bwd(c_rows_t, dmod_shard, w_mod, "mod_bwd")

    buf4, _ = _pack([ds_part[0, N_DEV]])
    got4 = _all_gather([buf4], "gather_c_ctx_grad", True)[0].reshape(N_DEV, buf4.shape[0], LANES)
    ds_ctx = _sum_devices(got4, "sum_c_ctx_grad").reshape(-1)[:Dm]
    grads["c_ctx"] = ds_ctx * _dsilu(c_ctx)

    for nm in ("g_final", "g_mix", "g_ffn", "q_gain", "k_gain", "w_sp", "b_sp"):
        grads[nm] = summed(nm).reshape(weights[nm].shape)
    for nm in ("b_pw1", "w_dw", "b_dw", "ln_g", "ln_b", "b_pw2"):
        n_loc = weights[nm].shape[-1]
        grads[nm] = lax.dynamic_slice_in_dim(summed(nm), me * n_loc, n_loc, axis=-1).reshape(weights[nm].shape)

    delta, new_m, new_v = {}, {}, {}
    shp = w_mod.shape
    outs = _adamw(w_mod.reshape(-1, shp[-1]), grads["w_mod"].reshape(-1, shp[-1]),
                  m_w_mod.reshape(-1, shp[-1]), v_w_mod.reshape(-1, shp[-1]), "adamw_w_mod")
    delta["w_mod"], new_m["w_mod"], new_v["w_mod"] = (o.reshape(shp) for o in outs)
    big_names = ("w_mod", "w_ffn_in", "w_ffn_out", "w_in", "w_out", "w_pw1", "w_pw2")
    small_names = [nm for nm in names if nm not in big_names]
    packs = [_pack([src[nm] for nm in small_names]) for src in (weights, grads, moments_m, moments_v)]
    offs_s = packs[0][1]
    outs = _adamw(*[pk[0] for pk in packs], "adamw_small")
    for o, dst in zip(outs, (delta, new_m, new_v)):
        o = o.reshape(-1)
        for nm, off in zip(small_names, offs_s):
            dst[nm] = _take(o, off, weights[nm].shape)

    def exchanged(k, handle):
        return _push_end(handle, outs[0], f"exchange_wait{k}")[1]

    r_ffi1, r_ffo1 = exchanged(0, ex0)
    r_pw1, r_pw2 = exchanged(1, ex1)
    r_ffi0, r_ffo0 = exchanged(2, ex2)
    r_in, r_out = exchanged(3, ex3)
    for nm, parts in (("w_ffn_in", [r_ffi0, r_ffi1]), ("w_ffn_out", [r_ffo0, r_ffo1]), ("w_pw1", [r_pw1]),
                      ("w_pw2", [r_pw2]), ("w_in", [r_in]), ("w_out", [r_out])):
        grads[nm], delta[nm], new_m[nm], new_v[nm] = _adamw_recv(
            weights[nm], moments_m[nm], moments_v[nm], parts, f"adamw_{nm}")

    return (loss, dh0[None], *[grads[n] for n in names], *[delta[n] for n in names],
            *[new_m[n] for n in names], *[new_v[n] for n in names])
```

```python
import math

import jax
import jax.numpy as jnp
from jax import lax
from jax.experimental import pallas as pl
from jax.experimental.pallas import tpu as pltpu

F32 = jnp.float32
BF16 = jnp.bfloat16
MESH = pl.DeviceIdType.MESH

N_DEV = 8
D_MODEL = 1024
EPS = 1e-6
HEAD_DIM = 64
ATTN_WIDTH = 512
KV_WIDTH = 128
SG_WIDTH = 512
N_SG_GROUPS = 4
CHUNK = 128
IN_WIDTH = 1792
D_FF = 2816
FF_SHARD = 2 * D_FF // N_DEV
CONV_WIDTH = 31
CONV_HALO = 16
GRID_W = 64
ROPE_THETA = 10000.0
LANES = 128
SUBLANES = 8
ROW_BLOCK = 256
ADAM_LR, ADAM_B1, ADAM_B2, ADAM_EPS, ADAM_WD, ADAM_STEP = 0.001, 0.9, 0.999, 1e-08, 0.01, 10


def _tile(n, target, mult=LANES):
    best = None
    for t in range(mult, min(n, target) + 1, mult):
        if n % t == 0:
            best = t
    return best if best is not None else n


def _sigmoid(x):
    return 1.0 / (1.0 + jnp.exp(-x))


def _silu(x):
    return x * _sigmoid(x)


def _dsilu(x):
    s = _sigmoid(x)
    return s * (1.0 + x * (1.0 - s))


_GELU_K = math.sqrt(2.0 / math.pi)


def _gelu(x):
    return 0.5 * x * (1.0 + jnp.tanh(_GELU_K * (x + 0.044715 * x * x * x)))


def _dgelu(x):
    t = jnp.tanh(_GELU_K * (x + 0.044715 * x * x * x))
    return 0.5 * (1.0 + t) + 0.5 * x * (1.0 - t * t) * _GELU_K * (1.0 + 3.0 * 0.044715 * x * x)


def _split_bf16(x):
    hi = x.astype(BF16)
    lo = (x - hi.astype(F32)).astype(BF16)
    return hi, lo


def _dot(a, b, dims):
    return lax.dot_general(a, b, (dims, ((), ())), preferred_element_type=F32)


def _dot3(a, b, dims):
    ah, al = _split_bf16(a)
    bh, bl = _split_bf16(b)
    return _dot(ah, bh, dims) + _dot(ah, bl, dims) + _dot(al, bh, dims)


NN = ((1,), (0,))
NT = ((1,), (1,))
TN = ((0,), (0,))


def _all_gather(xs, name, in_vmem):
    n_arr = len(xs)

    def body(*refs):
        x_refs, out_refs = refs[:n_arr], refs[n_arr:2 * n_arr]
        send_sems, recv_sems, local_sems = refs[2 * n_arr:]
        x, y, c = lax.axis_index("x"), lax.axis_index("y"), lax.axis_index("c")
        me, sibling = (x, y, c), (x, y, 1 - c)
        chips = [(1 - x, y), (x, 1 - y), (1 - x, 1 - y)]

        def rows(a, px, py, pc):
            m_per = xs[a].shape[0]
            return out_refs[a].at[pl.ds((4 * px + 2 * py + pc) * m_per, m_per), :]

        def copy(a, k, block, to, src=None):
            return pltpu.make_async_remote_copy(
                src_ref=rows(a, *block) if src is None else src,
                dst_ref=rows(a, *block),
                send_sem=send_sems.at[7 * a + k],
                recv_sem=recv_sems.at[7 * a + k],
                device_id=to,
                device_id_type=MESH,
            )

        mine, first, passed = [], [], []
        for a in range(n_arr):
            mine.append(pltpu.make_async_copy(x_refs[a], rows(a, *me), local_sems.at[a]))
            mine[-1].start()
            first.append(copy(a, 0, me, sibling, src=x_refs[a]))
            first += [copy(a, 1 + j, me, (*chip, c), src=x_refs[a]) for j, chip in enumerate(chips)]
        for cp in first:
            cp.start()
        for a in range(n_arr):
            for j, chip in enumerate(chips):
                copy(a, 1 + j, (*chip, c), me).wait_recv()
                passed.append(copy(a, 4 + j, (*chip, c), sibling))
                passed[-1].start()
        for a in range(n_arr):
            copy(a, 0, sibling, me).wait_recv()
            for j, chip in enumerate(chips):
                copy(a, 4 + j, (*chip, 1 - c), me).wait_recv()
        for cp in first + passed:
            cp.wait_send()
        for cp in mine:
            cp.wait()

    space = pltpu.VMEM if in_vmem else pl.ANY
    return pl.pallas_call(
        body,
        name=name,
        out_shape=[jax.ShapeDtypeStruct((N_DEV * t.shape[0], t.shape[1]), t.dtype) for t in xs],
        in_specs=[pl.BlockSpec(memory_space=space)] * n_arr,
        out_specs=[pl.BlockSpec(memory_space=space)] * n_arr,
        scratch_shapes=[
            pltpu.SemaphoreType.DMA((7 * n_arr,)),
            pltpu.SemaphoreType.DMA((7 * n_arr,)),
            pltpu.SemaphoreType.DMA((n_arr,)),
        ],
    )(*xs)


def _shard_exchange(gs, name):
    n_arr = len(gs)

    def body(*refs):
        g_refs, r_refs = refs[:n_arr], refs[n_arr:2 * n_arr]
        send_sems, recv_sems, local_sems = refs[2 * n_arr:]
        x, y, c = lax.axis_index("x"), lax.axis_index("y"), lax.axis_index("c")
        me = 4 * x + 2 * y + c

        def rows(ref, a, idx):
            m_per = gs[a].shape[0] // N_DEV
            return ref.at[pl.ds(idx * m_per, m_per), :]

        mine, sends, recvs = [], [], []
        for a in range(n_arr):
            mine.append(pltpu.make_async_copy(rows(g_refs[a], a, me), rows(r_refs[a], a, me), local_sems.at[a]))
            mine[-1].start()
            for k in range(1, N_DEV):
                px = 1 - x if (k >> 2) & 1 else x
                py = 1 - y if (k >> 1) & 1 else y
                pc = 1 - c if k & 1 else c
                peer = 4 * px + 2 * py + pc
                sem = 7 * a + k - 1
                sends.append(pltpu.make_async_remote_copy(
                    src_ref=rows(g_refs[a], a, peer), dst_ref=rows(r_refs[a], a, me),
                    send_sem=send_sems.at[sem], recv_sem=recv_sems.at[sem],
                    device_id=(px, py, pc), device_id_type=MESH))
                recvs.append(pltpu.make_async_remote_copy(
                    src_ref=rows(g_refs[a], a, me), dst_ref=rows(r_refs[a], a, peer),
                    send_sem=send_sems.at[sem], recv_sem=recv_sems.at[sem],
                    device_id=(px, py, pc), device_id_type=MESH))
        for cp in sends:
            cp.start()
        for cp in recvs:
            cp.wait_recv()
        for cp in sends:
            cp.wait_send()
        for cp in mine:
            cp.wait()

    return pl.pallas_call(
        body,
        name=name,
        out_shape=[jax.ShapeDtypeStruct(t.shape, t.dtype) for t in gs],
        in_specs=[pl.BlockSpec(memory_space=pl.ANY)] * n_arr,
        out_specs=[pl.BlockSpec(memory_space=pl.ANY)] * n_arr,
        scratch_shapes=[
            pltpu.SemaphoreType.DMA((7 * n_arr,)),
            pltpu.SemaphoreType.DMA((7 * n_arr,)),
            pltpu.SemaphoreType.DMA((n_arr,)),
        ],
    )(*gs)


HBM_SPEC = pl.BlockSpec(memory_space=pltpu.HBM)
SEM_SPEC = pl.BlockSpec(memory_space=pltpu.SEMAPHORE)
DATAFLOW_EFFECT = pltpu.SideEffectType.DATAFLOW_SIDE_EFFECTING


def _peers(x, y, c):
    for k in range(1, N_DEV):
        px = 1 - x if (k >> 2) & 1 else x
        py = 1 - y if (k >> 1) & 1 else y
        pc = 1 - c if k & 1 else c
        yield k - 1, (px, py, pc), 4 * px + 2 * py + pc


def _push_copies(src_refs, land_refs, send_sems, recv_sems, shapes, whole_src):
    x, y, c = lax.axis_index("x"), lax.axis_index("y"), lax.axis_index("c")
    me = 4 * x + 2 * y + c
    for a, (m_per, _) in enumerate(shapes):
        def block(ref, idx, m_per=m_per):
            return ref.at[pl.ds(idx * m_per, m_per), :]

        for k, peer, pidx in _peers(x, y, c):
            src = src_refs[a] if whole_src else block(src_refs[a], pidx)
            sems = dict(send_sem=send_sems.at[N_DEV * a + k], recv_sem=recv_sems.at[N_DEV * a + k],
                        device_id=peer, device_id_type=MESH)
            yield (pltpu.make_async_remote_copy(src_ref=src, dst_ref=block(land_refs[a], me), **sems),
                   pltpu.make_async_remote_copy(src_ref=src, dst_ref=block(land_refs[a], pidx), **sems))


def _own_copies(src_refs, land_refs, recv_sems, shapes, whole_src):
    me = 4 * lax.axis_index("x") + 2 * lax.axis_index("y") + lax.axis_index("c")
    for a, (m_per, _) in enumerate(shapes):
        mine = pl.ds(me * m_per, m_per)
        src = src_refs[a] if whole_src else src_refs[a].at[mine, :]
        yield pltpu.make_async_copy(src, land_refs[a].at[mine, :], recv_sems.at[N_DEV * a + N_DEV - 1])


def _push_begin(srcs, whole_src, name):
    n_arr = len(srcs)
    shapes = [(t.shape[0] if whole_src else t.shape[0] // N_DEV, t.shape[1]) for t in srcs]
    lands = [lax.empty((N_DEV * m, n), t.dtype) for (m, n), t in zip(shapes, srcs)]

    def body(*refs):
        src_refs, land_refs = refs[:n_arr], refs[n_arr:2 * n_arr]
        send_sems, recv_sems = refs[2 * n_arr], refs[2 * n_arr + 1]
        token = refs[-1]
        for outgoing, _ in _push_copies(src_refs, land_refs, send_sems, recv_sems, shapes, whole_src):
            outgoing.start()
        for own in _own_copies(src_refs, land_refs, recv_sems, shapes, whole_src):
            own.start()
        token[...] = jnp.zeros_like(token)

    operands = [pltpu.with_memory_space_constraint(t, pltpu.HBM) for t in list(srcs) + lands]
    outs = pl.pallas_call(
        body, name=name,
        out_shape=(pltpu.SemaphoreType.DMA((N_DEV * n_arr,)), pltpu.SemaphoreType.DMA((N_DEV * n_arr,)),
                   *[pltpu.HBM(t.shape, t.dtype) for t in operands],
                   jax.ShapeDtypeStruct((SUBLANES, LANES), F32)),
        in_specs=[HBM_SPEC] * (2 * n_arr),
        out_specs=(SEM_SPEC, SEM_SPEC, *[HBM_SPEC] * (2 * n_arr), pl.BlockSpec(memory_space=pltpu.VMEM)),
        input_output_aliases={i: 2 + i for i in range(2 * n_arr)},
        compiler_params=pltpu.CompilerParams(has_side_effects=DATAFLOW_EFFECT),
    )(*operands)
    return outs[0], outs[1], list(outs[2:2 + n_arr]), list(outs[2 + n_arr:2 + 2 * n_arr]), outs[-1], whole_src


def _push_end(handle, after, name):
    send_sems, recv_sems, srcs, lands, _, whole_src = handle
    n_arr = len(srcs)
    shapes = [(t.shape[0] // N_DEV, t.shape[1]) for t in lands]

    def body(*refs):
        src_refs, land_refs = refs[:n_arr], refs[n_arr:2 * n_arr]
        send_sems_ref, recv_sems_ref = refs[2 * n_arr], refs[2 * n_arr + 1]
        for outgoing, incoming in _push_copies(src_refs, land_refs, send_sems_ref, recv_sems_ref, shapes, whole_src):
            outgoing.wait_send()
            incoming.wait_recv()
        for own in _own_copies(src_refs, land_refs, recv_sems_ref, shapes, whole_src):
            own.wait()

    outs = pl.pallas_call(
        body, name=name,
        out_shape=tuple(pltpu.HBM(t.shape, t.dtype) for t in srcs + lands),
        in_specs=[HBM_SPEC] * (2 * n_arr) + [SEM_SPEC, SEM_SPEC, pl.BlockSpec(memory_space=pl.ANY)],
        out_specs=tuple([HBM_SPEC] * (2 * n_arr)),
        input_output_aliases={i: i for i in range(2 * n_arr)},
        compiler_params=pltpu.CompilerParams(has_side_effects=DATAFLOW_EFFECT),
    )(*srcs, *lands, send_sems, recv_sems, after)
    return list(outs[:n_arr]), list(outs[n_arr:])


def _sum_devices(r, name, rows_per_step=ROW_BLOCK):
    _, m, n = r.shape
    tm = _tile(m, rows_per_step, 8)

    def body(r_ref, o_ref):
        acc = r_ref[0].astype(F32)
        for s in range(1, N_DEV):
            acc = acc + r_ref[s].astype(F32)
        o_ref[...] = acc

    return pl.pallas_call(
        body,
        name=name,
        grid=(m // tm,),
        out_shape=jax.ShapeDtypeStruct((m, n), F32),
        in_specs=[pl.BlockSpec((N_DEV, tm, n), lambda i: (0, i, 0))],
        out_specs=pl.BlockSpec((tm, n), lambda i: (i, 0)),
        compiler_params=pltpu.CompilerParams(dimension_semantics=("parallel",)),
    )(r)


def _get(ref):
    return ref[0] if len(ref.shape) == 3 else ref[...]


def _put(ref, val):
    if len(ref.shape) == 3:
        ref[0] = val
    else:
        ref[...] = val


def _mm_call(name, a, b, a_spec, b_spec, out_sds, o_spec, grid, dims, acc_shape, bias=None,
             res=None, gate=None, raw_out=False, vec_spec=None):
    nk = grid[2]
    operands, in_specs = [a, b], [a_spec, b_spec]
    if bias is not None:
        operands.append(bias)
        in_specs.append(vec_spec)
    if res is not None:
        operands += [res, gate]
        in_specs += [o_spec, vec_spec]
    out_shape, out_specs = [out_sds], [o_spec]
    if raw_out:
        out_shape.append(jax.ShapeDtypeStruct(out_sds.shape, F32))
        out_specs.append(o_spec)

    def body(*refs):
        it = iter(refs)
        a_ref, b_ref = next(it), next(it)
        bias_ref = next(it) if bias is not None else None
        res_ref, gate_ref = (next(it), next(it)) if res is not None else (None, None)
        o_ref = next(it)
        raw_ref = next(it) if raw_out else None
        acc = next(it) if nk > 1 else None
        k = pl.program_id(2)
        part = _dot(_get(a_ref).astype(BF16), _get(b_ref).astype(BF16), dims)

        def finish(y):
            if bias_ref is not None:
                y = y + bias_ref[...]
            if raw_ref is not None:
                raw_ref[...] = y
            if res_ref is not None:
                y = res_ref[...] + gate_ref[...] * y
            _put(o_ref, y.astype(out_sds.dtype))

        if nk == 1:
            finish(part)
        else:
            @pl.when(k == 0)
            def _():
                acc[...] = part

            @pl.when(k > 0)
            def _():
                acc[...] += part

            @pl.when(k == nk - 1)
            def _():
                finish(acc[...])

    outs = pl.pallas_call(
        body,
        name=name,
        grid=grid,
        out_shape=out_shape,
        in_specs=in_specs,
        out_specs=out_specs,
        scratch_shapes=[pltpu.VMEM(acc_shape, F32)] if nk > 1 else [],
        compiler_params=pltpu.CompilerParams(dimension_semantics=("parallel", "parallel", "arbitrary")),
    )(*operands)
    return outs if raw_out else outs[0]


def _mm(a, b, mode, name, out_dtype=F32, bias=None, res=None, gate=None, raw_out=False,
        tm=512, tn=1024, tk=1024, a_row_off=0):
    if mode == "nn":
        K, N = b.shape
        M = a.shape[0] - a_row_off
    elif mode == "nt":
        N, K = b.shape
        M = a.shape[0] - a_row_off
    else:
        (K, M), N = a.shape, b.shape[1]
    tm, tn, tk = _tile(M, tm), _tile(N, tn), _tile(K, tk)
    off = a_row_off // tm
    dims = {"nn": NN, "nt": NT, "tn": TN}[mode]
    a_spec = (pl.BlockSpec((tk, tm), lambda i, j, k: (k, i)) if mode == "tn"
              else pl.BlockSpec((tm, tk), lambda i, j, k: (i + off, k)))
    b_spec = (pl.BlockSpec((tn, tk), lambda i, j, k: (j, k)) if mode == "nt"
              else pl.BlockSpec((tk, tn), lambda i, j, k: (k, j)))
    return _mm_call(name, a, b, a_spec, b_spec, jax.ShapeDtypeStruct((M, N), out_dtype),
                    pl.BlockSpec((tm, tn), lambda i, j, k: (i, j)), (M // tm, N // tn, K // tk), dims,
                    (tm, tn), bias, res, gate, raw_out, pl.BlockSpec((1, tn), lambda i, j, k: (0, j)))


def _mm_to_shards(a, b3, mode, name, out_dtype, tm=512):
    M, K = a.shape
    S = b3.shape[0]
    n = b3.shape[2] if mode == "nn" else b3.shape[1]
    tm = _tile(M, tm)
    return _mm_call(name, a, b3, pl.BlockSpec((tm, K), lambda i, j, k: (i, 0)),
                    pl.BlockSpec((1,) + b3.shape[1:], lambda i, j, k: (j, 0, 0)),
                    jax.ShapeDtypeStruct((S, M, n), out_dtype),
                    pl.BlockSpec((1, tm, n), lambda i, j, k: (j, i, 0)), (M // tm, S, 1),
                    NN if mode == "nn" else NT, (tm, n))


def _mm_over_shards(a3, b3, mode, name, out_dtype=F32, res=None, gate=None, raw_out=False, tm=512, tn=1024):
    S, M, kk = a3.shape
    N = b3.shape[2] if mode == "nn" else b3.shape[1]
    tm, tn = _tile(M, tm), _tile(N, tn)
    b_spec = (pl.BlockSpec((1, kk, tn), lambda i, j, k: (k, 0, j)) if mode == "nn"
              else pl.BlockSpec((1, tn, kk), lambda i, j, k: (k, j, 0)))
    return _mm_call(name, a3, b3, pl.BlockSpec((1, tm, kk), lambda i, j, k: (k, i, 0)), b_spec,
                    jax.ShapeDtypeStruct((M, N), out_dtype), pl.BlockSpec((tm, tn), lambda i, j, k: (i, j)),
                    (M // tm, N // tn, S), NN if mode == "nn" else NT, (tm, tn), None, res, gate, raw_out,
                    pl.BlockSpec((1, tn), lambda i, j, k: (0, j)))


def _mm_sum_shards(a3, b3, mode, name, out_dtype=F32, res=None, gate=None, raw_out=False, tm=512):
    S, M, kk = a3.shape
    N = b3.shape[2] if mode == "nn" else b3.shape[1]
    tm = _tile(M, tm)
    dims = NN if mode == "nn" else NT
    has_res = res is not None

    def body(*refs):
        it = iter(refs)
        a_ref, b_ref = next(it), next(it)
        res_ref, gate_ref = (next(it), next(it)) if has_res else (None, None)
        o_ref = next(it)
        raw_ref = next(it) if raw_out else None
        y = _dot(a_ref[0], b_ref[0], dims)
        for s in range(1, S):
            y = y + _dot(a_ref[s], b_ref[s], dims)
        if raw_ref is not None:
            raw_ref[...] = y
        if has_res:
            y = res_ref[...] + gate_ref[...] * y
        o_ref[...] = y.astype(out_dtype)

    tile = pl.BlockSpec((tm, N), lambda i: (i, 0))
    operands = [a3, b3] + ([res, gate] if has_res else [])
    in_specs = [pl.BlockSpec((S, tm, kk), lambda i: (0, i, 0)), pl.BlockSpec(b3.shape, lambda i: (0, 0, 0))]
    in_specs += [tile, _vec_spec(N)] if has_res else []
    out_shape = [jax.ShapeDtypeStruct((M, N), out_dtype)] + ([jax.ShapeDtypeStruct((M, N), F32)] if raw_out else [])
    outs = pl.pallas_call(
        body, name=name, grid=(M // tm,),
        out_shape=out_shape, in_specs=in_specs, out_specs=[tile] * len(out_shape),
        compiler_params=pltpu.CompilerParams(dimension_semantics=("parallel",)),
    )(*operands)
    return outs if raw_out else outs[0]


def _mm_tn_shard_rows(a3, b, name, out_dtype, tn=1024, tk=1024):
    S, T, m = a3.shape
    N = b.shape[1]
    tn, tk = _tile(N, tn), _tile(T, tk)
    return _mm_call(name, a3, b, pl.BlockSpec((1, tk, m), lambda i, j, k: (i, k, 0)),
                    pl.BlockSpec((tk, tn), lambda i, j, k: (k, j)), jax.ShapeDtypeStruct((S, m, N), out_dtype),
                    pl.BlockSpec((1, m, tn), lambda i, j, k: (i, 0, j)), (S, N // tn, T // tk), TN, (m, tn))


def _mm_tn_shard_cols(a, b3, name, out_dtype, tm=1024, tk=1024):
    T, M = a.shape
    S, _, n = b3.shape
    tm, tk = _tile(M, tm), _tile(T, tk)
    return _mm_call(name, a, b3, pl.BlockSpec((tk, tm), lambda i, j, k: (k, i)),
                    pl.BlockSpec((1, tk, n), lambda i, j, k: (j, k, 0)), jax.ShapeDtypeStruct((S, M, n), out_dtype),
                    pl.BlockSpec((1, tm, n), lambda i, j, k: (j, i, 0)), (M // tm, S, T // tk), TN, (tm, n))


def _row_spec(tm, width, off=0):
    return pl.BlockSpec((tm, width), lambda i: (i + off, 0))


def _vec_spec(width):
    return pl.BlockSpec((1, width), lambda i: (0, 0))


def _norm_mod_fwd(h, g, sc, sh, name):
    R, Dm = h.shape
    tm = _tile(R, ROW_BLOCK, 8)

    def body(h_ref, g_ref, sc_ref, sh_ref, o_ref):
        hv = h_ref[...]
        r = lax.rsqrt(jnp.mean(hv * hv, axis=-1, keepdims=True) + EPS)
        o_ref[...] = ((hv * r) * g_ref[...] * (1.0 + sc_ref[...]) + sh_ref[...]).astype(BF16)

    return pl.pallas_call(
        body, name=name, grid=(R // tm,),
        out_shape=jax.ShapeDtypeStruct((R, Dm), BF16),
        in_specs=[_row_spec(tm, Dm), _vec_spec(Dm), _vec_spec(Dm), _vec_spec(Dm)],
        out_specs=_row_spec(tm, Dm),
        compiler_params=pltpu.CompilerParams(dimension_semantics=("parallel",)),
    )(h, g, sc, sh)


def _norm_mod_fwd_cat(hc, h, g, csc, csh, sc, sh, name):
    (C, Dm), T = hc.shape, h.shape[0]
    tm = _tile(math.gcd(C, T), ROW_BLOCK, 8)
    off = C // tm

    def body(hc_ref, h_ref, g_ref, csc_ref, csh_ref, sc_ref, sh_ref, o_ref):
        is_ctx = pl.program_id(0) < off
        hv = jnp.where(is_ctx, hc_ref[...], h_ref[...])
        scv = jnp.where(is_ctx, csc_ref[...], sc_ref[...])
        shv = jnp.where(is_ctx, csh_ref[...], sh_ref[...])
        r = lax.rsqrt(jnp.mean(hv * hv, axis=-1, keepdims=True) + EPS)
        o_ref[...] = ((hv * r) * g_ref[...] * (1.0 + scv) + shv).astype(BF16)

    return pl.pallas_call(
        body, name=name, grid=((C + T) // tm,),
        out_shape=jax.ShapeDtypeStruct((C + T, Dm), BF16),
        in_specs=[pl.BlockSpec((tm, Dm), lambda i: (jnp.minimum(i, off - 1), 0)),
                  pl.BlockSpec((tm, Dm), lambda i: (jnp.maximum(i - off, 0), 0))] + [_vec_spec(Dm)] * 5,
        out_specs=_row_spec(tm, Dm),
        compiler_params=pltpu.CompilerParams(dimension_semantics=("parallel",)),
    )(hc, h, g, csc, csh, sc, sh)


def _norm_mod_bwd(h, g, sc, dxm, dres, name, dxm_row_off=0):
    R, Dm = h.shape
    tm = _tile(R, ROW_BLOCK, 8)
    off = dxm_row_off // tm
    has_res = dres is not None

    def body(*refs):
        it = iter(refs)
        h_ref, g_ref, sc_ref, dx_ref = next(it), next(it), next(it), next(it)
        dres_ref = next(it) if has_res else None
        dh_ref, da_ref, dsh_ref = next(it), next(it), next(it)
        i = pl.program_id(0)

        @pl.when(i == 0)
        def _():
            da_ref[...] = jnp.zeros_like(da_ref)
            dsh_ref[...] = jnp.zeros_like(dsh_ref)

        hv = h_ref[...]
        dx = dx_ref[...].astype(F32)
        r = lax.rsqrt(jnp.mean(hv * hv, axis=-1, keepdims=True) + EPS)
        n = hv * r
        da_ref[...] += jnp.sum(dx * n, axis=0, keepdims=True)
        dsh_ref[...] += jnp.sum(dx, axis=0, keepdims=True)
        dn = dx * (g_ref[...] * (1.0 + sc_ref[...]))
        dh = r * (dn - n * jnp.mean(dn * n, axis=-1, keepdims=True))
        if has_res:
            dh = dh + dres_ref[...]
        dh_ref[...] = dh

    operands = [h, g, sc, dxm] + ([dres] if has_res else [])
    in_specs = [_row_spec(tm, Dm), _vec_spec(Dm), _vec_spec(Dm), _row_spec(tm, Dm, off)]
    in_specs += [_row_spec(tm, Dm)] if has_res else []
    return pl.pallas_call(
        body, name=name, grid=(R // tm,),
        out_shape=[jax.ShapeDtypeStruct((R, Dm), F32), jax.ShapeDtypeStruct((1, Dm), F32),
                   jax.ShapeDtypeStruct((1, Dm), F32)],
        in_specs=in_specs,
        out_specs=[_row_spec(tm, Dm), _vec_spec(Dm), _vec_spec(Dm)],
        compiler_params=pltpu.CompilerParams(dimension_semantics=("arbitrary",)),
    )(*operands)


def _gate_bwd(dh, y, gt, name):
    R, Dm = dh.shape
    tm = _tile(R, ROW_BLOCK, 8)

    def body(dh_ref, y_ref, gt_ref, dy_ref, dgt_ref, dsum_ref):
        i = pl.program_id(0)

        @pl.when(i == 0)
        def _():
            dgt_ref[...] = jnp.zeros_like(dgt_ref)
            dsum_ref[...] = jnp.zeros_like(dsum_ref)

        dhv = dh_ref[...]
        dy = dhv * gt_ref[...]
        dgt_ref[...] += jnp.sum(dhv * y_ref[...], axis=0, keepdims=True)
        dsum_ref[...] += jnp.sum(dy, axis=0, keepdims=True)
        dy_ref[...] = dy.astype(BF16)

    return pl.pallas_call(
        body, name=name, grid=(R // tm,),
        out_shape=[jax.ShapeDtypeStruct((R, Dm), BF16), jax.ShapeDtypeStruct((1, Dm), F32),
                   jax.ShapeDtypeStruct((1, Dm), F32)],
        in_specs=[_row_spec(tm, Dm), _row_spec(tm, Dm), _vec_spec(Dm)],
        out_specs=[_row_spec(tm, Dm), _vec_spec(Dm), _vec_spec(Dm)],
        compiler_params=pltpu.CompilerParams(dimension_semantics=("arbitrary",)),
    )(dh, y, gt)


def _ffn_in_swiglu(xf, w3, name, tm=1024):
    T, K = xf.shape
    S, _, n = w3.shape
    half = S // 2
    tm = _tile(T, tm)

    def body(a_ref, wg_ref, wu_ref, gu_ref, act_ref):
        a = a_ref[...]
        g = _dot(a, wg_ref[0], NN)
        u = _dot(a, wu_ref[0], NN)
        gu_ref[0, 0] = g.astype(BF16)
        gu_ref[1, 0] = u.astype(BF16)
        act_ref[0] = (_silu(g) * u).astype(BF16)

    return pl.pallas_call(
        body, name=name, grid=(T // tm, half),
        out_shape=[jax.ShapeDtypeStruct((2, half, T, n), BF16), jax.ShapeDtypeStruct((half, T, n), BF16)],
        in_specs=[pl.BlockSpec((tm, K), lambda i, j: (i, 0)),
                  pl.BlockSpec((1, K, n), lambda i, j: (j, 0, 0)),
                  pl.BlockSpec((1, K, n), lambda i, j: (j + half, 0, 0))],
        out_specs=[pl.BlockSpec((2, 1, tm, n), lambda i, j: (0, j, i, 0)),
                   pl.BlockSpec((1, tm, n), lambda i, j: (j, i, 0))],
        compiler_params=pltpu.CompilerParams(dimension_semantics=("parallel", "parallel")),
    )(xf, w3, w3)


def _ffn_out_dx_swiglu(df, wo, gu, name, tm=1024):
    T, Dm = df.shape
    half, n, _ = wo.shape
    tm = _tile(T, tm)

    def body(df_ref, w_ref, gu_ref, o_ref):
        da = _dot(df_ref[...], w_ref[0], NT)
        g = gu_ref[0, 0].astype(F32)
        u = gu_ref[1, 0].astype(F32)
        s = _sigmoid(g)
        o_ref[0, 0] = (da * u * (s * (1.0 + g * (1.0 - s)))).astype(BF16)
        o_ref[1, 0] = (da * (g * s)).astype(BF16)

    gu_spec = pl.BlockSpec((2, 1, tm, n), lambda i, j: (0, j, i, 0))
    return pl.pallas_call(
        body, name=name, grid=(T // tm, half),
        out_shape=jax.ShapeDtypeStruct(gu.shape, BF16),
        in_specs=[pl.BlockSpec((tm, Dm), lambda i, j: (i, 0)),
                  pl.BlockSpec((1, n, Dm), lambda i, j: (j, 0, 0)), gu_spec],
        out_specs=gu_spec,
        compiler_params=pltpu.CompilerParams(dimension_semantics=("parallel", "parallel")),
    )(df, wo, gu)


def _glu_fwd(ag, name):
    R = ag.shape[0]
    tm = _tile(R, ROW_BLOCK, 8)

    def body(ag_ref, o_ref):
        o_ref[...] = ag_ref[:, :D_MODEL] * _sigmoid(ag_ref[:, D_MODEL:])

    return pl.pallas_call(
        body, name=name, grid=(R // tm,),
        out_shape=jax.ShapeDtypeStruct((R, D_MODEL), F32),
        in_specs=[_row_spec(tm, 2 * D_MODEL)],
        out_specs=_row_spec(tm, D_MODEL),
        compiler_params=pltpu.CompilerParams(dimension_semantics=("parallel",)),
    )(ag)


def _glu_bwd(ag, dhg, name):
    R = ag.shape[0]
    tm = _tile(R, ROW_BLOCK, 8)

    def body(ag_ref, dh_ref, o_ref, s_ref):
        i = pl.program_id(0)

        @pl.when(i == 0)
        def _():
            s_ref[...] = jnp.zeros_like(s_ref)

        a = ag_ref[:, :D_MODEL]
        s = _sigmoid(ag_ref[:, D_MODEL:])
        dh = dh_ref[...]
        da = dh * s
        dg = dh * a * s * (1.0 - s)
        o_ref[:, :D_MODEL] = da.astype(BF16)
        o_ref[:, D_MODEL:] = dg.astype(BF16)
        s_ref[:, :D_MODEL] += jnp.sum(da, axis=0, keepdims=True)
        s_ref[:, D_MODEL:] += jnp.sum(dg, axis=0, keepdims=True)

    return pl.pallas_call(
        body, name=name, grid=(R // tm,),
        out_shape=[jax.ShapeDtypeStruct((R, 2 * D_MODEL), BF16), jax.ShapeDtypeStruct((1, 2 * D_MODEL), F32)],
        in_specs=[_row_spec(tm, 2 * D_MODEL), _row_spec(tm, D_MODEL)],
        out_specs=[_row_spec(tm, 2 * D_MODEL), _vec_spec(2 * D_MODEL)],
        compiler_params=pltpu.CompilerParams(dimension_semantics=("arbitrary",)),
    )(ag, dhg)


def _halo_specs(tm, nblk, width):
    per = tm // CONV_HALO
    prev = pl.BlockSpec((CONV_HALO, width), lambda i: (jnp.maximum(i * per - 1, 0), 0))
    nxt = pl.BlockSpec((CONV_HALO, width), lambda i: (jnp.minimum((i + 1) * per, nblk * per - 1), 0))
    return prev, nxt


def _fill_halo(scr, prev_ref, cur_ref, next_ref, i, nblk, tm):
    scr[0:CONV_HALO, :] = jnp.where(i > 0, prev_ref[...], 0.0)
    scr[CONV_HALO:CONV_HALO + tm, :] = cur_ref[...]
    scr[CONV_HALO + tm:2 * CONV_HALO + tm, :] = jnp.where(i < nblk - 1, next_ref[...], 0.0)


CONV_ROWS = 128


def _windows(scr, cols, tm):
    reach = (CONV_WIDTH // SUBLANES) * SUBLANES
    for r in range(SUBLANES):
        base = scr[pl.ds(r, tm + reach), cols]
        for a in range(reach // SUBLANES + 1):
            off = SUBLANES * a + r
            if 1 <= off <= CONV_WIDTH:
                yield off, base[SUBLANES * a:SUBLANES * a + tm]


def _conv_fwd(hg, w_dw, b_dw, name):
    R, Dm = hg.shape
    tm = _tile(R, CONV_ROWS, CONV_HALO)
    nblk = R // tm
    prev_spec, next_spec = _halo_specs(tm, nblk, Dm)

    def body(prev_ref, cur_ref, next_ref, w_ref, bdw_ref, hd_ref, scr):
        _fill_halo(scr, prev_ref, cur_ref, next_ref, pl.program_id(0), nblk, tm)
        for cb in range(Dm // LANES):
            cols = slice(cb * LANES, (cb + 1) * LANES)
            acc = jnp.zeros((tm, LANES), F32) + bdw_ref[:, cols]
            for off, win in _windows(scr, cols, tm):
                acc = acc + w_ref[off - 1:off, cols] * win
            hd_ref[:, cols] = acc

    return pl.pallas_call(
        body, name=name, grid=(nblk,),
        out_shape=jax.ShapeDtypeStruct((R, Dm), F32),
        in_specs=[prev_spec, _row_spec(tm, Dm), next_spec,
                  pl.BlockSpec((CONV_WIDTH, Dm), lambda i: (0, 0)), _vec_spec(Dm)],
        out_specs=_row_spec(tm, Dm),
        scratch_shapes=[pltpu.VMEM((tm + 2 * CONV_HALO, Dm), F32)],
        compiler_params=pltpu.CompilerParams(dimension_semantics=("parallel",)),
    )(hg, hg, hg, w_dw, b_dw)


def _ln_silu_fwd(hd, ln_g, ln_b, name):
    R, Dm = hd.shape
    tm = _tile(R, ROW_BLOCK, 8)

    def body(hd_ref, g_ref, b_ref, hs_ref):
        hd = hd_ref[...]
        xc = hd - jnp.mean(hd, axis=-1, keepdims=True)
        rs = lax.rsqrt(jnp.mean(xc * xc, axis=-1, keepdims=True) + EPS)
        hs_ref[...] = _silu(xc * rs * g_ref[...] + b_ref[...]).astype(BF16)

    return pl.pallas_call(
        body, name=name, grid=(R // tm,),
        out_shape=jax.ShapeDtypeStruct((R, Dm), BF16),
        in_specs=[_row_spec(tm, Dm), _vec_spec(Dm), _vec_spec(Dm)],
        out_specs=_row_spec(tm, Dm),
        compiler_params=pltpu.CompilerParams(dimension_semantics=("parallel",)),
    )(hd, ln_g, ln_b)


def _ln_silu_bwd(dhs, hd, ln_g, ln_b, name):
    R, Dm = hd.shape
    tm = _tile(R, ROW_BLOCK, 8)

    def body(dhs_ref, hd_ref, g_ref, b_ref, dhd_ref, dg_ref, db_ref, dsum_ref):
        i = pl.program_id(0)

        @pl.when(i == 0)
        def _():
            dg_ref[...] = jnp.zeros_like(dg_ref)
            db_ref[...] = jnp.zeros_like(db_ref)
            dsum_ref[...] = jnp.zeros_like(dsum_ref)

        hd = hd_ref[...]
        mu = jnp.mean(hd, axis=-1, keepdims=True)
        xc = hd - mu
        rs = lax.rsqrt(jnp.mean(xc * xc, axis=-1, keepdims=True) + EPS)
        z = xc * rs
        hl = z * g_ref[...] + b_ref[...]
        dhl = dhs_ref[...] * _dsilu(hl)
        dg_ref[...] += jnp.sum(dhl * z, axis=0, keepdims=True)
        db_ref[...] += jnp.sum(dhl, axis=0, keepdims=True)
        dz = dhl * g_ref[...]
        dhd = rs * (dz - jnp.mean(dz, axis=-1, keepdims=True) - z * jnp.mean(dz * z, axis=-1, keepdims=True))
        dsum_ref[...] += jnp.sum(dhd, axis=0, keepdims=True)
        dhd_ref[...] = dhd

    return pl.pallas_call(
        body, name=name, grid=(R // tm,),
        out_shape=[jax.ShapeDtypeStruct((R, Dm), F32)] + [jax.ShapeDtypeStruct((1, Dm), F32)] * 3,
        in_specs=[_row_spec(tm, Dm), _row_spec(tm, Dm), _vec_spec(Dm), _vec_spec(Dm)],
        out_specs=[_row_spec(tm, Dm), _vec_spec(Dm), _vec_spec(Dm), _vec_spec(Dm)],
        compiler_params=pltpu.CompilerParams(dimension_semantics=("arbitrary",)),
    )(dhs, hd, ln_g, ln_b)


def _conv_bwd(dhd, hg, w_dw, name):
    R, Dm = hg.shape
    tm = _tile(R, CONV_ROWS, CONV_HALO)
    nblk = R // tm
    prev_spec, next_spec = _halo_specs(tm, nblk, Dm)

    def body(dprev, dcur, dnext, gprev, gcur, gnext, w_ref, dhg_ref, dw_ref, dscr, gscr, dwp):
        i = pl.program_id(0)

        @pl.when(i == 0)
        def _():
            dwp[...] = jnp.zeros_like(dwp)

        _fill_halo(dscr, dprev, dcur, dnext, i, nblk, tm)
        _fill_halo(gscr, gprev, gcur, gnext, i, nblk, tm)
        for cb in range(Dm // LANES):
            cols = slice(cb * LANES, (cb + 1) * LANES)
            acc = jnp.zeros((tm, LANES), F32)
            for off, win in _windows(dscr, cols, tm):
                j = CONV_WIDTH - off
                acc = acc + w_ref[j:j + 1, cols] * win
            dhg_ref[:, cols] = acc
            d_here = dcur[:, cols]
            for off, win in _windows(gscr, cols, tm):
                j = off - 1
                prod = d_here * win
                part = prod[0:SUBLANES]
                for k in range(1, tm // SUBLANES):
                    part = part + prod[k * SUBLANES:(k + 1) * SUBLANES]
                dwp[j * SUBLANES:(j + 1) * SUBLANES, cols] += part

        @pl.when(i == nblk - 1)
        def _():
            for j in range(CONV_WIDTH):
                dw_ref[j:j + 1, :] = jnp.sum(dwp[j * SUBLANES:(j + 1) * SUBLANES, :], axis=0, keepdims=True)

    return pl.pallas_call(
        body, name=name, grid=(nblk,),
        out_shape=[jax.ShapeDtypeStruct((R, Dm), F32), jax.ShapeDtypeStruct((CONV_WIDTH, Dm), F32)],
        in_specs=[prev_spec, _row_spec(tm, Dm), next_spec, prev_spec, _row_spec(tm, Dm), next_spec,
                  pl.BlockSpec((CONV_WIDTH, Dm), lambda i: (0, 0))],
        out_specs=[_row_spec(tm, Dm), pl.BlockSpec((CONV_WIDTH, Dm), lambda i: (0, 0))],
        scratch_shapes=[pltpu.VMEM((tm + 2 * CONV_HALO, Dm), F32)] * 2
        + [pltpu.VMEM((CONV_WIDTH * SUBLANES, Dm), F32)],
        compiler_params=pltpu.CompilerParams(dimension_semantics=("arbitrary",)),
    )(dhd, dhd, dhd, hg, hg, hg, w_dw)


def _swap16(y, lane):
    return jnp.where((lane & 16) == 0, pltpu.roll(y, LANES - 16, 1), pltpu.roll(y, 16, 1))


def _head_mean(v, bd):
    hi, lo = _split_bf16(v)
    return (_dot(hi, bd, NN) + _dot(lo, bd, NN)) * (1.0 / HEAD_DIM)


Q_COLS = (0, ATTN_WIDTH)
K_COLS = (ATTN_WIDTH, ATTN_WIDTH + HEAD_DIM * 2)
V_COLS = (K_COLS[1], K_COLS[1] + HEAD_DIM * 2)
SU_COLS = (V_COLS[1], V_COLS[1] + SG_WIDTH)
SV_COLS = (SU_COLS[1], SU_COLS[1] + SG_WIDTH)


def _mix_prep_fwd(p, ctx_rows, cos, sin, qg, kg, bd, w_sp, b_spt, name):
    TT = p.shape[0]
    off = ctx_rows // CHUNK
    q_scale = HEAD_DIM ** -0.5

    def body(p_ref, cos_ref, sin_ref, qg_ref, kg_ref, bd_ref, w_ref, b_ref,
             q_ref, kp_ref, vp_ref, kt_ref, vt_ref, sg_ref):
        lane = lax.broadcasted_iota(jnp.int32, (CHUNK, LANES), 1)
        low = lane < HEAD_DIM
        cs, sn, bdv = cos_ref[...], sin_ref[...], bd_ref[...]

        def norm_rope(xv, gain):
            r = lax.rsqrt(_head_mean(xv * xv, bdv) + EPS)
            yv = xv * r * gain
            return yv * cs + _swap16(yv, lane) * sn

        def pad_heads(ref, t):
            tr = pltpu.roll(t, HEAD_DIM, 1)
            ref[0, 0] = jnp.where(low, t, 0.0).astype(BF16)
            ref[0, 1] = jnp.where(low, 0.0, tr).astype(BF16)
            ref[1, 0] = jnp.where(low, tr, 0.0).astype(BF16)
            ref[1, 1] = jnp.where(low, 0.0, t).astype(BF16)

        for a in range(ATTN_WIDTH // LANES):
            xv = p_ref[:, a * LANES:(a + 1) * LANES]
            q_ref[:, a * LANES:(a + 1) * LANES] = (norm_rope(xv, qg_ref[...]) * q_scale).astype(BF16)
        kh = norm_rope(p_ref[:, K_COLS[0]:K_COLS[1]], kg_ref[...])
        pad_heads(kp_ref, kh)
        vh = p_ref[:, V_COLS[0]:V_COLS[1]]
        pad_heads(vp_ref, vh)
        for t_ref, t in ((kt_ref, kh.T), (vt_ref, vh.T)):
            t_ref[0] = t[:HEAD_DIM].astype(BF16)
            t_ref[1] = t[HEAD_DIM:].astype(BF16)
        for g in range(N_SG_GROUPS):
            u = _gelu(p_ref[:, SU_COLS[0] + g * LANES:SU_COLS[0] + (g + 1) * LANES])
            vg = _gelu(p_ref[:, SV_COLS[0] + g * LANES:SV_COLS[0] + (g + 1) * LANES])
            xc = vg - jnp.mean(vg, axis=-1, keepdims=True)
            vn = xc * lax.rsqrt(jnp.mean(xc * xc, axis=-1, keepdims=True) + EPS)
            mixed = _dot(w_ref[g].astype(BF16), vn.astype(BF16), NN) + b_ref[:, g:g + 1]
            sg_ref[:, g * LANES:(g + 1) * LANES] = (u * mixed).astype(BF16)

    def row(width):
        return pl.BlockSpec((CHUNK, width), lambda i: (i, 0))

    def whole(shape):
        return pl.BlockSpec(shape, lambda i: (0,) * len(shape))

    pad_spec = pl.BlockSpec((2, 2, CHUNK, LANES), lambda i: (0, 0, i, 0))
    return pl.pallas_call(
        body, name=name, grid=(TT // CHUNK,),
        out_shape=[jax.ShapeDtypeStruct((TT, ATTN_WIDTH), BF16),
                   jax.ShapeDtypeStruct((2, 2, TT, LANES), BF16), jax.ShapeDtypeStruct((2, 2, TT, LANES), BF16),
                   jax.ShapeDtypeStruct((2, HEAD_DIM, TT), BF16), jax.ShapeDtypeStruct((2, HEAD_DIM, TT), BF16),
                   jax.ShapeDtypeStruct((TT - ctx_rows, ATTN_WIDTH + SG_WIDTH), BF16)],
        in_specs=[row(IN_WIDTH), row(LANES), row(LANES), whole((1, LANES)), whole((1, LANES)),
                  whole((LANES, LANES)), whole((N_SG_GROUPS, CHUNK, CHUNK)), whole((CHUNK, N_SG_GROUPS))],
        out_specs=[row(ATTN_WIDTH), pad_spec, pad_spec,
                   pl.BlockSpec((2, HEAD_DIM, CHUNK), lambda i: (0, 0, i)),
                   pl.BlockSpec((2, HEAD_DIM, CHUNK), lambda i: (0, 0, i)),
                   pl.BlockSpec((CHUNK, SG_WIDTH), lambda i: (jnp.maximum(i - off, 0), 1))],
        compiler_params=pltpu.CompilerParams(dimension_semantics=("arbitrary",)),
    )(p, cos, sin, qg, kg, bd, w_sp, b_spt)


def _mix_prep_bwd(p, dq, f, dao, ctx_rows, cos, sin, qg, kg, bd, w_sp, w_spt, b_spt, name):
    TT = p.shape[0]
    off = ctx_rows // CHUNK
    q_scale = HEAD_DIM ** -0.5

    def body(p_ref, dq_ref, f_ref, dsg_ref, cos_ref, sin_ref, qg_ref, kg_ref, bd_ref, w_ref, wt_ref,
             b_ref, dp_ref, dqg_ref, dkg_ref, dw_ref, db_ref):
        i = pl.program_id(0)

        @pl.when(i == 0)
        def _():
            dqg_ref[...] = jnp.zeros_like(dqg_ref)
            dkg_ref[...] = jnp.zeros_like(dkg_ref)
            dw_ref[...] = jnp.zeros_like(dw_ref)
            db_ref[...] = jnp.zeros_like(db_ref)

        latent = (i >= off).astype(F32)
        lane = lax.broadcasted_iota(jnp.int32, (CHUNK, LANES), 1)
        low = lane < HEAD_DIM
        cs, sn, bdv = cos_ref[...], sin_ref[...], bd_ref[...]

        def fold(b0):
            return jnp.where(low, f_ref[0, b0] + pltpu.roll(f_ref[0, b0 + 1], HEAD_DIM, 1),
                             pltpu.roll(f_ref[1, b0], HEAD_DIM, 1) + f_ref[1, b0 + 1])

        def norm_rope_bwd(xv, dout, gain):
            r = lax.rsqrt(_head_mean(xv * xv, bdv) + EPS)
            n = xv * r
            dy = dout * cs + _swap16(dout * sn, lane)
            dn = dy * gain
            dx = r * (dn - n * _head_mean(dn * n, bdv))
            return dx, jnp.sum(dy * n, axis=0, keepdims=True)

        for a in range(ATTN_WIDTH // LANES):
            cols = slice(a * LANES, (a + 1) * LANES)
            dx, dg = norm_rope_bwd(p_ref[:, cols], dq_ref[:, cols] * (latent * q_scale), qg_ref[...])
            dp_ref[:, cols] = dx.astype(BF16)
            dqg_ref[...] += dg
        dx, dg = norm_rope_bwd(p_ref[:, K_COLS[0]:K_COLS[1]], fold(0), kg_ref[...])
        dp_ref[:, K_COLS[0]:K_COLS[1]] = dx.astype(BF16)
        dkg_ref[...] += dg
        dp_ref[:, V_COLS[0]:V_COLS[1]] = fold(2).astype(BF16)
        for g in range(N_SG_GROUPS):
            su = p_ref[:, SU_COLS[0] + g * LANES:SU_COLS[0] + (g + 1) * LANES]
            sv = p_ref[:, SV_COLS[0] + g * LANES:SV_COLS[0] + (g + 1) * LANES]
            u, vg = _gelu(su), _gelu(sv)
            xc = vg - jnp.mean(vg, axis=-1, keepdims=True)
            rs = lax.rsqrt(jnp.mean(xc * xc, axis=-1, keepdims=True) + EPS)
            vn = xc * rs
            vnb = vn.astype(BF16)
            mixed = _dot(w_ref[g].astype(BF16), vnb, NN) + b_ref[:, g:g + 1]
            dsg = dsg_ref[:, g * LANES:(g + 1) * LANES] * latent
            du = dsg * mixed
            dmix = dsg * u
            dmb = dmix.astype(BF16)
            db_ref[:, g:g + 1] += jnp.sum(dmix, axis=-1, keepdims=True)
            dw_ref[g] += _dot(dmb, vnb, NT)
            dvn = _dot(wt_ref[g].astype(BF16), dmb, NN)
            dvg = rs * (dvn - jnp.mean(dvn, axis=-1, keepdims=True)
                        - vn * jnp.mean(dvn * vn, axis=-1, keepdims=True))
            dp_ref[:, SU_COLS[0] + g * LANES:SU_COLS[0] + (g + 1) * LANES] = (du * _dgelu(su)).astype(BF16)
            dp_ref[:, SV_COLS[0] + g * LANES:SV_COLS[0] + (g + 1) * LANES] = (dvg * _dgelu(sv)).astype(BF16)

    def row(width):
        return pl.BlockSpec((CHUNK, width), lambda i: (i, 0))

    def latent_row(width, col_block):
        return pl.BlockSpec((CHUNK, width), lambda i: (jnp.maximum(i - off, 0), col_block))

    def whole(shape):
        return pl.BlockSpec(shape, lambda i: (0,) * len(shape))

    return pl.pallas_call(
        body, name=name, grid=(TT // CHUNK,),
        out_shape=[jax.ShapeDtypeStruct((TT, IN_WIDTH), BF16), jax.ShapeDtypeStruct((1, LANES), F32),
                   jax.ShapeDtypeStruct((1, LANES), F32),
                   jax.ShapeDtypeStruct((N_SG_GROUPS, CHUNK, CHUNK), F32),
                   jax.ShapeDtypeStruct((CHUNK, N_SG_GROUPS), F32)],
        in_specs=[row(IN_WIDTH), latent_row(ATTN_WIDTH, 0),
                  pl.BlockSpec((2, 4, CHUNK, LANES), lambda i: (0, 0, i, 0)),
                  latent_row(SG_WIDTH, 1), row(LANES), row(LANES), whole((1, LANES)), whole((1, LANES)),
                  whole((LANES, LANES)), whole((N_SG_GROUPS, CHUNK, CHUNK)),
                  whole((N_SG_GROUPS, CHUNK, CHUNK)), whole((CHUNK, N_SG_GROUPS))],
        out_specs=[row(IN_WIDTH), whole((1, LANES)), whole((1, LANES)),
                   whole((N_SG_GROUPS, CHUNK, CHUNK)), whole((CHUNK, N_SG_GROUPS))],
        compiler_params=pltpu.CompilerParams(dimension_semantics=("arbitrary",)),
    )(p, dq, f, dao, cos, sin, qg, kg, bd, w_sp, w_spt, b_spt)


def _col_reduce(t, pair_op, reduce_op, slab=256):
    R = t.shape[0]
    slab = _tile(R, slab, SUBLANES)
    part = t[0:slab]
    for k in range(1, R // slab):
        part = pair_op(part, t[k * slab:(k + 1) * slab])
    return reduce_op(part, axis=0, keepdims=True)


def _attn_fwd(q, kpad, vt, ao, ctx_rows, name, tq=256):
    TT = q.shape[0]
    T = TT - ctx_rows
    tq = _tile(T, tq)
    off = ctx_rows // tq
    group = 2 * LANES

    def body(q_ref, k_ref, vt_ref, ao_in, o_ref, lse_ref):
        del ao_in
        vtv = vt_ref[0]
        for a in range(2):
            qa = q_ref[:, a * LANES:(a + 1) * LANES]
            halves = []
            for b in range(2):
                st = _dot(k_ref[0, b], qa, NT)
                m = _col_reduce(st, jnp.maximum, jnp.max)
                e = jnp.exp(st - m)
                l = _col_reduce(e, jnp.add, jnp.sum)
                lse_ref[0, 2 * a + b:2 * a + b + 1, :] = m + jnp.log(l)
                halves.append(_dot(vtv, e.astype(BF16), NN) * (1.0 / l))
            o_ref[:, a * LANES:(a + 1) * LANES] = jnp.concatenate(halves, axis=0).T.astype(BF16)

    return pl.pallas_call(
        body, name=name, grid=(2, T // tq),
        out_shape=[jax.ShapeDtypeStruct(ao.shape, BF16), jax.ShapeDtypeStruct((2, 4, T), F32)],
        in_specs=[pl.BlockSpec((tq, group), lambda j, i: (i + off, j)),
                  pl.BlockSpec((1, 2, TT, LANES), lambda j, i: (j, 0, 0, 0)),
                  pl.BlockSpec((1, HEAD_DIM, TT), lambda j, i: (j, 0, 0)),
                  pl.BlockSpec(memory_space=pl.ANY)],
        out_specs=[pl.BlockSpec((tq, group), lambda j, i: (i, j)),
                   pl.BlockSpec((1, 4, tq), lambda j, i: (j, 0, i))],
        input_output_aliases={3: 0},
        compiler_params=pltpu.CompilerParams(dimension_semantics=("parallel", "parallel")),
    )(q, kpad, vt, ao)


def _attn_bwd(q, dao, ao, lse, kpad, vpad, kt, ctx_rows, name, tq=256):
    TT = q.shape[0]
    T = TT - ctx_rows
    tq = _tile(T, tq)
    off = ctx_rows // tq
    group = 2 * LANES

    def body(q_ref, do_ref, o_ref, lse_ref, k_ref, v_ref, kt_ref, dq_ref, f_ref):
        i = pl.program_id(1)

        @pl.when(i == 0)
        def _():
            f_ref[...] = jnp.zeros_like(f_ref)

        ktv = kt_ref[0]
        row = lax.broadcasted_iota(jnp.int32, (SUBLANES, LANES), 0)
        lane = lax.broadcasted_iota(jnp.int32, (SUBLANES, LANES), 1)
        half_ones = (jnp.where(lane < HEAD_DIM, 0, 1) == row).astype(BF16)
        for a in range(2):
            cols = slice(a * LANES, (a + 1) * LANES)
            qa = q_ref[:, cols]
            do32 = do_ref[:, cols]
            doa = do32.astype(BF16)
            hi, lo = _split_bf16(do32 * o_ref[:, cols].astype(F32))
            deltas = _dot(half_ones, hi, NT) + _dot(half_ones, lo, NT)
            halves = []
            for b in range(2):
                h = 2 * a + b
                st = _dot(k_ref[0, b], qa, NT)
                pt = jnp.exp(st - lse_ref[0, h:h + 1, :])
                dpt = _dot(v_ref[0, b], doa, NT)
                dst = (pt * (dpt - deltas[b:b + 1, :])).astype(BF16)
                f_ref[0, b] += _dot(dst, qa, NN)
                f_ref[0, 2 + b] += _dot(pt.astype(BF16), doa, NN)
                halves.append(_dot(ktv, dst, NN))
            dq_ref[:, cols] = jnp.concatenate(halves, axis=0).T

    kv_spec = pl.BlockSpec((1, 2, TT, LANES), lambda j, i: (j, 0, 0, 0))
    out_cols = pl.BlockSpec((tq, group), lambda j, i: (i, j))
    return pl.pallas_call(
        body, name=name, grid=(2, T // tq),
        out_shape=[jax.ShapeDtypeStruct((T, ATTN_WIDTH), F32), jax.ShapeDtypeStruct((2, 4, TT, LANES), F32)],
        in_specs=[pl.BlockSpec((tq, group), lambda j, i: (i + off, j)), out_cols, out_cols,
                  pl.BlockSpec((1, 4, tq), lambda j, i: (j, 0, i)),
                  kv_spec, kv_spec, pl.BlockSpec((1, HEAD_DIM, TT), lambda j, i: (j, 0, 0))],
        out_specs=[out_cols, pl.BlockSpec((1, 4, TT, LANES), lambda j, i: (j, 0, 0, 0))],
        compiler_params=pltpu.CompilerParams(dimension_semantics=("parallel", "arbitrary")),
    )(q, dao, ao, lse, kpad, vpad, kt)


def _final_fwd_bwd(h, g, target, name):
    R, Dm = h.shape
    tm = _tile(R, ROW_BLOCK, 8)

    def body(h_ref, g_ref, t_ref, dh_ref, loss_ref, dg_ref):
        i = pl.program_id(0)

        @pl.when(i == 0)
        def _():
            loss_ref[...] = jnp.zeros_like(loss_ref)
            dg_ref[...] = jnp.zeros_like(dg_ref)

        hv = h_ref[...]
        r = lax.rsqrt(jnp.mean(hv * hv, axis=-1, keepdims=True) + EPS)
        n = hv * r
        diff = n * g_ref[...] - t_ref[...]
        loss_ref[...] += jnp.sum(diff * diff)
        dout = diff * (1.0 / Dm)
        dg_ref[...] += jnp.sum(dout * n, axis=0, keepdims=True)
        dn = dout * g_ref[...]
        dh_ref[...] = r * (dn - n * jnp.mean(dn * n, axis=-1, keepdims=True))

    return pl.pallas_call(
        body, name=name, grid=(R // tm,),
        out_shape=[jax.ShapeDtypeStruct((R, Dm), F32), jax.ShapeDtypeStruct((1, LANES), F32),
                   jax.ShapeDtypeStruct((1, Dm), F32)],
        in_specs=[_row_spec(tm, Dm), _vec_spec(Dm), _row_spec(tm, Dm)],
        out_specs=[_row_spec(tm, Dm), _vec_spec(LANES), _vec_spec(Dm)],
        compiler_params=pltpu.CompilerParams(dimension_semantics=("arbitrary",)),
    )(h, g, target)


MOD_ROWS = 16


def _mod_fwd(c_rows, w_mod, name):
    L, Dm, n = w_mod.shape

    def body(c_ref, w_ref, o_ref):
        o_ref[0] = _dot3(_silu(c_ref[...]), w_ref[0], NN)

    return pl.pallas_call(
        body, name=name, grid=(L,),
        out_shape=jax.ShapeDtypeStruct((L, MOD_ROWS, n), F32),
        in_specs=[pl.BlockSpec((MOD_ROWS, Dm), lambda l: (0, 0)), pl.BlockSpec((1, Dm, n), lambda l: (l, 0, 0))],
        out_specs=pl.BlockSpec((1, MOD_ROWS, n), lambda l: (l, 0, 0)),
        compiler_params=pltpu.CompilerParams(dimension_semantics=("parallel",)),
    )(c_rows, w_mod)


def _mod_bwd(c_rows_t, dmod, w_mod, name):
    L, Dm, n = w_mod.shape

    def body(ct_ref, d_ref, w_ref, gw_ref, ds_ref):
        dm = d_ref[0]
        gw_ref[0] = _dot3(_silu(ct_ref[...]), dm, NN)
        ds_ref[0] = _dot3(dm[:MOD_ROWS], w_ref[0], NT)

    return pl.pallas_call(
        body, name=name, grid=(L,),
        out_shape=[jax.ShapeDtypeStruct((L, Dm, n), F32), jax.ShapeDtypeStruct((L, MOD_ROWS, Dm), F32)],
        in_specs=[pl.BlockSpec((Dm, LANES), lambda l: (0, 0)), pl.BlockSpec((1, LANES, n), lambda l: (l, 0, 0)),
                  pl.BlockSpec((1, Dm, n), lambda l: (l, 0, 0))],
        out_specs=[pl.BlockSpec((1, Dm, n), lambda l: (l, 0, 0)),
                   pl.BlockSpec((1, MOD_ROWS, Dm), lambda l: (l, 0, 0))],
        compiler_params=pltpu.CompilerParams(dimension_semantics=("parallel",)),
    )(c_rows_t, dmod, w_mod)


def _adam_update(w, g, m, v):
    c1 = 1.0 - ADAM_B1 ** ADAM_STEP
    c2 = 1.0 - ADAM_B2 ** ADAM_STEP
    mn = ADAM_B1 * m + (1.0 - ADAM_B1) * g
    vn = ADAM_B2 * v + (1.0 - ADAM_B2) * (g * g)
    return -ADAM_LR * ((mn / c1) / (jnp.sqrt(vn / c2) + ADAM_EPS) + ADAM_WD * w), mn, vn


def _adamw(w, g, m, v, name):
    R, Cw = w.shape
    tm = _tile(R, ROW_BLOCK, 8)

    def body(w_ref, g_ref, m_ref, v_ref, d_ref, mo_ref, vo_ref):
        d_ref[...], mo_ref[...], vo_ref[...] = _adam_update(w_ref[...], g_ref[...], m_ref[...], v_ref[...])

    spec = pl.BlockSpec((tm, Cw), lambda i: (i, 0))
    return pl.pallas_call(
        body, name=name, grid=(R // tm,),
        out_shape=[jax.ShapeDtypeStruct((R, Cw), F32)] * 3,
        in_specs=[spec] * 4, out_specs=[spec] * 3,
        compiler_params=pltpu.CompilerParams(dimension_semantics=("parallel",)),
    )(w, g, m, v)


def _adamw_recv(w, m, v, recvs, name):
    L, R, n = w.shape
    tm = _tile(R, ROW_BLOCK, 8)
    nblk = R // tm
    parts = [r.reshape(N_DEV, R, n) for r in recvs]

    def body(*refs):
        w_ref, m_ref, v_ref = refs[:3]
        part_refs = refs[3:3 + L]
        g_ref, d_ref, mo_ref, vo_ref, gsum = refs[3 + L:]
        l = pl.program_id(0)
        for ll in range(L):
            @pl.when(l == ll)
            def _(ll=ll):
                acc = part_refs[ll][0].astype(F32)
                for s in range(1, N_DEV):
                    acc = acc + part_refs[ll][s].astype(F32)
                gsum[...] = acc
        g = gsum[...]
        g_ref[0] = g
        d_ref[0], mo_ref[0], vo_ref[0] = _adam_update(w_ref[0], g, m_ref[0], v_ref[0])

    def part_spec(ll):
        return pl.BlockSpec((N_DEV, tm, n), lambda l, i: (0, jnp.where(l == ll, i, jnp.where(l < ll, 0, nblk - 1)), 0))

    spec = pl.BlockSpec((1, tm, n), lambda l, i: (l, i, 0))
    return pl.pallas_call(
        body, name=name, grid=(L, nblk),
        out_shape=[jax.ShapeDtypeStruct((L, R, n), F32)] * 4,
        in_specs=[spec] * 3 + [part_spec(ll) for ll in range(L)], out_specs=[spec] * 4,
        scratch_shapes=[pltpu.VMEM((tm, n), F32)],
        compiler_params=pltpu.CompilerParams(dimension_semantics=("parallel", "parallel")),
    )(w, m, v, *parts)


def _pack(parts, row_mult=8):
    flat, offs, pos = [], [], 0
    for t in parts:
        t = t.reshape(-1).astype(F32)
        size = -(-t.shape[0] // LANES) * LANES
        flat.append(jnp.pad(t, (0, size - t.shape[0])))
        offs.append(pos)
        pos += size
    total = -(-pos // (LANES * row_mult)) * (LANES * row_mult)
    if total > pos:
        flat.append(jnp.zeros((total - pos,), F32))
    return jnp.concatenate(flat).reshape(-1, LANES), offs


def _take(buf, off, shape):
    size = math.prod(shape)
    return buf[..., off:off + size].reshape(buf.shape[:-1] + tuple(shape))


def _rope_tables(T, ctx_rows):
    pos = jnp.arange(T)
    row = (pos // GRID_W).astype(F32)
    col = (pos % GRID_W).astype(F32)
    half = HEAD_DIM // 4
    inv = ROPE_THETA ** (-jnp.arange(0, 2 * half, 2, dtype=F32) / (2 * half))
    ang_r, ang_c = row[:, None] * inv[None, :], col[:, None] * inv[None, :]
    cos = jnp.concatenate([jnp.cos(ang_r)] * 2 + [jnp.cos(ang_c)] * 2, axis=1)
    sin = jnp.concatenate([-jnp.sin(ang_r), jnp.sin(ang_r), -jnp.sin(ang_c), jnp.sin(ang_c)], axis=1)
    cos = jnp.concatenate([jnp.ones((ctx_rows, HEAD_DIM), F32), cos], axis=0)
    sin = jnp.concatenate([jnp.zeros((ctx_rows, HEAD_DIM), F32), sin], axis=0)
    return jnp.tile(cos, (1, 2)), jnp.tile(sin, (1, 2))


def kernel(x, c, ctx, c_ctx, w_mod, b_mod, g_mix, g_ffn, w_ffn_in, w_ffn_out, w_in, q_gain, k_gain, w_sp, b_sp, w_out, w_pw1, b_pw1, w_dw, b_dw, ln_g, ln_b, w_pw2, b_pw2, g_final, loss_target, m_c_ctx, m_w_mod, m_b_mod, m_g_mix, m_g_ffn, m_w_ffn_in, m_w_ffn_out, m_w_in, m_q_gain, m_k_gain, m_w_sp, m_b_sp, m_w_out, m_w_pw1, m_b_pw1, m_w_dw, m_b_dw, m_ln_g, m_ln_b, m_w_pw2, m_b_pw2, m_g_final, v_c_ctx, v_w_mod, v_b_mod, v_g_mix, v_g_ffn, v_w_ffn_in, v_w_ffn_out, v_w_in, v_q_gain, v_k_gain, v_w_sp, v_b_sp, v_w_out, v_w_pw1, v_b_pw1, v_w_dw, v_b_dw, v_ln_g, v_ln_b, v_w_pw2, v_b_pw2, v_g_final):
    weights = dict(c_ctx=c_ctx, w_mod=w_mod, b_mod=b_mod, g_mix=g_mix, g_ffn=g_ffn, w_ffn_in=w_ffn_in,
                   w_ffn_out=w_ffn_out, w_in=w_in, q_gain=q_gain, k_gain=k_gain, w_sp=w_sp, b_sp=b_sp,
                   w_out=w_out, w_pw1=w_pw1, b_pw1=b_pw1, w_dw=w_dw, b_dw=b_dw, ln_g=ln_g, ln_b=ln_b,
                   w_pw2=w_pw2, b_pw2=b_pw2, g_final=g_final)
    moments_m = dict(c_ctx=m_c_ctx, w_mod=m_w_mod, b_mod=m_b_mod, g_mix=m_g_mix, g_ffn=m_g_ffn,
                     w_ffn_in=m_w_ffn_in, w_ffn_out=m_w_ffn_out, w_in=m_w_in, q_gain=m_q_gain,
                     k_gain=m_k_gain, w_sp=m_w_sp, b_sp=m_b_sp, w_out=m_w_out, w_pw1=m_w_pw1,
                     b_pw1=m_b_pw1, w_dw=m_w_dw, b_dw=m_b_dw, ln_g=m_ln_g, ln_b=m_ln_b, w_pw2=m_w_pw2,
                     b_pw2=m_b_pw2, g_final=m_g_final)
    moments_v = dict(c_ctx=v_c_ctx, w_mod=v_w_mod, b_mod=v_b_mod, g_mix=v_g_mix, g_ffn=v_g_ffn,
                     w_ffn_in=v_w_ffn_in, w_ffn_out=v_w_ffn_out, w_in=v_w_in, q_gain=v_q_gain,
                     k_gain=v_k_gain, w_sp=v_w_sp, b_sp=v_b_sp, w_out=v_w_out, w_pw1=v_w_pw1,
                     b_pw1=v_b_pw1, w_dw=v_w_dw, b_dw=v_b_dw, ln_g=v_ln_g, ln_b=v_ln_b, w_pw2=v_w_pw2,
                     b_pw2=v_b_pw2, g_final=v_g_final)
    names = list(weights)

    T, C = x.shape[1], ctx.shape[1]
    Dm = D_MODEL
    me = 4 * lax.axis_index("x") + 2 * lax.axis_index("y") + lax.axis_index("c")
    h0 = x[0]
    ctx2 = ctx[0]
    target = loss_target[0]

    small_sharded = (("w_dw", w_dw[0]), ("b_pw1", b_pw1), ("b_dw", b_dw), ("ln_g", ln_g), ("ln_b", ln_b),
                     ("b_pw2", b_pw2))
    buf1, offs1 = _pack([c] + [t for _, t in small_sharded])
    got1, W_in, W_out = _all_gather([buf1, w_in[0].astype(BF16), w_out[0].astype(BF16)], "gather_cond", False)
    got1 = got1.reshape(N_DEV, -1)
    c_all = _take(got1, offs1[0], (Dm,))
    full_small = {}
    for (nm, t), off in zip(small_sharded, offs1[1:]):
        seg = _take(got1, off, t.shape)
        full_small[nm] = jnp.moveaxis(seg, 0, -2).reshape(t.shape[:-1] + (N_DEV * t.shape[-1],))
    w_dw_f, b_pw1_f = full_small["w_dw"], full_small["b_pw1"]
    b_dw_f, ln_g_f, ln_b_f, b_pw2_f = (full_small[k] for k in ("b_dw", "ln_g", "ln_b", "b_pw2"))

    c_rows = jnp.concatenate([c_all, c_ctx[None, :], jnp.zeros((MOD_ROWS - N_DEV - 1, Dm), F32)], axis=0)
    mod_part = _mod_fwd(c_rows, w_mod, "mod_fwd")
    n_mod = w_mod.shape[2]
    got2 = _all_gather([mod_part.reshape(-1, LANES)], "gather_mod", True)[0]
    mod_all = got2.reshape(N_DEV, 2, MOD_ROWS, n_mod).transpose(1, 2, 0, 3).reshape(2, MOD_ROWS, N_DEV * n_mod)
    mod_all = mod_all + b_mod[:, None, :]
    my_mod = lax.dynamic_index_in_dim(mod_all, me, axis=1, keepdims=False)
    sh1, sc1, gt1, sh2, sc2, gt2 = ([my_mod[l:l + 1, k * Dm:(k + 1) * Dm] for l in range(2)] for k in range(6))
    csh1 = mod_all[0, N_DEV:N_DEV + 1, 0:Dm]
    csc1 = mod_all[0, N_DEV:N_DEV + 1, Dm:2 * Dm]

    behind = got2[0:1, 0:1] * 0.0
    gather_groups = [[w_ffn_in[0], w_ffn_out[0]], [w_pw1[0], w_pw2[0]], [w_ffn_in[1], w_ffn_out[1]]]
    gathers = [_push_begin([(t + behind).astype(BF16) for t in grp], True, f"gather_start{k}")
               for k, grp in enumerate(gather_groups)]
    started = sum(h[4][0:1, 0:1] for h in gathers)

    def gathered(k, after):
        return _push_end(gathers[k], after, f"gather_wait{k}")[1]

    def ffn_weights(k, after):
        wi, wo = gathered(k, after)
        return wi.reshape(N_DEV, Dm, FF_SHARD), wo.reshape(N_DEV // 2, FF_SHARD, Dm)

    def col_gathered(t, n):
        return t.reshape(N_DEV, Dm, n).transpose(1, 0, 2).reshape(Dm, N_DEV * n)

    W_ffi, W_ffo = [None, None], [None, None]

    g_mix_r = [g_mix[l:l + 1] for l in range(2)]
    g_ffn_r = [g_ffn[l:l + 1] for l in range(2)]
    g_fin = g_final[None, :]

    cos, sin = _rope_tables(T, C)
    qg = jnp.tile(q_gain, (1, 2))
    kg = jnp.tile(k_gain, (1, 2))
    lane_head = jnp.arange(LANES) // HEAD_DIM
    bd = (lane_head[:, None] == lane_head[None, :]).astype(BF16)
    w_sp0 = w_sp[0]
    w_spt0 = w_sp0.transpose(0, 2, 1)
    b_spt0 = b_sp[0].T

    XM = _norm_mod_fwd_cat(ctx2, h0, g_mix_r[0], csc1, csh1, sc1[0] + started, sh1[0], "norm_mix0")
    W_in = col_gathered(W_in, IN_WIDTH // N_DEV)
    P = _mm(XM, W_in, "nn", "in_proj", tn=IN_WIDTH)
    qh, kpad, vpad, kt, vt, ao = _mix_prep_fwd(P, C, cos, sin, qg, kg, bd, w_sp0, b_spt0, "mix_prep")
    ao, lse = _attn_fwd(qh, kpad, vt, ao, C, "attn_fwd")
    h1, y0 = _mm(ao, W_out, "nn", "out_proj", res=h0, gate=gt1[0], raw_out=True)

    def ffn_fwd(h_in, l):
        xf = _norm_mod_fwd(h_in, g_ffn_r[l], sc2[l], sh2[l], f"norm_ffn{l}")
        W_ffi[l], W_ffo[l] = ffn_weights(2 * l, xf)
        gu, act = _ffn_in_swiglu(xf, W_ffi[l], f"ffn_in{l}")
        h_out, f = _mm_sum_shards(act, W_ffo[l], "nn", f"ffn_out{l}", res=h_in, gate=gt2[l], raw_out=True)
        return h_out, (xf, gu, act, f)

    h2, saved_ffn0 = ffn_fwd(h1, 0)

    xm1 = _norm_mod_fwd(h2, g_mix_r[1], sc1[1], sh1[1], "norm_mix1")
    W_pw1, W_pw2 = gathered(1, xm1)
    W_pw1 = col_gathered(W_pw1, 2 * Dm // N_DEV)
    ag = _mm(xm1, W_pw1, "nn", "pw1", bias=b_pw1_f)
    hg = _glu_fwd(ag, "glu")
    hd = _conv_fwd(hg, w_dw_f, b_dw_f, "conv")
    hs = _ln_silu_fwd(hd, ln_g_f, ln_b_f, "ln_silu")
    h3, y1 = _mm(hs, W_pw2, "nn", "pw2", bias=b_pw2_f, res=h2, gate=gt1[1], raw_out=True)
    h4, saved_ffn1 = ffn_fwd(h3, 1)

    dh4, sq_err, dg_final = _final_fwd_bwd(h4, g_fin, target, "loss_head")
    loss_local = (0.5 / Dm) * sq_err[0, 0:1]

    def col_shards(g, n):
        return g.reshape(Dm, N_DEV, n).transpose(1, 0, 2).reshape(N_DEV * Dm, n)

    def exchange_begin(k, parts):
        return _push_begin(parts, False, f"exchange_start{k}")

    def zero_of(handle):
        return handle[4][0:1, 0:1]

    def ffn_bwd(dh_out, h_in, saved, l, zero):
        xf, gu, act, f = saved
        df, dgt, _ = _gate_bwd(dh_out, f, gt2[l] + zero, f"gate_ffn_bwd{l}")
        dw_out = _mm_tn_shard_rows(act, df, f"ffn_out_dw{l}", BF16)
        dgu = _ffn_out_dx_swiglu(df, W_ffo[l], gu, f"ffn_out_dx{l}").reshape(N_DEV, T, FF_SHARD)
        dw_in = _mm_tn_shard_cols(xf, dgu, f"ffn_in_dw{l}", BF16)
        dxf = _mm_sum_shards(dgu, W_ffi[l], "nt", f"ffn_in_dx{l}", tm=256)
        dh_in, da, dsh = _norm_mod_bwd(h_in, g_ffn_r[l], sc2[l], dxf, dh_out, f"norm_ffn_bwd{l}")
        return dh_in, dw_in, dw_out, (dsh, da * g_ffn_r[l], dgt), da * (1.0 + sc2[l])

    dh3, dW_ffi1, dW_ffo1, dmod_ffn1, dg_ffn1 = ffn_bwd(dh4, h3, saved_ffn1, 1, 0.0)
    ex0 = exchange_begin(0, [dW_ffi1.reshape(N_DEV * Dm, FF_SHARD), dW_ffo1.reshape(D_FF, Dm)])

    dy1, dgt1_1, db_pw2 = _gate_bwd(dh3, y1, gt1[1] + zero_of(ex0), "gate_conv_bwd")
    dW_pw2 = _mm(hs, dy1, "tn", "pw2_dw", BF16, tk=2048)
    dhs = _mm(dy1, W_pw2, "nt", "pw2_dx")
    dhd, dln_g, dln_b, db_dw = _ln_silu_bwd(dhs, hd, ln_g_f, ln_b_f, "ln_silu_bwd")
    dhg, dw_dw = _conv_bwd(dhd, hg, w_dw_f, "conv_bwd")
    dag, db_pw1 = _glu_bwd(ag, dhg, "glu_bwd")
    dW_pw1 = _mm(xm1, dag, "tn", "pw1_dw", BF16, tk=2048)
    dxm1 = _mm(dag, W_pw1, "nt", "pw1_dx", tk=2048)
    dh2, da, dsh = _norm_mod_bwd(h2, g_mix_r[1], sc1[1], dxm1, dh3, "norm_mix1_bwd")
    dmod_mix1 = (dsh, da * g_mix_r[1], dgt1_1)
    dg_mix1 = da * (1.0 + sc1[1])

    ex1 = exchange_begin(1, [col_shards(dW_pw1, 2 * Dm // N_DEV), dW_pw2])
    dh1, dW_ffi0, dW_ffo0, dmod_ffn0, dg_ffn0 = ffn_bwd(dh2, h1, saved_ffn0, 0, zero_of(ex1))
    ex2 = exchange_begin(2, [dW_ffi0.reshape(N_DEV * Dm, FF_SHARD), dW_ffo0.reshape(D_FF, Dm)])

    dy0, dgt1_0, _ = _gate_bwd(dh1, y0, gt1[0] + zero_of(ex2), "gate_mix_bwd")
    dW_out = _mm(ao, dy0, "tn", "out_proj_dw", BF16, tk=2048)
    dao = _mm(dy0, W_out, "nt", "out_proj_dx")
    dq, f_acc = _attn_bwd(qh, dao, ao, lse, kpad, vpad, kt, C, "attn_bwd")
    dP, dqg, dkg, dw_sp0, db_spt0 = _mix_prep_bwd(P, dq, f_acc, dao, C, cos, sin, qg, kg, bd, w_sp0, w_spt0,
                                                  b_spt0, "mix_prep_bwd")
    dW_in = _mm(XM, dP, "tn", "in_proj_dw", BF16, tn=896, tk=2176)
    ex3 = exchange_begin(3, [col_shards(dW_in, IN_WIDTH // N_DEV), dW_out])
    dXM = _mm(dP, W_in, "nt", "in_proj_dx", tk=IN_WIDTH)
    dh0, da, dsh = _norm_mod_bwd(h0, g_mix_r[0], sc1[0] + zero_of(ex3), dXM, dh1, "norm_mix0_bwd", dxm_row_off=C)
    _, dac, dcsh = _norm_mod_bwd(ctx2, g_mix_r[0], csc1, dXM, None, "norm_ctx_bwd")
    dmod_mix0 = (dsh, da * g_mix_r[0], dgt1_0)
    dg_mix0 = da * (1.0 + sc1[0]) + dac * (1.0 + csc1)
    dcmod = jnp.concatenate([dcsh, dac * g_mix_r[0]], axis=1)

    dmod_mine = jnp.stack([jnp.concatenate(dmod_mix0 + dmod_ffn0, axis=1)[0],
                           jnp.concatenate(dmod_mix1 + dmod_ffn1, axis=1)[0]])

    small_grads = [
        ("loss", loss_local), ("g_final", dg_final), ("g_mix", jnp.concatenate([dg_mix0, dg_mix1])),
        ("g_ffn", jnp.concatenate([dg_ffn0, dg_ffn1])),
        ("q_gain", dqg[:, :HEAD_DIM] + dqg[:, HEAD_DIM:]), ("k_gain", dkg[:, :HEAD_DIM] + dkg[:, HEAD_DIM:]),
        ("w_sp", dw_sp0[None]), ("b_sp", db_spt0.T[None]), ("b_pw1", db_pw1), ("w_dw", dw_dw[None]),
        ("b_dw", db_dw), ("ln_g", dln_g), ("ln_b", dln_b), ("b_pw2", db_pw2), ("dcmod", dcmod),
        ("dmod", dmod_mine),
    ]
    buf3, offs3 = _pack([t for _, t in small_grads])
    got3 = _all_gather([buf3], "gather_small_grads", True)[0].reshape(N_DEV, buf3.shape[0], LANES)
    sum3 = _sum_devices(got3, "sum_small_grads").reshape(-1)
    off3 = {nm: off for (nm, _), off in zip(small_grads, offs3)}
    shape3 = {nm: t.shape for nm, t in small_grads}

    def summed(nm):
        return _take(sum3, off3[nm], shape3[nm])

    loss = summed("loss")[0]
    dcmod_sum = summed("dcmod")
    dmod_rows = _take(got3.reshape(N_DEV, -1), off3["dmod"], (2, 6 * Dm)).transpose(1, 0, 2)
    ctx_row = jnp.concatenate([jnp.pad(dcmod_sum, ((0, 0), (0, 4 * Dm))), jnp.zeros((1, 6 * Dm), F32)])
    dmod_all = jnp.concatenate([dmod_rows, ctx_row[:, None, :],
                                jnp.zeros((2, LANES - N_DEV - 1, 6 * Dm), F32)], axis=1)
    grads = {}
    grads["b_mod"] = summed("dmod") + ctx_row
    dmod_shard = lax.dynamic_slice_in_dim(dmod_all, me * n_mod, n_mod, axis=2)
    c_rows_t = jnp.pad(c_rows.T, ((0, 0), (0, LANES - MOD_ROWS)))
    grads["w_mod"], ds_part = _mod_bwd(c_rows_t, dmod_shard, w_mod, "mod_bwd")

    buf4, _ = _pack([ds_part[0, N_DEV]])
    got4 = _all_gather([buf4], "gather_c_ctx_grad", True)[0].reshape(N_DEV, buf4.shape[0], LANES)
    ds_ctx = _sum_devices(got4, "sum_c_ctx_grad").reshape(-1)[:Dm]
    grads["c_ctx"] = ds_ctx * _dsilu(c_ctx)

    for nm in ("g_final", "g_mix", "g_ffn", "q_gain", "k_gain", "w_sp", "b_sp"):
        grads[nm] = summed(nm).reshape(weights[nm].shape)
    for nm in ("b_pw1", "w_dw", "b_dw", "ln_g", "ln_b", "b_pw2"):
        n_loc = weights[nm].shape[-1]
        grads[nm] = lax.dynamic_slice_in_dim(summed(nm), me * n_loc, n_loc, axis=-1).reshape(weights[nm].shape)

    delta, new_m, new_v = {}, {}, {}
    shp = w_mod.shape
    outs = _adamw(w_mod.reshape(-1, shp[-1]), grads["w_mod"].reshape(-1, shp[-1]),
                  m_w_mod.reshape(-1, shp[-1]), v_w_mod.reshape(-1, shp[-1]), "adamw_w_mod")
    delta["w_mod"], new_m["w_mod"], new_v["w_mod"] = (o.reshape(shp) for o in outs)
    big_names = ("w_mod", "w_ffn_in", "w_ffn_out", "w_in", "w_out", "w_pw1", "w_pw2")
    small_names = [nm for nm in names if nm not in big_names]
    packs = [_pack([src[nm] for nm in small_names]) for src in (weights, grads, moments_m, moments_v)]
    offs_s = packs[0][1]
    outs = _adamw(*[pk[0] for pk in packs], "adamw_small")
    for o, dst in zip(outs, (delta, new_m, new_v)):
        o = o.reshape(-1)
        for nm, off in zip(small_names, offs_s):
            dst[nm] = _take(o, off, weights[nm].shape)

    def exchanged(k, handle):
        return _push_end(handle, outs[0], f"exchange_wait{k}")[1]

    r_ffi1, r_ffo1 = exchanged(0, ex0)
    r_pw1, r_pw2 = exchanged(1, ex1)
    r_ffi0, r_ffo0 = exchanged(2, ex2)
    r_in, r_out = exchanged(3, ex3)
    for nm, parts in (("w_ffn_in", [r_ffi0, r_ffi1]), ("w_ffn_out", [r_ffo0, r_ffo1]), ("w_pw1", [r_pw1]),
                      ("w_pw2", [r_pw2]), ("w_in", [r_in]), ("w_out", [r_out])):
        grads[nm], delta[nm], new_m[nm], new_v[nm] = _adamw_recv(
            weights[nm], moments_m[nm], moments_v[nm], parts, f"adamw_{nm}")

    return (loss, dh0[None], *[grads[n] for n in names], *[delta[n] for n in names],
            *[new_m[n] for n in names], *[new_v[n] for n in names])
```

```python
import math

import jax
import jax.numpy as jnp
from jax import lax
from jax.experimental import pallas as pl
from jax.experimental.pallas import tpu as pltpu

F32 = jnp.float32
BF16 = jnp.bfloat16
MESH = pl.DeviceIdType.MESH

N_DEV = 8
D_MODEL = 1024
EPS = 1e-6
HEAD_DIM = 64
ATTN_WIDTH = 512
KV_WIDTH = 128
SG_WIDTH = 512
N_SG_GROUPS = 4
CHUNK = 128
IN_WIDTH = 1792
D_FF = 2816
FF_SHARD = 2 * D_FF // N_DEV
CONV_WIDTH = 31
CONV_HALO = 16
GRID_W = 64
ROPE_THETA = 10000.0
LANES = 128
SUBLANES = 8
ROW_BLOCK = 256
ADAM_LR, ADAM_B1, ADAM_B2, ADAM_EPS, ADAM_WD, ADAM_STEP = 0.001, 0.9, 0.999, 1e-08, 0.01, 10


def _tile(n, target, mult=LANES):
    best = None
    for t in range(mult, min(n, target) + 1, mult):
        if n % t == 0:
            best = t
    return best if best is not None else n


def _sigmoid(x):
    return 1.0 / (1.0 + jnp.exp(-x))


def _silu(x):
    return x * _sigmoid(x)


def _dsilu(x):
    s = _sigmoid(x)
    return s * (1.0 + x * (1.0 - s))


_GELU_K = math.sqrt(2.0 / math.pi)


def _gelu(x):
    return 0.5 * x * (1.0 + jnp.tanh(_GELU_K * (x + 0.044715 * x * x * x)))


def _dgelu(x):
    t = jnp.tanh(_GELU_K * (x + 0.044715 * x * x * x))
    return 0.5 * (1.0 + t) + 0.5 * x * (1.0 - t * t) * _GELU_K * (1.0 + 3.0 * 0.044715 * x * x)


def _split_bf16(x):
    hi = x.astype(BF16)
    lo = (x - hi.astype(F32)).astype(BF16)
    return hi, lo


def _dot(a, b, dims):
    return lax.dot_general(a, b, (dims, ((), ())), preferred_element_type=F32)


def _dot3(a, b, dims):
    ah, al = _split_bf16(a)
    bh, bl = _split_bf16(b)
    return _dot(ah, bh, dims) + _dot(ah, bl, dims) + _dot(al, bh, dims)


NN = ((1,), (0,))
NT = ((1,), (1,))
TN = ((0,), (0,))


def _all_gather(xs, name, in_vmem):
    n_arr = len(xs)

    def body(*refs):
        x_refs, out_refs = refs[:n_arr], refs[n_arr:2 * n_arr]
        send_sems, recv_sems, local_sems = refs[2 * n_arr:]
        x, y, c = lax.axis_index("x"), lax.axis_index("y"), lax.axis_index("c")
        me, sibling = (x, y, c), (x, y, 1 - c)
        chips = [(1 - x, y), (x, 1 - y), (1 - x, 1 - y)]

        def rows(a, px, py, pc):
            m_per = xs[a].shape[0]
            return out_refs[a].at[pl.ds((4 * px + 2 * py + pc) * m_per, m_per), :]

        def copy(a, k, block, to, src=None):
            return pltpu.make_async_remote_copy(
                src_ref=rows(a, *block) if src is None else src,
                dst_ref=rows(a, *block),
                send_sem=send_sems.at[7 * a + k],
                recv_sem=recv_sems.at[7 * a + k],
                device_id=to,
                device_id_type=MESH,
            )

        mine, first, passed = [], [], []
        for a in range(n_arr):
            mine.append(pltpu.make_async_copy(x_refs[a], rows(a, *me), local_sems.at[a]))
            mine[-1].start()
            first.append(copy(a, 0, me, sibling, src=x_refs[a]))
            first += [copy(a, 1 + j, me, (*chip, c), src=x_refs[a]) for j, chip in enumerate(chips)]
        for cp in first:
            cp.start()
        for a in range(n_arr):
            for j, chip in enumerate(chips):
                copy(a, 1 + j, (*chip, c), me).wait_recv()
                passed.append(copy(a, 4 + j, (*chip, c), sibling))
                passed[-1].start()
        for a in range(n_arr):
            copy(a, 0, sibling, me).wait_recv()
            for j, chip in enumerate(chips):
                copy(a, 4 + j, (*chip, 1 - c), me).wait_recv()
        for cp in first + passed:
            cp.wait_send()
        for cp in mine:
            cp.wait()

    space = pltpu.VMEM if in_vmem else pl.ANY
    return pl.pallas_call(
        body,
        name=name,
        out_shape=[jax.ShapeDtypeStruct((N_DEV * t.shape[0], t.shape[1]), t.dtype) for t in xs],
        in_specs=[pl.BlockSpec(memory_space=space)] * n_arr,
        out_specs=[pl.BlockSpec(memory_space=space)] * n_arr,
        scratch_shapes=[
            pltpu.SemaphoreType.DMA((7 * n_arr,)),
            pltpu.SemaphoreType.DMA((7 * n_arr,)),
            pltpu.SemaphoreType.DMA((n_arr,)),
        ],
    )(*xs)


HBM_SPEC = pl.BlockSpec(memory_space=pltpu.HBM)
SEM_SPEC = pl.BlockSpec(memory_space=pltpu.SEMAPHORE)
DATAFLOW_EFFECT = pltpu.SideEffectType.DATAFLOW_SIDE_EFFECTING


def _peers(x, y, c):
    for k in range(1, N_DEV):
        px = 1 - x if (k >> 2) & 1 else x
        py = 1 - y if (k >> 1) & 1 else y
        pc = 1 - c if k & 1 else c
        yield k - 1, (px, py, pc), 4 * px + 2 * py + pc


def _push_copies(src_refs, land_refs, send_sems, recv_sems, shapes, whole_src):
    x, y, c = lax.axis_index("x"), lax.axis_index("y"), lax.axis_index("c")
    me = 4 * x + 2 * y + c
    for a, (m_per, _) in enumerate(shapes):
        def block(ref, idx, m_per=m_per):
            return ref.at[pl.ds(idx * m_per, m_per), :]

        for k, peer, pidx in _peers(x, y, c):
            src = src_refs[a] if whole_src else block(src_refs[a], pidx)
            sems = dict(send_sem=send_sems.at[N_DEV * a + k], recv_sem=recv_sems.at[N_DEV * a + k],
                        device_id=peer, device_id_type=MESH)
            yield (pltpu.make_async_remote_copy(src_ref=src, dst_ref=block(land_refs[a], me), **sems),
                   pltpu.make_async_remote_copy(src_ref=src, dst_ref=block(land_refs[a], pidx), **sems))


def _own_copies(src_refs, land_refs, recv_sems, shapes, whole_src):
    me = 4 * lax.axis_index("x") + 2 * lax.axis_index("y") + lax.axis_index("c")
    for a, (m_per, _) in enumerate(shapes):
        mine = pl.ds(me * m_per, m_per)
        src = src_refs[a] if whole_src else src_refs[a].at[mine, :]
        yield pltpu.make_async_copy(src, land_refs[a].at[mine, :], recv_sems.at[N_DEV * a + N_DEV - 1])


def _push_begin(srcs, whole_src, name):
    n_arr = len(srcs)
    shapes = [(t.shape[0] if whole_src else t.shape[0] // N_DEV, t.shape[1]) for t in srcs]
    lands = [lax.empty((N_DEV * m, n), t.dtype) for (m, n), t in zip(shapes, srcs)]

    def body(*refs):
        src_refs, land_refs = refs[:n_arr], refs[n_arr:2 * n_arr]
        send_sems, recv_sems = refs[2 * n_arr], refs[2 * n_arr + 1]
        token = refs[-1]
        for outgoing, _ in _push_copies(src_refs, land_refs, send_sems, recv_sems, shapes, whole_src):
            outgoing.start()
        for own in _own_copies(src_refs, land_refs, recv_sems, shapes, whole_src):
            own.start()
        token[...] = jnp.zeros_like(token)

    operands = [pltpu.with_memory_space_constraint(t, pltpu.HBM) for t in list(srcs) + lands]
    outs = pl.pallas_call(
        body, name=name,
        out_shape=(pltpu.SemaphoreType.DMA((N_DEV * n_arr,)), pltpu.SemaphoreType.DMA((N_DEV * n_arr,)),
                   *[pltpu.HBM(t.shape, t.dtype) for t in operands],
                   jax.ShapeDtypeStruct((SUBLANES, LANES), F32)),
        in_specs=[HBM_SPEC] * (2 * n_arr),
        out_specs=(SEM_SPEC, SEM_SPEC, *[HBM_SPEC] * (2 * n_arr), pl.BlockSpec(memory_space=pltpu.VMEM)),
        input_output_aliases={i: 2 + i for i in range(2 * n_arr)},
        compiler_params=pltpu.CompilerParams(has_side_effects=DATAFLOW_EFFECT),
    )(*operands)
    return outs[0], outs[1], list(outs[2:2 + n_arr]), list(outs[2 + n_arr:2 + 2 * n_arr]), outs[-1], whole_src


def _push_end(handle, after, name):
    send_sems, recv_sems, srcs, lands, _, whole_src = handle
    n_arr = len(srcs)
    shapes = [(t.shape[0] // N_DEV, t.shape[1]) for t in lands]

    def body(*refs):
        src_refs, land_refs = refs[:n_arr], refs[n_arr:2 * n_arr]
        send_sems_ref, recv_sems_ref = refs[2 * n_arr], refs[2 * n_arr + 1]
        for outgoing, incoming in _push_copies(src_refs, land_refs, send_sems_ref, recv_sems_ref, shapes, whole_src):
            outgoing.wait_send()
            incoming.wait_recv()
        for own in _own_copies(src_refs, land_refs, recv_sems_ref, shapes, whole_src):
            own.wait()

    outs = pl.pallas_call(
        body, name=name,
        out_shape=tuple(pltpu.HBM(t.shape, t.dtype) for t in srcs + lands),
        in_specs=[HBM_SPEC] * (2 * n_arr) + [SEM_SPEC, SEM_SPEC, pl.BlockSpec(memory_space=pl.ANY)],
        out_specs=tuple([HBM_SPEC] * (2 * n_arr)),
        input_output_aliases={i: i for i in range(2 * n_arr)},
        compiler_params=pltpu.CompilerParams(has_side_effects=DATAFLOW_EFFECT),
    )(*srcs, *lands, send_sems, recv_sems, after)
    return list(outs[:n_arr]), list(outs[n_arr:])


def _sum_devices(r, name, rows_per_step=ROW_BLOCK):
    _, m, n = r.shape
    tm = _tile(m, rows_per_step, 8)

    def body(r_ref, o_ref):
        acc = r_ref[0].astype(F32)
        for s in range(1, N_DEV):
            acc = acc + r_ref[s].astype(F32)
        o_ref[...] = acc

    return pl.pallas_call(
        body,
        name=name,
        grid=(m // tm,),
        out_shape=jax.ShapeDtypeStruct((m, n), F32),
        in_specs=[pl.BlockSpec((N_DEV, tm, n), lambda i: (0, i, 0))],
        out_specs=pl.BlockSpec((tm, n), lambda i: (i, 0)),
        compiler_params=pltpu.CompilerParams(dimension_semantics=("parallel",)),
    )(r)


def _get(ref):
    return ref[0] if len(ref.shape) == 3 else ref[...]


def _put(ref, val):
    if len(ref.shape) == 3:
        ref[0] = val
    else:
        ref[...] = val


def _norm_mod(hv, g, sc, sh):
    r = lax.rsqrt(jnp.mean(hv * hv, axis=-1, keepdims=True) + EPS)
    return (hv * r) * g * (1.0 + sc) + sh


def _mm_call(name, a, b, a_spec, b_spec, out_sds, o_spec, grid, dims, acc_shape, bias=None,
             res=None, gate=None, raw_out=False, vec_spec=None, norm=None):
    nk = grid[2]
    operands, in_specs = [a, b], [a_spec, b_spec]
    if bias is not None:
        operands.append(bias)
        in_specs.append(vec_spec)
    if res is not None:
        operands += [res, gate]
        in_specs += [o_spec, vec_spec]
    if norm is not None:
        assert grid[1] == 1
        operands += list(norm)
        in_specs += [vec_spec] * 3
    out_shape, out_specs = [out_sds], [o_spec]
    if raw_out:
        out_shape.append(jax.ShapeDtypeStruct(out_sds.shape, BF16))
        out_specs.append(o_spec)
    if norm is not None:
        out_shape.append(jax.ShapeDtypeStruct(out_sds.shape, BF16))
        out_specs.append(o_spec)

    def body(*refs):
        it = iter(refs)
        a_ref, b_ref = next(it), next(it)
        bias_ref = next(it) if bias is not None else None
        res_ref, gate_ref = (next(it), next(it)) if res is not None else (None, None)
        norm_refs = (next(it), next(it), next(it)) if norm is not None else None
        o_ref = next(it)
        raw_ref = next(it) if raw_out else None
        xn_ref = next(it) if norm is not None else None
        acc = next(it) if nk > 1 else None
        k = pl.program_id(2)
        part = _dot(_get(a_ref).astype(BF16), _get(b_ref).astype(BF16), dims)

        def finish(y):
            if bias_ref is not None:
                y = y + bias_ref[...]
            if raw_ref is not None:
                raw_ref[...] = y.astype(BF16)
            if res_ref is not None:
                y = res_ref[...] + gate_ref[...] * y
            _put(o_ref, y.astype(out_sds.dtype))
            if xn_ref is not None:
                xn_ref[...] = _norm_mod(y, *[r[...] for r in norm_refs]).astype(BF16)

        if nk == 1:
            finish(part)
        else:
            @pl.when(k == 0)
            def _():
                acc[...] = part

            @pl.when(k > 0)
            def _():
                acc[...] += part

            @pl.when(k == nk - 1)
            def _():
                finish(acc[...])

    outs = pl.pallas_call(
        body,
        name=name,
        grid=grid,
        out_shape=out_shape,
        in_specs=in_specs,
        out_specs=out_specs,
        scratch_shapes=[pltpu.VMEM(acc_shape, F32)] if nk > 1 else [],
        compiler_params=pltpu.CompilerParams(dimension_semantics=("parallel", "parallel", "arbitrary")),
    )(*operands)
    return outs if len(outs) > 1 else outs[0]


def _mm(a, b, mode, name, out_dtype=F32, bias=None, res=None, gate=None, raw_out=False,
        tm=512, tn=1024, tk=1024, a_row_off=0, norm=None):
    if mode == "nn":
        K, N = b.shape
        M = a.shape[0] - a_row_off
    elif mode == "nt":
        N, K = b.shape
        M = a.shape[0] - a_row_off
    else:
        (K, M), N = a.shape, b.shape[1]
    tm, tn, tk = _tile(M, tm), _tile(N, tn), _tile(K, tk)
    off = a_row_off // tm
    dims = {"nn": NN, "nt": NT, "tn": TN}[mode]
    a_spec = (pl.BlockSpec((tk, tm), lambda i, j, k: (k, i)) if mode == "tn"
              else pl.BlockSpec((tm, tk), lambda i, j, k: (i + off, k)))
    b_spec = (pl.BlockSpec((tn, tk), lambda i, j, k: (j, k)) if mode == "nt"
              else pl.BlockSpec((tk, tn), lambda i, j, k: (k, j)))
    return _mm_call(name, a, b, a_spec, b_spec, jax.ShapeDtypeStruct((M, N), out_dtype),
                    pl.BlockSpec((tm, tn), lambda i, j, k: (i, j)), (M // tm, N // tn, K // tk), dims,
                    (tm, tn), bias, res, gate, raw_out, pl.BlockSpec((1, tn), lambda i, j, k: (0, j)), norm)


def _mm_sum_shards(a3, b3, mode, name, out_dtype=F32, res=None, gate=None, raw_out=False, tm=512, norm=None):
    S, M, kk = a3.shape
    N = b3.shape[2] if mode == "nn" else b3.shape[1]
    tm = _tile(M, tm)
    dims = NN if mode == "nn" else NT
    has_res = res is not None

    def body(*refs):
        it = iter(refs)
        a_ref, b_ref = next(it), next(it)
        res_ref, gate_ref = (next(it), next(it)) if has_res else (None, None)
        norm_refs = (next(it), next(it), next(it)) if norm is not None else None
        o_ref = next(it)
        raw_ref = next(it) if raw_out else None
        xn_ref = next(it) if norm is not None else None
        y = _dot(a_ref[0], b_ref[0], dims)
        for s in range(1, S):
            y = y + _dot(a_ref[s], b_ref[s], dims)
        if raw_ref is not None:
            raw_ref[...] = y.astype(BF16)
        if has_res:
            y = res_ref[...] + gate_ref[...] * y
        o_ref[...] = y.astype(out_dtype)
        if xn_ref is not None:
            xn_ref[...] = _norm_mod(y, *[r[...] for r in norm_refs]).astype(BF16)

    tile = pl.BlockSpec((tm, N), lambda i: (i, 0))
    operands = [a3, b3] + ([res, gate] if has_res else []) + (list(norm) if norm is not None else [])
    in_specs = [pl.BlockSpec((S, tm, kk), lambda i: (0, i, 0)), pl.BlockSpec(b3.shape, lambda i: (0, 0, 0))]
    in_specs += [tile, _vec_spec(N)] if has_res else []
    in_specs += [_vec_spec(N)] * 3 if norm is not None else []
    out_shape = [jax.ShapeDtypeStruct((M, N), out_dtype)] + ([jax.ShapeDtypeStruct((M, N), BF16)] if raw_out else [])
    out_shape += [jax.ShapeDtypeStruct((M, N), BF16)] if norm is not None else []
    outs = pl.pallas_call(
        body, name=name, grid=(M // tm,),
        out_shape=out_shape, in_specs=in_specs, out_specs=[tile] * len(out_shape),
        compiler_params=pltpu.CompilerParams(dimension_semantics=("parallel",)),
    )(*operands)
    return outs if len(outs) > 1 else outs[0]


def _mm_tn_shard_rows(a3, b, name, out_dtype, tn=1024, tk=1024):
    S, T, m = a3.shape
    N = b.shape[1]
    tn, tk = _tile(N, tn), _tile(T, tk)
    return _mm_call(name, a3, b, pl.BlockSpec((1, tk, m), lambda i, j, k: (i, k, 0)),
                    pl.BlockSpec((tk, tn), lambda i, j, k: (k, j)), jax.ShapeDtypeStruct((S, m, N), out_dtype),
                    pl.BlockSpec((1, m, tn), lambda i, j, k: (i, 0, j)), (S, N // tn, T // tk), TN, (m, tn))


def _mm_tn_shard_cols(a, b3, name, out_dtype, tm=1024, tk=1024):
    T, M = a.shape
    S, _, n = b3.shape
    tm, tk = _tile(M, tm), _tile(T, tk)
    return _mm_call(name, a, b3, pl.BlockSpec((tk, tm), lambda i, j, k: (k, i)),
                    pl.BlockSpec((1, tk, n), lambda i, j, k: (j, k, 0)), jax.ShapeDtypeStruct((S, M, n), out_dtype),
                    pl.BlockSpec((1, tm, n), lambda i, j, k: (j, i, 0)), (M // tm, S, T // tk), TN, (tm, n))


def _row_spec(tm, width, off=0):
    return pl.BlockSpec((tm, width), lambda i: (i + off, 0))


def _vec_spec(width):
    return pl.BlockSpec((1, width), lambda i: (0, 0))


def _norm_mod_fwd_cat(hc, h, g, csc, csh, sc, sh, name):
    (C, Dm), T = hc.shape, h.shape[0]
    tm = _tile(math.gcd(C, T), ROW_BLOCK, 8)
    off = C // tm

    def body(hc_ref, h_ref, g_ref, csc_ref, csh_ref, sc_ref, sh_ref, o_ref):
        is_ctx = pl.program_id(0) < off
        hv = jnp.where(is_ctx, hc_ref[...], h_ref[...])
        scv = jnp.where(is_ctx, csc_ref[...], sc_ref[...])
        shv = jnp.where(is_ctx, csh_ref[...], sh_ref[...])
        r = lax.rsqrt(jnp.mean(hv * hv, axis=-1, keepdims=True) + EPS)
        o_ref[...] = ((hv * r) * g_ref[...] * (1.0 + scv) + shv).astype(BF16)

    return pl.pallas_call(
        body, name=name, grid=((C + T) // tm,),
        out_shape=jax.ShapeDtypeStruct((C + T, Dm), BF16),
        in_specs=[pl.BlockSpec((tm, Dm), lambda i: (jnp.minimum(i, off - 1), 0)),
                  pl.BlockSpec((tm, Dm), lambda i: (jnp.maximum(i - off, 0), 0))] + [_vec_spec(Dm)] * 5,
        out_specs=_row_spec(tm, Dm),
        compiler_params=pltpu.CompilerParams(dimension_semantics=("parallel",)),
    )(hc, h, g, csc, csh, sc, sh)


def _gate_grads(dh, y_ref, gt_ref, dy_ref, dgt_ref, dsum_ref):
    dy = dh * gt_ref[...]
    dgt_ref[...] += jnp.sum(dh * y_ref[...].astype(F32), axis=0, keepdims=True)
    dsum_ref[...] += jnp.sum(dy, axis=0, keepdims=True)
    dy_ref[...] = dy.astype(BF16)


def _norm_mod_bwd(h, g, sc, dxm, dres, name, dxm_row_off=0, gate=None):
    R, Dm = h.shape
    tm = _tile(R, ROW_BLOCK, 8)
    off = dxm_row_off // tm
    has_res = dres is not None
    has_gate = gate is not None

    def body(*refs):
        it = iter(refs)
        h_ref, g_ref, sc_ref, dx_ref = next(it), next(it), next(it), next(it)
        dres_ref = next(it) if has_res else None
        y_ref, gt_ref = (next(it), next(it)) if has_gate else (None, None)
        dh_ref, da_ref, dsh_ref = next(it), next(it), next(it)
        gate_out = (next(it), next(it), next(it)) if has_gate else ()
        i = pl.program_id(0)

        @pl.when(i == 0)
        def _():
            for ref in (da_ref, dsh_ref) + gate_out[1:]:
                ref[...] = jnp.zeros_like(ref)

        hv = h_ref[...]
        dx = dx_ref[...].astype(F32)
        r = lax.rsqrt(jnp.mean(hv * hv, axis=-1, keepdims=True) + EPS)
        n = hv * r
        da_ref[...] += jnp.sum(dx * n, axis=0, keepdims=True)
        dsh_ref[...] += jnp.sum(dx, axis=0, keepdims=True)
        dn = dx * (g_ref[...] * (1.0 + sc_ref[...]))
        dh = r * (dn - n * jnp.mean(dn * n, axis=-1, keepdims=True))
        if has_res:
            dh = dh + dres_ref[...]
        dh_ref[...] = dh
        if has_gate:
            _gate_grads(dh, y_ref, gt_ref, *gate_out)

    operands = [h, g, sc, dxm] + ([dres] if has_res else []) + (list(gate) if has_gate else [])
    in_specs = [_row_spec(tm, Dm), _vec_spec(Dm), _vec_spec(Dm), _row_spec(tm, Dm, off)]
    in_specs += [_row_spec(tm, Dm)] if has_res else []
    in_specs += [_row_spec(tm, Dm), _vec_spec(Dm)] if has_gate else []
    vec = jax.ShapeDtypeStruct((1, Dm), F32)
    out_shape = [jax.ShapeDtypeStruct((R, Dm), F32), vec, vec]
    out_specs = [_row_spec(tm, Dm), _vec_spec(Dm), _vec_spec(Dm)]
    if has_gate:
        out_shape += [jax.ShapeDtypeStruct((R, Dm), BF16), vec, vec]
        out_specs += [_row_spec(tm, Dm), _vec_spec(Dm), _vec_spec(Dm)]
    return pl.pallas_call(
        body, name=name, grid=(R // tm,),
        out_shape=out_shape, in_specs=in_specs, out_specs=out_specs,
        compiler_params=pltpu.CompilerParams(dimension_semantics=("arbitrary",)),
    )(*operands)


def _ffn_in_swiglu(xf, w3, name, tm=1024):
    T, K = xf.shape
    S, _, n = w3.shape
    half = S // 2
    tm = _tile(T, tm)

    def body(a_ref, wg_ref, wu_ref, gu_ref, act_ref):
        a = a_ref[...]
        g = _dot(a, wg_ref[0], NN)
        u = _dot(a, wu_ref[0], NN)
        gu_ref[0, 0] = g.astype(BF16)
        gu_ref[1, 0] = u.astype(BF16)
        act_ref[0] = (_silu(g) * u).astype(BF16)

    return pl.pallas_call(
        body, name=name, grid=(T // tm, half),
        out_shape=[jax.ShapeDtypeStruct((2, half, T, n), BF16), jax.ShapeDtypeStruct((half, T, n), BF16)],
        in_specs=[pl.BlockSpec((tm, K), lambda i, j: (i, 0)),
                  pl.BlockSpec((1, K, n), lambda i, j: (j, 0, 0)),
                  pl.BlockSpec((1, K, n), lambda i, j: (j + half, 0, 0))],
        out_specs=[pl.BlockSpec((2, 1, tm, n), lambda i, j: (0, j, i, 0)),
                   pl.BlockSpec((1, tm, n), lambda i, j: (j, i, 0))],
        compiler_params=pltpu.CompilerParams(dimension_semantics=("parallel", "parallel")),
    )(xf, w3, w3)


def _ffn_out_dx_swiglu(df, wo, gu, name, tm=1024):
    T, Dm = df.shape
    half, n, _ = wo.shape
    tm = _tile(T, tm)

    def body(df_ref, w_ref, gu_ref, o_ref):
        da = _dot(df_ref[...], w_ref[0], NT)
        g = gu_ref[0, 0].astype(F32)
        u = gu_ref[1, 0].astype(F32)
        s = _sigmoid(g)
        o_ref[0, 0] = (da * u * (s * (1.0 + g * (1.0 - s)))).astype(BF16)
        o_ref[1, 0] = (da * (g * s)).astype(BF16)

    gu_spec = pl.BlockSpec((2, 1, tm, n), lambda i, j: (0, j, i, 0))
    return pl.pallas_call(
        body, name=name, grid=(T // tm, half),
        out_shape=jax.ShapeDtypeStruct(gu.shape, BF16),
        in_specs=[pl.BlockSpec((tm, Dm), lambda i, j: (i, 0)),
                  pl.BlockSpec((1, n, Dm), lambda i, j: (j, 0, 0)), gu_spec],
        out_specs=gu_spec,
        compiler_params=pltpu.CompilerParams(dimension_semantics=("parallel", "parallel")),
    )(df, wo, gu)


def _glu_fwd(ag, name):
    R = ag.shape[0]
    tm = _tile(R, ROW_BLOCK, 8)

    def body(ag_ref, o_ref):
        o_ref[...] = ag_ref[:, :D_MODEL].astype(F32) * _sigmoid(ag_ref[:, D_MODEL:].astype(F32))

    return pl.pallas_call(
        body, name=name, grid=(R // tm,),
        out_shape=jax.ShapeDtypeStruct((R, D_MODEL), F32),
        in_specs=[_row_spec(tm, 2 * D_MODEL)],
        out_specs=_row_spec(tm, D_MODEL),
        compiler_params=pltpu.CompilerParams(dimension_semantics=("parallel",)),
    )(ag)


def _glu_bwd(ag, dhg, name):
    R = ag.shape[0]
    tm = _tile(R, ROW_BLOCK, 8)

    def body(ag_ref, dh_ref, o_ref, s_ref):
        i = pl.program_id(0)

        @pl.when(i == 0)
        def _():
            s_ref[...] = jnp.zeros_like(s_ref)

        a = ag_ref[:, :D_MODEL].astype(F32)
        s = _sigmoid(ag_ref[:, D_MODEL:].astype(F32))
        dh = dh_ref[...]
        da = dh * s
        dg = dh * a * s * (1.0 - s)
        o_ref[:, :D_MODEL] = da.astype(BF16)
        o_ref[:, D_MODEL:] = dg.astype(BF16)
        s_ref[:, :D_MODEL] += jnp.sum(da, axis=0, keepdims=True)
        s_ref[:, D_MODEL:] += jnp.sum(dg, axis=0, keepdims=True)

    return pl.pallas_call(
        body, name=name, grid=(R // tm,),
        out_shape=[jax.ShapeDtypeStruct((R, 2 * D_MODEL), BF16), jax.ShapeDtypeStruct((1, 2 * D_MODEL), F32)],
        in_specs=[_row_spec(tm, 2 * D_MODEL), _row_spec(tm, D_MODEL)],
        out_specs=[_row_spec(tm, 2 * D_MODEL), _vec_spec(2 * D_MODEL)],
        compiler_params=pltpu.CompilerParams(dimension_semantics=("arbitrary",)),
    )(ag, dhg)


def _halo_specs(tm, nblk, width):
    per = tm // CONV_HALO
    prev = pl.BlockSpec((CONV_HALO, width), lambda i: (jnp.maximum(i * per - 1, 0), 0))
    nxt = pl.BlockSpec((CONV_HALO, width), lambda i: (jnp.minimum((i + 1) * per, nblk * per - 1), 0))
    return prev, nxt


def _fill_halo(scr, prev_ref, cur_ref, next_ref, i, nblk, tm):
    scr[0:CONV_HALO, :] = jnp.where(i > 0, prev_ref[...], 0.0)
    scr[CONV_HALO:CONV_HALO + tm, :] = cur_ref[...]
    scr[CONV_HALO + tm:2 * CONV_HALO + tm, :] = jnp.where(i < nblk - 1, next_ref[...], 0.0)


CONV_ROWS = 128


def _windows(scr, cols, tm):
    reach = (CONV_WIDTH // SUBLANES) * SUBLANES
    for r in range(SUBLANES):
        base = scr[pl.ds(r, tm + reach), cols]
        for a in range(reach // SUBLANES + 1):
            off = SUBLANES * a + r
            if 1 <= off <= CONV_WIDTH:
                yield off, base[SUBLANES * a:SUBLANES * a + tm]


def _conv_fwd(hg, w_dw, b_dw, name):
    R, Dm = hg.shape
    tm = _tile(R, CONV_ROWS, CONV_HALO)
    nblk = R // tm
    prev_spec, next_spec = _halo_specs(tm, nblk, Dm)

    def body(prev_ref, cur_ref, next_ref, w_ref, bdw_ref, hd_ref, scr):
        _fill_halo(scr, prev_ref, cur_ref, next_ref, pl.program_id(0), nblk, tm)
        for cb in range(Dm // LANES):
            cols = slice(cb * LANES, (cb + 1) * LANES)
            acc = jnp.zeros((tm, LANES), F32) + bdw_ref[:, cols]
            for off, win in _windows(scr, cols, tm):
                acc = acc + w_ref[off - 1:off, cols] * win
            hd_ref[:, cols] = acc

    return pl.pallas_call(
        body, name=name, grid=(nblk,),
        out_shape=jax.ShapeDtypeStruct((R, Dm), F32),
        in_specs=[prev_spec, _row_spec(tm, Dm), next_spec,
                  pl.BlockSpec((CONV_WIDTH, Dm), lambda i: (0, 0)), _vec_spec(Dm)],
        out_specs=_row_spec(tm, Dm),
        scratch_shapes=[pltpu.VMEM((tm + 2 * CONV_HALO, Dm), F32)],
        compiler_params=pltpu.CompilerParams(dimension_semantics=("parallel",)),
    )(hg, hg, hg, w_dw, b_dw)


def _ln_silu_fwd(hd, ln_g, ln_b, name):
    R, Dm = hd.shape
    tm = _tile(R, ROW_BLOCK, 8)

    def body(hd_ref, g_ref, b_ref, hs_ref):
        hd = hd_ref[...]
        xc = hd - jnp.mean(hd, axis=-1, keepdims=True)
        rs = lax.rsqrt(jnp.mean(xc * xc, axis=-1, keepdims=True) + EPS)
        hs_ref[...] = _silu(xc * rs * g_ref[...] + b_ref[...]).astype(BF16)

    return pl.pallas_call(
        body, name=name, grid=(R // tm,),
        out_shape=jax.ShapeDtypeStruct((R, Dm), BF16),
        in_specs=[_row_spec(tm, Dm), _vec_spec(Dm), _vec_spec(Dm)],
        out_specs=_row_spec(tm, Dm),
        compiler_params=pltpu.CompilerParams(dimension_semantics=("parallel",)),
    )(hd, ln_g, ln_b)


def _ln_silu_bwd(dhs, hd, ln_g, ln_b, name):
    R, Dm = hd.shape
    tm = _tile(R, ROW_BLOCK, 8)

    def body(dhs_ref, hd_ref, g_ref, b_ref, dhd_ref, dg_ref, db_ref, dsum_ref):
        i = pl.program_id(0)

        @pl.when(i == 0)
        def _():
            dg_ref[...] = jnp.zeros_like(dg_ref)
            db_ref[...] = jnp.zeros_like(db_ref)
            dsum_ref[...] = jnp.zeros_like(dsum_ref)

        hd = hd_ref[...]
        mu = jnp.mean(hd, axis=-1, keepdims=True)
        xc = hd - mu
        rs = lax.rsqrt(jnp.mean(xc * xc, axis=-1, keepdims=True) + EPS)
        z = xc * rs
        hl = z * g_ref[...] + b_ref[...]
        dhl = dhs_ref[...].astype(F32) * _dsilu(hl)
        dg_ref[...] += jnp.sum(dhl * z, axis=0, keepdims=True)
        db_ref[...] += jnp.sum(dhl, axis=0, keepdims=True)
        dz = dhl * g_ref[...]
        dhd = rs * (dz - jnp.mean(dz, axis=-1, keepdims=True) - z * jnp.mean(dz * z, axis=-1, keepdims=True))
        dsum_ref[...] += jnp.sum(dhd, axis=0, keepdims=True)
        dhd_ref[...] = dhd

    return pl.pallas_call(
        body, name=name, grid=(R // tm,),
        out_shape=[jax.ShapeDtypeStruct((R, Dm), F32)] + [jax.ShapeDtypeStruct((1, Dm), F32)] * 3,
        in_specs=[_row_spec(tm, Dm), _row_spec(tm, Dm), _vec_spec(Dm), _vec_spec(Dm)],
        out_specs=[_row_spec(tm, Dm), _vec_spec(Dm), _vec_spec(Dm), _vec_spec(Dm)],
        compiler_params=pltpu.CompilerParams(dimension_semantics=("arbitrary",)),
    )(dhs, hd, ln_g, ln_b)


def _conv_bwd(dhd, hg, w_dw, name):
    R, Dm = hg.shape
    tm = _tile(R, CONV_ROWS, CONV_HALO)
    nblk = R // tm
    prev_spec, next_spec = _halo_specs(tm, nblk, Dm)

    def body(dprev, dcur, dnext, gprev, gcur, gnext, w_ref, dhg_ref, dw_ref, dscr, gscr, dwp):
        i = pl.program_id(0)

        @pl.when(i == 0)
        def _():
            dwp[...] = jnp.zeros_like(dwp)

        _fill_halo(dscr, dprev, dcur, dnext, i, nblk, tm)
        _fill_halo(gscr, gprev, gcur, gnext, i, nblk, tm)
        for cb in range(Dm // LANES):
            cols = slice(cb * LANES, (cb + 1) * LANES)
            acc = jnp.zeros((tm, LANES), F32)
            for off, win in _windows(dscr, cols, tm):
                j = CONV_WIDTH - off
                acc = acc + w_ref[j:j + 1, cols] * win
            dhg_ref[:, cols] = acc
            d_here = dcur[:, cols]
            for off, win in _windows(gscr, cols, tm):
                j = off - 1
                prod = d_here * win
                part = prod[0:SUBLANES]
                for k in range(1, tm // SUBLANES):
                    part = part + prod[k * SUBLANES:(k + 1) * SUBLANES]
                dwp[j * SUBLANES:(j + 1) * SUBLANES, cols] += part

        @pl.when(i == nblk - 1)
        def _():
            for j in range(CONV_WIDTH):
                dw_ref[j:j + 1, :] = jnp.sum(dwp[j * SUBLANES:(j + 1) * SUBLANES, :], axis=0, keepdims=True)

    return pl.pallas_call(
        body, name=name, grid=(nblk,),
        out_shape=[jax.ShapeDtypeStruct((R, Dm), F32), jax.ShapeDtypeStruct((CONV_WIDTH, Dm), F32)],
        in_specs=[prev_spec, _row_spec(tm, Dm), next_spec, prev_spec, _row_spec(tm, Dm), next_spec,
                  pl.BlockSpec((CONV_WIDTH, Dm), lambda i: (0, 0))],
        out_specs=[_row_spec(tm, Dm), pl.BlockSpec((CONV_WIDTH, Dm), lambda i: (0, 0))],
        scratch_shapes=[pltpu.VMEM((tm + 2 * CONV_HALO, Dm), F32)] * 2
        + [pltpu.VMEM((CONV_WIDTH * SUBLANES, Dm), F32)],
        compiler_params=pltpu.CompilerParams(dimension_semantics=("arbitrary",)),
    )(dhd, dhd, dhd, hg, hg, hg, w_dw)


def _swap16(y, lane):
    return jnp.where((lane & 16) == 0, pltpu.roll(y, LANES - 16, 1), pltpu.roll(y, 16, 1))


def _head_mean(v, bd):
    hi, lo = _split_bf16(v)
    return (_dot(hi, bd, NN) + _dot(lo, bd, NN)) * (1.0 / HEAD_DIM)


Q_COLS = (0, ATTN_WIDTH)
K_COLS = (ATTN_WIDTH, ATTN_WIDTH + HEAD_DIM * 2)
V_COLS = (K_COLS[1], K_COLS[1] + HEAD_DIM * 2)
SU_COLS = (V_COLS[1], V_COLS[1] + SG_WIDTH)
SV_COLS = (SU_COLS[1], SU_COLS[1] + SG_WIDTH)


def _mix_prep_fwd(p, ctx_rows, cos, sin, qg, kg, bd, w_sp, b_spt, name):
    TT = p.shape[0]
    off = ctx_rows // CHUNK
    q_scale = HEAD_DIM ** -0.5

    def body(p_ref, cos_ref, sin_ref, qg_ref, kg_ref, bd_ref, w_ref, b_ref,
             q_ref, kp_ref, vp_ref, kt_ref, sg_ref):
        lane = lax.broadcasted_iota(jnp.int32, (CHUNK, LANES), 1)
        low = lane < HEAD_DIM
        cs, sn, bdv = cos_ref[...], sin_ref[...], bd_ref[...]

        def norm_rope(xv, gain):
            r = lax.rsqrt(_head_mean(xv * xv, bdv) + EPS)
            yv = xv * r * gain
            return yv * cs + _swap16(yv, lane) * sn

        def pad_heads(ref, t):
            tr = pltpu.roll(t, HEAD_DIM, 1)
            ref[0, 0] = jnp.where(low, t, 0.0).astype(BF16)
            ref[0, 1] = jnp.where(low, 0.0, tr).astype(BF16)
            ref[1, 0] = jnp.where(low, tr, 0.0).astype(BF16)
            ref[1, 1] = jnp.where(low, 0.0, t).astype(BF16)

        for a in range(ATTN_WIDTH // LANES):
            xv = p_ref[:, a * LANES:(a + 1) * LANES]
            q_ref[:, a * LANES:(a + 1) * LANES] = (norm_rope(xv, qg_ref[...]) * q_scale).astype(BF16)
        kh = norm_rope(p_ref[:, K_COLS[0]:K_COLS[1]], kg_ref[...])
        pad_heads(kp_ref, kh)
        pad_heads(vp_ref, p_ref[:, V_COLS[0]:V_COLS[1]])
        kht = kh.T
        kt_ref[0] = kht[:HEAD_DIM].astype(BF16)
        kt_ref[1] = kht[HEAD_DIM:].astype(BF16)
        for g in range(N_SG_GROUPS):
            u = _gelu(p_ref[:, SU_COLS[0] + g * LANES:SU_COLS[0] + (g + 1) * LANES])
            vg = _gelu(p_ref[:, SV_COLS[0] + g * LANES:SV_COLS[0] + (g + 1) * LANES])
            xc = vg - jnp.mean(vg, axis=-1, keepdims=True)
            vn = xc * lax.rsqrt(jnp.mean(xc * xc, axis=-1, keepdims=True) + EPS)
            mixed = _dot(w_ref[g].astype(BF16), vn.astype(BF16), NN) + b_ref[:, g:g + 1]
            sg_ref[:, g * LANES:(g + 1) * LANES] = (u * mixed).astype(BF16)

    def row(width):
        return pl.BlockSpec((CHUNK, width), lambda i: (i, 0))

    def whole(shape):
        return pl.BlockSpec(shape, lambda i: (0,) * len(shape))

    pad_spec = pl.BlockSpec((2, 2, CHUNK, LANES), lambda i: (0, 0, i, 0))
    return pl.pallas_call(
        body, name=name, grid=(TT // CHUNK,),
        out_shape=[jax.ShapeDtypeStruct((TT, ATTN_WIDTH), BF16),
                   jax.ShapeDtypeStruct((2, 2, TT, LANES), BF16), jax.ShapeDtypeStruct((2, 2, TT, LANES), BF16),
                   jax.ShapeDtypeStruct((2, HEAD_DIM, TT), BF16),
                   jax.ShapeDtypeStruct((TT - ctx_rows, ATTN_WIDTH + SG_WIDTH), BF16)],
        in_specs=[row(IN_WIDTH), row(LANES), row(LANES), whole((1, LANES)), whole((1, LANES)),
                  whole((LANES, LANES)), whole((N_SG_GROUPS, CHUNK, CHUNK)), whole((CHUNK, N_SG_GROUPS))],
        out_specs=[row(ATTN_WIDTH), pad_spec, pad_spec,
                   pl.BlockSpec((2, HEAD_DIM, CHUNK), lambda i: (0, 0, i)),
                   pl.BlockSpec((CHUNK, SG_WIDTH), lambda i: (jnp.maximum(i - off, 0), 1))],
        compiler_params=pltpu.CompilerParams(dimension_semantics=("arbitrary",)),
    )(p, cos, sin, qg, kg, bd, w_sp, b_spt)


def _mix_prep_bwd(p, dq, f, dao, ctx_rows, cos, sin, qg, kg, bd, w_sp, w_spt, b_spt, name):
    TT = p.shape[0]
    off = ctx_rows // CHUNK
    q_scale = HEAD_DIM ** -0.5

    def body(p_ref, dq_ref, f_ref, dsg_ref, cos_ref, sin_ref, qg_ref, kg_ref, bd_ref, w_ref, wt_ref,
             b_ref, dp_ref, dqg_ref, dkg_ref, dw_ref, db_ref):
        i = pl.program_id(0)

        @pl.when(i == 0)
        def _():
            dqg_ref[...] = jnp.zeros_like(dqg_ref)
            dkg_ref[...] = jnp.zeros_like(dkg_ref)
            dw_ref[...] = jnp.zeros_like(dw_ref)
            db_ref[...] = jnp.zeros_like(db_ref)

        latent = (i >= off).astype(F32)
        lane = lax.broadcasted_iota(jnp.int32, (CHUNK, LANES), 1)
        low = lane < HEAD_DIM
        cs, sn, bdv = cos_ref[...], sin_ref[...], bd_ref[...]

        def fold(b0):
            return jnp.where(low, f_ref[0, b0] + pltpu.roll(f_ref[0, b0 + 1], HEAD_DIM, 1),
                             pltpu.roll(f_ref[1, b0], HEAD_DIM, 1) + f_ref[1, b0 + 1])

        def norm_rope_bwd(xv, dout, gain):
            r = lax.rsqrt(_head_mean(xv * xv, bdv) + EPS)
            n = xv * r
            dy = dout * cs + _swap16(dout * sn, lane)
            dn = dy * gain
            dx = r * (dn - n * _head_mean(dn * n, bdv))
            return dx, jnp.sum(dy * n, axis=0, keepdims=True)

        for a in range(ATTN_WIDTH // LANES):
            cols = slice(a * LANES, (a + 1) * LANES)
            dx, dg = norm_rope_bwd(p_ref[:, cols], dq_ref[:, cols] * (latent * q_scale), qg_ref[...])
            dp_ref[:, cols] = dx.astype(BF16)
            dqg_ref[...] += dg
        dx, dg = norm_rope_bwd(p_ref[:, K_COLS[0]:K_COLS[1]], fold(0), kg_ref[...])
        dp_ref[:, K_COLS[0]:K_COLS[1]] = dx.astype(BF16)
        dkg_ref[...] += dg
        dp_ref[:, V_COLS[0]:V_COLS[1]] = fold(2).astype(BF16)
        for g in range(N_SG_GROUPS):
            su = p_ref[:, SU_COLS[0] + g * LANES:SU_COLS[0] + (g + 1) * LANES]
            sv = p_ref[:, SV_COLS[0] + g * LANES:SV_COLS[0] + (g + 1) * LANES]
            u, vg = _gelu(su), _gelu(sv)
            xc = vg - jnp.mean(vg, axis=-1, keepdims=True)
            rs = lax.rsqrt(jnp.mean(xc * xc, axis=-1, keepdims=True) + EPS)
            vn = xc * rs
            vnb = vn.astype(BF16)
            mixed = _dot(w_ref[g].astype(BF16), vnb, NN) + b_ref[:, g:g + 1]
            dsg = dsg_ref[:, g * LANES:(g + 1) * LANES].astype(F32) * latent
            du = dsg * mixed
            dmix = dsg * u
            dmb = dmix.astype(BF16)
            db_ref[:, g:g + 1] += jnp.sum(dmix, axis=-1, keepdims=True)
            dw_ref[g] += _dot(dmb, vnb, NT)
            dvn = _dot(wt_ref[g].astype(BF16), dmb, NN)
            dvg = rs * (dvn - jnp.mean(dvn, axis=-1, keepdims=True)
                        - vn * jnp.mean(dvn * vn, axis=-1, keepdims=True))
            dp_ref[:, SU_COLS[0] + g * LANES:SU_COLS[0] + (g + 1) * LANES] = (du * _dgelu(su)).astype(BF16)
            dp_ref[:, SV_COLS[0] + g * LANES:SV_COLS[0] + (g + 1) * LANES] = (dvg * _dgelu(sv)).astype(BF16)

    def row(width):
        return pl.BlockSpec((CHUNK, width), lambda i: (i, 0))

    def latent_row(width, col_block):
        return pl.BlockSpec((CHUNK, width), lambda i: (jnp.maximum(i - off, 0), col_block))

    def whole(shape):
        return pl.BlockSpec(shape, lambda i: (0,) * len(shape))

    return pl.pallas_call(
        body, name=name, grid=(TT // CHUNK,),
        out_shape=[jax.ShapeDtypeStruct((TT, IN_WIDTH), BF16), jax.ShapeDtypeStruct((1, LANES), F32),
                   jax.ShapeDtypeStruct((1, LANES), F32),
                   jax.ShapeDtypeStruct((N_SG_GROUPS, CHUNK, CHUNK), F32),
                   jax.ShapeDtypeStruct((CHUNK, N_SG_GROUPS), F32)],
        in_specs=[row(IN_WIDTH), latent_row(ATTN_WIDTH, 0),
                  pl.BlockSpec((2, 4, CHUNK, LANES), lambda i: (0, 0, i, 0)),
                  latent_row(SG_WIDTH, 1), row(LANES), row(LANES), whole((1, LANES)), whole((1, LANES)),
                  whole((LANES, LANES)), whole((N_SG_GROUPS, CHUNK, CHUNK)),
                  whole((N_SG_GROUPS, CHUNK, CHUNK)), whole((CHUNK, N_SG_GROUPS))],
        out_specs=[row(IN_WIDTH), whole((1, LANES)), whole((1, LANES)),
                   whole((N_SG_GROUPS, CHUNK, CHUNK)), whole((CHUNK, N_SG_GROUPS))],
        compiler_params=pltpu.CompilerParams(dimension_semantics=("arbitrary",)),
    )(p, dq, f, dao, cos, sin, qg, kg, bd, w_sp, w_spt, b_spt)


def _attn_fwd(q, kpad, vpad, ao, ctx_rows, name, tq=256):
    TT = q.shape[0]
    T = TT - ctx_rows
    tq = _tile(T, tq)
    off = ctx_rows // tq
    group = 2 * LANES

    def body(q_ref, k_ref, v_ref, ao_in, o_ref, lse_ref):
        del ao_in
        lane = lax.broadcasted_iota(jnp.int32, (tq, LANES), 1)
        lse = jnp.zeros((tq, LANES), F32)
        for a in range(2):
            acc = jnp.zeros((tq, LANES), F32)
            qa = q_ref[:, a * LANES:(a + 1) * LANES]
            for b in range(2):
                s = _dot(qa, k_ref[0, b], NT)
                m = jnp.max(s, axis=-1, keepdims=True)
                e = jnp.exp(s - m)
                l = jnp.sum(e, axis=-1, keepdims=True)
                acc = acc + _dot(e.astype(BF16), v_ref[0, b], NN) * (1.0 / l)
                lse = jnp.where(lane == 2 * a + b, m + jnp.log(l), lse)
            o_ref[:, a * LANES:(a + 1) * LANES] = acc.astype(BF16)
        lse_ref[0] = lse

    kv_spec = pl.BlockSpec((1, 2, TT, LANES), lambda j, i: (j, 0, 0, 0))
    return pl.pallas_call(
        body, name=name, grid=(2, T // tq),
        out_shape=[jax.ShapeDtypeStruct(ao.shape, BF16), jax.ShapeDtypeStruct((2, T, LANES), F32)],
        in_specs=[pl.BlockSpec((tq, group), lambda j, i: (i + off, j)), kv_spec, kv_spec,
                  pl.BlockSpec(memory_space=pl.ANY)],
        out_specs=[pl.BlockSpec((tq, group), lambda j, i: (i, j)),
                   pl.BlockSpec((1, tq, LANES), lambda j, i: (j, i, 0))],
        input_output_aliases={3: 0},
        compiler_params=pltpu.CompilerParams(dimension_semantics=("parallel", "parallel")),
    )(q, kpad, vpad, ao)


def _attn_bwd(q, dao, ao, lse, kpad, vpad, kt, ctx_rows, name, tq=256):
    TT = q.shape[0]
    T = TT - ctx_rows
    tq = _tile(T, tq)
    off = ctx_rows // tq
    group = 2 * LANES

    def body(q_ref, do_ref, o_ref, lse_ref, k_ref, v_ref, kt_ref, dq_ref, f_ref):
        i = pl.program_id(1)

        @pl.when(i == 0)
        def _():
            f_ref[...] = jnp.zeros_like(f_ref)

        ktv = kt_ref[0]
        lse_t = lse_ref[0].T
        row = lax.broadcasted_iota(jnp.int32, (SUBLANES, LANES), 0)
        lane = lax.broadcasted_iota(jnp.int32, (SUBLANES, LANES), 1)
        half_ones = (jnp.where(lane < HEAD_DIM, 0, 1) == row).astype(BF16)
        for a in range(2):
            cols = slice(a * LANES, (a + 1) * LANES)
            qa = q_ref[:, cols]
            do32 = do_ref[:, cols].astype(F32)
            doa = do32.astype(BF16)
            hi, lo = _split_bf16(do32 * o_ref[:, cols].astype(F32))
            deltas = _dot(half_ones, hi, NT) + _dot(half_ones, lo, NT)
            halves = []
            for b in range(2):
                h = 2 * a + b
                st = _dot(k_ref[0, b], qa, NT)
                pt = jnp.exp(st - lse_t[h:h + 1, :])
                dpt = _dot(v_ref[0, b], doa, NT)
                dst = (pt * (dpt - deltas[b:b + 1, :])).astype(BF16)
                f_ref[0, b] += _dot(dst, qa, NN)
                f_ref[0, 2 + b] += _dot(pt.astype(BF16), doa, NN)
                halves.append(_dot(ktv, dst, NN))
            dq_ref[:, cols] = jnp.concatenate(halves, axis=0).T

    kv_spec = pl.BlockSpec((1, 2, TT, LANES), lambda j, i: (j, 0, 0, 0))
    out_cols = pl.BlockSpec((tq, group), lambda j, i: (i, j))
    return pl.pallas_call(
        body, name=name, grid=(2, T // tq),
        out_shape=[jax.ShapeDtypeStruct((T, ATTN_WIDTH), F32), jax.ShapeDtypeStruct((2, 4, TT, LANES), F32)],
        in_specs=[pl.BlockSpec((tq, group), lambda j, i: (i + off, j)), out_cols, out_cols,
                  pl.BlockSpec((1, tq, LANES), lambda j, i: (j, i, 0)),
                  kv_spec, kv_spec, pl.BlockSpec((1, HEAD_DIM, TT), lambda j, i: (j, 0, 0))],
        out_specs=[out_cols, pl.BlockSpec((1, 4, TT, LANES), lambda j, i: (j, 0, 0, 0))],
        compiler_params=pltpu.CompilerParams(dimension_semantics=("parallel", "arbitrary")),
    )(q, dao, ao, lse, kpad, vpad, kt)


def _final_fwd_bwd(h, g, target, y, gt, name):
    R, Dm = h.shape
    tm = _tile(R, ROW_BLOCK, 8)

    def body(h_ref, g_ref, t_ref, y_ref, gt_ref, dh_ref, loss_ref, dg_ref, dy_ref, dgt_ref, dsum_ref):
        i = pl.program_id(0)

        @pl.when(i == 0)
        def _():
            for ref in (loss_ref, dg_ref, dgt_ref, dsum_ref):
                ref[...] = jnp.zeros_like(ref)

        hv = h_ref[...]
        r = lax.rsqrt(jnp.mean(hv * hv, axis=-1, keepdims=True) + EPS)
        n = hv * r
        diff = n * g_ref[...] - t_ref[...]
        loss_ref[...] += jnp.sum(diff * diff)
        dout = diff * (1.0 / Dm)
        dg_ref[...] += jnp.sum(dout * n, axis=0, keepdims=True)
        dn = dout * g_ref[...]
        dh = r * (dn - n * jnp.mean(dn * n, axis=-1, keepdims=True))
        dh_ref[...] = dh
        _gate_grads(dh, y_ref, gt_ref, dy_ref, dgt_ref, dsum_ref)

    vec = jax.ShapeDtypeStruct((1, Dm), F32)
    return pl.pallas_call(
        body, name=name, grid=(R // tm,),
        out_shape=[jax.ShapeDtypeStruct((R, Dm), F32), jax.ShapeDtypeStruct((1, LANES), F32), vec,
                   jax.ShapeDtypeStruct((R, Dm), BF16), vec, vec],
        in_specs=[_row_spec(tm, Dm), _vec_spec(Dm), _row_spec(tm, Dm), _row_spec(tm, Dm), _vec_spec(Dm)],
        out_specs=[_row_spec(tm, Dm), _vec_spec(LANES), _vec_spec(Dm), _row_spec(tm, Dm), _vec_spec(Dm),
                   _vec_spec(Dm)],
        compiler_params=pltpu.CompilerParams(dimension_semantics=("arbitrary",)),
    )(h, g, target, y, gt)


MOD_ROWS = 16


def _mod_fwd(c_rows, w_mod, name):
    L, Dm, n = w_mod.shape

    def body(c_ref, w_ref, o_ref):
        o_ref[0] = _dot3(_silu(c_ref[...]), w_ref[0], NN)

    return pl.pallas_call(
        body, name=name, grid=(L,),
        out_shape=jax.ShapeDtypeStruct((L, MOD_ROWS, n), F32),
        in_specs=[pl.BlockSpec((MOD_ROWS, Dm), lambda l: (0, 0)), pl.BlockSpec((1, Dm, n), lambda l: (l, 0, 0))],
        out_specs=pl.BlockSpec((1, MOD_ROWS, n), lambda l: (l, 0, 0)),
        compiler_params=pltpu.CompilerParams(dimension_semantics=("parallel",)),
    )(c_rows, w_mod)


def _mod_bwd(c_rows_t, dmod, w_mod, name):
    L, Dm, n = w_mod.shape

    def body(ct_ref, d_ref, w_ref, gw_ref, ds_ref):
        dm = d_ref[0]
        gw_ref[0] = _dot3(_silu(ct_ref[...]), dm, NN)
        ds_ref[0] = _dot3(dm[:MOD_ROWS], w_ref[0], NT)

    return pl.pallas_call(
        body, name=name, grid=(L,),
        out_shape=[jax.ShapeDtypeStruct((L, Dm, n), F32), jax.ShapeDtypeStruct((L, MOD_ROWS, Dm), F32)],
        in_specs=[pl.BlockSpec((Dm, LANES), lambda l: (0, 0)), pl.BlockSpec((1, LANES, n), lambda l: (l, 0, 0)),
                  pl.BlockSpec((1, Dm, n), lambda l: (l, 0, 0))],
        out_specs=[pl.BlockSpec((1, Dm, n), lambda l: (l, 0, 0)),
                   pl.BlockSpec((1, MOD_ROWS, Dm), lambda l: (l, 0, 0))],
        compiler_params=pltpu.CompilerParams(dimension_semantics=("parallel",)),
    )(c_rows_t, dmod, w_mod)


def _adam_update(w, g, m, v):
    c1 = 1.0 - ADAM_B1 ** ADAM_STEP
    c2 = 1.0 - ADAM_B2 ** ADAM_STEP
    mn = ADAM_B1 * m + (1.0 - ADAM_B1) * g
    vn = ADAM_B2 * v + (1.0 - ADAM_B2) * (g * g)
    return -ADAM_LR * ((mn / c1) / (jnp.sqrt(vn / c2) + ADAM_EPS) + ADAM_WD * w), mn, vn


def _adamw(w, g, m, v, name):
    R, Cw = w.shape
    tm = _tile(R, ROW_BLOCK, 8)

    def body(w_ref, g_ref, m_ref, v_ref, d_ref, mo_ref, vo_ref):
        d_ref[...], mo_ref[...], vo_ref[...] = _adam_update(w_ref[...], g_ref[...], m_ref[...], v_ref[...])

    spec = pl.BlockSpec((tm, Cw), lambda i: (i, 0))
    return pl.pallas_call(
        body, name=name, grid=(R // tm,),
        out_shape=[jax.ShapeDtypeStruct((R, Cw), F32)] * 3,
        in_specs=[spec] * 4, out_specs=[spec] * 3,
        compiler_params=pltpu.CompilerParams(dimension_semantics=("parallel",)),
    )(w, g, m, v)


def _adamw_recv(w, m, v, recvs, name):
    L, R, n = w.shape
    tm = _tile(R, ROW_BLOCK, 8)
    nblk = R // tm
    parts = [r.reshape(N_DEV, R, n) for r in recvs]

    def body(*refs):
        w_ref, m_ref, v_ref = refs[:3]
        part_refs = refs[3:3 + L]
        g_ref, d_ref, mo_ref, vo_ref, gsum = refs[3 + L:]
        l = pl.program_id(0)
        for ll in range(L):
            @pl.when(l == ll)
            def _(ll=ll):
                acc = part_refs[ll][0].astype(F32)
                for s in range(1, N_DEV):
                    acc = acc + part_refs[ll][s].astype(F32)
                gsum[...] = acc
        g = gsum[...]
        g_ref[0] = g
        d_ref[0], mo_ref[0], vo_ref[0] = _adam_update(w_ref[0], g, m_ref[0], v_ref[0])

    def part_spec(ll):
        return pl.BlockSpec((N_DEV, tm, n), lambda l, i: (0, jnp.where(l == ll, i, jnp.where(l < ll, 0, nblk - 1)), 0))

    spec = pl.BlockSpec((1, tm, n), lambda l, i: (l, i, 0))
    return pl.pallas_call(
        body, name=name, grid=(L, nblk),
        out_shape=[jax.ShapeDtypeStruct((L, R, n), F32)] * 4,
        in_specs=[spec] * 3 + [part_spec(ll) for ll in range(L)], out_specs=[spec] * 4,
        scratch_shapes=[pltpu.VMEM((tm, n), F32)],
        compiler_params=pltpu.CompilerParams(dimension_semantics=("parallel", "parallel")),
    )(w, m, v, *parts)


def _pack(parts, row_mult=8):
    flat, offs, pos = [], [], 0
    for t in parts:
        t = t.reshape(-1).astype(F32)
        size = -(-t.shape[0] // LANES) * LANES
        flat.append(jnp.pad(t, (0, size - t.shape[0])))
        offs.append(pos)
        pos += size
    total = -(-pos // (LANES * row_mult)) * (LANES * row_mult)
    if total > pos:
        flat.append(jnp.zeros((total - pos,), F32))
    return jnp.concatenate(flat).reshape(-1, LANES), offs


def _take(buf, off, shape):
    size = math.prod(shape)
    return buf[..., off:off + size].reshape(buf.shape[:-1] + tuple(shape))


def _rope_tables(T, ctx_rows):
    pos = jnp.arange(T)
    row = (pos // GRID_W).astype(F32)
    col = (pos % GRID_W).astype(F32)
    half = HEAD_DIM // 4
    inv = ROPE_THETA ** (-jnp.arange(0, 2 * half, 2, dtype=F32) / (2 * half))
    ang_r, ang_c = row[:, None] * inv[None, :], col[:, None] * inv[None, :]
    cos = jnp.concatenate([jnp.cos(ang_r)] * 2 + [jnp.cos(ang_c)] * 2, axis=1)
    sin = jnp.concatenate([-jnp.sin(ang_r), jnp.sin(ang_r), -jnp.sin(ang_c), jnp.sin(ang_c)], axis=1)
    cos = jnp.concatenate([jnp.ones((ctx_rows, HEAD_DIM), F32), cos], axis=0)
    sin = jnp.concatenate([jnp.zeros((ctx_rows, HEAD_DIM), F32), sin], axis=0)
    return jnp.tile(cos, (1, 2)), jnp.tile(sin, (1, 2))


def kernel(x, c, ctx, c_ctx, w_mod, b_mod, g_mix, g_ffn, w_ffn_in, w_ffn_out, w_in, q_gain, k_gain, w_sp, b_sp, w_out, w_pw1, b_pw1, w_dw, b_dw, ln_g, ln_b, w_pw2, b_pw2, g_final, loss_target, m_c_ctx, m_w_mod, m_b_mod, m_g_mix, m_g_ffn, m_w_ffn_in, m_w_ffn_out, m_w_in, m_q_gain, m_k_gain, m_w_sp, m_b_sp, m_w_out, m_w_pw1, m_b_pw1, m_w_dw, m_b_dw, m_ln_g, m_ln_b, m_w_pw2, m_b_pw2, m_g_final, v_c_ctx, v_w_mod, v_b_mod, v_g_mix, v_g_ffn, v_w_ffn_in, v_w_ffn_out, v_w_in, v_q_gain, v_k_gain, v_w_sp, v_b_sp, v_w_out, v_w_pw1, v_b_pw1, v_w_dw, v_b_dw, v_ln_g, v_ln_b, v_w_pw2, v_b_pw2, v_g_final):
    weights = dict(c_ctx=c_ctx, w_mod=w_mod, b_mod=b_mod, g_mix=g_mix, g_ffn=g_ffn, w_ffn_in=w_ffn_in,
                   w_ffn_out=w_ffn_out, w_in=w_in, q_gain=q_gain, k_gain=k_gain, w_sp=w_sp, b_sp=b_sp,
                   w_out=w_out, w_pw1=w_pw1, b_pw1=b_pw1, w_dw=w_dw, b_dw=b_dw, ln_g=ln_g, ln_b=ln_b,
                   w_pw2=w_pw2, b_pw2=b_pw2, g_final=g_final)
    moments_m = dict(c_ctx=m_c_ctx, w_mod=m_w_mod, b_mod=m_b_mod, g_mix=m_g_mix, g_ffn=m_g_ffn,
                     w_ffn_in=m_w_ffn_in, w_ffn_out=m_w_ffn_out, w_in=m_w_in, q_gain=m_q_gain,
                     k_gain=m_k_gain, w_sp=m_w_sp, b_sp=m_b_sp, w_out=m_w_out, w_pw1=m_w_pw1,
                     b_pw1=m_b_pw1, w_dw=m_w_dw, b_dw=m_b_dw, ln_g=m_ln_g, ln_b=m_ln_b, w_pw2=m_w_pw2,
                     b_pw2=m_b_pw2, g_final=m_g_final)
    moments_v = dict(c_ctx=v_c_ctx, w_mod=v_w_mod, b_mod=v_b_mod, g_mix=v_g_mix, g_ffn=v_g_ffn,
                     w_ffn_in=v_w_ffn_in, w_ffn_out=v_w_ffn_out, w_in=v_w_in, q_gain=v_q_gain,
                     k_gain=v_k_gain, w_sp=v_w_sp, b_sp=v_b_sp, w_out=v_w_out, w_pw1=v_w_pw1,
                     b_pw1=v_b_pw1, w_dw=v_w_dw, b_dw=v_b_dw, ln_g=v_ln_g, ln_b=v_ln_b, w_pw2=v_w_pw2,
                     b_pw2=v_b_pw2, g_final=v_g_final)
    names = list(weights)

    T, C = x.shape[1], ctx.shape[1]
    Dm = D_MODEL
    me = 4 * lax.axis_index("x") + 2 * lax.axis_index("y") + lax.axis_index("c")
    h0 = x[0]
    ctx2 = ctx[0]
    target = loss_target[0]

    small_sharded = (("w_dw", w_dw[0]), ("b_pw1", b_pw1), ("b_dw", b_dw), ("ln_g", ln_g), ("ln_b", ln_b),
                     ("b_pw2", b_pw2))
    buf1, offs1 = _pack([c] + [t for _, t in small_sharded])
    got1, W_in, W_out = _all_gather([buf1, w_in[0].astype(BF16), w_out[0].astype(BF16)], "gather_cond", False)
    got1 = got1.reshape(N_DEV, -1)
    c_all = _take(got1, offs1[0], (Dm,))
    full_small = {}
    for (nm, t), off in zip(small_sharded, offs1[1:]):
        seg = _take(got1, off, t.shape)
        full_small[nm] = jnp.moveaxis(seg, 0, -2).reshape(t.shape[:-1] + (N_DEV * t.shape[-1],))
    w_dw_f, b_pw1_f = full_small["w_dw"], full_small["b_pw1"]
    b_dw_f, ln_g_f, ln_b_f, b_pw2_f = (full_small[k] for k in ("b_dw", "ln_g", "ln_b", "b_pw2"))

    c_rows = jnp.concatenate([c_all, c_ctx[None, :], jnp.zeros((MOD_ROWS - N_DEV - 1, Dm), F32)], axis=0)
    mod_part = _mod_fwd(c_rows, w_mod, "mod_fwd")
    n_mod = w_mod.shape[2]
    got2 = _all_gather([mod_part.reshape(-1, LANES)], "gather_mod", True)[0]
    mod_all = got2.reshape(N_DEV, 2, MOD_ROWS, n_mod).transpose(1, 2, 0, 3).reshape(2, MOD_ROWS, N_DEV * n_mod)
    mod_all = mod_all + b_mod[:, None, :]
    my_mod = lax.dynamic_index_in_dim(mod_all, me, axis=1, keepdims=False)
    sh1, sc1, gt1, sh2, sc2, gt2 = ([my_mod[l:l + 1, k * Dm:(k + 1) * Dm] for l in range(2)] for k in range(6))
    csh1 = mod_all[0, N_DEV:N_DEV + 1, 0:Dm]
    csc1 = mod_all[0, N_DEV:N_DEV + 1, Dm:2 * Dm]

    behind = got2[0:1, 0:1] * 0.0
    gather_groups = [[w_ffn_in[0], w_ffn_out[0]], [w_pw1[0], w_pw2[0]], [w_ffn_in[1], w_ffn_out[1]]]
    gathers = [_push_begin([(t + behind).astype(BF16) for t in grp], True, f"gather_start{k}")
               for k, grp in enumerate(gather_groups)]
    started = sum(h[4][0:1, 0:1] for h in gathers)

    def gathered(k, after):
        return _push_end(gathers[k], after, f"gather_wait{k}")[1]

    def ffn_weights(k, after):
        wi, wo = gathered(k, after)
        return wi.reshape(N_DEV, Dm, FF_SHARD), wo.reshape(N_DEV // 2, FF_SHARD, Dm)

    def col_gathered(t, n):
        return t.reshape(N_DEV, Dm, n).transpose(1, 0, 2).reshape(Dm, N_DEV * n)

    W_ffi, W_ffo = [None, None], [None, None]

    g_mix_r = [g_mix[l:l + 1] for l in range(2)]
    g_ffn_r = [g_ffn[l:l + 1] for l in range(2)]
    g_fin = g_final[None, :]

    cos, sin = _rope_tables(T, C)
    qg = jnp.tile(q_gain, (1, 2))
    kg = jnp.tile(k_gain, (1, 2))
    lane_head = jnp.arange(LANES) // HEAD_DIM
    bd = (lane_head[:, None] == lane_head[None, :]).astype(BF16)
    w_sp0 = w_sp[0]
    w_spt0 = w_sp0.transpose(0, 2, 1)
    b_spt0 = b_sp[0].T

    XM = _norm_mod_fwd_cat(ctx2, h0, g_mix_r[0], csc1, csh1, sc1[0] + started, sh1[0], "norm_mix0")
    W_in = col_gathered(W_in, IN_WIDTH // N_DEV)
    P = _mm(XM, W_in, "nn", "in_proj", tn=IN_WIDTH)
    qh, kpad, vpad, kt, ao = _mix_prep_fwd(P, C, cos, sin, qg, kg, bd, w_sp0, b_spt0, "mix_prep")
    ao, lse = _attn_fwd(qh, kpad, vpad, ao, C, "attn_fwd")
    h1, y0, xf0 = _mm(ao, W_out, "nn", "out_proj", res=h0, gate=gt1[0], raw_out=True,
                      norm=(g_ffn_r[0], sc2[0], sh2[0]))

    def ffn_fwd(h_in, xf, l, norm_next):
        W_ffi[l], W_ffo[l] = ffn_weights(2 * l, xf)
        gu, act = _ffn_in_swiglu(xf, W_ffi[l], f"ffn_in{l}")
        outs = _mm_sum_shards(act, W_ffo[l], "nn", f"ffn_out{l}", res=h_in, gate=gt2[l], raw_out=True,
                              norm=norm_next)
        return tuple(outs) + (None,) * (3 - len(outs)) + (gu, act)

    h2, f0, xm1, gu0, act0 = ffn_fwd(h1, xf0, 0, (g_mix_r[1], sc1[1], sh1[1]))

    W_pw1, W_pw2 = gathered(1, xm1)
    W_pw1 = col_gathered(W_pw1, 2 * Dm // N_DEV)
    ag = _mm(xm1, W_pw1, "nn", "pw1", BF16, bias=b_pw1_f)
    hg = _glu_fwd(ag, "glu")
    hd = _conv_fwd(hg, w_dw_f, b_dw_f, "conv")
    hs = _ln_silu_fwd(hd, ln_g_f, ln_b_f, "ln_silu")
    h3, y1, xf1 = _mm(hs, W_pw2, "nn", "pw2", bias=b_pw2_f, res=h2, gate=gt1[1], raw_out=True,
                      norm=(g_ffn_r[1], sc2[1], sh2[1]))
    h4, f1, _, gu1, act1 = ffn_fwd(h3, xf1, 1, None)

    dh4, sq_err, dg_final, df1, dgt2_1, _ = _final_fwd_bwd(h4, g_fin, target, f1, gt2[1], "loss_head")
    loss_local = (0.5 / Dm) * sq_err[0, 0:1]

    def col_shards(g, n):
        return g.reshape(Dm, N_DEV, n).transpose(1, 0, 2).reshape(N_DEV * Dm, n)

    def exchange_begin(k, parts):
        return _push_begin(parts, False, f"exchange_start{k}")

    def zero_of(handle):
        return handle[4][0:1, 0:1]

    def ffn_bwd(df, xf, gu, act, l):
        dw_out = _mm_tn_shard_rows(act, df, f"ffn_out_dw{l}", BF16)
        dgu = _ffn_out_dx_swiglu(df, W_ffo[l], gu, f"ffn_out_dx{l}").reshape(N_DEV, T, FF_SHARD)
        dw_in = _mm_tn_shard_cols(xf, dgu, f"ffn_in_dw{l}", BF16)
        dxf = _mm_sum_shards(dgu, W_ffi[l], "nt", f"ffn_in_dx{l}", BF16, tm=256)
        return dw_in, dw_out, dxf

    dW_ffi1, dW_ffo1, dxf1 = ffn_bwd(df1, xf1, gu1, act1, 1)
    ex0 = exchange_begin(0, [dW_ffi1.reshape(N_DEV * Dm, FF_SHARD), dW_ffo1.reshape(D_FF, Dm)])
    dh3, da, dsh, dy1, dgt1_1, db_pw2 = _norm_mod_bwd(h3, g_ffn_r[1], sc2[1], dxf1, dh4, "norm_ffn_bwd1",
                                                       gate=(y1, gt1[1] + zero_of(ex0)))
    dmod_ffn1 = (dsh, da * g_ffn_r[1], dgt2_1)
    dg_ffn1 = da * (1.0 + sc2[1])

    dW_pw2 = _mm(hs, dy1, "tn", "pw2_dw", BF16, tk=2048)
    dhs = _mm(dy1, W_pw2, "nt", "pw2_dx", BF16)
    dhd, dln_g, dln_b, db_dw = _ln_silu_bwd(dhs, hd, ln_g_f, ln_b_f, "ln_silu_bwd")
    dhg, dw_dw = _conv_bwd(dhd, hg, w_dw_f, "conv_bwd")
    dag, db_pw1 = _glu_bwd(ag, dhg, "glu_bwd")
    dW_pw1 = _mm(xm1, dag, "tn", "pw1_dw", BF16, tk=2048)
    dxm1 = _mm(dag, W_pw1, "nt", "pw1_dx", BF16, tk=2048)
    ex1 = exchange_begin(1, [col_shards(dW_pw1, 2 * Dm // N_DEV), dW_pw2])
    dh2, da, dsh, df0, dgt2_0, _ = _norm_mod_bwd(h2, g_mix_r[1], sc1[1], dxm1, dh3, "norm_mix1_bwd",
                                                 gate=(f0, gt2[0] + zero_of(ex1)))
    dmod_mix1 = (dsh, da * g_mix_r[1], dgt1_1)
    dg_mix1 = da * (1.0 + sc1[1])

    dW_ffi0, dW_ffo0, dxf0 = ffn_bwd(df0, xf0, gu0, act0, 0)
    ex2 = exchange_begin(2, [dW_ffi0.reshape(N_DEV * Dm, FF_SHARD), dW_ffo0.reshape(D_FF, Dm)])
    dh1, da, dsh, dy0, dgt1_0, _ = _norm_mod_bwd(h1, g_ffn_r[0], sc2[0], dxf0, dh2, "norm_ffn_bwd0",
                                                 gate=(y0, gt1[0] + zero_of(ex2)))
    dmod_ffn0 = (dsh, da * g_ffn_r[0], dgt2_0)
    dg_ffn0 = da * (1.0 + sc2[0])

    dW_out = _mm(ao, dy0, "tn", "out_proj_dw", BF16, tk=2048)
    dao = _mm(dy0, W_out, "nt", "out_proj_dx", BF16)
    dq, f_acc = _attn_bwd(qh, dao, ao, lse, kpad, vpad, kt, C, "attn_bwd")
    dP, dqg, dkg, dw_sp0, db_spt0 = _mix_prep_bwd(P, dq, f_acc, dao, C, cos, sin, qg, kg, bd, w_sp0, w_spt0,
                                                  b_spt0, "mix_prep_bwd")
    dW_in = _mm(XM, dP, "tn", "in_proj_dw", BF16, tn=896, tk=2176)
    dXM = _mm(dP, W_in, "nt", "in_proj_dx", BF16, tk=IN_WIDTH)
    dh0, da, dsh = _norm_mod_bwd(h0, g_mix_r[0], sc1[0], dXM, dh1, "norm_mix0_bwd", dxm_row_off=C)
    _, dac, dcsh = _norm_mod_bwd(ctx2, g_mix_r[0], csc1, dXM, None, "norm_ctx_bwd")
    dmod_mix0 = (dsh, da * g_mix_r[0], dgt1_0)
    dg_mix0 = da * (1.0 + sc1[0]) + dac * (1.0 + csc1)
    dcmod = jnp.concatenate([dcsh, dac * g_mix_r[0]], axis=1)

    dmod_mine = jnp.stack([jnp.concatenate(dmod_mix0 + dmod_ffn0, axis=1)[0],
                           jnp.concatenate(dmod_mix1 + dmod_ffn1, axis=1)[0]])

    small_grads = [
        ("loss", loss_local), ("g_final", dg_final), ("g_mix", jnp.concatenate([dg_mix0, dg_mix1])),
        ("g_ffn", jnp.concatenate([dg_ffn0, dg_ffn1])),
        ("q_gain", dqg[:, :HEAD_DIM] + dqg[:, HEAD_DIM:]), ("k_gain", dkg[:, :HEAD_DIM] + dkg[:, HEAD_DIM:]),
        ("w_sp", dw_sp0[None]), ("b_sp", db_spt0.T[None]), ("b_pw1", db_pw1), ("w_dw", dw_dw[None]),
        ("b_dw", db_dw), ("ln_g", dln_g), ("ln_b", dln_b), ("b_pw2", db_pw2), ("dcmod", dcmod),
        ("dmod", dmod_mine),
    ]
    buf3, offs3 = _pack([t for _, t in small_grads])
    got3 = _all_gather([buf3], "gather_small_grads", True)[0].reshape(N_DEV, buf3.shape[0], LANES)
    sum3 = _sum_devices(got3, "sum_small_grads").reshape(-1)
    off3 = {nm: off for (nm, _), off in zip(small_grads, offs3)}
    shape3 = {nm: t.shape for nm, t in small_grads}

    def summed(nm):
        return _take(sum3, off3[nm], shape3[nm])

    loss = summed("loss")[0]
    dcmod_sum = summed("dcmod")
    dmod_rows = _take(got3.reshape(N_DEV, -1), off3["dmod"], (2, 6 * Dm)).transpose(1, 0, 2)
    ctx_row = jnp.concatenate([jnp.pad(dcmod_sum, ((0, 0), (0, 4 * Dm))), jnp.zeros((1, 6 * Dm), F32)])
    dmod_all = jnp.concatenate([dmod_rows, ctx_row[:, None, :],
                                jnp.zeros((2, LANES - N_DEV - 1, 6 * Dm), F32)], axis=1)
    grads = {}
    grads["b_mod"] = summed("dmod") + ctx_row
    dmod_shard = lax.dynamic_slice_in_dim(dmod_all, me * n_mod, n_mod, axis=2)
    c_rows_t = jnp.pad(c_rows.T, ((0, 0), (0, LANES - MOD_ROWS)))
    grads["w_mod"], ds_part = _mod_bwd(c_rows_t, dmod_shard, w_mod, "mod_bwd")

    buf4, _ = _pack([ds_part[0, N_DEV]])
    got4 = _all_gather([buf4], "gather_c_ctx_grad", True)[0].reshape(N_DEV, buf4.shape[0], LANES)
    ds_ctx = _sum_devices(got4, "sum_c_ctx_grad").reshape(-1)[:Dm]
    behind_small = (ds_ctx[0:1] * 0.0).astype(BF16)
    ex3 = exchange_begin(3, [col_shards(dW_in + behind_small, IN_WIDTH // N_DEV), dW_out])
    grads["c_ctx"] = ds_ctx * _dsilu(c_ctx) + zero_of(ex3)[0]

    for nm in ("g_final", "g_mix", "g_ffn", "q_gain", "k_gain", "w_sp", "b_sp"):
        grads[nm] = summed(nm).reshape(weights[nm].shape)
    for nm in ("b_pw1", "w_dw", "b_dw", "ln_g", "ln_b", "b_pw2"):
        n_loc = weights[nm].shape[-1]
        grads[nm] = lax.dynamic_slice_in_dim(summed(nm), me * n_loc, n_loc, axis=-1).reshape(weights[nm].shape)

    delta, new_m, new_v = {}, {}, {}
    shp = w_mod.shape
    outs = _adamw(w_mod.reshape(-1, shp[-1]), grads["w_mod"].reshape(-1, shp[-1]),
                  m_w_mod.reshape(-1, shp[-1]), v_w_mod.reshape(-1, shp[-1]), "adamw_w_mod")
    delta["w_mod"], new_m["w_mod"], new_v["w_mod"] = (o.reshape(shp) for o in outs)
    big_names = ("w_mod", "w_ffn_in", "w_ffn_out", "w_in", "w_out", "w_pw1", "w_pw2")
    small_names = [nm for nm in names if nm not in big_names]
    packs = [_pack([src[nm] for nm in small_names]) for src in (weights, grads, moments_m, moments_v)]
    offs_s = packs[0][1]
    outs = _adamw(*[pk[0] for pk in packs], "adamw_small")
    for o, dst in zip(outs, (delta, new_m, new_v)):
        o = o.reshape(-1)
        for nm, off in zip(small_names, offs_s):
            dst[nm] = _take(o, off, weights[nm].shape)

    def exchanged(k, handle):
        return _push_end(handle, outs[0], f"exchange_wait{k}")[1]

    r_ffi1, r_ffo1 = exchanged(0, ex0)
    r_pw1, r_pw2 = exchanged(1, ex1)
    r_ffi0, r_ffo0 = exchanged(2, ex2)
    r_in, r_out = exchanged(3, ex3)
    for nm, parts in (("w_ffn_in", [r_ffi0, r_ffi1]), ("w_ffn_out", [r_ffo0, r_ffo1]), ("w_pw1", [r_pw1]),
                      ("w_pw2", [r_pw2]), ("w_in", [r_in]), ("w_out", [r_out])):
        grads[nm], delta[nm], new_m[nm], new_v[nm] = _adamw_recv(
            weights[nm], moments_m[nm], moments_v[nm], parts, f"adamw_{nm}")

    return (loss, dh0[None], *[grads[n] for n in names], *[delta[n] for n in names],
            *[new_m[n] for n in names], *[new_v[n] for n in names])
```

```python
import math

import jax
import jax.numpy as jnp
from jax import lax
from jax.experimental import pallas as pl
from jax.experimental.pallas import tpu as pltpu

F32 = jnp.float32
BF16 = jnp.bfloat16
MESH = pl.DeviceIdType.MESH

N_DEV = 8
D_MODEL = 1024
EPS = 1e-6
HEAD_DIM = 64
ATTN_WIDTH = 512
KV_WIDTH = 128
SG_WIDTH = 512
N_SG_GROUPS = 4
CHUNK = 128
IN_WIDTH = 1792
D_FF = 2816
FF_SHARD = 2 * D_FF // N_DEV
CONV_WIDTH = 31
CONV_HALO = 16
GRID_W = 64
ROPE_THETA = 10000.0
LANES = 128
SUBLANES = 8
ROW_BLOCK = 256
ADAM_LR, ADAM_B1, ADAM_B2, ADAM_EPS, ADAM_WD, ADAM_STEP = 0.001, 0.9, 0.999, 1e-08, 0.01, 10


def _tile(n, target, mult=LANES):
    best = None
    for t in range(mult, min(n, target) + 1, mult):
        if n % t == 0:
            best = t
    return best if best is not None else n


def _sigmoid(x):
    return 1.0 / (1.0 + jnp.exp(-x))


def _silu(x):
    return x * _sigmoid(x)


def _dsilu(x):
    s = _sigmoid(x)
    return s * (1.0 + x * (1.0 - s))


_GELU_K = math.sqrt(2.0 / math.pi)


def _gelu(x):
    return 0.5 * x * (1.0 + jnp.tanh(_GELU_K * (x + 0.044715 * x * x * x)))


def _dgelu(x):
    t = jnp.tanh(_GELU_K * (x + 0.044715 * x * x * x))
    return 0.5 * (1.0 + t) + 0.5 * x * (1.0 - t * t) * _GELU_K * (1.0 + 3.0 * 0.044715 * x * x)


def _split_bf16(x):
    hi = x.astype(BF16)
    lo = (x - hi.astype(F32)).astype(BF16)
    return hi, lo


def _dot(a, b, dims):
    return lax.dot_general(a, b, (dims, ((), ())), preferred_element_type=F32)


def _dot3(a, b, dims):
    ah, al = _split_bf16(a)
    bh, bl = _split_bf16(b)
    return _dot(ah, bh, dims) + _dot(ah, bl, dims) + _dot(al, bh, dims)


NN = ((1,), (0,))
NT = ((1,), (1,))
TN = ((0,), (0,))


def _all_gather(xs, name, in_vmem):
    n_arr = len(xs)

    def body(*refs):
        x_refs, out_refs = refs[:n_arr], refs[n_arr:2 * n_arr]
        send_sems, recv_sems, local_sems = refs[2 * n_arr:]
        x, y, c = lax.axis_index("x"), lax.axis_index("y"), lax.axis_index("c")
        me, sibling = (x, y, c), (x, y, 1 - c)
        chips = [(1 - x, y), (x, 1 - y), (1 - x, 1 - y)]

        def rows(a, px, py, pc):
            m_per = xs[a].shape[0]
            return out_refs[a].at[pl.ds((4 * px + 2 * py + pc) * m_per, m_per), :]

        def copy(a, k, block, to, src=None):
            return pltpu.make_async_remote_copy(
                src_ref=rows(a, *block) if src is None else src,
                dst_ref=rows(a, *block),
                send_sem=send_sems.at[7 * a + k],
                recv_sem=recv_sems.at[7 * a + k],
                device_id=to,
                device_id_type=MESH,
            )

        mine, first, passed = [], [], []
        for a in range(n_arr):
            mine.append(pltpu.make_async_copy(x_refs[a], rows(a, *me), local_sems.at[a]))
            mine[-1].start()
            first.append(copy(a, 0, me, sibling, src=x_refs[a]))
            first += [copy(a, 1 + j, me, (*chip, c), src=x_refs[a]) for j, chip in enumerate(chips)]
        for cp in first:
            cp.start()
        for a in range(n_arr):
            for j, chip in enumerate(chips):
                copy(a, 1 + j, (*chip, c), me).wait_recv()
                passed.append(copy(a, 4 + j, (*chip, c), sibling))
                passed[-1].start()
        for a in range(n_arr):
            copy(a, 0, sibling, me).wait_recv()
            for j, chip in enumerate(chips):
                copy(a, 4 + j, (*chip, 1 - c), me).wait_recv()
        for cp in first + passed:
            cp.wait_send()
        for cp in mine:
            cp.wait()

    space = pltpu.VMEM if in_vmem else pl.ANY
    return pl.pallas_call(
        body,
        name=name,
        out_shape=[jax.ShapeDtypeStruct((N_DEV * t.shape[0], t.shape[1]), t.dtype) for t in xs],
        in_specs=[pl.BlockSpec(memory_space=space)] * n_arr,
        out_specs=[pl.BlockSpec(memory_space=space)] * n_arr,
        scratch_shapes=[
            pltpu.SemaphoreType.DMA((7 * n_arr,)),
            pltpu.SemaphoreType.DMA((7 * n_arr,)),
            pltpu.SemaphoreType.DMA((n_arr,)),
        ],
    )(*xs)


HBM_SPEC = pl.BlockSpec(memory_space=pltpu.HBM)
SEM_SPEC = pl.BlockSpec(memory_space=pltpu.SEMAPHORE)
DATAFLOW_EFFECT = pltpu.SideEffectType.DATAFLOW_SIDE_EFFECTING


def _peers(x, y, c):
    for k in range(1, N_DEV):
        px = 1 - x if (k >> 2) & 1 else x
        py = 1 - y if (k >> 1) & 1 else y
        pc = 1 - c if k & 1 else c
        yield k - 1, (px, py, pc), 4 * px + 2 * py + pc


def _push_copies(src_refs, land_refs, send_sems, recv_sems, shapes, whole_src):
    x, y, c = lax.axis_index("x"), lax.axis_index("y"), lax.axis_index("c")
    me = 4 * x + 2 * y + c
    for a, (m_per, _) in enumerate(shapes):
        def block(ref, idx, m_per=m_per):
            return ref.at[pl.ds(idx * m_per, m_per), :]

        for k, peer, pidx in _peers(x, y, c):
            src = src_refs[a] if whole_src else block(src_refs[a], pidx)
            sems = dict(send_sem=send_sems.at[N_DEV * a + k], recv_sem=recv_sems.at[N_DEV * a + k],
                        device_id=peer, device_id_type=MESH)
            yield (pltpu.make_async_remote_copy(src_ref=src, dst_ref=block(land_refs[a], me), **sems),
                   pltpu.make_async_remote_copy(src_ref=src, dst_ref=block(land_refs[a], pidx), **sems))


def _own_copies(src_refs, land_refs, recv_sems, shapes, whole_src):
    me = 4 * lax.axis_index("x") + 2 * lax.axis_index("y") + lax.axis_index("c")
    for a, (m_per, _) in enumerate(shapes):
        mine = pl.ds(me * m_per, m_per)
        src = src_refs[a] if whole_src else src_refs[a].at[mine, :]
        yield pltpu.make_async_copy(src, land_refs[a].at[mine, :], recv_sems.at[N_DEV * a + N_DEV - 1])


def _push_begin(srcs, whole_src, name):
    n_arr = len(srcs)
    shapes = [(t.shape[0] if whole_src else t.shape[0] // N_DEV, t.shape[1]) for t in srcs]
    lands = [lax.empty((N_DEV * m, n), t.dtype) for (m, n), t in zip(shapes, srcs)]

    def body(*refs):
        src_refs, land_refs = refs[:n_arr], refs[n_arr:2 * n_arr]
        send_sems, recv_sems = refs[2 * n_arr], refs[2 * n_arr + 1]
        token = refs[-1]
        for outgoing, _ in _push_copies(src_refs, land_refs, send_sems, recv_sems, shapes, whole_src):
            outgoing.start()
        for own in _own_copies(src_refs, land_refs, recv_sems, shapes, whole_src):
            own.start()
        token[...] = jnp.zeros_like(token)

    operands = [pltpu.with_memory_space_constraint(t, pltpu.HBM) for t in list(srcs) + lands]
    outs = pl.pallas_call(
        body, name=name,
        out_shape=(pltpu.SemaphoreType.DMA((N_DEV * n_arr,)), pltpu.SemaphoreType.DMA((N_DEV * n_arr,)),
                   *[pltpu.HBM(t.shape, t.dtype) for t in operands],
                   jax.ShapeDtypeStruct((SUBLANES, LANES), F32)),
        in_specs=[HBM_SPEC] * (2 * n_arr),
        out_specs=(SEM_SPEC, SEM_SPEC, *[HBM_SPEC] * (2 * n_arr), pl.BlockSpec(memory_space=pltpu.VMEM)),
        input_output_aliases={i: 2 + i for i in range(2 * n_arr)},
        compiler_params=pltpu.CompilerParams(has_side_effects=DATAFLOW_EFFECT),
    )(*operands)
    return outs[0], outs[1], list(outs[2:2 + n_arr]), list(outs[2 + n_arr:2 + 2 * n_arr]), outs[-1], whole_src


def _push_end(handle, after, name):
    send_sems, recv_sems, srcs, lands, _, whole_src = handle
    n_arr = len(srcs)
    shapes = [(t.shape[0] // N_DEV, t.shape[1]) for t in lands]

    def body(*refs):
        src_refs, land_refs = refs[:n_arr], refs[n_arr:2 * n_arr]
        send_sems_ref, recv_sems_ref = refs[2 * n_arr], refs[2 * n_arr + 1]
        for outgoing, incoming in _push_copies(src_refs, land_refs, send_sems_ref, recv_sems_ref, shapes, whole_src):
            outgoing.wait_send()
            incoming.wait_recv()
        for own in _own_copies(src_refs, land_refs, recv_sems_ref, shapes, whole_src):
            own.wait()

    outs = pl.pallas_call(
        body, name=name,
        out_shape=tuple(pltpu.HBM(t.shape, t.dtype) for t in srcs + lands),
        in_specs=[HBM_SPEC] * (2 * n_arr) + [SEM_SPEC, SEM_SPEC, pl.BlockSpec(memory_space=pl.ANY)],
        out_specs=tuple([HBM_SPEC] * (2 * n_arr)),
        input_output_aliases={i: i for i in range(2 * n_arr)},
        compiler_params=pltpu.CompilerParams(has_side_effects=DATAFLOW_EFFECT),
    )(*srcs, *lands, send_sems, recv_sems, after)
    return list(outs[:n_arr]), list(outs[n_arr:])


def _sum_devices(r, name, rows_per_step=ROW_BLOCK):
    _, m, n = r.shape
    tm = _tile(m, rows_per_step, 8)

    def body(r_ref, o_ref):
        acc = r_ref[0].astype(F32)
        for s in range(1, N_DEV):
            acc = acc + r_ref[s].astype(F32)
        o_ref[...] = acc

    return pl.pallas_call(
        body,
        name=name,
        grid=(m // tm,),
        out_shape=jax.ShapeDtypeStruct((m, n), F32),
        in_specs=[pl.BlockSpec((N_DEV, tm, n), lambda i: (0, i, 0))],
        out_specs=pl.BlockSpec((tm, n), lambda i: (i, 0)),
        compiler_params=pltpu.CompilerParams(dimension_semantics=("parallel",)),
    )(r)


def _get(ref):
    return ref[0] if len(ref.shape) == 3 else ref[...]


def _put(ref, val):
    if len(ref.shape) == 3:
        ref[0] = val
    else:
        ref[...] = val


def _norm_mod(hv, g, sc, sh):
    r = lax.rsqrt(jnp.mean(hv * hv, axis=-1, keepdims=True) + EPS)
    return (hv * r) * g * (1.0 + sc) + sh


def _mm_call(name, a, b, a_spec, b_spec, out_sds, o_spec, grid, dims, acc_shape, bias=None,
             res=None, gate=None, raw_out=False, vec_spec=None, norm=None):
    nk = grid[2]
    operands, in_specs = [a, b], [a_spec, b_spec]
    if bias is not None:
        operands.append(bias)
        in_specs.append(vec_spec)
    if res is not None:
        operands += [res, gate]
        in_specs += [o_spec, vec_spec]
    if norm is not None:
        assert grid[1] == 1
        operands += list(norm)
        in_specs += [vec_spec] * 3
    out_shape, out_specs = [out_sds], [o_spec]
    if raw_out:
        out_shape.append(jax.ShapeDtypeStruct(out_sds.shape, BF16))
        out_specs.append(o_spec)
    if norm is not None:
        out_shape.append(jax.ShapeDtypeStruct(out_sds.shape, BF16))
        out_specs.append(o_spec)

    def body(*refs):
        it = iter(refs)
        a_ref, b_ref = next(it), next(it)
        bias_ref = next(it) if bias is not None else None
        res_ref, gate_ref = (next(it), next(it)) if res is not None else (None, None)
        norm_refs = (next(it), next(it), next(it)) if norm is not None else None
        o_ref = next(it)
        raw_ref = next(it) if raw_out else None
        xn_ref = next(it) if norm is not None else None
        acc = next(it) if nk > 1 else None
        k = pl.program_id(2)
        part = _dot(_get(a_ref).astype(BF16), _get(b_ref).astype(BF16), dims)

        def finish(y):
            if bias_ref is not None:
                y = y + bias_ref[...]
            if raw_ref is not None:
                raw_ref[...] = y.astype(BF16)
            if res_ref is not None:
                y = res_ref[...] + gate_ref[...] * y
            _put(o_ref, y.astype(out_sds.dtype))
            if xn_ref is not None:
                xn_ref[...] = _norm_mod(y, *[r[...] for r in norm_refs]).astype(BF16)

        if nk == 1:
            finish(part)
        else:
            @pl.when(k == 0)
            def _():
                acc[...] = part

            @pl.when(k > 0)
            def _():
                acc[...] += part

            @pl.when(k == nk - 1)
            def _():
                finish(acc[...])

    outs = pl.pallas_call(
        body,
        name=name,
        grid=grid,
        out_shape=out_shape,
        in_specs=in_specs,
        out_specs=out_specs,
        scratch_shapes=[pltpu.VMEM(acc_shape, F32)] if nk > 1 else [],
        compiler_params=pltpu.CompilerParams(dimension_semantics=("parallel", "parallel", "arbitrary")),
    )(*operands)
    return outs if len(outs) > 1 else outs[0]


def _mm(a, b, mode, name, out_dtype=F32, bias=None, res=None, gate=None, raw_out=False,
        tm=512, tn=1024, tk=1024, a_row_off=0, norm=None):
    if mode == "nn":
        K, N = b.shape
        M = a.shape[0] - a_row_off
    elif mode == "nt":
        N, K = b.shape
        M = a.shape[0] - a_row_off
    else:
        (K, M), N = a.shape, b.shape[1]
    tm, tn, tk = _tile(M, tm), _tile(N, tn), _tile(K, tk)
    off = a_row_off // tm
    dims = {"nn": NN, "nt": NT, "tn": TN}[mode]
    a_spec = (pl.BlockSpec((tk, tm), lambda i, j, k: (k, i)) if mode == "tn"
              else pl.BlockSpec((tm, tk), lambda i, j, k: (i + off, k)))
    b_spec = (pl.BlockSpec((tn, tk), lambda i, j, k: (j, k)) if mode == "nt"
              else pl.BlockSpec((tk, tn), lambda i, j, k: (k, j)))
    return _mm_call(name, a, b, a_spec, b_spec, jax.ShapeDtypeStruct((M, N), out_dtype),
                    pl.BlockSpec((tm, tn), lambda i, j, k: (i, j)), (M // tm, N // tn, K // tk), dims,
                    (tm, tn), bias, res, gate, raw_out, pl.BlockSpec((1, tn), lambda i, j, k: (0, j)), norm)


def _mm_sum_shards(a3, b3, mode, name, out_dtype=F32, res=None, gate=None, raw_out=False, tm=512, norm=None):
    S, M, kk = a3.shape
    N = b3.shape[2] if mode == "nn" else b3.shape[1]
    tm = _tile(M, tm)
    dims = NN if mode == "nn" else NT
    has_res = res is not None

    def body(*refs):
        it = iter(refs)
        a_ref, b_ref = next(it), next(it)
        res_ref, gate_ref = (next(it), next(it)) if has_res else (None, None)
        norm_refs = (next(it), next(it), next(it)) if norm is not None else None
        o_ref = next(it)
        raw_ref = next(it) if raw_out else None
        xn_ref = next(it) if norm is not None else None
        y = _dot(a_ref[0], b_ref[0], dims)
        for s in range(1, S):
            y = y + _dot(a_ref[s], b_ref[s], dims)
        if raw_ref is not None:
            raw_ref[...] = y.astype(BF16)
        if has_res:
            y = res_ref[...] + gate_ref[...] * y
        o_ref[...] = y.astype(out_dtype)
        if xn_ref is not None:
            xn_ref[...] = _norm_mod(y, *[r[...] for r in norm_refs]).astype(BF16)

    tile = pl.BlockSpec((tm, N), lambda i: (i, 0))
    operands = [a3, b3] + ([res, gate] if has_res else []) + (list(norm) if norm is not None else [])
    in_specs = [pl.BlockSpec((S, tm, kk), lambda i: (0, i, 0)), pl.BlockSpec(b3.shape, lambda i: (0, 0, 0))]
    in_specs += [tile, _vec_spec(N)] if has_res else []
    in_specs += [_vec_spec(N)] * 3 if norm is not None else []
    out_shape = [jax.ShapeDtypeStruct((M, N), out_dtype)] + ([jax.ShapeDtypeStruct((M, N), BF16)] if raw_out else [])
    out_shape += [jax.ShapeDtypeStruct((M, N), BF16)] if norm is not None else []
    outs = pl.pallas_call(
        body, name=name, grid=(M // tm,),
        out_shape=out_shape, in_specs=in_specs, out_specs=[tile] * len(out_shape),
        compiler_params=pltpu.CompilerParams(dimension_semantics=("parallel",)),
    )(*operands)
    return outs if len(outs) > 1 else outs[0]


def _mm_tn_shard_rows(a3, b, name, out_dtype, tn=1024, tk=1024):
    S, T, m = a3.shape
    N = b.shape[1]
    tn, tk = _tile(N, tn), _tile(T, tk)
    return _mm_call(name, a3, b, pl.BlockSpec((1, tk, m), lambda i, j, k: (i, k, 0)),
                    pl.BlockSpec((tk, tn), lambda i, j, k: (k, j)), jax.ShapeDtypeStruct((S, m, N), out_dtype),
                    pl.BlockSpec((1, m, tn), lambda i, j, k: (i, 0, j)), (S, N // tn, T // tk), TN, (m, tn))


def _row_spec(tm, width, off=0):
    return pl.BlockSpec((tm, width), lambda i: (i + off, 0))


def _vec_spec(width):
    return pl.BlockSpec((1, width), lambda i: (0, 0))


def _norm_mod_fwd_cat(hc, h, g, csc, csh, sc, sh, name):
    (C, Dm), T = hc.shape, h.shape[0]
    tm = _tile(math.gcd(C, T), ROW_BLOCK, 8)
    off = C // tm

    def body(hc_ref, h_ref, g_ref, csc_ref, csh_ref, sc_ref, sh_ref, o_ref):
        is_ctx = pl.program_id(0) < off
        hv = jnp.where(is_ctx, hc_ref[...], h_ref[...])
        scv = jnp.where(is_ctx, csc_ref[...], sc_ref[...])
        shv = jnp.where(is_ctx, csh_ref[...], sh_ref[...])
        r = lax.rsqrt(jnp.mean(hv * hv, axis=-1, keepdims=True) + EPS)
        o_ref[...] = ((hv * r) * g_ref[...] * (1.0 + scv) + shv).astype(BF16)

    return pl.pallas_call(
        body, name=name, grid=((C + T) // tm,),
        out_shape=jax.ShapeDtypeStruct((C + T, Dm), BF16),
        in_specs=[pl.BlockSpec((tm, Dm), lambda i: (jnp.minimum(i, off - 1), 0)),
                  pl.BlockSpec((tm, Dm), lambda i: (jnp.maximum(i - off, 0), 0))] + [_vec_spec(Dm)] * 5,
        out_specs=_row_spec(tm, Dm),
        compiler_params=pltpu.CompilerParams(dimension_semantics=("parallel",)),
    )(hc, h, g, csc, csh, sc, sh)


def _gate_grads(dh, y_ref, gt_ref, dy_ref, dgt_ref, dsum_ref):
    dy = dh * gt_ref[...]
    dgt_ref[...] += jnp.sum(dh * y_ref[...].astype(F32), axis=0, keepdims=True)
    dsum_ref[...] += jnp.sum(dy, axis=0, keepdims=True)
    dy_ref[...] = dy.astype(BF16)


def _norm_mod_bwd(h, g, sc, dxm, dres, name, dxm_row_off=0, gate=None):
    R, Dm = h.shape
    tm = _tile(R, ROW_BLOCK, 8)
    off = dxm_row_off // tm
    has_res = dres is not None
    has_gate = gate is not None

    def body(*refs):
        it = iter(refs)
        h_ref, g_ref, sc_ref, dx_ref = next(it), next(it), next(it), next(it)
        dres_ref = next(it) if has_res else None
        y_ref, gt_ref = (next(it), next(it)) if has_gate else (None, None)
        dh_ref, da_ref, dsh_ref = next(it), next(it), next(it)
        gate_out = (next(it), next(it), next(it)) if has_gate else ()
        i = pl.program_id(0)

        @pl.when(i == 0)
        def _():
            for ref in (da_ref, dsh_ref) + gate_out[1:]:
                ref[...] = jnp.zeros_like(ref)

        hv = h_ref[...]
        dx = dx_ref[...].astype(F32)
        r = lax.rsqrt(jnp.mean(hv * hv, axis=-1, keepdims=True) + EPS)
        n = hv * r
        da_ref[...] += jnp.sum(dx * n, axis=0, keepdims=True)
        dsh_ref[...] += jnp.sum(dx, axis=0, keepdims=True)
        dn = dx * (g_ref[...] * (1.0 + sc_ref[...]))
        dh = r * (dn - n * jnp.mean(dn * n, axis=-1, keepdims=True))
        if has_res:
            dh = dh + dres_ref[...]
        dh_ref[...] = dh
        if has_gate:
            _gate_grads(dh, y_ref, gt_ref, *gate_out)

    operands = [h, g, sc, dxm] + ([dres] if has_res else []) + (list(gate) if has_gate else [])
    in_specs = [_row_spec(tm, Dm), _vec_spec(Dm), _vec_spec(Dm), _row_spec(tm, Dm, off)]
    in_specs += [_row_spec(tm, Dm)] if has_res else []
    in_specs += [_row_spec(tm, Dm), _vec_spec(Dm)] if has_gate else []
    vec = jax.ShapeDtypeStruct((1, Dm), F32)
    out_shape = [jax.ShapeDtypeStruct((R, Dm), F32), vec, vec]
    out_specs = [_row_spec(tm, Dm), _vec_spec(Dm), _vec_spec(Dm)]
    if has_gate:
        out_shape += [jax.ShapeDtypeStruct((R, Dm), BF16), vec, vec]
        out_specs += [_row_spec(tm, Dm), _vec_spec(Dm), _vec_spec(Dm)]
    return pl.pallas_call(
        body, name=name, grid=(R // tm,),
        out_shape=out_shape, in_specs=in_specs, out_specs=out_specs,
        compiler_params=pltpu.CompilerParams(dimension_semantics=("arbitrary",)),
    )(*operands)


def _ffn_in_swiglu(xf, w3, name, tm=1024):
    T, K = xf.shape
    S, n, _ = w3.shape
    half = S // 2
    tm = _tile(T, tm)

    def body(a_ref, wg_ref, wu_ref, gu_ref, act_ref):
        a = a_ref[...]
        g = _dot(a, wg_ref[0], NT)
        u = _dot(a, wu_ref[0], NT)
        gu_ref[0, 0] = g.astype(BF16)
        gu_ref[1, 0] = u.astype(BF16)
        act_ref[0] = (_silu(g) * u).astype(BF16)

    return pl.pallas_call(
        body, name=name, grid=(T // tm, half),
        out_shape=[jax.ShapeDtypeStruct((2, half, T, n), BF16), jax.ShapeDtypeStruct((half, T, n), BF16)],
        in_specs=[pl.BlockSpec((tm, K), lambda i, j: (i, 0)),
                  pl.BlockSpec((1, n, K), lambda i, j: (j, 0, 0)),
                  pl.BlockSpec((1, n, K), lambda i, j: (j + half, 0, 0))],
        out_specs=[pl.BlockSpec((2, 1, tm, n), lambda i, j: (0, j, i, 0)),
                   pl.BlockSpec((1, tm, n), lambda i, j: (j, i, 0))],
        compiler_params=pltpu.CompilerParams(dimension_semantics=("parallel", "parallel")),
    )(xf, w3, w3)


def _ffn_out_dx_swiglu(df, wo, gu, name, tm=1024):
    T, Dm = df.shape
    half, n, _ = wo.shape
    tm = _tile(T, tm)

    def body(df_ref, w_ref, gu_ref, o_ref):
        da = _dot(df_ref[...], w_ref[0], NT)
        g = gu_ref[0, 0].astype(F32)
        u = gu_ref[1, 0].astype(F32)
        s = _sigmoid(g)
        o_ref[0, 0] = (da * u * (s * (1.0 + g * (1.0 - s)))).astype(BF16)
        o_ref[1, 0] = (da * (g * s)).astype(BF16)

    gu_spec = pl.BlockSpec((2, 1, tm, n), lambda i, j: (0, j, i, 0))
    return pl.pallas_call(
        body, name=name, grid=(T // tm, half),
        out_shape=jax.ShapeDtypeStruct(gu.shape, BF16),
        in_specs=[pl.BlockSpec((tm, Dm), lambda i, j: (i, 0)),
                  pl.BlockSpec((1, n, Dm), lambda i, j: (j, 0, 0)), gu_spec],
        out_specs=gu_spec,
        compiler_params=pltpu.CompilerParams(dimension_semantics=("parallel", "parallel")),
    )(df, wo, gu)


def _glu_fwd(ag, name):
    R = ag.shape[0]
    tm = _tile(R, ROW_BLOCK, 8)

    def body(ag_ref, o_ref):
        o_ref[...] = ag_ref[:, :D_MODEL].astype(F32) * _sigmoid(ag_ref[:, D_MODEL:].astype(F32))

    return pl.pallas_call(
        body, name=name, grid=(R // tm,),
        out_shape=jax.ShapeDtypeStruct((R, D_MODEL), F32),
        in_specs=[_row_spec(tm, 2 * D_MODEL)],
        out_specs=_row_spec(tm, D_MODEL),
        compiler_params=pltpu.CompilerParams(dimension_semantics=("parallel",)),
    )(ag)


def _glu_bwd(ag, dhg, name):
    R = ag.shape[0]
    tm = _tile(R, ROW_BLOCK, 8)

    def body(ag_ref, dh_ref, o_ref, s_ref):
        i = pl.program_id(0)

        @pl.when(i == 0)
        def _():
            s_ref[...] = jnp.zeros_like(s_ref)

        a = ag_ref[:, :D_MODEL].astype(F32)
        s = _sigmoid(ag_ref[:, D_MODEL:].astype(F32))
        dh = dh_ref[...]
        da = dh * s
        dg = dh * a * s * (1.0 - s)
        o_ref[:, :D_MODEL] = da.astype(BF16)
        o_ref[:, D_MODEL:] = dg.astype(BF16)
        s_ref[:, :D_MODEL] += jnp.sum(da, axis=0, keepdims=True)
        s_ref[:, D_MODEL:] += jnp.sum(dg, axis=0, keepdims=True)

    return pl.pallas_call(
        body, name=name, grid=(R // tm,),
        out_shape=[jax.ShapeDtypeStruct((R, 2 * D_MODEL), BF16), jax.ShapeDtypeStruct((1, 2 * D_MODEL), F32)],
        in_specs=[_row_spec(tm, 2 * D_MODEL), _row_spec(tm, D_MODEL)],
        out_specs=[_row_spec(tm, 2 * D_MODEL), _vec_spec(2 * D_MODEL)],
        compiler_params=pltpu.CompilerParams(dimension_semantics=("arbitrary",)),
    )(ag, dhg)


def _halo_specs(tm, nblk, width):
    per = tm // CONV_HALO
    prev = pl.BlockSpec((CONV_HALO, width), lambda i: (jnp.maximum(i * per - 1, 0), 0))
    nxt = pl.BlockSpec((CONV_HALO, width), lambda i: (jnp.minimum((i + 1) * per, nblk * per - 1), 0))
    return prev, nxt


def _fill_halo(scr, prev_ref, cur_ref, next_ref, i, nblk, tm):
    scr[0:CONV_HALO, :] = jnp.where(i > 0, prev_ref[...], 0.0)
    scr[CONV_HALO:CONV_HALO + tm, :] = cur_ref[...]
    scr[CONV_HALO + tm:2 * CONV_HALO + tm, :] = jnp.where(i < nblk - 1, next_ref[...], 0.0)


CONV_ROWS = 128


def _windows(scr, cols, tm):
    reach = (CONV_WIDTH // SUBLANES) * SUBLANES
    for r in range(SUBLANES):
        base = scr[pl.ds(r, tm + reach), cols]
        for a in range(reach // SUBLANES + 1):
            off = SUBLANES * a + r
            if 1 <= off <= CONV_WIDTH:
                yield off, base[SUBLANES * a:SUBLANES * a + tm]


def _conv_fwd(hg, w_dw, b_dw, name):
    R, Dm = hg.shape
    tm = _tile(R, CONV_ROWS, CONV_HALO)
    nblk = R // tm
    prev_spec, next_spec = _halo_specs(tm, nblk, Dm)

    def body(prev_ref, cur_ref, next_ref, w_ref, bdw_ref, hd_ref, scr):
        _fill_halo(scr, prev_ref, cur_ref, next_ref, pl.program_id(0), nblk, tm)
        for cb in range(Dm // LANES):
            cols = slice(cb * LANES, (cb + 1) * LANES)
            acc = jnp.zeros((tm, LANES), F32) + bdw_ref[:, cols]
            for off, win in _windows(scr, cols, tm):
                acc = acc + w_ref[off - 1:off, cols] * win
            hd_ref[:, cols] = acc

    return pl.pallas_call(
        body, name=name, grid=(nblk,),
        out_shape=jax.ShapeDtypeStruct((R, Dm), F32),
        in_specs=[prev_spec, _row_spec(tm, Dm), next_spec,
                  pl.BlockSpec((CONV_WIDTH, Dm), lambda i: (0, 0)), _vec_spec(Dm)],
        out_specs=_row_spec(tm, Dm),
        scratch_shapes=[pltpu.VMEM((tm + 2 * CONV_HALO, Dm), F32)],
        compiler_params=pltpu.CompilerParams(dimension_semantics=("parallel",)),
    )(hg, hg, hg, w_dw, b_dw)


def _ln_silu_fwd(hd, ln_g, ln_b, name):
    R, Dm = hd.shape
    tm = _tile(R, ROW_BLOCK, 8)

    def body(hd_ref, g_ref, b_ref, hs_ref):
        hd = hd_ref[...]
        xc = hd - jnp.mean(hd, axis=-1, keepdims=True)
        rs = lax.rsqrt(jnp.mean(xc * xc, axis=-1, keepdims=True) + EPS)
        hs_ref[...] = _silu(xc * rs * g_ref[...] + b_ref[...]).astype(BF16)

    return pl.pallas_call(
        body, name=name, grid=(R // tm,),
        out_shape=jax.ShapeDtypeStruct((R, Dm), BF16),
        in_specs=[_row_spec(tm, Dm), _vec_spec(Dm), _vec_spec(Dm)],
        out_specs=_row_spec(tm, Dm),
        compiler_params=pltpu.CompilerParams(dimension_semantics=("parallel",)),
    )(hd, ln_g, ln_b)


def _ln_silu_bwd(dhs, hd, ln_g, ln_b, name):
    R, Dm = hd.shape
    tm = _tile(R, ROW_BLOCK, 8)

    def body(dhs_ref, hd_ref, g_ref, b_ref, dhd_ref, dg_ref, db_ref, dsum_ref):
        i = pl.program_id(0)

        @pl.when(i == 0)
        def _():
            dg_ref[...] = jnp.zeros_like(dg_ref)
            db_ref[...] = jnp.zeros_like(db_ref)
            dsum_ref[...] = jnp.zeros_like(dsum_ref)

        hd = hd_ref[...]
        mu = jnp.mean(hd, axis=-1, keepdims=True)
        xc = hd - mu
        rs = lax.rsqrt(jnp.mean(xc * xc, axis=-1, keepdims=True) + EPS)
        z = xc * rs
        hl = z * g_ref[...] + b_ref[...]
        dhl = dhs_ref[...].astype(F32) * _dsilu(hl)
        dg_ref[...] += jnp.sum(dhl * z, axis=0, keepdims=True)
        db_ref[...] += jnp.sum(dhl, axis=0, keepdims=True)
        dz = dhl * g_ref[...]
        dhd = rs * (dz - jnp.mean(dz, axis=-1, keepdims=True) - z * jnp.mean(dz * z, axis=-1, keepdims=True))
        dsum_ref[...] += jnp.sum(dhd, axis=0, keepdims=True)
        dhd_ref[...] = dhd

    return pl.pallas_call(
        body, name=name, grid=(R // tm,),
        out_shape=[jax.ShapeDtypeStruct((R, Dm), F32)] + [jax.ShapeDtypeStruct((1, Dm), F32)] * 3,
        in_specs=[_row_spec(tm, Dm), _row_spec(tm, Dm), _vec_spec(Dm), _vec_spec(Dm)],
        out_specs=[_row_spec(tm, Dm), _vec_spec(Dm), _vec_spec(Dm), _vec_spec(Dm)],
        compiler_params=pltpu.CompilerParams(dimension_semantics=("arbitrary",)),
    )(dhs, hd, ln_g, ln_b)


def _conv_bwd(dhd, hg, w_dw, name):
    R, Dm = hg.shape
    tm = _tile(R, CONV_ROWS, CONV_HALO)
    nblk = R // tm
    prev_spec, next_spec = _halo_specs(tm, nblk, Dm)

    def body(dprev, dcur, dnext, gprev, gcur, gnext, w_ref, dhg_ref, dw_ref, dscr, gscr, dwp):
        i = pl.program_id(0)

        @pl.when(i == 0)
        def _():
            dwp[...] = jnp.zeros_like(dwp)

        _fill_halo(dscr, dprev, dcur, dnext, i, nblk, tm)
        _fill_halo(gscr, gprev, gcur, gnext, i, nblk, tm)
        for cb in range(Dm // LANES):
            cols = slice(cb * LANES, (cb + 1) * LANES)
            acc = jnp.zeros((tm, LANES), F32)
            for off, win in _windows(dscr, cols, tm):
                j = CONV_WIDTH - off
                acc = acc + w_ref[j:j + 1, cols] * win
            dhg_ref[:, cols] = acc
            d_here = dcur[:, cols]
            for off, win in _windows(gscr, cols, tm):
                j = off - 1
                prod = d_here * win
                part = prod[0:SUBLANES]
                for k in range(1, tm // SUBLANES):
                    part = part + prod[k * SUBLANES:(k + 1) * SUBLANES]
                dwp[j * SUBLANES:(j + 1) * SUBLANES, cols] += part

        @pl.when(i == nblk - 1)
        def _():
            for j in range(CONV_WIDTH):
                dw_ref[j:j + 1, :] = jnp.sum(dwp[j * SUBLANES:(j + 1) * SUBLANES, :], axis=0, keepdims=True)

    return pl.pallas_call(
        body, name=name, grid=(nblk,),
        out_shape=[jax.ShapeDtypeStruct((R, Dm), F32), jax.ShapeDtypeStruct((CONV_WIDTH, Dm), F32)],
        in_specs=[prev_spec, _row_spec(tm, Dm), next_spec, prev_spec, _row_spec(tm, Dm), next_spec,
                  pl.BlockSpec((CONV_WIDTH, Dm), lambda i: (0, 0))],
        out_specs=[_row_spec(tm, Dm), pl.BlockSpec((CONV_WIDTH, Dm), lambda i: (0, 0))],
        scratch_shapes=[pltpu.VMEM((tm + 2 * CONV_HALO, Dm), F32)] * 2
        + [pltpu.VMEM((CONV_WIDTH * SUBLANES, Dm), F32)],
        compiler_params=pltpu.CompilerParams(dimension_semantics=("arbitrary",)),
    )(dhd, dhd, dhd, hg, hg, hg, w_dw)


def _swap16(y, lane):
    return jnp.where((lane & 16) == 0, pltpu.roll(y, LANES - 16, 1), pltpu.roll(y, 16, 1))


def _head_mean(v, bd):
    hi, lo = _split_bf16(v)
    return (_dot(hi, bd, NN) + _dot(lo, bd, NN)) * (1.0 / HEAD_DIM)


Q_COLS = (0, ATTN_WIDTH)
K_COLS = (ATTN_WIDTH, ATTN_WIDTH + HEAD_DIM * 2)
V_COLS = (K_COLS[1], K_COLS[1] + HEAD_DIM * 2)
SU_COLS = (V_COLS[1], V_COLS[1] + SG_WIDTH)
SV_COLS = (SU_COLS[1], SU_COLS[1] + SG_WIDTH)


def _mix_prep_fwd(p, ctx_rows, cos, sin, qg, kg, bd, w_sp, b_spt, name):
    TT = p.shape[0]
    off = ctx_rows // CHUNK
    q_scale = HEAD_DIM ** -0.5

    def body(p_ref, cos_ref, sin_ref, qg_ref, kg_ref, bd_ref, w_ref, b_ref,
             q_ref, kp_ref, vp_ref, kt_ref, sg_ref):
        lane = lax.broadcasted_iota(jnp.int32, (CHUNK, LANES), 1)
        low = lane < HEAD_DIM
        cs, sn, bdv = cos_ref[...], sin_ref[...], bd_ref[...]

        def norm_rope(xv, gain):
            r = lax.rsqrt(_head_mean(xv * xv, bdv) + EPS)
            yv = xv * r * gain
            return yv * cs + _swap16(yv, lane) * sn

        def pad_heads(ref, t):
            tr = pltpu.roll(t, HEAD_DIM, 1)
            ref[0, 0] = jnp.where(low, t, 0.0).astype(BF16)
            ref[0, 1] = jnp.where(low, 0.0, tr).astype(BF16)
            ref[1, 0] = jnp.where(low, tr, 0.0).astype(BF16)
            ref[1, 1] = jnp.where(low, 0.0, t).astype(BF16)

        for a in range(ATTN_WIDTH // LANES):
            xv = p_ref[:, a * LANES:(a + 1) * LANES]
            q_ref[:, a * LANES:(a + 1) * LANES] = (norm_rope(xv, qg_ref[...]) * q_scale).astype(BF16)
        kh = norm_rope(p_ref[:, K_COLS[0]:K_COLS[1]], kg_ref[...])
        pad_heads(kp_ref, kh)
        pad_heads(vp_ref, p_ref[:, V_COLS[0]:V_COLS[1]])
        kht = kh.T
        kt_ref[0] = kht[:HEAD_DIM].astype(BF16)
        kt_ref[1] = kht[HEAD_DIM:].astype(BF16)
        for g in range(N_SG_GROUPS):
            u = _gelu(p_ref[:, SU_COLS[0] + g * LANES:SU_COLS[0] + (g + 1) * LANES])
            vg = _gelu(p_ref[:, SV_COLS[0] + g * LANES:SV_COLS[0] + (g + 1) * LANES])
            xc = vg - jnp.mean(vg, axis=-1, keepdims=True)
            vn = xc * lax.rsqrt(jnp.mean(xc * xc, axis=-1, keepdims=True) + EPS)
            mixed = _dot(w_ref[g].astype(BF16), vn.astype(BF16), NN) + b_ref[:, g:g + 1]
            sg_ref[:, g * LANES:(g + 1) * LANES] = (u * mixed).astype(BF16)

    def row(width):
        return pl.BlockSpec((CHUNK, width), lambda i: (i, 0))

    def whole(shape):
        return pl.BlockSpec(shape, lambda i: (0,) * len(shape))

    pad_spec = pl.BlockSpec((2, 2, CHUNK, LANES), lambda i: (0, 0, i, 0))
    return pl.pallas_call(
        body, name=name, grid=(TT // CHUNK,),
        out_shape=[jax.ShapeDtypeStruct((TT, ATTN_WIDTH), BF16),
                   jax.ShapeDtypeStruct((2, 2, TT, LANES), BF16), jax.ShapeDtypeStruct((2, 2, TT, LANES), BF16),
                   jax.ShapeDtypeStruct((2, HEAD_DIM, TT), BF16),
                   jax.ShapeDtypeStruct((TT - ctx_rows, ATTN_WIDTH + SG_WIDTH), BF16)],
        in_specs=[row(IN_WIDTH), row(LANES), row(LANES), whole((1, LANES)), whole((1, LANES)),
                  whole((LANES, LANES)), whole((N_SG_GROUPS, CHUNK, CHUNK)), whole((CHUNK, N_SG_GROUPS))],
        out_specs=[row(ATTN_WIDTH), pad_spec, pad_spec,
                   pl.BlockSpec((2, HEAD_DIM, CHUNK), lambda i: (0, 0, i)),
                   pl.BlockSpec((CHUNK, SG_WIDTH), lambda i: (jnp.maximum(i - off, 0), 1))],
        compiler_params=pltpu.CompilerParams(dimension_semantics=("arbitrary",)),
    )(p, cos, sin, qg, kg, bd, w_sp, b_spt)


def _mix_prep_bwd(p, dq, f, dao, ctx_rows, cos, sin, qg, kg, bd, w_sp, w_spt, b_spt, name):
    TT = p.shape[0]
    off = ctx_rows // CHUNK
    q_scale = HEAD_DIM ** -0.5

    def body(p_ref, dq_ref, f_ref, dsg_ref, cos_ref, sin_ref, qg_ref, kg_ref, bd_ref, w_ref, wt_ref,
             b_ref, dp_ref, dqg_ref, dkg_ref, dw_ref, db_ref):
        i = pl.program_id(0)

        @pl.when(i == 0)
        def _():
            dqg_ref[...] = jnp.zeros_like(dqg_ref)
            dkg_ref[...] = jnp.zeros_like(dkg_ref)
            dw_ref[...] = jnp.zeros_like(dw_ref)
            db_ref[...] = jnp.zeros_like(db_ref)

        latent = (i >= off).astype(F32)
        lane = lax.broadcasted_iota(jnp.int32, (CHUNK, LANES), 1)
        low = lane < HEAD_DIM
        cs, sn, bdv = cos_ref[...], sin_ref[...], bd_ref[...]

        def fold(b0):
            return jnp.where(low, f_ref[0, b0] + pltpu.roll(f_ref[0, b0 + 1], HEAD_DIM, 1),
                             pltpu.roll(f_ref[1, b0], HEAD_DIM, 1) + f_ref[1, b0 + 1])

        def norm_rope_bwd(xv, dout, gain):
            r = lax.rsqrt(_head_mean(xv * xv, bdv) + EPS)
            n = xv * r
            dy = dout * cs + _swap16(dout * sn, lane)
            dn = dy * gain
            dx = r * (dn - n * _head_mean(dn * n, bdv))
            return dx, jnp.sum(dy * n, axis=0, keepdims=True)

        for a in range(ATTN_WIDTH // LANES):
            cols = slice(a * LANES, (a + 1) * LANES)
            dx, dg = norm_rope_bwd(p_ref[:, cols], dq_ref[:, cols] * (latent * q_scale), qg_ref[...])
            dp_ref[:, cols] = dx.astype(BF16)
            dqg_ref[...] += dg
        dx, dg = norm_rope_bwd(p_ref[:, K_COLS[0]:K_COLS[1]], fold(0), kg_ref[...])
        dp_ref[:, K_COLS[0]:K_COLS[1]] = dx.astype(BF16)
        dkg_ref[...] += dg
        dp_ref[:, V_COLS[0]:V_COLS[1]] = fold(2).astype(BF16)
        for g in range(N_SG_GROUPS):
            su = p_ref[:, SU_COLS[0] + g * LANES:SU_COLS[0] + (g + 1) * LANES]
            sv = p_ref[:, SV_COLS[0] + g * LANES:SV_COLS[0] + (g + 1) * LANES]
            u, vg = _gelu(su), _gelu(sv)
            xc = vg - jnp.mean(vg, axis=-1, keepdims=True)
            rs = lax.rsqrt(jnp.mean(xc * xc, axis=-1, keepdims=True) + EPS)
            vn = xc * rs
            vnb = vn.astype(BF16)
            mixed = _dot(w_ref[g].astype(BF16), vnb, NN) + b_ref[:, g:g + 1]
            dsg = dsg_ref[:, g * LANES:(g + 1) * LANES].astype(F32) * latent
            du = dsg * mixed
            dmix = dsg * u
            dmb = dmix.astype(BF16)
            db_ref[:, g:g + 1] += jnp.sum(dmix, axis=-1, keepdims=True)
            dw_ref[g] += _dot(dmb, vnb, NT)
            dvn = _dot(wt_ref[g].astype(BF16), dmb, NN)
            dvg = rs * (dvn - jnp.mean(dvn, axis=-1, keepdims=True)
                        - vn * jnp.mean(dvn * vn, axis=-1, keepdims=True))
            dp_ref[:, SU_COLS[0] + g * LANES:SU_COLS[0] + (g + 1) * LANES] = (du * _dgelu(su)).astype(BF16)
            dp_ref[:, SV_COLS[0] + g * LANES:SV_COLS[0] + (g + 1) * LANES] = (dvg * _dgelu(sv)).astype(BF16)

    def row(width):
        return pl.BlockSpec((CHUNK, width), lambda i: (i, 0))

    def latent_row(width, col_block):
        return pl.BlockSpec((CHUNK, width), lambda i: (jnp.maximum(i - off, 0), col_block))

    def whole(shape):
        return pl.BlockSpec(shape, lambda i: (0,) * len(shape))

    return pl.pallas_call(
        body, name=name, grid=(TT // CHUNK,),
        out_shape=[jax.ShapeDtypeStruct((TT, IN_WIDTH), BF16), jax.ShapeDtypeStruct((1, LANES), F32),
                   jax.ShapeDtypeStruct((1, LANES), F32),
                   jax.ShapeDtypeStruct((N_SG_GROUPS, CHUNK, CHUNK), F32),
                   jax.ShapeDtypeStruct((CHUNK, N_SG_GROUPS), F32)],
        in_specs=[row(IN_WIDTH), latent_row(ATTN_WIDTH, 0),
                  pl.BlockSpec((2, 4, CHUNK, LANES), lambda i: (0, 0, i, 0)),
                  latent_row(SG_WIDTH, 1), row(LANES), row(LANES), whole((1, LANES)), whole((1, LANES)),
                  whole((LANES, LANES)), whole((N_SG_GROUPS, CHUNK, CHUNK)),
                  whole((N_SG_GROUPS, CHUNK, CHUNK)), whole((CHUNK, N_SG_GROUPS))],
        out_specs=[row(IN_WIDTH), whole((1, LANES)), whole((1, LANES)),
                   whole((N_SG_GROUPS, CHUNK, CHUNK)), whole((CHUNK, N_SG_GROUPS))],
        compiler_params=pltpu.CompilerParams(dimension_semantics=("arbitrary",)),
    )(p, dq, f, dao, cos, sin, qg, kg, bd, w_sp, w_spt, b_spt)


def _attn_fwd(q, kpad, vpad, ao, ctx_rows, name, tq=256):
    TT = q.shape[0]
    T = TT - ctx_rows
    tq = _tile(T, tq)
    off = ctx_rows // tq
    group = 2 * LANES

    def body(q_ref, k_ref, v_ref, ao_in, o_ref, lse_ref):
        del ao_in
        lane = lax.broadcasted_iota(jnp.int32, (tq, LANES), 1)
        lse = jnp.zeros((tq, LANES), F32)
        for a in range(2):
            acc = jnp.zeros((tq, LANES), F32)
            qa = q_ref[:, a * LANES:(a + 1) * LANES]
            for b in range(2):
                s = _dot(qa, k_ref[0, b], NT)
                m = jnp.max(s, axis=-1, keepdims=True)
                e = jnp.exp(s - m)
                l = jnp.sum(e, axis=-1, keepdims=True)
                acc = acc + _dot(e.astype(BF16), v_ref[0, b], NN) * (1.0 / l)
                lse = jnp.where(lane == 2 * a + b, m + jnp.log(l), lse)
            o_ref[:, a * LANES:(a + 1) * LANES] = acc.astype(BF16)
        lse_ref[0] = lse

    kv_spec = pl.BlockSpec((1, 2, TT, LANES), lambda j, i: (j, 0, 0, 0))
    return pl.pallas_call(
        body, name=name, grid=(2, T // tq),
        out_shape=[jax.ShapeDtypeStruct(ao.shape, BF16), jax.ShapeDtypeStruct((2, T, LANES), F32)],
        in_specs=[pl.BlockSpec((tq, group), lambda j, i: (i + off, j)), kv_spec, kv_spec,
                  pl.BlockSpec(memory_space=pl.ANY)],
        out_specs=[pl.BlockSpec((tq, group), lambda j, i: (i, j)),
                   pl.BlockSpec((1, tq, LANES), lambda j, i: (j, i, 0))],
        input_output_aliases={3: 0},
        compiler_params=pltpu.CompilerParams(dimension_semantics=("parallel", "parallel")),
    )(q, kpad, vpad, ao)


def _attn_bwd(q, dao, ao, lse, kpad, vpad, kt, ctx_rows, name, tq=256):
    TT = q.shape[0]
    T = TT - ctx_rows
    tq = _tile(T, tq)
    off = ctx_rows // tq
    group = 2 * LANES

    def body(q_ref, do_ref, o_ref, lse_ref, k_ref, v_ref, kt_ref, dq_ref, f_ref):
        i = pl.program_id(1)

        @pl.when(i == 0)
        def _():
            f_ref[...] = jnp.zeros_like(f_ref)

        ktv = kt_ref[0]
        lse_t = lse_ref[0].T
        row = lax.broadcasted_iota(jnp.int32, (SUBLANES, LANES), 0)
        lane = lax.broadcasted_iota(jnp.int32, (SUBLANES, LANES), 1)
        half_ones = (jnp.where(lane < HEAD_DIM, 0, 1) == row).astype(BF16)
        for a in range(2):
            cols = slice(a * LANES, (a + 1) * LANES)
            qa = q_ref[:, cols]
            do32 = do_ref[:, cols].astype(F32)
            doa = do32.astype(BF16)
            hi, lo = _split_bf16(do32 * o_ref[:, cols].astype(F32))
            deltas = _dot(half_ones, hi, NT) + _dot(half_ones, lo, NT)
            halves = []
            for b in range(2):
                h = 2 * a + b
                st = _dot(k_ref[0, b], qa, NT)
                pt = jnp.exp(st - lse_t[h:h + 1, :])
                dpt = _dot(v_ref[0, b], doa, NT)
                dst = (pt * (dpt - deltas[b:b + 1, :])).astype(BF16)
                f_ref[0, b] += _dot(dst, qa, NN)
                f_ref[0, 2 + b] += _dot(pt.astype(BF16), doa, NN)
                halves.append(_dot(ktv, dst, NN))
            dq_ref[:, cols] = jnp.concatenate(halves, axis=0).T

    kv_spec = pl.BlockSpec((1, 2, TT, LANES), lambda j, i: (j, 0, 0, 0))
    out_cols = pl.BlockSpec((tq, group), lambda j, i: (i, j))
    return pl.pallas_call(
        body, name=name, grid=(2, T // tq),
        out_shape=[jax.ShapeDtypeStruct((T, ATTN_WIDTH), F32), jax.ShapeDtypeStruct((2, 4, TT, LANES), F32)],
        in_specs=[pl.BlockSpec((tq, group), lambda j, i: (i + off, j)), out_cols, out_cols,
                  pl.BlockSpec((1, tq, LANES), lambda j, i: (j, i, 0)),
                  kv_spec, kv_spec, pl.BlockSpec((1, HEAD_DIM, TT), lambda j, i: (j, 0, 0))],
        out_specs=[out_cols, pl.BlockSpec((1, 4, TT, LANES), lambda j, i: (j, 0, 0, 0))],
        compiler_params=pltpu.CompilerParams(dimension_semantics=("parallel", "arbitrary")),
    )(q, dao, ao, lse, kpad, vpad, kt)


def _final_fwd_bwd(h, g, target, y, gt, name):
    R, Dm = h.shape
    tm = _tile(R, ROW_BLOCK, 8)

    def body(h_ref, g_ref, t_ref, y_ref, gt_ref, dh_ref, loss_ref, dg_ref, dy_ref, dgt_ref, dsum_ref):
        i = pl.program_id(0)

        @pl.when(i == 0)
        def _():
            for ref in (loss_ref, dg_ref, dgt_ref, dsum_ref):
                ref[...] = jnp.zeros_like(ref)

        hv = h_ref[...]
        r = lax.rsqrt(jnp.mean(hv * hv, axis=-1, keepdims=True) + EPS)
        n = hv * r
        diff = n * g_ref[...] - t_ref[...]
        loss_ref[...] += jnp.sum(diff * diff)
        dout = diff * (1.0 / Dm)
        dg_ref[...] += jnp.sum(dout * n, axis=0, keepdims=True)
        dn = dout * g_ref[...]
        dh = r * (dn - n * jnp.mean(dn * n, axis=-1, keepdims=True))
        dh_ref[...] = dh
        _gate_grads(dh, y_ref, gt_ref, dy_ref, dgt_ref, dsum_ref)

    vec = jax.ShapeDtypeStruct((1, Dm), F32)
    return pl.pallas_call(
        body, name=name, grid=(R // tm,),
        out_shape=[jax.ShapeDtypeStruct((R, Dm), F32), jax.ShapeDtypeStruct((1, LANES), F32), vec,
                   jax.ShapeDtypeStruct((R, Dm), BF16), vec, vec],
        in_specs=[_row_spec(tm, Dm), _vec_spec(Dm), _row_spec(tm, Dm), _row_spec(tm, Dm), _vec_spec(Dm)],
        out_specs=[_row_spec(tm, Dm), _vec_spec(LANES), _vec_spec(Dm), _row_spec(tm, Dm), _vec_spec(Dm),
                   _vec_spec(Dm)],
        compiler_params=pltpu.CompilerParams(dimension_semantics=("arbitrary",)),
    )(h, g, target, y, gt)


MOD_ROWS = 16


def _mod_fwd(c_rows, w_mod, name):
    L, Dm, n = w_mod.shape

    def body(c_ref, w_ref, o_ref):
        o_ref[0] = _dot3(_silu(c_ref[...]), w_ref[0], NN)

    return pl.pallas_call(
        body, name=name, grid=(L,),
        out_shape=jax.ShapeDtypeStruct((L, MOD_ROWS, n), F32),
        in_specs=[pl.BlockSpec((MOD_ROWS, Dm), lambda l: (0, 0)), pl.BlockSpec((1, Dm, n), lambda l: (l, 0, 0))],
        out_specs=pl.BlockSpec((1, MOD_ROWS, n), lambda l: (l, 0, 0)),
        compiler_params=pltpu.CompilerParams(dimension_semantics=("parallel",)),
    )(c_rows, w_mod)


def _mod_bwd(c_rows_t, dmod, w_mod, name):
    L, Dm, n = w_mod.shape

    def body(ct_ref, d_ref, w_ref, gw_ref, ds_ref):
        dm = d_ref[0]
        gw_ref[0] = _dot3(_silu(ct_ref[...]), dm, NN)
        ds_ref[0] = _dot3(dm[:MOD_ROWS], w_ref[0], NT)

    return pl.pallas_call(
        body, name=name, grid=(L,),
        out_shape=[jax.ShapeDtypeStruct((L, Dm, n), F32), jax.ShapeDtypeStruct((L, MOD_ROWS, Dm), F32)],
        in_specs=[pl.BlockSpec((Dm, LANES), lambda l: (0, 0)), pl.BlockSpec((1, LANES, n), lambda l: (l, 0, 0)),
                  pl.BlockSpec((1, Dm, n), lambda l: (l, 0, 0))],
        out_specs=[pl.BlockSpec((1, Dm, n), lambda l: (l, 0, 0)),
                   pl.BlockSpec((1, MOD_ROWS, Dm), lambda l: (l, 0, 0))],
        compiler_params=pltpu.CompilerParams(dimension_semantics=("parallel",)),
    )(c_rows_t, dmod, w_mod)


def _adam_update(w, g, m, v):
    c1 = 1.0 - ADAM_B1 ** ADAM_STEP
    c2 = 1.0 - ADAM_B2 ** ADAM_STEP
    mn = ADAM_B1 * m + (1.0 - ADAM_B1) * g
    vn = ADAM_B2 * v + (1.0 - ADAM_B2) * (g * g)
    return -ADAM_LR * ((mn / c1) / (jnp.sqrt(vn / c2) + ADAM_EPS) + ADAM_WD * w), mn, vn


def _adamw(w, g, m, v, name):
    R, Cw = w.shape
    tm = _tile(R, ROW_BLOCK, 8)

    def body(w_ref, g_ref, m_ref, v_ref, d_ref, mo_ref, vo_ref):
        d_ref[...], mo_ref[...], vo_ref[...] = _adam_update(w_ref[...], g_ref[...], m_ref[...], v_ref[...])

    spec = pl.BlockSpec((tm, Cw), lambda i: (i, 0))
    return pl.pallas_call(
        body, name=name, grid=(R // tm,),
        out_shape=[jax.ShapeDtypeStruct((R, Cw), F32)] * 3,
        in_specs=[spec] * 4, out_specs=[spec] * 3,
        compiler_params=pltpu.CompilerParams(dimension_semantics=("parallel",)),
    )(w, g, m, v)


def _adamw_recv(w, m, v, recvs, name):
    L, R, n = w.shape
    tm = _tile(R, ROW_BLOCK, 8)
    nblk = R // tm
    parts = [r.reshape(N_DEV, R, n) for r in recvs]

    def body(*refs):
        w_ref, m_ref, v_ref = refs[:3]
        part_refs = refs[3:3 + L]
        g_ref, d_ref, mo_ref, vo_ref, gsum = refs[3 + L:]
        l = pl.program_id(0)
        for ll in range(L):
            @pl.when(l == ll)
            def _(ll=ll):
                acc = part_refs[ll][0].astype(F32)
                for s in range(1, N_DEV):
                    acc = acc + part_refs[ll][s].astype(F32)
                gsum[...] = acc
        g = gsum[...]
        g_ref[0] = g
        d_ref[0], mo_ref[0], vo_ref[0] = _adam_update(w_ref[0], g, m_ref[0], v_ref[0])

    def part_spec(ll):
        return pl.BlockSpec((N_DEV, tm, n), lambda l, i: (0, jnp.where(l == ll, i, jnp.where(l < ll, 0, nblk - 1)), 0))

    spec = pl.BlockSpec((1, tm, n), lambda l, i: (l, i, 0))
    return pl.pallas_call(
        body, name=name, grid=(L, nblk),
        out_shape=[jax.ShapeDtypeStruct((L, R, n), F32)] * 4,
        in_specs=[spec] * 3 + [part_spec(ll) for ll in range(L)], out_specs=[spec] * 4,
        scratch_shapes=[pltpu.VMEM((tm, n), F32)],
        compiler_params=pltpu.CompilerParams(dimension_semantics=("parallel", "parallel")),
    )(w, m, v, *parts)


def _pack(parts, row_mult=8):
    flat, offs, pos = [], [], 0
    for t in parts:
        t = t.reshape(-1).astype(F32)
        size = -(-t.shape[0] // LANES) * LANES
        flat.append(jnp.pad(t, (0, size - t.shape[0])))
        offs.append(pos)
        pos += size
    total = -(-pos // (LANES * row_mult)) * (LANES * row_mult)
    if total > pos:
        flat.append(jnp.zeros((total - pos,), F32))
    return jnp.concatenate(flat).reshape(-1, LANES), offs


def _take(buf, off, shape):
    size = math.prod(shape)
    return buf[..., off:off + size].reshape(buf.shape[:-1] + tuple(shape))


def _rope_tables(T, ctx_rows):
    pos = jnp.arange(T)
    row = (pos // GRID_W).astype(F32)
    col = (pos % GRID_W).astype(F32)
    half = HEAD_DIM // 4
    inv = ROPE_THETA ** (-jnp.arange(0, 2 * half, 2, dtype=F32) / (2 * half))
    ang_r, ang_c = row[:, None] * inv[None, :], col[:, None] * inv[None, :]
    cos = jnp.concatenate([jnp.cos(ang_r)] * 2 + [jnp.cos(ang_c)] * 2, axis=1)
    sin = jnp.concatenate([-jnp.sin(ang_r), jnp.sin(ang_r), -jnp.sin(ang_c), jnp.sin(ang_c)], axis=1)
    cos = jnp.concatenate([jnp.ones((ctx_rows, HEAD_DIM), F32), cos], axis=0)
    sin = jnp.concatenate([jnp.zeros((ctx_rows, HEAD_DIM), F32), sin], axis=0)
    return jnp.tile(cos, (1, 2)), jnp.tile(sin, (1, 2))


def kernel(x, c, ctx, c_ctx, w_mod, b_mod, g_mix, g_ffn, w_ffn_in, w_ffn_out, w_in, q_gain, k_gain, w_sp, b_sp, w_out, w_pw1, b_pw1, w_dw, b_dw, ln_g, ln_b, w_pw2, b_pw2, g_final, loss_target, m_c_ctx, m_w_mod, m_b_mod, m_g_mix, m_g_ffn, m_w_ffn_in, m_w_ffn_out, m_w_in, m_q_gain, m_k_gain, m_w_sp, m_b_sp, m_w_out, m_w_pw1, m_b_pw1, m_w_dw, m_b_dw, m_ln_g, m_ln_b, m_w_pw2, m_b_pw2, m_g_final, v_c_ctx, v_w_mod, v_b_mod, v_g_mix, v_g_ffn, v_w_ffn_in, v_w_ffn_out, v_w_in, v_q_gain, v_k_gain, v_w_sp, v_b_sp, v_w_out, v_w_pw1, v_b_pw1, v_w_dw, v_b_dw, v_ln_g, v_ln_b, v_w_pw2, v_b_pw2, v_g_final):
    weights = dict(c_ctx=c_ctx, w_mod=w_mod, b_mod=b_mod, g_mix=g_mix, g_ffn=g_ffn, w_ffn_in=w_ffn_in,
                   w_ffn_out=w_ffn_out, w_in=w_in, q_gain=q_gain, k_gain=k_gain, w_sp=w_sp, b_sp=b_sp,
                   w_out=w_out, w_pw1=w_pw1, b_pw1=b_pw1, w_dw=w_dw, b_dw=b_dw, ln_g=ln_g, ln_b=ln_b,
                   w_pw2=w_pw2, b_pw2=b_pw2, g_final=g_final)
    moments_m = dict(c_ctx=m_c_ctx, w_mod=m_w_mod, b_mod=m_b_mod, g_mix=m_g_mix, g_ffn=m_g_ffn,
                     w_ffn_in=m_w_ffn_in, w_ffn_out=m_w_ffn_out, w_in=m_w_in, q_gain=m_q_gain,
                     k_gain=m_k_gain, w_sp=m_w_sp, b_sp=m_b_sp, w_out=m_w_out, w_pw1=m_w_pw1,
                     b_pw1=m_b_pw1, w_dw=m_w_dw, b_dw=m_b_dw, ln_g=m_ln_g, ln_b=m_ln_b, w_pw2=m_w_pw2,
                     b_pw2=m_b_pw2, g_final=m_g_final)
    moments_v = dict(c_ctx=v_c_ctx, w_mod=v_w_mod, b_mod=v_b_mod, g_mix=v_g_mix, g_ffn=v_g_ffn,
                     w_ffn_in=v_w_ffn_in, w_ffn_out=v_w_ffn_out, w_in=v_w_in, q_gain=v_q_gain,
                     k_gain=v_k_gain, w_sp=v_w_sp, b_sp=v_b_sp, w_out=v_w_out, w_pw1=v_w_pw1,
                     b_pw1=v_b_pw1, w_dw=v_w_dw, b_dw=v_b_dw, ln_g=v_ln_g, ln_b=v_ln_b, w_pw2=v_w_pw2,
                     b_pw2=v_b_pw2, g_final=v_g_final)
    names = list(weights)

    T, C = x.shape[1], ctx.shape[1]
    Dm = D_MODEL
    me = 4 * lax.axis_index("x") + 2 * lax.axis_index("y") + lax.axis_index("c")
    h0 = x[0]
    ctx2 = ctx[0]
    target = loss_target[0]

    small_sharded = (("w_dw", w_dw[0]), ("b_pw1", b_pw1), ("b_dw", b_dw), ("ln_g", ln_g), ("ln_b", ln_b),
                     ("b_pw2", b_pw2))
    buf1, offs1 = _pack([c] + [t for _, t in small_sharded])
    w_in_t, m_w_in_t, v_w_in_t = (jnp.swapaxes(t, 1, 2) for t in (w_in, m_w_in, v_w_in))
    w_ffi_t, m_w_ffi_t, v_w_ffi_t = (jnp.swapaxes(t, 1, 2) for t in (w_ffn_in, m_w_ffn_in, v_w_ffn_in))
    got1, W_in_t, W_out = _all_gather([buf1, w_in_t[0].astype(BF16), w_out[0].astype(BF16)], "gather_cond", False)
    got1 = got1.reshape(N_DEV, -1)
    c_all = _take(got1, offs1[0], (Dm,))
    full_small = {}
    for (nm, t), off in zip(small_sharded, offs1[1:]):
        seg = _take(got1, off, t.shape)
        full_small[nm] = jnp.moveaxis(seg, 0, -2).reshape(t.shape[:-1] + (N_DEV * t.shape[-1],))
    w_dw_f, b_pw1_f = full_small["w_dw"], full_small["b_pw1"]
    b_dw_f, ln_g_f, ln_b_f, b_pw2_f = (full_small[k] for k in ("b_dw", "ln_g", "ln_b", "b_pw2"))

    c_rows = jnp.concatenate([c_all, c_ctx[None, :], jnp.zeros((MOD_ROWS - N_DEV - 1, Dm), F32)], axis=0)
    mod_part = _mod_fwd(c_rows, w_mod, "mod_fwd")
    n_mod = w_mod.shape[2]
    got2 = _all_gather([mod_part.reshape(-1, LANES)], "gather_mod", True)[0]
    mod_all = got2.reshape(N_DEV, 2, MOD_ROWS, n_mod).transpose(1, 2, 0, 3).reshape(2, MOD_ROWS, N_DEV * n_mod)
    mod_all = mod_all + b_mod[:, None, :]
    my_mod = lax.dynamic_index_in_dim(mod_all, me, axis=1, keepdims=False)
    sh1, sc1, gt1, sh2, sc2, gt2 = ([my_mod[l:l + 1, k * Dm:(k + 1) * Dm] for l in range(2)] for k in range(6))
    csh1 = mod_all[0, N_DEV:N_DEV + 1, 0:Dm]
    csc1 = mod_all[0, N_DEV:N_DEV + 1, Dm:2 * Dm]

    behind = got2[0:1, 0:1] * 0.0
    gather_groups = [[w_ffi_t[0], w_ffn_out[0]], [w_pw1[0], w_pw2[0]], [w_ffi_t[1], w_ffn_out[1]]]
    gathers = [_push_begin([(t + behind).astype(BF16) for t in grp], True, f"gather_start{k}")
               for k, grp in enumerate(gather_groups)]
    started = sum(h[4][0:1, 0:1] for h in gathers)

    def gathered(k, after):
        return _push_end(gathers[k], after, f"gather_wait{k}")[1]

    def ffn_weights(k, after):
        wi, wo = gathered(k, after)
        return wi.reshape(N_DEV, FF_SHARD, Dm), wo.reshape(N_DEV // 2, FF_SHARD, Dm)

    def col_gathered(t, n):
        return t.reshape(N_DEV, Dm, n).transpose(1, 0, 2).reshape(Dm, N_DEV * n)

    W_ffi, W_ffo = [None, None], [None, None]

    g_mix_r = [g_mix[l:l + 1] for l in range(2)]
    g_ffn_r = [g_ffn[l:l + 1] for l in range(2)]
    g_fin = g_final[None, :]

    cos, sin = _rope_tables(T, C)
    qg = jnp.tile(q_gain, (1, 2))
    kg = jnp.tile(k_gain, (1, 2))
    lane_head = jnp.arange(LANES) // HEAD_DIM
    bd = (lane_head[:, None] == lane_head[None, :]).astype(BF16)
    w_sp0 = w_sp[0]
    w_spt0 = w_sp0.transpose(0, 2, 1)
    b_spt0 = b_sp[0].T

    XM = _norm_mod_fwd_cat(ctx2, h0, g_mix_r[0], csc1, csh1, sc1[0] + started, sh1[0], "norm_mix0")
    P = _mm(XM, W_in_t, "nt", "in_proj", tn=IN_WIDTH)
    qh, kpad, vpad, kt, ao = _mix_prep_fwd(P, C, cos, sin, qg, kg, bd, w_sp0, b_spt0, "mix_prep")
    ao, lse = _attn_fwd(qh, kpad, vpad, ao, C, "attn_fwd")
    h1, y0, xf0 = _mm(ao, W_out, "nn", "out_proj", res=h0, gate=gt1[0], raw_out=True,
                      norm=(g_ffn_r[0], sc2[0], sh2[0]))

    def ffn_fwd(h_in, xf, l, norm_next):
        W_ffi[l], W_ffo[l] = ffn_weights(2 * l, xf)
        gu, act = _ffn_in_swiglu(xf, W_ffi[l], f"ffn_in{l}")
        outs = _mm_sum_shards(act, W_ffo[l], "nn", f"ffn_out{l}", res=h_in, gate=gt2[l], raw_out=True,
                              norm=norm_next)
        return tuple(outs) + (None,) * (3 - len(outs)) + (gu, act)

    h2, f0, xm1, gu0, act0 = ffn_fwd(h1, xf0, 0, (g_mix_r[1], sc1[1], sh1[1]))

    W_pw1, W_pw2 = gathered(1, xm1)
    W_pw1 = col_gathered(W_pw1, 2 * Dm // N_DEV)
    ag = _mm(xm1, W_pw1, "nn", "pw1", BF16, bias=b_pw1_f)
    hg = _glu_fwd(ag, "glu")
    hd = _conv_fwd(hg, w_dw_f, b_dw_f, "conv")
    hs = _ln_silu_fwd(hd, ln_g_f, ln_b_f, "ln_silu")
    h3, y1, xf1 = _mm(hs, W_pw2, "nn", "pw2", bias=b_pw2_f, res=h2, gate=gt1[1], raw_out=True,
                      norm=(g_ffn_r[1], sc2[1], sh2[1]))
    h4, f1, _, gu1, act1 = ffn_fwd(h3, xf1, 1, None)

    dh4, sq_err, dg_final, df1, dgt2_1, _ = _final_fwd_bwd(h4, g_fin, target, f1, gt2[1], "loss_head")
    loss_local = (0.5 / Dm) * sq_err[0, 0:1]

    def col_shards(g, n):
        return g.reshape(Dm, N_DEV, n).transpose(1, 0, 2).reshape(N_DEV * Dm, n)

    def exchange_begin(k, parts):
        return _push_begin(parts, False, f"exchange_start{k}")

    def zero_of(handle):
        return handle[4][0:1, 0:1]

    def ffn_bwd(df, xf, gu, act, l):
        dw_out = _mm_tn_shard_rows(act, df, f"ffn_out_dw{l}", BF16)
        dgu = _ffn_out_dx_swiglu(df, W_ffo[l], gu, f"ffn_out_dx{l}").reshape(N_DEV, T, FF_SHARD)
        dw_in = _mm_tn_shard_rows(dgu, xf, f"ffn_in_dw{l}", BF16)
        dxf = _mm_sum_shards(dgu, W_ffi[l], "nn", f"ffn_in_dx{l}", BF16, tm=256)
        return dw_in, dw_out, dxf

    dW_ffi1, dW_ffo1, dxf1 = ffn_bwd(df1, xf1, gu1, act1, 1)
    ex0 = exchange_begin(0, [dW_ffi1.reshape(2 * D_FF, Dm), dW_ffo1.reshape(D_FF, Dm)])
    dh3, da, dsh, dy1, dgt1_1, db_pw2 = _norm_mod_bwd(h3, g_ffn_r[1], sc2[1], dxf1, dh4, "norm_ffn_bwd1",
                                                       gate=(y1, gt1[1] + zero_of(ex0)))
    dmod_ffn1 = (dsh, da * g_ffn_r[1], dgt2_1)
    dg_ffn1 = da * (1.0 + sc2[1])

    dW_pw2 = _mm(hs, dy1, "tn", "pw2_dw", BF16, tk=2048)
    dhs = _mm(dy1, W_pw2, "nt", "pw2_dx", BF16)
    dhd, dln_g, dln_b, db_dw = _ln_silu_bwd(dhs, hd, ln_g_f, ln_b_f, "ln_silu_bwd")
    dhg, dw_dw = _conv_bwd(dhd, hg, w_dw_f, "conv_bwd")
    dag, db_pw1 = _glu_bwd(ag, dhg, "glu_bwd")
    dW_pw1 = _mm(xm1, dag, "tn", "pw1_dw", BF16, tk=2048)
    dxm1 = _mm(dag, W_pw1, "nt", "pw1_dx", BF16, tk=2048)
    ex1 = exchange_begin(1, [col_shards(dW_pw1, 2 * Dm // N_DEV), dW_pw2])
    dh2, da, dsh, df0, dgt2_0, _ = _norm_mod_bwd(h2, g_mix_r[1], sc1[1], dxm1, dh3, "norm_mix1_bwd",
                                                 gate=(f0, gt2[0] + zero_of(ex1)))
    dmod_mix1 = (dsh, da * g_mix_r[1], dgt1_1)
    dg_mix1 = da * (1.0 + sc1[1])

    dW_ffi0, dW_ffo0, dxf0 = ffn_bwd(df0, xf0, gu0, act0, 0)
    ex2 = exchange_begin(2, [dW_ffi0.reshape(2 * D_FF, Dm), dW_ffo0.reshape(D_FF, Dm)])
    dh1, da, dsh, dy0, dgt1_0, _ = _norm_mod_bwd(h1, g_ffn_r[0], sc2[0], dxf0, dh2, "norm_ffn_bwd0",
                                                 gate=(y0, gt1[0] + zero_of(ex2)))
    dmod_ffn0 = (dsh, da * g_ffn_r[0], dgt2_0)
    dg_ffn0 = da * (1.0 + sc2[0])

    dW_out = _mm(ao, dy0, "tn", "out_proj_dw", BF16, tk=2048)
    dao = _mm(dy0, W_out, "nt", "out_proj_dx", BF16)
    dq, f_acc = _attn_bwd(qh, dao, ao, lse, kpad, vpad, kt, C, "attn_bwd")
    dP, dqg, dkg, dw_sp0, db_spt0 = _mix_prep_bwd(P, dq, f_acc, dao, C, cos, sin, qg, kg, bd, w_sp0, w_spt0,
                                                  b_spt0, "mix_prep_bwd")
    dW_in_t = _mm(dP, XM, "tn", "in_proj_dw", BF16, tm=896, tk=2176)
    dXM = _mm(dP, W_in_t, "nn", "in_proj_dx", BF16, tk=IN_WIDTH)
    dh0, da, dsh = _norm_mod_bwd(h0, g_mix_r[0], sc1[0], dXM, dh1, "norm_mix0_bwd", dxm_row_off=C)
    _, dac, dcsh = _norm_mod_bwd(ctx2, g_mix_r[0], csc1, dXM, None, "norm_ctx_bwd")
    dmod_mix0 = (dsh, da * g_mix_r[0], dgt1_0)
    dg_mix0 = da * (1.0 + sc1[0]) + dac * (1.0 + csc1)
    dcmod = jnp.concatenate([dcsh, dac * g_mix_r[0]], axis=1)

    dmod_mine = jnp.stack([jnp.concatenate(dmod_mix0 + dmod_ffn0, axis=1)[0],
                           jnp.concatenate(dmod_mix1 + dmod_ffn1, axis=1)[0]])

    small_grads = [
        ("loss", loss_local), ("g_final", dg_final), ("g_mix", jnp.concatenate([dg_mix0, dg_mix1])),
        ("g_ffn", jnp.concatenate([dg_ffn0, dg_ffn1])),
        ("q_gain", dqg[:, :HEAD_DIM] + dqg[:, HEAD_DIM:]), ("k_gain", dkg[:, :HEAD_DIM] + dkg[:, HEAD_DIM:]),
        ("w_sp", dw_sp0[None]), ("b_sp", db_spt0.T[None]), ("b_pw1", db_pw1), ("w_dw", dw_dw[None]),
        ("b_dw", db_dw), ("ln_g", dln_g), ("ln_b", dln_b), ("b_pw2", db_pw2), ("dcmod", dcmod),
        ("dmod", dmod_mine),
    ]
    buf3, offs3 = _pack([t for _, t in small_grads])
    got3 = _all_gather([buf3], "gather_small_grads", True)[0].reshape(N_DEV, buf3.shape[0], LANES)
    sum3 = _sum_devices(got3, "sum_small_grads").reshape(-1)
    off3 = {nm: off for (nm, _), off in zip(small_grads, offs3)}
    shape3 = {nm: t.shape for nm, t in small_grads}

    def summed(nm):
        return _take(sum3, off3[nm], shape3[nm])

    loss = summed("loss")[0]
    dcmod_sum = summed("dcmod")
    dmod_rows = _take(got3.reshape(N_DEV, -1), off3["dmod"], (2, 6 * Dm)).transpose(1, 0, 2)
    ctx_row = jnp.concatenate([jnp.pad(dcmod_sum, ((0, 0), (0, 4 * Dm))), jnp.zeros((1, 6 * Dm), F32)])
    dmod_all = jnp.concatenate([dmod_rows, ctx_row[:, None, :],
                                jnp.zeros((2, LANES - N_DEV - 1, 6 * Dm), F32)], axis=1)
    grads = {}
    grads["b_mod"] = summed("dmod") + ctx_row
    dmod_shard = lax.dynamic_slice_in_dim(dmod_all, me * n_mod, n_mod, axis=2)
    c_rows_t = jnp.pad(c_rows.T, ((0, 0), (0, LANES - MOD_ROWS)))
    grads["w_mod"], ds_part = _mod_bwd(c_rows_t, dmod_shard, w_mod, "mod_bwd")

    buf4, _ = _pack([ds_part[0, N_DEV]])
    got4 = _all_gather([buf4], "gather_c_ctx_grad", True)[0].reshape(N_DEV, buf4.shape[0], LANES)
    ds_ctx = _sum_devices(got4, "sum_c_ctx_grad").reshape(-1)[:Dm]
    behind_small = (ds_ctx[0:1] * 0.0).astype(BF16)
    ex3 = exchange_begin(3, [dW_in_t, dW_out + behind_small])
    grads["c_ctx"] = ds_ctx * _dsilu(c_ctx) + zero_of(ex3)[0]

    for nm in ("g_final", "g_mix", "g_ffn", "q_gain", "k_gain", "w_sp", "b_sp"):
        grads[nm] = summed(nm).reshape(weights[nm].shape)
    for nm in ("b_pw1", "w_dw", "b_dw", "ln_g", "ln_b", "b_pw2"):
        n_loc = weights[nm].shape[-1]
        grads[nm] = lax.dynamic_slice_in_dim(summed(nm), me * n_loc, n_loc, axis=-1).reshape(weights[nm].shape)

    delta, new_m, new_v = {}, {}, {}
    shp = w_mod.shape
    outs = _adamw(w_mod.reshape(-1, shp[-1]), grads["w_mod"].reshape(-1, shp[-1]),
                  m_w_mod.reshape(-1, shp[-1]), v_w_mod.reshape(-1, shp[-1]), "adamw_w_mod")
    delta["w_mod"], new_m["w_mod"], new_v["w_mod"] = (o.reshape(shp) for o in outs)
    big_names = ("w_mod", "w_ffn_in", "w_ffn_out", "w_in", "w_out", "w_pw1", "w_pw2")
    small_names = [nm for nm in names if nm not in big_names]
    packs = [_pack([src[nm] for nm in small_names]) for src in (weights, grads, moments_m, moments_v)]
    offs_s = packs[0][1]
    outs = _adamw(*[pk[0] for pk in packs], "adamw_small")
    for o, dst in zip(outs, (delta, new_m, new_v)):
        o = o.reshape(-1)
        for nm, off in zip(small_names, offs_s):
            dst[nm] = _take(o, off, weights[nm].shape)

    def exchanged(k, handle, after):
        return _push_end(handle, after, f"exchange_wait{k}")[1]

    def adamw_big(nm, parts, transposed=False, wmv=None):
        w3, m3, v3 = wmv if wmv is not None else (weights[nm], moments_m[nm], moments_v[nm])
        outs4 = _adamw_recv(w3, m3, v3, parts, f"adamw_{nm}")
        if transposed:
            outs4 = [jnp.swapaxes(t, 1, 2) for t in outs4]
        grads[nm], delta[nm], new_m[nm], new_v[nm] = outs4

    r_ffi1, r_ffo1 = exchanged(0, ex0, outs[0])
    r_pw1, r_pw2 = exchanged(1, ex1, outs[0])
    r_ffi0, r_ffo0 = exchanged(2, ex2, outs[0])
    adamw_big("w_ffn_in", [r_ffi0, r_ffi1], True, (w_ffi_t, m_w_ffi_t, v_w_ffi_t))
    adamw_big("w_ffn_out", [r_ffo0, r_ffo1])
    adamw_big("w_pw1", [r_pw1])
    adamw_big("w_pw2", [r_pw2])
    r_in, r_out = exchanged(3, ex3, delta["w_pw2"])
    adamw_big("w_in", [r_in], True, (w_in_t, m_w_in_t, v_w_in_t))
    adamw_big("w_out", [r_out])

    return (loss, dh0[None], *[grads[n] for n in names], *[delta[n] for n in names],
            *[new_m[n] for n in names], *[new_v[n] for n in names])
```

```python
import math

import jax
import jax.numpy as jnp
from jax import lax
from jax.experimental import pallas as pl
from jax.experimental.pallas import tpu as pltpu

F32 = jnp.float32
BF16 = jnp.bfloat16
MESH = pl.DeviceIdType.MESH

N_DEV = 8
D_MODEL = 1024
EPS = 1e-6
HEAD_DIM = 64
ATTN_WIDTH = 512
KV_WIDTH = 128
SG_WIDTH = 512
N_SG_GROUPS = 4
CHUNK = 128
IN_WIDTH = 1792
D_FF = 2816
FF_SHARD = 2 * D_FF // N_DEV
CONV_WIDTH = 31
CONV_HALO = 16
GRID_W = 64
ROPE_THETA = 10000.0
LANES = 128
SUBLANES = 8
ROW_BLOCK = 256
ADAM_LR, ADAM_B1, ADAM_B2, ADAM_EPS, ADAM_WD, ADAM_STEP = 0.001, 0.9, 0.999, 1e-08, 0.01, 10


def _tile(n, target, mult=LANES):
    best = None
    for t in range(mult, min(n, target) + 1, mult):
        if n % t == 0:
            best = t
    return best if best is not None else n


def _sigmoid(x):
    return 1.0 / (1.0 + jnp.exp(-x))


def _silu(x):
    return x * _sigmoid(x)


def _dsilu(x):
    s = _sigmoid(x)
    return s * (1.0 + x * (1.0 - s))


_GELU_K = math.sqrt(2.0 / math.pi)


def _gelu(x):
    return 0.5 * x * (1.0 + jnp.tanh(_GELU_K * (x + 0.044715 * x * x * x)))


def _gelu_and_grad(x):
    x2 = x * x
    t = jnp.tanh(_GELU_K * x * (1.0 + 0.044715 * x2))
    half = 0.5 * (1.0 + t)
    return x * half, half + 0.5 * x * (1.0 - t * t) * _GELU_K * (1.0 + 3.0 * 0.044715 * x2)


def _split_bf16(x):
    hi = x.astype(BF16)
    lo = (x - hi.astype(F32)).astype(BF16)
    return hi, lo


def _dot(a, b, dims):
    return lax.dot_general(a, b, (dims, ((), ())), preferred_element_type=F32)


def _dot3(a, b, dims):
    ah, al = _split_bf16(a)
    bh, bl = _split_bf16(b)
    return _dot(ah, bh, dims) + _dot(ah, bl, dims) + _dot(al, bh, dims)


NN = ((1,), (0,))
NT = ((1,), (1,))
TN = ((0,), (0,))


def _all_gather(xs, name, in_vmem):
    n_arr = len(xs)

    def body(*refs):
        x_refs, out_refs = refs[:n_arr], refs[n_arr:2 * n_arr]
        send_sems, recv_sems, local_sems = refs[2 * n_arr:]
        x, y, c = lax.axis_index("x"), lax.axis_index("y"), lax.axis_index("c")
        me, sibling = (x, y, c), (x, y, 1 - c)
        chips = [(1 - x, y), (x, 1 - y), (1 - x, 1 - y)]

        def rows(a, px, py, pc):
            m_per = xs[a].shape[0]
            return out_refs[a].at[pl.ds((4 * px + 2 * py + pc) * m_per, m_per), :]

        def copy(a, k, block, to, src=None):
            return pltpu.make_async_remote_copy(
                src_ref=rows(a, *block) if src is None else src,
                dst_ref=rows(a, *block),
                send_sem=send_sems.at[7 * a + k],
                recv_sem=recv_sems.at[7 * a + k],
                device_id=to,
                device_id_type=MESH,
            )

        mine, first, passed = [], [], []
        for a in range(n_arr):
            mine.append(pltpu.make_async_copy(x_refs[a], rows(a, *me), local_sems.at[a]))
            mine[-1].start()
            first.append(copy(a, 0, me, sibling, src=x_refs[a]))
            first += [copy(a, 1 + j, me, (*chip, c), src=x_refs[a]) for j, chip in enumerate(chips)]
        for cp in first:
            cp.start()
        for a in range(n_arr):
            for j, chip in enumerate(chips):
                copy(a, 1 + j, (*chip, c), me).wait_recv()
                passed.append(copy(a, 4 + j, (*chip, c), sibling))
                passed[-1].start()
        for a in range(n_arr):
            copy(a, 0, sibling, me).wait_recv()
            for j, chip in enumerate(chips):
                copy(a, 4 + j, (*chip, 1 - c), me).wait_recv()
        for cp in first + passed:
            cp.wait_send()
        for cp in mine:
            cp.wait()

    space = pltpu.VMEM if in_vmem else pl.ANY
    return pl.pallas_call(
        body,
        name=name,
        out_shape=[jax.ShapeDtypeStruct((N_DEV * t.shape[0], t.shape[1]), t.dtype) for t in xs],
        in_specs=[pl.BlockSpec(memory_space=space)] * n_arr,
        out_specs=[pl.BlockSpec(memory_space=space)] * n_arr,
        scratch_shapes=[
            pltpu.SemaphoreType.DMA((7 * n_arr,)),
            pltpu.SemaphoreType.DMA((7 * n_arr,)),
            pltpu.SemaphoreType.DMA((n_arr,)),
        ],
    )(*xs)


HBM_SPEC = pl.BlockSpec(memory_space=pltpu.HBM)
SEM_SPEC = pl.BlockSpec(memory_space=pltpu.SEMAPHORE)
DATAFLOW_EFFECT = pltpu.SideEffectType.DATAFLOW_SIDE_EFFECTING


def _peers(x, y, c):
    for k in range(1, N_DEV):
        px = 1 - x if (k >> 2) & 1 else x
        py = 1 - y if (k >> 1) & 1 else y
        pc = 1 - c if k & 1 else c
        yield k - 1, (px, py, pc), 4 * px + 2 * py + pc


def _push_copies(src_refs, land_refs, send_sems, recv_sems, shapes, whole_src):
    x, y, c = lax.axis_index("x"), lax.axis_index("y"), lax.axis_index("c")
    me = 4 * x + 2 * y + c
    for a, (m_per, _) in enumerate(shapes):
        def block(ref, idx, m_per=m_per):
            return ref.at[pl.ds(idx * m_per, m_per), :]

        for k, peer, pidx in _peers(x, y, c):
            src = src_refs[a] if whole_src else block(src_refs[a], pidx)
            sems = dict(send_sem=send_sems.at[N_DEV * a + k], recv_sem=recv_sems.at[N_DEV * a + k],
                        device_id=peer, device_id_type=MESH)
            yield (pltpu.make_async_remote_copy(src_ref=src, dst_ref=block(land_refs[a], me), **sems),
                   pltpu.make_async_remote_copy(src_ref=src, dst_ref=block(land_refs[a], pidx), **sems))


def _own_copies(src_refs, land_refs, recv_sems, shapes, whole_src):
    me = 4 * lax.axis_index("x") + 2 * lax.axis_index("y") + lax.axis_index("c")
    for a, (m_per, _) in enumerate(shapes):
        mine = pl.ds(me * m_per, m_per)
        src = src_refs[a] if whole_src else src_refs[a].at[mine, :]
        yield pltpu.make_async_copy(src, land_refs[a].at[mine, :], recv_sems.at[N_DEV * a + N_DEV - 1])


def _push_begin(srcs, whole_src, name):
    n_arr = len(srcs)
    shapes = [(t.shape[0] if whole_src else t.shape[0] // N_DEV, t.shape[1]) for t in srcs]
    lands = [lax.empty((N_DEV * m, n), t.dtype) for (m, n), t in zip(shapes, srcs)]

    def body(*refs):
        src_refs, land_refs = refs[:n_arr], refs[n_arr:2 * n_arr]
        send_sems, recv_sems = refs[2 * n_arr], refs[2 * n_arr + 1]
        token = refs[-1]
        for outgoing, _ in _push_copies(src_refs, land_refs, send_sems, recv_sems, shapes, whole_src):
            outgoing.start()
        for own in _own_copies(src_refs, land_refs, recv_sems, shapes, whole_src):
            own.start()
        token[...] = jnp.zeros_like(token)

    operands = [pltpu.with_memory_space_constraint(t, pltpu.HBM) for t in list(srcs) + lands]
    outs = pl.pallas_call(
        body, name=name,
        out_shape=(pltpu.SemaphoreType.DMA((N_DEV * n_arr,)), pltpu.SemaphoreType.DMA((N_DEV * n_arr,)),
                   *[pltpu.HBM(t.shape, t.dtype) for t in operands],
                   jax.ShapeDtypeStruct((SUBLANES, LANES), F32)),
        in_specs=[HBM_SPEC] * (2 * n_arr),
        out_specs=(SEM_SPEC, SEM_SPEC, *[HBM_SPEC] * (2 * n_arr), pl.BlockSpec(memory_space=pltpu.VMEM)),
        input_output_aliases={i: 2 + i for i in range(2 * n_arr)},
        compiler_params=pltpu.CompilerParams(has_side_effects=DATAFLOW_EFFECT),
    )(*operands)
    return outs[0], outs[1], list(outs[2:2 + n_arr]), list(outs[2 + n_arr:2 + 2 * n_arr]), outs[-1], whole_src


def _push_end(handle, after, name):
    send_sems, recv_sems, srcs, lands, _, whole_src = handle
    n_arr = len(srcs)
    shapes = [(t.shape[0] // N_DEV, t.shape[1]) for t in lands]

    def body(*refs):
        src_refs, land_refs = refs[:n_arr], refs[n_arr:2 * n_arr]
        send_sems_ref, recv_sems_ref = refs[2 * n_arr], refs[2 * n_arr + 1]
        for outgoing, incoming in _push_copies(src_refs, land_refs, send_sems_ref, recv_sems_ref, shapes, whole_src):
            outgoing.wait_send()
            incoming.wait_recv()
        for own in _own_copies(src_refs, land_refs, recv_sems_ref, shapes, whole_src):
            own.wait()

    outs = pl.pallas_call(
        body, name=name,
        out_shape=tuple(pltpu.HBM(t.shape, t.dtype) for t in srcs + lands),
        in_specs=[HBM_SPEC] * (2 * n_arr) + [SEM_SPEC, SEM_SPEC, pl.BlockSpec(memory_space=pl.ANY)],
        out_specs=tuple([HBM_SPEC] * (2 * n_arr)),
        input_output_aliases={i: i for i in range(2 * n_arr)},
        compiler_params=pltpu.CompilerParams(has_side_effects=DATAFLOW_EFFECT),
    )(*srcs, *lands, send_sems, recv_sems, after)
    return list(outs[:n_arr]), list(outs[n_arr:])


def _sum_devices(r, name, rows_per_step=ROW_BLOCK):
    _, m, n = r.shape
    tm = _tile(m, rows_per_step, 8)

    def body(r_ref, o_ref):
        acc = r_ref[0].astype(F32)
        for s in range(1, N_DEV):
            acc = acc + r_ref[s].astype(F32)
        o_ref[...] = acc

    return pl.pallas_call(
        body,
        name=name,
        grid=(m // tm,),
        out_shape=jax.ShapeDtypeStruct((m, n), F32),
        in_specs=[pl.BlockSpec((N_DEV, tm, n), lambda i: (0, i, 0))],
        out_specs=pl.BlockSpec((tm, n), lambda i: (i, 0)),
        compiler_params=pltpu.CompilerParams(dimension_semantics=("parallel",)),
    )(r)


def _get(ref):
    return ref[0] if len(ref.shape) == 3 else ref[...]


def _put(ref, val):
    if len(ref.shape) == 3:
        ref[0] = val
    else:
        ref[...] = val


def _norm_mod(hv, g, sc, sh):
    r = lax.rsqrt(jnp.mean(hv * hv, axis=-1, keepdims=True) + EPS)
    return (hv * r) * g * (1.0 + sc) + sh


def _mm_call(name, a, b, a_spec, b_spec, out_sds, o_spec, grid, dims, acc_shape, bias=None,
             res=None, gate=None, raw_out=False, vec_spec=None, norm=None):
    nk = grid[2]
    operands, in_specs = [a, b], [a_spec, b_spec]
    if bias is not None:
        operands.append(bias)
        in_specs.append(vec_spec)
    if res is not None:
        operands += [res, gate]
        in_specs += [o_spec, vec_spec]
    if norm is not None:
        assert grid[1] == 1
        operands += list(norm)
        in_specs += [vec_spec] * 3
    out_shape, out_specs = [out_sds], [o_spec]
    if raw_out:
        out_shape.append(jax.ShapeDtypeStruct(out_sds.shape, BF16))
        out_specs.append(o_spec)
    if norm is not None:
        out_shape.append(jax.ShapeDtypeStruct(out_sds.shape, BF16))
        out_specs.append(o_spec)

    def body(*refs):
        it = iter(refs)
        a_ref, b_ref = next(it), next(it)
        bias_ref = next(it) if bias is not None else None
        res_ref, gate_ref = (next(it), next(it)) if res is not None else (None, None)
        norm_refs = (next(it), next(it), next(it)) if norm is not None else None
        o_ref = next(it)
        raw_ref = next(it) if raw_out else None
        xn_ref = next(it) if norm is not None else None
        acc = next(it) if nk > 1 else None
        k = pl.program_id(2)
        part = _dot(_get(a_ref).astype(BF16), _get(b_ref).astype(BF16), dims)

        def finish(y):
            if bias_ref is not None:
                y = y + bias_ref[...]
            if raw_ref is not None:
                raw_ref[...] = y.astype(BF16)
            if res_ref is not None:
                y = res_ref[...] + gate_ref[...] * y
            _put(o_ref, y.astype(out_sds.dtype))
            if xn_ref is not None:
                xn_ref[...] = _norm_mod(y, *[r[...] for r in norm_refs]).astype(BF16)

        if nk == 1:
            finish(part)
        else:
            @pl.when(k == 0)
            def _():
                acc[...] = part

            @pl.when(k > 0)
            def _():
                acc[...] += part

            @pl.when(k == nk - 1)
            def _():
                finish(acc[...])

    outs = pl.pallas_call(
        body,
        name=name,
        grid=grid,
        out_shape=out_shape,
        in_specs=in_specs,
        out_specs=out_specs,
        scratch_shapes=[pltpu.VMEM(acc_shape, F32)] if nk > 1 else [],
        compiler_params=pltpu.CompilerParams(dimension_semantics=("parallel", "parallel", "arbitrary")),
    )(*operands)
    return outs if len(outs) > 1 else outs[0]


def _mm(a, b, mode, name, out_dtype=F32, bias=None, res=None, gate=None, raw_out=False,
        tm=512, tn=1024, tk=1024, a_row_off=0, norm=None):
    if mode == "nn":
        K, N = b.shape
        M = a.shape[0] - a_row_off
    elif mode == "nt":
        N, K = b.shape
        M = a.shape[0] - a_row_off
    else:
        (K, M), N = a.shape, b.shape[1]
    tm, tn, tk = _tile(M, tm), _tile(N, tn), _tile(K, tk)
    off = a_row_off // tm
    dims = {"nn": NN, "nt": NT, "tn": TN}[mode]
    a_spec = (pl.BlockSpec((tk, tm), lambda i, j, k: (k, i)) if mode == "tn"
              else pl.BlockSpec((tm, tk), lambda i, j, k: (i + off, k)))
    b_spec = (pl.BlockSpec((tn, tk), lambda i, j, k: (j, k)) if mode == "nt"
              else pl.BlockSpec((tk, tn), lambda i, j, k: (k, j)))
    return _mm_call(name, a, b, a_spec, b_spec, jax.ShapeDtypeStruct((M, N), out_dtype),
                    pl.BlockSpec((tm, tn), lambda i, j, k: (i, j)), (M // tm, N // tn, K // tk), dims,
                    (tm, tn), bias, res, gate, raw_out, pl.BlockSpec((1, tn), lambda i, j, k: (0, j)), norm)


def _mm_sum_shards(a3, b3, mode, name, out_dtype=F32, res=None, gate=None, raw_out=False, tm=512, norm=None):
    S, M, kk = a3.shape
    N = b3.shape[2] if mode == "nn" else b3.shape[1]
    tm = _tile(M, tm)
    dims = NN if mode == "nn" else NT
    has_res = res is not None

    def body(*refs):
        it = iter(refs)
        a_ref, b_ref = next(it), next(it)
        res_ref, gate_ref = (next(it), next(it)) if has_res else (None, None)
        norm_refs = (next(it), next(it), next(it)) if norm is not None else None
        o_ref = next(it)
        raw_ref = next(it) if raw_out else None
        xn_ref = next(it) if norm is not None else None
        y = _dot(a_ref[0], b_ref[0], dims)
        for s in range(1, S):
            y = y + _dot(a_ref[s], b_ref[s], dims)
        if raw_ref is not None:
            raw_ref[...] = y.astype(BF16)
        if has_res:
            y = res_ref[...] + gate_ref[...] * y
        o_ref[...] = y.astype(out_dtype)
        if xn_ref is not None:
            xn_ref[...] = _norm_mod(y, *[r[...] for r in norm_refs]).astype(BF16)

    tile = pl.BlockSpec((tm, N), lambda i: (i, 0))
    operands = [a3, b3] + ([res, gate] if has_res else []) + (list(norm) if norm is not None else [])
    in_specs = [pl.BlockSpec((S, tm, kk), lambda i: (0, i, 0)), pl.BlockSpec(b3.shape, lambda i: (0, 0, 0))]
    in_specs += [tile, _vec_spec(N)] if has_res else []
    in_specs += [_vec_spec(N)] * 3 if norm is not None else []
    out_shape = [jax.ShapeDtypeStruct((M, N), out_dtype)] + ([jax.ShapeDtypeStruct((M, N), BF16)] if raw_out else [])
    out_shape += [jax.ShapeDtypeStruct((M, N), BF16)] if norm is not None else []
    outs = pl.pallas_call(
        body, name=name, grid=(M // tm,),
        out_shape=out_shape, in_specs=in_specs, out_specs=[tile] * len(out_shape),
        compiler_params=pltpu.CompilerParams(dimension_semantics=("parallel",)),
    )(*operands)
    return outs if len(outs) > 1 else outs[0]


def _mm_tn_shard_rows(a3, b, name, out_dtype, tn=1024, tk=4096):
    S, T, m = a3.shape
    N = b.shape[1]
    tn, tk = _tile(N, tn), _tile(T, tk)
    return _mm_call(name, a3, b, pl.BlockSpec((1, tk, m), lambda i, j, k: (i, k, 0)),
                    pl.BlockSpec((tk, tn), lambda i, j, k: (k, j)), jax.ShapeDtypeStruct((S, m, N), out_dtype),
                    pl.BlockSpec((1, m, tn), lambda i, j, k: (i, 0, j)), (S, N // tn, T // tk), TN, (m, tn))


def _row_spec(tm, width, off=0):
    return pl.BlockSpec((tm, width), lambda i: (i + off, 0))


def _vec_spec(width):
    return pl.BlockSpec((1, width), lambda i: (0, 0))


def _norm_mod_fwd_cat(hc, h, g, csc, csh, sc, sh, name):
    (C, Dm), T = hc.shape, h.shape[0]
    tm = _tile(math.gcd(C, T), ROW_BLOCK, 8)
    off = C // tm

    def body(hc_ref, h_ref, g_ref, csc_ref, csh_ref, sc_ref, sh_ref, o_ref):
        is_ctx = pl.program_id(0) < off
        hv = jnp.where(is_ctx, hc_ref[...], h_ref[...])
        scv = jnp.where(is_ctx, csc_ref[...], sc_ref[...])
        shv = jnp.where(is_ctx, csh_ref[...], sh_ref[...])
        r = lax.rsqrt(jnp.mean(hv * hv, axis=-1, keepdims=True) + EPS)
        o_ref[...] = ((hv * r) * g_ref[...] * (1.0 + scv) + shv).astype(BF16)

    return pl.pallas_call(
        body, name=name, grid=((C + T) // tm,),
        out_shape=jax.ShapeDtypeStruct((C + T, Dm), BF16),
        in_specs=[pl.BlockSpec((tm, Dm), lambda i: (jnp.minimum(i, off - 1), 0)),
                  pl.BlockSpec((tm, Dm), lambda i: (jnp.maximum(i - off, 0), 0))] + [_vec_spec(Dm)] * 5,
        out_specs=_row_spec(tm, Dm),
        compiler_params=pltpu.CompilerParams(dimension_semantics=("parallel",)),
    )(hc, h, g, csc, csh, sc, sh)


def _gate_grads(dh, y_ref, gt_ref, dy_ref, dgt_ref, dsum_ref):
    dy = dh * gt_ref[...]
    dgt_ref[...] += jnp.sum(dh * y_ref[...].astype(F32), axis=0, keepdims=True)
    dsum_ref[...] += jnp.sum(dy, axis=0, keepdims=True)
    dy_ref[...] = dy.astype(BF16)


def _norm_mod_bwd(h, g, sc, dxm, dres, name, dxm_row_off=0, gate=None):
    R, Dm = h.shape
    tm = _tile(R, ROW_BLOCK, 8)
    off = dxm_row_off // tm
    has_res = dres is not None
    has_gate = gate is not None

    def body(*refs):
        it = iter(refs)
        h_ref, g_ref, sc_ref, dx_ref = next(it), next(it), next(it), next(it)
        dres_ref = next(it) if has_res else None
        y_ref, gt_ref = (next(it), next(it)) if has_gate else (None, None)
        dh_ref, da_ref, dsh_ref = next(it), next(it), next(it)
        gate_out = (next(it), next(it), next(it)) if has_gate else ()
        i = pl.program_id(0)

        @pl.when(i == 0)
        def _():
            for ref in (da_ref, dsh_ref) + gate_out[1:]:
                ref[...] = jnp.zeros_like(ref)

        hv = h_ref[...]
        dx = dx_ref[...].astype(F32)
        r = lax.rsqrt(jnp.mean(hv * hv, axis=-1, keepdims=True) + EPS)
        n = hv * r
        da_ref[...] += jnp.sum(dx * n, axis=0, keepdims=True)
        dsh_ref[...] += jnp.sum(dx, axis=0, keepdims=True)
        dn = dx * (g_ref[...] * (1.0 + sc_ref[...]))
        dh = r * (dn - n * jnp.mean(dn * n, axis=-1, keepdims=True))
        if has_res:
            dh = dh + dres_ref[...]
        dh_ref[...] = dh
        if has_gate:
            _gate_grads(dh, y_ref, gt_ref, *gate_out)

    operands = [h, g, sc, dxm] + ([dres] if has_res else []) + (list(gate) if has_gate else [])
    in_specs = [_row_spec(tm, Dm), _vec_spec(Dm), _vec_spec(Dm), _row_spec(tm, Dm, off)]
    in_specs += [_row_spec(tm, Dm)] if has_res else []
    in_specs += [_row_spec(tm, Dm), _vec_spec(Dm)] if has_gate else []
    vec = jax.ShapeDtypeStruct((1, Dm), F32)
    out_shape = [jax.ShapeDtypeStruct((R, Dm), F32), vec, vec]
    out_specs = [_row_spec(tm, Dm), _vec_spec(Dm), _vec_spec(Dm)]
    if has_gate:
        out_shape += [jax.ShapeDtypeStruct((R, Dm), BF16), vec, vec]
        out_specs += [_row_spec(tm, Dm), _vec_spec(Dm), _vec_spec(Dm)]
    return pl.pallas_call(
        body, name=name, grid=(R // tm,),
        out_shape=out_shape, in_specs=in_specs, out_specs=out_specs,
        compiler_params=pltpu.CompilerParams(dimension_semantics=("arbitrary",)),
    )(*operands)


def _ffn_in_swiglu(xf, w3, name, tm=1024):
    T, K = xf.shape
    S, n, _ = w3.shape
    half = S // 2
    tm = _tile(T, tm)

    def body(a_ref, wg_ref, wu_ref, gu_ref, act_ref):
        a = a_ref[...]
        g = _dot(a, wg_ref[0], NT)
        u = _dot(a, wu_ref[0], NT)
        gu_ref[0, 0] = g.astype(BF16)
        gu_ref[1, 0] = u.astype(BF16)
        act_ref[0] = (_silu(g) * u).astype(BF16)

    return pl.pallas_call(
        body, name=name, grid=(T // tm, half),
        out_shape=[jax.ShapeDtypeStruct((2, half, T, n), BF16), jax.ShapeDtypeStruct((half, T, n), BF16)],
        in_specs=[pl.BlockSpec((tm, K), lambda i, j: (i, 0)),
                  pl.BlockSpec((1, n, K), lambda i, j: (j, 0, 0)),
                  pl.BlockSpec((1, n, K), lambda i, j: (j + half, 0, 0))],
        out_specs=[pl.BlockSpec((2, 1, tm, n), lambda i, j: (0, j, i, 0)),
                   pl.BlockSpec((1, tm, n), lambda i, j: (j, i, 0))],
        compiler_params=pltpu.CompilerParams(dimension_semantics=("parallel", "parallel")),
    )(xf, w3, w3)


def _ffn_out_dx_swiglu(df, wo, gu, name, tm=1024):
    T, Dm = df.shape
    half, n, _ = wo.shape
    tm = _tile(T, tm)

    def body(df_ref, w_ref, gu_ref, o_ref):
        da = _dot(df_ref[...], w_ref[0], NT)
        g = gu_ref[0, 0].astype(F32)
        u = gu_ref[1, 0].astype(F32)
        s = _sigmoid(g)
        o_ref[0, 0] = (da * u * (s * (1.0 + g * (1.0 - s)))).astype(BF16)
        o_ref[1, 0] = (da * (g * s)).astype(BF16)

    gu_spec = pl.BlockSpec((2, 1, tm, n), lambda i, j: (0, j, i, 0))
    return pl.pallas_call(
        body, name=name, grid=(T // tm, half),
        out_shape=jax.ShapeDtypeStruct(gu.shape, BF16),
        in_specs=[pl.BlockSpec((tm, Dm), lambda i, j: (i, 0)),
                  pl.BlockSpec((1, n, Dm), lambda i, j: (j, 0, 0)), gu_spec],
        out_specs=gu_spec,
        compiler_params=pltpu.CompilerParams(dimension_semantics=("parallel", "parallel")),
    )(df, wo, gu)


def _glu_fwd(ag, name):
    R = ag.shape[0]
    tm = _tile(R, ROW_BLOCK, 8)

    def body(ag_ref, o_ref):
        o_ref[...] = ag_ref[:, :D_MODEL].astype(F32) * _sigmoid(ag_ref[:, D_MODEL:].astype(F32))

    return pl.pallas_call(
        body, name=name, grid=(R // tm,),
        out_shape=jax.ShapeDtypeStruct((R, D_MODEL), F32),
        in_specs=[_row_spec(tm, 2 * D_MODEL)],
        out_specs=_row_spec(tm, D_MODEL),
        compiler_params=pltpu.CompilerParams(dimension_semantics=("parallel",)),
    )(ag)


def _glu_bwd(ag, dhg, name):
    R = ag.shape[0]
    tm = _tile(R, ROW_BLOCK, 8)

    def body(ag_ref, dh_ref, o_ref, s_ref):
        i = pl.program_id(0)

        @pl.when(i == 0)
        def _():
            s_ref[...] = jnp.zeros_like(s_ref)

        a = ag_ref[:, :D_MODEL].astype(F32)
        s = _sigmoid(ag_ref[:, D_MODEL:].astype(F32))
        dh = dh_ref[...]
        da = dh * s
        dg = dh * a * s * (1.0 - s)
        o_ref[:, :D_MODEL] = da.astype(BF16)
        o_ref[:, D_MODEL:] = dg.astype(BF16)
        s_ref[:, :D_MODEL] += jnp.sum(da, axis=0, keepdims=True)
        s_ref[:, D_MODEL:] += jnp.sum(dg, axis=0, keepdims=True)

    return pl.pallas_call(
        body, name=name, grid=(R // tm,),
        out_shape=[jax.ShapeDtypeStruct((R, 2 * D_MODEL), BF16), jax.ShapeDtypeStruct((1, 2 * D_MODEL), F32)],
        in_specs=[_row_spec(tm, 2 * D_MODEL), _row_spec(tm, D_MODEL)],
        out_specs=[_row_spec(tm, 2 * D_MODEL), _vec_spec(2 * D_MODEL)],
        compiler_params=pltpu.CompilerParams(dimension_semantics=("arbitrary",)),
    )(ag, dhg)


def _halo_specs(tm, nblk, width):
    per = tm // CONV_HALO
    prev = pl.BlockSpec((CONV_HALO, width), lambda i: (jnp.maximum(i * per - 1, 0), 0))
    nxt = pl.BlockSpec((CONV_HALO, width), lambda i: (jnp.minimum((i + 1) * per, nblk * per - 1), 0))
    return prev, nxt


def _fill_halo(scr, prev_ref, cur_ref, next_ref, i, nblk, tm):
    scr[0:CONV_HALO, :] = jnp.where(i > 0, prev_ref[...], 0.0)
    scr[CONV_HALO:CONV_HALO + tm, :] = cur_ref[...]
    scr[CONV_HALO + tm:2 * CONV_HALO + tm, :] = jnp.where(i < nblk - 1, next_ref[...], 0.0)


CONV_ROWS = 128


def _windows(scr, cols, tm):
    reach = (CONV_WIDTH // SUBLANES) * SUBLANES
    for r in range(SUBLANES):
        base = scr[pl.ds(r, tm + reach), cols]
        for a in range(reach // SUBLANES + 1):
            off = SUBLANES * a + r
            if 1 <= off <= CONV_WIDTH:
                yield off, base[SUBLANES * a:SUBLANES * a + tm]


def _conv_fwd(hg, w_dw, b_dw, name):
    R, Dm = hg.shape
    tm = _tile(R, CONV_ROWS, CONV_HALO)
    nblk = R // tm
    prev_spec, next_spec = _halo_specs(tm, nblk, Dm)

    def body(prev_ref, cur_ref, next_ref, w_ref, bdw_ref, hd_ref, scr):
        _fill_halo(scr, prev_ref, cur_ref, next_ref, pl.program_id(0), nblk, tm)
        for cb in range(Dm // LANES):
            cols = slice(cb * LANES, (cb + 1) * LANES)
            acc = jnp.zeros((tm, LANES), F32) + bdw_ref[:, cols]
            for off, win in _windows(scr, cols, tm):
                acc = acc + w_ref[off - 1:off, cols] * win
            hd_ref[:, cols] = acc

    return pl.pallas_call(
        body, name=name, grid=(nblk,),
        out_shape=jax.ShapeDtypeStruct((R, Dm), F32),
        in_specs=[prev_spec, _row_spec(tm, Dm), next_spec,
                  pl.BlockSpec((CONV_WIDTH, Dm), lambda i: (0, 0)), _vec_spec(Dm)],
        out_specs=_row_spec(tm, Dm),
        scratch_shapes=[pltpu.VMEM((tm + 2 * CONV_HALO, Dm), F32)],
        compiler_params=pltpu.CompilerParams(dimension_semantics=("parallel",)),
    )(hg, hg, hg, w_dw, b_dw)


def _ln_silu_fwd(hd, ln_g, ln_b, name):
    R, Dm = hd.shape
    tm = _tile(R, ROW_BLOCK, 8)

    def body(hd_ref, g_ref, b_ref, hs_ref):
        hd = hd_ref[...]
        xc = hd - jnp.mean(hd, axis=-1, keepdims=True)
        rs = lax.rsqrt(jnp.mean(xc * xc, axis=-1, keepdims=True) + EPS)
        hs_ref[...] = _silu(xc * rs * g_ref[...] + b_ref[...]).astype(BF16)

    return pl.pallas_call(
        body, name=name, grid=(R // tm,),
        out_shape=jax.ShapeDtypeStruct((R, Dm), BF16),
        in_specs=[_row_spec(tm, Dm), _vec_spec(Dm), _vec_spec(Dm)],
        out_specs=_row_spec(tm, Dm),
        compiler_params=pltpu.CompilerParams(dimension_semantics=("parallel",)),
    )(hd, ln_g, ln_b)


def _ln_silu_bwd(dhs, hd, ln_g, ln_b, name):
    R, Dm = hd.shape
    tm = _tile(R, ROW_BLOCK, 8)

    def body(dhs_ref, hd_ref, g_ref, b_ref, dhd_ref, dg_ref, db_ref, dsum_ref):
        i = pl.program_id(0)

        @pl.when(i == 0)
        def _():
            dg_ref[...] = jnp.zeros_like(dg_ref)
            db_ref[...] = jnp.zeros_like(db_ref)
            dsum_ref[...] = jnp.zeros_like(dsum_ref)

        hd = hd_ref[...]
        mu = jnp.mean(hd, axis=-1, keepdims=True)
        xc = hd - mu
        rs = lax.rsqrt(jnp.mean(xc * xc, axis=-1, keepdims=True) + EPS)
        z = xc * rs
        hl = z * g_ref[...] + b_ref[...]
        dhl = dhs_ref[...].astype(F32) * _dsilu(hl)
        dg_ref[...] += jnp.sum(dhl * z, axis=0, keepdims=True)
        db_ref[...] += jnp.sum(dhl, axis=0, keepdims=True)
        dz = dhl * g_ref[...]
        dhd = rs * (dz - jnp.mean(dz, axis=-1, keepdims=True) - z * jnp.mean(dz * z, axis=-1, keepdims=True))
        dsum_ref[...] += jnp.sum(dhd, axis=0, keepdims=True)
        dhd_ref[...] = dhd

    return pl.pallas_call(
        body, name=name, grid=(R // tm,),
        out_shape=[jax.ShapeDtypeStruct((R, Dm), F32)] + [jax.ShapeDtypeStruct((1, Dm), F32)] * 3,
        in_specs=[_row_spec(tm, Dm), _row_spec(tm, Dm), _vec_spec(Dm), _vec_spec(Dm)],
        out_specs=[_row_spec(tm, Dm), _vec_spec(Dm), _vec_spec(Dm), _vec_spec(Dm)],
        compiler_params=pltpu.CompilerParams(dimension_semantics=("arbitrary",)),
    )(dhs, hd, ln_g, ln_b)


def _conv_bwd(dhd, hg, w_dw, name):
    R, Dm = hg.shape
    tm = _tile(R, CONV_ROWS, CONV_HALO)
    nblk = R // tm
    prev_spec, next_spec = _halo_specs(tm, nblk, Dm)

    def body(dprev, dcur, dnext, gprev, gcur, gnext, w_ref, dhg_ref, dw_ref, dscr, gscr, dwp):
        i = pl.program_id(0)

        @pl.when(i == 0)
        def _():
            dwp[...] = jnp.zeros_like(dwp)

        _fill_halo(dscr, dprev, dcur, dnext, i, nblk, tm)
        _fill_halo(gscr, gprev, gcur, gnext, i, nblk, tm)
        for cb in range(Dm // LANES):
            cols = slice(cb * LANES, (cb + 1) * LANES)
            acc = jnp.zeros((tm, LANES), F32)
            for off, win in _windows(dscr, cols, tm):
                j = CONV_WIDTH - off
                acc = acc + w_ref[j:j + 1, cols] * win
            dhg_ref[:, cols] = acc
            d_here = dcur[:, cols]
            for off, win in _windows(gscr, cols, tm):
                j = off - 1
                prod = d_here * win
                part = prod[0:SUBLANES]
                for k in range(1, tm // SUBLANES):
                    part = part + prod[k * SUBLANES:(k + 1) * SUBLANES]
                dwp[j * SUBLANES:(j + 1) * SUBLANES, cols] += part

        @pl.when(i == nblk - 1)
        def _():
            for j in range(CONV_WIDTH):
                dw_ref[j:j + 1, :] = jnp.sum(dwp[j * SUBLANES:(j + 1) * SUBLANES, :], axis=0, keepdims=True)

    return pl.pallas_call(
        body, name=name, grid=(nblk,),
        out_shape=[jax.ShapeDtypeStruct((R, Dm), F32), jax.ShapeDtypeStruct((CONV_WIDTH, Dm), F32)],
        in_specs=[prev_spec, _row_spec(tm, Dm), next_spec, prev_spec, _row_spec(tm, Dm), next_spec,
                  pl.BlockSpec((CONV_WIDTH, Dm), lambda i: (0, 0))],
        out_specs=[_row_spec(tm, Dm), pl.BlockSpec((CONV_WIDTH, Dm), lambda i: (0, 0))],
        scratch_shapes=[pltpu.VMEM((tm + 2 * CONV_HALO, Dm), F32)] * 2
        + [pltpu.VMEM((CONV_WIDTH * SUBLANES, Dm), F32)],
        compiler_params=pltpu.CompilerParams(dimension_semantics=("arbitrary",)),
    )(dhd, dhd, dhd, hg, hg, hg, w_dw)


def _swap16(y, lane):
    return jnp.where((lane & 16) == 0, pltpu.roll(y, LANES - 16, 1), pltpu.roll(y, 16, 1))


def _head_mean(v, bd):
    hi, lo = _split_bf16(v)
    return (_dot(hi, bd, NN) + _dot(lo, bd, NN)) * (1.0 / HEAD_DIM)


Q_COLS = (0, ATTN_WIDTH)
K_COLS = (ATTN_WIDTH, ATTN_WIDTH + HEAD_DIM * 2)
V_COLS = (K_COLS[1], K_COLS[1] + HEAD_DIM * 2)
SU_COLS = (V_COLS[1], V_COLS[1] + SG_WIDTH)
SV_COLS = (SU_COLS[1], SU_COLS[1] + SG_WIDTH)


def _mix_prep_fwd(p, ctx_rows, cos, sin, qg, kg, bd, w_sp, b_spt, name):
    TT = p.shape[0]
    off = ctx_rows // CHUNK
    q_scale = HEAD_DIM ** -0.5

    def body(p_ref, cos_ref, sin_ref, qg_ref, kg_ref, bd_ref, w_ref, b_ref,
             q_ref, kp_ref, vp_ref, kt_ref, sg_ref):
        lane = lax.broadcasted_iota(jnp.int32, (CHUNK, LANES), 1)
        low = lane < HEAD_DIM
        cs, sn, bdv = cos_ref[...], sin_ref[...], bd_ref[...]

        def norm_rope(xv, gain):
            r = lax.rsqrt(_head_mean(xv * xv, bdv) + EPS)
            yv = xv * r * gain
            return yv * cs + _swap16(yv, lane) * sn

        def pad_heads(ref, t):
            tr = pltpu.roll(t, HEAD_DIM, 1)
            ref[0, 0] = jnp.where(low, t, 0.0).astype(BF16)
            ref[0, 1] = jnp.where(low, 0.0, tr).astype(BF16)
            ref[1, 0] = jnp.where(low, tr, 0.0).astype(BF16)
            ref[1, 1] = jnp.where(low, 0.0, t).astype(BF16)

        for a in range(ATTN_WIDTH // LANES):
            xv = p_ref[:, a * LANES:(a + 1) * LANES]
            q_ref[:, a * LANES:(a + 1) * LANES] = (norm_rope(xv, qg_ref[...]) * q_scale).astype(BF16)
        kh = norm_rope(p_ref[:, K_COLS[0]:K_COLS[1]], kg_ref[...])
        pad_heads(kp_ref, kh)
        pad_heads(vp_ref, p_ref[:, V_COLS[0]:V_COLS[1]])
        kht = kh.T
        kt_ref[0] = kht[:HEAD_DIM].astype(BF16)
        kt_ref[1] = kht[HEAD_DIM:].astype(BF16)
        for g in range(N_SG_GROUPS):
            u = _gelu(p_ref[:, SU_COLS[0] + g * LANES:SU_COLS[0] + (g + 1) * LANES])
            vg = _gelu(p_ref[:, SV_COLS[0] + g * LANES:SV_COLS[0] + (g + 1) * LANES])
            xc = vg - jnp.mean(vg, axis=-1, keepdims=True)
            vn = xc * lax.rsqrt(jnp.mean(xc * xc, axis=-1, keepdims=True) + EPS)
            mixed = _dot(w_ref[g].astype(BF16), vn.astype(BF16), NN) + b_ref[:, g:g + 1]
            sg_ref[:, g * LANES:(g + 1) * LANES] = (u * mixed).astype(BF16)

    def row(width):
        return pl.BlockSpec((CHUNK, width), lambda i: (i, 0))

    def whole(shape):
        return pl.BlockSpec(shape, lambda i: (0,) * len(shape))

    pad_spec = pl.BlockSpec((2, 2, CHUNK, LANES), lambda i: (0, 0, i, 0))
    return pl.pallas_call(
        body, name=name, grid=(TT // CHUNK,),
        out_shape=[jax.ShapeDtypeStruct((TT, ATTN_WIDTH), BF16),
                   jax.ShapeDtypeStruct((2, 2, TT, LANES), BF16), jax.ShapeDtypeStruct((2, 2, TT, LANES), BF16),
                   jax.ShapeDtypeStruct((2, HEAD_DIM, TT), BF16),
                   jax.ShapeDtypeStruct((TT - ctx_rows, ATTN_WIDTH + SG_WIDTH), BF16)],
        in_specs=[row(IN_WIDTH), row(LANES), row(LANES), whole((1, LANES)), whole((1, LANES)),
                  whole((LANES, LANES)), whole((N_SG_GROUPS, CHUNK, CHUNK)), whole((CHUNK, N_SG_GROUPS))],
        out_specs=[row(ATTN_WIDTH), pad_spec, pad_spec,
                   pl.BlockSpec((2, HEAD_DIM, CHUNK), lambda i: (0, 0, i)),
                   pl.BlockSpec((CHUNK, SG_WIDTH), lambda i: (jnp.maximum(i - off, 0), 1))],
        compiler_params=pltpu.CompilerParams(dimension_semantics=("arbitrary",)),
    )(p, cos, sin, qg, kg, bd, w_sp, b_spt)


def _mix_prep_bwd(p, dq, f, dao, ctx_rows, cos, sin, qg, kg, bd, w_sp, w_spt, b_spt, name):
    TT = p.shape[0]
    off = ctx_rows // CHUNK
    q_scale = HEAD_DIM ** -0.5

    def body(p_ref, dq_ref, f_ref, dsg_ref, cos_ref, sin_ref, qg_ref, kg_ref, bd_ref, w_ref, wt_ref,
             b_ref, dp_ref, dqg_ref, dkg_ref, dw_ref, db_ref):
        i = pl.program_id(0)

        @pl.when(i == 0)
        def _():
            dqg_ref[...] = jnp.zeros_like(dqg_ref)
            dkg_ref[...] = jnp.zeros_like(dkg_ref)
            dw_ref[...] = jnp.zeros_like(dw_ref)
            db_ref[...] = jnp.zeros_like(db_ref)

        latent = (i >= off).astype(F32)
        lane = lax.broadcasted_iota(jnp.int32, (CHUNK, LANES), 1)
        low = lane < HEAD_DIM
        cs, sn, bdv = cos_ref[...], sin_ref[...], bd_ref[...]

        def fold(b0):
            return jnp.where(low, f_ref[0, b0] + pltpu.roll(f_ref[0, b0 + 1], HEAD_DIM, 1),
                             pltpu.roll(f_ref[1, b0], HEAD_DIM, 1) + f_ref[1, b0 + 1])

        def norm_rope_bwd(xv, dout, gain):
            r = lax.rsqrt(_head_mean(xv * xv, bdv) + EPS)
            n = xv * r
            dy = dout * cs + _swap16(dout * sn, lane)
            dn = dy * gain
            dx = r * (dn - n * _head_mean(dn * n, bdv))
            return dx, jnp.sum(dy * n, axis=0, keepdims=True)

        for a in range(ATTN_WIDTH // LANES):
            cols = slice(a * LANES, (a + 1) * LANES)
            dx, dg = norm_rope_bwd(p_ref[:, cols], dq_ref[:, cols] * (latent * q_scale), qg_ref[...])
            dp_ref[:, cols] = dx.astype(BF16)
            dqg_ref[...] += dg
        dx, dg = norm_rope_bwd(p_ref[:, K_COLS[0]:K_COLS[1]], fold(0), kg_ref[...])
        dp_ref[:, K_COLS[0]:K_COLS[1]] = dx.astype(BF16)
        dkg_ref[...] += dg
        dp_ref[:, V_COLS[0]:V_COLS[1]] = fold(2).astype(BF16)
        for g in range(N_SG_GROUPS):
            su = p_ref[:, SU_COLS[0] + g * LANES:SU_COLS[0] + (g + 1) * LANES]
            sv = p_ref[:, SV_COLS[0] + g * LANES:SV_COLS[0] + (g + 1) * LANES]
            (u, dgelu_su), (vg, dgelu_sv) = _gelu_and_grad(su), _gelu_and_grad(sv)
            xc = vg - jnp.mean(vg, axis=-1, keepdims=True)
            rs = lax.rsqrt(jnp.mean(xc * xc, axis=-1, keepdims=True) + EPS)
            vn = xc * rs
            vnb = vn.astype(BF16)
            mixed = _dot(w_ref[g].astype(BF16), vnb, NN) + b_ref[:, g:g + 1]
            dsg = dsg_ref[:, g * LANES:(g + 1) * LANES].astype(F32) * latent
            du = dsg * mixed
            dmix = dsg * u
            dmb = dmix.astype(BF16)
            db_ref[:, g:g + 1] += jnp.sum(dmix, axis=-1, keepdims=True)
            dw_ref[g] += _dot(dmb, vnb, NT)
            dvn = _dot(wt_ref[g].astype(BF16), dmb, NN)
            dvg = rs * (dvn - jnp.mean(dvn, axis=-1, keepdims=True)
                        - vn * jnp.mean(dvn * vn, axis=-1, keepdims=True))
            dp_ref[:, SU_COLS[0] + g * LANES:SU_COLS[0] + (g + 1) * LANES] = (du * dgelu_su).astype(BF16)
            dp_ref[:, SV_COLS[0] + g * LANES:SV_COLS[0] + (g + 1) * LANES] = (dvg * dgelu_sv).astype(BF16)

    def row(width):
        return pl.BlockSpec((CHUNK, width), lambda i: (i, 0))

    def latent_row(width, col_block):
        return pl.BlockSpec((CHUNK, width), lambda i: (jnp.maximum(i - off, 0), col_block))

    def whole(shape):
        return pl.BlockSpec(shape, lambda i: (0,) * len(shape))

    return pl.pallas_call(
        body, name=name, grid=(TT // CHUNK,),
        out_shape=[jax.ShapeDtypeStruct((TT, IN_WIDTH), BF16), jax.ShapeDtypeStruct((1, LANES), F32),
                   jax.ShapeDtypeStruct((1, LANES), F32),
                   jax.ShapeDtypeStruct((N_SG_GROUPS, CHUNK, CHUNK), F32),
                   jax.ShapeDtypeStruct((CHUNK, N_SG_GROUPS), F32)],
        in_specs=[row(IN_WIDTH), latent_row(ATTN_WIDTH, 0),
                  pl.BlockSpec((2, 4, CHUNK, LANES), lambda i: (0, 0, i, 0)),
                  latent_row(SG_WIDTH, 1), row(LANES), row(LANES), whole((1, LANES)), whole((1, LANES)),
                  whole((LANES, LANES)), whole((N_SG_GROUPS, CHUNK, CHUNK)),
                  whole((N_SG_GROUPS, CHUNK, CHUNK)), whole((CHUNK, N_SG_GROUPS))],
        out_specs=[row(IN_WIDTH), whole((1, LANES)), whole((1, LANES)),
                   whole((N_SG_GROUPS, CHUNK, CHUNK)), whole((CHUNK, N_SG_GROUPS))],
        compiler_params=pltpu.CompilerParams(dimension_semantics=("arbitrary",)),
    )(p, dq, f, dao, cos, sin, qg, kg, bd, w_sp, w_spt, b_spt)


def _attn_fwd(q, kpad, vpad, ao, ctx_rows, name, tq=256):
    TT = q.shape[0]
    T = TT - ctx_rows
    tq = _tile(T, tq)
    off = ctx_rows // tq
    group = 2 * LANES

    def body(q_ref, k_ref, v_ref, ao_in, o_ref, lse_ref):
        del ao_in
        lane = lax.broadcasted_iota(jnp.int32, (tq, LANES), 1)
        lse = jnp.zeros((tq, LANES), F32)
        for a in range(2):
            acc = jnp.zeros((tq, LANES), F32)
            qa = q_ref[:, a * LANES:(a + 1) * LANES]
            for b in range(2):
                s = _dot(qa, k_ref[0, b], NT)
                m = jnp.max(s, axis=-1, keepdims=True)
                e = jnp.exp(s - m)
                l = jnp.sum(e, axis=-1, keepdims=True)
                acc = acc + _dot(e.astype(BF16), v_ref[0, b], NN) * (1.0 / l)
                lse = jnp.where(lane == 2 * a + b, m + jnp.log(l), lse)
            o_ref[:, a * LANES:(a + 1) * LANES] = acc.astype(BF16)
        lse_ref[0] = lse

    kv_spec = pl.BlockSpec((1, 2, TT, LANES), lambda j, i: (j, 0, 0, 0))
    return pl.pallas_call(
        body, name=name, grid=(2, T // tq),
        out_shape=[jax.ShapeDtypeStruct(ao.shape, BF16), jax.ShapeDtypeStruct((2, T, LANES), F32)],
        in_specs=[pl.BlockSpec((tq, group), lambda j, i: (i + off, j)), kv_spec, kv_spec,
                  pl.BlockSpec(memory_space=pl.ANY)],
        out_specs=[pl.BlockSpec((tq, group), lambda j, i: (i, j)),
                   pl.BlockSpec((1, tq, LANES), lambda j, i: (j, i, 0))],
        input_output_aliases={3: 0},
        compiler_params=pltpu.CompilerParams(dimension_semantics=("parallel", "parallel")),
    )(q, kpad, vpad, ao)


def _attn_bwd(q, dao, ao, lse, kpad, vpad, kt, ctx_rows, name, tq=256):
    TT = q.shape[0]
    T = TT - ctx_rows
    tq = _tile(T, tq)
    off = ctx_rows // tq
    group = 2 * LANES

    def body(q_ref, do_ref, o_ref, lse_ref, k_ref, v_ref, kt_ref, dq_ref, f_ref):
        i = pl.program_id(1)

        @pl.when(i == 0)
        def _():
            f_ref[...] = jnp.zeros_like(f_ref)

        ktv = kt_ref[0]
        lse_t = lse_ref[0].T
        row = lax.broadcasted_iota(jnp.int32, (SUBLANES, LANES), 0)
        lane = lax.broadcasted_iota(jnp.int32, (SUBLANES, LANES), 1)
        half_ones = (jnp.where(lane < HEAD_DIM, 0, 1) == row).astype(BF16)
        for a in range(2):
            cols = slice(a * LANES, (a + 1) * LANES)
            qa = q_ref[:, cols]
            do32 = do_ref[:, cols].astype(F32)
            doa = do32.astype(BF16)
            hi, lo = _split_bf16(do32 * o_ref[:, cols].astype(F32))
            deltas = _dot(half_ones, hi, NT) + _dot(half_ones, lo, NT)
            halves = []
            for b in range(2):
                h = 2 * a + b
                st = _dot(k_ref[0, b], qa, NT)
                pt = jnp.exp(st - lse_t[h:h + 1, :])
                dpt = _dot(v_ref[0, b], doa, NT)
                dst = (pt * (dpt - deltas[b:b + 1, :])).astype(BF16)
                f_ref[0, b] += _dot(dst, qa, NN)
                f_ref[0, 2 + b] += _dot(pt.astype(BF16), doa, NN)
                halves.append(_dot(ktv, dst, NN))
            dq_ref[:, cols] = jnp.concatenate(halves, axis=0).T

    kv_spec = pl.BlockSpec((1, 2, TT, LANES), lambda j, i: (j, 0, 0, 0))
    out_cols = pl.BlockSpec((tq, group), lambda j, i: (i, j))
    return pl.pallas_call(
        body, name=name, grid=(2, T // tq),
        out_shape=[jax.ShapeDtypeStruct((T, ATTN_WIDTH), F32), jax.ShapeDtypeStruct((2, 4, TT, LANES), F32)],
        in_specs=[pl.BlockSpec((tq, group), lambda j, i: (i + off, j)), out_cols, out_cols,
                  pl.BlockSpec((1, tq, LANES), lambda j, i: (j, i, 0)),
                  kv_spec, kv_spec, pl.BlockSpec((1, HEAD_DIM, TT), lambda j, i: (j, 0, 0))],
        out_specs=[out_cols, pl.BlockSpec((1, 4, TT, LANES), lambda j, i: (j, 0, 0, 0))],
        compiler_params=pltpu.CompilerParams(dimension_semantics=("parallel", "arbitrary")),
    )(q, dao, ao, lse, kpad, vpad, kt)


def _final_fwd_bwd(h, g, target, y, gt, name):
    R, Dm = h.shape
    tm = _tile(R, ROW_BLOCK, 8)

    def body(h_ref, g_ref, t_ref, y_ref, gt_ref, dh_ref, loss_ref, dg_ref, dy_ref, dgt_ref, dsum_ref):
        i = pl.program_id(0)

        @pl.when(i == 0)
        def _():
            for ref in (loss_ref, dg_ref, dgt_ref, dsum_ref):
                ref[...] = jnp.zeros_like(ref)

        hv = h_ref[...]
        r = lax.rsqrt(jnp.mean(hv * hv, axis=-1, keepdims=True) + EPS)
        n = hv * r
        diff = n * g_ref[...] - t_ref[...]
        loss_ref[...] += jnp.sum(diff * diff)
        dout = diff * (1.0 / Dm)
        dg_ref[...] += jnp.sum(dout * n, axis=0, keepdims=True)
        dn = dout * g_ref[...]
        dh = r * (dn - n * jnp.mean(dn * n, axis=-1, keepdims=True))
        dh_ref[...] = dh
        _gate_grads(dh, y_ref, gt_ref, dy_ref, dgt_ref, dsum_ref)

    vec = jax.ShapeDtypeStruct((1, Dm), F32)
    return pl.pallas_call(
        body, name=name, grid=(R // tm,),
        out_shape=[jax.ShapeDtypeStruct((R, Dm), F32), jax.ShapeDtypeStruct((1, LANES), F32), vec,
                   jax.ShapeDtypeStruct((R, Dm), BF16), vec, vec],
        in_specs=[_row_spec(tm, Dm), _vec_spec(Dm), _row_spec(tm, Dm), _row_spec(tm, Dm), _vec_spec(Dm)],
        out_specs=[_row_spec(tm, Dm), _vec_spec(LANES), _vec_spec(Dm), _row_spec(tm, Dm), _vec_spec(Dm),
                   _vec_spec(Dm)],
        compiler_params=pltpu.CompilerParams(dimension_semantics=("arbitrary",)),
    )(h, g, target, y, gt)


MOD_ROWS = 16


def _mod_fwd(c_rows, w_mod, name):
    L, Dm, n = w_mod.shape

    def body(c_ref, w_ref, o_ref):
        o_ref[0] = _dot3(_silu(c_ref[...]), w_ref[0], NN)

    return pl.pallas_call(
        body, name=name, grid=(L,),
        out_shape=jax.ShapeDtypeStruct((L, MOD_ROWS, n), F32),
        in_specs=[pl.BlockSpec((MOD_ROWS, Dm), lambda l: (0, 0)), pl.BlockSpec((1, Dm, n), lambda l: (l, 0, 0))],
        out_specs=pl.BlockSpec((1, MOD_ROWS, n), lambda l: (l, 0, 0)),
        compiler_params=pltpu.CompilerParams(dimension_semantics=("parallel",)),
    )(c_rows, w_mod)


def _mod_bwd(c_rows_t, dmod, w_mod, name):
    L, Dm, n = w_mod.shape

    def body(ct_ref, d_ref, w_ref, gw_ref, ds_ref):
        dm = d_ref[0]
        gw_ref[0] = _dot3(_silu(ct_ref[...]), dm, NN)
        ds_ref[0] = _dot3(dm[:MOD_ROWS], w_ref[0], NT)

    return pl.pallas_call(
        body, name=name, grid=(L,),
        out_shape=[jax.ShapeDtypeStruct((L, Dm, n), F32), jax.ShapeDtypeStruct((L, MOD_ROWS, Dm), F32)],
        in_specs=[pl.BlockSpec((Dm, LANES), lambda l: (0, 0)), pl.BlockSpec((1, LANES, n), lambda l: (l, 0, 0)),
                  pl.BlockSpec((1, Dm, n), lambda l: (l, 0, 0))],
        out_specs=[pl.BlockSpec((1, Dm, n), lambda l: (l, 0, 0)),
                   pl.BlockSpec((1, MOD_ROWS, Dm), lambda l: (l, 0, 0))],
        compiler_params=pltpu.CompilerParams(dimension_semantics=("parallel",)),
    )(c_rows_t, dmod, w_mod)


def _adam_update(w, g, m, v):
    c1 = 1.0 - ADAM_B1 ** ADAM_STEP
    c2 = 1.0 - ADAM_B2 ** ADAM_STEP
    mn = ADAM_B1 * m + (1.0 - ADAM_B1) * g
    vn = ADAM_B2 * v + (1.0 - ADAM_B2) * (g * g)
    return -ADAM_LR * ((mn / c1) / (jnp.sqrt(vn / c2) + ADAM_EPS) + ADAM_WD * w), mn, vn


def _adamw(w, g, m, v, name):
    R, Cw = w.shape
    tm = _tile(R, ROW_BLOCK, 8)

    def body(w_ref, g_ref, m_ref, v_ref, d_ref, mo_ref, vo_ref):
        d_ref[...], mo_ref[...], vo_ref[...] = _adam_update(w_ref[...], g_ref[...], m_ref[...], v_ref[...])

    spec = pl.BlockSpec((tm, Cw), lambda i: (i, 0))
    return pl.pallas_call(
        body, name=name, grid=(R // tm,),
        out_shape=[jax.ShapeDtypeStruct((R, Cw), F32)] * 3,
        in_specs=[spec] * 4, out_specs=[spec] * 3,
        compiler_params=pltpu.CompilerParams(dimension_semantics=("parallel",)),
    )(w, g, m, v)


def _adamw_recv(w, m, v, recvs, name):
    L, R, n = w.shape
    tm = _tile(R, ROW_BLOCK, 8)
    nblk = R // tm
    parts = [r.reshape(N_DEV, R, n) for r in recvs]

    def body(*refs):
        w_ref, m_ref, v_ref = refs[:3]
        part_refs = refs[3:3 + L]
        g_ref, d_ref, mo_ref, vo_ref, gsum = refs[3 + L:]
        l = pl.program_id(0)
        for ll in range(L):
            @pl.when(l == ll)
            def _(ll=ll):
                acc = part_refs[ll][0].astype(F32)
                for s in range(1, N_DEV):
                    acc = acc + part_refs[ll][s].astype(F32)
                gsum[...] = acc
        g = gsum[...]
        g_ref[0] = g
        d_ref[0], mo_ref[0], vo_ref[0] = _adam_update(w_ref[0], g, m_ref[0], v_ref[0])

    def part_spec(ll):
        return pl.BlockSpec((N_DEV, tm, n), lambda l, i: (0, jnp.where(l == ll, i, jnp.where(l < ll, 0, nblk - 1)), 0))

    spec = pl.BlockSpec((1, tm, n), lambda l, i: (l, i, 0))
    return pl.pallas_call(
        body, name=name, grid=(L, nblk),
        out_shape=[jax.ShapeDtypeStruct((L, R, n), F32)] * 4,
        in_specs=[spec] * 3 + [part_spec(ll) for ll in range(L)], out_specs=[spec] * 4,
        scratch_shapes=[pltpu.VMEM((tm, n), F32)],
        compiler_params=pltpu.CompilerParams(dimension_semantics=("parallel", "parallel")),
    )(w, m, v, *parts)


def _pack(parts, row_mult=8):
    flat, offs, pos = [], [], 0
    for t in parts:
        t = t.reshape(-1).astype(F32)
        size = -(-t.shape[0] // LANES) * LANES
        flat.append(jnp.pad(t, (0, size - t.shape[0])))
        offs.append(pos)
        pos += size
    total = -(-pos // (LANES * row_mult)) * (LANES * row_mult)
    if total > pos:
        flat.append(jnp.zeros((total - pos,), F32))
    return jnp.concatenate(flat).reshape(-1, LANES), offs


def _take(buf, off, shape):
    size = math.prod(shape)
    return buf[..., off:off + size].reshape(buf.shape[:-1] + tuple(shape))


def _rope_tables(T, ctx_rows):
    pos = jnp.arange(T)
    row = (pos // GRID_W).astype(F32)
    col = (pos % GRID_W).astype(F32)
    half = HEAD_DIM // 4
    inv = ROPE_THETA ** (-jnp.arange(0, 2 * half, 2, dtype=F32) / (2 * half))
    ang_r, ang_c = row[:, None] * inv[None, :], col[:, None] * inv[None, :]
    cos = jnp.concatenate([jnp.cos(ang_r)] * 2 + [jnp.cos(ang_c)] * 2, axis=1)
    sin = jnp.concatenate([-jnp.sin(ang_r), jnp.sin(ang_r), -jnp.sin(ang_c), jnp.sin(ang_c)], axis=1)
    cos = jnp.concatenate([jnp.ones((ctx_rows, HEAD_DIM), F32), cos], axis=0)
    sin = jnp.concatenate([jnp.zeros((ctx_rows, HEAD_DIM), F32), sin], axis=0)
    return jnp.tile(cos, (1, 2)), jnp.tile(sin, (1, 2))


def kernel(x, c, ctx, c_ctx, w_mod, b_mod, g_mix, g_ffn, w_ffn_in, w_ffn_out, w_in, q_gain, k_gain, w_sp, b_sp, w_out, w_pw1, b_pw1, w_dw, b_dw, ln_g, ln_b, w_pw2, b_pw2, g_final, loss_target, m_c_ctx, m_w_mod, m_b_mod, m_g_mix, m_g_ffn, m_w_ffn_in, m_w_ffn_out, m_w_in, m_q_gain, m_k_gain, m_w_sp, m_b_sp, m_w_out, m_w_pw1, m_b_pw1, m_w_dw, m_b_dw, m_ln_g, m_ln_b, m_w_pw2, m_b_pw2, m_g_final, v_c_ctx, v_w_mod, v_b_mod, v_g_mix, v_g_ffn, v_w_ffn_in, v_w_ffn_out, v_w_in, v_q_gain, v_k_gain, v_w_sp, v_b_sp, v_w_out, v_w_pw1, v_b_pw1, v_w_dw, v_b_dw, v_ln_g, v_ln_b, v_w_pw2, v_b_pw2, v_g_final):
    weights = dict(c_ctx=c_ctx, w_mod=w_mod, b_mod=b_mod, g_mix=g_mix, g_ffn=g_ffn, w_ffn_in=w_ffn_in,
                   w_ffn_out=w_ffn_out, w_in=w_in, q_gain=q_gain, k_gain=k_gain, w_sp=w_sp, b_sp=b_sp,
                   w_out=w_out, w_pw1=w_pw1, b_pw1=b_pw1, w_dw=w_dw, b_dw=b_dw, ln_g=ln_g, ln_b=ln_b,
                   w_pw2=w_pw2, b_pw2=b_pw2, g_final=g_final)
    moments_m = dict(c_ctx=m_c_ctx, w_mod=m_w_mod, b_mod=m_b_mod, g_mix=m_g_mix, g_ffn=m_g_ffn,
                     w_ffn_in=m_w_ffn_in, w_ffn_out=m_w_ffn_out, w_in=m_w_in, q_gain=m_q_gain,
                     k_gain=m_k_gain, w_sp=m_w_sp, b_sp=m_b_sp, w_out=m_w_out, w_pw1=m_w_pw1,
                     b_pw1=m_b_pw1, w_dw=m_w_dw, b_dw=m_b_dw, ln_g=m_ln_g, ln_b=m_ln_b, w_pw2=m_w_pw2,
                     b_pw2=m_b_pw2, g_final=m_g_final)
    moments_v = dict(c_ctx=v_c_ctx, w_mod=v_w_mod, b_mod=v_b_mod, g_mix=v_g_mix, g_ffn=v_g_ffn,
                     w_ffn_in=v_w_ffn_in, w_ffn_out=v_w_ffn_out, w_in=v_w_in, q_gain=v_q_gain,
                     k_gain=v_k_gain, w_sp=v_w_sp, b_sp=v_b_sp, w_out=v_w_out, w_pw1=v_w_pw1,
                     b_pw1=v_b_pw1, w_dw=v_w_dw, b_dw=v_b_dw, ln_g=v_ln_g, ln_b=v_ln_b, w_pw2=v_w_pw2,
                     b_pw2=v_b_pw2, g_final=v_g_final)
    names = list(weights)

    T, C = x.shape[1], ctx.shape[1]
    Dm = D_MODEL
    me = 4 * lax.axis_index("x") + 2 * lax.axis_index("y") + lax.axis_index("c")
    h0 = x[0]
    ctx2 = ctx[0]
    target = loss_target[0]

    small_sharded = (("w_dw", w_dw[0]), ("b_pw1", b_pw1), ("b_dw", b_dw), ("ln_g", ln_g), ("ln_b", ln_b),
                     ("b_pw2", b_pw2))
    buf1, offs1 = _pack([c] + [t for _, t in small_sharded])
    w_in_t, m_w_in_t, v_w_in_t = (jnp.swapaxes(t, 1, 2) for t in (w_in, m_w_in, v_w_in))
    w_ffi_t, m_w_ffi_t, v_w_ffi_t = (jnp.swapaxes(t, 1, 2) for t in (w_ffn_in, m_w_ffn_in, v_w_ffn_in))
    got1, W_in_t, W_out = _all_gather([buf1, w_in_t[0].astype(BF16), w_out[0].astype(BF16)], "gather_cond", False)
    got1 = got1.reshape(N_DEV, -1)
    c_all = _take(got1, offs1[0], (Dm,))
    full_small = {}
    for (nm, t), off in zip(small_sharded, offs1[1:]):
        seg = _take(got1, off, t.shape)
        full_small[nm] = jnp.moveaxis(seg, 0, -2).reshape(t.shape[:-1] + (N_DEV * t.shape[-1],))
    w_dw_f, b_pw1_f = full_small["w_dw"], full_small["b_pw1"]
    b_dw_f, ln_g_f, ln_b_f, b_pw2_f = (full_small[k] for k in ("b_dw", "ln_g", "ln_b", "b_pw2"))

    c_rows = jnp.concatenate([c_all, c_ctx[None, :], jnp.zeros((MOD_ROWS - N_DEV - 1, Dm), F32)], axis=0)
    mod_part = _mod_fwd(c_rows, w_mod, "mod_fwd")
    n_mod = w_mod.shape[2]
    got2 = _all_gather([mod_part.reshape(-1, LANES)], "gather_mod", True)[0]
    mod_all = got2.reshape(N_DEV, 2, MOD_ROWS, n_mod).transpose(1, 2, 0, 3).reshape(2, MOD_ROWS, N_DEV * n_mod)
    mod_all = mod_all + b_mod[:, None, :]
    my_mod = lax.dynamic_index_in_dim(mod_all, me, axis=1, keepdims=False)
    sh1, sc1, gt1, sh2, sc2, gt2 = ([my_mod[l:l + 1, k * Dm:(k + 1) * Dm] for l in range(2)] for k in range(6))
    csh1 = mod_all[0, N_DEV:N_DEV + 1, 0:Dm]
    csc1 = mod_all[0, N_DEV:N_DEV + 1, Dm:2 * Dm]

    behind = got2[0:1, 0:1] * 0.0
    gather_groups = [[w_ffi_t[0], w_ffn_out[0]], [w_pw1[0], w_pw2[0]], [w_ffi_t[1], w_ffn_out[1]]]
    gathers = [_push_begin([(t + behind).astype(BF16) for t in grp], True, f"gather_start{k}")
               for k, grp in enumerate(gather_groups)]
    started = sum(h[4][0:1, 0:1] for h in gathers)

    def gathered(k, after):
        return _push_end(gathers[k], after, f"gather_wait{k}")[1]

    def ffn_weights(k, after):
        wi, wo = gathered(k, after)
        return wi.reshape(N_DEV, FF_SHARD, Dm), wo.reshape(N_DEV // 2, FF_SHARD, Dm)

    def col_gathered(t, n):
        return t.reshape(N_DEV, Dm, n).transpose(1, 0, 2).reshape(Dm, N_DEV * n)

    W_ffi, W_ffo = [None, None], [None, None]

    g_mix_r = [g_mix[l:l + 1] for l in range(2)]
    g_ffn_r = [g_ffn[l:l + 1] for l in range(2)]
    g_fin = g_final[None, :]

    cos, sin = _rope_tables(T, C)
    qg = jnp.tile(q_gain, (1, 2))
    kg = jnp.tile(k_gain, (1, 2))
    lane_head = jnp.arange(LANES) // HEAD_DIM
    bd = (lane_head[:, None] == lane_head[None, :]).astype(BF16)
    w_sp0 = w_sp[0]
    w_spt0 = w_sp0.transpose(0, 2, 1)
    b_spt0 = b_sp[0].T

    XM = _norm_mod_fwd_cat(ctx2, h0, g_mix_r[0], csc1, csh1, sc1[0] + started, sh1[0], "norm_mix0")
    P = _mm(XM, W_in_t, "nt", "in_proj", tn=IN_WIDTH)
    qh, kpad, vpad, kt, ao = _mix_prep_fwd(P, C, cos, sin, qg, kg, bd, w_sp0, b_spt0, "mix_prep")
    ao, lse = _attn_fwd(qh, kpad, vpad, ao, C, "attn_fwd")
    h1, y0, xf0 = _mm(ao, W_out, "nn", "out_proj", res=h0, gate=gt1[0], raw_out=True,
                      norm=(g_ffn_r[0], sc2[0], sh2[0]))

    def ffn_fwd(h_in, xf, l, norm_next):
        W_ffi[l], W_ffo[l] = ffn_weights(2 * l, xf)
        gu, act = _ffn_in_swiglu(xf, W_ffi[l], f"ffn_in{l}")
        outs = _mm_sum_shards(act, W_ffo[l], "nn", f"ffn_out{l}", res=h_in, gate=gt2[l], raw_out=True,
                              norm=norm_next)
        return tuple(outs) + (None,) * (3 - len(outs)) + (gu, act)

    h2, f0, xm1, gu0, act0 = ffn_fwd(h1, xf0, 0, (g_mix_r[1], sc1[1], sh1[1]))

    W_pw1, W_pw2 = gathered(1, xm1)
    W_pw1 = col_gathered(W_pw1, 2 * Dm // N_DEV)
    ag = _mm(xm1, W_pw1, "nn", "pw1", BF16, bias=b_pw1_f)
    hg = _glu_fwd(ag, "glu")
    hd = _conv_fwd(hg, w_dw_f, b_dw_f, "conv")
    hs = _ln_silu_fwd(hd, ln_g_f, ln_b_f, "ln_silu")
    h3, y1, xf1 = _mm(hs, W_pw2, "nn", "pw2", bias=b_pw2_f, res=h2, gate=gt1[1], raw_out=True,
                      norm=(g_ffn_r[1], sc2[1], sh2[1]))
    h4, f1, _, gu1, act1 = ffn_fwd(h3, xf1, 1, None)

    dh4, sq_err, dg_final, df1, dgt2_1, _ = _final_fwd_bwd(h4, g_fin, target, f1, gt2[1], "loss_head")
    loss_local = (0.5 / Dm) * sq_err[0, 0:1]

    def col_shards(g, n):
        return g.reshape(Dm, N_DEV, n).transpose(1, 0, 2).reshape(N_DEV * Dm, n)

    def exchange_begin(k, parts):
        return _push_begin(parts, False, f"exchange_start{k}")

    def zero_of(handle):
        return handle[4][0:1, 0:1]

    def ffn_bwd(df, xf, gu, act, l):
        dw_out = _mm_tn_shard_rows(act, df, f"ffn_out_dw{l}", BF16)
        dgu = _ffn_out_dx_swiglu(df, W_ffo[l], gu, f"ffn_out_dx{l}").reshape(N_DEV, T, FF_SHARD)
        dw_in = _mm_tn_shard_rows(dgu, xf, f"ffn_in_dw{l}", BF16)
        dxf = _mm_sum_shards(dgu, W_ffi[l], "nn", f"ffn_in_dx{l}", BF16, tm=256)
        return dw_in, dw_out, dxf

    dW_ffi1, dW_ffo1, dxf1 = ffn_bwd(df1, xf1, gu1, act1, 1)
    ex0 = exchange_begin(0, [dW_ffi1.reshape(2 * D_FF, Dm), dW_ffo1.reshape(D_FF, Dm)])
    dh3, da, dsh, dy1, dgt1_1, db_pw2 = _norm_mod_bwd(h3, g_ffn_r[1], sc2[1], dxf1, dh4, "norm_ffn_bwd1",
                                                       gate=(y1, gt1[1] + zero_of(ex0)))
    dmod_ffn1 = (dsh, da * g_ffn_r[1], dgt2_1)
    dg_ffn1 = da * (1.0 + sc2[1])

    dW_pw2 = _mm(hs, dy1, "tn", "pw2_dw", BF16, tk=2048)
    dhs = _mm(dy1, W_pw2, "nt", "pw2_dx", BF16)
    dhd, dln_g, dln_b, db_dw = _ln_silu_bwd(dhs, hd, ln_g_f, ln_b_f, "ln_silu_bwd")
    dhg, dw_dw = _conv_bwd(dhd, hg, w_dw_f, "conv_bwd")
    dag, db_pw1 = _glu_bwd(ag, dhg, "glu_bwd")
    dW_pw1 = _mm(xm1, dag, "tn", "pw1_dw", BF16, tk=2048)
    dxm1 = _mm(dag, W_pw1, "nt", "pw1_dx", BF16, tk=2048)
    ex1 = exchange_begin(1, [col_shards(dW_pw1, 2 * Dm // N_DEV), dW_pw2])
    dh2, da, dsh, df0, dgt2_0, _ = _norm_mod_bwd(h2, g_mix_r[1], sc1[1], dxm1, dh3, "norm_mix1_bwd",
                                                 gate=(f0, gt2[0] + zero_of(ex1)))
    dmod_mix1 = (dsh, da * g_mix_r[1], dgt1_1)
    dg_mix1 = da * (1.0 + sc1[1])

    dW_ffi0, dW_ffo0, dxf0 = ffn_bwd(df0, xf0, gu0, act0, 0)
    ex2 = exchange_begin(2, [dW_ffi0.reshape(2 * D_FF, Dm), dW_ffo0.reshape(D_FF, Dm)])
    dh1, da, dsh, dy0, dgt1_0, _ = _norm_mod_bwd(h1, g_ffn_r[0], sc2[0], dxf0, dh2, "norm_ffn_bwd0",
                                                 gate=(y0, gt1[0] + zero_of(ex2)))
    dmod_ffn0 = (dsh, da * g_ffn_r[0], dgt2_0)
    dg_ffn0 = da * (1.0 + sc2[0])

    dW_out = _mm(ao, dy0, "tn", "out_proj_dw", BF16, tk=2048)
    ex_out = exchange_begin(4, [dW_out])
    dao = _mm(dy0, W_out + zero_of(ex_out).astype(BF16), "nt", "out_proj_dx", BF16)
    dq, f_acc = _attn_bwd(qh, dao, ao, lse, kpad, vpad, kt, C, "attn_bwd")
    dP, dqg, dkg, dw_sp0, db_spt0 = _mix_prep_bwd(P, dq, f_acc, dao, C, cos, sin, qg, kg, bd, w_sp0, w_spt0,
                                                  b_spt0, "mix_prep_bwd")
    dW_in_t = _mm(dP, XM, "tn", "in_proj_dw", BF16, tm=896, tk=2176)
    dXM = _mm(dP, W_in_t, "nn", "in_proj_dx", BF16, tk=IN_WIDTH)
    dh0, da, dsh = _norm_mod_bwd(h0, g_mix_r[0], sc1[0], dXM, dh1, "norm_mix0_bwd", dxm_row_off=C)
    _, dac, dcsh = _norm_mod_bwd(ctx2, g_mix_r[0], csc1, dXM, None, "norm_ctx_bwd")
    dmod_mix0 = (dsh, da * g_mix_r[0], dgt1_0)
    dg_mix0 = da * (1.0 + sc1[0]) + dac * (1.0 + csc1)
    dcmod = jnp.concatenate([dcsh, dac * g_mix_r[0]], axis=1)

    dmod_mine = jnp.stack([jnp.concatenate(dmod_mix0 + dmod_ffn0, axis=1)[0],
                           jnp.concatenate(dmod_mix1 + dmod_ffn1, axis=1)[0]])

    small_grads = [
        ("loss", loss_local), ("g_final", dg_final), ("g_mix", jnp.concatenate([dg_mix0, dg_mix1])),
        ("g_ffn", jnp.concatenate([dg_ffn0, dg_ffn1])),
        ("q_gain", dqg[:, :HEAD_DIM] + dqg[:, HEAD_DIM:]), ("k_gain", dkg[:, :HEAD_DIM] + dkg[:, HEAD_DIM:]),
        ("w_sp", dw_sp0[None]), ("b_sp", db_spt0.T[None]), ("b_pw1", db_pw1), ("w_dw", dw_dw[None]),
        ("b_dw", db_dw), ("ln_g", dln_g), ("ln_b", dln_b), ("b_pw2", db_pw2), ("dcmod", dcmod),
        ("dmod", dmod_mine),
    ]
    buf3, offs3 = _pack([t for _, t in small_grads])
    got3 = _all_gather([buf3], "gather_small_grads", True)[0].reshape(N_DEV, buf3.shape[0], LANES)
    sum3 = _sum_devices(got3, "sum_small_grads").reshape(-1)
    off3 = {nm: off for (nm, _), off in zip(small_grads, offs3)}
    shape3 = {nm: t.shape for nm, t in small_grads}

    def summed(nm):
        return _take(sum3, off3[nm], shape3[nm])

    loss = summed("loss")[0]
    dcmod_sum = summed("dcmod")
    dmod_rows = _take(got3.reshape(N_DEV, -1), off3["dmod"], (2, 6 * Dm)).transpose(1, 0, 2)
    ctx_row = jnp.concatenate([jnp.pad(dcmod_sum, ((0, 0), (0, 4 * Dm))), jnp.zeros((1, 6 * Dm), F32)])
    dmod_all = jnp.concatenate([dmod_rows, ctx_row[:, None, :],
                                jnp.zeros((2, LANES - N_DEV - 1, 6 * Dm), F32)], axis=1)
    grads = {}
    grads["b_mod"] = summed("dmod") + ctx_row
    dmod_shard = lax.dynamic_slice_in_dim(dmod_all, me * n_mod, n_mod, axis=2)
    c_rows_t = jnp.pad(c_rows.T, ((0, 0), (0, LANES - MOD_ROWS)))
    grads["w_mod"], ds_part = _mod_bwd(c_rows_t, dmod_shard, w_mod, "mod_bwd")

    buf4, _ = _pack([ds_part[0, N_DEV]])
    got4 = _all_gather([buf4], "gather_c_ctx_grad", True)[0].reshape(N_DEV, buf4.shape[0], LANES)
    ds_ctx = _sum_devices(got4, "sum_c_ctx_grad").reshape(-1)[:Dm]
    behind_small = (ds_ctx[0:1] * 0.0).astype(BF16)
    ex3 = exchange_begin(3, [dW_in_t + behind_small])
    grads["c_ctx"] = ds_ctx * _dsilu(c_ctx) + zero_of(ex3)[0]

    for nm in ("g_final", "g_mix", "g_ffn", "q_gain", "k_gain", "w_sp", "b_sp"):
        grads[nm] = summed(nm).reshape(weights[nm].shape)
    for nm in ("b_pw1", "w_dw", "b_dw", "ln_g", "ln_b", "b_pw2"):
        n_loc = weights[nm].shape[-1]
        grads[nm] = lax.dynamic_slice_in_dim(summed(nm), me * n_loc, n_loc, axis=-1).reshape(weights[nm].shape)

    delta, new_m, new_v = {}, {}, {}
    shp = w_mod.shape
    outs = _adamw(w_mod.reshape(-1, shp[-1]), grads["w_mod"].reshape(-1, shp[-1]),
                  m_w_mod.reshape(-1, shp[-1]), v_w_mod.reshape(-1, shp[-1]), "adamw_w_mod")
    delta["w_mod"], new_m["w_mod"], new_v["w_mod"] = (o.reshape(shp) for o in outs)
    big_names = ("w_mod", "w_ffn_in", "w_ffn_out", "w_in", "w_out", "w_pw1", "w_pw2")
    small_names = [nm for nm in names if nm not in big_names]
    packs = [_pack([src[nm] for nm in small_names]) for src in (weights, grads, moments_m, moments_v)]
    offs_s = packs[0][1]
    outs = _adamw(*[pk[0] for pk in packs], "adamw_small")
    for o, dst in zip(outs, (delta, new_m, new_v)):
        o = o.reshape(-1)
        for nm, off in zip(small_names, offs_s):
            dst[nm] = _take(o, off, weights[nm].shape)

    def exchanged(k, handle, after):
        return _push_end(handle, after, f"exchange_wait{k}")[1]

    def adamw_big(nm, parts, transposed=False, wmv=None):
        w3, m3, v3 = wmv if wmv is not None else (weights[nm], moments_m[nm], moments_v[nm])
        outs4 = _adamw_recv(w3, m3, v3, parts, f"adamw_{nm}")
        if transposed:
            outs4 = [jnp.swapaxes(t, 1, 2) for t in outs4]
        grads[nm], delta[nm], new_m[nm], new_v[nm] = outs4

    r_ffi1, r_ffo1 = exchanged(0, ex0, outs[0])
    r_pw1, r_pw2 = exchanged(1, ex1, outs[0])
    r_ffi0, r_ffo0 = exchanged(2, ex2, outs[0])
    r_out, = exchanged(4, ex_out, outs[0])
    adamw_big("w_ffn_in", [r_ffi0, r_ffi1], True, (w_ffi_t, m_w_ffi_t, v_w_ffi_t))
    adamw_big("w_ffn_out", [r_ffo0, r_ffo1])
    adamw_big("w_pw1", [r_pw1])
    adamw_big("w_pw2", [r_pw2])
    adamw_big("w_out", [r_out])
    r_in, = exchanged(3, ex3, delta["w_out"])
    adamw_big("w_in", [r_in], True, (w_in_t, m_w_in_t, v_w_in_t))

    return (loss, dh0[None], *[grads[n] for n in names], *[delta[n] for n in names],
            *[new_m[n] for n in names], *[new_v[n] for n in names])
```

```python
import math

import jax
import jax.numpy as jnp
from jax import lax
from jax.experimental import pallas as pl
from jax.experimental.pallas import tpu as pltpu

F32 = jnp.float32
BF16 = jnp.bfloat16
MESH = pl.DeviceIdType.MESH

N_DEV = 8
D_MODEL = 1024
EPS = 1e-6
HEAD_DIM = 64
ATTN_WIDTH = 512
KV_WIDTH = 128
SG_WIDTH = 512
N_SG_GROUPS = 4
CHUNK = 128
IN_WIDTH = 1792
D_FF = 2816
FF_SHARD = 2 * D_FF // N_DEV
CONV_WIDTH = 31
CONV_HALO = 16
GRID_W = 64
ROPE_THETA = 10000.0
LANES = 128
SUBLANES = 8
ROW_BLOCK = 256
ADAM_LR, ADAM_B1, ADAM_B2, ADAM_EPS, ADAM_WD, ADAM_STEP = 0.001, 0.9, 0.999, 1e-08, 0.01, 10


def _tile(n, target, mult=LANES):
    best = None
    for t in range(mult, min(n, target) + 1, mult):
        if n % t == 0:
            best = t
    return best if best is not None else n


def _sigmoid(x):
    return 1.0 / (1.0 + jnp.exp(-x))


def _silu(x):
    return x * _sigmoid(x)


def _dsilu(x):
    s = _sigmoid(x)
    return s * (1.0 + x * (1.0 - s))


_GELU_K = math.sqrt(2.0 / math.pi)


def _gelu(x):
    return 0.5 * x * (1.0 + jnp.tanh(_GELU_K * (x + 0.044715 * x * x * x)))


def _gelu_and_grad(x):
    x2 = x * x
    t = jnp.tanh(_GELU_K * x * (1.0 + 0.044715 * x2))
    half = 0.5 * (1.0 + t)
    return x * half, half + 0.5 * x * (1.0 - t * t) * _GELU_K * (1.0 + 3.0 * 0.044715 * x2)


def _split_bf16(x):
    hi = x.astype(BF16)
    lo = (x - hi.astype(F32)).astype(BF16)
    return hi, lo


def _dot(a, b, dims):
    return lax.dot_general(a, b, (dims, ((), ())), preferred_element_type=F32)


def _dot3(a, b, dims):
    ah, al = _split_bf16(a)
    bh, bl = _split_bf16(b)
    return _dot(ah, bh, dims) + _dot(ah, bl, dims) + _dot(al, bh, dims)


NN = ((1,), (0,))
NT = ((1,), (1,))
TN = ((0,), (0,))


def _all_gather(xs, name, in_vmem):
    n_arr = len(xs)

    def body(*refs):
        x_refs, out_refs = refs[:n_arr], refs[n_arr:2 * n_arr]
        send_sems, recv_sems, local_sems = refs[2 * n_arr:]
        x, y, c = lax.axis_index("x"), lax.axis_index("y"), lax.axis_index("c")
        me, sibling = (x, y, c), (x, y, 1 - c)
        chips = [(1 - x, y), (x, 1 - y), (1 - x, 1 - y)]

        def rows(a, px, py, pc):
            m_per = xs[a].shape[0]
            return out_refs[a].at[pl.ds((4 * px + 2 * py + pc) * m_per, m_per), :]

        def copy(a, k, block, to, src=None):
            return pltpu.make_async_remote_copy(
                src_ref=rows(a, *block) if src is None else src,
                dst_ref=rows(a, *block),
                send_sem=send_sems.at[7 * a + k],
                recv_sem=recv_sems.at[7 * a + k],
                device_id=to,
                device_id_type=MESH,
            )

        mine, first, passed = [], [], []
        for a in range(n_arr):
            mine.append(pltpu.make_async_copy(x_refs[a], rows(a, *me), local_sems.at[a]))
            mine[-1].start()
            first.append(copy(a, 0, me, sibling, src=x_refs[a]))
            first += [copy(a, 1 + j, me, (*chip, c), src=x_refs[a]) for j, chip in enumerate(chips)]
        for cp in first:
            cp.start()
        for a in range(n_arr):
            for j, chip in enumerate(chips):
                copy(a, 1 + j, (*chip, c), me).wait_recv()
                passed.append(copy(a, 4 + j, (*chip, c), sibling))
                passed[-1].start()
        for a in range(n_arr):
            copy(a, 0, sibling, me).wait_recv()
            for j, chip in enumerate(chips):
                copy(a, 4 + j, (*chip, 1 - c), me).wait_recv()
        for cp in first + passed:
            cp.wait_send()
        for cp in mine:
            cp.wait()

    space = pltpu.VMEM if in_vmem else pl.ANY
    return pl.pallas_call(
        body,
        name=name,
        out_shape=[jax.ShapeDtypeStruct((N_DEV * t.shape[0], t.shape[1]), t.dtype) for t in xs],
        in_specs=[pl.BlockSpec(memory_space=space)] * n_arr,
        out_specs=[pl.BlockSpec(memory_space=space)] * n_arr,
        scratch_shapes=[
            pltpu.SemaphoreType.DMA((7 * n_arr,)),
            pltpu.SemaphoreType.DMA((7 * n_arr,)),
            pltpu.SemaphoreType.DMA((n_arr,)),
        ],
    )(*xs)


HBM_SPEC = pl.BlockSpec(memory_space=pltpu.HBM)
SEM_SPEC = pl.BlockSpec(memory_space=pltpu.SEMAPHORE)
DATAFLOW_EFFECT = pltpu.SideEffectType.DATAFLOW_SIDE_EFFECTING


def _peers(x, y, c):
    for k in range(1, N_DEV):
        px = 1 - x if (k >> 2) & 1 else x
        py = 1 - y if (k >> 1) & 1 else y
        pc = 1 - c if k & 1 else c
        yield k - 1, (px, py, pc), 4 * px + 2 * py + pc


def _push_copies(src_refs, land_refs, send_sems, recv_sems, shapes, whole_src):
    x, y, c = lax.axis_index("x"), lax.axis_index("y"), lax.axis_index("c")
    me = 4 * x + 2 * y + c
    for a, (m_per, _) in enumerate(shapes):
        def block(ref, idx, m_per=m_per):
            return ref.at[pl.ds(idx * m_per, m_per), :]

        for k, peer, pidx in _peers(x, y, c):
            src = src_refs[a] if whole_src else block(src_refs[a], pidx)
            sems = dict(send_sem=send_sems.at[N_DEV * a + k], recv_sem=recv_sems.at[N_DEV * a + k],
                        device_id=peer, device_id_type=MESH)
            yield (pltpu.make_async_remote_copy(src_ref=src, dst_ref=block(land_refs[a], me), **sems),
                   pltpu.make_async_remote_copy(src_ref=src, dst_ref=block(land_refs[a], pidx), **sems))


def _own_copies(src_refs, land_refs, recv_sems, shapes, whole_src):
    me = 4 * lax.axis_index("x") + 2 * lax.axis_index("y") + lax.axis_index("c")
    for a, (m_per, _) in enumerate(shapes):
        mine = pl.ds(me * m_per, m_per)
        src = src_refs[a] if whole_src else src_refs[a].at[mine, :]
        yield pltpu.make_async_copy(src, land_refs[a].at[mine, :], recv_sems.at[N_DEV * a + N_DEV - 1])


def _push_begin(srcs, whole_src, name):
    n_arr = len(srcs)
    shapes = [(t.shape[0] if whole_src else t.shape[0] // N_DEV, t.shape[1]) for t in srcs]
    lands = [lax.empty((N_DEV * m, n), t.dtype) for (m, n), t in zip(shapes, srcs)]

    def body(*refs):
        src_refs, land_refs = refs[:n_arr], refs[n_arr:2 * n_arr]
        send_sems, recv_sems = refs[2 * n_arr], refs[2 * n_arr + 1]
        token = refs[-1]
        for outgoing, _ in _push_copies(src_refs, land_refs, send_sems, recv_sems, shapes, whole_src):
            outgoing.start()
        for own in _own_copies(src_refs, land_refs, recv_sems, shapes, whole_src):
            own.start()
        token[...] = jnp.zeros_like(token)

    operands = [pltpu.with_memory_space_constraint(t, pltpu.HBM) for t in list(srcs) + lands]
    outs = pl.pallas_call(
        body, name=name,
        out_shape=(pltpu.SemaphoreType.DMA((N_DEV * n_arr,)), pltpu.SemaphoreType.DMA((N_DEV * n_arr,)),
                   *[pltpu.HBM(t.shape, t.dtype) for t in operands],
                   jax.ShapeDtypeStruct((SUBLANES, LANES), F32)),
        in_specs=[HBM_SPEC] * (2 * n_arr),
        out_specs=(SEM_SPEC, SEM_SPEC, *[HBM_SPEC] * (2 * n_arr), pl.BlockSpec(memory_space=pltpu.VMEM)),
        input_output_aliases={i: 2 + i for i in range(2 * n_arr)},
        compiler_params=pltpu.CompilerParams(has_side_effects=DATAFLOW_EFFECT),
    )(*operands)
    return outs[0], outs[1], list(outs[2:2 + n_arr]), list(outs[2 + n_arr:2 + 2 * n_arr]), outs[-1], whole_src


def _push_end(handle, after, name):
    send_sems, recv_sems, srcs, lands, _, whole_src = handle
    n_arr = len(srcs)
    shapes = [(t.shape[0] // N_DEV, t.shape[1]) for t in lands]

    def body(*refs):
        src_refs, land_refs = refs[:n_arr], refs[n_arr:2 * n_arr]
        send_sems_ref, recv_sems_ref = refs[2 * n_arr], refs[2 * n_arr + 1]
        for outgoing, incoming in _push_copies(src_refs, land_refs, send_sems_ref, recv_sems_ref, shapes, whole_src):
            outgoing.wait_send()
            incoming.wait_recv()
        for own in _own_copies(src_refs, land_refs, recv_sems_ref, shapes, whole_src):
            own.wait()

    outs = pl.pallas_call(
        body, name=name,
        out_shape=tuple(pltpu.HBM(t.shape, t.dtype) for t in srcs + lands),
        in_specs=[HBM_SPEC] * (2 * n_arr) + [SEM_SPEC, SEM_SPEC, pl.BlockSpec(memory_space=pl.ANY)],
        out_specs=tuple([HBM_SPEC] * (2 * n_arr)),
        input_output_aliases={i: i for i in range(2 * n_arr)},
        compiler_params=pltpu.CompilerParams(has_side_effects=DATAFLOW_EFFECT),
    )(*srcs, *lands, send_sems, recv_sems, after)
    return list(outs[:n_arr]), list(outs[n_arr:])


def _sum_devices(r, name, rows_per_step=ROW_BLOCK):
    _, m, n = r.shape
    tm = _tile(m, rows_per_step, 8)

    def body(r_ref, o_ref):
        acc = r_ref[0].astype(F32)
        for s in range(1, N_DEV):
            acc = acc + r_ref[s].astype(F32)
        o_ref[...] = acc

    return pl.pallas_call(
        body,
        name=name,
        grid=(m // tm,),
        out_shape=jax.ShapeDtypeStruct((m, n), F32),
        in_specs=[pl.BlockSpec((N_DEV, tm, n), lambda i: (0, i, 0))],
        out_specs=pl.BlockSpec((tm, n), lambda i: (i, 0)),
        compiler_params=pltpu.CompilerParams(dimension_semantics=("parallel",)),
    )(r)


def _get(ref):
    return ref[0] if len(ref.shape) == 3 else ref[...]


def _put(ref, val):
    if len(ref.shape) == 3:
        ref[0] = val
    else:
        ref[...] = val


def _norm_mod(hv, g, sc, sh):
    r = lax.rsqrt(jnp.mean(hv * hv, axis=-1, keepdims=True) + EPS)
    return (hv * r) * g * (1.0 + sc) + sh


def _mm_call(name, a, b, a_spec, b_spec, out_sds, o_spec, grid, dims, acc_shape, bias=None,
             res=None, gate=None, raw_out=False, vec_spec=None, norm=None):
    nk = grid[2]
    operands, in_specs = [a, b], [a_spec, b_spec]
    if bias is not None:
        operands.append(bias)
        in_specs.append(vec_spec)
    if res is not None:
        operands += [res, gate]
        in_specs += [o_spec, vec_spec]
    if norm is not None:
        assert grid[1] == 1
        operands += list(norm)
        in_specs += [vec_spec] * 3
    out_shape, out_specs = [out_sds], [o_spec]
    if raw_out:
        out_shape.append(jax.ShapeDtypeStruct(out_sds.shape, BF16))
        out_specs.append(o_spec)
    if norm is not None:
        out_shape.append(jax.ShapeDtypeStruct(out_sds.shape, BF16))
        out_specs.append(o_spec)

    def body(*refs):
        it = iter(refs)
        a_ref, b_ref = next(it), next(it)
        bias_ref = next(it) if bias is not None else None
        res_ref, gate_ref = (next(it), next(it)) if res is not None else (None, None)
        norm_refs = (next(it), next(it), next(it)) if norm is not None else None
        o_ref = next(it)
        raw_ref = next(it) if raw_out else None
        xn_ref = next(it) if norm is not None else None
        acc = next(it) if nk > 1 else None
        k = pl.program_id(2)
        part = _dot(_get(a_ref).astype(BF16), _get(b_ref).astype(BF16), dims)

        def finish(y):
            if bias_ref is not None:
                y = y + bias_ref[...]
            if raw_ref is not None:
                raw_ref[...] = y.astype(BF16)
            if res_ref is not None:
                y = res_ref[...] + gate_ref[...] * y
            _put(o_ref, y.astype(out_sds.dtype))
            if xn_ref is not None:
                xn_ref[...] = _norm_mod(y, *[r[...] for r in norm_refs]).astype(BF16)

        if nk == 1:
            finish(part)
        else:
            @pl.when(k == 0)
            def _():
                acc[...] = part

            @pl.when(k > 0)
            def _():
                acc[...] += part

            @pl.when(k == nk - 1)
            def _():
                finish(acc[...])

    outs = pl.pallas_call(
        body,
        name=name,
        grid=grid,
        out_shape=out_shape,
        in_specs=in_specs,
        out_specs=out_specs,
        scratch_shapes=[pltpu.VMEM(acc_shape, F32)] if nk > 1 else [],
        compiler_params=pltpu.CompilerParams(dimension_semantics=("parallel", "parallel", "arbitrary")),
    )(*operands)
    return outs if len(outs) > 1 else outs[0]


def _mm(a, b, mode, name, out_dtype=F32, bias=None, res=None, gate=None, raw_out=False,
        tm=512, tn=1024, tk=1024, a_row_off=0, norm=None):
    if mode == "nn":
        K, N = b.shape
        M = a.shape[0] - a_row_off
    elif mode == "nt":
        N, K = b.shape
        M = a.shape[0] - a_row_off
    else:
        (K, M), N = a.shape, b.shape[1]
    tm, tn, tk = _tile(M, tm), _tile(N, tn), _tile(K, tk)
    off = a_row_off // tm
    dims = {"nn": NN, "nt": NT, "tn": TN}[mode]
    a_spec = (pl.BlockSpec((tk, tm), lambda i, j, k: (k, i)) if mode == "tn"
              else pl.BlockSpec((tm, tk), lambda i, j, k: (i + off, k)))
    b_spec = (pl.BlockSpec((tn, tk), lambda i, j, k: (j, k)) if mode == "nt"
              else pl.BlockSpec((tk, tn), lambda i, j, k: (k, j)))
    return _mm_call(name, a, b, a_spec, b_spec, jax.ShapeDtypeStruct((M, N), out_dtype),
                    pl.BlockSpec((tm, tn), lambda i, j, k: (i, j)), (M // tm, N // tn, K // tk), dims,
                    (tm, tn), bias, res, gate, raw_out, pl.BlockSpec((1, tn), lambda i, j, k: (0, j)), norm)


def _mm_sum_shards(a3, b3, mode, name, out_dtype=F32, res=None, gate=None, raw_out=False, tm=512, norm=None):
    S, M, kk = a3.shape
    N = b3.shape[2] if mode == "nn" else b3.shape[1]
    tm = _tile(M, tm)
    dims = NN if mode == "nn" else NT
    has_res = res is not None

    def body(*refs):
        it = iter(refs)
        a_ref, b_ref = next(it), next(it)
        res_ref, gate_ref = (next(it), next(it)) if has_res else (None, None)
        norm_refs = (next(it), next(it), next(it)) if norm is not None else None
        o_ref = next(it)
        raw_ref = next(it) if raw_out else None
        xn_ref = next(it) if norm is not None else None
        y = _dot(a_ref[0], b_ref[0], dims)
        for s in range(1, S):
            y = y + _dot(a_ref[s], b_ref[s], dims)
        if raw_ref is not None:
            raw_ref[...] = y.astype(BF16)
        if has_res:
            y = res_ref[...] + gate_ref[...] * y
        o_ref[...] = y.astype(out_dtype)
        if xn_ref is not None:
            xn_ref[...] = _norm_mod(y, *[r[...] for r in norm_refs]).astype(BF16)

    tile = pl.BlockSpec((tm, N), lambda i: (i, 0))
    operands = [a3, b3] + ([res, gate] if has_res else []) + (list(norm) if norm is not None else [])
    in_specs = [pl.BlockSpec((S, tm, kk), lambda i: (0, i, 0)), pl.BlockSpec(b3.shape, lambda i: (0, 0, 0))]
    in_specs += [tile, _vec_spec(N)] if has_res else []
    in_specs += [_vec_spec(N)] * 3 if norm is not None else []
    out_shape = [jax.ShapeDtypeStruct((M, N), out_dtype)] + ([jax.ShapeDtypeStruct((M, N), BF16)] if raw_out else [])
    out_shape += [jax.ShapeDtypeStruct((M, N), BF16)] if norm is not None else []
    outs = pl.pallas_call(
        body, name=name, grid=(M // tm,),
        out_shape=out_shape, in_specs=in_specs, out_specs=[tile] * len(out_shape),
        compiler_params=pltpu.CompilerParams(dimension_semantics=("parallel",)),
    )(*operands)
    return outs if len(outs) > 1 else outs[0]


def _mm_tn_shard_rows(a3, b, name, out_dtype, tn=1024, tk=4096):
    S, T, m = a3.shape
    N = b.shape[1]
    tn, tk = _tile(N, tn), _tile(T, tk)
    return _mm_call(name, a3, b, pl.BlockSpec((1, tk, m), lambda i, j, k: (i, k, 0)),
                    pl.BlockSpec((tk, tn), lambda i, j, k: (k, j)), jax.ShapeDtypeStruct((S, m, N), out_dtype),
                    pl.BlockSpec((1, m, tn), lambda i, j, k: (i, 0, j)), (S, N // tn, T // tk), TN, (m, tn))


def _row_spec(tm, width, off=0):
    return pl.BlockSpec((tm, width), lambda i: (i + off, 0))


def _vec_spec(width):
    return pl.BlockSpec((1, width), lambda i: (0, 0))


def _norm_mod_fwd_cat(hc, h, g, csc, csh, sc, sh, name):
    (C, Dm), T = hc.shape, h.shape[0]
    tm = _tile(math.gcd(C, T), ROW_BLOCK, 8)
    off = C // tm

    def body(hc_ref, h_ref, g_ref, csc_ref, csh_ref, sc_ref, sh_ref, o_ref):
        is_ctx = pl.program_id(0) < off
        hv = jnp.where(is_ctx, hc_ref[...], h_ref[...])
        scv = jnp.where(is_ctx, csc_ref[...], sc_ref[...])
        shv = jnp.where(is_ctx, csh_ref[...], sh_ref[...])
        r = lax.rsqrt(jnp.mean(hv * hv, axis=-1, keepdims=True) + EPS)
        o_ref[...] = ((hv * r) * g_ref[...] * (1.0 + scv) + shv).astype(BF16)

    return pl.pallas_call(
        body, name=name, grid=((C + T) // tm,),
        out_shape=jax.ShapeDtypeStruct((C + T, Dm), BF16),
        in_specs=[pl.BlockSpec((tm, Dm), lambda i: (jnp.minimum(i, off - 1), 0)),
                  pl.BlockSpec((tm, Dm), lambda i: (jnp.maximum(i - off, 0), 0))] + [_vec_spec(Dm)] * 5,
        out_specs=_row_spec(tm, Dm),
        compiler_params=pltpu.CompilerParams(dimension_semantics=("parallel",)),
    )(hc, h, g, csc, csh, sc, sh)


def _gate_grads(dh, y_ref, gt_ref, dy_ref, dgt_ref, dsum_ref):
    dy = dh * gt_ref[...]
    dgt_ref[...] += jnp.sum(dh * y_ref[...].astype(F32), axis=0, keepdims=True)
    dsum_ref[...] += jnp.sum(dy, axis=0, keepdims=True)
    dy_ref[...] = dy.astype(BF16)


def _norm_mod_bwd(h, g, sc, dxm, dres, name, dxm_row_off=0, gate=None):
    R, Dm = h.shape
    tm = _tile(R, ROW_BLOCK, 8)
    off = dxm_row_off // tm
    has_res = dres is not None
    has_gate = gate is not None

    def body(*refs):
        it = iter(refs)
        h_ref, g_ref, sc_ref, dx_ref = next(it), next(it), next(it), next(it)
        dres_ref = next(it) if has_res else None
        y_ref, gt_ref = (next(it), next(it)) if has_gate else (None, None)
        dh_ref, da_ref, dsh_ref = next(it), next(it), next(it)
        gate_out = (next(it), next(it), next(it)) if has_gate else ()
        i = pl.program_id(0)

        @pl.when(i == 0)
        def _():
            for ref in (da_ref, dsh_ref) + gate_out[1:]:
                ref[...] = jnp.zeros_like(ref)

        hv = h_ref[...]
        dx = dx_ref[...].astype(F32)
        r = lax.rsqrt(jnp.mean(hv * hv, axis=-1, keepdims=True) + EPS)
        n = hv * r
        da_ref[...] += jnp.sum(dx * n, axis=0, keepdims=True)
        dsh_ref[...] += jnp.sum(dx, axis=0, keepdims=True)
        dn = dx * (g_ref[...] * (1.0 + sc_ref[...]))
        dh = r * (dn - n * jnp.mean(dn * n, axis=-1, keepdims=True))
        if has_res:
            dh = dh + dres_ref[...]
        dh_ref[...] = dh
        if has_gate:
            _gate_grads(dh, y_ref, gt_ref, *gate_out)

    operands = [h, g, sc, dxm] + ([dres] if has_res else []) + (list(gate) if has_gate else [])
    in_specs = [_row_spec(tm, Dm), _vec_spec(Dm), _vec_spec(Dm), _row_spec(tm, Dm, off)]
    in_specs += [_row_spec(tm, Dm)] if has_res else []
    in_specs += [_row_spec(tm, Dm), _vec_spec(Dm)] if has_gate else []
    vec = jax.ShapeDtypeStruct((1, Dm), F32)
    out_shape = [jax.ShapeDtypeStruct((R, Dm), F32), vec, vec]
    out_specs = [_row_spec(tm, Dm), _vec_spec(Dm), _vec_spec(Dm)]
    if has_gate:
        out_shape += [jax.ShapeDtypeStruct((R, Dm), BF16), vec, vec]
        out_specs += [_row_spec(tm, Dm), _vec_spec(Dm), _vec_spec(Dm)]
    return pl.pallas_call(
        body, name=name, grid=(R // tm,),
        out_shape=out_shape, in_specs=in_specs, out_specs=out_specs,
        compiler_params=pltpu.CompilerParams(dimension_semantics=("arbitrary",)),
    )(*operands)


def _ffn_in_swiglu(xf, w3, name, tm=1024):
    T, K = xf.shape
    S, n, _ = w3.shape
    half = S // 2
    tm = _tile(T, tm)

    def body(a_ref, wg_ref, wu_ref, gu_ref, act_ref):
        a = a_ref[...]
        g = _dot(a, wg_ref[0], NT)
        u = _dot(a, wu_ref[0], NT)
        gu_ref[0, 0] = g.astype(BF16)
        gu_ref[1, 0] = u.astype(BF16)
        act_ref[0] = (_silu(g) * u).astype(BF16)

    return pl.pallas_call(
        body, name=name, grid=(T // tm, half),
        out_shape=[jax.ShapeDtypeStruct((2, half, T, n), BF16), jax.ShapeDtypeStruct((half, T, n), BF16)],
        in_specs=[pl.BlockSpec((tm, K), lambda i, j: (i, 0)),
                  pl.BlockSpec((1, n, K), lambda i, j: (j, 0, 0)),
                  pl.BlockSpec((1, n, K), lambda i, j: (j + half, 0, 0))],
        out_specs=[pl.BlockSpec((2, 1, tm, n), lambda i, j: (0, j, i, 0)),
                   pl.BlockSpec((1, tm, n), lambda i, j: (j, i, 0))],
        compiler_params=pltpu.CompilerParams(dimension_semantics=("parallel", "parallel")),
    )(xf, w3, w3)


def _ffn_out_dx_swiglu(df, wo, gu, name, tm=1024):
    T, Dm = df.shape
    half, n, _ = wo.shape
    tm = _tile(T, tm)

    def body(df_ref, w_ref, gu_ref, o_ref):
        da = _dot(df_ref[...], w_ref[0], NT)
        g = gu_ref[0, 0].astype(F32)
        u = gu_ref[1, 0].astype(F32)
        s = _sigmoid(g)
        o_ref[0, 0] = (da * u * (s * (1.0 + g * (1.0 - s)))).astype(BF16)
        o_ref[1, 0] = (da * (g * s)).astype(BF16)

    gu_spec = pl.BlockSpec((2, 1, tm, n), lambda i, j: (0, j, i, 0))
    return pl.pallas_call(
        body, name=name, grid=(T // tm, half),
        out_shape=jax.ShapeDtypeStruct(gu.shape, BF16),
        in_specs=[pl.BlockSpec((tm, Dm), lambda i, j: (i, 0)),
                  pl.BlockSpec((1, n, Dm), lambda i, j: (j, 0, 0)), gu_spec],
        out_specs=gu_spec,
        compiler_params=pltpu.CompilerParams(dimension_semantics=("parallel", "parallel")),
    )(df, wo, gu)


def _glu_fwd(ag, name):
    R = ag.shape[0]
    tm = _tile(R, ROW_BLOCK, 8)

    def body(ag_ref, o_ref):
        o_ref[...] = ag_ref[:, :D_MODEL].astype(F32) * _sigmoid(ag_ref[:, D_MODEL:].astype(F32))

    return pl.pallas_call(
        body, name=name, grid=(R // tm,),
        out_shape=jax.ShapeDtypeStruct((R, D_MODEL), F32),
        in_specs=[_row_spec(tm, 2 * D_MODEL)],
        out_specs=_row_spec(tm, D_MODEL),
        compiler_params=pltpu.CompilerParams(dimension_semantics=("parallel",)),
    )(ag)


def _glu_bwd(ag, dhg, name):
    R = ag.shape[0]
    tm = _tile(R, ROW_BLOCK, 8)

    def body(ag_ref, dh_ref, o_ref, s_ref):
        i = pl.program_id(0)

        @pl.when(i == 0)
        def _():
            s_ref[...] = jnp.zeros_like(s_ref)

        a = ag_ref[:, :D_MODEL].astype(F32)
        s = _sigmoid(ag_ref[:, D_MODEL:].astype(F32))
        dh = dh_ref[...]
        da = dh * s
        dg = dh * a * s * (1.0 - s)
        o_ref[:, :D_MODEL] = da.astype(BF16)
        o_ref[:, D_MODEL:] = dg.astype(BF16)
        s_ref[:, :D_MODEL] += jnp.sum(da, axis=0, keepdims=True)
        s_ref[:, D_MODEL:] += jnp.sum(dg, axis=0, keepdims=True)

    return pl.pallas_call(
        body, name=name, grid=(R // tm,),
        out_shape=[jax.ShapeDtypeStruct((R, 2 * D_MODEL), BF16), jax.ShapeDtypeStruct((1, 2 * D_MODEL), F32)],
        in_specs=[_row_spec(tm, 2 * D_MODEL), _row_spec(tm, D_MODEL)],
        out_specs=[_row_spec(tm, 2 * D_MODEL), _vec_spec(2 * D_MODEL)],
        compiler_params=pltpu.CompilerParams(dimension_semantics=("arbitrary",)),
    )(ag, dhg)


def _halo_specs(tm, nblk, width):
    per = tm // CONV_HALO
    prev = pl.BlockSpec((CONV_HALO, width), lambda i: (jnp.maximum(i * per - 1, 0), 0))
    nxt = pl.BlockSpec((CONV_HALO, width), lambda i: (jnp.minimum((i + 1) * per, nblk * per - 1), 0))
    return prev, nxt


def _fill_halo(scr, prev_ref, cur_ref, next_ref, i, nblk, tm):
    scr[0:CONV_HALO, :] = jnp.where(i > 0, prev_ref[...], 0.0)
    scr[CONV_HALO:CONV_HALO + tm, :] = cur_ref[...]
    scr[CONV_HALO + tm:2 * CONV_HALO + tm, :] = jnp.where(i < nblk - 1, next_ref[...], 0.0)


CONV_ROWS = 128


CONV_REACH = (CONV_WIDTH // SUBLANES) * SUBLANES


def _windows(scr, stage, cols, tm):
    for r in range(SUBLANES):
        if r:
            stage[r] = scr[pl.ds(r, tm + CONV_REACH), cols]
        for a in range(CONV_REACH // SUBLANES + 1):
            off = SUBLANES * a + r
            if 1 <= off <= CONV_WIDTH:
                yield off, (stage[r, SUBLANES * a:SUBLANES * a + tm, :] if r
                            else scr[SUBLANES * a:SUBLANES * a + tm, cols])


def _conv_fwd(hg, w_dw, b_dw, name):
    R, Dm = hg.shape
    tm = _tile(R, CONV_ROWS, CONV_HALO)
    nblk = R // tm
    prev_spec, next_spec = _halo_specs(tm, nblk, Dm)

    def body(prev_ref, cur_ref, next_ref, w_ref, bdw_ref, hd_ref, scr, stage):
        _fill_halo(scr, prev_ref, cur_ref, next_ref, pl.program_id(0), nblk, tm)
        for cb in range(Dm // LANES):
            cols = slice(cb * LANES, (cb + 1) * LANES)
            acc = jnp.zeros((tm, LANES), F32) + bdw_ref[:, cols]
            for off, win in _windows(scr, stage, cols, tm):
                acc = acc + w_ref[off - 1:off, cols] * win
            hd_ref[:, cols] = acc

    return pl.pallas_call(
        body, name=name, grid=(nblk,),
        out_shape=jax.ShapeDtypeStruct((R, Dm), F32),
        in_specs=[prev_spec, _row_spec(tm, Dm), next_spec,
                  pl.BlockSpec((CONV_WIDTH, Dm), lambda i: (0, 0)), _vec_spec(Dm)],
        out_specs=_row_spec(tm, Dm),
        scratch_shapes=[pltpu.VMEM((tm + 2 * CONV_HALO, Dm), F32),
                        pltpu.VMEM((SUBLANES, tm + CONV_REACH, LANES), F32)],
        compiler_params=pltpu.CompilerParams(dimension_semantics=("parallel",)),
    )(hg, hg, hg, w_dw, b_dw)


def _ln_silu_fwd(hd, ln_g, ln_b, name):
    R, Dm = hd.shape
    tm = _tile(R, ROW_BLOCK, 8)

    def body(hd_ref, g_ref, b_ref, hs_ref):
        hd = hd_ref[...]
        xc = hd - jnp.mean(hd, axis=-1, keepdims=True)
        rs = lax.rsqrt(jnp.mean(xc * xc, axis=-1, keepdims=True) + EPS)
        hs_ref[...] = _silu(xc * rs * g_ref[...] + b_ref[...]).astype(BF16)

    return pl.pallas_call(
        body, name=name, grid=(R // tm,),
        out_shape=jax.ShapeDtypeStruct((R, Dm), BF16),
        in_specs=[_row_spec(tm, Dm), _vec_spec(Dm), _vec_spec(Dm)],
        out_specs=_row_spec(tm, Dm),
        compiler_params=pltpu.CompilerParams(dimension_semantics=("parallel",)),
    )(hd, ln_g, ln_b)


def _ln_silu_bwd(dhs, hd, ln_g, ln_b, name):
    R, Dm = hd.shape
    tm = _tile(R, ROW_BLOCK, 8)

    def body(dhs_ref, hd_ref, g_ref, b_ref, dhd_ref, dg_ref, db_ref, dsum_ref):
        i = pl.program_id(0)

        @pl.when(i == 0)
        def _():
            dg_ref[...] = jnp.zeros_like(dg_ref)
            db_ref[...] = jnp.zeros_like(db_ref)
            dsum_ref[...] = jnp.zeros_like(dsum_ref)

        hd = hd_ref[...]
        mu = jnp.mean(hd, axis=-1, keepdims=True)
        xc = hd - mu
        rs = lax.rsqrt(jnp.mean(xc * xc, axis=-1, keepdims=True) + EPS)
        z = xc * rs
        hl = z * g_ref[...] + b_ref[...]
        dhl = dhs_ref[...].astype(F32) * _dsilu(hl)
        dg_ref[...] += jnp.sum(dhl * z, axis=0, keepdims=True)
        db_ref[...] += jnp.sum(dhl, axis=0, keepdims=True)
        dz = dhl * g_ref[...]
        dhd = rs * (dz - jnp.mean(dz, axis=-1, keepdims=True) - z * jnp.mean(dz * z, axis=-1, keepdims=True))
        dsum_ref[...] += jnp.sum(dhd, axis=0, keepdims=True)
        dhd_ref[...] = dhd

    return pl.pallas_call(
        body, name=name, grid=(R // tm,),
        out_shape=[jax.ShapeDtypeStruct((R, Dm), F32)] + [jax.ShapeDtypeStruct((1, Dm), F32)] * 3,
        in_specs=[_row_spec(tm, Dm), _row_spec(tm, Dm), _vec_spec(Dm), _vec_spec(Dm)],
        out_specs=[_row_spec(tm, Dm), _vec_spec(Dm), _vec_spec(Dm), _vec_spec(Dm)],
        compiler_params=pltpu.CompilerParams(dimension_semantics=("arbitrary",)),
    )(dhs, hd, ln_g, ln_b)


def _conv_bwd(dhd, hg, w_dw, name):
    R, Dm = hg.shape
    tm = _tile(R, CONV_ROWS, CONV_HALO)
    nblk = R // tm
    prev_spec, next_spec = _halo_specs(tm, nblk, Dm)

    def body(dprev, dcur, dnext, gprev, gcur, gnext, w_ref, dhg_ref, dw_ref, dscr, gscr, dwp, stage):
        i = pl.program_id(0)

        @pl.when(i == 0)
        def _():
            dwp[...] = jnp.zeros_like(dwp)

        _fill_halo(dscr, dprev, dcur, dnext, i, nblk, tm)
        _fill_halo(gscr, gprev, gcur, gnext, i, nblk, tm)
        for cb in range(Dm // LANES):
            cols = slice(cb * LANES, (cb + 1) * LANES)
            acc = jnp.zeros((tm, LANES), F32)
            for off, win in _windows(dscr, stage, cols, tm):
                j = CONV_WIDTH - off
                acc = acc + w_ref[j:j + 1, cols] * win
            dhg_ref[:, cols] = acc
            d_here = dcur[:, cols]
            for off, win in _windows(gscr, stage, cols, tm):
                j = off - 1
                prod = d_here * win
                part = prod[0:SUBLANES]
                for k in range(1, tm // SUBLANES):
                    part = part + prod[k * SUBLANES:(k + 1) * SUBLANES]
                dwp[j * SUBLANES:(j + 1) * SUBLANES, cols] += part

        @pl.when(i == nblk - 1)
        def _():
            for j in range(CONV_WIDTH):
                dw_ref[j:j + 1, :] = jnp.sum(dwp[j * SUBLANES:(j + 1) * SUBLANES, :], axis=0, keepdims=True)

    return pl.pallas_call(
        body, name=name, grid=(nblk,),
        out_shape=[jax.ShapeDtypeStruct((R, Dm), F32), jax.ShapeDtypeStruct((CONV_WIDTH, Dm), F32)],
        in_specs=[prev_spec, _row_spec(tm, Dm), next_spec, prev_spec, _row_spec(tm, Dm), next_spec,
                  pl.BlockSpec((CONV_WIDTH, Dm), lambda i: (0, 0))],
        out_specs=[_row_spec(tm, Dm), pl.BlockSpec((CONV_WIDTH, Dm), lambda i: (0, 0))],
        scratch_shapes=[pltpu.VMEM((tm + 2 * CONV_HALO, Dm), F32)] * 2
        + [pltpu.VMEM((CONV_WIDTH * SUBLANES, Dm), F32), pltpu.VMEM((SUBLANES, tm + CONV_REACH, LANES), F32)],
        compiler_params=pltpu.CompilerParams(dimension_semantics=("arbitrary",)),
    )(dhd, dhd, dhd, hg, hg, hg, w_dw)


def _swap16(y, lane):
    return jnp.where((lane & 16) == 0, pltpu.roll(y, LANES - 16, 1), pltpu.roll(y, 16, 1))


def _head_mean(v, bd):
    hi, lo = _split_bf16(v)
    return (_dot(hi, bd, NN) + _dot(lo, bd, NN)) * (1.0 / HEAD_DIM)


Q_COLS = (0, ATTN_WIDTH)
K_COLS = (ATTN_WIDTH, ATTN_WIDTH + HEAD_DIM * 2)
V_COLS = (K_COLS[1], K_COLS[1] + HEAD_DIM * 2)
SU_COLS = (V_COLS[1], V_COLS[1] + SG_WIDTH)
SV_COLS = (SU_COLS[1], SU_COLS[1] + SG_WIDTH)


def _mix_prep_fwd(p, ctx_rows, cos, sin, qg, kg, bd, w_sp, b_spt, name):
    TT = p.shape[0]
    off = ctx_rows // CHUNK
    q_scale = HEAD_DIM ** -0.5

    def body(p_ref, cos_ref, sin_ref, qg_ref, kg_ref, bd_ref, w_ref, b_ref,
             q_ref, kp_ref, vp_ref, kt_ref, sg_ref):
        lane = lax.broadcasted_iota(jnp.int32, (CHUNK, LANES), 1)
        low = lane < HEAD_DIM
        cs, sn, bdv = cos_ref[...], sin_ref[...], bd_ref[...]

        def norm_rope(xv, gain):
            r = lax.rsqrt(_head_mean(xv * xv, bdv) + EPS)
            yv = xv * r * gain
            return yv * cs + _swap16(yv, lane) * sn

        def pad_heads(ref, t):
            tr = pltpu.roll(t, HEAD_DIM, 1)
            ref[0, 0] = jnp.where(low, t, 0.0).astype(BF16)
            ref[0, 1] = jnp.where(low, 0.0, tr).astype(BF16)
            ref[1, 0] = jnp.where(low, tr, 0.0).astype(BF16)
            ref[1, 1] = jnp.where(low, 0.0, t).astype(BF16)

        for a in range(ATTN_WIDTH // LANES):
            xv = p_ref[:, a * LANES:(a + 1) * LANES]
            q_ref[:, a * LANES:(a + 1) * LANES] = (norm_rope(xv, qg_ref[...]) * q_scale).astype(BF16)
        kh = norm_rope(p_ref[:, K_COLS[0]:K_COLS[1]], kg_ref[...])
        pad_heads(kp_ref, kh)
        pad_heads(vp_ref, p_ref[:, V_COLS[0]:V_COLS[1]])
        kht = kh.T
        kt_ref[0] = kht[:HEAD_DIM].astype(BF16)
        kt_ref[1] = kht[HEAD_DIM:].astype(BF16)
        for g in range(N_SG_GROUPS):
            u = _gelu(p_ref[:, SU_COLS[0] + g * LANES:SU_COLS[0] + (g + 1) * LANES])
            vg = _gelu(p_ref[:, SV_COLS[0] + g * LANES:SV_COLS[0] + (g + 1) * LANES])
            xc = vg - jnp.mean(vg, axis=-1, keepdims=True)
            vn = xc * lax.rsqrt(jnp.mean(xc * xc, axis=-1, keepdims=True) + EPS)
            mixed = _dot(w_ref[g].astype(BF16), vn.astype(BF16), NN) + b_ref[:, g:g + 1]
            sg_ref[:, g * LANES:(g + 1) * LANES] = (u * mixed).astype(BF16)

    def row(width):
        return pl.BlockSpec((CHUNK, width), lambda i: (i, 0))

    def whole(shape):
        return pl.BlockSpec(shape, lambda i: (0,) * len(shape))

    pad_spec = pl.BlockSpec((2, 2, CHUNK, LANES), lambda i: (0, 0, i, 0))
    return pl.pallas_call(
        body, name=name, grid=(TT // CHUNK,),
        out_shape=[jax.ShapeDtypeStruct((TT, ATTN_WIDTH), BF16),
                   jax.ShapeDtypeStruct((2, 2, TT, LANES), BF16), jax.ShapeDtypeStruct((2, 2, TT, LANES), BF16),
                   jax.ShapeDtypeStruct((2, HEAD_DIM, TT), BF16),
                   jax.ShapeDtypeStruct((TT - ctx_rows, ATTN_WIDTH + SG_WIDTH), BF16)],
        in_specs=[row(IN_WIDTH), row(LANES), row(LANES), whole((1, LANES)), whole((1, LANES)),
                  whole((LANES, LANES)), whole((N_SG_GROUPS, CHUNK, CHUNK)), whole((CHUNK, N_SG_GROUPS))],
        out_specs=[row(ATTN_WIDTH), pad_spec, pad_spec,
                   pl.BlockSpec((2, HEAD_DIM, CHUNK), lambda i: (0, 0, i)),
                   pl.BlockSpec((CHUNK, SG_WIDTH), lambda i: (jnp.maximum(i - off, 0), 1))],
        compiler_params=pltpu.CompilerParams(dimension_semantics=("arbitrary",)),
    )(p, cos, sin, qg, kg, bd, w_sp, b_spt)


def _mix_prep_bwd(p, dq, f, dao, ctx_rows, cos, sin, qg, kg, bd, w_sp, w_spt, b_spt, name):
    TT = p.shape[0]
    off = ctx_rows // CHUNK
    q_scale = HEAD_DIM ** -0.5

    def body(p_ref, dq_ref, f_ref, dsg_ref, cos_ref, sin_ref, qg_ref, kg_ref, bd_ref, w_ref, wt_ref,
             b_ref, dp_ref, dqg_ref, dkg_ref, dw_ref, db_ref):
        i = pl.program_id(0)

        @pl.when(i == 0)
        def _():
            dqg_ref[...] = jnp.zeros_like(dqg_ref)
            dkg_ref[...] = jnp.zeros_like(dkg_ref)
            dw_ref[...] = jnp.zeros_like(dw_ref)
            db_ref[...] = jnp.zeros_like(db_ref)

        latent = (i >= off).astype(F32)
        lane = lax.broadcasted_iota(jnp.int32, (CHUNK, LANES), 1)
        low = lane < HEAD_DIM
        cs, sn, bdv = cos_ref[...], sin_ref[...], bd_ref[...]

        def fold(b0):
            return jnp.where(low, f_ref[0, b0] + pltpu.roll(f_ref[0, b0 + 1], HEAD_DIM, 1),
                             pltpu.roll(f_ref[1, b0], HEAD_DIM, 1) + f_ref[1, b0 + 1])

        def norm_rope_bwd(xv, dout, gain):
            r = lax.rsqrt(_head_mean(xv * xv, bdv) + EPS)
            n = xv * r
            dy = dout * cs + _swap16(dout * sn, lane)
            dn = dy * gain
            dx = r * (dn - n * _head_mean(dn * n, bdv))
            return dx, jnp.sum(dy * n, axis=0, keepdims=True)

        for a in range(ATTN_WIDTH // LANES):
            cols = slice(a * LANES, (a + 1) * LANES)
            dx, dg = norm_rope_bwd(p_ref[:, cols], dq_ref[:, cols] * (latent * q_scale), qg_ref[...])
            dp_ref[:, cols] = dx.astype(BF16)
            dqg_ref[...] += dg
        dx, dg = norm_rope_bwd(p_ref[:, K_COLS[0]:K_COLS[1]], fold(0), kg_ref[...])
        dp_ref[:, K_COLS[0]:K_COLS[1]] = dx.astype(BF16)
        dkg_ref[...] += dg
        dp_ref[:, V_COLS[0]:V_COLS[1]] = fold(2).astype(BF16)
        for g in range(N_SG_GROUPS):
            su = p_ref[:, SU_COLS[0] + g * LANES:SU_COLS[0] + (g + 1) * LANES]
            sv = p_ref[:, SV_COLS[0] + g * LANES:SV_COLS[0] + (g + 1) * LANES]
            (u, dgelu_su), (vg, dgelu_sv) = _gelu_and_grad(su), _gelu_and_grad(sv)
            xc = vg - jnp.mean(vg, axis=-1, keepdims=True)
            rs = lax.rsqrt(jnp.mean(xc * xc, axis=-1, keepdims=True) + EPS)
            vn = xc * rs
            vnb = vn.astype(BF16)
            mixed = _dot(w_ref[g].astype(BF16), vnb, NN) + b_ref[:, g:g + 1]
            dsg = dsg_ref[:, g * LANES:(g + 1) * LANES].astype(F32) * latent
            du = dsg * mixed
            dmix = dsg * u
            dmb = dmix.astype(BF16)
            db_ref[:, g:g + 1] += jnp.sum(dmix, axis=-1, keepdims=True)
            dw_ref[g] += _dot(dmb, vnb, NT)
            dvn = _dot(wt_ref[g].astype(BF16), dmb, NN)
            dvg = rs * (dvn - jnp.mean(dvn, axis=-1, keepdims=True)
                        - vn * jnp.mean(dvn * vn, axis=-1, keepdims=True))
            dp_ref[:, SU_COLS[0] + g * LANES:SU_COLS[0] + (g + 1) * LANES] = (du * dgelu_su).astype(BF16)
            dp_ref[:, SV_COLS[0] + g * LANES:SV_COLS[0] + (g + 1) * LANES] = (dvg * dgelu_sv).astype(BF16)

    def row(width):
        return pl.BlockSpec((CHUNK, width), lambda i: (i, 0))

    def latent_row(width, col_block):
        return pl.BlockSpec((CHUNK, width), lambda i: (jnp.maximum(i - off, 0), col_block))

    def whole(shape):
        return pl.BlockSpec(shape, lambda i: (0,) * len(shape))

    return pl.pallas_call(
        body, name=name, grid=(TT // CHUNK,),
        out_shape=[jax.ShapeDtypeStruct((TT, IN_WIDTH), BF16), jax.ShapeDtypeStruct((1, LANES), F32),
                   jax.ShapeDtypeStruct((1, LANES), F32),
                   jax.ShapeDtypeStruct((N_SG_GROUPS, CHUNK, CHUNK), F32),
                   jax.ShapeDtypeStruct((CHUNK, N_SG_GROUPS), F32)],
        in_specs=[row(IN_WIDTH), latent_row(ATTN_WIDTH, 0),
                  pl.BlockSpec((2, 4, CHUNK, LANES), lambda i: (0, 0, i, 0)),
                  latent_row(SG_WIDTH, 1), row(LANES), row(LANES), whole((1, LANES)), whole((1, LANES)),
                  whole((LANES, LANES)), whole((N_SG_GROUPS, CHUNK, CHUNK)),
                  whole((N_SG_GROUPS, CHUNK, CHUNK)), whole((CHUNK, N_SG_GROUPS))],
        out_specs=[row(IN_WIDTH), whole((1, LANES)), whole((1, LANES)),
                   whole((N_SG_GROUPS, CHUNK, CHUNK)), whole((CHUNK, N_SG_GROUPS))],
        compiler_params=pltpu.CompilerParams(dimension_semantics=("arbitrary",)),
    )(p, dq, f, dao, cos, sin, qg, kg, bd, w_sp, w_spt, b_spt)


def _attn_fwd(q, kpad, vpad, ao, ctx_rows, name, tq=256):
    TT = q.shape[0]
    T = TT - ctx_rows
    tq = _tile(T, tq)
    off = ctx_rows // tq
    group = 2 * LANES

    def body(q_ref, k_ref, v_ref, ao_in, o_ref, lse_ref):
        del ao_in
        lane = lax.broadcasted_iota(jnp.int32, (tq, LANES), 1)
        lse = jnp.zeros((tq, LANES), F32)
        for a in range(2):
            acc = jnp.zeros((tq, LANES), F32)
            qa = q_ref[:, a * LANES:(a + 1) * LANES]
            for b in range(2):
                s = _dot(qa, k_ref[0, b], NT)
                m = jnp.max(s, axis=-1, keepdims=True)
                e = jnp.exp(s - m)
                l = jnp.sum(e, axis=-1, keepdims=True)
                acc = acc + _dot(e.astype(BF16), v_ref[0, b], NN) * (1.0 / l)
                lse = jnp.where(lane == 2 * a + b, m + jnp.log(l), lse)
            o_ref[:, a * LANES:(a + 1) * LANES] = acc.astype(BF16)
        lse_ref[0] = lse

    kv_spec = pl.BlockSpec((1, 2, TT, LANES), lambda j, i: (j, 0, 0, 0))
    return pl.pallas_call(
        body, name=name, grid=(2, T // tq),
        out_shape=[jax.ShapeDtypeStruct(ao.shape, BF16), jax.ShapeDtypeStruct((2, T, LANES), F32)],
        in_specs=[pl.BlockSpec((tq, group), lambda j, i: (i + off, j)), kv_spec, kv_spec,
                  pl.BlockSpec(memory_space=pl.ANY)],
        out_specs=[pl.BlockSpec((tq, group), lambda j, i: (i, j)),
                   pl.BlockSpec((1, tq, LANES), lambda j, i: (j, i, 0))],
        input_output_aliases={3: 0},
        compiler_params=pltpu.CompilerParams(dimension_semantics=("parallel", "parallel")),
    )(q, kpad, vpad, ao)


def _attn_bwd(q, dao, ao, lse, kpad, vpad, kt, ctx_rows, name, tq=256):
    TT = q.shape[0]
    T = TT - ctx_rows
    tq = _tile(T, tq)
    off = ctx_rows // tq
    group = 2 * LANES

    def body(q_ref, do_ref, o_ref, lse_ref, k_ref, v_ref, kt_ref, dq_ref, f_ref):
        i = pl.program_id(1)

        @pl.when(i == 0)
        def _():
            f_ref[...] = jnp.zeros_like(f_ref)

        ktv = kt_ref[0]
        lse_t = lse_ref[0].T
        row = lax.broadcasted_iota(jnp.int32, (SUBLANES, LANES), 0)
        lane = lax.broadcasted_iota(jnp.int32, (SUBLANES, LANES), 1)
        half_ones = (jnp.where(lane < HEAD_DIM, 0, 1) == row).astype(BF16)
        for a in range(2):
            cols = slice(a * LANES, (a + 1) * LANES)
            qa = q_ref[:, cols]
            do32 = do_ref[:, cols].astype(F32)
            doa = do32.astype(BF16)
            hi, lo = _split_bf16(do32 * o_ref[:, cols].astype(F32))
            deltas = _dot(half_ones, hi, NT) + _dot(half_ones, lo, NT)
            halves = []
            for b in range(2):
                h = 2 * a + b
                st = _dot(k_ref[0, b], qa, NT)
                pt = jnp.exp(st - lse_t[h:h + 1, :])
                dpt = _dot(v_ref[0, b], doa, NT)
                dst = (pt * (dpt - deltas[b:b + 1, :])).astype(BF16)
                f_ref[0, b] += _dot(dst, qa, NN)
                f_ref[0, 2 + b] += _dot(pt.astype(BF16), doa, NN)
                halves.append(_dot(ktv, dst, NN))
            dq_ref[:, cols] = jnp.concatenate(halves, axis=0).T

    kv_spec = pl.BlockSpec((1, 2, TT, LANES), lambda j, i: (j, 0, 0, 0))
    out_cols = pl.BlockSpec((tq, group), lambda j, i: (i, j))
    return pl.pallas_call(
        body, name=name, grid=(2, T // tq),
        out_shape=[jax.ShapeDtypeStruct((T, ATTN_WIDTH), F32), jax.ShapeDtypeStruct((2, 4, TT, LANES), F32)],
        in_specs=[pl.BlockSpec((tq, group), lambda j, i: (i + off, j)), out_cols, out_cols,
                  pl.BlockSpec((1, tq, LANES), lambda j, i: (j, i, 0)),
                  kv_spec, kv_spec, pl.BlockSpec((1, HEAD_DIM, TT), lambda j, i: (j, 0, 0))],
        out_specs=[out_cols, pl.BlockSpec((1, 4, TT, LANES), lambda j, i: (j, 0, 0, 0))],
        compiler_params=pltpu.CompilerParams(dimension_semantics=("parallel", "arbitrary")),
    )(q, dao, ao, lse, kpad, vpad, kt)


def _final_fwd_bwd(h, g, target, y, gt, name):
    R, Dm = h.shape
    tm = _tile(R, ROW_BLOCK, 8)

    def body(h_ref, g_ref, t_ref, y_ref, gt_ref, dh_ref, loss_ref, dg_ref, dy_ref, dgt_ref, dsum_ref):
        i = pl.program_id(0)

        @pl.when(i == 0)
        def _():
            for ref in (loss_ref, dg_ref, dgt_ref, dsum_ref):
                ref[...] = jnp.zeros_like(ref)

        hv = h_ref[...]
        r = lax.rsqrt(jnp.mean(hv * hv, axis=-1, keepdims=True) + EPS)
        n = hv * r
        diff = n * g_ref[...] - t_ref[...]
        loss_ref[...] += jnp.sum(diff * diff)
        dout = diff * (1.0 / Dm)
        dg_ref[...] += jnp.sum(dout * n, axis=0, keepdims=True)
        dn = dout * g_ref[...]
        dh = r * (dn - n * jnp.mean(dn * n, axis=-1, keepdims=True))
        dh_ref[...] = dh
        _gate_grads(dh, y_ref, gt_ref, dy_ref, dgt_ref, dsum_ref)

    vec = jax.ShapeDtypeStruct((1, Dm), F32)
    return pl.pallas_call(
        body, name=name, grid=(R // tm,),
        out_shape=[jax.ShapeDtypeStruct((R, Dm), F32), jax.ShapeDtypeStruct((1, LANES), F32), vec,
                   jax.ShapeDtypeStruct((R, Dm), BF16), vec, vec],
        in_specs=[_row_spec(tm, Dm), _vec_spec(Dm), _row_spec(tm, Dm), _row_spec(tm, Dm), _vec_spec(Dm)],
        out_specs=[_row_spec(tm, Dm), _vec_spec(LANES), _vec_spec(Dm), _row_spec(tm, Dm), _vec_spec(Dm),
                   _vec_spec(Dm)],
        compiler_params=pltpu.CompilerParams(dimension_semantics=("arbitrary",)),
    )(h, g, target, y, gt)


MOD_ROWS = 16


def _mod_fwd(c_rows, w_mod, name):
    L, Dm, n = w_mod.shape

    def body(c_ref, w_ref, o_ref):
        o_ref[0] = _dot3(_silu(c_ref[...]), w_ref[0], NN)

    return pl.pallas_call(
        body, name=name, grid=(L,),
        out_shape=jax.ShapeDtypeStruct((L, MOD_ROWS, n), F32),
        in_specs=[pl.BlockSpec((MOD_ROWS, Dm), lambda l: (0, 0)), pl.BlockSpec((1, Dm, n), lambda l: (l, 0, 0))],
        out_specs=pl.BlockSpec((1, MOD_ROWS, n), lambda l: (l, 0, 0)),
        compiler_params=pltpu.CompilerParams(dimension_semantics=("parallel",)),
    )(c_rows, w_mod)


def _mod_bwd(c_rows_t, dmod, w_mod, name):
    L, Dm, n = w_mod.shape

    def body(ct_ref, d_ref, w_ref, gw_ref, ds_ref):
        dm = d_ref[0]
        gw_ref[0] = _dot3(_silu(ct_ref[...]), dm, NN)
        ds_ref[0] = _dot3(dm[:MOD_ROWS], w_ref[0], NT)

    return pl.pallas_call(
        body, name=name, grid=(L,),
        out_shape=[jax.ShapeDtypeStruct((L, Dm, n), F32), jax.ShapeDtypeStruct((L, MOD_ROWS, Dm), F32)],
        in_specs=[pl.BlockSpec((Dm, LANES), lambda l: (0, 0)), pl.BlockSpec((1, LANES, n), lambda l: (l, 0, 0)),
                  pl.BlockSpec((1, Dm, n), lambda l: (l, 0, 0))],
        out_specs=[pl.BlockSpec((1, Dm, n), lambda l: (l, 0, 0)),
                   pl.BlockSpec((1, MOD_ROWS, Dm), lambda l: (l, 0, 0))],
        compiler_params=pltpu.CompilerParams(dimension_semantics=("parallel",)),
    )(c_rows_t, dmod, w_mod)


def _adam_update(w, g, m, v):
    c1 = 1.0 - ADAM_B1 ** ADAM_STEP
    c2 = 1.0 - ADAM_B2 ** ADAM_STEP
    mn = ADAM_B1 * m + (1.0 - ADAM_B1) * g
    vn = ADAM_B2 * v + (1.0 - ADAM_B2) * (g * g)
    return -ADAM_LR * ((mn / c1) / (jnp.sqrt(vn / c2) + ADAM_EPS) + ADAM_WD * w), mn, vn


def _adamw(w, g, m, v, name):
    R, Cw = w.shape
    tm = _tile(R, ROW_BLOCK, 8)

    def body(w_ref, g_ref, m_ref, v_ref, d_ref, mo_ref, vo_ref):
        d_ref[...], mo_ref[...], vo_ref[...] = _adam_update(w_ref[...], g_ref[...], m_ref[...], v_ref[...])

    spec = pl.BlockSpec((tm, Cw), lambda i: (i, 0))
    return pl.pallas_call(
        body, name=name, grid=(R // tm,),
        out_shape=[jax.ShapeDtypeStruct((R, Cw), F32)] * 3,
        in_specs=[spec] * 4, out_specs=[spec] * 3,
        compiler_params=pltpu.CompilerParams(dimension_semantics=("parallel",)),
    )(w, g, m, v)


def _adamw_recv(w, m, v, recvs, name):
    L, R, n = w.shape
    tm = _tile(R, ROW_BLOCK, 8)
    nblk = R // tm
    parts = [r.reshape(N_DEV, R, n) for r in recvs]

    def body(*refs):
        w_ref, m_ref, v_ref = refs[:3]
        part_refs = refs[3:3 + L]
        g_ref, d_ref, mo_ref, vo_ref, gsum = refs[3 + L:]
        l = pl.program_id(0)
        for ll in range(L):
            @pl.when(l == ll)
            def _(ll=ll):
                acc = part_refs[ll][0].astype(F32)
                for s in range(1, N_DEV):
                    acc = acc + part_refs[ll][s].astype(F32)
                gsum[...] = acc
        g = gsum[...]
        g_ref[0] = g
        d_ref[0], mo_ref[0], vo_ref[0] = _adam_update(w_ref[0], g, m_ref[0], v_ref[0])

    def part_spec(ll):
        return pl.BlockSpec((N_DEV, tm, n), lambda l, i: (0, jnp.where(l == ll, i, jnp.where(l < ll, 0, nblk - 1)), 0))

    spec = pl.BlockSpec((1, tm, n), lambda l, i: (l, i, 0))
    return pl.pallas_call(
        body, name=name, grid=(L, nblk),
        out_shape=[jax.ShapeDtypeStruct((L, R, n), F32)] * 4,
        in_specs=[spec] * 3 + [part_spec(ll) for ll in range(L)], out_specs=[spec] * 4,
        scratch_shapes=[pltpu.VMEM((tm, n), F32)],
        compiler_params=pltpu.CompilerParams(dimension_semantics=("parallel", "parallel")),
    )(w, m, v, *parts)


def _pack(parts, row_mult=8):
    flat, offs, pos = [], [], 0
    for t in parts:
        t = t.reshape(-1).astype(F32)
        size = -(-t.shape[0] // LANES) * LANES
        flat.append(jnp.pad(t, (0, size - t.shape[0])))
        offs.append(pos)
        pos += size
    total = -(-pos // (LANES * row_mult)) * (LANES * row_mult)
    if total > pos:
        flat.append(jnp.zeros((total - pos,), F32))
    return jnp.concatenate(flat).reshape(-1, LANES), offs


def _take(buf, off, shape):
    size = math.prod(shape)
    return buf[..., off:off + size].reshape(buf.shape[:-1] + tuple(shape))


def _rope_tables(T, ctx_rows):
    pos = jnp.arange(T)
    row = (pos // GRID_W).astype(F32)
    col = (pos % GRID_W).astype(F32)
    half = HEAD_DIM // 4
    inv = ROPE_THETA ** (-jnp.arange(0, 2 * half, 2, dtype=F32) / (2 * half))
    ang_r, ang_c = row[:, None] * inv[None, :], col[:, None] * inv[None, :]
    cos = jnp.concatenate([jnp.cos(ang_r)] * 2 + [jnp.cos(ang_c)] * 2, axis=1)
    sin = jnp.concatenate([-jnp.sin(ang_r), jnp.sin(ang_r), -jnp.sin(ang_c), jnp.sin(ang_c)], axis=1)
    cos = jnp.concatenate([jnp.ones((ctx_rows, HEAD_DIM), F32), cos], axis=0)
    sin = jnp.concatenate([jnp.zeros((ctx_rows, HEAD_DIM), F32), sin], axis=0)
    return jnp.tile(cos, (1, 2)), jnp.tile(sin, (1, 2))


def kernel(x, c, ctx, c_ctx, w_mod, b_mod, g_mix, g_ffn, w_ffn_in, w_ffn_out, w_in, q_gain, k_gain, w_sp, b_sp, w_out, w_pw1, b_pw1, w_dw, b_dw, ln_g, ln_b, w_pw2, b_pw2, g_final, loss_target, m_c_ctx, m_w_mod, m_b_mod, m_g_mix, m_g_ffn, m_w_ffn_in, m_w_ffn_out, m_w_in, m_q_gain, m_k_gain, m_w_sp, m_b_sp, m_w_out, m_w_pw1, m_b_pw1, m_w_dw, m_b_dw, m_ln_g, m_ln_b, m_w_pw2, m_b_pw2, m_g_final, v_c_ctx, v_w_mod, v_b_mod, v_g_mix, v_g_ffn, v_w_ffn_in, v_w_ffn_out, v_w_in, v_q_gain, v_k_gain, v_w_sp, v_b_sp, v_w_out, v_w_pw1, v_b_pw1, v_w_dw, v_b_dw, v_ln_g, v_ln_b, v_w_pw2, v_b_pw2, v_g_final):
    weights = dict(c_ctx=c_ctx, w_mod=w_mod, b_mod=b_mod, g_mix=g_mix, g_ffn=g_ffn, w_ffn_in=w_ffn_in,
                   w_ffn_out=w_ffn_out, w_in=w_in, q_gain=q_gain, k_gain=k_gain, w_sp=w_sp, b_sp=b_sp,
                   w_out=w_out, w_pw1=w_pw1, b_pw1=b_pw1, w_dw=w_dw, b_dw=b_dw, ln_g=ln_g, ln_b=ln_b,
                   w_pw2=w_pw2, b_pw2=b_pw2, g_final=g_final)
    moments_m = dict(c_ctx=m_c_ctx, w_mod=m_w_mod, b_mod=m_b_mod, g_mix=m_g_mix, g_ffn=m_g_ffn,
                     w_ffn_in=m_w_ffn_in, w_ffn_out=m_w_ffn_out, w_in=m_w_in, q_gain=m_q_gain,
                     k_gain=m_k_gain, w_sp=m_w_sp, b_sp=m_b_sp, w_out=m_w_out, w_pw1=m_w_pw1,
                     b_pw1=m_b_pw1, w_dw=m_w_dw, b_dw=m_b_dw, ln_g=m_ln_g, ln_b=m_ln_b, w_pw2=m_w_pw2,
                     b_pw2=m_b_pw2, g_final=m_g_final)
    moments_v = dict(c_ctx=v_c_ctx, w_mod=v_w_mod, b_mod=v_b_mod, g_mix=v_g_mix, g_ffn=v_g_ffn,
                     w_ffn_in=v_w_ffn_in, w_ffn_out=v_w_ffn_out, w_in=v_w_in, q_gain=v_q_gain,
                     k_gain=v_k_gain, w_sp=v_w_sp, b_sp=v_b_sp, w_out=v_w_out, w_pw1=v_w_pw1,
                     b_pw1=v_b_pw1, w_dw=v_w_dw, b_dw=v_b_dw, ln_g=v_ln_g, ln_b=v_ln_b, w_pw2=v_w_pw2,
                     b_pw2=v_b_pw2, g_final=v_g_final)
    names = list(weights)

    T, C = x.shape[1], ctx.shape[1]
    Dm = D_MODEL
    me = 4 * lax.axis_index("x") + 2 * lax.axis_index("y") + lax.axis_index("c")
    h0 = x[0]
    ctx2 = ctx[0]
    target = loss_target[0]

    small_sharded = (("w_dw", w_dw[0]), ("b_pw1", b_pw1), ("b_dw", b_dw), ("ln_g", ln_g), ("ln_b", ln_b),
                     ("b_pw2", b_pw2))
    buf1, offs1 = _pack([c] + [t for _, t in small_sharded])
    w_in_t, m_w_in_t, v_w_in_t = (jnp.swapaxes(t, 1, 2) for t in (w_in, m_w_in, v_w_in))
    w_ffi_t, m_w_ffi_t, v_w_ffi_t = (jnp.swapaxes(t, 1, 2) for t in (w_ffn_in, m_w_ffn_in, v_w_ffn_in))
    got1, W_in_t, W_out = _all_gather([buf1, w_in_t[0].astype(BF16), w_out[0].astype(BF16)], "gather_cond", False)
    got1 = got1.reshape(N_DEV, -1)
    c_all = _take(got1, offs1[0], (Dm,))
    full_small = {}
    for (nm, t), off in zip(small_sharded, offs1[1:]):
        seg = _take(got1, off, t.shape)
        full_small[nm] = jnp.moveaxis(seg, 0, -2).reshape(t.shape[:-1] + (N_DEV * t.shape[-1],))
    w_dw_f, b_pw1_f = full_small["w_dw"], full_small["b_pw1"]
    b_dw_f, ln_g_f, ln_b_f, b_pw2_f = (full_small[k] for k in ("b_dw", "ln_g", "ln_b", "b_pw2"))

    c_rows = jnp.concatenate([c_all, c_ctx[None, :], jnp.zeros((MOD_ROWS - N_DEV - 1, Dm), F32)], axis=0)
    mod_part = _mod_fwd(c_rows, w_mod, "mod_fwd")
    n_mod = w_mod.shape[2]
    got2 = _all_gather([mod_part.reshape(-1, LANES)], "gather_mod", True)[0]
    mod_all = got2.reshape(N_DEV, 2, MOD_ROWS, n_mod).transpose(1, 2, 0, 3).reshape(2, MOD_ROWS, N_DEV * n_mod)
    mod_all = mod_all + b_mod[:, None, :]
    my_mod = lax.dynamic_index_in_dim(mod_all, me, axis=1, keepdims=False)
    sh1, sc1, gt1, sh2, sc2, gt2 = ([my_mod[l:l + 1, k * Dm:(k + 1) * Dm] for l in range(2)] for k in range(6))
    csh1 = mod_all[0, N_DEV:N_DEV + 1, 0:Dm]
    csc1 = mod_all[0, N_DEV:N_DEV + 1, Dm:2 * Dm]

    behind = got2[0:1, 0:1] * 0.0
    gather_groups = [[w_ffi_t[0], w_ffn_out[0]], [w_pw1[0], w_pw2[0]], [w_ffi_t[1], w_ffn_out[1]]]
    gathers = [_push_begin([(t + behind).astype(BF16) for t in grp], True, f"gather_start{k}")
               for k, grp in enumerate(gather_groups)]
    started = sum(h[4][0:1, 0:1] for h in gathers)

    def gathered(k, after):
        return _push_end(gathers[k], after, f"gather_wait{k}")[1]

    def ffn_weights(k, after):
        wi, wo = gathered(k, after)
        return wi.reshape(N_DEV, FF_SHARD, Dm), wo.reshape(N_DEV // 2, FF_SHARD, Dm)

    def col_gathered(t, n):
        return t.reshape(N_DEV, Dm, n).transpose(1, 0, 2).reshape(Dm, N_DEV * n)

    W_ffi, W_ffo = [None, None], [None, None]

    g_mix_r = [g_mix[l:l + 1] for l in range(2)]
    g_ffn_r = [g_ffn[l:l + 1] for l in range(2)]
    g_fin = g_final[None, :]

    cos, sin = _rope_tables(T, C)
    qg = jnp.tile(q_gain, (1, 2))
    kg = jnp.tile(k_gain, (1, 2))
    lane_head = jnp.arange(LANES) // HEAD_DIM
    bd = (lane_head[:, None] == lane_head[None, :]).astype(BF16)
    w_sp0 = w_sp[0]
    w_spt0 = w_sp0.transpose(0, 2, 1)
    b_spt0 = b_sp[0].T

    XM = _norm_mod_fwd_cat(ctx2, h0, g_mix_r[0], csc1, csh1, sc1[0] + started, sh1[0], "norm_mix0")
    P = _mm(XM, W_in_t, "nt", "in_proj", tn=IN_WIDTH)
    qh, kpad, vpad, kt, ao = _mix_prep_fwd(P, C, cos, sin, qg, kg, bd, w_sp0, b_spt0, "mix_prep")
    ao, lse = _attn_fwd(qh, kpad, vpad, ao, C, "attn_fwd")
    h1, y0, xf0 = _mm(ao, W_out, "nn", "out_proj", res=h0, gate=gt1[0], raw_out=True,
                      norm=(g_ffn_r[0], sc2[0], sh2[0]))

    def ffn_fwd(h_in, xf, l, norm_next):
        W_ffi[l], W_ffo[l] = ffn_weights(2 * l, xf)
        gu, act = _ffn_in_swiglu(xf, W_ffi[l], f"ffn_in{l}")
        outs = _mm_sum_shards(act, W_ffo[l], "nn", f"ffn_out{l}", res=h_in, gate=gt2[l], raw_out=True,
                              norm=norm_next)
        return tuple(outs) + (None,) * (3 - len(outs)) + (gu, act)

    h2, f0, xm1, gu0, act0 = ffn_fwd(h1, xf0, 0, (g_mix_r[1], sc1[1], sh1[1]))

    W_pw1, W_pw2 = gathered(1, xm1)
    W_pw1 = col_gathered(W_pw1, 2 * Dm // N_DEV)
    ag = _mm(xm1, W_pw1, "nn", "pw1", BF16, bias=b_pw1_f)
    hg = _glu_fwd(ag, "glu")
    hd = _conv_fwd(hg, w_dw_f, b_dw_f, "conv")
    hs = _ln_silu_fwd(hd, ln_g_f, ln_b_f, "ln_silu")
    h3, y1, xf1 = _mm(hs, W_pw2, "nn", "pw2", bias=b_pw2_f, res=h2, gate=gt1[1], raw_out=True,
                      norm=(g_ffn_r[1], sc2[1], sh2[1]))
    h4, f1, _, gu1, act1 = ffn_fwd(h3, xf1, 1, None)

    dh4, sq_err, dg_final, df1, dgt2_1, _ = _final_fwd_bwd(h4, g_fin, target, f1, gt2[1], "loss_head")
    loss_local = (0.5 / Dm) * sq_err[0, 0:1]

    def col_shards(g, n):
        return g.reshape(Dm, N_DEV, n).transpose(1, 0, 2).reshape(N_DEV * Dm, n)

    def exchange_begin(k, parts):
        return _push_begin(parts, False, f"exchange_start{k}")

    def zero_of(handle):
        return handle[4][0:1, 0:1]

    def ffn_bwd(df, xf, gu, act, l):
        dw_out = _mm_tn_shard_rows(act, df, f"ffn_out_dw{l}", BF16)
        dgu = _ffn_out_dx_swiglu(df, W_ffo[l], gu, f"ffn_out_dx{l}").reshape(N_DEV, T, FF_SHARD)
        dw_in = _mm_tn_shard_rows(dgu, xf, f"ffn_in_dw{l}", BF16)
        dxf = _mm_sum_shards(dgu, W_ffi[l], "nn", f"ffn_in_dx{l}", BF16, tm=256)
        return dw_in, dw_out, dxf

    dW_ffi1, dW_ffo1, dxf1 = ffn_bwd(df1, xf1, gu1, act1, 1)
    ex0 = exchange_begin(0, [dW_ffi1.reshape(2 * D_FF, Dm), dW_ffo1.reshape(D_FF, Dm)])
    dh3, da, dsh, dy1, dgt1_1, db_pw2 = _norm_mod_bwd(h3, g_ffn_r[1], sc2[1], dxf1, dh4, "norm_ffn_bwd1",
                                                       gate=(y1, gt1[1] + zero_of(ex0)))
    dmod_ffn1 = (dsh, da * g_ffn_r[1], dgt2_1)
    dg_ffn1 = da * (1.0 + sc2[1])

    dW_pw2 = _mm(hs, dy1, "tn", "pw2_dw", BF16, tk=2048)
    dhs = _mm(dy1, W_pw2, "nt", "pw2_dx", BF16)
    dhd, dln_g, dln_b, db_dw = _ln_silu_bwd(dhs, hd, ln_g_f, ln_b_f, "ln_silu_bwd")
    dhg, dw_dw = _conv_bwd(dhd, hg, w_dw_f, "conv_bwd")
    dag, db_pw1 = _glu_bwd(ag, dhg, "glu_bwd")
    dW_pw1 = _mm(xm1, dag, "tn", "pw1_dw", BF16, tk=2048)
    dxm1 = _mm(dag, W_pw1, "nt", "pw1_dx", BF16, tk=2048)
    ex1 = exchange_begin(1, [col_shards(dW_pw1, 2 * Dm // N_DEV), dW_pw2])
    dh2, da, dsh, df0, dgt2_0, _ = _norm_mod_bwd(h2, g_mix_r[1], sc1[1], dxm1, dh3, "norm_mix1_bwd",
                                                 gate=(f0, gt2[0] + zero_of(ex1)))
    dmod_mix1 = (dsh, da * g_mix_r[1], dgt1_1)
    dg_mix1 = da * (1.0 + sc1[1])

    dW_ffi0, dW_ffo0, dxf0 = ffn_bwd(df0, xf0, gu0, act0, 0)
    ex2 = exchange_begin(2, [dW_ffi0.reshape(2 * D_FF, Dm), dW_ffo0.reshape(D_FF, Dm)])
    dh1, da, dsh, dy0, dgt1_0, _ = _norm_mod_bwd(h1, g_ffn_r[0], sc2[0], dxf0, dh2, "norm_ffn_bwd0",
                                                 gate=(y0, gt1[0] + zero_of(ex2)))
    dmod_ffn0 = (dsh, da * g_ffn_r[0], dgt2_0)
    dg_ffn0 = da * (1.0 + sc2[0])

    dW_out = _mm(ao, dy0, "tn", "out_proj_dw", BF16, tk=2048)
    ex_out = exchange_begin(4, [dW_out])
    dao = _mm(dy0, W_out + zero_of(ex_out).astype(BF16), "nt", "out_proj_dx", BF16)
    dq, f_acc = _attn_bwd(qh, dao, ao, lse, kpad, vpad, kt, C, "attn_bwd")
    dP, dqg, dkg, dw_sp0, db_spt0 = _mix_prep_bwd(P, dq, f_acc, dao, C, cos, sin, qg, kg, bd, w_sp0, w_spt0,
                                                  b_spt0, "mix_prep_bwd")
    dW_in_t = _mm(dP, XM, "tn", "in_proj_dw", BF16, tm=896, tk=2176)
    dXM = _mm(dP, W_in_t, "nn", "in_proj_dx", BF16, tk=IN_WIDTH)
    dh0, da, dsh = _norm_mod_bwd(h0, g_mix_r[0], sc1[0], dXM, dh1, "norm_mix0_bwd", dxm_row_off=C)
    _, dac, dcsh = _norm_mod_bwd(ctx2, g_mix_r[0], csc1, dXM, None, "norm_ctx_bwd")
    dmod_mix0 = (dsh, da * g_mix_r[0], dgt1_0)
    dg_mix0 = da * (1.0 + sc1[0]) + dac * (1.0 + csc1)
    dcmod = jnp.concatenate([dcsh, dac * g_mix_r[0]], axis=1)

    dmod_mine = jnp.stack([jnp.concatenate(dmod_mix0 + dmod_ffn0, axis=1)[0],
                           jnp.concatenate(dmod_mix1 + dmod_ffn1, axis=1)[0]])

    small_grads = [
        ("loss", loss_local), ("g_final", dg_final), ("g_mix", jnp.concatenate([dg_mix0, dg_mix1])),
        ("g_ffn", jnp.concatenate([dg_ffn0, dg_ffn1])),
        ("q_gain", dqg[:, :HEAD_DIM] + dqg[:, HEAD_DIM:]), ("k_gain", dkg[:, :HEAD_DIM] + dkg[:, HEAD_DIM:]),
        ("w_sp", dw_sp0[None]), ("b_sp", db_spt0.T[None]), ("b_pw1", db_pw1), ("w_dw", dw_dw[None]),
        ("b_dw", db_dw), ("ln_g", dln_g), ("ln_b", dln_b), ("b_pw2", db_pw2), ("dcmod", dcmod),
        ("dmod", dmod_mine),
    ]
    buf3, offs3 = _pack([t for _, t in small_grads])
    got3 = _all_gather([buf3], "gather_small_grads", True)[0].reshape(N_DEV, buf3.shape[0], LANES)
    sum3 = _sum_devices(got3, "sum_small_grads").reshape(-1)
    off3 = {nm: off for (nm, _), off in zip(small_grads, offs3)}
    shape3 = {nm: t.shape for nm, t in small_grads}

    def summed(nm):
        return _take(sum3, off3[nm], shape3[nm])

    loss = summed("loss")[0]
    dcmod_sum = summed("dcmod")
    dmod_rows = _take(got3.reshape(N_DEV, -1), off3["dmod"], (2, 6 * Dm)).transpose(1, 0, 2)
    ctx_row = jnp.concatenate([jnp.pad(dcmod_sum, ((0, 0), (0, 4 * Dm))), jnp.zeros((1, 6 * Dm), F32)])
    dmod_all = jnp.concatenate([dmod_rows, ctx_row[:, None, :],
                                jnp.zeros((2, LANES - N_DEV - 1, 6 * Dm), F32)], axis=1)
    grads = {}
    grads["b_mod"] = summed("dmod") + ctx_row
    dmod_shard = lax.dynamic_slice_in_dim(dmod_all, me * n_mod, n_mod, axis=2)
    c_rows_t = jnp.pad(c_rows.T, ((0, 0), (0, LANES - MOD_ROWS)))
    grads["w_mod"], ds_part = _mod_bwd(c_rows_t, dmod_shard, w_mod, "mod_bwd")

    buf4, _ = _pack([ds_part[0, N_DEV]])
    got4 = _all_gather([buf4], "gather_c_ctx_grad", True)[0].reshape(N_DEV, buf4.shape[0], LANES)
    ds_ctx = _sum_devices(got4, "sum_c_ctx_grad").reshape(-1)[:Dm]
    behind_small = (ds_ctx[0:1] * 0.0).astype(BF16)
    ex3 = exchange_begin(3, [dW_in_t + behind_small])
    grads["c_ctx"] = ds_ctx * _dsilu(c_ctx) + zero_of(ex3)[0]

    for nm in ("g_final", "g_mix", "g_ffn", "q_gain", "k_gain", "w_sp", "b_sp"):
        grads[nm] = summed(nm).reshape(weights[nm].shape)
    for nm in ("b_pw1", "w_dw", "b_dw", "ln_g", "ln_b", "b_pw2"):
        n_loc = weights[nm].shape[-1]
        grads[nm] = lax.dynamic_slice_in_dim(summed(nm), me * n_loc, n_loc, axis=-1).reshape(weights[nm].shape)

    delta, new_m, new_v = {}, {}, {}
    shp = w_mod.shape
    outs = _adamw(w_mod.reshape(-1, shp[-1]), grads["w_mod"].reshape(-1, shp[-1]),
                  m_w_mod.reshape(-1, shp[-1]), v_w_mod.reshape(-1, shp[-1]), "adamw_w_mod")
    delta["w_mod"], new_m["w_mod"], new_v["w_mod"] = (o.reshape(shp) for o in outs)
    big_names = ("w_mod", "w_ffn_in", "w_ffn_out", "w_in", "w_out", "w_pw1", "w_pw2")
    small_names = [nm for nm in names if nm not in big_names]
    packs = [_pack([src[nm] for nm in small_names]) for src in (weights, grads, moments_m, moments_v)]
    offs_s = packs[0][1]
    outs = _adamw(*[pk[0] for pk in packs], "adamw_small")
    for o, dst in zip(outs, (delta, new_m, new_v)):
        o = o.reshape(-1)
        for nm, off in zip(small_names, offs_s):
            dst[nm] = _take(o, off, weights[nm].shape)

    def exchanged(k, handle, after):
        return _push_end(handle, after, f"exchange_wait{k}")[1]

    def adamw_big(nm, parts, transposed=False, wmv=None):
        w3, m3, v3 = wmv if wmv is not None else (weights[nm], moments_m[nm], moments_v[nm])
        outs4 = _adamw_recv(w3, m3, v3, parts, f"adamw_{nm}")
        if transposed:
            outs4 = [jnp.swapaxes(t, 1, 2) for t in outs4]
        grads[nm], delta[nm], new_m[nm], new_v[nm] = outs4

    r_ffi1, r_ffo1 = exchanged(0, ex0, outs[0])
    r_pw1, r_pw2 = exchanged(1, ex1, outs[0])
    r_ffi0, r_ffo0 = exchanged(2, ex2, outs[0])
    r_out, = exchanged(4, ex_out, outs[0])
    adamw_big("w_ffn_in", [r_ffi0, r_ffi1], True, (w_ffi_t, m_w_ffi_t, v_w_ffi_t))
    adamw_big("w_ffn_out", [r_ffo0, r_ffo1])
    adamw_big("w_pw1", [r_pw1])
    adamw_big("w_pw2", [r_pw2])
    adamw_big("w_out", [r_out])
    r_in, = exchanged(3, ex3, delta["w_out"])
    adamw_big("w_in", [r_in], True, (w_in_t, m_w_in_t, v_w_in_t))

    return (loss, dh0[None], *[grads[n] for n in names], *[delta[n] for n in names],
            *[new_m[n] for n in names], *[new_v[n] for n in names])
```

```python
import math

import jax
import jax.numpy as jnp
from jax import lax
from jax.experimental import pallas as pl
from jax.experimental.pallas import tpu as pltpu

F32 = jnp.float32
BF16 = jnp.bfloat16
MESH = pl.DeviceIdType.MESH

N_DEV = 8
D_MODEL = 1024
EPS = 1e-6
HEAD_DIM = 64
ATTN_WIDTH = 512
KV_WIDTH = 128
SG_WIDTH = 512
N_SG_GROUPS = 4
CHUNK = 128
IN_WIDTH = 1792
D_FF = 2816
FF_SHARD = 2 * D_FF // N_DEV
CONV_WIDTH = 31
CONV_HALO = 16
GRID_W = 64
ROPE_THETA = 10000.0
LANES = 128
SUBLANES = 8
ROW_BLOCK = 512
ADAM_ROWS = 256
ADAM_LR, ADAM_B1, ADAM_B2, ADAM_EPS, ADAM_WD, ADAM_STEP = 0.001, 0.9, 0.999, 1e-08, 0.01, 10


def _tile(n, target, mult=LANES):
    best = None
    for t in range(mult, min(n, target) + 1, mult):
        if n % t == 0:
            best = t
    return best if best is not None else n


def _sigmoid(x):
    return 1.0 / (1.0 + jnp.exp(-x))


def _silu(x):
    return x * _sigmoid(x)


def _dsilu(x):
    s = _sigmoid(x)
    return s * (1.0 + x * (1.0 - s))


_GELU_K = math.sqrt(2.0 / math.pi)


def _gelu(x):
    return 0.5 * x * (1.0 + jnp.tanh(_GELU_K * (x + 0.044715 * x * x * x)))


def _gelu_and_grad(x):
    x2 = x * x
    t = jnp.tanh(_GELU_K * x * (1.0 + 0.044715 * x2))
    half = 0.5 * (1.0 + t)
    return x * half, half + 0.5 * x * (1.0 - t * t) * _GELU_K * (1.0 + 3.0 * 0.044715 * x2)


def _split_bf16(x):
    hi = x.astype(BF16)
    lo = (x - hi.astype(F32)).astype(BF16)
    return hi, lo


def _dot(a, b, dims):
    return lax.dot_general(a, b, (dims, ((), ())), preferred_element_type=F32)


def _dot3(a, b, dims):
    ah, al = _split_bf16(a)
    bh, bl = _split_bf16(b)
    return _dot(ah, bh, dims) + _dot(ah, bl, dims) + _dot(al, bh, dims)


NN = ((1,), (0,))
NT = ((1,), (1,))
TN = ((0,), (0,))


def _all_gather(xs, name, in_vmem):
    n_arr = len(xs)

    def body(*refs):
        x_refs, out_refs = refs[:n_arr], refs[n_arr:2 * n_arr]
        send_sems, recv_sems, local_sems = refs[2 * n_arr:]
        x, y, c = lax.axis_index("x"), lax.axis_index("y"), lax.axis_index("c")
        me, sibling = (x, y, c), (x, y, 1 - c)
        chips = [(1 - x, y), (x, 1 - y), (1 - x, 1 - y)]

        def rows(a, px, py, pc):
            m_per = xs[a].shape[0]
            return out_refs[a].at[pl.ds((4 * px + 2 * py + pc) * m_per, m_per), :]

        def copy(a, k, block, to, src=None):
            return pltpu.make_async_remote_copy(
                src_ref=rows(a, *block) if src is None else src,
                dst_ref=rows(a, *block),
                send_sem=send_sems.at[7 * a + k],
                recv_sem=recv_sems.at[7 * a + k],
                device_id=to,
                device_id_type=MESH,
            )

        mine, first, passed = [], [], []
        for a in range(n_arr):
            mine.append(pltpu.make_async_copy(x_refs[a], rows(a, *me), local_sems.at[a]))
            mine[-1].start()
            first.append(copy(a, 0, me, sibling, src=x_refs[a]))
            first += [copy(a, 1 + j, me, (*chip, c), src=x_refs[a]) for j, chip in enumerate(chips)]
        for cp in first:
            cp.start()
        for a in range(n_arr):
            for j, chip in enumerate(chips):
                copy(a, 1 + j, (*chip, c), me).wait_recv()
                passed.append(copy(a, 4 + j, (*chip, c), sibling))
                passed[-1].start()
        for a in range(n_arr):
            copy(a, 0, sibling, me).wait_recv()
            for j, chip in enumerate(chips):
                copy(a, 4 + j, (*chip, 1 - c), me).wait_recv()
        for cp in first + passed:
            cp.wait_send()
        for cp in mine:
            cp.wait()

    space = pltpu.VMEM if in_vmem else pl.ANY
    return pl.pallas_call(
        body,
        name=name,
        out_shape=[jax.ShapeDtypeStruct((N_DEV * t.shape[0], t.shape[1]), t.dtype) for t in xs],
        in_specs=[pl.BlockSpec(memory_space=space)] * n_arr,
        out_specs=[pl.BlockSpec(memory_space=space)] * n_arr,
        scratch_shapes=[
            pltpu.SemaphoreType.DMA((7 * n_arr,)),
            pltpu.SemaphoreType.DMA((7 * n_arr,)),
            pltpu.SemaphoreType.DMA((n_arr,)),
        ],
    )(*xs)


HBM_SPEC = pl.BlockSpec(memory_space=pltpu.HBM)
SEM_SPEC = pl.BlockSpec(memory_space=pltpu.SEMAPHORE)
DATAFLOW_EFFECT = pltpu.SideEffectType.DATAFLOW_SIDE_EFFECTING


def _peers(x, y, c):
    for k in range(1, N_DEV):
        px = 1 - x if (k >> 2) & 1 else x
        py = 1 - y if (k >> 1) & 1 else y
        pc = 1 - c if k & 1 else c
        yield k - 1, (px, py, pc), 4 * px + 2 * py + pc


def _push_copies(src_refs, land_refs, send_sems, recv_sems, shapes, whole_src):
    x, y, c = lax.axis_index("x"), lax.axis_index("y"), lax.axis_index("c")
    me = 4 * x + 2 * y + c
    for a, (m_per, _) in enumerate(shapes):
        def block(ref, idx, m_per=m_per):
            return ref.at[pl.ds(idx * m_per, m_per), :]

        for k, peer, pidx in _peers(x, y, c):
            src = src_refs[a] if whole_src else block(src_refs[a], pidx)
            sems = dict(send_sem=send_sems.at[N_DEV * a + k], recv_sem=recv_sems.at[N_DEV * a + k],
                        device_id=peer, device_id_type=MESH)
            yield (pltpu.make_async_remote_copy(src_ref=src, dst_ref=block(land_refs[a], me), **sems),
                   pltpu.make_async_remote_copy(src_ref=src, dst_ref=block(land_refs[a], pidx), **sems))


def _own_copies(src_refs, land_refs, recv_sems, shapes, whole_src):
    me = 4 * lax.axis_index("x") + 2 * lax.axis_index("y") + lax.axis_index("c")
    for a, (m_per, _) in enumerate(shapes):
        mine = pl.ds(me * m_per, m_per)
        src = src_refs[a] if whole_src else src_refs[a].at[mine, :]
        yield pltpu.make_async_copy(src, land_refs[a].at[mine, :], recv_sems.at[N_DEV * a + N_DEV - 1])


def _push_begin(srcs, whole_src, name):
    n_arr = len(srcs)
    shapes = [(t.shape[0] if whole_src else t.shape[0] // N_DEV, t.shape[1]) for t in srcs]
    lands = [lax.empty((N_DEV * m, n), t.dtype) for (m, n), t in zip(shapes, srcs)]

    def body(*refs):
        src_refs, land_refs = refs[:n_arr], refs[n_arr:2 * n_arr]
        send_sems, recv_sems = refs[2 * n_arr], refs[2 * n_arr + 1]
        token = refs[-1]
        for outgoing, _ in _push_copies(src_refs, land_refs, send_sems, recv_sems, shapes, whole_src):
            outgoing.start()
        for own in _own_copies(src_refs, land_refs, recv_sems, shapes, whole_src):
            own.start()
        token[...] = jnp.zeros_like(token)

    operands = [pltpu.with_memory_space_constraint(t, pltpu.HBM) for t in list(srcs) + lands]
    outs = pl.pallas_call(
        body, name=name,
        out_shape=(pltpu.SemaphoreType.DMA((N_DEV * n_arr,)), pltpu.SemaphoreType.DMA((N_DEV * n_arr,)),
                   *[pltpu.HBM(t.shape, t.dtype) for t in operands],
                   jax.ShapeDtypeStruct((SUBLANES, LANES), F32)),
        in_specs=[HBM_SPEC] * (2 * n_arr),
        out_specs=(SEM_SPEC, SEM_SPEC, *[HBM_SPEC] * (2 * n_arr), pl.BlockSpec(memory_space=pltpu.VMEM)),
        input_output_aliases={i: 2 + i for i in range(2 * n_arr)},
        compiler_params=pltpu.CompilerParams(has_side_effects=DATAFLOW_EFFECT),
    )(*operands)
    return outs[0], outs[1], list(outs[2:2 + n_arr]), list(outs[2 + n_arr:2 + 2 * n_arr]), outs[-1], whole_src


def _push_end(handle, after, name):
    send_sems, recv_sems, srcs, lands, _, whole_src = handle
    n_arr = len(srcs)
    shapes = [(t.shape[0] // N_DEV, t.shape[1]) for t in lands]

    def body(*refs):
        src_refs, land_refs = refs[:n_arr], refs[n_arr:2 * n_arr]
        send_sems_ref, recv_sems_ref = refs[2 * n_arr], refs[2 * n_arr + 1]
        for outgoing, incoming in _push_copies(src_refs, land_refs, send_sems_ref, recv_sems_ref, shapes, whole_src):
            outgoing.wait_send()
            incoming.wait_recv()
        for own in _own_copies(src_refs, land_refs, recv_sems_ref, shapes, whole_src):
            own.wait()

    outs = pl.pallas_call(
        body, name=name,
        out_shape=tuple(pltpu.HBM(t.shape, t.dtype) for t in srcs + lands),
        in_specs=[HBM_SPEC] * (2 * n_arr) + [SEM_SPEC, SEM_SPEC, pl.BlockSpec(memory_space=pl.ANY)],
        out_specs=tuple([HBM_SPEC] * (2 * n_arr)),
        input_output_aliases={i: i for i in range(2 * n_arr)},
        compiler_params=pltpu.CompilerParams(has_side_effects=DATAFLOW_EFFECT),
    )(*srcs, *lands, send_sems, recv_sems, after)
    return list(outs[:n_arr]), list(outs[n_arr:])


def _sum_devices(r, name, rows_per_step=ADAM_ROWS):
    _, m, n = r.shape
    tm = _tile(m, rows_per_step, 8)

    def body(r_ref, o_ref):
        acc = r_ref[0].astype(F32)
        for s in range(1, N_DEV):
            acc = acc + r_ref[s].astype(F32)
        o_ref[...] = acc

    return pl.pallas_call(
        body,
        name=name,
        grid=(m // tm,),
        out_shape=jax.ShapeDtypeStruct((m, n), F32),
        in_specs=[pl.BlockSpec((N_DEV, tm, n), lambda i: (0, i, 0))],
        out_specs=pl.BlockSpec((tm, n), lambda i: (i, 0)),
        compiler_params=pltpu.CompilerParams(dimension_semantics=("parallel",)),
    )(r)


def _get(ref):
    return ref[0] if len(ref.shape) == 3 else ref[...]


def _put(ref, val):
    if len(ref.shape) == 3:
        ref[0] = val
    else:
        ref[...] = val


def _norm_mod(hv, g, sc, sh):
    r = lax.rsqrt(jnp.mean(hv * hv, axis=-1, keepdims=True) + EPS)
    return (hv * r) * g * (1.0 + sc) + sh


def _mm_call(name, a, b, a_spec, b_spec, out_sds, o_spec, grid, dims, acc_shape, bias=None,
             res=None, gate=None, raw_out=False, vec_spec=None, norm=None):
    nk = grid[2]
    operands, in_specs = [a, b], [a_spec, b_spec]
    if bias is not None:
        operands.append(bias)
        in_specs.append(vec_spec)
    if res is not None:
        operands += [res, gate]
        in_specs += [o_spec, vec_spec]
    if norm is not None:
        assert grid[1] == 1
        operands += list(norm)
        in_specs += [vec_spec] * 3
    out_shape, out_specs = [out_sds], [o_spec]
    if raw_out:
        out_shape.append(jax.ShapeDtypeStruct(out_sds.shape, BF16))
        out_specs.append(o_spec)
    if norm is not None:
        out_shape.append(jax.ShapeDtypeStruct(out_sds.shape, BF16))
        out_specs.append(o_spec)

    def body(*refs):
        it = iter(refs)
        a_ref, b_ref = next(it), next(it)
        bias_ref = next(it) if bias is not None else None
        res_ref, gate_ref = (next(it), next(it)) if res is not None else (None, None)
        norm_refs = (next(it), next(it), next(it)) if norm is not None else None
        o_ref = next(it)
        raw_ref = next(it) if raw_out else None
        xn_ref = next(it) if norm is not None else None
        acc = next(it) if nk > 1 else None
        k = pl.program_id(2)
        part = _dot(_get(a_ref).astype(BF16), _get(b_ref).astype(BF16), dims)

        def finish(y):
            if bias_ref is not None:
                y = y + bias_ref[...]
            if raw_ref is not None:
                raw_ref[...] = y.astype(BF16)
            if res_ref is not None:
                y = res_ref[...] + gate_ref[...] * y
            _put(o_ref, y.astype(out_sds.dtype))
            if xn_ref is not None:
                xn_ref[...] = _norm_mod(y, *[r[...] for r in norm_refs]).astype(BF16)

        if nk == 1:
            finish(part)
        else:
            @pl.when(k == 0)
            def _():
                acc[...] = part

            @pl.when(k > 0)
            def _():
                acc[...] += part

            @pl.when(k == nk - 1)
            def _():
                finish(acc[...])

    outs = pl.pallas_call(
        body,
        name=name,
        grid=grid,
        out_shape=out_shape,
        in_specs=in_specs,
        out_specs=out_specs,
        scratch_shapes=[pltpu.VMEM(acc_shape, F32)] if nk > 1 else [],
        compiler_params=pltpu.CompilerParams(dimension_semantics=("parallel", "parallel", "arbitrary")),
    )(*operands)
    return outs if len(outs) > 1 else outs[0]


def _mm(a, b, mode, name, out_dtype=F32, bias=None, res=None, gate=None, raw_out=False,
        tm=512, tn=1024, tk=1024, a_row_off=0, norm=None):
    if mode == "nn":
        K, N = b.shape
        M = a.shape[0] - a_row_off
    elif mode == "nt":
        N, K = b.shape
        M = a.shape[0] - a_row_off
    else:
        (K, M), N = a.shape, b.shape[1]
    tm, tn, tk = _tile(M, tm), _tile(N, tn), _tile(K, tk)
    off = a_row_off // tm
    dims = {"nn": NN, "nt": NT, "tn": TN}[mode]
    a_spec = (pl.BlockSpec((tk, tm), lambda i, j, k: (k, i)) if mode == "tn"
              else pl.BlockSpec((tm, tk), lambda i, j, k: (i + off, k)))
    b_spec = (pl.BlockSpec((tn, tk), lambda i, j, k: (j, k)) if mode == "nt"
              else pl.BlockSpec((tk, tn), lambda i, j, k: (k, j)))
    return _mm_call(name, a, b, a_spec, b_spec, jax.ShapeDtypeStruct((M, N), out_dtype),
                    pl.BlockSpec((tm, tn), lambda i, j, k: (i, j)), (M // tm, N // tn, K // tk), dims,
                    (tm, tn), bias, res, gate, raw_out, pl.BlockSpec((1, tn), lambda i, j, k: (0, j)), norm)


def _mm_sum_shards(a3, b3, mode, name, out_dtype=F32, res=None, gate=None, raw_out=False, tm=512, norm=None):
    S, M, kk = a3.shape
    N = b3.shape[2] if mode == "nn" else b3.shape[1]
    tm = _tile(M, tm)
    dims = NN if mode == "nn" else NT
    has_res = res is not None

    def body(*refs):
        it = iter(refs)
        a_ref, b_ref = next(it), next(it)
        res_ref, gate_ref = (next(it), next(it)) if has_res else (None, None)
        norm_refs = (next(it), next(it), next(it)) if norm is not None else None
        o_ref = next(it)
        raw_ref = next(it) if raw_out else None
        xn_ref = next(it) if norm is not None else None
        y = _dot(a_ref[0], b_ref[0], dims)
        for s in range(1, S):
            y = y + _dot(a_ref[s], b_ref[s], dims)
        if raw_ref is not None:
            raw_ref[...] = y.astype(BF16)
        if has_res:
            y = res_ref[...] + gate_ref[...] * y
        o_ref[...] = y.astype(out_dtype)
        if xn_ref is not None:
            xn_ref[...] = _norm_mod(y, *[r[...] for r in norm_refs]).astype(BF16)

    tile = pl.BlockSpec((tm, N), lambda i: (i, 0))
    operands = [a3, b3] + ([res, gate] if has_res else []) + (list(norm) if norm is not None else [])
    in_specs = [pl.BlockSpec((S, tm, kk), lambda i: (0, i, 0)), pl.BlockSpec(b3.shape, lambda i: (0, 0, 0))]
    in_specs += [tile, _vec_spec(N)] if has_res else []
    in_specs += [_vec_spec(N)] * 3 if norm is not None else []
    out_shape = [jax.ShapeDtypeStruct((M, N), out_dtype)] + ([jax.ShapeDtypeStruct((M, N), BF16)] if raw_out else [])
    out_shape += [jax.ShapeDtypeStruct((M, N), BF16)] if norm is not None else []
    outs = pl.pallas_call(
        body, name=name, grid=(M // tm,),
        out_shape=out_shape, in_specs=in_specs, out_specs=[tile] * len(out_shape),
        compiler_params=pltpu.CompilerParams(dimension_semantics=("parallel",)),
    )(*operands)
    return outs if len(outs) > 1 else outs[0]


def _mm_tn_shard_rows(a3, b, name, out_dtype, tn=1024, tk=4096):
    S, T, m = a3.shape
    N = b.shape[1]
    tn, tk = _tile(N, tn), _tile(T, tk)
    return _mm_call(name, a3, b, pl.BlockSpec((1, tk, m), lambda i, j, k: (i, k, 0)),
                    pl.BlockSpec((tk, tn), lambda i, j, k: (k, j)), jax.ShapeDtypeStruct((S, m, N), out_dtype),
                    pl.BlockSpec((1, m, tn), lambda i, j, k: (i, 0, j)), (S, N // tn, T // tk), TN, (m, tn))


def _row_spec(tm, width, off=0):
    return pl.BlockSpec((tm, width), lambda i: (i + off, 0))


def _vec_spec(width):
    return pl.BlockSpec((1, width), lambda i: (0, 0))


def _norm_mod_fwd_cat(hc, h, g, csc, csh, sc, sh, name):
    (C, Dm), T = hc.shape, h.shape[0]
    tm = _tile(math.gcd(C, T), ROW_BLOCK, 8)
    off = C // tm

    def body(hc_ref, h_ref, g_ref, csc_ref, csh_ref, sc_ref, sh_ref, o_ref):
        is_ctx = pl.program_id(0) < off
        hv = jnp.where(is_ctx, hc_ref[...], h_ref[...])
        scv = jnp.where(is_ctx, csc_ref[...], sc_ref[...])
        shv = jnp.where(is_ctx, csh_ref[...], sh_ref[...])
        r = lax.rsqrt(jnp.mean(hv * hv, axis=-1, keepdims=True) + EPS)
        o_ref[...] = ((hv * r) * g_ref[...] * (1.0 + scv) + shv).astype(BF16)

    return pl.pallas_call(
        body, name=name, grid=((C + T) // tm,),
        out_shape=jax.ShapeDtypeStruct((C + T, Dm), BF16),
        in_specs=[pl.BlockSpec((tm, Dm), lambda i: (jnp.minimum(i, off - 1), 0)),
                  pl.BlockSpec((tm, Dm), lambda i: (jnp.maximum(i - off, 0), 0))] + [_vec_spec(Dm)] * 5,
        out_specs=_row_spec(tm, Dm),
        compiler_params=pltpu.CompilerParams(dimension_semantics=("parallel",)),
    )(hc, h, g, csc, csh, sc, sh)


def _gate_grads(dh, y_ref, gt_ref, dy_ref, dgt_ref, dsum_ref):
    dy = dh * gt_ref[...]
    dgt_ref[...] += jnp.sum(dh * y_ref[...].astype(F32), axis=0, keepdims=True)
    dsum_ref[...] += jnp.sum(dy, axis=0, keepdims=True)
    dy_ref[...] = dy.astype(BF16)


def _norm_mod_bwd(h, g, sc, dxm, dres, name, dxm_row_off=0, gate=None):
    R, Dm = h.shape
    tm = _tile(math.gcd(R, dxm_row_off) if dxm_row_off else R, ROW_BLOCK, 8)
    off = dxm_row_off // tm
    has_res = dres is not None
    has_gate = gate is not None

    def body(*refs):
        it = iter(refs)
        h_ref, g_ref, sc_ref, dx_ref = next(it), next(it), next(it), next(it)
        dres_ref = next(it) if has_res else None
        y_ref, gt_ref = (next(it), next(it)) if has_gate else (None, None)
        dh_ref, da_ref, dsh_ref = next(it), next(it), next(it)
        gate_out = (next(it), next(it), next(it)) if has_gate else ()
        i = pl.program_id(0)

        @pl.when(i == 0)
        def _():
            for ref in (da_ref, dsh_ref) + gate_out[1:]:
                ref[...] = jnp.zeros_like(ref)

        hv = h_ref[...]
        dx = dx_ref[...].astype(F32)
        r = lax.rsqrt(jnp.mean(hv * hv, axis=-1, keepdims=True) + EPS)
        n = hv * r
        da_ref[...] += jnp.sum(dx * n, axis=0, keepdims=True)
        dsh_ref[...] += jnp.sum(dx, axis=0, keepdims=True)
        dn = dx * (g_ref[...] * (1.0 + sc_ref[...]))
        dh = r * (dn - n * jnp.mean(dn * n, axis=-1, keepdims=True))
        if has_res:
            dh = dh + dres_ref[...]
        dh_ref[...] = dh
        if has_gate:
            _gate_grads(dh, y_ref, gt_ref, *gate_out)

    operands = [h, g, sc, dxm] + ([dres] if has_res else []) + (list(gate) if has_gate else [])
    in_specs = [_row_spec(tm, Dm), _vec_spec(Dm), _vec_spec(Dm), _row_spec(tm, Dm, off)]
    in_specs += [_row_spec(tm, Dm)] if has_res else []
    in_specs += [_row_spec(tm, Dm), _vec_spec(Dm)] if has_gate else []
    vec = jax.ShapeDtypeStruct((1, Dm), F32)
    out_shape = [jax.ShapeDtypeStruct((R, Dm), F32), vec, vec]
    out_specs = [_row_spec(tm, Dm), _vec_spec(Dm), _vec_spec(Dm)]
    if has_gate:
        out_shape += [jax.ShapeDtypeStruct((R, Dm), BF16), vec, vec]
        out_specs += [_row_spec(tm, Dm), _vec_spec(Dm), _vec_spec(Dm)]
    return pl.pallas_call(
        body, name=name, grid=(R // tm,),
        out_shape=out_shape, in_specs=in_specs, out_specs=out_specs,
        compiler_params=pltpu.CompilerParams(dimension_semantics=("arbitrary",)),
    )(*operands)


def _ffn_in_swiglu(xf, w3, name, tm=1024):
    T, K = xf.shape
    S, n, _ = w3.shape
    half = S // 2
    tm = _tile(T, tm)

    def body(a_ref, wg_ref, wu_ref, gu_ref, act_ref):
        a = a_ref[...]
        g = _dot(a, wg_ref[0], NT)
        u = _dot(a, wu_ref[0], NT)
        gu_ref[0, 0] = g.astype(BF16)
        gu_ref[1, 0] = u.astype(BF16)
        act_ref[0] = (_silu(g) * u).astype(BF16)

    return pl.pallas_call(
        body, name=name, grid=(T // tm, half),
        out_shape=[jax.ShapeDtypeStruct((2, half, T, n), BF16), jax.ShapeDtypeStruct((half, T, n), BF16)],
        in_specs=[pl.BlockSpec((tm, K), lambda i, j: (i, 0)),
                  pl.BlockSpec((1, n, K), lambda i, j: (j, 0, 0)),
                  pl.BlockSpec((1, n, K), lambda i, j: (j + half, 0, 0))],
        out_specs=[pl.BlockSpec((2, 1, tm, n), lambda i, j: (0, j, i, 0)),
                   pl.BlockSpec((1, tm, n), lambda i, j: (j, i, 0))],
        compiler_params=pltpu.CompilerParams(dimension_semantics=("parallel", "parallel")),
    )(xf, w3, w3)


def _ffn_out_dx_swiglu(df, wo, gu, name, tm=1024):
    T, Dm = df.shape
    half, n, _ = wo.shape
    tm = _tile(T, tm)

    def body(df_ref, w_ref, gu_ref, o_ref):
        da = _dot(df_ref[...], w_ref[0], NT)
        g = gu_ref[0, 0].astype(F32)
        u = gu_ref[1, 0].astype(F32)
        s = _sigmoid(g)
        o_ref[0, 0] = (da * u * (s * (1.0 + g * (1.0 - s)))).astype(BF16)
        o_ref[1, 0] = (da * (g * s)).astype(BF16)

    gu_spec = pl.BlockSpec((2, 1, tm, n), lambda i, j: (0, j, i, 0))
    return pl.pallas_call(
        body, name=name, grid=(T // tm, half),
        out_shape=jax.ShapeDtypeStruct(gu.shape, BF16),
        in_specs=[pl.BlockSpec((tm, Dm), lambda i, j: (i, 0)),
                  pl.BlockSpec((1, n, Dm), lambda i, j: (j, 0, 0)), gu_spec],
        out_specs=gu_spec,
        compiler_params=pltpu.CompilerParams(dimension_semantics=("parallel", "parallel")),
    )(df, wo, gu)


def _glu_fwd(ag, name):
    R = ag.shape[0]
    tm = _tile(R, ROW_BLOCK, 8)

    def body(ag_ref, o_ref):
        o_ref[...] = ag_ref[:, :D_MODEL].astype(F32) * _sigmoid(ag_ref[:, D_MODEL:].astype(F32))

    return pl.pallas_call(
        body, name=name, grid=(R // tm,),
        out_shape=jax.ShapeDtypeStruct((R, D_MODEL), F32),
        in_specs=[_row_spec(tm, 2 * D_MODEL)],
        out_specs=_row_spec(tm, D_MODEL),
        compiler_params=pltpu.CompilerParams(dimension_semantics=("parallel",)),
    )(ag)


def _glu_bwd(ag, dhg, name):
    R = ag.shape[0]
    tm = _tile(R, ROW_BLOCK, 8)

    def body(ag_ref, dh_ref, o_ref, s_ref):
        i = pl.program_id(0)

        @pl.when(i == 0)
        def _():
            s_ref[...] = jnp.zeros_like(s_ref)

        a = ag_ref[:, :D_MODEL].astype(F32)
        s = _sigmoid(ag_ref[:, D_MODEL:].astype(F32))
        dh = dh_ref[...]
        da = dh * s
        dg = dh * a * s * (1.0 - s)
        o_ref[:, :D_MODEL] = da.astype(BF16)
        o_ref[:, D_MODEL:] = dg.astype(BF16)
        s_ref[:, :D_MODEL] += jnp.sum(da, axis=0, keepdims=True)
        s_ref[:, D_MODEL:] += jnp.sum(dg, axis=0, keepdims=True)

    return pl.pallas_call(
        body, name=name, grid=(R // tm,),
        out_shape=[jax.ShapeDtypeStruct((R, 2 * D_MODEL), BF16), jax.ShapeDtypeStruct((1, 2 * D_MODEL), F32)],
        in_specs=[_row_spec(tm, 2 * D_MODEL), _row_spec(tm, D_MODEL)],
        out_specs=[_row_spec(tm, 2 * D_MODEL), _vec_spec(2 * D_MODEL)],
        compiler_params=pltpu.CompilerParams(dimension_semantics=("arbitrary",)),
    )(ag, dhg)


def _halo_specs(tm, nblk, width):
    per = tm // CONV_HALO
    prev = pl.BlockSpec((CONV_HALO, width), lambda i: (jnp.maximum(i * per - 1, 0), 0))
    nxt = pl.BlockSpec((CONV_HALO, width), lambda i: (jnp.minimum((i + 1) * per, nblk * per - 1), 0))
    return prev, nxt


def _fill_halo(scr, prev_ref, cur_ref, next_ref, i, nblk, tm):
    scr[0:CONV_HALO, :] = jnp.where(i > 0, prev_ref[...], 0.0)
    scr[CONV_HALO:CONV_HALO + tm, :] = cur_ref[...]
    scr[CONV_HALO + tm:2 * CONV_HALO + tm, :] = jnp.where(i < nblk - 1, next_ref[...], 0.0)


CONV_ROWS = 128


CONV_REACH = (CONV_WIDTH // SUBLANES) * SUBLANES


def _windows(scr, stage, cols, tm):
    for r in range(SUBLANES):
        if r:
            stage[r] = scr[pl.ds(r, tm + CONV_REACH), cols]
        for a in range(CONV_REACH // SUBLANES + 1):
            off = SUBLANES * a + r
            if 1 <= off <= CONV_WIDTH:
                yield off, (stage[r, SUBLANES * a:SUBLANES * a + tm, :] if r
                            else scr[SUBLANES * a:SUBLANES * a + tm, cols])


def _conv_ln_fwd(hg, w_dw, b_dw, ln_g, ln_b, name):
    R, Dm = hg.shape
    tm = _tile(R, CONV_ROWS, CONV_HALO)
    nblk = R // tm
    prev_spec, next_spec = _halo_specs(tm, nblk, Dm)

    def body(prev_ref, cur_ref, next_ref, w_ref, bdw_ref, g_ref, b_ref, hd_ref, hs_ref, scr, stage):
        _fill_halo(scr, prev_ref, cur_ref, next_ref, pl.program_id(0), nblk, tm)
        for cb in range(Dm // LANES):
            cols = slice(cb * LANES, (cb + 1) * LANES)
            acc = jnp.zeros((tm, LANES), F32) + bdw_ref[:, cols]
            for off, win in _windows(scr, stage, cols, tm):
                acc = acc + w_ref[off - 1:off, cols] * win
            hd_ref[:, cols] = acc
        hd = hd_ref[...]
        xc = hd - jnp.mean(hd, axis=-1, keepdims=True)
        rs = lax.rsqrt(jnp.mean(xc * xc, axis=-1, keepdims=True) + EPS)
        hs_ref[...] = _silu(xc * rs * g_ref[...] + b_ref[...]).astype(BF16)

    return pl.pallas_call(
        body, name=name, grid=(nblk,),
        out_shape=[jax.ShapeDtypeStruct((R, Dm), F32), jax.ShapeDtypeStruct((R, Dm), BF16)],
        in_specs=[prev_spec, _row_spec(tm, Dm), next_spec,
                  pl.BlockSpec((CONV_WIDTH, Dm), lambda i: (0, 0)), _vec_spec(Dm), _vec_spec(Dm), _vec_spec(Dm)],
        out_specs=[_row_spec(tm, Dm), _row_spec(tm, Dm)],
        scratch_shapes=[pltpu.VMEM((tm + 2 * CONV_HALO, Dm), F32),
                        pltpu.VMEM((SUBLANES, tm + CONV_REACH, LANES), F32)],
        compiler_params=pltpu.CompilerParams(dimension_semantics=("parallel",)),
    )(hg, hg, hg, w_dw, b_dw, ln_g, ln_b)


def _ln_silu_bwd(dhs, hd, ln_g, ln_b, name):
    R, Dm = hd.shape
    tm = _tile(R, ROW_BLOCK, 8)

    def body(dhs_ref, hd_ref, g_ref, b_ref, dhd_ref, dg_ref, db_ref, dsum_ref):
        i = pl.program_id(0)

        @pl.when(i == 0)
        def _():
            dg_ref[...] = jnp.zeros_like(dg_ref)
            db_ref[...] = jnp.zeros_like(db_ref)
            dsum_ref[...] = jnp.zeros_like(dsum_ref)

        hd = hd_ref[...]
        mu = jnp.mean(hd, axis=-1, keepdims=True)
        xc = hd - mu
        rs = lax.rsqrt(jnp.mean(xc * xc, axis=-1, keepdims=True) + EPS)
        z = xc * rs
        hl = z * g_ref[...] + b_ref[...]
        dhl = dhs_ref[...].astype(F32) * _dsilu(hl)
        dg_ref[...] += jnp.sum(dhl * z, axis=0, keepdims=True)
        db_ref[...] += jnp.sum(dhl, axis=0, keepdims=True)
        dz = dhl * g_ref[...]
        dhd = rs * (dz - jnp.mean(dz, axis=-1, keepdims=True) - z * jnp.mean(dz * z, axis=-1, keepdims=True))
        dsum_ref[...] += jnp.sum(dhd, axis=0, keepdims=True)
        dhd_ref[...] = dhd

    return pl.pallas_call(
        body, name=name, grid=(R // tm,),
        out_shape=[jax.ShapeDtypeStruct((R, Dm), F32)] + [jax.ShapeDtypeStruct((1, Dm), F32)] * 3,
        in_specs=[_row_spec(tm, Dm), _row_spec(tm, Dm), _vec_spec(Dm), _vec_spec(Dm)],
        out_specs=[_row_spec(tm, Dm), _vec_spec(Dm), _vec_spec(Dm), _vec_spec(Dm)],
        compiler_params=pltpu.CompilerParams(dimension_semantics=("arbitrary",)),
    )(dhs, hd, ln_g, ln_b)


def _conv_bwd(dhd, hg, w_dw, name):
    R, Dm = hg.shape
    tm = _tile(R, CONV_ROWS, CONV_HALO)
    nblk = R // tm
    prev_spec, next_spec = _halo_specs(tm, nblk, Dm)

    def body(dprev, dcur, dnext, gprev, gcur, gnext, w_ref, dhg_ref, dw_ref, dscr, gscr, dwp, stage):
        i = pl.program_id(0)

        @pl.when(i == 0)
        def _():
            dwp[...] = jnp.zeros_like(dwp)

        _fill_halo(dscr, dprev, dcur, dnext, i, nblk, tm)
        _fill_halo(gscr, gprev, gcur, gnext, i, nblk, tm)
        for cb in range(Dm // LANES):
            cols = slice(cb * LANES, (cb + 1) * LANES)
            acc = jnp.zeros((tm, LANES), F32)
            for off, win in _windows(dscr, stage, cols, tm):
                j = CONV_WIDTH - off
                acc = acc + w_ref[j:j + 1, cols] * win
            dhg_ref[:, cols] = acc
            d_here = dcur[:, cols]
            for off, win in _windows(gscr, stage, cols, tm):
                j = off - 1
                prod = d_here * win
                part = prod[0:SUBLANES]
                for k in range(1, tm // SUBLANES):
                    part = part + prod[k * SUBLANES:(k + 1) * SUBLANES]
                dwp[j * SUBLANES:(j + 1) * SUBLANES, cols] += part

        @pl.when(i == nblk - 1)
        def _():
            for j in range(CONV_WIDTH):
                dw_ref[j:j + 1, :] = jnp.sum(dwp[j * SUBLANES:(j + 1) * SUBLANES, :], axis=0, keepdims=True)

    return pl.pallas_call(
        body, name=name, grid=(nblk,),
        out_shape=[jax.ShapeDtypeStruct((R, Dm), F32), jax.ShapeDtypeStruct((CONV_WIDTH, Dm), F32)],
        in_specs=[prev_spec, _row_spec(tm, Dm), next_spec, prev_spec, _row_spec(tm, Dm), next_spec,
                  pl.BlockSpec((CONV_WIDTH, Dm), lambda i: (0, 0))],
        out_specs=[_row_spec(tm, Dm), pl.BlockSpec((CONV_WIDTH, Dm), lambda i: (0, 0))],
        scratch_shapes=[pltpu.VMEM((tm + 2 * CONV_HALO, Dm), F32)] * 2
        + [pltpu.VMEM((CONV_WIDTH * SUBLANES, Dm), F32), pltpu.VMEM((SUBLANES, tm + CONV_REACH, LANES), F32)],
        compiler_params=pltpu.CompilerParams(dimension_semantics=("arbitrary",)),
    )(dhd, dhd, dhd, hg, hg, hg, w_dw)


def _swap16(y, lane):
    return jnp.where((lane & 16) == 0, pltpu.roll(y, LANES - 16, 1), pltpu.roll(y, 16, 1))


def _head_mean(v, bd):
    hi, lo = _split_bf16(v)
    return (_dot(hi, bd, NN) + _dot(lo, bd, NN)) * (1.0 / HEAD_DIM)


Q_COLS = (0, ATTN_WIDTH)
K_COLS = (ATTN_WIDTH, ATTN_WIDTH + HEAD_DIM * 2)
V_COLS = (K_COLS[1], K_COLS[1] + HEAD_DIM * 2)
SU_COLS = (V_COLS[1], V_COLS[1] + SG_WIDTH)
SV_COLS = (SU_COLS[1], SU_COLS[1] + SG_WIDTH)


def _mix_prep_fwd(p, ctx_rows, cos, sin, qg, kg, bd, w_sp, b_spt, name):
    TT = p.shape[0]
    off = ctx_rows // CHUNK
    q_scale = HEAD_DIM ** -0.5

    def body(p_ref, cos_ref, sin_ref, qg_ref, kg_ref, bd_ref, w_ref, b_ref,
             q_ref, kp_ref, vp_ref, kt_ref, sg_ref):
        lane = lax.broadcasted_iota(jnp.int32, (CHUNK, LANES), 1)
        low = lane < HEAD_DIM
        cs, sn, bdv = cos_ref[...], sin_ref[...], bd_ref[...]

        def norm_rope(xv, gain):
            r = lax.rsqrt(_head_mean(xv * xv, bdv) + EPS)
            yv = xv * r * gain
            return yv * cs + _swap16(yv, lane) * sn

        def pad_heads(ref, t):
            tr = pltpu.roll(t, HEAD_DIM, 1)
            ref[0, 0] = jnp.where(low, t, 0.0).astype(BF16)
            ref[0, 1] = jnp.where(low, 0.0, tr).astype(BF16)
            ref[1, 0] = jnp.where(low, tr, 0.0).astype(BF16)
            ref[1, 1] = jnp.where(low, 0.0, t).astype(BF16)

        for a in range(ATTN_WIDTH // LANES):
            xv = p_ref[:, a * LANES:(a + 1) * LANES]
            q_ref[:, a * LANES:(a + 1) * LANES] = (norm_rope(xv, qg_ref[...]) * q_scale).astype(BF16)
        kh = norm_rope(p_ref[:, K_COLS[0]:K_COLS[1]], kg_ref[...])
        pad_heads(kp_ref, kh)
        pad_heads(vp_ref, p_ref[:, V_COLS[0]:V_COLS[1]])
        kht = kh.T
        kt_ref[0] = kht[:HEAD_DIM].astype(BF16)
        kt_ref[1] = kht[HEAD_DIM:].astype(BF16)
        for g in range(N_SG_GROUPS):
            u = _gelu(p_ref[:, SU_COLS[0] + g * LANES:SU_COLS[0] + (g + 1) * LANES])
            vg = _gelu(p_ref[:, SV_COLS[0] + g * LANES:SV_COLS[0] + (g + 1) * LANES])
            xc = vg - jnp.mean(vg, axis=-1, keepdims=True)
            vn = xc * lax.rsqrt(jnp.mean(xc * xc, axis=-1, keepdims=True) + EPS)
            mixed = _dot(w_ref[g].astype(BF16), vn.astype(BF16), NN) + b_ref[:, g:g + 1]
            sg_ref[:, g * LANES:(g + 1) * LANES] = (u * mixed).astype(BF16)

    def row(width):
        return pl.BlockSpec((CHUNK, width), lambda i: (i, 0))

    def whole(shape):
        return pl.BlockSpec(shape, lambda i: (0,) * len(shape))

    pad_spec = pl.BlockSpec((2, 2, CHUNK, LANES), lambda i: (0, 0, i, 0))
    return pl.pallas_call(
        body, name=name, grid=(TT // CHUNK,),
        out_shape=[jax.ShapeDtypeStruct((TT, ATTN_WIDTH), BF16),
                   jax.ShapeDtypeStruct((2, 2, TT, LANES), BF16), jax.ShapeDtypeStruct((2, 2, TT, LANES), BF16),
                   jax.ShapeDtypeStruct((2, HEAD_DIM, TT), BF16),
                   jax.ShapeDtypeStruct((TT - ctx_rows, ATTN_WIDTH + SG_WIDTH), BF16)],
        in_specs=[row(IN_WIDTH), row(LANES), row(LANES), whole((1, LANES)), whole((1, LANES)),
                  whole((LANES, LANES)), whole((N_SG_GROUPS, CHUNK, CHUNK)), whole((CHUNK, N_SG_GROUPS))],
        out_specs=[row(ATTN_WIDTH), pad_spec, pad_spec,
                   pl.BlockSpec((2, HEAD_DIM, CHUNK), lambda i: (0, 0, i)),
                   pl.BlockSpec((CHUNK, SG_WIDTH), lambda i: (jnp.maximum(i - off, 0), 1))],
        compiler_params=pltpu.CompilerParams(dimension_semantics=("arbitrary",)),
    )(p, cos, sin, qg, kg, bd, w_sp, b_spt)


def _mix_prep_bwd(p, dq, f, dao, ctx_rows, cos, sin, qg, kg, bd, w_sp, w_spt, b_spt, name):
    TT = p.shape[0]
    off = ctx_rows // CHUNK
    q_scale = HEAD_DIM ** -0.5

    def body(p_ref, dq_ref, f_ref, dsg_ref, cos_ref, sin_ref, qg_ref, kg_ref, bd_ref, w_ref, wt_ref,
             b_ref, dp_ref, dqg_ref, dkg_ref, dw_ref, db_ref):
        i = pl.program_id(0)

        @pl.when(i == 0)
        def _():
            dqg_ref[...] = jnp.zeros_like(dqg_ref)
            dkg_ref[...] = jnp.zeros_like(dkg_ref)
            dw_ref[...] = jnp.zeros_like(dw_ref)
            db_ref[...] = jnp.zeros_like(db_ref)

        latent = (i >= off).astype(F32)
        lane = lax.broadcasted_iota(jnp.int32, (CHUNK, LANES), 1)
        low = lane < HEAD_DIM
        cs, sn, bdv = cos_ref[...], sin_ref[...], bd_ref[...]

        def fold(b0):
            return jnp.where(low, f_ref[0, b0] + pltpu.roll(f_ref[0, b0 + 1], HEAD_DIM, 1),
                             pltpu.roll(f_ref[1, b0], HEAD_DIM, 1) + f_ref[1, b0 + 1])

        def norm_rope_bwd(xv, dout, gain):
            r = lax.rsqrt(_head_mean(xv * xv, bdv) + EPS)
            n = xv * r
            dy = dout * cs + _swap16(dout * sn, lane)
            dn = dy * gain
            dx = r * (dn - n * _head_mean(dn * n, bdv))
            return dx, jnp.sum(dy * n, axis=0, keepdims=True)

        for a in range(ATTN_WIDTH // LANES):
            cols = slice(a * LANES, (a + 1) * LANES)
            dx, dg = norm_rope_bwd(p_ref[:, cols], dq_ref[:, cols] * (latent * q_scale), qg_ref[...])
            dp_ref[:, cols] = dx.astype(BF16)
            dqg_ref[...] += dg
        dx, dg = norm_rope_bwd(p_ref[:, K_COLS[0]:K_COLS[1]], fold(0), kg_ref[...])
        dp_ref[:, K_COLS[0]:K_COLS[1]] = dx.astype(BF16)
        dkg_ref[...] += dg
        dp_ref[:, V_COLS[0]:V_COLS[1]] = fold(2).astype(BF16)
        for g in range(N_SG_GROUPS):
            su = p_ref[:, SU_COLS[0] + g * LANES:SU_COLS[0] + (g + 1) * LANES]
            sv = p_ref[:, SV_COLS[0] + g * LANES:SV_COLS[0] + (g + 1) * LANES]
            (u, dgelu_su), (vg, dgelu_sv) = _gelu_and_grad(su), _gelu_and_grad(sv)
            xc = vg - jnp.mean(vg, axis=-1, keepdims=True)
            rs = lax.rsqrt(jnp.mean(xc * xc, axis=-1, keepdims=True) + EPS)
            vn = xc * rs
            vnb = vn.astype(BF16)
            mixed = _dot(w_ref[g].astype(BF16), vnb, NN) + b_ref[:, g:g + 1]
            dsg = dsg_ref[:, g * LANES:(g + 1) * LANES].astype(F32) * latent
            du = dsg * mixed
            dmix = dsg * u
            dmb = dmix.astype(BF16)
            db_ref[:, g:g + 1] += jnp.sum(dmix, axis=-1, keepdims=True)
            dw_ref[g] += _dot(dmb, vnb, NT)
            dvn = _dot(wt_ref[g].astype(BF16), dmb, NN)
            dvg = rs * (dvn - jnp.mean(dvn, axis=-1, keepdims=True)
                        - vn * jnp.mean(dvn * vn, axis=-1, keepdims=True))
            dp_ref[:, SU_COLS[0] + g * LANES:SU_COLS[0] + (g + 1) * LANES] = (du * dgelu_su).astype(BF16)
            dp_ref[:, SV_COLS[0] + g * LANES:SV_COLS[0] + (g + 1) * LANES] = (dvg * dgelu_sv).astype(BF16)

    def row(width):
        return pl.BlockSpec((CHUNK, width), lambda i: (i, 0))

    def latent_row(width, col_block):
        return pl.BlockSpec((CHUNK, width), lambda i: (jnp.maximum(i - off, 0), col_block))

    def whole(shape):
        return pl.BlockSpec(shape, lambda i: (0,) * len(shape))

    return pl.pallas_call(
        body, name=name, grid=(TT // CHUNK,),
        out_shape=[jax.ShapeDtypeStruct((TT, IN_WIDTH), BF16), jax.ShapeDtypeStruct((1, LANES), F32),
                   jax.ShapeDtypeStruct((1, LANES), F32),
                   jax.ShapeDtypeStruct((N_SG_GROUPS, CHUNK, CHUNK), F32),
                   jax.ShapeDtypeStruct((CHUNK, N_SG_GROUPS), F32)],
        in_specs=[row(IN_WIDTH), latent_row(ATTN_WIDTH, 0),
                  pl.BlockSpec((2, 4, CHUNK, LANES), lambda i: (0, 0, i, 0)),
                  latent_row(SG_WIDTH, 1), row(LANES), row(LANES), whole((1, LANES)), whole((1, LANES)),
                  whole((LANES, LANES)), whole((N_SG_GROUPS, CHUNK, CHUNK)),
                  whole((N_SG_GROUPS, CHUNK, CHUNK)), whole((CHUNK, N_SG_GROUPS))],
        out_specs=[row(IN_WIDTH), whole((1, LANES)), whole((1, LANES)),
                   whole((N_SG_GROUPS, CHUNK, CHUNK)), whole((CHUNK, N_SG_GROUPS))],
        compiler_params=pltpu.CompilerParams(dimension_semantics=("arbitrary",)),
    )(p, dq, f, dao, cos, sin, qg, kg, bd, w_sp, w_spt, b_spt)


def _attn_fwd(q, kpad, vpad, ao, ctx_rows, name, tq=256):
    TT = q.shape[0]
    T = TT - ctx_rows
    tq = _tile(T, tq)
    off = ctx_rows // tq
    group = 2 * LANES

    def body(q_ref, k_ref, v_ref, ao_in, o_ref, lse_ref):
        del ao_in
        lane = lax.broadcasted_iota(jnp.int32, (tq, LANES), 1)
        lse = jnp.zeros((tq, LANES), F32)
        for a in range(2):
            acc = jnp.zeros((tq, LANES), F32)
            qa = q_ref[:, a * LANES:(a + 1) * LANES]
            for b in range(2):
                s = _dot(qa, k_ref[0, b], NT)
                m = jnp.max(s, axis=-1, keepdims=True)
                e = jnp.exp(s - m)
                l = jnp.sum(e, axis=-1, keepdims=True)
                acc = acc + _dot(e.astype(BF16), v_ref[0, b], NN) * (1.0 / l)
                lse = jnp.where(lane == 2 * a + b, m + jnp.log(l), lse)
            o_ref[:, a * LANES:(a + 1) * LANES] = acc.astype(BF16)
        lse_ref[0] = lse

    kv_spec = pl.BlockSpec((1, 2, TT, LANES), lambda j, i: (j, 0, 0, 0))
    return pl.pallas_call(
        body, name=name, grid=(2, T // tq),
        out_shape=[jax.ShapeDtypeStruct(ao.shape, BF16), jax.ShapeDtypeStruct((2, T, LANES), F32)],
        in_specs=[pl.BlockSpec((tq, group), lambda j, i: (i + off, j)), kv_spec, kv_spec,
                  pl.BlockSpec(memory_space=pl.ANY)],
        out_specs=[pl.BlockSpec((tq, group), lambda j, i: (i, j)),
                   pl.BlockSpec((1, tq, LANES), lambda j, i: (j, i, 0))],
        input_output_aliases={3: 0},
        compiler_params=pltpu.CompilerParams(dimension_semantics=("parallel", "parallel")),
    )(q, kpad, vpad, ao)


def _attn_bwd(q, dao, ao, lse, kpad, vpad, kt, ctx_rows, name, tq=256):
    TT = q.shape[0]
    T = TT - ctx_rows
    tq = _tile(T, tq)
    off = ctx_rows // tq
    group = 2 * LANES

    def body(q_ref, do_ref, o_ref, lse_ref, k_ref, v_ref, kt_ref, dq_ref, f_ref):
        i = pl.program_id(1)

        @pl.when(i == 0)
        def _():
            f_ref[...] = jnp.zeros_like(f_ref)

        ktv = kt_ref[0]
        lse_t = lse_ref[0].T
        row = lax.broadcasted_iota(jnp.int32, (SUBLANES, LANES), 0)
        lane = lax.broadcasted_iota(jnp.int32, (SUBLANES, LANES), 1)
        half_ones = (jnp.where(lane < HEAD_DIM, 0, 1) == row).astype(BF16)
        for a in range(2):
            cols = slice(a * LANES, (a + 1) * LANES)
            qa = q_ref[:, cols]
            do32 = do_ref[:, cols].astype(F32)
            doa = do32.astype(BF16)
            hi, lo = _split_bf16(do32 * o_ref[:, cols].astype(F32))
            deltas = _dot(half_ones, hi, NT) + _dot(half_ones, lo, NT)
            halves = []
            for b in range(2):
                h = 2 * a + b
                st = _dot(k_ref[0, b], qa, NT)
                pt = jnp.exp(st - lse_t[h:h + 1, :])
                dpt = _dot(v_ref[0, b], doa, NT)
                dst = (pt * (dpt - deltas[b:b + 1, :])).astype(BF16)
                f_ref[0, b] += _dot(dst, qa, NN)
                f_ref[0, 2 + b] += _dot(pt.astype(BF16), doa, NN)
                halves.append(_dot(ktv, dst, NN))
            dq_ref[:, cols] = jnp.concatenate(halves, axis=0).T

    kv_spec = pl.BlockSpec((1, 2, TT, LANES), lambda j, i: (j, 0, 0, 0))
    out_cols = pl.BlockSpec((tq, group), lambda j, i: (i, j))
    return pl.pallas_call(
        body, name=name, grid=(2, T // tq),
        out_shape=[jax.ShapeDtypeStruct((T, ATTN_WIDTH), F32), jax.ShapeDtypeStruct((2, 4, TT, LANES), F32)],
        in_specs=[pl.BlockSpec((tq, group), lambda j, i: (i + off, j)), out_cols, out_cols,
                  pl.BlockSpec((1, tq, LANES), lambda j, i: (j, i, 0)),
                  kv_spec, kv_spec, pl.BlockSpec((1, HEAD_DIM, TT), lambda j, i: (j, 0, 0))],
        out_specs=[out_cols, pl.BlockSpec((1, 4, TT, LANES), lambda j, i: (j, 0, 0, 0))],
        compiler_params=pltpu.CompilerParams(dimension_semantics=("parallel", "arbitrary")),
    )(q, dao, ao, lse, kpad, vpad, kt)


def _final_fwd_bwd(h, g, target, y, gt, name):
    R, Dm = h.shape
    tm = _tile(R, ROW_BLOCK, 8)

    def body(h_ref, g_ref, t_ref, y_ref, gt_ref, dh_ref, loss_ref, dg_ref, dy_ref, dgt_ref, dsum_ref):
        i = pl.program_id(0)

        @pl.when(i == 0)
        def _():
            for ref in (loss_ref, dg_ref, dgt_ref, dsum_ref):
                ref[...] = jnp.zeros_like(ref)

        hv = h_ref[...]
        r = lax.rsqrt(jnp.mean(hv * hv, axis=-1, keepdims=True) + EPS)
        n = hv * r
        diff = n * g_ref[...] - t_ref[...]
        loss_ref[...] += jnp.sum(diff * diff)
        dout = diff * (1.0 / Dm)
        dg_ref[...] += jnp.sum(dout * n, axis=0, keepdims=True)
        dn = dout * g_ref[...]
        dh = r * (dn - n * jnp.mean(dn * n, axis=-1, keepdims=True))
        dh_ref[...] = dh
        _gate_grads(dh, y_ref, gt_ref, dy_ref, dgt_ref, dsum_ref)

    vec = jax.ShapeDtypeStruct((1, Dm), F32)
    return pl.pallas_call(
        body, name=name, grid=(R // tm,),
        out_shape=[jax.ShapeDtypeStruct((R, Dm), F32), jax.ShapeDtypeStruct((1, LANES), F32), vec,
                   jax.ShapeDtypeStruct((R, Dm), BF16), vec, vec],
        in_specs=[_row_spec(tm, Dm), _vec_spec(Dm), _row_spec(tm, Dm), _row_spec(tm, Dm), _vec_spec(Dm)],
        out_specs=[_row_spec(tm, Dm), _vec_spec(LANES), _vec_spec(Dm), _row_spec(tm, Dm), _vec_spec(Dm),
                   _vec_spec(Dm)],
        compiler_params=pltpu.CompilerParams(dimension_semantics=("arbitrary",)),
    )(h, g, target, y, gt)


MOD_ROWS = 16


def _mod_fwd(c_rows, w_mod, name):
    L, Dm, n = w_mod.shape

    def body(c_ref, w_ref, o_ref):
        o_ref[0] = _dot3(_silu(c_ref[...]), w_ref[0], NN)

    return pl.pallas_call(
        body, name=name, grid=(L,),
        out_shape=jax.ShapeDtypeStruct((L, MOD_ROWS, n), F32),
        in_specs=[pl.BlockSpec((MOD_ROWS, Dm), lambda l: (0, 0)), pl.BlockSpec((1, Dm, n), lambda l: (l, 0, 0))],
        out_specs=pl.BlockSpec((1, MOD_ROWS, n), lambda l: (l, 0, 0)),
        compiler_params=pltpu.CompilerParams(dimension_semantics=("parallel",)),
    )(c_rows, w_mod)


def _mod_bwd(c_rows_t, dmod, w_mod, name):
    L, Dm, n = w_mod.shape

    def body(ct_ref, d_ref, w_ref, gw_ref, ds_ref):
        dm = d_ref[0]
        gw_ref[0] = _dot3(_silu(ct_ref[...]), dm, NN)
        ds_ref[0] = _dot3(dm[:MOD_ROWS], w_ref[0], NT)

    return pl.pallas_call(
        body, name=name, grid=(L,),
        out_shape=[jax.ShapeDtypeStruct((L, Dm, n), F32), jax.ShapeDtypeStruct((L, MOD_ROWS, Dm), F32)],
        in_specs=[pl.BlockSpec((Dm, LANES), lambda l: (0, 0)), pl.BlockSpec((1, LANES, n), lambda l: (l, 0, 0)),
                  pl.BlockSpec((1, Dm, n), lambda l: (l, 0, 0))],
        out_specs=[pl.BlockSpec((1, Dm, n), lambda l: (l, 0, 0)),
                   pl.BlockSpec((1, MOD_ROWS, Dm), lambda l: (l, 0, 0))],
        compiler_params=pltpu.CompilerParams(dimension_semantics=("parallel",)),
    )(c_rows_t, dmod, w_mod)


def _adam_update(w, g, m, v):
    c1 = 1.0 - ADAM_B1 ** ADAM_STEP
    c2 = 1.0 - ADAM_B2 ** ADAM_STEP
    mn = ADAM_B1 * m + (1.0 - ADAM_B1) * g
    vn = ADAM_B2 * v + (1.0 - ADAM_B2) * (g * g)
    return -ADAM_LR * ((mn / c1) / (jnp.sqrt(vn / c2) + ADAM_EPS) + ADAM_WD * w), mn, vn


def _adamw(w, g, m, v, name):
    R, Cw = w.shape
    tm = _tile(R, ADAM_ROWS, 8)

    def body(w_ref, g_ref, m_ref, v_ref, d_ref, mo_ref, vo_ref):
        d_ref[...], mo_ref[...], vo_ref[...] = _adam_update(w_ref[...], g_ref[...], m_ref[...], v_ref[...])

    spec = pl.BlockSpec((tm, Cw), lambda i: (i, 0))
    return pl.pallas_call(
        body, name=name, grid=(R // tm,),
        out_shape=[jax.ShapeDtypeStruct((R, Cw), F32)] * 3,
        in_specs=[spec] * 4, out_specs=[spec] * 3,
        compiler_params=pltpu.CompilerParams(dimension_semantics=("parallel",)),
    )(w, g, m, v)


def _adamw_recv(w, m, v, recvs, name):
    L, R, n = w.shape
    tm = _tile(R, ADAM_ROWS, 8)
    nblk = R // tm
    parts = [r.reshape(N_DEV, R, n) for r in recvs]

    def body(*refs):
        w_ref, m_ref, v_ref = refs[:3]
        part_refs = refs[3:3 + L]
        g_ref, d_ref, mo_ref, vo_ref, gsum = refs[3 + L:]
        l = pl.program_id(0)
        for ll in range(L):
            @pl.when(l == ll)
            def _(ll=ll):
                acc = part_refs[ll][0].astype(F32)
                for s in range(1, N_DEV):
                    acc = acc + part_refs[ll][s].astype(F32)
                gsum[...] = acc
        g = gsum[...]
        g_ref[0] = g
        d_ref[0], mo_ref[0], vo_ref[0] = _adam_update(w_ref[0], g, m_ref[0], v_ref[0])

    def part_spec(ll):
        return pl.BlockSpec((N_DEV, tm, n), lambda l, i: (0, jnp.where(l == ll, i, jnp.where(l < ll, 0, nblk - 1)), 0))

    spec = pl.BlockSpec((1, tm, n), lambda l, i: (l, i, 0))
    return pl.pallas_call(
        body, name=name, grid=(L, nblk),
        out_shape=[jax.ShapeDtypeStruct((L, R, n), F32)] * 4,
        in_specs=[spec] * 3 + [part_spec(ll) for ll in range(L)], out_specs=[spec] * 4,
        scratch_shapes=[pltpu.VMEM((tm, n), F32)],
        compiler_params=pltpu.CompilerParams(dimension_semantics=("parallel", "parallel")),
    )(w, m, v, *parts)


def _pack(parts, row_mult=8):
    flat, offs, pos = [], [], 0
    for t in parts:
        t = t.reshape(-1).astype(F32)
        size = -(-t.shape[0] // LANES) * LANES
        flat.append(jnp.pad(t, (0, size - t.shape[0])))
        offs.append(pos)
        pos += size
    total = -(-pos // (LANES * row_mult)) * (LANES * row_mult)
    if total > pos:
        flat.append(jnp.zeros((total - pos,), F32))
    return jnp.concatenate(flat).reshape(-1, LANES), offs


def _take(buf, off, shape):
    size = math.prod(shape)
    return buf[..., off:off + size].reshape(buf.shape[:-1] + tuple(shape))


def _rope_tables(T, ctx_rows):
    pos = jnp.arange(T)
    row = (pos // GRID_W).astype(F32)
    col = (pos % GRID_W).astype(F32)
    half = HEAD_DIM // 4
    inv = ROPE_THETA ** (-jnp.arange(0, 2 * half, 2, dtype=F32) / (2 * half))
    ang_r, ang_c = row[:, None] * inv[None, :], col[:, None] * inv[None, :]
    cos = jnp.concatenate([jnp.cos(ang_r)] * 2 + [jnp.cos(ang_c)] * 2, axis=1)
    sin = jnp.concatenate([-jnp.sin(ang_r), jnp.sin(ang_r), -jnp.sin(ang_c), jnp.sin(ang_c)], axis=1)
    cos = jnp.concatenate([jnp.ones((ctx_rows, HEAD_DIM), F32), cos], axis=0)
    sin = jnp.concatenate([jnp.zeros((ctx_rows, HEAD_DIM), F32), sin], axis=0)
    return jnp.tile(cos, (1, 2)), jnp.tile(sin, (1, 2))


def kernel(x, c, ctx, c_ctx, w_mod, b_mod, g_mix, g_ffn, w_ffn_in, w_ffn_out, w_in, q_gain, k_gain, w_sp, b_sp, w_out, w_pw1, b_pw1, w_dw, b_dw, ln_g, ln_b, w_pw2, b_pw2, g_final, loss_target, m_c_ctx, m_w_mod, m_b_mod, m_g_mix, m_g_ffn, m_w_ffn_in, m_w_ffn_out, m_w_in, m_q_gain, m_k_gain, m_w_sp, m_b_sp, m_w_out, m_w_pw1, m_b_pw1, m_w_dw, m_b_dw, m_ln_g, m_ln_b, m_w_pw2, m_b_pw2, m_g_final, v_c_ctx, v_w_mod, v_b_mod, v_g_mix, v_g_ffn, v_w_ffn_in, v_w_ffn_out, v_w_in, v_q_gain, v_k_gain, v_w_sp, v_b_sp, v_w_out, v_w_pw1, v_b_pw1, v_w_dw, v_b_dw, v_ln_g, v_ln_b, v_w_pw2, v_b_pw2, v_g_final):
    weights = dict(c_ctx=c_ctx, w_mod=w_mod, b_mod=b_mod, g_mix=g_mix, g_ffn=g_ffn, w_ffn_in=w_ffn_in,
                   w_ffn_out=w_ffn_out, w_in=w_in, q_gain=q_gain, k_gain=k_gain, w_sp=w_sp, b_sp=b_sp,
                   w_out=w_out, w_pw1=w_pw1, b_pw1=b_pw1, w_dw=w_dw, b_dw=b_dw, ln_g=ln_g, ln_b=ln_b,
                   w_pw2=w_pw2, b_pw2=b_pw2, g_final=g_final)
    moments_m = dict(c_ctx=m_c_ctx, w_mod=m_w_mod, b_mod=m_b_mod, g_mix=m_g_mix, g_ffn=m_g_ffn,
                     w_ffn_in=m_w_ffn_in, w_ffn_out=m_w_ffn_out, w_in=m_w_in, q_gain=m_q_gain,
                     k_gain=m_k_gain, w_sp=m_w_sp, b_sp=m_b_sp, w_out=m_w_out, w_pw1=m_w_pw1,
                     b_pw1=m_b_pw1, w_dw=m_w_dw, b_dw=m_b_dw, ln_g=m_ln_g, ln_b=m_ln_b, w_pw2=m_w_pw2,
                     b_pw2=m_b_pw2, g_final=m_g_final)
    moments_v = dict(c_ctx=v_c_ctx, w_mod=v_w_mod, b_mod=v_b_mod, g_mix=v_g_mix, g_ffn=v_g_ffn,
                     w_ffn_in=v_w_ffn_in, w_ffn_out=v_w_ffn_out, w_in=v_w_in, q_gain=v_q_gain,
                     k_gain=v_k_gain, w_sp=v_w_sp, b_sp=v_b_sp, w_out=v_w_out, w_pw1=v_w_pw1,
                     b_pw1=v_b_pw1, w_dw=v_w_dw, b_dw=v_b_dw, ln_g=v_ln_g, ln_b=v_ln_b, w_pw2=v_w_pw2,
                     b_pw2=v_b_pw2, g_final=v_g_final)
    names = list(weights)

    T, C = x.shape[1], ctx.shape[1]
    Dm = D_MODEL
    me = 4 * lax.axis_index("x") + 2 * lax.axis_index("y") + lax.axis_index("c")
    h0 = x[0]
    ctx2 = ctx[0]
    target = loss_target[0]

    small_sharded = (("w_dw", w_dw[0]), ("b_pw1", b_pw1), ("b_dw", b_dw), ("ln_g", ln_g), ("ln_b", ln_b),
                     ("b_pw2", b_pw2))
    buf1, offs1 = _pack([c] + [t for _, t in small_sharded])
    w_in_t, m_w_in_t, v_w_in_t = (jnp.swapaxes(t, 1, 2) for t in (w_in, m_w_in, v_w_in))
    w_ffi_t, m_w_ffi_t, v_w_ffi_t = (jnp.swapaxes(t, 1, 2) for t in (w_ffn_in, m_w_ffn_in, v_w_ffn_in))
    got1, W_in_t, W_out = _all_gather([buf1, w_in_t[0].astype(BF16), w_out[0].astype(BF16)], "gather_cond", False)
    got1 = got1.reshape(N_DEV, -1)
    c_all = _take(got1, offs1[0], (Dm,))
    full_small = {}
    for (nm, t), off in zip(small_sharded, offs1[1:]):
        seg = _take(got1, off, t.shape)
        full_small[nm] = jnp.moveaxis(seg, 0, -2).reshape(t.shape[:-1] + (N_DEV * t.shape[-1],))
    w_dw_f, b_pw1_f = full_small["w_dw"], full_small["b_pw1"]
    b_dw_f, ln_g_f, ln_b_f, b_pw2_f = (full_small[k] for k in ("b_dw", "ln_g", "ln_b", "b_pw2"))

    c_rows = jnp.concatenate([c_all, c_ctx[None, :], jnp.zeros((MOD_ROWS - N_DEV - 1, Dm), F32)], axis=0)
    mod_part = _mod_fwd(c_rows, w_mod, "mod_fwd")
    n_mod = w_mod.shape[2]
    got2 = _all_gather([mod_part.reshape(-1, LANES)], "gather_mod", True)[0]
    mod_all = got2.reshape(N_DEV, 2, MOD_ROWS, n_mod).transpose(1, 2, 0, 3).reshape(2, MOD_ROWS, N_DEV * n_mod)
    mod_all = mod_all + b_mod[:, None, :]
    my_mod = lax.dynamic_index_in_dim(mod_all, me, axis=1, keepdims=False)
    sh1, sc1, gt1, sh2, sc2, gt2 = ([my_mod[l:l + 1, k * Dm:(k + 1) * Dm] for l in range(2)] for k in range(6))
    csh1 = mod_all[0, N_DEV:N_DEV + 1, 0:Dm]
    csc1 = mod_all[0, N_DEV:N_DEV + 1, Dm:2 * Dm]

    behind = got2[0:1, 0:1] * 0.0
    gather_groups = [[w_ffi_t[0], w_ffn_out[0]], [w_pw1[0], w_pw2[0]], [w_ffi_t[1], w_ffn_out[1]]]
    gathers = [_push_begin([(t + behind).astype(BF16) for t in grp], True, f"gather_start{k}")
               for k, grp in enumerate(gather_groups)]
    started = sum(h[4][0:1, 0:1] for h in gathers)

    def gathered(k, after):
        return _push_end(gathers[k], after, f"gather_wait{k}")[1]

    def ffn_weights(k, after):
        wi, wo = gathered(k, after)
        return wi.reshape(N_DEV, FF_SHARD, Dm), wo.reshape(N_DEV // 2, FF_SHARD, Dm)

    def col_gathered(t, n):
        return t.reshape(N_DEV, Dm, n).transpose(1, 0, 2).reshape(Dm, N_DEV * n)

    W_ffi, W_ffo = [None, None], [None, None]

    g_mix_r = [g_mix[l:l + 1] for l in range(2)]
    g_ffn_r = [g_ffn[l:l + 1] for l in range(2)]
    g_fin = g_final[None, :]

    cos, sin = _rope_tables(T, C)
    qg = jnp.tile(q_gain, (1, 2))
    kg = jnp.tile(k_gain, (1, 2))
    lane_head = jnp.arange(LANES) // HEAD_DIM
    bd = (lane_head[:, None] == lane_head[None, :]).astype(BF16)
    w_sp0 = w_sp[0]
    w_spt0 = w_sp0.transpose(0, 2, 1)
    b_spt0 = b_sp[0].T

    XM = _norm_mod_fwd_cat(ctx2, h0, g_mix_r[0], csc1, csh1, sc1[0] + started, sh1[0], "norm_mix0")
    P = _mm(XM, W_in_t, "nt", "in_proj", tn=IN_WIDTH)
    qh, kpad, vpad, kt, ao = _mix_prep_fwd(P, C, cos, sin, qg, kg, bd, w_sp0, b_spt0, "mix_prep")
    ao, lse = _attn_fwd(qh, kpad, vpad, ao, C, "attn_fwd")
    h1, y0, xf0 = _mm(ao, W_out, "nn", "out_proj", res=h0, gate=gt1[0], raw_out=True,
                      norm=(g_ffn_r[0], sc2[0], sh2[0]))

    def ffn_fwd(h_in, xf, l, norm_next):
        W_ffi[l], W_ffo[l] = ffn_weights(2 * l, xf)
        gu, act = _ffn_in_swiglu(xf, W_ffi[l], f"ffn_in{l}")
        outs = _mm_sum_shards(act, W_ffo[l], "nn", f"ffn_out{l}", res=h_in, gate=gt2[l], raw_out=True,
                              norm=norm_next)
        return tuple(outs) + (None,) * (3 - len(outs)) + (gu, act)

    h2, f0, xm1, gu0, act0 = ffn_fwd(h1, xf0, 0, (g_mix_r[1], sc1[1], sh1[1]))

    W_pw1, W_pw2 = gathered(1, xm1)
    W_pw1 = col_gathered(W_pw1, 2 * Dm // N_DEV)
    ag = _mm(xm1, W_pw1, "nn", "pw1", BF16, bias=b_pw1_f)
    hg = _glu_fwd(ag, "glu")
    hd, hs = _conv_ln_fwd(hg, w_dw_f, b_dw_f, ln_g_f, ln_b_f, "conv_ln")
    h3, y1, xf1 = _mm(hs, W_pw2, "nn", "pw2", bias=b_pw2_f, res=h2, gate=gt1[1], raw_out=True,
                      norm=(g_ffn_r[1], sc2[1], sh2[1]))
    h4, f1, _, gu1, act1 = ffn_fwd(h3, xf1, 1, None)

    dh4, sq_err, dg_final, df1, dgt2_1, _ = _final_fwd_bwd(h4, g_fin, target, f1, gt2[1], "loss_head")
    loss_local = (0.5 / Dm) * sq_err[0, 0:1]

    def col_shards(g, n):
        return g.reshape(Dm, N_DEV, n).transpose(1, 0, 2).reshape(N_DEV * Dm, n)

    def exchange_begin(k, parts):
        return _push_begin(parts, False, f"exchange_start{k}")

    def zero_of(handle):
        return handle[4][0:1, 0:1]

    def ffn_bwd(df, xf, gu, act, l):
        dw_out = _mm_tn_shard_rows(act, df, f"ffn_out_dw{l}", BF16)
        dgu = _ffn_out_dx_swiglu(df, W_ffo[l], gu, f"ffn_out_dx{l}").reshape(N_DEV, T, FF_SHARD)
        dw_in = _mm_tn_shard_rows(dgu, xf, f"ffn_in_dw{l}", BF16)
        dxf = _mm_sum_shards(dgu, W_ffi[l], "nn", f"ffn_in_dx{l}", BF16, tm=256)
        return dw_in, dw_out, dxf

    dW_ffi1, dW_ffo1, dxf1 = ffn_bwd(df1, xf1, gu1, act1, 1)
    ex0 = exchange_begin(0, [dW_ffi1.reshape(2 * D_FF, Dm), dW_ffo1.reshape(D_FF, Dm)])
    dh3, da, dsh, dy1, dgt1_1, db_pw2 = _norm_mod_bwd(h3, g_ffn_r[1], sc2[1], dxf1, dh4, "norm_ffn_bwd1",
                                                       gate=(y1, gt1[1] + zero_of(ex0)))
    dmod_ffn1 = (dsh, da * g_ffn_r[1], dgt2_1)
    dg_ffn1 = da * (1.0 + sc2[1])

    dW_pw2 = _mm(hs, dy1, "tn", "pw2_dw", BF16, tk=2048)
    dhs = _mm(dy1, W_pw2, "nt", "pw2_dx", BF16)
    dhd, dln_g, dln_b, db_dw = _ln_silu_bwd(dhs, hd, ln_g_f, ln_b_f, "ln_silu_bwd")
    dhg, dw_dw = _conv_bwd(dhd, hg, w_dw_f, "conv_bwd")
    dag, db_pw1 = _glu_bwd(ag, dhg, "glu_bwd")
    dW_pw1 = _mm(xm1, dag, "tn", "pw1_dw", BF16, tk=2048)
    dxm1 = _mm(dag, W_pw1, "nt", "pw1_dx", BF16, tk=2048)
    ex1 = exchange_begin(1, [col_shards(dW_pw1, 2 * Dm // N_DEV), dW_pw2])
    dh2, da, dsh, df0, dgt2_0, _ = _norm_mod_bwd(h2, g_mix_r[1], sc1[1], dxm1, dh3, "norm_mix1_bwd",
                                                 gate=(f0, gt2[0] + zero_of(ex1)))
    dmod_mix1 = (dsh, da * g_mix_r[1], dgt1_1)
    dg_mix1 = da * (1.0 + sc1[1])

    dW_ffi0, dW_ffo0, dxf0 = ffn_bwd(df0, xf0, gu0, act0, 0)
    ex2 = exchange_begin(2, [dW_ffi0.reshape(2 * D_FF, Dm), dW_ffo0.reshape(D_FF, Dm)])
    dh1, da, dsh, dy0, dgt1_0, _ = _norm_mod_bwd(h1, g_ffn_r[0], sc2[0], dxf0, dh2, "norm_ffn_bwd0",
                                                 gate=(y0, gt1[0] + zero_of(ex2)))
    dmod_ffn0 = (dsh, da * g_ffn_r[0], dgt2_0)
    dg_ffn0 = da * (1.0 + sc2[0])

    dW_out = _mm(ao, dy0, "tn", "out_proj_dw", BF16, tk=2048)
    ex_out = exchange_begin(4, [dW_out])
    dao = _mm(dy0, W_out + zero_of(ex_out).astype(BF16), "nt", "out_proj_dx", BF16)
    dq, f_acc = _attn_bwd(qh, dao, ao, lse, kpad, vpad, kt, C, "attn_bwd")
    dP, dqg, dkg, dw_sp0, db_spt0 = _mix_prep_bwd(P, dq, f_acc, dao, C, cos, sin, qg, kg, bd, w_sp0, w_spt0,
                                                  b_spt0, "mix_prep_bwd")
    dW_in_t = _mm(dP, XM, "tn", "in_proj_dw", BF16, tm=896, tk=2176)
    dXM = _mm(dP, W_in_t, "nn", "in_proj_dx", BF16, tk=IN_WIDTH)
    dh0, da, dsh = _norm_mod_bwd(h0, g_mix_r[0], sc1[0], dXM, dh1, "norm_mix0_bwd", dxm_row_off=C)
    _, dac, dcsh = _norm_mod_bwd(ctx2, g_mix_r[0], csc1, dXM, None, "norm_ctx_bwd")
    dmod_mix0 = (dsh, da * g_mix_r[0], dgt1_0)
    dg_mix0 = da * (1.0 + sc1[0]) + dac * (1.0 + csc1)
    dcmod = jnp.concatenate([dcsh, dac * g_mix_r[0]], axis=1)

    dmod_mine = jnp.stack([jnp.concatenate(dmod_mix0 + dmod_ffn0, axis=1)[0],
                           jnp.concatenate(dmod_mix1 + dmod_ffn1, axis=1)[0]])

    small_grads = [
        ("loss", loss_local), ("g_final", dg_final), ("g_mix", jnp.concatenate([dg_mix0, dg_mix1])),
        ("g_ffn", jnp.concatenate([dg_ffn0, dg_ffn1])),
        ("q_gain", dqg[:, :HEAD_DIM] + dqg[:, HEAD_DIM:]), ("k_gain", dkg[:, :HEAD_DIM] + dkg[:, HEAD_DIM:]),
        ("w_sp", dw_sp0[None]), ("b_sp", db_spt0.T[None]), ("b_pw1", db_pw1), ("w_dw", dw_dw[None]),
        ("b_dw", db_dw), ("ln_g", dln_g), ("ln_b", dln_b), ("b_pw2", db_pw2), ("dcmod", dcmod),
        ("dmod", dmod_mine),
    ]
    buf3, offs3 = _pack([t for _, t in small_grads])
    got3 = _all_gather([buf3], "gather_small_grads", True)[0].reshape(N_DEV, buf3.shape[0], LANES)
    sum3 = _sum_devices(got3, "sum_small_grads").reshape(-1)
    off3 = {nm: off for (nm, _), off in zip(small_grads, offs3)}
    shape3 = {nm: t.shape for nm, t in small_grads}

    def summed(nm):
        return _take(sum3, off3[nm], shape3[nm])

    loss = summed("loss")[0]
    dcmod_sum = summed("dcmod")
    dmod_rows = _take(got3.reshape(N_DEV, -1), off3["dmod"], (2, 6 * Dm)).transpose(1, 0, 2)
    ctx_row = jnp.concatenate([jnp.pad(dcmod_sum, ((0, 0), (0, 4 * Dm))), jnp.zeros((1, 6 * Dm), F32)])
    dmod_all = jnp.concatenate([dmod_rows, ctx_row[:, None, :],
                                jnp.zeros((2, LANES - N_DEV - 1, 6 * Dm), F32)], axis=1)
    grads = {}
    grads["b_mod"] = summed("dmod") + ctx_row
    dmod_shard = lax.dynamic_slice_in_dim(dmod_all, me * n_mod, n_mod, axis=2)
    c_rows_t = jnp.pad(c_rows.T, ((0, 0), (0, LANES - MOD_ROWS)))
    grads["w_mod"], ds_part = _mod_bwd(c_rows_t, dmod_shard, w_mod, "mod_bwd")

    buf4, _ = _pack([ds_part[0, N_DEV]])
    got4 = _all_gather([buf4], "gather_c_ctx_grad", True)[0].reshape(N_DEV, buf4.shape[0], LANES)
    ds_ctx = _sum_devices(got4, "sum_c_ctx_grad").reshape(-1)[:Dm]
    behind_small = (ds_ctx[0:1] * 0.0).astype(BF16)
    ex3 = exchange_begin(3, [dW_in_t + behind_small])
    grads["c_ctx"] = ds_ctx * _dsilu(c_ctx) + zero_of(ex3)[0]

    for nm in ("g_final", "g_mix", "g_ffn", "q_gain", "k_gain", "w_sp", "b_sp"):
        grads[nm] = summed(nm).reshape(weights[nm].shape)
    for nm in ("b_pw1", "w_dw", "b_dw", "ln_g", "ln_b", "b_pw2"):
        n_loc = weights[nm].shape[-1]
        grads[nm] = lax.dynamic_slice_in_dim(summed(nm), me * n_loc, n_loc, axis=-1).reshape(weights[nm].shape)

    delta, new_m, new_v = {}, {}, {}
    shp = w_mod.shape
    outs = _adamw(w_mod.reshape(-1, shp[-1]), grads["w_mod"].reshape(-1, shp[-1]),
                  m_w_mod.reshape(-1, shp[-1]), v_w_mod.reshape(-1, shp[-1]), "adamw_w_mod")
    delta["w_mod"], new_m["w_mod"], new_v["w_mod"] = (o.reshape(shp) for o in outs)
    big_names = ("w_mod", "w_ffn_in", "w_ffn_out", "w_in", "w_out", "w_pw1", "w_pw2")
    small_names = [nm for nm in names if nm not in big_names]
    packs = [_pack([src[nm] for nm in small_names]) for src in (weights, grads, moments_m, moments_v)]
    offs_s = packs[0][1]
    outs = _adamw(*[pk[0] for pk in packs], "adamw_small")
    for o, dst in zip(outs, (delta, new_m, new_v)):
        o = o.reshape(-1)
        for nm, off in zip(small_names, offs_s):
            dst[nm] = _take(o, off, weights[nm].shape)

    def exchanged(k, handle, after):
        return _push_end(handle, after, f"exchange_wait{k}")[1]

    def adamw_big(nm, parts, transposed=False, wmv=None):
        w3, m3, v3 = wmv if wmv is not None else (weights[nm], moments_m[nm], moments_v[nm])
        outs4 = _adamw_recv(w3, m3, v3, parts, f"adamw_{nm}")
        if transposed:
            outs4 = [jnp.swapaxes(t, 1, 2) for t in outs4]
        grads[nm], delta[nm], new_m[nm], new_v[nm] = outs4

    r_ffi1, r_ffo1 = exchanged(0, ex0, outs[0])
    r_pw1, r_pw2 = exchanged(1, ex1, outs[0])
    r_ffi0, r_ffo0 = exchanged(2, ex2, outs[0])
    r_out, = exchanged(4, ex_out, outs[0])
    adamw_big("w_ffn_in", [r_ffi0, r_ffi1], True, (w_ffi_t, m_w_ffi_t, v_w_ffi_t))
    adamw_big("w_ffn_out", [r_ffo0, r_ffo1])
    adamw_big("w_pw1", [r_pw1])
    adamw_big("w_pw2", [r_pw2])
    adamw_big("w_out", [r_out])
    r_in, = exchanged(3, ex3, delta["w_out"])
    adamw_big("w_in", [r_in], True, (w_in_t, m_w_in_t, v_w_in_t))

    return (loss, dh0[None], *[grads[n] for n in names], *[delta[n] for n in names],
            *[new_m[n] for n in names], *[new_v[n] for n in names])
```

```python
import math

import jax
import jax.numpy as jnp
from jax import lax
from jax.experimental import pallas as pl
from jax.experimental.pallas import tpu as pltpu

F32 = jnp.float32
BF16 = jnp.bfloat16
MESH = pl.DeviceIdType.MESH

N_DEV = 8
D_MODEL = 1024
EPS = 1e-6
HEAD_DIM = 64
ATTN_WIDTH = 512
KV_WIDTH = 128
SG_WIDTH = 512
N_SG_GROUPS = 4
CHUNK = 128
IN_WIDTH = 1792
D_FF = 2816
FF_SHARD = 2 * D_FF // N_DEV
CONV_WIDTH = 31
CONV_HALO = 16
GRID_W = 64
ROPE_THETA = 10000.0
LANES = 128
SUBLANES = 8
ROW_BLOCK = 512
ADAM_ROWS = 256
ADAM_LR, ADAM_B1, ADAM_B2, ADAM_EPS, ADAM_WD, ADAM_STEP = 0.001, 0.9, 0.999, 1e-08, 0.01, 10


def _tile(n, target, mult=LANES):
    best = None
    for t in range(mult, min(n, target) + 1, mult):
        if n % t == 0:
            best = t
    return best if best is not None else n


def _sigmoid(x):
    return 1.0 / (1.0 + jnp.exp(-x))


def _silu(x):
    return x * _sigmoid(x)


def _dsilu(x):
    s = _sigmoid(x)
    return s * (1.0 + x * (1.0 - s))


_GELU_K = math.sqrt(2.0 / math.pi)


def _gelu(x):
    return 0.5 * x * (1.0 + jnp.tanh(_GELU_K * (x + 0.044715 * x * x * x)))


def _gelu_and_grad(x):
    x2 = x * x
    t = jnp.tanh(_GELU_K * x * (1.0 + 0.044715 * x2))
    half = 0.5 * (1.0 + t)
    return x * half, half + 0.5 * x * (1.0 - t * t) * _GELU_K * (1.0 + 3.0 * 0.044715 * x2)


def _split_bf16(x):
    hi = x.astype(BF16)
    lo = (x - hi.astype(F32)).astype(BF16)
    return hi, lo


def _dot(a, b, dims):
    return lax.dot_general(a, b, (dims, ((), ())), preferred_element_type=F32)


def _dot3(a, b, dims):
    ah, al = _split_bf16(a)
    bh, bl = _split_bf16(b)
    return _dot(ah, bh, dims) + _dot(ah, bl, dims) + _dot(al, bh, dims)


NN = ((1,), (0,))
NT = ((1,), (1,))
TN = ((0,), (0,))


def _all_gather(xs, name, in_vmem):
    n_arr = len(xs)

    def body(*refs):
        x_refs, out_refs = refs[:n_arr], refs[n_arr:2 * n_arr]
        send_sems, recv_sems, local_sems = refs[2 * n_arr:]
        x, y, c = lax.axis_index("x"), lax.axis_index("y"), lax.axis_index("c")
        me, sibling = (x, y, c), (x, y, 1 - c)
        chips = [(1 - x, y), (x, 1 - y), (1 - x, 1 - y)]

        def rows(a, px, py, pc):
            m_per = xs[a].shape[0]
            return out_refs[a].at[pl.ds((4 * px + 2 * py + pc) * m_per, m_per), :]

        def copy(a, k, block, to, src=None):
            return pltpu.make_async_remote_copy(
                src_ref=rows(a, *block) if src is None else src,
                dst_ref=rows(a, *block),
                send_sem=send_sems.at[7 * a + k],
                recv_sem=recv_sems.at[7 * a + k],
                device_id=to,
                device_id_type=MESH,
            )

        mine, first, passed = [], [], []
        for a in range(n_arr):
            mine.append(pltpu.make_async_copy(x_refs[a], rows(a, *me), local_sems.at[a]))
            mine[-1].start()
            first.append(copy(a, 0, me, sibling, src=x_refs[a]))
            first += [copy(a, 1 + j, me, (*chip, c), src=x_refs[a]) for j, chip in enumerate(chips)]
        for cp in first:
            cp.start()
        for a in range(n_arr):
            for j, chip in enumerate(chips):
                copy(a, 1 + j, (*chip, c), me).wait_recv()
                passed.append(copy(a, 4 + j, (*chip, c), sibling))
                passed[-1].start()
        for a in range(n_arr):
            copy(a, 0, sibling, me).wait_recv()
            for j, chip in enumerate(chips):
                copy(a, 4 + j, (*chip, 1 - c), me).wait_recv()
        for cp in first + passed:
            cp.wait_send()
        for cp in mine:
            cp.wait()

    space = pltpu.VMEM if in_vmem else pl.ANY
    return pl.pallas_call(
        body,
        name=name,
        out_shape=[jax.ShapeDtypeStruct((N_DEV * t.shape[0], t.shape[1]), t.dtype) for t in xs],
        in_specs=[pl.BlockSpec(memory_space=space)] * n_arr,
        out_specs=[pl.BlockSpec(memory_space=space)] * n_arr,
        scratch_shapes=[
            pltpu.SemaphoreType.DMA((7 * n_arr,)),
            pltpu.SemaphoreType.DMA((7 * n_arr,)),
            pltpu.SemaphoreType.DMA((n_arr,)),
        ],
    )(*xs)


HBM_SPEC = pl.BlockSpec(memory_space=pltpu.HBM)
SEM_SPEC = pl.BlockSpec(memory_space=pltpu.SEMAPHORE)
DATAFLOW_EFFECT = pltpu.SideEffectType.DATAFLOW_SIDE_EFFECTING


def _peers(x, y, c):
    for k in range(1, N_DEV):
        px = 1 - x if (k >> 2) & 1 else x
        py = 1 - y if (k >> 1) & 1 else y
        pc = 1 - c if k & 1 else c
        yield k - 1, (px, py, pc), 4 * px + 2 * py + pc


def _push_copies(src_refs, land_refs, send_sems, recv_sems, shapes, whole_src):
    x, y, c = lax.axis_index("x"), lax.axis_index("y"), lax.axis_index("c")
    me = 4 * x + 2 * y + c
    for a, (m_per, _) in enumerate(shapes):
        def block(ref, idx, m_per=m_per):
            return ref.at[pl.ds(idx * m_per, m_per), :]

        for k, peer, pidx in _peers(x, y, c):
            src = src_refs[a] if whole_src else block(src_refs[a], pidx)
            sems = dict(send_sem=send_sems.at[N_DEV * a + k], recv_sem=recv_sems.at[N_DEV * a + k],
                        device_id=peer, device_id_type=MESH)
            yield (pltpu.make_async_remote_copy(src_ref=src, dst_ref=block(land_refs[a], me), **sems),
                   pltpu.make_async_remote_copy(src_ref=src, dst_ref=block(land_refs[a], pidx), **sems))


def _own_copies(src_refs, land_refs, recv_sems, shapes, whole_src):
    me = 4 * lax.axis_index("x") + 2 * lax.axis_index("y") + lax.axis_index("c")
    for a, (m_per, _) in enumerate(shapes):
        mine = pl.ds(me * m_per, m_per)
        src = src_refs[a] if whole_src else src_refs[a].at[mine, :]
        yield pltpu.make_async_copy(src, land_refs[a].at[mine, :], recv_sems.at[N_DEV * a + N_DEV - 1])


def _push_begin(srcs, whole_src, name):
    n_arr = len(srcs)
    shapes = [(t.shape[0] if whole_src else t.shape[0] // N_DEV, t.shape[1]) for t in srcs]
    lands = [lax.empty((N_DEV * m, n), t.dtype) for (m, n), t in zip(shapes, srcs)]

    def body(*refs):
        src_refs, land_refs = refs[:n_arr], refs[n_arr:2 * n_arr]
        send_sems, recv_sems = refs[2 * n_arr], refs[2 * n_arr + 1]
        token = refs[-1]
        for outgoing, _ in _push_copies(src_refs, land_refs, send_sems, recv_sems, shapes, whole_src):
            outgoing.start()
        for own in _own_copies(src_refs, land_refs, recv_sems, shapes, whole_src):
            own.start()
        token[...] = jnp.zeros_like(token)

    operands = [pltpu.with_memory_space_constraint(t, pltpu.HBM) for t in list(srcs) + lands]
    outs = pl.pallas_call(
        body, name=name,
        out_shape=(pltpu.SemaphoreType.DMA((N_DEV * n_arr,)), pltpu.SemaphoreType.DMA((N_DEV * n_arr,)),
                   *[pltpu.HBM(t.shape, t.dtype) for t in operands],
                   jax.ShapeDtypeStruct((SUBLANES, LANES), F32)),
        in_specs=[HBM_SPEC] * (2 * n_arr),
        out_specs=(SEM_SPEC, SEM_SPEC, *[HBM_SPEC] * (2 * n_arr), pl.BlockSpec(memory_space=pltpu.VMEM)),
        input_output_aliases={i: 2 + i for i in range(2 * n_arr)},
        compiler_params=pltpu.CompilerParams(has_side_effects=DATAFLOW_EFFECT),
    )(*operands)
    return outs[0], outs[1], list(outs[2:2 + n_arr]), list(outs[2 + n_arr:2 + 2 * n_arr]), outs[-1], whole_src


def _push_end(handle, after, name):
    send_sems, recv_sems, srcs, lands, _, whole_src = handle
    n_arr = len(srcs)
    shapes = [(t.shape[0] // N_DEV, t.shape[1]) for t in lands]

    def body(*refs):
        src_refs, land_refs = refs[:n_arr], refs[n_arr:2 * n_arr]
        send_sems_ref, recv_sems_ref = refs[2 * n_arr], refs[2 * n_arr + 1]
        for outgoing, incoming in _push_copies(src_refs, land_refs, send_sems_ref, recv_sems_ref, shapes, whole_src):
            outgoing.wait_send()
            incoming.wait_recv()
        for own in _own_copies(src_refs, land_refs, recv_sems_ref, shapes, whole_src):
            own.wait()

    outs = pl.pallas_call(
        body, name=name,
        out_shape=tuple(pltpu.HBM(t.shape, t.dtype) for t in srcs + lands),
        in_specs=[HBM_SPEC] * (2 * n_arr) + [SEM_SPEC, SEM_SPEC, pl.BlockSpec(memory_space=pl.ANY)],
        out_specs=tuple([HBM_SPEC] * (2 * n_arr)),
        input_output_aliases={i: i for i in range(2 * n_arr)},
        compiler_params=pltpu.CompilerParams(has_side_effects=DATAFLOW_EFFECT),
    )(*srcs, *lands, send_sems, recv_sems, after)
    return list(outs[:n_arr]), list(outs[n_arr:])


def _sum_devices(r, name, rows_per_step=ADAM_ROWS):
    _, m, n = r.shape
    tm = _tile(m, rows_per_step, 8)

    def body(r_ref, o_ref):
        acc = r_ref[0].astype(F32)
        for s in range(1, N_DEV):
            acc = acc + r_ref[s].astype(F32)
        o_ref[...] = acc

    return pl.pallas_call(
        body,
        name=name,
        grid=(m // tm,),
        out_shape=jax.ShapeDtypeStruct((m, n), F32),
        in_specs=[pl.BlockSpec((N_DEV, tm, n), lambda i: (0, i, 0))],
        out_specs=pl.BlockSpec((tm, n), lambda i: (i, 0)),
        compiler_params=pltpu.CompilerParams(dimension_semantics=("parallel",)),
    )(r)


def _get(ref):
    return ref[0] if len(ref.shape) == 3 else ref[...]


def _put(ref, val):
    if len(ref.shape) == 3:
        ref[0] = val
    else:
        ref[...] = val


def _norm_mod(hv, g, sc, sh):
    r = lax.rsqrt(jnp.mean(hv * hv, axis=-1, keepdims=True) + EPS)
    return (hv * r) * g * (1.0 + sc) + sh


def _mm_call(name, a, b, a_spec, b_spec, out_sds, o_spec, grid, dims, acc_shape, bias=None,
             res=None, gate=None, raw_out=False, vec_spec=None, norm=None):
    nk = grid[2]
    operands, in_specs = [a, b], [a_spec, b_spec]
    if bias is not None:
        operands.append(bias)
        in_specs.append(vec_spec)
    if res is not None:
        operands += [res, gate]
        in_specs += [o_spec, vec_spec]
    if norm is not None:
        assert grid[1] == 1
        operands += list(norm)
        in_specs += [vec_spec] * 3
    out_shape, out_specs = [out_sds], [o_spec]
    if raw_out:
        out_shape.append(jax.ShapeDtypeStruct(out_sds.shape, BF16))
        out_specs.append(o_spec)
    if norm is not None:
        out_shape.append(jax.ShapeDtypeStruct(out_sds.shape, BF16))
        out_specs.append(o_spec)

    def body(*refs):
        it = iter(refs)
        a_ref, b_ref = next(it), next(it)
        bias_ref = next(it) if bias is not None else None
        res_ref, gate_ref = (next(it), next(it)) if res is not None else (None, None)
        norm_refs = (next(it), next(it), next(it)) if norm is not None else None
        o_ref = next(it)
        raw_ref = next(it) if raw_out else None
        xn_ref = next(it) if norm is not None else None
        acc = next(it) if nk > 1 else None
        k = pl.program_id(2)
        part = _dot(_get(a_ref).astype(BF16), _get(b_ref).astype(BF16), dims)

        def finish(y):
            if bias_ref is not None:
                y = y + bias_ref[...]
            if raw_ref is not None:
                raw_ref[...] = y.astype(BF16)
            if res_ref is not None:
                y = res_ref[...] + gate_ref[...] * y
            _put(o_ref, y.astype(out_sds.dtype))
            if xn_ref is not None:
                xn_ref[...] = _norm_mod(y, *[r[...] for r in norm_refs]).astype(BF16)

        if nk == 1:
            finish(part)
        else:
            @pl.when(k == 0)
            def _():
                acc[...] = part

            @pl.when(k > 0)
            def _():
                acc[...] += part

            @pl.when(k == nk - 1)
            def _():
                finish(acc[...])

    outs = pl.pallas_call(
        body,
        name=name,
        grid=grid,
        out_shape=out_shape,
        in_specs=in_specs,
        out_specs=out_specs,
        scratch_shapes=[pltpu.VMEM(acc_shape, F32)] if nk > 1 else [],
        compiler_params=pltpu.CompilerParams(dimension_semantics=("parallel", "parallel", "arbitrary")),
    )(*operands)
    return outs if len(outs) > 1 else outs[0]


def _mm(a, b, mode, name, out_dtype=F32, bias=None, res=None, gate=None, raw_out=False,
        tm=1024, tn=1024, tk=1024, a_row_off=0, norm=None):
    if mode == "nn":
        K, N = b.shape
        M = a.shape[0] - a_row_off
    elif mode == "nt":
        N, K = b.shape
        M = a.shape[0] - a_row_off
    else:
        (K, M), N = a.shape, b.shape[1]
    tm, tn, tk = _tile(M, tm, LANES if mode == "tn" else 2 * SUBLANES), _tile(N, tn), _tile(K, tk)
    off = a_row_off // tm
    dims = {"nn": NN, "nt": NT, "tn": TN}[mode]
    a_spec = (pl.BlockSpec((tk, tm), lambda i, j, k: (k, i)) if mode == "tn"
              else pl.BlockSpec((tm, tk), lambda i, j, k: (i + off, k)))
    b_spec = (pl.BlockSpec((tn, tk), lambda i, j, k: (j, k)) if mode == "nt"
              else pl.BlockSpec((tk, tn), lambda i, j, k: (k, j)))
    return _mm_call(name, a, b, a_spec, b_spec, jax.ShapeDtypeStruct((M, N), out_dtype),
                    pl.BlockSpec((tm, tn), lambda i, j, k: (i, j)), (M // tm, N // tn, K // tk), dims,
                    (tm, tn), bias, res, gate, raw_out, pl.BlockSpec((1, tn), lambda i, j, k: (0, j)), norm)


def _mm_sum_shards(a3, b3, mode, name, out_dtype=F32, res=None, gate=None, raw_out=False, tm=512, norm=None):
    S, M, kk = a3.shape
    N = b3.shape[2] if mode == "nn" else b3.shape[1]
    tm = _tile(M, tm)
    dims = NN if mode == "nn" else NT
    has_res = res is not None

    def body(*refs):
        it = iter(refs)
        a_ref, b_ref = next(it), next(it)
        res_ref, gate_ref = (next(it), next(it)) if has_res else (None, None)
        norm_refs = (next(it), next(it), next(it)) if norm is not None else None
        o_ref = next(it)
        raw_ref = next(it) if raw_out else None
        xn_ref = next(it) if norm is not None else None
        y = _dot(a_ref[0], b_ref[0], dims)
        for s in range(1, S):
            y = y + _dot(a_ref[s], b_ref[s], dims)
        if raw_ref is not None:
            raw_ref[...] = y.astype(BF16)
        if has_res:
            y = res_ref[...] + gate_ref[...] * y
        o_ref[...] = y.astype(out_dtype)
        if xn_ref is not None:
            xn_ref[...] = _norm_mod(y, *[r[...] for r in norm_refs]).astype(BF16)

    tile = pl.BlockSpec((tm, N), lambda i: (i, 0))
    operands = [a3, b3] + ([res, gate] if has_res else []) + (list(norm) if norm is not None else [])
    in_specs = [pl.BlockSpec((S, tm, kk), lambda i: (0, i, 0)), pl.BlockSpec(b3.shape, lambda i: (0, 0, 0))]
    in_specs += [tile, _vec_spec(N)] if has_res else []
    in_specs += [_vec_spec(N)] * 3 if norm is not None else []
    out_shape = [jax.ShapeDtypeStruct((M, N), out_dtype)] + ([jax.ShapeDtypeStruct((M, N), BF16)] if raw_out else [])
    out_shape += [jax.ShapeDtypeStruct((M, N), BF16)] if norm is not None else []
    outs = pl.pallas_call(
        body, name=name, grid=(M // tm,),
        out_shape=out_shape, in_specs=in_specs, out_specs=[tile] * len(out_shape),
        compiler_params=pltpu.CompilerParams(dimension_semantics=("parallel",)),
    )(*operands)
    return outs if len(outs) > 1 else outs[0]


def _mm_tn_shard_rows(a3, b, name, out_dtype, tn=1024, tk=4096):
    S, T, m = a3.shape
    N = b.shape[1]
    tn, tk = _tile(N, tn), _tile(T, tk)
    return _mm_call(name, a3, b, pl.BlockSpec((1, tk, m), lambda i, j, k: (i, k, 0)),
                    pl.BlockSpec((tk, tn), lambda i, j, k: (k, j)), jax.ShapeDtypeStruct((S, m, N), out_dtype),
                    pl.BlockSpec((1, m, tn), lambda i, j, k: (i, 0, j)), (S, N // tn, T // tk), TN, (m, tn))


def _row_spec(tm, width, off=0):
    return pl.BlockSpec((tm, width), lambda i: (i + off, 0))


def _vec_spec(width):
    return pl.BlockSpec((1, width), lambda i: (0, 0))


def _norm_mod_fwd_cat(hc, h, g, csc, csh, sc, sh, name):
    (C, Dm), T = hc.shape, h.shape[0]
    tm = _tile(math.gcd(C, T), ROW_BLOCK, 8)
    off = C // tm

    def body(hc_ref, h_ref, g_ref, csc_ref, csh_ref, sc_ref, sh_ref, o_ref):
        is_ctx = pl.program_id(0) < off
        hv = jnp.where(is_ctx, hc_ref[...], h_ref[...])
        scv = jnp.where(is_ctx, csc_ref[...], sc_ref[...])
        shv = jnp.where(is_ctx, csh_ref[...], sh_ref[...])
        r = lax.rsqrt(jnp.mean(hv * hv, axis=-1, keepdims=True) + EPS)
        o_ref[...] = ((hv * r) * g_ref[...] * (1.0 + scv) + shv).astype(BF16)

    return pl.pallas_call(
        body, name=name, grid=((C + T) // tm,),
        out_shape=jax.ShapeDtypeStruct((C + T, Dm), BF16),
        in_specs=[pl.BlockSpec((tm, Dm), lambda i: (jnp.minimum(i, off - 1), 0)),
                  pl.BlockSpec((tm, Dm), lambda i: (jnp.maximum(i - off, 0), 0))] + [_vec_spec(Dm)] * 5,
        out_specs=_row_spec(tm, Dm),
        compiler_params=pltpu.CompilerParams(dimension_semantics=("parallel",)),
    )(hc, h, g, csc, csh, sc, sh)


def _gate_grads(dh, y_ref, gt_ref, dy_ref, dgt_ref, dsum_ref):
    dy = dh * gt_ref[...]
    dgt_ref[...] += jnp.sum(dh * y_ref[...].astype(F32), axis=0, keepdims=True)
    dsum_ref[...] += jnp.sum(dy, axis=0, keepdims=True)
    dy_ref[...] = dy.astype(BF16)


def _norm_mod_bwd(h, g, sc, dxm, dres, name, dxm_row_off=0, gate=None):
    R, Dm = h.shape
    tm = _tile(math.gcd(R, dxm_row_off) if dxm_row_off else R, ROW_BLOCK, 8)
    off = dxm_row_off // tm
    has_res = dres is not None
    has_gate = gate is not None

    def body(*refs):
        it = iter(refs)
        h_ref, g_ref, sc_ref, dx_ref = next(it), next(it), next(it), next(it)
        dres_ref = next(it) if has_res else None
        y_ref, gt_ref = (next(it), next(it)) if has_gate else (None, None)
        dh_ref, da_ref, dsh_ref = next(it), next(it), next(it)
        gate_out = (next(it), next(it), next(it)) if has_gate else ()
        i = pl.program_id(0)

        @pl.when(i == 0)
        def _():
            for ref in (da_ref, dsh_ref) + gate_out[1:]:
                ref[...] = jnp.zeros_like(ref)

        hv = h_ref[...]
        dx = dx_ref[...].astype(F32)
        r = lax.rsqrt(jnp.mean(hv * hv, axis=-1, keepdims=True) + EPS)
        n = hv * r
        da_ref[...] += jnp.sum(dx * n, axis=0, keepdims=True)
        dsh_ref[...] += jnp.sum(dx, axis=0, keepdims=True)
        dn = dx * (g_ref[...] * (1.0 + sc_ref[...]))
        dh = r * (dn - n * jnp.mean(dn * n, axis=-1, keepdims=True))
        if has_res:
            dh = dh + dres_ref[...]
        dh_ref[...] = dh
        if has_gate:
            _gate_grads(dh, y_ref, gt_ref, *gate_out)

    operands = [h, g, sc, dxm] + ([dres] if has_res else []) + (list(gate) if has_gate else [])
    in_specs = [_row_spec(tm, Dm), _vec_spec(Dm), _vec_spec(Dm), _row_spec(tm, Dm, off)]
    in_specs += [_row_spec(tm, Dm)] if has_res else []
    in_specs += [_row_spec(tm, Dm), _vec_spec(Dm)] if has_gate else []
    vec = jax.ShapeDtypeStruct((1, Dm), F32)
    out_shape = [jax.ShapeDtypeStruct((R, Dm), F32), vec, vec]
    out_specs = [_row_spec(tm, Dm), _vec_spec(Dm), _vec_spec(Dm)]
    if has_gate:
        out_shape += [jax.ShapeDtypeStruct((R, Dm), BF16), vec, vec]
        out_specs += [_row_spec(tm, Dm), _vec_spec(Dm), _vec_spec(Dm)]
    return pl.pallas_call(
        body, name=name, grid=(R // tm,),
        out_shape=out_shape, in_specs=in_specs, out_specs=out_specs,
        compiler_params=pltpu.CompilerParams(dimension_semantics=("arbitrary",)),
    )(*operands)


def _ffn_in_swiglu(xf, w3, name, tm=1024):
    T, K = xf.shape
    S, n, _ = w3.shape
    half = S // 2
    tm = _tile(T, tm)

    def body(a_ref, wg_ref, wu_ref, gu_ref, act_ref):
        a = a_ref[...]
        g = _dot(a, wg_ref[0], NT)
        u = _dot(a, wu_ref[0], NT)
        gu_ref[0, 0] = g.astype(BF16)
        gu_ref[1, 0] = u.astype(BF16)
        act_ref[0] = (_silu(g) * u).astype(BF16)

    return pl.pallas_call(
        body, name=name, grid=(T // tm, half),
        out_shape=[jax.ShapeDtypeStruct((2, half, T, n), BF16), jax.ShapeDtypeStruct((half, T, n), BF16)],
        in_specs=[pl.BlockSpec((tm, K), lambda i, j: (i, 0)),
                  pl.BlockSpec((1, n, K), lambda i, j: (j, 0, 0)),
                  pl.BlockSpec((1, n, K), lambda i, j: (j + half, 0, 0))],
        out_specs=[pl.BlockSpec((2, 1, tm, n), lambda i, j: (0, j, i, 0)),
                   pl.BlockSpec((1, tm, n), lambda i, j: (j, i, 0))],
        compiler_params=pltpu.CompilerParams(dimension_semantics=("parallel", "parallel")),
    )(xf, w3, w3)


def _ffn_out_dx_swiglu(df, wo, gu, name, tm=1024):
    T, Dm = df.shape
    half, n, _ = wo.shape
    tm = _tile(T, tm)

    def body(df_ref, w_ref, gu_ref, o_ref):
        da = _dot(df_ref[...], w_ref[0], NT)
        g = gu_ref[0, 0].astype(F32)
        u = gu_ref[1, 0].astype(F32)
        s = _sigmoid(g)
        o_ref[0, 0] = (da * u * (s * (1.0 + g * (1.0 - s)))).astype(BF16)
        o_ref[1, 0] = (da * (g * s)).astype(BF16)

    gu_spec = pl.BlockSpec((2, 1, tm, n), lambda i, j: (0, j, i, 0))
    return pl.pallas_call(
        body, name=name, grid=(T // tm, half),
        out_shape=jax.ShapeDtypeStruct(gu.shape, BF16),
        in_specs=[pl.BlockSpec((tm, Dm), lambda i, j: (i, 0)),
                  pl.BlockSpec((1, n, Dm), lambda i, j: (j, 0, 0)), gu_spec],
        out_specs=gu_spec,
        compiler_params=pltpu.CompilerParams(dimension_semantics=("parallel", "parallel")),
    )(df, wo, gu)


def _glu_fwd(ag, name):
    R = ag.shape[0]
    tm = _tile(R, ROW_BLOCK, 8)

    def body(ag_ref, o_ref):
        o_ref[...] = ag_ref[:, :D_MODEL].astype(F32) * _sigmoid(ag_ref[:, D_MODEL:].astype(F32))

    return pl.pallas_call(
        body, name=name, grid=(R // tm,),
        out_shape=jax.ShapeDtypeStruct((R, D_MODEL), F32),
        in_specs=[_row_spec(tm, 2 * D_MODEL)],
        out_specs=_row_spec(tm, D_MODEL),
        compiler_params=pltpu.CompilerParams(dimension_semantics=("parallel",)),
    )(ag)


def _glu_bwd(ag, dhg, name):
    R = ag.shape[0]
    tm = _tile(R, ROW_BLOCK, 8)

    def body(ag_ref, dh_ref, o_ref, s_ref):
        i = pl.program_id(0)

        @pl.when(i == 0)
        def _():
            s_ref[...] = jnp.zeros_like(s_ref)

        a = ag_ref[:, :D_MODEL].astype(F32)
        s = _sigmoid(ag_ref[:, D_MODEL:].astype(F32))
        dh = dh_ref[...]
        da = dh * s
        dg = dh * a * s * (1.0 - s)
        o_ref[:, :D_MODEL] = da.astype(BF16)
        o_ref[:, D_MODEL:] = dg.astype(BF16)
        s_ref[:, :D_MODEL] += jnp.sum(da, axis=0, keepdims=True)
        s_ref[:, D_MODEL:] += jnp.sum(dg, axis=0, keepdims=True)

    return pl.pallas_call(
        body, name=name, grid=(R // tm,),
        out_shape=[jax.ShapeDtypeStruct((R, 2 * D_MODEL), BF16), jax.ShapeDtypeStruct((1, 2 * D_MODEL), F32)],
        in_specs=[_row_spec(tm, 2 * D_MODEL), _row_spec(tm, D_MODEL)],
        out_specs=[_row_spec(tm, 2 * D_MODEL), _vec_spec(2 * D_MODEL)],
        compiler_params=pltpu.CompilerParams(dimension_semantics=("arbitrary",)),
    )(ag, dhg)


def _halo_specs(tm, nblk, width):
    per = tm // CONV_HALO
    prev = pl.BlockSpec((CONV_HALO, width), lambda i: (jnp.maximum(i * per - 1, 0), 0))
    nxt = pl.BlockSpec((CONV_HALO, width), lambda i: (jnp.minimum((i + 1) * per, nblk * per - 1), 0))
    return prev, nxt


def _fill_halo(scr, prev_ref, cur_ref, next_ref, i, nblk, tm):
    scr[0:CONV_HALO, :] = jnp.where(i > 0, prev_ref[...], 0.0)
    scr[CONV_HALO:CONV_HALO + tm, :] = cur_ref[...]
    scr[CONV_HALO + tm:2 * CONV_HALO + tm, :] = jnp.where(i < nblk - 1, next_ref[...], 0.0)


CONV_ROWS = 128


CONV_REACH = (CONV_WIDTH // SUBLANES) * SUBLANES


def _windows(scr, stage, cols, tm):
    for r in range(SUBLANES):
        if r:
            stage[r] = scr[pl.ds(r, tm + CONV_REACH), cols]
        for a in range(CONV_REACH // SUBLANES + 1):
            off = SUBLANES * a + r
            if 1 <= off <= CONV_WIDTH:
                yield off, (stage[r, SUBLANES * a:SUBLANES * a + tm, :] if r
                            else scr[SUBLANES * a:SUBLANES * a + tm, cols])


def _conv_fwd(hg, w_dw, b_dw, name):
    R, Dm = hg.shape
    tm = _tile(R, CONV_ROWS, CONV_HALO)
    nblk = R // tm
    prev_spec, next_spec = _halo_specs(tm, nblk, Dm)

    def body(prev_ref, cur_ref, next_ref, w_ref, bdw_ref, hd_ref, scr, stage):
        _fill_halo(scr, prev_ref, cur_ref, next_ref, pl.program_id(0), nblk, tm)
        for cb in range(Dm // LANES):
            cols = slice(cb * LANES, (cb + 1) * LANES)
            acc = jnp.zeros((tm, LANES), F32) + bdw_ref[:, cols]
            for off, win in _windows(scr, stage, cols, tm):
                acc = acc + w_ref[off - 1:off, cols] * win
            hd_ref[:, cols] = acc

    return pl.pallas_call(
        body, name=name, grid=(nblk,),
        out_shape=jax.ShapeDtypeStruct((R, Dm), F32),
        in_specs=[prev_spec, _row_spec(tm, Dm), next_spec,
                  pl.BlockSpec((CONV_WIDTH, Dm), lambda i: (0, 0)), _vec_spec(Dm)],
        out_specs=_row_spec(tm, Dm),
        scratch_shapes=[pltpu.VMEM((tm + 2 * CONV_HALO, Dm), F32),
                        pltpu.VMEM((SUBLANES, tm + CONV_REACH, LANES), F32)],
        compiler_params=pltpu.CompilerParams(dimension_semantics=("parallel",)),
    )(hg, hg, hg, w_dw, b_dw)


def _ln_silu_fwd(hd, ln_g, ln_b, name):
    R, Dm = hd.shape
    tm = _tile(R, ROW_BLOCK, 8)

    def body(hd_ref, g_ref, b_ref, hs_ref):
        hd = hd_ref[...]
        xc = hd - jnp.mean(hd, axis=-1, keepdims=True)
        rs = lax.rsqrt(jnp.mean(xc * xc, axis=-1, keepdims=True) + EPS)
        hs_ref[...] = _silu(xc * rs * g_ref[...] + b_ref[...]).astype(BF16)

    return pl.pallas_call(
        body, name=name, grid=(R // tm,),
        out_shape=jax.ShapeDtypeStruct((R, Dm), BF16),
        in_specs=[_row_spec(tm, Dm), _vec_spec(Dm), _vec_spec(Dm)],
        out_specs=_row_spec(tm, Dm),
        compiler_params=pltpu.CompilerParams(dimension_semantics=("parallel",)),
    )(hd, ln_g, ln_b)


def _ln_silu_bwd(dhs, hd, ln_g, ln_b, name):
    R, Dm = hd.shape
    tm = _tile(R, ROW_BLOCK, 8)

    def body(dhs_ref, hd_ref, g_ref, b_ref, dhd_ref, dg_ref, db_ref, dsum_ref):
        i = pl.program_id(0)

        @pl.when(i == 0)
        def _():
            dg_ref[...] = jnp.zeros_like(dg_ref)
            db_ref[...] = jnp.zeros_like(db_ref)
            dsum_ref[...] = jnp.zeros_like(dsum_ref)

        hd = hd_ref[...]
        mu = jnp.mean(hd, axis=-1, keepdims=True)
        xc = hd - mu
        rs = lax.rsqrt(jnp.mean(xc * xc, axis=-1, keepdims=True) + EPS)
        z = xc * rs
        hl = z * g_ref[...] + b_ref[...]
        dhl = dhs_ref[...].astype(F32) * _dsilu(hl)
        dg_ref[...] += jnp.sum(dhl * z, axis=0, keepdims=True)
        db_ref[...] += jnp.sum(dhl, axis=0, keepdims=True)
        dz = dhl * g_ref[...]
        dhd = rs * (dz - jnp.mean(dz, axis=-1, keepdims=True) - z * jnp.mean(dz * z, axis=-1, keepdims=True))
        dsum_ref[...] += jnp.sum(dhd, axis=0, keepdims=True)
        dhd_ref[...] = dhd

    return pl.pallas_call(
        body, name=name, grid=(R // tm,),
        out_shape=[jax.ShapeDtypeStruct((R, Dm), F32)] + [jax.ShapeDtypeStruct((1, Dm), F32)] * 3,
        in_specs=[_row_spec(tm, Dm), _row_spec(tm, Dm), _vec_spec(Dm), _vec_spec(Dm)],
        out_specs=[_row_spec(tm, Dm), _vec_spec(Dm), _vec_spec(Dm), _vec_spec(Dm)],
        compiler_params=pltpu.CompilerParams(dimension_semantics=("arbitrary",)),
    )(dhs, hd, ln_g, ln_b)


def _conv_bwd(dhd, hg, w_dw, name):
    R, Dm = hg.shape
    tm = _tile(R, CONV_ROWS, CONV_HALO)
    nblk = R // tm
    prev_spec, next_spec = _halo_specs(tm, nblk, Dm)

    def body(dprev, dcur, dnext, gprev, gcur, gnext, w_ref, dhg_ref, dw_ref, dscr, gscr, dwp, stage):
        i = pl.program_id(0)

        @pl.when(i == 0)
        def _():
            dwp[...] = jnp.zeros_like(dwp)

        _fill_halo(dscr, dprev, dcur, dnext, i, nblk, tm)
        _fill_halo(gscr, gprev, gcur, gnext, i, nblk, tm)
        for cb in range(Dm // LANES):
            cols = slice(cb * LANES, (cb + 1) * LANES)
            acc = jnp.zeros((tm, LANES), F32)
            for off, win in _windows(dscr, stage, cols, tm):
                j = CONV_WIDTH - off
                acc = acc + w_ref[j:j + 1, cols] * win
            dhg_ref[:, cols] = acc
            d_here = dcur[:, cols]
            for off, win in _windows(gscr, stage, cols, tm):
                j = off - 1
                prod = d_here * win
                part = prod[0:SUBLANES]
                for k in range(1, tm // SUBLANES):
                    part = part + prod[k * SUBLANES:(k + 1) * SUBLANES]
                dwp[j * SUBLANES:(j + 1) * SUBLANES, cols] += part

        @pl.when(i == nblk - 1)
        def _():
            for j in range(CONV_WIDTH):
                dw_ref[j:j + 1, :] = jnp.sum(dwp[j * SUBLANES:(j + 1) * SUBLANES, :], axis=0, keepdims=True)

    return pl.pallas_call(
        body, name=name, grid=(nblk,),
        out_shape=[jax.ShapeDtypeStruct((R, Dm), F32), jax.ShapeDtypeStruct((CONV_WIDTH, Dm), F32)],
        in_specs=[prev_spec, _row_spec(tm, Dm), next_spec, prev_spec, _row_spec(tm, Dm), next_spec,
                  pl.BlockSpec((CONV_WIDTH, Dm), lambda i: (0, 0))],
        out_specs=[_row_spec(tm, Dm), pl.BlockSpec((CONV_WIDTH, Dm), lambda i: (0, 0))],
        scratch_shapes=[pltpu.VMEM((tm + 2 * CONV_HALO, Dm), F32)] * 2
        + [pltpu.VMEM((CONV_WIDTH * SUBLANES, Dm), F32), pltpu.VMEM((SUBLANES, tm + CONV_REACH, LANES), F32)],
        compiler_params=pltpu.CompilerParams(dimension_semantics=("arbitrary",)),
    )(dhd, dhd, dhd, hg, hg, hg, w_dw)


def _swap16(y, lane):
    return jnp.where((lane & 16) == 0, pltpu.roll(y, LANES - 16, 1), pltpu.roll(y, 16, 1))


def _head_mean(v, bd):
    hi, lo = _split_bf16(v)
    return (_dot(hi, bd, NN) + _dot(lo, bd, NN)) * (1.0 / HEAD_DIM)


Q_COLS = (0, ATTN_WIDTH)
K_COLS = (ATTN_WIDTH, ATTN_WIDTH + HEAD_DIM * 2)
V_COLS = (K_COLS[1], K_COLS[1] + HEAD_DIM * 2)
SU_COLS = (V_COLS[1], V_COLS[1] + SG_WIDTH)
SV_COLS = (SU_COLS[1], SU_COLS[1] + SG_WIDTH)


def _mix_prep_fwd(p, ctx_rows, cos, sin, qg, kg, bd, w_sp, b_spt, name):
    TT = p.shape[0]
    off = ctx_rows // CHUNK
    q_scale = HEAD_DIM ** -0.5

    def body(p_ref, cos_ref, sin_ref, qg_ref, kg_ref, bd_ref, w_ref, b_ref,
             q_ref, kp_ref, vp_ref, kt_ref, sg_ref):
        lane = lax.broadcasted_iota(jnp.int32, (CHUNK, LANES), 1)
        low = lane < HEAD_DIM
        cs, sn, bdv = cos_ref[...], sin_ref[...], bd_ref[...]

        def norm_rope(xv, gain):
            r = lax.rsqrt(_head_mean(xv * xv, bdv) + EPS)
            yv = xv * r * gain
            return yv * cs + _swap16(yv, lane) * sn

        def pad_heads(ref, t):
            tr = pltpu.roll(t, HEAD_DIM, 1)
            ref[0, 0] = jnp.where(low, t, 0.0).astype(BF16)
            ref[0, 1] = jnp.where(low, 0.0, tr).astype(BF16)
            ref[1, 0] = jnp.where(low, tr, 0.0).astype(BF16)
            ref[1, 1] = jnp.where(low, 0.0, t).astype(BF16)

        for a in range(ATTN_WIDTH // LANES):
            xv = p_ref[:, a * LANES:(a + 1) * LANES]
            q_ref[:, a * LANES:(a + 1) * LANES] = (norm_rope(xv, qg_ref[...]) * q_scale).astype(BF16)
        kh = norm_rope(p_ref[:, K_COLS[0]:K_COLS[1]], kg_ref[...])
        pad_heads(kp_ref, kh)
        pad_heads(vp_ref, p_ref[:, V_COLS[0]:V_COLS[1]])
        kht = kh.T
        kt_ref[0] = kht[:HEAD_DIM].astype(BF16)
        kt_ref[1] = kht[HEAD_DIM:].astype(BF16)
        for g in range(N_SG_GROUPS):
            u = _gelu(p_ref[:, SU_COLS[0] + g * LANES:SU_COLS[0] + (g + 1) * LANES])
            vg = _gelu(p_ref[:, SV_COLS[0] + g * LANES:SV_COLS[0] + (g + 1) * LANES])
            xc = vg - jnp.mean(vg, axis=-1, keepdims=True)
            vn = xc * lax.rsqrt(jnp.mean(xc * xc, axis=-1, keepdims=True) + EPS)
            mixed = _dot(w_ref[g].astype(BF16), vn.astype(BF16), NN) + b_ref[:, g:g + 1]
            sg_ref[:, g * LANES:(g + 1) * LANES] = (u * mixed).astype(BF16)

    def row(width):
        return pl.BlockSpec((CHUNK, width), lambda i: (i, 0))

    def whole(shape):
        return pl.BlockSpec(shape, lambda i: (0,) * len(shape))

    pad_spec = pl.BlockSpec((2, 2, CHUNK, LANES), lambda i: (0, 0, i, 0))
    return pl.pallas_call(
        body, name=name, grid=(TT // CHUNK,),
        out_shape=[jax.ShapeDtypeStruct((TT, ATTN_WIDTH), BF16),
                   jax.ShapeDtypeStruct((2, 2, TT, LANES), BF16), jax.ShapeDtypeStruct((2, 2, TT, LANES), BF16),
                   jax.ShapeDtypeStruct((2, HEAD_DIM, TT), BF16),
                   jax.ShapeDtypeStruct((TT - ctx_rows, ATTN_WIDTH + SG_WIDTH), BF16)],
        in_specs=[row(IN_WIDTH), row(LANES), row(LANES), whole((1, LANES)), whole((1, LANES)),
                  whole((LANES, LANES)), whole((N_SG_GROUPS, CHUNK, CHUNK)), whole((CHUNK, N_SG_GROUPS))],
        out_specs=[row(ATTN_WIDTH), pad_spec, pad_spec,
                   pl.BlockSpec((2, HEAD_DIM, CHUNK), lambda i: (0, 0, i)),
                   pl.BlockSpec((CHUNK, SG_WIDTH), lambda i: (jnp.maximum(i - off, 0), 1))],
        compiler_params=pltpu.CompilerParams(dimension_semantics=("arbitrary",)),
    )(p, cos, sin, qg, kg, bd, w_sp, b_spt)


def _mix_prep_bwd(p, dq, f, dao, ctx_rows, cos, sin, qg, kg, bd, w_sp, w_spt, b_spt, name):
    TT = p.shape[0]
    off = ctx_rows // CHUNK
    q_scale = HEAD_DIM ** -0.5

    def body(p_ref, dq_ref, f_ref, dsg_ref, cos_ref, sin_ref, qg_ref, kg_ref, bd_ref, w_ref, wt_ref,
             b_ref, dp_ref, dqg_ref, dkg_ref, dw_ref, db_ref):
        i = pl.program_id(0)

        @pl.when(i == 0)
        def _():
            dqg_ref[...] = jnp.zeros_like(dqg_ref)
            dkg_ref[...] = jnp.zeros_like(dkg_ref)
            dw_ref[...] = jnp.zeros_like(dw_ref)
            db_ref[...] = jnp.zeros_like(db_ref)

        latent = (i >= off).astype(F32)
        lane = lax.broadcasted_iota(jnp.int32, (CHUNK, LANES), 1)
        low = lane < HEAD_DIM
        cs, sn, bdv = cos_ref[...], sin_ref[...], bd_ref[...]

        def fold(b0):
            return jnp.where(low, f_ref[0, b0] + pltpu.roll(f_ref[0, b0 + 1], HEAD_DIM, 1),
                             pltpu.roll(f_ref[1, b0], HEAD_DIM, 1) + f_ref[1, b0 + 1])

        def norm_rope_bwd(xv, dout, gain):
            r = lax.rsqrt(_head_mean(xv * xv, bdv) + EPS)
            n = xv * r
            dy = dout * cs + _swap16(dout * sn, lane)
            dn = dy * gain
            dx = r * (dn - n * _head_mean(dn * n, bdv))
            return dx, jnp.sum(dy * n, axis=0, keepdims=True)

        for a in range(ATTN_WIDTH // LANES):
            cols = slice(a * LANES, (a + 1) * LANES)
            dx, dg = norm_rope_bwd(p_ref[:, cols], dq_ref[:, cols] * (latent * q_scale), qg_ref[...])
            dp_ref[:, cols] = dx.astype(BF16)
            dqg_ref[...] += dg
        dx, dg = norm_rope_bwd(p_ref[:, K_COLS[0]:K_COLS[1]], fold(0), kg_ref[...])
        dp_ref[:, K_COLS[0]:K_COLS[1]] = dx.astype(BF16)
        dkg_ref[...] += dg
        dp_ref[:, V_COLS[0]:V_COLS[1]] = fold(2).astype(BF16)
        for g in range(N_SG_GROUPS):
            su = p_ref[:, SU_COLS[0] + g * LANES:SU_COLS[0] + (g + 1) * LANES]
            sv = p_ref[:, SV_COLS[0] + g * LANES:SV_COLS[0] + (g + 1) * LANES]
            (u, dgelu_su), (vg, dgelu_sv) = _gelu_and_grad(su), _gelu_and_grad(sv)
            xc = vg - jnp.mean(vg, axis=-1, keepdims=True)
            rs = lax.rsqrt(jnp.mean(xc * xc, axis=-1, keepdims=True) + EPS)
            vn = xc * rs
            vnb = vn.astype(BF16)
            mixed = _dot(w_ref[g].astype(BF16), vnb, NN) + b_ref[:, g:g + 1]
            dsg = dsg_ref[:, g * LANES:(g + 1) * LANES].astype(F32) * latent
            du = dsg * mixed
            dmix = dsg * u
            dmb = dmix.astype(BF16)
            db_ref[:, g:g + 1] += jnp.sum(dmix, axis=-1, keepdims=True)
            dw_ref[g] += _dot(dmb, vnb, NT)
            dvn = _dot(wt_ref[g].astype(BF16), dmb, NN)
            dvg = rs * (dvn - jnp.mean(dvn, axis=-1, keepdims=True)
                        - vn * jnp.mean(dvn * vn, axis=-1, keepdims=True))
            dp_ref[:, SU_COLS[0] + g * LANES:SU_COLS[0] + (g + 1) * LANES] = (du * dgelu_su).astype(BF16)
            dp_ref[:, SV_COLS[0] + g * LANES:SV_COLS[0] + (g + 1) * LANES] = (dvg * dgelu_sv).astype(BF16)

    def row(width):
        return pl.BlockSpec((CHUNK, width), lambda i: (i, 0))

    def latent_row(width, col_block):
        return pl.BlockSpec((CHUNK, width), lambda i: (jnp.maximum(i - off, 0), col_block))

    def whole(shape):
        return pl.BlockSpec(shape, lambda i: (0,) * len(shape))

    return pl.pallas_call(
        body, name=name, grid=(TT // CHUNK,),
        out_shape=[jax.ShapeDtypeStruct((TT, IN_WIDTH), BF16), jax.ShapeDtypeStruct((1, LANES), F32),
                   jax.ShapeDtypeStruct((1, LANES), F32),
                   jax.ShapeDtypeStruct((N_SG_GROUPS, CHUNK, CHUNK), F32),
                   jax.ShapeDtypeStruct((CHUNK, N_SG_GROUPS), F32)],
        in_specs=[row(IN_WIDTH), latent_row(ATTN_WIDTH, 0),
                  pl.BlockSpec((2, 4, CHUNK, LANES), lambda i: (0, 0, i, 0)),
                  latent_row(SG_WIDTH, 1), row(LANES), row(LANES), whole((1, LANES)), whole((1, LANES)),
                  whole((LANES, LANES)), whole((N_SG_GROUPS, CHUNK, CHUNK)),
                  whole((N_SG_GROUPS, CHUNK, CHUNK)), whole((CHUNK, N_SG_GROUPS))],
        out_specs=[row(IN_WIDTH), whole((1, LANES)), whole((1, LANES)),
                   whole((N_SG_GROUPS, CHUNK, CHUNK)), whole((CHUNK, N_SG_GROUPS))],
        compiler_params=pltpu.CompilerParams(dimension_semantics=("arbitrary",)),
    )(p, dq, f, dao, cos, sin, qg, kg, bd, w_sp, w_spt, b_spt)


def _attn_fwd(q, kpad, vpad, ao, ctx_rows, name, tq=256):
    TT = q.shape[0]
    T = TT - ctx_rows
    tq = _tile(T, tq)
    off = ctx_rows // tq
    group = 2 * LANES

    def body(q_ref, k_ref, v_ref, ao_in, o_ref, lse_ref):
        del ao_in
        lane = lax.broadcasted_iota(jnp.int32, (tq, LANES), 1)
        lse = jnp.zeros((tq, LANES), F32)
        for a in range(2):
            acc = jnp.zeros((tq, LANES), F32)
            qa = q_ref[:, a * LANES:(a + 1) * LANES]
            for b in range(2):
                s = _dot(qa, k_ref[0, b], NT)
                m = jnp.max(s, axis=-1, keepdims=True)
                e = jnp.exp(s - m)
                l = jnp.sum(e, axis=-1, keepdims=True)
                acc = acc + _dot(e.astype(BF16), v_ref[0, b], NN) * (1.0 / l)
                lse = jnp.where(lane == 2 * a + b, m + jnp.log(l), lse)
            o_ref[:, a * LANES:(a + 1) * LANES] = acc.astype(BF16)
        lse_ref[0] = lse

    kv_spec = pl.BlockSpec((1, 2, TT, LANES), lambda j, i: (j, 0, 0, 0))
    return pl.pallas_call(
        body, name=name, grid=(2, T // tq),
        out_shape=[jax.ShapeDtypeStruct(ao.shape, BF16), jax.ShapeDtypeStruct((2, T, LANES), F32)],
        in_specs=[pl.BlockSpec((tq, group), lambda j, i: (i + off, j)), kv_spec, kv_spec,
                  pl.BlockSpec(memory_space=pl.ANY)],
        out_specs=[pl.BlockSpec((tq, group), lambda j, i: (i, j)),
                   pl.BlockSpec((1, tq, LANES), lambda j, i: (j, i, 0))],
        input_output_aliases={3: 0},
        compiler_params=pltpu.CompilerParams(dimension_semantics=("parallel", "parallel")),
    )(q, kpad, vpad, ao)


def _attn_bwd(q, dao, ao, lse, kpad, vpad, kt, ctx_rows, name, tq=256):
    TT = q.shape[0]
    T = TT - ctx_rows
    tq = _tile(T, tq)
    off = ctx_rows // tq
    group = 2 * LANES

    def body(q_ref, do_ref, o_ref, lse_ref, k_ref, v_ref, kt_ref, dq_ref, f_ref):
        i = pl.program_id(1)

        @pl.when(i == 0)
        def _():
            f_ref[...] = jnp.zeros_like(f_ref)

        ktv = kt_ref[0]
        lse_t = lse_ref[0].T
        row = lax.broadcasted_iota(jnp.int32, (SUBLANES, LANES), 0)
        lane = lax.broadcasted_iota(jnp.int32, (SUBLANES, LANES), 1)
        half_ones = (jnp.where(lane < HEAD_DIM, 0, 1) == row).astype(BF16)
        for a in range(2):
            cols = slice(a * LANES, (a + 1) * LANES)
            qa = q_ref[:, cols]
            do32 = do_ref[:, cols].astype(F32)
            doa = do32.astype(BF16)
            hi, lo = _split_bf16(do32 * o_ref[:, cols].astype(F32))
            deltas = _dot(half_ones, hi, NT) + _dot(half_ones, lo, NT)
            halves = []
            for b in range(2):
                h = 2 * a + b
                st = _dot(k_ref[0, b], qa, NT)
                pt = jnp.exp(st - lse_t[h:h + 1, :])
                dpt = _dot(v_ref[0, b], doa, NT)
                dst = (pt * (dpt - deltas[b:b + 1, :])).astype(BF16)
                f_ref[0, b] += _dot(dst, qa, NN)
                f_ref[0, 2 + b] += _dot(pt.astype(BF16), doa, NN)
                halves.append(_dot(ktv, dst, NN))
            dq_ref[:, cols] = jnp.concatenate(halves, axis=0).T

    kv_spec = pl.BlockSpec((1, 2, TT, LANES), lambda j, i: (j, 0, 0, 0))
    out_cols = pl.BlockSpec((tq, group), lambda j, i: (i, j))
    return pl.pallas_call(
        body, name=name, grid=(2, T // tq),
        out_shape=[jax.ShapeDtypeStruct((T, ATTN_WIDTH), F32), jax.ShapeDtypeStruct((2, 4, TT, LANES), F32)],
        in_specs=[pl.BlockSpec((tq, group), lambda j, i: (i + off, j)), out_cols, out_cols,
                  pl.BlockSpec((1, tq, LANES), lambda j, i: (j, i, 0)),
                  kv_spec, kv_spec, pl.BlockSpec((1, HEAD_DIM, TT), lambda j, i: (j, 0, 0))],
        out_specs=[out_cols, pl.BlockSpec((1, 4, TT, LANES), lambda j, i: (j, 0, 0, 0))],
        compiler_params=pltpu.CompilerParams(dimension_semantics=("parallel", "arbitrary")),
    )(q, dao, ao, lse, kpad, vpad, kt)


def _final_fwd_bwd(h, g, target, y, gt, name):
    R, Dm = h.shape
    tm = _tile(R, ROW_BLOCK, 8)

    def body(h_ref, g_ref, t_ref, y_ref, gt_ref, dh_ref, loss_ref, dg_ref, dy_ref, dgt_ref, dsum_ref):
        i = pl.program_id(0)

        @pl.when(i == 0)
        def _():
            for ref in (loss_ref, dg_ref, dgt_ref, dsum_ref):
                ref[...] = jnp.zeros_like(ref)

        hv = h_ref[...]
        r = lax.rsqrt(jnp.mean(hv * hv, axis=-1, keepdims=True) + EPS)
        n = hv * r
        diff = n * g_ref[...] - t_ref[...]
        loss_ref[...] += jnp.sum(diff * diff)
        dout = diff * (1.0 / Dm)
        dg_ref[...] += jnp.sum(dout * n, axis=0, keepdims=True)
        dn = dout * g_ref[...]
        dh = r * (dn - n * jnp.mean(dn * n, axis=-1, keepdims=True))
        dh_ref[...] = dh
        _gate_grads(dh, y_ref, gt_ref, dy_ref, dgt_ref, dsum_ref)

    vec = jax.ShapeDtypeStruct((1, Dm), F32)
    return pl.pallas_call(
        body, name=name, grid=(R // tm,),
        out_shape=[jax.ShapeDtypeStruct((R, Dm), F32), jax.ShapeDtypeStruct((1, LANES), F32), vec,
                   jax.ShapeDtypeStruct((R, Dm), BF16), vec, vec],
        in_specs=[_row_spec(tm, Dm), _vec_spec(Dm), _row_spec(tm, Dm), _row_spec(tm, Dm), _vec_spec(Dm)],
        out_specs=[_row_spec(tm, Dm), _vec_spec(LANES), _vec_spec(Dm), _row_spec(tm, Dm), _vec_spec(Dm),
                   _vec_spec(Dm)],
        compiler_params=pltpu.CompilerParams(dimension_semantics=("arbitrary",)),
    )(h, g, target, y, gt)


MOD_ROWS = 16


def _mod_fwd(c_rows, w_mod, name):
    L, Dm, n = w_mod.shape

    def body(c_ref, w_ref, o_ref):
        o_ref[0] = _dot3(_silu(c_ref[...]), w_ref[0], NN)

    return pl.pallas_call(
        body, name=name, grid=(L,),
        out_shape=jax.ShapeDtypeStruct((L, MOD_ROWS, n), F32),
        in_specs=[pl.BlockSpec((MOD_ROWS, Dm), lambda l: (0, 0)), pl.BlockSpec((1, Dm, n), lambda l: (l, 0, 0))],
        out_specs=pl.BlockSpec((1, MOD_ROWS, n), lambda l: (l, 0, 0)),
        compiler_params=pltpu.CompilerParams(dimension_semantics=("parallel",)),
    )(c_rows, w_mod)


def _mod_bwd(c_rows_t, dmod, w_mod, name):
    L, Dm, n = w_mod.shape

    def body(ct_ref, d_ref, w_ref, gw_ref, ds_ref):
        dm = d_ref[0]
        gw_ref[0] = _dot3(_silu(ct_ref[...]), dm, NN)
        ds_ref[0] = _dot3(dm[:MOD_ROWS], w_ref[0], NT)

    return pl.pallas_call(
        body, name=name, grid=(L,),
        out_shape=[jax.ShapeDtypeStruct((L, Dm, n), F32), jax.ShapeDtypeStruct((L, MOD_ROWS, Dm), F32)],
        in_specs=[pl.BlockSpec((Dm, LANES), lambda l: (0, 0)), pl.BlockSpec((1, LANES, n), lambda l: (l, 0, 0)),
                  pl.BlockSpec((1, Dm, n), lambda l: (l, 0, 0))],
        out_specs=[pl.BlockSpec((1, Dm, n), lambda l: (l, 0, 0)),
                   pl.BlockSpec((1, MOD_ROWS, Dm), lambda l: (l, 0, 0))],
        compiler_params=pltpu.CompilerParams(dimension_semantics=("parallel",)),
    )(c_rows_t, dmod, w_mod)


def _adam_update(w, g, m, v):
    c1 = 1.0 - ADAM_B1 ** ADAM_STEP
    c2 = 1.0 - ADAM_B2 ** ADAM_STEP
    mn = ADAM_B1 * m + (1.0 - ADAM_B1) * g
    vn = ADAM_B2 * v + (1.0 - ADAM_B2) * (g * g)
    return -ADAM_LR * ((mn / c1) / (jnp.sqrt(vn / c2) + ADAM_EPS) + ADAM_WD * w), mn, vn


def _adamw(w, g, m, v, name):
    R, Cw = w.shape
    tm = _tile(R, ADAM_ROWS, 8)

    def body(w_ref, g_ref, m_ref, v_ref, d_ref, mo_ref, vo_ref):
        d_ref[...], mo_ref[...], vo_ref[...] = _adam_update(w_ref[...], g_ref[...], m_ref[...], v_ref[...])

    spec = pl.BlockSpec((tm, Cw), lambda i: (i, 0))
    return pl.pallas_call(
        body, name=name, grid=(R // tm,),
        out_shape=[jax.ShapeDtypeStruct((R, Cw), F32)] * 3,
        in_specs=[spec] * 4, out_specs=[spec] * 3,
        compiler_params=pltpu.CompilerParams(dimension_semantics=("parallel",)),
    )(w, g, m, v)


def _adamw_recv(w, m, v, recvs, name):
    L, R, n = w.shape
    tm = _tile(R, ADAM_ROWS, 8)
    nblk = R // tm
    parts = [r.reshape(N_DEV, R, n) for r in recvs]

    def body(*refs):
        w_ref, m_ref, v_ref = refs[:3]
        part_refs = refs[3:3 + L]
        g_ref, d_ref, mo_ref, vo_ref, gsum = refs[3 + L:]
        l = pl.program_id(0)
        for ll in range(L):
            @pl.when(l == ll)
            def _(ll=ll):
                acc = part_refs[ll][0].astype(F32)
                for s in range(1, N_DEV):
                    acc = acc + part_refs[ll][s].astype(F32)
                gsum[...] = acc
        g = gsum[...]
        g_ref[0] = g
        d_ref[0], mo_ref[0], vo_ref[0] = _adam_update(w_ref[0], g, m_ref[0], v_ref[0])

    def part_spec(ll):
        return pl.BlockSpec((N_DEV, tm, n), lambda l, i: (0, jnp.where(l == ll, i, jnp.where(l < ll, 0, nblk - 1)), 0))

    spec = pl.BlockSpec((1, tm, n), lambda l, i: (l, i, 0))
    return pl.pallas_call(
        body, name=name, grid=(L, nblk),
        out_shape=[jax.ShapeDtypeStruct((L, R, n), F32)] * 4,
        in_specs=[spec] * 3 + [part_spec(ll) for ll in range(L)], out_specs=[spec] * 4,
        scratch_shapes=[pltpu.VMEM((tm, n), F32)],
        compiler_params=pltpu.CompilerParams(dimension_semantics=("parallel", "parallel")),
    )(w, m, v, *parts)


def _pack(parts, row_mult=8):
    flat, offs, pos = [], [], 0
    for t in parts:
        t = t.reshape(-1).astype(F32)
        size = -(-t.shape[0] // LANES) * LANES
        flat.append(jnp.pad(t, (0, size - t.shape[0])))
        offs.append(pos)
        pos += size
    total = -(-pos // (LANES * row_mult)) * (LANES * row_mult)
    if total > pos:
        flat.append(jnp.zeros((total - pos,), F32))
    return jnp.concatenate(flat).reshape(-1, LANES), offs


def _take(buf, off, shape):
    size = math.prod(shape)
    return buf[..., off:off + size].reshape(buf.shape[:-1] + tuple(shape))


def _rope_tables(T, ctx_rows):
    pos = jnp.arange(T)
    row = (pos // GRID_W).astype(F32)
    col = (pos % GRID_W).astype(F32)
    half = HEAD_DIM // 4
    inv = ROPE_THETA ** (-jnp.arange(0, 2 * half, 2, dtype=F32) / (2 * half))
    ang_r, ang_c = row[:, None] * inv[None, :], col[:, None] * inv[None, :]
    cos = jnp.concatenate([jnp.cos(ang_r)] * 2 + [jnp.cos(ang_c)] * 2, axis=1)
    sin = jnp.concatenate([-jnp.sin(ang_r), jnp.sin(ang_r), -jnp.sin(ang_c), jnp.sin(ang_c)], axis=1)
    cos = jnp.concatenate([jnp.ones((ctx_rows, HEAD_DIM), F32), cos], axis=0)
    sin = jnp.concatenate([jnp.zeros((ctx_rows, HEAD_DIM), F32), sin], axis=0)
    return jnp.tile(cos, (1, 2)), jnp.tile(sin, (1, 2))


def kernel(x, c, ctx, c_ctx, w_mod, b_mod, g_mix, g_ffn, w_ffn_in, w_ffn_out, w_in, q_gain, k_gain, w_sp, b_sp, w_out, w_pw1, b_pw1, w_dw, b_dw, ln_g, ln_b, w_pw2, b_pw2, g_final, loss_target, m_c_ctx, m_w_mod, m_b_mod, m_g_mix, m_g_ffn, m_w_ffn_in, m_w_ffn_out, m_w_in, m_q_gain, m_k_gain, m_w_sp, m_b_sp, m_w_out, m_w_pw1, m_b_pw1, m_w_dw, m_b_dw, m_ln_g, m_ln_b, m_w_pw2, m_b_pw2, m_g_final, v_c_ctx, v_w_mod, v_b_mod, v_g_mix, v_g_ffn, v_w_ffn_in, v_w_ffn_out, v_w_in, v_q_gain, v_k_gain, v_w_sp, v_b_sp, v_w_out, v_w_pw1, v_b_pw1, v_w_dw, v_b_dw, v_ln_g, v_ln_b, v_w_pw2, v_b_pw2, v_g_final):
    weights = dict(c_ctx=c_ctx, w_mod=w_mod, b_mod=b_mod, g_mix=g_mix, g_ffn=g_ffn, w_ffn_in=w_ffn_in,
                   w_ffn_out=w_ffn_out, w_in=w_in, q_gain=q_gain, k_gain=k_gain, w_sp=w_sp, b_sp=b_sp,
                   w_out=w_out, w_pw1=w_pw1, b_pw1=b_pw1, w_dw=w_dw, b_dw=b_dw, ln_g=ln_g, ln_b=ln_b,
                   w_pw2=w_pw2, b_pw2=b_pw2, g_final=g_final)
    moments_m = dict(c_ctx=m_c_ctx, w_mod=m_w_mod, b_mod=m_b_mod, g_mix=m_g_mix, g_ffn=m_g_ffn,
                     w_ffn_in=m_w_ffn_in, w_ffn_out=m_w_ffn_out, w_in=m_w_in, q_gain=m_q_gain,
                     k_gain=m_k_gain, w_sp=m_w_sp, b_sp=m_b_sp, w_out=m_w_out, w_pw1=m_w_pw1,
                     b_pw1=m_b_pw1, w_dw=m_w_dw, b_dw=m_b_dw, ln_g=m_ln_g, ln_b=m_ln_b, w_pw2=m_w_pw2,
                     b_pw2=m_b_pw2, g_final=m_g_final)
    moments_v = dict(c_ctx=v_c_ctx, w_mod=v_w_mod, b_mod=v_b_mod, g_mix=v_g_mix, g_ffn=v_g_ffn,
                     w_ffn_in=v_w_ffn_in, w_ffn_out=v_w_ffn_out, w_in=v_w_in, q_gain=v_q_gain,
                     k_gain=v_k_gain, w_sp=v_w_sp, b_sp=v_b_sp, w_out=v_w_out, w_pw1=v_w_pw1,
                     b_pw1=v_b_pw1, w_dw=v_w_dw, b_dw=v_b_dw, ln_g=v_ln_g, ln_b=v_ln_b, w_pw2=v_w_pw2,
                     b_pw2=v_b_pw2, g_final=v_g_final)
    names = list(weights)

    T, C = x.shape[1], ctx.shape[1]
    Dm = D_MODEL
    me = 4 * lax.axis_index("x") + 2 * lax.axis_index("y") + lax.axis_index("c")
    h0 = x[0]
    ctx2 = ctx[0]
    target = loss_target[0]

    small_sharded = (("w_dw", w_dw[0]), ("b_pw1", b_pw1), ("b_dw", b_dw), ("ln_g", ln_g), ("ln_b", ln_b),
                     ("b_pw2", b_pw2))
    buf1, offs1 = _pack([c] + [t for _, t in small_sharded])
    w_in_t, m_w_in_t, v_w_in_t = (jnp.swapaxes(t, 1, 2) for t in (w_in, m_w_in, v_w_in))
    w_ffi_t, m_w_ffi_t, v_w_ffi_t = (jnp.swapaxes(t, 1, 2) for t in (w_ffn_in, m_w_ffn_in, v_w_ffn_in))
    got1, W_in_t, W_out = _all_gather([buf1, w_in_t[0].astype(BF16), w_out[0].astype(BF16)], "gather_cond", False)
    got1 = got1.reshape(N_DEV, -1)
    c_all = _take(got1, offs1[0], (Dm,))
    full_small = {}
    for (nm, t), off in zip(small_sharded, offs1[1:]):
        seg = _take(got1, off, t.shape)
        full_small[nm] = jnp.moveaxis(seg, 0, -2).reshape(t.shape[:-1] + (N_DEV * t.shape[-1],))
    w_dw_f, b_pw1_f = full_small["w_dw"], full_small["b_pw1"]
    b_dw_f, ln_g_f, ln_b_f, b_pw2_f = (full_small[k] for k in ("b_dw", "ln_g", "ln_b", "b_pw2"))

    c_rows = jnp.concatenate([c_all, c_ctx[None, :], jnp.zeros((MOD_ROWS - N_DEV - 1, Dm), F32)], axis=0)
    mod_part = _mod_fwd(c_rows, w_mod, "mod_fwd")
    n_mod = w_mod.shape[2]
    got2 = _all_gather([mod_part.reshape(-1, LANES)], "gather_mod", True)[0]
    mod_all = got2.reshape(N_DEV, 2, MOD_ROWS, n_mod).transpose(1, 2, 0, 3).reshape(2, MOD_ROWS, N_DEV * n_mod)
    mod_all = mod_all + b_mod[:, None, :]
    my_mod = lax.dynamic_index_in_dim(mod_all, me, axis=1, keepdims=False)
    sh1, sc1, gt1, sh2, sc2, gt2 = ([my_mod[l:l + 1, k * Dm:(k + 1) * Dm] for l in range(2)] for k in range(6))
    csh1 = mod_all[0, N_DEV:N_DEV + 1, 0:Dm]
    csc1 = mod_all[0, N_DEV:N_DEV + 1, Dm:2 * Dm]

    behind = got2[0:1, 0:1] * 0.0
    gather_groups = [[w_ffi_t[0], w_ffn_out[0]], [w_pw1[0], w_pw2[0]], [w_ffi_t[1], w_ffn_out[1]]]
    gathers = [_push_begin([(t + behind).astype(BF16) for t in grp], True, f"gather_start{k}")
               for k, grp in enumerate(gather_groups)]
    started = sum(h[4][0:1, 0:1] for h in gathers)

    def gathered(k, after):
        return _push_end(gathers[k], after, f"gather_wait{k}")[1]

    def ffn_weights(k, after):
        wi, wo = gathered(k, after)
        return wi.reshape(N_DEV, FF_SHARD, Dm), wo.reshape(N_DEV // 2, FF_SHARD, Dm)

    def col_gathered(t, n):
        return t.reshape(N_DEV, Dm, n).transpose(1, 0, 2).reshape(Dm, N_DEV * n)

    W_ffi, W_ffo = [None, None], [None, None]

    g_mix_r = [g_mix[l:l + 1] for l in range(2)]
    g_ffn_r = [g_ffn[l:l + 1] for l in range(2)]
    g_fin = g_final[None, :]

    cos, sin = _rope_tables(T, C)
    qg = jnp.tile(q_gain, (1, 2))
    kg = jnp.tile(k_gain, (1, 2))
    lane_head = jnp.arange(LANES) // HEAD_DIM
    bd = (lane_head[:, None] == lane_head[None, :]).astype(BF16)
    w_sp0 = w_sp[0]
    w_spt0 = w_sp0.transpose(0, 2, 1)
    b_spt0 = b_sp[0].T

    XM = _norm_mod_fwd_cat(ctx2, h0, g_mix_r[0], csc1, csh1, sc1[0] + started, sh1[0], "norm_mix0")
    P = _mm(XM, W_in_t, "nt", "in_proj", tm=1088, tn=IN_WIDTH)
    qh, kpad, vpad, kt, ao = _mix_prep_fwd(P, C, cos, sin, qg, kg, bd, w_sp0, b_spt0, "mix_prep")
    ao, lse = _attn_fwd(qh, kpad, vpad, ao, C, "attn_fwd")
    h1, y0, xf0 = _mm(ao, W_out, "nn", "out_proj", res=h0, gate=gt1[0], raw_out=True,
                      norm=(g_ffn_r[0], sc2[0], sh2[0]))

    def ffn_fwd(h_in, xf, l, norm_next):
        W_ffi[l], W_ffo[l] = ffn_weights(2 * l, xf)
        gu, act = _ffn_in_swiglu(xf, W_ffi[l], f"ffn_in{l}")
        outs = _mm_sum_shards(act, W_ffo[l], "nn", f"ffn_out{l}", res=h_in, gate=gt2[l], raw_out=True,
                              norm=norm_next)
        return tuple(outs) + (None,) * (3 - len(outs)) + (gu, act)

    h2, f0, xm1, gu0, act0 = ffn_fwd(h1, xf0, 0, (g_mix_r[1], sc1[1], sh1[1]))

    W_pw1, W_pw2 = gathered(1, xm1)
    W_pw1 = col_gathered(W_pw1, 2 * Dm // N_DEV)
    ag = _mm(xm1, W_pw1, "nn", "pw1", BF16, bias=b_pw1_f)
    hg = _glu_fwd(ag, "glu")
    hd = _conv_fwd(hg, w_dw_f, b_dw_f, "conv")
    hs = _ln_silu_fwd(hd, ln_g_f, ln_b_f, "ln_silu")
    h3, y1, xf1 = _mm(hs, W_pw2, "nn", "pw2", bias=b_pw2_f, res=h2, gate=gt1[1], raw_out=True,
                      norm=(g_ffn_r[1], sc2[1], sh2[1]))
    h4, f1, _, gu1, act1 = ffn_fwd(h3, xf1, 1, None)

    dh4, sq_err, dg_final, df1, dgt2_1, _ = _final_fwd_bwd(h4, g_fin, target, f1, gt2[1], "loss_head")
    loss_local = (0.5 / Dm) * sq_err[0, 0:1]

    def col_shards(g, n):
        return g.reshape(Dm, N_DEV, n).transpose(1, 0, 2).reshape(N_DEV * Dm, n)

    def exchange_begin(k, parts):
        return _push_begin(parts, False, f"exchange_start{k}")

    def zero_of(handle):
        return handle[4][0:1, 0:1]

    def ffn_bwd(df, xf, gu, act, l):
        dw_out = _mm_tn_shard_rows(act, df, f"ffn_out_dw{l}", BF16)
        dgu = _ffn_out_dx_swiglu(df, W_ffo[l], gu, f"ffn_out_dx{l}").reshape(N_DEV, T, FF_SHARD)
        dw_in = _mm_tn_shard_rows(dgu, xf, f"ffn_in_dw{l}", BF16)
        dxf = _mm_sum_shards(dgu, W_ffi[l], "nn", f"ffn_in_dx{l}", BF16, tm=512)
        return dw_in, dw_out, dxf

    dW_ffi1, dW_ffo1, dxf1 = ffn_bwd(df1, xf1, gu1, act1, 1)
    ex0 = exchange_begin(0, [dW_ffi1.reshape(2 * D_FF, Dm), dW_ffo1.reshape(D_FF, Dm)])
    dh3, da, dsh, dy1, dgt1_1, db_pw2 = _norm_mod_bwd(h3, g_ffn_r[1], sc2[1], dxf1, dh4, "norm_ffn_bwd1",
                                                       gate=(y1, gt1[1] + zero_of(ex0)))
    dmod_ffn1 = (dsh, da * g_ffn_r[1], dgt2_1)
    dg_ffn1 = da * (1.0 + sc2[1])

    dW_pw2 = _mm(hs, dy1, "tn", "pw2_dw", BF16, tk=2048)
    dhs = _mm(dy1, W_pw2, "nt", "pw2_dx", BF16)
    dhd, dln_g, dln_b, db_dw = _ln_silu_bwd(dhs, hd, ln_g_f, ln_b_f, "ln_silu_bwd")
    dhg, dw_dw = _conv_bwd(dhd, hg, w_dw_f, "conv_bwd")
    dag, db_pw1 = _glu_bwd(ag, dhg, "glu_bwd")
    dW_pw1 = _mm(xm1, dag, "tn", "pw1_dw", BF16, tk=2048)
    dxm1 = _mm(dag, W_pw1, "nt", "pw1_dx", BF16, tk=2048)
    ex1 = exchange_begin(1, [col_shards(dW_pw1, 2 * Dm // N_DEV), dW_pw2])
    dh2, da, dsh, df0, dgt2_0, _ = _norm_mod_bwd(h2, g_mix_r[1], sc1[1], dxm1, dh3, "norm_mix1_bwd",
                                                 gate=(f0, gt2[0] + zero_of(ex1)))
    dmod_mix1 = (dsh, da * g_mix_r[1], dgt1_1)
    dg_mix1 = da * (1.0 + sc1[1])

    dW_ffi0, dW_ffo0, dxf0 = ffn_bwd(df0, xf0, gu0, act0, 0)
    ex2 = exchange_begin(2, [dW_ffi0.reshape(2 * D_FF, Dm), dW_ffo0.reshape(D_FF, Dm)])
    dh1, da, dsh, dy0, dgt1_0, _ = _norm_mod_bwd(h1, g_ffn_r[0], sc2[0], dxf0, dh2, "norm_ffn_bwd0",
                                                 gate=(y0, gt1[0] + zero_of(ex2)))
    dmod_ffn0 = (dsh, da * g_ffn_r[0], dgt2_0)
    dg_ffn0 = da * (1.0 + sc2[0])

    dW_out = _mm(ao, dy0, "tn", "out_proj_dw", BF16, tk=2048)
    ex_out = exchange_begin(4, [dW_out])
    dao = _mm(dy0, W_out + zero_of(ex_out).astype(BF16), "nt", "out_proj_dx", BF16)
    dq, f_acc = _attn_bwd(qh, dao, ao, lse, kpad, vpad, kt, C, "attn_bwd")
    dP, dqg, dkg, dw_sp0, db_spt0 = _mix_prep_bwd(P, dq, f_acc, dao, C, cos, sin, qg, kg, bd, w_sp0, w_spt0,
                                                  b_spt0, "mix_prep_bwd")
    dW_in_t = _mm(dP, XM, "tn", "in_proj_dw", BF16, tm=896, tk=2176)
    dXM = _mm(dP, W_in_t, "nn", "in_proj_dx", BF16, tm=1088, tk=IN_WIDTH)
    dh0, da, dsh = _norm_mod_bwd(h0, g_mix_r[0], sc1[0], dXM, dh1, "norm_mix0_bwd", dxm_row_off=C)
    _, dac, dcsh = _norm_mod_bwd(ctx2, g_mix_r[0], csc1, dXM, None, "norm_ctx_bwd")
    dmod_mix0 = (dsh, da * g_mix_r[0], dgt1_0)
    dg_mix0 = da * (1.0 + sc1[0]) + dac * (1.0 + csc1)
    dcmod = jnp.concatenate([dcsh, dac * g_mix_r[0]], axis=1)

    dmod_mine = jnp.stack([jnp.concatenate(dmod_mix0 + dmod_ffn0, axis=1)[0],
                           jnp.concatenate(dmod_mix1 + dmod_ffn1, axis=1)[0]])

    small_grads = [
        ("loss", loss_local), ("g_final", dg_final), ("g_mix", jnp.concatenate([dg_mix0, dg_mix1])),
        ("g_ffn", jnp.concatenate([dg_ffn0, dg_ffn1])),
        ("q_gain", dqg[:, :HEAD_DIM] + dqg[:, HEAD_DIM:]), ("k_gain", dkg[:, :HEAD_DIM] + dkg[:, HEAD_DIM:]),
        ("w_sp", dw_sp0[None]), ("b_sp", db_spt0.T[None]), ("b_pw1", db_pw1), ("w_dw", dw_dw[None]),
        ("b_dw", db_dw), ("ln_g", dln_g), ("ln_b", dln_b), ("b_pw2", db_pw2), ("dcmod", dcmod),
        ("dmod", dmod_mine),
    ]
    buf3, offs3 = _pack([t for _, t in small_grads])
    got3 = _all_gather([buf3], "gather_small_grads", True)[0].reshape(N_DEV, buf3.shape[0], LANES)
    sum3 = _sum_devices(got3, "sum_small_grads").reshape(-1)
    off3 = {nm: off for (nm, _), off in zip(small_grads, offs3)}
    shape3 = {nm: t.shape for nm, t in small_grads}

    def summed(nm):
        return _take(sum3, off3[nm], shape3[nm])

    loss = summed("loss")[0]
    dcmod_sum = summed("dcmod")
    dmod_rows = _take(got3.reshape(N_DEV, -1), off3["dmod"], (2, 6 * Dm)).transpose(1, 0, 2)
    ctx_row = jnp.concatenate([jnp.pad(dcmod_sum, ((0, 0), (0, 4 * Dm))), jnp.zeros((1, 6 * Dm), F32)])
    dmod_all = jnp.concatenate([dmod_rows, ctx_row[:, None, :],
                                jnp.zeros((2, LANES - N_DEV - 1, 6 * Dm), F32)], axis=1)
    grads = {}
    grads["b_mod"] = summed("dmod") + ctx_row
    dmod_shard = lax.dynamic_slice_in_dim(dmod_all, me * n_mod, n_mod, axis=2)
    c_rows_t = jnp.pad(c_rows.T, ((0, 0), (0, LANES - MOD_ROWS)))
    grads["w_mod"], ds_part = _mod_bwd(c_rows_t, dmod_shard, w_mod, "mod_bwd")

    buf4, _ = _pack([ds_part[0, N_DEV]])
    got4 = _all_gather([buf4], "gather_c_ctx_grad", True)[0].reshape(N_DEV, buf4.shape[0], LANES)
    ds_ctx = _sum_devices(got4, "sum_c_ctx_grad").reshape(-1)[:Dm]
    behind_small = (ds_ctx[0:1] * 0.0).astype(BF16)
    ex3 = exchange_begin(3, [dW_in_t + behind_small])
    grads["c_ctx"] = ds_ctx * _dsilu(c_ctx) + zero_of(ex3)[0]

    for nm in ("g_final", "g_mix", "g_ffn", "q_gain", "k_gain", "w_sp", "b_sp"):
        grads[nm] = summed(nm).reshape(weights[nm].shape)
    for nm in ("b_pw1", "w_dw", "b_dw", "ln_g", "ln_b", "b_pw2"):
        n_loc = weights[nm].shape[-1]
        grads[nm] = lax.dynamic_slice_in_dim(summed(nm), me * n_loc, n_loc, axis=-1).reshape(weights[nm].shape)

    delta, new_m, new_v = {}, {}, {}
    shp = w_mod.shape
    outs = _adamw(w_mod.reshape(-1, shp[-1]), grads["w_mod"].reshape(-1, shp[-1]),
                  m_w_mod.reshape(-1, shp[-1]), v_w_mod.reshape(-1, shp[-1]), "adamw_w_mod")
    delta["w_mod"], new_m["w_mod"], new_v["w_mod"] = (o.reshape(shp) for o in outs)
    big_names = ("w_mod", "w_ffn_in", "w_ffn_out", "w_in", "w_out", "w_pw1", "w_pw2")
    small_names = [nm for nm in names if nm not in big_names]
    packs = [_pack([src[nm] for nm in small_names]) for src in (weights, grads, moments_m, moments_v)]
    offs_s = packs[0][1]
    outs = _adamw(*[pk[0] for pk in packs], "adamw_small")
    for o, dst in zip(outs, (delta, new_m, new_v)):
        o = o.reshape(-1)
        for nm, off in zip(small_names, offs_s):
            dst[nm] = _take(o, off, weights[nm].shape)

    def exchanged(k, handle, after):
        return _push_end(handle, after, f"exchange_wait{k}")[1]

    def adamw_big(nm, parts, transposed=False, wmv=None):
        w3, m3, v3 = wmv if wmv is not None else (weights[nm], moments_m[nm], moments_v[nm])
        outs4 = _adamw_recv(w3, m3, v3, parts, f"adamw_{nm}")
        if transposed:
            outs4 = [jnp.swapaxes(t, 1, 2) for t in outs4]
        grads[nm], delta[nm], new_m[nm], new_v[nm] = outs4

    r_ffi1, r_ffo1 = exchanged(0, ex0, outs[0])
    r_pw1, r_pw2 = exchanged(1, ex1, outs[0])
    r_ffi0, r_ffo0 = exchanged(2, ex2, outs[0])
    r_out, = exchanged(4, ex_out, outs[0])
    adamw_big("w_ffn_in", [r_ffi0, r_ffi1], True, (w_ffi_t, m_w_ffi_t, v_w_ffi_t))
    adamw_big("w_ffn_out", [r_ffo0, r_ffo1])
    adamw_big("w_pw1", [r_pw1])
    adamw_big("w_pw2", [r_pw2])
    adamw_big("w_out", [r_out])
    r_in, = exchanged(3, ex3, delta["w_out"])
    adamw_big("w_in", [r_in], True, (w_in_t, m_w_in_t, v_w_in_t))

    return (loss, dh0[None], *[grads[n] for n in names], *[delta[n] for n in names],
            *[new_m[n] for n in names], *[new_v[n] for n in names])
```

```python
import math

import jax
import jax.numpy as jnp
from jax import lax
from jax.experimental import pallas as pl
from jax.experimental.pallas import tpu as pltpu

F32 = jnp.float32
BF16 = jnp.bfloat16
MESH = pl.DeviceIdType.MESH

N_DEV = 8
D_MODEL = 1024
EPS = 1e-6
HEAD_DIM = 64
ATTN_WIDTH = 512
KV_WIDTH = 128
SG_WIDTH = 512
N_SG_GROUPS = 4
CHUNK = 128
IN_WIDTH = 1792
D_FF = 2816
FF_SHARD = 2 * D_FF // N_DEV
CONV_WIDTH = 31
CONV_HALO = 16
GRID_W = 64
ROPE_THETA = 10000.0
LANES = 128
SUBLANES = 8
ROW_BLOCK = 512
ADAM_ROWS = 256
ADAM_LR, ADAM_B1, ADAM_B2, ADAM_EPS, ADAM_WD, ADAM_STEP = 0.001, 0.9, 0.999, 1e-08, 0.01, 10


def _tile(n, target, mult=LANES):
    best = None
    for t in range(mult, min(n, target) + 1, mult):
        if n % t == 0:
            best = t
    return best if best is not None else n


def _sigmoid(x):
    return 1.0 / (1.0 + jnp.exp(-x))


def _silu(x):
    return x * _sigmoid(x)


def _dsilu(x):
    s = _sigmoid(x)
    return s * (1.0 + x * (1.0 - s))


_GELU_K = math.sqrt(2.0 / math.pi)


def _gelu(x):
    return 0.5 * x * (1.0 + jnp.tanh(_GELU_K * (x + 0.044715 * x * x * x)))


def _gelu_and_grad(x):
    x2 = x * x
    t = jnp.tanh(_GELU_K * x * (1.0 + 0.044715 * x2))
    half = 0.5 * (1.0 + t)
    return x * half, half + 0.5 * x * (1.0 - t * t) * _GELU_K * (1.0 + 3.0 * 0.044715 * x2)


def _split_bf16(x):
    hi = x.astype(BF16)
    lo = (x - hi.astype(F32)).astype(BF16)
    return hi, lo


def _dot(a, b, dims):
    return lax.dot_general(a, b, (dims, ((), ())), preferred_element_type=F32)


def _dot3(a, b, dims):
    ah, al = _split_bf16(a)
    bh, bl = _split_bf16(b)
    return _dot(ah, bh, dims) + _dot(ah, bl, dims) + _dot(al, bh, dims)


NN = ((1,), (0,))
NT = ((1,), (1,))
TN = ((0,), (0,))


def _all_gather(xs, name, in_vmem):
    n_arr = len(xs)

    def body(*refs):
        x_refs, out_refs = refs[:n_arr], refs[n_arr:2 * n_arr]
        send_sems, recv_sems, local_sems = refs[2 * n_arr:]
        x, y, c = lax.axis_index("x"), lax.axis_index("y"), lax.axis_index("c")
        me, sibling = (x, y, c), (x, y, 1 - c)
        chips = [(1 - x, y), (x, 1 - y), (1 - x, 1 - y)]

        def rows(a, px, py, pc):
            m_per = xs[a].shape[0]
            return out_refs[a].at[pl.ds((4 * px + 2 * py + pc) * m_per, m_per), :]

        def copy(a, k, block, to, src=None):
            return pltpu.make_async_remote_copy(
                src_ref=rows(a, *block) if src is None else src,
                dst_ref=rows(a, *block),
                send_sem=send_sems.at[7 * a + k],
                recv_sem=recv_sems.at[7 * a + k],
                device_id=to,
                device_id_type=MESH,
            )

        mine, first, passed = [], [], []
        for a in range(n_arr):
            mine.append(pltpu.make_async_copy(x_refs[a], rows(a, *me), local_sems.at[a]))
            mine[-1].start()
            first.append(copy(a, 0, me, sibling, src=x_refs[a]))
            first += [copy(a, 1 + j, me, (*chip, c), src=x_refs[a]) for j, chip in enumerate(chips)]
        for cp in first:
            cp.start()
        for a in range(n_arr):
            for j, chip in enumerate(chips):
                copy(a, 1 + j, (*chip, c), me).wait_recv()
                passed.append(copy(a, 4 + j, (*chip, c), sibling))
                passed[-1].start()
        for a in range(n_arr):
            copy(a, 0, sibling, me).wait_recv()
            for j, chip in enumerate(chips):
                copy(a, 4 + j, (*chip, 1 - c), me).wait_recv()
        for cp in first + passed:
            cp.wait_send()
        for cp in mine:
            cp.wait()

    space = pltpu.VMEM if in_vmem else pl.ANY
    return pl.pallas_call(
        body,
        name=name,
        out_shape=[jax.ShapeDtypeStruct((N_DEV * t.shape[0], t.shape[1]), t.dtype) for t in xs],
        in_specs=[pl.BlockSpec(memory_space=space)] * n_arr,
        out_specs=[pl.BlockSpec(memory_space=space)] * n_arr,
        scratch_shapes=[
            pltpu.SemaphoreType.DMA((7 * n_arr,)),
            pltpu.SemaphoreType.DMA((7 * n_arr,)),
            pltpu.SemaphoreType.DMA((n_arr,)),
        ],
    )(*xs)


HBM_SPEC = pl.BlockSpec(memory_space=pltpu.HBM)
SEM_SPEC = pl.BlockSpec(memory_space=pltpu.SEMAPHORE)
DATAFLOW_EFFECT = pltpu.SideEffectType.DATAFLOW_SIDE_EFFECTING


def _peers(x, y, c):
    for k in range(1, N_DEV):
        px = 1 - x if (k >> 2) & 1 else x
        py = 1 - y if (k >> 1) & 1 else y
        pc = 1 - c if k & 1 else c
        yield k - 1, (px, py, pc), 4 * px + 2 * py + pc


def _push_copies(src_refs, land_refs, send_sems, recv_sems, shapes, whole_src):
    x, y, c = lax.axis_index("x"), lax.axis_index("y"), lax.axis_index("c")
    me = 4 * x + 2 * y + c
    for a, (m_per, _) in enumerate(shapes):
        def block(ref, idx, m_per=m_per):
            return ref.at[pl.ds(idx * m_per, m_per), :]

        for k, peer, pidx in _peers(x, y, c):
            src = src_refs[a] if whole_src else block(src_refs[a], pidx)
            sems = dict(send_sem=send_sems.at[N_DEV * a + k], recv_sem=recv_sems.at[N_DEV * a + k],
                        device_id=peer, device_id_type=MESH)
            yield (pltpu.make_async_remote_copy(src_ref=src, dst_ref=block(land_refs[a], me), **sems),
                   pltpu.make_async_remote_copy(src_ref=src, dst_ref=block(land_refs[a], pidx), **sems))


def _own_copies(src_refs, land_refs, recv_sems, shapes, whole_src):
    me = 4 * lax.axis_index("x") + 2 * lax.axis_index("y") + lax.axis_index("c")
    for a, (m_per, _) in enumerate(shapes):
        mine = pl.ds(me * m_per, m_per)
        src = src_refs[a] if whole_src else src_refs[a].at[mine, :]
        yield pltpu.make_async_copy(src, land_refs[a].at[mine, :], recv_sems.at[N_DEV * a + N_DEV - 1])


def _push_begin(srcs, whole_src, name):
    n_arr = len(srcs)
    shapes = [(t.shape[0] if whole_src else t.shape[0] // N_DEV, t.shape[1]) for t in srcs]
    lands = [lax.empty((N_DEV * m, n), t.dtype) for (m, n), t in zip(shapes, srcs)]

    def body(*refs):
        src_refs, land_refs = refs[:n_arr], refs[n_arr:2 * n_arr]
        send_sems, recv_sems = refs[2 * n_arr], refs[2 * n_arr + 1]
        token = refs[-1]
        for outgoing, _ in _push_copies(src_refs, land_refs, send_sems, recv_sems, shapes, whole_src):
            outgoing.start()
        for own in _own_copies(src_refs, land_refs, recv_sems, shapes, whole_src):
            own.start()
        token[...] = jnp.zeros_like(token)

    operands = [pltpu.with_memory_space_constraint(t, pltpu.HBM) for t in list(srcs) + lands]
    outs = pl.pallas_call(
        body, name=name,
        out_shape=(pltpu.SemaphoreType.DMA((N_DEV * n_arr,)), pltpu.SemaphoreType.DMA((N_DEV * n_arr,)),
                   *[pltpu.HBM(t.shape, t.dtype) for t in operands],
                   jax.ShapeDtypeStruct((SUBLANES, LANES), F32)),
        in_specs=[HBM_SPEC] * (2 * n_arr),
        out_specs=(SEM_SPEC, SEM_SPEC, *[HBM_SPEC] * (2 * n_arr), pl.BlockSpec(memory_space=pltpu.VMEM)),
        input_output_aliases={i: 2 + i for i in range(2 * n_arr)},
        compiler_params=pltpu.CompilerParams(has_side_effects=DATAFLOW_EFFECT),
    )(*operands)
    return outs[0], outs[1], list(outs[2:2 + n_arr]), list(outs[2 + n_arr:2 + 2 * n_arr]), outs[-1], whole_src


def _push_end(handle, after, name):
    send_sems, recv_sems, srcs, lands, _, whole_src = handle
    n_arr = len(srcs)
    shapes = [(t.shape[0] // N_DEV, t.shape[1]) for t in lands]

    def body(*refs):
        src_refs, land_refs = refs[:n_arr], refs[n_arr:2 * n_arr]
        send_sems_ref, recv_sems_ref = refs[2 * n_arr], refs[2 * n_arr + 1]
        for outgoing, incoming in _push_copies(src_refs, land_refs, send_sems_ref, recv_sems_ref, shapes, whole_src):
            outgoing.wait_send()
            incoming.wait_recv()
        for own in _own_copies(src_refs, land_refs, recv_sems_ref, shapes, whole_src):
            own.wait()

    outs = pl.pallas_call(
        body, name=name,
        out_shape=tuple(pltpu.HBM(t.shape, t.dtype) for t in srcs + lands),
        in_specs=[HBM_SPEC] * (2 * n_arr) + [SEM_SPEC, SEM_SPEC, pl.BlockSpec(memory_space=pl.ANY)],
        out_specs=tuple([HBM_SPEC] * (2 * n_arr)),
        input_output_aliases={i: i for i in range(2 * n_arr)},
        compiler_params=pltpu.CompilerParams(has_side_effects=DATAFLOW_EFFECT),
    )(*srcs, *lands, send_sems, recv_sems, after)
    return list(outs[:n_arr]), list(outs[n_arr:])


def _sum_devices(r, name, rows_per_step=ADAM_ROWS):
    _, m, n = r.shape
    tm = _tile(m, rows_per_step, 8)

    def body(r_ref, o_ref):
        acc = r_ref[0].astype(F32)
        for s in range(1, N_DEV):
            acc = acc + r_ref[s].astype(F32)
        o_ref[...] = acc

    return pl.pallas_call(
        body,
        name=name,
        grid=(m // tm,),
        out_shape=jax.ShapeDtypeStruct((m, n), F32),
        in_specs=[pl.BlockSpec((N_DEV, tm, n), lambda i: (0, i, 0))],
        out_specs=pl.BlockSpec((tm, n), lambda i: (i, 0)),
        compiler_params=pltpu.CompilerParams(dimension_semantics=("parallel",)),
    )(r)


def _get(ref):
    return ref[0] if len(ref.shape) == 3 else ref[...]


def _put(ref, val):
    if len(ref.shape) == 3:
        ref[0] = val
    else:
        ref[...] = val


def _norm_mod(hv, g, sc, sh):
    r = lax.rsqrt(jnp.mean(hv * hv, axis=-1, keepdims=True) + EPS)
    return (hv * r) * g * (1.0 + sc) + sh


def _mm_call(name, a, b, a_spec, b_spec, out_sds, o_spec, grid, dims, acc_shape, bias=None,
             res=None, gate=None, raw_out=False, vec_spec=None, norm=None):
    nk = grid[2]
    operands, in_specs = [a, b], [a_spec, b_spec]
    if bias is not None:
        operands.append(bias)
        in_specs.append(vec_spec)
    if res is not None:
        operands += [res, gate]
        in_specs += [o_spec, vec_spec]
    if norm is not None:
        assert grid[1] == 1
        operands += list(norm)
        in_specs += [vec_spec] * 3
    out_shape, out_specs = [out_sds], [o_spec]
    if raw_out:
        out_shape.append(jax.ShapeDtypeStruct(out_sds.shape, BF16))
        out_specs.append(o_spec)
    if norm is not None:
        out_shape.append(jax.ShapeDtypeStruct(out_sds.shape, BF16))
        out_specs.append(o_spec)

    def body(*refs):
        it = iter(refs)
        a_ref, b_ref = next(it), next(it)
        bias_ref = next(it) if bias is not None else None
        res_ref, gate_ref = (next(it), next(it)) if res is not None else (None, None)
        norm_refs = (next(it), next(it), next(it)) if norm is not None else None
        o_ref = next(it)
        raw_ref = next(it) if raw_out else None
        xn_ref = next(it) if norm is not None else None
        acc = next(it) if nk > 1 else None
        k = pl.program_id(2)
        part = _dot(_get(a_ref).astype(BF16), _get(b_ref).astype(BF16), dims)

        def finish(y):
            if bias_ref is not None:
                y = y + bias_ref[...]
            if raw_ref is not None:
                raw_ref[...] = y.astype(BF16)
            if res_ref is not None:
                y = res_ref[...] + gate_ref[...] * y
            _put(o_ref, y.astype(out_sds.dtype))
            if xn_ref is not None:
                xn_ref[...] = _norm_mod(y, *[r[...] for r in norm_refs]).astype(BF16)

        if nk == 1:
            finish(part)
        else:
            @pl.when(k == 0)
            def _():
                acc[...] = part

            @pl.when(k > 0)
            def _():
                acc[...] += part

            @pl.when(k == nk - 1)
            def _():
                finish(acc[...])

    outs = pl.pallas_call(
        body,
        name=name,
        grid=grid,
        out_shape=out_shape,
        in_specs=in_specs,
        out_specs=out_specs,
        scratch_shapes=[pltpu.VMEM(acc_shape, F32)] if nk > 1 else [],
        compiler_params=pltpu.CompilerParams(dimension_semantics=("parallel", "parallel", "arbitrary")),
    )(*operands)
    return outs if len(outs) > 1 else outs[0]


def _mm(a, b, mode, name, out_dtype=F32, bias=None, res=None, gate=None, raw_out=False,
        tm=1024, tn=1024, tk=1024, a_row_off=0, norm=None):
    if mode == "nn":
        K, N = b.shape
        M = a.shape[0] - a_row_off
    elif mode == "nt":
        N, K = b.shape
        M = a.shape[0] - a_row_off
    else:
        (K, M), N = a.shape, b.shape[1]
    tm, tn, tk = _tile(M, tm, LANES if mode == "tn" else 2 * SUBLANES), _tile(N, tn), _tile(K, tk)
    off = a_row_off // tm
    dims = {"nn": NN, "nt": NT, "tn": TN}[mode]
    a_spec = (pl.BlockSpec((tk, tm), lambda i, j, k: (k, i)) if mode == "tn"
              else pl.BlockSpec((tm, tk), lambda i, j, k: (i + off, k)))
    b_spec = (pl.BlockSpec((tn, tk), lambda i, j, k: (j, k)) if mode == "nt"
              else pl.BlockSpec((tk, tn), lambda i, j, k: (k, j)))
    return _mm_call(name, a, b, a_spec, b_spec, jax.ShapeDtypeStruct((M, N), out_dtype),
                    pl.BlockSpec((tm, tn), lambda i, j, k: (i, j)), (M // tm, N // tn, K // tk), dims,
                    (tm, tn), bias, res, gate, raw_out, pl.BlockSpec((1, tn), lambda i, j, k: (0, j)), norm)


def _mm_sum_shards(a3, b3, mode, name, out_dtype=F32, res=None, gate=None, raw_out=False, tm=512, norm=None):
    S, M, kk = a3.shape
    N = b3.shape[2] if mode == "nn" else b3.shape[1]
    tm = _tile(M, tm)
    dims = NN if mode == "nn" else NT
    has_res = res is not None

    def body(*refs):
        it = iter(refs)
        a_ref, b_ref = next(it), next(it)
        res_ref, gate_ref = (next(it), next(it)) if has_res else (None, None)
        norm_refs = (next(it), next(it), next(it)) if norm is not None else None
        o_ref = next(it)
        raw_ref = next(it) if raw_out else None
        xn_ref = next(it) if norm is not None else None
        y = _dot(a_ref[0], b_ref[0], dims)
        for s in range(1, S):
            y = y + _dot(a_ref[s], b_ref[s], dims)
        if raw_ref is not None:
            raw_ref[...] = y.astype(BF16)
        if has_res:
            y = res_ref[...] + gate_ref[...] * y
        o_ref[...] = y.astype(out_dtype)
        if xn_ref is not None:
            xn_ref[...] = _norm_mod(y, *[r[...] for r in norm_refs]).astype(BF16)

    tile = pl.BlockSpec((tm, N), lambda i: (i, 0))
    operands = [a3, b3] + ([res, gate] if has_res else []) + (list(norm) if norm is not None else [])
    in_specs = [pl.BlockSpec((S, tm, kk), lambda i: (0, i, 0)), pl.BlockSpec(b3.shape, lambda i: (0, 0, 0))]
    in_specs += [tile, _vec_spec(N)] if has_res else []
    in_specs += [_vec_spec(N)] * 3 if norm is not None else []
    out_shape = [jax.ShapeDtypeStruct((M, N), out_dtype)] + ([jax.ShapeDtypeStruct((M, N), BF16)] if raw_out else [])
    out_shape += [jax.ShapeDtypeStruct((M, N), BF16)] if norm is not None else []
    outs = pl.pallas_call(
        body, name=name, grid=(M // tm,),
        out_shape=out_shape, in_specs=in_specs, out_specs=[tile] * len(out_shape),
        compiler_params=pltpu.CompilerParams(dimension_semantics=("parallel",)),
    )(*operands)
    return outs if len(outs) > 1 else outs[0]


def _mm_tn_shard_rows(a3, b, name, out_dtype, tn=1024, tk=4096):
    S, T, m = a3.shape
    N = b.shape[1]
    tn, tk = _tile(N, tn), _tile(T, tk)
    return _mm_call(name, a3, b, pl.BlockSpec((1, tk, m), lambda i, j, k: (i, k, 0)),
                    pl.BlockSpec((tk, tn), lambda i, j, k: (k, j)), jax.ShapeDtypeStruct((S, m, N), out_dtype),
                    pl.BlockSpec((1, m, tn), lambda i, j, k: (i, 0, j)), (S, N // tn, T // tk), TN, (m, tn))


def _row_spec(tm, width, off=0):
    return pl.BlockSpec((tm, width), lambda i: (i + off, 0))


def _vec_spec(width):
    return pl.BlockSpec((1, width), lambda i: (0, 0))


def _norm_mod_fwd_cat(hc, h, g, csc, csh, sc, sh, name):
    (C, Dm), T = hc.shape, h.shape[0]
    tm = _tile(math.gcd(C, T), ROW_BLOCK, 8)
    off = C // tm

    def body(hc_ref, h_ref, g_ref, csc_ref, csh_ref, sc_ref, sh_ref, o_ref):
        is_ctx = pl.program_id(0) < off
        hv = jnp.where(is_ctx, hc_ref[...], h_ref[...])
        scv = jnp.where(is_ctx, csc_ref[...], sc_ref[...])
        shv = jnp.where(is_ctx, csh_ref[...], sh_ref[...])
        r = lax.rsqrt(jnp.mean(hv * hv, axis=-1, keepdims=True) + EPS)
        o_ref[...] = ((hv * r) * g_ref[...] * (1.0 + scv) + shv).astype(BF16)

    return pl.pallas_call(
        body, name=name, grid=((C + T) // tm,),
        out_shape=jax.ShapeDtypeStruct((C + T, Dm), BF16),
        in_specs=[pl.BlockSpec((tm, Dm), lambda i: (jnp.minimum(i, off - 1), 0)),
                  pl.BlockSpec((tm, Dm), lambda i: (jnp.maximum(i - off, 0), 0))] + [_vec_spec(Dm)] * 5,
        out_specs=_row_spec(tm, Dm),
        compiler_params=pltpu.CompilerParams(dimension_semantics=("parallel",)),
    )(hc, h, g, csc, csh, sc, sh)


def _gate_grads(dh, y_ref, gt_ref, dy_ref, dgt_ref, dsum_ref):
    dy = dh * gt_ref[...]
    dgt_ref[...] += jnp.sum(dh * y_ref[...].astype(F32), axis=0, keepdims=True)
    dsum_ref[...] += jnp.sum(dy, axis=0, keepdims=True)
    dy_ref[...] = dy.astype(BF16)


def _norm_mod_bwd(h, g, sc, dxm, dres, name, dxm_row_off=0, gate=None):
    R, Dm = h.shape
    tm = _tile(math.gcd(R, dxm_row_off) if dxm_row_off else R, ROW_BLOCK, 8)
    off = dxm_row_off // tm
    has_res = dres is not None
    has_gate = gate is not None

    def body(*refs):
        it = iter(refs)
        h_ref, g_ref, sc_ref, dx_ref = next(it), next(it), next(it), next(it)
        dres_ref = next(it) if has_res else None
        y_ref, gt_ref = (next(it), next(it)) if has_gate else (None, None)
        dh_ref, da_ref, dsh_ref = next(it), next(it), next(it)
        gate_out = (next(it), next(it), next(it)) if has_gate else ()
        i = pl.program_id(0)

        @pl.when(i == 0)
        def _():
            for ref in (da_ref, dsh_ref) + gate_out[1:]:
                ref[...] = jnp.zeros_like(ref)

        hv = h_ref[...]
        dx = dx_ref[...].astype(F32)
        r = lax.rsqrt(jnp.mean(hv * hv, axis=-1, keepdims=True) + EPS)
        n = hv * r
        da_ref[...] += jnp.sum(dx * n, axis=0, keepdims=True)
        dsh_ref[...] += jnp.sum(dx, axis=0, keepdims=True)
        dn = dx * (g_ref[...] * (1.0 + sc_ref[...]))
        dh = r * (dn - n * jnp.mean(dn * n, axis=-1, keepdims=True))
        if has_res:
            dh = dh + dres_ref[...]
        dh_ref[...] = dh
        if has_gate:
            _gate_grads(dh, y_ref, gt_ref, *gate_out)

    operands = [h, g, sc, dxm] + ([dres] if has_res else []) + (list(gate) if has_gate else [])
    in_specs = [_row_spec(tm, Dm), _vec_spec(Dm), _vec_spec(Dm), _row_spec(tm, Dm, off)]
    in_specs += [_row_spec(tm, Dm)] if has_res else []
    in_specs += [_row_spec(tm, Dm), _vec_spec(Dm)] if has_gate else []
    vec = jax.ShapeDtypeStruct((1, Dm), F32)
    out_shape = [jax.ShapeDtypeStruct((R, Dm), F32), vec, vec]
    out_specs = [_row_spec(tm, Dm), _vec_spec(Dm), _vec_spec(Dm)]
    if has_gate:
        out_shape += [jax.ShapeDtypeStruct((R, Dm), BF16), vec, vec]
        out_specs += [_row_spec(tm, Dm), _vec_spec(Dm), _vec_spec(Dm)]
    return pl.pallas_call(
        body, name=name, grid=(R // tm,),
        out_shape=out_shape, in_specs=in_specs, out_specs=out_specs,
        compiler_params=pltpu.CompilerParams(dimension_semantics=("arbitrary",)),
    )(*operands)


def _ffn_in_swiglu(xf, w3, name, tm=1024):
    T, K = xf.shape
    S, n, _ = w3.shape
    half = S // 2
    tm = _tile(T, tm)

    def body(a_ref, wg_ref, wu_ref, gu_ref, act_ref):
        a = a_ref[...]
        g = _dot(a, wg_ref[0], NT)
        u = _dot(a, wu_ref[0], NT)
        gu_ref[0, 0] = g.astype(BF16)
        gu_ref[1, 0] = u.astype(BF16)
        act_ref[0] = (_silu(g) * u).astype(BF16)

    return pl.pallas_call(
        body, name=name, grid=(T // tm, half),
        out_shape=[jax.ShapeDtypeStruct((2, half, T, n), BF16), jax.ShapeDtypeStruct((half, T, n), BF16)],
        in_specs=[pl.BlockSpec((tm, K), lambda i, j: (i, 0)),
                  pl.BlockSpec((1, n, K), lambda i, j: (j, 0, 0)),
                  pl.BlockSpec((1, n, K), lambda i, j: (j + half, 0, 0))],
        out_specs=[pl.BlockSpec((2, 1, tm, n), lambda i, j: (0, j, i, 0)),
                   pl.BlockSpec((1, tm, n), lambda i, j: (j, i, 0))],
        compiler_params=pltpu.CompilerParams(dimension_semantics=("parallel", "parallel")),
    )(xf, w3, w3)


def _ffn_out_dx_swiglu(df, wo, gu, name, tm=1024):
    T, Dm = df.shape
    half, n, _ = wo.shape
    tm = _tile(T, tm)

    def body(df_ref, w_ref, gu_ref, o_ref):
        da = _dot(df_ref[...], w_ref[0], NT)
        g = gu_ref[0, 0].astype(F32)
        u = gu_ref[1, 0].astype(F32)
        s = _sigmoid(g)
        o_ref[0, 0] = (da * u * (s * (1.0 + g * (1.0 - s)))).astype(BF16)
        o_ref[1, 0] = (da * (g * s)).astype(BF16)

    gu_spec = pl.BlockSpec((2, 1, tm, n), lambda i, j: (0, j, i, 0))
    return pl.pallas_call(
        body, name=name, grid=(T // tm, half),
        out_shape=jax.ShapeDtypeStruct(gu.shape, BF16),
        in_specs=[pl.BlockSpec((tm, Dm), lambda i, j: (i, 0)),
                  pl.BlockSpec((1, n, Dm), lambda i, j: (j, 0, 0)), gu_spec],
        out_specs=gu_spec,
        compiler_params=pltpu.CompilerParams(dimension_semantics=("parallel", "parallel")),
    )(df, wo, gu)


def _glu_fwd(ag, name):
    R = ag.shape[0]
    tm = _tile(R, ROW_BLOCK, 8)

    def body(ag_ref, o_ref):
        o_ref[...] = ag_ref[:, :D_MODEL].astype(F32) * _sigmoid(ag_ref[:, D_MODEL:].astype(F32))

    return pl.pallas_call(
        body, name=name, grid=(R // tm,),
        out_shape=jax.ShapeDtypeStruct((R, D_MODEL), F32),
        in_specs=[_row_spec(tm, 2 * D_MODEL)],
        out_specs=_row_spec(tm, D_MODEL),
        compiler_params=pltpu.CompilerParams(dimension_semantics=("parallel",)),
    )(ag)


def _glu_bwd(ag, dhg, name):
    R = ag.shape[0]
    tm = _tile(R, ROW_BLOCK, 8)

    def body(ag_ref, dh_ref, o_ref, s_ref):
        i = pl.program_id(0)

        @pl.when(i == 0)
        def _():
            s_ref[...] = jnp.zeros_like(s_ref)

        a = ag_ref[:, :D_MODEL].astype(F32)
        s = _sigmoid(ag_ref[:, D_MODEL:].astype(F32))
        dh = dh_ref[...]
        da = dh * s
        dg = dh * a * s * (1.0 - s)
        o_ref[:, :D_MODEL] = da.astype(BF16)
        o_ref[:, D_MODEL:] = dg.astype(BF16)
        s_ref[:, :D_MODEL] += jnp.sum(da, axis=0, keepdims=True)
        s_ref[:, D_MODEL:] += jnp.sum(dg, axis=0, keepdims=True)

    return pl.pallas_call(
        body, name=name, grid=(R // tm,),
        out_shape=[jax.ShapeDtypeStruct((R, 2 * D_MODEL), BF16), jax.ShapeDtypeStruct((1, 2 * D_MODEL), F32)],
        in_specs=[_row_spec(tm, 2 * D_MODEL), _row_spec(tm, D_MODEL)],
        out_specs=[_row_spec(tm, 2 * D_MODEL), _vec_spec(2 * D_MODEL)],
        compiler_params=pltpu.CompilerParams(dimension_semantics=("arbitrary",)),
    )(ag, dhg)


def _halo_specs(tm, nblk, width):
    per = tm // CONV_HALO
    prev = pl.BlockSpec((CONV_HALO, width), lambda i: (jnp.maximum(i * per - 1, 0), 0))
    nxt = pl.BlockSpec((CONV_HALO, width), lambda i: (jnp.minimum((i + 1) * per, nblk * per - 1), 0))
    return prev, nxt


def _fill_halo(scr, prev_ref, cur_ref, next_ref, i, nblk, tm):
    scr[0:CONV_HALO, :] = jnp.where(i > 0, prev_ref[...], 0.0)
    scr[CONV_HALO:CONV_HALO + tm, :] = cur_ref[...]
    scr[CONV_HALO + tm:2 * CONV_HALO + tm, :] = jnp.where(i < nblk - 1, next_ref[...], 0.0)


CONV_ROWS = 128


CONV_REACH = (CONV_WIDTH // SUBLANES) * SUBLANES


def _windows(scr, stage, cols, tm):
    for r in range(SUBLANES):
        if r:
            stage[r] = scr[pl.ds(r, tm + CONV_REACH), cols]
        for a in range(CONV_REACH // SUBLANES + 1):
            off = SUBLANES * a + r
            if 1 <= off <= CONV_WIDTH:
                yield off, (stage[r, SUBLANES * a:SUBLANES * a + tm, :] if r
                            else scr[SUBLANES * a:SUBLANES * a + tm, cols])


def _conv_fwd(hg, w_dw, b_dw, name):
    R, Dm = hg.shape
    tm = _tile(R, CONV_ROWS, CONV_HALO)
    nblk = R // tm
    prev_spec, next_spec = _halo_specs(tm, nblk, Dm)

    def body(prev_ref, cur_ref, next_ref, w_ref, bdw_ref, hd_ref, scr, stage):
        _fill_halo(scr, prev_ref, cur_ref, next_ref, pl.program_id(0), nblk, tm)
        for cb in range(Dm // LANES):
            cols = slice(cb * LANES, (cb + 1) * LANES)
            acc = jnp.zeros((tm, LANES), F32) + bdw_ref[:, cols]
            for off, win in _windows(scr, stage, cols, tm):
                acc = acc + w_ref[off - 1:off, cols] * win
            hd_ref[:, cols] = acc

    return pl.pallas_call(
        body, name=name, grid=(nblk,),
        out_shape=jax.ShapeDtypeStruct((R, Dm), F32),
        in_specs=[prev_spec, _row_spec(tm, Dm), next_spec,
                  pl.BlockSpec((CONV_WIDTH, Dm), lambda i: (0, 0)), _vec_spec(Dm)],
        out_specs=_row_spec(tm, Dm),
        scratch_shapes=[pltpu.VMEM((tm + 2 * CONV_HALO, Dm), F32),
                        pltpu.VMEM((SUBLANES, tm + CONV_REACH, LANES), F32)],
        compiler_params=pltpu.CompilerParams(dimension_semantics=("parallel",)),
    )(hg, hg, hg, w_dw, b_dw)


def _ln_silu_fwd(hd, ln_g, ln_b, name):
    R, Dm = hd.shape
    tm = _tile(R, ROW_BLOCK, 8)

    def body(hd_ref, g_ref, b_ref, hs_ref):
        hd = hd_ref[...]
        xc = hd - jnp.mean(hd, axis=-1, keepdims=True)
        rs = lax.rsqrt(jnp.mean(xc * xc, axis=-1, keepdims=True) + EPS)
        hs_ref[...] = _silu(xc * rs * g_ref[...] + b_ref[...]).astype(BF16)

    return pl.pallas_call(
        body, name=name, grid=(R // tm,),
        out_shape=jax.ShapeDtypeStruct((R, Dm), BF16),
        in_specs=[_row_spec(tm, Dm), _vec_spec(Dm), _vec_spec(Dm)],
        out_specs=_row_spec(tm, Dm),
        compiler_params=pltpu.CompilerParams(dimension_semantics=("parallel",)),
    )(hd, ln_g, ln_b)


def _ln_silu_bwd(dhs, hd, ln_g, ln_b, name):
    R, Dm = hd.shape
    tm = _tile(R, ROW_BLOCK, 8)

    def body(dhs_ref, hd_ref, g_ref, b_ref, dhd_ref, dg_ref, db_ref, dsum_ref):
        i = pl.program_id(0)

        @pl.when(i == 0)
        def _():
            dg_ref[...] = jnp.zeros_like(dg_ref)
            db_ref[...] = jnp.zeros_like(db_ref)
            dsum_ref[...] = jnp.zeros_like(dsum_ref)

        hd = hd_ref[...]
        mu = jnp.mean(hd, axis=-1, keepdims=True)
        xc = hd - mu
        rs = lax.rsqrt(jnp.mean(xc * xc, axis=-1, keepdims=True) + EPS)
        z = xc * rs
        hl = z * g_ref[...] + b_ref[...]
        dhl = dhs_ref[...].astype(F32) * _dsilu(hl)
        dg_ref[...] += jnp.sum(dhl * z, axis=0, keepdims=True)
        db_ref[...] += jnp.sum(dhl, axis=0, keepdims=True)
        dz = dhl * g_ref[...]
        dhd = rs * (dz - jnp.mean(dz, axis=-1, keepdims=True) - z * jnp.mean(dz * z, axis=-1, keepdims=True))
        dsum_ref[...] += jnp.sum(dhd, axis=0, keepdims=True)
        dhd_ref[...] = dhd

    return pl.pallas_call(
        body, name=name, grid=(R // tm,),
        out_shape=[jax.ShapeDtypeStruct((R, Dm), F32)] + [jax.ShapeDtypeStruct((1, Dm), F32)] * 3,
        in_specs=[_row_spec(tm, Dm), _row_spec(tm, Dm), _vec_spec(Dm), _vec_spec(Dm)],
        out_specs=[_row_spec(tm, Dm), _vec_spec(Dm), _vec_spec(Dm), _vec_spec(Dm)],
        compiler_params=pltpu.CompilerParams(dimension_semantics=("arbitrary",)),
    )(dhs, hd, ln_g, ln_b)


def _conv_bwd(dhd, hg, w_dw, name):
    R, Dm = hg.shape
    tm = _tile(R, CONV_ROWS, CONV_HALO)
    nblk = R // tm
    prev_spec, next_spec = _halo_specs(tm, nblk, Dm)

    def body(dprev, dcur, dnext, gprev, gcur, gnext, w_ref, dhg_ref, dw_ref, dscr, gscr, dwp, stage):
        i = pl.program_id(0)

        @pl.when(i == 0)
        def _():
            dwp[...] = jnp.zeros_like(dwp)

        _fill_halo(dscr, dprev, dcur, dnext, i, nblk, tm)
        _fill_halo(gscr, gprev, gcur, gnext, i, nblk, tm)
        for cb in range(Dm // LANES):
            cols = slice(cb * LANES, (cb + 1) * LANES)
            acc = jnp.zeros((tm, LANES), F32)
            for off, win in _windows(dscr, stage, cols, tm):
                j = CONV_WIDTH - off
                acc = acc + w_ref[j:j + 1, cols] * win
            dhg_ref[:, cols] = acc
            d_here = dcur[:, cols]
            for off, win in _windows(gscr, stage, cols, tm):
                j = off - 1
                prod = d_here * win
                part = prod[0:SUBLANES]
                for k in range(1, tm // SUBLANES):
                    part = part + prod[k * SUBLANES:(k + 1) * SUBLANES]
                dwp[j * SUBLANES:(j + 1) * SUBLANES, cols] += part

        @pl.when(i == nblk - 1)
        def _():
            for j in range(CONV_WIDTH):
                dw_ref[j:j + 1, :] = jnp.sum(dwp[j * SUBLANES:(j + 1) * SUBLANES, :], axis=0, keepdims=True)

    return pl.pallas_call(
        body, name=name, grid=(nblk,),
        out_shape=[jax.ShapeDtypeStruct((R, Dm), F32), jax.ShapeDtypeStruct((CONV_WIDTH, Dm), F32)],
        in_specs=[prev_spec, _row_spec(tm, Dm), next_spec, prev_spec, _row_spec(tm, Dm), next_spec,
                  pl.BlockSpec((CONV_WIDTH, Dm), lambda i: (0, 0))],
        out_specs=[_row_spec(tm, Dm), pl.BlockSpec((CONV_WIDTH, Dm), lambda i: (0, 0))],
        scratch_shapes=[pltpu.VMEM((tm + 2 * CONV_HALO, Dm), F32)] * 2
        + [pltpu.VMEM((CONV_WIDTH * SUBLANES, Dm), F32), pltpu.VMEM((SUBLANES, tm + CONV_REACH, LANES), F32)],
        compiler_params=pltpu.CompilerParams(dimension_semantics=("arbitrary",)),
    )(dhd, dhd, dhd, hg, hg, hg, w_dw)


def _swap16(y, lane):
    return jnp.where((lane & 16) == 0, pltpu.roll(y, LANES - 16, 1), pltpu.roll(y, 16, 1))


def _head_mean(v, bd):
    hi, lo = _split_bf16(v)
    return (_dot(hi, bd, NN) + _dot(lo, bd, NN)) * (1.0 / HEAD_DIM)


Q_COLS = (0, ATTN_WIDTH)
K_COLS = (ATTN_WIDTH, ATTN_WIDTH + HEAD_DIM * 2)
V_COLS = (K_COLS[1], K_COLS[1] + HEAD_DIM * 2)
SU_COLS = (V_COLS[1], V_COLS[1] + SG_WIDTH)
SV_COLS = (SU_COLS[1], SU_COLS[1] + SG_WIDTH)


def _mix_prep_fwd(p, ctx_rows, cos, sin, qg, kg, bd, w_sp, b_spt, name):
    TT = p.shape[0]
    off = ctx_rows // CHUNK
    q_scale = HEAD_DIM ** -0.5

    def body(p_ref, cos_ref, sin_ref, qg_ref, kg_ref, bd_ref, w_ref, b_ref,
             q_ref, kp_ref, vp_ref, kt_ref, sg_ref):
        lane = lax.broadcasted_iota(jnp.int32, (CHUNK, LANES), 1)
        low = lane < HEAD_DIM
        cs, sn, bdv = cos_ref[...], sin_ref[...], bd_ref[...]

        def norm_rope(xv, gain):
            r = lax.rsqrt(_head_mean(xv * xv, bdv) + EPS)
            yv = xv * r * gain
            return yv * cs + _swap16(yv, lane) * sn

        def pad_heads(ref, t):
            tr = pltpu.roll(t, HEAD_DIM, 1)
            ref[0, 0] = jnp.where(low, t, 0.0).astype(BF16)
            ref[0, 1] = jnp.where(low, 0.0, tr).astype(BF16)
            ref[1, 0] = jnp.where(low, tr, 0.0).astype(BF16)
            ref[1, 1] = jnp.where(low, 0.0, t).astype(BF16)

        for a in range(ATTN_WIDTH // LANES):
            xv = p_ref[:, a * LANES:(a + 1) * LANES]
            q_ref[:, a * LANES:(a + 1) * LANES] = (norm_rope(xv, qg_ref[...]) * q_scale).astype(BF16)
        kh = norm_rope(p_ref[:, K_COLS[0]:K_COLS[1]], kg_ref[...])
        pad_heads(kp_ref, kh)
        pad_heads(vp_ref, p_ref[:, V_COLS[0]:V_COLS[1]])
        kht = kh.T
        kt_ref[0] = kht[:HEAD_DIM].astype(BF16)
        kt_ref[1] = kht[HEAD_DIM:].astype(BF16)
        for g in range(N_SG_GROUPS):
            u = _gelu(p_ref[:, SU_COLS[0] + g * LANES:SU_COLS[0] + (g + 1) * LANES])
            vg = _gelu(p_ref[:, SV_COLS[0] + g * LANES:SV_COLS[0] + (g + 1) * LANES])
            xc = vg - jnp.mean(vg, axis=-1, keepdims=True)
            vn = xc * lax.rsqrt(jnp.mean(xc * xc, axis=-1, keepdims=True) + EPS)
            mixed = _dot(w_ref[g].astype(BF16), vn.astype(BF16), NN) + b_ref[:, g:g + 1]
            sg_ref[:, g * LANES:(g + 1) * LANES] = (u * mixed).astype(BF16)

    def row(width):
        return pl.BlockSpec((CHUNK, width), lambda i: (i, 0))

    def whole(shape):
        return pl.BlockSpec(shape, lambda i: (0,) * len(shape))

    pad_spec = pl.BlockSpec((2, 2, CHUNK, LANES), lambda i: (0, 0, i, 0))
    return pl.pallas_call(
        body, name=name, grid=(TT // CHUNK,),
        out_shape=[jax.ShapeDtypeStruct((TT, ATTN_WIDTH), BF16),
                   jax.ShapeDtypeStruct((2, 2, TT, LANES), BF16), jax.ShapeDtypeStruct((2, 2, TT, LANES), BF16),
                   jax.ShapeDtypeStruct((2, HEAD_DIM, TT), BF16),
                   jax.ShapeDtypeStruct((TT - ctx_rows, ATTN_WIDTH + SG_WIDTH), BF16)],
        in_specs=[row(IN_WIDTH), row(LANES), row(LANES), whole((1, LANES)), whole((1, LANES)),
                  whole((LANES, LANES)), whole((N_SG_GROUPS, CHUNK, CHUNK)), whole((CHUNK, N_SG_GROUPS))],
        out_specs=[row(ATTN_WIDTH), pad_spec, pad_spec,
                   pl.BlockSpec((2, HEAD_DIM, CHUNK), lambda i: (0, 0, i)),
                   pl.BlockSpec((CHUNK, SG_WIDTH), lambda i: (jnp.maximum(i - off, 0), 1))],
        compiler_params=pltpu.CompilerParams(dimension_semantics=("arbitrary",)),
    )(p, cos, sin, qg, kg, bd, w_sp, b_spt)


def _mix_prep_bwd(p, dq, f, dao, ctx_rows, cos, sin, qg, kg, bd, w_sp, w_spt, b_spt, name):
    TT = p.shape[0]
    off = ctx_rows // CHUNK
    q_scale = HEAD_DIM ** -0.5

    def body(p_ref, dq_ref, f_ref, dsg_ref, cos_ref, sin_ref, qg_ref, kg_ref, bd_ref, w_ref, wt_ref,
             b_ref, dp_ref, dqg_ref, dkg_ref, dw_ref, db_ref):
        i = pl.program_id(0)

        @pl.when(i == 0)
        def _():
            dqg_ref[...] = jnp.zeros_like(dqg_ref)
            dkg_ref[...] = jnp.zeros_like(dkg_ref)
            dw_ref[...] = jnp.zeros_like(dw_ref)
            db_ref[...] = jnp.zeros_like(db_ref)

        latent = (i >= off).astype(F32)
        lane = lax.broadcasted_iota(jnp.int32, (CHUNK, LANES), 1)
        low = lane < HEAD_DIM
        cs, sn, bdv = cos_ref[...], sin_ref[...], bd_ref[...]

        def fold(b0):
            return jnp.where(low, f_ref[0, b0] + pltpu.roll(f_ref[0, b0 + 1], HEAD_DIM, 1),
                             pltpu.roll(f_ref[1, b0], HEAD_DIM, 1) + f_ref[1, b0 + 1])

        def norm_rope_bwd(xv, dout, gain):
            r = lax.rsqrt(_head_mean(xv * xv, bdv) + EPS)
            n = xv * r
            dy = dout * cs + _swap16(dout * sn, lane)
            dn = dy * gain
            dx = r * (dn - n * _head_mean(dn * n, bdv))
            return dx, jnp.sum(dy * n, axis=0, keepdims=True)

        for a in range(ATTN_WIDTH // LANES):
            cols = slice(a * LANES, (a + 1) * LANES)
            dx, dg = norm_rope_bwd(p_ref[:, cols], dq_ref[:, cols] * (latent * q_scale), qg_ref[...])
            dp_ref[:, cols] = dx.astype(BF16)
            dqg_ref[...] += dg
        dx, dg = norm_rope_bwd(p_ref[:, K_COLS[0]:K_COLS[1]], fold(0), kg_ref[...])
        dp_ref[:, K_COLS[0]:K_COLS[1]] = dx.astype(BF16)
        dkg_ref[...] += dg
        dp_ref[:, V_COLS[0]:V_COLS[1]] = fold(2).astype(BF16)
        for g in range(N_SG_GROUPS):
            su = p_ref[:, SU_COLS[0] + g * LANES:SU_COLS[0] + (g + 1) * LANES]
            sv = p_ref[:, SV_COLS[0] + g * LANES:SV_COLS[0] + (g + 1) * LANES]
            (u, dgelu_su), (vg, dgelu_sv) = _gelu_and_grad(su), _gelu_and_grad(sv)
            xc = vg - jnp.mean(vg, axis=-1, keepdims=True)
            rs = lax.rsqrt(jnp.mean(xc * xc, axis=-1, keepdims=True) + EPS)
            vn = xc * rs
            vnb = vn.astype(BF16)
            mixed = _dot(w_ref[g].astype(BF16), vnb, NN) + b_ref[:, g:g + 1]
            dsg = dsg_ref[:, g * LANES:(g + 1) * LANES].astype(F32) * latent
            du = dsg * mixed
            dmix = dsg * u
            dmb = dmix.astype(BF16)
            db_ref[:, g:g + 1] += jnp.sum(dmix, axis=-1, keepdims=True)
            dw_ref[g] += _dot(dmb, vnb, NT)
            dvn = _dot(wt_ref[g].astype(BF16), dmb, NN)
            dvg = rs * (dvn - jnp.mean(dvn, axis=-1, keepdims=True)
                        - vn * jnp.mean(dvn * vn, axis=-1, keepdims=True))
            dp_ref[:, SU_COLS[0] + g * LANES:SU_COLS[0] + (g + 1) * LANES] = (du * dgelu_su).astype(BF16)
            dp_ref[:, SV_COLS[0] + g * LANES:SV_COLS[0] + (g + 1) * LANES] = (dvg * dgelu_sv).astype(BF16)

    def row(width):
        return pl.BlockSpec((CHUNK, width), lambda i: (i, 0))

    def latent_row(width, col_block):
        return pl.BlockSpec((CHUNK, width), lambda i: (jnp.maximum(i - off, 0), col_block))

    def whole(shape):
        return pl.BlockSpec(shape, lambda i: (0,) * len(shape))

    return pl.pallas_call(
        body, name=name, grid=(TT // CHUNK,),
        out_shape=[jax.ShapeDtypeStruct((TT, IN_WIDTH), BF16), jax.ShapeDtypeStruct((1, LANES), F32),
                   jax.ShapeDtypeStruct((1, LANES), F32),
                   jax.ShapeDtypeStruct((N_SG_GROUPS, CHUNK, CHUNK), F32),
                   jax.ShapeDtypeStruct((CHUNK, N_SG_GROUPS), F32)],
        in_specs=[row(IN_WIDTH), latent_row(ATTN_WIDTH, 0),
                  pl.BlockSpec((2, 4, CHUNK, LANES), lambda i: (0, 0, i, 0)),
                  latent_row(SG_WIDTH, 1), row(LANES), row(LANES), whole((1, LANES)), whole((1, LANES)),
                  whole((LANES, LANES)), whole((N_SG_GROUPS, CHUNK, CHUNK)),
                  whole((N_SG_GROUPS, CHUNK, CHUNK)), whole((CHUNK, N_SG_GROUPS))],
        out_specs=[row(IN_WIDTH), whole((1, LANES)), whole((1, LANES)),
                   whole((N_SG_GROUPS, CHUNK, CHUNK)), whole((CHUNK, N_SG_GROUPS))],
        compiler_params=pltpu.CompilerParams(dimension_semantics=("arbitrary",)),
    )(p, dq, f, dao, cos, sin, qg, kg, bd, w_sp, w_spt, b_spt)


def _attn_fwd(q, kpad, vpad, ao, ctx_rows, name, tq=256):
    TT = q.shape[0]
    T = TT - ctx_rows
    tq = _tile(T, tq)
    off = ctx_rows // tq
    group = 2 * LANES

    def body(q_ref, k_ref, v_ref, ao_in, o_ref, lse_ref):
        del ao_in
        lane = lax.broadcasted_iota(jnp.int32, (tq, LANES), 1)
        lse = jnp.zeros((tq, LANES), F32)
        for a in range(2):
            acc = jnp.zeros((tq, LANES), F32)
            qa = q_ref[:, a * LANES:(a + 1) * LANES]
            for b in range(2):
                s = _dot(qa, k_ref[0, b], NT)
                m = jnp.max(s, axis=-1, keepdims=True)
                e = jnp.exp(s - m)
                l = jnp.sum(e, axis=-1, keepdims=True)
                acc = acc + _dot(e.astype(BF16), v_ref[0, b], NN) * (1.0 / l)
                lse = jnp.where(lane == 2 * a + b, m + jnp.log(l), lse)
            o_ref[:, a * LANES:(a + 1) * LANES] = acc.astype(BF16)
        lse_ref[0] = lse

    kv_spec = pl.BlockSpec((1, 2, TT, LANES), lambda j, i: (j, 0, 0, 0))
    return pl.pallas_call(
        body, name=name, grid=(2, T // tq),
        out_shape=[jax.ShapeDtypeStruct(ao.shape, BF16), jax.ShapeDtypeStruct((2, T, LANES), F32)],
        in_specs=[pl.BlockSpec((tq, group), lambda j, i: (i + off, j)), kv_spec, kv_spec,
                  pl.BlockSpec(memory_space=pl.ANY)],
        out_specs=[pl.BlockSpec((tq, group), lambda j, i: (i, j)),
                   pl.BlockSpec((1, tq, LANES), lambda j, i: (j, i, 0))],
        input_output_aliases={3: 0},
        compiler_params=pltpu.CompilerParams(dimension_semantics=("parallel", "parallel")),
    )(q, kpad, vpad, ao)


def _attn_bwd(q, dao, ao, lse, kpad, vpad, kt, ctx_rows, name, tq=256):
    TT = q.shape[0]
    T = TT - ctx_rows
    tq = _tile(T, tq)
    off = ctx_rows // tq
    group = 2 * LANES

    def body(q_ref, do_ref, o_ref, lse_ref, k_ref, v_ref, kt_ref, dq_ref, f_ref):
        i = pl.program_id(1)

        @pl.when(i == 0)
        def _():
            f_ref[...] = jnp.zeros_like(f_ref)

        ktv = kt_ref[0]
        lse_t = lse_ref[0].T
        row = lax.broadcasted_iota(jnp.int32, (SUBLANES, LANES), 0)
        lane = lax.broadcasted_iota(jnp.int32, (SUBLANES, LANES), 1)
        half_ones = (jnp.where(lane < HEAD_DIM, 0, 1) == row).astype(BF16)
        for a in range(2):
            cols = slice(a * LANES, (a + 1) * LANES)
            qa = q_ref[:, cols]
            do32 = do_ref[:, cols].astype(F32)
            doa = do32.astype(BF16)
            hi, lo = _split_bf16(do32 * o_ref[:, cols].astype(F32))
            deltas = _dot(half_ones, hi, NT) + _dot(half_ones, lo, NT)
            halves = []
            for b in range(2):
                h = 2 * a + b
                st = _dot(k_ref[0, b], qa, NT)
                pt = jnp.exp(st - lse_t[h:h + 1, :])
                dpt = _dot(v_ref[0, b], doa, NT)
                dst = (pt * (dpt - deltas[b:b + 1, :])).astype(BF16)
                f_ref[0, b] += _dot(dst, qa, NN)
                f_ref[0, 2 + b] += _dot(pt.astype(BF16), doa, NN)
                halves.append(_dot(ktv, dst, NN))
            dq_ref[:, cols] = jnp.concatenate(halves, axis=0).T

    kv_spec = pl.BlockSpec((1, 2, TT, LANES), lambda j, i: (j, 0, 0, 0))
    out_cols = pl.BlockSpec((tq, group), lambda j, i: (i, j))
    return pl.pallas_call(
        body, name=name, grid=(2, T // tq),
        out_shape=[jax.ShapeDtypeStruct((T, ATTN_WIDTH), F32), jax.ShapeDtypeStruct((2, 4, TT, LANES), F32)],
        in_specs=[pl.BlockSpec((tq, group), lambda j, i: (i + off, j)), out_cols, out_cols,
                  pl.BlockSpec((1, tq, LANES), lambda j, i: (j, i, 0)),
                  kv_spec, kv_spec, pl.BlockSpec((1, HEAD_DIM, TT), lambda j, i: (j, 0, 0))],
        out_specs=[out_cols, pl.BlockSpec((1, 4, TT, LANES), lambda j, i: (j, 0, 0, 0))],
        compiler_params=pltpu.CompilerParams(dimension_semantics=("parallel", "arbitrary")),
    )(q, dao, ao, lse, kpad, vpad, kt)


def _final_fwd_bwd(h, g, target, y, gt, name):
    R, Dm = h.shape
    tm = _tile(R, ROW_BLOCK, 8)

    def body(h_ref, g_ref, t_ref, y_ref, gt_ref, dh_ref, loss_ref, dg_ref, dy_ref, dgt_ref, dsum_ref):
        i = pl.program_id(0)

        @pl.when(i == 0)
        def _():
            for ref in (loss_ref, dg_ref, dgt_ref, dsum_ref):
                ref[...] = jnp.zeros_like(ref)

        hv = h_ref[...]
        r = lax.rsqrt(jnp.mean(hv * hv, axis=-1, keepdims=True) + EPS)
        n = hv * r
        diff = n * g_ref[...] - t_ref[...]
        loss_ref[...] += jnp.sum(diff * diff)
        dout = diff * (1.0 / Dm)
        dg_ref[...] += jnp.sum(dout * n, axis=0, keepdims=True)
        dn = dout * g_ref[...]
        dh = r * (dn - n * jnp.mean(dn * n, axis=-1, keepdims=True))
        dh_ref[...] = dh
        _gate_grads(dh, y_ref, gt_ref, dy_ref, dgt_ref, dsum_ref)

    vec = jax.ShapeDtypeStruct((1, Dm), F32)
    return pl.pallas_call(
        body, name=name, grid=(R // tm,),
        out_shape=[jax.ShapeDtypeStruct((R, Dm), F32), jax.ShapeDtypeStruct((1, LANES), F32), vec,
                   jax.ShapeDtypeStruct((R, Dm), BF16), vec, vec],
        in_specs=[_row_spec(tm, Dm), _vec_spec(Dm), _row_spec(tm, Dm), _row_spec(tm, Dm), _vec_spec(Dm)],
        out_specs=[_row_spec(tm, Dm), _vec_spec(LANES), _vec_spec(Dm), _row_spec(tm, Dm), _vec_spec(Dm),
                   _vec_spec(Dm)],
        compiler_params=pltpu.CompilerParams(dimension_semantics=("arbitrary",)),
    )(h, g, target, y, gt)


MOD_ROWS = 16


def _mod_fwd(c_rows, w_mod, name):
    L, Dm, n = w_mod.shape

    def body(c_ref, w_ref, o_ref):
        o_ref[0] = _dot3(_silu(c_ref[...]), w_ref[0], NN)

    return pl.pallas_call(
        body, name=name, grid=(L,),
        out_shape=jax.ShapeDtypeStruct((L, MOD_ROWS, n), F32),
        in_specs=[pl.BlockSpec((MOD_ROWS, Dm), lambda l: (0, 0)), pl.BlockSpec((1, Dm, n), lambda l: (l, 0, 0))],
        out_specs=pl.BlockSpec((1, MOD_ROWS, n), lambda l: (l, 0, 0)),
        compiler_params=pltpu.CompilerParams(dimension_semantics=("parallel",)),
    )(c_rows, w_mod)


def _mod_bwd(c_rows_t, dmod, w_mod, name):
    L, Dm, n = w_mod.shape

    def body(ct_ref, d_ref, w_ref, gw_ref, ds_ref):
        dm = d_ref[0]
        gw_ref[0] = _dot3(_silu(ct_ref[...]), dm, NN)
        ds_ref[0] = _dot3(dm[:MOD_ROWS], w_ref[0], NT)

    return pl.pallas_call(
        body, name=name, grid=(L,),
        out_shape=[jax.ShapeDtypeStruct((L, Dm, n), F32), jax.ShapeDtypeStruct((L, MOD_ROWS, Dm), F32)],
        in_specs=[pl.BlockSpec((Dm, LANES), lambda l: (0, 0)), pl.BlockSpec((1, LANES, n), lambda l: (l, 0, 0)),
                  pl.BlockSpec((1, Dm, n), lambda l: (l, 0, 0))],
        out_specs=[pl.BlockSpec((1, Dm, n), lambda l: (l, 0, 0)),
                   pl.BlockSpec((1, MOD_ROWS, Dm), lambda l: (l, 0, 0))],
        compiler_params=pltpu.CompilerParams(dimension_semantics=("parallel",)),
    )(c_rows_t, dmod, w_mod)


def _adam_update(w, g, m, v):
    c1 = 1.0 - ADAM_B1 ** ADAM_STEP
    c2 = 1.0 - ADAM_B2 ** ADAM_STEP
    mn = ADAM_B1 * m + (1.0 - ADAM_B1) * g
    vn = ADAM_B2 * v + (1.0 - ADAM_B2) * (g * g)
    return -ADAM_LR * ((mn / c1) / (jnp.sqrt(vn / c2) + ADAM_EPS) + ADAM_WD * w), mn, vn


def _adamw(w, g, m, v, name):
    R, Cw = w.shape
    tm = _tile(R, ADAM_ROWS, 8)

    def body(w_ref, g_ref, m_ref, v_ref, d_ref, mo_ref, vo_ref):
        d_ref[...], mo_ref[...], vo_ref[...] = _adam_update(w_ref[...], g_ref[...], m_ref[...], v_ref[...])

    spec = pl.BlockSpec((tm, Cw), lambda i: (i, 0))
    return pl.pallas_call(
        body, name=name, grid=(R // tm,),
        out_shape=[jax.ShapeDtypeStruct((R, Cw), F32)] * 3,
        in_specs=[spec] * 4, out_specs=[spec] * 3,
        compiler_params=pltpu.CompilerParams(dimension_semantics=("parallel",)),
    )(w, g, m, v)


def _adamw_recv(w, m, v, recvs, name):
    L, R, n = w.shape
    tm = _tile(R, ADAM_ROWS, 8)
    nblk = R // tm
    parts = [r.reshape(N_DEV, R, n) for r in recvs]

    def body(*refs):
        w_ref, m_ref, v_ref = refs[:3]
        part_refs = refs[3:3 + L]
        g_ref, d_ref, mo_ref, vo_ref, gsum = refs[3 + L:]
        l = pl.program_id(0)
        for ll in range(L):
            @pl.when(l == ll)
            def _(ll=ll):
                acc = part_refs[ll][0].astype(F32)
                for s in range(1, N_DEV):
                    acc = acc + part_refs[ll][s].astype(F32)
                gsum[...] = acc
        g = gsum[...]
        g_ref[0] = g
        d_ref[0], mo_ref[0], vo_ref[0] = _adam_update(w_ref[0], g, m_ref[0], v_ref[0])

    def part_spec(ll):
        return pl.BlockSpec((N_DEV, tm, n), lambda l, i: (0, jnp.where(l == ll, i, jnp.where(l < ll, 0, nblk - 1)), 0))

    spec = pl.BlockSpec((1, tm, n), lambda l, i: (l, i, 0))
    return pl.pallas_call(
        body, name=name, grid=(L, nblk),
        out_shape=[jax.ShapeDtypeStruct((L, R, n), F32)] * 4,
        in_specs=[spec] * 3 + [part_spec(ll) for ll in range(L)], out_specs=[spec] * 4,
        scratch_shapes=[pltpu.VMEM((tm, n), F32)],
        compiler_params=pltpu.CompilerParams(dimension_semantics=("parallel", "parallel")),
    )(w, m, v, *parts)


def _pack(parts, row_mult=8):
    flat, offs, pos = [], [], 0
    for t in parts:
        t = t.reshape(-1).astype(F32)
        size = -(-t.shape[0] // LANES) * LANES
        flat.append(jnp.pad(t, (0, size - t.shape[0])))
        offs.append(pos)
        pos += size
    total = -(-pos // (LANES * row_mult)) * (LANES * row_mult)
    if total > pos:
        flat.append(jnp.zeros((total - pos,), F32))
    return jnp.concatenate(flat).reshape(-1, LANES), offs


def _take(buf, off, shape):
    size = math.prod(shape)
    return buf[..., off:off + size].reshape(buf.shape[:-1] + tuple(shape))


def _rope_tables(T, ctx_rows):
    pos = jnp.arange(T)
    row = (pos // GRID_W).astype(F32)
    col = (pos % GRID_W).astype(F32)
    half = HEAD_DIM // 4
    inv = ROPE_THETA ** (-jnp.arange(0, 2 * half, 2, dtype=F32) / (2 * half))
    ang_r, ang_c = row[:, None] * inv[None, :], col[:, None] * inv[None, :]
    cos = jnp.concatenate([jnp.cos(ang_r)] * 2 + [jnp.cos(ang_c)] * 2, axis=1)
    sin = jnp.concatenate([-jnp.sin(ang_r), jnp.sin(ang_r), -jnp.sin(ang_c), jnp.sin(ang_c)], axis=1)
    cos = jnp.concatenate([jnp.ones((ctx_rows, HEAD_DIM), F32), cos], axis=0)
    sin = jnp.concatenate([jnp.zeros((ctx_rows, HEAD_DIM), F32), sin], axis=0)
    return jnp.tile(cos, (1, 2)), jnp.tile(sin, (1, 2))


def kernel(x, c, ctx, c_ctx, w_mod, b_mod, g_mix, g_ffn, w_ffn_in, w_ffn_out, w_in, q_gain, k_gain, w_sp, b_sp, w_out, w_pw1, b_pw1, w_dw, b_dw, ln_g, ln_b, w_pw2, b_pw2, g_final, loss_target, m_c_ctx, m_w_mod, m_b_mod, m_g_mix, m_g_ffn, m_w_ffn_in, m_w_ffn_out, m_w_in, m_q_gain, m_k_gain, m_w_sp, m_b_sp, m_w_out, m_w_pw1, m_b_pw1, m_w_dw, m_b_dw, m_ln_g, m_ln_b, m_w_pw2, m_b_pw2, m_g_final, v_c_ctx, v_w_mod, v_b_mod, v_g_mix, v_g_ffn, v_w_ffn_in, v_w_ffn_out, v_w_in, v_q_gain, v_k_gain, v_w_sp, v_b_sp, v_w_out, v_w_pw1, v_b_pw1, v_w_dw, v_b_dw, v_ln_g, v_ln_b, v_w_pw2, v_b_pw2, v_g_final):
    weights = dict(c_ctx=c_ctx, w_mod=w_mod, b_mod=b_mod, g_mix=g_mix, g_ffn=g_ffn, w_ffn_in=w_ffn_in,
                   w_ffn_out=w_ffn_out, w_in=w_in, q_gain=q_gain, k_gain=k_gain, w_sp=w_sp, b_sp=b_sp,
                   w_out=w_out, w_pw1=w_pw1, b_pw1=b_pw1, w_dw=w_dw, b_dw=b_dw, ln_g=ln_g, ln_b=ln_b,
                   w_pw2=w_pw2, b_pw2=b_pw2, g_final=g_final)
    moments_m = dict(c_ctx=m_c_ctx, w_mod=m_w_mod, b_mod=m_b_mod, g_mix=m_g_mix, g_ffn=m_g_ffn,
                     w_ffn_in=m_w_ffn_in, w_ffn_out=m_w_ffn_out, w_in=m_w_in, q_gain=m_q_gain,
                     k_gain=m_k_gain, w_sp=m_w_sp, b_sp=m_b_sp, w_out=m_w_out, w_pw1=m_w_pw1,
                     b_pw1=m_b_pw1, w_dw=m_w_dw, b_dw=m_b_dw, ln_g=m_ln_g, ln_b=m_ln_b, w_pw2=m_w_pw2,
                     b_pw2=m_b_pw2, g_final=m_g_final)
    moments_v = dict(c_ctx=v_c_ctx, w_mod=v_w_mod, b_mod=v_b_mod, g_mix=v_g_mix, g_ffn=v_g_ffn,
                     w_ffn_in=v_w_ffn_in, w_ffn_out=v_w_ffn_out, w_in=v_w_in, q_gain=v_q_gain,
                     k_gain=v_k_gain, w_sp=v_w_sp, b_sp=v_b_sp, w_out=v_w_out, w_pw1=v_w_pw1,
                     b_pw1=v_b_pw1, w_dw=v_w_dw, b_dw=v_b_dw, ln_g=v_ln_g, ln_b=v_ln_b, w_pw2=v_w_pw2,
                     b_pw2=v_b_pw2, g_final=v_g_final)
    names = list(weights)

    T, C = x.shape[1], ctx.shape[1]
    Dm = D_MODEL
    me = 4 * lax.axis_index("x") + 2 * lax.axis_index("y") + lax.axis_index("c")
    h0 = x[0]
    ctx2 = ctx[0]
    target = loss_target[0]

    small_sharded = (("w_dw", w_dw[0]), ("b_pw1", b_pw1), ("b_dw", b_dw), ("ln_g", ln_g), ("ln_b", ln_b),
                     ("b_pw2", b_pw2))
    buf1, offs1 = _pack([c] + [t for _, t in small_sharded])
    w_in_t, m_w_in_t, v_w_in_t = (jnp.swapaxes(t, 1, 2) for t in (w_in, m_w_in, v_w_in))
    w_ffi_t, m_w_ffi_t, v_w_ffi_t = (jnp.swapaxes(t, 1, 2) for t in (w_ffn_in, m_w_ffn_in, v_w_ffn_in))
    got1, W_in_t = _all_gather([buf1, w_in_t[0].astype(BF16)], "gather_cond", False)
    got1 = got1.reshape(N_DEV, -1)
    c_all = _take(got1, offs1[0], (Dm,))
    full_small = {}
    for (nm, t), off in zip(small_sharded, offs1[1:]):
        seg = _take(got1, off, t.shape)
        full_small[nm] = jnp.moveaxis(seg, 0, -2).reshape(t.shape[:-1] + (N_DEV * t.shape[-1],))
    w_dw_f, b_pw1_f = full_small["w_dw"], full_small["b_pw1"]
    b_dw_f, ln_g_f, ln_b_f, b_pw2_f = (full_small[k] for k in ("b_dw", "ln_g", "ln_b", "b_pw2"))

    c_rows = jnp.concatenate([c_all, c_ctx[None, :], jnp.zeros((MOD_ROWS - N_DEV - 1, Dm), F32)], axis=0)
    mod_part = _mod_fwd(c_rows, w_mod, "mod_fwd")
    n_mod = w_mod.shape[2]
    got2 = _all_gather([mod_part.reshape(-1, LANES)], "gather_mod", True)[0]
    mod_all = got2.reshape(N_DEV, 2, MOD_ROWS, n_mod).transpose(1, 2, 0, 3).reshape(2, MOD_ROWS, N_DEV * n_mod)
    mod_all = mod_all + b_mod[:, None, :]
    my_mod = lax.dynamic_index_in_dim(mod_all, me, axis=1, keepdims=False)
    sh1, sc1, gt1, sh2, sc2, gt2 = ([my_mod[l:l + 1, k * Dm:(k + 1) * Dm] for l in range(2)] for k in range(6))
    csh1 = mod_all[0, N_DEV:N_DEV + 1, 0:Dm]
    csc1 = mod_all[0, N_DEV:N_DEV + 1, Dm:2 * Dm]

    behind = got2[0:1, 0:1] * 0.0
    gather_groups = [[w_out[0]], [w_ffi_t[0], w_ffn_out[0]], [w_pw1[0], w_pw2[0]], [w_ffi_t[1], w_ffn_out[1]]]
    gathers = [_push_begin([(t + behind).astype(BF16) for t in grp], True, f"gather_start{k}")
               for k, grp in enumerate(gather_groups)]
    started = sum(h[4][0:1, 0:1] for h in gathers)

    def gathered(k, after):
        return _push_end(gathers[k], after, f"gather_wait{k}")[1]

    def ffn_weights(k, after):
        wi, wo = gathered(k, after)
        return wi.reshape(N_DEV, FF_SHARD, Dm), wo.reshape(N_DEV // 2, FF_SHARD, Dm)

    def col_gathered(t, n):
        return t.reshape(N_DEV, Dm, n).transpose(1, 0, 2).reshape(Dm, N_DEV * n)

    W_ffi, W_ffo = [None, None], [None, None]

    g_mix_r = [g_mix[l:l + 1] for l in range(2)]
    g_ffn_r = [g_ffn[l:l + 1] for l in range(2)]
    g_fin = g_final[None, :]

    cos, sin = _rope_tables(T, C)
    qg = jnp.tile(q_gain, (1, 2))
    kg = jnp.tile(k_gain, (1, 2))
    lane_head = jnp.arange(LANES) // HEAD_DIM
    bd = (lane_head[:, None] == lane_head[None, :]).astype(BF16)
    w_sp0 = w_sp[0]
    w_spt0 = w_sp0.transpose(0, 2, 1)
    b_spt0 = b_sp[0].T

    XM = _norm_mod_fwd_cat(ctx2, h0, g_mix_r[0], csc1, csh1, sc1[0] + started, sh1[0], "norm_mix0")
    P = _mm(XM, W_in_t, "nt", "in_proj", tm=1088, tn=IN_WIDTH)
    qh, kpad, vpad, kt, ao = _mix_prep_fwd(P, C, cos, sin, qg, kg, bd, w_sp0, b_spt0, "mix_prep")
    ao, lse = _attn_fwd(qh, kpad, vpad, ao, C, "attn_fwd")
    W_out, = gathered(0, ao)
    h1, y0, xf0 = _mm(ao, W_out, "nn", "out_proj", res=h0, gate=gt1[0], raw_out=True,
                      norm=(g_ffn_r[0], sc2[0], sh2[0]))

    def ffn_fwd(h_in, xf, l, norm_next):
        W_ffi[l], W_ffo[l] = ffn_weights(1 + 2 * l, xf)
        gu, act = _ffn_in_swiglu(xf, W_ffi[l], f"ffn_in{l}")
        outs = _mm_sum_shards(act, W_ffo[l], "nn", f"ffn_out{l}", res=h_in, gate=gt2[l], raw_out=True,
                              norm=norm_next)
        return tuple(outs) + (None,) * (3 - len(outs)) + (gu, act)

    h2, f0, xm1, gu0, act0 = ffn_fwd(h1, xf0, 0, (g_mix_r[1], sc1[1], sh1[1]))

    W_pw1, W_pw2 = gathered(2, xm1)
    W_pw1 = col_gathered(W_pw1, 2 * Dm // N_DEV)
    ag = _mm(xm1, W_pw1, "nn", "pw1", BF16, bias=b_pw1_f)
    hg = _glu_fwd(ag, "glu")
    hd = _conv_fwd(hg, w_dw_f, b_dw_f, "conv")
    hs = _ln_silu_fwd(hd, ln_g_f, ln_b_f, "ln_silu")
    h3, y1, xf1 = _mm(hs, W_pw2, "nn", "pw2", bias=b_pw2_f, res=h2, gate=gt1[1], raw_out=True,
                      norm=(g_ffn_r[1], sc2[1], sh2[1]))
    h4, f1, _, gu1, act1 = ffn_fwd(h3, xf1, 1, None)

    dh4, sq_err, dg_final, df1, dgt2_1, _ = _final_fwd_bwd(h4, g_fin, target, f1, gt2[1], "loss_head")
    loss_local = (0.5 / Dm) * sq_err[0, 0:1]

    def col_shards(g, n):
        return g.reshape(Dm, N_DEV, n).transpose(1, 0, 2).reshape(N_DEV * Dm, n)

    def exchange_begin(k, parts):
        return _push_begin(parts, False, f"exchange_start{k}")

    def zero_of(handle):
        return handle[4][0:1, 0:1]

    def ffn_bwd(df, xf, gu, act, l):
        dw_out = _mm_tn_shard_rows(act, df, f"ffn_out_dw{l}", BF16)
        dgu = _ffn_out_dx_swiglu(df, W_ffo[l], gu, f"ffn_out_dx{l}").reshape(N_DEV, T, FF_SHARD)
        dw_in = _mm_tn_shard_rows(dgu, xf, f"ffn_in_dw{l}", BF16)
        dxf = _mm_sum_shards(dgu, W_ffi[l], "nn", f"ffn_in_dx{l}", BF16, tm=512)
        return dw_in, dw_out, dxf

    dW_ffi1, dW_ffo1, dxf1 = ffn_bwd(df1, xf1, gu1, act1, 1)
    ex0 = exchange_begin(0, [dW_ffi1.reshape(2 * D_FF, Dm), dW_ffo1.reshape(D_FF, Dm)])
    dh3, da, dsh, dy1, dgt1_1, db_pw2 = _norm_mod_bwd(h3, g_ffn_r[1], sc2[1], dxf1, dh4, "norm_ffn_bwd1",
                                                       gate=(y1, gt1[1] + zero_of(ex0)))
    dmod_ffn1 = (dsh, da * g_ffn_r[1], dgt2_1)
    dg_ffn1 = da * (1.0 + sc2[1])

    dW_pw2 = _mm(hs, dy1, "tn", "pw2_dw", BF16, tk=2048)
    dhs = _mm(dy1, W_pw2, "nt", "pw2_dx", BF16)
    dhd, dln_g, dln_b, db_dw = _ln_silu_bwd(dhs, hd, ln_g_f, ln_b_f, "ln_silu_bwd")
    dhg, dw_dw = _conv_bwd(dhd, hg, w_dw_f, "conv_bwd")
    dag, db_pw1 = _glu_bwd(ag, dhg, "glu_bwd")
    dW_pw1 = _mm(xm1, dag, "tn", "pw1_dw", BF16, tk=2048)
    dxm1 = _mm(dag, W_pw1, "nt", "pw1_dx", BF16, tk=2048)
    ex1 = exchange_begin(1, [col_shards(dW_pw1, 2 * Dm // N_DEV), dW_pw2])
    dh2, da, dsh, df0, dgt2_0, _ = _norm_mod_bwd(h2, g_mix_r[1], sc1[1], dxm1, dh3, "norm_mix1_bwd",
                                                 gate=(f0, gt2[0] + zero_of(ex1)))
    dmod_mix1 = (dsh, da * g_mix_r[1], dgt1_1)
    dg_mix1 = da * (1.0 + sc1[1])

    dW_ffi0, dW_ffo0, dxf0 = ffn_bwd(df0, xf0, gu0, act0, 0)
    ex2 = exchange_begin(2, [dW_ffi0.reshape(2 * D_FF, Dm), dW_ffo0.reshape(D_FF, Dm)])
    dh1, da, dsh, dy0, dgt1_0, _ = _norm_mod_bwd(h1, g_ffn_r[0], sc2[0], dxf0, dh2, "norm_ffn_bwd0",
                                                 gate=(y0, gt1[0] + zero_of(ex2)))
    dmod_ffn0 = (dsh, da * g_ffn_r[0], dgt2_0)
    dg_ffn0 = da * (1.0 + sc2[0])

    dW_out = _mm(ao, dy0, "tn", "out_proj_dw", BF16, tk=2048)
    ex_out = exchange_begin(4, [dW_out])
    dao = _mm(dy0, W_out + zero_of(ex_out).astype(BF16), "nt", "out_proj_dx", BF16)
    dq, f_acc = _attn_bwd(qh, dao, ao, lse, kpad, vpad, kt, C, "attn_bwd")
    dP, dqg, dkg, dw_sp0, db_spt0 = _mix_prep_bwd(P, dq, f_acc, dao, C, cos, sin, qg, kg, bd, w_sp0, w_spt0,
                                                  b_spt0, "mix_prep_bwd")
    dW_in_t = _mm(dP, XM, "tn", "in_proj_dw", BF16, tm=896, tk=2176)
    dXM = _mm(dP, W_in_t, "nn", "in_proj_dx", BF16, tm=1088, tk=IN_WIDTH)
    dh0, da, dsh = _norm_mod_bwd(h0, g_mix_r[0], sc1[0], dXM, dh1, "norm_mix0_bwd", dxm_row_off=C)
    _, dac, dcsh = _norm_mod_bwd(ctx2, g_mix_r[0], csc1, dXM, None, "norm_ctx_bwd")
    dmod_mix0 = (dsh, da * g_mix_r[0], dgt1_0)
    dg_mix0 = da * (1.0 + sc1[0]) + dac * (1.0 + csc1)
    dcmod = jnp.concatenate([dcsh, dac * g_mix_r[0]], axis=1)

    dmod_mine = jnp.stack([jnp.concatenate(dmod_mix0 + dmod_ffn0, axis=1)[0],
                           jnp.concatenate(dmod_mix1 + dmod_ffn1, axis=1)[0]])

    small_grads = [
        ("loss", loss_local), ("g_final", dg_final), ("g_mix", jnp.concatenate([dg_mix0, dg_mix1])),
        ("g_ffn", jnp.concatenate([dg_ffn0, dg_ffn1])),
        ("q_gain", dqg[:, :HEAD_DIM] + dqg[:, HEAD_DIM:]), ("k_gain", dkg[:, :HEAD_DIM] + dkg[:, HEAD_DIM:]),
        ("w_sp", dw_sp0[None]), ("b_sp", db_spt0.T[None]), ("b_pw1", db_pw1), ("w_dw", dw_dw[None]),
        ("b_dw", db_dw), ("ln_g", dln_g), ("ln_b", dln_b), ("b_pw2", db_pw2), ("dcmod", dcmod),
        ("dmod", dmod_mine),
    ]
    buf3, offs3 = _pack([t for _, t in small_grads])
    got3 = _all_gather([buf3], "gather_small_grads", True)[0].reshape(N_DEV, buf3.shape[0], LANES)
    sum3 = _sum_devices(got3, "sum_small_grads").reshape(-1)
    off3 = {nm: off for (nm, _), off in zip(small_grads, offs3)}
    shape3 = {nm: t.shape for nm, t in small_grads}

    def summed(nm):
        return _take(sum3, off3[nm], shape3[nm])

    loss = summed("loss")[0]
    dcmod_sum = summed("dcmod")
    dmod_rows = _take(got3.reshape(N_DEV, -1), off3["dmod"], (2, 6 * Dm)).transpose(1, 0, 2)
    ctx_row = jnp.concatenate([jnp.pad(dcmod_sum, ((0, 0), (0, 4 * Dm))), jnp.zeros((1, 6 * Dm), F32)])
    dmod_all = jnp.concatenate([dmod_rows, ctx_row[:, None, :],
                                jnp.zeros((2, LANES - N_DEV - 1, 6 * Dm), F32)], axis=1)
    grads = {}
    grads["b_mod"] = summed("dmod") + ctx_row
    dmod_shard = lax.dynamic_slice_in_dim(dmod_all, me * n_mod, n_mod, axis=2)
    c_rows_t = jnp.pad(c_rows.T, ((0, 0), (0, LANES - MOD_ROWS)))
    grads["w_mod"], ds_part = _mod_bwd(c_rows_t, dmod_shard, w_mod, "mod_bwd")

    buf4, _ = _pack([ds_part[0, N_DEV]])
    got4 = _all_gather([buf4], "gather_c_ctx_grad", True)[0].reshape(N_DEV, buf4.shape[0], LANES)
    ds_ctx = _sum_devices(got4, "sum_c_ctx_grad").reshape(-1)[:Dm]
    behind_small = (ds_ctx[0:1] * 0.0).astype(BF16)
    ex3 = exchange_begin(3, [dW_in_t + behind_small])
    grads["c_ctx"] = ds_ctx * _dsilu(c_ctx) + zero_of(ex3)[0]

    for nm in ("g_final", "g_mix", "g_ffn", "q_gain", "k_gain", "w_sp", "b_sp"):
        grads[nm] = summed(nm).reshape(weights[nm].shape)
    for nm in ("b_pw1", "w_dw", "b_dw", "ln_g", "ln_b", "b_pw2"):
        n_loc = weights[nm].shape[-1]
        grads[nm] = lax.dynamic_slice_in_dim(summed(nm), me * n_loc, n_loc, axis=-1).reshape(weights[nm].shape)

    delta, new_m, new_v = {}, {}, {}
    shp = w_mod.shape
    outs = _adamw(w_mod.reshape(-1, shp[-1]), grads["w_mod"].reshape(-1, shp[-1]),
                  m_w_mod.reshape(-1, shp[-1]), v_w_mod.reshape(-1, shp[-1]), "adamw_w_mod")
    delta["w_mod"], new_m["w_mod"], new_v["w_mod"] = (o.reshape(shp) for o in outs)
    big_names = ("w_mod", "w_ffn_in", "w_ffn_out", "w_in", "w_out", "w_pw1", "w_pw2")
    small_names = [nm for nm in names if nm not in big_names]
    packs = [_pack([src[nm] for nm in small_names]) for src in (weights, grads, moments_m, moments_v)]
    offs_s = packs[0][1]
    outs = _adamw(*[pk[0] for pk in packs], "adamw_small")
    for o, dst in zip(outs, (delta, new_m, new_v)):
        o = o.reshape(-1)
        for nm, off in zip(small_names, offs_s):
            dst[nm] = _take(o, off, weights[nm].shape)

    def exchanged(k, handle, after):
        return _push_end(handle, after, f"exchange_wait{k}")[1]

    def adamw_big(nm, parts, transposed=False, wmv=None):
        w3, m3, v3 = wmv if wmv is not None else (weights[nm], moments_m[nm], moments_v[nm])
        outs4 = _adamw_recv(w3, m3, v3, parts, f"adamw_{nm}")
        if transposed:
            outs4 = [jnp.swapaxes(t, 1, 2) for t in outs4]
        grads[nm], delta[nm], new_m[nm], new_v[nm] = outs4

    r_ffi1, r_ffo1 = exchanged(0, ex0, outs[0])
    r_pw1, r_pw2 = exchanged(1, ex1, outs[0])
    r_ffi0, r_ffo0 = exchanged(2, ex2, outs[0])
    r_out, = exchanged(4, ex_out, outs[0])
    adamw_big("w_ffn_in", [r_ffi0, r_ffi1], True, (w_ffi_t, m_w_ffi_t, v_w_ffi_t))
    adamw_big("w_ffn_out", [r_ffo0, r_ffo1])
    adamw_big("w_pw1", [r_pw1])
    adamw_big("w_pw2", [r_pw2])
    adamw_big("w_out", [r_out])
    r_in, = exchanged(3, ex3, delta["w_out"])
    adamw_big("w_in", [r_in], True, (w_in_t, m_w_in_t, v_w_in_t))

    return (loss, dh0[None], *[grads[n] for n in names], *[delta[n] for n in names],
            *[new_m[n] for n in names], *[new_v[n] for n in names])
```

```python
import math

import jax
import jax.numpy as jnp
from jax import lax
from jax.experimental import pallas as pl
from jax.experimental.pallas import tpu as pltpu

F32 = jnp.float32
BF16 = jnp.bfloat16
MESH = pl.DeviceIdType.MESH

N_DEV = 8
D_MODEL = 1024
EPS = 1e-6
HEAD_DIM = 64
ATTN_WIDTH = 512
KV_WIDTH = 128
SG_WIDTH = 512
N_SG_GROUPS = 4
CHUNK = 128
IN_WIDTH = 1792
D_FF = 2816
FF_SHARD = 2 * D_FF // N_DEV
CONV_WIDTH = 31
CONV_HALO = 16
GRID_W = 64
ROPE_THETA = 10000.0
LANES = 128
SUBLANES = 8
ROW_BLOCK = 512
ADAM_ROWS = 256
ADAM_LR, ADAM_B1, ADAM_B2, ADAM_EPS, ADAM_WD, ADAM_STEP = 0.001, 0.9, 0.999, 1e-08, 0.01, 10


def _tile(n, target, mult=LANES):
    best = None
    for t in range(mult, min(n, target) + 1, mult):
        if n % t == 0:
            best = t
    return best if best is not None else n


def _sigmoid(x):
    return 1.0 / (1.0 + jnp.exp(-x))


def _silu(x):
    return x * _sigmoid(x)


def _dsilu(x):
    s = _sigmoid(x)
    return s * (1.0 + x * (1.0 - s))


_GELU_K = math.sqrt(2.0 / math.pi)


def _gelu(x):
    return 0.5 * x * (1.0 + jnp.tanh(_GELU_K * (x + 0.044715 * x * x * x)))


def _gelu_and_grad(x):
    x2 = x * x
    t = jnp.tanh(_GELU_K * x * (1.0 + 0.044715 * x2))
    half = 0.5 * (1.0 + t)
    return x * half, half + 0.5 * x * (1.0 - t * t) * _GELU_K * (1.0 + 3.0 * 0.044715 * x2)


def _split_bf16(x):
    hi = x.astype(BF16)
    lo = (x - hi.astype(F32)).astype(BF16)
    return hi, lo


def _dot(a, b, dims):
    return lax.dot_general(a, b, (dims, ((), ())), preferred_element_type=F32)


def _dot3(a, b, dims):
    ah, al = _split_bf16(a)
    bh, bl = _split_bf16(b)
    return _dot(ah, bh, dims) + _dot(ah, bl, dims) + _dot(al, bh, dims)


NN = ((1,), (0,))
NT = ((1,), (1,))
TN = ((0,), (0,))


def _all_gather(xs, name, in_vmem):
    n_arr = len(xs)

    def body(*refs):
        x_refs, out_refs = refs[:n_arr], refs[n_arr:2 * n_arr]
        send_sems, recv_sems, local_sems = refs[2 * n_arr:]
        x, y, c = lax.axis_index("x"), lax.axis_index("y"), lax.axis_index("c")
        me, sibling = (x, y, c), (x, y, 1 - c)
        chips = [(1 - x, y), (x, 1 - y), (1 - x, 1 - y)]

        def rows(a, px, py, pc):
            m_per = xs[a].shape[0]
            return out_refs[a].at[pl.ds((4 * px + 2 * py + pc) * m_per, m_per), :]

        def copy(a, k, block, to, src=None):
            return pltpu.make_async_remote_copy(
                src_ref=rows(a, *block) if src is None else src,
                dst_ref=rows(a, *block),
                send_sem=send_sems.at[7 * a + k],
                recv_sem=recv_sems.at[7 * a + k],
                device_id=to,
                device_id_type=MESH,
            )

        mine, first, passed = [], [], []
        for a in range(n_arr):
            mine.append(pltpu.make_async_copy(x_refs[a], rows(a, *me), local_sems.at[a]))
            mine[-1].start()
            first.append(copy(a, 0, me, sibling, src=x_refs[a]))
            first += [copy(a, 1 + j, me, (*chip, c), src=x_refs[a]) for j, chip in enumerate(chips)]
        for cp in first:
            cp.start()
        for a in range(n_arr):
            for j, chip in enumerate(chips):
                copy(a, 1 + j, (*chip, c), me).wait_recv()
                passed.append(copy(a, 4 + j, (*chip, c), sibling))
                passed[-1].start()
        for a in range(n_arr):
            copy(a, 0, sibling, me).wait_recv()
            for j, chip in enumerate(chips):
                copy(a, 4 + j, (*chip, 1 - c), me).wait_recv()
        for cp in first + passed:
            cp.wait_send()
        for cp in mine:
            cp.wait()

    space = pltpu.VMEM if in_vmem else pl.ANY
    return pl.pallas_call(
        body,
        name=name,
        out_shape=[jax.ShapeDtypeStruct((N_DEV * t.shape[0], t.shape[1]), t.dtype) for t in xs],
        in_specs=[pl.BlockSpec(memory_space=space)] * n_arr,
        out_specs=[pl.BlockSpec(memory_space=space)] * n_arr,
        scratch_shapes=[
            pltpu.SemaphoreType.DMA((7 * n_arr,)),
            pltpu.SemaphoreType.DMA((7 * n_arr,)),
            pltpu.SemaphoreType.DMA((n_arr,)),
        ],
    )(*xs)


SMALL_SPLIT = 4


def _gather_split(buf, parts, name, in_vmem):
    rows = buf.shape[0] // parts
    got = _all_gather([buf[k * rows:(k + 1) * rows] for k in range(parts)], name, in_vmem)
    return jnp.concatenate([g.reshape(N_DEV, rows, buf.shape[1]) for g in got], axis=1)


HBM_SPEC = pl.BlockSpec(memory_space=pltpu.HBM)
SEM_SPEC = pl.BlockSpec(memory_space=pltpu.SEMAPHORE)
DATAFLOW_EFFECT = pltpu.SideEffectType.DATAFLOW_SIDE_EFFECTING


def _peers(x, y, c):
    for k in range(1, N_DEV):
        px = 1 - x if (k >> 2) & 1 else x
        py = 1 - y if (k >> 1) & 1 else y
        pc = 1 - c if k & 1 else c
        yield k - 1, (px, py, pc), 4 * px + 2 * py + pc


def _push_copies(src_refs, land_refs, send_sems, recv_sems, shapes, whole_src):
    x, y, c = lax.axis_index("x"), lax.axis_index("y"), lax.axis_index("c")
    me = 4 * x + 2 * y + c
    for a, (m_per, _) in enumerate(shapes):
        def block(ref, idx, m_per=m_per):
            return ref.at[pl.ds(idx * m_per, m_per), :]

        for k, peer, pidx in _peers(x, y, c):
            src = src_refs[a] if whole_src else block(src_refs[a], pidx)
            sems = dict(send_sem=send_sems.at[N_DEV * a + k], recv_sem=recv_sems.at[N_DEV * a + k],
                        device_id=peer, device_id_type=MESH)
            yield (pltpu.make_async_remote_copy(src_ref=src, dst_ref=block(land_refs[a], me), **sems),
                   pltpu.make_async_remote_copy(src_ref=src, dst_ref=block(land_refs[a], pidx), **sems))


def _own_copies(src_refs, land_refs, recv_sems, shapes, whole_src):
    me = 4 * lax.axis_index("x") + 2 * lax.axis_index("y") + lax.axis_index("c")
    for a, (m_per, _) in enumerate(shapes):
        mine = pl.ds(me * m_per, m_per)
        src = src_refs[a] if whole_src else src_refs[a].at[mine, :]
        yield pltpu.make_async_copy(src, land_refs[a].at[mine, :], recv_sems.at[N_DEV * a + N_DEV - 1])


def _push_begin(srcs, whole_src, name):
    n_arr = len(srcs)
    shapes = [(t.shape[0] if whole_src else t.shape[0] // N_DEV, t.shape[1]) for t in srcs]
    lands = [lax.empty((N_DEV * m, n), t.dtype) for (m, n), t in zip(shapes, srcs)]

    def body(*refs):
        src_refs, land_refs = refs[:n_arr], refs[n_arr:2 * n_arr]
        send_sems, recv_sems = refs[2 * n_arr], refs[2 * n_arr + 1]
        token = refs[-1]
        for outgoing, _ in _push_copies(src_refs, land_refs, send_sems, recv_sems, shapes, whole_src):
            outgoing.start()
        for own in _own_copies(src_refs, land_refs, recv_sems, shapes, whole_src):
            own.start()
        token[...] = jnp.zeros_like(token)

    operands = [pltpu.with_memory_space_constraint(t, pltpu.HBM) for t in list(srcs) + lands]
    outs = pl.pallas_call(
        body, name=name,
        out_shape=(pltpu.SemaphoreType.DMA((N_DEV * n_arr,)), pltpu.SemaphoreType.DMA((N_DEV * n_arr,)),
                   *[pltpu.HBM(t.shape, t.dtype) for t in operands],
                   jax.ShapeDtypeStruct((SUBLANES, LANES), F32)),
        in_specs=[HBM_SPEC] * (2 * n_arr),
        out_specs=(SEM_SPEC, SEM_SPEC, *[HBM_SPEC] * (2 * n_arr), pl.BlockSpec(memory_space=pltpu.VMEM)),
        input_output_aliases={i: 2 + i for i in range(2 * n_arr)},
        compiler_params=pltpu.CompilerParams(has_side_effects=DATAFLOW_EFFECT),
    )(*operands)
    return outs[0], outs[1], list(outs[2:2 + n_arr]), list(outs[2 + n_arr:2 + 2 * n_arr]), outs[-1], whole_src


def _push_end(handle, after, name):
    send_sems, recv_sems, srcs, lands, _, whole_src = handle
    n_arr = len(srcs)
    shapes = [(t.shape[0] // N_DEV, t.shape[1]) for t in lands]

    def body(*refs):
        src_refs, land_refs = refs[:n_arr], refs[n_arr:2 * n_arr]
        send_sems_ref, recv_sems_ref = refs[2 * n_arr], refs[2 * n_arr + 1]
        for outgoing, incoming in _push_copies(src_refs, land_refs, send_sems_ref, recv_sems_ref, shapes, whole_src):
            outgoing.wait_send()
            incoming.wait_recv()
        for own in _own_copies(src_refs, land_refs, recv_sems_ref, shapes, whole_src):
            own.wait()

    outs = pl.pallas_call(
        body, name=name,
        out_shape=tuple(pltpu.HBM(t.shape, t.dtype) for t in srcs + lands),
        in_specs=[HBM_SPEC] * (2 * n_arr) + [SEM_SPEC, SEM_SPEC, pl.BlockSpec(memory_space=pl.ANY)],
        out_specs=tuple([HBM_SPEC] * (2 * n_arr)),
        input_output_aliases={i: i for i in range(2 * n_arr)},
        compiler_params=pltpu.CompilerParams(has_side_effects=DATAFLOW_EFFECT),
    )(*srcs, *lands, send_sems, recv_sems, after)
    return list(outs[:n_arr]), list(outs[n_arr:])


def _sum_devices(r, name, rows_per_step=ADAM_ROWS):
    _, m, n = r.shape
    tm = _tile(m, rows_per_step, 8)

    def body(r_ref, o_ref):
        acc = r_ref[0].astype(F32)
        for s in range(1, N_DEV):
            acc = acc + r_ref[s].astype(F32)
        o_ref[...] = acc

    return pl.pallas_call(
        body,
        name=name,
        grid=(m // tm,),
        out_shape=jax.ShapeDtypeStruct((m, n), F32),
        in_specs=[pl.BlockSpec((N_DEV, tm, n), lambda i: (0, i, 0))],
        out_specs=pl.BlockSpec((tm, n), lambda i: (i, 0)),
        compiler_params=pltpu.CompilerParams(dimension_semantics=("parallel",)),
    )(r)


def _get(ref):
    return ref[0] if len(ref.shape) == 3 else ref[...]


def _put(ref, val):
    if len(ref.shape) == 3:
        ref[0] = val
    else:
        ref[...] = val


def _norm_mod(hv, g, sc, sh):
    r = lax.rsqrt(jnp.mean(hv * hv, axis=-1, keepdims=True) + EPS)
    return (hv * r) * g * (1.0 + sc) + sh


def _mm_call(name, a, b, a_spec, b_spec, out_sds, o_spec, grid, dims, acc_shape, bias=None,
             res=None, gate=None, raw_out=False, vec_spec=None, norm=None):
    nk = grid[2]
    operands, in_specs = [a, b], [a_spec, b_spec]
    if bias is not None:
        operands.append(bias)
        in_specs.append(vec_spec)
    if res is not None:
        operands += [res, gate]
        in_specs += [o_spec, vec_spec]
    if norm is not None:
        assert grid[1] == 1
        operands += list(norm)
        in_specs += [vec_spec] * 3
    out_shape, out_specs = [out_sds], [o_spec]
    if raw_out:
        out_shape.append(jax.ShapeDtypeStruct(out_sds.shape, BF16))
        out_specs.append(o_spec)
    if norm is not None:
        out_shape.append(jax.ShapeDtypeStruct(out_sds.shape, BF16))
        out_specs.append(o_spec)

    def body(*refs):
        it = iter(refs)
        a_ref, b_ref = next(it), next(it)
        bias_ref = next(it) if bias is not None else None
        res_ref, gate_ref = (next(it), next(it)) if res is not None else (None, None)
        norm_refs = (next(it), next(it), next(it)) if norm is not None else None
        o_ref = next(it)
        raw_ref = next(it) if raw_out else None
        xn_ref = next(it) if norm is not None else None
        acc = next(it) if nk > 1 else None
        k = pl.program_id(2)
        part = _dot(_get(a_ref).astype(BF16), _get(b_ref).astype(BF16), dims)

        def finish(y):
            if bias_ref is not None:
                y = y + bias_ref[...]
            if raw_ref is not None:
                raw_ref[...] = y.astype(BF16)
            if res_ref is not None:
                y = res_ref[...] + gate_ref[...] * y
            _put(o_ref, y.astype(out_sds.dtype))
            if xn_ref is not None:
                xn_ref[...] = _norm_mod(y, *[r[...] for r in norm_refs]).astype(BF16)

        if nk == 1:
            finish(part)
        else:
            @pl.when(k == 0)
            def _():
                acc[...] = part

            @pl.when(k > 0)
            def _():
                acc[...] += part

            @pl.when(k == nk - 1)
            def _():
                finish(acc[...])

    outs = pl.pallas_call(
        body,
        name=name,
        grid=grid,
        out_shape=out_shape,
        in_specs=in_specs,
        out_specs=out_specs,
        scratch_shapes=[pltpu.VMEM(acc_shape, F32)] if nk > 1 else [],
        compiler_params=pltpu.CompilerParams(dimension_semantics=("parallel", "parallel", "arbitrary")),
    )(*operands)
    return outs if len(outs) > 1 else outs[0]


def _mm(a, b, mode, name, out_dtype=F32, bias=None, res=None, gate=None, raw_out=False,
        tm=1024, tn=1024, tk=1024, a_row_off=0, norm=None):
    if mode == "nn":
        K, N = b.shape
        M = a.shape[0] - a_row_off
    elif mode == "nt":
        N, K = b.shape
        M = a.shape[0] - a_row_off
    else:
        (K, M), N = a.shape, b.shape[1]
    tm, tn, tk = _tile(M, tm, LANES if mode == "tn" else 2 * SUBLANES), _tile(N, tn), _tile(K, tk)
    off = a_row_off // tm
    dims = {"nn": NN, "nt": NT, "tn": TN}[mode]
    a_spec = (pl.BlockSpec((tk, tm), lambda i, j, k: (k, i)) if mode == "tn"
              else pl.BlockSpec((tm, tk), lambda i, j, k: (i + off, k)))
    b_spec = (pl.BlockSpec((tn, tk), lambda i, j, k: (j, k)) if mode == "nt"
              else pl.BlockSpec((tk, tn), lambda i, j, k: (k, j)))
    return _mm_call(name, a, b, a_spec, b_spec, jax.ShapeDtypeStruct((M, N), out_dtype),
                    pl.BlockSpec((tm, tn), lambda i, j, k: (i, j)), (M // tm, N // tn, K // tk), dims,
                    (tm, tn), bias, res, gate, raw_out, pl.BlockSpec((1, tn), lambda i, j, k: (0, j)), norm)


def _mm_sum_shards(a3, b3, mode, name, out_dtype=F32, res=None, gate=None, raw_out=False, tm=512, norm=None):
    S, M, kk = a3.shape
    N = b3.shape[2] if mode == "nn" else b3.shape[1]
    tm = _tile(M, tm)
    dims = NN if mode == "nn" else NT
    has_res = res is not None

    def body(*refs):
        it = iter(refs)
        a_ref, b_ref = next(it), next(it)
        res_ref, gate_ref = (next(it), next(it)) if has_res else (None, None)
        norm_refs = (next(it), next(it), next(it)) if norm is not None else None
        o_ref = next(it)
        raw_ref = next(it) if raw_out else None
        xn_ref = next(it) if norm is not None else None
        y = _dot(a_ref[0], b_ref[0], dims)
        for s in range(1, S):
            y = y + _dot(a_ref[s], b_ref[s], dims)
        if raw_ref is not None:
            raw_ref[...] = y.astype(BF16)
        if has_res:
            y = res_ref[...] + gate_ref[...] * y
        o_ref[...] = y.astype(out_dtype)
        if xn_ref is not None:
            xn_ref[...] = _norm_mod(y, *[r[...] for r in norm_refs]).astype(BF16)

    tile = pl.BlockSpec((tm, N), lambda i: (i, 0))
    operands = [a3, b3] + ([res, gate] if has_res else []) + (list(norm) if norm is not None else [])
    in_specs = [pl.BlockSpec((S, tm, kk), lambda i: (0, i, 0)), pl.BlockSpec(b3.shape, lambda i: (0, 0, 0))]
    in_specs += [tile, _vec_spec(N)] if has_res else []
    in_specs += [_vec_spec(N)] * 3 if norm is not None else []
    out_shape = [jax.ShapeDtypeStruct((M, N), out_dtype)] + ([jax.ShapeDtypeStruct((M, N), BF16)] if raw_out else [])
    out_shape += [jax.ShapeDtypeStruct((M, N), BF16)] if norm is not None else []
    outs = pl.pallas_call(
        body, name=name, grid=(M // tm,),
        out_shape=out_shape, in_specs=in_specs, out_specs=[tile] * len(out_shape),
        compiler_params=pltpu.CompilerParams(dimension_semantics=("parallel",)),
    )(*operands)
    return outs if len(outs) > 1 else outs[0]


def _mm_tn_shard_rows(a3, b, name, out_dtype, tn=1024, tk=4096):
    S, T, m = a3.shape
    N = b.shape[1]
    tn, tk = _tile(N, tn), _tile(T, tk)
    return _mm_call(name, a3, b, pl.BlockSpec((1, tk, m), lambda i, j, k: (i, k, 0)),
                    pl.BlockSpec((tk, tn), lambda i, j, k: (k, j)), jax.ShapeDtypeStruct((S, m, N), out_dtype),
                    pl.BlockSpec((1, m, tn), lambda i, j, k: (i, 0, j)), (S, N // tn, T // tk), TN, (m, tn))


def _row_spec(tm, width, off=0):
    return pl.BlockSpec((tm, width), lambda i: (i + off, 0))


def _vec_spec(width):
    return pl.BlockSpec((1, width), lambda i: (0, 0))


def _norm_mod_fwd_cat(hc, h, g, csc, csh, sc, sh, name):
    (C, Dm), T = hc.shape, h.shape[0]
    tm = _tile(math.gcd(C, T), ROW_BLOCK, 8)
    off = C // tm

    def body(hc_ref, h_ref, g_ref, csc_ref, csh_ref, sc_ref, sh_ref, o_ref):
        is_ctx = pl.program_id(0) < off
        hv = jnp.where(is_ctx, hc_ref[...], h_ref[...])
        scv = jnp.where(is_ctx, csc_ref[...], sc_ref[...])
        shv = jnp.where(is_ctx, csh_ref[...], sh_ref[...])
        r = lax.rsqrt(jnp.mean(hv * hv, axis=-1, keepdims=True) + EPS)
        o_ref[...] = ((hv * r) * g_ref[...] * (1.0 + scv) + shv).astype(BF16)

    return pl.pallas_call(
        body, name=name, grid=((C + T) // tm,),
        out_shape=jax.ShapeDtypeStruct((C + T, Dm), BF16),
        in_specs=[pl.BlockSpec((tm, Dm), lambda i: (jnp.minimum(i, off - 1), 0)),
                  pl.BlockSpec((tm, Dm), lambda i: (jnp.maximum(i - off, 0), 0))] + [_vec_spec(Dm)] * 5,
        out_specs=_row_spec(tm, Dm),
        compiler_params=pltpu.CompilerParams(dimension_semantics=("parallel",)),
    )(hc, h, g, csc, csh, sc, sh)


def _gate_grads(dh, y_ref, gt_ref, dy_ref, dgt_ref, dsum_ref):
    dy = dh * gt_ref[...]
    dgt_ref[...] += jnp.sum(dh * y_ref[...].astype(F32), axis=0, keepdims=True)
    dsum_ref[...] += jnp.sum(dy, axis=0, keepdims=True)
    dy_ref[...] = dy.astype(BF16)


def _norm_mod_bwd(h, g, sc, dxm, dres, name, dxm_row_off=0, gate=None):
    R, Dm = h.shape
    tm = _tile(math.gcd(R, dxm_row_off) if dxm_row_off else R, ROW_BLOCK, 8)
    off = dxm_row_off // tm
    has_res = dres is not None
    has_gate = gate is not None

    def body(*refs):
        it = iter(refs)
        h_ref, g_ref, sc_ref, dx_ref = next(it), next(it), next(it), next(it)
        dres_ref = next(it) if has_res else None
        y_ref, gt_ref = (next(it), next(it)) if has_gate else (None, None)
        dh_ref, da_ref, dsh_ref = next(it), next(it), next(it)
        gate_out = (next(it), next(it), next(it)) if has_gate else ()
        i = pl.program_id(0)

        @pl.when(i == 0)
        def _():
            for ref in (da_ref, dsh_ref) + gate_out[1:]:
                ref[...] = jnp.zeros_like(ref)

        hv = h_ref[...]
        dx = dx_ref[...].astype(F32)
        r = lax.rsqrt(jnp.mean(hv * hv, axis=-1, keepdims=True) + EPS)
        n = hv * r
        da_ref[...] += jnp.sum(dx * n, axis=0, keepdims=True)
        dsh_ref[...] += jnp.sum(dx, axis=0, keepdims=True)
        dn = dx * (g_ref[...] * (1.0 + sc_ref[...]))
        dh = r * (dn - n * jnp.mean(dn * n, axis=-1, keepdims=True))
        if has_res:
            dh = dh + dres_ref[...]
        dh_ref[...] = dh
        if has_gate:
            _gate_grads(dh, y_ref, gt_ref, *gate_out)

    operands = [h, g, sc, dxm] + ([dres] if has_res else []) + (list(gate) if has_gate else [])
    in_specs = [_row_spec(tm, Dm), _vec_spec(Dm), _vec_spec(Dm), _row_spec(tm, Dm, off)]
    in_specs += [_row_spec(tm, Dm)] if has_res else []
    in_specs += [_row_spec(tm, Dm), _vec_spec(Dm)] if has_gate else []
    vec = jax.ShapeDtypeStruct((1, Dm), F32)
    out_shape = [jax.ShapeDtypeStruct((R, Dm), F32), vec, vec]
    out_specs = [_row_spec(tm, Dm), _vec_spec(Dm), _vec_spec(Dm)]
    if has_gate:
        out_shape += [jax.ShapeDtypeStruct((R, Dm), BF16), vec, vec]
        out_specs += [_row_spec(tm, Dm), _vec_spec(Dm), _vec_spec(Dm)]
    return pl.pallas_call(
        body, name=name, grid=(R // tm,),
        out_shape=out_shape, in_specs=in_specs, out_specs=out_specs,
        compiler_params=pltpu.CompilerParams(dimension_semantics=("arbitrary",)),
    )(*operands)


def _ffn_in_swiglu(xf, w3, name, tm=1024):
    T, K = xf.shape
    S, n, _ = w3.shape
    half = S // 2
    tm = _tile(T, tm)

    def body(a_ref, wg_ref, wu_ref, gu_ref, act_ref):
        a = a_ref[...]
        g = _dot(a, wg_ref[0], NT)
        u = _dot(a, wu_ref[0], NT)
        gu_ref[0, 0] = g.astype(BF16)
        gu_ref[1, 0] = u.astype(BF16)
        act_ref[0] = (_silu(g) * u).astype(BF16)

    return pl.pallas_call(
        body, name=name, grid=(T // tm, half),
        out_shape=[jax.ShapeDtypeStruct((2, half, T, n), BF16), jax.ShapeDtypeStruct((half, T, n), BF16)],
        in_specs=[pl.BlockSpec((tm, K), lambda i, j: (i, 0)),
                  pl.BlockSpec((1, n, K), lambda i, j: (j, 0, 0)),
                  pl.BlockSpec((1, n, K), lambda i, j: (j + half, 0, 0))],
        out_specs=[pl.BlockSpec((2, 1, tm, n), lambda i, j: (0, j, i, 0)),
                   pl.BlockSpec((1, tm, n), lambda i, j: (j, i, 0))],
        compiler_params=pltpu.CompilerParams(dimension_semantics=("parallel", "parallel")),
    )(xf, w3, w3)


def _ffn_out_dx_swiglu(df, wo, gu, name, tm=1024):
    T, Dm = df.shape
    half, n, _ = wo.shape
    tm = _tile(T, tm)

    def body(df_ref, w_ref, gu_ref, o_ref):
        da = _dot(df_ref[...], w_ref[0], NT)
        g = gu_ref[0, 0].astype(F32)
        u = gu_ref[1, 0].astype(F32)
        s = _sigmoid(g)
        o_ref[0, 0] = (da * u * (s * (1.0 + g * (1.0 - s)))).astype(BF16)
        o_ref[1, 0] = (da * (g * s)).astype(BF16)

    gu_spec = pl.BlockSpec((2, 1, tm, n), lambda i, j: (0, j, i, 0))
    return pl.pallas_call(
        body, name=name, grid=(T // tm, half),
        out_shape=jax.ShapeDtypeStruct(gu.shape, BF16),
        in_specs=[pl.BlockSpec((tm, Dm), lambda i, j: (i, 0)),
                  pl.BlockSpec((1, n, Dm), lambda i, j: (j, 0, 0)), gu_spec],
        out_specs=gu_spec,
        compiler_params=pltpu.CompilerParams(dimension_semantics=("parallel", "parallel")),
    )(df, wo, gu)


def _glu_fwd(ag, name):
    R = ag.shape[0]
    tm = _tile(R, ROW_BLOCK, 8)

    def body(ag_ref, o_ref):
        o_ref[...] = ag_ref[:, :D_MODEL].astype(F32) * _sigmoid(ag_ref[:, D_MODEL:].astype(F32))

    return pl.pallas_call(
        body, name=name, grid=(R // tm,),
        out_shape=jax.ShapeDtypeStruct((R, D_MODEL), F32),
        in_specs=[_row_spec(tm, 2 * D_MODEL)],
        out_specs=_row_spec(tm, D_MODEL),
        compiler_params=pltpu.CompilerParams(dimension_semantics=("parallel",)),
    )(ag)


def _glu_bwd(ag, dhg, name):
    R = ag.shape[0]
    tm = _tile(R, ROW_BLOCK, 8)

    def body(ag_ref, dh_ref, o_ref, s_ref):
        i = pl.program_id(0)

        @pl.when(i == 0)
        def _():
            s_ref[...] = jnp.zeros_like(s_ref)

        a = ag_ref[:, :D_MODEL].astype(F32)
        s = _sigmoid(ag_ref[:, D_MODEL:].astype(F32))
        dh = dh_ref[...]
        da = dh * s
        dg = dh * a * s * (1.0 - s)
        o_ref[:, :D_MODEL] = da.astype(BF16)
        o_ref[:, D_MODEL:] = dg.astype(BF16)
        s_ref[:, :D_MODEL] += jnp.sum(da, axis=0, keepdims=True)
        s_ref[:, D_MODEL:] += jnp.sum(dg, axis=0, keepdims=True)

    return pl.pallas_call(
        body, name=name, grid=(R // tm,),
        out_shape=[jax.ShapeDtypeStruct((R, 2 * D_MODEL), BF16), jax.ShapeDtypeStruct((1, 2 * D_MODEL), F32)],
        in_specs=[_row_spec(tm, 2 * D_MODEL), _row_spec(tm, D_MODEL)],
        out_specs=[_row_spec(tm, 2 * D_MODEL), _vec_spec(2 * D_MODEL)],
        compiler_params=pltpu.CompilerParams(dimension_semantics=("arbitrary",)),
    )(ag, dhg)


def _halo_specs(tm, nblk, width):
    per = tm // CONV_HALO
    prev = pl.BlockSpec((CONV_HALO, width), lambda i: (jnp.maximum(i * per - 1, 0), 0))
    nxt = pl.BlockSpec((CONV_HALO, width), lambda i: (jnp.minimum((i + 1) * per, nblk * per - 1), 0))
    return prev, nxt


def _fill_halo(scr, prev_ref, cur_ref, next_ref, i, nblk, tm):
    scr[0:CONV_HALO, :] = jnp.where(i > 0, prev_ref[...], 0.0)
    scr[CONV_HALO:CONV_HALO + tm, :] = cur_ref[...]
    scr[CONV_HALO + tm:2 * CONV_HALO + tm, :] = jnp.where(i < nblk - 1, next_ref[...], 0.0)


CONV_ROWS = 128


CONV_REACH = (CONV_WIDTH // SUBLANES) * SUBLANES


def _windows(scr, stage, cols, tm):
    for r in range(SUBLANES):
        if r:
            stage[r] = scr[pl.ds(r, tm + CONV_REACH), cols]
        for a in range(CONV_REACH // SUBLANES + 1):
            off = SUBLANES * a + r
            if 1 <= off <= CONV_WIDTH:
                yield off, (stage[r, SUBLANES * a:SUBLANES * a + tm, :] if r
                            else scr[SUBLANES * a:SUBLANES * a + tm, cols])


def _conv_fwd(hg, w_dw, b_dw, name):
    R, Dm = hg.shape
    tm = _tile(R, CONV_ROWS, CONV_HALO)
    nblk = R // tm
    prev_spec, next_spec = _halo_specs(tm, nblk, Dm)

    def body(prev_ref, cur_ref, next_ref, w_ref, bdw_ref, hd_ref, scr, stage):
        _fill_halo(scr, prev_ref, cur_ref, next_ref, pl.program_id(0), nblk, tm)
        for cb in range(Dm // LANES):
            cols = slice(cb * LANES, (cb + 1) * LANES)
            acc = jnp.zeros((tm, LANES), F32) + bdw_ref[:, cols]
            for off, win in _windows(scr, stage, cols, tm):
                acc = acc + w_ref[off - 1:off, cols] * win
            hd_ref[:, cols] = acc

    return pl.pallas_call(
        body, name=name, grid=(nblk,),
        out_shape=jax.ShapeDtypeStruct((R, Dm), F32),
        in_specs=[prev_spec, _row_spec(tm, Dm), next_spec,
                  pl.BlockSpec((CONV_WIDTH, Dm), lambda i: (0, 0)), _vec_spec(Dm)],
        out_specs=_row_spec(tm, Dm),
        scratch_shapes=[pltpu.VMEM((tm + 2 * CONV_HALO, Dm), F32),
                        pltpu.VMEM((SUBLANES, tm + CONV_REACH, LANES), F32)],
        compiler_params=pltpu.CompilerParams(dimension_semantics=("parallel",)),
    )(hg, hg, hg, w_dw, b_dw)


def _ln_silu_fwd(hd, ln_g, ln_b, name):
    R, Dm = hd.shape
    tm = _tile(R, ROW_BLOCK, 8)

    def body(hd_ref, g_ref, b_ref, hs_ref):
        hd = hd_ref[...]
        xc = hd - jnp.mean(hd, axis=-1, keepdims=True)
        rs = lax.rsqrt(jnp.mean(xc * xc, axis=-1, keepdims=True) + EPS)
        hs_ref[...] = _silu(xc * rs * g_ref[...] + b_ref[...]).astype(BF16)

    return pl.pallas_call(
        body, name=name, grid=(R // tm,),
        out_shape=jax.ShapeDtypeStruct((R, Dm), BF16),
        in_specs=[_row_spec(tm, Dm), _vec_spec(Dm), _vec_spec(Dm)],
        out_specs=_row_spec(tm, Dm),
        compiler_params=pltpu.CompilerParams(dimension_semantics=("parallel",)),
    )(hd, ln_g, ln_b)


def _ln_silu_bwd(dhs, hd, ln_g, ln_b, name):
    R, Dm = hd.shape
    tm = _tile(R, ROW_BLOCK, 8)

    def body(dhs_ref, hd_ref, g_ref, b_ref, dhd_ref, dg_ref, db_ref, dsum_ref):
        i = pl.program_id(0)

        @pl.when(i == 0)
        def _():
            dg_ref[...] = jnp.zeros_like(dg_ref)
            db_ref[...] = jnp.zeros_like(db_ref)
            dsum_ref[...] = jnp.zeros_like(dsum_ref)

        hd = hd_ref[...]
        mu = jnp.mean(hd, axis=-1, keepdims=True)
        xc = hd - mu
        rs = lax.rsqrt(jnp.mean(xc * xc, axis=-1, keepdims=True) + EPS)
        z = xc * rs
        hl = z * g_ref[...] + b_ref[...]
        dhl = dhs_ref[...].astype(F32) * _dsilu(hl)
        dg_ref[...] += jnp.sum(dhl * z, axis=0, keepdims=True)
        db_ref[...] += jnp.sum(dhl, axis=0, keepdims=True)
        dz = dhl * g_ref[...]
        dhd = rs * (dz - jnp.mean(dz, axis=-1, keepdims=True) - z * jnp.mean(dz * z, axis=-1, keepdims=True))
        dsum_ref[...] += jnp.sum(dhd, axis=0, keepdims=True)
        dhd_ref[...] = dhd

    return pl.pallas_call(
        body, name=name, grid=(R // tm,),
        out_shape=[jax.ShapeDtypeStruct((R, Dm), F32)] + [jax.ShapeDtypeStruct((1, Dm), F32)] * 3,
        in_specs=[_row_spec(tm, Dm), _row_spec(tm, Dm), _vec_spec(Dm), _vec_spec(Dm)],
        out_specs=[_row_spec(tm, Dm), _vec_spec(Dm), _vec_spec(Dm), _vec_spec(Dm)],
        compiler_params=pltpu.CompilerParams(dimension_semantics=("arbitrary",)),
    )(dhs, hd, ln_g, ln_b)


def _conv_bwd(dhd, hg, w_dw, name):
    R, Dm = hg.shape
    tm = _tile(R, CONV_ROWS, CONV_HALO)
    nblk = R // tm
    prev_spec, next_spec = _halo_specs(tm, nblk, Dm)

    def body(dprev, dcur, dnext, gprev, gcur, gnext, w_ref, dhg_ref, dw_ref, dscr, gscr, dwp, stage):
        i = pl.program_id(0)

        @pl.when(i == 0)
        def _():
            dwp[...] = jnp.zeros_like(dwp)

        _fill_halo(dscr, dprev, dcur, dnext, i, nblk, tm)
        _fill_halo(gscr, gprev, gcur, gnext, i, nblk, tm)
        for cb in range(Dm // LANES):
            cols = slice(cb * LANES, (cb + 1) * LANES)
            acc = jnp.zeros((tm, LANES), F32)
            for off, win in _windows(dscr, stage, cols, tm):
                j = CONV_WIDTH - off
                acc = acc + w_ref[j:j + 1, cols] * win
            dhg_ref[:, cols] = acc
            d_here = dcur[:, cols]
            for off, win in _windows(gscr, stage, cols, tm):
                j = off - 1
                prod = d_here * win
                part = prod[0:SUBLANES]
                for k in range(1, tm // SUBLANES):
                    part = part + prod[k * SUBLANES:(k + 1) * SUBLANES]
                dwp[j * SUBLANES:(j + 1) * SUBLANES, cols] += part

        @pl.when(i == nblk - 1)
        def _():
            for j in range(CONV_WIDTH):
                dw_ref[j:j + 1, :] = jnp.sum(dwp[j * SUBLANES:(j + 1) * SUBLANES, :], axis=0, keepdims=True)

    return pl.pallas_call(
        body, name=name, grid=(nblk,),
        out_shape=[jax.ShapeDtypeStruct((R, Dm), F32), jax.ShapeDtypeStruct((CONV_WIDTH, Dm), F32)],
        in_specs=[prev_spec, _row_spec(tm, Dm), next_spec, prev_spec, _row_spec(tm, Dm), next_spec,
                  pl.BlockSpec((CONV_WIDTH, Dm), lambda i: (0, 0))],
        out_specs=[_row_spec(tm, Dm), pl.BlockSpec((CONV_WIDTH, Dm), lambda i: (0, 0))],
        scratch_shapes=[pltpu.VMEM((tm + 2 * CONV_HALO, Dm), F32)] * 2
        + [pltpu.VMEM((CONV_WIDTH * SUBLANES, Dm), F32), pltpu.VMEM((SUBLANES, tm + CONV_REACH, LANES), F32)],
        compiler_params=pltpu.CompilerParams(dimension_semantics=("arbitrary",)),
    )(dhd, dhd, dhd, hg, hg, hg, w_dw)


def _swap16(y, lane):
    return jnp.where((lane & 16) == 0, pltpu.roll(y, LANES - 16, 1), pltpu.roll(y, 16, 1))


def _head_mean(v, bd):
    hi, lo = _split_bf16(v)
    return (_dot(hi, bd, NN) + _dot(lo, bd, NN)) * (1.0 / HEAD_DIM)


Q_COLS = (0, ATTN_WIDTH)
K_COLS = (ATTN_WIDTH, ATTN_WIDTH + HEAD_DIM * 2)
V_COLS = (K_COLS[1], K_COLS[1] + HEAD_DIM * 2)
SU_COLS = (V_COLS[1], V_COLS[1] + SG_WIDTH)
SV_COLS = (SU_COLS[1], SU_COLS[1] + SG_WIDTH)


def _mix_prep_fwd(p, ctx_rows, cos, sin, qg, kg, bd, w_sp, b_spt, name):
    TT = p.shape[0]
    off = ctx_rows // CHUNK
    q_scale = HEAD_DIM ** -0.5

    def body(p_ref, cos_ref, sin_ref, qg_ref, kg_ref, bd_ref, w_ref, b_ref,
             q_ref, kp_ref, vp_ref, kt_ref, sg_ref):
        lane = lax.broadcasted_iota(jnp.int32, (CHUNK, LANES), 1)
        low = lane < HEAD_DIM
        cs, sn, bdv = cos_ref[...], sin_ref[...], bd_ref[...]

        def norm_rope(xv, gain):
            r = lax.rsqrt(_head_mean(xv * xv, bdv) + EPS)
            yv = xv * r * gain
            return yv * cs + _swap16(yv, lane) * sn

        def pad_heads(ref, t):
            tr = pltpu.roll(t, HEAD_DIM, 1)
            ref[0, 0] = jnp.where(low, t, 0.0).astype(BF16)
            ref[0, 1] = jnp.where(low, 0.0, tr).astype(BF16)
            ref[1, 0] = jnp.where(low, tr, 0.0).astype(BF16)
            ref[1, 1] = jnp.where(low, 0.0, t).astype(BF16)

        for a in range(ATTN_WIDTH // LANES):
            xv = p_ref[:, a * LANES:(a + 1) * LANES]
            q_ref[:, a * LANES:(a + 1) * LANES] = (norm_rope(xv, qg_ref[...]) * q_scale).astype(BF16)
        kh = norm_rope(p_ref[:, K_COLS[0]:K_COLS[1]], kg_ref[...])
        pad_heads(kp_ref, kh)
        pad_heads(vp_ref, p_ref[:, V_COLS[0]:V_COLS[1]])
        kht = kh.T
        kt_ref[0] = kht[:HEAD_DIM].astype(BF16)
        kt_ref[1] = kht[HEAD_DIM:].astype(BF16)
        for g in range(N_SG_GROUPS):
            u = _gelu(p_ref[:, SU_COLS[0] + g * LANES:SU_COLS[0] + (g + 1) * LANES])
            vg = _gelu(p_ref[:, SV_COLS[0] + g * LANES:SV_COLS[0] + (g + 1) * LANES])
            xc = vg - jnp.mean(vg, axis=-1, keepdims=True)
            vn = xc * lax.rsqrt(jnp.mean(xc * xc, axis=-1, keepdims=True) + EPS)
            mixed = _dot(w_ref[g].astype(BF16), vn.astype(BF16), NN) + b_ref[:, g:g + 1]
            sg_ref[:, g * LANES:(g + 1) * LANES] = (u * mixed).astype(BF16)

    def row(width):
        return pl.BlockSpec((CHUNK, width), lambda i: (i, 0))

    def whole(shape):
        return pl.BlockSpec(shape, lambda i: (0,) * len(shape))

    pad_spec = pl.BlockSpec((2, 2, CHUNK, LANES), lambda i: (0, 0, i, 0))
    return pl.pallas_call(
        body, name=name, grid=(TT // CHUNK,),
        out_shape=[jax.ShapeDtypeStruct((TT, ATTN_WIDTH), BF16),
                   jax.ShapeDtypeStruct((2, 2, TT, LANES), BF16), jax.ShapeDtypeStruct((2, 2, TT, LANES), BF16),
                   jax.ShapeDtypeStruct((2, HEAD_DIM, TT), BF16),
                   jax.ShapeDtypeStruct((TT - ctx_rows, ATTN_WIDTH + SG_WIDTH), BF16)],
        in_specs=[row(IN_WIDTH), row(LANES), row(LANES), whole((1, LANES)), whole((1, LANES)),
                  whole((LANES, LANES)), whole((N_SG_GROUPS, CHUNK, CHUNK)), whole((CHUNK, N_SG_GROUPS))],
        out_specs=[row(ATTN_WIDTH), pad_spec, pad_spec,
                   pl.BlockSpec((2, HEAD_DIM, CHUNK), lambda i: (0, 0, i)),
                   pl.BlockSpec((CHUNK, SG_WIDTH), lambda i: (jnp.maximum(i - off, 0), 1))],
        compiler_params=pltpu.CompilerParams(dimension_semantics=("arbitrary",)),
    )(p, cos, sin, qg, kg, bd, w_sp, b_spt)


def _mix_prep_bwd(p, dq, f, dao, ctx_rows, cos, sin, qg, kg, bd, w_sp, w_spt, b_spt, name):
    TT = p.shape[0]
    off = ctx_rows // CHUNK
    q_scale = HEAD_DIM ** -0.5

    def body(p_ref, dq_ref, f_ref, dsg_ref, cos_ref, sin_ref, qg_ref, kg_ref, bd_ref, w_ref, wt_ref,
             b_ref, dp_ref, dqg_ref, dkg_ref, dw_ref, db_ref):
        i = pl.program_id(0)

        @pl.when(i == 0)
        def _():
            dqg_ref[...] = jnp.zeros_like(dqg_ref)
            dkg_ref[...] = jnp.zeros_like(dkg_ref)
            dw_ref[...] = jnp.zeros_like(dw_ref)
            db_ref[...] = jnp.zeros_like(db_ref)

        latent = (i >= off).astype(F32)
        lane = lax.broadcasted_iota(jnp.int32, (CHUNK, LANES), 1)
        low = lane < HEAD_DIM
        cs, sn, bdv = cos_ref[...], sin_ref[...], bd_ref[...]

        def fold(b0):
            return jnp.where(low, f_ref[0, b0] + pltpu.roll(f_ref[0, b0 + 1], HEAD_DIM, 1),
                             pltpu.roll(f_ref[1, b0], HEAD_DIM, 1) + f_ref[1, b0 + 1])

        def norm_rope_bwd(xv, dout, gain):
            r = lax.rsqrt(_head_mean(xv * xv, bdv) + EPS)
            n = xv * r
            dy = dout * cs + _swap16(dout * sn, lane)
            dn = dy * gain
            dx = r * (dn - n * _head_mean(dn * n, bdv))
            return dx, jnp.sum(dy * n, axis=0, keepdims=True)

        for a in range(ATTN_WIDTH // LANES):
            cols = slice(a * LANES, (a + 1) * LANES)
            dx, dg = norm_rope_bwd(p_ref[:, cols], dq_ref[:, cols] * (latent * q_scale), qg_ref[...])
            dp_ref[:, cols] = dx.astype(BF16)
            dqg_ref[...] += dg
        dx, dg = norm_rope_bwd(p_ref[:, K_COLS[0]:K_COLS[1]], fold(0), kg_ref[...])
        dp_ref[:, K_COLS[0]:K_COLS[1]] = dx.astype(BF16)
        dkg_ref[...] += dg
        dp_ref[:, V_COLS[0]:V_COLS[1]] = fold(2).astype(BF16)
        for g in range(N_SG_GROUPS):
            su = p_ref[:, SU_COLS[0] + g * LANES:SU_COLS[0] + (g + 1) * LANES]
            sv = p_ref[:, SV_COLS[0] + g * LANES:SV_COLS[0] + (g + 1) * LANES]
            (u, dgelu_su), (vg, dgelu_sv) = _gelu_and_grad(su), _gelu_and_grad(sv)
            xc = vg - jnp.mean(vg, axis=-1, keepdims=True)
            rs = lax.rsqrt(jnp.mean(xc * xc, axis=-1, keepdims=True) + EPS)
            vn = xc * rs
            vnb = vn.astype(BF16)
            mixed = _dot(w_ref[g].astype(BF16), vnb, NN) + b_ref[:, g:g + 1]
            dsg = dsg_ref[:, g * LANES:(g + 1) * LANES].astype(F32) * latent
            du = dsg * mixed
            dmix = dsg * u
            dmb = dmix.astype(BF16)
            db_ref[:, g:g + 1] += jnp.sum(dmix, axis=-1, keepdims=True)
            dw_ref[g] += _dot(dmb, vnb, NT)
            dvn = _dot(wt_ref[g].astype(BF16), dmb, NN)
            dvg = rs * (dvn - jnp.mean(dvn, axis=-1, keepdims=True)
                        - vn * jnp.mean(dvn * vn, axis=-1, keepdims=True))
            dp_ref[:, SU_COLS[0] + g * LANES:SU_COLS[0] + (g + 1) * LANES] = (du * dgelu_su).astype(BF16)
            dp_ref[:, SV_COLS[0] + g * LANES:SV_COLS[0] + (g + 1) * LANES] = (dvg * dgelu_sv).astype(BF16)

    def row(width):
        return pl.BlockSpec((CHUNK, width), lambda i: (i, 0))

    def latent_row(width, col_block):
        return pl.BlockSpec((CHUNK, width), lambda i: (jnp.maximum(i - off, 0), col_block))

    def whole(shape):
        return pl.BlockSpec(shape, lambda i: (0,) * len(shape))

    return pl.pallas_call(
        body, name=name, grid=(TT // CHUNK,),
        out_shape=[jax.ShapeDtypeStruct((TT, IN_WIDTH), BF16), jax.ShapeDtypeStruct((1, LANES), F32),
                   jax.ShapeDtypeStruct((1, LANES), F32),
                   jax.ShapeDtypeStruct((N_SG_GROUPS, CHUNK, CHUNK), F32),
                   jax.ShapeDtypeStruct((CHUNK, N_SG_GROUPS), F32)],
        in_specs=[row(IN_WIDTH), latent_row(ATTN_WIDTH, 0),
                  pl.BlockSpec((2, 4, CHUNK, LANES), lambda i: (0, 0, i, 0)),
                  latent_row(SG_WIDTH, 1), row(LANES), row(LANES), whole((1, LANES)), whole((1, LANES)),
                  whole((LANES, LANES)), whole((N_SG_GROUPS, CHUNK, CHUNK)),
                  whole((N_SG_GROUPS, CHUNK, CHUNK)), whole((CHUNK, N_SG_GROUPS))],
        out_specs=[row(IN_WIDTH), whole((1, LANES)), whole((1, LANES)),
                   whole((N_SG_GROUPS, CHUNK, CHUNK)), whole((CHUNK, N_SG_GROUPS))],
        compiler_params=pltpu.CompilerParams(dimension_semantics=("arbitrary",)),
    )(p, dq, f, dao, cos, sin, qg, kg, bd, w_sp, w_spt, b_spt)


def _attn_fwd(q, kpad, vpad, ao, ctx_rows, name, tq=256):
    TT = q.shape[0]
    T = TT - ctx_rows
    tq = _tile(T, tq)
    off = ctx_rows // tq
    group = 2 * LANES

    def body(q_ref, k_ref, v_ref, ao_in, o_ref, lse_ref):
        del ao_in
        lane = lax.broadcasted_iota(jnp.int32, (tq, LANES), 1)
        lse = jnp.zeros((tq, LANES), F32)
        for a in range(2):
            acc = jnp.zeros((tq, LANES), F32)
            qa = q_ref[:, a * LANES:(a + 1) * LANES]
            for b in range(2):
                s = _dot(qa, k_ref[0, b], NT)
                m = jnp.max(s, axis=-1, keepdims=True)
                e = jnp.exp(s - m)
                l = jnp.sum(e, axis=-1, keepdims=True)
                acc = acc + _dot(e.astype(BF16), v_ref[0, b], NN) * (1.0 / l)
                lse = jnp.where(lane == 2 * a + b, m + jnp.log(l), lse)
            o_ref[:, a * LANES:(a + 1) * LANES] = acc.astype(BF16)
        lse_ref[0] = lse

    kv_spec = pl.BlockSpec((1, 2, TT, LANES), lambda j, i: (j, 0, 0, 0))
    return pl.pallas_call(
        body, name=name, grid=(2, T // tq),
        out_shape=[jax.ShapeDtypeStruct(ao.shape, BF16), jax.ShapeDtypeStruct((2, T, LANES), F32)],
        in_specs=[pl.BlockSpec((tq, group), lambda j, i: (i + off, j)), kv_spec, kv_spec,
                  pl.BlockSpec(memory_space=pl.ANY)],
        out_specs=[pl.BlockSpec((tq, group), lambda j, i: (i, j)),
                   pl.BlockSpec((1, tq, LANES), lambda j, i: (j, i, 0))],
        input_output_aliases={3: 0},
        compiler_params=pltpu.CompilerParams(dimension_semantics=("parallel", "parallel")),
    )(q, kpad, vpad, ao)


def _attn_bwd(q, dao, ao, lse, kpad, vpad, kt, ctx_rows, name, tq=256):
    TT = q.shape[0]
    T = TT - ctx_rows
    tq = _tile(T, tq)
    off = ctx_rows // tq
    group = 2 * LANES

    def body(q_ref, do_ref, o_ref, lse_ref, k_ref, v_ref, kt_ref, dq_ref, f_ref):
        i = pl.program_id(1)

        @pl.when(i == 0)
        def _():
            f_ref[...] = jnp.zeros_like(f_ref)

        ktv = kt_ref[0]
        lse_t = lse_ref[0].T
        row = lax.broadcasted_iota(jnp.int32, (SUBLANES, LANES), 0)
        lane = lax.broadcasted_iota(jnp.int32, (SUBLANES, LANES), 1)
        half_ones = (jnp.where(lane < HEAD_DIM, 0, 1) == row).astype(BF16)
        for a in range(2):
            cols = slice(a * LANES, (a + 1) * LANES)
            qa = q_ref[:, cols]
            do32 = do_ref[:, cols].astype(F32)
            doa = do32.astype(BF16)
            hi, lo = _split_bf16(do32 * o_ref[:, cols].astype(F32))
            deltas = _dot(half_ones, hi, NT) + _dot(half_ones, lo, NT)
            halves = []
            for b in range(2):
                h = 2 * a + b
                st = _dot(k_ref[0, b], qa, NT)
                pt = jnp.exp(st - lse_t[h:h + 1, :])
                dpt = _dot(v_ref[0, b], doa, NT)
                dst = (pt * (dpt - deltas[b:b + 1, :])).astype(BF16)
                f_ref[0, b] += _dot(dst, qa, NN)
                f_ref[0, 2 + b] += _dot(pt.astype(BF16), doa, NN)
                halves.append(_dot(ktv, dst, NN))
            dq_ref[:, cols] = jnp.concatenate(halves, axis=0).T

    kv_spec = pl.BlockSpec((1, 2, TT, LANES), lambda j, i: (j, 0, 0, 0))
    out_cols = pl.BlockSpec((tq, group), lambda j, i: (i, j))
    return pl.pallas_call(
        body, name=name, grid=(2, T // tq),
        out_shape=[jax.ShapeDtypeStruct((T, ATTN_WIDTH), F32), jax.ShapeDtypeStruct((2, 4, TT, LANES), F32)],
        in_specs=[pl.BlockSpec((tq, group), lambda j, i: (i + off, j)), out_cols, out_cols,
                  pl.BlockSpec((1, tq, LANES), lambda j, i: (j, i, 0)),
                  kv_spec, kv_spec, pl.BlockSpec((1, HEAD_DIM, TT), lambda j, i: (j, 0, 0))],
        out_specs=[out_cols, pl.BlockSpec((1, 4, TT, LANES), lambda j, i: (j, 0, 0, 0))],
        compiler_params=pltpu.CompilerParams(dimension_semantics=("parallel", "arbitrary")),
    )(q, dao, ao, lse, kpad, vpad, kt)


def _final_fwd_bwd(h, g, target, y, gt, name):
    R, Dm = h.shape
    tm = _tile(R, ROW_BLOCK, 8)

    def body(h_ref, g_ref, t_ref, y_ref, gt_ref, dh_ref, loss_ref, dg_ref, dy_ref, dgt_ref, dsum_ref):
        i = pl.program_id(0)

        @pl.when(i == 0)
        def _():
            for ref in (loss_ref, dg_ref, dgt_ref, dsum_ref):
                ref[...] = jnp.zeros_like(ref)

        hv = h_ref[...]
        r = lax.rsqrt(jnp.mean(hv * hv, axis=-1, keepdims=True) + EPS)
        n = hv * r
        diff = n * g_ref[...] - t_ref[...]
        loss_ref[...] += jnp.sum(diff * diff)
        dout = diff * (1.0 / Dm)
        dg_ref[...] += jnp.sum(dout * n, axis=0, keepdims=True)
        dn = dout * g_ref[...]
        dh = r * (dn - n * jnp.mean(dn * n, axis=-1, keepdims=True))
        dh_ref[...] = dh
        _gate_grads(dh, y_ref, gt_ref, dy_ref, dgt_ref, dsum_ref)

    vec = jax.ShapeDtypeStruct((1, Dm), F32)
    return pl.pallas_call(
        body, name=name, grid=(R // tm,),
        out_shape=[jax.ShapeDtypeStruct((R, Dm), F32), jax.ShapeDtypeStruct((1, LANES), F32), vec,
                   jax.ShapeDtypeStruct((R, Dm), BF16), vec, vec],
        in_specs=[_row_spec(tm, Dm), _vec_spec(Dm), _row_spec(tm, Dm), _row_spec(tm, Dm), _vec_spec(Dm)],
        out_specs=[_row_spec(tm, Dm), _vec_spec(LANES), _vec_spec(Dm), _row_spec(tm, Dm), _vec_spec(Dm),
                   _vec_spec(Dm)],
        compiler_params=pltpu.CompilerParams(dimension_semantics=("arbitrary",)),
    )(h, g, target, y, gt)


MOD_ROWS = 16


def _mod_fwd(c_rows, w_mod, name):
    L, Dm, n = w_mod.shape

    def body(c_ref, w_ref, o_ref):
        o_ref[0] = _dot3(_silu(c_ref[...]), w_ref[0], NN)

    return pl.pallas_call(
        body, name=name, grid=(L,),
        out_shape=jax.ShapeDtypeStruct((L, MOD_ROWS, n), F32),
        in_specs=[pl.BlockSpec((MOD_ROWS, Dm), lambda l: (0, 0)), pl.BlockSpec((1, Dm, n), lambda l: (l, 0, 0))],
        out_specs=pl.BlockSpec((1, MOD_ROWS, n), lambda l: (l, 0, 0)),
        compiler_params=pltpu.CompilerParams(dimension_semantics=("parallel",)),
    )(c_rows, w_mod)


def _mod_bwd(c_rows_t, dmod, w_mod, name):
    L, Dm, n = w_mod.shape

    def body(ct_ref, d_ref, w_ref, gw_ref, ds_ref):
        dm = d_ref[0]
        gw_ref[0] = _dot3(_silu(ct_ref[...]), dm, NN)
        ds_ref[0] = _dot3(dm[:MOD_ROWS], w_ref[0], NT)

    return pl.pallas_call(
        body, name=name, grid=(L,),
        out_shape=[jax.ShapeDtypeStruct((L, Dm, n), F32), jax.ShapeDtypeStruct((L, MOD_ROWS, Dm), F32)],
        in_specs=[pl.BlockSpec((Dm, LANES), lambda l: (0, 0)), pl.BlockSpec((1, LANES, n), lambda l: (l, 0, 0)),
                  pl.BlockSpec((1, Dm, n), lambda l: (l, 0, 0))],
        out_specs=[pl.BlockSpec((1, Dm, n), lambda l: (l, 0, 0)),
                   pl.BlockSpec((1, MOD_ROWS, Dm), lambda l: (l, 0, 0))],
        compiler_params=pltpu.CompilerParams(dimension_semantics=("parallel",)),
    )(c_rows_t, dmod, w_mod)


def _adam_update(w, g, m, v):
    c1 = 1.0 - ADAM_B1 ** ADAM_STEP
    c2 = 1.0 - ADAM_B2 ** ADAM_STEP
    mn = ADAM_B1 * m + (1.0 - ADAM_B1) * g
    vn = ADAM_B2 * v + (1.0 - ADAM_B2) * (g * g)
    return -ADAM_LR * ((mn / c1) / (jnp.sqrt(vn / c2) + ADAM_EPS) + ADAM_WD * w), mn, vn


def _adamw(w, g, m, v, name):
    R, Cw = w.shape
    tm = _tile(R, ADAM_ROWS, 8)

    def body(w_ref, g_ref, m_ref, v_ref, d_ref, mo_ref, vo_ref):
        d_ref[...], mo_ref[...], vo_ref[...] = _adam_update(w_ref[...], g_ref[...], m_ref[...], v_ref[...])

    spec = pl.BlockSpec((tm, Cw), lambda i: (i, 0))
    return pl.pallas_call(
        body, name=name, grid=(R // tm,),
        out_shape=[jax.ShapeDtypeStruct((R, Cw), F32)] * 3,
        in_specs=[spec] * 4, out_specs=[spec] * 3,
        compiler_params=pltpu.CompilerParams(dimension_semantics=("parallel",)),
    )(w, g, m, v)


def _adamw_recv(w, m, v, recvs, name):
    L, R, n = w.shape
    tm = _tile(R, ADAM_ROWS, 8)
    nblk = R // tm
    parts = [r.reshape(N_DEV, R, n) for r in recvs]

    def body(*refs):
        w_ref, m_ref, v_ref = refs[:3]
        part_refs = refs[3:3 + L]
        g_ref, d_ref, mo_ref, vo_ref, gsum = refs[3 + L:]
        l = pl.program_id(0)
        for ll in range(L):
            @pl.when(l == ll)
            def _(ll=ll):
                acc = part_refs[ll][0].astype(F32)
                for s in range(1, N_DEV):
                    acc = acc + part_refs[ll][s].astype(F32)
                gsum[...] = acc
        g = gsum[...]
        g_ref[0] = g
        d_ref[0], mo_ref[0], vo_ref[0] = _adam_update(w_ref[0], g, m_ref[0], v_ref[0])

    def part_spec(ll):
        return pl.BlockSpec((N_DEV, tm, n), lambda l, i: (0, jnp.where(l == ll, i, jnp.where(l < ll, 0, nblk - 1)), 0))

    spec = pl.BlockSpec((1, tm, n), lambda l, i: (l, i, 0))
    return pl.pallas_call(
        body, name=name, grid=(L, nblk),
        out_shape=[jax.ShapeDtypeStruct((L, R, n), F32)] * 4,
        in_specs=[spec] * 3 + [part_spec(ll) for ll in range(L)], out_specs=[spec] * 4,
        scratch_shapes=[pltpu.VMEM((tm, n), F32)],
        compiler_params=pltpu.CompilerParams(dimension_semantics=("parallel", "parallel")),
    )(w, m, v, *parts)


def _pack(parts, row_mult=8):
    flat, offs, pos = [], [], 0
    for t in parts:
        t = t.reshape(-1).astype(F32)
        size = -(-t.shape[0] // LANES) * LANES
        flat.append(jnp.pad(t, (0, size - t.shape[0])))
        offs.append(pos)
        pos += size
    total = -(-pos // (LANES * row_mult)) * (LANES * row_mult)
    if total > pos:
        flat.append(jnp.zeros((total - pos,), F32))
    return jnp.concatenate(flat).reshape(-1, LANES), offs


def _take(buf, off, shape):
    size = math.prod(shape)
    return buf[..., off:off + size].reshape(buf.shape[:-1] + tuple(shape))


def _rope_tables(T, ctx_rows):
    pos = jnp.arange(T)
    row = (pos // GRID_W).astype(F32)
    col = (pos % GRID_W).astype(F32)
    half = HEAD_DIM // 4
    inv = ROPE_THETA ** (-jnp.arange(0, 2 * half, 2, dtype=F32) / (2 * half))
    ang_r, ang_c = row[:, None] * inv[None, :], col[:, None] * inv[None, :]
    cos = jnp.concatenate([jnp.cos(ang_r)] * 2 + [jnp.cos(ang_c)] * 2, axis=1)
    sin = jnp.concatenate([-jnp.sin(ang_r), jnp.sin(ang_r), -jnp.sin(ang_c), jnp.sin(ang_c)], axis=1)
    cos = jnp.concatenate([jnp.ones((ctx_rows, HEAD_DIM), F32), cos], axis=0)
    sin = jnp.concatenate([jnp.zeros((ctx_rows, HEAD_DIM), F32), sin], axis=0)
    return jnp.tile(cos, (1, 2)), jnp.tile(sin, (1, 2))


def kernel(x, c, ctx, c_ctx, w_mod, b_mod, g_mix, g_ffn, w_ffn_in, w_ffn_out, w_in, q_gain, k_gain, w_sp, b_sp, w_out, w_pw1, b_pw1, w_dw, b_dw, ln_g, ln_b, w_pw2, b_pw2, g_final, loss_target, m_c_ctx, m_w_mod, m_b_mod, m_g_mix, m_g_ffn, m_w_ffn_in, m_w_ffn_out, m_w_in, m_q_gain, m_k_gain, m_w_sp, m_b_sp, m_w_out, m_w_pw1, m_b_pw1, m_w_dw, m_b_dw, m_ln_g, m_ln_b, m_w_pw2, m_b_pw2, m_g_final, v_c_ctx, v_w_mod, v_b_mod, v_g_mix, v_g_ffn, v_w_ffn_in, v_w_ffn_out, v_w_in, v_q_gain, v_k_gain, v_w_sp, v_b_sp, v_w_out, v_w_pw1, v_b_pw1, v_w_dw, v_b_dw, v_ln_g, v_ln_b, v_w_pw2, v_b_pw2, v_g_final):
    weights = dict(c_ctx=c_ctx, w_mod=w_mod, b_mod=b_mod, g_mix=g_mix, g_ffn=g_ffn, w_ffn_in=w_ffn_in,
                   w_ffn_out=w_ffn_out, w_in=w_in, q_gain=q_gain, k_gain=k_gain, w_sp=w_sp, b_sp=b_sp,
                   w_out=w_out, w_pw1=w_pw1, b_pw1=b_pw1, w_dw=w_dw, b_dw=b_dw, ln_g=ln_g, ln_b=ln_b,
                   w_pw2=w_pw2, b_pw2=b_pw2, g_final=g_final)
    moments_m = dict(c_ctx=m_c_ctx, w_mod=m_w_mod, b_mod=m_b_mod, g_mix=m_g_mix, g_ffn=m_g_ffn,
                     w_ffn_in=m_w_ffn_in, w_ffn_out=m_w_ffn_out, w_in=m_w_in, q_gain=m_q_gain,
                     k_gain=m_k_gain, w_sp=m_w_sp, b_sp=m_b_sp, w_out=m_w_out, w_pw1=m_w_pw1,
                     b_pw1=m_b_pw1, w_dw=m_w_dw, b_dw=m_b_dw, ln_g=m_ln_g, ln_b=m_ln_b, w_pw2=m_w_pw2,
                     b_pw2=m_b_pw2, g_final=m_g_final)
    moments_v = dict(c_ctx=v_c_ctx, w_mod=v_w_mod, b_mod=v_b_mod, g_mix=v_g_mix, g_ffn=v_g_ffn,
                     w_ffn_in=v_w_ffn_in, w_ffn_out=v_w_ffn_out, w_in=v_w_in, q_gain=v_q_gain,
                     k_gain=v_k_gain, w_sp=v_w_sp, b_sp=v_b_sp, w_out=v_w_out, w_pw1=v_w_pw1,
                     b_pw1=v_b_pw1, w_dw=v_w_dw, b_dw=v_b_dw, ln_g=v_ln_g, ln_b=v_ln_b, w_pw2=v_w_pw2,
                     b_pw2=v_b_pw2, g_final=v_g_final)
    names = list(weights)

    T, C = x.shape[1], ctx.shape[1]
    Dm = D_MODEL
    me = 4 * lax.axis_index("x") + 2 * lax.axis_index("y") + lax.axis_index("c")
    h0 = x[0]
    ctx2 = ctx[0]
    target = loss_target[0]

    small_sharded = (("w_dw", w_dw[0]), ("b_pw1", b_pw1), ("b_dw", b_dw), ("ln_g", ln_g), ("ln_b", ln_b),
                     ("b_pw2", b_pw2))
    buf1, offs1 = _pack([c] + [t for _, t in small_sharded])
    w_in_t, m_w_in_t, v_w_in_t = (jnp.swapaxes(t, 1, 2) for t in (w_in, m_w_in, v_w_in))
    w_ffi_t, m_w_ffi_t, v_w_ffi_t = (jnp.swapaxes(t, 1, 2) for t in (w_ffn_in, m_w_ffn_in, v_w_ffn_in))
    w_in_b = w_in_t[0].astype(BF16)
    half_rows = w_in_b.shape[0] // 2
    got1, in_lo, in_hi = _all_gather([buf1, w_in_b[:half_rows], w_in_b[half_rows:]], "gather_cond", False)
    W_in_t = jnp.concatenate([in_lo.reshape(N_DEV, half_rows, Dm), in_hi.reshape(N_DEV, half_rows, Dm)],
                             axis=1).reshape(-1, Dm)
    got1 = got1.reshape(N_DEV, -1)
    c_all = _take(got1, offs1[0], (Dm,))
    full_small = {}
    for (nm, t), off in zip(small_sharded, offs1[1:]):
        seg = _take(got1, off, t.shape)
        full_small[nm] = jnp.moveaxis(seg, 0, -2).reshape(t.shape[:-1] + (N_DEV * t.shape[-1],))
    w_dw_f, b_pw1_f = full_small["w_dw"], full_small["b_pw1"]
    b_dw_f, ln_g_f, ln_b_f, b_pw2_f = (full_small[k] for k in ("b_dw", "ln_g", "ln_b", "b_pw2"))

    c_rows = jnp.concatenate([c_all, c_ctx[None, :], jnp.zeros((MOD_ROWS - N_DEV - 1, Dm), F32)], axis=0)
    mod_part = _mod_fwd(c_rows, w_mod, "mod_fwd")
    n_mod = w_mod.shape[2]
    got2 = _all_gather([mod_part.reshape(-1, LANES)], "gather_mod", True)[0]
    mod_all = got2.reshape(N_DEV, 2, MOD_ROWS, n_mod).transpose(1, 2, 0, 3).reshape(2, MOD_ROWS, N_DEV * n_mod)
    mod_all = mod_all + b_mod[:, None, :]
    my_mod = lax.dynamic_index_in_dim(mod_all, me, axis=1, keepdims=False)
    sh1, sc1, gt1, sh2, sc2, gt2 = ([my_mod[l:l + 1, k * Dm:(k + 1) * Dm] for l in range(2)] for k in range(6))
    csh1 = mod_all[0, N_DEV:N_DEV + 1, 0:Dm]
    csc1 = mod_all[0, N_DEV:N_DEV + 1, Dm:2 * Dm]

    behind = got2[0:1, 0:1] * 0.0
    gather_groups = [[w_out[0]], [w_ffi_t[0], w_ffn_out[0]], [w_pw1[0], w_pw2[0]], [w_ffi_t[1], w_ffn_out[1]]]
    gathers = [_push_begin([(t + behind).astype(BF16) for t in grp], True, f"gather_start{k}")
               for k, grp in enumerate(gather_groups)]
    started = sum(h[4][0:1, 0:1] for h in gathers)

    def gathered(k, after):
        return _push_end(gathers[k], after, f"gather_wait{k}")[1]

    def ffn_weights(k, after):
        wi, wo = gathered(k, after)
        return wi.reshape(N_DEV, FF_SHARD, Dm), wo.reshape(N_DEV // 2, FF_SHARD, Dm)

    def col_gathered(t, n):
        return t.reshape(N_DEV, Dm, n).transpose(1, 0, 2).reshape(Dm, N_DEV * n)

    W_ffi, W_ffo = [None, None], [None, None]

    g_mix_r = [g_mix[l:l + 1] for l in range(2)]
    g_ffn_r = [g_ffn[l:l + 1] for l in range(2)]
    g_fin = g_final[None, :]

    cos, sin = _rope_tables(T, C)
    qg = jnp.tile(q_gain, (1, 2))
    kg = jnp.tile(k_gain, (1, 2))
    lane_head = jnp.arange(LANES) // HEAD_DIM
    bd = (lane_head[:, None] == lane_head[None, :]).astype(BF16)
    w_sp0 = w_sp[0]
    w_spt0 = w_sp0.transpose(0, 2, 1)
    b_spt0 = b_sp[0].T

    XM = _norm_mod_fwd_cat(ctx2, h0, g_mix_r[0], csc1, csh1, sc1[0] + started, sh1[0], "norm_mix0")
    P = _mm(XM, W_in_t, "nt", "in_proj", tm=1088, tn=IN_WIDTH)
    qh, kpad, vpad, kt, ao = _mix_prep_fwd(P, C, cos, sin, qg, kg, bd, w_sp0, b_spt0, "mix_prep")
    ao, lse = _attn_fwd(qh, kpad, vpad, ao, C, "attn_fwd")
    W_out, = gathered(0, ao)
    h1, y0, xf0 = _mm(ao, W_out, "nn", "out_proj", res=h0, gate=gt1[0], raw_out=True,
                      norm=(g_ffn_r[0], sc2[0], sh2[0]))

    def ffn_fwd(h_in, xf, l, norm_next):
        W_ffi[l], W_ffo[l] = ffn_weights(1 + 2 * l, xf)
        gu, act = _ffn_in_swiglu(xf, W_ffi[l], f"ffn_in{l}")
        outs = _mm_sum_shards(act, W_ffo[l], "nn", f"ffn_out{l}", res=h_in, gate=gt2[l], raw_out=True,
                              norm=norm_next)
        return tuple(outs) + (None,) * (3 - len(outs)) + (gu, act)

    h2, f0, xm1, gu0, act0 = ffn_fwd(h1, xf0, 0, (g_mix_r[1], sc1[1], sh1[1]))

    W_pw1, W_pw2 = gathered(2, xm1)
    W_pw1 = col_gathered(W_pw1, 2 * Dm // N_DEV)
    ag = _mm(xm1, W_pw1, "nn", "pw1", BF16, bias=b_pw1_f)
    hg = _glu_fwd(ag, "glu")
    hd = _conv_fwd(hg, w_dw_f, b_dw_f, "conv")
    hs = _ln_silu_fwd(hd, ln_g_f, ln_b_f, "ln_silu")
    h3, y1, xf1 = _mm(hs, W_pw2, "nn", "pw2", bias=b_pw2_f, res=h2, gate=gt1[1], raw_out=True,
                      norm=(g_ffn_r[1], sc2[1], sh2[1]))
    h4, f1, _, gu1, act1 = ffn_fwd(h3, xf1, 1, None)

    dh4, sq_err, dg_final, df1, dgt2_1, _ = _final_fwd_bwd(h4, g_fin, target, f1, gt2[1], "loss_head")
    loss_local = (0.5 / Dm) * sq_err[0, 0:1]

    def col_shards(g, n):
        return g.reshape(Dm, N_DEV, n).transpose(1, 0, 2).reshape(N_DEV * Dm, n)

    def exchange_begin(k, parts):
        return _push_begin(parts, False, f"exchange_start{k}")

    def zero_of(handle):
        return handle[4][0:1, 0:1]

    def ffn_bwd(df, xf, gu, act, l):
        dw_out = _mm_tn_shard_rows(act, df, f"ffn_out_dw{l}", BF16)
        dgu = _ffn_out_dx_swiglu(df, W_ffo[l], gu, f"ffn_out_dx{l}").reshape(N_DEV, T, FF_SHARD)
        dw_in = _mm_tn_shard_rows(dgu, xf, f"ffn_in_dw{l}", BF16)
        dxf = _mm_sum_shards(dgu, W_ffi[l], "nn", f"ffn_in_dx{l}", BF16, tm=512)
        return dw_in, dw_out, dxf

    dW_ffi1, dW_ffo1, dxf1 = ffn_bwd(df1, xf1, gu1, act1, 1)
    ex0 = exchange_begin(0, [dW_ffi1.reshape(2 * D_FF, Dm), dW_ffo1.reshape(D_FF, Dm)])
    dh3, da, dsh, dy1, dgt1_1, db_pw2 = _norm_mod_bwd(h3, g_ffn_r[1], sc2[1], dxf1, dh4, "norm_ffn_bwd1",
                                                       gate=(y1, gt1[1] + zero_of(ex0)))
    dmod_ffn1 = (dsh, da * g_ffn_r[1], dgt2_1)
    dg_ffn1 = da * (1.0 + sc2[1])

    dW_pw2 = _mm(hs, dy1, "tn", "pw2_dw", BF16, tk=2048)
    dhs = _mm(dy1, W_pw2, "nt", "pw2_dx", BF16)
    dhd, dln_g, dln_b, db_dw = _ln_silu_bwd(dhs, hd, ln_g_f, ln_b_f, "ln_silu_bwd")
    dhg, dw_dw = _conv_bwd(dhd, hg, w_dw_f, "conv_bwd")
    dag, db_pw1 = _glu_bwd(ag, dhg, "glu_bwd")
    dW_pw1 = _mm(xm1, dag, "tn", "pw1_dw", BF16, tk=2048)
    dxm1 = _mm(dag, W_pw1, "nt", "pw1_dx", BF16, tk=2048)
    ex1 = exchange_begin(1, [col_shards(dW_pw1, 2 * Dm // N_DEV), dW_pw2])
    dh2, da, dsh, df0, dgt2_0, _ = _norm_mod_bwd(h2, g_mix_r[1], sc1[1], dxm1, dh3, "norm_mix1_bwd",
                                                 gate=(f0, gt2[0] + zero_of(ex1)))
    dmod_mix1 = (dsh, da * g_mix_r[1], dgt1_1)
    dg_mix1 = da * (1.0 + sc1[1])

    dW_ffi0, dW_ffo0, dxf0 = ffn_bwd(df0, xf0, gu0, act0, 0)
    ex2 = exchange_begin(2, [dW_ffi0.reshape(2 * D_FF, Dm), dW_ffo0.reshape(D_FF, Dm)])
    dh1, da, dsh, dy0, dgt1_0, _ = _norm_mod_bwd(h1, g_ffn_r[0], sc2[0], dxf0, dh2, "norm_ffn_bwd0",
                                                 gate=(y0, gt1[0] + zero_of(ex2)))
    dmod_ffn0 = (dsh, da * g_ffn_r[0], dgt2_0)
    dg_ffn0 = da * (1.0 + sc2[0])

    dW_out = _mm(ao, dy0, "tn", "out_proj_dw", BF16, tk=2048)
    ex_out = exchange_begin(4, [dW_out])
    dao = _mm(dy0, W_out + zero_of(ex_out).astype(BF16), "nt", "out_proj_dx", BF16)
    dq, f_acc = _attn_bwd(qh, dao, ao, lse, kpad, vpad, kt, C, "attn_bwd")
    dP, dqg, dkg, dw_sp0, db_spt0 = _mix_prep_bwd(P, dq, f_acc, dao, C, cos, sin, qg, kg, bd, w_sp0, w_spt0,
                                                  b_spt0, "mix_prep_bwd")
    dW_in_t = _mm(dP, XM, "tn", "in_proj_dw", BF16, tm=896, tk=2176)
    dXM = _mm(dP, W_in_t, "nn", "in_proj_dx", BF16, tm=1088, tk=IN_WIDTH)
    dh0, da, dsh = _norm_mod_bwd(h0, g_mix_r[0], sc1[0], dXM, dh1, "norm_mix0_bwd", dxm_row_off=C)
    _, dac, dcsh = _norm_mod_bwd(ctx2, g_mix_r[0], csc1, dXM, None, "norm_ctx_bwd")
    dmod_mix0 = (dsh, da * g_mix_r[0], dgt1_0)
    dg_mix0 = da * (1.0 + sc1[0]) + dac * (1.0 + csc1)
    dcmod = jnp.concatenate([dcsh, dac * g_mix_r[0]], axis=1)

    dmod_mine = jnp.stack([jnp.concatenate(dmod_mix0 + dmod_ffn0, axis=1)[0],
                           jnp.concatenate(dmod_mix1 + dmod_ffn1, axis=1)[0]])

    small_grads = [
        ("loss", loss_local), ("g_final", dg_final), ("g_mix", jnp.concatenate([dg_mix0, dg_mix1])),
        ("g_ffn", jnp.concatenate([dg_ffn0, dg_ffn1])),
        ("q_gain", dqg[:, :HEAD_DIM] + dqg[:, HEAD_DIM:]), ("k_gain", dkg[:, :HEAD_DIM] + dkg[:, HEAD_DIM:]),
        ("w_sp", dw_sp0[None]), ("b_sp", db_spt0.T[None]), ("b_pw1", db_pw1), ("w_dw", dw_dw[None]),
        ("b_dw", db_dw), ("ln_g", dln_g), ("ln_b", dln_b), ("b_pw2", db_pw2), ("dcmod", dcmod),
        ("dmod", dmod_mine),
    ]
    buf3, offs3 = _pack([t for _, t in small_grads], row_mult=SUBLANES * SMALL_SPLIT)
    got3 = _gather_split(buf3, SMALL_SPLIT, "gather_small_grads", True)
    sum3 = _sum_devices(got3, "sum_small_grads").reshape(-1)
    off3 = {nm: off for (nm, _), off in zip(small_grads, offs3)}
    shape3 = {nm: t.shape for nm, t in small_grads}

    def summed(nm):
        return _take(sum3, off3[nm], shape3[nm])

    loss = summed("loss")[0]
    dcmod_sum = summed("dcmod")
    dmod_rows = _take(got3.reshape(N_DEV, -1), off3["dmod"], (2, 6 * Dm)).transpose(1, 0, 2)
    ctx_row = jnp.concatenate([jnp.pad(dcmod_sum, ((0, 0), (0, 4 * Dm))), jnp.zeros((1, 6 * Dm), F32)])
    dmod_all = jnp.concatenate([dmod_rows, ctx_row[:, None, :],
                                jnp.zeros((2, LANES - N_DEV - 1, 6 * Dm), F32)], axis=1)
    grads = {}
    grads["b_mod"] = summed("dmod") + ctx_row
    dmod_shard = lax.dynamic_slice_in_dim(dmod_all, me * n_mod, n_mod, axis=2)
    c_rows_t = jnp.pad(c_rows.T, ((0, 0), (0, LANES - MOD_ROWS)))
    grads["w_mod"], ds_part = _mod_bwd(c_rows_t, dmod_shard, w_mod, "mod_bwd")

    buf4, _ = _pack([ds_part[0, N_DEV]])
    got4 = _all_gather([buf4], "gather_c_ctx_grad", True)[0].reshape(N_DEV, buf4.shape[0], LANES)
    ds_ctx = _sum_devices(got4, "sum_c_ctx_grad").reshape(-1)[:Dm]
    behind_small = (ds_ctx[0:1] * 0.0).astype(BF16)
    ex3 = exchange_begin(3, [dW_in_t + behind_small])
    grads["c_ctx"] = ds_ctx * _dsilu(c_ctx) + zero_of(ex3)[0]

    for nm in ("g_final", "g_mix", "g_ffn", "q_gain", "k_gain", "w_sp", "b_sp"):
        grads[nm] = summed(nm).reshape(weights[nm].shape)
    for nm in ("b_pw1", "w_dw", "b_dw", "ln_g", "ln_b", "b_pw2"):
        n_loc = weights[nm].shape[-1]
        grads[nm] = lax.dynamic_slice_in_dim(summed(nm), me * n_loc, n_loc, axis=-1).reshape(weights[nm].shape)

    delta, new_m, new_v = {}, {}, {}
    shp = w_mod.shape
    outs = _adamw(w_mod.reshape(-1, shp[-1]), grads["w_mod"].reshape(-1, shp[-1]),
                  m_w_mod.reshape(-1, shp[-1]), v_w_mod.reshape(-1, shp[-1]), "adamw_w_mod")
    delta["w_mod"], new_m["w_mod"], new_v["w_mod"] = (o.reshape(shp) for o in outs)
    big_names = ("w_mod", "w_ffn_in", "w_ffn_out", "w_in", "w_out", "w_pw1", "w_pw2")
    small_names = [nm for nm in names if nm not in big_names]
    packs = [_pack([src[nm] for nm in small_names]) for src in (weights, grads, moments_m, moments_v)]
    offs_s = packs[0][1]
    outs = _adamw(*[pk[0] for pk in packs], "adamw_small")
    for o, dst in zip(outs, (delta, new_m, new_v)):
        o = o.reshape(-1)
        for nm, off in zip(small_names, offs_s):
            dst[nm] = _take(o, off, weights[nm].shape)

    def exchanged(k, handle, after):
        return _push_end(handle, after, f"exchange_wait{k}")[1]

    def adamw_big(nm, parts, transposed=False, wmv=None):
        w3, m3, v3 = wmv if wmv is not None else (weights[nm], moments_m[nm], moments_v[nm])
        outs4 = _adamw_recv(w3, m3, v3, parts, f"adamw_{nm}")
        if transposed:
            outs4 = [jnp.swapaxes(t, 1, 2) for t in outs4]
        grads[nm], delta[nm], new_m[nm], new_v[nm] = outs4

    r_ffi1, r_ffo1 = exchanged(0, ex0, outs[0])
    r_pw1, r_pw2 = exchanged(1, ex1, outs[0])
    r_ffi0, r_ffo0 = exchanged(2, ex2, outs[0])
    r_out, = exchanged(4, ex_out, outs[0])
    adamw_big("w_ffn_in", [r_ffi0, r_ffi1], True, (w_ffi_t, m_w_ffi_t, v_w_ffi_t))
    adamw_big("w_ffn_out", [r_ffo0, r_ffo1])
    adamw_big("w_pw1", [r_pw1])
    adamw_big("w_pw2", [r_pw2])
    adamw_big("w_out", [r_out])
    r_in, = exchanged(3, ex3, delta["w_out"])
    adamw_big("w_in", [r_in], True, (w_in_t, m_w_in_t, v_w_in_t))

    return (loss, dh0[None], *[grads[n] for n in names], *[delta[n] for n in names],
            *[new_m[n] for n in names], *[new_v[n] for n in names])
```

```python
import math

import jax
import jax.numpy as jnp
from jax import lax
from jax.experimental import pallas as pl
from jax.experimental.pallas import tpu as pltpu

F32 = jnp.float32
BF16 = jnp.bfloat16
MESH = pl.DeviceIdType.MESH

N_DEV = 8
D_MODEL = 1024
EPS = 1e-6
HEAD_DIM = 64
ATTN_WIDTH = 512
KV_WIDTH = 128
SG_WIDTH = 512
N_SG_GROUPS = 4
CHUNK = 128
IN_WIDTH = 1792
D_FF = 2816
FF_SHARD = 2 * D_FF // N_DEV
CONV_WIDTH = 31
CONV_HALO = 16
GRID_W = 64
ROPE_THETA = 10000.0
LANES = 128
SUBLANES = 8
ROW_BLOCK = 512
ADAM_ROWS = 256
ADAM_LR, ADAM_B1, ADAM_B2, ADAM_EPS, ADAM_WD, ADAM_STEP = 0.001, 0.9, 0.999, 1e-08, 0.01, 10


def _tile(n, target, mult=LANES):
    best = None
    for t in range(mult, min(n, target) + 1, mult):
        if n % t == 0:
            best = t
    return best if best is not None else n


def _sigmoid(x):
    return 1.0 / (1.0 + jnp.exp(-x))


def _silu(x):
    return x * _sigmoid(x)


def _dsilu(x):
    s = _sigmoid(x)
    return s * (1.0 + x * (1.0 - s))


_GELU_K = math.sqrt(2.0 / math.pi)


def _gelu(x):
    return 0.5 * x * (1.0 + jnp.tanh(_GELU_K * (x + 0.044715 * x * x * x)))


def _gelu_and_grad(x):
    x2 = x * x
    t = jnp.tanh(_GELU_K * x * (1.0 + 0.044715 * x2))
    half = 0.5 * (1.0 + t)
    return x * half, half + 0.5 * x * (1.0 - t * t) * _GELU_K * (1.0 + 3.0 * 0.044715 * x2)


def _split_bf16(x):
    hi = x.astype(BF16)
    lo = (x - hi.astype(F32)).astype(BF16)
    return hi, lo


def _dot(a, b, dims):
    return lax.dot_general(a, b, (dims, ((), ())), preferred_element_type=F32)


def _dot3(a, b, dims):
    ah, al = _split_bf16(a)
    bh, bl = _split_bf16(b)
    return _dot(ah, bh, dims) + _dot(ah, bl, dims) + _dot(al, bh, dims)


NN = ((1,), (0,))
NT = ((1,), (1,))
TN = ((0,), (0,))


def _all_gather(xs, name, in_vmem):
    n_arr = len(xs)

    def body(*refs):
        x_refs, out_refs = refs[:n_arr], refs[n_arr:2 * n_arr]
        send_sems, recv_sems, local_sems = refs[2 * n_arr:]
        x, y, c = lax.axis_index("x"), lax.axis_index("y"), lax.axis_index("c")
        me, sibling = (x, y, c), (x, y, 1 - c)
        chips = [(1 - x, y), (x, 1 - y), (1 - x, 1 - y)]

        def rows(a, px, py, pc):
            m_per = xs[a].shape[0]
            return out_refs[a].at[pl.ds((4 * px + 2 * py + pc) * m_per, m_per), :]

        def copy(a, k, block, to, src=None):
            return pltpu.make_async_remote_copy(
                src_ref=rows(a, *block) if src is None else src,
                dst_ref=rows(a, *block),
                send_sem=send_sems.at[7 * a + k],
                recv_sem=recv_sems.at[7 * a + k],
                device_id=to,
                device_id_type=MESH,
            )

        mine, first, passed = [], [], []
        for a in range(n_arr):
            mine.append(pltpu.make_async_copy(x_refs[a], rows(a, *me), local_sems.at[a]))
            mine[-1].start()
            first.append(copy(a, 0, me, sibling, src=x_refs[a]))
            first += [copy(a, 1 + j, me, (*chip, c), src=x_refs[a]) for j, chip in enumerate(chips)]
        for cp in first:
            cp.start()
        for a in range(n_arr):
            for j, chip in enumerate(chips):
                copy(a, 1 + j, (*chip, c), me).wait_recv()
                passed.append(copy(a, 4 + j, (*chip, c), sibling))
                passed[-1].start()
        for a in range(n_arr):
            copy(a, 0, sibling, me).wait_recv()
            for j, chip in enumerate(chips):
                copy(a, 4 + j, (*chip, 1 - c), me).wait_recv()
        for cp in first + passed:
            cp.wait_send()
        for cp in mine:
            cp.wait()

    space = pltpu.VMEM if in_vmem else pl.ANY
    return pl.pallas_call(
        body,
        name=name,
        out_shape=[jax.ShapeDtypeStruct((N_DEV * t.shape[0], t.shape[1]), t.dtype) for t in xs],
        in_specs=[pl.BlockSpec(memory_space=space)] * n_arr,
        out_specs=[pl.BlockSpec(memory_space=space)] * n_arr,
        scratch_shapes=[
            pltpu.SemaphoreType.DMA((7 * n_arr,)),
            pltpu.SemaphoreType.DMA((7 * n_arr,)),
            pltpu.SemaphoreType.DMA((n_arr,)),
        ],
    )(*xs)


HBM_SPEC = pl.BlockSpec(memory_space=pltpu.HBM)
SEM_SPEC = pl.BlockSpec(memory_space=pltpu.SEMAPHORE)
DATAFLOW_EFFECT = pltpu.SideEffectType.DATAFLOW_SIDE_EFFECTING


def _peers(x, y, c):
    for k in range(1, N_DEV):
        px = 1 - x if (k >> 2) & 1 else x
        py = 1 - y if (k >> 1) & 1 else y
        pc = 1 - c if k & 1 else c
        yield k - 1, (px, py, pc), 4 * px + 2 * py + pc


def _push_copies(src_refs, land_refs, send_sems, recv_sems, shapes, whole_src):
    x, y, c = lax.axis_index("x"), lax.axis_index("y"), lax.axis_index("c")
    me = 4 * x + 2 * y + c
    for a, (m_per, _) in enumerate(shapes):
        def block(ref, idx, m_per=m_per):
            return ref.at[pl.ds(idx * m_per, m_per), :]

        for k, peer, pidx in _peers(x, y, c):
            src = src_refs[a] if whole_src else block(src_refs[a], pidx)
            sems = dict(send_sem=send_sems.at[N_DEV * a + k], recv_sem=recv_sems.at[N_DEV * a + k],
                        device_id=peer, device_id_type=MESH)
            yield (pltpu.make_async_remote_copy(src_ref=src, dst_ref=block(land_refs[a], me), **sems),
                   pltpu.make_async_remote_copy(src_ref=src, dst_ref=block(land_refs[a], pidx), **sems))


def _own_copies(src_refs, land_refs, recv_sems, shapes, whole_src):
    me = 4 * lax.axis_index("x") + 2 * lax.axis_index("y") + lax.axis_index("c")
    for a, (m_per, _) in enumerate(shapes):
        mine = pl.ds(me * m_per, m_per)
        src = src_refs[a] if whole_src else src_refs[a].at[mine, :]
        yield pltpu.make_async_copy(src, land_refs[a].at[mine, :], recv_sems.at[N_DEV * a + N_DEV - 1])


def _push_begin(srcs, whole_src, name):
    n_arr = len(srcs)
    shapes = [(t.shape[0] if whole_src else t.shape[0] // N_DEV, t.shape[1]) for t in srcs]
    lands = [lax.empty((N_DEV * m, n), t.dtype) for (m, n), t in zip(shapes, srcs)]

    def body(*refs):
        src_refs, land_refs = refs[:n_arr], refs[n_arr:2 * n_arr]
        send_sems, recv_sems = refs[2 * n_arr], refs[2 * n_arr + 1]
        token = refs[-1]
        for outgoing, _ in _push_copies(src_refs, land_refs, send_sems, recv_sems, shapes, whole_src):
            outgoing.start()
        for own in _own_copies(src_refs, land_refs, recv_sems, shapes, whole_src):
            own.start()
        token[...] = jnp.zeros_like(token)

    operands = [pltpu.with_memory_space_constraint(t, pltpu.HBM) for t in list(srcs) + lands]
    outs = pl.pallas_call(
        body, name=name,
        out_shape=(pltpu.SemaphoreType.DMA((N_DEV * n_arr,)), pltpu.SemaphoreType.DMA((N_DEV * n_arr,)),
                   *[pltpu.HBM(t.shape, t.dtype) for t in operands],
                   jax.ShapeDtypeStruct((SUBLANES, LANES), F32)),
        in_specs=[HBM_SPEC] * (2 * n_arr),
        out_specs=(SEM_SPEC, SEM_SPEC, *[HBM_SPEC] * (2 * n_arr), pl.BlockSpec(memory_space=pltpu.VMEM)),
        input_output_aliases={i: 2 + i for i in range(2 * n_arr)},
        compiler_params=pltpu.CompilerParams(has_side_effects=DATAFLOW_EFFECT),
    )(*operands)
    return outs[0], outs[1], list(outs[2:2 + n_arr]), list(outs[2 + n_arr:2 + 2 * n_arr]), outs[-1], whole_src


def _push_end(handle, after, name):
    send_sems, recv_sems, srcs, lands, _, whole_src = handle
    n_arr = len(srcs)
    shapes = [(t.shape[0] // N_DEV, t.shape[1]) for t in lands]

    def body(*refs):
        src_refs, land_refs = refs[:n_arr], refs[n_arr:2 * n_arr]
        send_sems_ref, recv_sems_ref = refs[2 * n_arr], refs[2 * n_arr + 1]
        for outgoing, incoming in _push_copies(src_refs, land_refs, send_sems_ref, recv_sems_ref, shapes, whole_src):
            outgoing.wait_send()
            incoming.wait_recv()
        for own in _own_copies(src_refs, land_refs, recv_sems_ref, shapes, whole_src):
            own.wait()

    outs = pl.pallas_call(
        body, name=name,
        out_shape=tuple(pltpu.HBM(t.shape, t.dtype) for t in srcs + lands),
        in_specs=[HBM_SPEC] * (2 * n_arr) + [SEM_SPEC, SEM_SPEC, pl.BlockSpec(memory_space=pl.ANY)],
        out_specs=tuple([HBM_SPEC] * (2 * n_arr)),
        input_output_aliases={i: i for i in range(2 * n_arr)},
        compiler_params=pltpu.CompilerParams(has_side_effects=DATAFLOW_EFFECT),
    )(*srcs, *lands, send_sems, recv_sems, after)
    return list(outs[:n_arr]), list(outs[n_arr:])


def _sum_devices(r, name, rows_per_step=ADAM_ROWS):
    _, m, n = r.shape
    tm = _tile(m, rows_per_step, 8)

    def body(r_ref, o_ref):
        acc = r_ref[0].astype(F32)
        for s in range(1, N_DEV):
            acc = acc + r_ref[s].astype(F32)
        o_ref[...] = acc

    return pl.pallas_call(
        body,
        name=name,
        grid=(m // tm,),
        out_shape=jax.ShapeDtypeStruct((m, n), F32),
        in_specs=[pl.BlockSpec((N_DEV, tm, n), lambda i: (0, i, 0))],
        out_specs=pl.BlockSpec((tm, n), lambda i: (i, 0)),
        compiler_params=pltpu.CompilerParams(dimension_semantics=("parallel",)),
    )(r)


def _get(ref):
    return ref[0] if len(ref.shape) == 3 else ref[...]


def _put(ref, val):
    if len(ref.shape) == 3:
        ref[0] = val
    else:
        ref[...] = val


def _norm_mod(hv, g, sc, sh):
    r = lax.rsqrt(jnp.mean(hv * hv, axis=-1, keepdims=True) + EPS)
    return (hv * r) * g * (1.0 + sc) + sh


def _mm_call(name, a, b, a_spec, b_spec, out_sds, o_spec, grid, dims, acc_shape, bias=None,
             res=None, gate=None, raw_out=False, vec_spec=None, norm=None):
    nk = grid[2]
    operands, in_specs = [a, b], [a_spec, b_spec]
    if bias is not None:
        operands.append(bias)
        in_specs.append(vec_spec)
    if res is not None:
        operands += [res, gate]
        in_specs += [o_spec, vec_spec]
    if norm is not None:
        assert grid[1] == 1
        operands += list(norm)
        in_specs += [vec_spec] * 3
    out_shape, out_specs = [out_sds], [o_spec]
    if raw_out:
        out_shape.append(jax.ShapeDtypeStruct(out_sds.shape, BF16))
        out_specs.append(o_spec)
    if norm is not None:
        out_shape.append(jax.ShapeDtypeStruct(out_sds.shape, BF16))
        out_specs.append(o_spec)

    def body(*refs):
        it = iter(refs)
        a_ref, b_ref = next(it), next(it)
        bias_ref = next(it) if bias is not None else None
        res_ref, gate_ref = (next(it), next(it)) if res is not None else (None, None)
        norm_refs = (next(it), next(it), next(it)) if norm is not None else None
        o_ref = next(it)
        raw_ref = next(it) if raw_out else None
        xn_ref = next(it) if norm is not None else None
        acc = next(it) if nk > 1 else None
        k = pl.program_id(2)
        part = _dot(_get(a_ref).astype(BF16), _get(b_ref).astype(BF16), dims)

        def finish(y):
            if bias_ref is not None:
                y = y + bias_ref[...]
            if raw_ref is not None:
                raw_ref[...] = y.astype(BF16)
            if res_ref is not None:
                y = res_ref[...] + gate_ref[...] * y
            _put(o_ref, y.astype(out_sds.dtype))
            if xn_ref is not None:
                xn_ref[...] = _norm_mod(y, *[r[...] for r in norm_refs]).astype(BF16)

        if nk == 1:
            finish(part)
        else:
            @pl.when(k == 0)
            def _():
                acc[...] = part

            @pl.when(k > 0)
            def _():
                acc[...] += part

            @pl.when(k == nk - 1)
            def _():
                finish(acc[...])

    outs = pl.pallas_call(
        body,
        name=name,
        grid=grid,
        out_shape=out_shape,
        in_specs=in_specs,
        out_specs=out_specs,
        scratch_shapes=[pltpu.VMEM(acc_shape, F32)] if nk > 1 else [],
        compiler_params=pltpu.CompilerParams(dimension_semantics=("parallel", "parallel", "arbitrary")),
    )(*operands)
    return outs if len(outs) > 1 else outs[0]


def _mm(a, b, mode, name, out_dtype=F32, bias=None, res=None, gate=None, raw_out=False,
        tm=1024, tn=1024, tk=1024, a_row_off=0, norm=None):
    if mode == "nn":
        K, N = b.shape
        M = a.shape[0] - a_row_off
    elif mode == "nt":
        N, K = b.shape
        M = a.shape[0] - a_row_off
    else:
        (K, M), N = a.shape, b.shape[1]
    tm, tn, tk = _tile(M, tm, LANES if mode == "tn" else 2 * SUBLANES), _tile(N, tn), _tile(K, tk)
    off = a_row_off // tm
    dims = {"nn": NN, "nt": NT, "tn": TN}[mode]
    a_spec = (pl.BlockSpec((tk, tm), lambda i, j, k: (k, i)) if mode == "tn"
              else pl.BlockSpec((tm, tk), lambda i, j, k: (i + off, k)))
    b_spec = (pl.BlockSpec((tn, tk), lambda i, j, k: (j, k)) if mode == "nt"
              else pl.BlockSpec((tk, tn), lambda i, j, k: (k, j)))
    return _mm_call(name, a, b, a_spec, b_spec, jax.ShapeDtypeStruct((M, N), out_dtype),
                    pl.BlockSpec((tm, tn), lambda i, j, k: (i, j)), (M // tm, N // tn, K // tk), dims,
                    (tm, tn), bias, res, gate, raw_out, pl.BlockSpec((1, tn), lambda i, j, k: (0, j)), norm)


def _mm_sum_shards(a3, b3, mode, name, out_dtype=F32, res=None, gate=None, raw_out=False, tm=512, norm=None):
    S, M, kk = a3.shape
    N = b3.shape[2] if mode == "nn" else b3.shape[1]
    tm = _tile(M, tm)
    dims = NN if mode == "nn" else NT
    has_res = res is not None

    def body(*refs):
        it = iter(refs)
        a_ref, b_ref = next(it), next(it)
        res_ref, gate_ref = (next(it), next(it)) if has_res else (None, None)
        norm_refs = (next(it), next(it), next(it)) if norm is not None else None
        o_ref = next(it)
        raw_ref = next(it) if raw_out else None
        xn_ref = next(it) if norm is not None else None
        y = _dot(a_ref[0], b_ref[0], dims)
        for s in range(1, S):
            y = y + _dot(a_ref[s], b_ref[s], dims)
        if raw_ref is not None:
            raw_ref[...] = y.astype(BF16)
        if has_res:
            y = res_ref[...] + gate_ref[...] * y
        o_ref[...] = y.astype(out_dtype)
        if xn_ref is not None:
            xn_ref[...] = _norm_mod(y, *[r[...] for r in norm_refs]).astype(BF16)

    tile = pl.BlockSpec((tm, N), lambda i: (i, 0))
    operands = [a3, b3] + ([res, gate] if has_res else []) + (list(norm) if norm is not None else [])
    in_specs = [pl.BlockSpec((S, tm, kk), lambda i: (0, i, 0)), pl.BlockSpec(b3.shape, lambda i: (0, 0, 0))]
    in_specs += [tile, _vec_spec(N)] if has_res else []
    in_specs += [_vec_spec(N)] * 3 if norm is not None else []
    out_shape = [jax.ShapeDtypeStruct((M, N), out_dtype)] + ([jax.ShapeDtypeStruct((M, N), BF16)] if raw_out else [])
    out_shape += [jax.ShapeDtypeStruct((M, N), BF16)] if norm is not None else []
    outs = pl.pallas_call(
        body, name=name, grid=(M // tm,),
        out_shape=out_shape, in_specs=in_specs, out_specs=[tile] * len(out_shape),
        compiler_params=pltpu.CompilerParams(dimension_semantics=("parallel",)),
    )(*operands)
    return outs if len(outs) > 1 else outs[0]


def _mm_tn_shard_rows(a3, b, name, out_dtype, tn=1024, tk=4096):
    S, T, m = a3.shape
    N = b.shape[1]
    tn, tk = _tile(N, tn), _tile(T, tk)
    return _mm_call(name, a3, b, pl.BlockSpec((1, tk, m), lambda i, j, k: (i, k, 0)),
                    pl.BlockSpec((tk, tn), lambda i, j, k: (k, j)), jax.ShapeDtypeStruct((S, m, N), out_dtype),
                    pl.BlockSpec((1, m, tn), lambda i, j, k: (i, 0, j)), (S, N // tn, T // tk), TN, (m, tn))


def _row_spec(tm, width, off=0):
    return pl.BlockSpec((tm, width), lambda i: (i + off, 0))


def _vec_spec(width):
    return pl.BlockSpec((1, width), lambda i: (0, 0))


def _norm_mod_fwd_cat(hc, h, g, csc, csh, sc, sh, name):
    (C, Dm), T = hc.shape, h.shape[0]
    tm = _tile(math.gcd(C, T), ROW_BLOCK, 8)
    off = C // tm

    def body(hc_ref, h_ref, g_ref, csc_ref, csh_ref, sc_ref, sh_ref, o_ref):
        is_ctx = pl.program_id(0) < off
        hv = jnp.where(is_ctx, hc_ref[...], h_ref[...])
        scv = jnp.where(is_ctx, csc_ref[...], sc_ref[...])
        shv = jnp.where(is_ctx, csh_ref[...], sh_ref[...])
        r = lax.rsqrt(jnp.mean(hv * hv, axis=-1, keepdims=True) + EPS)
        o_ref[...] = ((hv * r) * g_ref[...] * (1.0 + scv) + shv).astype(BF16)

    return pl.pallas_call(
        body, name=name, grid=((C + T) // tm,),
        out_shape=jax.ShapeDtypeStruct((C + T, Dm), BF16),
        in_specs=[pl.BlockSpec((tm, Dm), lambda i: (jnp.minimum(i, off - 1), 0)),
                  pl.BlockSpec((tm, Dm), lambda i: (jnp.maximum(i - off, 0), 0))] + [_vec_spec(Dm)] * 5,
        out_specs=_row_spec(tm, Dm),
        compiler_params=pltpu.CompilerParams(dimension_semantics=("parallel",)),
    )(hc, h, g, csc, csh, sc, sh)


def _gate_grads(dh, y_ref, gt_ref, dy_ref, dgt_ref, dsum_ref):
    dy = dh * gt_ref[...]
    dgt_ref[...] += jnp.sum(dh * y_ref[...].astype(F32), axis=0, keepdims=True)
    dsum_ref[...] += jnp.sum(dy, axis=0, keepdims=True)
    dy_ref[...] = dy.astype(BF16)


def _norm_mod_bwd(h, g, sc, dxm, dres, name, dxm_row_off=0, gate=None):
    R, Dm = h.shape
    tm = _tile(math.gcd(R, dxm_row_off) if dxm_row_off else R, ROW_BLOCK, 8)
    off = dxm_row_off // tm
    has_res = dres is not None
    has_gate = gate is not None

    def body(*refs):
        it = iter(refs)
        h_ref, g_ref, sc_ref, dx_ref = next(it), next(it), next(it), next(it)
        dres_ref = next(it) if has_res else None
        y_ref, gt_ref = (next(it), next(it)) if has_gate else (None, None)
        dh_ref, da_ref, dsh_ref = next(it), next(it), next(it)
        gate_out = (next(it), next(it), next(it)) if has_gate else ()
        i = pl.program_id(0)

        @pl.when(i == 0)
        def _():
            for ref in (da_ref, dsh_ref) + gate_out[1:]:
                ref[...] = jnp.zeros_like(ref)

        hv = h_ref[...]
        dx = dx_ref[...].astype(F32)
        r = lax.rsqrt(jnp.mean(hv * hv, axis=-1, keepdims=True) + EPS)
        n = hv * r
        da_ref[...] += jnp.sum(dx * n, axis=0, keepdims=True)
        dsh_ref[...] += jnp.sum(dx, axis=0, keepdims=True)
        dn = dx * (g_ref[...] * (1.0 + sc_ref[...]))
        dh = r * (dn - n * jnp.mean(dn * n, axis=-1, keepdims=True))
        if has_res:
            dh = dh + dres_ref[...]
        dh_ref[...] = dh
        if has_gate:
            _gate_grads(dh, y_ref, gt_ref, *gate_out)

    operands = [h, g, sc, dxm] + ([dres] if has_res else []) + (list(gate) if has_gate else [])
    in_specs = [_row_spec(tm, Dm), _vec_spec(Dm), _vec_spec(Dm), _row_spec(tm, Dm, off)]
    in_specs += [_row_spec(tm, Dm)] if has_res else []
    in_specs += [_row_spec(tm, Dm), _vec_spec(Dm)] if has_gate else []
    vec = jax.ShapeDtypeStruct((1, Dm), F32)
    out_shape = [jax.ShapeDtypeStruct((R, Dm), F32), vec, vec]
    out_specs = [_row_spec(tm, Dm), _vec_spec(Dm), _vec_spec(Dm)]
    if has_gate:
        out_shape += [jax.ShapeDtypeStruct((R, Dm), BF16), vec, vec]
        out_specs += [_row_spec(tm, Dm), _vec_spec(Dm), _vec_spec(Dm)]
    return pl.pallas_call(
        body, name=name, grid=(R // tm,),
        out_shape=out_shape, in_specs=in_specs, out_specs=out_specs,
        compiler_params=pltpu.CompilerParams(dimension_semantics=("arbitrary",)),
    )(*operands)


def _ffn_in_swiglu(xf, w3, name, tm=1024):
    T, K = xf.shape
    S, n, _ = w3.shape
    half = S // 2
    tm = _tile(T, tm)

    def body(a_ref, wg_ref, wu_ref, gu_ref, act_ref):
        a = a_ref[...]
        g = _dot(a, wg_ref[0], NT)
        u = _dot(a, wu_ref[0], NT)
        gu_ref[0, 0] = g.astype(BF16)
        gu_ref[1, 0] = u.astype(BF16)
        act_ref[0] = (_silu(g) * u).astype(BF16)

    return pl.pallas_call(
        body, name=name, grid=(T // tm, half),
        out_shape=[jax.ShapeDtypeStruct((2, half, T, n), BF16), jax.ShapeDtypeStruct((half, T, n), BF16)],
        in_specs=[pl.BlockSpec((tm, K), lambda i, j: (i, 0)),
                  pl.BlockSpec((1, n, K), lambda i, j: (j, 0, 0)),
                  pl.BlockSpec((1, n, K), lambda i, j: (j + half, 0, 0))],
        out_specs=[pl.BlockSpec((2, 1, tm, n), lambda i, j: (0, j, i, 0)),
                   pl.BlockSpec((1, tm, n), lambda i, j: (j, i, 0))],
        compiler_params=pltpu.CompilerParams(dimension_semantics=("parallel", "parallel")),
    )(xf, w3, w3)


def _ffn_out_dx_swiglu(df, wo, gu, name, tm=1024):
    T, Dm = df.shape
    half, n, _ = wo.shape
    tm = _tile(T, tm)

    def body(df_ref, w_ref, gu_ref, o_ref):
        da = _dot(df_ref[...], w_ref[0], NT)
        g = gu_ref[0, 0].astype(F32)
        u = gu_ref[1, 0].astype(F32)
        s = _sigmoid(g)
        o_ref[0, 0] = (da * u * (s * (1.0 + g * (1.0 - s)))).astype(BF16)
        o_ref[1, 0] = (da * (g * s)).astype(BF16)

    gu_spec = pl.BlockSpec((2, 1, tm, n), lambda i, j: (0, j, i, 0))
    return pl.pallas_call(
        body, name=name, grid=(T // tm, half),
        out_shape=jax.ShapeDtypeStruct(gu.shape, BF16),
        in_specs=[pl.BlockSpec((tm, Dm), lambda i, j: (i, 0)),
                  pl.BlockSpec((1, n, Dm), lambda i, j: (j, 0, 0)), gu_spec],
        out_specs=gu_spec,
        compiler_params=pltpu.CompilerParams(dimension_semantics=("parallel", "parallel")),
    )(df, wo, gu)


def _glu_fwd(ag, name):
    R = ag.shape[0]
    tm = _tile(R, ROW_BLOCK, 8)

    def body(ag_ref, o_ref):
        o_ref[...] = ag_ref[:, :D_MODEL].astype(F32) * _sigmoid(ag_ref[:, D_MODEL:].astype(F32))

    return pl.pallas_call(
        body, name=name, grid=(R // tm,),
        out_shape=jax.ShapeDtypeStruct((R, D_MODEL), F32),
        in_specs=[_row_spec(tm, 2 * D_MODEL)],
        out_specs=_row_spec(tm, D_MODEL),
        compiler_params=pltpu.CompilerParams(dimension_semantics=("parallel",)),
    )(ag)


def _glu_bwd(ag, dhg, name):
    R = ag.shape[0]
    tm = _tile(R, ROW_BLOCK, 8)

    def body(ag_ref, dh_ref, o_ref, s_ref):
        i = pl.program_id(0)

        @pl.when(i == 0)
        def _():
            s_ref[...] = jnp.zeros_like(s_ref)

        a = ag_ref[:, :D_MODEL].astype(F32)
        s = _sigmoid(ag_ref[:, D_MODEL:].astype(F32))
        dh = dh_ref[...]
        da = dh * s
        dg = dh * a * s * (1.0 - s)
        o_ref[:, :D_MODEL] = da.astype(BF16)
        o_ref[:, D_MODEL:] = dg.astype(BF16)
        s_ref[:, :D_MODEL] += jnp.sum(da, axis=0, keepdims=True)
        s_ref[:, D_MODEL:] += jnp.sum(dg, axis=0, keepdims=True)

    return pl.pallas_call(
        body, name=name, grid=(R // tm,),
        out_shape=[jax.ShapeDtypeStruct((R, 2 * D_MODEL), BF16), jax.ShapeDtypeStruct((1, 2 * D_MODEL), F32)],
        in_specs=[_row_spec(tm, 2 * D_MODEL), _row_spec(tm, D_MODEL)],
        out_specs=[_row_spec(tm, 2 * D_MODEL), _vec_spec(2 * D_MODEL)],
        compiler_params=pltpu.CompilerParams(dimension_semantics=("arbitrary",)),
    )(ag, dhg)


def _halo_specs(tm, nblk, width):
    per = tm // CONV_HALO
    prev = pl.BlockSpec((CONV_HALO, width), lambda i: (jnp.maximum(i * per - 1, 0), 0))
    nxt = pl.BlockSpec((CONV_HALO, width), lambda i: (jnp.minimum((i + 1) * per, nblk * per - 1), 0))
    return prev, nxt


def _fill_halo(scr, prev_ref, cur_ref, next_ref, i, nblk, tm):
    scr[0:CONV_HALO, :] = jnp.where(i > 0, prev_ref[...], 0.0)
    scr[CONV_HALO:CONV_HALO + tm, :] = cur_ref[...]
    scr[CONV_HALO + tm:2 * CONV_HALO + tm, :] = jnp.where(i < nblk - 1, next_ref[...], 0.0)


CONV_ROWS = 128


CONV_REACH = (CONV_WIDTH // SUBLANES) * SUBLANES


def _windows(scr, stage, cols, tm):
    for r in range(SUBLANES):
        if r:
            stage[r] = scr[pl.ds(r, tm + CONV_REACH), cols]
        for a in range(CONV_REACH // SUBLANES + 1):
            off = SUBLANES * a + r
            if 1 <= off <= CONV_WIDTH:
                yield off, (stage[r, SUBLANES * a:SUBLANES * a + tm, :] if r
                            else scr[SUBLANES * a:SUBLANES * a + tm, cols])


def _conv_fwd(hg, w_dw, b_dw, name):
    R, Dm = hg.shape
    tm = _tile(R, CONV_ROWS, CONV_HALO)
    nblk = R // tm
    prev_spec, next_spec = _halo_specs(tm, nblk, Dm)

    def body(prev_ref, cur_ref, next_ref, w_ref, bdw_ref, hd_ref, scr, stage):
        _fill_halo(scr, prev_ref, cur_ref, next_ref, pl.program_id(0), nblk, tm)
        for cb in range(Dm // LANES):
            cols = slice(cb * LANES, (cb + 1) * LANES)
            acc = jnp.zeros((tm, LANES), F32) + bdw_ref[:, cols]
            for off, win in _windows(scr, stage, cols, tm):
                acc = acc + w_ref[off - 1:off, cols] * win
            hd_ref[:, cols] = acc

    return pl.pallas_call(
        body, name=name, grid=(nblk,),
        out_shape=jax.ShapeDtypeStruct((R, Dm), F32),
        in_specs=[prev_spec, _row_spec(tm, Dm), next_spec,
                  pl.BlockSpec((CONV_WIDTH, Dm), lambda i: (0, 0)), _vec_spec(Dm)],
        out_specs=_row_spec(tm, Dm),
        scratch_shapes=[pltpu.VMEM((tm + 2 * CONV_HALO, Dm), F32),
                        pltpu.VMEM((SUBLANES, tm + CONV_REACH, LANES), F32)],
        compiler_params=pltpu.CompilerParams(dimension_semantics=("parallel",)),
    )(hg, hg, hg, w_dw, b_dw)


def _ln_silu_fwd(hd, ln_g, ln_b, name):
    R, Dm = hd.shape
    tm = _tile(R, ROW_BLOCK, 8)

    def body(hd_ref, g_ref, b_ref, hs_ref):
        hd = hd_ref[...]
        xc = hd - jnp.mean(hd, axis=-1, keepdims=True)
        rs = lax.rsqrt(jnp.mean(xc * xc, axis=-1, keepdims=True) + EPS)
        hs_ref[...] = _silu(xc * rs * g_ref[...] + b_ref[...]).astype(BF16)

    return pl.pallas_call(
        body, name=name, grid=(R // tm,),
        out_shape=jax.ShapeDtypeStruct((R, Dm), BF16),
        in_specs=[_row_spec(tm, Dm), _vec_spec(Dm), _vec_spec(Dm)],
        out_specs=_row_spec(tm, Dm),
        compiler_params=pltpu.CompilerParams(dimension_semantics=("parallel",)),
    )(hd, ln_g, ln_b)


def _ln_silu_bwd(dhs, hd, ln_g, ln_b, name):
    R, Dm = hd.shape
    tm = _tile(R, ROW_BLOCK, 8)

    def body(dhs_ref, hd_ref, g_ref, b_ref, dhd_ref, dg_ref, db_ref, dsum_ref):
        i = pl.program_id(0)

        @pl.when(i == 0)
        def _():
            dg_ref[...] = jnp.zeros_like(dg_ref)
            db_ref[...] = jnp.zeros_like(db_ref)
            dsum_ref[...] = jnp.zeros_like(dsum_ref)

        hd = hd_ref[...]
        mu = jnp.mean(hd, axis=-1, keepdims=True)
        xc = hd - mu
        rs = lax.rsqrt(jnp.mean(xc * xc, axis=-1, keepdims=True) + EPS)
        z = xc * rs
        hl = z * g_ref[...] + b_ref[...]
        dhl = dhs_ref[...].astype(F32) * _dsilu(hl)
        dg_ref[...] += jnp.sum(dhl * z, axis=0, keepdims=True)
        db_ref[...] += jnp.sum(dhl, axis=0, keepdims=True)
        dz = dhl * g_ref[...]
        dhd = rs * (dz - jnp.mean(dz, axis=-1, keepdims=True) - z * jnp.mean(dz * z, axis=-1, keepdims=True))
        dsum_ref[...] += jnp.sum(dhd, axis=0, keepdims=True)
        dhd_ref[...] = dhd

    return pl.pallas_call(
        body, name=name, grid=(R // tm,),
        out_shape=[jax.ShapeDtypeStruct((R, Dm), F32)] + [jax.ShapeDtypeStruct((1, Dm), F32)] * 3,
        in_specs=[_row_spec(tm, Dm), _row_spec(tm, Dm), _vec_spec(Dm), _vec_spec(Dm)],
        out_specs=[_row_spec(tm, Dm), _vec_spec(Dm), _vec_spec(Dm), _vec_spec(Dm)],
        compiler_params=pltpu.CompilerParams(dimension_semantics=("arbitrary",)),
    )(dhs, hd, ln_g, ln_b)


def _conv_bwd(dhd, hg, w_dw, name):
    R, Dm = hg.shape
    tm = _tile(R, CONV_ROWS, CONV_HALO)
    nblk = R // tm
    prev_spec, next_spec = _halo_specs(tm, nblk, Dm)

    def body(dprev, dcur, dnext, gprev, gcur, gnext, w_ref, dhg_ref, dw_ref, dscr, gscr, dwp, stage):
        i = pl.program_id(0)

        @pl.when(i == 0)
        def _():
            dwp[...] = jnp.zeros_like(dwp)

        _fill_halo(dscr, dprev, dcur, dnext, i, nblk, tm)
        _fill_halo(gscr, gprev, gcur, gnext, i, nblk, tm)
        for cb in range(Dm // LANES):
            cols = slice(cb * LANES, (cb + 1) * LANES)
            acc = jnp.zeros((tm, LANES), F32)
            for off, win in _windows(dscr, stage, cols, tm):
                j = CONV_WIDTH - off
                acc = acc + w_ref[j:j + 1, cols] * win
            dhg_ref[:, cols] = acc
            d_here = dcur[:, cols]
            for off, win in _windows(gscr, stage, cols, tm):
                j = off - 1
                prod = d_here * win
                part = prod[0:SUBLANES]
                for k in range(1, tm // SUBLANES):
                    part = part + prod[k * SUBLANES:(k + 1) * SUBLANES]
                dwp[j * SUBLANES:(j + 1) * SUBLANES, cols] += part

        @pl.when(i == nblk - 1)
        def _():
            for j in range(CONV_WIDTH):
                dw_ref[j:j + 1, :] = jnp.sum(dwp[j * SUBLANES:(j + 1) * SUBLANES, :], axis=0, keepdims=True)

    return pl.pallas_call(
        body, name=name, grid=(nblk,),
        out_shape=[jax.ShapeDtypeStruct((R, Dm), F32), jax.ShapeDtypeStruct((CONV_WIDTH, Dm), F32)],
        in_specs=[prev_spec, _row_spec(tm, Dm), next_spec, prev_spec, _row_spec(tm, Dm), next_spec,
                  pl.BlockSpec((CONV_WIDTH, Dm), lambda i: (0, 0))],
        out_specs=[_row_spec(tm, Dm), pl.BlockSpec((CONV_WIDTH, Dm), lambda i: (0, 0))],
        scratch_shapes=[pltpu.VMEM((tm + 2 * CONV_HALO, Dm), F32)] * 2
        + [pltpu.VMEM((CONV_WIDTH * SUBLANES, Dm), F32), pltpu.VMEM((SUBLANES, tm + CONV_REACH, LANES), F32)],
        compiler_params=pltpu.CompilerParams(dimension_semantics=("arbitrary",)),
    )(dhd, dhd, dhd, hg, hg, hg, w_dw)


def _swap16(y, lane):
    return jnp.where((lane & 16) == 0, pltpu.roll(y, LANES - 16, 1), pltpu.roll(y, 16, 1))


def _head_mean(v, bd):
    hi, lo = _split_bf16(v)
    return (_dot(hi, bd, NN) + _dot(lo, bd, NN)) * (1.0 / HEAD_DIM)


Q_COLS = (0, ATTN_WIDTH)
K_COLS = (ATTN_WIDTH, ATTN_WIDTH + HEAD_DIM * 2)
V_COLS = (K_COLS[1], K_COLS[1] + HEAD_DIM * 2)
SU_COLS = (V_COLS[1], V_COLS[1] + SG_WIDTH)
SV_COLS = (SU_COLS[1], SU_COLS[1] + SG_WIDTH)


def _mix_prep_fwd(p, ctx_rows, cos, sin, qg, kg, bd, w_sp, b_spt, name):
    TT = p.shape[0]
    off = ctx_rows // CHUNK
    q_scale = HEAD_DIM ** -0.5

    def body(p_ref, cos_ref, sin_ref, qg_ref, kg_ref, bd_ref, w_ref, b_ref,
             q_ref, kp_ref, vp_ref, kt_ref, sg_ref):
        lane = lax.broadcasted_iota(jnp.int32, (CHUNK, LANES), 1)
        low = lane < HEAD_DIM
        cs, sn, bdv = cos_ref[...], sin_ref[...], bd_ref[...]

        def norm_rope(xv, gain):
            r = lax.rsqrt(_head_mean(xv * xv, bdv) + EPS)
            yv = xv * r * gain
            return yv * cs + _swap16(yv, lane) * sn

        def pad_heads(ref, t):
            tr = pltpu.roll(t, HEAD_DIM, 1)
            ref[0, 0] = jnp.where(low, t, 0.0).astype(BF16)
            ref[0, 1] = jnp.where(low, 0.0, tr).astype(BF16)
            ref[1, 0] = jnp.where(low, tr, 0.0).astype(BF16)
            ref[1, 1] = jnp.where(low, 0.0, t).astype(BF16)

        for a in range(ATTN_WIDTH // LANES):
            xv = p_ref[:, a * LANES:(a + 1) * LANES]
            q_ref[:, a * LANES:(a + 1) * LANES] = (norm_rope(xv, qg_ref[...]) * q_scale).astype(BF16)
        kh = norm_rope(p_ref[:, K_COLS[0]:K_COLS[1]], kg_ref[...])
        pad_heads(kp_ref, kh)
        pad_heads(vp_ref, p_ref[:, V_COLS[0]:V_COLS[1]])
        kht = kh.T
        kt_ref[0] = kht[:HEAD_DIM].astype(BF16)
        kt_ref[1] = kht[HEAD_DIM:].astype(BF16)
        for g in range(N_SG_GROUPS):
            u = _gelu(p_ref[:, SU_COLS[0] + g * LANES:SU_COLS[0] + (g + 1) * LANES])
            vg = _gelu(p_ref[:, SV_COLS[0] + g * LANES:SV_COLS[0] + (g + 1) * LANES])
            xc = vg - jnp.mean(vg, axis=-1, keepdims=True)
            vn = xc * lax.rsqrt(jnp.mean(xc * xc, axis=-1, keepdims=True) + EPS)
            mixed = _dot(w_ref[g].astype(BF16), vn.astype(BF16), NN) + b_ref[:, g:g + 1]
            sg_ref[:, g * LANES:(g + 1) * LANES] = (u * mixed).astype(BF16)

    def row(width):
        return pl.BlockSpec((CHUNK, width), lambda i: (i, 0))

    def whole(shape):
        return pl.BlockSpec(shape, lambda i: (0,) * len(shape))

    pad_spec = pl.BlockSpec((2, 2, CHUNK, LANES), lambda i: (0, 0, i, 0))
    return pl.pallas_call(
        body, name=name, grid=(TT // CHUNK,),
        out_shape=[jax.ShapeDtypeStruct((TT, ATTN_WIDTH), BF16),
                   jax.ShapeDtypeStruct((2, 2, TT, LANES), BF16), jax.ShapeDtypeStruct((2, 2, TT, LANES), BF16),
                   jax.ShapeDtypeStruct((2, HEAD_DIM, TT), BF16),
                   jax.ShapeDtypeStruct((TT - ctx_rows, ATTN_WIDTH + SG_WIDTH), BF16)],
        in_specs=[row(IN_WIDTH), row(LANES), row(LANES), whole((1, LANES)), whole((1, LANES)),
                  whole((LANES, LANES)), whole((N_SG_GROUPS, CHUNK, CHUNK)), whole((CHUNK, N_SG_GROUPS))],
        out_specs=[row(ATTN_WIDTH), pad_spec, pad_spec,
                   pl.BlockSpec((2, HEAD_DIM, CHUNK), lambda i: (0, 0, i)),
                   pl.BlockSpec((CHUNK, SG_WIDTH), lambda i: (jnp.maximum(i - off, 0), 1))],
        compiler_params=pltpu.CompilerParams(dimension_semantics=("arbitrary",)),
    )(p, cos, sin, qg, kg, bd, w_sp, b_spt)


def _mix_prep_bwd(p, dq, f, dao, ctx_rows, cos, sin, qg, kg, bd, w_sp, w_spt, b_spt, name):
    TT = p.shape[0]
    off = ctx_rows // CHUNK
    q_scale = HEAD_DIM ** -0.5

    def body(p_ref, dq_ref, f_ref, dsg_ref, cos_ref, sin_ref, qg_ref, kg_ref, bd_ref, w_ref, wt_ref,
             b_ref, dp_ref, dqg_ref, dkg_ref, dw_ref, db_ref):
        i = pl.program_id(0)

        @pl.when(i == 0)
        def _():
            dqg_ref[...] = jnp.zeros_like(dqg_ref)
            dkg_ref[...] = jnp.zeros_like(dkg_ref)
            dw_ref[...] = jnp.zeros_like(dw_ref)
            db_ref[...] = jnp.zeros_like(db_ref)

        latent = (i >= off).astype(F32)
        lane = lax.broadcasted_iota(jnp.int32, (CHUNK, LANES), 1)
        low = lane < HEAD_DIM
        cs, sn, bdv = cos_ref[...], sin_ref[...], bd_ref[...]

        def fold(b0):
            return jnp.where(low, f_ref[0, b0] + pltpu.roll(f_ref[0, b0 + 1], HEAD_DIM, 1),
                             pltpu.roll(f_ref[1, b0], HEAD_DIM, 1) + f_ref[1, b0 + 1])

        def norm_rope_bwd(xv, dout, gain):
            r = lax.rsqrt(_head_mean(xv * xv, bdv) + EPS)
            n = xv * r
            dy = dout * cs + _swap16(dout * sn, lane)
            dn = dy * gain
            dx = r * (dn - n * _head_mean(dn * n, bdv))
            return dx, jnp.sum(dy * n, axis=0, keepdims=True)

        for a in range(ATTN_WIDTH // LANES):
            cols = slice(a * LANES, (a + 1) * LANES)
            dx, dg = norm_rope_bwd(p_ref[:, cols], dq_ref[:, cols] * (latent * q_scale), qg_ref[...])
            dp_ref[:, cols] = dx.astype(BF16)
            dqg_ref[...] += dg
        dx, dg = norm_rope_bwd(p_ref[:, K_COLS[0]:K_COLS[1]], fold(0), kg_ref[...])
        dp_ref[:, K_COLS[0]:K_COLS[1]] = dx.astype(BF16)
        dkg_ref[...] += dg
        dp_ref[:, V_COLS[0]:V_COLS[1]] = fold(2).astype(BF16)
        for g in range(N_SG_GROUPS):
            su = p_ref[:, SU_COLS[0] + g * LANES:SU_COLS[0] + (g + 1) * LANES]
            sv = p_ref[:, SV_COLS[0] + g * LANES:SV_COLS[0] + (g + 1) * LANES]
            (u, dgelu_su), (vg, dgelu_sv) = _gelu_and_grad(su), _gelu_and_grad(sv)
            xc = vg - jnp.mean(vg, axis=-1, keepdims=True)
            rs = lax.rsqrt(jnp.mean(xc * xc, axis=-1, keepdims=True) + EPS)
            vn = xc * rs
            vnb = vn.astype(BF16)
            mixed = _dot(w_ref[g].astype(BF16), vnb, NN) + b_ref[:, g:g + 1]
            dsg = dsg_ref[:, g * LANES:(g + 1) * LANES].astype(F32) * latent
            du = dsg * mixed
            dmix = dsg * u
            dmb = dmix.astype(BF16)
            db_ref[:, g:g + 1] += jnp.sum(dmix, axis=-1, keepdims=True)
            dw_ref[g] += _dot(dmb, vnb, NT)
            dvn = _dot(wt_ref[g].astype(BF16), dmb, NN)
            dvg = rs * (dvn - jnp.mean(dvn, axis=-1, keepdims=True)
                        - vn * jnp.mean(dvn * vn, axis=-1, keepdims=True))
            dp_ref[:, SU_COLS[0] + g * LANES:SU_COLS[0] + (g + 1) * LANES] = (du * dgelu_su).astype(BF16)
            dp_ref[:, SV_COLS[0] + g * LANES:SV_COLS[0] + (g + 1) * LANES] = (dvg * dgelu_sv).astype(BF16)

    def row(width):
        return pl.BlockSpec((CHUNK, width), lambda i: (i, 0))

    def latent_row(width, col_block):
        return pl.BlockSpec((CHUNK, width), lambda i: (jnp.maximum(i - off, 0), col_block))

    def whole(shape):
        return pl.BlockSpec(shape, lambda i: (0,) * len(shape))

    return pl.pallas_call(
        body, name=name, grid=(TT // CHUNK,),
        out_shape=[jax.ShapeDtypeStruct((TT, IN_WIDTH), BF16), jax.ShapeDtypeStruct((1, LANES), F32),
                   jax.ShapeDtypeStruct((1, LANES), F32),
                   jax.ShapeDtypeStruct((N_SG_GROUPS, CHUNK, CHUNK), F32),
                   jax.ShapeDtypeStruct((CHUNK, N_SG_GROUPS), F32)],
        in_specs=[row(IN_WIDTH), latent_row(ATTN_WIDTH, 0),
                  pl.BlockSpec((2, 4, CHUNK, LANES), lambda i: (0, 0, i, 0)),
                  latent_row(SG_WIDTH, 1), row(LANES), row(LANES), whole((1, LANES)), whole((1, LANES)),
                  whole((LANES, LANES)), whole((N_SG_GROUPS, CHUNK, CHUNK)),
                  whole((N_SG_GROUPS, CHUNK, CHUNK)), whole((CHUNK, N_SG_GROUPS))],
        out_specs=[row(IN_WIDTH), whole((1, LANES)), whole((1, LANES)),
                   whole((N_SG_GROUPS, CHUNK, CHUNK)), whole((CHUNK, N_SG_GROUPS))],
        compiler_params=pltpu.CompilerParams(dimension_semantics=("arbitrary",)),
    )(p, dq, f, dao, cos, sin, qg, kg, bd, w_sp, w_spt, b_spt)


def _attn_fwd(q, kpad, vpad, ao, ctx_rows, name, tq=256):
    TT = q.shape[0]
    T = TT - ctx_rows
    tq = _tile(T, tq)
    off = ctx_rows // tq
    group = 2 * LANES

    def body(q_ref, k_ref, v_ref, ao_in, o_ref, lse_ref):
        del ao_in
        lane = lax.broadcasted_iota(jnp.int32, (tq, LANES), 1)
        lse = jnp.zeros((tq, LANES), F32)
        for a in range(2):
            acc = jnp.zeros((tq, LANES), F32)
            qa = q_ref[:, a * LANES:(a + 1) * LANES]
            for b in range(2):
                s = _dot(qa, k_ref[0, b], NT)
                m = jnp.max(s, axis=-1, keepdims=True)
                e = jnp.exp(s - m)
                l = jnp.sum(e, axis=-1, keepdims=True)
                acc = acc + _dot(e.astype(BF16), v_ref[0, b], NN) * (1.0 / l)
                lse = jnp.where(lane == 2 * a + b, m + jnp.log(l), lse)
            o_ref[:, a * LANES:(a + 1) * LANES] = acc.astype(BF16)
        lse_ref[0] = lse

    kv_spec = pl.BlockSpec((1, 2, TT, LANES), lambda j, i: (j, 0, 0, 0))
    return pl.pallas_call(
        body, name=name, grid=(2, T // tq),
        out_shape=[jax.ShapeDtypeStruct(ao.shape, BF16), jax.ShapeDtypeStruct((2, T, LANES), F32)],
        in_specs=[pl.BlockSpec((tq, group), lambda j, i: (i + off, j)), kv_spec, kv_spec,
                  pl.BlockSpec(memory_space=pl.ANY)],
        out_specs=[pl.BlockSpec((tq, group), lambda j, i: (i, j)),
                   pl.BlockSpec((1, tq, LANES), lambda j, i: (j, i, 0))],
        input_output_aliases={3: 0},
        compiler_params=pltpu.CompilerParams(dimension_semantics=("parallel", "parallel")),
    )(q, kpad, vpad, ao)


def _attn_bwd(q, dao, ao, lse, kpad, vpad, kt, ctx_rows, name, tq=256):
    TT = q.shape[0]
    T = TT - ctx_rows
    tq = _tile(T, tq)
    off = ctx_rows // tq
    group = 2 * LANES

    def body(q_ref, do_ref, o_ref, lse_ref, k_ref, v_ref, kt_ref, dq_ref, f_ref):
        i = pl.program_id(1)

        @pl.when(i == 0)
        def _():
            f_ref[...] = jnp.zeros_like(f_ref)

        ktv = kt_ref[0]
        lse_t = lse_ref[0].T
        row = lax.broadcasted_iota(jnp.int32, (SUBLANES, LANES), 0)
        lane = lax.broadcasted_iota(jnp.int32, (SUBLANES, LANES), 1)
        half_ones = (jnp.where(lane < HEAD_DIM, 0, 1) == row).astype(BF16)
        for a in range(2):
            cols = slice(a * LANES, (a + 1) * LANES)
            qa = q_ref[:, cols]
            do32 = do_ref[:, cols].astype(F32)
            doa = do32.astype(BF16)
            hi, lo = _split_bf16(do32 * o_ref[:, cols].astype(F32))
            deltas = _dot(half_ones, hi, NT) + _dot(half_ones, lo, NT)
            halves = []
            for b in range(2):
                h = 2 * a + b
                st = _dot(k_ref[0, b], qa, NT)
                pt = jnp.exp(st - lse_t[h:h + 1, :])
                dpt = _dot(v_ref[0, b], doa, NT)
                dst = (pt * (dpt - deltas[b:b + 1, :])).astype(BF16)
                f_ref[0, b] += _dot(dst, qa, NN)
                f_ref[0, 2 + b] += _dot(pt.astype(BF16), doa, NN)
                halves.append(_dot(ktv, dst, NN))
            dq_ref[:, cols] = jnp.concatenate(halves, axis=0).T

    kv_spec = pl.BlockSpec((1, 2, TT, LANES), lambda j, i: (j, 0, 0, 0))
    out_cols = pl.BlockSpec((tq, group), lambda j, i: (i, j))
    return pl.pallas_call(
        body, name=name, grid=(2, T // tq),
        out_shape=[jax.ShapeDtypeStruct((T, ATTN_WIDTH), F32), jax.ShapeDtypeStruct((2, 4, TT, LANES), F32)],
        in_specs=[pl.BlockSpec((tq, group), lambda j, i: (i + off, j)), out_cols, out_cols,
                  pl.BlockSpec((1, tq, LANES), lambda j, i: (j, i, 0)),
                  kv_spec, kv_spec, pl.BlockSpec((1, HEAD_DIM, TT), lambda j, i: (j, 0, 0))],
        out_specs=[out_cols, pl.BlockSpec((1, 4, TT, LANES), lambda j, i: (j, 0, 0, 0))],
        compiler_params=pltpu.CompilerParams(dimension_semantics=("parallel", "arbitrary")),
    )(q, dao, ao, lse, kpad, vpad, kt)


def _final_fwd_bwd(h, g, target, y, gt, name):
    R, Dm = h.shape
    tm = _tile(R, ROW_BLOCK, 8)

    def body(h_ref, g_ref, t_ref, y_ref, gt_ref, dh_ref, loss_ref, dg_ref, dy_ref, dgt_ref, dsum_ref):
        i = pl.program_id(0)

        @pl.when(i == 0)
        def _():
            for ref in (loss_ref, dg_ref, dgt_ref, dsum_ref):
                ref[...] = jnp.zeros_like(ref)

        hv = h_ref[...]
        r = lax.rsqrt(jnp.mean(hv * hv, axis=-1, keepdims=True) + EPS)
        n = hv * r
        diff = n * g_ref[...] - t_ref[...]
        loss_ref[...] += jnp.sum(diff * diff)
        dout = diff * (1.0 / Dm)
        dg_ref[...] += jnp.sum(dout * n, axis=0, keepdims=True)
        dn = dout * g_ref[...]
        dh = r * (dn - n * jnp.mean(dn * n, axis=-1, keepdims=True))
        dh_ref[...] = dh
        _gate_grads(dh, y_ref, gt_ref, dy_ref, dgt_ref, dsum_ref)

    vec = jax.ShapeDtypeStruct((1, Dm), F32)
    return pl.pallas_call(
        body, name=name, grid=(R // tm,),
        out_shape=[jax.ShapeDtypeStruct((R, Dm), F32), jax.ShapeDtypeStruct((1, LANES), F32), vec,
                   jax.ShapeDtypeStruct((R, Dm), BF16), vec, vec],
        in_specs=[_row_spec(tm, Dm), _vec_spec(Dm), _row_spec(tm, Dm), _row_spec(tm, Dm), _vec_spec(Dm)],
        out_specs=[_row_spec(tm, Dm), _vec_spec(LANES), _vec_spec(Dm), _row_spec(tm, Dm), _vec_spec(Dm),
                   _vec_spec(Dm)],
        compiler_params=pltpu.CompilerParams(dimension_semantics=("arbitrary",)),
    )(h, g, target, y, gt)


MOD_ROWS = 16


def _mod_fwd(c_rows, w_mod, name):
    L, Dm, n = w_mod.shape

    def body(c_ref, w_ref, o_ref):
        o_ref[0] = _dot3(_silu(c_ref[...]), w_ref[0], NN)

    return pl.pallas_call(
        body, name=name, grid=(L,),
        out_shape=jax.ShapeDtypeStruct((L, MOD_ROWS, n), F32),
        in_specs=[pl.BlockSpec((MOD_ROWS, Dm), lambda l: (0, 0)), pl.BlockSpec((1, Dm, n), lambda l: (l, 0, 0))],
        out_specs=pl.BlockSpec((1, MOD_ROWS, n), lambda l: (l, 0, 0)),
        compiler_params=pltpu.CompilerParams(dimension_semantics=("parallel",)),
    )(c_rows, w_mod)


def _mod_bwd(c_rows_t, dmod, w_mod, name):
    L, Dm, n = w_mod.shape

    def body(ct_ref, d_ref, w_ref, gw_ref, ds_ref):
        dm = d_ref[0]
        gw_ref[0] = _dot3(_silu(ct_ref[...]), dm, NN)
        ds_ref[0] = _dot3(dm[:MOD_ROWS], w_ref[0], NT)

    return pl.pallas_call(
        body, name=name, grid=(L,),
        out_shape=[jax.ShapeDtypeStruct((L, Dm, n), F32), jax.ShapeDtypeStruct((L, MOD_ROWS, Dm), F32)],
        in_specs=[pl.BlockSpec((Dm, LANES), lambda l: (0, 0)), pl.BlockSpec((1, LANES, n), lambda l: (l, 0, 0)),
                  pl.BlockSpec((1, Dm, n), lambda l: (l, 0, 0))],
        out_specs=[pl.BlockSpec((1, Dm, n), lambda l: (l, 0, 0)),
                   pl.BlockSpec((1, MOD_ROWS, Dm), lambda l: (l, 0, 0))],
        compiler_params=pltpu.CompilerParams(dimension_semantics=("parallel",)),
    )(c_rows_t, dmod, w_mod)


def _adam_update(w, g, m, v):
    c1 = 1.0 - ADAM_B1 ** ADAM_STEP
    c2 = 1.0 - ADAM_B2 ** ADAM_STEP
    mn = ADAM_B1 * m + (1.0 - ADAM_B1) * g
    vn = ADAM_B2 * v + (1.0 - ADAM_B2) * (g * g)
    return -ADAM_LR * ((mn / c1) / (jnp.sqrt(vn / c2) + ADAM_EPS) + ADAM_WD * w), mn, vn


def _adamw(w, g, m, v, name):
    R, Cw = w.shape
    tm = _tile(R, ADAM_ROWS, 8)

    def body(w_ref, g_ref, m_ref, v_ref, d_ref, mo_ref, vo_ref):
        d_ref[...], mo_ref[...], vo_ref[...] = _adam_update(w_ref[...], g_ref[...], m_ref[...], v_ref[...])

    spec = pl.BlockSpec((tm, Cw), lambda i: (i, 0))
    return pl.pallas_call(
        body, name=name, grid=(R // tm,),
        out_shape=[jax.ShapeDtypeStruct((R, Cw), F32)] * 3,
        in_specs=[spec] * 4, out_specs=[spec] * 3,
        compiler_params=pltpu.CompilerParams(dimension_semantics=("parallel",)),
    )(w, g, m, v)


def _adamw_recv(w, m, v, recvs, name):
    L, R, n = w.shape
    tm = _tile(R, ADAM_ROWS, 8)
    nblk = R // tm
    parts = [r.reshape(N_DEV, R, n) for r in recvs]

    def body(*refs):
        w_ref, m_ref, v_ref = refs[:3]
        part_refs = refs[3:3 + L]
        g_ref, d_ref, mo_ref, vo_ref, gsum = refs[3 + L:]
        l = pl.program_id(0)
        for ll in range(L):
            @pl.when(l == ll)
            def _(ll=ll):
                acc = part_refs[ll][0].astype(F32)
                for s in range(1, N_DEV):
                    acc = acc + part_refs[ll][s].astype(F32)
                gsum[...] = acc
        g = gsum[...]
        g_ref[0] = g
        d_ref[0], mo_ref[0], vo_ref[0] = _adam_update(w_ref[0], g, m_ref[0], v_ref[0])

    def part_spec(ll):
        return pl.BlockSpec((N_DEV, tm, n), lambda l, i: (0, jnp.where(l == ll, i, jnp.where(l < ll, 0, nblk - 1)), 0))

    spec = pl.BlockSpec((1, tm, n), lambda l, i: (l, i, 0))
    return pl.pallas_call(
        body, name=name, grid=(L, nblk),
        out_shape=[jax.ShapeDtypeStruct((L, R, n), F32)] * 4,
        in_specs=[spec] * 3 + [part_spec(ll) for ll in range(L)], out_specs=[spec] * 4,
        scratch_shapes=[pltpu.VMEM((tm, n), F32)],
        compiler_params=pltpu.CompilerParams(dimension_semantics=("parallel", "parallel")),
    )(w, m, v, *parts)


def _pack(parts, row_mult=8):
    flat, offs, pos = [], [], 0
    for t in parts:
        t = t.reshape(-1).astype(F32)
        size = -(-t.shape[0] // LANES) * LANES
        flat.append(jnp.pad(t, (0, size - t.shape[0])))
        offs.append(pos)
        pos += size
    total = -(-pos // (LANES * row_mult)) * (LANES * row_mult)
    if total > pos:
        flat.append(jnp.zeros((total - pos,), F32))
    return jnp.concatenate(flat).reshape(-1, LANES), offs


def _take(buf, off, shape):
    size = math.prod(shape)
    return buf[..., off:off + size].reshape(buf.shape[:-1] + tuple(shape))


def _rope_tables(T, ctx_rows):
    pos = jnp.arange(T)
    row = (pos // GRID_W).astype(F32)
    col = (pos % GRID_W).astype(F32)
    half = HEAD_DIM // 4
    inv = ROPE_THETA ** (-jnp.arange(0, 2 * half, 2, dtype=F32) / (2 * half))
    ang_r, ang_c = row[:, None] * inv[None, :], col[:, None] * inv[None, :]
    cos = jnp.concatenate([jnp.cos(ang_r)] * 2 + [jnp.cos(ang_c)] * 2, axis=1)
    sin = jnp.concatenate([-jnp.sin(ang_r), jnp.sin(ang_r), -jnp.sin(ang_c), jnp.sin(ang_c)], axis=1)
    cos = jnp.concatenate([jnp.ones((ctx_rows, HEAD_DIM), F32), cos], axis=0)
    sin = jnp.concatenate([jnp.zeros((ctx_rows, HEAD_DIM), F32), sin], axis=0)
    return jnp.tile(cos, (1, 2)), jnp.tile(sin, (1, 2))


def kernel(x, c, ctx, c_ctx, w_mod, b_mod, g_mix, g_ffn, w_ffn_in, w_ffn_out, w_in, q_gain, k_gain, w_sp, b_sp, w_out, w_pw1, b_pw1, w_dw, b_dw, ln_g, ln_b, w_pw2, b_pw2, g_final, loss_target, m_c_ctx, m_w_mod, m_b_mod, m_g_mix, m_g_ffn, m_w_ffn_in, m_w_ffn_out, m_w_in, m_q_gain, m_k_gain, m_w_sp, m_b_sp, m_w_out, m_w_pw1, m_b_pw1, m_w_dw, m_b_dw, m_ln_g, m_ln_b, m_w_pw2, m_b_pw2, m_g_final, v_c_ctx, v_w_mod, v_b_mod, v_g_mix, v_g_ffn, v_w_ffn_in, v_w_ffn_out, v_w_in, v_q_gain, v_k_gain, v_w_sp, v_b_sp, v_w_out, v_w_pw1, v_b_pw1, v_w_dw, v_b_dw, v_ln_g, v_ln_b, v_w_pw2, v_b_pw2, v_g_final):
    weights = dict(c_ctx=c_ctx, w_mod=w_mod, b_mod=b_mod, g_mix=g_mix, g_ffn=g_ffn, w_ffn_in=w_ffn_in,
                   w_ffn_out=w_ffn_out, w_in=w_in, q_gain=q_gain, k_gain=k_gain, w_sp=w_sp, b_sp=b_sp,
                   w_out=w_out, w_pw1=w_pw1, b_pw1=b_pw1, w_dw=w_dw, b_dw=b_dw, ln_g=ln_g, ln_b=ln_b,
                   w_pw2=w_pw2, b_pw2=b_pw2, g_final=g_final)
    moments_m = dict(c_ctx=m_c_ctx, w_mod=m_w_mod, b_mod=m_b_mod, g_mix=m_g_mix, g_ffn=m_g_ffn,
                     w_ffn_in=m_w_ffn_in, w_ffn_out=m_w_ffn_out, w_in=m_w_in, q_gain=m_q_gain,
                     k_gain=m_k_gain, w_sp=m_w_sp, b_sp=m_b_sp, w_out=m_w_out, w_pw1=m_w_pw1,
                     b_pw1=m_b_pw1, w_dw=m_w_dw, b_dw=m_b_dw, ln_g=m_ln_g, ln_b=m_ln_b, w_pw2=m_w_pw2,
                     b_pw2=m_b_pw2, g_final=m_g_final)
    moments_v = dict(c_ctx=v_c_ctx, w_mod=v_w_mod, b_mod=v_b_mod, g_mix=v_g_mix, g_ffn=v_g_ffn,
                     w_ffn_in=v_w_ffn_in, w_ffn_out=v_w_ffn_out, w_in=v_w_in, q_gain=v_q_gain,
                     k_gain=v_k_gain, w_sp=v_w_sp, b_sp=v_b_sp, w_out=v_w_out, w_pw1=v_w_pw1,
                     b_pw1=v_b_pw1, w_dw=v_w_dw, b_dw=v_b_dw, ln_g=v_ln_g, ln_b=v_ln_b, w_pw2=v_w_pw2,
                     b_pw2=v_b_pw2, g_final=v_g_final)
    names = list(weights)

    T, C = x.shape[1], ctx.shape[1]
    Dm = D_MODEL
    me = 4 * lax.axis_index("x") + 2 * lax.axis_index("y") + lax.axis_index("c")
    h0 = x[0]
    ctx2 = ctx[0]
    target = loss_target[0]

    small_sharded = (("w_dw", w_dw[0]), ("b_pw1", b_pw1), ("b_dw", b_dw), ("ln_g", ln_g), ("ln_b", ln_b),
                     ("b_pw2", b_pw2))
    buf1, offs1 = _pack([c] + [t for _, t in small_sharded])
    w_in_t, m_w_in_t, v_w_in_t = (jnp.swapaxes(t, 1, 2) for t in (w_in, m_w_in, v_w_in))
    w_ffi_t, m_w_ffi_t, v_w_ffi_t = (jnp.swapaxes(t, 1, 2) for t in (w_ffn_in, m_w_ffn_in, v_w_ffn_in))
    got1, W_in_t = _all_gather([buf1, w_in_t[0].astype(BF16)], "gather_cond", False)
    got1 = got1.reshape(N_DEV, -1)
    c_all = _take(got1, offs1[0], (Dm,))
    full_small = {}
    for (nm, t), off in zip(small_sharded, offs1[1:]):
        seg = _take(got1, off, t.shape)
        full_small[nm] = jnp.moveaxis(seg, 0, -2).reshape(t.shape[:-1] + (N_DEV * t.shape[-1],))
    w_dw_f, b_pw1_f = full_small["w_dw"], full_small["b_pw1"]
    b_dw_f, ln_g_f, ln_b_f, b_pw2_f = (full_small[k] for k in ("b_dw", "ln_g", "ln_b", "b_pw2"))

    c_rows = jnp.concatenate([c_all, c_ctx[None, :], jnp.zeros((MOD_ROWS - N_DEV - 1, Dm), F32)], axis=0)
    mod_part = _mod_fwd(c_rows, w_mod, "mod_fwd")
    n_mod = w_mod.shape[2]
    got2 = _all_gather([mod_part.reshape(-1, LANES)], "gather_mod", True)[0]
    mod_all = got2.reshape(N_DEV, 2, MOD_ROWS, n_mod).transpose(1, 2, 0, 3).reshape(2, MOD_ROWS, N_DEV * n_mod)
    mod_all = mod_all + b_mod[:, None, :]
    my_mod = lax.dynamic_index_in_dim(mod_all, me, axis=1, keepdims=False)
    sh1, sc1, gt1, sh2, sc2, gt2 = ([my_mod[l:l + 1, k * Dm:(k + 1) * Dm] for l in range(2)] for k in range(6))
    csh1 = mod_all[0, N_DEV:N_DEV + 1, 0:Dm]
    csc1 = mod_all[0, N_DEV:N_DEV + 1, Dm:2 * Dm]

    behind = got2[0:1, 0:1] * 0.0
    gather_groups = [[w_out[0]], [w_ffi_t[0], w_ffn_out[0]], [w_pw1[0], w_pw2[0]], [w_ffi_t[1], w_ffn_out[1]]]
    gathers = [_push_begin([(t + behind).astype(BF16) for t in grp], True, f"gather_start{k}")
               for k, grp in enumerate(gather_groups)]
    started = sum(h[4][0:1, 0:1] for h in gathers)

    def gathered(k, after):
        return _push_end(gathers[k], after, f"gather_wait{k}")[1]

    def ffn_weights(k, after):
        wi, wo = gathered(k, after)
        return wi.reshape(N_DEV, FF_SHARD, Dm), wo.reshape(N_DEV // 2, FF_SHARD, Dm)

    def col_gathered(t, n):
        return t.reshape(N_DEV, Dm, n).transpose(1, 0, 2).reshape(Dm, N_DEV * n)

    W_ffi, W_ffo = [None, None], [None, None]

    g_mix_r = [g_mix[l:l + 1] for l in range(2)]
    g_ffn_r = [g_ffn[l:l + 1] for l in range(2)]
    g_fin = g_final[None, :]

    cos, sin = _rope_tables(T, C)
    qg = jnp.tile(q_gain, (1, 2))
    kg = jnp.tile(k_gain, (1, 2))
    lane_head = jnp.arange(LANES) // HEAD_DIM
    bd = (lane_head[:, None] == lane_head[None, :]).astype(BF16)
    w_sp0 = w_sp[0]
    w_spt0 = w_sp0.transpose(0, 2, 1)
    b_spt0 = b_sp[0].T

    XM = _norm_mod_fwd_cat(ctx2, h0, g_mix_r[0], csc1, csh1, sc1[0] + started, sh1[0], "norm_mix0")
    P = _mm(XM, W_in_t, "nt", "in_proj", tm=1088, tn=IN_WIDTH)
    qh, kpad, vpad, kt, ao = _mix_prep_fwd(P, C, cos, sin, qg, kg, bd, w_sp0, b_spt0, "mix_prep")
    ao, lse = _attn_fwd(qh, kpad, vpad, ao, C, "attn_fwd")
    W_out, = gathered(0, ao)
    h1, y0, xf0 = _mm(ao, W_out, "nn", "out_proj", res=h0, gate=gt1[0], raw_out=True,
                      norm=(g_ffn_r[0], sc2[0], sh2[0]))

    def ffn_fwd(h_in, xf, l, norm_next):
        W_ffi[l], W_ffo[l] = ffn_weights(1 + 2 * l, xf)
        gu, act = _ffn_in_swiglu(xf, W_ffi[l], f"ffn_in{l}")
        outs = _mm_sum_shards(act, W_ffo[l], "nn", f"ffn_out{l}", res=h_in, gate=gt2[l], raw_out=True,
                              norm=norm_next)
        return tuple(outs) + (None,) * (3 - len(outs)) + (gu, act)

    h2, f0, xm1, gu0, act0 = ffn_fwd(h1, xf0, 0, (g_mix_r[1], sc1[1], sh1[1]))

    W_pw1, W_pw2 = gathered(2, xm1)
    W_pw1 = col_gathered(W_pw1, 2 * Dm // N_DEV)
    ag = _mm(xm1, W_pw1, "nn", "pw1", BF16, bias=b_pw1_f)
    hg = _glu_fwd(ag, "glu")
    hd = _conv_fwd(hg, w_dw_f, b_dw_f, "conv")
    hs = _ln_silu_fwd(hd, ln_g_f, ln_b_f, "ln_silu")
    h3, y1, xf1 = _mm(hs, W_pw2, "nn", "pw2", bias=b_pw2_f, res=h2, gate=gt1[1], raw_out=True,
                      norm=(g_ffn_r[1], sc2[1], sh2[1]))
    h4, f1, _, gu1, act1 = ffn_fwd(h3, xf1, 1, None)

    dh4, sq_err, dg_final, df1, dgt2_1, _ = _final_fwd_bwd(h4, g_fin, target, f1, gt2[1], "loss_head")
    loss_local = (0.5 / Dm) * sq_err[0, 0:1]

    def col_shards(g, n):
        return g.reshape(Dm, N_DEV, n).transpose(1, 0, 2).reshape(N_DEV * Dm, n)

    def exchange_begin(k, parts):
        return _push_begin(parts, False, f"exchange_start{k}")

    def zero_of(handle):
        return handle[4][0:1, 0:1]

    def ffn_bwd(df, xf, gu, act, l):
        dw_out = _mm_tn_shard_rows(act, df, f"ffn_out_dw{l}", BF16)
        dgu = _ffn_out_dx_swiglu(df, W_ffo[l], gu, f"ffn_out_dx{l}").reshape(N_DEV, T, FF_SHARD)
        dw_in = _mm_tn_shard_rows(dgu, xf, f"ffn_in_dw{l}", BF16)
        dxf = _mm_sum_shards(dgu, W_ffi[l], "nn", f"ffn_in_dx{l}", BF16, tm=512)
        return dw_in, dw_out, dxf

    dW_ffi1, dW_ffo1, dxf1 = ffn_bwd(df1, xf1, gu1, act1, 1)
    ex0 = exchange_begin(0, [dW_ffi1.reshape(2 * D_FF, Dm), dW_ffo1.reshape(D_FF, Dm)])
    dh3, da, dsh, dy1, dgt1_1, db_pw2 = _norm_mod_bwd(h3, g_ffn_r[1], sc2[1], dxf1, dh4, "norm_ffn_bwd1",
                                                       gate=(y1, gt1[1] + zero_of(ex0)))
    dmod_ffn1 = (dsh, da * g_ffn_r[1], dgt2_1)
    dg_ffn1 = da * (1.0 + sc2[1])

    dW_pw2 = _mm(hs, dy1, "tn", "pw2_dw", BF16, tk=2048)
    dhs = _mm(dy1, W_pw2, "nt", "pw2_dx", BF16)
    dhd, dln_g, dln_b, db_dw = _ln_silu_bwd(dhs, hd, ln_g_f, ln_b_f, "ln_silu_bwd")
    dhg, dw_dw = _conv_bwd(dhd, hg, w_dw_f, "conv_bwd")
    dag, db_pw1 = _glu_bwd(ag, dhg, "glu_bwd")
    dW_pw1 = _mm(xm1, dag, "tn", "pw1_dw", BF16, tk=2048)
    dxm1 = _mm(dag, W_pw1, "nt", "pw1_dx", BF16, tk=2048)
    ex1 = exchange_begin(1, [col_shards(dW_pw1, 2 * Dm // N_DEV), dW_pw2])
    dh2, da, dsh, df0, dgt2_0, _ = _norm_mod_bwd(h2, g_mix_r[1], sc1[1], dxm1, dh3, "norm_mix1_bwd",
                                                 gate=(f0, gt2[0] + zero_of(ex1)))
    dmod_mix1 = (dsh, da * g_mix_r[1], dgt1_1)
    dg_mix1 = da * (1.0 + sc1[1])

    dW_ffi0, dW_ffo0, dxf0 = ffn_bwd(df0, xf0, gu0, act0, 0)
    ex2 = exchange_begin(2, [dW_ffi0.reshape(2 * D_FF, Dm), dW_ffo0.reshape(D_FF, Dm)])
    dh1, da, dsh, dy0, dgt1_0, _ = _norm_mod_bwd(h1, g_ffn_r[0], sc2[0], dxf0, dh2, "norm_ffn_bwd0",
                                                 gate=(y0, gt1[0] + zero_of(ex2)))
    dmod_ffn0 = (dsh, da * g_ffn_r[0], dgt2_0)
    dg_ffn0 = da * (1.0 + sc2[0])

    dW_out = _mm(ao, dy0, "tn", "out_proj_dw", BF16, tk=2048)
    ex_out = exchange_begin(4, [dW_out])
    dao = _mm(dy0, W_out + zero_of(ex_out).astype(BF16), "nt", "out_proj_dx", BF16)
    dq, f_acc = _attn_bwd(qh, dao, ao, lse, kpad, vpad, kt, C, "attn_bwd")
    dP, dqg, dkg, dw_sp0, db_spt0 = _mix_prep_bwd(P, dq, f_acc, dao, C, cos, sin, qg, kg, bd, w_sp0, w_spt0,
                                                  b_spt0, "mix_prep_bwd")
    dW_in_t = _mm(dP, XM, "tn", "in_proj_dw", BF16, tm=896, tk=2176)
    ex3 = exchange_begin(3, [dW_in_t])
    dXM = _mm(dP, W_in_t, "nn", "in_proj_dx", BF16, tm=1088, tk=IN_WIDTH)
    dh0, da, dsh = _norm_mod_bwd(h0, g_mix_r[0], sc1[0] + zero_of(ex3), dXM, dh1, "norm_mix0_bwd", dxm_row_off=C)
    _, dac, dcsh = _norm_mod_bwd(ctx2, g_mix_r[0], csc1, dXM, None, "norm_ctx_bwd")
    dmod_mix0 = (dsh, da * g_mix_r[0], dgt1_0)
    dg_mix0 = da * (1.0 + sc1[0]) + dac * (1.0 + csc1)
    dcmod = jnp.concatenate([dcsh, dac * g_mix_r[0]], axis=1)

    dmod_mine = jnp.stack([jnp.concatenate(dmod_mix0 + dmod_ffn0, axis=1)[0],
                           jnp.concatenate(dmod_mix1 + dmod_ffn1, axis=1)[0]])

    small_grads = [
        ("loss", loss_local), ("g_final", dg_final), ("g_mix", jnp.concatenate([dg_mix0, dg_mix1])),
        ("g_ffn", jnp.concatenate([dg_ffn0, dg_ffn1])),
        ("q_gain", dqg[:, :HEAD_DIM] + dqg[:, HEAD_DIM:]), ("k_gain", dkg[:, :HEAD_DIM] + dkg[:, HEAD_DIM:]),
        ("w_sp", dw_sp0[None]), ("b_sp", db_spt0.T[None]), ("b_pw1", db_pw1), ("w_dw", dw_dw[None]),
        ("b_dw", db_dw), ("ln_g", dln_g), ("ln_b", dln_b), ("b_pw2", db_pw2), ("dcmod", dcmod),
        ("dmod", dmod_mine),
    ]
    buf3, offs3 = _pack([t for _, t in small_grads])
    off3 = {nm: off for (nm, _), off in zip(small_grads, offs3)}
    shape3 = {nm: t.shape for nm, t in small_grads}
    small_push = _push_begin([buf3], True, "small_grads_start")

    grads, delta, new_m, new_v = {}, {}, {}, {}

    def exchanged(k, handle, after):
        return _push_end(handle, after, f"exchange_wait{k}")[1]

    def adamw_big(nm, parts, transposed=False, wmv=None):
        w3, m3, v3 = wmv if wmv is not None else (weights[nm], moments_m[nm], moments_v[nm])
        outs4 = _adamw_recv(w3, m3, v3, parts, f"adamw_{nm}")
        if transposed:
            outs4 = [jnp.swapaxes(t, 1, 2) for t in outs4]
        grads[nm], delta[nm], new_m[nm], new_v[nm] = outs4

    pushed = small_push[4]
    r_ffi1, r_ffo1 = exchanged(0, ex0, pushed)
    r_pw1, r_pw2 = exchanged(1, ex1, pushed)
    r_ffi0, r_ffo0 = exchanged(2, ex2, pushed)
    r_out, = exchanged(4, ex_out, pushed)
    adamw_big("w_ffn_in", [r_ffi0, r_ffi1], True, (w_ffi_t, m_w_ffi_t, v_w_ffi_t))
    adamw_big("w_ffn_out", [r_ffo0, r_ffo1])
    adamw_big("w_pw1", [r_pw1])
    adamw_big("w_pw2", [r_pw2])
    adamw_big("w_out", [r_out])

    got3 = _push_end(small_push, delta["w_out"], "small_grads_wait")[1][0].reshape(N_DEV, buf3.shape[0], LANES)
    sum3 = _sum_devices(got3, "sum_small_grads").reshape(-1)

    def summed(nm):
        return _take(sum3, off3[nm], shape3[nm])

    loss = summed("loss")[0]
    dcmod_sum = summed("dcmod")
    dmod_rows = _take(got3.reshape(N_DEV, -1), off3["dmod"], (2, 6 * Dm)).transpose(1, 0, 2)
    ctx_row = jnp.concatenate([jnp.pad(dcmod_sum, ((0, 0), (0, 4 * Dm))), jnp.zeros((1, 6 * Dm), F32)])
    dmod_all = jnp.concatenate([dmod_rows, ctx_row[:, None, :],
                                jnp.zeros((2, LANES - N_DEV - 1, 6 * Dm), F32)], axis=1)
    grads["b_mod"] = summed("dmod") + ctx_row
    dmod_shard = lax.dynamic_slice_in_dim(dmod_all, me * n_mod, n_mod, axis=2)
    c_rows_t = jnp.pad(c_rows.T, ((0, 0), (0, LANES - MOD_ROWS)))
    grads["w_mod"], ds_part = _mod_bwd(c_rows_t, dmod_shard, w_mod, "mod_bwd")

    buf4, _ = _pack([ds_part[0, N_DEV]])
    got4 = _all_gather([buf4], "gather_c_ctx_grad", True)[0].reshape(N_DEV, buf4.shape[0], LANES)
    ds_ctx = _sum_devices(got4, "sum_c_ctx_grad").reshape(-1)[:Dm]
    grads["c_ctx"] = ds_ctx * _dsilu(c_ctx)

    for nm in ("g_final", "g_mix", "g_ffn", "q_gain", "k_gain", "w_sp", "b_sp"):
        grads[nm] = summed(nm).reshape(weights[nm].shape)
    for nm in ("b_pw1", "w_dw", "b_dw", "ln_g", "ln_b", "b_pw2"):
        n_loc = weights[nm].shape[-1]
        grads[nm] = lax.dynamic_slice_in_dim(summed(nm), me * n_loc, n_loc, axis=-1).reshape(weights[nm].shape)

    shp = w_mod.shape
    outs = _adamw(w_mod.reshape(-1, shp[-1]), grads["w_mod"].reshape(-1, shp[-1]),
                  m_w_mod.reshape(-1, shp[-1]), v_w_mod.reshape(-1, shp[-1]), "adamw_w_mod")
    delta["w_mod"], new_m["w_mod"], new_v["w_mod"] = (o.reshape(shp) for o in outs)
    big_names = ("w_mod", "w_ffn_in", "w_ffn_out", "w_in", "w_out", "w_pw1", "w_pw2")
    small_names = [nm for nm in names if nm not in big_names]
    packs = [_pack([src[nm] for nm in small_names]) for src in (weights, grads, moments_m, moments_v)]
    offs_s = packs[0][1]
    outs = _adamw(*[pk[0] for pk in packs], "adamw_small")
    for o, dst in zip(outs, (delta, new_m, new_v)):
        o = o.reshape(-1)
        for nm, off in zip(small_names, offs_s):
            dst[nm] = _take(o, off, weights[nm].shape)
    r_in, = exchanged(3, ex3, outs[0])
    adamw_big("w_in", [r_in], True, (w_in_t, m_w_in_t, v_w_in_t))

    return (loss, dh0[None], *[grads[n] for n in names], *[delta[n] for n in names],
            *[new_m[n] for n in names], *[new_v[n] for n in names])
```

```python
import math

import jax
import jax.numpy as jnp
from jax import lax
from jax.experimental import pallas as pl
from jax.experimental.pallas import tpu as pltpu

F32 = jnp.float32
BF16 = jnp.bfloat16
MESH = pl.DeviceIdType.MESH

N_DEV = 8
D_MODEL = 1024
EPS = 1e-6
HEAD_DIM = 64
ATTN_WIDTH = 512
KV_WIDTH = 128
SG_WIDTH = 512
N_SG_GROUPS = 4
CHUNK = 128
IN_WIDTH = 1792
D_FF = 2816
FF_SHARD = 2 * D_FF // N_DEV
CONV_WIDTH = 31
CONV_HALO = 16
GRID_W = 64
ROPE_THETA = 10000.0
LANES = 128
SUBLANES = 8
ROW_BLOCK = 512
ADAM_ROWS = 256
ADAM_LR, ADAM_B1, ADAM_B2, ADAM_EPS, ADAM_WD, ADAM_STEP = 0.001, 0.9, 0.999, 1e-08, 0.01, 10


def _tile(n, target, mult=LANES):
    best = None
    for t in range(mult, min(n, target) + 1, mult):
        if n % t == 0:
            best = t
    return best if best is not None else n


def _sigmoid(x):
    return 1.0 / (1.0 + jnp.exp(-x))


def _silu(x):
    return x * _sigmoid(x)


def _dsilu(x):
    s = _sigmoid(x)
    return s * (1.0 + x * (1.0 - s))


_GELU_K = math.sqrt(2.0 / math.pi)


def _gelu(x):
    return 0.5 * x * (1.0 + jnp.tanh(_GELU_K * (x + 0.044715 * x * x * x)))


def _gelu_and_grad(x):
    x2 = x * x
    t = jnp.tanh(_GELU_K * x * (1.0 + 0.044715 * x2))
    half = 0.5 * (1.0 + t)
    return x * half, half + 0.5 * x * (1.0 - t * t) * _GELU_K * (1.0 + 3.0 * 0.044715 * x2)


def _split_bf16(x):
    hi = x.astype(BF16)
    lo = (x - hi.astype(F32)).astype(BF16)
    return hi, lo


def _dot(a, b, dims):
    return lax.dot_general(a, b, (dims, ((), ())), preferred_element_type=F32)


def _dot3(a, b, dims):
    ah, al = _split_bf16(a)
    bh, bl = _split_bf16(b)
    return _dot(ah, bh, dims) + _dot(ah, bl, dims) + _dot(al, bh, dims)


NN = ((1,), (0,))
NT = ((1,), (1,))
TN = ((0,), (0,))


def _all_gather(xs, name, in_vmem):
    n_arr = len(xs)

    def body(*refs):
        x_refs, out_refs = refs[:n_arr], refs[n_arr:2 * n_arr]
        send_sems, recv_sems, local_sems = refs[2 * n_arr:]
        x, y, c = lax.axis_index("x"), lax.axis_index("y"), lax.axis_index("c")
        me, sibling = (x, y, c), (x, y, 1 - c)
        chips = [(1 - x, y), (x, 1 - y), (1 - x, 1 - y)]

        def rows(a, px, py, pc):
            m_per = xs[a].shape[0]
            return out_refs[a].at[pl.ds((4 * px + 2 * py + pc) * m_per, m_per), :]

        def copy(a, k, block, to, src=None):
            return pltpu.make_async_remote_copy(
                src_ref=rows(a, *block) if src is None else src,
                dst_ref=rows(a, *block),
                send_sem=send_sems.at[7 * a + k],
                recv_sem=recv_sems.at[7 * a + k],
                device_id=to,
                device_id_type=MESH,
            )

        mine, first, passed = [], [], []
        for a in range(n_arr):
            mine.append(pltpu.make_async_copy(x_refs[a], rows(a, *me), local_sems.at[a]))
            mine[-1].start()
            first.append(copy(a, 0, me, sibling, src=x_refs[a]))
            first += [copy(a, 1 + j, me, (*chip, c), src=x_refs[a]) for j, chip in enumerate(chips)]
        for cp in first:
            cp.start()
        for a in range(n_arr):
            for j, chip in enumerate(chips):
                copy(a, 1 + j, (*chip, c), me).wait_recv()
                passed.append(copy(a, 4 + j, (*chip, c), sibling))
                passed[-1].start()
        for a in range(n_arr):
            copy(a, 0, sibling, me).wait_recv()
            for j, chip in enumerate(chips):
                copy(a, 4 + j, (*chip, 1 - c), me).wait_recv()
        for cp in first + passed:
            cp.wait_send()
        for cp in mine:
            cp.wait()

    space = pltpu.VMEM if in_vmem else pl.ANY
    return pl.pallas_call(
        body,
        name=name,
        out_shape=[jax.ShapeDtypeStruct((N_DEV * t.shape[0], t.shape[1]), t.dtype) for t in xs],
        in_specs=[pl.BlockSpec(memory_space=space)] * n_arr,
        out_specs=[pl.BlockSpec(memory_space=space)] * n_arr,
        scratch_shapes=[
            pltpu.SemaphoreType.DMA((7 * n_arr,)),
            pltpu.SemaphoreType.DMA((7 * n_arr,)),
            pltpu.SemaphoreType.DMA((n_arr,)),
        ],
    )(*xs)


HBM_SPEC = pl.BlockSpec(memory_space=pltpu.HBM)
SEM_SPEC = pl.BlockSpec(memory_space=pltpu.SEMAPHORE)
DATAFLOW_EFFECT = pltpu.SideEffectType.DATAFLOW_SIDE_EFFECTING


def _peers(x, y, c):
    for k in range(1, N_DEV):
        px = 1 - x if (k >> 2) & 1 else x
        py = 1 - y if (k >> 1) & 1 else y
        pc = 1 - c if k & 1 else c
        yield k - 1, (px, py, pc), 4 * px + 2 * py + pc


def _push_copies(src_refs, land_refs, send_sems, recv_sems, shapes, whole_src):
    x, y, c = lax.axis_index("x"), lax.axis_index("y"), lax.axis_index("c")
    me = 4 * x + 2 * y + c
    for a, (m_per, _) in enumerate(shapes):
        def block(ref, idx, m_per=m_per):
            return ref.at[pl.ds(idx * m_per, m_per), :]

        for k, peer, pidx in _peers(x, y, c):
            src = src_refs[a] if whole_src else block(src_refs[a], pidx)
            sems = dict(send_sem=send_sems.at[N_DEV * a + k], recv_sem=recv_sems.at[N_DEV * a + k],
                        device_id=peer, device_id_type=MESH)
            yield (pltpu.make_async_remote_copy(src_ref=src, dst_ref=block(land_refs[a], me), **sems),
                   pltpu.make_async_remote_copy(src_ref=src, dst_ref=block(land_refs[a], pidx), **sems))


def _own_copies(src_refs, land_refs, recv_sems, shapes, whole_src):
    me = 4 * lax.axis_index("x") + 2 * lax.axis_index("y") + lax.axis_index("c")
    for a, (m_per, _) in enumerate(shapes):
        mine = pl.ds(me * m_per, m_per)
        src = src_refs[a] if whole_src else src_refs[a].at[mine, :]
        yield pltpu.make_async_copy(src, land_refs[a].at[mine, :], recv_sems.at[N_DEV * a + N_DEV - 1])


def _push_begin(srcs, whole_src, name):
    n_arr = len(srcs)
    shapes = [(t.shape[0] if whole_src else t.shape[0] // N_DEV, t.shape[1]) for t in srcs]
    lands = [lax.empty((N_DEV * m, n), t.dtype) for (m, n), t in zip(shapes, srcs)]

    def body(*refs):
        src_refs, land_refs = refs[:n_arr], refs[n_arr:2 * n_arr]
        send_sems, recv_sems = refs[2 * n_arr], refs[2 * n_arr + 1]
        token = refs[-1]
        for outgoing, _ in _push_copies(src_refs, land_refs, send_sems, recv_sems, shapes, whole_src):
            outgoing.start()
        for own in _own_copies(src_refs, land_refs, recv_sems, shapes, whole_src):
            own.start()
        token[...] = jnp.zeros_like(token)

    operands = [pltpu.with_memory_space_constraint(t, pltpu.HBM) for t in list(srcs) + lands]
    outs = pl.pallas_call(
        body, name=name,
        out_shape=(pltpu.SemaphoreType.DMA((N_DEV * n_arr,)), pltpu.SemaphoreType.DMA((N_DEV * n_arr,)),
                   *[pltpu.HBM(t.shape, t.dtype) for t in operands],
                   jax.ShapeDtypeStruct((SUBLANES, LANES), F32)),
        in_specs=[HBM_SPEC] * (2 * n_arr),
        out_specs=(SEM_SPEC, SEM_SPEC, *[HBM_SPEC] * (2 * n_arr), pl.BlockSpec(memory_space=pltpu.VMEM)),
        input_output_aliases={i: 2 + i for i in range(2 * n_arr)},
        compiler_params=pltpu.CompilerParams(has_side_effects=DATAFLOW_EFFECT),
    )(*operands)
    return outs[0], outs[1], list(outs[2:2 + n_arr]), list(outs[2 + n_arr:2 + 2 * n_arr]), outs[-1], whole_src


def _push_end(handle, after, name):
    send_sems, recv_sems, srcs, lands, _, whole_src = handle
    n_arr = len(srcs)
    shapes = [(t.shape[0] // N_DEV, t.shape[1]) for t in lands]

    def body(*refs):
        src_refs, land_refs = refs[:n_arr], refs[n_arr:2 * n_arr]
        send_sems_ref, recv_sems_ref = refs[2 * n_arr], refs[2 * n_arr + 1]
        for outgoing, incoming in _push_copies(src_refs, land_refs, send_sems_ref, recv_sems_ref, shapes, whole_src):
            outgoing.wait_send()
            incoming.wait_recv()
        for own in _own_copies(src_refs, land_refs, recv_sems_ref, shapes, whole_src):
            own.wait()

    outs = pl.pallas_call(
        body, name=name,
        out_shape=tuple(pltpu.HBM(t.shape, t.dtype) for t in srcs + lands),
        in_specs=[HBM_SPEC] * (2 * n_arr) + [SEM_SPEC, SEM_SPEC, pl.BlockSpec(memory_space=pl.ANY)],
        out_specs=tuple([HBM_SPEC] * (2 * n_arr)),
        input_output_aliases={i: i for i in range(2 * n_arr)},
        compiler_params=pltpu.CompilerParams(has_side_effects=DATAFLOW_EFFECT),
    )(*srcs, *lands, send_sems, recv_sems, after)
    return list(outs[:n_arr]), list(outs[n_arr:])


def _sum_devices(r, name, rows_per_step=ADAM_ROWS):
    _, m, n = r.shape
    tm = _tile(m, rows_per_step, 8)

    def body(r_ref, o_ref):
        acc = r_ref[0].astype(F32)
        for s in range(1, N_DEV):
            acc = acc + r_ref[s].astype(F32)
        o_ref[...] = acc

    return pl.pallas_call(
        body,
        name=name,
        grid=(m // tm,),
        out_shape=jax.ShapeDtypeStruct((m, n), F32),
        in_specs=[pl.BlockSpec((N_DEV, tm, n), lambda i: (0, i, 0))],
        out_specs=pl.BlockSpec((tm, n), lambda i: (i, 0)),
        compiler_params=pltpu.CompilerParams(dimension_semantics=("parallel",)),
    )(r)


def _get(ref):
    return ref[0] if len(ref.shape) == 3 else ref[...]


def _put(ref, val):
    if len(ref.shape) == 3:
        ref[0] = val
    else:
        ref[...] = val


def _norm_mod(hv, g, sc, sh):
    r = lax.rsqrt(jnp.mean(hv * hv, axis=-1, keepdims=True) + EPS)
    return (hv * r) * g * (1.0 + sc) + sh


def _mm_call(name, a, b, a_spec, b_spec, out_sds, o_spec, grid, dims, acc_shape, bias=None,
             res=None, gate=None, raw_out=False, vec_spec=None, norm=None):
    nk = grid[2]
    operands, in_specs = [a, b], [a_spec, b_spec]
    if bias is not None:
        operands.append(bias)
        in_specs.append(vec_spec)
    if res is not None:
        operands += [res, gate]
        in_specs += [o_spec, vec_spec]
    if norm is not None:
        assert grid[1] == 1
        operands += list(norm)
        in_specs += [vec_spec] * 3
    out_shape, out_specs = [out_sds], [o_spec]
    if raw_out:
        out_shape.append(jax.ShapeDtypeStruct(out_sds.shape, BF16))
        out_specs.append(o_spec)
    if norm is not None:
        out_shape.append(jax.ShapeDtypeStruct(out_sds.shape, BF16))
        out_specs.append(o_spec)

    def body(*refs):
        it = iter(refs)
        a_ref, b_ref = next(it), next(it)
        bias_ref = next(it) if bias is not None else None
        res_ref, gate_ref = (next(it), next(it)) if res is not None else (None, None)
        norm_refs = (next(it), next(it), next(it)) if norm is not None else None
        o_ref = next(it)
        raw_ref = next(it) if raw_out else None
        xn_ref = next(it) if norm is not None else None
        acc = next(it) if nk > 1 else None
        k = pl.program_id(2)
        part = _dot(_get(a_ref).astype(BF16), _get(b_ref).astype(BF16), dims)

        def finish(y):
            if bias_ref is not None:
                y = y + bias_ref[...]
            if raw_ref is not None:
                raw_ref[...] = y.astype(BF16)
            if res_ref is not None:
                y = res_ref[...] + gate_ref[...] * y
            _put(o_ref, y.astype(out_sds.dtype))
            if xn_ref is not None:
                xn_ref[...] = _norm_mod(y, *[r[...] for r in norm_refs]).astype(BF16)

        if nk == 1:
            finish(part)
        else:
            @pl.when(k == 0)
            def _():
                acc[...] = part

            @pl.when(k > 0)
            def _():
                acc[...] += part

            @pl.when(k == nk - 1)
            def _():
                finish(acc[...])

    outs = pl.pallas_call(
        body,
        name=name,
        grid=grid,
        out_shape=out_shape,
        in_specs=in_specs,
        out_specs=out_specs,
        scratch_shapes=[pltpu.VMEM(acc_shape, F32)] if nk > 1 else [],
        compiler_params=pltpu.CompilerParams(dimension_semantics=("parallel", "parallel", "arbitrary")),
    )(*operands)
    return outs if len(outs) > 1 else outs[0]


def _mm(a, b, mode, name, out_dtype=F32, bias=None, res=None, gate=None, raw_out=False,
        tm=1024, tn=1024, tk=1024, a_row_off=0, norm=None):
    if mode == "nn":
        K, N = b.shape
        M = a.shape[0] - a_row_off
    elif mode == "nt":
        N, K = b.shape
        M = a.shape[0] - a_row_off
    else:
        (K, M), N = a.shape, b.shape[1]
    tm, tn, tk = _tile(M, tm, LANES if mode == "tn" else 2 * SUBLANES), _tile(N, tn), _tile(K, tk)
    off = a_row_off // tm
    dims = {"nn": NN, "nt": NT, "tn": TN}[mode]
    a_spec = (pl.BlockSpec((tk, tm), lambda i, j, k: (k, i)) if mode == "tn"
              else pl.BlockSpec((tm, tk), lambda i, j, k: (i + off, k)))
    b_spec = (pl.BlockSpec((tn, tk), lambda i, j, k: (j, k)) if mode == "nt"
              else pl.BlockSpec((tk, tn), lambda i, j, k: (k, j)))
    return _mm_call(name, a, b, a_spec, b_spec, jax.ShapeDtypeStruct((M, N), out_dtype),
                    pl.BlockSpec((tm, tn), lambda i, j, k: (i, j)), (M // tm, N // tn, K // tk), dims,
                    (tm, tn), bias, res, gate, raw_out, pl.BlockSpec((1, tn), lambda i, j, k: (0, j)), norm)


def _mm_sum_shards(a3, b3, mode, name, out_dtype=F32, res=None, gate=None, raw_out=False, tm=512, norm=None):
    S, M, kk = a3.shape
    N = b3.shape[2] if mode == "nn" else b3.shape[1]
    tm = _tile(M, tm)
    dims = NN if mode == "nn" else NT
    has_res = res is not None

    def body(*refs):
        it = iter(refs)
        a_ref, b_ref = next(it), next(it)
        res_ref, gate_ref = (next(it), next(it)) if has_res else (None, None)
        norm_refs = (next(it), next(it), next(it)) if norm is not None else None
        o_ref = next(it)
        raw_ref = next(it) if raw_out else None
        xn_ref = next(it) if norm is not None else None
        y = _dot(a_ref[0], b_ref[0], dims)
        for s in range(1, S):
            y = y + _dot(a_ref[s], b_ref[s], dims)
        if raw_ref is not None:
            raw_ref[...] = y.astype(BF16)
        if has_res:
            y = res_ref[...] + gate_ref[...] * y
        o_ref[...] = y.astype(out_dtype)
        if xn_ref is not None:
            xn_ref[...] = _norm_mod(y, *[r[...] for r in norm_refs]).astype(BF16)

    tile = pl.BlockSpec((tm, N), lambda i: (i, 0))
    operands = [a3, b3] + ([res, gate] if has_res else []) + (list(norm) if norm is not None else [])
    in_specs = [pl.BlockSpec((S, tm, kk), lambda i: (0, i, 0)), pl.BlockSpec(b3.shape, lambda i: (0, 0, 0))]
    in_specs += [tile, _vec_spec(N)] if has_res else []
    in_specs += [_vec_spec(N)] * 3 if norm is not None else []
    out_shape = [jax.ShapeDtypeStruct((M, N), out_dtype)] + ([jax.ShapeDtypeStruct((M, N), BF16)] if raw_out else [])
    out_shape += [jax.ShapeDtypeStruct((M, N), BF16)] if norm is not None else []
    outs = pl.pallas_call(
        body, name=name, grid=(M // tm,),
        out_shape=out_shape, in_specs=in_specs, out_specs=[tile] * len(out_shape),
        compiler_params=pltpu.CompilerParams(dimension_semantics=("parallel",)),
    )(*operands)
    return outs if len(outs) > 1 else outs[0]


def _mm_tn_shard_rows(a3, b, name, out_dtype, tn=1024, tk=4096):
    S, T, m = a3.shape
    N = b.shape[1]
    tn, tk = _tile(N, tn), _tile(T, tk)
    return _mm_call(name, a3, b, pl.BlockSpec((1, tk, m), lambda i, j, k: (i, k, 0)),
                    pl.BlockSpec((tk, tn), lambda i, j, k: (k, j)), jax.ShapeDtypeStruct((S, m, N), out_dtype),
                    pl.BlockSpec((1, m, tn), lambda i, j, k: (i, 0, j)), (S, N // tn, T // tk), TN, (m, tn))


def _row_spec(tm, width, off=0):
    return pl.BlockSpec((tm, width), lambda i: (i + off, 0))


def _vec_spec(width):
    return pl.BlockSpec((1, width), lambda i: (0, 0))


def _norm_mod_fwd_cat(hc, h, g, csc, csh, sc, sh, name):
    (C, Dm), T = hc.shape, h.shape[0]
    tm = _tile(math.gcd(C, T), ROW_BLOCK, 8)
    off = C // tm

    def body(hc_ref, h_ref, g_ref, csc_ref, csh_ref, sc_ref, sh_ref, o_ref):
        is_ctx = pl.program_id(0) < off
        hv = jnp.where(is_ctx, hc_ref[...], h_ref[...])
        scv = jnp.where(is_ctx, csc_ref[...], sc_ref[...])
        shv = jnp.where(is_ctx, csh_ref[...], sh_ref[...])
        r = lax.rsqrt(jnp.mean(hv * hv, axis=-1, keepdims=True) + EPS)
        o_ref[...] = ((hv * r) * g_ref[...] * (1.0 + scv) + shv).astype(BF16)

    return pl.pallas_call(
        body, name=name, grid=((C + T) // tm,),
        out_shape=jax.ShapeDtypeStruct((C + T, Dm), BF16),
        in_specs=[pl.BlockSpec((tm, Dm), lambda i: (jnp.minimum(i, off - 1), 0)),
                  pl.BlockSpec((tm, Dm), lambda i: (jnp.maximum(i - off, 0), 0))] + [_vec_spec(Dm)] * 5,
        out_specs=_row_spec(tm, Dm),
        compiler_params=pltpu.CompilerParams(dimension_semantics=("parallel",)),
    )(hc, h, g, csc, csh, sc, sh)


def _gate_grads(dh, y_ref, gt_ref, dy_ref, dgt_ref, dsum_ref):
    dy = dh * gt_ref[...]
    dgt_ref[...] += jnp.sum(dh * y_ref[...].astype(F32), axis=0, keepdims=True)
    dsum_ref[...] += jnp.sum(dy, axis=0, keepdims=True)
    dy_ref[...] = dy.astype(BF16)


def _norm_mod_bwd(h, g, sc, dxm, dres, name, dxm_row_off=0, gate=None):
    R, Dm = h.shape
    tm = _tile(math.gcd(R, dxm_row_off) if dxm_row_off else R, ROW_BLOCK, 8)
    off = dxm_row_off // tm
    has_res = dres is not None
    has_gate = gate is not None

    def body(*refs):
        it = iter(refs)
        h_ref, g_ref, sc_ref, dx_ref = next(it), next(it), next(it), next(it)
        dres_ref = next(it) if has_res else None
        y_ref, gt_ref = (next(it), next(it)) if has_gate else (None, None)
        dh_ref, da_ref, dsh_ref = next(it), next(it), next(it)
        gate_out = (next(it), next(it), next(it)) if has_gate else ()
        i = pl.program_id(0)

        @pl.when(i == 0)
        def _():
            for ref in (da_ref, dsh_ref) + gate_out[1:]:
                ref[...] = jnp.zeros_like(ref)

        hv = h_ref[...]
        dx = dx_ref[...].astype(F32)
        r = lax.rsqrt(jnp.mean(hv * hv, axis=-1, keepdims=True) + EPS)
        n = hv * r
        da_ref[...] += jnp.sum(dx * n, axis=0, keepdims=True)
        dsh_ref[...] += jnp.sum(dx, axis=0, keepdims=True)
        dn = dx * (g_ref[...] * (1.0 + sc_ref[...]))
        dh = r * (dn - n * jnp.mean(dn * n, axis=-1, keepdims=True))
        if has_res:
            dh = dh + dres_ref[...]
        dh_ref[...] = dh
        if has_gate:
            _gate_grads(dh, y_ref, gt_ref, *gate_out)

    operands = [h, g, sc, dxm] + ([dres] if has_res else []) + (list(gate) if has_gate else [])
    in_specs = [_row_spec(tm, Dm), _vec_spec(Dm), _vec_spec(Dm), _row_spec(tm, Dm, off)]
    in_specs += [_row_spec(tm, Dm)] if has_res else []
    in_specs += [_row_spec(tm, Dm), _vec_spec(Dm)] if has_gate else []
    vec = jax.ShapeDtypeStruct((1, Dm), F32)
    out_shape = [jax.ShapeDtypeStruct((R, Dm), F32), vec, vec]
    out_specs = [_row_spec(tm, Dm), _vec_spec(Dm), _vec_spec(Dm)]
    if has_gate:
        out_shape += [jax.ShapeDtypeStruct((R, Dm), BF16), vec, vec]
        out_specs += [_row_spec(tm, Dm), _vec_spec(Dm), _vec_spec(Dm)]
    return pl.pallas_call(
        body, name=name, grid=(R // tm,),
        out_shape=out_shape, in_specs=in_specs, out_specs=out_specs,
        compiler_params=pltpu.CompilerParams(dimension_semantics=("arbitrary",)),
    )(*operands)


def _ffn_in_swiglu(xf, w3, name, tm=1024):
    T, K = xf.shape
    S, n, _ = w3.shape
    half = S // 2
    tm = _tile(T, tm)

    def body(a_ref, wg_ref, wu_ref, gu_ref, act_ref):
        a = a_ref[...]
        g = _dot(a, wg_ref[0], NT)
        u = _dot(a, wu_ref[0], NT)
        gu_ref[0, 0] = g.astype(BF16)
        gu_ref[1, 0] = u.astype(BF16)
        act_ref[0] = (_silu(g) * u).astype(BF16)

    return pl.pallas_call(
        body, name=name, grid=(T // tm, half),
        out_shape=[jax.ShapeDtypeStruct((2, half, T, n), BF16), jax.ShapeDtypeStruct((half, T, n), BF16)],
        in_specs=[pl.BlockSpec((tm, K), lambda i, j: (i, 0)),
                  pl.BlockSpec((1, n, K), lambda i, j: (j, 0, 0)),
                  pl.BlockSpec((1, n, K), lambda i, j: (j + half, 0, 0))],
        out_specs=[pl.BlockSpec((2, 1, tm, n), lambda i, j: (0, j, i, 0)),
                   pl.BlockSpec((1, tm, n), lambda i, j: (j, i, 0))],
        compiler_params=pltpu.CompilerParams(dimension_semantics=("parallel", "parallel")),
    )(xf, w3, w3)


def _ffn_out_dx_swiglu(df, wo, gu, name, tm=1024):
    T, Dm = df.shape
    half, n, _ = wo.shape
    tm = _tile(T, tm)

    def body(df_ref, w_ref, gu_ref, o_ref):
        da = _dot(df_ref[...], w_ref[0], NT)
        g = gu_ref[0, 0].astype(F32)
        u = gu_ref[1, 0].astype(F32)
        s = _sigmoid(g)
        o_ref[0, 0] = (da * u * (s * (1.0 + g * (1.0 - s)))).astype(BF16)
        o_ref[1, 0] = (da * (g * s)).astype(BF16)

    gu_spec = pl.BlockSpec((2, 1, tm, n), lambda i, j: (0, j, i, 0))
    return pl.pallas_call(
        body, name=name, grid=(T // tm, half),
        out_shape=jax.ShapeDtypeStruct(gu.shape, BF16),
        in_specs=[pl.BlockSpec((tm, Dm), lambda i, j: (i, 0)),
                  pl.BlockSpec((1, n, Dm), lambda i, j: (j, 0, 0)), gu_spec],
        out_specs=gu_spec,
        compiler_params=pltpu.CompilerParams(dimension_semantics=("parallel", "parallel")),
    )(df, wo, gu)


def _glu_fwd(ag, name):
    R = ag.shape[0]
    tm = _tile(R, ROW_BLOCK, 8)

    def body(ag_ref, o_ref):
        o_ref[...] = ag_ref[:, :D_MODEL].astype(F32) * _sigmoid(ag_ref[:, D_MODEL:].astype(F32))

    return pl.pallas_call(
        body, name=name, grid=(R // tm,),
        out_shape=jax.ShapeDtypeStruct((R, D_MODEL), F32),
        in_specs=[_row_spec(tm, 2 * D_MODEL)],
        out_specs=_row_spec(tm, D_MODEL),
        compiler_params=pltpu.CompilerParams(dimension_semantics=("parallel",)),
    )(ag)


def _glu_bwd(ag, dhg, name):
    R = ag.shape[0]
    tm = _tile(R, ROW_BLOCK, 8)

    def body(ag_ref, dh_ref, o_ref, s_ref):
        i = pl.program_id(0)

        @pl.when(i == 0)
        def _():
            s_ref[...] = jnp.zeros_like(s_ref)

        a = ag_ref[:, :D_MODEL].astype(F32)
        s = _sigmoid(ag_ref[:, D_MODEL:].astype(F32))
        dh = dh_ref[...]
        da = dh * s
        dg = dh * a * s * (1.0 - s)
        o_ref[:, :D_MODEL] = da.astype(BF16)
        o_ref[:, D_MODEL:] = dg.astype(BF16)
        s_ref[:, :D_MODEL] += jnp.sum(da, axis=0, keepdims=True)
        s_ref[:, D_MODEL:] += jnp.sum(dg, axis=0, keepdims=True)

    return pl.pallas_call(
        body, name=name, grid=(R // tm,),
        out_shape=[jax.ShapeDtypeStruct((R, 2 * D_MODEL), BF16), jax.ShapeDtypeStruct((1, 2 * D_MODEL), F32)],
        in_specs=[_row_spec(tm, 2 * D_MODEL), _row_spec(tm, D_MODEL)],
        out_specs=[_row_spec(tm, 2 * D_MODEL), _vec_spec(2 * D_MODEL)],
        compiler_params=pltpu.CompilerParams(dimension_semantics=("arbitrary",)),
    )(ag, dhg)


def _halo_specs(tm, nblk, width):
    per = tm // CONV_HALO
    prev = pl.BlockSpec((CONV_HALO, width), lambda i: (jnp.maximum(i * per - 1, 0), 0))
    nxt = pl.BlockSpec((CONV_HALO, width), lambda i: (jnp.minimum((i + 1) * per, nblk * per - 1), 0))
    return prev, nxt


def _fill_halo(scr, prev_ref, cur_ref, next_ref, i, nblk, tm):
    scr[0:CONV_HALO, :] = jnp.where(i > 0, prev_ref[...], 0.0)
    scr[CONV_HALO:CONV_HALO + tm, :] = cur_ref[...]
    scr[CONV_HALO + tm:2 * CONV_HALO + tm, :] = jnp.where(i < nblk - 1, next_ref[...], 0.0)


CONV_ROWS = 128


CONV_REACH = (CONV_WIDTH // SUBLANES) * SUBLANES


def _windows(scr, stage, cols, tm):
    for r in range(SUBLANES):
        if r:
            stage[r] = scr[pl.ds(r, tm + CONV_REACH), cols]
        for a in range(CONV_REACH // SUBLANES + 1):
            off = SUBLANES * a + r
            if 1 <= off <= CONV_WIDTH:
                yield off, (stage[r, SUBLANES * a:SUBLANES * a + tm, :] if r
                            else scr[SUBLANES * a:SUBLANES * a + tm, cols])


def _conv_fwd(hg, w_dw, b_dw, name):
    R, Dm = hg.shape
    tm = _tile(R, CONV_ROWS, CONV_HALO)
    nblk = R // tm
    prev_spec, next_spec = _halo_specs(tm, nblk, Dm)

    def body(prev_ref, cur_ref, next_ref, w_ref, bdw_ref, hd_ref, scr, stage):
        _fill_halo(scr, prev_ref, cur_ref, next_ref, pl.program_id(0), nblk, tm)
        for cb in range(Dm // LANES):
            cols = slice(cb * LANES, (cb + 1) * LANES)
            acc = jnp.zeros((tm, LANES), F32) + bdw_ref[:, cols]
            for off, win in _windows(scr, stage, cols, tm):
                acc = acc + w_ref[off - 1:off, cols] * win
            hd_ref[:, cols] = acc

    return pl.pallas_call(
        body, name=name, grid=(nblk,),
        out_shape=jax.ShapeDtypeStruct((R, Dm), F32),
        in_specs=[prev_spec, _row_spec(tm, Dm), next_spec,
                  pl.BlockSpec((CONV_WIDTH, Dm), lambda i: (0, 0)), _vec_spec(Dm)],
        out_specs=_row_spec(tm, Dm),
        scratch_shapes=[pltpu.VMEM((tm + 2 * CONV_HALO, Dm), F32),
                        pltpu.VMEM((SUBLANES, tm + CONV_REACH, LANES), F32)],
        compiler_params=pltpu.CompilerParams(dimension_semantics=("parallel",)),
    )(hg, hg, hg, w_dw, b_dw)


def _ln_silu_fwd(hd, ln_g, ln_b, name):
    R, Dm = hd.shape
    tm = _tile(R, ROW_BLOCK, 8)

    def body(hd_ref, g_ref, b_ref, hs_ref):
        hd = hd_ref[...]
        xc = hd - jnp.mean(hd, axis=-1, keepdims=True)
        rs = lax.rsqrt(jnp.mean(xc * xc, axis=-1, keepdims=True) + EPS)
        hs_ref[...] = _silu(xc * rs * g_ref[...] + b_ref[...]).astype(BF16)

    return pl.pallas_call(
        body, name=name, grid=(R // tm,),
        out_shape=jax.ShapeDtypeStruct((R, Dm), BF16),
        in_specs=[_row_spec(tm, Dm), _vec_spec(Dm), _vec_spec(Dm)],
        out_specs=_row_spec(tm, Dm),
        compiler_params=pltpu.CompilerParams(dimension_semantics=("parallel",)),
    )(hd, ln_g, ln_b)


def _ln_silu_bwd(dhs, hd, ln_g, ln_b, name):
    R, Dm = hd.shape
    tm = _tile(R, ROW_BLOCK, 8)

    def body(dhs_ref, hd_ref, g_ref, b_ref, dhd_ref, dg_ref, db_ref, dsum_ref):
        i = pl.program_id(0)

        @pl.when(i == 0)
        def _():
            dg_ref[...] = jnp.zeros_like(dg_ref)
            db_ref[...] = jnp.zeros_like(db_ref)
            dsum_ref[...] = jnp.zeros_like(dsum_ref)

        hd = hd_ref[...]
        mu = jnp.mean(hd, axis=-1, keepdims=True)
        xc = hd - mu
        rs = lax.rsqrt(jnp.mean(xc * xc, axis=-1, keepdims=True) + EPS)
        z = xc * rs
        hl = z * g_ref[...] + b_ref[...]
        dhl = dhs_ref[...].astype(F32) * _dsilu(hl)
        dg_ref[...] += jnp.sum(dhl * z, axis=0, keepdims=True)
        db_ref[...] += jnp.sum(dhl, axis=0, keepdims=True)
        dz = dhl * g_ref[...]
        dhd = rs * (dz - jnp.mean(dz, axis=-1, keepdims=True) - z * jnp.mean(dz * z, axis=-1, keepdims=True))
        dsum_ref[...] += jnp.sum(dhd, axis=0, keepdims=True)
        dhd_ref[...] = dhd

    return pl.pallas_call(
        body, name=name, grid=(R // tm,),
        out_shape=[jax.ShapeDtypeStruct((R, Dm), F32)] + [jax.ShapeDtypeStruct((1, Dm), F32)] * 3,
        in_specs=[_row_spec(tm, Dm), _row_spec(tm, Dm), _vec_spec(Dm), _vec_spec(Dm)],
        out_specs=[_row_spec(tm, Dm), _vec_spec(Dm), _vec_spec(Dm), _vec_spec(Dm)],
        compiler_params=pltpu.CompilerParams(dimension_semantics=("arbitrary",)),
    )(dhs, hd, ln_g, ln_b)


def _conv_bwd(dhd, hg, w_dw, name):
    R, Dm = hg.shape
    tm = _tile(R, CONV_ROWS, CONV_HALO)
    nblk = R // tm
    prev_spec, next_spec = _halo_specs(tm, nblk, Dm)

    def body(dprev, dcur, dnext, gprev, gcur, gnext, w_ref, dhg_ref, dw_ref, dscr, gscr, dwp, stage):
        i = pl.program_id(0)

        @pl.when(i == 0)
        def _():
            dwp[...] = jnp.zeros_like(dwp)

        _fill_halo(dscr, dprev, dcur, dnext, i, nblk, tm)
        _fill_halo(gscr, gprev, gcur, gnext, i, nblk, tm)
        for cb in range(Dm // LANES):
            cols = slice(cb * LANES, (cb + 1) * LANES)
            acc = jnp.zeros((tm, LANES), F32)
            for off, win in _windows(dscr, stage, cols, tm):
                j = CONV_WIDTH - off
                acc = acc + w_ref[j:j + 1, cols] * win
            dhg_ref[:, cols] = acc
            d_here = dcur[:, cols]
            for off, win in _windows(gscr, stage, cols, tm):
                j = off - 1
                prod = d_here * win
                part = prod[0:SUBLANES]
                for k in range(1, tm // SUBLANES):
                    part = part + prod[k * SUBLANES:(k + 1) * SUBLANES]
                dwp[j * SUBLANES:(j + 1) * SUBLANES, cols] += part

        @pl.when(i == nblk - 1)
        def _():
            for j in range(CONV_WIDTH):
                dw_ref[j:j + 1, :] = jnp.sum(dwp[j * SUBLANES:(j + 1) * SUBLANES, :], axis=0, keepdims=True)

    return pl.pallas_call(
        body, name=name, grid=(nblk,),
        out_shape=[jax.ShapeDtypeStruct((R, Dm), F32), jax.ShapeDtypeStruct((CONV_WIDTH, Dm), F32)],
        in_specs=[prev_spec, _row_spec(tm, Dm), next_spec, prev_spec, _row_spec(tm, Dm), next_spec,
                  pl.BlockSpec((CONV_WIDTH, Dm), lambda i: (0, 0))],
        out_specs=[_row_spec(tm, Dm), pl.BlockSpec((CONV_WIDTH, Dm), lambda i: (0, 0))],
        scratch_shapes=[pltpu.VMEM((tm + 2 * CONV_HALO, Dm), F32)] * 2
        + [pltpu.VMEM((CONV_WIDTH * SUBLANES, Dm), F32), pltpu.VMEM((SUBLANES, tm + CONV_REACH, LANES), F32)],
        compiler_params=pltpu.CompilerParams(dimension_semantics=("arbitrary",)),
    )(dhd, dhd, dhd, hg, hg, hg, w_dw)


def _swap16(y, lane):
    return jnp.where((lane & 16) == 0, pltpu.roll(y, LANES - 16, 1), pltpu.roll(y, 16, 1))


def _head_mean(v, bd):
    hi, lo = _split_bf16(v)
    return (_dot(hi, bd, NN) + _dot(lo, bd, NN)) * (1.0 / HEAD_DIM)


Q_COLS = (0, ATTN_WIDTH)
K_COLS = (ATTN_WIDTH, ATTN_WIDTH + HEAD_DIM * 2)
V_COLS = (K_COLS[1], K_COLS[1] + HEAD_DIM * 2)
SU_COLS = (V_COLS[1], V_COLS[1] + SG_WIDTH)
SV_COLS = (SU_COLS[1], SU_COLS[1] + SG_WIDTH)


def _mix_prep_fwd(p, ctx_rows, cos, sin, qg, kg, bd, w_sp, b_spt, name):
    TT = p.shape[0]
    off = ctx_rows // CHUNK
    q_scale = HEAD_DIM ** -0.5

    def body(p_ref, cos_ref, sin_ref, qg_ref, kg_ref, bd_ref, w_ref, b_ref,
             q_ref, kp_ref, vp_ref, kt_ref, sg_ref):
        lane = lax.broadcasted_iota(jnp.int32, (CHUNK, LANES), 1)
        low = lane < HEAD_DIM
        cs, sn, bdv = cos_ref[...], sin_ref[...], bd_ref[...]

        def norm_rope(xv, gain):
            r = lax.rsqrt(_head_mean(xv * xv, bdv) + EPS)
            yv = xv * r * gain
            return yv * cs + _swap16(yv, lane) * sn

        def pad_heads(ref, t):
            tr = pltpu.roll(t, HEAD_DIM, 1)
            ref[0, 0] = jnp.where(low, t, 0.0).astype(BF16)
            ref[0, 1] = jnp.where(low, 0.0, tr).astype(BF16)
            ref[1, 0] = jnp.where(low, tr, 0.0).astype(BF16)
            ref[1, 1] = jnp.where(low, 0.0, t).astype(BF16)

        for a in range(ATTN_WIDTH // LANES):
            xv = p_ref[:, a * LANES:(a + 1) * LANES]
            q_ref[:, a * LANES:(a + 1) * LANES] = (norm_rope(xv, qg_ref[...]) * q_scale).astype(BF16)
        kh = norm_rope(p_ref[:, K_COLS[0]:K_COLS[1]], kg_ref[...])
        pad_heads(kp_ref, kh)
        pad_heads(vp_ref, p_ref[:, V_COLS[0]:V_COLS[1]])
        kht = kh.T
        kt_ref[0] = kht[:HEAD_DIM].astype(BF16)
        kt_ref[1] = kht[HEAD_DIM:].astype(BF16)
        for g in range(N_SG_GROUPS):
            u = _gelu(p_ref[:, SU_COLS[0] + g * LANES:SU_COLS[0] + (g + 1) * LANES])
            vg = _gelu(p_ref[:, SV_COLS[0] + g * LANES:SV_COLS[0] + (g + 1) * LANES])
            xc = vg - jnp.mean(vg, axis=-1, keepdims=True)
            vn = xc * lax.rsqrt(jnp.mean(xc * xc, axis=-1, keepdims=True) + EPS)
            mixed = _dot(w_ref[g].astype(BF16), vn.astype(BF16), NN) + b_ref[:, g:g + 1]
            sg_ref[:, g * LANES:(g + 1) * LANES] = (u * mixed).astype(BF16)

    def row(width):
        return pl.BlockSpec((CHUNK, width), lambda i: (i, 0))

    def whole(shape):
        return pl.BlockSpec(shape, lambda i: (0,) * len(shape))

    pad_spec = pl.BlockSpec((2, 2, CHUNK, LANES), lambda i: (0, 0, i, 0))
    return pl.pallas_call(
        body, name=name, grid=(TT // CHUNK,),
        out_shape=[jax.ShapeDtypeStruct((TT, ATTN_WIDTH), BF16),
                   jax.ShapeDtypeStruct((2, 2, TT, LANES), BF16), jax.ShapeDtypeStruct((2, 2, TT, LANES), BF16),
                   jax.ShapeDtypeStruct((2, HEAD_DIM, TT), BF16),
                   jax.ShapeDtypeStruct((TT - ctx_rows, ATTN_WIDTH + SG_WIDTH), BF16)],
        in_specs=[row(IN_WIDTH), row(LANES), row(LANES), whole((1, LANES)), whole((1, LANES)),
                  whole((LANES, LANES)), whole((N_SG_GROUPS, CHUNK, CHUNK)), whole((CHUNK, N_SG_GROUPS))],
        out_specs=[row(ATTN_WIDTH), pad_spec, pad_spec,
                   pl.BlockSpec((2, HEAD_DIM, CHUNK), lambda i: (0, 0, i)),
                   pl.BlockSpec((CHUNK, SG_WIDTH), lambda i: (jnp.maximum(i - off, 0), 1))],
        compiler_params=pltpu.CompilerParams(dimension_semantics=("arbitrary",)),
    )(p, cos, sin, qg, kg, bd, w_sp, b_spt)


def _mix_prep_bwd(p, dq, f, dao, ctx_rows, cos, sin, qg, kg, bd, w_sp, w_spt, b_spt, name):
    TT = p.shape[0]
    off = ctx_rows // CHUNK
    q_scale = HEAD_DIM ** -0.5

    def body(p_ref, dq_ref, f_ref, dsg_ref, cos_ref, sin_ref, qg_ref, kg_ref, bd_ref, w_ref, wt_ref,
             b_ref, dp_ref, dqg_ref, dkg_ref, dw_ref, db_ref):
        i = pl.program_id(0)

        @pl.when(i == 0)
        def _():
            dqg_ref[...] = jnp.zeros_like(dqg_ref)
            dkg_ref[...] = jnp.zeros_like(dkg_ref)
            dw_ref[...] = jnp.zeros_like(dw_ref)
            db_ref[...] = jnp.zeros_like(db_ref)

        latent = (i >= off).astype(F32)
        lane = lax.broadcasted_iota(jnp.int32, (CHUNK, LANES), 1)
        low = lane < HEAD_DIM
        cs, sn, bdv = cos_ref[...], sin_ref[...], bd_ref[...]

        def fold(b0):
            return jnp.where(low, f_ref[0, b0] + pltpu.roll(f_ref[0, b0 + 1], HEAD_DIM, 1),
                             pltpu.roll(f_ref[1, b0], HEAD_DIM, 1) + f_ref[1, b0 + 1])

        def norm_rope_bwd(xv, dout, gain):
            r = lax.rsqrt(_head_mean(xv * xv, bdv) + EPS)
            n = xv * r
            dy = dout * cs + _swap16(dout * sn, lane)
            dn = dy * gain
            dx = r * (dn - n * _head_mean(dn * n, bdv))
            return dx, jnp.sum(dy * n, axis=0, keepdims=True)

        for a in range(ATTN_WIDTH // LANES):
            cols = slice(a * LANES, (a + 1) * LANES)
            dx, dg = norm_rope_bwd(p_ref[:, cols], dq_ref[:, cols] * (latent * q_scale), qg_ref[...])
            dp_ref[:, cols] = dx.astype(BF16)
            dqg_ref[...] += dg
        dx, dg = norm_rope_bwd(p_ref[:, K_COLS[0]:K_COLS[1]], fold(0), kg_ref[...])
        dp_ref[:, K_COLS[0]:K_COLS[1]] = dx.astype(BF16)
        dkg_ref[...] += dg
        dp_ref[:, V_COLS[0]:V_COLS[1]] = fold(2).astype(BF16)
        for g in range(N_SG_GROUPS):
            su = p_ref[:, SU_COLS[0] + g * LANES:SU_COLS[0] + (g + 1) * LANES]
            sv = p_ref[:, SV_COLS[0] + g * LANES:SV_COLS[0] + (g + 1) * LANES]
            (u, dgelu_su), (vg, dgelu_sv) = _gelu_and_grad(su), _gelu_and_grad(sv)
            xc = vg - jnp.mean(vg, axis=-1, keepdims=True)
            rs = lax.rsqrt(jnp.mean(xc * xc, axis=-1, keepdims=True) + EPS)
            vn = xc * rs
            vnb = vn.astype(BF16)
            mixed = _dot(w_ref[g].astype(BF16), vnb, NN) + b_ref[:, g:g + 1]
            dsg = dsg_ref[:, g * LANES:(g + 1) * LANES].astype(F32) * latent
            du = dsg * mixed
            dmix = dsg * u
            dmb = dmix.astype(BF16)
            db_ref[:, g:g + 1] += jnp.sum(dmix, axis=-1, keepdims=True)
            dw_ref[g] += _dot(dmb, vnb, NT)
            dvn = _dot(wt_ref[g].astype(BF16), dmb, NN)
            dvg = rs * (dvn - jnp.mean(dvn, axis=-1, keepdims=True)
                        - vn * jnp.mean(dvn * vn, axis=-1, keepdims=True))
            dp_ref[:, SU_COLS[0] + g * LANES:SU_COLS[0] + (g + 1) * LANES] = (du * dgelu_su).astype(BF16)
            dp_ref[:, SV_COLS[0] + g * LANES:SV_COLS[0] + (g + 1) * LANES] = (dvg * dgelu_sv).astype(BF16)

    def row(width):
        return pl.BlockSpec((CHUNK, width), lambda i: (i, 0))

    def latent_row(width, col_block):
        return pl.BlockSpec((CHUNK, width), lambda i: (jnp.maximum(i - off, 0), col_block))

    def whole(shape):
        return pl.BlockSpec(shape, lambda i: (0,) * len(shape))

    return pl.pallas_call(
        body, name=name, grid=(TT // CHUNK,),
        out_shape=[jax.ShapeDtypeStruct((TT, IN_WIDTH), BF16), jax.ShapeDtypeStruct((1, LANES), F32),
                   jax.ShapeDtypeStruct((1, LANES), F32),
                   jax.ShapeDtypeStruct((N_SG_GROUPS, CHUNK, CHUNK), F32),
                   jax.ShapeDtypeStruct((CHUNK, N_SG_GROUPS), F32)],
        in_specs=[row(IN_WIDTH), latent_row(ATTN_WIDTH, 0),
                  pl.BlockSpec((2, 4, CHUNK, LANES), lambda i: (0, 0, i, 0)),
                  latent_row(SG_WIDTH, 1), row(LANES), row(LANES), whole((1, LANES)), whole((1, LANES)),
                  whole((LANES, LANES)), whole((N_SG_GROUPS, CHUNK, CHUNK)),
                  whole((N_SG_GROUPS, CHUNK, CHUNK)), whole((CHUNK, N_SG_GROUPS))],
        out_specs=[row(IN_WIDTH), whole((1, LANES)), whole((1, LANES)),
                   whole((N_SG_GROUPS, CHUNK, CHUNK)), whole((CHUNK, N_SG_GROUPS))],
        compiler_params=pltpu.CompilerParams(dimension_semantics=("arbitrary",)),
    )(p, dq, f, dao, cos, sin, qg, kg, bd, w_sp, w_spt, b_spt)


def _attn_fwd(q, kpad, vpad, ao, ctx_rows, name, tq=256):
    TT = q.shape[0]
    T = TT - ctx_rows
    tq = _tile(T, tq)
    off = ctx_rows // tq
    group = 2 * LANES

    def body(q_ref, k_ref, v_ref, ao_in, o_ref, lse_ref):
        del ao_in
        lane = lax.broadcasted_iota(jnp.int32, (tq, LANES), 1)
        lse = jnp.zeros((tq, LANES), F32)
        for a in range(2):
            acc = jnp.zeros((tq, LANES), F32)
            qa = q_ref[:, a * LANES:(a + 1) * LANES]
            for b in range(2):
                s = _dot(qa, k_ref[0, b], NT)
                m = jnp.max(s, axis=-1, keepdims=True)
                e = jnp.exp(s - m)
                l = jnp.sum(e, axis=-1, keepdims=True)
                acc = acc + _dot(e.astype(BF16), v_ref[0, b], NN) * (1.0 / l)
                lse = jnp.where(lane == 2 * a + b, m + jnp.log(l), lse)
            o_ref[:, a * LANES:(a + 1) * LANES] = acc.astype(BF16)
        lse_ref[0] = lse

    kv_spec = pl.BlockSpec((1, 2, TT, LANES), lambda j, i: (j, 0, 0, 0))
    return pl.pallas_call(
        body, name=name, grid=(2, T // tq),
        out_shape=[jax.ShapeDtypeStruct(ao.shape, BF16), jax.ShapeDtypeStruct((2, T, LANES), F32)],
        in_specs=[pl.BlockSpec((tq, group), lambda j, i: (i + off, j)), kv_spec, kv_spec,
                  pl.BlockSpec(memory_space=pl.ANY)],
        out_specs=[pl.BlockSpec((tq, group), lambda j, i: (i, j)),
                   pl.BlockSpec((1, tq, LANES), lambda j, i: (j, i, 0))],
        input_output_aliases={3: 0},
        compiler_params=pltpu.CompilerParams(dimension_semantics=("parallel", "parallel")),
    )(q, kpad, vpad, ao)


def _attn_bwd(q, dao, ao, lse, kpad, vpad, kt, ctx_rows, name, tq=256):
    TT = q.shape[0]
    T = TT - ctx_rows
    tq = _tile(T, tq)
    off = ctx_rows // tq
    group = 2 * LANES

    def body(q_ref, do_ref, o_ref, lse_ref, k_ref, v_ref, kt_ref, dq_ref, f_ref):
        i = pl.program_id(1)

        @pl.when(i == 0)
        def _():
            f_ref[...] = jnp.zeros_like(f_ref)

        ktv = kt_ref[0]
        lse_t = lse_ref[0].T
        row = lax.broadcasted_iota(jnp.int32, (SUBLANES, LANES), 0)
        lane = lax.broadcasted_iota(jnp.int32, (SUBLANES, LANES), 1)
        half_ones = (jnp.where(lane < HEAD_DIM, 0, 1) == row).astype(BF16)
        for a in range(2):
            cols = slice(a * LANES, (a + 1) * LANES)
            qa = q_ref[:, cols]
            do32 = do_ref[:, cols].astype(F32)
            doa = do32.astype(BF16)
            hi, lo = _split_bf16(do32 * o_ref[:, cols].astype(F32))
            deltas = _dot(half_ones, hi, NT) + _dot(half_ones, lo, NT)
            halves = []
            for b in range(2):
                h = 2 * a + b
                st = _dot(k_ref[0, b], qa, NT)
                pt = jnp.exp(st - lse_t[h:h + 1, :])
                dpt = _dot(v_ref[0, b], doa, NT)
                dst = (pt * (dpt - deltas[b:b + 1, :])).astype(BF16)
                f_ref[0, b] += _dot(dst, qa, NN)
                f_ref[0, 2 + b] += _dot(pt.astype(BF16), doa, NN)
                halves.append(_dot(ktv, dst, NN))
            dq_ref[:, cols] = jnp.concatenate(halves, axis=0).T

    kv_spec = pl.BlockSpec((1, 2, TT, LANES), lambda j, i: (j, 0, 0, 0))
    out_cols = pl.BlockSpec((tq, group), lambda j, i: (i, j))
    return pl.pallas_call(
        body, name=name, grid=(2, T // tq),
        out_shape=[jax.ShapeDtypeStruct((T, ATTN_WIDTH), F32), jax.ShapeDtypeStruct((2, 4, TT, LANES), F32)],
        in_specs=[pl.BlockSpec((tq, group), lambda j, i: (i + off, j)), out_cols, out_cols,
                  pl.BlockSpec((1, tq, LANES), lambda j, i: (j, i, 0)),
                  kv_spec, kv_spec, pl.BlockSpec((1, HEAD_DIM, TT), lambda j, i: (j, 0, 0))],
        out_specs=[out_cols, pl.BlockSpec((1, 4, TT, LANES), lambda j, i: (j, 0, 0, 0))],
        compiler_params=pltpu.CompilerParams(dimension_semantics=("parallel", "arbitrary")),
    )(q, dao, ao, lse, kpad, vpad, kt)


def _final_fwd_bwd(h, g, target, y, gt, name):
    R, Dm = h.shape
    tm = _tile(R, ROW_BLOCK, 8)

    def body(h_ref, g_ref, t_ref, y_ref, gt_ref, dh_ref, loss_ref, dg_ref, dy_ref, dgt_ref, dsum_ref):
        i = pl.program_id(0)

        @pl.when(i == 0)
        def _():
            for ref in (loss_ref, dg_ref, dgt_ref, dsum_ref):
                ref[...] = jnp.zeros_like(ref)

        hv = h_ref[...]
        r = lax.rsqrt(jnp.mean(hv * hv, axis=-1, keepdims=True) + EPS)
        n = hv * r
        diff = n * g_ref[...] - t_ref[...]
        loss_ref[...] += jnp.sum(diff * diff)
        dout = diff * (1.0 / Dm)
        dg_ref[...] += jnp.sum(dout * n, axis=0, keepdims=True)
        dn = dout * g_ref[...]
        dh = r * (dn - n * jnp.mean(dn * n, axis=-1, keepdims=True))
        dh_ref[...] = dh
        _gate_grads(dh, y_ref, gt_ref, dy_ref, dgt_ref, dsum_ref)

    vec = jax.ShapeDtypeStruct((1, Dm), F32)
    return pl.pallas_call(
        body, name=name, grid=(R // tm,),
        out_shape=[jax.ShapeDtypeStruct((R, Dm), F32), jax.ShapeDtypeStruct((1, LANES), F32), vec,
                   jax.ShapeDtypeStruct((R, Dm), BF16), vec, vec],
        in_specs=[_row_spec(tm, Dm), _vec_spec(Dm), _row_spec(tm, Dm), _row_spec(tm, Dm), _vec_spec(Dm)],
        out_specs=[_row_spec(tm, Dm), _vec_spec(LANES), _vec_spec(Dm), _row_spec(tm, Dm), _vec_spec(Dm),
                   _vec_spec(Dm)],
        compiler_params=pltpu.CompilerParams(dimension_semantics=("arbitrary",)),
    )(h, g, target, y, gt)


MOD_ROWS = 16


def _mod_fwd(c_rows, w_mod, name):
    L, Dm, n = w_mod.shape

    def body(c_ref, w_ref, o_ref):
        o_ref[0] = _dot3(_silu(c_ref[...]), w_ref[0], NN)

    return pl.pallas_call(
        body, name=name, grid=(L,),
        out_shape=jax.ShapeDtypeStruct((L, MOD_ROWS, n), F32),
        in_specs=[pl.BlockSpec((MOD_ROWS, Dm), lambda l: (0, 0)), pl.BlockSpec((1, Dm, n), lambda l: (l, 0, 0))],
        out_specs=pl.BlockSpec((1, MOD_ROWS, n), lambda l: (l, 0, 0)),
        compiler_params=pltpu.CompilerParams(dimension_semantics=("parallel",)),
    )(c_rows, w_mod)


def _mod_bwd(c_rows_t, dmod, w_mod, name):
    L, Dm, n = w_mod.shape

    def body(ct_ref, d_ref, w_ref, gw_ref, ds_ref):
        dm = d_ref[0]
        gw_ref[0] = _dot3(_silu(ct_ref[...]), dm, NN)
        ds_ref[0] = _dot3(dm[:MOD_ROWS], w_ref[0], NT)

    return pl.pallas_call(
        body, name=name, grid=(L,),
        out_shape=[jax.ShapeDtypeStruct((L, Dm, n), F32), jax.ShapeDtypeStruct((L, MOD_ROWS, Dm), F32)],
        in_specs=[pl.BlockSpec((Dm, LANES), lambda l: (0, 0)), pl.BlockSpec((1, LANES, n), lambda l: (l, 0, 0)),
                  pl.BlockSpec((1, Dm, n), lambda l: (l, 0, 0))],
        out_specs=[pl.BlockSpec((1, Dm, n), lambda l: (l, 0, 0)),
                   pl.BlockSpec((1, MOD_ROWS, Dm), lambda l: (l, 0, 0))],
        compiler_params=pltpu.CompilerParams(dimension_semantics=("parallel",)),
    )(c_rows_t, dmod, w_mod)


def _adam_update(w, g, m, v):
    c1 = 1.0 - ADAM_B1 ** ADAM_STEP
    c2 = 1.0 - ADAM_B2 ** ADAM_STEP
    mn = ADAM_B1 * m + (1.0 - ADAM_B1) * g
    vn = ADAM_B2 * v + (1.0 - ADAM_B2) * (g * g)
    return -ADAM_LR * ((mn / c1) / (jnp.sqrt(vn / c2) + ADAM_EPS) + ADAM_WD * w), mn, vn


def _adamw(w, g, m, v, name):
    R, Cw = w.shape
    tm = _tile(R, ADAM_ROWS, 8)

    def body(w_ref, g_ref, m_ref, v_ref, d_ref, mo_ref, vo_ref):
        d_ref[...], mo_ref[...], vo_ref[...] = _adam_update(w_ref[...], g_ref[...], m_ref[...], v_ref[...])

    spec = pl.BlockSpec((tm, Cw), lambda i: (i, 0))
    return pl.pallas_call(
        body, name=name, grid=(R // tm,),
        out_shape=[jax.ShapeDtypeStruct((R, Cw), F32)] * 3,
        in_specs=[spec] * 4, out_specs=[spec] * 3,
        compiler_params=pltpu.CompilerParams(dimension_semantics=("parallel",)),
    )(w, g, m, v)


def _adamw_recv(w, m, v, recvs, name):
    L, R, n = w.shape
    tm = _tile(R, ADAM_ROWS, 8)
    nblk = R // tm
    parts = [r.reshape(N_DEV, R, n) for r in recvs]

    def body(*refs):
        w_ref, m_ref, v_ref = refs[:3]
        part_refs = refs[3:3 + L]
        g_ref, d_ref, mo_ref, vo_ref, gsum = refs[3 + L:]
        l = pl.program_id(0)
        for ll in range(L):
            @pl.when(l == ll)
            def _(ll=ll):
                acc = part_refs[ll][0].astype(F32)
                for s in range(1, N_DEV):
                    acc = acc + part_refs[ll][s].astype(F32)
                gsum[...] = acc
        g = gsum[...]
        g_ref[0] = g
        d_ref[0], mo_ref[0], vo_ref[0] = _adam_update(w_ref[0], g, m_ref[0], v_ref[0])

    def part_spec(ll):
        return pl.BlockSpec((N_DEV, tm, n), lambda l, i: (0, jnp.where(l == ll, i, jnp.where(l < ll, 0, nblk - 1)), 0))

    spec = pl.BlockSpec((1, tm, n), lambda l, i: (l, i, 0))
    return pl.pallas_call(
        body, name=name, grid=(L, nblk),
        out_shape=[jax.ShapeDtypeStruct((L, R, n), F32)] * 4,
        in_specs=[spec] * 3 + [part_spec(ll) for ll in range(L)], out_specs=[spec] * 4,
        scratch_shapes=[pltpu.VMEM((tm, n), F32)],
        compiler_params=pltpu.CompilerParams(dimension_semantics=("parallel", "parallel")),
    )(w, m, v, *parts)


def _pack(parts, row_mult=8):
    flat, offs, pos = [], [], 0
    for t in parts:
        t = t.reshape(-1).astype(F32)
        size = -(-t.shape[0] // LANES) * LANES
        flat.append(jnp.pad(t, (0, size - t.shape[0])))
        offs.append(pos)
        pos += size
    total = -(-pos // (LANES * row_mult)) * (LANES * row_mult)
    if total > pos:
        flat.append(jnp.zeros((total - pos,), F32))
    return jnp.concatenate(flat).reshape(-1, LANES), offs


def _take(buf, off, shape):
    size = math.prod(shape)
    return buf[..., off:off + size].reshape(buf.shape[:-1] + tuple(shape))


def _rope_tables(T, ctx_rows):
    pos = jnp.arange(T)
    row = (pos // GRID_W).astype(F32)
    col = (pos % GRID_W).astype(F32)
    half = HEAD_DIM // 4
    inv = ROPE_THETA ** (-jnp.arange(0, 2 * half, 2, dtype=F32) / (2 * half))
    ang_r, ang_c = row[:, None] * inv[None, :], col[:, None] * inv[None, :]
    cos = jnp.concatenate([jnp.cos(ang_r)] * 2 + [jnp.cos(ang_c)] * 2, axis=1)
    sin = jnp.concatenate([-jnp.sin(ang_r), jnp.sin(ang_r), -jnp.sin(ang_c), jnp.sin(ang_c)], axis=1)
    cos = jnp.concatenate([jnp.ones((ctx_rows, HEAD_DIM), F32), cos], axis=0)
    sin = jnp.concatenate([jnp.zeros((ctx_rows, HEAD_DIM), F32), sin], axis=0)
    return jnp.tile(cos, (1, 2)), jnp.tile(sin, (1, 2))


def kernel(x, c, ctx, c_ctx, w_mod, b_mod, g_mix, g_ffn, w_ffn_in, w_ffn_out, w_in, q_gain, k_gain, w_sp, b_sp, w_out, w_pw1, b_pw1, w_dw, b_dw, ln_g, ln_b, w_pw2, b_pw2, g_final, loss_target, m_c_ctx, m_w_mod, m_b_mod, m_g_mix, m_g_ffn, m_w_ffn_in, m_w_ffn_out, m_w_in, m_q_gain, m_k_gain, m_w_sp, m_b_sp, m_w_out, m_w_pw1, m_b_pw1, m_w_dw, m_b_dw, m_ln_g, m_ln_b, m_w_pw2, m_b_pw2, m_g_final, v_c_ctx, v_w_mod, v_b_mod, v_g_mix, v_g_ffn, v_w_ffn_in, v_w_ffn_out, v_w_in, v_q_gain, v_k_gain, v_w_sp, v_b_sp, v_w_out, v_w_pw1, v_b_pw1, v_w_dw, v_b_dw, v_ln_g, v_ln_b, v_w_pw2, v_b_pw2, v_g_final):
    weights = dict(c_ctx=c_ctx, w_mod=w_mod, b_mod=b_mod, g_mix=g_mix, g_ffn=g_ffn, w_ffn_in=w_ffn_in,
                   w_ffn_out=w_ffn_out, w_in=w_in, q_gain=q_gain, k_gain=k_gain, w_sp=w_sp, b_sp=b_sp,
                   w_out=w_out, w_pw1=w_pw1, b_pw1=b_pw1, w_dw=w_dw, b_dw=b_dw, ln_g=ln_g, ln_b=ln_b,
                   w_pw2=w_pw2, b_pw2=b_pw2, g_final=g_final)
    moments_m = dict(c_ctx=m_c_ctx, w_mod=m_w_mod, b_mod=m_b_mod, g_mix=m_g_mix, g_ffn=m_g_ffn,
                     w_ffn_in=m_w_ffn_in, w_ffn_out=m_w_ffn_out, w_in=m_w_in, q_gain=m_q_gain,
                     k_gain=m_k_gain, w_sp=m_w_sp, b_sp=m_b_sp, w_out=m_w_out, w_pw1=m_w_pw1,
                     b_pw1=m_b_pw1, w_dw=m_w_dw, b_dw=m_b_dw, ln_g=m_ln_g, ln_b=m_ln_b, w_pw2=m_w_pw2,
                     b_pw2=m_b_pw2, g_final=m_g_final)
    moments_v = dict(c_ctx=v_c_ctx, w_mod=v_w_mod, b_mod=v_b_mod, g_mix=v_g_mix, g_ffn=v_g_ffn,
                     w_ffn_in=v_w_ffn_in, w_ffn_out=v_w_ffn_out, w_in=v_w_in, q_gain=v_q_gain,
                     k_gain=v_k_gain, w_sp=v_w_sp, b_sp=v_b_sp, w_out=v_w_out, w_pw1=v_w_pw1,
                     b_pw1=v_b_pw1, w_dw=v_w_dw, b_dw=v_b_dw, ln_g=v_ln_g, ln_b=v_ln_b, w_pw2=v_w_pw2,
                     b_pw2=v_b_pw2, g_final=v_g_final)
    names = list(weights)

    T, C = x.shape[1], ctx.shape[1]
    Dm = D_MODEL
    me = 4 * lax.axis_index("x") + 2 * lax.axis_index("y") + lax.axis_index("c")
    h0 = x[0]
    ctx2 = ctx[0]
    target = loss_target[0]

    small_sharded = (("w_dw", w_dw[0]), ("b_pw1", b_pw1), ("b_dw", b_dw), ("ln_g", ln_g), ("ln_b", ln_b),
                     ("b_pw2", b_pw2))
    buf1, offs1 = _pack([c] + [t for _, t in small_sharded])
    w_in_t, m_w_in_t, v_w_in_t = (jnp.swapaxes(t, 1, 2) for t in (w_in, m_w_in, v_w_in))
    w_ffi_t, m_w_ffi_t, v_w_ffi_t = (jnp.swapaxes(t, 1, 2) for t in (w_ffn_in, m_w_ffn_in, v_w_ffn_in))
    got1, W_in_t = _all_gather([buf1, w_in_t[0].astype(BF16)], "gather_cond", False)
    got1 = got1.reshape(N_DEV, -1)
    c_all = _take(got1, offs1[0], (Dm,))
    full_small = {}
    for (nm, t), off in zip(small_sharded, offs1[1:]):
        seg = _take(got1, off, t.shape)
        full_small[nm] = jnp.moveaxis(seg, 0, -2).reshape(t.shape[:-1] + (N_DEV * t.shape[-1],))
    w_dw_f, b_pw1_f = full_small["w_dw"], full_small["b_pw1"]
    b_dw_f, ln_g_f, ln_b_f, b_pw2_f = (full_small[k] for k in ("b_dw", "ln_g", "ln_b", "b_pw2"))

    c_rows = jnp.concatenate([c_all, c_ctx[None, :], jnp.zeros((MOD_ROWS - N_DEV - 1, Dm), F32)], axis=0)
    mod_part = _mod_fwd(c_rows, w_mod, "mod_fwd")
    n_mod = w_mod.shape[2]
    got2 = _all_gather([mod_part.reshape(-1, LANES)], "gather_mod", True)[0]
    mod_all = got2.reshape(N_DEV, 2, MOD_ROWS, n_mod).transpose(1, 2, 0, 3).reshape(2, MOD_ROWS, N_DEV * n_mod)
    mod_all = mod_all + b_mod[:, None, :]
    my_mod = lax.dynamic_index_in_dim(mod_all, me, axis=1, keepdims=False)
    sh1, sc1, gt1, sh2, sc2, gt2 = ([my_mod[l:l + 1, k * Dm:(k + 1) * Dm] for l in range(2)] for k in range(6))
    csh1 = mod_all[0, N_DEV:N_DEV + 1, 0:Dm]
    csc1 = mod_all[0, N_DEV:N_DEV + 1, Dm:2 * Dm]

    behind = got2[0:1, 0:1] * 0.0
    gather_groups = [[w_out[0]], [w_ffi_t[0], w_ffn_out[0]], [w_pw1[0], w_pw2[0]], [w_ffi_t[1], w_ffn_out[1]]]
    gathers = [_push_begin([(t + behind).astype(BF16) for t in grp], True, f"gather_start{k}")
               for k, grp in enumerate(gather_groups)]
    started = sum(h[4][0:1, 0:1] for h in gathers)

    def gathered(k, after):
        return _push_end(gathers[k], after, f"gather_wait{k}")[1]

    def ffn_weights(k, after):
        wi, wo = gathered(k, after)
        return wi.reshape(N_DEV, FF_SHARD, Dm), wo.reshape(N_DEV // 2, FF_SHARD, Dm)

    def col_gathered(t, n):
        return t.reshape(N_DEV, Dm, n).transpose(1, 0, 2).reshape(Dm, N_DEV * n)

    W_ffi, W_ffo = [None, None], [None, None]

    g_mix_r = [g_mix[l:l + 1] for l in range(2)]
    g_ffn_r = [g_ffn[l:l + 1] for l in range(2)]
    g_fin = g_final[None, :]

    cos, sin = _rope_tables(T, C)
    qg = jnp.tile(q_gain, (1, 2))
    kg = jnp.tile(k_gain, (1, 2))
    lane_head = jnp.arange(LANES) // HEAD_DIM
    bd = (lane_head[:, None] == lane_head[None, :]).astype(BF16)
    w_sp0 = w_sp[0]
    w_spt0 = w_sp0.transpose(0, 2, 1)
    b_spt0 = b_sp[0].T

    XM = _norm_mod_fwd_cat(ctx2, h0, g_mix_r[0], csc1, csh1, sc1[0] + started, sh1[0], "norm_mix0")
    P = _mm(XM, W_in_t, "nt", "in_proj", tm=1088, tn=IN_WIDTH)
    qh, kpad, vpad, kt, ao = _mix_prep_fwd(P, C, cos, sin, qg, kg, bd, w_sp0, b_spt0, "mix_prep")
    ao, lse = _attn_fwd(qh, kpad, vpad, ao, C, "attn_fwd")
    W_out, = gathered(0, ao)
    h1, y0, xf0 = _mm(ao, W_out, "nn", "out_proj", res=h0, gate=gt1[0], raw_out=True,
                      norm=(g_ffn_r[0], sc2[0], sh2[0]))

    def ffn_fwd(h_in, xf, l, norm_next):
        W_ffi[l], W_ffo[l] = ffn_weights(1 + 2 * l, xf)
        gu, act = _ffn_in_swiglu(xf, W_ffi[l], f"ffn_in{l}")
        outs = _mm_sum_shards(act, W_ffo[l], "nn", f"ffn_out{l}", res=h_in, gate=gt2[l], raw_out=True,
                              norm=norm_next)
        return tuple(outs) + (None,) * (3 - len(outs)) + (gu, act)

    h2, f0, xm1, gu0, act0 = ffn_fwd(h1, xf0, 0, (g_mix_r[1], sc1[1], sh1[1]))

    W_pw1, W_pw2 = gathered(2, xm1)
    W_pw1 = col_gathered(W_pw1, 2 * Dm // N_DEV)
    ag = _mm(xm1, W_pw1, "nn", "pw1", BF16, bias=b_pw1_f)
    hg = _glu_fwd(ag, "glu")
    hd = _conv_fwd(hg, w_dw_f, b_dw_f, "conv")
    hs = _ln_silu_fwd(hd, ln_g_f, ln_b_f, "ln_silu")
    h3, y1, xf1 = _mm(hs, W_pw2, "nn", "pw2", bias=b_pw2_f, res=h2, gate=gt1[1], raw_out=True,
                      norm=(g_ffn_r[1], sc2[1], sh2[1]))
    h4, f1, _, gu1, act1 = ffn_fwd(h3, xf1, 1, None)

    dh4, sq_err, dg_final, df1, dgt2_1, _ = _final_fwd_bwd(h4, g_fin, target, f1, gt2[1], "loss_head")
    loss_local = (0.5 / Dm) * sq_err[0, 0:1]

    def col_shards(g, n):
        return g.reshape(Dm, N_DEV, n).transpose(1, 0, 2).reshape(N_DEV * Dm, n)

    def exchange_begin(k, parts):
        return _push_begin(parts, False, f"exchange_start{k}")

    def zero_of(handle):
        return handle[4][0:1, 0:1]

    def ffn_bwd(df, xf, gu, act, l):
        dw_out = _mm_tn_shard_rows(act, df, f"ffn_out_dw{l}", BF16)
        dgu = _ffn_out_dx_swiglu(df, W_ffo[l], gu, f"ffn_out_dx{l}").reshape(N_DEV, T, FF_SHARD)
        dw_in = _mm_tn_shard_rows(dgu, xf, f"ffn_in_dw{l}", BF16)
        dxf = _mm_sum_shards(dgu, W_ffi[l], "nn", f"ffn_in_dx{l}", BF16, tm=512)
        return dw_in, dw_out, dxf

    dW_ffi1, dW_ffo1, dxf1 = ffn_bwd(df1, xf1, gu1, act1, 1)
    ex0 = exchange_begin(0, [dW_ffi1.reshape(2 * D_FF, Dm), dW_ffo1.reshape(D_FF, Dm)])
    dh3, da, dsh, dy1, dgt1_1, db_pw2 = _norm_mod_bwd(h3, g_ffn_r[1], sc2[1], dxf1, dh4, "norm_ffn_bwd1",
                                                       gate=(y1, gt1[1] + zero_of(ex0)))
    dmod_ffn1 = (dsh, da * g_ffn_r[1], dgt2_1)
    dg_ffn1 = da * (1.0 + sc2[1])

    dW_pw2 = _mm(hs, dy1, "tn", "pw2_dw", BF16, tk=2048)
    dhs = _mm(dy1, W_pw2, "nt", "pw2_dx", BF16)
    dhd, dln_g, dln_b, db_dw = _ln_silu_bwd(dhs, hd, ln_g_f, ln_b_f, "ln_silu_bwd")
    dhg, dw_dw = _conv_bwd(dhd, hg, w_dw_f, "conv_bwd")
    dag, db_pw1 = _glu_bwd(ag, dhg, "glu_bwd")
    dW_pw1 = _mm(xm1, dag, "tn", "pw1_dw", BF16, tk=2048)
    dxm1 = _mm(dag, W_pw1, "nt", "pw1_dx", BF16, tk=2048)
    ex1 = exchange_begin(1, [col_shards(dW_pw1, 2 * Dm // N_DEV), dW_pw2])
    dh2, da, dsh, df0, dgt2_0, _ = _norm_mod_bwd(h2, g_mix_r[1], sc1[1], dxm1, dh3, "norm_mix1_bwd",
                                                 gate=(f0, gt2[0] + zero_of(ex1)))
    dmod_mix1 = (dsh, da * g_mix_r[1], dgt1_1)
    dg_mix1 = da * (1.0 + sc1[1])

    dW_ffi0, dW_ffo0, dxf0 = ffn_bwd(df0, xf0, gu0, act0, 0)
    ex2 = exchange_begin(2, [dW_ffi0.reshape(2 * D_FF, Dm), dW_ffo0.reshape(D_FF, Dm)])
    dh1, da, dsh, dy0, dgt1_0, _ = _norm_mod_bwd(h1, g_ffn_r[0], sc2[0], dxf0, dh2, "norm_ffn_bwd0",
                                                 gate=(y0, gt1[0] + zero_of(ex2)))
    dmod_ffn0 = (dsh, da * g_ffn_r[0], dgt2_0)
    dg_ffn0 = da * (1.0 + sc2[0])

    dW_out = _mm(ao, dy0, "tn", "out_proj_dw", BF16, tk=2048)
    ex_out = exchange_begin(4, [dW_out])
    dao = _mm(dy0, W_out + zero_of(ex_out).astype(BF16), "nt", "out_proj_dx", BF16)
    dq, f_acc = _attn_bwd(qh, dao, ao, lse, kpad, vpad, kt, C, "attn_bwd")
    dP, dqg, dkg, dw_sp0, db_spt0 = _mix_prep_bwd(P, dq, f_acc, dao, C, cos, sin, qg, kg, bd, w_sp0, w_spt0,
                                                  b_spt0, "mix_prep_bwd")
    dW_in_t = _mm(dP, XM, "tn", "in_proj_dw", BF16, tm=896, tk=2176)
    dXM = _mm(dP, W_in_t, "nn", "in_proj_dx", BF16, tm=1088, tk=IN_WIDTH)
    dh0, da, dsh = _norm_mod_bwd(h0, g_mix_r[0], sc1[0], dXM, dh1, "norm_mix0_bwd", dxm_row_off=C)
    _, dac, dcsh = _norm_mod_bwd(ctx2, g_mix_r[0], csc1, dXM, None, "norm_ctx_bwd")
    dmod_mix0 = (dsh, da * g_mix_r[0], dgt1_0)
    dg_mix0 = da * (1.0 + sc1[0]) + dac * (1.0 + csc1)
    dcmod = jnp.concatenate([dcsh, dac * g_mix_r[0]], axis=1)

    dmod_mine = jnp.stack([jnp.concatenate(dmod_mix0 + dmod_ffn0, axis=1)[0],
                           jnp.concatenate(dmod_mix1 + dmod_ffn1, axis=1)[0]])

    small_grads = [
        ("loss", loss_local), ("g_final", dg_final), ("g_mix", jnp.concatenate([dg_mix0, dg_mix1])),
        ("g_ffn", jnp.concatenate([dg_ffn0, dg_ffn1])),
        ("q_gain", dqg[:, :HEAD_DIM] + dqg[:, HEAD_DIM:]), ("k_gain", dkg[:, :HEAD_DIM] + dkg[:, HEAD_DIM:]),
        ("w_sp", dw_sp0[None]), ("b_sp", db_spt0.T[None]), ("b_pw1", db_pw1), ("w_dw", dw_dw[None]),
        ("b_dw", db_dw), ("ln_g", dln_g), ("ln_b", dln_b), ("b_pw2", db_pw2), ("dcmod", dcmod),
        ("dmod", dmod_mine),
    ]
    buf3, offs3 = _pack([t for _, t in small_grads])
    off3 = {nm: off for (nm, _), off in zip(small_grads, offs3)}
    shape3 = {nm: t.shape for nm, t in small_grads}
    small_push = _push_begin([buf3], True, "small_grads_start")
    ex3 = exchange_begin(3, [dW_in_t + zero_of(small_push).astype(BF16)])

    grads, delta, new_m, new_v = {}, {}, {}, {}

    def exchanged(k, handle, after):
        return _push_end(handle, after, f"exchange_wait{k}")[1]

    def adamw_big(nm, parts, transposed=False, wmv=None):
        w3, m3, v3 = wmv if wmv is not None else (weights[nm], moments_m[nm], moments_v[nm])
        outs4 = _adamw_recv(w3, m3, v3, parts, f"adamw_{nm}")
        if transposed:
            outs4 = [jnp.swapaxes(t, 1, 2) for t in outs4]
        grads[nm], delta[nm], new_m[nm], new_v[nm] = outs4

    pushed = ex3[4]
    r_ffi1, r_ffo1 = exchanged(0, ex0, pushed)
    r_pw1, r_pw2 = exchanged(1, ex1, pushed)
    r_ffi0, r_ffo0 = exchanged(2, ex2, pushed)
    r_out, = exchanged(4, ex_out, pushed)
    adamw_big("w_ffn_in", [r_ffi0, r_ffi1], True, (w_ffi_t, m_w_ffi_t, v_w_ffi_t))
    adamw_big("w_ffn_out", [r_ffo0, r_ffo1])
    adamw_big("w_pw1", [r_pw1])
    adamw_big("w_pw2", [r_pw2])
    adamw_big("w_out", [r_out])

    got3 = _push_end(small_push, delta["w_out"], "small_grads_wait")[1][0].reshape(N_DEV, buf3.shape[0], LANES)
    sum3 = _sum_devices(got3, "sum_small_grads").reshape(-1)

    def summed(nm):
        return _take(sum3, off3[nm], shape3[nm])

    loss = summed("loss")[0]
    dcmod_sum = summed("dcmod")
    dmod_rows = _take(got3.reshape(N_DEV, -1), off3["dmod"], (2, 6 * Dm)).transpose(1, 0, 2)
    ctx_row = jnp.concatenate([jnp.pad(dcmod_sum, ((0, 0), (0, 4 * Dm))), jnp.zeros((1, 6 * Dm), F32)])
    dmod_all = jnp.concatenate([dmod_rows, ctx_row[:, None, :],
                                jnp.zeros((2, LANES - N_DEV - 1, 6 * Dm), F32)], axis=1)
    grads["b_mod"] = summed("dmod") + ctx_row
    dmod_shard = lax.dynamic_slice_in_dim(dmod_all, me * n_mod, n_mod, axis=2)
    c_rows_t = jnp.pad(c_rows.T, ((0, 0), (0, LANES - MOD_ROWS)))
    grads["w_mod"], ds_part = _mod_bwd(c_rows_t, dmod_shard, w_mod, "mod_bwd")

    buf4, _ = _pack([ds_part[0, N_DEV]])
    got4 = _all_gather([buf4], "gather_c_ctx_grad", True)[0].reshape(N_DEV, buf4.shape[0], LANES)
    ds_ctx = _sum_devices(got4, "sum_c_ctx_grad").reshape(-1)[:Dm]
    grads["c_ctx"] = ds_ctx * _dsilu(c_ctx)

    for nm in ("g_final", "g_mix", "g_ffn", "q_gain", "k_gain", "w_sp", "b_sp"):
        grads[nm] = summed(nm).reshape(weights[nm].shape)
    for nm in ("b_pw1", "w_dw", "b_dw", "ln_g", "ln_b", "b_pw2"):
        n_loc = weights[nm].shape[-1]
        grads[nm] = lax.dynamic_slice_in_dim(summed(nm), me * n_loc, n_loc, axis=-1).reshape(weights[nm].shape)

    shp = w_mod.shape
    outs = _adamw(w_mod.reshape(-1, shp[-1]), grads["w_mod"].reshape(-1, shp[-1]),
                  m_w_mod.reshape(-1, shp[-1]), v_w_mod.reshape(-1, shp[-1]), "adamw_w_mod")
    delta["w_mod"], new_m["w_mod"], new_v["w_mod"] = (o.reshape(shp) for o in outs)
    big_names = ("w_mod", "w_ffn_in", "w_ffn_out", "w_in", "w_out", "w_pw1", "w_pw2")
    small_names = [nm for nm in names if nm not in big_names]
    packs = [_pack([src[nm] for nm in small_names]) for src in (weights, grads, moments_m, moments_v)]
    offs_s = packs[0][1]
    outs = _adamw(*[pk[0] for pk in packs], "adamw_small")
    for o, dst in zip(outs, (delta, new_m, new_v)):
        o = o.reshape(-1)
        for nm, off in zip(small_names, offs_s):
            dst[nm] = _take(o, off, weights[nm].shape)
    r_in, = exchanged(3, ex3, outs[0])
    adamw_big("w_in", [r_in], True, (w_in_t, m_w_in_t, v_w_in_t))

    return (loss, dh0[None], *[grads[n] for n in names], *[delta[n] for n in names],
            *[new_m[n] for n in names], *[new_v[n] for n in names])
```

```python
import math

import jax
import jax.numpy as jnp
from jax import lax
from jax.experimental import pallas as pl
from jax.experimental.pallas import tpu as pltpu

F32 = jnp.float32
BF16 = jnp.bfloat16
MESH = pl.DeviceIdType.MESH

N_DEV = 8
D_MODEL = 1024
EPS = 1e-6
HEAD_DIM = 64
ATTN_WIDTH = 512
SG_WIDTH = 512
N_SG_GROUPS = 4
CHUNK = 128
IN_WIDTH = 1792
D_FF = 2816
FF_SHARD = 2 * D_FF // N_DEV
CONV_WIDTH = 31
CONV_HALO = 16
GRID_W = 64
ROPE_THETA = 10000.0
LANES = 128
SUBLANES = 8
ROW_BLOCK = 1024
ADAM_ROWS = 256
ADAM_LR, ADAM_B1, ADAM_B2, ADAM_EPS, ADAM_WD, ADAM_STEP = 0.001, 0.9, 0.999, 1e-08, 0.01, 10


def _tile(n, target, mult=LANES):
    best = None
    for t in range(mult, min(n, target) + 1, mult):
        if n % t == 0:
            best = t
    return best if best is not None else n


def _sigmoid(x):
    return 1.0 / (1.0 + jnp.exp(-x))


def _silu(x):
    return x * _sigmoid(x)


def _dsilu(x):
    s = _sigmoid(x)
    return s * (1.0 + x * (1.0 - s))


_GELU_K = math.sqrt(2.0 / math.pi)


def _gelu(x):
    return 0.5 * x * (1.0 + jnp.tanh(_GELU_K * (x + 0.044715 * x * x * x)))


def _gelu_and_grad(x):
    x2 = x * x
    t = jnp.tanh(_GELU_K * x * (1.0 + 0.044715 * x2))
    half = 0.5 * (1.0 + t)
    return x * half, half + 0.5 * x * (1.0 - t * t) * _GELU_K * (1.0 + 3.0 * 0.044715 * x2)


def _split_bf16(x):
    hi = x.astype(BF16)
    lo = (x - hi.astype(F32)).astype(BF16)
    return hi, lo


def _dot(a, b, dims):
    return lax.dot_general(a, b, (dims, ((), ())), preferred_element_type=F32)


def _dot3(a, b, dims):
    ah, al = _split_bf16(a)
    bh, bl = _split_bf16(b)
    return _dot(ah, bh, dims) + _dot(ah, bl, dims) + _dot(al, bh, dims)


NN = ((1,), (0,))
NT = ((1,), (1,))
TN = ((0,), (0,))


def _all_gather(xs, name, in_vmem):
    n_arr = len(xs)

    def body(*refs):
        x_refs, out_refs = refs[:n_arr], refs[n_arr:2 * n_arr]
        send_sems, recv_sems, local_sems = refs[2 * n_arr:]
        x, y, c = lax.axis_index("x"), lax.axis_index("y"), lax.axis_index("c")
        me, sibling = (x, y, c), (x, y, 1 - c)
        chips = [(1 - x, y), (x, 1 - y), (1 - x, 1 - y)]

        def rows(a, px, py, pc):
            m_per = xs[a].shape[0]
            return out_refs[a].at[pl.ds((4 * px + 2 * py + pc) * m_per, m_per), :]

        def copy(a, k, block, to, src=None):
            return pltpu.make_async_remote_copy(
                src_ref=rows(a, *block) if src is None else src,
                dst_ref=rows(a, *block),
                send_sem=send_sems.at[7 * a + k],
                recv_sem=recv_sems.at[7 * a + k],
                device_id=to,
                device_id_type=MESH,
            )

        mine, first, passed = [], [], []
        for a in range(n_arr):
            mine.append(pltpu.make_async_copy(x_refs[a], rows(a, *me), local_sems.at[a]))
            mine[-1].start()
            first.append(copy(a, 0, me, sibling, src=x_refs[a]))
            first += [copy(a, 1 + j, me, (*chip, c), src=x_refs[a]) for j, chip in enumerate(chips)]
        for cp in first:
            cp.start()
        for a in range(n_arr):
            for j, chip in enumerate(chips):
                copy(a, 1 + j, (*chip, c), me).wait_recv()
                passed.append(copy(a, 4 + j, (*chip, c), sibling))
                passed[-1].start()
        for a in range(n_arr):
            copy(a, 0, sibling, me).wait_recv()
            for j, chip in enumerate(chips):
                copy(a, 4 + j, (*chip, 1 - c), me).wait_recv()
        for cp in first + passed:
            cp.wait_send()
        for cp in mine:
            cp.wait()

    space = pltpu.VMEM if in_vmem else pl.ANY
    return pl.pallas_call(
        body,
        name=name,
        out_shape=[jax.ShapeDtypeStruct((N_DEV * t.shape[0], t.shape[1]), t.dtype) for t in xs],
        in_specs=[pl.BlockSpec(memory_space=space)] * n_arr,
        out_specs=[pl.BlockSpec(memory_space=space)] * n_arr,
        scratch_shapes=[
            pltpu.SemaphoreType.DMA((7 * n_arr,)),
            pltpu.SemaphoreType.DMA((7 * n_arr,)),
            pltpu.SemaphoreType.DMA((n_arr,)),
        ],
    )(*xs)


HBM_SPEC = pl.BlockSpec(memory_space=pltpu.HBM)
SEM_SPEC = pl.BlockSpec(memory_space=pltpu.SEMAPHORE)
DATAFLOW_EFFECT = pltpu.SideEffectType.DATAFLOW_SIDE_EFFECTING


def _peers(x, y, c):
    for k in range(1, N_DEV):
        px = 1 - x if (k >> 2) & 1 else x
        py = 1 - y if (k >> 1) & 1 else y
        pc = 1 - c if k & 1 else c
        yield k - 1, (px, py, pc), 4 * px + 2 * py + pc


def _push_copies(src_refs, land_refs, send_sems, recv_sems, shapes, whole_src):
    x, y, c = lax.axis_index("x"), lax.axis_index("y"), lax.axis_index("c")
    me = 4 * x + 2 * y + c
    for a, (m_per, _) in enumerate(shapes):
        def block(ref, idx, m_per=m_per):
            return ref.at[pl.ds(idx * m_per, m_per), :]

        for k, peer, pidx in _peers(x, y, c):
            src = src_refs[a] if whole_src else block(src_refs[a], pidx)
            sems = dict(send_sem=send_sems.at[N_DEV * a + k], recv_sem=recv_sems.at[N_DEV * a + k],
                        device_id=peer, device_id_type=MESH)
            yield (pltpu.make_async_remote_copy(src_ref=src, dst_ref=block(land_refs[a], me), **sems),
                   pltpu.make_async_remote_copy(src_ref=src, dst_ref=block(land_refs[a], pidx), **sems))


def _own_copies(src_refs, land_refs, recv_sems, shapes, whole_src):
    me = 4 * lax.axis_index("x") + 2 * lax.axis_index("y") + lax.axis_index("c")
    for a, (m_per, _) in enumerate(shapes):
        mine = pl.ds(me * m_per, m_per)
        src = src_refs[a] if whole_src else src_refs[a].at[mine, :]
        yield pltpu.make_async_copy(src, land_refs[a].at[mine, :], recv_sems.at[N_DEV * a + N_DEV - 1])


def _push_begin(srcs, whole_src, name):
    n_arr = len(srcs)
    shapes = [(t.shape[0] if whole_src else t.shape[0] // N_DEV, t.shape[1]) for t in srcs]
    lands = [lax.empty((N_DEV * m, n), t.dtype) for (m, n), t in zip(shapes, srcs)]

    def body(*refs):
        src_refs, land_refs = refs[:n_arr], refs[n_arr:2 * n_arr]
        send_sems, recv_sems = refs[2 * n_arr], refs[2 * n_arr + 1]
        token = refs[-1]
        for outgoing, _ in _push_copies(src_refs, land_refs, send_sems, recv_sems, shapes, whole_src):
            outgoing.start()
        for own in _own_copies(src_refs, land_refs, recv_sems, shapes, whole_src):
            own.start()
        token[...] = jnp.zeros_like(token)

    operands = [pltpu.with_memory_space_constraint(t, pltpu.HBM) for t in list(srcs) + lands]
    outs = pl.pallas_call(
        body, name=name,
        out_shape=(pltpu.SemaphoreType.DMA((N_DEV * n_arr,)), pltpu.SemaphoreType.DMA((N_DEV * n_arr,)),
                   *[pltpu.HBM(t.shape, t.dtype) for t in operands],
                   jax.ShapeDtypeStruct((SUBLANES, LANES), F32)),
        in_specs=[HBM_SPEC] * (2 * n_arr),
        out_specs=(SEM_SPEC, SEM_SPEC, *[HBM_SPEC] * (2 * n_arr), pl.BlockSpec(memory_space=pltpu.VMEM)),
        input_output_aliases={i: 2 + i for i in range(2 * n_arr)},
        compiler_params=pltpu.CompilerParams(has_side_effects=DATAFLOW_EFFECT),
    )(*operands)
    return outs[0], outs[1], list(outs[2:2 + n_arr]), list(outs[2 + n_arr:2 + 2 * n_arr]), outs[-1], whole_src


def _push_end(handle, after, name):
    send_sems, recv_sems, srcs, lands, _, whole_src = handle
    n_arr = len(srcs)
    shapes = [(t.shape[0] // N_DEV, t.shape[1]) for t in lands]

    def body(*refs):
        src_refs, land_refs = refs[:n_arr], refs[n_arr:2 * n_arr]
        send_sems_ref, recv_sems_ref = refs[2 * n_arr], refs[2 * n_arr + 1]
        for outgoing, incoming in _push_copies(src_refs, land_refs, send_sems_ref, recv_sems_ref, shapes, whole_src):
            outgoing.wait_send()
            incoming.wait_recv()
        for own in _own_copies(src_refs, land_refs, recv_sems_ref, shapes, whole_src):
            own.wait()

    outs = pl.pallas_call(
        body, name=name,
        out_shape=tuple(pltpu.HBM(t.shape, t.dtype) for t in srcs + lands),
        in_specs=[HBM_SPEC] * (2 * n_arr) + [SEM_SPEC, SEM_SPEC, pl.BlockSpec(memory_space=pl.ANY)],
        out_specs=tuple([HBM_SPEC] * (2 * n_arr)),
        input_output_aliases={i: i for i in range(2 * n_arr)},
        compiler_params=pltpu.CompilerParams(has_side_effects=DATAFLOW_EFFECT),
    )(*srcs, *lands, send_sems, recv_sems, after)
    return list(outs[:n_arr]), list(outs[n_arr:])


def _sum_devices(r, name, rows_per_step=ADAM_ROWS):
    _, m, n = r.shape
    tm = _tile(m, rows_per_step, 8)

    def body(r_ref, o_ref):
        acc = r_ref[0].astype(F32)
        for s in range(1, N_DEV):
            acc = acc + r_ref[s].astype(F32)
        o_ref[...] = acc

    return pl.pallas_call(
        body,
        name=name,
        grid=(m // tm,),
        out_shape=jax.ShapeDtypeStruct((m, n), F32),
        in_specs=[pl.BlockSpec((N_DEV, tm, n), lambda i: (0, i, 0))],
        out_specs=pl.BlockSpec((tm, n), lambda i: (i, 0)),
        compiler_params=pltpu.CompilerParams(dimension_semantics=("parallel",)),
    )(r)


def _get(ref):
    return ref[0] if len(ref.shape) == 3 else ref[...]


def _put(ref, val):
    if len(ref.shape) == 3:
        ref[0] = val
    else:
        ref[...] = val


def _norm_mod(hv, g, sc, sh):
    r = lax.rsqrt(jnp.mean(hv * hv, axis=-1, keepdims=True) + EPS)
    return (hv * r) * g * (1.0 + sc) + sh


def _mm_call(name, a, b, a_spec, b_spec, out_sds, o_spec, grid, dims, acc_shape, bias=None,
             res=None, gate=None, raw_out=False, vec_spec=None, norm=None):
    nk = grid[2]
    operands, in_specs = [a, b], [a_spec, b_spec]
    if bias is not None:
        operands.append(bias)
        in_specs.append(vec_spec)
    if res is not None:
        operands += [res, gate]
        in_specs += [o_spec, vec_spec]
    if norm is not None:
        assert grid[1] == 1
        operands += list(norm)
        in_specs += [vec_spec] * 3
    out_shape, out_specs = [out_sds], [o_spec]
    if raw_out:
        out_shape.append(jax.ShapeDtypeStruct(out_sds.shape, BF16))
        out_specs.append(o_spec)
    if norm is not None:
        out_shape.append(jax.ShapeDtypeStruct(out_sds.shape, BF16))
        out_specs.append(o_spec)

    def body(*refs):
        it = iter(refs)
        a_ref, b_ref = next(it), next(it)
        bias_ref = next(it) if bias is not None else None
        res_ref, gate_ref = (next(it), next(it)) if res is not None else (None, None)
        norm_refs = (next(it), next(it), next(it)) if norm is not None else None
        o_ref = next(it)
        raw_ref = next(it) if raw_out else None
        xn_ref = next(it) if norm is not None else None
        acc = next(it) if nk > 1 else None
        k = pl.program_id(2)
        part = _dot(_get(a_ref).astype(BF16), _get(b_ref).astype(BF16), dims)

        def finish(y):
            if bias_ref is not None:
                y = y + bias_ref[...]
            if raw_ref is not None:
                raw_ref[...] = y.astype(BF16)
            if res_ref is not None:
                y = res_ref[...] + gate_ref[...] * y
            _put(o_ref, y.astype(out_sds.dtype))
            if xn_ref is not None:
                xn_ref[...] = _norm_mod(y, *[r[...] for r in norm_refs]).astype(BF16)

        if nk == 1:
            finish(part)
        else:
            @pl.when(k == 0)
            def _():
                acc[...] = part

            @pl.when(k > 0)
            def _():
                acc[...] += part

            @pl.when(k == nk - 1)
            def _():
                finish(acc[...])

    outs = pl.pallas_call(
        body,
        name=name,
        grid=grid,
        out_shape=out_shape,
        in_specs=in_specs,
        out_specs=out_specs,
        scratch_shapes=[pltpu.VMEM(acc_shape, F32)] if nk > 1 else [],
        compiler_params=pltpu.CompilerParams(dimension_semantics=("parallel", "parallel", "arbitrary")),
    )(*operands)
    return outs if len(outs) > 1 else outs[0]


def _mm(a, b, mode, name, out_dtype=F32, bias=None, res=None, gate=None, raw_out=False,
        tm=1024, tn=1024, tk=1024, a_row_off=0, norm=None):
    if mode == "nn":
        K, N = b.shape
        M = a.shape[0] - a_row_off
    elif mode == "nt":
        N, K = b.shape
        M = a.shape[0] - a_row_off
    else:
        (K, M), N = a.shape, b.shape[1]
    tm, tn, tk = _tile(M, tm, LANES if mode == "tn" else 2 * SUBLANES), _tile(N, tn), _tile(K, tk)
    off = a_row_off // tm
    dims = {"nn": NN, "nt": NT, "tn": TN}[mode]
    a_spec = (pl.BlockSpec((tk, tm), lambda i, j, k: (k, i)) if mode == "tn"
              else pl.BlockSpec((tm, tk), lambda i, j, k: (i + off, k)))
    b_spec = (pl.BlockSpec((tn, tk), lambda i, j, k: (j, k)) if mode == "nt"
              else pl.BlockSpec((tk, tn), lambda i, j, k: (k, j)))
    return _mm_call(name, a, b, a_spec, b_spec, jax.ShapeDtypeStruct((M, N), out_dtype),
                    pl.BlockSpec((tm, tn), lambda i, j, k: (i, j)), (M // tm, N // tn, K // tk), dims,
                    (tm, tn), bias, res, gate, raw_out, pl.BlockSpec((1, tn), lambda i, j, k: (0, j)), norm)


def _mm_sum_shards(a3, b3, mode, name, out_dtype=F32, res=None, gate=None, raw_out=False, tm=512, norm=None):
    S, M, kk = a3.shape
    N = b3.shape[2] if mode == "nn" else b3.shape[1]
    tm = _tile(M, tm)
    dims = NN if mode == "nn" else NT
    has_res = res is not None

    def body(*refs):
        it = iter(refs)
        a_ref, b_ref = next(it), next(it)
        res_ref, gate_ref = (next(it), next(it)) if has_res else (None, None)
        norm_refs = (next(it), next(it), next(it)) if norm is not None else None
        o_ref = next(it)
        raw_ref = next(it) if raw_out else None
        xn_ref = next(it) if norm is not None else None
        y = _dot(a_ref[0], b_ref[0], dims)
        for s in range(1, S):
            y = y + _dot(a_ref[s], b_ref[s], dims)
        if raw_ref is not None:
            raw_ref[...] = y.astype(BF16)
        if has_res:
            y = res_ref[...] + gate_ref[...] * y
        o_ref[...] = y.astype(out_dtype)
        if xn_ref is not None:
            xn_ref[...] = _norm_mod(y, *[r[...] for r in norm_refs]).astype(BF16)

    tile = pl.BlockSpec((tm, N), lambda i: (i, 0))
    operands = [a3, b3] + ([res, gate] if has_res else []) + (list(norm) if norm is not None else [])
    in_specs = [pl.BlockSpec((S, tm, kk), lambda i: (0, i, 0)), pl.BlockSpec(b3.shape, lambda i: (0, 0, 0))]
    in_specs += [tile, _vec_spec(N)] if has_res else []
    in_specs += [_vec_spec(N)] * 3 if norm is not None else []
    out_shape = [jax.ShapeDtypeStruct((M, N), out_dtype)] + ([jax.ShapeDtypeStruct((M, N), BF16)] if raw_out else [])
    out_shape += [jax.ShapeDtypeStruct((M, N), BF16)] if norm is not None else []
    outs = pl.pallas_call(
        body, name=name, grid=(M // tm,),
        out_shape=out_shape, in_specs=in_specs, out_specs=[tile] * len(out_shape),
        compiler_params=pltpu.CompilerParams(dimension_semantics=("parallel",)),
    )(*operands)
    return outs if len(outs) > 1 else outs[0]


def _mm_tn_shard_rows(a3, b, name, out_dtype, tn=1024, tk=4096):
    S, T, m = a3.shape
    N = b.shape[1]
    tn, tk = _tile(N, tn), _tile(T, tk)
    return _mm_call(name, a3, b, pl.BlockSpec((1, tk, m), lambda i, j, k: (i, k, 0)),
                    pl.BlockSpec((tk, tn), lambda i, j, k: (k, j)), jax.ShapeDtypeStruct((S, m, N), out_dtype),
                    pl.BlockSpec((1, m, tn), lambda i, j, k: (i, 0, j)), (S, N // tn, T // tk), TN, (m, tn))


def _row_spec(tm, width, off=0):
    return pl.BlockSpec((tm, width), lambda i: (i + off, 0))


def _vec_spec(width):
    return pl.BlockSpec((1, width), lambda i: (0, 0))


def _norm_mod_fwd_cat(hc, h, g, csc, csh, sc, sh, name):
    (C, Dm), T = hc.shape, h.shape[0]
    tm = _tile(math.gcd(C, T), ROW_BLOCK, 8)
    off = C // tm

    def body(hc_ref, h_ref, g_ref, csc_ref, csh_ref, sc_ref, sh_ref, o_ref):
        is_ctx = pl.program_id(0) < off
        hv = jnp.where(is_ctx, hc_ref[...], h_ref[...])
        scv = jnp.where(is_ctx, csc_ref[...], sc_ref[...])
        shv = jnp.where(is_ctx, csh_ref[...], sh_ref[...])
        r = lax.rsqrt(jnp.mean(hv * hv, axis=-1, keepdims=True) + EPS)
        o_ref[...] = ((hv * r) * g_ref[...] * (1.0 + scv) + shv).astype(BF16)

    return pl.pallas_call(
        body, name=name, grid=((C + T) // tm,),
        out_shape=jax.ShapeDtypeStruct((C + T, Dm), BF16),
        in_specs=[pl.BlockSpec((tm, Dm), lambda i: (jnp.minimum(i, off - 1), 0)),
                  pl.BlockSpec((tm, Dm), lambda i: (jnp.maximum(i - off, 0), 0))] + [_vec_spec(Dm)] * 5,
        out_specs=_row_spec(tm, Dm),
        compiler_params=pltpu.CompilerParams(dimension_semantics=("parallel",)),
    )(hc, h, g, csc, csh, sc, sh)


def _gate_grads(dh, y_ref, gt_ref, dy_ref, dgt_ref, dsum_ref):
    dy = dh * gt_ref[...]
    dgt_ref[...] += jnp.sum(dh * y_ref[...].astype(F32), axis=0, keepdims=True)
    dsum_ref[...] += jnp.sum(dy, axis=0, keepdims=True)
    dy_ref[...] = dy.astype(BF16)


def _norm_mod_bwd(h, g, sc, dxm, dres, name, dxm_row_off=0, gate=None):
    R, Dm = h.shape
    tm = _tile(math.gcd(R, dxm_row_off) if dxm_row_off else R, ROW_BLOCK, 8)
    off = dxm_row_off // tm
    has_res = dres is not None
    has_gate = gate is not None

    def body(*refs):
        it = iter(refs)
        h_ref, g_ref, sc_ref, dx_ref = next(it), next(it), next(it), next(it)
        dres_ref = next(it) if has_res else None
        y_ref, gt_ref = (next(it), next(it)) if has_gate else (None, None)
        dh_ref, da_ref, dsh_ref = next(it), next(it), next(it)
        gate_out = (next(it), next(it), next(it)) if has_gate else ()
        i = pl.program_id(0)

        @pl.when(i == 0)
        def _():
            for ref in (da_ref, dsh_ref) + gate_out[1:]:
                ref[...] = jnp.zeros_like(ref)

        hv = h_ref[...]
        dx = dx_ref[...].astype(F32)
        r = lax.rsqrt(jnp.mean(hv * hv, axis=-1, keepdims=True) + EPS)
        n = hv * r
        da_ref[...] += jnp.sum(dx * n, axis=0, keepdims=True)
        dsh_ref[...] += jnp.sum(dx, axis=0, keepdims=True)
        dn = dx * (g_ref[...] * (1.0 + sc_ref[...]))
        dh = r * (dn - n * jnp.mean(dn * n, axis=-1, keepdims=True))
        if has_res:
            dh = dh + dres_ref[...]
        dh_ref[...] = dh
        if has_gate:
            _gate_grads(dh, y_ref, gt_ref, *gate_out)

    operands = [h, g, sc, dxm] + ([dres] if has_res else []) + (list(gate) if has_gate else [])
    in_specs = [_row_spec(tm, Dm), _vec_spec(Dm), _vec_spec(Dm), _row_spec(tm, Dm, off)]
    in_specs += [_row_spec(tm, Dm)] if has_res else []
    in_specs += [_row_spec(tm, Dm), _vec_spec(Dm)] if has_gate else []
    vec = jax.ShapeDtypeStruct((1, Dm), F32)
    out_shape = [jax.ShapeDtypeStruct((R, Dm), F32), vec, vec]
    out_specs = [_row_spec(tm, Dm), _vec_spec(Dm), _vec_spec(Dm)]
    if has_gate:
        out_shape += [jax.ShapeDtypeStruct((R, Dm), BF16), vec, vec]
        out_specs += [_row_spec(tm, Dm), _vec_spec(Dm), _vec_spec(Dm)]
    return pl.pallas_call(
        body, name=name, grid=(R // tm,),
        out_shape=out_shape, in_specs=in_specs, out_specs=out_specs,
        compiler_params=pltpu.CompilerParams(dimension_semantics=("arbitrary",)),
    )(*operands)


def _ffn_in_swiglu(xf, w3, name, tm=1024):
    T, K = xf.shape
    S, n, _ = w3.shape
    half = S // 2
    tm = _tile(T, tm)

    def body(a_ref, wg_ref, wu_ref, gu_ref, act_ref):
        a = a_ref[...]
        g = _dot(a, wg_ref[0], NT)
        u = _dot(a, wu_ref[0], NT)
        gu_ref[0, 0] = g.astype(BF16)
        gu_ref[1, 0] = u.astype(BF16)
        act_ref[0] = (_silu(g) * u).astype(BF16)

    return pl.pallas_call(
        body, name=name, grid=(T // tm, half),
        out_shape=[jax.ShapeDtypeStruct((2, half, T, n), BF16), jax.ShapeDtypeStruct((half, T, n), BF16)],
        in_specs=[pl.BlockSpec((tm, K), lambda i, j: (i, 0)),
                  pl.BlockSpec((1, n, K), lambda i, j: (j, 0, 0)),
                  pl.BlockSpec((1, n, K), lambda i, j: (j + half, 0, 0))],
        out_specs=[pl.BlockSpec((2, 1, tm, n), lambda i, j: (0, j, i, 0)),
                   pl.BlockSpec((1, tm, n), lambda i, j: (j, i, 0))],
        compiler_params=pltpu.CompilerParams(dimension_semantics=("parallel", "parallel")),
    )(xf, w3, w3)


def _ffn_out_dx_swiglu(df, wo, gu, name, tm=1024):
    T, Dm = df.shape
    half, n, _ = wo.shape
    tm = _tile(T, tm)

    def body(df_ref, w_ref, gu_ref, o_ref):
        da = _dot(df_ref[...], w_ref[0], NT)
        g = gu_ref[0, 0].astype(F32)
        u = gu_ref[1, 0].astype(F32)
        s = _sigmoid(g)
        o_ref[0, 0] = (da * u * (s * (1.0 + g * (1.0 - s)))).astype(BF16)
        o_ref[1, 0] = (da * (g * s)).astype(BF16)

    gu_spec = pl.BlockSpec((2, 1, tm, n), lambda i, j: (0, j, i, 0))
    return pl.pallas_call(
        body, name=name, grid=(T // tm, half),
        out_shape=jax.ShapeDtypeStruct(gu.shape, BF16),
        in_specs=[pl.BlockSpec((tm, Dm), lambda i, j: (i, 0)),
                  pl.BlockSpec((1, n, Dm), lambda i, j: (j, 0, 0)), gu_spec],
        out_specs=gu_spec,
        compiler_params=pltpu.CompilerParams(dimension_semantics=("parallel", "parallel")),
    )(df, wo, gu)


def _glu_fwd(ag, name):
    R = ag.shape[0]
    tm = _tile(R, ROW_BLOCK, 8)

    def body(ag_ref, o_ref):
        o_ref[...] = ag_ref[:, :D_MODEL].astype(F32) * _sigmoid(ag_ref[:, D_MODEL:].astype(F32))

    return pl.pallas_call(
        body, name=name, grid=(R // tm,),
        out_shape=jax.ShapeDtypeStruct((R, D_MODEL), F32),
        in_specs=[_row_spec(tm, 2 * D_MODEL)],
        out_specs=_row_spec(tm, D_MODEL),
        compiler_params=pltpu.CompilerParams(dimension_semantics=("parallel",)),
    )(ag)


def _glu_bwd(ag, dhg, name):
    R = ag.shape[0]
    tm = _tile(R, ROW_BLOCK, 8)

    def body(ag_ref, dh_ref, o_ref, s_ref):
        i = pl.program_id(0)

        @pl.when(i == 0)
        def _():
            s_ref[...] = jnp.zeros_like(s_ref)

        a = ag_ref[:, :D_MODEL].astype(F32)
        s = _sigmoid(ag_ref[:, D_MODEL:].astype(F32))
        dh = dh_ref[...]
        da = dh * s
        dg = dh * a * s * (1.0 - s)
        o_ref[:, :D_MODEL] = da.astype(BF16)
        o_ref[:, D_MODEL:] = dg.astype(BF16)
        s_ref[:, :D_MODEL] += jnp.sum(da, axis=0, keepdims=True)
        s_ref[:, D_MODEL:] += jnp.sum(dg, axis=0, keepdims=True)

    return pl.pallas_call(
        body, name=name, grid=(R // tm,),
        out_shape=[jax.ShapeDtypeStruct((R, 2 * D_MODEL), BF16), jax.ShapeDtypeStruct((1, 2 * D_MODEL), F32)],
        in_specs=[_row_spec(tm, 2 * D_MODEL), _row_spec(tm, D_MODEL)],
        out_specs=[_row_spec(tm, 2 * D_MODEL), _vec_spec(2 * D_MODEL)],
        compiler_params=pltpu.CompilerParams(dimension_semantics=("arbitrary",)),
    )(ag, dhg)


def _halo_specs(tm, nblk, width):
    per = tm // CONV_HALO
    prev = pl.BlockSpec((CONV_HALO, width), lambda i: (jnp.maximum(i * per - 1, 0), 0))
    nxt = pl.BlockSpec((CONV_HALO, width), lambda i: (jnp.minimum((i + 1) * per, nblk * per - 1), 0))
    return prev, nxt


def _fill_halo(scr, prev_ref, cur_ref, next_ref, i, nblk, tm):
    scr[0:CONV_HALO, :] = jnp.where(i > 0, prev_ref[...], 0.0)
    scr[CONV_HALO:CONV_HALO + tm, :] = cur_ref[...]
    scr[CONV_HALO + tm:2 * CONV_HALO + tm, :] = jnp.where(i < nblk - 1, next_ref[...], 0.0)


CONV_ROWS = 128


CONV_REACH = (CONV_WIDTH // SUBLANES) * SUBLANES


def _windows(scr, stage, cols, tm):
    for r in range(SUBLANES):
        if r:
            stage[r] = scr[pl.ds(r, tm + CONV_REACH), cols]
        for a in range(CONV_REACH // SUBLANES + 1):
            off = SUBLANES * a + r
            if 1 <= off <= CONV_WIDTH:
                yield off, (stage[r, SUBLANES * a:SUBLANES * a + tm, :] if r
                            else scr[SUBLANES * a:SUBLANES * a + tm, cols])


def _conv_fwd(hg, w_dw, b_dw, name):
    R, Dm = hg.shape
    tm = _tile(R, CONV_ROWS, CONV_HALO)
    nblk = R // tm
    prev_spec, next_spec = _halo_specs(tm, nblk, Dm)

    def body(prev_ref, cur_ref, next_ref, w_ref, bdw_ref, hd_ref, scr, stage):
        _fill_halo(scr, prev_ref, cur_ref, next_ref, pl.program_id(0), nblk, tm)
        for cb in range(Dm // LANES):
            cols = slice(cb * LANES, (cb + 1) * LANES)
            acc = jnp.zeros((tm, LANES), F32) + bdw_ref[:, cols]
            for off, win in _windows(scr, stage, cols, tm):
                acc = acc + w_ref[off - 1:off, cols] * win
            hd_ref[:, cols] = acc

    return pl.pallas_call(
        body, name=name, grid=(nblk,),
        out_shape=jax.ShapeDtypeStruct((R, Dm), F32),
        in_specs=[prev_spec, _row_spec(tm, Dm), next_spec,
                  pl.BlockSpec((CONV_WIDTH, Dm), lambda i: (0, 0)), _vec_spec(Dm)],
        out_specs=_row_spec(tm, Dm),
        scratch_shapes=[pltpu.VMEM((tm + 2 * CONV_HALO, Dm), F32),
                        pltpu.VMEM((SUBLANES, tm + CONV_REACH, LANES), F32)],
        compiler_params=pltpu.CompilerParams(dimension_semantics=("parallel",)),
    )(hg, hg, hg, w_dw, b_dw)


def _ln_silu_fwd(hd, ln_g, ln_b, name):
    R, Dm = hd.shape
    tm = _tile(R, ROW_BLOCK, 8)

    def body(hd_ref, g_ref, b_ref, hs_ref):
        hd = hd_ref[...]
        xc = hd - jnp.mean(hd, axis=-1, keepdims=True)
        rs = lax.rsqrt(jnp.mean(xc * xc, axis=-1, keepdims=True) + EPS)
        hs_ref[...] = _silu(xc * rs * g_ref[...] + b_ref[...]).astype(BF16)

    return pl.pallas_call(
        body, name=name, grid=(R // tm,),
        out_shape=jax.ShapeDtypeStruct((R, Dm), BF16),
        in_specs=[_row_spec(tm, Dm), _vec_spec(Dm), _vec_spec(Dm)],
        out_specs=_row_spec(tm, Dm),
        compiler_params=pltpu.CompilerParams(dimension_semantics=("parallel",)),
    )(hd, ln_g, ln_b)


def _ln_silu_bwd(dhs, hd, ln_g, ln_b, name):
    R, Dm = hd.shape
    tm = _tile(R, ROW_BLOCK, 8)

    def body(dhs_ref, hd_ref, g_ref, b_ref, dhd_ref, dg_ref, db_ref, dsum_ref):
        i = pl.program_id(0)

        @pl.when(i == 0)
        def _():
            dg_ref[...] = jnp.zeros_like(dg_ref)
            db_ref[...] = jnp.zeros_like(db_ref)
            dsum_ref[...] = jnp.zeros_like(dsum_ref)

        hd = hd_ref[...]
        mu = jnp.mean(hd, axis=-1, keepdims=True)
        xc = hd - mu
        rs = lax.rsqrt(jnp.mean(xc * xc, axis=-1, keepdims=True) + EPS)
        z = xc * rs
        hl = z * g_ref[...] + b_ref[...]
        dhl = dhs_ref[...].astype(F32) * _dsilu(hl)
        dg_ref[...] += jnp.sum(dhl * z, axis=0, keepdims=True)
        db_ref[...] += jnp.sum(dhl, axis=0, keepdims=True)
        dz = dhl * g_ref[...]
        dhd = rs * (dz - jnp.mean(dz, axis=-1, keepdims=True) - z * jnp.mean(dz * z, axis=-1, keepdims=True))
        dsum_ref[...] += jnp.sum(dhd, axis=0, keepdims=True)
        dhd_ref[...] = dhd

    return pl.pallas_call(
        body, name=name, grid=(R // tm,),
        out_shape=[jax.ShapeDtypeStruct((R, Dm), F32)] + [jax.ShapeDtypeStruct((1, Dm), F32)] * 3,
        in_specs=[_row_spec(tm, Dm), _row_spec(tm, Dm), _vec_spec(Dm), _vec_spec(Dm)],
        out_specs=[_row_spec(tm, Dm), _vec_spec(Dm), _vec_spec(Dm), _vec_spec(Dm)],
        compiler_params=pltpu.CompilerParams(dimension_semantics=("arbitrary",)),
    )(dhs, hd, ln_g, ln_b)


def _conv_bwd(dhd, hg, w_dw, name):
    R, Dm = hg.shape
    tm = _tile(R, CONV_ROWS, CONV_HALO)
    nblk = R // tm
    prev_spec, next_spec = _halo_specs(tm, nblk, Dm)

    def body(dprev, dcur, dnext, gprev, gcur, gnext, w_ref, dhg_ref, dw_ref, dscr, gscr, dwp, stage):
        i = pl.program_id(0)

        @pl.when(i == 0)
        def _():
            dwp[...] = jnp.zeros_like(dwp)

        _fill_halo(dscr, dprev, dcur, dnext, i, nblk, tm)
        _fill_halo(gscr, gprev, gcur, gnext, i, nblk, tm)
        for cb in range(Dm // LANES):
            cols = slice(cb * LANES, (cb + 1) * LANES)
            acc = jnp.zeros((tm, LANES), F32)
            for off, win in _windows(dscr, stage, cols, tm):
                j = CONV_WIDTH - off
                acc = acc + w_ref[j:j + 1, cols] * win
            dhg_ref[:, cols] = acc
            d_here = dcur[:, cols]
            for off, win in _windows(gscr, stage, cols, tm):
                j = off - 1
                prod = d_here * win
                part = prod[0:SUBLANES]
                for k in range(1, tm // SUBLANES):
                    part = part + prod[k * SUBLANES:(k + 1) * SUBLANES]
                dwp[j * SUBLANES:(j + 1) * SUBLANES, cols] += part

        @pl.when(i == nblk - 1)
        def _():
            for j in range(CONV_WIDTH):
                dw_ref[j:j + 1, :] = jnp.sum(dwp[j * SUBLANES:(j + 1) * SUBLANES, :], axis=0, keepdims=True)

    return pl.pallas_call(
        body, name=name, grid=(nblk,),
        out_shape=[jax.ShapeDtypeStruct((R, Dm), F32), jax.ShapeDtypeStruct((CONV_WIDTH, Dm), F32)],
        in_specs=[prev_spec, _row_spec(tm, Dm), next_spec, prev_spec, _row_spec(tm, Dm), next_spec,
                  pl.BlockSpec((CONV_WIDTH, Dm), lambda i: (0, 0))],
        out_specs=[_row_spec(tm, Dm), pl.BlockSpec((CONV_WIDTH, Dm), lambda i: (0, 0))],
        scratch_shapes=[pltpu.VMEM((tm + 2 * CONV_HALO, Dm), F32)] * 2
        + [pltpu.VMEM((CONV_WIDTH * SUBLANES, Dm), F32), pltpu.VMEM((SUBLANES, tm + CONV_REACH, LANES), F32)],
        compiler_params=pltpu.CompilerParams(dimension_semantics=("arbitrary",)),
    )(dhd, dhd, dhd, hg, hg, hg, w_dw)


def _swap16(y, lane):
    return jnp.where((lane & 16) == 0, pltpu.roll(y, LANES - 16, 1), pltpu.roll(y, 16, 1))


def _head_mean(v, bd):
    hi, lo = _split_bf16(v)
    return (_dot(hi, bd, NN) + _dot(lo, bd, NN)) * (1.0 / HEAD_DIM)


K_COLS = (ATTN_WIDTH, ATTN_WIDTH + HEAD_DIM * 2)
V_COLS = (K_COLS[1], K_COLS[1] + HEAD_DIM * 2)
SU_COLS = (V_COLS[1], V_COLS[1] + SG_WIDTH)
SV_COLS = (SU_COLS[1], SU_COLS[1] + SG_WIDTH)


def _mix_prep_fwd(p, ctx_rows, cos, sin, qg, kg, bd, w_sp, b_spt, name):
    TT = p.shape[0]
    off = ctx_rows // CHUNK
    q_scale = HEAD_DIM ** -0.5

    def body(p_ref, cos_ref, sin_ref, qg_ref, kg_ref, bd_ref, w_ref, b_ref,
             q_ref, kp_ref, vp_ref, kt_ref, sg_ref):
        lane = lax.broadcasted_iota(jnp.int32, (CHUNK, LANES), 1)
        low = lane < HEAD_DIM
        cs, sn, bdv = cos_ref[...], sin_ref[...], bd_ref[...]

        def norm_rope(xv, gain):
            r = lax.rsqrt(_head_mean(xv * xv, bdv) + EPS)
            yv = xv * r * gain
            return yv * cs + _swap16(yv, lane) * sn

        def pad_heads(ref, t):
            tr = pltpu.roll(t, HEAD_DIM, 1)
            ref[0, 0] = jnp.where(low, t, 0.0).astype(BF16)
            ref[0, 1] = jnp.where(low, 0.0, tr).astype(BF16)
            ref[1, 0] = jnp.where(low, tr, 0.0).astype(BF16)
            ref[1, 1] = jnp.where(low, 0.0, t).astype(BF16)

        for a in range(ATTN_WIDTH // LANES):
            xv = p_ref[:, a * LANES:(a + 1) * LANES]
            q_ref[:, a * LANES:(a + 1) * LANES] = (norm_rope(xv, qg_ref[...]) * q_scale).astype(BF16)
        kh = norm_rope(p_ref[:, K_COLS[0]:K_COLS[1]], kg_ref[...])
        pad_heads(kp_ref, kh)
        pad_heads(vp_ref, p_ref[:, V_COLS[0]:V_COLS[1]])
        kht = kh.T
        kt_ref[0] = kht[:HEAD_DIM].astype(BF16)
        kt_ref[1] = kht[HEAD_DIM:].astype(BF16)
        for g in range(N_SG_GROUPS):
            u = _gelu(p_ref[:, SU_COLS[0] + g * LANES:SU_COLS[0] + (g + 1) * LANES])
            vg = _gelu(p_ref[:, SV_COLS[0] + g * LANES:SV_COLS[0] + (g + 1) * LANES])
            xc = vg - jnp.mean(vg, axis=-1, keepdims=True)
            vn = xc * lax.rsqrt(jnp.mean(xc * xc, axis=-1, keepdims=True) + EPS)
            mixed = _dot(w_ref[g].astype(BF16), vn.astype(BF16), NN) + b_ref[:, g:g + 1]
            sg_ref[:, g * LANES:(g + 1) * LANES] = (u * mixed).astype(BF16)

    def row(width):
        return pl.BlockSpec((CHUNK, width), lambda i: (i, 0))

    def whole(shape):
        return pl.BlockSpec(shape, lambda i: (0,) * len(shape))

    pad_spec = pl.BlockSpec((2, 2, CHUNK, LANES), lambda i: (0, 0, i, 0))
    return pl.pallas_call(
        body, name=name, grid=(TT // CHUNK,),
        out_shape=[jax.ShapeDtypeStruct((TT, ATTN_WIDTH), BF16),
                   jax.ShapeDtypeStruct((2, 2, TT, LANES), BF16), jax.ShapeDtypeStruct((2, 2, TT, LANES), BF16),
                   jax.ShapeDtypeStruct((2, HEAD_DIM, TT), BF16),
                   jax.ShapeDtypeStruct((TT - ctx_rows, ATTN_WIDTH + SG_WIDTH), BF16)],
        in_specs=[row(IN_WIDTH), row(LANES), row(LANES), whole((1, LANES)), whole((1, LANES)),
                  whole((LANES, LANES)), whole((N_SG_GROUPS, CHUNK, CHUNK)), whole((CHUNK, N_SG_GROUPS))],
        out_specs=[row(ATTN_WIDTH), pad_spec, pad_spec,
                   pl.BlockSpec((2, HEAD_DIM, CHUNK), lambda i: (0, 0, i)),
                   pl.BlockSpec((CHUNK, SG_WIDTH), lambda i: (jnp.maximum(i - off, 0), 1))],
        compiler_params=pltpu.CompilerParams(dimension_semantics=("arbitrary",)),
    )(p, cos, sin, qg, kg, bd, w_sp, b_spt)


def _mix_prep_bwd(p, dq, f, dao, ctx_rows, cos, sin, qg, kg, bd, w_sp, w_spt, b_spt, name):
    TT = p.shape[0]
    off = ctx_rows // CHUNK
    q_scale = HEAD_DIM ** -0.5

    def body(p_ref, dq_ref, f_ref, dsg_ref, cos_ref, sin_ref, qg_ref, kg_ref, bd_ref, w_ref, wt_ref,
             b_ref, dp_ref, dqg_ref, dkg_ref, dw_ref, db_ref):
        i = pl.program_id(0)

        @pl.when(i == 0)
        def _():
            dqg_ref[...] = jnp.zeros_like(dqg_ref)
            dkg_ref[...] = jnp.zeros_like(dkg_ref)
            dw_ref[...] = jnp.zeros_like(dw_ref)
            db_ref[...] = jnp.zeros_like(db_ref)

        latent = (i >= off).astype(F32)
        lane = lax.broadcasted_iota(jnp.int32, (CHUNK, LANES), 1)
        low = lane < HEAD_DIM
        cs, sn, bdv = cos_ref[...], sin_ref[...], bd_ref[...]

        def fold(b0):
            return jnp.where(low, f_ref[0, b0] + pltpu.roll(f_ref[0, b0 + 1], HEAD_DIM, 1),
                             pltpu.roll(f_ref[1, b0], HEAD_DIM, 1) + f_ref[1, b0 + 1])

        def norm_rope_bwd(xv, dout, gain):
            r = lax.rsqrt(_head_mean(xv * xv, bdv) + EPS)
            n = xv * r
            dy = dout * cs + _swap16(dout * sn, lane)
            dn = dy * gain
            dx = r * (dn - n * _head_mean(dn * n, bdv))
            return dx, jnp.sum(dy * n, axis=0, keepdims=True)

        for a in range(ATTN_WIDTH // LANES):
            cols = slice(a * LANES, (a + 1) * LANES)
            dx, dg = norm_rope_bwd(p_ref[:, cols], dq_ref[:, cols] * (latent * q_scale), qg_ref[...])
            dp_ref[:, cols] = dx.astype(BF16)
            dqg_ref[...] += dg
        dx, dg = norm_rope_bwd(p_ref[:, K_COLS[0]:K_COLS[1]], fold(0), kg_ref[...])
        dp_ref[:, K_COLS[0]:K_COLS[1]] = dx.astype(BF16)
        dkg_ref[...] += dg
        dp_ref[:, V_COLS[0]:V_COLS[1]] = fold(2).astype(BF16)
        for g in range(N_SG_GROUPS):
            su = p_ref[:, SU_COLS[0] + g * LANES:SU_COLS[0] + (g + 1) * LANES]
            sv = p_ref[:, SV_COLS[0] + g * LANES:SV_COLS[0] + (g + 1) * LANES]
            (u, dgelu_su), (vg, dgelu_sv) = _gelu_and_grad(su), _gelu_and_grad(sv)
            xc = vg - jnp.mean(vg, axis=-1, keepdims=True)
            rs = lax.rsqrt(jnp.mean(xc * xc, axis=-1, keepdims=True) + EPS)
            vn = xc * rs
            vnb = vn.astype(BF16)
            mixed = _dot(w_ref[g].astype(BF16), vnb, NN) + b_ref[:, g:g + 1]
            dsg = dsg_ref[:, g * LANES:(g + 1) * LANES].astype(F32) * latent
            du = dsg * mixed
            dmix = dsg * u
            dmb = dmix.astype(BF16)
            db_ref[:, g:g + 1] += jnp.sum(dmix, axis=-1, keepdims=True)
            dw_ref[g] += _dot(dmb, vnb, NT)
            dvn = _dot(wt_ref[g].astype(BF16), dmb, NN)
            dvg = rs * (dvn - jnp.mean(dvn, axis=-1, keepdims=True)
                        - vn * jnp.mean(dvn * vn, axis=-1, keepdims=True))
            dp_ref[:, SU_COLS[0] + g * LANES:SU_COLS[0] + (g + 1) * LANES] = (du * dgelu_su).astype(BF16)
            dp_ref[:, SV_COLS[0] + g * LANES:SV_COLS[0] + (g + 1) * LANES] = (dvg * dgelu_sv).astype(BF16)

    def row(width):
        return pl.BlockSpec((CHUNK, width), lambda i: (i, 0))

    def latent_row(width, col_block):
        return pl.BlockSpec((CHUNK, width), lambda i: (jnp.maximum(i - off, 0), col_block))

    def whole(shape):
        return pl.BlockSpec(shape, lambda i: (0,) * len(shape))

    return pl.pallas_call(
        body, name=name, grid=(TT // CHUNK,),
        out_shape=[jax.ShapeDtypeStruct((TT, IN_WIDTH), BF16), jax.ShapeDtypeStruct((1, LANES), F32),
                   jax.ShapeDtypeStruct((1, LANES), F32),
                   jax.ShapeDtypeStruct((N_SG_GROUPS, CHUNK, CHUNK), F32),
                   jax.ShapeDtypeStruct((CHUNK, N_SG_GROUPS), F32)],
        in_specs=[row(IN_WIDTH), latent_row(ATTN_WIDTH, 0),
                  pl.BlockSpec((2, 4, CHUNK, LANES), lambda i: (0, 0, i, 0)),
                  latent_row(SG_WIDTH, 1), row(LANES), row(LANES), whole((1, LANES)), whole((1, LANES)),
                  whole((LANES, LANES)), whole((N_SG_GROUPS, CHUNK, CHUNK)),
                  whole((N_SG_GROUPS, CHUNK, CHUNK)), whole((CHUNK, N_SG_GROUPS))],
        out_specs=[row(IN_WIDTH), whole((1, LANES)), whole((1, LANES)),
                   whole((N_SG_GROUPS, CHUNK, CHUNK)), whole((CHUNK, N_SG_GROUPS))],
        compiler_params=pltpu.CompilerParams(dimension_semantics=("arbitrary",)),
    )(p, dq, f, dao, cos, sin, qg, kg, bd, w_sp, w_spt, b_spt)


def _attn_fwd(q, kpad, vpad, ao, ctx_rows, name, tq=256):
    TT = q.shape[0]
    T = TT - ctx_rows
    tq = _tile(T, tq)
    off = ctx_rows // tq
    group = 2 * LANES

    def body(q_ref, k_ref, v_ref, ao_in, o_ref, lse_ref):
        del ao_in
        lane = lax.broadcasted_iota(jnp.int32, (tq, LANES), 1)
        lse = jnp.zeros((tq, LANES), F32)
        for a in range(2):
            acc = jnp.zeros((tq, LANES), F32)
            qa = q_ref[:, a * LANES:(a + 1) * LANES]
            for b in range(2):
                s = _dot(qa, k_ref[0, b], NT)
                m = jnp.max(s, axis=-1, keepdims=True)
                e = jnp.exp(s - m)
                l = jnp.sum(e, axis=-1, keepdims=True)
                acc = acc + _dot(e.astype(BF16), v_ref[0, b], NN) * (1.0 / l)
                lse = jnp.where(lane == 2 * a + b, m + jnp.log(l), lse)
            o_ref[:, a * LANES:(a + 1) * LANES] = acc.astype(BF16)
        lse_ref[0] = lse

    kv_spec = pl.BlockSpec((1, 2, TT, LANES), lambda j, i: (j, 0, 0, 0))
    return pl.pallas_call(
        body, name=name, grid=(2, T // tq),
        out_shape=[jax.ShapeDtypeStruct(ao.shape, BF16), jax.ShapeDtypeStruct((2, T, LANES), F32)],
        in_specs=[pl.BlockSpec((tq, group), lambda j, i: (i + off, j)), kv_spec, kv_spec,
                  pl.BlockSpec(memory_space=pl.ANY)],
        out_specs=[pl.BlockSpec((tq, group), lambda j, i: (i, j)),
                   pl.BlockSpec((1, tq, LANES), lambda j, i: (j, i, 0))],
        input_output_aliases={3: 0},
        compiler_params=pltpu.CompilerParams(dimension_semantics=("parallel", "parallel")),
    )(q, kpad, vpad, ao)


def _attn_bwd(q, dao, ao, lse, kpad, vpad, kt, ctx_rows, name, tq=256):
    TT = q.shape[0]
    T = TT - ctx_rows
    tq = _tile(T, tq)
    off = ctx_rows // tq
    group = 2 * LANES

    def body(q_ref, do_ref, o_ref, lse_ref, k_ref, v_ref, kt_ref, dq_ref, f_ref):
        i = pl.program_id(1)

        @pl.when(i == 0)
        def _():
            f_ref[...] = jnp.zeros_like(f_ref)

        ktv = kt_ref[0]
        lse_t = lse_ref[0].T
        row = lax.broadcasted_iota(jnp.int32, (SUBLANES, LANES), 0)
        lane = lax.broadcasted_iota(jnp.int32, (SUBLANES, LANES), 1)
        half_ones = (jnp.where(lane < HEAD_DIM, 0, 1) == row).astype(BF16)
        for a in range(2):
            cols = slice(a * LANES, (a + 1) * LANES)
            qa = q_ref[:, cols]
            do32 = do_ref[:, cols].astype(F32)
            doa = do32.astype(BF16)
            hi, lo = _split_bf16(do32 * o_ref[:, cols].astype(F32))
            deltas = _dot(half_ones, hi, NT) + _dot(half_ones, lo, NT)
            halves = []
            for b in range(2):
                h = 2 * a + b
                st = _dot(k_ref[0, b], qa, NT)
                pt = jnp.exp(st - lse_t[h:h + 1, :])
                dpt = _dot(v_ref[0, b], doa, NT)
                dst = (pt * (dpt - deltas[b:b + 1, :])).astype(BF16)
                f_ref[0, b] += _dot(dst, qa, NN)
                f_ref[0, 2 + b] += _dot(pt.astype(BF16), doa, NN)
                halves.append(_dot(ktv, dst, NN))
            dq_ref[:, cols] = jnp.concatenate(halves, axis=0).T

    kv_spec = pl.BlockSpec((1, 2, TT, LANES), lambda j, i: (j, 0, 0, 0))
    out_cols = pl.BlockSpec((tq, group), lambda j, i: (i, j))
    return pl.pallas_call(
        body, name=name, grid=(2, T // tq),
        out_shape=[jax.ShapeDtypeStruct((T, ATTN_WIDTH), F32), jax.ShapeDtypeStruct((2, 4, TT, LANES), F32)],
        in_specs=[pl.BlockSpec((tq, group), lambda j, i: (i + off, j)), out_cols, out_cols,
                  pl.BlockSpec((1, tq, LANES), lambda j, i: (j, i, 0)),
                  kv_spec, kv_spec, pl.BlockSpec((1, HEAD_DIM, TT), lambda j, i: (j, 0, 0))],
        out_specs=[out_cols, pl.BlockSpec((1, 4, TT, LANES), lambda j, i: (j, 0, 0, 0))],
        compiler_params=pltpu.CompilerParams(dimension_semantics=("parallel", "arbitrary")),
    )(q, dao, ao, lse, kpad, vpad, kt)


def _final_fwd_bwd(h, g, target, y, gt, name):
    R, Dm = h.shape
    tm = _tile(R, ROW_BLOCK, 8)

    def body(h_ref, g_ref, t_ref, y_ref, gt_ref, dh_ref, loss_ref, dg_ref, dy_ref, dgt_ref, dsum_ref):
        i = pl.program_id(0)

        @pl.when(i == 0)
        def _():
            for ref in (loss_ref, dg_ref, dgt_ref, dsum_ref):
                ref[...] = jnp.zeros_like(ref)

        hv = h_ref[...]
        r = lax.rsqrt(jnp.mean(hv * hv, axis=-1, keepdims=True) + EPS)
        n = hv * r
        diff = n * g_ref[...] - t_ref[...]
        loss_ref[...] += jnp.sum(diff * diff)
        dout = diff * (1.0 / Dm)
        dg_ref[...] += jnp.sum(dout * n, axis=0, keepdims=True)
        dn = dout * g_ref[...]
        dh = r * (dn - n * jnp.mean(dn * n, axis=-1, keepdims=True))
        dh_ref[...] = dh
        _gate_grads(dh, y_ref, gt_ref, dy_ref, dgt_ref, dsum_ref)

    vec = jax.ShapeDtypeStruct((1, Dm), F32)
    return pl.pallas_call(
        body, name=name, grid=(R // tm,),
        out_shape=[jax.ShapeDtypeStruct((R, Dm), F32), jax.ShapeDtypeStruct((1, LANES), F32), vec,
                   jax.ShapeDtypeStruct((R, Dm), BF16), vec, vec],
        in_specs=[_row_spec(tm, Dm), _vec_spec(Dm), _row_spec(tm, Dm), _row_spec(tm, Dm), _vec_spec(Dm)],
        out_specs=[_row_spec(tm, Dm), _vec_spec(LANES), _vec_spec(Dm), _row_spec(tm, Dm), _vec_spec(Dm),
                   _vec_spec(Dm)],
        compiler_params=pltpu.CompilerParams(dimension_semantics=("arbitrary",)),
    )(h, g, target, y, gt)


MOD_ROWS = 16


def _mod_fwd(c_rows, w_mod, name):
    L, Dm, n = w_mod.shape

    def body(c_ref, w_ref, o_ref):
        o_ref[0] = _dot3(_silu(c_ref[...]), w_ref[0], NN)

    return pl.pallas_call(
        body, name=name, grid=(L,),
        out_shape=jax.ShapeDtypeStruct((L, MOD_ROWS, n), F32),
        in_specs=[pl.BlockSpec((MOD_ROWS, Dm), lambda l: (0, 0)), pl.BlockSpec((1, Dm, n), lambda l: (l, 0, 0))],
        out_specs=pl.BlockSpec((1, MOD_ROWS, n), lambda l: (l, 0, 0)),
        compiler_params=pltpu.CompilerParams(dimension_semantics=("parallel",)),
    )(c_rows, w_mod)


def _mod_bwd(c_rows_t, dmod, w_mod, name):
    L, Dm, n = w_mod.shape

    def body(ct_ref, d_ref, w_ref, gw_ref, ds_ref):
        dm = d_ref[0]
        gw_ref[0] = _dot3(_silu(ct_ref[...]), dm, NN)
        ds_ref[0] = _dot3(dm[:MOD_ROWS], w_ref[0], NT)

    return pl.pallas_call(
        body, name=name, grid=(L,),
        out_shape=[jax.ShapeDtypeStruct((L, Dm, n), F32), jax.ShapeDtypeStruct((L, MOD_ROWS, Dm), F32)],
        in_specs=[pl.BlockSpec((Dm, LANES), lambda l: (0, 0)), pl.BlockSpec((1, LANES, n), lambda l: (l, 0, 0)),
                  pl.BlockSpec((1, Dm, n), lambda l: (l, 0, 0))],
        out_specs=[pl.BlockSpec((1, Dm, n), lambda l: (l, 0, 0)),
                   pl.BlockSpec((1, MOD_ROWS, Dm), lambda l: (l, 0, 0))],
        compiler_params=pltpu.CompilerParams(dimension_semantics=("parallel",)),
    )(c_rows_t, dmod, w_mod)


def _adam_update(w, g, m, v):
    c1 = 1.0 - ADAM_B1 ** ADAM_STEP
    c2 = 1.0 - ADAM_B2 ** ADAM_STEP
    mn = ADAM_B1 * m + (1.0 - ADAM_B1) * g
    vn = ADAM_B2 * v + (1.0 - ADAM_B2) * (g * g)
    return -ADAM_LR * ((mn / c1) / (jnp.sqrt(vn / c2) + ADAM_EPS) + ADAM_WD * w), mn, vn


def _adamw(w, g, m, v, name):
    R, Cw = w.shape
    tm = _tile(R, ADAM_ROWS, 8)

    def body(w_ref, g_ref, m_ref, v_ref, d_ref, mo_ref, vo_ref):
        d_ref[...], mo_ref[...], vo_ref[...] = _adam_update(w_ref[...], g_ref[...], m_ref[...], v_ref[...])

    spec = pl.BlockSpec((tm, Cw), lambda i: (i, 0))
    return pl.pallas_call(
        body, name=name, grid=(R // tm,),
        out_shape=[jax.ShapeDtypeStruct((R, Cw), F32)] * 3,
        in_specs=[spec] * 4, out_specs=[spec] * 3,
        compiler_params=pltpu.CompilerParams(dimension_semantics=("parallel",)),
    )(w, g, m, v)


def _adamw_recv(w, m, v, recvs, name):
    L, R, n = w.shape
    tm = _tile(R, ADAM_ROWS, 8)
    nblk = R // tm
    parts = [r.reshape(N_DEV, R, n) for r in recvs]

    def body(*refs):
        w_ref, m_ref, v_ref = refs[:3]
        part_refs = refs[3:3 + L]
        g_ref, d_ref, mo_ref, vo_ref, gsum = refs[3 + L:]
        l = pl.program_id(0)
        for ll in range(L):
            @pl.when(l == ll)
            def _(ll=ll):
                acc = part_refs[ll][0].astype(F32)
                for s in range(1, N_DEV):
                    acc = acc + part_refs[ll][s].astype(F32)
                gsum[...] = acc
        g = gsum[...]
        g_ref[0] = g
        d_ref[0], mo_ref[0], vo_ref[0] = _adam_update(w_ref[0], g, m_ref[0], v_ref[0])

    def part_spec(ll):
        return pl.BlockSpec((N_DEV, tm, n), lambda l, i: (0, jnp.where(l == ll, i, jnp.where(l < ll, 0, nblk - 1)), 0))

    spec = pl.BlockSpec((1, tm, n), lambda l, i: (l, i, 0))
    return pl.pallas_call(
        body, name=name, grid=(L, nblk),
        out_shape=[jax.ShapeDtypeStruct((L, R, n), F32)] * 4,
        in_specs=[spec] * 3 + [part_spec(ll) for ll in range(L)], out_specs=[spec] * 4,
        scratch_shapes=[pltpu.VMEM((tm, n), F32)],
        compiler_params=pltpu.CompilerParams(dimension_semantics=("parallel", "parallel")),
    )(w, m, v, *parts)


def _pack(parts, row_mult=8):
    flat, offs, pos = [], [], 0
    for t in parts:
        t = t.reshape(-1).astype(F32)
        size = -(-t.shape[0] // LANES) * LANES
        flat.append(jnp.pad(t, (0, size - t.shape[0])))
        offs.append(pos)
        pos += size
    total = -(-pos // (LANES * row_mult)) * (LANES * row_mult)
    if total > pos:
        flat.append(jnp.zeros((total - pos,), F32))
    return jnp.concatenate(flat).reshape(-1, LANES), offs


def _take(buf, off, shape):
    size = math.prod(shape)
    return buf[..., off:off + size].reshape(buf.shape[:-1] + tuple(shape))


def _rope_tables(T, ctx_rows):
    pos = jnp.arange(T)
    row = (pos // GRID_W).astype(F32)
    col = (pos % GRID_W).astype(F32)
    half = HEAD_DIM // 4
    inv = ROPE_THETA ** (-jnp.arange(0, 2 * half, 2, dtype=F32) / (2 * half))
    ang_r, ang_c = row[:, None] * inv[None, :], col[:, None] * inv[None, :]
    cos = jnp.concatenate([jnp.cos(ang_r)] * 2 + [jnp.cos(ang_c)] * 2, axis=1)
    sin = jnp.concatenate([-jnp.sin(ang_r), jnp.sin(ang_r), -jnp.sin(ang_c), jnp.sin(ang_c)], axis=1)
    cos = jnp.concatenate([jnp.ones((ctx_rows, HEAD_DIM), F32), cos], axis=0)
    sin = jnp.concatenate([jnp.zeros((ctx_rows, HEAD_DIM), F32), sin], axis=0)
    return jnp.tile(cos, (1, 2)), jnp.tile(sin, (1, 2))


def kernel(x, c, ctx, c_ctx, w_mod, b_mod, g_mix, g_ffn, w_ffn_in, w_ffn_out, w_in, q_gain, k_gain, w_sp, b_sp, w_out, w_pw1, b_pw1, w_dw, b_dw, ln_g, ln_b, w_pw2, b_pw2, g_final, loss_target, m_c_ctx, m_w_mod, m_b_mod, m_g_mix, m_g_ffn, m_w_ffn_in, m_w_ffn_out, m_w_in, m_q_gain, m_k_gain, m_w_sp, m_b_sp, m_w_out, m_w_pw1, m_b_pw1, m_w_dw, m_b_dw, m_ln_g, m_ln_b, m_w_pw2, m_b_pw2, m_g_final, v_c_ctx, v_w_mod, v_b_mod, v_g_mix, v_g_ffn, v_w_ffn_in, v_w_ffn_out, v_w_in, v_q_gain, v_k_gain, v_w_sp, v_b_sp, v_w_out, v_w_pw1, v_b_pw1, v_w_dw, v_b_dw, v_ln_g, v_ln_b, v_w_pw2, v_b_pw2, v_g_final):
    weights = dict(c_ctx=c_ctx, w_mod=w_mod, b_mod=b_mod, g_mix=g_mix, g_ffn=g_ffn, w_ffn_in=w_ffn_in,
                   w_ffn_out=w_ffn_out, w_in=w_in, q_gain=q_gain, k_gain=k_gain, w_sp=w_sp, b_sp=b_sp,
                   w_out=w_out, w_pw1=w_pw1, b_pw1=b_pw1, w_dw=w_dw, b_dw=b_dw, ln_g=ln_g, ln_b=ln_b,
                   w_pw2=w_pw2, b_pw2=b_pw2, g_final=g_final)
    moments_m = dict(c_ctx=m_c_ctx, w_mod=m_w_mod, b_mod=m_b_mod, g_mix=m_g_mix, g_ffn=m_g_ffn,
                     w_ffn_in=m_w_ffn_in, w_ffn_out=m_w_ffn_out, w_in=m_w_in, q_gain=m_q_gain,
                     k_gain=m_k_gain, w_sp=m_w_sp, b_sp=m_b_sp, w_out=m_w_out, w_pw1=m_w_pw1,
                     b_pw1=m_b_pw1, w_dw=m_w_dw, b_dw=m_b_dw, ln_g=m_ln_g, ln_b=m_ln_b, w_pw2=m_w_pw2,
                     b_pw2=m_b_pw2, g_final=m_g_final)
    moments_v = dict(c_ctx=v_c_ctx, w_mod=v_w_mod, b_mod=v_b_mod, g_mix=v_g_mix, g_ffn=v_g_ffn,
                     w_ffn_in=v_w_ffn_in, w_ffn_out=v_w_ffn_out, w_in=v_w_in, q_gain=v_q_gain,
                     k_gain=v_k_gain, w_sp=v_w_sp, b_sp=v_b_sp, w_out=v_w_out, w_pw1=v_w_pw1,
                     b_pw1=v_b_pw1, w_dw=v_w_dw, b_dw=v_b_dw, ln_g=v_ln_g, ln_b=v_ln_b, w_pw2=v_w_pw2,
                     b_pw2=v_b_pw2, g_final=v_g_final)
    names = list(weights)

    T, C = x.shape[1], ctx.shape[1]
    Dm = D_MODEL
    me = 4 * lax.axis_index("x") + 2 * lax.axis_index("y") + lax.axis_index("c")
    h0 = x[0]
    ctx2 = ctx[0]
    target = loss_target[0]

    small_sharded = (("w_dw", w_dw[0]), ("b_pw1", b_pw1), ("b_dw", b_dw), ("ln_g", ln_g), ("ln_b", ln_b),
                     ("b_pw2", b_pw2))
    buf1, offs1 = _pack([c] + [t for _, t in small_sharded])
    w_in_t, m_w_in_t, v_w_in_t = (jnp.swapaxes(t, 1, 2) for t in (w_in, m_w_in, v_w_in))
    w_ffi_t, m_w_ffi_t, v_w_ffi_t = (jnp.swapaxes(t, 1, 2) for t in (w_ffn_in, m_w_ffn_in, v_w_ffn_in))
    got1, W_in_t = _all_gather([buf1, w_in_t[0].astype(BF16)], "gather_cond", False)
    got1 = got1.reshape(N_DEV, -1)
    c_all = _take(got1, offs1[0], (Dm,))
    full_small = {}
    for (nm, t), off in zip(small_sharded, offs1[1:]):
        seg = _take(got1, off, t.shape)
        full_small[nm] = jnp.moveaxis(seg, 0, -2).reshape(t.shape[:-1] + (N_DEV * t.shape[-1],))
    w_dw_f, b_pw1_f = full_small["w_dw"], full_small["b_pw1"]
    b_dw_f, ln_g_f, ln_b_f, b_pw2_f = (full_small[k] for k in ("b_dw", "ln_g", "ln_b", "b_pw2"))

    c_rows = jnp.concatenate([c_all, c_ctx[None, :], jnp.zeros((MOD_ROWS - N_DEV - 1, Dm), F32)], axis=0)
    mod_part = _mod_fwd(c_rows, w_mod, "mod_fwd")
    n_mod = w_mod.shape[2]
    got2 = _all_gather([mod_part.reshape(-1, LANES)], "gather_mod", True)[0]
    mod_all = got2.reshape(N_DEV, 2, MOD_ROWS, n_mod).transpose(1, 2, 0, 3).reshape(2, MOD_ROWS, N_DEV * n_mod)
    mod_all = mod_all + b_mod[:, None, :]
    my_mod = lax.dynamic_index_in_dim(mod_all, me, axis=1, keepdims=False)
    sh1, sc1, gt1, sh2, sc2, gt2 = ([my_mod[l:l + 1, k * Dm:(k + 1) * Dm] for l in range(2)] for k in range(6))
    csh1 = mod_all[0, N_DEV:N_DEV + 1, 0:Dm]
    csc1 = mod_all[0, N_DEV:N_DEV + 1, Dm:2 * Dm]

    behind = got2[0:1, 0:1] * 0.0
    gather_groups = [[w_out[0]], [w_ffi_t[0], w_ffn_out[0]], [w_pw1[0], w_pw2[0]], [w_ffi_t[1], w_ffn_out[1]]]
    gathers = [_push_begin([(t + behind).astype(BF16) for t in grp], True, f"gather_start{k}")
               for k, grp in enumerate(gather_groups)]
    started = sum(h[4][0:1, 0:1] for h in gathers)

    def gathered(k, after):
        return _push_end(gathers[k], after, f"gather_wait{k}")[1]

    def ffn_weights(k, after):
        wi, wo = gathered(k, after)
        return wi.reshape(N_DEV, FF_SHARD, Dm), wo.reshape(N_DEV // 2, FF_SHARD, Dm)

    def col_gathered(t, n):
        return t.reshape(N_DEV, Dm, n).transpose(1, 0, 2).reshape(Dm, N_DEV * n)

    W_ffi, W_ffo = [None, None], [None, None]

    g_mix_r = [g_mix[l:l + 1] for l in range(2)]
    g_ffn_r = [g_ffn[l:l + 1] for l in range(2)]
    g_fin = g_final[None, :]

    cos, sin = _rope_tables(T, C)
    qg = jnp.tile(q_gain, (1, 2))
    kg = jnp.tile(k_gain, (1, 2))
    lane_head = jnp.arange(LANES) // HEAD_DIM
    bd = (lane_head[:, None] == lane_head[None, :]).astype(BF16)
    w_sp0 = w_sp[0]
    w_spt0 = w_sp0.transpose(0, 2, 1)
    b_spt0 = b_sp[0].T

    XM = _norm_mod_fwd_cat(ctx2, h0, g_mix_r[0], csc1, csh1, sc1[0] + started, sh1[0], "norm_mix0")
    P = _mm(XM, W_in_t, "nt", "in_proj", tm=1088, tn=IN_WIDTH)
    qh, kpad, vpad, kt, ao = _mix_prep_fwd(P, C, cos, sin, qg, kg, bd, w_sp0, b_spt0, "mix_prep")
    ao, lse = _attn_fwd(qh, kpad, vpad, ao, C, "attn_fwd")
    W_out, = gathered(0, ao)
    h1, y0, xf0 = _mm(ao, W_out, "nn", "out_proj", res=h0, gate=gt1[0], raw_out=True,
                      norm=(g_ffn_r[0], sc2[0], sh2[0]))

    def ffn_fwd(h_in, xf, l, norm_next):
        W_ffi[l], W_ffo[l] = ffn_weights(1 + 2 * l, xf)
        gu, act = _ffn_in_swiglu(xf, W_ffi[l], f"ffn_in{l}")
        outs = _mm_sum_shards(act, W_ffo[l], "nn", f"ffn_out{l}", res=h_in, gate=gt2[l], raw_out=True,
                              norm=norm_next)
        return tuple(outs) + (None,) * (3 - len(outs)) + (gu, act)

    h2, f0, xm1, gu0, act0 = ffn_fwd(h1, xf0, 0, (g_mix_r[1], sc1[1], sh1[1]))

    W_pw1, W_pw2 = gathered(2, xm1)
    W_pw1 = col_gathered(W_pw1, 2 * Dm // N_DEV)
    ag = _mm(xm1, W_pw1, "nn", "pw1", BF16, bias=b_pw1_f)
    hg = _glu_fwd(ag, "glu")
    hd = _conv_fwd(hg, w_dw_f, b_dw_f, "conv")
    hs = _ln_silu_fwd(hd, ln_g_f, ln_b_f, "ln_silu")
    h3, y1, xf1 = _mm(hs, W_pw2, "nn", "pw2", bias=b_pw2_f, res=h2, gate=gt1[1], raw_out=True,
                      norm=(g_ffn_r[1], sc2[1], sh2[1]))
    h4, f1, _, gu1, act1 = ffn_fwd(h3, xf1, 1, None)

    dh4, sq_err, dg_final, df1, dgt2_1, _ = _final_fwd_bwd(h4, g_fin, target, f1, gt2[1], "loss_head")
    loss_local = (0.5 / Dm) * sq_err[0, 0:1]

    def col_shards(g, n):
        return g.reshape(Dm, N_DEV, n).transpose(1, 0, 2).reshape(N_DEV * Dm, n)

    def exchange_begin(k, parts):
        return _push_begin(parts, False, f"exchange_start{k}")

    def zero_of(handle):
        return handle[4][0:1, 0:1]

    def ffn_bwd(df, xf, gu, act, l):
        dw_out = _mm_tn_shard_rows(act, df, f"ffn_out_dw{l}", BF16)
        dgu = _ffn_out_dx_swiglu(df, W_ffo[l], gu, f"ffn_out_dx{l}").reshape(N_DEV, T, FF_SHARD)
        dw_in = _mm_tn_shard_rows(dgu, xf, f"ffn_in_dw{l}", BF16)
        dxf = _mm_sum_shards(dgu, W_ffi[l], "nn", f"ffn_in_dx{l}", BF16, tm=512)
        return dw_in, dw_out, dxf

    dW_ffi1, dW_ffo1, dxf1 = ffn_bwd(df1, xf1, gu1, act1, 1)
    ex0 = exchange_begin(0, [dW_ffi1.reshape(2 * D_FF, Dm), dW_ffo1.reshape(D_FF, Dm)])
    dh3, da, dsh, dy1, dgt1_1, db_pw2 = _norm_mod_bwd(h3, g_ffn_r[1], sc2[1], dxf1, dh4, "norm_ffn_bwd1",
                                                       gate=(y1, gt1[1] + zero_of(ex0)))
    dmod_ffn1 = (dsh, da * g_ffn_r[1], dgt2_1)
    dg_ffn1 = da * (1.0 + sc2[1])

    dW_pw2 = _mm(hs, dy1, "tn", "pw2_dw", BF16, tk=2048)
    dhs = _mm(dy1, W_pw2, "nt", "pw2_dx", BF16)
    dhd, dln_g, dln_b, db_dw = _ln_silu_bwd(dhs, hd, ln_g_f, ln_b_f, "ln_silu_bwd")
    dhg, dw_dw = _conv_bwd(dhd, hg, w_dw_f, "conv_bwd")
    dag, db_pw1 = _glu_bwd(ag, dhg, "glu_bwd")
    dW_pw1 = _mm(xm1, dag, "tn", "pw1_dw", BF16, tk=2048)
    dxm1 = _mm(dag, W_pw1, "nt", "pw1_dx", BF16, tk=2048)
    ex1 = exchange_begin(1, [col_shards(dW_pw1, 2 * Dm // N_DEV), dW_pw2])
    dh2, da, dsh, df0, dgt2_0, _ = _norm_mod_bwd(h2, g_mix_r[1], sc1[1], dxm1, dh3, "norm_mix1_bwd",
                                                 gate=(f0, gt2[0] + zero_of(ex1)))
    dmod_mix1 = (dsh, da * g_mix_r[1], dgt1_1)
    dg_mix1 = da * (1.0 + sc1[1])

    dW_ffi0, dW_ffo0, dxf0 = ffn_bwd(df0, xf0, gu0, act0, 0)
    ex2 = exchange_begin(2, [dW_ffi0.reshape(2 * D_FF, Dm), dW_ffo0.reshape(D_FF, Dm)])
    dh1, da, dsh, dy0, dgt1_0, _ = _norm_mod_bwd(h1, g_ffn_r[0], sc2[0], dxf0, dh2, "norm_ffn_bwd0",
                                                 gate=(y0, gt1[0] + zero_of(ex2)))
    dmod_ffn0 = (dsh, da * g_ffn_r[0], dgt2_0)
    dg_ffn0 = da * (1.0 + sc2[0])

    dW_out = _mm(ao, dy0, "tn", "out_proj_dw", BF16, tk=2048)
    ex_out = exchange_begin(4, [dW_out])
    dao = _mm(dy0, W_out + zero_of(ex_out).astype(BF16), "nt", "out_proj_dx", BF16)
    dq, f_acc = _attn_bwd(qh, dao, ao, lse, kpad, vpad, kt, C, "attn_bwd")
    dP, dqg, dkg, dw_sp0, db_spt0 = _mix_prep_bwd(P, dq, f_acc, dao, C, cos, sin, qg, kg, bd, w_sp0, w_spt0,
                                                  b_spt0, "mix_prep_bwd")
    dW_in_t = _mm(dP, XM, "tn", "in_proj_dw", BF16, tm=896, tk=2176)
    dXM = _mm(dP, W_in_t, "nn", "in_proj_dx", BF16, tm=1088, tk=IN_WIDTH)
    dh0, da, dsh = _norm_mod_bwd(h0, g_mix_r[0], sc1[0], dXM, dh1, "norm_mix0_bwd", dxm_row_off=C)
    _, dac, dcsh = _norm_mod_bwd(ctx2, g_mix_r[0], csc1, dXM, None, "norm_ctx_bwd")
    dmod_mix0 = (dsh, da * g_mix_r[0], dgt1_0)
    dg_mix0 = da * (1.0 + sc1[0]) + dac * (1.0 + csc1)
    dcmod = jnp.concatenate([dcsh, dac * g_mix_r[0]], axis=1)

    dmod_mine = jnp.stack([jnp.concatenate(dmod_mix0 + dmod_ffn0, axis=1)[0],
                           jnp.concatenate(dmod_mix1 + dmod_ffn1, axis=1)[0]])

    small_grads = [
        ("loss", loss_local), ("g_final", dg_final), ("g_mix", jnp.concatenate([dg_mix0, dg_mix1])),
        ("g_ffn", jnp.concatenate([dg_ffn0, dg_ffn1])),
        ("q_gain", dqg[:, :HEAD_DIM] + dqg[:, HEAD_DIM:]), ("k_gain", dkg[:, :HEAD_DIM] + dkg[:, HEAD_DIM:]),
        ("w_sp", dw_sp0[None]), ("b_sp", db_spt0.T[None]), ("b_pw1", db_pw1), ("w_dw", dw_dw[None]),
        ("b_dw", db_dw), ("ln_g", dln_g), ("ln_b", dln_b), ("b_pw2", db_pw2), ("dcmod", dcmod),
        ("dmod", dmod_mine),
    ]
    buf3, offs3 = _pack([t for _, t in small_grads])
    off3 = {nm: off for (nm, _), off in zip(small_grads, offs3)}
    shape3 = {nm: t.shape for nm, t in small_grads}
    small_push = _push_begin([buf3], True, "small_grads_start")
    ex3 = exchange_begin(3, [dW_in_t + zero_of(small_push).astype(BF16)])

    grads, delta, new_m, new_v = {}, {}, {}, {}

    def exchanged(k, handle, after):
        return _push_end(handle, after, f"exchange_wait{k}")[1]

    def adamw_big(nm, parts, transposed=False, wmv=None):
        w3, m3, v3 = wmv if wmv is not None else (weights[nm], moments_m[nm], moments_v[nm])
        outs4 = _adamw_recv(w3, m3, v3, parts, f"adamw_{nm}")
        if transposed:
            outs4 = [jnp.swapaxes(t, 1, 2) for t in outs4]
        grads[nm], delta[nm], new_m[nm], new_v[nm] = outs4

    pushed = ex3[4]
    r_ffi1, r_ffo1 = exchanged(0, ex0, pushed)
    r_pw1, r_pw2 = exchanged(1, ex1, pushed)
    r_ffi0, r_ffo0 = exchanged(2, ex2, pushed)
    r_out, = exchanged(4, ex_out, pushed)
    adamw_big("w_ffn_in", [r_ffi0, r_ffi1], True, (w_ffi_t, m_w_ffi_t, v_w_ffi_t))
    adamw_big("w_ffn_out", [r_ffo0, r_ffo1])
    adamw_big("w_pw1", [r_pw1])
    adamw_big("w_pw2", [r_pw2])
    adamw_big("w_out", [r_out])

    got3 = _push_end(small_push, delta["w_out"], "small_grads_wait")[1][0].reshape(N_DEV, buf3.shape[0], LANES)
    sum3 = _sum_devices(got3, "sum_small_grads").reshape(-1)

    def summed(nm):
        return _take(sum3, off3[nm], shape3[nm])

    loss = summed("loss")[0]
    dcmod_sum = summed("dcmod")
    dmod_rows = _take(got3.reshape(N_DEV, -1), off3["dmod"], (2, 6 * Dm)).transpose(1, 0, 2)
    ctx_row = jnp.concatenate([jnp.pad(dcmod_sum, ((0, 0), (0, 4 * Dm))), jnp.zeros((1, 6 * Dm), F32)])
    dmod_all = jnp.concatenate([dmod_rows, ctx_row[:, None, :],
                                jnp.zeros((2, LANES - N_DEV - 1, 6 * Dm), F32)], axis=1)
    grads["b_mod"] = summed("dmod") + ctx_row
    dmod_shard = lax.dynamic_slice_in_dim(dmod_all, me * n_mod, n_mod, axis=2)
    c_rows_t = jnp.pad(c_rows.T, ((0, 0), (0, LANES - MOD_ROWS)))
    grads["w_mod"], ds_part = _mod_bwd(c_rows_t, dmod_shard, w_mod, "mod_bwd")

    buf4, _ = _pack([ds_part[0, N_DEV]])
    got4 = _all_gather([buf4], "gather_c_ctx_grad", True)[0].reshape(N_DEV, buf4.shape[0], LANES)
    ds_ctx = _sum_devices(got4, "sum_c_ctx_grad").reshape(-1)[:Dm]
    grads["c_ctx"] = ds_ctx * _dsilu(c_ctx)

    for nm in ("g_final", "g_mix", "g_ffn", "q_gain", "k_gain", "w_sp", "b_sp"):
        grads[nm] = summed(nm).reshape(weights[nm].shape)
    for nm in ("b_pw1", "w_dw", "b_dw", "ln_g", "ln_b", "b_pw2"):
        n_loc = weights[nm].shape[-1]
        grads[nm] = lax.dynamic_slice_in_dim(summed(nm), me * n_loc, n_loc, axis=-1).reshape(weights[nm].shape)

    shp = w_mod.shape
    outs = _adamw(w_mod.reshape(-1, shp[-1]), grads["w_mod"].reshape(-1, shp[-1]),
                  m_w_mod.reshape(-1, shp[-1]), v_w_mod.reshape(-1, shp[-1]), "adamw_w_mod")
    delta["w_mod"], new_m["w_mod"], new_v["w_mod"] = (o.reshape(shp) for o in outs)
    big_names = ("w_mod", "w_ffn_in", "w_ffn_out", "w_in", "w_out", "w_pw1", "w_pw2")
    small_names = [nm for nm in names if nm not in big_names]
    packs = [_pack([src[nm] for nm in small_names]) for src in (weights, grads, moments_m, moments_v)]
    offs_s = packs[0][1]
    outs = _adamw(*[pk[0] for pk in packs], "adamw_small")
    for o, dst in zip(outs, (delta, new_m, new_v)):
        o = o.reshape(-1)
        for nm, off in zip(small_names, offs_s):
            dst[nm] = _take(o, off, weights[nm].shape)
    r_in, = exchanged(3, ex3, outs[0])
    adamw_big("w_in", [r_in], True, (w_in_t, m_w_in_t, v_w_in_t))

    return (loss, dh0[None], *[grads[n] for n in names], *[delta[n] for n in names],
            *[new_m[n] for n in names], *[new_v[n] for n in names])
```

```python
import math

import jax
import jax.numpy as jnp
from jax import lax
from jax.experimental import pallas as pl
from jax.experimental.pallas import tpu as pltpu

F32 = jnp.float32
BF16 = jnp.bfloat16
MESH = pl.DeviceIdType.MESH

N_DEV = 8
D_MODEL = 1024
EPS = 1e-6
HEAD_DIM = 64
ATTN_WIDTH = 512
KV_WIDTH = 128
SG_WIDTH = 512
N_SG_GROUPS = 4
CHUNK = 128
IN_WIDTH = 1792
D_FF = 2816
FF_SHARD = 2 * D_FF // N_DEV
CONV_WIDTH = 31
CONV_HALO = 16
GRID_W = 64
ROPE_THETA = 10000.0
LANES = 128
SUBLANES = 8
ROW_BLOCK = 512
ADAM_ROWS = 256
ADAM_LR, ADAM_B1, ADAM_B2, ADAM_EPS, ADAM_WD, ADAM_STEP = 0.001, 0.9, 0.999, 1e-08, 0.01, 10


def _tile(n, target, mult=LANES):
    best = None
    for t in range(mult, min(n, target) + 1, mult):
        if n % t == 0:
            best = t
    return best if best is not None else n


def _sigmoid(x):
    return 1.0 / (1.0 + jnp.exp(-x))


def _silu(x):
    return x * _sigmoid(x)


def _dsilu(x):
    s = _sigmoid(x)
    return s * (1.0 + x * (1.0 - s))


_GELU_K = math.sqrt(2.0 / math.pi)


def _gelu(x):
    return 0.5 * x * (1.0 + jnp.tanh(_GELU_K * (x + 0.044715 * x * x * x)))


def _gelu_and_grad(x):
    x2 = x * x
    t = jnp.tanh(_GELU_K * x * (1.0 + 0.044715 * x2))
    half = 0.5 * (1.0 + t)
    return x * half, half + 0.5 * x * (1.0 - t * t) * _GELU_K * (1.0 + 3.0 * 0.044715 * x2)


def _split_bf16(x):
    hi = x.astype(BF16)
    lo = (x - hi.astype(F32)).astype(BF16)
    return hi, lo


def _dot(a, b, dims):
    return lax.dot_general(a, b, (dims, ((), ())), preferred_element_type=F32)


def _dot3(a, b, dims):
    ah, al = _split_bf16(a)
    bh, bl = _split_bf16(b)
    return _dot(ah, bh, dims) + _dot(ah, bl, dims) + _dot(al, bh, dims)


NN = ((1,), (0,))
NT = ((1,), (1,))
TN = ((0,), (0,))


def _all_gather(xs, name, in_vmem):
    n_arr = len(xs)

    def body(*refs):
        x_refs, out_refs = refs[:n_arr], refs[n_arr:2 * n_arr]
        send_sems, recv_sems, local_sems = refs[2 * n_arr:]
        x, y, c = lax.axis_index("x"), lax.axis_index("y"), lax.axis_index("c")
        me, sibling = (x, y, c), (x, y, 1 - c)
        chips = [(1 - x, y), (x, 1 - y), (1 - x, 1 - y)]

        def rows(a, px, py, pc):
            m_per = xs[a].shape[0]
            return out_refs[a].at[pl.ds((4 * px + 2 * py + pc) * m_per, m_per), :]

        def copy(a, k, block, to, src=None):
            return pltpu.make_async_remote_copy(
                src_ref=rows(a, *block) if src is None else src,
                dst_ref=rows(a, *block),
                send_sem=send_sems.at[7 * a + k],
                recv_sem=recv_sems.at[7 * a + k],
                device_id=to,
                device_id_type=MESH,
            )

        mine, first, passed = [], [], []
        for a in range(n_arr):
            mine.append(pltpu.make_async_copy(x_refs[a], rows(a, *me), local_sems.at[a]))
            mine[-1].start()
            first.append(copy(a, 0, me, sibling, src=x_refs[a]))
            first += [copy(a, 1 + j, me, (*chip, c), src=x_refs[a]) for j, chip in enumerate(chips)]
        for cp in first:
            cp.start()
        for a in range(n_arr):
            for j, chip in enumerate(chips):
                copy(a, 1 + j, (*chip, c), me).wait_recv()
                passed.append(copy(a, 4 + j, (*chip, c), sibling))
                passed[-1].start()
        for a in range(n_arr):
            copy(a, 0, sibling, me).wait_recv()
            for j, chip in enumerate(chips):
                copy(a, 4 + j, (*chip, 1 - c), me).wait_recv()
        for cp in first + passed:
            cp.wait_send()
        for cp in mine:
            cp.wait()

    space = pltpu.VMEM if in_vmem else pl.ANY
    return pl.pallas_call(
        body,
        name=name,
        out_shape=[jax.ShapeDtypeStruct((N_DEV * t.shape[0], t.shape[1]), t.dtype) for t in xs],
        in_specs=[pl.BlockSpec(memory_space=space)] * n_arr,
        out_specs=[pl.BlockSpec(memory_space=space)] * n_arr,
        scratch_shapes=[
            pltpu.SemaphoreType.DMA((7 * n_arr,)),
            pltpu.SemaphoreType.DMA((7 * n_arr,)),
            pltpu.SemaphoreType.DMA((n_arr,)),
        ],
    )(*xs)


HBM_SPEC = pl.BlockSpec(memory_space=pltpu.HBM)
SEM_SPEC = pl.BlockSpec(memory_space=pltpu.SEMAPHORE)
DATAFLOW_EFFECT = pltpu.SideEffectType.DATAFLOW_SIDE_EFFECTING


def _peers(x, y, c):
    for k in range(1, N_DEV):
        px = 1 - x if (k >> 2) & 1 else x
        py = 1 - y if (k >> 1) & 1 else y
        pc = 1 - c if k & 1 else c
        yield k - 1, (px, py, pc), 4 * px + 2 * py + pc


def _push_copies(src_refs, land_refs, send_sems, recv_sems, shapes, whole_src):
    x, y, c = lax.axis_index("x"), lax.axis_index("y"), lax.axis_index("c")
    me = 4 * x + 2 * y + c
    for a, (m_per, _) in enumerate(shapes):
        def block(ref, idx, m_per=m_per):
            return ref.at[pl.ds(idx * m_per, m_per), :]

        for k, peer, pidx in _peers(x, y, c):
            src = src_refs[a] if whole_src else block(src_refs[a], pidx)
            sems = dict(send_sem=send_sems.at[N_DEV * a + k], recv_sem=recv_sems.at[N_DEV * a + k],
                        device_id=peer, device_id_type=MESH)
            yield (pltpu.make_async_remote_copy(src_ref=src, dst_ref=block(land_refs[a], me), **sems),
                   pltpu.make_async_remote_copy(src_ref=src, dst_ref=block(land_refs[a], pidx), **sems))


def _own_copies(src_refs, land_refs, recv_sems, shapes, whole_src):
    me = 4 * lax.axis_index("x") + 2 * lax.axis_index("y") + lax.axis_index("c")
    for a, (m_per, _) in enumerate(shapes):
        mine = pl.ds(me * m_per, m_per)
        src = src_refs[a] if whole_src else src_refs[a].at[mine, :]
        yield pltpu.make_async_copy(src, land_refs[a].at[mine, :], recv_sems.at[N_DEV * a + N_DEV - 1])


def _push_begin(srcs, whole_src, name):
    n_arr = len(srcs)
    shapes = [(t.shape[0] if whole_src else t.shape[0] // N_DEV, t.shape[1]) for t in srcs]
    lands = [lax.empty((N_DEV * m, n), t.dtype) for (m, n), t in zip(shapes, srcs)]

    def body(*refs):
        src_refs, land_refs = refs[:n_arr], refs[n_arr:2 * n_arr]
        send_sems, recv_sems = refs[2 * n_arr], refs[2 * n_arr + 1]
        token = refs[-1]
        for outgoing, _ in _push_copies(src_refs, land_refs, send_sems, recv_sems, shapes, whole_src):
            outgoing.start()
        for own in _own_copies(src_refs, land_refs, recv_sems, shapes, whole_src):
            own.start()
        token[...] = jnp.zeros_like(token)

    operands = [pltpu.with_memory_space_constraint(t, pltpu.HBM) for t in list(srcs) + lands]
    outs = pl.pallas_call(
        body, name=name,
        out_shape=(pltpu.SemaphoreType.DMA((N_DEV * n_arr,)), pltpu.SemaphoreType.DMA((N_DEV * n_arr,)),
                   *[pltpu.HBM(t.shape, t.dtype) for t in operands],
                   jax.ShapeDtypeStruct((SUBLANES, LANES), F32)),
        in_specs=[HBM_SPEC] * (2 * n_arr),
        out_specs=(SEM_SPEC, SEM_SPEC, *[HBM_SPEC] * (2 * n_arr), pl.BlockSpec(memory_space=pltpu.VMEM)),
        input_output_aliases={i: 2 + i for i in range(2 * n_arr)},
        compiler_params=pltpu.CompilerParams(has_side_effects=DATAFLOW_EFFECT),
    )(*operands)
    return outs[0], outs[1], list(outs[2:2 + n_arr]), list(outs[2 + n_arr:2 + 2 * n_arr]), outs[-1], whole_src


def _push_end(handle, after, name):
    send_sems, recv_sems, srcs, lands, _, whole_src = handle
    n_arr = len(srcs)
    shapes = [(t.shape[0] // N_DEV, t.shape[1]) for t in lands]

    def body(*refs):
        src_refs, land_refs = refs[:n_arr], refs[n_arr:2 * n_arr]
        send_sems_ref, recv_sems_ref = refs[2 * n_arr], refs[2 * n_arr + 1]
        for outgoing, incoming in _push_copies(src_refs, land_refs, send_sems_ref, recv_sems_ref, shapes, whole_src):
            outgoing.wait_send()
            incoming.wait_recv()
        for own in _own_copies(src_refs, land_refs, recv_sems_ref, shapes, whole_src):
            own.wait()

    outs = pl.pallas_call(
        body, name=name,
        out_shape=tuple(pltpu.HBM(t.shape, t.dtype) for t in srcs + lands),
        in_specs=[HBM_SPEC] * (2 * n_arr) + [SEM_SPEC, SEM_SPEC, pl.BlockSpec(memory_space=pl.ANY)],
        out_specs=tuple([HBM_SPEC] * (2 * n_arr)),
        input_output_aliases={i: i for i in range(2 * n_arr)},
        compiler_params=pltpu.CompilerParams(has_side_effects=DATAFLOW_EFFECT),
    )(*srcs, *lands, send_sems, recv_sems, after)
    return list(outs[:n_arr]), list(outs[n_arr:])


def _sum_devices(r, name, rows_per_step=ADAM_ROWS):
    _, m, n = r.shape
    tm = _tile(m, rows_per_step, 8)

    def body(r_ref, o_ref):
        acc = r_ref[0].astype(F32)
        for s in range(1, N_DEV):
            acc = acc + r_ref[s].astype(F32)
        o_ref[...] = acc

    return pl.pallas_call(
        body,
        name=name,
        grid=(m // tm,),
        out_shape=jax.ShapeDtypeStruct((m, n), F32),
        in_specs=[pl.BlockSpec((N_DEV, tm, n), lambda i: (0, i, 0))],
        out_specs=pl.BlockSpec((tm, n), lambda i: (i, 0)),
        compiler_params=pltpu.CompilerParams(dimension_semantics=("parallel",)),
    )(r)


def _get(ref):
    return ref[0] if len(ref.shape) == 3 else ref[...]


def _put(ref, val):
    if len(ref.shape) == 3:
        ref[0] = val
    else:
        ref[...] = val


def _norm_mod(hv, g, sc, sh):
    r = lax.rsqrt(jnp.mean(hv * hv, axis=-1, keepdims=True) + EPS)
    return (hv * r) * g * (1.0 + sc) + sh


def _mm_call(name, a, b, a_spec, b_spec, out_sds, o_spec, grid, dims, acc_shape, bias=None,
             res=None, gate=None, raw_out=False, vec_spec=None, norm=None):
    nk = grid[2]
    operands, in_specs = [a, b], [a_spec, b_spec]
    if bias is not None:
        operands.append(bias)
        in_specs.append(vec_spec)
    if res is not None:
        operands += [res, gate]
        in_specs += [o_spec, vec_spec]
    if norm is not None:
        assert grid[1] == 1
        operands += list(norm)
        in_specs += [vec_spec] * 3
    out_shape, out_specs = [out_sds], [o_spec]
    if raw_out:
        out_shape.append(jax.ShapeDtypeStruct(out_sds.shape, BF16))
        out_specs.append(o_spec)
    if norm is not None:
        out_shape.append(jax.ShapeDtypeStruct(out_sds.shape, BF16))
        out_specs.append(o_spec)

    def body(*refs):
        it = iter(refs)
        a_ref, b_ref = next(it), next(it)
        bias_ref = next(it) if bias is not None else None
        res_ref, gate_ref = (next(it), next(it)) if res is not None else (None, None)
        norm_refs = (next(it), next(it), next(it)) if norm is not None else None
        o_ref = next(it)
        raw_ref = next(it) if raw_out else None
        xn_ref = next(it) if norm is not None else None
        acc = next(it) if nk > 1 else None
        k = pl.program_id(2)
        part = _dot(_get(a_ref).astype(BF16), _get(b_ref).astype(BF16), dims)

        def finish(y):
            if bias_ref is not None:
                y = y + bias_ref[...]
            if raw_ref is not None:
                raw_ref[...] = y.astype(BF16)
            if res_ref is not None:
                y = res_ref[...] + gate_ref[...] * y
            _put(o_ref, y.astype(out_sds.dtype))
            if xn_ref is not None:
                xn_ref[...] = _norm_mod(y, *[r[...] for r in norm_refs]).astype(BF16)

        if nk == 1:
            finish(part)
        else:
            @pl.when(k == 0)
            def _():
                acc[...] = part

            @pl.when(k > 0)
            def _():
                acc[...] += part

            @pl.when(k == nk - 1)
            def _():
                finish(acc[...])

    outs = pl.pallas_call(
        body,
        name=name,
        grid=grid,
        out_shape=out_shape,
        in_specs=in_specs,
        out_specs=out_specs,
        scratch_shapes=[pltpu.VMEM(acc_shape, F32)] if nk > 1 else [],
        compiler_params=pltpu.CompilerParams(dimension_semantics=("parallel", "parallel", "arbitrary")),
    )(*operands)
    return outs if len(outs) > 1 else outs[0]


def _mm(a, b, mode, name, out_dtype=F32, bias=None, res=None, gate=None, raw_out=False,
        tm=1024, tn=1024, tk=1024, a_row_off=0, norm=None):
    if mode == "nn":
        K, N = b.shape
        M = a.shape[0] - a_row_off
    elif mode == "nt":
        N, K = b.shape
        M = a.shape[0] - a_row_off
    else:
        (K, M), N = a.shape, b.shape[1]
    tm, tn, tk = _tile(M, tm, LANES if mode == "tn" else 2 * SUBLANES), _tile(N, tn), _tile(K, tk)
    off = a_row_off // tm
    dims = {"nn": NN, "nt": NT, "tn": TN}[mode]
    a_spec = (pl.BlockSpec((tk, tm), lambda i, j, k: (k, i)) if mode == "tn"
              else pl.BlockSpec((tm, tk), lambda i, j, k: (i + off, k)))
    b_spec = (pl.BlockSpec((tn, tk), lambda i, j, k: (j, k)) if mode == "nt"
              else pl.BlockSpec((tk, tn), lambda i, j, k: (k, j)))
    return _mm_call(name, a, b, a_spec, b_spec, jax.ShapeDtypeStruct((M, N), out_dtype),
                    pl.BlockSpec((tm, tn), lambda i, j, k: (i, j)), (M // tm, N // tn, K // tk), dims,
                    (tm, tn), bias, res, gate, raw_out, pl.BlockSpec((1, tn), lambda i, j, k: (0, j)), norm)


def _mm_sum_shards(a3, b3, mode, name, out_dtype=F32, res=None, gate=None, raw_out=False, tm=512, norm=None):
    S, M, kk = a3.shape
    N = b3.shape[2] if mode == "nn" else b3.shape[1]
    tm = _tile(M, tm)
    dims = NN if mode == "nn" else NT
    has_res = res is not None

    def body(*refs):
        it = iter(refs)
        a_ref, b_ref = next(it), next(it)
        res_ref, gate_ref = (next(it), next(it)) if has_res else (None, None)
        norm_refs = (next(it), next(it), next(it)) if norm is not None else None
        o_ref = next(it)
        raw_ref = next(it) if raw_out else None
        xn_ref = next(it) if norm is not None else None
        y = _dot(a_ref[0], b_ref[0], dims)
        for s in range(1, S):
            y = y + _dot(a_ref[s], b_ref[s], dims)
        if raw_ref is not None:
            raw_ref[...] = y.astype(BF16)
        if has_res:
            y = res_ref[...] + gate_ref[...] * y
        o_ref[...] = y.astype(out_dtype)
        if xn_ref is not None:
            xn_ref[...] = _norm_mod(y, *[r[...] for r in norm_refs]).astype(BF16)

    tile = pl.BlockSpec((tm, N), lambda i: (i, 0))
    operands = [a3, b3] + ([res, gate] if has_res else []) + (list(norm) if norm is not None else [])
    in_specs = [pl.BlockSpec((S, tm, kk), lambda i: (0, i, 0)), pl.BlockSpec(b3.shape, lambda i: (0, 0, 0))]
    in_specs += [tile, _vec_spec(N)] if has_res else []
    in_specs += [_vec_spec(N)] * 3 if norm is not None else []
    out_shape = [jax.ShapeDtypeStruct((M, N), out_dtype)] + ([jax.ShapeDtypeStruct((M, N), BF16)] if raw_out else [])
    out_shape += [jax.ShapeDtypeStruct((M, N), BF16)] if norm is not None else []
    outs = pl.pallas_call(
        body, name=name, grid=(M // tm,),
        out_shape=out_shape, in_specs=in_specs, out_specs=[tile] * len(out_shape),
        compiler_params=pltpu.CompilerParams(dimension_semantics=("parallel",)),
    )(*operands)
    return outs if len(outs) > 1 else outs[0]


def _mm_tn_shard_rows(a3, b, name, out_dtype, tn=1024, tk=4096):
    S, T, m = a3.shape
    N = b.shape[1]
    tn, tk = _tile(N, tn), _tile(T, tk)
    return _mm_call(name, a3, b, pl.BlockSpec((1, tk, m), lambda i, j, k: (i, k, 0)),
                    pl.BlockSpec((tk, tn), lambda i, j, k: (k, j)), jax.ShapeDtypeStruct((S, m, N), out_dtype),
                    pl.BlockSpec((1, m, tn), lambda i, j, k: (i, 0, j)), (S, N // tn, T // tk), TN, (m, tn))


def _row_spec(tm, width, off=0):
    return pl.BlockSpec((tm, width), lambda i: (i + off, 0))


def _vec_spec(width):
    return pl.BlockSpec((1, width), lambda i: (0, 0))


def _norm_mod_fwd_cat(hc, h, g, csc, csh, sc, sh, name):
    (C, Dm), T = hc.shape, h.shape[0]
    tm = _tile(math.gcd(C, T), ROW_BLOCK, 8)
    off = C // tm

    def body(hc_ref, h_ref, g_ref, csc_ref, csh_ref, sc_ref, sh_ref, o_ref):
        is_ctx = pl.program_id(0) < off
        hv = jnp.where(is_ctx, hc_ref[...], h_ref[...])
        scv = jnp.where(is_ctx, csc_ref[...], sc_ref[...])
        shv = jnp.where(is_ctx, csh_ref[...], sh_ref[...])
        r = lax.rsqrt(jnp.mean(hv * hv, axis=-1, keepdims=True) + EPS)
        o_ref[...] = ((hv * r) * g_ref[...] * (1.0 + scv) + shv).astype(BF16)

    return pl.pallas_call(
        body, name=name, grid=((C + T) // tm,),
        out_shape=jax.ShapeDtypeStruct((C + T, Dm), BF16),
        in_specs=[pl.BlockSpec((tm, Dm), lambda i: (jnp.minimum(i, off - 1), 0)),
                  pl.BlockSpec((tm, Dm), lambda i: (jnp.maximum(i - off, 0), 0))] + [_vec_spec(Dm)] * 5,
        out_specs=_row_spec(tm, Dm),
        compiler_params=pltpu.CompilerParams(dimension_semantics=("parallel",)),
    )(hc, h, g, csc, csh, sc, sh)


def _gate_grads(dh, y_ref, gt_ref, dy_ref, dgt_ref, dsum_ref):
    dy = dh * gt_ref[...]
    dgt_ref[...] += jnp.sum(dh * y_ref[...].astype(F32), axis=0, keepdims=True)
    dsum_ref[...] += jnp.sum(dy, axis=0, keepdims=True)
    dy_ref[...] = dy.astype(BF16)


def _norm_mod_bwd(h, g, sc, dxm, dres, name, dxm_row_off=0, gate=None):
    R, Dm = h.shape
    tm = _tile(math.gcd(R, dxm_row_off) if dxm_row_off else R, ROW_BLOCK, 8)
    off = dxm_row_off // tm
    has_res = dres is not None
    has_gate = gate is not None

    def body(*refs):
        it = iter(refs)
        h_ref, g_ref, sc_ref, dx_ref = next(it), next(it), next(it), next(it)
        dres_ref = next(it) if has_res else None
        y_ref, gt_ref = (next(it), next(it)) if has_gate else (None, None)
        dh_ref, da_ref, dsh_ref = next(it), next(it), next(it)
        gate_out = (next(it), next(it), next(it)) if has_gate else ()
        i = pl.program_id(0)

        @pl.when(i == 0)
        def _():
            for ref in (da_ref, dsh_ref) + gate_out[1:]:
                ref[...] = jnp.zeros_like(ref)

        hv = h_ref[...]
        dx = dx_ref[...].astype(F32)
        r = lax.rsqrt(jnp.mean(hv * hv, axis=-1, keepdims=True) + EPS)
        n = hv * r
        da_ref[...] += jnp.sum(dx * n, axis=0, keepdims=True)
        dsh_ref[...] += jnp.sum(dx, axis=0, keepdims=True)
        dn = dx * (g_ref[...] * (1.0 + sc_ref[...]))
        dh = r * (dn - n * jnp.mean(dn * n, axis=-1, keepdims=True))
        if has_res:
            dh = dh + dres_ref[...]
        dh_ref[...] = dh
        if has_gate:
            _gate_grads(dh, y_ref, gt_ref, *gate_out)

    operands = [h, g, sc, dxm] + ([dres] if has_res else []) + (list(gate) if has_gate else [])
    in_specs = [_row_spec(tm, Dm), _vec_spec(Dm), _vec_spec(Dm), _row_spec(tm, Dm, off)]
    in_specs += [_row_spec(tm, Dm)] if has_res else []
    in_specs += [_row_spec(tm, Dm), _vec_spec(Dm)] if has_gate else []
    vec = jax.ShapeDtypeStruct((1, Dm), F32)
    out_shape = [jax.ShapeDtypeStruct((R, Dm), F32), vec, vec]
    out_specs = [_row_spec(tm, Dm), _vec_spec(Dm), _vec_spec(Dm)]
    if has_gate:
        out_shape += [jax.ShapeDtypeStruct((R, Dm), BF16), vec, vec]
        out_specs += [_row_spec(tm, Dm), _vec_spec(Dm), _vec_spec(Dm)]
    return pl.pallas_call(
        body, name=name, grid=(R // tm,),
        out_shape=out_shape, in_specs=in_specs, out_specs=out_specs,
        compiler_params=pltpu.CompilerParams(dimension_semantics=("arbitrary",)),
    )(*operands)


def _ffn_in_swiglu(xf, w3, name, tm=1024):
    T, K = xf.shape
    S, n, _ = w3.shape
    half = S // 2
    tm = _tile(T, tm)

    def body(a_ref, wg_ref, wu_ref, gu_ref, act_ref):
        a = a_ref[...]
        g = _dot(a, wg_ref[0], NT)
        u = _dot(a, wu_ref[0], NT)
        gu_ref[0, 0] = g.astype(BF16)
        gu_ref[1, 0] = u.astype(BF16)
        act_ref[0] = (_silu(g) * u).astype(BF16)

    return pl.pallas_call(
        body, name=name, grid=(T // tm, half),
        out_shape=[jax.ShapeDtypeStruct((2, half, T, n), BF16), jax.ShapeDtypeStruct((half, T, n), BF16)],
        in_specs=[pl.BlockSpec((tm, K), lambda i, j: (i, 0)),
                  pl.BlockSpec((1, n, K), lambda i, j: (j, 0, 0)),
                  pl.BlockSpec((1, n, K), lambda i, j: (j + half, 0, 0))],
        out_specs=[pl.BlockSpec((2, 1, tm, n), lambda i, j: (0, j, i, 0)),
                   pl.BlockSpec((1, tm, n), lambda i, j: (j, i, 0))],
        compiler_params=pltpu.CompilerParams(dimension_semantics=("parallel", "parallel")),
    )(xf, w3, w3)


def _ffn_out_dx_swiglu(df, wo, gu, name, tm=1024):
    T, Dm = df.shape
    half, n, _ = wo.shape
    tm = _tile(T, tm)

    def body(df_ref, w_ref, gu_ref, o_ref):
        da = _dot(df_ref[...], w_ref[0], NT)
        g = gu_ref[0, 0].astype(F32)
        u = gu_ref[1, 0].astype(F32)
        s = _sigmoid(g)
        o_ref[0, 0] = (da * u * (s * (1.0 + g * (1.0 - s)))).astype(BF16)
        o_ref[1, 0] = (da * (g * s)).astype(BF16)

    gu_spec = pl.BlockSpec((2, 1, tm, n), lambda i, j: (0, j, i, 0))
    return pl.pallas_call(
        body, name=name, grid=(T // tm, half),
        out_shape=jax.ShapeDtypeStruct(gu.shape, BF16),
        in_specs=[pl.BlockSpec((tm, Dm), lambda i, j: (i, 0)),
                  pl.BlockSpec((1, n, Dm), lambda i, j: (j, 0, 0)), gu_spec],
        out_specs=gu_spec,
        compiler_params=pltpu.CompilerParams(dimension_semantics=("parallel", "parallel")),
    )(df, wo, gu)


def _pw1_glu(xm, w, bias, name, tm=512):
    T, K = xm.shape
    N = w.shape[1]
    tm = _tile(T, tm, 2 * SUBLANES)

    def body(a_ref, w_ref, b_ref, ag_ref, hg_ref):
        ag = (_dot(a_ref[...], w_ref[...], NN) + b_ref[...]).astype(BF16)
        ag_ref[...] = ag
        hg_ref[...] = ag[:, :N // 2].astype(F32) * _sigmoid(ag[:, N // 2:].astype(F32))

    return pl.pallas_call(
        body, name=name, grid=(T // tm,),
        out_shape=[jax.ShapeDtypeStruct((T, N), BF16), jax.ShapeDtypeStruct((T, N // 2), F32)],
        in_specs=[_row_spec(tm, K), pl.BlockSpec((K, N), lambda i: (0, 0)), _vec_spec(N)],
        out_specs=[_row_spec(tm, N), _row_spec(tm, N // 2)],
        compiler_params=pltpu.CompilerParams(dimension_semantics=("parallel",)),
    )(xm, w, bias)


def _glu_bwd(ag, dhg, name):
    R = ag.shape[0]
    tm = _tile(R, ROW_BLOCK, 8)

    def body(ag_ref, dh_ref, o_ref, s_ref):
        i = pl.program_id(0)

        @pl.when(i == 0)
        def _():
            s_ref[...] = jnp.zeros_like(s_ref)

        a = ag_ref[:, :D_MODEL].astype(F32)
        s = _sigmoid(ag_ref[:, D_MODEL:].astype(F32))
        dh = dh_ref[...]
        da = dh * s
        dg = dh * a * s * (1.0 - s)
        o_ref[:, :D_MODEL] = da.astype(BF16)
        o_ref[:, D_MODEL:] = dg.astype(BF16)
        s_ref[:, :D_MODEL] += jnp.sum(da, axis=0, keepdims=True)
        s_ref[:, D_MODEL:] += jnp.sum(dg, axis=0, keepdims=True)

    return pl.pallas_call(
        body, name=name, grid=(R // tm,),
        out_shape=[jax.ShapeDtypeStruct((R, 2 * D_MODEL), BF16), jax.ShapeDtypeStruct((1, 2 * D_MODEL), F32)],
        in_specs=[_row_spec(tm, 2 * D_MODEL), _row_spec(tm, D_MODEL)],
        out_specs=[_row_spec(tm, 2 * D_MODEL), _vec_spec(2 * D_MODEL)],
        compiler_params=pltpu.CompilerParams(dimension_semantics=("arbitrary",)),
    )(ag, dhg)


def _halo_specs(tm, nblk, width):
    per = tm // CONV_HALO
    prev = pl.BlockSpec((CONV_HALO, width), lambda i: (jnp.maximum(i * per - 1, 0), 0))
    nxt = pl.BlockSpec((CONV_HALO, width), lambda i: (jnp.minimum((i + 1) * per, nblk * per - 1), 0))
    return prev, nxt


def _fill_halo(scr, prev_ref, cur_ref, next_ref, i, nblk, tm):
    scr[0:CONV_HALO, :] = jnp.where(i > 0, prev_ref[...], 0.0)
    scr[CONV_HALO:CONV_HALO + tm, :] = cur_ref[...]
    scr[CONV_HALO + tm:2 * CONV_HALO + tm, :] = jnp.where(i < nblk - 1, next_ref[...], 0.0)


CONV_ROWS = 128


CONV_REACH = (CONV_WIDTH // SUBLANES) * SUBLANES


def _windows(scr, stage, cols, tm):
    for r in range(SUBLANES):
        if r:
            stage[r] = scr[pl.ds(r, tm + CONV_REACH), cols]
        for a in range(CONV_REACH // SUBLANES + 1):
            off = SUBLANES * a + r
            if 1 <= off <= CONV_WIDTH:
                yield off, (stage[r, SUBLANES * a:SUBLANES * a + tm, :] if r
                            else scr[SUBLANES * a:SUBLANES * a + tm, cols])


def _conv_fwd(hg, w_dw, b_dw, name):
    R, Dm = hg.shape
    tm = _tile(R, CONV_ROWS, CONV_HALO)
    nblk = R // tm
    prev_spec, next_spec = _halo_specs(tm, nblk, Dm)

    def body(prev_ref, cur_ref, next_ref, w_ref, bdw_ref, hd_ref, scr, stage):
        _fill_halo(scr, prev_ref, cur_ref, next_ref, pl.program_id(0), nblk, tm)
        for cb in range(Dm // LANES):
            cols = slice(cb * LANES, (cb + 1) * LANES)
            acc = jnp.zeros((tm, LANES), F32) + bdw_ref[:, cols]
            for off, win in _windows(scr, stage, cols, tm):
                acc = acc + w_ref[off - 1:off, cols] * win
            hd_ref[:, cols] = acc

    return pl.pallas_call(
        body, name=name, grid=(nblk,),
        out_shape=jax.ShapeDtypeStruct((R, Dm), F32),
        in_specs=[prev_spec, _row_spec(tm, Dm), next_spec,
                  pl.BlockSpec((CONV_WIDTH, Dm), lambda i: (0, 0)), _vec_spec(Dm)],
        out_specs=_row_spec(tm, Dm),
        scratch_shapes=[pltpu.VMEM((tm + 2 * CONV_HALO, Dm), F32),
                        pltpu.VMEM((SUBLANES, tm + CONV_REACH, LANES), F32)],
        compiler_params=pltpu.CompilerParams(dimension_semantics=("parallel",)),
    )(hg, hg, hg, w_dw, b_dw)


def _ln_silu_fwd(hd, ln_g, ln_b, name):
    R, Dm = hd.shape
    tm = _tile(R, ROW_BLOCK, 8)

    def body(hd_ref, g_ref, b_ref, hs_ref):
        hd = hd_ref[...]
        xc = hd - jnp.mean(hd, axis=-1, keepdims=True)
        rs = lax.rsqrt(jnp.mean(xc * xc, axis=-1, keepdims=True) + EPS)
        hs_ref[...] = _silu(xc * rs * g_ref[...] + b_ref[...]).astype(BF16)

    return pl.pallas_call(
        body, name=name, grid=(R // tm,),
        out_shape=jax.ShapeDtypeStruct((R, Dm), BF16),
        in_specs=[_row_spec(tm, Dm), _vec_spec(Dm), _vec_spec(Dm)],
        out_specs=_row_spec(tm, Dm),
        compiler_params=pltpu.CompilerParams(dimension_semantics=("parallel",)),
    )(hd, ln_g, ln_b)


def _ln_silu_bwd(dhs, hd, ln_g, ln_b, name):
    R, Dm = hd.shape
    tm = _tile(R, ROW_BLOCK, 8)

    def body(dhs_ref, hd_ref, g_ref, b_ref, dhd_ref, dg_ref, db_ref, dsum_ref):
        i = pl.program_id(0)

        @pl.when(i == 0)
        def _():
            dg_ref[...] = jnp.zeros_like(dg_ref)
            db_ref[...] = jnp.zeros_like(db_ref)
            dsum_ref[...] = jnp.zeros_like(dsum_ref)

        hd = hd_ref[...]
        mu = jnp.mean(hd, axis=-1, keepdims=True)
        xc = hd - mu
        rs = lax.rsqrt(jnp.mean(xc * xc, axis=-1, keepdims=True) + EPS)
        z = xc * rs
        hl = z * g_ref[...] + b_ref[...]
        dhl = dhs_ref[...].astype(F32) * _dsilu(hl)
        dg_ref[...] += jnp.sum(dhl * z, axis=0, keepdims=True)
        db_ref[...] += jnp.sum(dhl, axis=0, keepdims=True)
        dz = dhl * g_ref[...]
        dhd = rs * (dz - jnp.mean(dz, axis=-1, keepdims=True) - z * jnp.mean(dz * z, axis=-1, keepdims=True))
        dsum_ref[...] += jnp.sum(dhd, axis=0, keepdims=True)
        dhd_ref[...] = dhd

    return pl.pallas_call(
        body, name=name, grid=(R // tm,),
        out_shape=[jax.ShapeDtypeStruct((R, Dm), F32)] + [jax.ShapeDtypeStruct((1, Dm), F32)] * 3,
        in_specs=[_row_spec(tm, Dm), _row_spec(tm, Dm), _vec_spec(Dm), _vec_spec(Dm)],
        out_specs=[_row_spec(tm, Dm), _vec_spec(Dm), _vec_spec(Dm), _vec_spec(Dm)],
        compiler_params=pltpu.CompilerParams(dimension_semantics=("arbitrary",)),
    )(dhs, hd, ln_g, ln_b)


def _conv_bwd(dhd, hg, w_dw, name):
    R, Dm = hg.shape
    tm = _tile(R, CONV_ROWS, CONV_HALO)
    nblk = R // tm
    prev_spec, next_spec = _halo_specs(tm, nblk, Dm)

    def body(dprev, dcur, dnext, gprev, gcur, gnext, w_ref, dhg_ref, dw_ref, dscr, gscr, dwp, stage):
        i = pl.program_id(0)

        @pl.when(i == 0)
        def _():
            dwp[...] = jnp.zeros_like(dwp)

        _fill_halo(dscr, dprev, dcur, dnext, i, nblk, tm)
        _fill_halo(gscr, gprev, gcur, gnext, i, nblk, tm)
        for cb in range(Dm // LANES):
            cols = slice(cb * LANES, (cb + 1) * LANES)
            acc = jnp.zeros((tm, LANES), F32)
            for off, win in _windows(dscr, stage, cols, tm):
                j = CONV_WIDTH - off
                acc = acc + w_ref[j:j + 1, cols] * win
            dhg_ref[:, cols] = acc
            d_here = dcur[:, cols]
            for off, win in _windows(gscr, stage, cols, tm):
                j = off - 1
                prod = d_here * win
                part = prod[0:SUBLANES]
                for k in range(1, tm // SUBLANES):
                    part = part + prod[k * SUBLANES:(k + 1) * SUBLANES]
                dwp[j * SUBLANES:(j + 1) * SUBLANES, cols] += part

        @pl.when(i == nblk - 1)
        def _():
            for j in range(CONV_WIDTH):
                dw_ref[j:j + 1, :] = jnp.sum(dwp[j * SUBLANES:(j + 1) * SUBLANES, :], axis=0, keepdims=True)

    return pl.pallas_call(
        body, name=name, grid=(nblk,),
        out_shape=[jax.ShapeDtypeStruct((R, Dm), F32), jax.ShapeDtypeStruct((CONV_WIDTH, Dm), F32)],
        in_specs=[prev_spec, _row_spec(tm, Dm), next_spec, prev_spec, _row_spec(tm, Dm), next_spec,
                  pl.BlockSpec((CONV_WIDTH, Dm), lambda i: (0, 0))],
        out_specs=[_row_spec(tm, Dm), pl.BlockSpec((CONV_WIDTH, Dm), lambda i: (0, 0))],
        scratch_shapes=[pltpu.VMEM((tm + 2 * CONV_HALO, Dm), F32)] * 2
        + [pltpu.VMEM((CONV_WIDTH * SUBLANES, Dm), F32), pltpu.VMEM((SUBLANES, tm + CONV_REACH, LANES), F32)],
        compiler_params=pltpu.CompilerParams(dimension_semantics=("arbitrary",)),
    )(dhd, dhd, dhd, hg, hg, hg, w_dw)


def _swap16(y, lane):
    return jnp.where((lane & 16) == 0, pltpu.roll(y, LANES - 16, 1), pltpu.roll(y, 16, 1))


def _head_mean(v, bd):
    hi, lo = _split_bf16(v)
    return (_dot(hi, bd, NN) + _dot(lo, bd, NN)) * (1.0 / HEAD_DIM)


Q_COLS = (0, ATTN_WIDTH)
K_COLS = (ATTN_WIDTH, ATTN_WIDTH + HEAD_DIM * 2)
V_COLS = (K_COLS[1], K_COLS[1] + HEAD_DIM * 2)
SU_COLS = (V_COLS[1], V_COLS[1] + SG_WIDTH)
SV_COLS = (SU_COLS[1], SU_COLS[1] + SG_WIDTH)


def _mix_prep_fwd(p, ctx_rows, cos, sin, qg, kg, bd, w_sp, b_spt, name):
    TT = p.shape[0]
    off = ctx_rows // CHUNK
    q_scale = HEAD_DIM ** -0.5

    def body(p_ref, cos_ref, sin_ref, qg_ref, kg_ref, bd_ref, w_ref, b_ref,
             q_ref, kp_ref, vp_ref, kt_ref, sg_ref):
        lane = lax.broadcasted_iota(jnp.int32, (CHUNK, LANES), 1)
        low = lane < HEAD_DIM
        cs, sn, bdv = cos_ref[...], sin_ref[...], bd_ref[...]

        def norm_rope(xv, gain):
            r = lax.rsqrt(_head_mean(xv * xv, bdv) + EPS)
            yv = xv * r * gain
            return yv * cs + _swap16(yv, lane) * sn

        def pad_heads(ref, t):
            tr = pltpu.roll(t, HEAD_DIM, 1)
            ref[0, 0] = jnp.where(low, t, 0.0).astype(BF16)
            ref[0, 1] = jnp.where(low, 0.0, tr).astype(BF16)
            ref[1, 0] = jnp.where(low, tr, 0.0).astype(BF16)
            ref[1, 1] = jnp.where(low, 0.0, t).astype(BF16)

        for a in range(ATTN_WIDTH // LANES):
            xv = p_ref[:, a * LANES:(a + 1) * LANES]
            q_ref[:, a * LANES:(a + 1) * LANES] = (norm_rope(xv, qg_ref[...]) * q_scale).astype(BF16)
        kh = norm_rope(p_ref[:, K_COLS[0]:K_COLS[1]], kg_ref[...])
        pad_heads(kp_ref, kh)
        pad_heads(vp_ref, p_ref[:, V_COLS[0]:V_COLS[1]])
        kht = kh.T
        kt_ref[0] = kht[:HEAD_DIM].astype(BF16)
        kt_ref[1] = kht[HEAD_DIM:].astype(BF16)
        for g in range(N_SG_GROUPS):
            u = _gelu(p_ref[:, SU_COLS[0] + g * LANES:SU_COLS[0] + (g + 1) * LANES])
            vg = _gelu(p_ref[:, SV_COLS[0] + g * LANES:SV_COLS[0] + (g + 1) * LANES])
            xc = vg - jnp.mean(vg, axis=-1, keepdims=True)
            vn = xc * lax.rsqrt(jnp.mean(xc * xc, axis=-1, keepdims=True) + EPS)
            mixed = _dot(w_ref[g].astype(BF16), vn.astype(BF16), NN) + b_ref[:, g:g + 1]
            sg_ref[:, g * LANES:(g + 1) * LANES] = (u * mixed).astype(BF16)

    def row(width):
        return pl.BlockSpec((CHUNK, width), lambda i: (i, 0))

    def whole(shape):
        return pl.BlockSpec(shape, lambda i: (0,) * len(shape))

    pad_spec = pl.BlockSpec((2, 2, CHUNK, LANES), lambda i: (0, 0, i, 0))
    return pl.pallas_call(
        body, name=name, grid=(TT // CHUNK,),
        out_shape=[jax.ShapeDtypeStruct((TT, ATTN_WIDTH), BF16),
                   jax.ShapeDtypeStruct((2, 2, TT, LANES), BF16), jax.ShapeDtypeStruct((2, 2, TT, LANES), BF16),
                   jax.ShapeDtypeStruct((2, HEAD_DIM, TT), BF16),
                   jax.ShapeDtypeStruct((TT - ctx_rows, ATTN_WIDTH + SG_WIDTH), BF16)],
        in_specs=[row(IN_WIDTH), row(LANES), row(LANES), whole((1, LANES)), whole((1, LANES)),
                  whole((LANES, LANES)), whole((N_SG_GROUPS, CHUNK, CHUNK)), whole((CHUNK, N_SG_GROUPS))],
        out_specs=[row(ATTN_WIDTH), pad_spec, pad_spec,
                   pl.BlockSpec((2, HEAD_DIM, CHUNK), lambda i: (0, 0, i)),
                   pl.BlockSpec((CHUNK, SG_WIDTH), lambda i: (jnp.maximum(i - off, 0), 1))],
        compiler_params=pltpu.CompilerParams(dimension_semantics=("arbitrary",)),
    )(p, cos, sin, qg, kg, bd, w_sp, b_spt)


def _mix_prep_bwd(p, dq, f, dao, ctx_rows, cos, sin, qg, kg, bd, w_sp, w_spt, b_spt, name):
    TT = p.shape[0]
    off = ctx_rows // CHUNK
    q_scale = HEAD_DIM ** -0.5

    def body(p_ref, dq_ref, f_ref, dsg_ref, cos_ref, sin_ref, qg_ref, kg_ref, bd_ref, w_ref, wt_ref,
             b_ref, dp_ref, dqg_ref, dkg_ref, dw_ref, db_ref):
        i = pl.program_id(0)

        @pl.when(i == 0)
        def _():
            dqg_ref[...] = jnp.zeros_like(dqg_ref)
            dkg_ref[...] = jnp.zeros_like(dkg_ref)
            dw_ref[...] = jnp.zeros_like(dw_ref)
            db_ref[...] = jnp.zeros_like(db_ref)

        latent = (i >= off).astype(F32)
        lane = lax.broadcasted_iota(jnp.int32, (CHUNK, LANES), 1)
        low = lane < HEAD_DIM
        cs, sn, bdv = cos_ref[...], sin_ref[...], bd_ref[...]

        def fold(b0):
            return jnp.where(low, f_ref[0, b0] + pltpu.roll(f_ref[0, b0 + 1], HEAD_DIM, 1),
                             pltpu.roll(f_ref[1, b0], HEAD_DIM, 1) + f_ref[1, b0 + 1])

        def norm_rope_bwd(xv, dout, gain):
            r = lax.rsqrt(_head_mean(xv * xv, bdv) + EPS)
            n = xv * r
            dy = dout * cs + _swap16(dout * sn, lane)
            dn = dy * gain
            dx = r * (dn - n * _head_mean(dn * n, bdv))
            return dx, jnp.sum(dy * n, axis=0, keepdims=True)

        for a in range(ATTN_WIDTH // LANES):
            cols = slice(a * LANES, (a + 1) * LANES)
            dx, dg = norm_rope_bwd(p_ref[:, cols], dq_ref[:, cols] * (latent * q_scale), qg_ref[...])
            dp_ref[:, cols] = dx.astype(BF16)
            dqg_ref[...] += dg
        dx, dg = norm_rope_bwd(p_ref[:, K_COLS[0]:K_COLS[1]], fold(0), kg_ref[...])
        dp_ref[:, K_COLS[0]:K_COLS[1]] = dx.astype(BF16)
        dkg_ref[...] += dg
        dp_ref[:, V_COLS[0]:V_COLS[1]] = fold(2).astype(BF16)
        for g in range(N_SG_GROUPS):
            su = p_ref[:, SU_COLS[0] + g * LANES:SU_COLS[0] + (g + 1) * LANES]
            sv = p_ref[:, SV_COLS[0] + g * LANES:SV_COLS[0] + (g + 1) * LANES]
            (u, dgelu_su), (vg, dgelu_sv) = _gelu_and_grad(su), _gelu_and_grad(sv)
            xc = vg - jnp.mean(vg, axis=-1, keepdims=True)
            rs = lax.rsqrt(jnp.mean(xc * xc, axis=-1, keepdims=True) + EPS)
            vn = xc * rs
            vnb = vn.astype(BF16)
            mixed = _dot(w_ref[g].astype(BF16), vnb, NN) + b_ref[:, g:g + 1]
            dsg = dsg_ref[:, g * LANES:(g + 1) * LANES].astype(F32) * latent
            du = dsg * mixed
            dmix = dsg * u
            dmb = dmix.astype(BF16)
            db_ref[:, g:g + 1] += jnp.sum(dmix, axis=-1, keepdims=True)
            dw_ref[g] += _dot(dmb, vnb, NT)
            dvn = _dot(wt_ref[g].astype(BF16), dmb, NN)
            dvg = rs * (dvn - jnp.mean(dvn, axis=-1, keepdims=True)
                        - vn * jnp.mean(dvn * vn, axis=-1, keepdims=True))
            dp_ref[:, SU_COLS[0] + g * LANES:SU_COLS[0] + (g + 1) * LANES] = (du * dgelu_su).astype(BF16)
            dp_ref[:, SV_COLS[0] + g * LANES:SV_COLS[0] + (g + 1) * LANES] = (dvg * dgelu_sv).astype(BF16)

    def row(width):
        return pl.BlockSpec((CHUNK, width), lambda i: (i, 0))

    def latent_row(width, col_block):
        return pl.BlockSpec((CHUNK, width), lambda i: (jnp.maximum(i - off, 0), col_block))

    def whole(shape):
        return pl.BlockSpec(shape, lambda i: (0,) * len(shape))

    return pl.pallas_call(
        body, name=name, grid=(TT // CHUNK,),
        out_shape=[jax.ShapeDtypeStruct((TT, IN_WIDTH), BF16), jax.ShapeDtypeStruct((1, LANES), F32),
                   jax.ShapeDtypeStruct((1, LANES), F32),
                   jax.ShapeDtypeStruct((N_SG_GROUPS, CHUNK, CHUNK), F32),
                   jax.ShapeDtypeStruct((CHUNK, N_SG_GROUPS), F32)],
        in_specs=[row(IN_WIDTH), latent_row(ATTN_WIDTH, 0),
                  pl.BlockSpec((2, 4, CHUNK, LANES), lambda i: (0, 0, i, 0)),
                  latent_row(SG_WIDTH, 1), row(LANES), row(LANES), whole((1, LANES)), whole((1, LANES)),
                  whole((LANES, LANES)), whole((N_SG_GROUPS, CHUNK, CHUNK)),
                  whole((N_SG_GROUPS, CHUNK, CHUNK)), whole((CHUNK, N_SG_GROUPS))],
        out_specs=[row(IN_WIDTH), whole((1, LANES)), whole((1, LANES)),
                   whole((N_SG_GROUPS, CHUNK, CHUNK)), whole((CHUNK, N_SG_GROUPS))],
        compiler_params=pltpu.CompilerParams(dimension_semantics=("arbitrary",)),
    )(p, dq, f, dao, cos, sin, qg, kg, bd, w_sp, w_spt, b_spt)


def _attn_fwd(q, kpad, vpad, ao, ctx_rows, name, tq=256):
    TT = q.shape[0]
    T = TT - ctx_rows
    tq = _tile(T, tq)
    off = ctx_rows // tq
    group = 2 * LANES

    def body(q_ref, k_ref, v_ref, ao_in, o_ref, lse_ref):
        del ao_in
        lane = lax.broadcasted_iota(jnp.int32, (tq, LANES), 1)
        lse = jnp.zeros((tq, LANES), F32)
        for a in range(2):
            acc = jnp.zeros((tq, LANES), F32)
            qa = q_ref[:, a * LANES:(a + 1) * LANES]
            for b in range(2):
                s = _dot(qa, k_ref[0, b], NT)
                m = jnp.max(s, axis=-1, keepdims=True)
                e = jnp.exp(s - m)
                l = jnp.sum(e, axis=-1, keepdims=True)
                acc = acc + _dot(e.astype(BF16), v_ref[0, b], NN) * (1.0 / l)
                lse = jnp.where(lane == 2 * a + b, m + jnp.log(l), lse)
            o_ref[:, a * LANES:(a + 1) * LANES] = acc.astype(BF16)
        lse_ref[0] = lse

    kv_spec = pl.BlockSpec((1, 2, TT, LANES), lambda j, i: (j, 0, 0, 0))
    return pl.pallas_call(
        body, name=name, grid=(2, T // tq),
        out_shape=[jax.ShapeDtypeStruct(ao.shape, BF16), jax.ShapeDtypeStruct((2, T, LANES), F32)],
        in_specs=[pl.BlockSpec((tq, group), lambda j, i: (i + off, j)), kv_spec, kv_spec,
                  pl.BlockSpec(memory_space=pl.ANY)],
        out_specs=[pl.BlockSpec((tq, group), lambda j, i: (i, j)),
                   pl.BlockSpec((1, tq, LANES), lambda j, i: (j, i, 0))],
        input_output_aliases={3: 0},
        compiler_params=pltpu.CompilerParams(dimension_semantics=("parallel", "parallel")),
    )(q, kpad, vpad, ao)


def _attn_bwd(q, dao, ao, lse, kpad, vpad, kt, ctx_rows, name, tq=256):
    TT = q.shape[0]
    T = TT - ctx_rows
    tq = _tile(T, tq)
    off = ctx_rows // tq
    group = 2 * LANES

    def body(q_ref, do_ref, o_ref, lse_ref, k_ref, v_ref, kt_ref, dq_ref, f_ref):
        i = pl.program_id(1)

        @pl.when(i == 0)
        def _():
            f_ref[...] = jnp.zeros_like(f_ref)

        ktv = kt_ref[0]
        lse_t = lse_ref[0].T
        row = lax.broadcasted_iota(jnp.int32, (SUBLANES, LANES), 0)
        lane = lax.broadcasted_iota(jnp.int32, (SUBLANES, LANES), 1)
        half_ones = (jnp.where(lane < HEAD_DIM, 0, 1) == row).astype(BF16)
        for a in range(2):
            cols = slice(a * LANES, (a + 1) * LANES)
            qa = q_ref[:, cols]
            do32 = do_ref[:, cols].astype(F32)
            doa = do32.astype(BF16)
            hi, lo = _split_bf16(do32 * o_ref[:, cols].astype(F32))
            deltas = _dot(half_ones, hi, NT) + _dot(half_ones, lo, NT)
            halves = []
            for b in range(2):
                h = 2 * a + b
                st = _dot(k_ref[0, b], qa, NT)
                pt = jnp.exp(st - lse_t[h:h + 1, :])
                dpt = _dot(v_ref[0, b], doa, NT)
                dst = (pt * (dpt - deltas[b:b + 1, :])).astype(BF16)
                f_ref[0, b] += _dot(dst, qa, NN)
                f_ref[0, 2 + b] += _dot(pt.astype(BF16), doa, NN)
                halves.append(_dot(ktv, dst, NN))
            dq_ref[:, cols] = jnp.concatenate(halves, axis=0).T

    kv_spec = pl.BlockSpec((1, 2, TT, LANES), lambda j, i: (j, 0, 0, 0))
    out_cols = pl.BlockSpec((tq, group), lambda j, i: (i, j))
    return pl.pallas_call(
        body, name=name, grid=(2, T // tq),
        out_shape=[jax.ShapeDtypeStruct((T, ATTN_WIDTH), F32), jax.ShapeDtypeStruct((2, 4, TT, LANES), F32)],
        in_specs=[pl.BlockSpec((tq, group), lambda j, i: (i + off, j)), out_cols, out_cols,
                  pl.BlockSpec((1, tq, LANES), lambda j, i: (j, i, 0)),
                  kv_spec, kv_spec, pl.BlockSpec((1, HEAD_DIM, TT), lambda j, i: (j, 0, 0))],
        out_specs=[out_cols, pl.BlockSpec((1, 4, TT, LANES), lambda j, i: (j, 0, 0, 0))],
        compiler_params=pltpu.CompilerParams(dimension_semantics=("parallel", "arbitrary")),
    )(q, dao, ao, lse, kpad, vpad, kt)


def _final_fwd_bwd(h, g, target, y, gt, name):
    R, Dm = h.shape
    tm = _tile(R, ROW_BLOCK, 8)

    def body(h_ref, g_ref, t_ref, y_ref, gt_ref, dh_ref, loss_ref, dg_ref, dy_ref, dgt_ref, dsum_ref):
        i = pl.program_id(0)

        @pl.when(i == 0)
        def _():
            for ref in (loss_ref, dg_ref, dgt_ref, dsum_ref):
                ref[...] = jnp.zeros_like(ref)

        hv = h_ref[...]
        r = lax.rsqrt(jnp.mean(hv * hv, axis=-1, keepdims=True) + EPS)
        n = hv * r
        diff = n * g_ref[...] - t_ref[...]
        loss_ref[...] += jnp.sum(diff * diff)
        dout = diff * (1.0 / Dm)
        dg_ref[...] += jnp.sum(dout * n, axis=0, keepdims=True)
        dn = dout * g_ref[...]
        dh = r * (dn - n * jnp.mean(dn * n, axis=-1, keepdims=True))
        dh_ref[...] = dh
        _gate_grads(dh, y_ref, gt_ref, dy_ref, dgt_ref, dsum_ref)

    vec = jax.ShapeDtypeStruct((1, Dm), F32)
    return pl.pallas_call(
        body, name=name, grid=(R // tm,),
        out_shape=[jax.ShapeDtypeStruct((R, Dm), F32), jax.ShapeDtypeStruct((1, LANES), F32), vec,
                   jax.ShapeDtypeStruct((R, Dm), BF16), vec, vec],
        in_specs=[_row_spec(tm, Dm), _vec_spec(Dm), _row_spec(tm, Dm), _row_spec(tm, Dm), _vec_spec(Dm)],
        out_specs=[_row_spec(tm, Dm), _vec_spec(LANES), _vec_spec(Dm), _row_spec(tm, Dm), _vec_spec(Dm),
                   _vec_spec(Dm)],
        compiler_params=pltpu.CompilerParams(dimension_semantics=("arbitrary",)),
    )(h, g, target, y, gt)


MOD_ROWS = 16


def _mod_fwd(c_rows, w_mod, name):
    L, Dm, n = w_mod.shape

    def body(c_ref, w_ref, o_ref):
        o_ref[0] = _dot3(_silu(c_ref[...]), w_ref[0], NN)

    return pl.pallas_call(
        body, name=name, grid=(L,),
        out_shape=jax.ShapeDtypeStruct((L, MOD_ROWS, n), F32),
        in_specs=[pl.BlockSpec((MOD_ROWS, Dm), lambda l: (0, 0)), pl.BlockSpec((1, Dm, n), lambda l: (l, 0, 0))],
        out_specs=pl.BlockSpec((1, MOD_ROWS, n), lambda l: (l, 0, 0)),
        compiler_params=pltpu.CompilerParams(dimension_semantics=("parallel",)),
    )(c_rows, w_mod)


def _mod_bwd(c_rows_t, dmod, w_mod, name):
    L, Dm, n = w_mod.shape

    def body(ct_ref, d_ref, w_ref, gw_ref, ds_ref):
        dm = d_ref[0]
        gw_ref[0] = _dot3(_silu(ct_ref[...]), dm, NN)
        ds_ref[0] = _dot3(dm[:MOD_ROWS], w_ref[0], NT)

    return pl.pallas_call(
        body, name=name, grid=(L,),
        out_shape=[jax.ShapeDtypeStruct((L, Dm, n), F32), jax.ShapeDtypeStruct((L, MOD_ROWS, Dm), F32)],
        in_specs=[pl.BlockSpec((Dm, LANES), lambda l: (0, 0)), pl.BlockSpec((1, LANES, n), lambda l: (l, 0, 0)),
                  pl.BlockSpec((1, Dm, n), lambda l: (l, 0, 0))],
        out_specs=[pl.BlockSpec((1, Dm, n), lambda l: (l, 0, 0)),
                   pl.BlockSpec((1, MOD_ROWS, Dm), lambda l: (l, 0, 0))],
        compiler_params=pltpu.CompilerParams(dimension_semantics=("parallel",)),
    )(c_rows_t, dmod, w_mod)


def _adam_update(w, g, m, v):
    c1 = 1.0 - ADAM_B1 ** ADAM_STEP
    c2 = 1.0 - ADAM_B2 ** ADAM_STEP
    mn = ADAM_B1 * m + (1.0 - ADAM_B1) * g
    vn = ADAM_B2 * v + (1.0 - ADAM_B2) * (g * g)
    return -ADAM_LR * ((mn / c1) / (jnp.sqrt(vn / c2) + ADAM_EPS) + ADAM_WD * w), mn, vn


def _adamw(w, g, m, v, name):
    R, Cw = w.shape
    tm = _tile(R, ADAM_ROWS, 8)

    def body(w_ref, g_ref, m_ref, v_ref, d_ref, mo_ref, vo_ref):
        d_ref[...], mo_ref[...], vo_ref[...] = _adam_update(w_ref[...], g_ref[...], m_ref[...], v_ref[...])

    spec = pl.BlockSpec((tm, Cw), lambda i: (i, 0))
    return pl.pallas_call(
        body, name=name, grid=(R // tm,),
        out_shape=[jax.ShapeDtypeStruct((R, Cw), F32)] * 3,
        in_specs=[spec] * 4, out_specs=[spec] * 3,
        compiler_params=pltpu.CompilerParams(dimension_semantics=("parallel",)),
    )(w, g, m, v)


def _adamw_recv(w, m, v, recvs, name):
    L, R, n = w.shape
    tm = _tile(R, ADAM_ROWS, 8)
    nblk = R // tm
    parts = [r.reshape(N_DEV, R, n) for r in recvs]

    def body(*refs):
        w_ref, m_ref, v_ref = refs[:3]
        part_refs = refs[3:3 + L]
        g_ref, d_ref, mo_ref, vo_ref, gsum = refs[3 + L:]
        l = pl.program_id(0)
        for ll in range(L):
            @pl.when(l == ll)
            def _(ll=ll):
                acc = part_refs[ll][0].astype(F32)
                for s in range(1, N_DEV):
                    acc = acc + part_refs[ll][s].astype(F32)
                gsum[...] = acc
        g = gsum[...]
        g_ref[0] = g
        d_ref[0], mo_ref[0], vo_ref[0] = _adam_update(w_ref[0], g, m_ref[0], v_ref[0])

    def part_spec(ll):
        return pl.BlockSpec((N_DEV, tm, n), lambda l, i: (0, jnp.where(l == ll, i, jnp.where(l < ll, 0, nblk - 1)), 0))

    spec = pl.BlockSpec((1, tm, n), lambda l, i: (l, i, 0))
    return pl.pallas_call(
        body, name=name, grid=(L, nblk),
        out_shape=[jax.ShapeDtypeStruct((L, R, n), F32)] * 4,
        in_specs=[spec] * 3 + [part_spec(ll) for ll in range(L)], out_specs=[spec] * 4,
        scratch_shapes=[pltpu.VMEM((tm, n), F32)],
        compiler_params=pltpu.CompilerParams(dimension_semantics=("parallel", "parallel")),
    )(w, m, v, *parts)


def _pack(parts, row_mult=8):
    flat, offs, pos = [], [], 0
    for t in parts:
        t = t.reshape(-1).astype(F32)
        size = -(-t.shape[0] // LANES) * LANES
        flat.append(jnp.pad(t, (0, size - t.shape[0])))
        offs.append(pos)
        pos += size
    total = -(-pos // (LANES * row_mult)) * (LANES * row_mult)
    if total > pos:
        flat.append(jnp.zeros((total - pos,), F32))
    return jnp.concatenate(flat).reshape(-1, LANES), offs


def _take(buf, off, shape):
    size = math.prod(shape)
    return buf[..., off:off + size].reshape(buf.shape[:-1] + tuple(shape))


def _rope_tables(T, ctx_rows):
    pos = jnp.arange(T)
    row = (pos // GRID_W).astype(F32)
    col = (pos % GRID_W).astype(F32)
    half = HEAD_DIM // 4
    inv = ROPE_THETA ** (-jnp.arange(0, 2 * half, 2, dtype=F32) / (2 * half))
    ang_r, ang_c = row[:, None] * inv[None, :], col[:, None] * inv[None, :]
    cos = jnp.concatenate([jnp.cos(ang_r)] * 2 + [jnp.cos(ang_c)] * 2, axis=1)
    sin = jnp.concatenate([-jnp.sin(ang_r), jnp.sin(ang_r), -jnp.sin(ang_c), jnp.sin(ang_c)], axis=1)
    cos = jnp.concatenate([jnp.ones((ctx_rows, HEAD_DIM), F32), cos], axis=0)
    sin = jnp.concatenate([jnp.zeros((ctx_rows, HEAD_DIM), F32), sin], axis=0)
    return jnp.tile(cos, (1, 2)), jnp.tile(sin, (1, 2))


def kernel(x, c, ctx, c_ctx, w_mod, b_mod, g_mix, g_ffn, w_ffn_in, w_ffn_out, w_in, q_gain, k_gain, w_sp, b_sp, w_out, w_pw1, b_pw1, w_dw, b_dw, ln_g, ln_b, w_pw2, b_pw2, g_final, loss_target, m_c_ctx, m_w_mod, m_b_mod, m_g_mix, m_g_ffn, m_w_ffn_in, m_w_ffn_out, m_w_in, m_q_gain, m_k_gain, m_w_sp, m_b_sp, m_w_out, m_w_pw1, m_b_pw1, m_w_dw, m_b_dw, m_ln_g, m_ln_b, m_w_pw2, m_b_pw2, m_g_final, v_c_ctx, v_w_mod, v_b_mod, v_g_mix, v_g_ffn, v_w_ffn_in, v_w_ffn_out, v_w_in, v_q_gain, v_k_gain, v_w_sp, v_b_sp, v_w_out, v_w_pw1, v_b_pw1, v_w_dw, v_b_dw, v_ln_g, v_ln_b, v_w_pw2, v_b_pw2, v_g_final):
    weights = dict(c_ctx=c_ctx, w_mod=w_mod, b_mod=b_mod, g_mix=g_mix, g_ffn=g_ffn, w_ffn_in=w_ffn_in,
                   w_ffn_out=w_ffn_out, w_in=w_in, q_gain=q_gain, k_gain=k_gain, w_sp=w_sp, b_sp=b_sp,
                   w_out=w_out, w_pw1=w_pw1, b_pw1=b_pw1, w_dw=w_dw, b_dw=b_dw, ln_g=ln_g, ln_b=ln_b,
                   w_pw2=w_pw2, b_pw2=b_pw2, g_final=g_final)
    moments_m = dict(c_ctx=m_c_ctx, w_mod=m_w_mod, b_mod=m_b_mod, g_mix=m_g_mix, g_ffn=m_g_ffn,
                     w_ffn_in=m_w_ffn_in, w_ffn_out=m_w_ffn_out, w_in=m_w_in, q_gain=m_q_gain,
                     k_gain=m_k_gain, w_sp=m_w_sp, b_sp=m_b_sp, w_out=m_w_out, w_pw1=m_w_pw1,
                     b_pw1=m_b_pw1, w_dw=m_w_dw, b_dw=m_b_dw, ln_g=m_ln_g, ln_b=m_ln_b, w_pw2=m_w_pw2,
                     b_pw2=m_b_pw2, g_final=m_g_final)
    moments_v = dict(c_ctx=v_c_ctx, w_mod=v_w_mod, b_mod=v_b_mod, g_mix=v_g_mix, g_ffn=v_g_ffn,
                     w_ffn_in=v_w_ffn_in, w_ffn_out=v_w_ffn_out, w_in=v_w_in, q_gain=v_q_gain,
                     k_gain=v_k_gain, w_sp=v_w_sp, b_sp=v_b_sp, w_out=v_w_out, w_pw1=v_w_pw1,
                     b_pw1=v_b_pw1, w_dw=v_w_dw, b_dw=v_b_dw, ln_g=v_ln_g, ln_b=v_ln_b, w_pw2=v_w_pw2,
                     b_pw2=v_b_pw2, g_final=v_g_final)
    names = list(weights)

    T, C = x.shape[1], ctx.shape[1]
    Dm = D_MODEL
    me = 4 * lax.axis_index("x") + 2 * lax.axis_index("y") + lax.axis_index("c")
    h0 = x[0]
    ctx2 = ctx[0]
    target = loss_target[0]

    small_sharded = (("w_dw", w_dw[0]), ("b_pw1", b_pw1), ("b_dw", b_dw), ("ln_g", ln_g), ("ln_b", ln_b),
                     ("b_pw2", b_pw2))
    buf1, offs1 = _pack([c] + [t for _, t in small_sharded])
    w_in_t, m_w_in_t, v_w_in_t = (jnp.swapaxes(t, 1, 2) for t in (w_in, m_w_in, v_w_in))
    w_ffi_t, m_w_ffi_t, v_w_ffi_t = (jnp.swapaxes(t, 1, 2) for t in (w_ffn_in, m_w_ffn_in, v_w_ffn_in))
    got1, W_in_t = _all_gather([buf1, w_in_t[0].astype(BF16)], "gather_cond", False)
    got1 = got1.reshape(N_DEV, -1)
    c_all = _take(got1, offs1[0], (Dm,))
    full_small = {}
    for (nm, t), off in zip(small_sharded, offs1[1:]):
        seg = _take(got1, off, t.shape)
        full_small[nm] = jnp.moveaxis(seg, 0, -2).reshape(t.shape[:-1] + (N_DEV * t.shape[-1],))
    w_dw_f, b_pw1_f = full_small["w_dw"], full_small["b_pw1"]
    b_dw_f, ln_g_f, ln_b_f, b_pw2_f = (full_small[k] for k in ("b_dw", "ln_g", "ln_b", "b_pw2"))

    c_rows = jnp.concatenate([c_all, c_ctx[None, :], jnp.zeros((MOD_ROWS - N_DEV - 1, Dm), F32)], axis=0)
    mod_part = _mod_fwd(c_rows, w_mod, "mod_fwd")
    n_mod = w_mod.shape[2]
    got2 = _all_gather([mod_part.reshape(-1, LANES)], "gather_mod", True)[0]
    mod_all = got2.reshape(N_DEV, 2, MOD_ROWS, n_mod).transpose(1, 2, 0, 3).reshape(2, MOD_ROWS, N_DEV * n_mod)
    mod_all = mod_all + b_mod[:, None, :]
    my_mod = lax.dynamic_index_in_dim(mod_all, me, axis=1, keepdims=False)
    sh1, sc1, gt1, sh2, sc2, gt2 = ([my_mod[l:l + 1, k * Dm:(k + 1) * Dm] for l in range(2)] for k in range(6))
    csh1 = mod_all[0, N_DEV:N_DEV + 1, 0:Dm]
    csc1 = mod_all[0, N_DEV:N_DEV + 1, Dm:2 * Dm]

    behind = got2[0:1, 0:1] * 0.0
    gather_groups = [[w_out[0]], [w_ffi_t[0], w_ffn_out[0]], [w_pw1[0], w_pw2[0]], [w_ffi_t[1], w_ffn_out[1]]]
    gathers = [_push_begin([(t + behind).astype(BF16) for t in grp], True, f"gather_start{k}")
               for k, grp in enumerate(gather_groups)]
    started = sum(h[4][0:1, 0:1] for h in gathers)

    def gathered(k, after):
        return _push_end(gathers[k], after, f"gather_wait{k}")[1]

    def ffn_weights(k, after):
        wi, wo = gathered(k, after)
        return wi.reshape(N_DEV, FF_SHARD, Dm), wo.reshape(N_DEV // 2, FF_SHARD, Dm)

    def col_gathered(t, n):
        return t.reshape(N_DEV, Dm, n).transpose(1, 0, 2).reshape(Dm, N_DEV * n)

    W_ffi, W_ffo = [None, None], [None, None]

    g_mix_r = [g_mix[l:l + 1] for l in range(2)]
    g_ffn_r = [g_ffn[l:l + 1] for l in range(2)]
    g_fin = g_final[None, :]

    cos, sin = _rope_tables(T, C)
    qg = jnp.tile(q_gain, (1, 2))
    kg = jnp.tile(k_gain, (1, 2))
    lane_head = jnp.arange(LANES) // HEAD_DIM
    bd = (lane_head[:, None] == lane_head[None, :]).astype(BF16)
    w_sp0 = w_sp[0]
    w_spt0 = w_sp0.transpose(0, 2, 1)
    b_spt0 = b_sp[0].T

    XM = _norm_mod_fwd_cat(ctx2, h0, g_mix_r[0], csc1, csh1, sc1[0] + started, sh1[0], "norm_mix0")
    P = _mm(XM, W_in_t, "nt", "in_proj", tm=1088, tn=IN_WIDTH)
    qh, kpad, vpad, kt, ao = _mix_prep_fwd(P, C, cos, sin, qg, kg, bd, w_sp0, b_spt0, "mix_prep")
    ao, lse = _attn_fwd(qh, kpad, vpad, ao, C, "attn_fwd")
    W_out, = gathered(0, ao)
    h1, y0, xf0 = _mm(ao, W_out, "nn", "out_proj", res=h0, gate=gt1[0], raw_out=True,
                      norm=(g_ffn_r[0], sc2[0], sh2[0]))

    def ffn_fwd(h_in, xf, l, norm_next):
        W_ffi[l], W_ffo[l] = ffn_weights(1 + 2 * l, xf)
        gu, act = _ffn_in_swiglu(xf, W_ffi[l], f"ffn_in{l}")
        outs = _mm_sum_shards(act, W_ffo[l], "nn", f"ffn_out{l}", res=h_in, gate=gt2[l], raw_out=True,
                              norm=norm_next)
        return tuple(outs) + (None,) * (3 - len(outs)) + (gu, act)

    h2, f0, xm1, gu0, act0 = ffn_fwd(h1, xf0, 0, (g_mix_r[1], sc1[1], sh1[1]))

    W_pw1, W_pw2 = gathered(2, xm1)
    W_pw1 = col_gathered(W_pw1, 2 * Dm // N_DEV)
    ag, hg = _pw1_glu(xm1, W_pw1, b_pw1_f, "pw1")
    hd = _conv_fwd(hg, w_dw_f, b_dw_f, "conv")
    hs = _ln_silu_fwd(hd, ln_g_f, ln_b_f, "ln_silu")
    h3, y1, xf1 = _mm(hs, W_pw2, "nn", "pw2", bias=b_pw2_f, res=h2, gate=gt1[1], raw_out=True,
                      norm=(g_ffn_r[1], sc2[1], sh2[1]))
    h4, f1, _, gu1, act1 = ffn_fwd(h3, xf1, 1, None)

    dh4, sq_err, dg_final, df1, dgt2_1, _ = _final_fwd_bwd(h4, g_fin, target, f1, gt2[1], "loss_head")
    loss_local = (0.5 / Dm) * sq_err[0, 0:1]

    def col_shards(g, n):
        return g.reshape(Dm, N_DEV, n).transpose(1, 0, 2).reshape(N_DEV * Dm, n)

    def exchange_begin(k, parts):
        return _push_begin(parts, False, f"exchange_start{k}")

    def zero_of(handle):
        return handle[4][0:1, 0:1]

    def ffn_bwd(df, xf, gu, act, l):
        dw_out = _mm_tn_shard_rows(act, df, f"ffn_out_dw{l}", BF16)
        dgu = _ffn_out_dx_swiglu(df, W_ffo[l], gu, f"ffn_out_dx{l}").reshape(N_DEV, T, FF_SHARD)
        dw_in = _mm_tn_shard_rows(dgu, xf, f"ffn_in_dw{l}", BF16)
        dxf = _mm_sum_shards(dgu, W_ffi[l], "nn", f"ffn_in_dx{l}", BF16, tm=512)
        return dw_in, dw_out, dxf

    dW_ffi1, dW_ffo1, dxf1 = ffn_bwd(df1, xf1, gu1, act1, 1)
    ex0 = exchange_begin(0, [dW_ffi1.reshape(2 * D_FF, Dm), dW_ffo1.reshape(D_FF, Dm)])
    dh3, da, dsh, dy1, dgt1_1, db_pw2 = _norm_mod_bwd(h3, g_ffn_r[1], sc2[1], dxf1, dh4, "norm_ffn_bwd1",
                                                       gate=(y1, gt1[1] + zero_of(ex0)))
    dmod_ffn1 = (dsh, da * g_ffn_r[1], dgt2_1)
    dg_ffn1 = da * (1.0 + sc2[1])

    dW_pw2 = _mm(hs, dy1, "tn", "pw2_dw", BF16, tk=2048)
    dhs = _mm(dy1, W_pw2, "nt", "pw2_dx", BF16)
    dhd, dln_g, dln_b, db_dw = _ln_silu_bwd(dhs, hd, ln_g_f, ln_b_f, "ln_silu_bwd")
    dhg, dw_dw = _conv_bwd(dhd, hg, w_dw_f, "conv_bwd")
    dag, db_pw1 = _glu_bwd(ag, dhg, "glu_bwd")
    dW_pw1 = _mm(xm1, dag, "tn", "pw1_dw", BF16, tk=2048)
    dxm1 = _mm(dag, W_pw1, "nt", "pw1_dx", BF16, tk=2048)
    ex1 = exchange_begin(1, [col_shards(dW_pw1, 2 * Dm // N_DEV), dW_pw2])
    dh2, da, dsh, df0, dgt2_0, _ = _norm_mod_bwd(h2, g_mix_r[1], sc1[1], dxm1, dh3, "norm_mix1_bwd",
                                                 gate=(f0, gt2[0] + zero_of(ex1)))
    dmod_mix1 = (dsh, da * g_mix_r[1], dgt1_1)
    dg_mix1 = da * (1.0 + sc1[1])

    dW_ffi0, dW_ffo0, dxf0 = ffn_bwd(df0, xf0, gu0, act0, 0)
    ex2 = exchange_begin(2, [dW_ffi0.reshape(2 * D_FF, Dm), dW_ffo0.reshape(D_FF, Dm)])
    dh1, da, dsh, dy0, dgt1_0, _ = _norm_mod_bwd(h1, g_ffn_r[0], sc2[0], dxf0, dh2, "norm_ffn_bwd0",
                                                 gate=(y0, gt1[0] + zero_of(ex2)))
    dmod_ffn0 = (dsh, da * g_ffn_r[0], dgt2_0)
    dg_ffn0 = da * (1.0 + sc2[0])

    dW_out = _mm(ao, dy0, "tn", "out_proj_dw", BF16, tk=2048)
    ex_out = exchange_begin(4, [dW_out])
    dao = _mm(dy0, W_out + zero_of(ex_out).astype(BF16), "nt", "out_proj_dx", BF16)
    dq, f_acc = _attn_bwd(qh, dao, ao, lse, kpad, vpad, kt, C, "attn_bwd")
    dP, dqg, dkg, dw_sp0, db_spt0 = _mix_prep_bwd(P, dq, f_acc, dao, C, cos, sin, qg, kg, bd, w_sp0, w_spt0,
                                                  b_spt0, "mix_prep_bwd")
    dW_in_t = _mm(dP, XM, "tn", "in_proj_dw", BF16, tm=896, tk=2176)
    dXM = _mm(dP, W_in_t, "nn", "in_proj_dx", BF16, tm=1088, tk=IN_WIDTH)
    dh0, da, dsh = _norm_mod_bwd(h0, g_mix_r[0], sc1[0], dXM, dh1, "norm_mix0_bwd", dxm_row_off=C)
    _, dac, dcsh = _norm_mod_bwd(ctx2, g_mix_r[0], csc1, dXM, None, "norm_ctx_bwd")
    dmod_mix0 = (dsh, da * g_mix_r[0], dgt1_0)
    dg_mix0 = da * (1.0 + sc1[0]) + dac * (1.0 + csc1)
    dcmod = jnp.concatenate([dcsh, dac * g_mix_r[0]], axis=1)

    dmod_mine = jnp.stack([jnp.concatenate(dmod_mix0 + dmod_ffn0, axis=1)[0],
                           jnp.concatenate(dmod_mix1 + dmod_ffn1, axis=1)[0]])

    small_grads = [
        ("loss", loss_local), ("g_final", dg_final), ("g_mix", jnp.concatenate([dg_mix0, dg_mix1])),
        ("g_ffn", jnp.concatenate([dg_ffn0, dg_ffn1])),
        ("q_gain", dqg[:, :HEAD_DIM] + dqg[:, HEAD_DIM:]), ("k_gain", dkg[:, :HEAD_DIM] + dkg[:, HEAD_DIM:]),
        ("w_sp", dw_sp0[None]), ("b_sp", db_spt0.T[None]), ("b_pw1", db_pw1), ("w_dw", dw_dw[None]),
        ("b_dw", db_dw), ("ln_g", dln_g), ("ln_b", dln_b), ("b_pw2", db_pw2), ("dcmod", dcmod),
        ("dmod", dmod_mine),
    ]
    buf3, offs3 = _pack([t for _, t in small_grads])
    off3 = {nm: off for (nm, _), off in zip(small_grads, offs3)}
    shape3 = {nm: t.shape for nm, t in small_grads}
    small_push = _push_begin([buf3], True, "small_grads_start")
    ex3 = exchange_begin(3, [dW_in_t + zero_of(small_push).astype(BF16)])

    grads, delta, new_m, new_v = {}, {}, {}, {}

    def exchanged(k, handle, after):
        return _push_end(handle, after, f"exchange_wait{k}")[1]

    def adamw_big(nm, parts, transposed=False, wmv=None):
        w3, m3, v3 = wmv if wmv is not None else (weights[nm], moments_m[nm], moments_v[nm])
        outs4 = _adamw_recv(w3, m3, v3, parts, f"adamw_{nm}")
        if transposed:
            outs4 = [jnp.swapaxes(t, 1, 2) for t in outs4]
        grads[nm], delta[nm], new_m[nm], new_v[nm] = outs4

    pushed = ex3[4]
    r_ffi1, r_ffo1 = exchanged(0, ex0, pushed)
    r_pw1, r_pw2 = exchanged(1, ex1, pushed)
    r_ffi0, r_ffo0 = exchanged(2, ex2, pushed)
    r_out, = exchanged(4, ex_out, pushed)
    adamw_big("w_ffn_in", [r_ffi0, r_ffi1], True, (w_ffi_t, m_w_ffi_t, v_w_ffi_t))
    adamw_big("w_ffn_out", [r_ffo0, r_ffo1])
    adamw_big("w_pw1", [r_pw1])
    adamw_big("w_pw2", [r_pw2])
    adamw_big("w_out", [r_out])

    got3 = _push_end(small_push, delta["w_out"], "small_grads_wait")[1][0].reshape(N_DEV, buf3.shape[0], LANES)
    sum3 = _sum_devices(got3, "sum_small_grads").reshape(-1)

    def summed(nm):
        return _take(sum3, off3[nm], shape3[nm])

    loss = summed("loss")[0]
    dcmod_sum = summed("dcmod")
    dmod_rows = _take(got3.reshape(N_DEV, -1), off3["dmod"], (2, 6 * Dm)).transpose(1, 0, 2)
    ctx_row = jnp.concatenate([jnp.pad(dcmod_sum, ((0, 0), (0, 4 * Dm))), jnp.zeros((1, 6 * Dm), F32)])
    dmod_all = jnp.concatenate([dmod_rows, ctx_row[:, None, :],
                                jnp.zeros((2, LANES - N_DEV - 1, 6 * Dm), F32)], axis=1)
    grads["b_mod"] = summed("dmod") + ctx_row
    dmod_shard = lax.dynamic_slice_in_dim(dmod_all, me * n_mod, n_mod, axis=2)
    c_rows_t = jnp.pad(c_rows.T, ((0, 0), (0, LANES - MOD_ROWS)))
    grads["w_mod"], ds_part = _mod_bwd(c_rows_t, dmod_shard, w_mod, "mod_bwd")

    buf4, _ = _pack([ds_part[0, N_DEV]])
    got4 = _all_gather([buf4], "gather_c_ctx_grad", True)[0].reshape(N_DEV, buf4.shape[0], LANES)
    ds_ctx = _sum_devices(got4, "sum_c_ctx_grad").reshape(-1)[:Dm]
    grads["c_ctx"] = ds_ctx * _dsilu(c_ctx)

    for nm in ("g_final", "g_mix", "g_ffn", "q_gain", "k_gain", "w_sp", "b_sp"):
        grads[nm] = summed(nm).reshape(weights[nm].shape)
    for nm in ("b_pw1", "w_dw", "b_dw", "ln_g", "ln_b", "b_pw2"):
        n_loc = weights[nm].shape[-1]
        grads[nm] = lax.dynamic_slice_in_dim(summed(nm), me * n_loc, n_loc, axis=-1).reshape(weights[nm].shape)

    shp = w_mod.shape
    outs = _adamw(w_mod.reshape(-1, shp[-1]), grads["w_mod"].reshape(-1, shp[-1]),
                  m_w_mod.reshape(-1, shp[-1]), v_w_mod.reshape(-1, shp[-1]), "adamw_w_mod")
    delta["w_mod"], new_m["w_mod"], new_v["w_mod"] = (o.reshape(shp) for o in outs)
    big_names = ("w_mod", "w_ffn_in", "w_ffn_out", "w_in", "w_out", "w_pw1", "w_pw2")
    small_names = [nm for nm in names if nm not in big_names]
    packs = [_pack([src[nm] for nm in small_names]) for src in (weights, grads, moments_m, moments_v)]
    offs_s = packs[0][1]
    outs = _adamw(*[pk[0] for pk in packs], "adamw_small")
    for o, dst in zip(outs, (delta, new_m, new_v)):
        o = o.reshape(-1)
        for nm, off in zip(small_names, offs_s):
            dst[nm] = _take(o, off, weights[nm].shape)
    r_in, = exchanged(3, ex3, outs[0])
    adamw_big("w_in", [r_in], True, (w_in_t, m_w_in_t, v_w_in_t))

    return (loss, dh0[None], *[grads[n] for n in names], *[delta[n] for n in names],
            *[new_m[n] for n in names], *[new_v[n] for n in names])
```

```python
import math

import jax
import jax.numpy as jnp
from jax import lax
from jax.experimental import pallas as pl
from jax.experimental.pallas import tpu as pltpu

F32 = jnp.float32
BF16 = jnp.bfloat16
MESH = pl.DeviceIdType.MESH

N_DEV = 8
D_MODEL = 1024
EPS = 1e-6
HEAD_DIM = 64
ATTN_WIDTH = 512
KV_WIDTH = 128
SG_WIDTH = 512
N_SG_GROUPS = 4
CHUNK = 128
IN_WIDTH = 1792
D_FF = 2816
FF_SHARD = 2 * D_FF // N_DEV
CONV_WIDTH = 31
CONV_HALO = 16
GRID_W = 64
ROPE_THETA = 10000.0
LANES = 128
SUBLANES = 8
ROW_BLOCK = 512
ADAM_ROWS = 256
ADAM_LR, ADAM_B1, ADAM_B2, ADAM_EPS, ADAM_WD, ADAM_STEP = 0.001, 0.9, 0.999, 1e-08, 0.01, 10


def _tile(n, target, mult=LANES):
    best = None
    for t in range(mult, min(n, target) + 1, mult):
        if n % t == 0:
            best = t
    return best if best is not None else n


def _sigmoid(x):
    return 1.0 / (1.0 + jnp.exp(-x))


def _silu(x):
    return x * _sigmoid(x)


def _dsilu(x):
    s = _sigmoid(x)
    return s * (1.0 + x * (1.0 - s))


_GELU_K = math.sqrt(2.0 / math.pi)


def _gelu(x):
    return 0.5 * x * (1.0 + jnp.tanh(_GELU_K * (x + 0.044715 * x * x * x)))


def _gelu_and_grad(x):
    x2 = x * x
    t = jnp.tanh(_GELU_K * x * (1.0 + 0.044715 * x2))
    half = 0.5 * (1.0 + t)
    return x * half, half + 0.5 * x * (1.0 - t * t) * _GELU_K * (1.0 + 3.0 * 0.044715 * x2)


def _split_bf16(x):
    hi = x.astype(BF16)
    lo = (x - hi.astype(F32)).astype(BF16)
    return hi, lo


def _dot(a, b, dims):
    return lax.dot_general(a, b, (dims, ((), ())), preferred_element_type=F32)


def _dot3(a, b, dims):
    ah, al = _split_bf16(a)
    bh, bl = _split_bf16(b)
    return _dot(ah, bh, dims) + _dot(ah, bl, dims) + _dot(al, bh, dims)


NN = ((1,), (0,))
NT = ((1,), (1,))
TN = ((0,), (0,))


def _all_gather(xs, name, in_vmem):
    n_arr = len(xs)

    def body(*refs):
        x_refs, out_refs = refs[:n_arr], refs[n_arr:2 * n_arr]
        send_sems, recv_sems, local_sems = refs[2 * n_arr:]
        x, y, c = lax.axis_index("x"), lax.axis_index("y"), lax.axis_index("c")
        me, sibling = (x, y, c), (x, y, 1 - c)
        chips = [(1 - x, y), (x, 1 - y), (1 - x, 1 - y)]

        def rows(a, px, py, pc):
            m_per = xs[a].shape[0]
            return out_refs[a].at[pl.ds((4 * px + 2 * py + pc) * m_per, m_per), :]

        def copy(a, k, block, to, src=None):
            return pltpu.make_async_remote_copy(
                src_ref=rows(a, *block) if src is None else src,
                dst_ref=rows(a, *block),
                send_sem=send_sems.at[7 * a + k],
                recv_sem=recv_sems.at[7 * a + k],
                device_id=to,
                device_id_type=MESH,
            )

        mine, first, passed = [], [], []
        for a in range(n_arr):
            mine.append(pltpu.make_async_copy(x_refs[a], rows(a, *me), local_sems.at[a]))
            mine[-1].start()
            first.append(copy(a, 0, me, sibling, src=x_refs[a]))
            first += [copy(a, 1 + j, me, (*chip, c), src=x_refs[a]) for j, chip in enumerate(chips)]
        for cp in first:
            cp.start()
        for a in range(n_arr):
            for j, chip in enumerate(chips):
                copy(a, 1 + j, (*chip, c), me).wait_recv()
                passed.append(copy(a, 4 + j, (*chip, c), sibling))
                passed[-1].start()
        for a in range(n_arr):
            copy(a, 0, sibling, me).wait_recv()
            for j, chip in enumerate(chips):
                copy(a, 4 + j, (*chip, 1 - c), me).wait_recv()
        for cp in first + passed:
            cp.wait_send()
        for cp in mine:
            cp.wait()

    space = pltpu.VMEM if in_vmem else pl.ANY
    return pl.pallas_call(
        body,
        name=name,
        out_shape=[jax.ShapeDtypeStruct((N_DEV * t.shape[0], t.shape[1]), t.dtype) for t in xs],
        in_specs=[pl.BlockSpec(memory_space=space)] * n_arr,
        out_specs=[pl.BlockSpec(memory_space=space)] * n_arr,
        scratch_shapes=[
            pltpu.SemaphoreType.DMA((7 * n_arr,)),
            pltpu.SemaphoreType.DMA((7 * n_arr,)),
            pltpu.SemaphoreType.DMA((n_arr,)),
        ],
    )(*xs)


HBM_SPEC = pl.BlockSpec(memory_space=pltpu.HBM)
SEM_SPEC = pl.BlockSpec(memory_space=pltpu.SEMAPHORE)
DATAFLOW_EFFECT = pltpu.SideEffectType.DATAFLOW_SIDE_EFFECTING


def _peers(x, y, c):
    for k in range(1, N_DEV):
        px = 1 - x if (k >> 2) & 1 else x
        py = 1 - y if (k >> 1) & 1 else y
        pc = 1 - c if k & 1 else c
        yield k - 1, (px, py, pc), 4 * px + 2 * py + pc


def _push_copies(src_refs, land_refs, send_sems, recv_sems, shapes, whole_src):
    x, y, c = lax.axis_index("x"), lax.axis_index("y"), lax.axis_index("c")
    me = 4 * x + 2 * y + c
    for a, (m_per, _) in enumerate(shapes):
        def block(ref, idx, m_per=m_per):
            return ref.at[pl.ds(idx * m_per, m_per), :]

        for k, peer, pidx in _peers(x, y, c):
            src = src_refs[a] if whole_src else block(src_refs[a], pidx)
            sems = dict(send_sem=send_sems.at[N_DEV * a + k], recv_sem=recv_sems.at[N_DEV * a + k],
                        device_id=peer, device_id_type=MESH)
            yield (pltpu.make_async_remote_copy(src_ref=src, dst_ref=block(land_refs[a], me), **sems),
                   pltpu.make_async_remote_copy(src_ref=src, dst_ref=block(land_refs[a], pidx), **sems))


def _own_copies(src_refs, land_refs, recv_sems, shapes, whole_src):
    me = 4 * lax.axis_index("x") + 2 * lax.axis_index("y") + lax.axis_index("c")
    for a, (m_per, _) in enumerate(shapes):
        mine = pl.ds(me * m_per, m_per)
        src = src_refs[a] if whole_src else src_refs[a].at[mine, :]
        yield pltpu.make_async_copy(src, land_refs[a].at[mine, :], recv_sems.at[N_DEV * a + N_DEV - 1])


def _push_begin(srcs, whole_src, name):
    n_arr = len(srcs)
    shapes = [(t.shape[0] if whole_src else t.shape[0] // N_DEV, t.shape[1]) for t in srcs]
    lands = [lax.empty((N_DEV * m, n), t.dtype) for (m, n), t in zip(shapes, srcs)]

    def body(*refs):
        src_refs, land_refs = refs[:n_arr], refs[n_arr:2 * n_arr]
        send_sems, recv_sems = refs[2 * n_arr], refs[2 * n_arr + 1]
        token = refs[-1]
        for outgoing, _ in _push_copies(src_refs, land_refs, send_sems, recv_sems, shapes, whole_src):
            outgoing.start()
        for own in _own_copies(src_refs, land_refs, recv_sems, shapes, whole_src):
            own.start()
        token[...] = jnp.zeros_like(token)

    operands = [pltpu.with_memory_space_constraint(t, pltpu.HBM) for t in list(srcs) + lands]
    outs = pl.pallas_call(
        body, name=name,
        out_shape=(pltpu.SemaphoreType.DMA((N_DEV * n_arr,)), pltpu.SemaphoreType.DMA((N_DEV * n_arr,)),
                   *[pltpu.HBM(t.shape, t.dtype) for t in operands],
                   jax.ShapeDtypeStruct((SUBLANES, LANES), F32)),
        in_specs=[HBM_SPEC] * (2 * n_arr),
        out_specs=(SEM_SPEC, SEM_SPEC, *[HBM_SPEC] * (2 * n_arr), pl.BlockSpec(memory_space=pltpu.VMEM)),
        input_output_aliases={i: 2 + i for i in range(2 * n_arr)},
        compiler_params=pltpu.CompilerParams(has_side_effects=DATAFLOW_EFFECT),
    )(*operands)
    return outs[0], outs[1], list(outs[2:2 + n_arr]), list(outs[2 + n_arr:2 + 2 * n_arr]), outs[-1], whole_src


def _push_end(handle, after, name):
    send_sems, recv_sems, srcs, lands, _, whole_src = handle
    n_arr = len(srcs)
    shapes = [(t.shape[0] // N_DEV, t.shape[1]) for t in lands]

    def body(*refs):
        src_refs, land_refs = refs[:n_arr], refs[n_arr:2 * n_arr]
        send_sems_ref, recv_sems_ref = refs[2 * n_arr], refs[2 * n_arr + 1]
        for outgoing, incoming in _push_copies(src_refs, land_refs, send_sems_ref, recv_sems_ref, shapes, whole_src):
            outgoing.wait_send()
            incoming.wait_recv()
        for own in _own_copies(src_refs, land_refs, recv_sems_ref, shapes, whole_src):
            own.wait()

    outs = pl.pallas_call(
        body, name=name,
        out_shape=tuple(pltpu.HBM(t.shape, t.dtype) for t in srcs + lands),
        in_specs=[HBM_SPEC] * (2 * n_arr) + [SEM_SPEC, SEM_SPEC, pl.BlockSpec(memory_space=pl.ANY)],
        out_specs=tuple([HBM_SPEC] * (2 * n_arr)),
        input_output_aliases={i: i for i in range(2 * n_arr)},
        compiler_params=pltpu.CompilerParams(has_side_effects=DATAFLOW_EFFECT),
    )(*srcs, *lands, send_sems, recv_sems, after)
    return list(outs[:n_arr]), list(outs[n_arr:])


def _sum_devices(r, name, rows_per_step=ADAM_ROWS):
    _, m, n = r.shape
    tm = _tile(m, rows_per_step, 8)

    def body(r_ref, o_ref):
        acc = r_ref[0].astype(F32)
        for s in range(1, N_DEV):
            acc = acc + r_ref[s].astype(F32)
        o_ref[...] = acc

    return pl.pallas_call(
        body,
        name=name,
        grid=(m // tm,),
        out_shape=jax.ShapeDtypeStruct((m, n), F32),
        in_specs=[pl.BlockSpec((N_DEV, tm, n), lambda i: (0, i, 0))],
        out_specs=pl.BlockSpec((tm, n), lambda i: (i, 0)),
        compiler_params=pltpu.CompilerParams(dimension_semantics=("parallel",)),
    )(r)


def _get(ref):
    return ref[0] if len(ref.shape) == 3 else ref[...]


def _put(ref, val):
    if len(ref.shape) == 3:
        ref[0] = val
    else:
        ref[...] = val


def _norm_mod(hv, g, sc, sh):
    r = lax.rsqrt(jnp.mean(hv * hv, axis=-1, keepdims=True) + EPS)
    return (hv * r) * g * (1.0 + sc) + sh


def _mm_call(name, a, b, a_spec, b_spec, out_sds, o_spec, grid, dims, acc_shape, bias=None,
             res=None, gate=None, raw_out=False, vec_spec=None, norm=None):
    nk = grid[2]
    operands, in_specs = [a, b], [a_spec, b_spec]
    if bias is not None:
        operands.append(bias)
        in_specs.append(vec_spec)
    if res is not None:
        operands += [res, gate]
        in_specs += [o_spec, vec_spec]
    if norm is not None:
        assert grid[1] == 1
        operands += list(norm)
        in_specs += [vec_spec] * 3
    out_shape, out_specs = [out_sds], [o_spec]
    if raw_out:
        out_shape.append(jax.ShapeDtypeStruct(out_sds.shape, BF16))
        out_specs.append(o_spec)
    if norm is not None:
        out_shape.append(jax.ShapeDtypeStruct(out_sds.shape, BF16))
        out_specs.append(o_spec)

    def body(*refs):
        it = iter(refs)
        a_ref, b_ref = next(it), next(it)
        bias_ref = next(it) if bias is not None else None
        res_ref, gate_ref = (next(it), next(it)) if res is not None else (None, None)
        norm_refs = (next(it), next(it), next(it)) if norm is not None else None
        o_ref = next(it)
        raw_ref = next(it) if raw_out else None
        xn_ref = next(it) if norm is not None else None
        acc = next(it) if nk > 1 else None
        k = pl.program_id(2)
        part = _dot(_get(a_ref).astype(BF16), _get(b_ref).astype(BF16), dims)

        def finish(y):
            if bias_ref is not None:
                y = y + bias_ref[...]
            if raw_ref is not None:
                raw_ref[...] = y.astype(BF16)
            if res_ref is not None:
                y = res_ref[...] + gate_ref[...] * y
            _put(o_ref, y.astype(out_sds.dtype))
            if xn_ref is not None:
                xn_ref[...] = _norm_mod(y, *[r[...] for r in norm_refs]).astype(BF16)

        if nk == 1:
            finish(part)
        else:
            @pl.when(k == 0)
            def _():
                acc[...] = part

            @pl.when(k > 0)
            def _():
                acc[...] += part

            @pl.when(k == nk - 1)
            def _():
                finish(acc[...])

    outs = pl.pallas_call(
        body,
        name=name,
        grid=grid,
        out_shape=out_shape,
        in_specs=in_specs,
        out_specs=out_specs,
        scratch_shapes=[pltpu.VMEM(acc_shape, F32)] if nk > 1 else [],
        compiler_params=pltpu.CompilerParams(dimension_semantics=("parallel", "parallel", "arbitrary")),
    )(*operands)
    return outs if len(outs) > 1 else outs[0]


def _mm(a, b, mode, name, out_dtype=F32, bias=None, res=None, gate=None, raw_out=False,
        tm=1024, tn=1024, tk=1024, a_row_off=0, norm=None):
    if mode == "nn":
        K, N = b.shape
        M = a.shape[0] - a_row_off
    elif mode == "nt":
        N, K = b.shape
        M = a.shape[0] - a_row_off
    else:
        (K, M), N = a.shape, b.shape[1]
    tm, tn, tk = _tile(M, tm, LANES if mode == "tn" else 2 * SUBLANES), _tile(N, tn), _tile(K, tk)
    off = a_row_off // tm
    dims = {"nn": NN, "nt": NT, "tn": TN}[mode]
    a_spec = (pl.BlockSpec((tk, tm), lambda i, j, k: (k, i)) if mode == "tn"
              else pl.BlockSpec((tm, tk), lambda i, j, k: (i + off, k)))
    b_spec = (pl.BlockSpec((tn, tk), lambda i, j, k: (j, k)) if mode == "nt"
              else pl.BlockSpec((tk, tn), lambda i, j, k: (k, j)))
    return _mm_call(name, a, b, a_spec, b_spec, jax.ShapeDtypeStruct((M, N), out_dtype),
                    pl.BlockSpec((tm, tn), lambda i, j, k: (i, j)), (M // tm, N // tn, K // tk), dims,
                    (tm, tn), bias, res, gate, raw_out, pl.BlockSpec((1, tn), lambda i, j, k: (0, j)), norm)


def _mm_sum_shards(a3, b3, mode, name, out_dtype=F32, res=None, gate=None, raw_out=False, tm=512, norm=None):
    S, M, kk = a3.shape
    N = b3.shape[2] if mode == "nn" else b3.shape[1]
    tm = _tile(M, tm)
    dims = NN if mode == "nn" else NT
    has_res = res is not None

    def body(*refs):
        it = iter(refs)
        a_ref, b_ref = next(it), next(it)
        res_ref, gate_ref = (next(it), next(it)) if has_res else (None, None)
        norm_refs = (next(it), next(it), next(it)) if norm is not None else None
        o_ref = next(it)
        raw_ref = next(it) if raw_out else None
        xn_ref = next(it) if norm is not None else None
        y = _dot(a_ref[0], b_ref[0], dims)
        for s in range(1, S):
            y = y + _dot(a_ref[s], b_ref[s], dims)
        if raw_ref is not None:
            raw_ref[...] = y.astype(BF16)
        if has_res:
            y = res_ref[...] + gate_ref[...] * y
        o_ref[...] = y.astype(out_dtype)
        if xn_ref is not None:
            xn_ref[...] = _norm_mod(y, *[r[...] for r in norm_refs]).astype(BF16)

    tile = pl.BlockSpec((tm, N), lambda i: (i, 0))
    operands = [a3, b3] + ([res, gate] if has_res else []) + (list(norm) if norm is not None else [])
    in_specs = [pl.BlockSpec((S, tm, kk), lambda i: (0, i, 0)), pl.BlockSpec(b3.shape, lambda i: (0, 0, 0))]
    in_specs += [tile, _vec_spec(N)] if has_res else []
    in_specs += [_vec_spec(N)] * 3 if norm is not None else []
    out_shape = [jax.ShapeDtypeStruct((M, N), out_dtype)] + ([jax.ShapeDtypeStruct((M, N), BF16)] if raw_out else [])
    out_shape += [jax.ShapeDtypeStruct((M, N), BF16)] if norm is not None else []
    outs = pl.pallas_call(
        body, name=name, grid=(M // tm,),
        out_shape=out_shape, in_specs=in_specs, out_specs=[tile] * len(out_shape),
        compiler_params=pltpu.CompilerParams(dimension_semantics=("parallel",)),
    )(*operands)
    return outs if len(outs) > 1 else outs[0]


def _mm_tn_shard_rows(a3, b, name, out_dtype, tn=1024, tk=4096):
    S, T, m = a3.shape
    N = b.shape[1]
    tn, tk = _tile(N, tn), _tile(T, tk)
    return _mm_call(name, a3, b, pl.BlockSpec((1, tk, m), lambda i, j, k: (i, k, 0)),
                    pl.BlockSpec((tk, tn), lambda i, j, k: (k, j)), jax.ShapeDtypeStruct((S, m, N), out_dtype),
                    pl.BlockSpec((1, m, tn), lambda i, j, k: (i, 0, j)), (S, N // tn, T // tk), TN, (m, tn))


def _row_spec(tm, width, off=0):
    return pl.BlockSpec((tm, width), lambda i: (i + off, 0))


def _vec_spec(width):
    return pl.BlockSpec((1, width), lambda i: (0, 0))


def _norm_mod_fwd_cat(hc, h, g, csc, csh, sc, sh, name):
    (C, Dm), T = hc.shape, h.shape[0]
    tm = _tile(math.gcd(C, T), ROW_BLOCK, 8)
    off = C // tm

    def body(hc_ref, h_ref, g_ref, csc_ref, csh_ref, sc_ref, sh_ref, o_ref):
        is_ctx = pl.program_id(0) < off
        hv = jnp.where(is_ctx, hc_ref[...], h_ref[...])
        scv = jnp.where(is_ctx, csc_ref[...], sc_ref[...])
        shv = jnp.where(is_ctx, csh_ref[...], sh_ref[...])
        r = lax.rsqrt(jnp.mean(hv * hv, axis=-1, keepdims=True) + EPS)
        o_ref[...] = ((hv * r) * g_ref[...] * (1.0 + scv) + shv).astype(BF16)

    return pl.pallas_call(
        body, name=name, grid=((C + T) // tm,),
        out_shape=jax.ShapeDtypeStruct((C + T, Dm), BF16),
        in_specs=[pl.BlockSpec((tm, Dm), lambda i: (jnp.minimum(i, off - 1), 0)),
                  pl.BlockSpec((tm, Dm), lambda i: (jnp.maximum(i - off, 0), 0))] + [_vec_spec(Dm)] * 5,
        out_specs=_row_spec(tm, Dm),
        compiler_params=pltpu.CompilerParams(dimension_semantics=("parallel",)),
    )(hc, h, g, csc, csh, sc, sh)


def _gate_grads(dh, y_ref, gt_ref, dy_ref, dgt_ref, dsum_ref):
    dy = dh * gt_ref[...]
    dgt_ref[...] += jnp.sum(dh * y_ref[...].astype(F32), axis=0, keepdims=True)
    dsum_ref[...] += jnp.sum(dy, axis=0, keepdims=True)
    dy_ref[...] = dy.astype(BF16)


def _norm_mod_bwd(h, g, sc, dxm, dres, name, dxm_row_off=0, gate=None):
    R, Dm = h.shape
    tm = _tile(math.gcd(R, dxm_row_off) if dxm_row_off else R, ROW_BLOCK, 8)
    off = dxm_row_off // tm
    has_res = dres is not None
    has_gate = gate is not None

    def body(*refs):
        it = iter(refs)
        h_ref, g_ref, sc_ref, dx_ref = next(it), next(it), next(it), next(it)
        dres_ref = next(it) if has_res else None
        y_ref, gt_ref = (next(it), next(it)) if has_gate else (None, None)
        dh_ref, da_ref, dsh_ref = next(it), next(it), next(it)
        gate_out = (next(it), next(it), next(it)) if has_gate else ()
        i = pl.program_id(0)

        @pl.when(i == 0)
        def _():
            for ref in (da_ref, dsh_ref) + gate_out[1:]:
                ref[...] = jnp.zeros_like(ref)

        hv = h_ref[...]
        dx = dx_ref[...].astype(F32)
        r = lax.rsqrt(jnp.mean(hv * hv, axis=-1, keepdims=True) + EPS)
        n = hv * r
        da_ref[...] += jnp.sum(dx * n, axis=0, keepdims=True)
        dsh_ref[...] += jnp.sum(dx, axis=0, keepdims=True)
        dn = dx * (g_ref[...] * (1.0 + sc_ref[...]))
        dh = r * (dn - n * jnp.mean(dn * n, axis=-1, keepdims=True))
        if has_res:
            dh = dh + dres_ref[...]
        dh_ref[...] = dh
        if has_gate:
            _gate_grads(dh, y_ref, gt_ref, *gate_out)

    operands = [h, g, sc, dxm] + ([dres] if has_res else []) + (list(gate) if has_gate else [])
    in_specs = [_row_spec(tm, Dm), _vec_spec(Dm), _vec_spec(Dm), _row_spec(tm, Dm, off)]
    in_specs += [_row_spec(tm, Dm)] if has_res else []
    in_specs += [_row_spec(tm, Dm), _vec_spec(Dm)] if has_gate else []
    vec = jax.ShapeDtypeStruct((1, Dm), F32)
    out_shape = [jax.ShapeDtypeStruct((R, Dm), F32), vec, vec]
    out_specs = [_row_spec(tm, Dm), _vec_spec(Dm), _vec_spec(Dm)]
    if has_gate:
        out_shape += [jax.ShapeDtypeStruct((R, Dm), BF16), vec, vec]
        out_specs += [_row_spec(tm, Dm), _vec_spec(Dm), _vec_spec(Dm)]
    return pl.pallas_call(
        body, name=name, grid=(R // tm,),
        out_shape=out_shape, in_specs=in_specs, out_specs=out_specs,
        compiler_params=pltpu.CompilerParams(dimension_semantics=("arbitrary",)),
    )(*operands)


def _ffn_in_swiglu(xf, w3, name, tm=1024):
    T, K = xf.shape
    S, n, _ = w3.shape
    half = S // 2
    tm = _tile(T, tm)

    def body(a_ref, wg_ref, wu_ref, gu_ref, act_ref):
        a = a_ref[...]
        g = _dot(a, wg_ref[0], NT)
        u = _dot(a, wu_ref[0], NT)
        gu_ref[0, 0] = g.astype(BF16)
        gu_ref[1, 0] = u.astype(BF16)
        act_ref[0] = (_silu(g) * u).astype(BF16)

    return pl.pallas_call(
        body, name=name, grid=(T // tm, half),
        out_shape=[jax.ShapeDtypeStruct((2, half, T, n), BF16), jax.ShapeDtypeStruct((half, T, n), BF16)],
        in_specs=[pl.BlockSpec((tm, K), lambda i, j: (i, 0)),
                  pl.BlockSpec((1, n, K), lambda i, j: (j, 0, 0)),
                  pl.BlockSpec((1, n, K), lambda i, j: (j + half, 0, 0))],
        out_specs=[pl.BlockSpec((2, 1, tm, n), lambda i, j: (0, j, i, 0)),
                   pl.BlockSpec((1, tm, n), lambda i, j: (j, i, 0))],
        compiler_params=pltpu.CompilerParams(dimension_semantics=("parallel", "parallel")),
    )(xf, w3, w3)


def _ffn_out_dx_swiglu(df, wo, gu, name, tm=1024):
    T, Dm = df.shape
    half, n, _ = wo.shape
    tm = _tile(T, tm)

    def body(df_ref, w_ref, gu_ref, o_ref):
        da = _dot(df_ref[...], w_ref[0], NT)
        g = gu_ref[0, 0].astype(F32)
        u = gu_ref[1, 0].astype(F32)
        s = _sigmoid(g)
        o_ref[0, 0] = (da * u * (s * (1.0 + g * (1.0 - s)))).astype(BF16)
        o_ref[1, 0] = (da * (g * s)).astype(BF16)

    gu_spec = pl.BlockSpec((2, 1, tm, n), lambda i, j: (0, j, i, 0))
    return pl.pallas_call(
        body, name=name, grid=(T // tm, half),
        out_shape=jax.ShapeDtypeStruct(gu.shape, BF16),
        in_specs=[pl.BlockSpec((tm, Dm), lambda i, j: (i, 0)),
                  pl.BlockSpec((1, n, Dm), lambda i, j: (j, 0, 0)), gu_spec],
        out_specs=gu_spec,
        compiler_params=pltpu.CompilerParams(dimension_semantics=("parallel", "parallel")),
    )(df, wo, gu)


def _pw1_glu(xm, w, bias, name, tm=512):
    T, K = xm.shape
    N = w.shape[1]
    tm = _tile(T, tm, 2 * SUBLANES)

    def body(a_ref, w_ref, b_ref, ag_ref, hg_ref):
        ag = (_dot(a_ref[...], w_ref[...], NN) + b_ref[...]).astype(BF16)
        ag_ref[...] = ag
        hg_ref[...] = ag[:, :N // 2].astype(F32) * _sigmoid(ag[:, N // 2:].astype(F32))

    return pl.pallas_call(
        body, name=name, grid=(T // tm,),
        out_shape=[jax.ShapeDtypeStruct((T, N), BF16), jax.ShapeDtypeStruct((T, N // 2), F32)],
        in_specs=[_row_spec(tm, K), pl.BlockSpec((K, N), lambda i: (0, 0)), _vec_spec(N)],
        out_specs=[_row_spec(tm, N), _row_spec(tm, N // 2)],
        compiler_params=pltpu.CompilerParams(dimension_semantics=("parallel",)),
    )(xm, w, bias)


def _glu_bwd_pw1_dx(ag, dhg, w, name, tm=512):
    T, N = ag.shape
    Dm = N // 2
    tm = _tile(T, tm, 2 * SUBLANES)

    def body(ag_ref, dh_ref, w_ref, dag_ref, s_ref, dx_ref):
        i = pl.program_id(0)

        @pl.when(i == 0)
        def _():
            s_ref[...] = jnp.zeros_like(s_ref)

        a = ag_ref[:, :Dm].astype(F32)
        s = _sigmoid(ag_ref[:, Dm:].astype(F32))
        dh = dh_ref[...]
        da = dh * s
        dg = dh * a * s * (1.0 - s)
        dag_ref[:, :Dm] = da.astype(BF16)
        dag_ref[:, Dm:] = dg.astype(BF16)
        s_ref[:, :Dm] += jnp.sum(da, axis=0, keepdims=True)
        s_ref[:, Dm:] += jnp.sum(dg, axis=0, keepdims=True)
        dx_ref[...] = _dot(dag_ref[...], w_ref[...], NT).astype(BF16)

    return pl.pallas_call(
        body, name=name, grid=(T // tm,),
        out_shape=[jax.ShapeDtypeStruct((T, N), BF16), jax.ShapeDtypeStruct((1, N), F32),
                   jax.ShapeDtypeStruct((T, Dm), BF16)],
        in_specs=[_row_spec(tm, N), _row_spec(tm, Dm), pl.BlockSpec(w.shape, lambda i: (0, 0))],
        out_specs=[_row_spec(tm, N), _vec_spec(N), _row_spec(tm, Dm)],
        compiler_params=pltpu.CompilerParams(dimension_semantics=("arbitrary",)),
    )(ag, dhg, w)


def _halo_specs(tm, nblk, width):
    per = tm // CONV_HALO
    prev = pl.BlockSpec((CONV_HALO, width), lambda i: (jnp.maximum(i * per - 1, 0), 0))
    nxt = pl.BlockSpec((CONV_HALO, width), lambda i: (jnp.minimum((i + 1) * per, nblk * per - 1), 0))
    return prev, nxt


def _fill_halo(scr, prev_ref, cur_ref, next_ref, i, nblk, tm):
    scr[0:CONV_HALO, :] = jnp.where(i > 0, prev_ref[...], 0.0)
    scr[CONV_HALO:CONV_HALO + tm, :] = cur_ref[...]
    scr[CONV_HALO + tm:2 * CONV_HALO + tm, :] = jnp.where(i < nblk - 1, next_ref[...], 0.0)


CONV_ROWS = 128


CONV_REACH = (CONV_WIDTH // SUBLANES) * SUBLANES


def _windows(scr, stage, cols, tm):
    for r in range(SUBLANES):
        if r:
            stage[r] = scr[pl.ds(r, tm + CONV_REACH), cols]
        for a in range(CONV_REACH // SUBLANES + 1):
            off = SUBLANES * a + r
            if 1 <= off <= CONV_WIDTH:
                yield off, (stage[r, SUBLANES * a:SUBLANES * a + tm, :] if r
                            else scr[SUBLANES * a:SUBLANES * a + tm, cols])


def _conv_fwd(hg, w_dw, b_dw, name):
    R, Dm = hg.shape
    tm = _tile(R, CONV_ROWS, CONV_HALO)
    nblk = R // tm
    prev_spec, next_spec = _halo_specs(tm, nblk, Dm)

    def body(prev_ref, cur_ref, next_ref, w_ref, bdw_ref, hd_ref, scr, stage):
        _fill_halo(scr, prev_ref, cur_ref, next_ref, pl.program_id(0), nblk, tm)
        for cb in range(Dm // LANES):
            cols = slice(cb * LANES, (cb + 1) * LANES)
            acc = jnp.zeros((tm, LANES), F32) + bdw_ref[:, cols]
            for off, win in _windows(scr, stage, cols, tm):
                acc = acc + w_ref[off - 1:off, cols] * win
            hd_ref[:, cols] = acc

    return pl.pallas_call(
        body, name=name, grid=(nblk,),
        out_shape=jax.ShapeDtypeStruct((R, Dm), F32),
        in_specs=[prev_spec, _row_spec(tm, Dm), next_spec,
                  pl.BlockSpec((CONV_WIDTH, Dm), lambda i: (0, 0)), _vec_spec(Dm)],
        out_specs=_row_spec(tm, Dm),
        scratch_shapes=[pltpu.VMEM((tm + 2 * CONV_HALO, Dm), F32),
                        pltpu.VMEM((SUBLANES, tm + CONV_REACH, LANES), F32)],
        compiler_params=pltpu.CompilerParams(dimension_semantics=("parallel",)),
    )(hg, hg, hg, w_dw, b_dw)


def _ln_silu_fwd(hd, ln_g, ln_b, name):
    R, Dm = hd.shape
    tm = _tile(R, ROW_BLOCK, 8)

    def body(hd_ref, g_ref, b_ref, hs_ref):
        hd = hd_ref[...]
        xc = hd - jnp.mean(hd, axis=-1, keepdims=True)
        rs = lax.rsqrt(jnp.mean(xc * xc, axis=-1, keepdims=True) + EPS)
        hs_ref[...] = _silu(xc * rs * g_ref[...] + b_ref[...]).astype(BF16)

    return pl.pallas_call(
        body, name=name, grid=(R // tm,),
        out_shape=jax.ShapeDtypeStruct((R, Dm), BF16),
        in_specs=[_row_spec(tm, Dm), _vec_spec(Dm), _vec_spec(Dm)],
        out_specs=_row_spec(tm, Dm),
        compiler_params=pltpu.CompilerParams(dimension_semantics=("parallel",)),
    )(hd, ln_g, ln_b)


def _ln_silu_bwd(dhs, hd, ln_g, ln_b, name):
    R, Dm = hd.shape
    tm = _tile(R, ROW_BLOCK, 8)

    def body(dhs_ref, hd_ref, g_ref, b_ref, dhd_ref, dg_ref, db_ref, dsum_ref):
        i = pl.program_id(0)

        @pl.when(i == 0)
        def _():
            dg_ref[...] = jnp.zeros_like(dg_ref)
            db_ref[...] = jnp.zeros_like(db_ref)
            dsum_ref[...] = jnp.zeros_like(dsum_ref)

        hd = hd_ref[...]
        mu = jnp.mean(hd, axis=-1, keepdims=True)
        xc = hd - mu
        rs = lax.rsqrt(jnp.mean(xc * xc, axis=-1, keepdims=True) + EPS)
        z = xc * rs
        hl = z * g_ref[...] + b_ref[...]
        dhl = dhs_ref[...].astype(F32) * _dsilu(hl)
        dg_ref[...] += jnp.sum(dhl * z, axis=0, keepdims=True)
        db_ref[...] += jnp.sum(dhl, axis=0, keepdims=True)
        dz = dhl * g_ref[...]
        dhd = rs * (dz - jnp.mean(dz, axis=-1, keepdims=True) - z * jnp.mean(dz * z, axis=-1, keepdims=True))
        dsum_ref[...] += jnp.sum(dhd, axis=0, keepdims=True)
        dhd_ref[...] = dhd

    return pl.pallas_call(
        body, name=name, grid=(R // tm,),
        out_shape=[jax.ShapeDtypeStruct((R, Dm), F32)] + [jax.ShapeDtypeStruct((1, Dm), F32)] * 3,
        in_specs=[_row_spec(tm, Dm), _row_spec(tm, Dm), _vec_spec(Dm), _vec_spec(Dm)],
        out_specs=[_row_spec(tm, Dm), _vec_spec(Dm), _vec_spec(Dm), _vec_spec(Dm)],
        compiler_params=pltpu.CompilerParams(dimension_semantics=("arbitrary",)),
    )(dhs, hd, ln_g, ln_b)


def _conv_bwd(dhd, hg, w_dw, name):
    R, Dm = hg.shape
    tm = _tile(R, CONV_ROWS, CONV_HALO)
    nblk = R // tm
    prev_spec, next_spec = _halo_specs(tm, nblk, Dm)

    def body(dprev, dcur, dnext, gprev, gcur, gnext, w_ref, dhg_ref, dw_ref, dscr, gscr, dwp, stage):
        i = pl.program_id(0)

        @pl.when(i == 0)
        def _():
            dwp[...] = jnp.zeros_like(dwp)

        _fill_halo(dscr, dprev, dcur, dnext, i, nblk, tm)
        _fill_halo(gscr, gprev, gcur, gnext, i, nblk, tm)
        for cb in range(Dm // LANES):
            cols = slice(cb * LANES, (cb + 1) * LANES)
            acc = jnp.zeros((tm, LANES), F32)
            for off, win in _windows(dscr, stage, cols, tm):
                j = CONV_WIDTH - off
                acc = acc + w_ref[j:j + 1, cols] * win
            dhg_ref[:, cols] = acc
            d_here = dcur[:, cols]
            for off, win in _windows(gscr, stage, cols, tm):
                j = off - 1
                prod = d_here * win
                part = prod[0:SUBLANES]
                for k in range(1, tm // SUBLANES):
                    part = part + prod[k * SUBLANES:(k + 1) * SUBLANES]
                dwp[j * SUBLANES:(j + 1) * SUBLANES, cols] += part

        @pl.when(i == nblk - 1)
        def _():
            for j in range(CONV_WIDTH):
                dw_ref[j:j + 1, :] = jnp.sum(dwp[j * SUBLANES:(j + 1) * SUBLANES, :], axis=0, keepdims=True)

    return pl.pallas_call(
        body, name=name, grid=(nblk,),
        out_shape=[jax.ShapeDtypeStruct((R, Dm), F32), jax.ShapeDtypeStruct((CONV_WIDTH, Dm), F32)],
        in_specs=[prev_spec, _row_spec(tm, Dm), next_spec, prev_spec, _row_spec(tm, Dm), next_spec,
                  pl.BlockSpec((CONV_WIDTH, Dm), lambda i: (0, 0))],
        out_specs=[_row_spec(tm, Dm), pl.BlockSpec((CONV_WIDTH, Dm), lambda i: (0, 0))],
        scratch_shapes=[pltpu.VMEM((tm + 2 * CONV_HALO, Dm), F32)] * 2
        + [pltpu.VMEM((CONV_WIDTH * SUBLANES, Dm), F32), pltpu.VMEM((SUBLANES, tm + CONV_REACH, LANES), F32)],
        compiler_params=pltpu.CompilerParams(dimension_semantics=("arbitrary",)),
    )(dhd, dhd, dhd, hg, hg, hg, w_dw)


def _swap16(y, lane):
    return jnp.where((lane & 16) == 0, pltpu.roll(y, LANES - 16, 1), pltpu.roll(y, 16, 1))


def _head_mean(v, bd):
    hi, lo = _split_bf16(v)
    return (_dot(hi, bd, NN) + _dot(lo, bd, NN)) * (1.0 / HEAD_DIM)


Q_COLS = (0, ATTN_WIDTH)
K_COLS = (ATTN_WIDTH, ATTN_WIDTH + HEAD_DIM * 2)
V_COLS = (K_COLS[1], K_COLS[1] + HEAD_DIM * 2)
SU_COLS = (V_COLS[1], V_COLS[1] + SG_WIDTH)
SV_COLS = (SU_COLS[1], SU_COLS[1] + SG_WIDTH)


def _mix_prep_fwd(p, ctx_rows, cos, sin, qg, kg, bd, w_sp, b_spt, name):
    TT = p.shape[0]
    off = ctx_rows // CHUNK
    q_scale = HEAD_DIM ** -0.5

    def body(p_ref, cos_ref, sin_ref, qg_ref, kg_ref, bd_ref, w_ref, b_ref,
             q_ref, kp_ref, vp_ref, kt_ref, sg_ref):
        lane = lax.broadcasted_iota(jnp.int32, (CHUNK, LANES), 1)
        low = lane < HEAD_DIM
        cs, sn, bdv = cos_ref[...], sin_ref[...], bd_ref[...]

        def norm_rope(xv, gain):
            r = lax.rsqrt(_head_mean(xv * xv, bdv) + EPS)
            yv = xv * r * gain
            return yv * cs + _swap16(yv, lane) * sn

        def pad_heads(ref, t):
            tr = pltpu.roll(t, HEAD_DIM, 1)
            ref[0, 0] = jnp.where(low, t, 0.0).astype(BF16)
            ref[0, 1] = jnp.where(low, 0.0, tr).astype(BF16)
            ref[1, 0] = jnp.where(low, tr, 0.0).astype(BF16)
            ref[1, 1] = jnp.where(low, 0.0, t).astype(BF16)

        for a in range(ATTN_WIDTH // LANES):
            xv = p_ref[:, a * LANES:(a + 1) * LANES]
            q_ref[:, a * LANES:(a + 1) * LANES] = (norm_rope(xv, qg_ref[...]) * q_scale).astype(BF16)
        kh = norm_rope(p_ref[:, K_COLS[0]:K_COLS[1]], kg_ref[...])
        pad_heads(kp_ref, kh)
        pad_heads(vp_ref, p_ref[:, V_COLS[0]:V_COLS[1]])
        kht = kh.T
        kt_ref[0] = kht[:HEAD_DIM].astype(BF16)
        kt_ref[1] = kht[HEAD_DIM:].astype(BF16)
        for g in range(N_SG_GROUPS):
            u = _gelu(p_ref[:, SU_COLS[0] + g * LANES:SU_COLS[0] + (g + 1) * LANES])
            vg = _gelu(p_ref[:, SV_COLS[0] + g * LANES:SV_COLS[0] + (g + 1) * LANES])
            xc = vg - jnp.mean(vg, axis=-1, keepdims=True)
            vn = xc * lax.rsqrt(jnp.mean(xc * xc, axis=-1, keepdims=True) + EPS)
            mixed = _dot(w_ref[g].astype(BF16), vn.astype(BF16), NN) + b_ref[:, g:g + 1]
            sg_ref[:, g * LANES:(g + 1) * LANES] = (u * mixed).astype(BF16)

    def row(width):
        return pl.BlockSpec((CHUNK, width), lambda i: (i, 0))

    def whole(shape):
        return pl.BlockSpec(shape, lambda i: (0,) * len(shape))

    pad_spec = pl.BlockSpec((2, 2, CHUNK, LANES), lambda i: (0, 0, i, 0))
    return pl.pallas_call(
        body, name=name, grid=(TT // CHUNK,),
        out_shape=[jax.ShapeDtypeStruct((TT, ATTN_WIDTH), BF16),
                   jax.ShapeDtypeStruct((2, 2, TT, LANES), BF16), jax.ShapeDtypeStruct((2, 2, TT, LANES), BF16),
                   jax.ShapeDtypeStruct((2, HEAD_DIM, TT), BF16),
                   jax.ShapeDtypeStruct((TT - ctx_rows, ATTN_WIDTH + SG_WIDTH), BF16)],
        in_specs=[row(IN_WIDTH), row(LANES), row(LANES), whole((1, LANES)), whole((1, LANES)),
                  whole((LANES, LANES)), whole((N_SG_GROUPS, CHUNK, CHUNK)), whole((CHUNK, N_SG_GROUPS))],
        out_specs=[row(ATTN_WIDTH), pad_spec, pad_spec,
                   pl.BlockSpec((2, HEAD_DIM, CHUNK), lambda i: (0, 0, i)),
                   pl.BlockSpec((CHUNK, SG_WIDTH), lambda i: (jnp.maximum(i - off, 0), 1))],
        compiler_params=pltpu.CompilerParams(dimension_semantics=("arbitrary",)),
    )(p, cos, sin, qg, kg, bd, w_sp, b_spt)


def _mix_prep_bwd(p, dq, f, dao, ctx_rows, cos, sin, qg, kg, bd, w_sp, w_spt, b_spt, name):
    TT = p.shape[0]
    off = ctx_rows // CHUNK
    q_scale = HEAD_DIM ** -0.5

    def body(p_ref, dq_ref, f_ref, dsg_ref, cos_ref, sin_ref, qg_ref, kg_ref, bd_ref, w_ref, wt_ref,
             b_ref, dp_ref, dqg_ref, dkg_ref, dw_ref, db_ref):
        i = pl.program_id(0)

        @pl.when(i == 0)
        def _():
            dqg_ref[...] = jnp.zeros_like(dqg_ref)
            dkg_ref[...] = jnp.zeros_like(dkg_ref)
            dw_ref[...] = jnp.zeros_like(dw_ref)
            db_ref[...] = jnp.zeros_like(db_ref)

        latent = (i >= off).astype(F32)
        lane = lax.broadcasted_iota(jnp.int32, (CHUNK, LANES), 1)
        low = lane < HEAD_DIM
        cs, sn, bdv = cos_ref[...], sin_ref[...], bd_ref[...]

        def fold(b0):
            return jnp.where(low, f_ref[0, b0] + pltpu.roll(f_ref[0, b0 + 1], HEAD_DIM, 1),
                             pltpu.roll(f_ref[1, b0], HEAD_DIM, 1) + f_ref[1, b0 + 1])

        def norm_rope_bwd(xv, dout, gain):
            r = lax.rsqrt(_head_mean(xv * xv, bdv) + EPS)
            n = xv * r
            dy = dout * cs + _swap16(dout * sn, lane)
            dn = dy * gain
            dx = r * (dn - n * _head_mean(dn * n, bdv))
            return dx, jnp.sum(dy * n, axis=0, keepdims=True)

        for a in range(ATTN_WIDTH // LANES):
            cols = slice(a * LANES, (a + 1) * LANES)
            dx, dg = norm_rope_bwd(p_ref[:, cols], dq_ref[:, cols] * (latent * q_scale), qg_ref[...])
            dp_ref[:, cols] = dx.astype(BF16)
            dqg_ref[...] += dg
        dx, dg = norm_rope_bwd(p_ref[:, K_COLS[0]:K_COLS[1]], fold(0), kg_ref[...])
        dp_ref[:, K_COLS[0]:K_COLS[1]] = dx.astype(BF16)
        dkg_ref[...] += dg
        dp_ref[:, V_COLS[0]:V_COLS[1]] = fold(2).astype(BF16)
        for g in range(N_SG_GROUPS):
            su = p_ref[:, SU_COLS[0] + g * LANES:SU_COLS[0] + (g + 1) * LANES]
            sv = p_ref[:, SV_COLS[0] + g * LANES:SV_COLS[0] + (g + 1) * LANES]
            (u, dgelu_su), (vg, dgelu_sv) = _gelu_and_grad(su), _gelu_and_grad(sv)
            xc = vg - jnp.mean(vg, axis=-1, keepdims=True)
            rs = lax.rsqrt(jnp.mean(xc * xc, axis=-1, keepdims=True) + EPS)
            vn = xc * rs
            vnb = vn.astype(BF16)
            mixed = _dot(w_ref[g].astype(BF16), vnb, NN) + b_ref[:, g:g + 1]
            dsg = dsg_ref[:, g * LANES:(g + 1) * LANES].astype(F32) * latent
            du = dsg * mixed
            dmix = dsg * u
            dmb = dmix.astype(BF16)
            db_ref[:, g:g + 1] += jnp.sum(dmix, axis=-1, keepdims=True)
            dw_ref[g] += _dot(dmb, vnb, NT)
            dvn = _dot(wt_ref[g].astype(BF16), dmb, NN)
            dvg = rs * (dvn - jnp.mean(dvn, axis=-1, keepdims=True)
                        - vn * jnp.mean(dvn * vn, axis=-1, keepdims=True))
            dp_ref[:, SU_COLS[0] + g * LANES:SU_COLS[0] + (g + 1) * LANES] = (du * dgelu_su).astype(BF16)
            dp_ref[:, SV_COLS[0] + g * LANES:SV_COLS[0] + (g + 1) * LANES] = (dvg * dgelu_sv).astype(BF16)

    def row(width):
        return pl.BlockSpec((CHUNK, width), lambda i: (i, 0))

    def latent_row(width, col_block):
        return pl.BlockSpec((CHUNK, width), lambda i: (jnp.maximum(i - off, 0), col_block))

    def whole(shape):
        return pl.BlockSpec(shape, lambda i: (0,) * len(shape))

    return pl.pallas_call(
        body, name=name, grid=(TT // CHUNK,),
        out_shape=[jax.ShapeDtypeStruct((TT, IN_WIDTH), BF16), jax.ShapeDtypeStruct((1, LANES), F32),
                   jax.ShapeDtypeStruct((1, LANES), F32),
                   jax.ShapeDtypeStruct((N_SG_GROUPS, CHUNK, CHUNK), F32),
                   jax.ShapeDtypeStruct((CHUNK, N_SG_GROUPS), F32)],
        in_specs=[row(IN_WIDTH), latent_row(ATTN_WIDTH, 0),
                  pl.BlockSpec((2, 4, CHUNK, LANES), lambda i: (0, 0, i, 0)),
                  latent_row(SG_WIDTH, 1), row(LANES), row(LANES), whole((1, LANES)), whole((1, LANES)),
                  whole((LANES, LANES)), whole((N_SG_GROUPS, CHUNK, CHUNK)),
                  whole((N_SG_GROUPS, CHUNK, CHUNK)), whole((CHUNK, N_SG_GROUPS))],
        out_specs=[row(IN_WIDTH), whole((1, LANES)), whole((1, LANES)),
                   whole((N_SG_GROUPS, CHUNK, CHUNK)), whole((CHUNK, N_SG_GROUPS))],
        compiler_params=pltpu.CompilerParams(dimension_semantics=("arbitrary",)),
    )(p, dq, f, dao, cos, sin, qg, kg, bd, w_sp, w_spt, b_spt)


def _attn_fwd(q, kpad, vpad, ao, ctx_rows, name, tq=256):
    TT = q.shape[0]
    T = TT - ctx_rows
    tq = _tile(T, tq)
    off = ctx_rows // tq
    group = 2 * LANES

    def body(q_ref, k_ref, v_ref, ao_in, o_ref, lse_ref):
        del ao_in
        lane = lax.broadcasted_iota(jnp.int32, (tq, LANES), 1)
        lse = jnp.zeros((tq, LANES), F32)
        for a in range(2):
            acc = jnp.zeros((tq, LANES), F32)
            qa = q_ref[:, a * LANES:(a + 1) * LANES]
            for b in range(2):
                s = _dot(qa, k_ref[0, b], NT)
                m = jnp.max(s, axis=-1, keepdims=True)
                e = jnp.exp(s - m)
                l = jnp.sum(e, axis=-1, keepdims=True)
                acc = acc + _dot(e.astype(BF16), v_ref[0, b], NN) * (1.0 / l)
                lse = jnp.where(lane == 2 * a + b, m + jnp.log(l), lse)
            o_ref[:, a * LANES:(a + 1) * LANES] = acc.astype(BF16)
        lse_ref[0] = lse

    kv_spec = pl.BlockSpec((1, 2, TT, LANES), lambda j, i: (j, 0, 0, 0))
    return pl.pallas_call(
        body, name=name, grid=(2, T // tq),
        out_shape=[jax.ShapeDtypeStruct(ao.shape, BF16), jax.ShapeDtypeStruct((2, T, LANES), F32)],
        in_specs=[pl.BlockSpec((tq, group), lambda j, i: (i + off, j)), kv_spec, kv_spec,
                  pl.BlockSpec(memory_space=pl.ANY)],
        out_specs=[pl.BlockSpec((tq, group), lambda j, i: (i, j)),
                   pl.BlockSpec((1, tq, LANES), lambda j, i: (j, i, 0))],
        input_output_aliases={3: 0},
        compiler_params=pltpu.CompilerParams(dimension_semantics=("parallel", "parallel")),
    )(q, kpad, vpad, ao)


def _attn_bwd(q, dao, ao, lse, kpad, vpad, kt, ctx_rows, name, tq=256):
    TT = q.shape[0]
    T = TT - ctx_rows
    tq = _tile(T, tq)
    off = ctx_rows // tq
    group = 2 * LANES

    def body(q_ref, do_ref, o_ref, lse_ref, k_ref, v_ref, kt_ref, dq_ref, f_ref):
        i = pl.program_id(1)

        @pl.when(i == 0)
        def _():
            f_ref[...] = jnp.zeros_like(f_ref)

        ktv = kt_ref[0]
        lse_t = lse_ref[0].T
        row = lax.broadcasted_iota(jnp.int32, (SUBLANES, LANES), 0)
        lane = lax.broadcasted_iota(jnp.int32, (SUBLANES, LANES), 1)
        half_ones = (jnp.where(lane < HEAD_DIM, 0, 1) == row).astype(BF16)
        for a in range(2):
            cols = slice(a * LANES, (a + 1) * LANES)
            qa = q_ref[:, cols]
            do32 = do_ref[:, cols].astype(F32)
            doa = do32.astype(BF16)
            hi, lo = _split_bf16(do32 * o_ref[:, cols].astype(F32))
            deltas = _dot(half_ones, hi, NT) + _dot(half_ones, lo, NT)
            halves = []
            for b in range(2):
                h = 2 * a + b
                st = _dot(k_ref[0, b], qa, NT)
                pt = jnp.exp(st - lse_t[h:h + 1, :])
                dpt = _dot(v_ref[0, b], doa, NT)
                dst = (pt * (dpt - deltas[b:b + 1, :])).astype(BF16)
                f_ref[0, b] += _dot(dst, qa, NN)
                f_ref[0, 2 + b] += _dot(pt.astype(BF16), doa, NN)
                halves.append(_dot(ktv, dst, NN))
            dq_ref[:, cols] = jnp.concatenate(halves, axis=0).T

    kv_spec = pl.BlockSpec((1, 2, TT, LANES), lambda j, i: (j, 0, 0, 0))
    out_cols = pl.BlockSpec((tq, group), lambda j, i: (i, j))
    return pl.pallas_call(
        body, name=name, grid=(2, T // tq),
        out_shape=[jax.ShapeDtypeStruct((T, ATTN_WIDTH), F32), jax.ShapeDtypeStruct((2, 4, TT, LANES), F32)],
        in_specs=[pl.BlockSpec((tq, group), lambda j, i: (i + off, j)), out_cols, out_cols,
                  pl.BlockSpec((1, tq, LANES), lambda j, i: (j, i, 0)),
                  kv_spec, kv_spec, pl.BlockSpec((1, HEAD_DIM, TT), lambda j, i: (j, 0, 0))],
        out_specs=[out_cols, pl.BlockSpec((1, 4, TT, LANES), lambda j, i: (j, 0, 0, 0))],
        compiler_params=pltpu.CompilerParams(dimension_semantics=("parallel", "arbitrary")),
    )(q, dao, ao, lse, kpad, vpad, kt)


def _final_fwd_bwd(h, g, target, y, gt, name):
    R, Dm = h.shape
    tm = _tile(R, ROW_BLOCK, 8)

    def body(h_ref, g_ref, t_ref, y_ref, gt_ref, dh_ref, loss_ref, dg_ref, dy_ref, dgt_ref, dsum_ref):
        i = pl.program_id(0)

        @pl.when(i == 0)
        def _():
            for ref in (loss_ref, dg_ref, dgt_ref, dsum_ref):
                ref[...] = jnp.zeros_like(ref)

        hv = h_ref[...]
        r = lax.rsqrt(jnp.mean(hv * hv, axis=-1, keepdims=True) + EPS)
        n = hv * r
        diff = n * g_ref[...] - t_ref[...]
        loss_ref[...] += jnp.sum(diff * diff)
        dout = diff * (1.0 / Dm)
        dg_ref[...] += jnp.sum(dout * n, axis=0, keepdims=True)
        dn = dout * g_ref[...]
        dh = r * (dn - n * jnp.mean(dn * n, axis=-1, keepdims=True))
        dh_ref[...] = dh
        _gate_grads(dh, y_ref, gt_ref, dy_ref, dgt_ref, dsum_ref)

    vec = jax.ShapeDtypeStruct((1, Dm), F32)
    return pl.pallas_call(
        body, name=name, grid=(R // tm,),
        out_shape=[jax.ShapeDtypeStruct((R, Dm), F32), jax.ShapeDtypeStruct((1, LANES), F32), vec,
                   jax.ShapeDtypeStruct((R, Dm), BF16), vec, vec],
        in_specs=[_row_spec(tm, Dm), _vec_spec(Dm), _row_spec(tm, Dm), _row_spec(tm, Dm), _vec_spec(Dm)],
        out_specs=[_row_spec(tm, Dm), _vec_spec(LANES), _vec_spec(Dm), _row_spec(tm, Dm), _vec_spec(Dm),
                   _vec_spec(Dm)],
        compiler_params=pltpu.CompilerParams(dimension_semantics=("arbitrary",)),
    )(h, g, target, y, gt)


MOD_ROWS = 16


def _mod_fwd(c_rows, w_mod, name):
    L, Dm, n = w_mod.shape

    def body(c_ref, w_ref, o_ref):
        o_ref[0] = _dot3(_silu(c_ref[...]), w_ref[0], NN)

    return pl.pallas_call(
        body, name=name, grid=(L,),
        out_shape=jax.ShapeDtypeStruct((L, MOD_ROWS, n), F32),
        in_specs=[pl.BlockSpec((MOD_ROWS, Dm), lambda l: (0, 0)), pl.BlockSpec((1, Dm, n), lambda l: (l, 0, 0))],
        out_specs=pl.BlockSpec((1, MOD_ROWS, n), lambda l: (l, 0, 0)),
        compiler_params=pltpu.CompilerParams(dimension_semantics=("parallel",)),
    )(c_rows, w_mod)


def _mod_bwd(c_rows_t, dmod, w_mod, name):
    L, Dm, n = w_mod.shape

    def body(ct_ref, d_ref, w_ref, gw_ref, ds_ref):
        dm = d_ref[0]
        gw_ref[0] = _dot3(_silu(ct_ref[...]), dm, NN)
        ds_ref[0] = _dot3(dm[:MOD_ROWS], w_ref[0], NT)

    return pl.pallas_call(
        body, name=name, grid=(L,),
        out_shape=[jax.ShapeDtypeStruct((L, Dm, n), F32), jax.ShapeDtypeStruct((L, MOD_ROWS, Dm), F32)],
        in_specs=[pl.BlockSpec((Dm, LANES), lambda l: (0, 0)), pl.BlockSpec((1, LANES, n), lambda l: (l, 0, 0)),
                  pl.BlockSpec((1, Dm, n), lambda l: (l, 0, 0))],
        out_specs=[pl.BlockSpec((1, Dm, n), lambda l: (l, 0, 0)),
                   pl.BlockSpec((1, MOD_ROWS, Dm), lambda l: (l, 0, 0))],
        compiler_params=pltpu.CompilerParams(dimension_semantics=("parallel",)),
    )(c_rows_t, dmod, w_mod)


def _adam_update(w, g, m, v):
    c1 = 1.0 - ADAM_B1 ** ADAM_STEP
    c2 = 1.0 - ADAM_B2 ** ADAM_STEP
    mn = ADAM_B1 * m + (1.0 - ADAM_B1) * g
    vn = ADAM_B2 * v + (1.0 - ADAM_B2) * (g * g)
    return -ADAM_LR * ((mn / c1) / (jnp.sqrt(vn / c2) + ADAM_EPS) + ADAM_WD * w), mn, vn


def _adamw(w, g, m, v, name):
    R, Cw = w.shape
    tm = _tile(R, ADAM_ROWS, 8)

    def body(w_ref, g_ref, m_ref, v_ref, d_ref, mo_ref, vo_ref):
        d_ref[...], mo_ref[...], vo_ref[...] = _adam_update(w_ref[...], g_ref[...], m_ref[...], v_ref[...])

    spec = pl.BlockSpec((tm, Cw), lambda i: (i, 0))
    return pl.pallas_call(
        body, name=name, grid=(R // tm,),
        out_shape=[jax.ShapeDtypeStruct((R, Cw), F32)] * 3,
        in_specs=[spec] * 4, out_specs=[spec] * 3,
        compiler_params=pltpu.CompilerParams(dimension_semantics=("parallel",)),
    )(w, g, m, v)


def _adamw_recv(w, m, v, recvs, name):
    L, R, n = w.shape
    tm = _tile(R, ADAM_ROWS, 8)
    nblk = R // tm
    parts = [r.reshape(N_DEV, R, n) for r in recvs]

    def body(*refs):
        w_ref, m_ref, v_ref = refs[:3]
        part_refs = refs[3:3 + L]
        g_ref, d_ref, mo_ref, vo_ref, gsum = refs[3 + L:]
        l = pl.program_id(0)
        for ll in range(L):
            @pl.when(l == ll)
            def _(ll=ll):
                acc = part_refs[ll][0].astype(F32)
                for s in range(1, N_DEV):
                    acc = acc + part_refs[ll][s].astype(F32)
                gsum[...] = acc
        g = gsum[...]
        g_ref[0] = g
        d_ref[0], mo_ref[0], vo_ref[0] = _adam_update(w_ref[0], g, m_ref[0], v_ref[0])

    def part_spec(ll):
        return pl.BlockSpec((N_DEV, tm, n), lambda l, i: (0, jnp.where(l == ll, i, jnp.where(l < ll, 0, nblk - 1)), 0))

    spec = pl.BlockSpec((1, tm, n), lambda l, i: (l, i, 0))
    return pl.pallas_call(
        body, name=name, grid=(L, nblk),
        out_shape=[jax.ShapeDtypeStruct((L, R, n), F32)] * 4,
        in_specs=[spec] * 3 + [part_spec(ll) for ll in range(L)], out_specs=[spec] * 4,
        scratch_shapes=[pltpu.VMEM((tm, n), F32)],
        compiler_params=pltpu.CompilerParams(dimension_semantics=("parallel", "parallel")),
    )(w, m, v, *parts)


def _pack(parts, row_mult=8):
    flat, offs, pos = [], [], 0
    for t in parts:
        t = t.reshape(-1).astype(F32)
        size = -(-t.shape[0] // LANES) * LANES
        flat.append(jnp.pad(t, (0, size - t.shape[0])))
        offs.append(pos)
        pos += size
    total = -(-pos // (LANES * row_mult)) * (LANES * row_mult)
    if total > pos:
        flat.append(jnp.zeros((total - pos,), F32))
    return jnp.concatenate(flat).reshape(-1, LANES), offs


def _take(buf, off, shape):
    size = math.prod(shape)
    return buf[..., off:off + size].reshape(buf.shape[:-1] + tuple(shape))


def _rope_tables(T, ctx_rows):
    pos = jnp.arange(T)
    row = (pos // GRID_W).astype(F32)
    col = (pos % GRID_W).astype(F32)
    half = HEAD_DIM // 4
    inv = ROPE_THETA ** (-jnp.arange(0, 2 * half, 2, dtype=F32) / (2 * half))
    ang_r, ang_c = row[:, None] * inv[None, :], col[:, None] * inv[None, :]
    cos = jnp.concatenate([jnp.cos(ang_r)] * 2 + [jnp.cos(ang_c)] * 2, axis=1)
    sin = jnp.concatenate([-jnp.sin(ang_r), jnp.sin(ang_r), -jnp.sin(ang_c), jnp.sin(ang_c)], axis=1)
    cos = jnp.concatenate([jnp.ones((ctx_rows, HEAD_DIM), F32), cos], axis=0)
    sin = jnp.concatenate([jnp.zeros((ctx_rows, HEAD_DIM), F32), sin], axis=0)
    return jnp.tile(cos, (1, 2)), jnp.tile(sin, (1, 2))


def kernel(x, c, ctx, c_ctx, w_mod, b_mod, g_mix, g_ffn, w_ffn_in, w_ffn_out, w_in, q_gain, k_gain, w_sp, b_sp, w_out, w_pw1, b_pw1, w_dw, b_dw, ln_g, ln_b, w_pw2, b_pw2, g_final, loss_target, m_c_ctx, m_w_mod, m_b_mod, m_g_mix, m_g_ffn, m_w_ffn_in, m_w_ffn_out, m_w_in, m_q_gain, m_k_gain, m_w_sp, m_b_sp, m_w_out, m_w_pw1, m_b_pw1, m_w_dw, m_b_dw, m_ln_g, m_ln_b, m_w_pw2, m_b_pw2, m_g_final, v_c_ctx, v_w_mod, v_b_mod, v_g_mix, v_g_ffn, v_w_ffn_in, v_w_ffn_out, v_w_in, v_q_gain, v_k_gain, v_w_sp, v_b_sp, v_w_out, v_w_pw1, v_b_pw1, v_w_dw, v_b_dw, v_ln_g, v_ln_b, v_w_pw2, v_b_pw2, v_g_final):
    weights = dict(c_ctx=c_ctx, w_mod=w_mod, b_mod=b_mod, g_mix=g_mix, g_ffn=g_ffn, w_ffn_in=w_ffn_in,
                   w_ffn_out=w_ffn_out, w_in=w_in, q_gain=q_gain, k_gain=k_gain, w_sp=w_sp, b_sp=b_sp,
                   w_out=w_out, w_pw1=w_pw1, b_pw1=b_pw1, w_dw=w_dw, b_dw=b_dw, ln_g=ln_g, ln_b=ln_b,
                   w_pw2=w_pw2, b_pw2=b_pw2, g_final=g_final)
    moments_m = dict(c_ctx=m_c_ctx, w_mod=m_w_mod, b_mod=m_b_mod, g_mix=m_g_mix, g_ffn=m_g_ffn,
                     w_ffn_in=m_w_ffn_in, w_ffn_out=m_w_ffn_out, w_in=m_w_in, q_gain=m_q_gain,
                     k_gain=m_k_gain, w_sp=m_w_sp, b_sp=m_b_sp, w_out=m_w_out, w_pw1=m_w_pw1,
                     b_pw1=m_b_pw1, w_dw=m_w_dw, b_dw=m_b_dw, ln_g=m_ln_g, ln_b=m_ln_b, w_pw2=m_w_pw2,
                     b_pw2=m_b_pw2, g_final=m_g_final)
    moments_v = dict(c_ctx=v_c_ctx, w_mod=v_w_mod, b_mod=v_b_mod, g_mix=v_g_mix, g_ffn=v_g_ffn,
                     w_ffn_in=v_w_ffn_in, w_ffn_out=v_w_ffn_out, w_in=v_w_in, q_gain=v_q_gain,
                     k_gain=v_k_gain, w_sp=v_w_sp, b_sp=v_b_sp, w_out=v_w_out, w_pw1=v_w_pw1,
                     b_pw1=v_b_pw1, w_dw=v_w_dw, b_dw=v_b_dw, ln_g=v_ln_g, ln_b=v_ln_b, w_pw2=v_w_pw2,
                     b_pw2=v_b_pw2, g_final=v_g_final)
    names = list(weights)

    T, C = x.shape[1], ctx.shape[1]
    Dm = D_MODEL
    me = 4 * lax.axis_index("x") + 2 * lax.axis_index("y") + lax.axis_index("c")
    h0 = x[0]
    ctx2 = ctx[0]
    target = loss_target[0]

    small_sharded = (("w_dw", w_dw[0]), ("b_pw1", b_pw1), ("b_dw", b_dw), ("ln_g", ln_g), ("ln_b", ln_b),
                     ("b_pw2", b_pw2))
    buf1, offs1 = _pack([c] + [t for _, t in small_sharded])
    w_in_t, m_w_in_t, v_w_in_t = (jnp.swapaxes(t, 1, 2) for t in (w_in, m_w_in, v_w_in))
    w_ffi_t, m_w_ffi_t, v_w_ffi_t = (jnp.swapaxes(t, 1, 2) for t in (w_ffn_in, m_w_ffn_in, v_w_ffn_in))
    got1, W_in_t = _all_gather([buf1, w_in_t[0].astype(BF16)], "gather_cond", False)
    got1 = got1.reshape(N_DEV, -1)
    c_all = _take(got1, offs1[0], (Dm,))
    full_small = {}
    for (nm, t), off in zip(small_sharded, offs1[1:]):
        seg = _take(got1, off, t.shape)
        full_small[nm] = jnp.moveaxis(seg, 0, -2).reshape(t.shape[:-1] + (N_DEV * t.shape[-1],))
    w_dw_f, b_pw1_f = full_small["w_dw"], full_small["b_pw1"]
    b_dw_f, ln_g_f, ln_b_f, b_pw2_f = (full_small[k] for k in ("b_dw", "ln_g", "ln_b", "b_pw2"))

    c_rows = jnp.concatenate([c_all, c_ctx[None, :], jnp.zeros((MOD_ROWS - N_DEV - 1, Dm), F32)], axis=0)
    mod_part = _mod_fwd(c_rows, w_mod, "mod_fwd")
    n_mod = w_mod.shape[2]
    got2 = _all_gather([mod_part.reshape(-1, LANES)], "gather_mod", True)[0]
    mod_all = got2.reshape(N_DEV, 2, MOD_ROWS, n_mod).transpose(1, 2, 0, 3).reshape(2, MOD_ROWS, N_DEV * n_mod)
    mod_all = mod_all + b_mod[:, None, :]
    my_mod = lax.dynamic_index_in_dim(mod_all, me, axis=1, keepdims=False)
    sh1, sc1, gt1, sh2, sc2, gt2 = ([my_mod[l:l + 1, k * Dm:(k + 1) * Dm] for l in range(2)] for k in range(6))
    csh1 = mod_all[0, N_DEV:N_DEV + 1, 0:Dm]
    csc1 = mod_all[0, N_DEV:N_DEV + 1, Dm:2 * Dm]

    behind = got2[0:1, 0:1] * 0.0
    gather_groups = [[w_out[0]], [w_ffi_t[0], w_ffn_out[0]], [w_pw1[0], w_pw2[0]], [w_ffi_t[1], w_ffn_out[1]]]
    gathers = [_push_begin([(t + behind).astype(BF16) for t in grp], True, f"gather_start{k}")
               for k, grp in enumerate(gather_groups)]
    started = sum(h[4][0:1, 0:1] for h in gathers)

    def gathered(k, after):
        return _push_end(gathers[k], after, f"gather_wait{k}")[1]

    def ffn_weights(k, after):
        wi, wo = gathered(k, after)
        return wi.reshape(N_DEV, FF_SHARD, Dm), wo.reshape(N_DEV // 2, FF_SHARD, Dm)

    def col_gathered(t, n):
        return t.reshape(N_DEV, Dm, n).transpose(1, 0, 2).reshape(Dm, N_DEV * n)

    W_ffi, W_ffo = [None, None], [None, None]

    g_mix_r = [g_mix[l:l + 1] for l in range(2)]
    g_ffn_r = [g_ffn[l:l + 1] for l in range(2)]
    g_fin = g_final[None, :]

    cos, sin = _rope_tables(T, C)
    qg = jnp.tile(q_gain, (1, 2))
    kg = jnp.tile(k_gain, (1, 2))
    lane_head = jnp.arange(LANES) // HEAD_DIM
    bd = (lane_head[:, None] == lane_head[None, :]).astype(BF16)
    w_sp0 = w_sp[0]
    w_spt0 = w_sp0.transpose(0, 2, 1)
    b_spt0 = b_sp[0].T

    XM = _norm_mod_fwd_cat(ctx2, h0, g_mix_r[0], csc1, csh1, sc1[0] + started, sh1[0], "norm_mix0")
    P = _mm(XM, W_in_t, "nt", "in_proj", tm=1088, tn=IN_WIDTH)
    qh, kpad, vpad, kt, ao = _mix_prep_fwd(P, C, cos, sin, qg, kg, bd, w_sp0, b_spt0, "mix_prep")
    ao, lse = _attn_fwd(qh, kpad, vpad, ao, C, "attn_fwd")
    W_out, = gathered(0, ao)
    h1, y0, xf0 = _mm(ao, W_out, "nn", "out_proj", res=h0, gate=gt1[0], raw_out=True,
                      norm=(g_ffn_r[0], sc2[0], sh2[0]))

    def ffn_fwd(h_in, xf, l, norm_next):
        W_ffi[l], W_ffo[l] = ffn_weights(1 + 2 * l, xf)
        gu, act = _ffn_in_swiglu(xf, W_ffi[l], f"ffn_in{l}")
        outs = _mm_sum_shards(act, W_ffo[l], "nn", f"ffn_out{l}", res=h_in, gate=gt2[l], raw_out=True,
                              norm=norm_next)
        return tuple(outs) + (None,) * (3 - len(outs)) + (gu, act)

    h2, f0, xm1, gu0, act0 = ffn_fwd(h1, xf0, 0, (g_mix_r[1], sc1[1], sh1[1]))

    W_pw1, W_pw2 = gathered(2, xm1)
    W_pw1 = col_gathered(W_pw1, 2 * Dm // N_DEV)
    ag, hg = _pw1_glu(xm1, W_pw1, b_pw1_f, "pw1")
    hd = _conv_fwd(hg, w_dw_f, b_dw_f, "conv")
    hs = _ln_silu_fwd(hd, ln_g_f, ln_b_f, "ln_silu")
    h3, y1, xf1 = _mm(hs, W_pw2, "nn", "pw2", bias=b_pw2_f, res=h2, gate=gt1[1], raw_out=True,
                      norm=(g_ffn_r[1], sc2[1], sh2[1]))
    h4, f1, _, gu1, act1 = ffn_fwd(h3, xf1, 1, None)

    dh4, sq_err, dg_final, df1, dgt2_1, _ = _final_fwd_bwd(h4, g_fin, target, f1, gt2[1], "loss_head")
    loss_local = (0.5 / Dm) * sq_err[0, 0:1]

    def col_shards(g, n):
        return g.reshape(Dm, N_DEV, n).transpose(1, 0, 2).reshape(N_DEV * Dm, n)

    def exchange_begin(k, parts):
        return _push_begin(parts, False, f"exchange_start{k}")

    def zero_of(handle):
        return handle[4][0:1, 0:1]

    def ffn_bwd(df, xf, gu, act, l):
        dw_out = _mm_tn_shard_rows(act, df, f"ffn_out_dw{l}", BF16)
        dgu = _ffn_out_dx_swiglu(df, W_ffo[l], gu, f"ffn_out_dx{l}").reshape(N_DEV, T, FF_SHARD)
        dw_in = _mm_tn_shard_rows(dgu, xf, f"ffn_in_dw{l}", BF16)
        dxf = _mm_sum_shards(dgu, W_ffi[l], "nn", f"ffn_in_dx{l}", BF16, tm=512)
        return dw_in, dw_out, dxf

    dW_ffi1, dW_ffo1, dxf1 = ffn_bwd(df1, xf1, gu1, act1, 1)
    ex0 = exchange_begin(0, [dW_ffi1.reshape(2 * D_FF, Dm), dW_ffo1.reshape(D_FF, Dm)])
    dh3, da, dsh, dy1, dgt1_1, db_pw2 = _norm_mod_bwd(h3, g_ffn_r[1], sc2[1], dxf1, dh4, "norm_ffn_bwd1",
                                                       gate=(y1, gt1[1] + zero_of(ex0)))
    dmod_ffn1 = (dsh, da * g_ffn_r[1], dgt2_1)
    dg_ffn1 = da * (1.0 + sc2[1])

    dW_pw2 = _mm(hs, dy1, "tn", "pw2_dw", BF16, tk=2048)
    dhs = _mm(dy1, W_pw2, "nt", "pw2_dx", BF16)
    dhd, dln_g, dln_b, db_dw = _ln_silu_bwd(dhs, hd, ln_g_f, ln_b_f, "ln_silu_bwd")
    dhg, dw_dw = _conv_bwd(dhd, hg, w_dw_f, "conv_bwd")
    dag, db_pw1, dxm1 = _glu_bwd_pw1_dx(ag, dhg, W_pw1, "pw1_dx")
    dW_pw1 = _mm(xm1, dag, "tn", "pw1_dw", BF16, tk=2048)
    ex1 = exchange_begin(1, [col_shards(dW_pw1, 2 * Dm // N_DEV), dW_pw2])
    dh2, da, dsh, df0, dgt2_0, _ = _norm_mod_bwd(h2, g_mix_r[1], sc1[1], dxm1, dh3, "norm_mix1_bwd",
                                                 gate=(f0, gt2[0] + zero_of(ex1)))
    dmod_mix1 = (dsh, da * g_mix_r[1], dgt1_1)
    dg_mix1 = da * (1.0 + sc1[1])

    dW_ffi0, dW_ffo0, dxf0 = ffn_bwd(df0, xf0, gu0, act0, 0)
    ex2 = exchange_begin(2, [dW_ffi0.reshape(2 * D_FF, Dm), dW_ffo0.reshape(D_FF, Dm)])
    dh1, da, dsh, dy0, dgt1_0, _ = _norm_mod_bwd(h1, g_ffn_r[0], sc2[0], dxf0, dh2, "norm_ffn_bwd0",
                                                 gate=(y0, gt1[0] + zero_of(ex2)))
    dmod_ffn0 = (dsh, da * g_ffn_r[0], dgt2_0)
    dg_ffn0 = da * (1.0 + sc2[0])

    dW_out = _mm(ao, dy0, "tn", "out_proj_dw", BF16, tk=2048)
    ex_out = exchange_begin(4, [dW_out])
    dao = _mm(dy0, W_out + zero_of(ex_out).astype(BF16), "nt", "out_proj_dx", BF16)
    dq, f_acc = _attn_bwd(qh, dao, ao, lse, kpad, vpad, kt, C, "attn_bwd")
    dP, dqg, dkg, dw_sp0, db_spt0 = _mix_prep_bwd(P, dq, f_acc, dao, C, cos, sin, qg, kg, bd, w_sp0, w_spt0,
                                                  b_spt0, "mix_prep_bwd")
    dW_in_t = _mm(dP, XM, "tn", "in_proj_dw", BF16, tm=896, tk=2176)
    dXM = _mm(dP, W_in_t, "nn", "in_proj_dx", BF16, tm=1088, tk=IN_WIDTH)
    dh0, da, dsh = _norm_mod_bwd(h0, g_mix_r[0], sc1[0], dXM, dh1, "norm_mix0_bwd", dxm_row_off=C)
    _, dac, dcsh = _norm_mod_bwd(ctx2, g_mix_r[0], csc1, dXM, None, "norm_ctx_bwd")
    dmod_mix0 = (dsh, da * g_mix_r[0], dgt1_0)
    dg_mix0 = da * (1.0 + sc1[0]) + dac * (1.0 + csc1)
    dcmod = jnp.concatenate([dcsh, dac * g_mix_r[0]], axis=1)

    dmod_mine = jnp.stack([jnp.concatenate(dmod_mix0 + dmod_ffn0, axis=1)[0],
                           jnp.concatenate(dmod_mix1 + dmod_ffn1, axis=1)[0]])

    small_grads = [
        ("loss", loss_local), ("g_final", dg_final), ("g_mix", jnp.concatenate([dg_mix0, dg_mix1])),
        ("g_ffn", jnp.concatenate([dg_ffn0, dg_ffn1])),
        ("q_gain", dqg[:, :HEAD_DIM] + dqg[:, HEAD_DIM:]), ("k_gain", dkg[:, :HEAD_DIM] + dkg[:, HEAD_DIM:]),
        ("w_sp", dw_sp0[None]), ("b_sp", db_spt0.T[None]), ("b_pw1", db_pw1), ("w_dw", dw_dw[None]),
        ("b_dw", db_dw), ("ln_g", dln_g), ("ln_b", dln_b), ("b_pw2", db_pw2), ("dcmod", dcmod),
        ("dmod", dmod_mine),
    ]
    buf3, offs3 = _pack([t for _, t in small_grads])
    off3 = {nm: off for (nm, _), off in zip(small_grads, offs3)}
    shape3 = {nm: t.shape for nm, t in small_grads}
    small_push = _push_begin([buf3], True, "small_grads_start")
    ex3 = exchange_begin(3, [dW_in_t + zero_of(small_push).astype(BF16)])

    grads, delta, new_m, new_v = {}, {}, {}, {}

    def exchanged(k, handle, after):
        return _push_end(handle, after, f"exchange_wait{k}")[1]

    def adamw_big(nm, parts, transposed=False, wmv=None):
        w3, m3, v3 = wmv if wmv is not None else (weights[nm], moments_m[nm], moments_v[nm])
        outs4 = _adamw_recv(w3, m3, v3, parts, f"adamw_{nm}")
        if transposed:
            outs4 = [jnp.swapaxes(t, 1, 2) for t in outs4]
        grads[nm], delta[nm], new_m[nm], new_v[nm] = outs4

    pushed = ex3[4]
    r_ffi1, r_ffo1 = exchanged(0, ex0, pushed)
    r_pw1, r_pw2 = exchanged(1, ex1, pushed)
    r_ffi0, r_ffo0 = exchanged(2, ex2, pushed)
    r_out, = exchanged(4, ex_out, pushed)
    adamw_big("w_ffn_in", [r_ffi0, r_ffi1], True, (w_ffi_t, m_w_ffi_t, v_w_ffi_t))
    adamw_big("w_ffn_out", [r_ffo0, r_ffo1])
    adamw_big("w_pw1", [r_pw1])
    adamw_big("w_pw2", [r_pw2])
    adamw_big("w_out", [r_out])

    got3 = _push_end(small_push, delta["w_out"], "small_grads_wait")[1][0].reshape(N_DEV, buf3.shape[0], LANES)
    sum3 = _sum_devices(got3, "sum_small_grads").reshape(-1)

    def summed(nm):
        return _take(sum3, off3[nm], shape3[nm])

    loss = summed("loss")[0]
    dcmod_sum = summed("dcmod")
    dmod_rows = _take(got3.reshape(N_DEV, -1), off3["dmod"], (2, 6 * Dm)).transpose(1, 0, 2)
    ctx_row = jnp.concatenate([jnp.pad(dcmod_sum, ((0, 0), (0, 4 * Dm))), jnp.zeros((1, 6 * Dm), F32)])
    dmod_all = jnp.concatenate([dmod_rows, ctx_row[:, None, :],
                                jnp.zeros((2, LANES - N_DEV - 1, 6 * Dm), F32)], axis=1)
    grads["b_mod"] = summed("dmod") + ctx_row
    dmod_shard = lax.dynamic_slice_in_dim(dmod_all, me * n_mod, n_mod, axis=2)
    c_rows_t = jnp.pad(c_rows.T, ((0, 0), (0, LANES - MOD_ROWS)))
    grads["w_mod"], ds_part = _mod_bwd(c_rows_t, dmod_shard, w_mod, "mod_bwd")

    buf4, _ = _pack([ds_part[0, N_DEV]])
    got4 = _all_gather([buf4], "gather_c_ctx_grad", True)[0].reshape(N_DEV, buf4.shape[0], LANES)
    ds_ctx = _sum_devices(got4, "sum_c_ctx_grad").reshape(-1)[:Dm]
    grads["c_ctx"] = ds_ctx * _dsilu(c_ctx)

    for nm in ("g_final", "g_mix", "g_ffn", "q_gain", "k_gain", "w_sp", "b_sp"):
        grads[nm] = summed(nm).reshape(weights[nm].shape)
    for nm in ("b_pw1", "w_dw", "b_dw", "ln_g", "ln_b", "b_pw2"):
        n_loc = weights[nm].shape[-1]
        grads[nm] = lax.dynamic_slice_in_dim(summed(nm), me * n_loc, n_loc, axis=-1).reshape(weights[nm].shape)

    shp = w_mod.shape
    outs = _adamw(w_mod.reshape(-1, shp[-1]), grads["w_mod"].reshape(-1, shp[-1]),
                  m_w_mod.reshape(-1, shp[-1]), v_w_mod.reshape(-1, shp[-1]), "adamw_w_mod")
    delta["w_mod"], new_m["w_mod"], new_v["w_mod"] = (o.reshape(shp) for o in outs)
    big_names = ("w_mod", "w_ffn_in", "w_ffn_out", "w_in", "w_out", "w_pw1", "w_pw2")
    small_names = [nm for nm in names if nm not in big_names]
    packs = [_pack([src[nm] for nm in small_names]) for src in (weights, grads, moments_m, moments_v)]
    offs_s = packs[0][1]
    outs = _adamw(*[pk[0] for pk in packs], "adamw_small")
    for o, dst in zip(outs, (delta, new_m, new_v)):
        o = o.reshape(-1)
        for nm, off in zip(small_names, offs_s):
            dst[nm] = _take(o, off, weights[nm].shape)
    r_in, = exchanged(3, ex3, outs[0])
    adamw_big("w_in", [r_in], True, (w_in_t, m_w_in_t, v_w_in_t))

    return (loss, dh0[None], *[grads[n] for n in names], *[delta[n] for n in names],
            *[new_m[n] for n in names], *[new_v[n] for n in names])
```

```python
import math

import jax
import jax.numpy as jnp
from jax import lax
from jax.experimental import pallas as pl
from jax.experimental.pallas import tpu as pltpu

F32 = jnp.float32
BF16 = jnp.bfloat16
MESH = pl.DeviceIdType.MESH

N_DEV = 8
D_MODEL = 1024
EPS = 1e-6
HEAD_DIM = 64
ATTN_WIDTH = 512
KV_WIDTH = 128
SG_WIDTH = 512
N_SG_GROUPS = 4
CHUNK = 128
IN_WIDTH = 1792
D_FF = 2816
FF_SHARD = 2 * D_FF // N_DEV
CONV_WIDTH = 31
CONV_HALO = 16
GRID_W = 64
ROPE_THETA = 10000.0
LANES = 128
SUBLANES = 8
ROW_BLOCK = 512
ADAM_ROWS = 256
ADAM_LR, ADAM_B1, ADAM_B2, ADAM_EPS, ADAM_WD, ADAM_STEP = 0.001, 0.9, 0.999, 1e-08, 0.01, 10


def _tile(n, target, mult=LANES):
    best = None
    for t in range(mult, min(n, target) + 1, mult):
        if n % t == 0:
            best = t
    return best if best is not None else n


def _sigmoid(x):
    return 1.0 / (1.0 + jnp.exp(-x))


def _silu(x):
    return x * _sigmoid(x)


def _dsilu(x):
    s = _sigmoid(x)
    return s * (1.0 + x * (1.0 - s))


_GELU_K = math.sqrt(2.0 / math.pi)


def _gelu(x):
    return 0.5 * x * (1.0 + jnp.tanh(_GELU_K * (x + 0.044715 * x * x * x)))


def _gelu_and_grad(x):
    x2 = x * x
    t = jnp.tanh(_GELU_K * x * (1.0 + 0.044715 * x2))
    half = 0.5 * (1.0 + t)
    return x * half, half + 0.5 * x * (1.0 - t * t) * _GELU_K * (1.0 + 3.0 * 0.044715 * x2)


def _split_bf16(x):
    hi = x.astype(BF16)
    lo = (x - hi.astype(F32)).astype(BF16)
    return hi, lo


def _dot(a, b, dims):
    return lax.dot_general(a, b, (dims, ((), ())), preferred_element_type=F32)


def _dot3(a, b, dims):
    ah, al = _split_bf16(a)
    bh, bl = _split_bf16(b)
    return _dot(ah, bh, dims) + _dot(ah, bl, dims) + _dot(al, bh, dims)


NN = ((1,), (0,))
NT = ((1,), (1,))
TN = ((0,), (0,))


def _all_gather(xs, name, in_vmem):
    n_arr = len(xs)

    def body(*refs):
        x_refs, out_refs = refs[:n_arr], refs[n_arr:2 * n_arr]
        send_sems, recv_sems, local_sems = refs[2 * n_arr:]
        x, y, c = lax.axis_index("x"), lax.axis_index("y"), lax.axis_index("c")
        me, sibling = (x, y, c), (x, y, 1 - c)
        chips = [(1 - x, y), (x, 1 - y), (1 - x, 1 - y)]

        def rows(a, px, py, pc):
            m_per = xs[a].shape[0]
            return out_refs[a].at[pl.ds((4 * px + 2 * py + pc) * m_per, m_per), :]

        def copy(a, k, block, to, src=None):
            return pltpu.make_async_remote_copy(
                src_ref=rows(a, *block) if src is None else src,
                dst_ref=rows(a, *block),
                send_sem=send_sems.at[7 * a + k],
                recv_sem=recv_sems.at[7 * a + k],
                device_id=to,
                device_id_type=MESH,
            )

        mine, first, passed = [], [], []
        for a in range(n_arr):
            mine.append(pltpu.make_async_copy(x_refs[a], rows(a, *me), local_sems.at[a]))
            mine[-1].start()
            first.append(copy(a, 0, me, sibling, src=x_refs[a]))
            first += [copy(a, 1 + j, me, (*chip, c), src=x_refs[a]) for j, chip in enumerate(chips)]
        for cp in first:
            cp.start()
        for a in range(n_arr):
            for j, chip in enumerate(chips):
                copy(a, 1 + j, (*chip, c), me).wait_recv()
                passed.append(copy(a, 4 + j, (*chip, c), sibling))
                passed[-1].start()
        for a in range(n_arr):
            copy(a, 0, sibling, me).wait_recv()
            for j, chip in enumerate(chips):
                copy(a, 4 + j, (*chip, 1 - c), me).wait_recv()
        for cp in first + passed:
            cp.wait_send()
        for cp in mine:
            cp.wait()

    space = pltpu.VMEM if in_vmem else pl.ANY
    return pl.pallas_call(
        body,
        name=name,
        out_shape=[jax.ShapeDtypeStruct((N_DEV * t.shape[0], t.shape[1]), t.dtype) for t in xs],
        in_specs=[pl.BlockSpec(memory_space=space)] * n_arr,
        out_specs=[pl.BlockSpec(memory_space=space)] * n_arr,
        scratch_shapes=[
            pltpu.SemaphoreType.DMA((7 * n_arr,)),
            pltpu.SemaphoreType.DMA((7 * n_arr,)),
            pltpu.SemaphoreType.DMA((n_arr,)),
        ],
    )(*xs)


HBM_SPEC = pl.BlockSpec(memory_space=pltpu.HBM)
SEM_SPEC = pl.BlockSpec(memory_space=pltpu.SEMAPHORE)
DATAFLOW_EFFECT = pltpu.SideEffectType.DATAFLOW_SIDE_EFFECTING


def _peers(x, y, c):
    for k in range(1, N_DEV):
        px = 1 - x if (k >> 2) & 1 else x
        py = 1 - y if (k >> 1) & 1 else y
        pc = 1 - c if k & 1 else c
        yield k - 1, (px, py, pc), 4 * px + 2 * py + pc


def _push_copies(src_refs, land_refs, send_sems, recv_sems, shapes, whole_src):
    x, y, c = lax.axis_index("x"), lax.axis_index("y"), lax.axis_index("c")
    me = 4 * x + 2 * y + c
    for a, (m_per, _) in enumerate(shapes):
        def block(ref, idx, m_per=m_per):
            return ref.at[pl.ds(idx * m_per, m_per), :]

        for k, peer, pidx in _peers(x, y, c):
            src = src_refs[a] if whole_src else block(src_refs[a], pidx)
            sems = dict(send_sem=send_sems.at[N_DEV * a + k], recv_sem=recv_sems.at[N_DEV * a + k],
                        device_id=peer, device_id_type=MESH)
            yield (pltpu.make_async_remote_copy(src_ref=src, dst_ref=block(land_refs[a], me), **sems),
                   pltpu.make_async_remote_copy(src_ref=src, dst_ref=block(land_refs[a], pidx), **sems))


def _own_copies(src_refs, land_refs, recv_sems, shapes, whole_src):
    me = 4 * lax.axis_index("x") + 2 * lax.axis_index("y") + lax.axis_index("c")
    for a, (m_per, _) in enumerate(shapes):
        mine = pl.ds(me * m_per, m_per)
        src = src_refs[a] if whole_src else src_refs[a].at[mine, :]
        yield pltpu.make_async_copy(src, land_refs[a].at[mine, :], recv_sems.at[N_DEV * a + N_DEV - 1])


def _push_begin(srcs, whole_src, name):
    n_arr = len(srcs)
    shapes = [(t.shape[0] if whole_src else t.shape[0] // N_DEV, t.shape[1]) for t in srcs]
    lands = [lax.empty((N_DEV * m, n), t.dtype) for (m, n), t in zip(shapes, srcs)]

    def body(*refs):
        src_refs, land_refs = refs[:n_arr], refs[n_arr:2 * n_arr]
        send_sems, recv_sems = refs[2 * n_arr], refs[2 * n_arr + 1]
        token = refs[-1]
        for outgoing, _ in _push_copies(src_refs, land_refs, send_sems, recv_sems, shapes, whole_src):
            outgoing.start()
        for own in _own_copies(src_refs, land_refs, recv_sems, shapes, whole_src):
            own.start()
        token[...] = jnp.zeros_like(token)

    operands = [pltpu.with_memory_space_constraint(t, pltpu.HBM) for t in list(srcs) + lands]
    outs = pl.pallas_call(
        body, name=name,
        out_shape=(pltpu.SemaphoreType.DMA((N_DEV * n_arr,)), pltpu.SemaphoreType.DMA((N_DEV * n_arr,)),
                   *[pltpu.HBM(t.shape, t.dtype) for t in operands],
                   jax.ShapeDtypeStruct((SUBLANES, LANES), F32)),
        in_specs=[HBM_SPEC] * (2 * n_arr),
        out_specs=(SEM_SPEC, SEM_SPEC, *[HBM_SPEC] * (2 * n_arr), pl.BlockSpec(memory_space=pltpu.VMEM)),
        input_output_aliases={i: 2 + i for i in range(2 * n_arr)},
        compiler_params=pltpu.CompilerParams(has_side_effects=DATAFLOW_EFFECT),
    )(*operands)
    return outs[0], outs[1], list(outs[2:2 + n_arr]), list(outs[2 + n_arr:2 + 2 * n_arr]), outs[-1], whole_src


def _push_end(handle, after, name):
    send_sems, recv_sems, srcs, lands, _, whole_src = handle
    n_arr = len(srcs)
    shapes = [(t.shape[0] // N_DEV, t.shape[1]) for t in lands]

    def body(*refs):
        src_refs, land_refs = refs[:n_arr], refs[n_arr:2 * n_arr]
        send_sems_ref, recv_sems_ref = refs[2 * n_arr], refs[2 * n_arr + 1]
        for outgoing, incoming in _push_copies(src_refs, land_refs, send_sems_ref, recv_sems_ref, shapes, whole_src):
            outgoing.wait_send()
            incoming.wait_recv()
        for own in _own_copies(src_refs, land_refs, recv_sems_ref, shapes, whole_src):
            own.wait()

    outs = pl.pallas_call(
        body, name=name,
        out_shape=tuple(pltpu.HBM(t.shape, t.dtype) for t in srcs + lands),
        in_specs=[HBM_SPEC] * (2 * n_arr) + [SEM_SPEC, SEM_SPEC, pl.BlockSpec(memory_space=pl.ANY)],
        out_specs=tuple([HBM_SPEC] * (2 * n_arr)),
        input_output_aliases={i: i for i in range(2 * n_arr)},
        compiler_params=pltpu.CompilerParams(has_side_effects=DATAFLOW_EFFECT),
    )(*srcs, *lands, send_sems, recv_sems, after)
    return list(outs[:n_arr]), list(outs[n_arr:])


def _sum_devices(r, name, rows_per_step=ADAM_ROWS):
    _, m, n = r.shape
    tm = _tile(m, rows_per_step, 8)

    def body(r_ref, o_ref):
        acc = r_ref[0].astype(F32)
        for s in range(1, N_DEV):
            acc = acc + r_ref[s].astype(F32)
        o_ref[...] = acc

    return pl.pallas_call(
        body,
        name=name,
        grid=(m // tm,),
        out_shape=jax.ShapeDtypeStruct((m, n), F32),
        in_specs=[pl.BlockSpec((N_DEV, tm, n), lambda i: (0, i, 0))],
        out_specs=pl.BlockSpec((tm, n), lambda i: (i, 0)),
        compiler_params=pltpu.CompilerParams(dimension_semantics=("parallel",)),
    )(r)


def _get(ref):
    return ref[0] if len(ref.shape) == 3 else ref[...]


def _put(ref, val):
    if len(ref.shape) == 3:
        ref[0] = val
    else:
        ref[...] = val


def _norm_mod(hv, g, sc, sh):
    r = lax.rsqrt(jnp.mean(hv * hv, axis=-1, keepdims=True) + EPS)
    return (hv * r) * g * (1.0 + sc) + sh


def _mm_call(name, a, b, a_spec, b_spec, out_sds, o_spec, grid, dims, acc_shape, bias=None,
             res=None, gate=None, raw_out=False, vec_spec=None, norm=None):
    nk = grid[2]
    operands, in_specs = [a, b], [a_spec, b_spec]
    if bias is not None:
        operands.append(bias)
        in_specs.append(vec_spec)
    if res is not None:
        operands += [res, gate]
        in_specs += [o_spec, vec_spec]
    if norm is not None:
        assert grid[1] == 1
        operands += list(norm)
        in_specs += [vec_spec] * 3
    out_shape, out_specs = [out_sds], [o_spec]
    if raw_out:
        out_shape.append(jax.ShapeDtypeStruct(out_sds.shape, BF16))
        out_specs.append(o_spec)
    if norm is not None:
        out_shape.append(jax.ShapeDtypeStruct(out_sds.shape, BF16))
        out_specs.append(o_spec)

    def body(*refs):
        it = iter(refs)
        a_ref, b_ref = next(it), next(it)
        bias_ref = next(it) if bias is not None else None
        res_ref, gate_ref = (next(it), next(it)) if res is not None else (None, None)
        norm_refs = (next(it), next(it), next(it)) if norm is not None else None
        o_ref = next(it)
        raw_ref = next(it) if raw_out else None
        xn_ref = next(it) if norm is not None else None
        acc = next(it) if nk > 1 else None
        k = pl.program_id(2)
        part = _dot(_get(a_ref).astype(BF16), _get(b_ref).astype(BF16), dims)

        def finish(y):
            if bias_ref is not None:
                y = y + bias_ref[...]
            if raw_ref is not None:
                raw_ref[...] = y.astype(BF16)
            if res_ref is not None:
                y = res_ref[...] + gate_ref[...] * y
            _put(o_ref, y.astype(out_sds.dtype))
            if xn_ref is not None:
                xn_ref[...] = _norm_mod(y, *[r[...] for r in norm_refs]).astype(BF16)

        if nk == 1:
            finish(part)
        else:
            @pl.when(k == 0)
            def _():
                acc[...] = part

            @pl.when(k > 0)
            def _():
                acc[...] += part

            @pl.when(k == nk - 1)
            def _():
                finish(acc[...])

    outs = pl.pallas_call(
        body,
        name=name,
        grid=grid,
        out_shape=out_shape,
        in_specs=in_specs,
        out_specs=out_specs,
        scratch_shapes=[pltpu.VMEM(acc_shape, F32)] if nk > 1 else [],
        compiler_params=pltpu.CompilerParams(dimension_semantics=("parallel", "parallel", "arbitrary")),
    )(*operands)
    return outs if len(outs) > 1 else outs[0]


def _mm(a, b, mode, name, out_dtype=F32, bias=None, res=None, gate=None, raw_out=False,
        tm=1024, tn=1024, tk=1024, a_row_off=0, norm=None):
    if mode == "nn":
        K, N = b.shape
        M = a.shape[0] - a_row_off
    elif mode == "nt":
        N, K = b.shape
        M = a.shape[0] - a_row_off
    else:
        (K, M), N = a.shape, b.shape[1]
    tm, tn, tk = _tile(M, tm, LANES if mode == "tn" else 2 * SUBLANES), _tile(N, tn), _tile(K, tk)
    off = a_row_off // tm
    dims = {"nn": NN, "nt": NT, "tn": TN}[mode]
    a_spec = (pl.BlockSpec((tk, tm), lambda i, j, k: (k, i)) if mode == "tn"
              else pl.BlockSpec((tm, tk), lambda i, j, k: (i + off, k)))
    b_spec = (pl.BlockSpec((tn, tk), lambda i, j, k: (j, k)) if mode == "nt"
              else pl.BlockSpec((tk, tn), lambda i, j, k: (k, j)))
    return _mm_call(name, a, b, a_spec, b_spec, jax.ShapeDtypeStruct((M, N), out_dtype),
                    pl.BlockSpec((tm, tn), lambda i, j, k: (i, j)), (M // tm, N // tn, K // tk), dims,
                    (tm, tn), bias, res, gate, raw_out, pl.BlockSpec((1, tn), lambda i, j, k: (0, j)), norm)


def _mm_sum_shards(a3, b3, mode, name, out_dtype=F32, res=None, gate=None, raw_out=False, tm=512, norm=None):
    S, M, kk = a3.shape
    N = b3.shape[2] if mode == "nn" else b3.shape[1]
    tm = _tile(M, tm)
    dims = NN if mode == "nn" else NT
    has_res = res is not None

    def body(*refs):
        it = iter(refs)
        a_ref, b_ref = next(it), next(it)
        res_ref, gate_ref = (next(it), next(it)) if has_res else (None, None)
        norm_refs = (next(it), next(it), next(it)) if norm is not None else None
        o_ref = next(it)
        raw_ref = next(it) if raw_out else None
        xn_ref = next(it) if norm is not None else None
        y = _dot(a_ref[0], b_ref[0], dims)
        for s in range(1, S):
            y = y + _dot(a_ref[s], b_ref[s], dims)
        if raw_ref is not None:
            raw_ref[...] = y.astype(BF16)
        if has_res:
            y = res_ref[...] + gate_ref[...] * y
        o_ref[...] = y.astype(out_dtype)
        if xn_ref is not None:
            xn_ref[...] = _norm_mod(y, *[r[...] for r in norm_refs]).astype(BF16)

    tile = pl.BlockSpec((tm, N), lambda i: (i, 0))
    operands = [a3, b3] + ([res, gate] if has_res else []) + (list(norm) if norm is not None else [])
    in_specs = [pl.BlockSpec((S, tm, kk), lambda i: (0, i, 0)), pl.BlockSpec(b3.shape, lambda i: (0, 0, 0))]
    in_specs += [tile, _vec_spec(N)] if has_res else []
    in_specs += [_vec_spec(N)] * 3 if norm is not None else []
    out_shape = [jax.ShapeDtypeStruct((M, N), out_dtype)] + ([jax.ShapeDtypeStruct((M, N), BF16)] if raw_out else [])
    out_shape += [jax.ShapeDtypeStruct((M, N), BF16)] if norm is not None else []
    outs = pl.pallas_call(
        body, name=name, grid=(M // tm,),
        out_shape=out_shape, in_specs=in_specs, out_specs=[tile] * len(out_shape),
        compiler_params=pltpu.CompilerParams(dimension_semantics=("parallel",)),
    )(*operands)
    return outs if len(outs) > 1 else outs[0]


def _mm_tn_shard_rows(a3, b, name, out_dtype, tn=1024, tk=4096):
    S, T, m = a3.shape
    N = b.shape[1]
    tn, tk = _tile(N, tn), _tile(T, tk)
    return _mm_call(name, a3, b, pl.BlockSpec((1, tk, m), lambda i, j, k: (i, k, 0)),
                    pl.BlockSpec((tk, tn), lambda i, j, k: (k, j)), jax.ShapeDtypeStruct((S, m, N), out_dtype),
                    pl.BlockSpec((1, m, tn), lambda i, j, k: (i, 0, j)), (S, N // tn, T // tk), TN, (m, tn))


def _row_spec(tm, width, off=0):
    return pl.BlockSpec((tm, width), lambda i: (i + off, 0))


def _vec_spec(width):
    return pl.BlockSpec((1, width), lambda i: (0, 0))


def _norm_mod_fwd_cat(hc, h, g, csc, csh, sc, sh, name):
    (C, Dm), T = hc.shape, h.shape[0]
    tm = _tile(math.gcd(C, T), ROW_BLOCK, 8)
    off = C // tm

    def body(hc_ref, h_ref, g_ref, csc_ref, csh_ref, sc_ref, sh_ref, o_ref):
        is_ctx = pl.program_id(0) < off
        hv = jnp.where(is_ctx, hc_ref[...], h_ref[...])
        scv = jnp.where(is_ctx, csc_ref[...], sc_ref[...])
        shv = jnp.where(is_ctx, csh_ref[...], sh_ref[...])
        r = lax.rsqrt(jnp.mean(hv * hv, axis=-1, keepdims=True) + EPS)
        o_ref[...] = ((hv * r) * g_ref[...] * (1.0 + scv) + shv).astype(BF16)

    return pl.pallas_call(
        body, name=name, grid=((C + T) // tm,),
        out_shape=jax.ShapeDtypeStruct((C + T, Dm), BF16),
        in_specs=[pl.BlockSpec((tm, Dm), lambda i: (jnp.minimum(i, off - 1), 0)),
                  pl.BlockSpec((tm, Dm), lambda i: (jnp.maximum(i - off, 0), 0))] + [_vec_spec(Dm)] * 5,
        out_specs=_row_spec(tm, Dm),
        compiler_params=pltpu.CompilerParams(dimension_semantics=("parallel",)),
    )(hc, h, g, csc, csh, sc, sh)


def _gate_grads(dh, y_ref, gt_ref, dy_ref, dgt_ref, dsum_ref):
    dy = dh * gt_ref[...]
    dgt_ref[...] += jnp.sum(dh * y_ref[...].astype(F32), axis=0, keepdims=True)
    dsum_ref[...] += jnp.sum(dy, axis=0, keepdims=True)
    dy_ref[...] = dy.astype(BF16)


def _norm_mod_bwd(h, g, sc, dxm, dres, name, dxm_row_off=0, gate=None):
    R, Dm = h.shape
    tm = _tile(math.gcd(R, dxm_row_off) if dxm_row_off else R, ROW_BLOCK, 8)
    off = dxm_row_off // tm
    has_res = dres is not None
    has_gate = gate is not None

    def body(*refs):
        it = iter(refs)
        h_ref, g_ref, sc_ref, dx_ref = next(it), next(it), next(it), next(it)
        dres_ref = next(it) if has_res else None
        y_ref, gt_ref = (next(it), next(it)) if has_gate else (None, None)
        dh_ref, da_ref, dsh_ref = next(it), next(it), next(it)
        gate_out = (next(it), next(it), next(it)) if has_gate else ()
        i = pl.program_id(0)

        @pl.when(i == 0)
        def _():
            for ref in (da_ref, dsh_ref) + gate_out[1:]:
                ref[...] = jnp.zeros_like(ref)

        hv = h_ref[...]
        dx = dx_ref[...].astype(F32)
        r = lax.rsqrt(jnp.mean(hv * hv, axis=-1, keepdims=True) + EPS)
        n = hv * r
        da_ref[...] += jnp.sum(dx * n, axis=0, keepdims=True)
        dsh_ref[...] += jnp.sum(dx, axis=0, keepdims=True)
        dn = dx * (g_ref[...] * (1.0 + sc_ref[...]))
        dh = r * (dn - n * jnp.mean(dn * n, axis=-1, keepdims=True))
        if has_res:
            dh = dh + dres_ref[...]
        dh_ref[...] = dh
        if has_gate:
            _gate_grads(dh, y_ref, gt_ref, *gate_out)

    operands = [h, g, sc, dxm] + ([dres] if has_res else []) + (list(gate) if has_gate else [])
    in_specs = [_row_spec(tm, Dm), _vec_spec(Dm), _vec_spec(Dm), _row_spec(tm, Dm, off)]
    in_specs += [_row_spec(tm, Dm)] if has_res else []
    in_specs += [_row_spec(tm, Dm), _vec_spec(Dm)] if has_gate else []
    vec = jax.ShapeDtypeStruct((1, Dm), F32)
    out_shape = [jax.ShapeDtypeStruct((R, Dm), F32), vec, vec]
    out_specs = [_row_spec(tm, Dm), _vec_spec(Dm), _vec_spec(Dm)]
    if has_gate:
        out_shape += [jax.ShapeDtypeStruct((R, Dm), BF16), vec, vec]
        out_specs += [_row_spec(tm, Dm), _vec_spec(Dm), _vec_spec(Dm)]
    return pl.pallas_call(
        body, name=name, grid=(R // tm,),
        out_shape=out_shape, in_specs=in_specs, out_specs=out_specs,
        compiler_params=pltpu.CompilerParams(dimension_semantics=("arbitrary",)),
    )(*operands)


def _ffn_in_swiglu(xf, w3, name, tm=1024):
    T, K = xf.shape
    S, n, _ = w3.shape
    half = S // 2
    tm = _tile(T, tm)

    def body(a_ref, wg_ref, wu_ref, gu_ref, act_ref):
        a = a_ref[...]
        g = _dot(a, wg_ref[0], NT)
        u = _dot(a, wu_ref[0], NT)
        gu_ref[0, 0] = g.astype(BF16)
        gu_ref[1, 0] = u.astype(BF16)
        act_ref[0] = (_silu(g) * u).astype(BF16)

    return pl.pallas_call(
        body, name=name, grid=(T // tm, half),
        out_shape=[jax.ShapeDtypeStruct((2, half, T, n), BF16), jax.ShapeDtypeStruct((half, T, n), BF16)],
        in_specs=[pl.BlockSpec((tm, K), lambda i, j: (i, 0)),
                  pl.BlockSpec((1, n, K), lambda i, j: (j, 0, 0)),
                  pl.BlockSpec((1, n, K), lambda i, j: (j + half, 0, 0))],
        out_specs=[pl.BlockSpec((2, 1, tm, n), lambda i, j: (0, j, i, 0)),
                   pl.BlockSpec((1, tm, n), lambda i, j: (j, i, 0))],
        compiler_params=pltpu.CompilerParams(dimension_semantics=("parallel", "parallel")),
    )(xf, w3, w3)


def _ffn_out_dx_swiglu(df, wo, gu, name, tm=1024):
    T, Dm = df.shape
    half, n, _ = wo.shape
    tm = _tile(T, tm)

    def body(df_ref, w_ref, gu_ref, o_ref):
        da = _dot(df_ref[...], w_ref[0], NT)
        g = gu_ref[0, 0].astype(F32)
        u = gu_ref[1, 0].astype(F32)
        s = _sigmoid(g)
        o_ref[0, 0] = (da * u * (s * (1.0 + g * (1.0 - s)))).astype(BF16)
        o_ref[1, 0] = (da * (g * s)).astype(BF16)

    gu_spec = pl.BlockSpec((2, 1, tm, n), lambda i, j: (0, j, i, 0))
    return pl.pallas_call(
        body, name=name, grid=(T // tm, half),
        out_shape=jax.ShapeDtypeStruct(gu.shape, BF16),
        in_specs=[pl.BlockSpec((tm, Dm), lambda i, j: (i, 0)),
                  pl.BlockSpec((1, n, Dm), lambda i, j: (j, 0, 0)), gu_spec],
        out_specs=gu_spec,
        compiler_params=pltpu.CompilerParams(dimension_semantics=("parallel", "parallel")),
    )(df, wo, gu)


def _pw1_glu(xm, w, bias, name, tm=512):
    T, K = xm.shape
    N = w.shape[1]
    tm = _tile(T, tm, 2 * SUBLANES)

    def body(a_ref, w_ref, b_ref, ag_ref, hg_ref):
        ag = (_dot(a_ref[...], w_ref[...], NN) + b_ref[...]).astype(BF16)
        ag_ref[...] = ag
        hg_ref[...] = ag[:, :N // 2].astype(F32) * _sigmoid(ag[:, N // 2:].astype(F32))

    return pl.pallas_call(
        body, name=name, grid=(T // tm,),
        out_shape=[jax.ShapeDtypeStruct((T, N), BF16), jax.ShapeDtypeStruct((T, N // 2), F32)],
        in_specs=[_row_spec(tm, K), pl.BlockSpec((K, N), lambda i: (0, 0)), _vec_spec(N)],
        out_specs=[_row_spec(tm, N), _row_spec(tm, N // 2)],
        compiler_params=pltpu.CompilerParams(dimension_semantics=("parallel",)),
    )(xm, w, bias)


def _glu_bwd_pw1_dx(ag, dhg, w, name, tm=512):
    T, N = ag.shape
    Dm = N // 2
    tm = _tile(T, tm, 2 * SUBLANES)

    def body(ag_ref, dh_ref, w_ref, dag_ref, s_ref, dx_ref):
        i = pl.program_id(0)

        @pl.when(i == 0)
        def _():
            s_ref[...] = jnp.zeros_like(s_ref)

        a = ag_ref[:, :Dm].astype(F32)
        s = _sigmoid(ag_ref[:, Dm:].astype(F32))
        dh = dh_ref[...]
        da = dh * s
        dg = dh * a * s * (1.0 - s)
        dag_ref[:, :Dm] = da.astype(BF16)
        dag_ref[:, Dm:] = dg.astype(BF16)
        s_ref[:, :Dm] += jnp.sum(da, axis=0, keepdims=True)
        s_ref[:, Dm:] += jnp.sum(dg, axis=0, keepdims=True)
        dx_ref[...] = _dot(dag_ref[...], w_ref[...], NT).astype(BF16)

    return pl.pallas_call(
        body, name=name, grid=(T // tm,),
        out_shape=[jax.ShapeDtypeStruct((T, N), BF16), jax.ShapeDtypeStruct((1, N), F32),
                   jax.ShapeDtypeStruct((T, Dm), BF16)],
        in_specs=[_row_spec(tm, N), _row_spec(tm, Dm), pl.BlockSpec(w.shape, lambda i: (0, 0))],
        out_specs=[_row_spec(tm, N), _vec_spec(N), _row_spec(tm, Dm)],
        compiler_params=pltpu.CompilerParams(dimension_semantics=("arbitrary",)),
    )(ag, dhg, w)


def _halo_specs(tm, nblk, width):
    per = tm // CONV_HALO
    prev = pl.BlockSpec((CONV_HALO, width), lambda i: (jnp.maximum(i * per - 1, 0), 0))
    nxt = pl.BlockSpec((CONV_HALO, width), lambda i: (jnp.minimum((i + 1) * per, nblk * per - 1), 0))
    return prev, nxt


def _fill_halo(scr, prev_ref, cur_ref, next_ref, i, nblk, tm):
    scr[0:CONV_HALO, :] = jnp.where(i > 0, prev_ref[...], 0.0)
    scr[CONV_HALO:CONV_HALO + tm, :] = cur_ref[...]
    scr[CONV_HALO + tm:2 * CONV_HALO + tm, :] = jnp.where(i < nblk - 1, next_ref[...], 0.0)


CONV_ROWS = 128


CONV_REACH = (CONV_WIDTH // SUBLANES) * SUBLANES


def _windows(scr, stage, cols, tm):
    for r in range(SUBLANES):
        if r:
            stage[r] = scr[pl.ds(r, tm + CONV_REACH), cols]
        for a in range(CONV_REACH // SUBLANES + 1):
            off = SUBLANES * a + r
            if 1 <= off <= CONV_WIDTH:
                yield off, (stage[r, SUBLANES * a:SUBLANES * a + tm, :] if r
                            else scr[SUBLANES * a:SUBLANES * a + tm, cols])


def _conv_fwd(hg, w_dw, b_dw, name):
    R, Dm = hg.shape
    tm = _tile(R, CONV_ROWS, CONV_HALO)
    nblk = R // tm
    prev_spec, next_spec = _halo_specs(tm, nblk, Dm)

    def body(prev_ref, cur_ref, next_ref, w_ref, bdw_ref, hd_ref, scr, stage):
        _fill_halo(scr, prev_ref, cur_ref, next_ref, pl.program_id(0), nblk, tm)
        for cb in range(Dm // LANES):
            cols = slice(cb * LANES, (cb + 1) * LANES)
            acc = jnp.zeros((tm, LANES), F32) + bdw_ref[:, cols]
            for off, win in _windows(scr, stage, cols, tm):
                acc = acc + w_ref[off - 1:off, cols] * win
            hd_ref[:, cols] = acc

    return pl.pallas_call(
        body, name=name, grid=(nblk,),
        out_shape=jax.ShapeDtypeStruct((R, Dm), F32),
        in_specs=[prev_spec, _row_spec(tm, Dm), next_spec,
                  pl.BlockSpec((CONV_WIDTH, Dm), lambda i: (0, 0)), _vec_spec(Dm)],
        out_specs=_row_spec(tm, Dm),
        scratch_shapes=[pltpu.VMEM((tm + 2 * CONV_HALO, Dm), F32),
                        pltpu.VMEM((SUBLANES, tm + CONV_REACH, LANES), F32)],
        compiler_params=pltpu.CompilerParams(dimension_semantics=("parallel",)),
    )(hg, hg, hg, w_dw, b_dw)


def _ln_silu_fwd(hd, ln_g, ln_b, name):
    R, Dm = hd.shape
    tm = _tile(R, ROW_BLOCK, 8)

    def body(hd_ref, g_ref, b_ref, hs_ref):
        hd = hd_ref[...]
        xc = hd - jnp.mean(hd, axis=-1, keepdims=True)
        rs = lax.rsqrt(jnp.mean(xc * xc, axis=-1, keepdims=True) + EPS)
        hs_ref[...] = _silu(xc * rs * g_ref[...] + b_ref[...]).astype(BF16)

    return pl.pallas_call(
        body, name=name, grid=(R // tm,),
        out_shape=jax.ShapeDtypeStruct((R, Dm), BF16),
        in_specs=[_row_spec(tm, Dm), _vec_spec(Dm), _vec_spec(Dm)],
        out_specs=_row_spec(tm, Dm),
        compiler_params=pltpu.CompilerParams(dimension_semantics=("parallel",)),
    )(hd, ln_g, ln_b)


def _pw2_dx_ln_silu_bwd(dy, w, hd, ln_g, ln_b, name, tm=512):
    R, Dm = hd.shape
    tm = _tile(R, tm, 2 * SUBLANES)

    def body(dy_ref, w_ref, hd_ref, g_ref, b_ref, dhd_ref, dg_ref, db_ref, dsum_ref):
        i = pl.program_id(0)

        @pl.when(i == 0)
        def _():
            dg_ref[...] = jnp.zeros_like(dg_ref)
            db_ref[...] = jnp.zeros_like(db_ref)
            dsum_ref[...] = jnp.zeros_like(dsum_ref)

        hd = hd_ref[...]
        mu = jnp.mean(hd, axis=-1, keepdims=True)
        xc = hd - mu
        rs = lax.rsqrt(jnp.mean(xc * xc, axis=-1, keepdims=True) + EPS)
        z = xc * rs
        hl = z * g_ref[...] + b_ref[...]
        dhl = _dot(dy_ref[...], w_ref[...], NT) * _dsilu(hl)
        dg_ref[...] += jnp.sum(dhl * z, axis=0, keepdims=True)
        db_ref[...] += jnp.sum(dhl, axis=0, keepdims=True)
        dz = dhl * g_ref[...]
        dhd = rs * (dz - jnp.mean(dz, axis=-1, keepdims=True) - z * jnp.mean(dz * z, axis=-1, keepdims=True))
        dsum_ref[...] += jnp.sum(dhd, axis=0, keepdims=True)
        dhd_ref[...] = dhd

    return pl.pallas_call(
        body, name=name, grid=(R // tm,),
        out_shape=[jax.ShapeDtypeStruct((R, Dm), F32)] + [jax.ShapeDtypeStruct((1, Dm), F32)] * 3,
        in_specs=[_row_spec(tm, Dm), pl.BlockSpec(w.shape, lambda i: (0, 0)), _row_spec(tm, Dm),
                  _vec_spec(Dm), _vec_spec(Dm)],
        out_specs=[_row_spec(tm, Dm), _vec_spec(Dm), _vec_spec(Dm), _vec_spec(Dm)],
        compiler_params=pltpu.CompilerParams(dimension_semantics=("arbitrary",)),
    )(dy, w, hd, ln_g, ln_b)


def _conv_bwd(dhd, hg, w_dw, name):
    R, Dm = hg.shape
    tm = _tile(R, CONV_ROWS, CONV_HALO)
    nblk = R // tm
    prev_spec, next_spec = _halo_specs(tm, nblk, Dm)

    def body(dprev, dcur, dnext, gprev, gcur, gnext, w_ref, dhg_ref, dw_ref, dscr, gscr, dwp, stage):
        i = pl.program_id(0)

        @pl.when(i == 0)
        def _():
            dwp[...] = jnp.zeros_like(dwp)

        _fill_halo(dscr, dprev, dcur, dnext, i, nblk, tm)
        _fill_halo(gscr, gprev, gcur, gnext, i, nblk, tm)
        for cb in range(Dm // LANES):
            cols = slice(cb * LANES, (cb + 1) * LANES)
            acc = jnp.zeros((tm, LANES), F32)
            for off, win in _windows(dscr, stage, cols, tm):
                j = CONV_WIDTH - off
                acc = acc + w_ref[j:j + 1, cols] * win
            dhg_ref[:, cols] = acc
            d_here = dcur[:, cols]
            for off, win in _windows(gscr, stage, cols, tm):
                j = off - 1
                prod = d_here * win
                part = prod[0:SUBLANES]
                for k in range(1, tm // SUBLANES):
                    part = part + prod[k * SUBLANES:(k + 1) * SUBLANES]
                dwp[j * SUBLANES:(j + 1) * SUBLANES, cols] += part

        @pl.when(i == nblk - 1)
        def _():
            for j in range(CONV_WIDTH):
                dw_ref[j:j + 1, :] = jnp.sum(dwp[j * SUBLANES:(j + 1) * SUBLANES, :], axis=0, keepdims=True)

    return pl.pallas_call(
        body, name=name, grid=(nblk,),
        out_shape=[jax.ShapeDtypeStruct((R, Dm), F32), jax.ShapeDtypeStruct((CONV_WIDTH, Dm), F32)],
        in_specs=[prev_spec, _row_spec(tm, Dm), next_spec, prev_spec, _row_spec(tm, Dm), next_spec,
                  pl.BlockSpec((CONV_WIDTH, Dm), lambda i: (0, 0))],
        out_specs=[_row_spec(tm, Dm), pl.BlockSpec((CONV_WIDTH, Dm), lambda i: (0, 0))],
        scratch_shapes=[pltpu.VMEM((tm + 2 * CONV_HALO, Dm), F32)] * 2
        + [pltpu.VMEM((CONV_WIDTH * SUBLANES, Dm), F32), pltpu.VMEM((SUBLANES, tm + CONV_REACH, LANES), F32)],
        compiler_params=pltpu.CompilerParams(dimension_semantics=("arbitrary",)),
    )(dhd, dhd, dhd, hg, hg, hg, w_dw)


def _swap16(y, lane):
    return jnp.where((lane & 16) == 0, pltpu.roll(y, LANES - 16, 1), pltpu.roll(y, 16, 1))


def _head_mean(v, bd):
    hi, lo = _split_bf16(v)
    return (_dot(hi, bd, NN) + _dot(lo, bd, NN)) * (1.0 / HEAD_DIM)


Q_COLS = (0, ATTN_WIDTH)
K_COLS = (ATTN_WIDTH, ATTN_WIDTH + HEAD_DIM * 2)
V_COLS = (K_COLS[1], K_COLS[1] + HEAD_DIM * 2)
SU_COLS = (V_COLS[1], V_COLS[1] + SG_WIDTH)
SV_COLS = (SU_COLS[1], SU_COLS[1] + SG_WIDTH)


def _mix_prep_fwd(p, ctx_rows, cos, sin, qg, kg, bd, w_sp, b_spt, name):
    TT = p.shape[0]
    off = ctx_rows // CHUNK
    q_scale = HEAD_DIM ** -0.5

    def body(p_ref, cos_ref, sin_ref, qg_ref, kg_ref, bd_ref, w_ref, b_ref,
             q_ref, kp_ref, vp_ref, kt_ref, sg_ref):
        lane = lax.broadcasted_iota(jnp.int32, (CHUNK, LANES), 1)
        low = lane < HEAD_DIM
        cs, sn, bdv = cos_ref[...], sin_ref[...], bd_ref[...]

        def norm_rope(xv, gain):
            r = lax.rsqrt(_head_mean(xv * xv, bdv) + EPS)
            yv = xv * r * gain
            return yv * cs + _swap16(yv, lane) * sn

        def pad_heads(ref, t):
            tr = pltpu.roll(t, HEAD_DIM, 1)
            ref[0, 0] = jnp.where(low, t, 0.0).astype(BF16)
            ref[0, 1] = jnp.where(low, 0.0, tr).astype(BF16)
            ref[1, 0] = jnp.where(low, tr, 0.0).astype(BF16)
            ref[1, 1] = jnp.where(low, 0.0, t).astype(BF16)

        for a in range(ATTN_WIDTH // LANES):
            xv = p_ref[:, a * LANES:(a + 1) * LANES]
            q_ref[:, a * LANES:(a + 1) * LANES] = (norm_rope(xv, qg_ref[...]) * q_scale).astype(BF16)
        kh = norm_rope(p_ref[:, K_COLS[0]:K_COLS[1]], kg_ref[...])
        pad_heads(kp_ref, kh)
        pad_heads(vp_ref, p_ref[:, V_COLS[0]:V_COLS[1]])
        kht = kh.T
        kt_ref[0] = kht[:HEAD_DIM].astype(BF16)
        kt_ref[1] = kht[HEAD_DIM:].astype(BF16)
        for g in range(N_SG_GROUPS):
            u = _gelu(p_ref[:, SU_COLS[0] + g * LANES:SU_COLS[0] + (g + 1) * LANES])
            vg = _gelu(p_ref[:, SV_COLS[0] + g * LANES:SV_COLS[0] + (g + 1) * LANES])
            xc = vg - jnp.mean(vg, axis=-1, keepdims=True)
            vn = xc * lax.rsqrt(jnp.mean(xc * xc, axis=-1, keepdims=True) + EPS)
            mixed = _dot(w_ref[g].astype(BF16), vn.astype(BF16), NN) + b_ref[:, g:g + 1]
            sg_ref[:, g * LANES:(g + 1) * LANES] = (u * mixed).astype(BF16)

    def row(width):
        return pl.BlockSpec((CHUNK, width), lambda i: (i, 0))

    def whole(shape):
        return pl.BlockSpec(shape, lambda i: (0,) * len(shape))

    pad_spec = pl.BlockSpec((2, 2, CHUNK, LANES), lambda i: (0, 0, i, 0))
    return pl.pallas_call(
        body, name=name, grid=(TT // CHUNK,),
        out_shape=[jax.ShapeDtypeStruct((TT, ATTN_WIDTH), BF16),
                   jax.ShapeDtypeStruct((2, 2, TT, LANES), BF16), jax.ShapeDtypeStruct((2, 2, TT, LANES), BF16),
                   jax.ShapeDtypeStruct((2, HEAD_DIM, TT), BF16),
                   jax.ShapeDtypeStruct((TT - ctx_rows, ATTN_WIDTH + SG_WIDTH), BF16)],
        in_specs=[row(IN_WIDTH), row(LANES), row(LANES), whole((1, LANES)), whole((1, LANES)),
                  whole((LANES, LANES)), whole((N_SG_GROUPS, CHUNK, CHUNK)), whole((CHUNK, N_SG_GROUPS))],
        out_specs=[row(ATTN_WIDTH), pad_spec, pad_spec,
                   pl.BlockSpec((2, HEAD_DIM, CHUNK), lambda i: (0, 0, i)),
                   pl.BlockSpec((CHUNK, SG_WIDTH), lambda i: (jnp.maximum(i - off, 0), 1))],
        compiler_params=pltpu.CompilerParams(dimension_semantics=("arbitrary",)),
    )(p, cos, sin, qg, kg, bd, w_sp, b_spt)


def _mix_prep_bwd(p, dq, f, dao, ctx_rows, cos, sin, qg, kg, bd, w_sp, w_spt, b_spt, name):
    TT = p.shape[0]
    off = ctx_rows // CHUNK
    q_scale = HEAD_DIM ** -0.5

    def body(p_ref, dq_ref, f_ref, dsg_ref, cos_ref, sin_ref, qg_ref, kg_ref, bd_ref, w_ref, wt_ref,
             b_ref, dp_ref, dqg_ref, dkg_ref, dw_ref, db_ref):
        i = pl.program_id(0)

        @pl.when(i == 0)
        def _():
            dqg_ref[...] = jnp.zeros_like(dqg_ref)
            dkg_ref[...] = jnp.zeros_like(dkg_ref)
            dw_ref[...] = jnp.zeros_like(dw_ref)
            db_ref[...] = jnp.zeros_like(db_ref)

        latent = (i >= off).astype(F32)
        lane = lax.broadcasted_iota(jnp.int32, (CHUNK, LANES), 1)
        low = lane < HEAD_DIM
        cs, sn, bdv = cos_ref[...], sin_ref[...], bd_ref[...]

        def fold(b0):
            return jnp.where(low, f_ref[0, b0] + pltpu.roll(f_ref[0, b0 + 1], HEAD_DIM, 1),
                             pltpu.roll(f_ref[1, b0], HEAD_DIM, 1) + f_ref[1, b0 + 1])

        def norm_rope_bwd(xv, dout, gain):
            r = lax.rsqrt(_head_mean(xv * xv, bdv) + EPS)
            n = xv * r
            dy = dout * cs + _swap16(dout * sn, lane)
            dn = dy * gain
            dx = r * (dn - n * _head_mean(dn * n, bdv))
            return dx, jnp.sum(dy * n, axis=0, keepdims=True)

        for a in range(ATTN_WIDTH // LANES):
            cols = slice(a * LANES, (a + 1) * LANES)
            dx, dg = norm_rope_bwd(p_ref[:, cols], dq_ref[:, cols] * (latent * q_scale), qg_ref[...])
            dp_ref[:, cols] = dx.astype(BF16)
            dqg_ref[...] += dg
        dx, dg = norm_rope_bwd(p_ref[:, K_COLS[0]:K_COLS[1]], fold(0), kg_ref[...])
        dp_ref[:, K_COLS[0]:K_COLS[1]] = dx.astype(BF16)
        dkg_ref[...] += dg
        dp_ref[:, V_COLS[0]:V_COLS[1]] = fold(2).astype(BF16)
        for g in range(N_SG_GROUPS):
            su = p_ref[:, SU_COLS[0] + g * LANES:SU_COLS[0] + (g + 1) * LANES]
            sv = p_ref[:, SV_COLS[0] + g * LANES:SV_COLS[0] + (g + 1) * LANES]
            (u, dgelu_su), (vg, dgelu_sv) = _gelu_and_grad(su), _gelu_and_grad(sv)
            xc = vg - jnp.mean(vg, axis=-1, keepdims=True)
            rs = lax.rsqrt(jnp.mean(xc * xc, axis=-1, keepdims=True) + EPS)
            vn = xc * rs
            vnb = vn.astype(BF16)
            mixed = _dot(w_ref[g].astype(BF16), vnb, NN) + b_ref[:, g:g + 1]
            dsg = dsg_ref[:, g * LANES:(g + 1) * LANES].astype(F32) * latent
            du = dsg * mixed
            dmix = dsg * u
            dmb = dmix.astype(BF16)
            db_ref[:, g:g + 1] += jnp.sum(dmix, axis=-1, keepdims=True)
            dw_ref[g] += _dot(dmb, vnb, NT)
            dvn = _dot(wt_ref[g].astype(BF16), dmb, NN)
            dvg = rs * (dvn - jnp.mean(dvn, axis=-1, keepdims=True)
                        - vn * jnp.mean(dvn * vn, axis=-1, keepdims=True))
            dp_ref[:, SU_COLS[0] + g * LANES:SU_COLS[0] + (g + 1) * LANES] = (du * dgelu_su).astype(BF16)
            dp_ref[:, SV_COLS[0] + g * LANES:SV_COLS[0] + (g + 1) * LANES] = (dvg * dgelu_sv).astype(BF16)

    def row(width):
        return pl.BlockSpec((CHUNK, width), lambda i: (i, 0))

    def latent_row(width, col_block):
        return pl.BlockSpec((CHUNK, width), lambda i: (jnp.maximum(i - off, 0), col_block))

    def whole(shape):
        return pl.BlockSpec(shape, lambda i: (0,) * len(shape))

    return pl.pallas_call(
        body, name=name, grid=(TT // CHUNK,),
        out_shape=[jax.ShapeDtypeStruct((TT, IN_WIDTH), BF16), jax.ShapeDtypeStruct((1, LANES), F32),
                   jax.ShapeDtypeStruct((1, LANES), F32),
                   jax.ShapeDtypeStruct((N_SG_GROUPS, CHUNK, CHUNK), F32),
                   jax.ShapeDtypeStruct((CHUNK, N_SG_GROUPS), F32)],
        in_specs=[row(IN_WIDTH), latent_row(ATTN_WIDTH, 0),
                  pl.BlockSpec((2, 4, CHUNK, LANES), lambda i: (0, 0, i, 0)),
                  latent_row(SG_WIDTH, 1), row(LANES), row(LANES), whole((1, LANES)), whole((1, LANES)),
                  whole((LANES, LANES)), whole((N_SG_GROUPS, CHUNK, CHUNK)),
                  whole((N_SG_GROUPS, CHUNK, CHUNK)), whole((CHUNK, N_SG_GROUPS))],
        out_specs=[row(IN_WIDTH), whole((1, LANES)), whole((1, LANES)),
                   whole((N_SG_GROUPS, CHUNK, CHUNK)), whole((CHUNK, N_SG_GROUPS))],
        compiler_params=pltpu.CompilerParams(dimension_semantics=("arbitrary",)),
    )(p, dq, f, dao, cos, sin, qg, kg, bd, w_sp, w_spt, b_spt)


def _attn_fwd(q, kpad, vpad, ao, ctx_rows, name, tq=256):
    TT = q.shape[0]
    T = TT - ctx_rows
    tq = _tile(T, tq)
    off = ctx_rows // tq
    group = 2 * LANES

    def body(q_ref, k_ref, v_ref, ao_in, o_ref, lse_ref):
        del ao_in
        lane = lax.broadcasted_iota(jnp.int32, (tq, LANES), 1)
        lse = jnp.zeros((tq, LANES), F32)
        for a in range(2):
            acc = jnp.zeros((tq, LANES), F32)
            qa = q_ref[:, a * LANES:(a + 1) * LANES]
            for b in range(2):
                s = _dot(qa, k_ref[0, b], NT)
                m = jnp.max(s, axis=-1, keepdims=True)
                e = jnp.exp(s - m)
                l = jnp.sum(e, axis=-1, keepdims=True)
                acc = acc + _dot(e.astype(BF16), v_ref[0, b], NN) * (1.0 / l)
                lse = jnp.where(lane == 2 * a + b, m + jnp.log(l), lse)
            o_ref[:, a * LANES:(a + 1) * LANES] = acc.astype(BF16)
        lse_ref[0] = lse

    kv_spec = pl.BlockSpec((1, 2, TT, LANES), lambda j, i: (j, 0, 0, 0))
    return pl.pallas_call(
        body, name=name, grid=(2, T // tq),
        out_shape=[jax.ShapeDtypeStruct(ao.shape, BF16), jax.ShapeDtypeStruct((2, T, LANES), F32)],
        in_specs=[pl.BlockSpec((tq, group), lambda j, i: (i + off, j)), kv_spec, kv_spec,
                  pl.BlockSpec(memory_space=pl.ANY)],
        out_specs=[pl.BlockSpec((tq, group), lambda j, i: (i, j)),
                   pl.BlockSpec((1, tq, LANES), lambda j, i: (j, i, 0))],
        input_output_aliases={3: 0},
        compiler_params=pltpu.CompilerParams(dimension_semantics=("parallel", "parallel")),
    )(q, kpad, vpad, ao)


def _attn_bwd(q, dao, ao, lse, kpad, vpad, kt, ctx_rows, name, tq=256):
    TT = q.shape[0]
    T = TT - ctx_rows
    tq = _tile(T, tq)
    off = ctx_rows // tq
    group = 2 * LANES

    def body(q_ref, do_ref, o_ref, lse_ref, k_ref, v_ref, kt_ref, dq_ref, f_ref):
        i = pl.program_id(1)

        @pl.when(i == 0)
        def _():
            f_ref[...] = jnp.zeros_like(f_ref)

        ktv = kt_ref[0]
        lse_t = lse_ref[0].T
        row = lax.broadcasted_iota(jnp.int32, (SUBLANES, LANES), 0)
        lane = lax.broadcasted_iota(jnp.int32, (SUBLANES, LANES), 1)
        half_ones = (jnp.where(lane < HEAD_DIM, 0, 1) == row).astype(BF16)
        for a in range(2):
            cols = slice(a * LANES, (a + 1) * LANES)
            qa = q_ref[:, cols]
            do32 = do_ref[:, cols].astype(F32)
            doa = do32.astype(BF16)
            hi, lo = _split_bf16(do32 * o_ref[:, cols].astype(F32))
            deltas = _dot(half_ones, hi, NT) + _dot(half_ones, lo, NT)
            halves = []
            for b in range(2):
                h = 2 * a + b
                st = _dot(k_ref[0, b], qa, NT)
                pt = jnp.exp(st - lse_t[h:h + 1, :])
                dpt = _dot(v_ref[0, b], doa, NT)
                dst = (pt * (dpt - deltas[b:b + 1, :])).astype(BF16)
                f_ref[0, b] += _dot(dst, qa, NN)
                f_ref[0, 2 + b] += _dot(pt.astype(BF16), doa, NN)
                halves.append(_dot(ktv, dst, NN))
            dq_ref[:, cols] = jnp.concatenate(halves, axis=0).T

    kv_spec = pl.BlockSpec((1, 2, TT, LANES), lambda j, i: (j, 0, 0, 0))
    out_cols = pl.BlockSpec((tq, group), lambda j, i: (i, j))
    return pl.pallas_call(
        body, name=name, grid=(2, T // tq),
        out_shape=[jax.ShapeDtypeStruct((T, ATTN_WIDTH), F32), jax.ShapeDtypeStruct((2, 4, TT, LANES), F32)],
        in_specs=[pl.BlockSpec((tq, group), lambda j, i: (i + off, j)), out_cols, out_cols,
                  pl.BlockSpec((1, tq, LANES), lambda j, i: (j, i, 0)),
                  kv_spec, kv_spec, pl.BlockSpec((1, HEAD_DIM, TT), lambda j, i: (j, 0, 0))],
        out_specs=[out_cols, pl.BlockSpec((1, 4, TT, LANES), lambda j, i: (j, 0, 0, 0))],
        compiler_params=pltpu.CompilerParams(dimension_semantics=("parallel", "arbitrary")),
    )(q, dao, ao, lse, kpad, vpad, kt)


def _final_fwd_bwd(h, g, target, y, gt, name):
    R, Dm = h.shape
    tm = _tile(R, ROW_BLOCK, 8)

    def body(h_ref, g_ref, t_ref, y_ref, gt_ref, dh_ref, loss_ref, dg_ref, dy_ref, dgt_ref, dsum_ref):
        i = pl.program_id(0)

        @pl.when(i == 0)
        def _():
            for ref in (loss_ref, dg_ref, dgt_ref, dsum_ref):
                ref[...] = jnp.zeros_like(ref)

        hv = h_ref[...]
        r = lax.rsqrt(jnp.mean(hv * hv, axis=-1, keepdims=True) + EPS)
        n = hv * r
        diff = n * g_ref[...] - t_ref[...]
        loss_ref[...] += jnp.sum(diff * diff)
        dout = diff * (1.0 / Dm)
        dg_ref[...] += jnp.sum(dout * n, axis=0, keepdims=True)
        dn = dout * g_ref[...]
        dh = r * (dn - n * jnp.mean(dn * n, axis=-1, keepdims=True))
        dh_ref[...] = dh
        _gate_grads(dh, y_ref, gt_ref, dy_ref, dgt_ref, dsum_ref)

    vec = jax.ShapeDtypeStruct((1, Dm), F32)
    return pl.pallas_call(
        body, name=name, grid=(R // tm,),
        out_shape=[jax.ShapeDtypeStruct((R, Dm), F32), jax.ShapeDtypeStruct((1, LANES), F32), vec,
                   jax.ShapeDtypeStruct((R, Dm), BF16), vec, vec],
        in_specs=[_row_spec(tm, Dm), _vec_spec(Dm), _row_spec(tm, Dm), _row_spec(tm, Dm), _vec_spec(Dm)],
        out_specs=[_row_spec(tm, Dm), _vec_spec(LANES), _vec_spec(Dm), _row_spec(tm, Dm), _vec_spec(Dm),
                   _vec_spec(Dm)],
        compiler_params=pltpu.CompilerParams(dimension_semantics=("arbitrary",)),
    )(h, g, target, y, gt)


MOD_ROWS = 16


def _mod_fwd(c_rows, w_mod, name):
    L, Dm, n = w_mod.shape

    def body(c_ref, w_ref, o_ref):
        o_ref[0] = _dot3(_silu(c_ref[...]), w_ref[0], NN)

    return pl.pallas_call(
        body, name=name, grid=(L,),
        out_shape=jax.ShapeDtypeStruct((L, MOD_ROWS, n), F32),
        in_specs=[pl.BlockSpec((MOD_ROWS, Dm), lambda l: (0, 0)), pl.BlockSpec((1, Dm, n), lambda l: (l, 0, 0))],
        out_specs=pl.BlockSpec((1, MOD_ROWS, n), lambda l: (l, 0, 0)),
        compiler_params=pltpu.CompilerParams(dimension_semantics=("parallel",)),
    )(c_rows, w_mod)


def _mod_bwd(c_rows_t, dmod, w_mod, name):
    L, Dm, n = w_mod.shape

    def body(ct_ref, d_ref, w_ref, gw_ref, ds_ref):
        dm = d_ref[0]
        gw_ref[0] = _dot3(_silu(ct_ref[...]), dm, NN)
        ds_ref[0] = _dot3(dm[:MOD_ROWS], w_ref[0], NT)

    return pl.pallas_call(
        body, name=name, grid=(L,),
        out_shape=[jax.ShapeDtypeStruct((L, Dm, n), F32), jax.ShapeDtypeStruct((L, MOD_ROWS, Dm), F32)],
        in_specs=[pl.BlockSpec((Dm, LANES), lambda l: (0, 0)), pl.BlockSpec((1, LANES, n), lambda l: (l, 0, 0)),
                  pl.BlockSpec((1, Dm, n), lambda l: (l, 0, 0))],
        out_specs=[pl.BlockSpec((1, Dm, n), lambda l: (l, 0, 0)),
                   pl.BlockSpec((1, MOD_ROWS, Dm), lambda l: (l, 0, 0))],
        compiler_params=pltpu.CompilerParams(dimension_semantics=("parallel",)),
    )(c_rows_t, dmod, w_mod)


def _adam_update(w, g, m, v):
    c1 = 1.0 - ADAM_B1 ** ADAM_STEP
    c2 = 1.0 - ADAM_B2 ** ADAM_STEP
    mn = ADAM_B1 * m + (1.0 - ADAM_B1) * g
    vn = ADAM_B2 * v + (1.0 - ADAM_B2) * (g * g)
    return -ADAM_LR * ((mn / c1) / (jnp.sqrt(vn / c2) + ADAM_EPS) + ADAM_WD * w), mn, vn


def _adamw(w, g, m, v, name):
    R, Cw = w.shape
    tm = _tile(R, ADAM_ROWS, 8)

    def body(w_ref, g_ref, m_ref, v_ref, d_ref, mo_ref, vo_ref):
        d_ref[...], mo_ref[...], vo_ref[...] = _adam_update(w_ref[...], g_ref[...], m_ref[...], v_ref[...])

    spec = pl.BlockSpec((tm, Cw), lambda i: (i, 0))
    return pl.pallas_call(
        body, name=name, grid=(R // tm,),
        out_shape=[jax.ShapeDtypeStruct((R, Cw), F32)] * 3,
        in_specs=[spec] * 4, out_specs=[spec] * 3,
        compiler_params=pltpu.CompilerParams(dimension_semantics=("parallel",)),
    )(w, g, m, v)


def _adamw_recv(w, m, v, recvs, name):
    L, R, n = w.shape
    tm = _tile(R, ADAM_ROWS, 8)
    nblk = R // tm
    parts = [r.reshape(N_DEV, R, n) for r in recvs]

    def body(*refs):
        w_ref, m_ref, v_ref = refs[:3]
        part_refs = refs[3:3 + L]
        g_ref, d_ref, mo_ref, vo_ref, gsum = refs[3 + L:]
        l = pl.program_id(0)
        for ll in range(L):
            @pl.when(l == ll)
            def _(ll=ll):
                acc = part_refs[ll][0].astype(F32)
                for s in range(1, N_DEV):
                    acc = acc + part_refs[ll][s].astype(F32)
                gsum[...] = acc
        g = gsum[...]
        g_ref[0] = g
        d_ref[0], mo_ref[0], vo_ref[0] = _adam_update(w_ref[0], g, m_ref[0], v_ref[0])

    def part_spec(ll):
        return pl.BlockSpec((N_DEV, tm, n), lambda l, i: (0, jnp.where(l == ll, i, jnp.where(l < ll, 0, nblk - 1)), 0))

    spec = pl.BlockSpec((1, tm, n), lambda l, i: (l, i, 0))
    return pl.pallas_call(
        body, name=name, grid=(L, nblk),
        out_shape=[jax.ShapeDtypeStruct((L, R, n), F32)] * 4,
        in_specs=[spec] * 3 + [part_spec(ll) for ll in range(L)], out_specs=[spec] * 4,
        scratch_shapes=[pltpu.VMEM((tm, n), F32)],
        compiler_params=pltpu.CompilerParams(dimension_semantics=("parallel", "parallel")),
    )(w, m, v, *parts)


def _pack(parts, row_mult=8):
    flat, offs, pos = [], [], 0
    for t in parts:
        t = t.reshape(-1).astype(F32)
        size = -(-t.shape[0] // LANES) * LANES
        flat.append(jnp.pad(t, (0, size - t.shape[0])))
        offs.append(pos)
        pos += size
    total = -(-pos // (LANES * row_mult)) * (LANES * row_mult)
    if total > pos:
        flat.append(jnp.zeros((total - pos,), F32))
    return jnp.concatenate(flat).reshape(-1, LANES), offs


def _take(buf, off, shape):
    size = math.prod(shape)
    return buf[..., off:off + size].reshape(buf.shape[:-1] + tuple(shape))


def _rope_tables(T, ctx_rows):
    pos = jnp.arange(T)
    row = (pos // GRID_W).astype(F32)
    col = (pos % GRID_W).astype(F32)
    half = HEAD_DIM // 4
    inv = ROPE_THETA ** (-jnp.arange(0, 2 * half, 2, dtype=F32) / (2 * half))
    ang_r, ang_c = row[:, None] * inv[None, :], col[:, None] * inv[None, :]
    cos = jnp.concatenate([jnp.cos(ang_r)] * 2 + [jnp.cos(ang_c)] * 2, axis=1)
    sin = jnp.concatenate([-jnp.sin(ang_r), jnp.sin(ang_r), -jnp.sin(ang_c), jnp.sin(ang_c)], axis=1)
    cos = jnp.concatenate([jnp.ones((ctx_rows, HEAD_DIM), F32), cos], axis=0)
    sin = jnp.concatenate([jnp.zeros((ctx_rows, HEAD_DIM), F32), sin], axis=0)
    return jnp.tile(cos, (1, 2)), jnp.tile(sin, (1, 2))


def kernel(x, c, ctx, c_ctx, w_mod, b_mod, g_mix, g_ffn, w_ffn_in, w_ffn_out, w_in, q_gain, k_gain, w_sp, b_sp, w_out, w_pw1, b_pw1, w_dw, b_dw, ln_g, ln_b, w_pw2, b_pw2, g_final, loss_target, m_c_ctx, m_w_mod, m_b_mod, m_g_mix, m_g_ffn, m_w_ffn_in, m_w_ffn_out, m_w_in, m_q_gain, m_k_gain, m_w_sp, m_b_sp, m_w_out, m_w_pw1, m_b_pw1, m_w_dw, m_b_dw, m_ln_g, m_ln_b, m_w_pw2, m_b_pw2, m_g_final, v_c_ctx, v_w_mod, v_b_mod, v_g_mix, v_g_ffn, v_w_ffn_in, v_w_ffn_out, v_w_in, v_q_gain, v_k_gain, v_w_sp, v_b_sp, v_w_out, v_w_pw1, v_b_pw1, v_w_dw, v_b_dw, v_ln_g, v_ln_b, v_w_pw2, v_b_pw2, v_g_final):
    weights = dict(c_ctx=c_ctx, w_mod=w_mod, b_mod=b_mod, g_mix=g_mix, g_ffn=g_ffn, w_ffn_in=w_ffn_in,
                   w_ffn_out=w_ffn_out, w_in=w_in, q_gain=q_gain, k_gain=k_gain, w_sp=w_sp, b_sp=b_sp,
                   w_out=w_out, w_pw1=w_pw1, b_pw1=b_pw1, w_dw=w_dw, b_dw=b_dw, ln_g=ln_g, ln_b=ln_b,
                   w_pw2=w_pw2, b_pw2=b_pw2, g_final=g_final)
    moments_m = dict(c_ctx=m_c_ctx, w_mod=m_w_mod, b_mod=m_b_mod, g_mix=m_g_mix, g_ffn=m_g_ffn,
                     w_ffn_in=m_w_ffn_in, w_ffn_out=m_w_ffn_out, w_in=m_w_in, q_gain=m_q_gain,
                     k_gain=m_k_gain, w_sp=m_w_sp, b_sp=m_b_sp, w_out=m_w_out, w_pw1=m_w_pw1,
                     b_pw1=m_b_pw1, w_dw=m_w_dw, b_dw=m_b_dw, ln_g=m_ln_g, ln_b=m_ln_b, w_pw2=m_w_pw2,
                     b_pw2=m_b_pw2, g_final=m_g_final)
    moments_v = dict(c_ctx=v_c_ctx, w_mod=v_w_mod, b_mod=v_b_mod, g_mix=v_g_mix, g_ffn=v_g_ffn,
                     w_ffn_in=v_w_ffn_in, w_ffn_out=v_w_ffn_out, w_in=v_w_in, q_gain=v_q_gain,
                     k_gain=v_k_gain, w_sp=v_w_sp, b_sp=v_b_sp, w_out=v_w_out, w_pw1=v_w_pw1,
                     b_pw1=v_b_pw1, w_dw=v_w_dw, b_dw=v_b_dw, ln_g=v_ln_g, ln_b=v_ln_b, w_pw2=v_w_pw2,
                     b_pw2=v_b_pw2, g_final=v_g_final)
    names = list(weights)

    T, C = x.shape[1], ctx.shape[1]
    Dm = D_MODEL
    me = 4 * lax.axis_index("x") + 2 * lax.axis_index("y") + lax.axis_index("c")
    h0 = x[0]
    ctx2 = ctx[0]
    target = loss_target[0]

    small_sharded = (("w_dw", w_dw[0]), ("b_pw1", b_pw1), ("b_dw", b_dw), ("ln_g", ln_g), ("ln_b", ln_b),
                     ("b_pw2", b_pw2))
    buf1, offs1 = _pack([c] + [t for _, t in small_sharded])
    w_in_t, m_w_in_t, v_w_in_t = (jnp.swapaxes(t, 1, 2) for t in (w_in, m_w_in, v_w_in))
    w_ffi_t, m_w_ffi_t, v_w_ffi_t = (jnp.swapaxes(t, 1, 2) for t in (w_ffn_in, m_w_ffn_in, v_w_ffn_in))
    got1, W_in_t = _all_gather([buf1, w_in_t[0].astype(BF16)], "gather_cond", False)
    got1 = got1.reshape(N_DEV, -1)
    c_all = _take(got1, offs1[0], (Dm,))
    full_small = {}
    for (nm, t), off in zip(small_sharded, offs1[1:]):
        seg = _take(got1, off, t.shape)
        full_small[nm] = jnp.moveaxis(seg, 0, -2).reshape(t.shape[:-1] + (N_DEV * t.shape[-1],))
    w_dw_f, b_pw1_f = full_small["w_dw"], full_small["b_pw1"]
    b_dw_f, ln_g_f, ln_b_f, b_pw2_f = (full_small[k] for k in ("b_dw", "ln_g", "ln_b", "b_pw2"))

    c_rows = jnp.concatenate([c_all, c_ctx[None, :], jnp.zeros((MOD_ROWS - N_DEV - 1, Dm), F32)], axis=0)
    mod_part = _mod_fwd(c_rows, w_mod, "mod_fwd")
    n_mod = w_mod.shape[2]
    got2 = _all_gather([mod_part.reshape(-1, LANES)], "gather_mod", True)[0]
    mod_all = got2.reshape(N_DEV, 2, MOD_ROWS, n_mod).transpose(1, 2, 0, 3).reshape(2, MOD_ROWS, N_DEV * n_mod)
    mod_all = mod_all + b_mod[:, None, :]
    my_mod = lax.dynamic_index_in_dim(mod_all, me, axis=1, keepdims=False)
    sh1, sc1, gt1, sh2, sc2, gt2 = ([my_mod[l:l + 1, k * Dm:(k + 1) * Dm] for l in range(2)] for k in range(6))
    csh1 = mod_all[0, N_DEV:N_DEV + 1, 0:Dm]
    csc1 = mod_all[0, N_DEV:N_DEV + 1, Dm:2 * Dm]

    behind = got2[0:1, 0:1] * 0.0
    gather_groups = [[w_out[0]], [w_ffi_t[0], w_ffn_out[0]], [w_pw1[0], w_pw2[0]], [w_ffi_t[1], w_ffn_out[1]]]
    gathers = [_push_begin([(t + behind).astype(BF16) for t in grp], True, f"gather_start{k}")
               for k, grp in enumerate(gather_groups)]
    started = sum(h[4][0:1, 0:1] for h in gathers)

    def gathered(k, after):
        return _push_end(gathers[k], after, f"gather_wait{k}")[1]

    def ffn_weights(k, after):
        wi, wo = gathered(k, after)
        return wi.reshape(N_DEV, FF_SHARD, Dm), wo.reshape(N_DEV // 2, FF_SHARD, Dm)

    def col_gathered(t, n):
        return t.reshape(N_DEV, Dm, n).transpose(1, 0, 2).reshape(Dm, N_DEV * n)

    W_ffi, W_ffo = [None, None], [None, None]

    g_mix_r = [g_mix[l:l + 1] for l in range(2)]
    g_ffn_r = [g_ffn[l:l + 1] for l in range(2)]
    g_fin = g_final[None, :]

    cos, sin = _rope_tables(T, C)
    qg = jnp.tile(q_gain, (1, 2))
    kg = jnp.tile(k_gain, (1, 2))
    lane_head = jnp.arange(LANES) // HEAD_DIM
    bd = (lane_head[:, None] == lane_head[None, :]).astype(BF16)
    w_sp0 = w_sp[0]
    w_spt0 = w_sp0.transpose(0, 2, 1)
    b_spt0 = b_sp[0].T

    XM = _norm_mod_fwd_cat(ctx2, h0, g_mix_r[0], csc1, csh1, sc1[0] + started, sh1[0], "norm_mix0")
    P = _mm(XM, W_in_t, "nt", "in_proj", tm=1088, tn=IN_WIDTH)
    qh, kpad, vpad, kt, ao = _mix_prep_fwd(P, C, cos, sin, qg, kg, bd, w_sp0, b_spt0, "mix_prep")
    ao, lse = _attn_fwd(qh, kpad, vpad, ao, C, "attn_fwd")
    W_out, = gathered(0, ao)
    h1, y0, xf0 = _mm(ao, W_out, "nn", "out_proj", res=h0, gate=gt1[0], raw_out=True,
                      norm=(g_ffn_r[0], sc2[0], sh2[0]))

    def ffn_fwd(h_in, xf, l, norm_next):
        W_ffi[l], W_ffo[l] = ffn_weights(1 + 2 * l, xf)
        gu, act = _ffn_in_swiglu(xf, W_ffi[l], f"ffn_in{l}")
        outs = _mm_sum_shards(act, W_ffo[l], "nn", f"ffn_out{l}", res=h_in, gate=gt2[l], raw_out=True,
                              norm=norm_next)
        return tuple(outs) + (None,) * (3 - len(outs)) + (gu, act)

    h2, f0, xm1, gu0, act0 = ffn_fwd(h1, xf0, 0, (g_mix_r[1], sc1[1], sh1[1]))

    W_pw1, W_pw2 = gathered(2, xm1)
    W_pw1 = col_gathered(W_pw1, 2 * Dm // N_DEV)
    ag, hg = _pw1_glu(xm1, W_pw1, b_pw1_f, "pw1")
    hd = _conv_fwd(hg, w_dw_f, b_dw_f, "conv")
    hs = _ln_silu_fwd(hd, ln_g_f, ln_b_f, "ln_silu")
    h3, y1, xf1 = _mm(hs, W_pw2, "nn", "pw2", bias=b_pw2_f, res=h2, gate=gt1[1], raw_out=True,
                      norm=(g_ffn_r[1], sc2[1], sh2[1]))
    h4, f1, _, gu1, act1 = ffn_fwd(h3, xf1, 1, None)

    dh4, sq_err, dg_final, df1, dgt2_1, _ = _final_fwd_bwd(h4, g_fin, target, f1, gt2[1], "loss_head")
    loss_local = (0.5 / Dm) * sq_err[0, 0:1]

    def col_shards(g, n):
        return g.reshape(Dm, N_DEV, n).transpose(1, 0, 2).reshape(N_DEV * Dm, n)

    def exchange_begin(k, parts):
        return _push_begin(parts, False, f"exchange_start{k}")

    def zero_of(handle):
        return handle[4][0:1, 0:1]

    def ffn_bwd(df, xf, gu, act, l):
        dw_out = _mm_tn_shard_rows(act, df, f"ffn_out_dw{l}", BF16)
        dgu = _ffn_out_dx_swiglu(df, W_ffo[l], gu, f"ffn_out_dx{l}").reshape(N_DEV, T, FF_SHARD)
        dw_in = _mm_tn_shard_rows(dgu, xf, f"ffn_in_dw{l}", BF16)
        dxf = _mm_sum_shards(dgu, W_ffi[l], "nn", f"ffn_in_dx{l}", BF16, tm=512)
        return dw_in, dw_out, dxf

    dW_ffi1, dW_ffo1, dxf1 = ffn_bwd(df1, xf1, gu1, act1, 1)
    ex0 = exchange_begin(0, [dW_ffi1.reshape(2 * D_FF, Dm), dW_ffo1.reshape(D_FF, Dm)])
    dh3, da, dsh, dy1, dgt1_1, db_pw2 = _norm_mod_bwd(h3, g_ffn_r[1], sc2[1], dxf1, dh4, "norm_ffn_bwd1",
                                                       gate=(y1, gt1[1] + zero_of(ex0)))
    dmod_ffn1 = (dsh, da * g_ffn_r[1], dgt2_1)
    dg_ffn1 = da * (1.0 + sc2[1])

    dW_pw2 = _mm(hs, dy1, "tn", "pw2_dw", BF16, tk=2048)
    dhd, dln_g, dln_b, db_dw = _pw2_dx_ln_silu_bwd(dy1, W_pw2, hd, ln_g_f, ln_b_f, "pw2_dx")
    dhg, dw_dw = _conv_bwd(dhd, hg, w_dw_f, "conv_bwd")
    dag, db_pw1, dxm1 = _glu_bwd_pw1_dx(ag, dhg, W_pw1, "pw1_dx")
    dW_pw1 = _mm(xm1, dag, "tn", "pw1_dw", BF16, tk=2048)
    ex1 = exchange_begin(1, [col_shards(dW_pw1, 2 * Dm // N_DEV), dW_pw2])
    dh2, da, dsh, df0, dgt2_0, _ = _norm_mod_bwd(h2, g_mix_r[1], sc1[1], dxm1, dh3, "norm_mix1_bwd",
                                                 gate=(f0, gt2[0] + zero_of(ex1)))
    dmod_mix1 = (dsh, da * g_mix_r[1], dgt1_1)
    dg_mix1 = da * (1.0 + sc1[1])

    dW_ffi0, dW_ffo0, dxf0 = ffn_bwd(df0, xf0, gu0, act0, 0)
    ex2 = exchange_begin(2, [dW_ffi0.reshape(2 * D_FF, Dm), dW_ffo0.reshape(D_FF, Dm)])
    dh1, da, dsh, dy0, dgt1_0, _ = _norm_mod_bwd(h1, g_ffn_r[0], sc2[0], dxf0, dh2, "norm_ffn_bwd0",
                                                 gate=(y0, gt1[0] + zero_of(ex2)))
    dmod_ffn0 = (dsh, da * g_ffn_r[0], dgt2_0)
    dg_ffn0 = da * (1.0 + sc2[0])

    dW_out = _mm(ao, dy0, "tn", "out_proj_dw", BF16, tk=2048)
    ex_out = exchange_begin(4, [dW_out])
    dao = _mm(dy0, W_out + zero_of(ex_out).astype(BF16), "nt", "out_proj_dx", BF16)
    dq, f_acc = _attn_bwd(qh, dao, ao, lse, kpad, vpad, kt, C, "attn_bwd")
    dP, dqg, dkg, dw_sp0, db_spt0 = _mix_prep_bwd(P, dq, f_acc, dao, C, cos, sin, qg, kg, bd, w_sp0, w_spt0,
                                                  b_spt0, "mix_prep_bwd")
    dW_in_t = _mm(dP, XM, "tn", "in_proj_dw", BF16, tm=896, tk=2176)
    dXM = _mm(dP, W_in_t, "nn", "in_proj_dx", BF16, tm=1088, tk=IN_WIDTH)
    dh0, da, dsh = _norm_mod_bwd(h0, g_mix_r[0], sc1[0], dXM, dh1, "norm_mix0_bwd", dxm_row_off=C)
    _, dac, dcsh = _norm_mod_bwd(ctx2, g_mix_r[0], csc1, dXM, None, "norm_ctx_bwd")
    dmod_mix0 = (dsh, da * g_mix_r[0], dgt1_0)
    dg_mix0 = da * (1.0 + sc1[0]) + dac * (1.0 + csc1)
    dcmod = jnp.concatenate([dcsh, dac * g_mix_r[0]], axis=1)

    dmod_mine = jnp.stack([jnp.concatenate(dmod_mix0 + dmod_ffn0, axis=1)[0],
                           jnp.concatenate(dmod_mix1 + dmod_ffn1, axis=1)[0]])

    small_grads = [
        ("loss", loss_local), ("g_final", dg_final), ("g_mix", jnp.concatenate([dg_mix0, dg_mix1])),
        ("g_ffn", jnp.concatenate([dg_ffn0, dg_ffn1])),
        ("q_gain", dqg[:, :HEAD_DIM] + dqg[:, HEAD_DIM:]), ("k_gain", dkg[:, :HEAD_DIM] + dkg[:, HEAD_DIM:]),
        ("w_sp", dw_sp0[None]), ("b_sp", db_spt0.T[None]), ("b_pw1", db_pw1), ("w_dw", dw_dw[None]),
        ("b_dw", db_dw), ("ln_g", dln_g), ("ln_b", dln_b), ("b_pw2", db_pw2), ("dcmod", dcmod),
        ("dmod", dmod_mine),
    ]
    buf3, offs3 = _pack([t for _, t in small_grads])
    off3 = {nm: off for (nm, _), off in zip(small_grads, offs3)}
    shape3 = {nm: t.shape for nm, t in small_grads}
    small_push = _push_begin([buf3], True, "small_grads_start")
    ex3 = exchange_begin(3, [dW_in_t + zero_of(small_push).astype(BF16)])

    grads, delta, new_m, new_v = {}, {}, {}, {}

    def exchanged(k, handle, after):
        return _push_end(handle, after, f"exchange_wait{k}")[1]

    def adamw_big(nm, parts, transposed=False, wmv=None):
        w3, m3, v3 = wmv if wmv is not None else (weights[nm], moments_m[nm], moments_v[nm])
        outs4 = _adamw_recv(w3, m3, v3, parts, f"adamw_{nm}")
        if transposed:
            outs4 = [jnp.swapaxes(t, 1, 2) for t in outs4]
        grads[nm], delta[nm], new_m[nm], new_v[nm] = outs4

    pushed = ex3[4]
    r_ffi1, r_ffo1 = exchanged(0, ex0, pushed)
    r_pw1, r_pw2 = exchanged(1, ex1, pushed)
    r_ffi0, r_ffo0 = exchanged(2, ex2, pushed)
    r_out, = exchanged(4, ex_out, pushed)
    adamw_big("w_ffn_in", [r_ffi0, r_ffi1], True, (w_ffi_t, m_w_ffi_t, v_w_ffi_t))
    adamw_big("w_ffn_out", [r_ffo0, r_ffo1])
    adamw_big("w_pw1", [r_pw1])
    adamw_big("w_pw2", [r_pw2])
    adamw_big("w_out", [r_out])

    got3 = _push_end(small_push, delta["w_out"], "small_grads_wait")[1][0].reshape(N_DEV, buf3.shape[0], LANES)
    sum3 = _sum_devices(got3, "sum_small_grads").reshape(-1)

    def summed(nm):
        return _take(sum3, off3[nm], shape3[nm])

    loss = summed("loss")[0]
    dcmod_sum = summed("dcmod")
    dmod_rows = _take(got3.reshape(N_DEV, -1), off3["dmod"], (2, 6 * Dm)).transpose(1, 0, 2)
    ctx_row = jnp.concatenate([jnp.pad(dcmod_sum, ((0, 0), (0, 4 * Dm))), jnp.zeros((1, 6 * Dm), F32)])
    dmod_all = jnp.concatenate([dmod_rows, ctx_row[:, None, :],
                                jnp.zeros((2, LANES - N_DEV - 1, 6 * Dm), F32)], axis=1)
    grads["b_mod"] = summed("dmod") + ctx_row
    dmod_shard = lax.dynamic_slice_in_dim(dmod_all, me * n_mod, n_mod, axis=2)
    c_rows_t = jnp.pad(c_rows.T, ((0, 0), (0, LANES - MOD_ROWS)))
    grads["w_mod"], ds_part = _mod_bwd(c_rows_t, dmod_shard, w_mod, "mod_bwd")

    buf4, _ = _pack([ds_part[0, N_DEV]])
    got4 = _all_gather([buf4], "gather_c_ctx_grad", True)[0].reshape(N_DEV, buf4.shape[0], LANES)
    ds_ctx = _sum_devices(got4, "sum_c_ctx_grad").reshape(-1)[:Dm]
    grads["c_ctx"] = ds_ctx * _dsilu(c_ctx)

    for nm in ("g_final", "g_mix", "g_ffn", "q_gain", "k_gain", "w_sp", "b_sp"):
        grads[nm] = summed(nm).reshape(weights[nm].shape)
    for nm in ("b_pw1", "w_dw", "b_dw", "ln_g", "ln_b", "b_pw2"):
        n_loc = weights[nm].shape[-1]
        grads[nm] = lax.dynamic_slice_in_dim(summed(nm), me * n_loc, n_loc, axis=-1).reshape(weights[nm].shape)

    shp = w_mod.shape
    outs = _adamw(w_mod.reshape(-1, shp[-1]), grads["w_mod"].reshape(-1, shp[-1]),
                  m_w_mod.reshape(-1, shp[-1]), v_w_mod.reshape(-1, shp[-1]), "adamw_w_mod")
    delta["w_mod"], new_m["w_mod"], new_v["w_mod"] = (o.reshape(shp) for o in outs)
    big_names = ("w_mod", "w_ffn_in", "w_ffn_out", "w_in", "w_out", "w_pw1", "w_pw2")
    small_names = [nm for nm in names if nm not in big_names]
    packs = [_pack([src[nm] for nm in small_names]) for src in (weights, grads, moments_m, moments_v)]
    offs_s = packs[0][1]
    outs = _adamw(*[pk[0] for pk in packs], "adamw_small")
    for o, dst in zip(outs, (delta, new_m, new_v)):
        o = o.reshape(-1)
        for nm, off in zip(small_names, offs_s):
            dst[nm] = _take(o, off, weights[nm].shape)
    r_in, = exchanged(3, ex3, outs[0])
    adamw_big("w_in", [r_in], True, (w_in_t, m_w_in_t, v_w_in_t))

    return (loss, dh0[None], *[grads[n] for n in names], *[delta[n] for n in names],
            *[new_m[n] for n in names], *[new_v[n] for n in names])
```

```python
import math

import jax
import jax.numpy as jnp
from jax import lax
from jax.experimental import pallas as pl
from jax.experimental.pallas import tpu as pltpu

F32 = jnp.float32
BF16 = jnp.bfloat16
MESH = pl.DeviceIdType.MESH

N_DEV = 8
D_MODEL = 1024
EPS = 1e-6
HEAD_DIM = 64
ATTN_WIDTH = 512
KV_WIDTH = 128
SG_WIDTH = 512
N_SG_GROUPS = 4
CHUNK = 128
IN_WIDTH = 1792
D_FF = 2816
FF_SHARD = 2 * D_FF // N_DEV
CONV_WIDTH = 31
CONV_HALO = 16
GRID_W = 64
ROPE_THETA = 10000.0
LANES = 128
SUBLANES = 8
ROW_BLOCK = 512
ADAM_ROWS = 256
ADAM_LR, ADAM_B1, ADAM_B2, ADAM_EPS, ADAM_WD, ADAM_STEP = 0.001, 0.9, 0.999, 1e-08, 0.01, 10


def _tile(n, target, mult=LANES):
    best = None
    for t in range(mult, min(n, target) + 1, mult):
        if n % t == 0:
            best = t
    return best if best is not None else n


def _sigmoid(x):
    return 1.0 / (1.0 + jnp.exp(-x))


def _silu(x):
    return x * _sigmoid(x)


def _dsilu(x):
    s = _sigmoid(x)
    return s * (1.0 + x * (1.0 - s))


_GELU_K = math.sqrt(2.0 / math.pi)


def _gelu(x):
    return 0.5 * x * (1.0 + jnp.tanh(_GELU_K * (x + 0.044715 * x * x * x)))


def _gelu_and_grad(x):
    x2 = x * x
    t = jnp.tanh(_GELU_K * x * (1.0 + 0.044715 * x2))
    half = 0.5 * (1.0 + t)
    return x * half, half + 0.5 * x * (1.0 - t * t) * _GELU_K * (1.0 + 3.0 * 0.044715 * x2)


def _split_bf16(x):
    hi = x.astype(BF16)
    lo = (x - hi.astype(F32)).astype(BF16)
    return hi, lo


def _dot(a, b, dims):
    return lax.dot_general(a, b, (dims, ((), ())), preferred_element_type=F32)


def _dot3(a, b, dims):
    ah, al = _split_bf16(a)
    bh, bl = _split_bf16(b)
    return _dot(ah, bh, dims) + _dot(ah, bl, dims) + _dot(al, bh, dims)


NN = ((1,), (0,))
NT = ((1,), (1,))
TN = ((0,), (0,))


def _all_gather(xs, name, in_vmem):
    n_arr = len(xs)

    def body(*refs):
        x_refs, out_refs = refs[:n_arr], refs[n_arr:2 * n_arr]
        send_sems, recv_sems, local_sems = refs[2 * n_arr:]
        x, y, c = lax.axis_index("x"), lax.axis_index("y"), lax.axis_index("c")
        me, sibling = (x, y, c), (x, y, 1 - c)
        chips = [(1 - x, y), (x, 1 - y), (1 - x, 1 - y)]

        def rows(a, px, py, pc):
            m_per = xs[a].shape[0]
            return out_refs[a].at[pl.ds((4 * px + 2 * py + pc) * m_per, m_per), :]

        def copy(a, k, block, to, src=None):
            return pltpu.make_async_remote_copy(
                src_ref=rows(a, *block) if src is None else src,
                dst_ref=rows(a, *block),
                send_sem=send_sems.at[7 * a + k],
                recv_sem=recv_sems.at[7 * a + k],
                device_id=to,
                device_id_type=MESH,
            )

        mine, first, passed = [], [], []
        for a in range(n_arr):
            mine.append(pltpu.make_async_copy(x_refs[a], rows(a, *me), local_sems.at[a]))
            mine[-1].start()
            first.append(copy(a, 0, me, sibling, src=x_refs[a]))
            first += [copy(a, 1 + j, me, (*chip, c), src=x_refs[a]) for j, chip in enumerate(chips)]
        for cp in first:
            cp.start()
        for a in range(n_arr):
            for j, chip in enumerate(chips):
                copy(a, 1 + j, (*chip, c), me).wait_recv()
                passed.append(copy(a, 4 + j, (*chip, c), sibling))
                passed[-1].start()
        for a in range(n_arr):
            copy(a, 0, sibling, me).wait_recv()
            for j, chip in enumerate(chips):
                copy(a, 4 + j, (*chip, 1 - c), me).wait_recv()
        for cp in first + passed:
            cp.wait_send()
        for cp in mine:
            cp.wait()

    space = pltpu.VMEM if in_vmem else pl.ANY
    return pl.pallas_call(
        body,
        name=name,
        out_shape=[jax.ShapeDtypeStruct((N_DEV * t.shape[0], t.shape[1]), t.dtype) for t in xs],
        in_specs=[pl.BlockSpec(memory_space=space)] * n_arr,
        out_specs=[pl.BlockSpec(memory_space=space)] * n_arr,
        scratch_shapes=[
            pltpu.SemaphoreType.DMA((7 * n_arr,)),
            pltpu.SemaphoreType.DMA((7 * n_arr,)),
            pltpu.SemaphoreType.DMA((n_arr,)),
        ],
    )(*xs)


HBM_SPEC = pl.BlockSpec(memory_space=pltpu.HBM)
SEM_SPEC = pl.BlockSpec(memory_space=pltpu.SEMAPHORE)
DATAFLOW_EFFECT = pltpu.SideEffectType.DATAFLOW_SIDE_EFFECTING


def _peers(x, y, c):
    for k in range(1, N_DEV):
        px = 1 - x if (k >> 2) & 1 else x
        py = 1 - y if (k >> 1) & 1 else y
        pc = 1 - c if k & 1 else c
        yield k - 1, (px, py, pc), 4 * px + 2 * py + pc


def _push_copies(src_refs, land_refs, send_sems, recv_sems, shapes, whole_src):
    x, y, c = lax.axis_index("x"), lax.axis_index("y"), lax.axis_index("c")
    me = 4 * x + 2 * y + c
    for a, (m_per, _) in enumerate(shapes):
        def block(ref, idx, m_per=m_per):
            return ref.at[pl.ds(idx * m_per, m_per), :]

        for k, peer, pidx in _peers(x, y, c):
            src = src_refs[a] if whole_src else block(src_refs[a], pidx)
            sems = dict(send_sem=send_sems.at[N_DEV * a + k], recv_sem=recv_sems.at[N_DEV * a + k],
                        device_id=peer, device_id_type=MESH)
            yield (pltpu.make_async_remote_copy(src_ref=src, dst_ref=block(land_refs[a], me), **sems),
                   pltpu.make_async_remote_copy(src_ref=src, dst_ref=block(land_refs[a], pidx), **sems))


def _own_copies(src_refs, land_refs, recv_sems, shapes, whole_src):
    me = 4 * lax.axis_index("x") + 2 * lax.axis_index("y") + lax.axis_index("c")
    for a, (m_per, _) in enumerate(shapes):
        mine = pl.ds(me * m_per, m_per)
        src = src_refs[a] if whole_src else src_refs[a].at[mine, :]
        yield pltpu.make_async_copy(src, land_refs[a].at[mine, :], recv_sems.at[N_DEV * a + N_DEV - 1])


def _push_begin(srcs, whole_src, name):
    n_arr = len(srcs)
    shapes = [(t.shape[0] if whole_src else t.shape[0] // N_DEV, t.shape[1]) for t in srcs]
    lands = [lax.empty((N_DEV * m, n), t.dtype) for (m, n), t in zip(shapes, srcs)]

    def body(*refs):
        src_refs, land_refs = refs[:n_arr], refs[n_arr:2 * n_arr]
        send_sems, recv_sems = refs[2 * n_arr], refs[2 * n_arr + 1]
        token = refs[-1]
        for outgoing, _ in _push_copies(src_refs, land_refs, send_sems, recv_sems, shapes, whole_src):
            outgoing.start()
        for own in _own_copies(src_refs, land_refs, recv_sems, shapes, whole_src):
            own.start()
        token[...] = jnp.zeros_like(token)

    operands = [pltpu.with_memory_space_constraint(t, pltpu.HBM) for t in list(srcs) + lands]
    outs = pl.pallas_call(
        body, name=name,
        out_shape=(pltpu.SemaphoreType.DMA((N_DEV * n_arr,)), pltpu.SemaphoreType.DMA((N_DEV * n_arr,)),
                   *[pltpu.HBM(t.shape, t.dtype) for t in operands],
                   jax.ShapeDtypeStruct((SUBLANES, LANES), F32)),
        in_specs=[HBM_SPEC] * (2 * n_arr),
        out_specs=(SEM_SPEC, SEM_SPEC, *[HBM_SPEC] * (2 * n_arr), pl.BlockSpec(memory_space=pltpu.VMEM)),
        input_output_aliases={i: 2 + i for i in range(2 * n_arr)},
        compiler_params=pltpu.CompilerParams(has_side_effects=DATAFLOW_EFFECT),
    )(*operands)
    return outs[0], outs[1], list(outs[2:2 + n_arr]), list(outs[2 + n_arr:2 + 2 * n_arr]), outs[-1], whole_src


def _push_end(handle, after, name):
    send_sems, recv_sems, srcs, lands, _, whole_src = handle
    n_arr = len(srcs)
    shapes = [(t.shape[0] // N_DEV, t.shape[1]) for t in lands]

    def body(*refs):
        src_refs, land_refs = refs[:n_arr], refs[n_arr:2 * n_arr]
        send_sems_ref, recv_sems_ref = refs[2 * n_arr], refs[2 * n_arr + 1]
        for outgoing, incoming in _push_copies(src_refs, land_refs, send_sems_ref, recv_sems_ref, shapes, whole_src):
            outgoing.wait_send()
            incoming.wait_recv()
        for own in _own_copies(src_refs, land_refs, recv_sems_ref, shapes, whole_src):
            own.wait()

    outs = pl.pallas_call(
        body, name=name,
        out_shape=tuple(pltpu.HBM(t.shape, t.dtype) for t in srcs + lands),
        in_specs=[HBM_SPEC] * (2 * n_arr) + [SEM_SPEC, SEM_SPEC, pl.BlockSpec(memory_space=pl.ANY)],
        out_specs=tuple([HBM_SPEC] * (2 * n_arr)),
        input_output_aliases={i: i for i in range(2 * n_arr)},
        compiler_params=pltpu.CompilerParams(has_side_effects=DATAFLOW_EFFECT),
    )(*srcs, *lands, send_sems, recv_sems, after)
    return list(outs[:n_arr]), list(outs[n_arr:])


def _sum_devices(r, name, rows_per_step=ADAM_ROWS):
    _, m, n = r.shape
    tm = _tile(m, rows_per_step, 8)

    def body(r_ref, o_ref):
        acc = r_ref[0].astype(F32)
        for s in range(1, N_DEV):
            acc = acc + r_ref[s].astype(F32)
        o_ref[...] = acc

    return pl.pallas_call(
        body,
        name=name,
        grid=(m // tm,),
        out_shape=jax.ShapeDtypeStruct((m, n), F32),
        in_specs=[pl.BlockSpec((N_DEV, tm, n), lambda i: (0, i, 0))],
        out_specs=pl.BlockSpec((tm, n), lambda i: (i, 0)),
        compiler_params=pltpu.CompilerParams(dimension_semantics=("parallel",)),
    )(r)


def _get(ref):
    return ref[0] if len(ref.shape) == 3 else ref[...]


def _put(ref, val):
    if len(ref.shape) == 3:
        ref[0] = val
    else:
        ref[...] = val


def _norm_mod(hv, g, sc, sh):
    r = lax.rsqrt(jnp.mean(hv * hv, axis=-1, keepdims=True) + EPS)
    return (hv * r) * g * (1.0 + sc) + sh


def _mm_call(name, a, b, a_spec, b_spec, out_sds, o_spec, grid, dims, acc_shape, bias=None,
             res=None, gate=None, raw_out=False, vec_spec=None, norm=None):
    nk = grid[2]
    operands, in_specs = [a, b], [a_spec, b_spec]
    if bias is not None:
        operands.append(bias)
        in_specs.append(vec_spec)
    if res is not None:
        operands += [res, gate]
        in_specs += [o_spec, vec_spec]
    if norm is not None:
        assert grid[1] == 1
        operands += list(norm)
        in_specs += [vec_spec] * 3
    out_shape, out_specs = [out_sds], [o_spec]
    if raw_out:
        out_shape.append(jax.ShapeDtypeStruct(out_sds.shape, BF16))
        out_specs.append(o_spec)
    if norm is not None:
        out_shape.append(jax.ShapeDtypeStruct(out_sds.shape, BF16))
        out_specs.append(o_spec)

    def body(*refs):
        it = iter(refs)
        a_ref, b_ref = next(it), next(it)
        bias_ref = next(it) if bias is not None else None
        res_ref, gate_ref = (next(it), next(it)) if res is not None else (None, None)
        norm_refs = (next(it), next(it), next(it)) if norm is not None else None
        o_ref = next(it)
        raw_ref = next(it) if raw_out else None
        xn_ref = next(it) if norm is not None else None
        acc = next(it) if nk > 1 else None
        k = pl.program_id(2)
        part = _dot(_get(a_ref).astype(BF16), _get(b_ref).astype(BF16), dims)

        def finish(y):
            if bias_ref is not None:
                y = y + bias_ref[...]
            if raw_ref is not None:
                raw_ref[...] = y.astype(BF16)
            if res_ref is not None:
                y = res_ref[...] + gate_ref[...] * y
            _put(o_ref, y.astype(out_sds.dtype))
            if xn_ref is not None:
                xn_ref[...] = _norm_mod(y, *[r[...] for r in norm_refs]).astype(BF16)

        if nk == 1:
            finish(part)
        else:
            @pl.when(k == 0)
            def _():
                acc[...] = part

            @pl.when(k > 0)
            def _():
                acc[...] += part

            @pl.when(k == nk - 1)
            def _():
                finish(acc[...])

    outs = pl.pallas_call(
        body,
        name=name,
        grid=grid,
        out_shape=out_shape,
        in_specs=in_specs,
        out_specs=out_specs,
        scratch_shapes=[pltpu.VMEM(acc_shape, F32)] if nk > 1 else [],
        compiler_params=pltpu.CompilerParams(dimension_semantics=("parallel", "parallel", "arbitrary")),
    )(*operands)
    return outs if len(outs) > 1 else outs[0]


def _mm(a, b, mode, name, out_dtype=F32, bias=None, res=None, gate=None, raw_out=False,
        tm=1024, tn=1024, tk=1024, a_row_off=0, norm=None):
    if mode == "nn":
        K, N = b.shape
        M = a.shape[0] - a_row_off
    elif mode == "nt":
        N, K = b.shape
        M = a.shape[0] - a_row_off
    else:
        (K, M), N = a.shape, b.shape[1]
    tm, tn, tk = _tile(M, tm, LANES if mode == "tn" else 2 * SUBLANES), _tile(N, tn), _tile(K, tk)
    off = a_row_off // tm
    dims = {"nn": NN, "nt": NT, "tn": TN}[mode]
    a_spec = (pl.BlockSpec((tk, tm), lambda i, j, k: (k, i)) if mode == "tn"
              else pl.BlockSpec((tm, tk), lambda i, j, k: (i + off, k)))
    b_spec = (pl.BlockSpec((tn, tk), lambda i, j, k: (j, k)) if mode == "nt"
              else pl.BlockSpec((tk, tn), lambda i, j, k: (k, j)))
    return _mm_call(name, a, b, a_spec, b_spec, jax.ShapeDtypeStruct((M, N), out_dtype),
                    pl.BlockSpec((tm, tn), lambda i, j, k: (i, j)), (M // tm, N // tn, K // tk), dims,
                    (tm, tn), bias, res, gate, raw_out, pl.BlockSpec((1, tn), lambda i, j, k: (0, j)), norm)


def _mm_sum_shards(a3, b3, mode, name, out_dtype=F32, res=None, gate=None, raw_out=False, tm=512, norm=None):
    S, M, kk = a3.shape
    N = b3.shape[2] if mode == "nn" else b3.shape[1]
    tm = _tile(M, tm)
    dims = NN if mode == "nn" else NT
    has_res = res is not None

    def body(*refs):
        it = iter(refs)
        a_ref, b_ref = next(it), next(it)
        res_ref, gate_ref = (next(it), next(it)) if has_res else (None, None)
        norm_refs = (next(it), next(it), next(it)) if norm is not None else None
        o_ref = next(it)
        raw_ref = next(it) if raw_out else None
        xn_ref = next(it) if norm is not None else None
        y = _dot(a_ref[0], b_ref[0], dims)
        for s in range(1, S):
            y = y + _dot(a_ref[s], b_ref[s], dims)
        if raw_ref is not None:
            raw_ref[...] = y.astype(BF16)
        if has_res:
            y = res_ref[...] + gate_ref[...] * y
        o_ref[...] = y.astype(out_dtype)
        if xn_ref is not None:
            xn_ref[...] = _norm_mod(y, *[r[...] for r in norm_refs]).astype(BF16)

    tile = pl.BlockSpec((tm, N), lambda i: (i, 0))
    operands = [a3, b3] + ([res, gate] if has_res else []) + (list(norm) if norm is not None else [])
    in_specs = [pl.BlockSpec((S, tm, kk), lambda i: (0, i, 0)), pl.BlockSpec(b3.shape, lambda i: (0, 0, 0))]
    in_specs += [tile, _vec_spec(N)] if has_res else []
    in_specs += [_vec_spec(N)] * 3 if norm is not None else []
    out_shape = [jax.ShapeDtypeStruct((M, N), out_dtype)] + ([jax.ShapeDtypeStruct((M, N), BF16)] if raw_out else [])
    out_shape += [jax.ShapeDtypeStruct((M, N), BF16)] if norm is not None else []
    outs = pl.pallas_call(
        body, name=name, grid=(M // tm,),
        out_shape=out_shape, in_specs=in_specs, out_specs=[tile] * len(out_shape),
        compiler_params=pltpu.CompilerParams(dimension_semantics=("parallel",)),
    )(*operands)
    return outs if len(outs) > 1 else outs[0]


def _mm_tn_shard_rows(a3, b, name, out_dtype, tn=1024, tk=4096):
    S, T, m = a3.shape
    N = b.shape[1]
    tn, tk = _tile(N, tn), _tile(T, tk)
    return _mm_call(name, a3, b, pl.BlockSpec((1, tk, m), lambda i, j, k: (i, k, 0)),
                    pl.BlockSpec((tk, tn), lambda i, j, k: (k, j)), jax.ShapeDtypeStruct((S, m, N), out_dtype),
                    pl.BlockSpec((1, m, tn), lambda i, j, k: (i, 0, j)), (S, N // tn, T // tk), TN, (m, tn))


def _row_spec(tm, width, off=0):
    return pl.BlockSpec((tm, width), lambda i: (i + off, 0))


def _vec_spec(width):
    return pl.BlockSpec((1, width), lambda i: (0, 0))


def _norm_mod_fwd_cat(hc, h, g, csc, csh, sc, sh, name):
    (C, Dm), T = hc.shape, h.shape[0]
    tm = _tile(math.gcd(C, T), ROW_BLOCK, 8)
    off = C // tm

    def body(hc_ref, h_ref, g_ref, csc_ref, csh_ref, sc_ref, sh_ref, o_ref):
        is_ctx = pl.program_id(0) < off
        hv = jnp.where(is_ctx, hc_ref[...], h_ref[...])
        scv = jnp.where(is_ctx, csc_ref[...], sc_ref[...])
        shv = jnp.where(is_ctx, csh_ref[...], sh_ref[...])
        r = lax.rsqrt(jnp.mean(hv * hv, axis=-1, keepdims=True) + EPS)
        o_ref[...] = ((hv * r) * g_ref[...] * (1.0 + scv) + shv).astype(BF16)

    return pl.pallas_call(
        body, name=name, grid=((C + T) // tm,),
        out_shape=jax.ShapeDtypeStruct((C + T, Dm), BF16),
        in_specs=[pl.BlockSpec((tm, Dm), lambda i: (jnp.minimum(i, off - 1), 0)),
                  pl.BlockSpec((tm, Dm), lambda i: (jnp.maximum(i - off, 0), 0))] + [_vec_spec(Dm)] * 5,
        out_specs=_row_spec(tm, Dm),
        compiler_params=pltpu.CompilerParams(dimension_semantics=("parallel",)),
    )(hc, h, g, csc, csh, sc, sh)


def _gate_grads(dh, y_ref, gt_ref, dy_ref, dgt_ref, dsum_ref):
    dy = dh * gt_ref[...]
    dgt_ref[...] += jnp.sum(dh * y_ref[...].astype(F32), axis=0, keepdims=True)
    dsum_ref[...] += jnp.sum(dy, axis=0, keepdims=True)
    dy_ref[...] = dy.astype(BF16)


def _norm_mod_bwd(h, g, sc, dxm, dres, name, dxm_row_off=0, gate=None):
    R, Dm = h.shape
    tm = _tile(math.gcd(R, dxm_row_off) if dxm_row_off else R, ROW_BLOCK, 8)
    off = dxm_row_off // tm
    has_res = dres is not None
    has_gate = gate is not None

    def body(*refs):
        it = iter(refs)
        h_ref, g_ref, sc_ref, dx_ref = next(it), next(it), next(it), next(it)
        dres_ref = next(it) if has_res else None
        y_ref, gt_ref = (next(it), next(it)) if has_gate else (None, None)
        dh_ref, da_ref, dsh_ref = next(it), next(it), next(it)
        gate_out = (next(it), next(it), next(it)) if has_gate else ()
        i = pl.program_id(0)

        @pl.when(i == 0)
        def _():
            for ref in (da_ref, dsh_ref) + gate_out[1:]:
                ref[...] = jnp.zeros_like(ref)

        hv = h_ref[...]
        dx = dx_ref[...].astype(F32)
        r = lax.rsqrt(jnp.mean(hv * hv, axis=-1, keepdims=True) + EPS)
        n = hv * r
        da_ref[...] += jnp.sum(dx * n, axis=0, keepdims=True)
        dsh_ref[...] += jnp.sum(dx, axis=0, keepdims=True)
        dn = dx * (g_ref[...] * (1.0 + sc_ref[...]))
        dh = r * (dn - n * jnp.mean(dn * n, axis=-1, keepdims=True))
        if has_res:
            dh = dh + dres_ref[...]
        dh_ref[...] = dh
        if has_gate:
            _gate_grads(dh, y_ref, gt_ref, *gate_out)

    operands = [h, g, sc, dxm] + ([dres] if has_res else []) + (list(gate) if has_gate else [])
    in_specs = [_row_spec(tm, Dm), _vec_spec(Dm), _vec_spec(Dm), _row_spec(tm, Dm, off)]
    in_specs += [_row_spec(tm, Dm)] if has_res else []
    in_specs += [_row_spec(tm, Dm), _vec_spec(Dm)] if has_gate else []
    vec = jax.ShapeDtypeStruct((1, Dm), F32)
    out_shape = [jax.ShapeDtypeStruct((R, Dm), F32), vec, vec]
    out_specs = [_row_spec(tm, Dm), _vec_spec(Dm), _vec_spec(Dm)]
    if has_gate:
        out_shape += [jax.ShapeDtypeStruct((R, Dm), BF16), vec, vec]
        out_specs += [_row_spec(tm, Dm), _vec_spec(Dm), _vec_spec(Dm)]
    return pl.pallas_call(
        body, name=name, grid=(R // tm,),
        out_shape=out_shape, in_specs=in_specs, out_specs=out_specs,
        compiler_params=pltpu.CompilerParams(dimension_semantics=("arbitrary",)),
    )(*operands)


def _ffn_in_swiglu(xf, w3, name, tm=1024):
    T, K = xf.shape
    S, n, _ = w3.shape
    half = S // 2
    tm = _tile(T, tm)

    def body(a_ref, wg_ref, wu_ref, gu_ref, act_ref):
        a = a_ref[...]
        g = _dot(a, wg_ref[0], NT)
        u = _dot(a, wu_ref[0], NT)
        gu_ref[0, 0] = g.astype(BF16)
        gu_ref[1, 0] = u.astype(BF16)
        act_ref[0] = (_silu(g) * u).astype(BF16)

    return pl.pallas_call(
        body, name=name, grid=(T // tm, half),
        out_shape=[jax.ShapeDtypeStruct((2, half, T, n), BF16), jax.ShapeDtypeStruct((half, T, n), BF16)],
        in_specs=[pl.BlockSpec((tm, K), lambda i, j: (i, 0)),
                  pl.BlockSpec((1, n, K), lambda i, j: (j, 0, 0)),
                  pl.BlockSpec((1, n, K), lambda i, j: (j + half, 0, 0))],
        out_specs=[pl.BlockSpec((2, 1, tm, n), lambda i, j: (0, j, i, 0)),
                   pl.BlockSpec((1, tm, n), lambda i, j: (j, i, 0))],
        compiler_params=pltpu.CompilerParams(dimension_semantics=("parallel", "parallel")),
    )(xf, w3, w3)


def _ffn_out_dx_swiglu(df, wo, gu, name, tm=1024):
    T, Dm = df.shape
    half, n, _ = wo.shape
    tm = _tile(T, tm)

    def body(df_ref, w_ref, gu_ref, o_ref):
        da = _dot(df_ref[...], w_ref[0], NT)
        g = gu_ref[0, 0].astype(F32)
        u = gu_ref[1, 0].astype(F32)
        s = _sigmoid(g)
        o_ref[0, 0] = (da * u * (s * (1.0 + g * (1.0 - s)))).astype(BF16)
        o_ref[1, 0] = (da * (g * s)).astype(BF16)

    gu_spec = pl.BlockSpec((2, 1, tm, n), lambda i, j: (0, j, i, 0))
    return pl.pallas_call(
        body, name=name, grid=(T // tm, half),
        out_shape=jax.ShapeDtypeStruct(gu.shape, BF16),
        in_specs=[pl.BlockSpec((tm, Dm), lambda i, j: (i, 0)),
                  pl.BlockSpec((1, n, Dm), lambda i, j: (j, 0, 0)), gu_spec],
        out_specs=gu_spec,
        compiler_params=pltpu.CompilerParams(dimension_semantics=("parallel", "parallel")),
    )(df, wo, gu)


def _pw1_glu(xm, w, bias, name, tm=512):
    T, K = xm.shape
    N = w.shape[1]
    tm = _tile(T, tm, 2 * SUBLANES)

    def body(a_ref, w_ref, b_ref, ag_ref, hg_ref):
        ag = (_dot(a_ref[...], w_ref[...], NN) + b_ref[...]).astype(BF16)
        ag_ref[...] = ag
        hg_ref[...] = ag[:, :N // 2].astype(F32) * _sigmoid(ag[:, N // 2:].astype(F32))

    return pl.pallas_call(
        body, name=name, grid=(T // tm,),
        out_shape=[jax.ShapeDtypeStruct((T, N), BF16), jax.ShapeDtypeStruct((T, N // 2), F32)],
        in_specs=[_row_spec(tm, K), pl.BlockSpec((K, N), lambda i: (0, 0)), _vec_spec(N)],
        out_specs=[_row_spec(tm, N), _row_spec(tm, N // 2)],
        compiler_params=pltpu.CompilerParams(dimension_semantics=("parallel",)),
    )(xm, w, bias)


def _glu_bwd_pw1_dx(ag, dhg, w, name, tm=512):
    T, N = ag.shape
    Dm = N // 2
    tm = _tile(T, tm, 2 * SUBLANES)

    def body(ag_ref, dh_ref, w_ref, dag_ref, s_ref, dx_ref):
        i = pl.program_id(0)

        @pl.when(i == 0)
        def _():
            s_ref[...] = jnp.zeros_like(s_ref)

        a = ag_ref[:, :Dm].astype(F32)
        s = _sigmoid(ag_ref[:, Dm:].astype(F32))
        dh = dh_ref[...]
        da = dh * s
        dg = dh * a * s * (1.0 - s)
        dag_ref[:, :Dm] = da.astype(BF16)
        dag_ref[:, Dm:] = dg.astype(BF16)
        s_ref[:, :Dm] += jnp.sum(da, axis=0, keepdims=True)
        s_ref[:, Dm:] += jnp.sum(dg, axis=0, keepdims=True)
        dx_ref[...] = _dot(dag_ref[...], w_ref[...], NT).astype(BF16)

    return pl.pallas_call(
        body, name=name, grid=(T // tm,),
        out_shape=[jax.ShapeDtypeStruct((T, N), BF16), jax.ShapeDtypeStruct((1, N), F32),
                   jax.ShapeDtypeStruct((T, Dm), BF16)],
        in_specs=[_row_spec(tm, N), _row_spec(tm, Dm), pl.BlockSpec(w.shape, lambda i: (0, 0))],
        out_specs=[_row_spec(tm, N), _vec_spec(N), _row_spec(tm, Dm)],
        compiler_params=pltpu.CompilerParams(dimension_semantics=("arbitrary",)),
    )(ag, dhg, w)


def _halo_specs(tm, nblk, width):
    per = tm // CONV_HALO
    prev = pl.BlockSpec((CONV_HALO, width), lambda i: (jnp.maximum(i * per - 1, 0), 0))
    nxt = pl.BlockSpec((CONV_HALO, width), lambda i: (jnp.minimum((i + 1) * per, nblk * per - 1), 0))
    return prev, nxt


def _fill_halo(scr, prev_ref, cur_ref, next_ref, i, nblk, tm):
    scr[0:CONV_HALO, :] = jnp.where(i > 0, prev_ref[...], 0.0)
    scr[CONV_HALO:CONV_HALO + tm, :] = cur_ref[...]
    scr[CONV_HALO + tm:2 * CONV_HALO + tm, :] = jnp.where(i < nblk - 1, next_ref[...], 0.0)


CONV_ROWS = 128


CONV_REACH = (CONV_WIDTH // SUBLANES) * SUBLANES


def _windows(scr, stage, cols, tm):
    for r in range(SUBLANES):
        if r:
            stage[r] = scr[pl.ds(r, tm + CONV_REACH), cols]
        for a in range(CONV_REACH // SUBLANES + 1):
            off = SUBLANES * a + r
            if 1 <= off <= CONV_WIDTH:
                yield off, (stage[r, SUBLANES * a:SUBLANES * a + tm, :] if r
                            else scr[SUBLANES * a:SUBLANES * a + tm, cols])


def _conv_fwd(hg, w_dw, b_dw, name):
    R, Dm = hg.shape
    tm = _tile(R, CONV_ROWS, CONV_HALO)
    nblk = R // tm
    prev_spec, next_spec = _halo_specs(tm, nblk, Dm)

    def body(prev_ref, cur_ref, next_ref, w_ref, bdw_ref, hd_ref, scr, stage):
        _fill_halo(scr, prev_ref, cur_ref, next_ref, pl.program_id(0), nblk, tm)
        for cb in range(Dm // LANES):
            cols = slice(cb * LANES, (cb + 1) * LANES)
            acc = jnp.zeros((tm, LANES), F32) + bdw_ref[:, cols]
            for off, win in _windows(scr, stage, cols, tm):
                acc = acc + w_ref[off - 1:off, cols] * win
            hd_ref[:, cols] = acc

    return pl.pallas_call(
        body, name=name, grid=(nblk,),
        out_shape=jax.ShapeDtypeStruct((R, Dm), F32),
        in_specs=[prev_spec, _row_spec(tm, Dm), next_spec,
                  pl.BlockSpec((CONV_WIDTH, Dm), lambda i: (0, 0)), _vec_spec(Dm)],
        out_specs=_row_spec(tm, Dm),
        scratch_shapes=[pltpu.VMEM((tm + 2 * CONV_HALO, Dm), F32),
                        pltpu.VMEM((SUBLANES, tm + CONV_REACH, LANES), F32)],
        compiler_params=pltpu.CompilerParams(dimension_semantics=("parallel",)),
    )(hg, hg, hg, w_dw, b_dw)


def _ln_silu_pw2(hd, ln_g, ln_b, w, bias, res, gate, norm, name, tm=512):
    R, Dm = hd.shape
    tm = _tile(R, tm, 2 * SUBLANES)

    def body(hd_ref, g_ref, b_ref, w_ref, bias_ref, res_ref, gate_ref, ng_ref, nsc_ref, nsh_ref,
             hs_ref, h_ref, y_ref, xn_ref):
        hd = hd_ref[...]
        xc = hd - jnp.mean(hd, axis=-1, keepdims=True)
        rs = lax.rsqrt(jnp.mean(xc * xc, axis=-1, keepdims=True) + EPS)
        hs = _silu(xc * rs * g_ref[...] + b_ref[...]).astype(BF16)
        hs_ref[...] = hs
        y = _dot(hs, w_ref[...], NN) + bias_ref[...]
        y_ref[...] = y.astype(BF16)
        h = res_ref[...] + gate_ref[...] * y
        h_ref[...] = h
        xn_ref[...] = _norm_mod(h, ng_ref[...], nsc_ref[...], nsh_ref[...]).astype(BF16)

    row, vec = _row_spec(tm, Dm), _vec_spec(Dm)
    return pl.pallas_call(
        body, name=name, grid=(R // tm,),
        out_shape=[jax.ShapeDtypeStruct((R, Dm), BF16), jax.ShapeDtypeStruct((R, Dm), F32),
                   jax.ShapeDtypeStruct((R, Dm), BF16), jax.ShapeDtypeStruct((R, Dm), BF16)],
        in_specs=[row, vec, vec, pl.BlockSpec(w.shape, lambda i: (0, 0)), vec, row, vec, vec, vec, vec],
        out_specs=[row, row, row, row],
        compiler_params=pltpu.CompilerParams(dimension_semantics=("parallel",)),
    )(hd, ln_g, ln_b, w, bias, res, gate, *norm)


def _pw2_dx_ln_silu_bwd(dy, w, hd, ln_g, ln_b, name, tm=512):
    R, Dm = hd.shape
    tm = _tile(R, tm, 2 * SUBLANES)

    def body(dy_ref, w_ref, hd_ref, g_ref, b_ref, dhd_ref, dg_ref, db_ref, dsum_ref):
        i = pl.program_id(0)

        @pl.when(i == 0)
        def _():
            dg_ref[...] = jnp.zeros_like(dg_ref)
            db_ref[...] = jnp.zeros_like(db_ref)
            dsum_ref[...] = jnp.zeros_like(dsum_ref)

        hd = hd_ref[...]
        mu = jnp.mean(hd, axis=-1, keepdims=True)
        xc = hd - mu
        rs = lax.rsqrt(jnp.mean(xc * xc, axis=-1, keepdims=True) + EPS)
        z = xc * rs
        hl = z * g_ref[...] + b_ref[...]
        dhl = _dot(dy_ref[...], w_ref[...], NT) * _dsilu(hl)
        dg_ref[...] += jnp.sum(dhl * z, axis=0, keepdims=True)
        db_ref[...] += jnp.sum(dhl, axis=0, keepdims=True)
        dz = dhl * g_ref[...]
        dhd = rs * (dz - jnp.mean(dz, axis=-1, keepdims=True) - z * jnp.mean(dz * z, axis=-1, keepdims=True))
        dsum_ref[...] += jnp.sum(dhd, axis=0, keepdims=True)
        dhd_ref[...] = dhd

    return pl.pallas_call(
        body, name=name, grid=(R // tm,),
        out_shape=[jax.ShapeDtypeStruct((R, Dm), F32)] + [jax.ShapeDtypeStruct((1, Dm), F32)] * 3,
        in_specs=[_row_spec(tm, Dm), pl.BlockSpec(w.shape, lambda i: (0, 0)), _row_spec(tm, Dm),
                  _vec_spec(Dm), _vec_spec(Dm)],
        out_specs=[_row_spec(tm, Dm), _vec_spec(Dm), _vec_spec(Dm), _vec_spec(Dm)],
        compiler_params=pltpu.CompilerParams(dimension_semantics=("arbitrary",)),
    )(dy, w, hd, ln_g, ln_b)


def _conv_bwd(dhd, hg, w_dw, name):
    R, Dm = hg.shape
    tm = _tile(R, CONV_ROWS, CONV_HALO)
    nblk = R // tm
    prev_spec, next_spec = _halo_specs(tm, nblk, Dm)

    def body(dprev, dcur, dnext, gprev, gcur, gnext, w_ref, dhg_ref, dw_ref, dscr, gscr, dwp, stage):
        i = pl.program_id(0)

        @pl.when(i == 0)
        def _():
            dwp[...] = jnp.zeros_like(dwp)

        _fill_halo(dscr, dprev, dcur, dnext, i, nblk, tm)
        _fill_halo(gscr, gprev, gcur, gnext, i, nblk, tm)
        for cb in range(Dm // LANES):
            cols = slice(cb * LANES, (cb + 1) * LANES)
            acc = jnp.zeros((tm, LANES), F32)
            for off, win in _windows(dscr, stage, cols, tm):
                j = CONV_WIDTH - off
                acc = acc + w_ref[j:j + 1, cols] * win
            dhg_ref[:, cols] = acc
            d_here = dcur[:, cols]
            for off, win in _windows(gscr, stage, cols, tm):
                j = off - 1
                prod = d_here * win
                part = prod[0:SUBLANES]
                for k in range(1, tm // SUBLANES):
                    part = part + prod[k * SUBLANES:(k + 1) * SUBLANES]
                dwp[j * SUBLANES:(j + 1) * SUBLANES, cols] += part

        @pl.when(i == nblk - 1)
        def _():
            for j in range(CONV_WIDTH):
                dw_ref[j:j + 1, :] = jnp.sum(dwp[j * SUBLANES:(j + 1) * SUBLANES, :], axis=0, keepdims=True)

    return pl.pallas_call(
        body, name=name, grid=(nblk,),
        out_shape=[jax.ShapeDtypeStruct((R, Dm), F32), jax.ShapeDtypeStruct((CONV_WIDTH, Dm), F32)],
        in_specs=[prev_spec, _row_spec(tm, Dm), next_spec, prev_spec, _row_spec(tm, Dm), next_spec,
                  pl.BlockSpec((CONV_WIDTH, Dm), lambda i: (0, 0))],
        out_specs=[_row_spec(tm, Dm), pl.BlockSpec((CONV_WIDTH, Dm), lambda i: (0, 0))],
        scratch_shapes=[pltpu.VMEM((tm + 2 * CONV_HALO, Dm), F32)] * 2
        + [pltpu.VMEM((CONV_WIDTH * SUBLANES, Dm), F32), pltpu.VMEM((SUBLANES, tm + CONV_REACH, LANES), F32)],
        compiler_params=pltpu.CompilerParams(dimension_semantics=("arbitrary",)),
    )(dhd, dhd, dhd, hg, hg, hg, w_dw)


def _swap16(y, lane):
    return jnp.where((lane & 16) == 0, pltpu.roll(y, LANES - 16, 1), pltpu.roll(y, 16, 1))


def _head_mean(v, bd):
    hi, lo = _split_bf16(v)
    return (_dot(hi, bd, NN) + _dot(lo, bd, NN)) * (1.0 / HEAD_DIM)


Q_COLS = (0, ATTN_WIDTH)
K_COLS = (ATTN_WIDTH, ATTN_WIDTH + HEAD_DIM * 2)
V_COLS = (K_COLS[1], K_COLS[1] + HEAD_DIM * 2)
SU_COLS = (V_COLS[1], V_COLS[1] + SG_WIDTH)
SV_COLS = (SU_COLS[1], SU_COLS[1] + SG_WIDTH)


def _mix_prep_fwd(p, ctx_rows, cos, sin, qg, kg, bd, w_sp, b_spt, name):
    TT = p.shape[0]
    off = ctx_rows // CHUNK
    q_scale = HEAD_DIM ** -0.5

    def body(p_ref, cos_ref, sin_ref, qg_ref, kg_ref, bd_ref, w_ref, b_ref,
             q_ref, kp_ref, vp_ref, kt_ref, sg_ref):
        lane = lax.broadcasted_iota(jnp.int32, (CHUNK, LANES), 1)
        low = lane < HEAD_DIM
        cs, sn, bdv = cos_ref[...], sin_ref[...], bd_ref[...]

        def norm_rope(xv, gain):
            r = lax.rsqrt(_head_mean(xv * xv, bdv) + EPS)
            yv = xv * r * gain
            return yv * cs + _swap16(yv, lane) * sn

        def pad_heads(ref, t):
            tr = pltpu.roll(t, HEAD_DIM, 1)
            ref[0, 0] = jnp.where(low, t, 0.0).astype(BF16)
            ref[0, 1] = jnp.where(low, 0.0, tr).astype(BF16)
            ref[1, 0] = jnp.where(low, tr, 0.0).astype(BF16)
            ref[1, 1] = jnp.where(low, 0.0, t).astype(BF16)

        for a in range(ATTN_WIDTH // LANES):
            xv = p_ref[:, a * LANES:(a + 1) * LANES]
            q_ref[:, a * LANES:(a + 1) * LANES] = (norm_rope(xv, qg_ref[...]) * q_scale).astype(BF16)
        kh = norm_rope(p_ref[:, K_COLS[0]:K_COLS[1]], kg_ref[...])
        pad_heads(kp_ref, kh)
        pad_heads(vp_ref, p_ref[:, V_COLS[0]:V_COLS[1]])
        kht = kh.T
        kt_ref[0] = kht[:HEAD_DIM].astype(BF16)
        kt_ref[1] = kht[HEAD_DIM:].astype(BF16)
        for g in range(N_SG_GROUPS):
            u = _gelu(p_ref[:, SU_COLS[0] + g * LANES:SU_COLS[0] + (g + 1) * LANES])
            vg = _gelu(p_ref[:, SV_COLS[0] + g * LANES:SV_COLS[0] + (g + 1) * LANES])
            xc = vg - jnp.mean(vg, axis=-1, keepdims=True)
            vn = xc * lax.rsqrt(jnp.mean(xc * xc, axis=-1, keepdims=True) + EPS)
            mixed = _dot(w_ref[g].astype(BF16), vn.astype(BF16), NN) + b_ref[:, g:g + 1]
            sg_ref[:, g * LANES:(g + 1) * LANES] = (u * mixed).astype(BF16)

    def row(width):
        return pl.BlockSpec((CHUNK, width), lambda i: (i, 0))

    def whole(shape):
        return pl.BlockSpec(shape, lambda i: (0,) * len(shape))

    pad_spec = pl.BlockSpec((2, 2, CHUNK, LANES), lambda i: (0, 0, i, 0))
    return pl.pallas_call(
        body, name=name, grid=(TT // CHUNK,),
        out_shape=[jax.ShapeDtypeStruct((TT, ATTN_WIDTH), BF16),
                   jax.ShapeDtypeStruct((2, 2, TT, LANES), BF16), jax.ShapeDtypeStruct((2, 2, TT, LANES), BF16),
                   jax.ShapeDtypeStruct((2, HEAD_DIM, TT), BF16),
                   jax.ShapeDtypeStruct((TT - ctx_rows, ATTN_WIDTH + SG_WIDTH), BF16)],
        in_specs=[row(IN_WIDTH), row(LANES), row(LANES), whole((1, LANES)), whole((1, LANES)),
                  whole((LANES, LANES)), whole((N_SG_GROUPS, CHUNK, CHUNK)), whole((CHUNK, N_SG_GROUPS))],
        out_specs=[row(ATTN_WIDTH), pad_spec, pad_spec,
                   pl.BlockSpec((2, HEAD_DIM, CHUNK), lambda i: (0, 0, i)),
                   pl.BlockSpec((CHUNK, SG_WIDTH), lambda i: (jnp.maximum(i - off, 0), 1))],
        compiler_params=pltpu.CompilerParams(dimension_semantics=("arbitrary",)),
    )(p, cos, sin, qg, kg, bd, w_sp, b_spt)


def _mix_prep_bwd(p, dq, f, dao, ctx_rows, cos, sin, qg, kg, bd, w_sp, w_spt, b_spt, name):
    TT = p.shape[0]
    off = ctx_rows // CHUNK
    q_scale = HEAD_DIM ** -0.5

    def body(p_ref, dq_ref, f_ref, dsg_ref, cos_ref, sin_ref, qg_ref, kg_ref, bd_ref, w_ref, wt_ref,
             b_ref, dp_ref, dqg_ref, dkg_ref, dw_ref, db_ref):
        i = pl.program_id(0)

        @pl.when(i == 0)
        def _():
            dqg_ref[...] = jnp.zeros_like(dqg_ref)
            dkg_ref[...] = jnp.zeros_like(dkg_ref)
            dw_ref[...] = jnp.zeros_like(dw_ref)
            db_ref[...] = jnp.zeros_like(db_ref)

        latent = (i >= off).astype(F32)
        lane = lax.broadcasted_iota(jnp.int32, (CHUNK, LANES), 1)
        low = lane < HEAD_DIM
        cs, sn, bdv = cos_ref[...], sin_ref[...], bd_ref[...]

        def fold(b0):
            return jnp.where(low, f_ref[0, b0] + pltpu.roll(f_ref[0, b0 + 1], HEAD_DIM, 1),
                             pltpu.roll(f_ref[1, b0], HEAD_DIM, 1) + f_ref[1, b0 + 1])

        def norm_rope_bwd(xv, dout, gain):
            r = lax.rsqrt(_head_mean(xv * xv, bdv) + EPS)
            n = xv * r
            dy = dout * cs + _swap16(dout * sn, lane)
            dn = dy * gain
            dx = r * (dn - n * _head_mean(dn * n, bdv))
            return dx, jnp.sum(dy * n, axis=0, keepdims=True)

        for a in range(ATTN_WIDTH // LANES):
            cols = slice(a * LANES, (a + 1) * LANES)
            dx, dg = norm_rope_bwd(p_ref[:, cols], dq_ref[:, cols] * (latent * q_scale), qg_ref[...])
            dp_ref[:, cols] = dx.astype(BF16)
            dqg_ref[...] += dg
        dx, dg = norm_rope_bwd(p_ref[:, K_COLS[0]:K_COLS[1]], fold(0), kg_ref[...])
        dp_ref[:, K_COLS[0]:K_COLS[1]] = dx.astype(BF16)
        dkg_ref[...] += dg
        dp_ref[:, V_COLS[0]:V_COLS[1]] = fold(2).astype(BF16)
        for g in range(N_SG_GROUPS):
            su = p_ref[:, SU_COLS[0] + g * LANES:SU_COLS[0] + (g + 1) * LANES]
            sv = p_ref[:, SV_COLS[0] + g * LANES:SV_COLS[0] + (g + 1) * LANES]
            (u, dgelu_su), (vg, dgelu_sv) = _gelu_and_grad(su), _gelu_and_grad(sv)
            xc = vg - jnp.mean(vg, axis=-1, keepdims=True)
            rs = lax.rsqrt(jnp.mean(xc * xc, axis=-1, keepdims=True) + EPS)
            vn = xc * rs
            vnb = vn.astype(BF16)
            mixed = _dot(w_ref[g].astype(BF16), vnb, NN) + b_ref[:, g:g + 1]
            dsg = dsg_ref[:, g * LANES:(g + 1) * LANES].astype(F32) * latent
            du = dsg * mixed
            dmix = dsg * u
            dmb = dmix.astype(BF16)
            db_ref[:, g:g + 1] += jnp.sum(dmix, axis=-1, keepdims=True)
            dw_ref[g] += _dot(dmb, vnb, NT)
            dvn = _dot(wt_ref[g].astype(BF16), dmb, NN)
            dvg = rs * (dvn - jnp.mean(dvn, axis=-1, keepdims=True)
                        - vn * jnp.mean(dvn * vn, axis=-1, keepdims=True))
            dp_ref[:, SU_COLS[0] + g * LANES:SU_COLS[0] + (g + 1) * LANES] = (du * dgelu_su).astype(BF16)
            dp_ref[:, SV_COLS[0] + g * LANES:SV_COLS[0] + (g + 1) * LANES] = (dvg * dgelu_sv).astype(BF16)

    def row(width):
        return pl.BlockSpec((CHUNK, width), lambda i: (i, 0))

    def latent_row(width, col_block):
        return pl.BlockSpec((CHUNK, width), lambda i: (jnp.maximum(i - off, 0), col_block))

    def whole(shape):
        return pl.BlockSpec(shape, lambda i: (0,) * len(shape))

    return pl.pallas_call(
        body, name=name, grid=(TT // CHUNK,),
        out_shape=[jax.ShapeDtypeStruct((TT, IN_WIDTH), BF16), jax.ShapeDtypeStruct((1, LANES), F32),
                   jax.ShapeDtypeStruct((1, LANES), F32),
                   jax.ShapeDtypeStruct((N_SG_GROUPS, CHUNK, CHUNK), F32),
                   jax.ShapeDtypeStruct((CHUNK, N_SG_GROUPS), F32)],
        in_specs=[row(IN_WIDTH), latent_row(ATTN_WIDTH, 0),
                  pl.BlockSpec((2, 4, CHUNK, LANES), lambda i: (0, 0, i, 0)),
                  latent_row(SG_WIDTH, 1), row(LANES), row(LANES), whole((1, LANES)), whole((1, LANES)),
                  whole((LANES, LANES)), whole((N_SG_GROUPS, CHUNK, CHUNK)),
                  whole((N_SG_GROUPS, CHUNK, CHUNK)), whole((CHUNK, N_SG_GROUPS))],
        out_specs=[row(IN_WIDTH), whole((1, LANES)), whole((1, LANES)),
                   whole((N_SG_GROUPS, CHUNK, CHUNK)), whole((CHUNK, N_SG_GROUPS))],
        compiler_params=pltpu.CompilerParams(dimension_semantics=("arbitrary",)),
    )(p, dq, f, dao, cos, sin, qg, kg, bd, w_sp, w_spt, b_spt)


def _attn_fwd(q, kpad, vpad, ao, ctx_rows, name, tq=256):
    TT = q.shape[0]
    T = TT - ctx_rows
    tq = _tile(T, tq)
    off = ctx_rows // tq
    group = 2 * LANES

    def body(q_ref, k_ref, v_ref, ao_in, o_ref, lse_ref):
        del ao_in
        lane = lax.broadcasted_iota(jnp.int32, (tq, LANES), 1)
        lse = jnp.zeros((tq, LANES), F32)
        for a in range(2):
            acc = jnp.zeros((tq, LANES), F32)
            qa = q_ref[:, a * LANES:(a + 1) * LANES]
            for b in range(2):
                s = _dot(qa, k_ref[0, b], NT)
                m = jnp.max(s, axis=-1, keepdims=True)
                e = jnp.exp(s - m)
                l = jnp.sum(e, axis=-1, keepdims=True)
                acc = acc + _dot(e.astype(BF16), v_ref[0, b], NN) * (1.0 / l)
                lse = jnp.where(lane == 2 * a + b, m + jnp.log(l), lse)
            o_ref[:, a * LANES:(a + 1) * LANES] = acc.astype(BF16)
        lse_ref[0] = lse

    kv_spec = pl.BlockSpec((1, 2, TT, LANES), lambda j, i: (j, 0, 0, 0))
    return pl.pallas_call(
        body, name=name, grid=(2, T // tq),
        out_shape=[jax.ShapeDtypeStruct(ao.shape, BF16), jax.ShapeDtypeStruct((2, T, LANES), F32)],
        in_specs=[pl.BlockSpec((tq, group), lambda j, i: (i + off, j)), kv_spec, kv_spec,
                  pl.BlockSpec(memory_space=pl.ANY)],
        out_specs=[pl.BlockSpec((tq, group), lambda j, i: (i, j)),
                   pl.BlockSpec((1, tq, LANES), lambda j, i: (j, i, 0))],
        input_output_aliases={3: 0},
        compiler_params=pltpu.CompilerParams(dimension_semantics=("parallel", "parallel")),
    )(q, kpad, vpad, ao)


def _attn_bwd(q, dao, ao, lse, kpad, vpad, kt, ctx_rows, name, tq=256):
    TT = q.shape[0]
    T = TT - ctx_rows
    tq = _tile(T, tq)
    off = ctx_rows // tq
    group = 2 * LANES

    def body(q_ref, do_ref, o_ref, lse_ref, k_ref, v_ref, kt_ref, dq_ref, f_ref):
        i = pl.program_id(1)

        @pl.when(i == 0)
        def _():
            f_ref[...] = jnp.zeros_like(f_ref)

        ktv = kt_ref[0]
        lse_t = lse_ref[0].T
        row = lax.broadcasted_iota(jnp.int32, (SUBLANES, LANES), 0)
        lane = lax.broadcasted_iota(jnp.int32, (SUBLANES, LANES), 1)
        half_ones = (jnp.where(lane < HEAD_DIM, 0, 1) == row).astype(BF16)
        for a in range(2):
            cols = slice(a * LANES, (a + 1) * LANES)
            qa = q_ref[:, cols]
            do32 = do_ref[:, cols].astype(F32)
            doa = do32.astype(BF16)
            hi, lo = _split_bf16(do32 * o_ref[:, cols].astype(F32))
            deltas = _dot(half_ones, hi, NT) + _dot(half_ones, lo, NT)
            halves = []
            for b in range(2):
                h = 2 * a + b
                st = _dot(k_ref[0, b], qa, NT)
                pt = jnp.exp(st - lse_t[h:h + 1, :])
                dpt = _dot(v_ref[0, b], doa, NT)
                dst = (pt * (dpt - deltas[b:b + 1, :])).astype(BF16)
                f_ref[0, b] += _dot(dst, qa, NN)
                f_ref[0, 2 + b] += _dot(pt.astype(BF16), doa, NN)
                halves.append(_dot(ktv, dst, NN))
            dq_ref[:, cols] = jnp.concatenate(halves, axis=0).T

    kv_spec = pl.BlockSpec((1, 2, TT, LANES), lambda j, i: (j, 0, 0, 0))
    out_cols = pl.BlockSpec((tq, group), lambda j, i: (i, j))
    return pl.pallas_call(
        body, name=name, grid=(2, T // tq),
        out_shape=[jax.ShapeDtypeStruct((T, ATTN_WIDTH), F32), jax.ShapeDtypeStruct((2, 4, TT, LANES), F32)],
        in_specs=[pl.BlockSpec((tq, group), lambda j, i: (i + off, j)), out_cols, out_cols,
                  pl.BlockSpec((1, tq, LANES), lambda j, i: (j, i, 0)),
                  kv_spec, kv_spec, pl.BlockSpec((1, HEAD_DIM, TT), lambda j, i: (j, 0, 0))],
        out_specs=[out_cols, pl.BlockSpec((1, 4, TT, LANES), lambda j, i: (j, 0, 0, 0))],
        compiler_params=pltpu.CompilerParams(dimension_semantics=("parallel", "arbitrary")),
    )(q, dao, ao, lse, kpad, vpad, kt)


def _final_fwd_bwd(h, g, target, y, gt, name):
    R, Dm = h.shape
    tm = _tile(R, ROW_BLOCK, 8)

    def body(h_ref, g_ref, t_ref, y_ref, gt_ref, dh_ref, loss_ref, dg_ref, dy_ref, dgt_ref, dsum_ref):
        i = pl.program_id(0)

        @pl.when(i == 0)
        def _():
            for ref in (loss_ref, dg_ref, dgt_ref, dsum_ref):
                ref[...] = jnp.zeros_like(ref)

        hv = h_ref[...]
        r = lax.rsqrt(jnp.mean(hv * hv, axis=-1, keepdims=True) + EPS)
        n = hv * r
        diff = n * g_ref[...] - t_ref[...]
        loss_ref[...] += jnp.sum(diff * diff)
        dout = diff * (1.0 / Dm)
        dg_ref[...] += jnp.sum(dout * n, axis=0, keepdims=True)
        dn = dout * g_ref[...]
        dh = r * (dn - n * jnp.mean(dn * n, axis=-1, keepdims=True))
        dh_ref[...] = dh
        _gate_grads(dh, y_ref, gt_ref, dy_ref, dgt_ref, dsum_ref)

    vec = jax.ShapeDtypeStruct((1, Dm), F32)
    return pl.pallas_call(
        body, name=name, grid=(R // tm,),
        out_shape=[jax.ShapeDtypeStruct((R, Dm), F32), jax.ShapeDtypeStruct((1, LANES), F32), vec,
                   jax.ShapeDtypeStruct((R, Dm), BF16), vec, vec],
        in_specs=[_row_spec(tm, Dm), _vec_spec(Dm), _row_spec(tm, Dm), _row_spec(tm, Dm), _vec_spec(Dm)],
        out_specs=[_row_spec(tm, Dm), _vec_spec(LANES), _vec_spec(Dm), _row_spec(tm, Dm), _vec_spec(Dm),
                   _vec_spec(Dm)],
        compiler_params=pltpu.CompilerParams(dimension_semantics=("arbitrary",)),
    )(h, g, target, y, gt)


MOD_ROWS = 16


def _mod_fwd(c_rows, w_mod, name):
    L, Dm, n = w_mod.shape

    def body(c_ref, w_ref, o_ref):
        o_ref[0] = _dot3(_silu(c_ref[...]), w_ref[0], NN)

    return pl.pallas_call(
        body, name=name, grid=(L,),
        out_shape=jax.ShapeDtypeStruct((L, MOD_ROWS, n), F32),
        in_specs=[pl.BlockSpec((MOD_ROWS, Dm), lambda l: (0, 0)), pl.BlockSpec((1, Dm, n), lambda l: (l, 0, 0))],
        out_specs=pl.BlockSpec((1, MOD_ROWS, n), lambda l: (l, 0, 0)),
        compiler_params=pltpu.CompilerParams(dimension_semantics=("parallel",)),
    )(c_rows, w_mod)


def _mod_bwd(c_rows_t, dmod, w_mod, name):
    L, Dm, n = w_mod.shape

    def body(ct_ref, d_ref, w_ref, gw_ref, ds_ref):
        dm = d_ref[0]
        gw_ref[0] = _dot3(_silu(ct_ref[...]), dm, NN)
        ds_ref[0] = _dot3(dm[:MOD_ROWS], w_ref[0], NT)

    return pl.pallas_call(
        body, name=name, grid=(L,),
        out_shape=[jax.ShapeDtypeStruct((L, Dm, n), F32), jax.ShapeDtypeStruct((L, MOD_ROWS, Dm), F32)],
        in_specs=[pl.BlockSpec((Dm, LANES), lambda l: (0, 0)), pl.BlockSpec((1, LANES, n), lambda l: (l, 0, 0)),
                  pl.BlockSpec((1, Dm, n), lambda l: (l, 0, 0))],
        out_specs=[pl.BlockSpec((1, Dm, n), lambda l: (l, 0, 0)),
                   pl.BlockSpec((1, MOD_ROWS, Dm), lambda l: (l, 0, 0))],
        compiler_params=pltpu.CompilerParams(dimension_semantics=("parallel",)),
    )(c_rows_t, dmod, w_mod)


def _adam_update(w, g, m, v):
    c1 = 1.0 - ADAM_B1 ** ADAM_STEP
    c2 = 1.0 - ADAM_B2 ** ADAM_STEP
    mn = ADAM_B1 * m + (1.0 - ADAM_B1) * g
    vn = ADAM_B2 * v + (1.0 - ADAM_B2) * (g * g)
    return -ADAM_LR * ((mn / c1) / (jnp.sqrt(vn / c2) + ADAM_EPS) + ADAM_WD * w), mn, vn


def _adamw(w, g, m, v, name):
    R, Cw = w.shape
    tm = _tile(R, ADAM_ROWS, 8)

    def body(w_ref, g_ref, m_ref, v_ref, d_ref, mo_ref, vo_ref):
        d_ref[...], mo_ref[...], vo_ref[...] = _adam_update(w_ref[...], g_ref[...], m_ref[...], v_ref[...])

    spec = pl.BlockSpec((tm, Cw), lambda i: (i, 0))
    return pl.pallas_call(
        body, name=name, grid=(R // tm,),
        out_shape=[jax.ShapeDtypeStruct((R, Cw), F32)] * 3,
        in_specs=[spec] * 4, out_specs=[spec] * 3,
        compiler_params=pltpu.CompilerParams(dimension_semantics=("parallel",)),
    )(w, g, m, v)


def _adamw_recv(w, m, v, recvs, name):
    L, R, n = w.shape
    tm = _tile(R, ADAM_ROWS, 8)
    nblk = R // tm
    parts = [r.reshape(N_DEV, R, n) for r in recvs]

    def body(*refs):
        w_ref, m_ref, v_ref = refs[:3]
        part_refs = refs[3:3 + L]
        g_ref, d_ref, mo_ref, vo_ref, gsum = refs[3 + L:]
        l = pl.program_id(0)
        for ll in range(L):
            @pl.when(l == ll)
            def _(ll=ll):
                acc = part_refs[ll][0].astype(F32)
                for s in range(1, N_DEV):
                    acc = acc + part_refs[ll][s].astype(F32)
                gsum[...] = acc
        g = gsum[...]
        g_ref[0] = g
        d_ref[0], mo_ref[0], vo_ref[0] = _adam_update(w_ref[0], g, m_ref[0], v_ref[0])

    def part_spec(ll):
        return pl.BlockSpec((N_DEV, tm, n), lambda l, i: (0, jnp.where(l == ll, i, jnp.where(l < ll, 0, nblk - 1)), 0))

    spec = pl.BlockSpec((1, tm, n), lambda l, i: (l, i, 0))
    return pl.pallas_call(
        body, name=name, grid=(L, nblk),
        out_shape=[jax.ShapeDtypeStruct((L, R, n), F32)] * 4,
        in_specs=[spec] * 3 + [part_spec(ll) for ll in range(L)], out_specs=[spec] * 4,
        scratch_shapes=[pltpu.VMEM((tm, n), F32)],
        compiler_params=pltpu.CompilerParams(dimension_semantics=("parallel", "parallel")),
    )(w, m, v, *parts)


def _pack(parts, row_mult=8):
    flat, offs, pos = [], [], 0
    for t in parts:
        t = t.reshape(-1).astype(F32)
        size = -(-t.shape[0] // LANES) * LANES
        flat.append(jnp.pad(t, (0, size - t.shape[0])))
        offs.append(pos)
        pos += size
    total = -(-pos // (LANES * row_mult)) * (LANES * row_mult)
    if total > pos:
        flat.append(jnp.zeros((total - pos,), F32))
    return jnp.concatenate(flat).reshape(-1, LANES), offs


def _take(buf, off, shape):
    size = math.prod(shape)
    return buf[..., off:off + size].reshape(buf.shape[:-1] + tuple(shape))


def _rope_tables(T, ctx_rows):
    pos = jnp.arange(T)
    row = (pos // GRID_W).astype(F32)
    col = (pos % GRID_W).astype(F32)
    half = HEAD_DIM // 4
    inv = ROPE_THETA ** (-jnp.arange(0, 2 * half, 2, dtype=F32) / (2 * half))
    ang_r, ang_c = row[:, None] * inv[None, :], col[:, None] * inv[None, :]
    cos = jnp.concatenate([jnp.cos(ang_r)] * 2 + [jnp.cos(ang_c)] * 2, axis=1)
    sin = jnp.concatenate([-jnp.sin(ang_r), jnp.sin(ang_r), -jnp.sin(ang_c), jnp.sin(ang_c)], axis=1)
    cos = jnp.concatenate([jnp.ones((ctx_rows, HEAD_DIM), F32), cos], axis=0)
    sin = jnp.concatenate([jnp.zeros((ctx_rows, HEAD_DIM), F32), sin], axis=0)
    return jnp.tile(cos, (1, 2)), jnp.tile(sin, (1, 2))


def kernel(x, c, ctx, c_ctx, w_mod, b_mod, g_mix, g_ffn, w_ffn_in, w_ffn_out, w_in, q_gain, k_gain, w_sp, b_sp, w_out, w_pw1, b_pw1, w_dw, b_dw, ln_g, ln_b, w_pw2, b_pw2, g_final, loss_target, m_c_ctx, m_w_mod, m_b_mod, m_g_mix, m_g_ffn, m_w_ffn_in, m_w_ffn_out, m_w_in, m_q_gain, m_k_gain, m_w_sp, m_b_sp, m_w_out, m_w_pw1, m_b_pw1, m_w_dw, m_b_dw, m_ln_g, m_ln_b, m_w_pw2, m_b_pw2, m_g_final, v_c_ctx, v_w_mod, v_b_mod, v_g_mix, v_g_ffn, v_w_ffn_in, v_w_ffn_out, v_w_in, v_q_gain, v_k_gain, v_w_sp, v_b_sp, v_w_out, v_w_pw1, v_b_pw1, v_w_dw, v_b_dw, v_ln_g, v_ln_b, v_w_pw2, v_b_pw2, v_g_final):
    weights = dict(c_ctx=c_ctx, w_mod=w_mod, b_mod=b_mod, g_mix=g_mix, g_ffn=g_ffn, w_ffn_in=w_ffn_in,
                   w_ffn_out=w_ffn_out, w_in=w_in, q_gain=q_gain, k_gain=k_gain, w_sp=w_sp, b_sp=b_sp,
                   w_out=w_out, w_pw1=w_pw1, b_pw1=b_pw1, w_dw=w_dw, b_dw=b_dw, ln_g=ln_g, ln_b=ln_b,
                   w_pw2=w_pw2, b_pw2=b_pw2, g_final=g_final)
    moments_m = dict(c_ctx=m_c_ctx, w_mod=m_w_mod, b_mod=m_b_mod, g_mix=m_g_mix, g_ffn=m_g_ffn,
                     w_ffn_in=m_w_ffn_in, w_ffn_out=m_w_ffn_out, w_in=m_w_in, q_gain=m_q_gain,
                     k_gain=m_k_gain, w_sp=m_w_sp, b_sp=m_b_sp, w_out=m_w_out, w_pw1=m_w_pw1,
                     b_pw1=m_b_pw1, w_dw=m_w_dw, b_dw=m_b_dw, ln_g=m_ln_g, ln_b=m_ln_b, w_pw2=m_w_pw2,
                     b_pw2=m_b_pw2, g_final=m_g_final)
    moments_v = dict(c_ctx=v_c_ctx, w_mod=v_w_mod, b_mod=v_b_mod, g_mix=v_g_mix, g_ffn=v_g_ffn,
                     w_ffn_in=v_w_ffn_in, w_ffn_out=v_w_ffn_out, w_in=v_w_in, q_gain=v_q_gain,
                     k_gain=v_k_gain, w_sp=v_w_sp, b_sp=v_b_sp, w_out=v_w_out, w_pw1=v_w_pw1,
                     b_pw1=v_b_pw1, w_dw=v_w_dw, b_dw=v_b_dw, ln_g=v_ln_g, ln_b=v_ln_b, w_pw2=v_w_pw2,
                     b_pw2=v_b_pw2, g_final=v_g_final)
    names = list(weights)

    T, C = x.shape[1], ctx.shape[1]
    Dm = D_MODEL
    me = 4 * lax.axis_index("x") + 2 * lax.axis_index("y") + lax.axis_index("c")
    h0 = x[0]
    ctx2 = ctx[0]
    target = loss_target[0]

    small_sharded = (("w_dw", w_dw[0]), ("b_pw1", b_pw1), ("b_dw", b_dw), ("ln_g", ln_g), ("ln_b", ln_b),
                     ("b_pw2", b_pw2))
    buf1, offs1 = _pack([c] + [t for _, t in small_sharded])
    w_in_t, m_w_in_t, v_w_in_t = (jnp.swapaxes(t, 1, 2) for t in (w_in, m_w_in, v_w_in))
    w_ffi_t, m_w_ffi_t, v_w_ffi_t = (jnp.swapaxes(t, 1, 2) for t in (w_ffn_in, m_w_ffn_in, v_w_ffn_in))
    got1, W_in_t = _all_gather([buf1, w_in_t[0].astype(BF16)], "gather_cond", False)
    got1 = got1.reshape(N_DEV, -1)
    c_all = _take(got1, offs1[0], (Dm,))
    full_small = {}
    for (nm, t), off in zip(small_sharded, offs1[1:]):
        seg = _take(got1, off, t.shape)
        full_small[nm] = jnp.moveaxis(seg, 0, -2).reshape(t.shape[:-1] + (N_DEV * t.shape[-1],))
    w_dw_f, b_pw1_f = full_small["w_dw"], full_small["b_pw1"]
    b_dw_f, ln_g_f, ln_b_f, b_pw2_f = (full_small[k] for k in ("b_dw", "ln_g", "ln_b", "b_pw2"))

    c_rows = jnp.concatenate([c_all, c_ctx[None, :], jnp.zeros((MOD_ROWS - N_DEV - 1, Dm), F32)], axis=0)
    mod_part = _mod_fwd(c_rows, w_mod, "mod_fwd")
    n_mod = w_mod.shape[2]
    got2 = _all_gather([mod_part.reshape(-1, LANES)], "gather_mod", True)[0]
    mod_all = got2.reshape(N_DEV, 2, MOD_ROWS, n_mod).transpose(1, 2, 0, 3).reshape(2, MOD_ROWS, N_DEV * n_mod)
    mod_all = mod_all + b_mod[:, None, :]
    my_mod = lax.dynamic_index_in_dim(mod_all, me, axis=1, keepdims=False)
    sh1, sc1, gt1, sh2, sc2, gt2 = ([my_mod[l:l + 1, k * Dm:(k + 1) * Dm] for l in range(2)] for k in range(6))
    csh1 = mod_all[0, N_DEV:N_DEV + 1, 0:Dm]
    csc1 = mod_all[0, N_DEV:N_DEV + 1, Dm:2 * Dm]

    behind = got2[0:1, 0:1] * 0.0
    gather_groups = [[w_out[0]], [w_ffi_t[0], w_ffn_out[0]], [w_pw1[0], w_pw2[0]], [w_ffi_t[1], w_ffn_out[1]]]
    gathers = [_push_begin([(t + behind).astype(BF16) for t in grp], True, f"gather_start{k}")
               for k, grp in enumerate(gather_groups)]
    started = sum(h[4][0:1, 0:1] for h in gathers)

    def gathered(k, after):
        return _push_end(gathers[k], after, f"gather_wait{k}")[1]

    def ffn_weights(k, after):
        wi, wo = gathered(k, after)
        return wi.reshape(N_DEV, FF_SHARD, Dm), wo.reshape(N_DEV // 2, FF_SHARD, Dm)

    def col_gathered(t, n):
        return t.reshape(N_DEV, Dm, n).transpose(1, 0, 2).reshape(Dm, N_DEV * n)

    W_ffi, W_ffo = [None, None], [None, None]

    g_mix_r = [g_mix[l:l + 1] for l in range(2)]
    g_ffn_r = [g_ffn[l:l + 1] for l in range(2)]
    g_fin = g_final[None, :]

    cos, sin = _rope_tables(T, C)
    qg = jnp.tile(q_gain, (1, 2))
    kg = jnp.tile(k_gain, (1, 2))
    lane_head = jnp.arange(LANES) // HEAD_DIM
    bd = (lane_head[:, None] == lane_head[None, :]).astype(BF16)
    w_sp0 = w_sp[0]
    w_spt0 = w_sp0.transpose(0, 2, 1)
    b_spt0 = b_sp[0].T

    XM = _norm_mod_fwd_cat(ctx2, h0, g_mix_r[0], csc1, csh1, sc1[0] + started, sh1[0], "norm_mix0")
    P = _mm(XM, W_in_t, "nt", "in_proj", tm=1088, tn=IN_WIDTH)
    qh, kpad, vpad, kt, ao = _mix_prep_fwd(P, C, cos, sin, qg, kg, bd, w_sp0, b_spt0, "mix_prep")
    ao, lse = _attn_fwd(qh, kpad, vpad, ao, C, "attn_fwd")
    W_out, = gathered(0, ao)
    h1, y0, xf0 = _mm(ao, W_out, "nn", "out_proj", res=h0, gate=gt1[0], raw_out=True,
                      norm=(g_ffn_r[0], sc2[0], sh2[0]))

    def ffn_fwd(h_in, xf, l, norm_next):
        W_ffi[l], W_ffo[l] = ffn_weights(1 + 2 * l, xf)
        gu, act = _ffn_in_swiglu(xf, W_ffi[l], f"ffn_in{l}")
        outs = _mm_sum_shards(act, W_ffo[l], "nn", f"ffn_out{l}", res=h_in, gate=gt2[l], raw_out=True,
                              norm=norm_next)
        return tuple(outs) + (None,) * (3 - len(outs)) + (gu, act)

    h2, f0, xm1, gu0, act0 = ffn_fwd(h1, xf0, 0, (g_mix_r[1], sc1[1], sh1[1]))

    W_pw1, W_pw2 = gathered(2, xm1)
    W_pw1 = col_gathered(W_pw1, 2 * Dm // N_DEV)
    ag, hg = _pw1_glu(xm1, W_pw1, b_pw1_f, "pw1")
    hd = _conv_fwd(hg, w_dw_f, b_dw_f, "conv")
    hs, h3, y1, xf1 = _ln_silu_pw2(hd, ln_g_f, ln_b_f, W_pw2, b_pw2_f, h2, gt1[1],
                                   (g_ffn_r[1], sc2[1], sh2[1]), "pw2")
    h4, f1, _, gu1, act1 = ffn_fwd(h3, xf1, 1, None)

    dh4, sq_err, dg_final, df1, dgt2_1, _ = _final_fwd_bwd(h4, g_fin, target, f1, gt2[1], "loss_head")
    loss_local = (0.5 / Dm) * sq_err[0, 0:1]

    def col_shards(g, n):
        return g.reshape(Dm, N_DEV, n).transpose(1, 0, 2).reshape(N_DEV * Dm, n)

    def exchange_begin(k, parts):
        return _push_begin(parts, False, f"exchange_start{k}")

    def zero_of(handle):
        return handle[4][0:1, 0:1]

    def ffn_bwd(df, xf, gu, act, l):
        dw_out = _mm_tn_shard_rows(act, df, f"ffn_out_dw{l}", BF16)
        dgu = _ffn_out_dx_swiglu(df, W_ffo[l], gu, f"ffn_out_dx{l}").reshape(N_DEV, T, FF_SHARD)
        dw_in = _mm_tn_shard_rows(dgu, xf, f"ffn_in_dw{l}", BF16)
        dxf = _mm_sum_shards(dgu, W_ffi[l], "nn", f"ffn_in_dx{l}", BF16, tm=512)
        return dw_in, dw_out, dxf

    dW_ffi1, dW_ffo1, dxf1 = ffn_bwd(df1, xf1, gu1, act1, 1)
    ex0 = exchange_begin(0, [dW_ffi1.reshape(2 * D_FF, Dm), dW_ffo1.reshape(D_FF, Dm)])
    dh3, da, dsh, dy1, dgt1_1, db_pw2 = _norm_mod_bwd(h3, g_ffn_r[1], sc2[1], dxf1, dh4, "norm_ffn_bwd1",
                                                       gate=(y1, gt1[1] + zero_of(ex0)))
    dmod_ffn1 = (dsh, da * g_ffn_r[1], dgt2_1)
    dg_ffn1 = da * (1.0 + sc2[1])

    dW_pw2 = _mm(hs, dy1, "tn", "pw2_dw", BF16, tk=2048)
    dhd, dln_g, dln_b, db_dw = _pw2_dx_ln_silu_bwd(dy1, W_pw2, hd, ln_g_f, ln_b_f, "pw2_dx")
    dhg, dw_dw = _conv_bwd(dhd, hg, w_dw_f, "conv_bwd")
    dag, db_pw1, dxm1 = _glu_bwd_pw1_dx(ag, dhg, W_pw1, "pw1_dx")
    dW_pw1 = _mm(xm1, dag, "tn", "pw1_dw", BF16, tk=2048)
    ex1 = exchange_begin(1, [col_shards(dW_pw1, 2 * Dm // N_DEV), dW_pw2])
    dh2, da, dsh, df0, dgt2_0, _ = _norm_mod_bwd(h2, g_mix_r[1], sc1[1], dxm1, dh3, "norm_mix1_bwd",
                                                 gate=(f0, gt2[0] + zero_of(ex1)))
    dmod_mix1 = (dsh, da * g_mix_r[1], dgt1_1)
    dg_mix1 = da * (1.0 + sc1[1])

    dW_ffi0, dW_ffo0, dxf0 = ffn_bwd(df0, xf0, gu0, act0, 0)
    ex2 = exchange_begin(2, [dW_ffi0.reshape(2 * D_FF, Dm), dW_ffo0.reshape(D_FF, Dm)])
    dh1, da, dsh, dy0, dgt1_0, _ = _norm_mod_bwd(h1, g_ffn_r[0], sc2[0], dxf0, dh2, "norm_ffn_bwd0",
                                                 gate=(y0, gt1[0] + zero_of(ex2)))
    dmod_ffn0 = (dsh, da * g_ffn_r[0], dgt2_0)
    dg_ffn0 = da * (1.0 + sc2[0])

    dW_out = _mm(ao, dy0, "tn", "out_proj_dw", BF16, tk=2048)
    ex_out = exchange_begin(4, [dW_out])
    dao = _mm(dy0, W_out + zero_of(ex_out).astype(BF16), "nt", "out_proj_dx", BF16)
    dq, f_acc = _attn_bwd(qh, dao, ao, lse, kpad, vpad, kt, C, "attn_bwd")
    dP, dqg, dkg, dw_sp0, db_spt0 = _mix_prep_bwd(P, dq, f_acc, dao, C, cos, sin, qg, kg, bd, w_sp0, w_spt0,
                                                  b_spt0, "mix_prep_bwd")
    dW_in_t = _mm(dP, XM, "tn", "in_proj_dw", BF16, tm=896, tk=2176)
    dXM = _mm(dP, W_in_t, "nn", "in_proj_dx", BF16, tm=1088, tk=IN_WIDTH)
    dh0, da, dsh = _norm_mod_bwd(h0, g_mix_r[0], sc1[0], dXM, dh1, "norm_mix0_bwd", dxm_row_off=C)
    _, dac, dcsh = _norm_mod_bwd(ctx2, g_mix_r[0], csc1, dXM, None, "norm_ctx_bwd")
    dmod_mix0 = (dsh, da * g_mix_r[0], dgt1_0)
    dg_mix0 = da * (1.0 + sc1[0]) + dac * (1.0 + csc1)
    dcmod = jnp.concatenate([dcsh, dac * g_mix_r[0]], axis=1)

    dmod_mine = jnp.stack([jnp.concatenate(dmod_mix0 + dmod_ffn0, axis=1)[0],
                           jnp.concatenate(dmod_mix1 + dmod_ffn1, axis=1)[0]])

    small_grads = [
        ("loss", loss_local), ("g_final", dg_final), ("g_mix", jnp.concatenate([dg_mix0, dg_mix1])),
        ("g_ffn", jnp.concatenate([dg_ffn0, dg_ffn1])),
        ("q_gain", dqg[:, :HEAD_DIM] + dqg[:, HEAD_DIM:]), ("k_gain", dkg[:, :HEAD_DIM] + dkg[:, HEAD_DIM:]),
        ("w_sp", dw_sp0[None]), ("b_sp", db_spt0.T[None]), ("b_pw1", db_pw1), ("w_dw", dw_dw[None]),
        ("b_dw", db_dw), ("ln_g", dln_g), ("ln_b", dln_b), ("b_pw2", db_pw2), ("dcmod", dcmod),
        ("dmod", dmod_mine),
    ]
    buf3, offs3 = _pack([t for _, t in small_grads])
    off3 = {nm: off for (nm, _), off in zip(small_grads, offs3)}
    shape3 = {nm: t.shape for nm, t in small_grads}
    small_push = _push_begin([buf3], True, "small_grads_start")
    ex3 = exchange_begin(3, [dW_in_t + zero_of(small_push).astype(BF16)])

    grads, delta, new_m, new_v = {}, {}, {}, {}

    def exchanged(k, handle, after):
        return _push_end(handle, after, f"exchange_wait{k}")[1]

    def adamw_big(nm, parts, transposed=False, wmv=None):
        w3, m3, v3 = wmv if wmv is not None else (weights[nm], moments_m[nm], moments_v[nm])
        outs4 = _adamw_recv(w3, m3, v3, parts, f"adamw_{nm}")
        if transposed:
            outs4 = [jnp.swapaxes(t, 1, 2) for t in outs4]
        grads[nm], delta[nm], new_m[nm], new_v[nm] = outs4

    pushed = ex3[4]
    r_ffi1, r_ffo1 = exchanged(0, ex0, pushed)
    r_pw1, r_pw2 = exchanged(1, ex1, pushed)
    r_ffi0, r_ffo0 = exchanged(2, ex2, pushed)
    r_out, = exchanged(4, ex_out, pushed)
    adamw_big("w_ffn_in", [r_ffi0, r_ffi1], True, (w_ffi_t, m_w_ffi_t, v_w_ffi_t))
    adamw_big("w_ffn_out", [r_ffo0, r_ffo1])
    adamw_big("w_pw1", [r_pw1])
    adamw_big("w_pw2", [r_pw2])
    adamw_big("w_out", [r_out])

    got3 = _push_end(small_push, delta["w_out"], "small_grads_wait")[1][0].reshape(N_DEV, buf3.shape[0], LANES)
    sum3 = _sum_devices(got3, "sum_small_grads").reshape(-1)

    def summed(nm):
        return _take(sum3, off3[nm], shape3[nm])

    loss = summed("loss")[0]
    dcmod_sum = summed("dcmod")
    dmod_rows = _take(got3.reshape(N_DEV, -1), off3["dmod"], (2, 6 * Dm)).transpose(1, 0, 2)
    ctx_row = jnp.concatenate([jnp.pad(dcmod_sum, ((0, 0), (0, 4 * Dm))), jnp.zeros((1, 6 * Dm), F32)])
    dmod_all = jnp.concatenate([dmod_rows, ctx_row[:, None, :],
                                jnp.zeros((2, LANES - N_DEV - 1, 6 * Dm), F32)], axis=1)
    grads["b_mod"] = summed("dmod") + ctx_row
    dmod_shard = lax.dynamic_slice_in_dim(dmod_all, me * n_mod, n_mod, axis=2)
    c_rows_t = jnp.pad(c_rows.T, ((0, 0), (0, LANES - MOD_ROWS)))
    grads["w_mod"], ds_part = _mod_bwd(c_rows_t, dmod_shard, w_mod, "mod_bwd")

    buf4, _ = _pack([ds_part[0, N_DEV]])
    got4 = _all_gather([buf4], "gather_c_ctx_grad", True)[0].reshape(N_DEV, buf4.shape[0], LANES)
    ds_ctx = _sum_devices(got4, "sum_c_ctx_grad").reshape(-1)[:Dm]
    grads["c_ctx"] = ds_ctx * _dsilu(c_ctx)

    for nm in ("g_final", "g_mix", "g_ffn", "q_gain", "k_gain", "w_sp", "b_sp"):
        grads[nm] = summed(nm).reshape(weights[nm].shape)
    for nm in ("b_pw1", "w_dw", "b_dw", "ln_g", "ln_b", "b_pw2"):
        n_loc = weights[nm].shape[-1]
        grads[nm] = lax.dynamic_slice_in_dim(summed(nm), me * n_loc, n_loc, axis=-1).reshape(weights[nm].shape)

    shp = w_mod.shape
    outs = _adamw(w_mod.reshape(-1, shp[-1]), grads["w_mod"].reshape(-1, shp[-1]),
                  m_w_mod.reshape(-1, shp[-1]), v_w_mod.reshape(-1, shp[-1]), "adamw_w_mod")
    delta["w_mod"], new_m["w_mod"], new_v["w_mod"] = (o.reshape(shp) for o in outs)
    big_names = ("w_mod", "w_ffn_in", "w_ffn_out", "w_in", "w_out", "w_pw1", "w_pw2")
    small_names = [nm for nm in names if nm not in big_names]
    packs = [_pack([src[nm] for nm in small_names]) for src in (weights, grads, moments_m, moments_v)]
    offs_s = packs[0][1]
    outs = _adamw(*[pk[0] for pk in packs], "adamw_small")
    for o, dst in zip(outs, (delta, new_m, new_v)):
        o = o.reshape(-1)
        for nm, off in zip(small_names, offs_s):
            dst[nm] = _take(o, off, weights[nm].shape)
    r_in, = exchanged(3, ex3, outs[0])
    adamw_big("w_in", [r_in], True, (w_in_t, m_w_in_t, v_w_in_t))

    return (loss, dh0[None], *[grads[n] for n in names], *[delta[n] for n in names],
            *[new_m[n] for n in names], *[new_v[n] for n in names])
```

```python
import math

import jax
import jax.numpy as jnp
from jax import lax
from jax.experimental import pallas as pl
from jax.experimental.pallas import tpu as pltpu

F32 = jnp.float32
BF16 = jnp.bfloat16
MESH = pl.DeviceIdType.MESH

N_DEV = 8
D_MODEL = 1024
EPS = 1e-6
HEAD_DIM = 64
ATTN_WIDTH = 512
KV_WIDTH = 128
SG_WIDTH = 512
N_SG_GROUPS = 4
CHUNK = 128
IN_WIDTH = 1792
D_FF = 2816
FF_SHARD = 2 * D_FF // N_DEV
CONV_WIDTH = 31
CONV_HALO = 16
GRID_W = 64
ROPE_THETA = 10000.0
LANES = 128
SUBLANES = 8
ROW_BLOCK = 512
ADAM_ROWS = 256
ADAM_LR, ADAM_B1, ADAM_B2, ADAM_EPS, ADAM_WD, ADAM_STEP = 0.001, 0.9, 0.999, 1e-08, 0.01, 10


def _tile(n, target, mult=LANES):
    best = None
    for t in range(mult, min(n, target) + 1, mult):
        if n % t == 0:
            best = t
    return best if best is not None else n


def _sigmoid(x):
    return 1.0 / (1.0 + jnp.exp(-x))


def _silu(x):
    return x * _sigmoid(x)


def _dsilu(x):
    s = _sigmoid(x)
    return s * (1.0 + x * (1.0 - s))


_GELU_K = math.sqrt(2.0 / math.pi)


def _gelu(x):
    return 0.5 * x * (1.0 + jnp.tanh(_GELU_K * (x + 0.044715 * x * x * x)))


def _gelu_and_grad(x):
    x2 = x * x
    t = jnp.tanh(_GELU_K * x * (1.0 + 0.044715 * x2))
    half = 0.5 * (1.0 + t)
    return x * half, half + 0.5 * x * (1.0 - t * t) * _GELU_K * (1.0 + 3.0 * 0.044715 * x2)


def _split_bf16(x):
    hi = x.astype(BF16)
    lo = (x - hi.astype(F32)).astype(BF16)
    return hi, lo


def _dot(a, b, dims):
    return lax.dot_general(a, b, (dims, ((), ())), preferred_element_type=F32)


def _dot3(a, b, dims):
    ah, al = _split_bf16(a)
    bh, bl = _split_bf16(b)
    return _dot(ah, bh, dims) + _dot(ah, bl, dims) + _dot(al, bh, dims)


NN = ((1,), (0,))
NT = ((1,), (1,))
TN = ((0,), (0,))


def _all_gather(xs, name, in_vmem):
    n_arr = len(xs)

    def body(*refs):
        x_refs, out_refs = refs[:n_arr], refs[n_arr:2 * n_arr]
        send_sems, recv_sems, local_sems = refs[2 * n_arr:]
        x, y, c = lax.axis_index("x"), lax.axis_index("y"), lax.axis_index("c")
        me, sibling = (x, y, c), (x, y, 1 - c)
        chips = [(1 - x, y), (x, 1 - y), (1 - x, 1 - y)]

        def rows(a, px, py, pc):
            m_per = xs[a].shape[0]
            return out_refs[a].at[pl.ds((4 * px + 2 * py + pc) * m_per, m_per), :]

        def copy(a, k, block, to, src=None):
            return pltpu.make_async_remote_copy(
                src_ref=rows(a, *block) if src is None else src,
                dst_ref=rows(a, *block),
                send_sem=send_sems.at[7 * a + k],
                recv_sem=recv_sems.at[7 * a + k],
                device_id=to,
                device_id_type=MESH,
            )

        mine, first, passed = [], [], []
        for a in range(n_arr):
            mine.append(pltpu.make_async_copy(x_refs[a], rows(a, *me), local_sems.at[a]))
            mine[-1].start()
            first.append(copy(a, 0, me, sibling, src=x_refs[a]))
            first += [copy(a, 1 + j, me, (*chip, c), src=x_refs[a]) for j, chip in enumerate(chips)]
        for cp in first:
            cp.start()
        for a in range(n_arr):
            for j, chip in enumerate(chips):
                copy(a, 1 + j, (*chip, c), me).wait_recv()
                passed.append(copy(a, 4 + j, (*chip, c), sibling))
                passed[-1].start()
        for a in range(n_arr):
            copy(a, 0, sibling, me).wait_recv()
            for j, chip in enumerate(chips):
                copy(a, 4 + j, (*chip, 1 - c), me).wait_recv()
        for cp in first + passed:
            cp.wait_send()
        for cp in mine:
            cp.wait()

    space = pltpu.VMEM if in_vmem else pl.ANY
    return pl.pallas_call(
        body,
        name=name,
        out_shape=[jax.ShapeDtypeStruct((N_DEV * t.shape[0], t.shape[1]), t.dtype) for t in xs],
        in_specs=[pl.BlockSpec(memory_space=space)] * n_arr,
        out_specs=[pl.BlockSpec(memory_space=space)] * n_arr,
        scratch_shapes=[
            pltpu.SemaphoreType.DMA((7 * n_arr,)),
            pltpu.SemaphoreType.DMA((7 * n_arr,)),
            pltpu.SemaphoreType.DMA((n_arr,)),
        ],
    )(*xs)


HBM_SPEC = pl.BlockSpec(memory_space=pltpu.HBM)
SEM_SPEC = pl.BlockSpec(memory_space=pltpu.SEMAPHORE)
DATAFLOW_EFFECT = pltpu.SideEffectType.DATAFLOW_SIDE_EFFECTING


def _peers(x, y, c):
    for k in range(1, N_DEV):
        px = 1 - x if (k >> 2) & 1 else x
        py = 1 - y if (k >> 1) & 1 else y
        pc = 1 - c if k & 1 else c
        yield k - 1, (px, py, pc), 4 * px + 2 * py + pc


def _push_copies(src_refs, land_refs, send_sems, recv_sems, shapes, whole_src):
    x, y, c = lax.axis_index("x"), lax.axis_index("y"), lax.axis_index("c")
    me = 4 * x + 2 * y + c
    for a, (m_per, _) in enumerate(shapes):
        def block(ref, idx, m_per=m_per):
            return ref.at[pl.ds(idx * m_per, m_per), :]

        for k, peer, pidx in _peers(x, y, c):
            src = src_refs[a] if whole_src else block(src_refs[a], pidx)
            sems = dict(send_sem=send_sems.at[N_DEV * a + k], recv_sem=recv_sems.at[N_DEV * a + k],
                        device_id=peer, device_id_type=MESH)
            yield (pltpu.make_async_remote_copy(src_ref=src, dst_ref=block(land_refs[a], me), **sems),
                   pltpu.make_async_remote_copy(src_ref=src, dst_ref=block(land_refs[a], pidx), **sems))


def _own_copies(src_refs, land_refs, recv_sems, shapes, whole_src):
    me = 4 * lax.axis_index("x") + 2 * lax.axis_index("y") + lax.axis_index("c")
    for a, (m_per, _) in enumerate(shapes):
        mine = pl.ds(me * m_per, m_per)
        src = src_refs[a] if whole_src else src_refs[a].at[mine, :]
        yield pltpu.make_async_copy(src, land_refs[a].at[mine, :], recv_sems.at[N_DEV * a + N_DEV - 1])


def _push_begin(srcs, whole_src, name):
    n_arr = len(srcs)
    shapes = [(t.shape[0] if whole_src else t.shape[0] // N_DEV, t.shape[1]) for t in srcs]
    lands = [lax.empty((N_DEV * m, n), t.dtype) for (m, n), t in zip(shapes, srcs)]

    def body(*refs):
        src_refs, land_refs = refs[:n_arr], refs[n_arr:2 * n_arr]
        send_sems, recv_sems = refs[2 * n_arr], refs[2 * n_arr + 1]
        token = refs[-1]
        for outgoing, _ in _push_copies(src_refs, land_refs, send_sems, recv_sems, shapes, whole_src):
            outgoing.start()
        for own in _own_copies(src_refs, land_refs, recv_sems, shapes, whole_src):
            own.start()
        token[...] = jnp.zeros_like(token)

    operands = [pltpu.with_memory_space_constraint(t, pltpu.HBM) for t in list(srcs) + lands]
    outs = pl.pallas_call(
        body, name=name,
        out_shape=(pltpu.SemaphoreType.DMA((N_DEV * n_arr,)), pltpu.SemaphoreType.DMA((N_DEV * n_arr,)),
                   *[pltpu.HBM(t.shape, t.dtype) for t in operands],
                   jax.ShapeDtypeStruct((SUBLANES, LANES), F32)),
        in_specs=[HBM_SPEC] * (2 * n_arr),
        out_specs=(SEM_SPEC, SEM_SPEC, *[HBM_SPEC] * (2 * n_arr), pl.BlockSpec(memory_space=pltpu.VMEM)),
        input_output_aliases={i: 2 + i for i in range(2 * n_arr)},
        compiler_params=pltpu.CompilerParams(has_side_effects=DATAFLOW_EFFECT),
    )(*operands)
    return outs[0], outs[1], list(outs[2:2 + n_arr]), list(outs[2 + n_arr:2 + 2 * n_arr]), outs[-1], whole_src


def _push_end(handle, after, name):
    send_sems, recv_sems, srcs, lands, _, whole_src = handle
    n_arr = len(srcs)
    shapes = [(t.shape[0] // N_DEV, t.shape[1]) for t in lands]

    def body(*refs):
        src_refs, land_refs = refs[:n_arr], refs[n_arr:2 * n_arr]
        send_sems_ref, recv_sems_ref = refs[2 * n_arr], refs[2 * n_arr + 1]
        for outgoing, incoming in _push_copies(src_refs, land_refs, send_sems_ref, recv_sems_ref, shapes, whole_src):
            outgoing.wait_send()
            incoming.wait_recv()
        for own in _own_copies(src_refs, land_refs, recv_sems_ref, shapes, whole_src):
            own.wait()

    outs = pl.pallas_call(
        body, name=name,
        out_shape=tuple(pltpu.HBM(t.shape, t.dtype) for t in srcs + lands),
        in_specs=[HBM_SPEC] * (2 * n_arr) + [SEM_SPEC, SEM_SPEC, pl.BlockSpec(memory_space=pl.ANY)],
        out_specs=tuple([HBM_SPEC] * (2 * n_arr)),
        input_output_aliases={i: i for i in range(2 * n_arr)},
        compiler_params=pltpu.CompilerParams(has_side_effects=DATAFLOW_EFFECT),
    )(*srcs, *lands, send_sems, recv_sems, after)
    return list(outs[:n_arr]), list(outs[n_arr:])


def _sum_devices(r, name, rows_per_step=ADAM_ROWS):
    _, m, n = r.shape
    tm = _tile(m, rows_per_step, 8)

    def body(r_ref, o_ref):
        acc = r_ref[0].astype(F32)
        for s in range(1, N_DEV):
            acc = acc + r_ref[s].astype(F32)
        o_ref[...] = acc

    return pl.pallas_call(
        body,
        name=name,
        grid=(m // tm,),
        out_shape=jax.ShapeDtypeStruct((m, n), F32),
        in_specs=[pl.BlockSpec((N_DEV, tm, n), lambda i: (0, i, 0))],
        out_specs=pl.BlockSpec((tm, n), lambda i: (i, 0)),
        compiler_params=pltpu.CompilerParams(dimension_semantics=("parallel",)),
    )(r)


def _get(ref):
    return ref[0] if len(ref.shape) == 3 else ref[...]


def _put(ref, val):
    if len(ref.shape) == 3:
        ref[0] = val
    else:
        ref[...] = val


def _norm_mod(hv, g, sc, sh):
    r = lax.rsqrt(jnp.mean(hv * hv, axis=-1, keepdims=True) + EPS)
    return (hv * r) * g * (1.0 + sc) + sh


def _mm_call(name, a, b, a_spec, b_spec, out_sds, o_spec, grid, dims, acc_shape, bias=None,
             res=None, gate=None, raw_out=False, vec_spec=None, norm=None):
    nk = grid[2]
    operands, in_specs = [a, b], [a_spec, b_spec]
    if bias is not None:
        operands.append(bias)
        in_specs.append(vec_spec)
    if res is not None:
        operands += [res, gate]
        in_specs += [o_spec, vec_spec]
    if norm is not None:
        assert grid[1] == 1
        operands += list(norm)
        in_specs += [vec_spec] * 3
    out_shape, out_specs = [out_sds], [o_spec]
    if raw_out:
        out_shape.append(jax.ShapeDtypeStruct(out_sds.shape, BF16))
        out_specs.append(o_spec)
    if norm is not None:
        out_shape.append(jax.ShapeDtypeStruct(out_sds.shape, BF16))
        out_specs.append(o_spec)

    def body(*refs):
        it = iter(refs)
        a_ref, b_ref = next(it), next(it)
        bias_ref = next(it) if bias is not None else None
        res_ref, gate_ref = (next(it), next(it)) if res is not None else (None, None)
        norm_refs = (next(it), next(it), next(it)) if norm is not None else None
        o_ref = next(it)
        raw_ref = next(it) if raw_out else None
        xn_ref = next(it) if norm is not None else None
        acc = next(it) if nk > 1 else None
        k = pl.program_id(2)
        part = _dot(_get(a_ref).astype(BF16), _get(b_ref).astype(BF16), dims)

        def finish(y):
            if bias_ref is not None:
                y = y + bias_ref[...]
            if raw_ref is not None:
                raw_ref[...] = y.astype(BF16)
            if res_ref is not None:
                y = res_ref[...] + gate_ref[...] * y
            _put(o_ref, y.astype(out_sds.dtype))
            if xn_ref is not None:
                xn_ref[...] = _norm_mod(y, *[r[...] for r in norm_refs]).astype(BF16)

        if nk == 1:
            finish(part)
        else:
            @pl.when(k == 0)
            def _():
                acc[...] = part

            @pl.when(k > 0)
            def _():
                acc[...] += part

            @pl.when(k == nk - 1)
            def _():
                finish(acc[...])

    outs = pl.pallas_call(
        body,
        name=name,
        grid=grid,
        out_shape=out_shape,
        in_specs=in_specs,
        out_specs=out_specs,
        scratch_shapes=[pltpu.VMEM(acc_shape, F32)] if nk > 1 else [],
        compiler_params=pltpu.CompilerParams(dimension_semantics=("parallel", "parallel", "arbitrary")),
    )(*operands)
    return outs if len(outs) > 1 else outs[0]


def _mm(a, b, mode, name, out_dtype=F32, bias=None, res=None, gate=None, raw_out=False,
        tm=1024, tn=1024, tk=1024, a_row_off=0, norm=None):
    if mode == "nn":
        K, N = b.shape
        M = a.shape[0] - a_row_off
    elif mode == "nt":
        N, K = b.shape
        M = a.shape[0] - a_row_off
    else:
        (K, M), N = a.shape, b.shape[1]
    tm, tn, tk = _tile(M, tm, LANES if mode == "tn" else 2 * SUBLANES), _tile(N, tn), _tile(K, tk)
    off = a_row_off // tm
    dims = {"nn": NN, "nt": NT, "tn": TN}[mode]
    a_spec = (pl.BlockSpec((tk, tm), lambda i, j, k: (k, i)) if mode == "tn"
              else pl.BlockSpec((tm, tk), lambda i, j, k: (i + off, k)))
    b_spec = (pl.BlockSpec((tn, tk), lambda i, j, k: (j, k)) if mode == "nt"
              else pl.BlockSpec((tk, tn), lambda i, j, k: (k, j)))
    return _mm_call(name, a, b, a_spec, b_spec, jax.ShapeDtypeStruct((M, N), out_dtype),
                    pl.BlockSpec((tm, tn), lambda i, j, k: (i, j)), (M // tm, N // tn, K // tk), dims,
                    (tm, tn), bias, res, gate, raw_out, pl.BlockSpec((1, tn), lambda i, j, k: (0, j)), norm)


def _mm_sum_shards(a3, b3, mode, name, out_dtype=F32, res=None, gate=None, raw_out=False, tm=512, norm=None):
    S, M, kk = a3.shape
    N = b3.shape[2] if mode == "nn" else b3.shape[1]
    tm = _tile(M, tm)
    dims = NN if mode == "nn" else NT
    has_res = res is not None

    def body(*refs):
        it = iter(refs)
        a_ref, b_ref = next(it), next(it)
        res_ref, gate_ref = (next(it), next(it)) if has_res else (None, None)
        norm_refs = (next(it), next(it), next(it)) if norm is not None else None
        o_ref = next(it)
        raw_ref = next(it) if raw_out else None
        xn_ref = next(it) if norm is not None else None
        y = _dot(a_ref[0], b_ref[0], dims)
        for s in range(1, S):
            y = y + _dot(a_ref[s], b_ref[s], dims)
        if raw_ref is not None:
            raw_ref[...] = y.astype(BF16)
        if has_res:
            y = res_ref[...] + gate_ref[...] * y
        o_ref[...] = y.astype(out_dtype)
        if xn_ref is not None:
            xn_ref[...] = _norm_mod(y, *[r[...] for r in norm_refs]).astype(BF16)

    tile = pl.BlockSpec((tm, N), lambda i: (i, 0))
    operands = [a3, b3] + ([res, gate] if has_res else []) + (list(norm) if norm is not None else [])
    in_specs = [pl.BlockSpec((S, tm, kk), lambda i: (0, i, 0)), pl.BlockSpec(b3.shape, lambda i: (0, 0, 0))]
    in_specs += [tile, _vec_spec(N)] if has_res else []
    in_specs += [_vec_spec(N)] * 3 if norm is not None else []
    out_shape = [jax.ShapeDtypeStruct((M, N), out_dtype)] + ([jax.ShapeDtypeStruct((M, N), BF16)] if raw_out else [])
    out_shape += [jax.ShapeDtypeStruct((M, N), BF16)] if norm is not None else []
    outs = pl.pallas_call(
        body, name=name, grid=(M // tm,),
        out_shape=out_shape, in_specs=in_specs, out_specs=[tile] * len(out_shape),
        compiler_params=pltpu.CompilerParams(dimension_semantics=("parallel",)),
    )(*operands)
    return outs if len(outs) > 1 else outs[0]


def _mm_tn_shard_rows(a3, b, name, out_dtype, tn=1024, tk=4096):
    S, T, m = a3.shape
    N = b.shape[1]
    tn, tk = _tile(N, tn), _tile(T, tk)
    return _mm_call(name, a3, b, pl.BlockSpec((1, tk, m), lambda i, j, k: (i, k, 0)),
                    pl.BlockSpec((tk, tn), lambda i, j, k: (k, j)), jax.ShapeDtypeStruct((S, m, N), out_dtype),
                    pl.BlockSpec((1, m, tn), lambda i, j, k: (i, 0, j)), (S, N // tn, T // tk), TN, (m, tn))


def _row_spec(tm, width, off=0):
    return pl.BlockSpec((tm, width), lambda i: (i + off, 0))


def _vec_spec(width):
    return pl.BlockSpec((1, width), lambda i: (0, 0))


def _norm_mod_fwd_cat(hc, h, g, csc, csh, sc, sh, name):
    (C, Dm), T = hc.shape, h.shape[0]
    tm = _tile(math.gcd(C, T), ROW_BLOCK, 8)
    off = C // tm

    def body(hc_ref, h_ref, g_ref, csc_ref, csh_ref, sc_ref, sh_ref, o_ref):
        is_ctx = pl.program_id(0) < off
        hv = jnp.where(is_ctx, hc_ref[...], h_ref[...])
        scv = jnp.where(is_ctx, csc_ref[...], sc_ref[...])
        shv = jnp.where(is_ctx, csh_ref[...], sh_ref[...])
        r = lax.rsqrt(jnp.mean(hv * hv, axis=-1, keepdims=True) + EPS)
        o_ref[...] = ((hv * r) * g_ref[...] * (1.0 + scv) + shv).astype(BF16)

    return pl.pallas_call(
        body, name=name, grid=((C + T) // tm,),
        out_shape=jax.ShapeDtypeStruct((C + T, Dm), BF16),
        in_specs=[pl.BlockSpec((tm, Dm), lambda i: (jnp.minimum(i, off - 1), 0)),
                  pl.BlockSpec((tm, Dm), lambda i: (jnp.maximum(i - off, 0), 0))] + [_vec_spec(Dm)] * 5,
        out_specs=_row_spec(tm, Dm),
        compiler_params=pltpu.CompilerParams(dimension_semantics=("parallel",)),
    )(hc, h, g, csc, csh, sc, sh)


def _gate_grads(dh, y_ref, gt_ref, dy_ref, dgt_ref, dsum_ref):
    dy = dh * gt_ref[...]
    dgt_ref[...] += jnp.sum(dh * y_ref[...].astype(F32), axis=0, keepdims=True)
    dsum_ref[...] += jnp.sum(dy, axis=0, keepdims=True)
    dy_ref[...] = dy.astype(BF16)


def _norm_mod_bwd(h, g, sc, dxm, dres, name, dxm_row_off=0, gate=None):
    R, Dm = h.shape
    tm = _tile(math.gcd(R, dxm_row_off) if dxm_row_off else R, ROW_BLOCK, 8)
    off = dxm_row_off // tm
    has_res = dres is not None
    has_gate = gate is not None

    def body(*refs):
        it = iter(refs)
        h_ref, g_ref, sc_ref, dx_ref = next(it), next(it), next(it), next(it)
        dres_ref = next(it) if has_res else None
        y_ref, gt_ref = (next(it), next(it)) if has_gate else (None, None)
        dh_ref, da_ref, dsh_ref = next(it), next(it), next(it)
        gate_out = (next(it), next(it), next(it)) if has_gate else ()
        i = pl.program_id(0)

        @pl.when(i == 0)
        def _():
            for ref in (da_ref, dsh_ref) + gate_out[1:]:
                ref[...] = jnp.zeros_like(ref)

        hv = h_ref[...]
        dx = dx_ref[...].astype(F32)
        r = lax.rsqrt(jnp.mean(hv * hv, axis=-1, keepdims=True) + EPS)
        n = hv * r
        da_ref[...] += jnp.sum(dx * n, axis=0, keepdims=True)
        dsh_ref[...] += jnp.sum(dx, axis=0, keepdims=True)
        dn = dx * (g_ref[...] * (1.0 + sc_ref[...]))
        dh = r * (dn - n * jnp.mean(dn * n, axis=-1, keepdims=True))
        if has_res:
            dh = dh + dres_ref[...]
        dh_ref[...] = dh
        if has_gate:
            _gate_grads(dh, y_ref, gt_ref, *gate_out)

    operands = [h, g, sc, dxm] + ([dres] if has_res else []) + (list(gate) if has_gate else [])
    in_specs = [_row_spec(tm, Dm), _vec_spec(Dm), _vec_spec(Dm), _row_spec(tm, Dm, off)]
    in_specs += [_row_spec(tm, Dm)] if has_res else []
    in_specs += [_row_spec(tm, Dm), _vec_spec(Dm)] if has_gate else []
    vec = jax.ShapeDtypeStruct((1, Dm), F32)
    out_shape = [jax.ShapeDtypeStruct((R, Dm), F32), vec, vec]
    out_specs = [_row_spec(tm, Dm), _vec_spec(Dm), _vec_spec(Dm)]
    if has_gate:
        out_shape += [jax.ShapeDtypeStruct((R, Dm), BF16), vec, vec]
        out_specs += [_row_spec(tm, Dm), _vec_spec(Dm), _vec_spec(Dm)]
    return pl.pallas_call(
        body, name=name, grid=(R // tm,),
        out_shape=out_shape, in_specs=in_specs, out_specs=out_specs,
        compiler_params=pltpu.CompilerParams(dimension_semantics=("arbitrary",)),
    )(*operands)


def _ffn_in_dx_norm_bwd(dgu, w3, h, g, sc, dres, y, gt, name, tm=256):
    S, T, n = dgu.shape
    Dm = w3.shape[2]
    tm = _tile(T, tm)

    def body(a_ref, b_ref, h_ref, g_ref, sc_ref, dres_ref, y_ref, gt_ref,
             dh_ref, da_ref, dsh_ref, dy_ref, dgt_ref, dsum_ref):
        @pl.when(pl.program_id(0) == 0)
        def _():
            for ref in (da_ref, dsh_ref, dgt_ref, dsum_ref):
                ref[...] = jnp.zeros_like(ref)

        dx = _dot(a_ref[0], b_ref[0], NN)
        for s in range(1, S):
            dx = dx + _dot(a_ref[s], b_ref[s], NN)
        dx = dx.astype(BF16).astype(F32)
        hv = h_ref[...]
        r = lax.rsqrt(jnp.mean(hv * hv, axis=-1, keepdims=True) + EPS)
        nrm = hv * r
        da_ref[...] += jnp.sum(dx * nrm, axis=0, keepdims=True)
        dsh_ref[...] += jnp.sum(dx, axis=0, keepdims=True)
        dn = dx * (g_ref[...] * (1.0 + sc_ref[...]))
        dh = r * (dn - nrm * jnp.mean(dn * nrm, axis=-1, keepdims=True)) + dres_ref[...]
        dh_ref[...] = dh
        _gate_grads(dh, y_ref, gt_ref, dy_ref, dgt_ref, dsum_ref)

    vec = jax.ShapeDtypeStruct((1, Dm), F32)
    return pl.pallas_call(
        body, name=name, grid=(T // tm,),
        out_shape=[jax.ShapeDtypeStruct((T, Dm), F32), vec, vec, jax.ShapeDtypeStruct((T, Dm), BF16), vec, vec],
        in_specs=[pl.BlockSpec((S, tm, n), lambda i: (0, i, 0)), pl.BlockSpec(w3.shape, lambda i: (0, 0, 0)),
                  _row_spec(tm, Dm), _vec_spec(Dm), _vec_spec(Dm), _row_spec(tm, Dm), _row_spec(tm, Dm),
                  _vec_spec(Dm)],
        out_specs=[_row_spec(tm, Dm), _vec_spec(Dm), _vec_spec(Dm), _row_spec(tm, Dm), _vec_spec(Dm),
                   _vec_spec(Dm)],
        compiler_params=pltpu.CompilerParams(dimension_semantics=("arbitrary",)),
    )(dgu, w3, h, g, sc, dres, y, gt)


def _ffn_in_swiglu(xf, w3, name, tm=1024):
    T, K = xf.shape
    S, n, _ = w3.shape
    half = S // 2
    tm = _tile(T, tm)

    def body(a_ref, wg_ref, wu_ref, gu_ref, act_ref):
        a = a_ref[...]
        g = _dot(a, wg_ref[0], NT)
        u = _dot(a, wu_ref[0], NT)
        gu_ref[0, 0] = g.astype(BF16)
        gu_ref[1, 0] = u.astype(BF16)
        act_ref[0] = (_silu(g) * u).astype(BF16)

    return pl.pallas_call(
        body, name=name, grid=(T // tm, half),
        out_shape=[jax.ShapeDtypeStruct((2, half, T, n), BF16), jax.ShapeDtypeStruct((half, T, n), BF16)],
        in_specs=[pl.BlockSpec((tm, K), lambda i, j: (i, 0)),
                  pl.BlockSpec((1, n, K), lambda i, j: (j, 0, 0)),
                  pl.BlockSpec((1, n, K), lambda i, j: (j + half, 0, 0))],
        out_specs=[pl.BlockSpec((2, 1, tm, n), lambda i, j: (0, j, i, 0)),
                   pl.BlockSpec((1, tm, n), lambda i, j: (j, i, 0))],
        compiler_params=pltpu.CompilerParams(dimension_semantics=("parallel", "parallel")),
    )(xf, w3, w3)


def _ffn_out_dx_swiglu(df, wo, gu, name, tm=1024):
    T, Dm = df.shape
    half, n, _ = wo.shape
    tm = _tile(T, tm)

    def body(df_ref, w_ref, gu_ref, o_ref):
        da = _dot(df_ref[...], w_ref[0], NT)
        g = gu_ref[0, 0].astype(F32)
        u = gu_ref[1, 0].astype(F32)
        s = _sigmoid(g)
        o_ref[0, 0] = (da * u * (s * (1.0 + g * (1.0 - s)))).astype(BF16)
        o_ref[1, 0] = (da * (g * s)).astype(BF16)

    gu_spec = pl.BlockSpec((2, 1, tm, n), lambda i, j: (0, j, i, 0))
    return pl.pallas_call(
        body, name=name, grid=(T // tm, half),
        out_shape=jax.ShapeDtypeStruct(gu.shape, BF16),
        in_specs=[pl.BlockSpec((tm, Dm), lambda i, j: (i, 0)),
                  pl.BlockSpec((1, n, Dm), lambda i, j: (j, 0, 0)), gu_spec],
        out_specs=gu_spec,
        compiler_params=pltpu.CompilerParams(dimension_semantics=("parallel", "parallel")),
    )(df, wo, gu)


def _pw1_glu(xm, w, bias, name, tm=512):
    T, K = xm.shape
    N = w.shape[1]
    tm = _tile(T, tm, 2 * SUBLANES)

    def body(a_ref, w_ref, b_ref, ag_ref, hg_ref):
        ag = (_dot(a_ref[...], w_ref[...], NN) + b_ref[...]).astype(BF16)
        ag_ref[...] = ag
        hg_ref[...] = ag[:, :N // 2].astype(F32) * _sigmoid(ag[:, N // 2:].astype(F32))

    return pl.pallas_call(
        body, name=name, grid=(T // tm,),
        out_shape=[jax.ShapeDtypeStruct((T, N), BF16), jax.ShapeDtypeStruct((T, N // 2), F32)],
        in_specs=[_row_spec(tm, K), pl.BlockSpec((K, N), lambda i: (0, 0)), _vec_spec(N)],
        out_specs=[_row_spec(tm, N), _row_spec(tm, N // 2)],
        compiler_params=pltpu.CompilerParams(dimension_semantics=("parallel",)),
    )(xm, w, bias)


def _glu_bwd_pw1_dx(ag, dhg, w, name, tm=512):
    T, N = ag.shape
    Dm = N // 2
    tm = _tile(T, tm, 2 * SUBLANES)

    def body(ag_ref, dh_ref, w_ref, dag_ref, s_ref, dx_ref):
        i = pl.program_id(0)

        @pl.when(i == 0)
        def _():
            s_ref[...] = jnp.zeros_like(s_ref)

        a = ag_ref[:, :Dm].astype(F32)
        s = _sigmoid(ag_ref[:, Dm:].astype(F32))
        dh = dh_ref[...]
        da = dh * s
        dg = dh * a * s * (1.0 - s)
        dag_ref[:, :Dm] = da.astype(BF16)
        dag_ref[:, Dm:] = dg.astype(BF16)
        s_ref[:, :Dm] += jnp.sum(da, axis=0, keepdims=True)
        s_ref[:, Dm:] += jnp.sum(dg, axis=0, keepdims=True)
        dx_ref[...] = _dot(dag_ref[...], w_ref[...], NT).astype(BF16)

    return pl.pallas_call(
        body, name=name, grid=(T // tm,),
        out_shape=[jax.ShapeDtypeStruct((T, N), BF16), jax.ShapeDtypeStruct((1, N), F32),
                   jax.ShapeDtypeStruct((T, Dm), BF16)],
        in_specs=[_row_spec(tm, N), _row_spec(tm, Dm), pl.BlockSpec(w.shape, lambda i: (0, 0))],
        out_specs=[_row_spec(tm, N), _vec_spec(N), _row_spec(tm, Dm)],
        compiler_params=pltpu.CompilerParams(dimension_semantics=("arbitrary",)),
    )(ag, dhg, w)


def _halo_specs(tm, nblk, width):
    per = tm // CONV_HALO
    prev = pl.BlockSpec((CONV_HALO, width), lambda i: (jnp.maximum(i * per - 1, 0), 0))
    nxt = pl.BlockSpec((CONV_HALO, width), lambda i: (jnp.minimum((i + 1) * per, nblk * per - 1), 0))
    return prev, nxt


def _fill_halo(scr, prev_ref, cur_ref, next_ref, i, nblk, tm):
    scr[0:CONV_HALO, :] = jnp.where(i > 0, prev_ref[...], 0.0)
    scr[CONV_HALO:CONV_HALO + tm, :] = cur_ref[...]
    scr[CONV_HALO + tm:2 * CONV_HALO + tm, :] = jnp.where(i < nblk - 1, next_ref[...], 0.0)


CONV_ROWS = 128


CONV_REACH = (CONV_WIDTH // SUBLANES) * SUBLANES


def _windows(scr, stage, cols, tm):
    for r in range(SUBLANES):
        if r:
            stage[r] = scr[pl.ds(r, tm + CONV_REACH), cols]
        for a in range(CONV_REACH // SUBLANES + 1):
            off = SUBLANES * a + r
            if 1 <= off <= CONV_WIDTH:
                yield off, (stage[r, SUBLANES * a:SUBLANES * a + tm, :] if r
                            else scr[SUBLANES * a:SUBLANES * a + tm, cols])


def _conv_fwd(hg, w_dw, b_dw, name):
    R, Dm = hg.shape
    tm = _tile(R, CONV_ROWS, CONV_HALO)
    nblk = R // tm
    prev_spec, next_spec = _halo_specs(tm, nblk, Dm)

    def body(prev_ref, cur_ref, next_ref, w_ref, bdw_ref, hd_ref, scr, stage):
        _fill_halo(scr, prev_ref, cur_ref, next_ref, pl.program_id(0), nblk, tm)
        for cb in range(Dm // LANES):
            cols = slice(cb * LANES, (cb + 1) * LANES)
            acc = jnp.zeros((tm, LANES), F32) + bdw_ref[:, cols]
            for off, win in _windows(scr, stage, cols, tm):
                acc = acc + w_ref[off - 1:off, cols] * win
            hd_ref[:, cols] = acc

    return pl.pallas_call(
        body, name=name, grid=(nblk,),
        out_shape=jax.ShapeDtypeStruct((R, Dm), F32),
        in_specs=[prev_spec, _row_spec(tm, Dm), next_spec,
                  pl.BlockSpec((CONV_WIDTH, Dm), lambda i: (0, 0)), _vec_spec(Dm)],
        out_specs=_row_spec(tm, Dm),
        scratch_shapes=[pltpu.VMEM((tm + 2 * CONV_HALO, Dm), F32),
                        pltpu.VMEM((SUBLANES, tm + CONV_REACH, LANES), F32)],
        compiler_params=pltpu.CompilerParams(dimension_semantics=("parallel",)),
    )(hg, hg, hg, w_dw, b_dw)


def _ln_silu_pw2(hd, ln_g, ln_b, w, bias, res, gate, norm, name, tm=512):
    R, Dm = hd.shape
    tm = _tile(R, tm, 2 * SUBLANES)

    def body(hd_ref, g_ref, b_ref, w_ref, bias_ref, res_ref, gate_ref, ng_ref, nsc_ref, nsh_ref,
             hs_ref, h_ref, y_ref, xn_ref):
        hd = hd_ref[...]
        xc = hd - jnp.mean(hd, axis=-1, keepdims=True)
        rs = lax.rsqrt(jnp.mean(xc * xc, axis=-1, keepdims=True) + EPS)
        hs = _silu(xc * rs * g_ref[...] + b_ref[...]).astype(BF16)
        hs_ref[...] = hs
        y = _dot(hs, w_ref[...], NN) + bias_ref[...]
        y_ref[...] = y.astype(BF16)
        h = res_ref[...] + gate_ref[...] * y
        h_ref[...] = h
        xn_ref[...] = _norm_mod(h, ng_ref[...], nsc_ref[...], nsh_ref[...]).astype(BF16)

    row, vec = _row_spec(tm, Dm), _vec_spec(Dm)
    return pl.pallas_call(
        body, name=name, grid=(R // tm,),
        out_shape=[jax.ShapeDtypeStruct((R, Dm), BF16), jax.ShapeDtypeStruct((R, Dm), F32),
                   jax.ShapeDtypeStruct((R, Dm), BF16), jax.ShapeDtypeStruct((R, Dm), BF16)],
        in_specs=[row, vec, vec, pl.BlockSpec(w.shape, lambda i: (0, 0)), vec, row, vec, vec, vec, vec],
        out_specs=[row, row, row, row],
        compiler_params=pltpu.CompilerParams(dimension_semantics=("parallel",)),
    )(hd, ln_g, ln_b, w, bias, res, gate, *norm)


def _pw2_dx_ln_silu_bwd(dy, w, hd, ln_g, ln_b, name, tm=512):
    R, Dm = hd.shape
    tm = _tile(R, tm, 2 * SUBLANES)

    def body(dy_ref, w_ref, hd_ref, g_ref, b_ref, dhd_ref, dg_ref, db_ref, dsum_ref):
        i = pl.program_id(0)

        @pl.when(i == 0)
        def _():
            dg_ref[...] = jnp.zeros_like(dg_ref)
            db_ref[...] = jnp.zeros_like(db_ref)
            dsum_ref[...] = jnp.zeros_like(dsum_ref)

        hd = hd_ref[...]
        mu = jnp.mean(hd, axis=-1, keepdims=True)
        xc = hd - mu
        rs = lax.rsqrt(jnp.mean(xc * xc, axis=-1, keepdims=True) + EPS)
        z = xc * rs
        hl = z * g_ref[...] + b_ref[...]
        dhl = _dot(dy_ref[...], w_ref[...], NT) * _dsilu(hl)
        dg_ref[...] += jnp.sum(dhl * z, axis=0, keepdims=True)
        db_ref[...] += jnp.sum(dhl, axis=0, keepdims=True)
        dz = dhl * g_ref[...]
        dhd = rs * (dz - jnp.mean(dz, axis=-1, keepdims=True) - z * jnp.mean(dz * z, axis=-1, keepdims=True))
        dsum_ref[...] += jnp.sum(dhd, axis=0, keepdims=True)
        dhd_ref[...] = dhd

    return pl.pallas_call(
        body, name=name, grid=(R // tm,),
        out_shape=[jax.ShapeDtypeStruct((R, Dm), F32)] + [jax.ShapeDtypeStruct((1, Dm), F32)] * 3,
        in_specs=[_row_spec(tm, Dm), pl.BlockSpec(w.shape, lambda i: (0, 0)), _row_spec(tm, Dm),
                  _vec_spec(Dm), _vec_spec(Dm)],
        out_specs=[_row_spec(tm, Dm), _vec_spec(Dm), _vec_spec(Dm), _vec_spec(Dm)],
        compiler_params=pltpu.CompilerParams(dimension_semantics=("arbitrary",)),
    )(dy, w, hd, ln_g, ln_b)


def _conv_bwd(dhd, hg, w_dw, name):
    R, Dm = hg.shape
    tm = _tile(R, CONV_ROWS, CONV_HALO)
    nblk = R // tm
    prev_spec, next_spec = _halo_specs(tm, nblk, Dm)

    def body(dprev, dcur, dnext, gprev, gcur, gnext, w_ref, dhg_ref, dw_ref, dscr, gscr, dwp, stage):
        i = pl.program_id(0)

        @pl.when(i == 0)
        def _():
            dwp[...] = jnp.zeros_like(dwp)

        _fill_halo(dscr, dprev, dcur, dnext, i, nblk, tm)
        _fill_halo(gscr, gprev, gcur, gnext, i, nblk, tm)
        for cb in range(Dm // LANES):
            cols = slice(cb * LANES, (cb + 1) * LANES)
            acc = jnp.zeros((tm, LANES), F32)
            for off, win in _windows(dscr, stage, cols, tm):
                j = CONV_WIDTH - off
                acc = acc + w_ref[j:j + 1, cols] * win
            dhg_ref[:, cols] = acc
            d_here = dcur[:, cols]
            for off, win in _windows(gscr, stage, cols, tm):
                j = off - 1
                prod = d_here * win
                part = prod[0:SUBLANES]
                for k in range(1, tm // SUBLANES):
                    part = part + prod[k * SUBLANES:(k + 1) * SUBLANES]
                dwp[j * SUBLANES:(j + 1) * SUBLANES, cols] += part

        @pl.when(i == nblk - 1)
        def _():
            for j in range(CONV_WIDTH):
                dw_ref[j:j + 1, :] = jnp.sum(dwp[j * SUBLANES:(j + 1) * SUBLANES, :], axis=0, keepdims=True)

    return pl.pallas_call(
        body, name=name, grid=(nblk,),
        out_shape=[jax.ShapeDtypeStruct((R, Dm), F32), jax.ShapeDtypeStruct((CONV_WIDTH, Dm), F32)],
        in_specs=[prev_spec, _row_spec(tm, Dm), next_spec, prev_spec, _row_spec(tm, Dm), next_spec,
                  pl.BlockSpec((CONV_WIDTH, Dm), lambda i: (0, 0))],
        out_specs=[_row_spec(tm, Dm), pl.BlockSpec((CONV_WIDTH, Dm), lambda i: (0, 0))],
        scratch_shapes=[pltpu.VMEM((tm + 2 * CONV_HALO, Dm), F32)] * 2
        + [pltpu.VMEM((CONV_WIDTH * SUBLANES, Dm), F32), pltpu.VMEM((SUBLANES, tm + CONV_REACH, LANES), F32)],
        compiler_params=pltpu.CompilerParams(dimension_semantics=("arbitrary",)),
    )(dhd, dhd, dhd, hg, hg, hg, w_dw)


def _swap16(y, lane):
    return jnp.where((lane & 16) == 0, pltpu.roll(y, LANES - 16, 1), pltpu.roll(y, 16, 1))


def _head_mean(v, bd):
    hi, lo = _split_bf16(v)
    return (_dot(hi, bd, NN) + _dot(lo, bd, NN)) * (1.0 / HEAD_DIM)


Q_COLS = (0, ATTN_WIDTH)
K_COLS = (ATTN_WIDTH, ATTN_WIDTH + HEAD_DIM * 2)
V_COLS = (K_COLS[1], K_COLS[1] + HEAD_DIM * 2)
SU_COLS = (V_COLS[1], V_COLS[1] + SG_WIDTH)
SV_COLS = (SU_COLS[1], SU_COLS[1] + SG_WIDTH)


def _mix_prep_fwd(p, ctx_rows, cos, sin, qg, kg, bd, w_sp, b_spt, name):
    TT = p.shape[0]
    off = ctx_rows // CHUNK
    q_scale = HEAD_DIM ** -0.5

    def body(p_ref, cos_ref, sin_ref, qg_ref, kg_ref, bd_ref, w_ref, b_ref,
             q_ref, kp_ref, vp_ref, kt_ref, sg_ref):
        lane = lax.broadcasted_iota(jnp.int32, (CHUNK, LANES), 1)
        low = lane < HEAD_DIM
        cs, sn, bdv = cos_ref[...], sin_ref[...], bd_ref[...]

        def norm_rope(xv, gain):
            r = lax.rsqrt(_head_mean(xv * xv, bdv) + EPS)
            yv = xv * r * gain
            return yv * cs + _swap16(yv, lane) * sn

        def pad_heads(ref, t):
            tr = pltpu.roll(t, HEAD_DIM, 1)
            ref[0, 0] = jnp.where(low, t, 0.0).astype(BF16)
            ref[0, 1] = jnp.where(low, 0.0, tr).astype(BF16)
            ref[1, 0] = jnp.where(low, tr, 0.0).astype(BF16)
            ref[1, 1] = jnp.where(low, 0.0, t).astype(BF16)

        for a in range(ATTN_WIDTH // LANES):
            xv = p_ref[:, a * LANES:(a + 1) * LANES]
            q_ref[:, a * LANES:(a + 1) * LANES] = (norm_rope(xv, qg_ref[...]) * q_scale).astype(BF16)
        kh = norm_rope(p_ref[:, K_COLS[0]:K_COLS[1]], kg_ref[...])
        pad_heads(kp_ref, kh)
        pad_heads(vp_ref, p_ref[:, V_COLS[0]:V_COLS[1]])
        kht = kh.T
        kt_ref[0] = kht[:HEAD_DIM].astype(BF16)
        kt_ref[1] = kht[HEAD_DIM:].astype(BF16)
        for g in range(N_SG_GROUPS):
            u = _gelu(p_ref[:, SU_COLS[0] + g * LANES:SU_COLS[0] + (g + 1) * LANES])
            vg = _gelu(p_ref[:, SV_COLS[0] + g * LANES:SV_COLS[0] + (g + 1) * LANES])
            xc = vg - jnp.mean(vg, axis=-1, keepdims=True)
            vn = xc * lax.rsqrt(jnp.mean(xc * xc, axis=-1, keepdims=True) + EPS)
            mixed = _dot(w_ref[g].astype(BF16), vn.astype(BF16), NN) + b_ref[:, g:g + 1]
            sg_ref[:, g * LANES:(g + 1) * LANES] = (u * mixed).astype(BF16)

    def row(width):
        return pl.BlockSpec((CHUNK, width), lambda i: (i, 0))

    def whole(shape):
        return pl.BlockSpec(shape, lambda i: (0,) * len(shape))

    pad_spec = pl.BlockSpec((2, 2, CHUNK, LANES), lambda i: (0, 0, i, 0))
    return pl.pallas_call(
        body, name=name, grid=(TT // CHUNK,),
        out_shape=[jax.ShapeDtypeStruct((TT, ATTN_WIDTH), BF16),
                   jax.ShapeDtypeStruct((2, 2, TT, LANES), BF16), jax.ShapeDtypeStruct((2, 2, TT, LANES), BF16),
                   jax.ShapeDtypeStruct((2, HEAD_DIM, TT), BF16),
                   jax.ShapeDtypeStruct((TT - ctx_rows, ATTN_WIDTH + SG_WIDTH), BF16)],
        in_specs=[row(IN_WIDTH), row(LANES), row(LANES), whole((1, LANES)), whole((1, LANES)),
                  whole((LANES, LANES)), whole((N_SG_GROUPS, CHUNK, CHUNK)), whole((CHUNK, N_SG_GROUPS))],
        out_specs=[row(ATTN_WIDTH), pad_spec, pad_spec,
                   pl.BlockSpec((2, HEAD_DIM, CHUNK), lambda i: (0, 0, i)),
                   pl.BlockSpec((CHUNK, SG_WIDTH), lambda i: (jnp.maximum(i - off, 0), 1))],
        compiler_params=pltpu.CompilerParams(dimension_semantics=("arbitrary",)),
    )(p, cos, sin, qg, kg, bd, w_sp, b_spt)


def _mix_prep_bwd(p, dq, f, dao, ctx_rows, cos, sin, qg, kg, bd, w_sp, w_spt, b_spt, name):
    TT = p.shape[0]
    off = ctx_rows // CHUNK
    q_scale = HEAD_DIM ** -0.5

    def body(p_ref, dq_ref, f_ref, dsg_ref, cos_ref, sin_ref, qg_ref, kg_ref, bd_ref, w_ref, wt_ref,
             b_ref, dp_ref, dqg_ref, dkg_ref, dw_ref, db_ref):
        i = pl.program_id(0)

        @pl.when(i == 0)
        def _():
            dqg_ref[...] = jnp.zeros_like(dqg_ref)
            dkg_ref[...] = jnp.zeros_like(dkg_ref)
            dw_ref[...] = jnp.zeros_like(dw_ref)
            db_ref[...] = jnp.zeros_like(db_ref)

        latent = (i >= off).astype(F32)
        lane = lax.broadcasted_iota(jnp.int32, (CHUNK, LANES), 1)
        low = lane < HEAD_DIM
        cs, sn, bdv = cos_ref[...], sin_ref[...], bd_ref[...]

        def fold(b0):
            return jnp.where(low, f_ref[0, b0] + pltpu.roll(f_ref[0, b0 + 1], HEAD_DIM, 1),
                             pltpu.roll(f_ref[1, b0], HEAD_DIM, 1) + f_ref[1, b0 + 1])

        def norm_rope_bwd(xv, dout, gain):
            r = lax.rsqrt(_head_mean(xv * xv, bdv) + EPS)
            n = xv * r
            dy = dout * cs + _swap16(dout * sn, lane)
            dn = dy * gain
            dx = r * (dn - n * _head_mean(dn * n, bdv))
            return dx, jnp.sum(dy * n, axis=0, keepdims=True)

        for a in range(ATTN_WIDTH // LANES):
            cols = slice(a * LANES, (a + 1) * LANES)
            dx, dg = norm_rope_bwd(p_ref[:, cols], dq_ref[:, cols] * (latent * q_scale), qg_ref[...])
            dp_ref[:, cols] = dx.astype(BF16)
            dqg_ref[...] += dg
        dx, dg = norm_rope_bwd(p_ref[:, K_COLS[0]:K_COLS[1]], fold(0), kg_ref[...])
        dp_ref[:, K_COLS[0]:K_COLS[1]] = dx.astype(BF16)
        dkg_ref[...] += dg
        dp_ref[:, V_COLS[0]:V_COLS[1]] = fold(2).astype(BF16)
        for g in range(N_SG_GROUPS):
            su = p_ref[:, SU_COLS[0] + g * LANES:SU_COLS[0] + (g + 1) * LANES]
            sv = p_ref[:, SV_COLS[0] + g * LANES:SV_COLS[0] + (g + 1) * LANES]
            (u, dgelu_su), (vg, dgelu_sv) = _gelu_and_grad(su), _gelu_and_grad(sv)
            xc = vg - jnp.mean(vg, axis=-1, keepdims=True)
            rs = lax.rsqrt(jnp.mean(xc * xc, axis=-1, keepdims=True) + EPS)
            vn = xc * rs
            vnb = vn.astype(BF16)
            mixed = _dot(w_ref[g].astype(BF16), vnb, NN) + b_ref[:, g:g + 1]
            dsg = dsg_ref[:, g * LANES:(g + 1) * LANES].astype(F32) * latent
            du = dsg * mixed
            dmix = dsg * u
            dmb = dmix.astype(BF16)
            db_ref[:, g:g + 1] += jnp.sum(dmix, axis=-1, keepdims=True)
            dw_ref[g] += _dot(dmb, vnb, NT)
            dvn = _dot(wt_ref[g].astype(BF16), dmb, NN)
            dvg = rs * (dvn - jnp.mean(dvn, axis=-1, keepdims=True)
                        - vn * jnp.mean(dvn * vn, axis=-1, keepdims=True))
            dp_ref[:, SU_COLS[0] + g * LANES:SU_COLS[0] + (g + 1) * LANES] = (du * dgelu_su).astype(BF16)
            dp_ref[:, SV_COLS[0] + g * LANES:SV_COLS[0] + (g + 1) * LANES] = (dvg * dgelu_sv).astype(BF16)

    def row(width):
        return pl.BlockSpec((CHUNK, width), lambda i: (i, 0))

    def latent_row(width, col_block):
        return pl.BlockSpec((CHUNK, width), lambda i: (jnp.maximum(i - off, 0), col_block))

    def whole(shape):
        return pl.BlockSpec(shape, lambda i: (0,) * len(shape))

    return pl.pallas_call(
        body, name=name, grid=(TT // CHUNK,),
        out_shape=[jax.ShapeDtypeStruct((TT, IN_WIDTH), BF16), jax.ShapeDtypeStruct((1, LANES), F32),
                   jax.ShapeDtypeStruct((1, LANES), F32),
                   jax.ShapeDtypeStruct((N_SG_GROUPS, CHUNK, CHUNK), F32),
                   jax.ShapeDtypeStruct((CHUNK, N_SG_GROUPS), F32)],
        in_specs=[row(IN_WIDTH), latent_row(ATTN_WIDTH, 0),
                  pl.BlockSpec((2, 4, CHUNK, LANES), lambda i: (0, 0, i, 0)),
                  latent_row(SG_WIDTH, 1), row(LANES), row(LANES), whole((1, LANES)), whole((1, LANES)),
                  whole((LANES, LANES)), whole((N_SG_GROUPS, CHUNK, CHUNK)),
                  whole((N_SG_GROUPS, CHUNK, CHUNK)), whole((CHUNK, N_SG_GROUPS))],
        out_specs=[row(IN_WIDTH), whole((1, LANES)), whole((1, LANES)),
                   whole((N_SG_GROUPS, CHUNK, CHUNK)), whole((CHUNK, N_SG_GROUPS))],
        compiler_params=pltpu.CompilerParams(dimension_semantics=("arbitrary",)),
    )(p, dq, f, dao, cos, sin, qg, kg, bd, w_sp, w_spt, b_spt)


def _attn_fwd(q, kpad, vpad, ao, ctx_rows, name, tq=256):
    TT = q.shape[0]
    T = TT - ctx_rows
    tq = _tile(T, tq)
    off = ctx_rows // tq
    group = 2 * LANES

    def body(q_ref, k_ref, v_ref, ao_in, o_ref, lse_ref):
        del ao_in
        lane = lax.broadcasted_iota(jnp.int32, (tq, LANES), 1)
        lse = jnp.zeros((tq, LANES), F32)
        for a in range(2):
            acc = jnp.zeros((tq, LANES), F32)
            qa = q_ref[:, a * LANES:(a + 1) * LANES]
            for b in range(2):
                s = _dot(qa, k_ref[0, b], NT)
                m = jnp.max(s, axis=-1, keepdims=True)
                e = jnp.exp(s - m)
                l = jnp.sum(e, axis=-1, keepdims=True)
                acc = acc + _dot(e.astype(BF16), v_ref[0, b], NN) * (1.0 / l)
                lse = jnp.where(lane == 2 * a + b, m + jnp.log(l), lse)
            o_ref[:, a * LANES:(a + 1) * LANES] = acc.astype(BF16)
        lse_ref[0] = lse

    kv_spec = pl.BlockSpec((1, 2, TT, LANES), lambda j, i: (j, 0, 0, 0))
    return pl.pallas_call(
        body, name=name, grid=(2, T // tq),
        out_shape=[jax.ShapeDtypeStruct(ao.shape, BF16), jax.ShapeDtypeStruct((2, T, LANES), F32)],
        in_specs=[pl.BlockSpec((tq, group), lambda j, i: (i + off, j)), kv_spec, kv_spec,
                  pl.BlockSpec(memory_space=pl.ANY)],
        out_specs=[pl.BlockSpec((tq, group), lambda j, i: (i, j)),
                   pl.BlockSpec((1, tq, LANES), lambda j, i: (j, i, 0))],
        input_output_aliases={3: 0},
        compiler_params=pltpu.CompilerParams(dimension_semantics=("parallel", "parallel")),
    )(q, kpad, vpad, ao)


def _attn_bwd(q, dao, ao, lse, kpad, vpad, kt, ctx_rows, name, tq=256):
    TT = q.shape[0]
    T = TT - ctx_rows
    tq = _tile(T, tq)
    off = ctx_rows // tq
    group = 2 * LANES

    def body(q_ref, do_ref, o_ref, lse_ref, k_ref, v_ref, kt_ref, dq_ref, f_ref):
        i = pl.program_id(1)

        @pl.when(i == 0)
        def _():
            f_ref[...] = jnp.zeros_like(f_ref)

        ktv = kt_ref[0]
        lse_t = lse_ref[0].T
        row = lax.broadcasted_iota(jnp.int32, (SUBLANES, LANES), 0)
        lane = lax.broadcasted_iota(jnp.int32, (SUBLANES, LANES), 1)
        half_ones = (jnp.where(lane < HEAD_DIM, 0, 1) == row).astype(BF16)
        for a in range(2):
            cols = slice(a * LANES, (a + 1) * LANES)
            qa = q_ref[:, cols]
            do32 = do_ref[:, cols].astype(F32)
            doa = do32.astype(BF16)
            hi, lo = _split_bf16(do32 * o_ref[:, cols].astype(F32))
            deltas = _dot(half_ones, hi, NT) + _dot(half_ones, lo, NT)
            halves = []
            for b in range(2):
                h = 2 * a + b
                st = _dot(k_ref[0, b], qa, NT)
                pt = jnp.exp(st - lse_t[h:h + 1, :])
                dpt = _dot(v_ref[0, b], doa, NT)
                dst = (pt * (dpt - deltas[b:b + 1, :])).astype(BF16)
                f_ref[0, b] += _dot(dst, qa, NN)
                f_ref[0, 2 + b] += _dot(pt.astype(BF16), doa, NN)
                halves.append(_dot(ktv, dst, NN))
            dq_ref[:, cols] = jnp.concatenate(halves, axis=0).T

    kv_spec = pl.BlockSpec((1, 2, TT, LANES), lambda j, i: (j, 0, 0, 0))
    out_cols = pl.BlockSpec((tq, group), lambda j, i: (i, j))
    return pl.pallas_call(
        body, name=name, grid=(2, T // tq),
        out_shape=[jax.ShapeDtypeStruct((T, ATTN_WIDTH), F32), jax.ShapeDtypeStruct((2, 4, TT, LANES), F32)],
        in_specs=[pl.BlockSpec((tq, group), lambda j, i: (i + off, j)), out_cols, out_cols,
                  pl.BlockSpec((1, tq, LANES), lambda j, i: (j, i, 0)),
                  kv_spec, kv_spec, pl.BlockSpec((1, HEAD_DIM, TT), lambda j, i: (j, 0, 0))],
        out_specs=[out_cols, pl.BlockSpec((1, 4, TT, LANES), lambda j, i: (j, 0, 0, 0))],
        compiler_params=pltpu.CompilerParams(dimension_semantics=("parallel", "arbitrary")),
    )(q, dao, ao, lse, kpad, vpad, kt)


def _final_fwd_bwd(h, g, target, y, gt, name):
    R, Dm = h.shape
    tm = _tile(R, ROW_BLOCK, 8)

    def body(h_ref, g_ref, t_ref, y_ref, gt_ref, dh_ref, loss_ref, dg_ref, dy_ref, dgt_ref, dsum_ref):
        i = pl.program_id(0)

        @pl.when(i == 0)
        def _():
            for ref in (loss_ref, dg_ref, dgt_ref, dsum_ref):
                ref[...] = jnp.zeros_like(ref)

        hv = h_ref[...]
        r = lax.rsqrt(jnp.mean(hv * hv, axis=-1, keepdims=True) + EPS)
        n = hv * r
        diff = n * g_ref[...] - t_ref[...]
        loss_ref[...] += jnp.sum(diff * diff)
        dout = diff * (1.0 / Dm)
        dg_ref[...] += jnp.sum(dout * n, axis=0, keepdims=True)
        dn = dout * g_ref[...]
        dh = r * (dn - n * jnp.mean(dn * n, axis=-1, keepdims=True))
        dh_ref[...] = dh
        _gate_grads(dh, y_ref, gt_ref, dy_ref, dgt_ref, dsum_ref)

    vec = jax.ShapeDtypeStruct((1, Dm), F32)
    return pl.pallas_call(
        body, name=name, grid=(R // tm,),
        out_shape=[jax.ShapeDtypeStruct((R, Dm), F32), jax.ShapeDtypeStruct((1, LANES), F32), vec,
                   jax.ShapeDtypeStruct((R, Dm), BF16), vec, vec],
        in_specs=[_row_spec(tm, Dm), _vec_spec(Dm), _row_spec(tm, Dm), _row_spec(tm, Dm), _vec_spec(Dm)],
        out_specs=[_row_spec(tm, Dm), _vec_spec(LANES), _vec_spec(Dm), _row_spec(tm, Dm), _vec_spec(Dm),
                   _vec_spec(Dm)],
        compiler_params=pltpu.CompilerParams(dimension_semantics=("arbitrary",)),
    )(h, g, target, y, gt)


MOD_ROWS = 16


def _mod_fwd(c_rows, w_mod, name):
    L, Dm, n = w_mod.shape

    def body(c_ref, w_ref, o_ref):
        o_ref[0] = _dot3(_silu(c_ref[...]), w_ref[0], NN)

    return pl.pallas_call(
        body, name=name, grid=(L,),
        out_shape=jax.ShapeDtypeStruct((L, MOD_ROWS, n), F32),
        in_specs=[pl.BlockSpec((MOD_ROWS, Dm), lambda l: (0, 0)), pl.BlockSpec((1, Dm, n), lambda l: (l, 0, 0))],
        out_specs=pl.BlockSpec((1, MOD_ROWS, n), lambda l: (l, 0, 0)),
        compiler_params=pltpu.CompilerParams(dimension_semantics=("parallel",)),
    )(c_rows, w_mod)


def _mod_bwd(c_rows_t, dmod, w_mod, name):
    L, Dm, n = w_mod.shape

    def body(ct_ref, d_ref, w_ref, gw_ref, ds_ref):
        dm = d_ref[0]
        gw_ref[0] = _dot3(_silu(ct_ref[...]), dm, NN)
        ds_ref[0] = _dot3(dm[:MOD_ROWS], w_ref[0], NT)

    return pl.pallas_call(
        body, name=name, grid=(L,),
        out_shape=[jax.ShapeDtypeStruct((L, Dm, n), F32), jax.ShapeDtypeStruct((L, MOD_ROWS, Dm), F32)],
        in_specs=[pl.BlockSpec((Dm, LANES), lambda l: (0, 0)), pl.BlockSpec((1, LANES, n), lambda l: (l, 0, 0)),
                  pl.BlockSpec((1, Dm, n), lambda l: (l, 0, 0))],
        out_specs=[pl.BlockSpec((1, Dm, n), lambda l: (l, 0, 0)),
                   pl.BlockSpec((1, MOD_ROWS, Dm), lambda l: (l, 0, 0))],
        compiler_params=pltpu.CompilerParams(dimension_semantics=("parallel",)),
    )(c_rows_t, dmod, w_mod)


def _adam_update(w, g, m, v):
    c1 = 1.0 - ADAM_B1 ** ADAM_STEP
    c2 = 1.0 - ADAM_B2 ** ADAM_STEP
    mn = ADAM_B1 * m + (1.0 - ADAM_B1) * g
    vn = ADAM_B2 * v + (1.0 - ADAM_B2) * (g * g)
    return -ADAM_LR * ((mn / c1) / (jnp.sqrt(vn / c2) + ADAM_EPS) + ADAM_WD * w), mn, vn


def _adamw(w, g, m, v, name):
    R, Cw = w.shape
    tm = _tile(R, ADAM_ROWS, 8)

    def body(w_ref, g_ref, m_ref, v_ref, d_ref, mo_ref, vo_ref):
        d_ref[...], mo_ref[...], vo_ref[...] = _adam_update(w_ref[...], g_ref[...], m_ref[...], v_ref[...])

    spec = pl.BlockSpec((tm, Cw), lambda i: (i, 0))
    return pl.pallas_call(
        body, name=name, grid=(R // tm,),
        out_shape=[jax.ShapeDtypeStruct((R, Cw), F32)] * 3,
        in_specs=[spec] * 4, out_specs=[spec] * 3,
        compiler_params=pltpu.CompilerParams(dimension_semantics=("parallel",)),
    )(w, g, m, v)


def _adamw_recv(w, m, v, recvs, name):
    L, R, n = w.shape
    tm = _tile(R, ADAM_ROWS, 8)
    nblk = R // tm
    parts = [r.reshape(N_DEV, R, n) for r in recvs]

    def body(*refs):
        w_ref, m_ref, v_ref = refs[:3]
        part_refs = refs[3:3 + L]
        g_ref, d_ref, mo_ref, vo_ref, gsum = refs[3 + L:]
        l = pl.program_id(0)
        for ll in range(L):
            @pl.when(l == ll)
            def _(ll=ll):
                acc = part_refs[ll][0].astype(F32)
                for s in range(1, N_DEV):
                    acc = acc + part_refs[ll][s].astype(F32)
                gsum[...] = acc
        g = gsum[...]
        g_ref[0] = g
        d_ref[0], mo_ref[0], vo_ref[0] = _adam_update(w_ref[0], g, m_ref[0], v_ref[0])

    def part_spec(ll):
        return pl.BlockSpec((N_DEV, tm, n), lambda l, i: (0, jnp.where(l == ll, i, jnp.where(l < ll, 0, nblk - 1)), 0))

    spec = pl.BlockSpec((1, tm, n), lambda l, i: (l, i, 0))
    return pl.pallas_call(
        body, name=name, grid=(L, nblk),
        out_shape=[jax.ShapeDtypeStruct((L, R, n), F32)] * 4,
        in_specs=[spec] * 3 + [part_spec(ll) for ll in range(L)], out_specs=[spec] * 4,
        scratch_shapes=[pltpu.VMEM((tm, n), F32)],
        compiler_params=pltpu.CompilerParams(dimension_semantics=("parallel", "parallel")),
    )(w, m, v, *parts)


def _pack(parts, row_mult=8):
    flat, offs, pos = [], [], 0
    for t in parts:
        t = t.reshape(-1).astype(F32)
        size = -(-t.shape[0] // LANES) * LANES
        flat.append(jnp.pad(t, (0, size - t.shape[0])))
        offs.append(pos)
        pos += size
    total = -(-pos // (LANES * row_mult)) * (LANES * row_mult)
    if total > pos:
        flat.append(jnp.zeros((total - pos,), F32))
    return jnp.concatenate(flat).reshape(-1, LANES), offs


def _take(buf, off, shape):
    size = math.prod(shape)
    return buf[..., off:off + size].reshape(buf.shape[:-1] + tuple(shape))


def _rope_tables(T, ctx_rows):
    pos = jnp.arange(T)
    row = (pos // GRID_W).astype(F32)
    col = (pos % GRID_W).astype(F32)
    half = HEAD_DIM // 4
    inv = ROPE_THETA ** (-jnp.arange(0, 2 * half, 2, dtype=F32) / (2 * half))
    ang_r, ang_c = row[:, None] * inv[None, :], col[:, None] * inv[None, :]
    cos = jnp.concatenate([jnp.cos(ang_r)] * 2 + [jnp.cos(ang_c)] * 2, axis=1)
    sin = jnp.concatenate([-jnp.sin(ang_r), jnp.sin(ang_r), -jnp.sin(ang_c), jnp.sin(ang_c)], axis=1)
    cos = jnp.concatenate([jnp.ones((ctx_rows, HEAD_DIM), F32), cos], axis=0)
    sin = jnp.concatenate([jnp.zeros((ctx_rows, HEAD_DIM), F32), sin], axis=0)
    return jnp.tile(cos, (1, 2)), jnp.tile(sin, (1, 2))


def kernel(x, c, ctx, c_ctx, w_mod, b_mod, g_mix, g_ffn, w_ffn_in, w_ffn_out, w_in, q_gain, k_gain, w_sp, b_sp, w_out, w_pw1, b_pw1, w_dw, b_dw, ln_g, ln_b, w_pw2, b_pw2, g_final, loss_target, m_c_ctx, m_w_mod, m_b_mod, m_g_mix, m_g_ffn, m_w_ffn_in, m_w_ffn_out, m_w_in, m_q_gain, m_k_gain, m_w_sp, m_b_sp, m_w_out, m_w_pw1, m_b_pw1, m_w_dw, m_b_dw, m_ln_g, m_ln_b, m_w_pw2, m_b_pw2, m_g_final, v_c_ctx, v_w_mod, v_b_mod, v_g_mix, v_g_ffn, v_w_ffn_in, v_w_ffn_out, v_w_in, v_q_gain, v_k_gain, v_w_sp, v_b_sp, v_w_out, v_w_pw1, v_b_pw1, v_w_dw, v_b_dw, v_ln_g, v_ln_b, v_w_pw2, v_b_pw2, v_g_final):
    weights = dict(c_ctx=c_ctx, w_mod=w_mod, b_mod=b_mod, g_mix=g_mix, g_ffn=g_ffn, w_ffn_in=w_ffn_in,
                   w_ffn_out=w_ffn_out, w_in=w_in, q_gain=q_gain, k_gain=k_gain, w_sp=w_sp, b_sp=b_sp,
                   w_out=w_out, w_pw1=w_pw1, b_pw1=b_pw1, w_dw=w_dw, b_dw=b_dw, ln_g=ln_g, ln_b=ln_b,
                   w_pw2=w_pw2, b_pw2=b_pw2, g_final=g_final)
    moments_m = dict(c_ctx=m_c_ctx, w_mod=m_w_mod, b_mod=m_b_mod, g_mix=m_g_mix, g_ffn=m_g_ffn,
                     w_ffn_in=m_w_ffn_in, w_ffn_out=m_w_ffn_out, w_in=m_w_in, q_gain=m_q_gain,
                     k_gain=m_k_gain, w_sp=m_w_sp, b_sp=m_b_sp, w_out=m_w_out, w_pw1=m_w_pw1,
                     b_pw1=m_b_pw1, w_dw=m_w_dw, b_dw=m_b_dw, ln_g=m_ln_g, ln_b=m_ln_b, w_pw2=m_w_pw2,
                     b_pw2=m_b_pw2, g_final=m_g_final)
    moments_v = dict(c_ctx=v_c_ctx, w_mod=v_w_mod, b_mod=v_b_mod, g_mix=v_g_mix, g_ffn=v_g_ffn,
                     w_ffn_in=v_w_ffn_in, w_ffn_out=v_w_ffn_out, w_in=v_w_in, q_gain=v_q_gain,
                     k_gain=v_k_gain, w_sp=v_w_sp, b_sp=v_b_sp, w_out=v_w_out, w_pw1=v_w_pw1,
                     b_pw1=v_b_pw1, w_dw=v_w_dw, b_dw=v_b_dw, ln_g=v_ln_g, ln_b=v_ln_b, w_pw2=v_w_pw2,
                     b_pw2=v_b_pw2, g_final=v_g_final)
    names = list(weights)

    T, C = x.shape[1], ctx.shape[1]
    Dm = D_MODEL
    me = 4 * lax.axis_index("x") + 2 * lax.axis_index("y") + lax.axis_index("c")
    h0 = x[0]
    ctx2 = ctx[0]
    target = loss_target[0]

    small_sharded = (("w_dw", w_dw[0]), ("b_pw1", b_pw1), ("b_dw", b_dw), ("ln_g", ln_g), ("ln_b", ln_b),
                     ("b_pw2", b_pw2))
    buf1, offs1 = _pack([c] + [t for _, t in small_sharded])
    w_in_t, m_w_in_t, v_w_in_t = (jnp.swapaxes(t, 1, 2) for t in (w_in, m_w_in, v_w_in))
    w_ffi_t, m_w_ffi_t, v_w_ffi_t = (jnp.swapaxes(t, 1, 2) for t in (w_ffn_in, m_w_ffn_in, v_w_ffn_in))
    got1, W_in_t = _all_gather([buf1, w_in_t[0].astype(BF16)], "gather_cond", False)
    got1 = got1.reshape(N_DEV, -1)
    c_all = _take(got1, offs1[0], (Dm,))
    full_small = {}
    for (nm, t), off in zip(small_sharded, offs1[1:]):
        seg = _take(got1, off, t.shape)
        full_small[nm] = jnp.moveaxis(seg, 0, -2).reshape(t.shape[:-1] + (N_DEV * t.shape[-1],))
    w_dw_f, b_pw1_f = full_small["w_dw"], full_small["b_pw1"]
    b_dw_f, ln_g_f, ln_b_f, b_pw2_f = (full_small[k] for k in ("b_dw", "ln_g", "ln_b", "b_pw2"))

    c_rows = jnp.concatenate([c_all, c_ctx[None, :], jnp.zeros((MOD_ROWS - N_DEV - 1, Dm), F32)], axis=0)
    mod_part = _mod_fwd(c_rows, w_mod, "mod_fwd")
    n_mod = w_mod.shape[2]
    got2 = _all_gather([mod_part.reshape(-1, LANES)], "gather_mod", True)[0]
    mod_all = got2.reshape(N_DEV, 2, MOD_ROWS, n_mod).transpose(1, 2, 0, 3).reshape(2, MOD_ROWS, N_DEV * n_mod)
    mod_all = mod_all + b_mod[:, None, :]
    my_mod = lax.dynamic_index_in_dim(mod_all, me, axis=1, keepdims=False)
    sh1, sc1, gt1, sh2, sc2, gt2 = ([my_mod[l:l + 1, k * Dm:(k + 1) * Dm] for l in range(2)] for k in range(6))
    csh1 = mod_all[0, N_DEV:N_DEV + 1, 0:Dm]
    csc1 = mod_all[0, N_DEV:N_DEV + 1, Dm:2 * Dm]

    behind = got2[0:1, 0:1] * 0.0
    gather_groups = [[w_out[0]], [w_ffi_t[0], w_ffn_out[0]], [w_pw1[0], w_pw2[0]], [w_ffi_t[1], w_ffn_out[1]]]
    gathers = [_push_begin([(t + behind).astype(BF16) for t in grp], True, f"gather_start{k}")
               for k, grp in enumerate(gather_groups)]
    started = sum(h[4][0:1, 0:1] for h in gathers)

    def gathered(k, after):
        return _push_end(gathers[k], after, f"gather_wait{k}")[1]

    def ffn_weights(k, after):
        wi, wo = gathered(k, after)
        return wi.reshape(N_DEV, FF_SHARD, Dm), wo.reshape(N_DEV // 2, FF_SHARD, Dm)

    def col_gathered(t, n):
        return t.reshape(N_DEV, Dm, n).transpose(1, 0, 2).reshape(Dm, N_DEV * n)

    W_ffi, W_ffo = [None, None], [None, None]

    g_mix_r = [g_mix[l:l + 1] for l in range(2)]
    g_ffn_r = [g_ffn[l:l + 1] for l in range(2)]
    g_fin = g_final[None, :]

    cos, sin = _rope_tables(T, C)
    qg = jnp.tile(q_gain, (1, 2))
    kg = jnp.tile(k_gain, (1, 2))
    lane_head = jnp.arange(LANES) // HEAD_DIM
    bd = (lane_head[:, None] == lane_head[None, :]).astype(BF16)
    w_sp0 = w_sp[0]
    w_spt0 = w_sp0.transpose(0, 2, 1)
    b_spt0 = b_sp[0].T

    XM = _norm_mod_fwd_cat(ctx2, h0, g_mix_r[0], csc1, csh1, sc1[0] + started, sh1[0], "norm_mix0")
    P = _mm(XM, W_in_t, "nt", "in_proj", tm=1088, tn=IN_WIDTH)
    qh, kpad, vpad, kt, ao = _mix_prep_fwd(P, C, cos, sin, qg, kg, bd, w_sp0, b_spt0, "mix_prep")
    ao, lse = _attn_fwd(qh, kpad, vpad, ao, C, "attn_fwd")
    W_out, = gathered(0, ao)
    h1, y0, xf0 = _mm(ao, W_out, "nn", "out_proj", res=h0, gate=gt1[0], raw_out=True,
                      norm=(g_ffn_r[0], sc2[0], sh2[0]))

    def ffn_fwd(h_in, xf, l, norm_next):
        W_ffi[l], W_ffo[l] = ffn_weights(1 + 2 * l, xf)
        gu, act = _ffn_in_swiglu(xf, W_ffi[l], f"ffn_in{l}")
        outs = _mm_sum_shards(act, W_ffo[l], "nn", f"ffn_out{l}", res=h_in, gate=gt2[l], raw_out=True,
                              norm=norm_next)
        return tuple(outs) + (None,) * (3 - len(outs)) + (gu, act)

    h2, f0, xm1, gu0, act0 = ffn_fwd(h1, xf0, 0, (g_mix_r[1], sc1[1], sh1[1]))

    W_pw1, W_pw2 = gathered(2, xm1)
    W_pw1 = col_gathered(W_pw1, 2 * Dm // N_DEV)
    ag, hg = _pw1_glu(xm1, W_pw1, b_pw1_f, "pw1")
    hd = _conv_fwd(hg, w_dw_f, b_dw_f, "conv")
    hs, h3, y1, xf1 = _ln_silu_pw2(hd, ln_g_f, ln_b_f, W_pw2, b_pw2_f, h2, gt1[1],
                                   (g_ffn_r[1], sc2[1], sh2[1]), "pw2")
    h4, f1, _, gu1, act1 = ffn_fwd(h3, xf1, 1, None)

    dh4, sq_err, dg_final, df1, dgt2_1, _ = _final_fwd_bwd(h4, g_fin, target, f1, gt2[1], "loss_head")
    loss_local = (0.5 / Dm) * sq_err[0, 0:1]

    def col_shards(g, n):
        return g.reshape(Dm, N_DEV, n).transpose(1, 0, 2).reshape(N_DEV * Dm, n)

    def exchange_begin(k, parts):
        return _push_begin(parts, False, f"exchange_start{k}")

    def zero_of(handle):
        return handle[4][0:1, 0:1]

    def ffn_bwd(df, xf, gu, act, l):
        dw_out = _mm_tn_shard_rows(act, df, f"ffn_out_dw{l}", BF16)
        dgu = _ffn_out_dx_swiglu(df, W_ffo[l], gu, f"ffn_out_dx{l}").reshape(N_DEV, T, FF_SHARD)
        dw_in = _mm_tn_shard_rows(dgu, xf, f"ffn_in_dw{l}", BF16)
        return dw_in, dw_out, dgu

    dW_ffi1, dW_ffo1, dgu1 = ffn_bwd(df1, xf1, gu1, act1, 1)
    ex0 = exchange_begin(0, [dW_ffi1.reshape(2 * D_FF, Dm), dW_ffo1.reshape(D_FF, Dm)])
    dh3, da, dsh, dy1, dgt1_1, db_pw2 = _ffn_in_dx_norm_bwd(dgu1, W_ffi[1], h3, g_ffn_r[1], sc2[1], dh4, y1,
                                                             gt1[1] + zero_of(ex0), "ffn_in_dx1")
    dmod_ffn1 = (dsh, da * g_ffn_r[1], dgt2_1)
    dg_ffn1 = da * (1.0 + sc2[1])

    dW_pw2 = _mm(hs, dy1, "tn", "pw2_dw", BF16, tk=2048)
    dhd, dln_g, dln_b, db_dw = _pw2_dx_ln_silu_bwd(dy1, W_pw2, hd, ln_g_f, ln_b_f, "pw2_dx")
    dhg, dw_dw = _conv_bwd(dhd, hg, w_dw_f, "conv_bwd")
    dag, db_pw1, dxm1 = _glu_bwd_pw1_dx(ag, dhg, W_pw1, "pw1_dx")
    dW_pw1 = _mm(xm1, dag, "tn", "pw1_dw", BF16, tk=2048)
    ex1 = exchange_begin(1, [col_shards(dW_pw1, 2 * Dm // N_DEV), dW_pw2])
    dh2, da, dsh, df0, dgt2_0, _ = _norm_mod_bwd(h2, g_mix_r[1], sc1[1], dxm1, dh3, "norm_mix1_bwd",
                                                 gate=(f0, gt2[0] + zero_of(ex1)))
    dmod_mix1 = (dsh, da * g_mix_r[1], dgt1_1)
    dg_mix1 = da * (1.0 + sc1[1])

    dW_ffi0, dW_ffo0, dgu0 = ffn_bwd(df0, xf0, gu0, act0, 0)
    ex2 = exchange_begin(2, [dW_ffi0.reshape(2 * D_FF, Dm), dW_ffo0.reshape(D_FF, Dm)])
    dh1, da, dsh, dy0, dgt1_0, _ = _ffn_in_dx_norm_bwd(dgu0, W_ffi[0], h1, g_ffn_r[0], sc2[0], dh2, y0,
                                                       gt1[0] + zero_of(ex2), "ffn_in_dx0")
    dmod_ffn0 = (dsh, da * g_ffn_r[0], dgt2_0)
    dg_ffn0 = da * (1.0 + sc2[0])

    dW_out = _mm(ao, dy0, "tn", "out_proj_dw", BF16, tk=2048)
    ex_out = exchange_begin(4, [dW_out])
    dao = _mm(dy0, W_out + zero_of(ex_out).astype(BF16), "nt", "out_proj_dx", BF16)
    dq, f_acc = _attn_bwd(qh, dao, ao, lse, kpad, vpad, kt, C, "attn_bwd")
    dP, dqg, dkg, dw_sp0, db_spt0 = _mix_prep_bwd(P, dq, f_acc, dao, C, cos, sin, qg, kg, bd, w_sp0, w_spt0,
                                                  b_spt0, "mix_prep_bwd")
    dW_in_t = _mm(dP, XM, "tn", "in_proj_dw", BF16, tm=896, tk=2176)
    dXM = _mm(dP, W_in_t, "nn", "in_proj_dx", BF16, tm=1088, tk=IN_WIDTH)
    dh0, da, dsh = _norm_mod_bwd(h0, g_mix_r[0], sc1[0], dXM, dh1, "norm_mix0_bwd", dxm_row_off=C)
    _, dac, dcsh = _norm_mod_bwd(ctx2, g_mix_r[0], csc1, dXM, None, "norm_ctx_bwd")
    dmod_mix0 = (dsh, da * g_mix_r[0], dgt1_0)
    dg_mix0 = da * (1.0 + sc1[0]) + dac * (1.0 + csc1)
    dcmod = jnp.concatenate([dcsh, dac * g_mix_r[0]], axis=1)

    dmod_mine = jnp.stack([jnp.concatenate(dmod_mix0 + dmod_ffn0, axis=1)[0],
                           jnp.concatenate(dmod_mix1 + dmod_ffn1, axis=1)[0]])

    small_grads = [
        ("loss", loss_local), ("g_final", dg_final), ("g_mix", jnp.concatenate([dg_mix0, dg_mix1])),
        ("g_ffn", jnp.concatenate([dg_ffn0, dg_ffn1])),
        ("q_gain", dqg[:, :HEAD_DIM] + dqg[:, HEAD_DIM:]), ("k_gain", dkg[:, :HEAD_DIM] + dkg[:, HEAD_DIM:]),
        ("w_sp", dw_sp0[None]), ("b_sp", db_spt0.T[None]), ("b_pw1", db_pw1), ("w_dw", dw_dw[None]),
        ("b_dw", db_dw), ("ln_g", dln_g), ("ln_b", dln_b), ("b_pw2", db_pw2), ("dcmod", dcmod),
        ("dmod", dmod_mine),
    ]
    buf3, offs3 = _pack([t for _, t in small_grads])
    off3 = {nm: off for (nm, _), off in zip(small_grads, offs3)}
    shape3 = {nm: t.shape for nm, t in small_grads}
    small_push = _push_begin([buf3], True, "small_grads_start")
    ex3 = exchange_begin(3, [dW_in_t + zero_of(small_push).astype(BF16)])

    grads, delta, new_m, new_v = {}, {}, {}, {}

    def exchanged(k, handle, after):
        return _push_end(handle, after, f"exchange_wait{k}")[1]

    def adamw_big(nm, parts, transposed=False, wmv=None):
        w3, m3, v3 = wmv if wmv is not None else (weights[nm], moments_m[nm], moments_v[nm])
        outs4 = _adamw_recv(w3, m3, v3, parts, f"adamw_{nm}")
        if transposed:
            outs4 = [jnp.swapaxes(t, 1, 2) for t in outs4]
        grads[nm], delta[nm], new_m[nm], new_v[nm] = outs4

    pushed = ex3[4]
    r_ffi1, r_ffo1 = exchanged(0, ex0, pushed)
    r_pw1, r_pw2 = exchanged(1, ex1, pushed)
    r_ffi0, r_ffo0 = exchanged(2, ex2, pushed)
    r_out, = exchanged(4, ex_out, pushed)
    adamw_big("w_ffn_in", [r_ffi0, r_ffi1], True, (w_ffi_t, m_w_ffi_t, v_w_ffi_t))
    adamw_big("w_ffn_out", [r_ffo0, r_ffo1])
    adamw_big("w_pw1", [r_pw1])
    adamw_big("w_pw2", [r_pw2])
    adamw_big("w_out", [r_out])

    got3 = _push_end(small_push, delta["w_out"], "small_grads_wait")[1][0].reshape(N_DEV, buf3.shape[0], LANES)
    sum3 = _sum_devices(got3, "sum_small_grads").reshape(-1)

    def summed(nm):
        return _take(sum3, off3[nm], shape3[nm])

    loss = summed("loss")[0]
    dcmod_sum = summed("dcmod")
    dmod_rows = _take(got3.reshape(N_DEV, -1), off3["dmod"], (2, 6 * Dm)).transpose(1, 0, 2)
    ctx_row = jnp.concatenate([jnp.pad(dcmod_sum, ((0, 0), (0, 4 * Dm))), jnp.zeros((1, 6 * Dm), F32)])
    dmod_all = jnp.concatenate([dmod_rows, ctx_row[:, None, :],
                                jnp.zeros((2, LANES - N_DEV - 1, 6 * Dm), F32)], axis=1)
    grads["b_mod"] = summed("dmod") + ctx_row
    dmod_shard = lax.dynamic_slice_in_dim(dmod_all, me * n_mod, n_mod, axis=2)
    c_rows_t = jnp.pad(c_rows.T, ((0, 0), (0, LANES - MOD_ROWS)))
    grads["w_mod"], ds_part = _mod_bwd(c_rows_t, dmod_shard, w_mod, "mod_bwd")

    buf4, _ = _pack([ds_part[0, N_DEV]])
    got4 = _all_gather([buf4], "gather_c_ctx_grad", True)[0].reshape(N_DEV, buf4.shape[0], LANES)
    ds_ctx = _sum_devices(got4, "sum_c_ctx_grad").reshape(-1)[:Dm]
    grads["c_ctx"] = ds_ctx * _dsilu(c_ctx)

    for nm in ("g_final", "g_mix", "g_ffn", "q_gain", "k_gain", "w_sp", "b_sp"):
        grads[nm] = summed(nm).reshape(weights[nm].shape)
    for nm in ("b_pw1", "w_dw", "b_dw", "ln_g", "ln_b", "b_pw2"):
        n_loc = weights[nm].shape[-1]
        grads[nm] = lax.dynamic_slice_in_dim(summed(nm), me * n_loc, n_loc, axis=-1).reshape(weights[nm].shape)

    shp = w_mod.shape
    outs = _adamw(w_mod.reshape(-1, shp[-1]), grads["w_mod"].reshape(-1, shp[-1]),
                  m_w_mod.reshape(-1, shp[-1]), v_w_mod.reshape(-1, shp[-1]), "adamw_w_mod")
    delta["w_mod"], new_m["w_mod"], new_v["w_mod"] = (o.reshape(shp) for o in outs)
    big_names = ("w_mod", "w_ffn_in", "w_ffn_out", "w_in", "w_out", "w_pw1", "w_pw2")
    small_names = [nm for nm in names if nm not in big_names]
    packs = [_pack([src[nm] for nm in small_names]) for src in (weights, grads, moments_m, moments_v)]
    offs_s = packs[0][1]
    outs = _adamw(*[pk[0] for pk in packs], "adamw_small")
    for o, dst in zip(outs, (delta, new_m, new_v)):
        o = o.reshape(-1)
        for nm, off in zip(small_names, offs_s):
            dst[nm] = _take(o, off, weights[nm].shape)
    r_in, = exchanged(3, ex3, outs[0])
    adamw_big("w_in", [r_in], True, (w_in_t, m_w_in_t, v_w_in_t))

    return (loss, dh0[None], *[grads[n] for n in names], *[delta[n] for n in names],
            *[new_m[n] for n in names], *[new_v[n] for n in names])
```

```python
import math

import jax
import jax.numpy as jnp
from jax import lax
from jax.experimental import pallas as pl
from jax.experimental.pallas import tpu as pltpu

F32 = jnp.float32
BF16 = jnp.bfloat16
MESH = pl.DeviceIdType.MESH

N_DEV = 8
D_MODEL = 1024
EPS = 1e-6
HEAD_DIM = 64
ATTN_WIDTH = 512
KV_WIDTH = 128
SG_WIDTH = 512
N_SG_GROUPS = 4
CHUNK = 128
IN_WIDTH = 1792
D_FF = 2816
FF_SHARD = 2 * D_FF // N_DEV
CONV_WIDTH = 31
CONV_HALO = 16
GRID_W = 64
ROPE_THETA = 10000.0
LANES = 128
SUBLANES = 8
ROW_BLOCK = 512
ADAM_ROWS = 256
ADAM_LR, ADAM_B1, ADAM_B2, ADAM_EPS, ADAM_WD, ADAM_STEP = 0.001, 0.9, 0.999, 1e-08, 0.01, 10


def _tile(n, target, mult=LANES):
    best = None
    for t in range(mult, min(n, target) + 1, mult):
        if n % t == 0:
            best = t
    return best if best is not None else n


def _sigmoid(x):
    return 1.0 / (1.0 + jnp.exp(-x))


def _silu(x):
    return x * _sigmoid(x)


def _dsilu(x):
    s = _sigmoid(x)
    return s * (1.0 + x * (1.0 - s))


_GELU_K = math.sqrt(2.0 / math.pi)


def _gelu(x):
    return 0.5 * x * (1.0 + jnp.tanh(_GELU_K * (x + 0.044715 * x * x * x)))


def _gelu_and_grad(x):
    x2 = x * x
    t = jnp.tanh(_GELU_K * x * (1.0 + 0.044715 * x2))
    half = 0.5 * (1.0 + t)
    return x * half, half + 0.5 * x * (1.0 - t * t) * _GELU_K * (1.0 + 3.0 * 0.044715 * x2)


def _split_bf16(x):
    hi = x.astype(BF16)
    lo = (x - hi.astype(F32)).astype(BF16)
    return hi, lo


def _dot(a, b, dims):
    return lax.dot_general(a, b, (dims, ((), ())), preferred_element_type=F32)


def _dot3(a, b, dims):
    ah, al = _split_bf16(a)
    bh, bl = _split_bf16(b)
    return _dot(ah, bh, dims) + _dot(ah, bl, dims) + _dot(al, bh, dims)


NN = ((1,), (0,))
NT = ((1,), (1,))
TN = ((0,), (0,))


def _all_gather(xs, name, in_vmem):
    n_arr = len(xs)

    def body(*refs):
        x_refs, out_refs = refs[:n_arr], refs[n_arr:2 * n_arr]
        send_sems, recv_sems, local_sems = refs[2 * n_arr:]
        x, y, c = lax.axis_index("x"), lax.axis_index("y"), lax.axis_index("c")
        me, sibling = (x, y, c), (x, y, 1 - c)
        chips = [(1 - x, y), (x, 1 - y), (1 - x, 1 - y)]

        def rows(a, px, py, pc):
            m_per = xs[a].shape[0]
            return out_refs[a].at[pl.ds((4 * px + 2 * py + pc) * m_per, m_per), :]

        def copy(a, k, block, to, src=None):
            return pltpu.make_async_remote_copy(
                src_ref=rows(a, *block) if src is None else src,
                dst_ref=rows(a, *block),
                send_sem=send_sems.at[7 * a + k],
                recv_sem=recv_sems.at[7 * a + k],
                device_id=to,
                device_id_type=MESH,
            )

        mine, first, passed = [], [], []
        for a in range(n_arr):
            mine.append(pltpu.make_async_copy(x_refs[a], rows(a, *me), local_sems.at[a]))
            mine[-1].start()
            first.append(copy(a, 0, me, sibling, src=x_refs[a]))
            first += [copy(a, 1 + j, me, (*chip, c), src=x_refs[a]) for j, chip in enumerate(chips)]
        for cp in first:
            cp.start()
        for a in range(n_arr):
            for j, chip in enumerate(chips):
                copy(a, 1 + j, (*chip, c), me).wait_recv()
                passed.append(copy(a, 4 + j, (*chip, c), sibling))
                passed[-1].start()
        for a in range(n_arr):
            copy(a, 0, sibling, me).wait_recv()
            for j, chip in enumerate(chips):
                copy(a, 4 + j, (*chip, 1 - c), me).wait_recv()
        for cp in first + passed:
            cp.wait_send()
        for cp in mine:
            cp.wait()

    space = pltpu.VMEM if in_vmem else pl.ANY
    return pl.pallas_call(
        body,
        name=name,
        out_shape=[jax.ShapeDtypeStruct((N_DEV * t.shape[0], t.shape[1]), t.dtype) for t in xs],
        in_specs=[pl.BlockSpec(memory_space=space)] * n_arr,
        out_specs=[pl.BlockSpec(memory_space=space)] * n_arr,
        scratch_shapes=[
            pltpu.SemaphoreType.DMA((7 * n_arr,)),
            pltpu.SemaphoreType.DMA((7 * n_arr,)),
            pltpu.SemaphoreType.DMA((n_arr,)),
        ],
    )(*xs)


HBM_SPEC = pl.BlockSpec(memory_space=pltpu.HBM)
SEM_SPEC = pl.BlockSpec(memory_space=pltpu.SEMAPHORE)
DATAFLOW_EFFECT = pltpu.SideEffectType.DATAFLOW_SIDE_EFFECTING


def _peers(x, y, c):
    for k in range(1, N_DEV):
        px = 1 - x if (k >> 2) & 1 else x
        py = 1 - y if (k >> 1) & 1 else y
        pc = 1 - c if k & 1 else c
        yield k - 1, (px, py, pc), 4 * px + 2 * py + pc


def _push_copies(src_refs, land_refs, send_sems, recv_sems, shapes, whole_src):
    x, y, c = lax.axis_index("x"), lax.axis_index("y"), lax.axis_index("c")
    me = 4 * x + 2 * y + c
    for a, (m_per, _) in enumerate(shapes):
        def block(ref, idx, m_per=m_per):
            return ref.at[pl.ds(idx * m_per, m_per), :]

        for k, peer, pidx in _peers(x, y, c):
            src = src_refs[a] if whole_src else block(src_refs[a], pidx)
            sems = dict(send_sem=send_sems.at[N_DEV * a + k], recv_sem=recv_sems.at[N_DEV * a + k],
                        device_id=peer, device_id_type=MESH)
            yield (pltpu.make_async_remote_copy(src_ref=src, dst_ref=block(land_refs[a], me), **sems),
                   pltpu.make_async_remote_copy(src_ref=src, dst_ref=block(land_refs[a], pidx), **sems))


def _own_copies(src_refs, land_refs, recv_sems, shapes, whole_src):
    me = 4 * lax.axis_index("x") + 2 * lax.axis_index("y") + lax.axis_index("c")
    for a, (m_per, _) in enumerate(shapes):
        mine = pl.ds(me * m_per, m_per)
        src = src_refs[a] if whole_src else src_refs[a].at[mine, :]
        yield pltpu.make_async_copy(src, land_refs[a].at[mine, :], recv_sems.at[N_DEV * a + N_DEV - 1])


def _push_begin(srcs, whole_src, name):
    n_arr = len(srcs)
    shapes = [(t.shape[0] if whole_src else t.shape[0] // N_DEV, t.shape[1]) for t in srcs]
    lands = [lax.empty((N_DEV * m, n), t.dtype) for (m, n), t in zip(shapes, srcs)]

    def body(*refs):
        src_refs, land_refs = refs[:n_arr], refs[n_arr:2 * n_arr]
        send_sems, recv_sems = refs[2 * n_arr], refs[2 * n_arr + 1]
        token = refs[-1]
        for outgoing, _ in _push_copies(src_refs, land_refs, send_sems, recv_sems, shapes, whole_src):
            outgoing.start()
        for own in _own_copies(src_refs, land_refs, recv_sems, shapes, whole_src):
            own.start()
        token[...] = jnp.zeros_like(token)

    operands = [pltpu.with_memory_space_constraint(t, pltpu.HBM) for t in list(srcs) + lands]
    outs = pl.pallas_call(
        body, name=name,
        out_shape=(pltpu.SemaphoreType.DMA((N_DEV * n_arr,)), pltpu.SemaphoreType.DMA((N_DEV * n_arr,)),
                   *[pltpu.HBM(t.shape, t.dtype) for t in operands],
                   jax.ShapeDtypeStruct((SUBLANES, LANES), F32)),
        in_specs=[HBM_SPEC] * (2 * n_arr),
        out_specs=(SEM_SPEC, SEM_SPEC, *[HBM_SPEC] * (2 * n_arr), pl.BlockSpec(memory_space=pltpu.VMEM)),
        input_output_aliases={i: 2 + i for i in range(2 * n_arr)},
        compiler_params=pltpu.CompilerParams(has_side_effects=DATAFLOW_EFFECT),
    )(*operands)
    return outs[0], outs[1], list(outs[2:2 + n_arr]), list(outs[2 + n_arr:2 + 2 * n_arr]), outs[-1], whole_src


def _push_end(handle, after, name):
    send_sems, recv_sems, srcs, lands, _, whole_src = handle
    n_arr = len(srcs)
    shapes = [(t.shape[0] // N_DEV, t.shape[1]) for t in lands]

    def body(*refs):
        src_refs, land_refs = refs[:n_arr], refs[n_arr:2 * n_arr]
        send_sems_ref, recv_sems_ref = refs[2 * n_arr], refs[2 * n_arr + 1]
        for outgoing, incoming in _push_copies(src_refs, land_refs, send_sems_ref, recv_sems_ref, shapes, whole_src):
            outgoing.wait_send()
            incoming.wait_recv()
        for own in _own_copies(src_refs, land_refs, recv_sems_ref, shapes, whole_src):
            own.wait()

    outs = pl.pallas_call(
        body, name=name,
        out_shape=tuple(pltpu.HBM(t.shape, t.dtype) for t in srcs + lands),
        in_specs=[HBM_SPEC] * (2 * n_arr) + [SEM_SPEC, SEM_SPEC, pl.BlockSpec(memory_space=pl.ANY)],
        out_specs=tuple([HBM_SPEC] * (2 * n_arr)),
        input_output_aliases={i: i for i in range(2 * n_arr)},
        compiler_params=pltpu.CompilerParams(has_side_effects=DATAFLOW_EFFECT),
    )(*srcs, *lands, send_sems, recv_sems, after)
    return list(outs[:n_arr]), list(outs[n_arr:])


def _sum_devices(r, name, rows_per_step=ADAM_ROWS):
    _, m, n = r.shape
    tm = _tile(m, rows_per_step, 8)

    def body(r_ref, o_ref):
        acc = r_ref[0].astype(F32)
        for s in range(1, N_DEV):
            acc = acc + r_ref[s].astype(F32)
        o_ref[...] = acc

    return pl.pallas_call(
        body,
        name=name,
        grid=(m // tm,),
        out_shape=jax.ShapeDtypeStruct((m, n), F32),
        in_specs=[pl.BlockSpec((N_DEV, tm, n), lambda i: (0, i, 0))],
        out_specs=pl.BlockSpec((tm, n), lambda i: (i, 0)),
        compiler_params=pltpu.CompilerParams(dimension_semantics=("parallel",)),
    )(r)


def _get(ref):
    return ref[0] if len(ref.shape) == 3 else ref[...]


def _put(ref, val):
    if len(ref.shape) == 3:
        ref[0] = val
    else:
        ref[...] = val


def _norm_mod(hv, g, sc, sh):
    r = lax.rsqrt(jnp.mean(hv * hv, axis=-1, keepdims=True) + EPS)
    return (hv * r) * g * (1.0 + sc) + sh


def _mm_call(name, a, b, a_spec, b_spec, out_sds, o_spec, grid, dims, acc_shape, bias=None,
             res=None, gate=None, raw_out=False, vec_spec=None, norm=None):
    nk = grid[2]
    operands, in_specs = [a, b], [a_spec, b_spec]
    if bias is not None:
        operands.append(bias)
        in_specs.append(vec_spec)
    if res is not None:
        operands += [res, gate]
        in_specs += [o_spec, vec_spec]
    if norm is not None:
        assert grid[1] == 1
        operands += list(norm)
        in_specs += [vec_spec] * 3
    out_shape, out_specs = [out_sds], [o_spec]
    if raw_out:
        out_shape.append(jax.ShapeDtypeStruct(out_sds.shape, BF16))
        out_specs.append(o_spec)
    if norm is not None:
        out_shape.append(jax.ShapeDtypeStruct(out_sds.shape, BF16))
        out_specs.append(o_spec)

    def body(*refs):
        it = iter(refs)
        a_ref, b_ref = next(it), next(it)
        bias_ref = next(it) if bias is not None else None
        res_ref, gate_ref = (next(it), next(it)) if res is not None else (None, None)
        norm_refs = (next(it), next(it), next(it)) if norm is not None else None
        o_ref = next(it)
        raw_ref = next(it) if raw_out else None
        xn_ref = next(it) if norm is not None else None
        acc = next(it) if nk > 1 else None
        k = pl.program_id(2)
        part = _dot(_get(a_ref).astype(BF16), _get(b_ref).astype(BF16), dims)

        def finish(y):
            if bias_ref is not None:
                y = y + bias_ref[...]
            if raw_ref is not None:
                raw_ref[...] = y.astype(BF16)
            if res_ref is not None:
                y = res_ref[...] + gate_ref[...] * y
            _put(o_ref, y.astype(out_sds.dtype))
            if xn_ref is not None:
                xn_ref[...] = _norm_mod(y, *[r[...] for r in norm_refs]).astype(BF16)

        if nk == 1:
            finish(part)
        else:
            @pl.when(k == 0)
            def _():
                acc[...] = part

            @pl.when(k > 0)
            def _():
                acc[...] += part

            @pl.when(k == nk - 1)
            def _():
                finish(acc[...])

    outs = pl.pallas_call(
        body,
        name=name,
        grid=grid,
        out_shape=out_shape,
        in_specs=in_specs,
        out_specs=out_specs,
        scratch_shapes=[pltpu.VMEM(acc_shape, F32)] if nk > 1 else [],
        compiler_params=pltpu.CompilerParams(dimension_semantics=("parallel", "parallel", "arbitrary")),
    )(*operands)
    return outs if len(outs) > 1 else outs[0]


def _mm(a, b, mode, name, out_dtype=F32, bias=None, res=None, gate=None, raw_out=False,
        tm=1024, tn=1024, tk=1024, a_row_off=0, norm=None):
    if mode == "nn":
        K, N = b.shape
        M = a.shape[0] - a_row_off
    elif mode == "nt":
        N, K = b.shape
        M = a.shape[0] - a_row_off
    else:
        (K, M), N = a.shape, b.shape[1]
    tm, tn, tk = _tile(M, tm, LANES if mode == "tn" else 2 * SUBLANES), _tile(N, tn), _tile(K, tk)
    off = a_row_off // tm
    dims = {"nn": NN, "nt": NT, "tn": TN}[mode]
    a_spec = (pl.BlockSpec((tk, tm), lambda i, j, k: (k, i)) if mode == "tn"
              else pl.BlockSpec((tm, tk), lambda i, j, k: (i + off, k)))
    b_spec = (pl.BlockSpec((tn, tk), lambda i, j, k: (j, k)) if mode == "nt"
              else pl.BlockSpec((tk, tn), lambda i, j, k: (k, j)))
    return _mm_call(name, a, b, a_spec, b_spec, jax.ShapeDtypeStruct((M, N), out_dtype),
                    pl.BlockSpec((tm, tn), lambda i, j, k: (i, j)), (M // tm, N // tn, K // tk), dims,
                    (tm, tn), bias, res, gate, raw_out, pl.BlockSpec((1, tn), lambda i, j, k: (0, j)), norm)


def _mm_sum_shards(a3, b3, mode, name, out_dtype=F32, res=None, gate=None, raw_out=False, tm=512, norm=None):
    S, M, kk = a3.shape
    N = b3.shape[2] if mode == "nn" else b3.shape[1]
    tm = _tile(M, tm)
    dims = NN if mode == "nn" else NT
    has_res = res is not None

    def body(*refs):
        it = iter(refs)
        a_ref, b_ref = next(it), next(it)
        res_ref, gate_ref = (next(it), next(it)) if has_res else (None, None)
        norm_refs = (next(it), next(it), next(it)) if norm is not None else None
        o_ref = next(it)
        raw_ref = next(it) if raw_out else None
        xn_ref = next(it) if norm is not None else None
        y = _dot(a_ref[0], b_ref[0], dims)
        for s in range(1, S):
            y = y + _dot(a_ref[s], b_ref[s], dims)
        if raw_ref is not None:
            raw_ref[...] = y.astype(BF16)
        if has_res:
            y = res_ref[...] + gate_ref[...] * y
        o_ref[...] = y.astype(out_dtype)
        if xn_ref is not None:
            xn_ref[...] = _norm_mod(y, *[r[...] for r in norm_refs]).astype(BF16)

    tile = pl.BlockSpec((tm, N), lambda i: (i, 0))
    operands = [a3, b3] + ([res, gate] if has_res else []) + (list(norm) if norm is not None else [])
    in_specs = [pl.BlockSpec((S, tm, kk), lambda i: (0, i, 0)), pl.BlockSpec(b3.shape, lambda i: (0, 0, 0))]
    in_specs += [tile, _vec_spec(N)] if has_res else []
    in_specs += [_vec_spec(N)] * 3 if norm is not None else []
    out_shape = [jax.ShapeDtypeStruct((M, N), out_dtype)] + ([jax.ShapeDtypeStruct((M, N), BF16)] if raw_out else [])
    out_shape += [jax.ShapeDtypeStruct((M, N), BF16)] if norm is not None else []
    outs = pl.pallas_call(
        body, name=name, grid=(M // tm,),
        out_shape=out_shape, in_specs=in_specs, out_specs=[tile] * len(out_shape),
        compiler_params=pltpu.CompilerParams(dimension_semantics=("parallel",)),
    )(*operands)
    return outs if len(outs) > 1 else outs[0]


def _mm_tn_shard_rows(a3, b, name, out_dtype, tn=1024, tk=4096):
    S, T, m = a3.shape
    N = b.shape[1]
    tn, tk = _tile(N, tn), _tile(T, tk)
    return _mm_call(name, a3, b, pl.BlockSpec((1, tk, m), lambda i, j, k: (i, k, 0)),
                    pl.BlockSpec((tk, tn), lambda i, j, k: (k, j)), jax.ShapeDtypeStruct((S, m, N), out_dtype),
                    pl.BlockSpec((1, m, tn), lambda i, j, k: (i, 0, j)), (S, N // tn, T // tk), TN, (m, tn))


def _row_spec(tm, width, off=0):
    return pl.BlockSpec((tm, width), lambda i: (i + off, 0))


def _vec_spec(width):
    return pl.BlockSpec((1, width), lambda i: (0, 0))


def _norm_mod_fwd_cat(hc, h, g, csc, csh, sc, sh, name):
    (C, Dm), T = hc.shape, h.shape[0]
    tm = _tile(math.gcd(C, T), ROW_BLOCK, 8)
    off = C // tm

    def body(hc_ref, h_ref, g_ref, csc_ref, csh_ref, sc_ref, sh_ref, o_ref):
        is_ctx = pl.program_id(0) < off
        hv = jnp.where(is_ctx, hc_ref[...], h_ref[...])
        scv = jnp.where(is_ctx, csc_ref[...], sc_ref[...])
        shv = jnp.where(is_ctx, csh_ref[...], sh_ref[...])
        r = lax.rsqrt(jnp.mean(hv * hv, axis=-1, keepdims=True) + EPS)
        o_ref[...] = ((hv * r) * g_ref[...] * (1.0 + scv) + shv).astype(BF16)

    return pl.pallas_call(
        body, name=name, grid=((C + T) // tm,),
        out_shape=jax.ShapeDtypeStruct((C + T, Dm), BF16),
        in_specs=[pl.BlockSpec((tm, Dm), lambda i: (jnp.minimum(i, off - 1), 0)),
                  pl.BlockSpec((tm, Dm), lambda i: (jnp.maximum(i - off, 0), 0))] + [_vec_spec(Dm)] * 5,
        out_specs=_row_spec(tm, Dm),
        compiler_params=pltpu.CompilerParams(dimension_semantics=("parallel",)),
    )(hc, h, g, csc, csh, sc, sh)


def _gate_grads(dh, y_ref, gt_ref, dy_ref, dgt_ref, dsum_ref):
    dy = dh * gt_ref[...]
    dgt_ref[...] += jnp.sum(dh * y_ref[...].astype(F32), axis=0, keepdims=True)
    dsum_ref[...] += jnp.sum(dy, axis=0, keepdims=True)
    dy_ref[...] = dy.astype(BF16)


def _norm_mod_bwd(h, g, sc, dxm, dres, name, dxm_row_off=0, gate=None):
    R, Dm = h.shape
    tm = _tile(math.gcd(R, dxm_row_off) if dxm_row_off else R, ROW_BLOCK, 8)
    off = dxm_row_off // tm
    has_res = dres is not None
    has_gate = gate is not None

    def body(*refs):
        it = iter(refs)
        h_ref, g_ref, sc_ref, dx_ref = next(it), next(it), next(it), next(it)
        dres_ref = next(it) if has_res else None
        y_ref, gt_ref = (next(it), next(it)) if has_gate else (None, None)
        dh_ref, da_ref, dsh_ref = next(it), next(it), next(it)
        gate_out = (next(it), next(it), next(it)) if has_gate else ()
        i = pl.program_id(0)

        @pl.when(i == 0)
        def _():
            for ref in (da_ref, dsh_ref) + gate_out[1:]:
                ref[...] = jnp.zeros_like(ref)

        hv = h_ref[...]
        dx = dx_ref[...].astype(F32)
        r = lax.rsqrt(jnp.mean(hv * hv, axis=-1, keepdims=True) + EPS)
        n = hv * r
        da_ref[...] += jnp.sum(dx * n, axis=0, keepdims=True)
        dsh_ref[...] += jnp.sum(dx, axis=0, keepdims=True)
        dn = dx * (g_ref[...] * (1.0 + sc_ref[...]))
        dh = r * (dn - n * jnp.mean(dn * n, axis=-1, keepdims=True))
        if has_res:
            dh = dh + dres_ref[...]
        dh_ref[...] = dh
        if has_gate:
            _gate_grads(dh, y_ref, gt_ref, *gate_out)

    operands = [h, g, sc, dxm] + ([dres] if has_res else []) + (list(gate) if has_gate else [])
    in_specs = [_row_spec(tm, Dm), _vec_spec(Dm), _vec_spec(Dm), _row_spec(tm, Dm, off)]
    in_specs += [_row_spec(tm, Dm)] if has_res else []
    in_specs += [_row_spec(tm, Dm), _vec_spec(Dm)] if has_gate else []
    vec = jax.ShapeDtypeStruct((1, Dm), F32)
    out_shape = [jax.ShapeDtypeStruct((R, Dm), F32), vec, vec]
    out_specs = [_row_spec(tm, Dm), _vec_spec(Dm), _vec_spec(Dm)]
    if has_gate:
        out_shape += [jax.ShapeDtypeStruct((R, Dm), BF16), vec, vec]
        out_specs += [_row_spec(tm, Dm), _vec_spec(Dm), _vec_spec(Dm)]
    return pl.pallas_call(
        body, name=name, grid=(R // tm,),
        out_shape=out_shape, in_specs=in_specs, out_specs=out_specs,
        compiler_params=pltpu.CompilerParams(dimension_semantics=("arbitrary",)),
    )(*operands)


def _ffn_in_dx_norm_bwd(dgu, w3, h, g, sc, dres, y, gt, name, tm=256):
    S, T, n = dgu.shape
    Dm = w3.shape[2]
    tm = _tile(T, tm)

    def body(a_ref, b_ref, h_ref, g_ref, sc_ref, dres_ref, y_ref, gt_ref,
             dh_ref, da_ref, dsh_ref, dy_ref, dgt_ref, dsum_ref):
        @pl.when(pl.program_id(0) == 0)
        def _():
            for ref in (da_ref, dsh_ref, dgt_ref, dsum_ref):
                ref[...] = jnp.zeros_like(ref)

        dx = _dot(a_ref[0], b_ref[0], NN)
        for s in range(1, S):
            dx = dx + _dot(a_ref[s], b_ref[s], NN)
        dx = dx.astype(BF16).astype(F32)
        hv = h_ref[...]
        r = lax.rsqrt(jnp.mean(hv * hv, axis=-1, keepdims=True) + EPS)
        nrm = hv * r
        da_ref[...] += jnp.sum(dx * nrm, axis=0, keepdims=True)
        dsh_ref[...] += jnp.sum(dx, axis=0, keepdims=True)
        dn = dx * (g_ref[...] * (1.0 + sc_ref[...]))
        dh = r * (dn - nrm * jnp.mean(dn * nrm, axis=-1, keepdims=True)) + dres_ref[...]
        dh_ref[...] = dh
        _gate_grads(dh, y_ref, gt_ref, dy_ref, dgt_ref, dsum_ref)

    vec = jax.ShapeDtypeStruct((1, Dm), F32)
    return pl.pallas_call(
        body, name=name, grid=(T // tm,),
        out_shape=[jax.ShapeDtypeStruct((T, Dm), F32), vec, vec, jax.ShapeDtypeStruct((T, Dm), BF16), vec, vec],
        in_specs=[pl.BlockSpec((S, tm, n), lambda i: (0, i, 0)), pl.BlockSpec(w3.shape, lambda i: (0, 0, 0)),
                  _row_spec(tm, Dm), _vec_spec(Dm), _vec_spec(Dm), _row_spec(tm, Dm), _row_spec(tm, Dm),
                  _vec_spec(Dm)],
        out_specs=[_row_spec(tm, Dm), _vec_spec(Dm), _vec_spec(Dm), _row_spec(tm, Dm), _vec_spec(Dm),
                   _vec_spec(Dm)],
        compiler_params=pltpu.CompilerParams(dimension_semantics=("arbitrary",)),
    )(dgu, w3, h, g, sc, dres, y, gt)


def _ffn_in_swiglu(xf, w3, name, tm=1024):
    T, K = xf.shape
    S, n, _ = w3.shape
    half = S // 2
    tm = _tile(T, tm)

    def body(a_ref, wg_ref, wu_ref, gu_ref, act_ref):
        a = a_ref[...]
        g = _dot(a, wg_ref[0], NT)
        u = _dot(a, wu_ref[0], NT)
        gu_ref[0, 0] = g.astype(BF16)
        gu_ref[1, 0] = u.astype(BF16)
        act_ref[0] = (_silu(g) * u).astype(BF16)

    return pl.pallas_call(
        body, name=name, grid=(T // tm, half),
        out_shape=[jax.ShapeDtypeStruct((2, half, T, n), BF16), jax.ShapeDtypeStruct((half, T, n), BF16)],
        in_specs=[pl.BlockSpec((tm, K), lambda i, j: (i, 0)),
                  pl.BlockSpec((1, n, K), lambda i, j: (j, 0, 0)),
                  pl.BlockSpec((1, n, K), lambda i, j: (j + half, 0, 0))],
        out_specs=[pl.BlockSpec((2, 1, tm, n), lambda i, j: (0, j, i, 0)),
                   pl.BlockSpec((1, tm, n), lambda i, j: (j, i, 0))],
        compiler_params=pltpu.CompilerParams(dimension_semantics=("parallel", "parallel")),
    )(xf, w3, w3)


def _ffn_out_dx_swiglu(df, wo, gu, name, tm=1024):
    T, Dm = df.shape
    half, n, _ = wo.shape
    tm = _tile(T, tm)

    def body(df_ref, w_ref, gu_ref, o_ref):
        da = _dot(df_ref[...], w_ref[0], NT)
        g = gu_ref[0, 0].astype(F32)
        u = gu_ref[1, 0].astype(F32)
        s = _sigmoid(g)
        o_ref[0, 0] = (da * u * (s * (1.0 + g * (1.0 - s)))).astype(BF16)
        o_ref[1, 0] = (da * (g * s)).astype(BF16)

    gu_spec = pl.BlockSpec((2, 1, tm, n), lambda i, j: (0, j, i, 0))
    return pl.pallas_call(
        body, name=name, grid=(T // tm, half),
        out_shape=jax.ShapeDtypeStruct(gu.shape, BF16),
        in_specs=[pl.BlockSpec((tm, Dm), lambda i, j: (i, 0)),
                  pl.BlockSpec((1, n, Dm), lambda i, j: (j, 0, 0)), gu_spec],
        out_specs=gu_spec,
        compiler_params=pltpu.CompilerParams(dimension_semantics=("parallel", "parallel")),
    )(df, wo, gu)


def _pw1_glu(xm, w, bias, name, tm=512):
    T, K = xm.shape
    N = w.shape[1]
    tm = _tile(T, tm, 2 * SUBLANES)

    def body(a_ref, w_ref, b_ref, ag_ref, hg_ref):
        ag = (_dot(a_ref[...], w_ref[...], NN) + b_ref[...]).astype(BF16)
        ag_ref[...] = ag
        hg_ref[...] = ag[:, :N // 2].astype(F32) * _sigmoid(ag[:, N // 2:].astype(F32))

    return pl.pallas_call(
        body, name=name, grid=(T // tm,),
        out_shape=[jax.ShapeDtypeStruct((T, N), BF16), jax.ShapeDtypeStruct((T, N // 2), F32)],
        in_specs=[_row_spec(tm, K), pl.BlockSpec((K, N), lambda i: (0, 0)), _vec_spec(N)],
        out_specs=[_row_spec(tm, N), _row_spec(tm, N // 2)],
        compiler_params=pltpu.CompilerParams(dimension_semantics=("parallel",)),
    )(xm, w, bias)


def _glu_bwd_pw1_dx(ag, dhg, w, name, tm=512):
    T, N = ag.shape
    Dm = N // 2
    tm = _tile(T, tm, 2 * SUBLANES)

    def body(ag_ref, dh_ref, w_ref, dag_ref, s_ref, dx_ref):
        i = pl.program_id(0)

        @pl.when(i == 0)
        def _():
            s_ref[...] = jnp.zeros_like(s_ref)

        a = ag_ref[:, :Dm].astype(F32)
        s = _sigmoid(ag_ref[:, Dm:].astype(F32))
        dh = dh_ref[...]
        da = dh * s
        dg = dh * a * s * (1.0 - s)
        dag_ref[:, :Dm] = da.astype(BF16)
        dag_ref[:, Dm:] = dg.astype(BF16)
        s_ref[:, :Dm] += jnp.sum(da, axis=0, keepdims=True)
        s_ref[:, Dm:] += jnp.sum(dg, axis=0, keepdims=True)
        dx_ref[...] = _dot(dag_ref[...], w_ref[...], NT).astype(BF16)

    return pl.pallas_call(
        body, name=name, grid=(T // tm,),
        out_shape=[jax.ShapeDtypeStruct((T, N), BF16), jax.ShapeDtypeStruct((1, N), F32),
                   jax.ShapeDtypeStruct((T, Dm), BF16)],
        in_specs=[_row_spec(tm, N), _row_spec(tm, Dm), pl.BlockSpec(w.shape, lambda i: (0, 0))],
        out_specs=[_row_spec(tm, N), _vec_spec(N), _row_spec(tm, Dm)],
        compiler_params=pltpu.CompilerParams(dimension_semantics=("arbitrary",)),
    )(ag, dhg, w)


def _glu_bwd_pw1_dx_norm_bwd(ag, dhg, w, h, g, sc, dres, y, gt, name, tm=256):
    T, N = ag.shape
    Dm = N // 2
    tm = _tile(T, tm, 2 * SUBLANES)

    def body(ag_ref, dhg_ref, w_ref, h_ref, g_ref, sc_ref, dres_ref, y_ref, gt_ref,
             dag_ref, s_ref, dh_ref, da_ref, dsh_ref, dy_ref, dgt_ref, dsum_ref):
        @pl.when(pl.program_id(0) == 0)
        def _():
            for ref in (s_ref, da_ref, dsh_ref, dgt_ref, dsum_ref):
                ref[...] = jnp.zeros_like(ref)

        a = ag_ref[:, :Dm].astype(F32)
        s = _sigmoid(ag_ref[:, Dm:].astype(F32))
        dhg_v = dhg_ref[...]
        d_a = dhg_v * s
        d_g = dhg_v * a * s * (1.0 - s)
        dag_ref[:, :Dm] = d_a.astype(BF16)
        dag_ref[:, Dm:] = d_g.astype(BF16)
        s_ref[:, :Dm] += jnp.sum(d_a, axis=0, keepdims=True)
        s_ref[:, Dm:] += jnp.sum(d_g, axis=0, keepdims=True)
        dx = _dot(dag_ref[...], w_ref[...], NT).astype(BF16).astype(F32)
        hv = h_ref[...]
        r = lax.rsqrt(jnp.mean(hv * hv, axis=-1, keepdims=True) + EPS)
        nrm = hv * r
        da_ref[...] += jnp.sum(dx * nrm, axis=0, keepdims=True)
        dsh_ref[...] += jnp.sum(dx, axis=0, keepdims=True)
        dn = dx * (g_ref[...] * (1.0 + sc_ref[...]))
        dh = r * (dn - nrm * jnp.mean(dn * nrm, axis=-1, keepdims=True)) + dres_ref[...]
        dh_ref[...] = dh
        _gate_grads(dh, y_ref, gt_ref, dy_ref, dgt_ref, dsum_ref)

    vec = jax.ShapeDtypeStruct((1, Dm), F32)
    return pl.pallas_call(
        body, name=name, grid=(T // tm,),
        out_shape=[jax.ShapeDtypeStruct((T, N), BF16), jax.ShapeDtypeStruct((1, N), F32),
                   jax.ShapeDtypeStruct((T, Dm), F32), vec, vec, jax.ShapeDtypeStruct((T, Dm), BF16), vec, vec],
        in_specs=[_row_spec(tm, N), _row_spec(tm, Dm), pl.BlockSpec(w.shape, lambda i: (0, 0)),
                  _row_spec(tm, Dm), _vec_spec(Dm), _vec_spec(Dm), _row_spec(tm, Dm), _row_spec(tm, Dm),
                  _vec_spec(Dm)],
        out_specs=[_row_spec(tm, N), _vec_spec(N), _row_spec(tm, Dm), _vec_spec(Dm), _vec_spec(Dm),
                   _row_spec(tm, Dm), _vec_spec(Dm), _vec_spec(Dm)],
        compiler_params=pltpu.CompilerParams(dimension_semantics=("arbitrary",)),
    )(ag, dhg, w, h, g, sc, dres, y, gt)


def _halo_specs(tm, nblk, width):
    per = tm // CONV_HALO
    prev = pl.BlockSpec((CONV_HALO, width), lambda i: (jnp.maximum(i * per - 1, 0), 0))
    nxt = pl.BlockSpec((CONV_HALO, width), lambda i: (jnp.minimum((i + 1) * per, nblk * per - 1), 0))
    return prev, nxt


def _fill_halo(scr, prev_ref, cur_ref, next_ref, i, nblk, tm):
    scr[0:CONV_HALO, :] = jnp.where(i > 0, prev_ref[...], 0.0)
    scr[CONV_HALO:CONV_HALO + tm, :] = cur_ref[...]
    scr[CONV_HALO + tm:2 * CONV_HALO + tm, :] = jnp.where(i < nblk - 1, next_ref[...], 0.0)


CONV_ROWS = 128


CONV_REACH = (CONV_WIDTH // SUBLANES) * SUBLANES


def _windows(scr, stage, cols, tm):
    for r in range(SUBLANES):
        if r:
            stage[r] = scr[pl.ds(r, tm + CONV_REACH), cols]
        for a in range(CONV_REACH // SUBLANES + 1):
            off = SUBLANES * a + r
            if 1 <= off <= CONV_WIDTH:
                yield off, (stage[r, SUBLANES * a:SUBLANES * a + tm, :] if r
                            else scr[SUBLANES * a:SUBLANES * a + tm, cols])


def _conv_fwd(hg, w_dw, b_dw, name):
    R, Dm = hg.shape
    tm = _tile(R, CONV_ROWS, CONV_HALO)
    nblk = R // tm
    prev_spec, next_spec = _halo_specs(tm, nblk, Dm)

    def body(prev_ref, cur_ref, next_ref, w_ref, bdw_ref, hd_ref, scr, stage):
        _fill_halo(scr, prev_ref, cur_ref, next_ref, pl.program_id(0), nblk, tm)
        for cb in range(Dm // LANES):
            cols = slice(cb * LANES, (cb + 1) * LANES)
            acc = jnp.zeros((tm, LANES), F32) + bdw_ref[:, cols]
            for off, win in _windows(scr, stage, cols, tm):
                acc = acc + w_ref[off - 1:off, cols] * win
            hd_ref[:, cols] = acc

    return pl.pallas_call(
        body, name=name, grid=(nblk,),
        out_shape=jax.ShapeDtypeStruct((R, Dm), F32),
        in_specs=[prev_spec, _row_spec(tm, Dm), next_spec,
                  pl.BlockSpec((CONV_WIDTH, Dm), lambda i: (0, 0)), _vec_spec(Dm)],
        out_specs=_row_spec(tm, Dm),
        scratch_shapes=[pltpu.VMEM((tm + 2 * CONV_HALO, Dm), F32),
                        pltpu.VMEM((SUBLANES, tm + CONV_REACH, LANES), F32)],
        compiler_params=pltpu.CompilerParams(dimension_semantics=("parallel",)),
    )(hg, hg, hg, w_dw, b_dw)


def _ln_silu_pw2(hd, ln_g, ln_b, w, bias, res, gate, norm, name, tm=512):
    R, Dm = hd.shape
    tm = _tile(R, tm, 2 * SUBLANES)

    def body(hd_ref, g_ref, b_ref, w_ref, bias_ref, res_ref, gate_ref, ng_ref, nsc_ref, nsh_ref,
             hs_ref, h_ref, y_ref, xn_ref):
        hd = hd_ref[...]
        xc = hd - jnp.mean(hd, axis=-1, keepdims=True)
        rs = lax.rsqrt(jnp.mean(xc * xc, axis=-1, keepdims=True) + EPS)
        hs = _silu(xc * rs * g_ref[...] + b_ref[...]).astype(BF16)
        hs_ref[...] = hs
        y = _dot(hs, w_ref[...], NN) + bias_ref[...]
        y_ref[...] = y.astype(BF16)
        h = res_ref[...] + gate_ref[...] * y
        h_ref[...] = h
        xn_ref[...] = _norm_mod(h, ng_ref[...], nsc_ref[...], nsh_ref[...]).astype(BF16)

    row, vec = _row_spec(tm, Dm), _vec_spec(Dm)
    return pl.pallas_call(
        body, name=name, grid=(R // tm,),
        out_shape=[jax.ShapeDtypeStruct((R, Dm), BF16), jax.ShapeDtypeStruct((R, Dm), F32),
                   jax.ShapeDtypeStruct((R, Dm), BF16), jax.ShapeDtypeStruct((R, Dm), BF16)],
        in_specs=[row, vec, vec, pl.BlockSpec(w.shape, lambda i: (0, 0)), vec, row, vec, vec, vec, vec],
        out_specs=[row, row, row, row],
        compiler_params=pltpu.CompilerParams(dimension_semantics=("parallel",)),
    )(hd, ln_g, ln_b, w, bias, res, gate, *norm)


def _pw2_dx_ln_silu_bwd(dy, w, hd, ln_g, ln_b, name, tm=512):
    R, Dm = hd.shape
    tm = _tile(R, tm, 2 * SUBLANES)

    def body(dy_ref, w_ref, hd_ref, g_ref, b_ref, dhd_ref, dg_ref, db_ref, dsum_ref):
        i = pl.program_id(0)

        @pl.when(i == 0)
        def _():
            dg_ref[...] = jnp.zeros_like(dg_ref)
            db_ref[...] = jnp.zeros_like(db_ref)
            dsum_ref[...] = jnp.zeros_like(dsum_ref)

        hd = hd_ref[...]
        mu = jnp.mean(hd, axis=-1, keepdims=True)
        xc = hd - mu
        rs = lax.rsqrt(jnp.mean(xc * xc, axis=-1, keepdims=True) + EPS)
        z = xc * rs
        hl = z * g_ref[...] + b_ref[...]
        dhl = _dot(dy_ref[...], w_ref[...], NT) * _dsilu(hl)
        dg_ref[...] += jnp.sum(dhl * z, axis=0, keepdims=True)
        db_ref[...] += jnp.sum(dhl, axis=0, keepdims=True)
        dz = dhl * g_ref[...]
        dhd = rs * (dz - jnp.mean(dz, axis=-1, keepdims=True) - z * jnp.mean(dz * z, axis=-1, keepdims=True))
        dsum_ref[...] += jnp.sum(dhd, axis=0, keepdims=True)
        dhd_ref[...] = dhd

    return pl.pallas_call(
        body, name=name, grid=(R // tm,),
        out_shape=[jax.ShapeDtypeStruct((R, Dm), F32)] + [jax.ShapeDtypeStruct((1, Dm), F32)] * 3,
        in_specs=[_row_spec(tm, Dm), pl.BlockSpec(w.shape, lambda i: (0, 0)), _row_spec(tm, Dm),
                  _vec_spec(Dm), _vec_spec(Dm)],
        out_specs=[_row_spec(tm, Dm), _vec_spec(Dm), _vec_spec(Dm), _vec_spec(Dm)],
        compiler_params=pltpu.CompilerParams(dimension_semantics=("arbitrary",)),
    )(dy, w, hd, ln_g, ln_b)


def _conv_bwd(dhd, hg, w_dw, name):
    R, Dm = hg.shape
    tm = _tile(R, CONV_ROWS, CONV_HALO)
    nblk = R // tm
    prev_spec, next_spec = _halo_specs(tm, nblk, Dm)

    def body(dprev, dcur, dnext, gprev, gcur, gnext, w_ref, dhg_ref, dw_ref, dscr, gscr, dwp, stage):
        i = pl.program_id(0)

        @pl.when(i == 0)
        def _():
            dwp[...] = jnp.zeros_like(dwp)

        _fill_halo(dscr, dprev, dcur, dnext, i, nblk, tm)
        _fill_halo(gscr, gprev, gcur, gnext, i, nblk, tm)
        for cb in range(Dm // LANES):
            cols = slice(cb * LANES, (cb + 1) * LANES)
            acc = jnp.zeros((tm, LANES), F32)
            for off, win in _windows(dscr, stage, cols, tm):
                j = CONV_WIDTH - off
                acc = acc + w_ref[j:j + 1, cols] * win
            dhg_ref[:, cols] = acc
            d_here = dcur[:, cols]
            for off, win in _windows(gscr, stage, cols, tm):
                j = off - 1
                prod = d_here * win
                part = prod[0:SUBLANES]
                for k in range(1, tm // SUBLANES):
                    part = part + prod[k * SUBLANES:(k + 1) * SUBLANES]
                dwp[j * SUBLANES:(j + 1) * SUBLANES, cols] += part

        @pl.when(i == nblk - 1)
        def _():
            for j in range(CONV_WIDTH):
                dw_ref[j:j + 1, :] = jnp.sum(dwp[j * SUBLANES:(j + 1) * SUBLANES, :], axis=0, keepdims=True)

    return pl.pallas_call(
        body, name=name, grid=(nblk,),
        out_shape=[jax.ShapeDtypeStruct((R, Dm), F32), jax.ShapeDtypeStruct((CONV_WIDTH, Dm), F32)],
        in_specs=[prev_spec, _row_spec(tm, Dm), next_spec, prev_spec, _row_spec(tm, Dm), next_spec,
                  pl.BlockSpec((CONV_WIDTH, Dm), lambda i: (0, 0))],
        out_specs=[_row_spec(tm, Dm), pl.BlockSpec((CONV_WIDTH, Dm), lambda i: (0, 0))],
        scratch_shapes=[pltpu.VMEM((tm + 2 * CONV_HALO, Dm), F32)] * 2
        + [pltpu.VMEM((CONV_WIDTH * SUBLANES, Dm), F32), pltpu.VMEM((SUBLANES, tm + CONV_REACH, LANES), F32)],
        compiler_params=pltpu.CompilerParams(dimension_semantics=("arbitrary",)),
    )(dhd, dhd, dhd, hg, hg, hg, w_dw)


def _swap16(y, lane):
    return jnp.where((lane & 16) == 0, pltpu.roll(y, LANES - 16, 1), pltpu.roll(y, 16, 1))


def _head_mean(v, bd):
    hi, lo = _split_bf16(v)
    return (_dot(hi, bd, NN) + _dot(lo, bd, NN)) * (1.0 / HEAD_DIM)


Q_COLS = (0, ATTN_WIDTH)
K_COLS = (ATTN_WIDTH, ATTN_WIDTH + HEAD_DIM * 2)
V_COLS = (K_COLS[1], K_COLS[1] + HEAD_DIM * 2)
SU_COLS = (V_COLS[1], V_COLS[1] + SG_WIDTH)
SV_COLS = (SU_COLS[1], SU_COLS[1] + SG_WIDTH)


def _mix_prep_fwd(p, ctx_rows, cos, sin, qg, kg, bd, w_sp, b_spt, name):
    TT = p.shape[0]
    off = ctx_rows // CHUNK
    q_scale = HEAD_DIM ** -0.5

    def body(p_ref, cos_ref, sin_ref, qg_ref, kg_ref, bd_ref, w_ref, b_ref,
             q_ref, kp_ref, vp_ref, kt_ref, sg_ref):
        lane = lax.broadcasted_iota(jnp.int32, (CHUNK, LANES), 1)
        low = lane < HEAD_DIM
        cs, sn, bdv = cos_ref[...], sin_ref[...], bd_ref[...]

        def norm_rope(xv, gain):
            r = lax.rsqrt(_head_mean(xv * xv, bdv) + EPS)
            yv = xv * r * gain
            return yv * cs + _swap16(yv, lane) * sn

        def pad_heads(ref, t):
            tr = pltpu.roll(t, HEAD_DIM, 1)
            ref[0, 0] = jnp.where(low, t, 0.0).astype(BF16)
            ref[0, 1] = jnp.where(low, 0.0, tr).astype(BF16)
            ref[1, 0] = jnp.where(low, tr, 0.0).astype(BF16)
            ref[1, 1] = jnp.where(low, 0.0, t).astype(BF16)

        for a in range(ATTN_WIDTH // LANES):
            xv = p_ref[:, a * LANES:(a + 1) * LANES]
            q_ref[:, a * LANES:(a + 1) * LANES] = (norm_rope(xv, qg_ref[...]) * q_scale).astype(BF16)
        kh = norm_rope(p_ref[:, K_COLS[0]:K_COLS[1]], kg_ref[...])
        pad_heads(kp_ref, kh)
        pad_heads(vp_ref, p_ref[:, V_COLS[0]:V_COLS[1]])
        kht = kh.T
        kt_ref[0] = kht[:HEAD_DIM].astype(BF16)
        kt_ref[1] = kht[HEAD_DIM:].astype(BF16)
        for g in range(N_SG_GROUPS):
            u = _gelu(p_ref[:, SU_COLS[0] + g * LANES:SU_COLS[0] + (g + 1) * LANES])
            vg = _gelu(p_ref[:, SV_COLS[0] + g * LANES:SV_COLS[0] + (g + 1) * LANES])
            xc = vg - jnp.mean(vg, axis=-1, keepdims=True)
            vn = xc * lax.rsqrt(jnp.mean(xc * xc, axis=-1, keepdims=True) + EPS)
            mixed = _dot(w_ref[g].astype(BF16), vn.astype(BF16), NN) + b_ref[:, g:g + 1]
            sg_ref[:, g * LANES:(g + 1) * LANES] = (u * mixed).astype(BF16)

    def row(width):
        return pl.BlockSpec((CHUNK, width), lambda i: (i, 0))

    def whole(shape):
        return pl.BlockSpec(shape, lambda i: (0,) * len(shape))

    pad_spec = pl.BlockSpec((2, 2, CHUNK, LANES), lambda i: (0, 0, i, 0))
    return pl.pallas_call(
        body, name=name, grid=(TT // CHUNK,),
        out_shape=[jax.ShapeDtypeStruct((TT, ATTN_WIDTH), BF16),
                   jax.ShapeDtypeStruct((2, 2, TT, LANES), BF16), jax.ShapeDtypeStruct((2, 2, TT, LANES), BF16),
                   jax.ShapeDtypeStruct((2, HEAD_DIM, TT), BF16),
                   jax.ShapeDtypeStruct((TT - ctx_rows, ATTN_WIDTH + SG_WIDTH), BF16)],
        in_specs=[row(IN_WIDTH), row(LANES), row(LANES), whole((1, LANES)), whole((1, LANES)),
                  whole((LANES, LANES)), whole((N_SG_GROUPS, CHUNK, CHUNK)), whole((CHUNK, N_SG_GROUPS))],
        out_specs=[row(ATTN_WIDTH), pad_spec, pad_spec,
                   pl.BlockSpec((2, HEAD_DIM, CHUNK), lambda i: (0, 0, i)),
                   pl.BlockSpec((CHUNK, SG_WIDTH), lambda i: (jnp.maximum(i - off, 0), 1))],
        compiler_params=pltpu.CompilerParams(dimension_semantics=("arbitrary",)),
    )(p, cos, sin, qg, kg, bd, w_sp, b_spt)


def _mix_prep_bwd(p, dq, f, dao, ctx_rows, cos, sin, qg, kg, bd, w_sp, w_spt, b_spt, name):
    TT = p.shape[0]
    off = ctx_rows // CHUNK
    q_scale = HEAD_DIM ** -0.5

    def body(p_ref, dq_ref, f_ref, dsg_ref, cos_ref, sin_ref, qg_ref, kg_ref, bd_ref, w_ref, wt_ref,
             b_ref, dp_ref, dqg_ref, dkg_ref, dw_ref, db_ref):
        i = pl.program_id(0)

        @pl.when(i == 0)
        def _():
            dqg_ref[...] = jnp.zeros_like(dqg_ref)
            dkg_ref[...] = jnp.zeros_like(dkg_ref)
            dw_ref[...] = jnp.zeros_like(dw_ref)
            db_ref[...] = jnp.zeros_like(db_ref)

        latent = (i >= off).astype(F32)
        lane = lax.broadcasted_iota(jnp.int32, (CHUNK, LANES), 1)
        low = lane < HEAD_DIM
        cs, sn, bdv = cos_ref[...], sin_ref[...], bd_ref[...]

        def fold(b0):
            return jnp.where(low, f_ref[0, b0] + pltpu.roll(f_ref[0, b0 + 1], HEAD_DIM, 1),
                             pltpu.roll(f_ref[1, b0], HEAD_DIM, 1) + f_ref[1, b0 + 1])

        def norm_rope_bwd(xv, dout, gain):
            r = lax.rsqrt(_head_mean(xv * xv, bdv) + EPS)
            n = xv * r
            dy = dout * cs + _swap16(dout * sn, lane)
            dn = dy * gain
            dx = r * (dn - n * _head_mean(dn * n, bdv))
            return dx, jnp.sum(dy * n, axis=0, keepdims=True)

        for a in range(ATTN_WIDTH // LANES):
            cols = slice(a * LANES, (a + 1) * LANES)
            dx, dg = norm_rope_bwd(p_ref[:, cols], dq_ref[:, cols] * (latent * q_scale), qg_ref[...])
            dp_ref[:, cols] = dx.astype(BF16)
            dqg_ref[...] += dg
        dx, dg = norm_rope_bwd(p_ref[:, K_COLS[0]:K_COLS[1]], fold(0), kg_ref[...])
        dp_ref[:, K_COLS[0]:K_COLS[1]] = dx.astype(BF16)
        dkg_ref[...] += dg
        dp_ref[:, V_COLS[0]:V_COLS[1]] = fold(2).astype(BF16)
        for g in range(N_SG_GROUPS):
            su = p_ref[:, SU_COLS[0] + g * LANES:SU_COLS[0] + (g + 1) * LANES]
            sv = p_ref[:, SV_COLS[0] + g * LANES:SV_COLS[0] + (g + 1) * LANES]
            (u, dgelu_su), (vg, dgelu_sv) = _gelu_and_grad(su), _gelu_and_grad(sv)
            xc = vg - jnp.mean(vg, axis=-1, keepdims=True)
            rs = lax.rsqrt(jnp.mean(xc * xc, axis=-1, keepdims=True) + EPS)
            vn = xc * rs
            vnb = vn.astype(BF16)
            mixed = _dot(w_ref[g].astype(BF16), vnb, NN) + b_ref[:, g:g + 1]
            dsg = dsg_ref[:, g * LANES:(g + 1) * LANES].astype(F32) * latent
            du = dsg * mixed
            dmix = dsg * u
            dmb = dmix.astype(BF16)
            db_ref[:, g:g + 1] += jnp.sum(dmix, axis=-1, keepdims=True)
            dw_ref[g] += _dot(dmb, vnb, NT)
            dvn = _dot(wt_ref[g].astype(BF16), dmb, NN)
            dvg = rs * (dvn - jnp.mean(dvn, axis=-1, keepdims=True)
                        - vn * jnp.mean(dvn * vn, axis=-1, keepdims=True))
            dp_ref[:, SU_COLS[0] + g * LANES:SU_COLS[0] + (g + 1) * LANES] = (du * dgelu_su).astype(BF16)
            dp_ref[:, SV_COLS[0] + g * LANES:SV_COLS[0] + (g + 1) * LANES] = (dvg * dgelu_sv).astype(BF16)

    def row(width):
        return pl.BlockSpec((CHUNK, width), lambda i: (i, 0))

    def latent_row(width, col_block):
        return pl.BlockSpec((CHUNK, width), lambda i: (jnp.maximum(i - off, 0), col_block))

    def whole(shape):
        return pl.BlockSpec(shape, lambda i: (0,) * len(shape))

    return pl.pallas_call(
        body, name=name, grid=(TT // CHUNK,),
        out_shape=[jax.ShapeDtypeStruct((TT, IN_WIDTH), BF16), jax.ShapeDtypeStruct((1, LANES), F32),
                   jax.ShapeDtypeStruct((1, LANES), F32),
                   jax.ShapeDtypeStruct((N_SG_GROUPS, CHUNK, CHUNK), F32),
                   jax.ShapeDtypeStruct((CHUNK, N_SG_GROUPS), F32)],
        in_specs=[row(IN_WIDTH), latent_row(ATTN_WIDTH, 0),
                  pl.BlockSpec((2, 4, CHUNK, LANES), lambda i: (0, 0, i, 0)),
                  latent_row(SG_WIDTH, 1), row(LANES), row(LANES), whole((1, LANES)), whole((1, LANES)),
                  whole((LANES, LANES)), whole((N_SG_GROUPS, CHUNK, CHUNK)),
                  whole((N_SG_GROUPS, CHUNK, CHUNK)), whole((CHUNK, N_SG_GROUPS))],
        out_specs=[row(IN_WIDTH), whole((1, LANES)), whole((1, LANES)),
                   whole((N_SG_GROUPS, CHUNK, CHUNK)), whole((CHUNK, N_SG_GROUPS))],
        compiler_params=pltpu.CompilerParams(dimension_semantics=("arbitrary",)),
    )(p, dq, f, dao, cos, sin, qg, kg, bd, w_sp, w_spt, b_spt)


def _attn_fwd(q, kpad, vpad, ao, ctx_rows, name, tq=256):
    TT = q.shape[0]
    T = TT - ctx_rows
    tq = _tile(T, tq)
    off = ctx_rows // tq
    group = 2 * LANES

    def body(q_ref, k_ref, v_ref, ao_in, o_ref, lse_ref):
        del ao_in
        lane = lax.broadcasted_iota(jnp.int32, (tq, LANES), 1)
        lse = jnp.zeros((tq, LANES), F32)
        for a in range(2):
            acc = jnp.zeros((tq, LANES), F32)
            qa = q_ref[:, a * LANES:(a + 1) * LANES]
            for b in range(2):
                s = _dot(qa, k_ref[0, b], NT)
                m = jnp.max(s, axis=-1, keepdims=True)
                e = jnp.exp(s - m)
                l = jnp.sum(e, axis=-1, keepdims=True)
                acc = acc + _dot(e.astype(BF16), v_ref[0, b], NN) * (1.0 / l)
                lse = jnp.where(lane == 2 * a + b, m + jnp.log(l), lse)
            o_ref[:, a * LANES:(a + 1) * LANES] = acc.astype(BF16)
        lse_ref[0] = lse

    kv_spec = pl.BlockSpec((1, 2, TT, LANES), lambda j, i: (j, 0, 0, 0))
    return pl.pallas_call(
        body, name=name, grid=(2, T // tq),
        out_shape=[jax.ShapeDtypeStruct(ao.shape, BF16), jax.ShapeDtypeStruct((2, T, LANES), F32)],
        in_specs=[pl.BlockSpec((tq, group), lambda j, i: (i + off, j)), kv_spec, kv_spec,
                  pl.BlockSpec(memory_space=pl.ANY)],
        out_specs=[pl.BlockSpec((tq, group), lambda j, i: (i, j)),
                   pl.BlockSpec((1, tq, LANES), lambda j, i: (j, i, 0))],
        input_output_aliases={3: 0},
        compiler_params=pltpu.CompilerParams(dimension_semantics=("parallel", "parallel")),
    )(q, kpad, vpad, ao)


def _attn_bwd(q, dao, ao, lse, kpad, vpad, kt, ctx_rows, name, tq=256):
    TT = q.shape[0]
    T = TT - ctx_rows
    tq = _tile(T, tq)
    off = ctx_rows // tq
    group = 2 * LANES

    def body(q_ref, do_ref, o_ref, lse_ref, k_ref, v_ref, kt_ref, dq_ref, f_ref):
        i = pl.program_id(1)

        @pl.when(i == 0)
        def _():
            f_ref[...] = jnp.zeros_like(f_ref)

        ktv = kt_ref[0]
        lse_t = lse_ref[0].T
        row = lax.broadcasted_iota(jnp.int32, (SUBLANES, LANES), 0)
        lane = lax.broadcasted_iota(jnp.int32, (SUBLANES, LANES), 1)
        half_ones = (jnp.where(lane < HEAD_DIM, 0, 1) == row).astype(BF16)
        for a in range(2):
            cols = slice(a * LANES, (a + 1) * LANES)
            qa = q_ref[:, cols]
            do32 = do_ref[:, cols].astype(F32)
            doa = do32.astype(BF16)
            hi, lo = _split_bf16(do32 * o_ref[:, cols].astype(F32))
            deltas = _dot(half_ones, hi, NT) + _dot(half_ones, lo, NT)
            halves = []
            for b in range(2):
                h = 2 * a + b
                st = _dot(k_ref[0, b], qa, NT)
                pt = jnp.exp(st - lse_t[h:h + 1, :])
                dpt = _dot(v_ref[0, b], doa, NT)
                dst = (pt * (dpt - deltas[b:b + 1, :])).astype(BF16)
                f_ref[0, b] += _dot(dst, qa, NN)
                f_ref[0, 2 + b] += _dot(pt.astype(BF16), doa, NN)
                halves.append(_dot(ktv, dst, NN))
            dq_ref[:, cols] = jnp.concatenate(halves, axis=0).T

    kv_spec = pl.BlockSpec((1, 2, TT, LANES), lambda j, i: (j, 0, 0, 0))
    out_cols = pl.BlockSpec((tq, group), lambda j, i: (i, j))
    return pl.pallas_call(
        body, name=name, grid=(2, T // tq),
        out_shape=[jax.ShapeDtypeStruct((T, ATTN_WIDTH), F32), jax.ShapeDtypeStruct((2, 4, TT, LANES), F32)],
        in_specs=[pl.BlockSpec((tq, group), lambda j, i: (i + off, j)), out_cols, out_cols,
                  pl.BlockSpec((1, tq, LANES), lambda j, i: (j, i, 0)),
                  kv_spec, kv_spec, pl.BlockSpec((1, HEAD_DIM, TT), lambda j, i: (j, 0, 0))],
        out_specs=[out_cols, pl.BlockSpec((1, 4, TT, LANES), lambda j, i: (j, 0, 0, 0))],
        compiler_params=pltpu.CompilerParams(dimension_semantics=("parallel", "arbitrary")),
    )(q, dao, ao, lse, kpad, vpad, kt)


def _final_fwd_bwd(h, g, target, y, gt, name):
    R, Dm = h.shape
    tm = _tile(R, ROW_BLOCK, 8)

    def body(h_ref, g_ref, t_ref, y_ref, gt_ref, dh_ref, loss_ref, dg_ref, dy_ref, dgt_ref, dsum_ref):
        i = pl.program_id(0)

        @pl.when(i == 0)
        def _():
            for ref in (loss_ref, dg_ref, dgt_ref, dsum_ref):
                ref[...] = jnp.zeros_like(ref)

        hv = h_ref[...]
        r = lax.rsqrt(jnp.mean(hv * hv, axis=-1, keepdims=True) + EPS)
        n = hv * r
        diff = n * g_ref[...] - t_ref[...]
        loss_ref[...] += jnp.sum(diff * diff)
        dout = diff * (1.0 / Dm)
        dg_ref[...] += jnp.sum(dout * n, axis=0, keepdims=True)
        dn = dout * g_ref[...]
        dh = r * (dn - n * jnp.mean(dn * n, axis=-1, keepdims=True))
        dh_ref[...] = dh
        _gate_grads(dh, y_ref, gt_ref, dy_ref, dgt_ref, dsum_ref)

    vec = jax.ShapeDtypeStruct((1, Dm), F32)
    return pl.pallas_call(
        body, name=name, grid=(R // tm,),
        out_shape=[jax.ShapeDtypeStruct((R, Dm), F32), jax.ShapeDtypeStruct((1, LANES), F32), vec,
                   jax.ShapeDtypeStruct((R, Dm), BF16), vec, vec],
        in_specs=[_row_spec(tm, Dm), _vec_spec(Dm), _row_spec(tm, Dm), _row_spec(tm, Dm), _vec_spec(Dm)],
        out_specs=[_row_spec(tm, Dm), _vec_spec(LANES), _vec_spec(Dm), _row_spec(tm, Dm), _vec_spec(Dm),
                   _vec_spec(Dm)],
        compiler_params=pltpu.CompilerParams(dimension_semantics=("arbitrary",)),
    )(h, g, target, y, gt)


MOD_ROWS = 16


def _mod_fwd(c_rows, w_mod, name):
    L, Dm, n = w_mod.shape

    def body(c_ref, w_ref, o_ref):
        o_ref[0] = _dot3(_silu(c_ref[...]), w_ref[0], NN)

    return pl.pallas_call(
        body, name=name, grid=(L,),
        out_shape=jax.ShapeDtypeStruct((L, MOD_ROWS, n), F32),
        in_specs=[pl.BlockSpec((MOD_ROWS, Dm), lambda l: (0, 0)), pl.BlockSpec((1, Dm, n), lambda l: (l, 0, 0))],
        out_specs=pl.BlockSpec((1, MOD_ROWS, n), lambda l: (l, 0, 0)),
        compiler_params=pltpu.CompilerParams(dimension_semantics=("parallel",)),
    )(c_rows, w_mod)


def _mod_bwd(c_rows_t, dmod, w_mod, name):
    L, Dm, n = w_mod.shape

    def body(ct_ref, d_ref, w_ref, gw_ref, ds_ref):
        dm = d_ref[0]
        gw_ref[0] = _dot3(_silu(ct_ref[...]), dm, NN)
        ds_ref[0] = _dot3(dm[:MOD_ROWS], w_ref[0], NT)

    return pl.pallas_call(
        body, name=name, grid=(L,),
        out_shape=[jax.ShapeDtypeStruct((L, Dm, n), F32), jax.ShapeDtypeStruct((L, MOD_ROWS, Dm), F32)],
        in_specs=[pl.BlockSpec((Dm, LANES), lambda l: (0, 0)), pl.BlockSpec((1, LANES, n), lambda l: (l, 0, 0)),
                  pl.BlockSpec((1, Dm, n), lambda l: (l, 0, 0))],
        out_specs=[pl.BlockSpec((1, Dm, n), lambda l: (l, 0, 0)),
                   pl.BlockSpec((1, MOD_ROWS, Dm), lambda l: (l, 0, 0))],
        compiler_params=pltpu.CompilerParams(dimension_semantics=("parallel",)),
    )(c_rows_t, dmod, w_mod)


def _adam_update(w, g, m, v):
    c1 = 1.0 - ADAM_B1 ** ADAM_STEP
    c2 = 1.0 - ADAM_B2 ** ADAM_STEP
    mn = ADAM_B1 * m + (1.0 - ADAM_B1) * g
    vn = ADAM_B2 * v + (1.0 - ADAM_B2) * (g * g)
    return -ADAM_LR * ((mn / c1) / (jnp.sqrt(vn / c2) + ADAM_EPS) + ADAM_WD * w), mn, vn


def _adamw(w, g, m, v, name):
    R, Cw = w.shape
    tm = _tile(R, ADAM_ROWS, 8)

    def body(w_ref, g_ref, m_ref, v_ref, d_ref, mo_ref, vo_ref):
        d_ref[...], mo_ref[...], vo_ref[...] = _adam_update(w_ref[...], g_ref[...], m_ref[...], v_ref[...])

    spec = pl.BlockSpec((tm, Cw), lambda i: (i, 0))
    return pl.pallas_call(
        body, name=name, grid=(R // tm,),
        out_shape=[jax.ShapeDtypeStruct((R, Cw), F32)] * 3,
        in_specs=[spec] * 4, out_specs=[spec] * 3,
        compiler_params=pltpu.CompilerParams(dimension_semantics=("parallel",)),
    )(w, g, m, v)


def _adamw_recv(w, m, v, recvs, name):
    L, R, n = w.shape
    tm = _tile(R, ADAM_ROWS, 8)
    nblk = R // tm
    parts = [r.reshape(N_DEV, R, n) for r in recvs]

    def body(*refs):
        w_ref, m_ref, v_ref = refs[:3]
        part_refs = refs[3:3 + L]
        g_ref, d_ref, mo_ref, vo_ref, gsum = refs[3 + L:]
        l = pl.program_id(0)
        for ll in range(L):
            @pl.when(l == ll)
            def _(ll=ll):
                acc = part_refs[ll][0].astype(F32)
                for s in range(1, N_DEV):
                    acc = acc + part_refs[ll][s].astype(F32)
                gsum[...] = acc
        g = gsum[...]
        g_ref[0] = g
        d_ref[0], mo_ref[0], vo_ref[0] = _adam_update(w_ref[0], g, m_ref[0], v_ref[0])

    def part_spec(ll):
        return pl.BlockSpec((N_DEV, tm, n), lambda l, i: (0, jnp.where(l == ll, i, jnp.where(l < ll, 0, nblk - 1)), 0))

    spec = pl.BlockSpec((1, tm, n), lambda l, i: (l, i, 0))
    return pl.pallas_call(
        body, name=name, grid=(L, nblk),
        out_shape=[jax.ShapeDtypeStruct((L, R, n), F32)] * 4,
        in_specs=[spec] * 3 + [part_spec(ll) for ll in range(L)], out_specs=[spec] * 4,
        scratch_shapes=[pltpu.VMEM((tm, n), F32)],
        compiler_params=pltpu.CompilerParams(dimension_semantics=("parallel", "parallel")),
    )(w, m, v, *parts)


def _pack(parts, row_mult=8):
    flat, offs, pos = [], [], 0
    for t in parts:
        t = t.reshape(-1).astype(F32)
        size = -(-t.shape[0] // LANES) * LANES
        flat.append(jnp.pad(t, (0, size - t.shape[0])))
        offs.append(pos)
        pos += size
    total = -(-pos // (LANES * row_mult)) * (LANES * row_mult)
    if total > pos:
        flat.append(jnp.zeros((total - pos,), F32))
    return jnp.concatenate(flat).reshape(-1, LANES), offs


def _take(buf, off, shape):
    size = math.prod(shape)
    return buf[..., off:off + size].reshape(buf.shape[:-1] + tuple(shape))


def _rope_tables(T, ctx_rows):
    pos = jnp.arange(T)
    row = (pos // GRID_W).astype(F32)
    col = (pos % GRID_W).astype(F32)
    half = HEAD_DIM // 4
    inv = ROPE_THETA ** (-jnp.arange(0, 2 * half, 2, dtype=F32) / (2 * half))
    ang_r, ang_c = row[:, None] * inv[None, :], col[:, None] * inv[None, :]
    cos = jnp.concatenate([jnp.cos(ang_r)] * 2 + [jnp.cos(ang_c)] * 2, axis=1)
    sin = jnp.concatenate([-jnp.sin(ang_r), jnp.sin(ang_r), -jnp.sin(ang_c), jnp.sin(ang_c)], axis=1)
    cos = jnp.concatenate([jnp.ones((ctx_rows, HEAD_DIM), F32), cos], axis=0)
    sin = jnp.concatenate([jnp.zeros((ctx_rows, HEAD_DIM), F32), sin], axis=0)
    return jnp.tile(cos, (1, 2)), jnp.tile(sin, (1, 2))


def kernel(x, c, ctx, c_ctx, w_mod, b_mod, g_mix, g_ffn, w_ffn_in, w_ffn_out, w_in, q_gain, k_gain, w_sp, b_sp, w_out, w_pw1, b_pw1, w_dw, b_dw, ln_g, ln_b, w_pw2, b_pw2, g_final, loss_target, m_c_ctx, m_w_mod, m_b_mod, m_g_mix, m_g_ffn, m_w_ffn_in, m_w_ffn_out, m_w_in, m_q_gain, m_k_gain, m_w_sp, m_b_sp, m_w_out, m_w_pw1, m_b_pw1, m_w_dw, m_b_dw, m_ln_g, m_ln_b, m_w_pw2, m_b_pw2, m_g_final, v_c_ctx, v_w_mod, v_b_mod, v_g_mix, v_g_ffn, v_w_ffn_in, v_w_ffn_out, v_w_in, v_q_gain, v_k_gain, v_w_sp, v_b_sp, v_w_out, v_w_pw1, v_b_pw1, v_w_dw, v_b_dw, v_ln_g, v_ln_b, v_w_pw2, v_b_pw2, v_g_final):
    weights = dict(c_ctx=c_ctx, w_mod=w_mod, b_mod=b_mod, g_mix=g_mix, g_ffn=g_ffn, w_ffn_in=w_ffn_in,
                   w_ffn_out=w_ffn_out, w_in=w_in, q_gain=q_gain, k_gain=k_gain, w_sp=w_sp, b_sp=b_sp,
                   w_out=w_out, w_pw1=w_pw1, b_pw1=b_pw1, w_dw=w_dw, b_dw=b_dw, ln_g=ln_g, ln_b=ln_b,
                   w_pw2=w_pw2, b_pw2=b_pw2, g_final=g_final)
    moments_m = dict(c_ctx=m_c_ctx, w_mod=m_w_mod, b_mod=m_b_mod, g_mix=m_g_mix, g_ffn=m_g_ffn,
                     w_ffn_in=m_w_ffn_in, w_ffn_out=m_w_ffn_out, w_in=m_w_in, q_gain=m_q_gain,
                     k_gain=m_k_gain, w_sp=m_w_sp, b_sp=m_b_sp, w_out=m_w_out, w_pw1=m_w_pw1,
                     b_pw1=m_b_pw1, w_dw=m_w_dw, b_dw=m_b_dw, ln_g=m_ln_g, ln_b=m_ln_b, w_pw2=m_w_pw2,
                     b_pw2=m_b_pw2, g_final=m_g_final)
    moments_v = dict(c_ctx=v_c_ctx, w_mod=v_w_mod, b_mod=v_b_mod, g_mix=v_g_mix, g_ffn=v_g_ffn,
                     w_ffn_in=v_w_ffn_in, w_ffn_out=v_w_ffn_out, w_in=v_w_in, q_gain=v_q_gain,
                     k_gain=v_k_gain, w_sp=v_w_sp, b_sp=v_b_sp, w_out=v_w_out, w_pw1=v_w_pw1,
                     b_pw1=v_b_pw1, w_dw=v_w_dw, b_dw=v_b_dw, ln_g=v_ln_g, ln_b=v_ln_b, w_pw2=v_w_pw2,
                     b_pw2=v_b_pw2, g_final=v_g_final)
    names = list(weights)

    T, C = x.shape[1], ctx.shape[1]
    Dm = D_MODEL
    me = 4 * lax.axis_index("x") + 2 * lax.axis_index("y") + lax.axis_index("c")
    h0 = x[0]
    ctx2 = ctx[0]
    target = loss_target[0]

    small_sharded = (("w_dw", w_dw[0]), ("b_pw1", b_pw1), ("b_dw", b_dw), ("ln_g", ln_g), ("ln_b", ln_b),
                     ("b_pw2", b_pw2))
    buf1, offs1 = _pack([c] + [t for _, t in small_sharded])
    w_in_t, m_w_in_t, v_w_in_t = (jnp.swapaxes(t, 1, 2) for t in (w_in, m_w_in, v_w_in))
    w_ffi_t, m_w_ffi_t, v_w_ffi_t = (jnp.swapaxes(t, 1, 2) for t in (w_ffn_in, m_w_ffn_in, v_w_ffn_in))
    got1, W_in_t = _all_gather([buf1, w_in_t[0].astype(BF16)], "gather_cond", False)
    got1 = got1.reshape(N_DEV, -1)
    c_all = _take(got1, offs1[0], (Dm,))
    full_small = {}
    for (nm, t), off in zip(small_sharded, offs1[1:]):
        seg = _take(got1, off, t.shape)
        full_small[nm] = jnp.moveaxis(seg, 0, -2).reshape(t.shape[:-1] + (N_DEV * t.shape[-1],))
    w_dw_f, b_pw1_f = full_small["w_dw"], full_small["b_pw1"]
    b_dw_f, ln_g_f, ln_b_f, b_pw2_f = (full_small[k] for k in ("b_dw", "ln_g", "ln_b", "b_pw2"))

    c_rows = jnp.concatenate([c_all, c_ctx[None, :], jnp.zeros((MOD_ROWS - N_DEV - 1, Dm), F32)], axis=0)
    mod_part = _mod_fwd(c_rows, w_mod, "mod_fwd")
    n_mod = w_mod.shape[2]
    got2 = _all_gather([mod_part.reshape(-1, LANES)], "gather_mod", True)[0]
    mod_all = got2.reshape(N_DEV, 2, MOD_ROWS, n_mod).transpose(1, 2, 0, 3).reshape(2, MOD_ROWS, N_DEV * n_mod)
    mod_all = mod_all + b_mod[:, None, :]
    my_mod = lax.dynamic_index_in_dim(mod_all, me, axis=1, keepdims=False)
    sh1, sc1, gt1, sh2, sc2, gt2 = ([my_mod[l:l + 1, k * Dm:(k + 1) * Dm] for l in range(2)] for k in range(6))
    csh1 = mod_all[0, N_DEV:N_DEV + 1, 0:Dm]
    csc1 = mod_all[0, N_DEV:N_DEV + 1, Dm:2 * Dm]

    behind = got2[0:1, 0:1] * 0.0
    gather_groups = [[w_out[0]], [w_ffi_t[0], w_ffn_out[0]], [w_pw1[0], w_pw2[0]], [w_ffi_t[1], w_ffn_out[1]]]
    gathers = [_push_begin([(t + behind).astype(BF16) for t in grp], True, f"gather_start{k}")
               for k, grp in enumerate(gather_groups)]
    started = sum(h[4][0:1, 0:1] for h in gathers)

    def gathered(k, after):
        return _push_end(gathers[k], after, f"gather_wait{k}")[1]

    def ffn_weights(k, after):
        wi, wo = gathered(k, after)
        return wi.reshape(N_DEV, FF_SHARD, Dm), wo.reshape(N_DEV // 2, FF_SHARD, Dm)

    def col_gathered(t, n):
        return t.reshape(N_DEV, Dm, n).transpose(1, 0, 2).reshape(Dm, N_DEV * n)

    W_ffi, W_ffo = [None, None], [None, None]

    g_mix_r = [g_mix[l:l + 1] for l in range(2)]
    g_ffn_r = [g_ffn[l:l + 1] for l in range(2)]
    g_fin = g_final[None, :]

    cos, sin = _rope_tables(T, C)
    qg = jnp.tile(q_gain, (1, 2))
    kg = jnp.tile(k_gain, (1, 2))
    lane_head = jnp.arange(LANES) // HEAD_DIM
    bd = (lane_head[:, None] == lane_head[None, :]).astype(BF16)
    w_sp0 = w_sp[0]
    w_spt0 = w_sp0.transpose(0, 2, 1)
    b_spt0 = b_sp[0].T

    XM = _norm_mod_fwd_cat(ctx2, h0, g_mix_r[0], csc1, csh1, sc1[0] + started, sh1[0], "norm_mix0")
    P = _mm(XM, W_in_t, "nt", "in_proj", tm=1088, tn=IN_WIDTH)
    qh, kpad, vpad, kt, ao = _mix_prep_fwd(P, C, cos, sin, qg, kg, bd, w_sp0, b_spt0, "mix_prep")
    ao, lse = _attn_fwd(qh, kpad, vpad, ao, C, "attn_fwd")
    W_out, = gathered(0, ao)
    h1, y0, xf0 = _mm(ao, W_out, "nn", "out_proj", res=h0, gate=gt1[0], raw_out=True,
                      norm=(g_ffn_r[0], sc2[0], sh2[0]))

    def ffn_fwd(h_in, xf, l, norm_next):
        W_ffi[l], W_ffo[l] = ffn_weights(1 + 2 * l, xf)
        gu, act = _ffn_in_swiglu(xf, W_ffi[l], f"ffn_in{l}")
        outs = _mm_sum_shards(act, W_ffo[l], "nn", f"ffn_out{l}", res=h_in, gate=gt2[l], raw_out=True,
                              norm=norm_next)
        return tuple(outs) + (None,) * (3 - len(outs)) + (gu, act)

    h2, f0, xm1, gu0, act0 = ffn_fwd(h1, xf0, 0, (g_mix_r[1], sc1[1], sh1[1]))

    W_pw1, W_pw2 = gathered(2, xm1)
    W_pw1 = col_gathered(W_pw1, 2 * Dm // N_DEV)
    ag, hg = _pw1_glu(xm1, W_pw1, b_pw1_f, "pw1")
    hd = _conv_fwd(hg, w_dw_f, b_dw_f, "conv")
    hs, h3, y1, xf1 = _ln_silu_pw2(hd, ln_g_f, ln_b_f, W_pw2, b_pw2_f, h2, gt1[1],
                                   (g_ffn_r[1], sc2[1], sh2[1]), "pw2")
    h4, f1, _, gu1, act1 = ffn_fwd(h3, xf1, 1, None)

    dh4, sq_err, dg_final, df1, dgt2_1, _ = _final_fwd_bwd(h4, g_fin, target, f1, gt2[1], "loss_head")
    loss_local = (0.5 / Dm) * sq_err[0, 0:1]

    def col_shards(g, n):
        return g.reshape(Dm, N_DEV, n).transpose(1, 0, 2).reshape(N_DEV * Dm, n)

    def exchange_begin(k, parts):
        return _push_begin(parts, False, f"exchange_start{k}")

    def zero_of(handle):
        return handle[4][0:1, 0:1]

    def ffn_bwd(df, xf, gu, act, l):
        dw_out = _mm_tn_shard_rows(act, df, f"ffn_out_dw{l}", BF16)
        dgu = _ffn_out_dx_swiglu(df, W_ffo[l], gu, f"ffn_out_dx{l}").reshape(N_DEV, T, FF_SHARD)
        dw_in = _mm_tn_shard_rows(dgu, xf, f"ffn_in_dw{l}", BF16)
        return dw_in, dw_out, dgu

    dW_ffi1, dW_ffo1, dgu1 = ffn_bwd(df1, xf1, gu1, act1, 1)
    ex0 = exchange_begin(0, [dW_ffi1.reshape(2 * D_FF, Dm), dW_ffo1.reshape(D_FF, Dm)])
    dh3, da, dsh, dy1, dgt1_1, db_pw2 = _ffn_in_dx_norm_bwd(dgu1, W_ffi[1], h3, g_ffn_r[1], sc2[1], dh4, y1,
                                                             gt1[1] + zero_of(ex0), "ffn_in_dx1")
    dmod_ffn1 = (dsh, da * g_ffn_r[1], dgt2_1)
    dg_ffn1 = da * (1.0 + sc2[1])

    dW_pw2 = _mm(hs, dy1, "tn", "pw2_dw", BF16, tk=2048)
    dhd, dln_g, dln_b, db_dw = _pw2_dx_ln_silu_bwd(dy1, W_pw2, hd, ln_g_f, ln_b_f, "pw2_dx")
    dhg, dw_dw = _conv_bwd(dhd, hg, w_dw_f, "conv_bwd")
    dag, db_pw1, dh2, da, dsh, df0, dgt2_0, _ = _glu_bwd_pw1_dx_norm_bwd(
        ag, dhg, W_pw1, h2, g_mix_r[1], sc1[1], dh3, f0, gt2[0], "pw1_dx")
    dW_pw1 = _mm(xm1, dag, "tn", "pw1_dw", BF16, tk=2048)
    ex1 = exchange_begin(1, [col_shards(dW_pw1, 2 * Dm // N_DEV), dW_pw2])
    dmod_mix1 = (dsh, da * g_mix_r[1], dgt1_1)
    dg_mix1 = da * (1.0 + sc1[1])

    dW_ffi0, dW_ffo0, dgu0 = ffn_bwd(df0, xf0, gu0, act0, 0)
    ex2 = exchange_begin(2, [dW_ffi0.reshape(2 * D_FF, Dm), dW_ffo0.reshape(D_FF, Dm)])
    dh1, da, dsh, dy0, dgt1_0, _ = _ffn_in_dx_norm_bwd(dgu0, W_ffi[0], h1, g_ffn_r[0], sc2[0], dh2, y0,
                                                       gt1[0] + zero_of(ex1) + zero_of(ex2), "ffn_in_dx0")
    dmod_ffn0 = (dsh, da * g_ffn_r[0], dgt2_0)
    dg_ffn0 = da * (1.0 + sc2[0])

    dW_out = _mm(ao, dy0, "tn", "out_proj_dw", BF16, tk=2048)
    ex_out = exchange_begin(4, [dW_out])
    dao = _mm(dy0, W_out + zero_of(ex_out).astype(BF16), "nt", "out_proj_dx", BF16)
    dq, f_acc = _attn_bwd(qh, dao, ao, lse, kpad, vpad, kt, C, "attn_bwd")
    dP, dqg, dkg, dw_sp0, db_spt0 = _mix_prep_bwd(P, dq, f_acc, dao, C, cos, sin, qg, kg, bd, w_sp0, w_spt0,
                                                  b_spt0, "mix_prep_bwd")
    dW_in_t = _mm(dP, XM, "tn", "in_proj_dw", BF16, tm=896, tk=2176)
    dXM = _mm(dP, W_in_t, "nn", "in_proj_dx", BF16, tm=1088, tk=IN_WIDTH)
    dh0, da, dsh = _norm_mod_bwd(h0, g_mix_r[0], sc1[0], dXM, dh1, "norm_mix0_bwd", dxm_row_off=C)
    _, dac, dcsh = _norm_mod_bwd(ctx2, g_mix_r[0], csc1, dXM, None, "norm_ctx_bwd")
    dmod_mix0 = (dsh, da * g_mix_r[0], dgt1_0)
    dg_mix0 = da * (1.0 + sc1[0]) + dac * (1.0 + csc1)
    dcmod = jnp.concatenate([dcsh, dac * g_mix_r[0]], axis=1)

    dmod_mine = jnp.stack([jnp.concatenate(dmod_mix0 + dmod_ffn0, axis=1)[0],
                           jnp.concatenate(dmod_mix1 + dmod_ffn1, axis=1)[0]])

    small_grads = [
        ("loss", loss_local), ("g_final", dg_final), ("g_mix", jnp.concatenate([dg_mix0, dg_mix1])),
        ("g_ffn", jnp.concatenate([dg_ffn0, dg_ffn1])),
        ("q_gain", dqg[:, :HEAD_DIM] + dqg[:, HEAD_DIM:]), ("k_gain", dkg[:, :HEAD_DIM] + dkg[:, HEAD_DIM:]),
        ("w_sp", dw_sp0[None]), ("b_sp", db_spt0.T[None]), ("b_pw1", db_pw1), ("w_dw", dw_dw[None]),
        ("b_dw", db_dw), ("ln_g", dln_g), ("ln_b", dln_b), ("b_pw2", db_pw2), ("dcmod", dcmod),
        ("dmod", dmod_mine),
    ]
    buf3, offs3 = _pack([t for _, t in small_grads])
    off3 = {nm: off for (nm, _), off in zip(small_grads, offs3)}
    shape3 = {nm: t.shape for nm, t in small_grads}
    small_push = _push_begin([buf3], True, "small_grads_start")
    ex3 = exchange_begin(3, [dW_in_t + zero_of(small_push).astype(BF16)])

    grads, delta, new_m, new_v = {}, {}, {}, {}

    def exchanged(k, handle, after):
        return _push_end(handle, after, f"exchange_wait{k}")[1]

    def adamw_big(nm, parts, transposed=False, wmv=None):
        w3, m3, v3 = wmv if wmv is not None else (weights[nm], moments_m[nm], moments_v[nm])
        outs4 = _adamw_recv(w3, m3, v3, parts, f"adamw_{nm}")
        if transposed:
            outs4 = [jnp.swapaxes(t, 1, 2) for t in outs4]
        grads[nm], delta[nm], new_m[nm], new_v[nm] = outs4

    pushed = ex3[4]
    r_ffi1, r_ffo1 = exchanged(0, ex0, pushed)
    r_pw1, r_pw2 = exchanged(1, ex1, pushed)
    r_ffi0, r_ffo0 = exchanged(2, ex2, pushed)
    r_out, = exchanged(4, ex_out, pushed)
    adamw_big("w_ffn_in", [r_ffi0, r_ffi1], True, (w_ffi_t, m_w_ffi_t, v_w_ffi_t))
    adamw_big("w_ffn_out", [r_ffo0, r_ffo1])
    adamw_big("w_pw1", [r_pw1])
    adamw_big("w_pw2", [r_pw2])
    adamw_big("w_out", [r_out])

    got3 = _push_end(small_push, delta["w_out"], "small_grads_wait")[1][0].reshape(N_DEV, buf3.shape[0], LANES)
    sum3 = _sum_devices(got3, "sum_small_grads").reshape(-1)

    def summed(nm):
        return _take(sum3, off3[nm], shape3[nm])

    loss = summed("loss")[0]
    dcmod_sum = summed("dcmod")
    dmod_rows = _take(got3.reshape(N_DEV, -1), off3["dmod"], (2, 6 * Dm)).transpose(1, 0, 2)
    ctx_row = jnp.concatenate([jnp.pad(dcmod_sum, ((0, 0), (0, 4 * Dm))), jnp.zeros((1, 6 * Dm), F32)])
    dmod_all = jnp.concatenate([dmod_rows, ctx_row[:, None, :],
                                jnp.zeros((2, LANES - N_DEV - 1, 6 * Dm), F32)], axis=1)
    grads["b_mod"] = summed("dmod") + ctx_row
    dmod_shard = lax.dynamic_slice_in_dim(dmod_all, me * n_mod, n_mod, axis=2)
    c_rows_t = jnp.pad(c_rows.T, ((0, 0), (0, LANES - MOD_ROWS)))
    grads["w_mod"], ds_part = _mod_bwd(c_rows_t, dmod_shard, w_mod, "mod_bwd")

    buf4, _ = _pack([ds_part[0, N_DEV]])
    got4 = _all_gather([buf4], "gather_c_ctx_grad", True)[0].reshape(N_DEV, buf4.shape[0], LANES)
    ds_ctx = _sum_devices(got4, "sum_c_ctx_grad").reshape(-1)[:Dm]
    grads["c_ctx"] = ds_ctx * _dsilu(c_ctx)

    for nm in ("g_final", "g_mix", "g_ffn", "q_gain", "k_gain", "w_sp", "b_sp"):
        grads[nm] = summed(nm).reshape(weights[nm].shape)
    for nm in ("b_pw1", "w_dw", "b_dw", "ln_g", "ln_b", "b_pw2"):
        n_loc = weights[nm].shape[-1]
        grads[nm] = lax.dynamic_slice_in_dim(summed(nm), me * n_loc, n_loc, axis=-1).reshape(weights[nm].shape)

    shp = w_mod.shape
    outs = _adamw(w_mod.reshape(-1, shp[-1]), grads["w_mod"].reshape(-1, shp[-1]),
                  m_w_mod.reshape(-1, shp[-1]), v_w_mod.reshape(-1, shp[-1]), "adamw_w_mod")
    delta["w_mod"], new_m["w_mod"], new_v["w_mod"] = (o.reshape(shp) for o in outs)
    big_names = ("w_mod", "w_ffn_in", "w_ffn_out", "w_in", "w_out", "w_pw1", "w_pw2")
    small_names = [nm for nm in names if nm not in big_names]
    packs = [_pack([src[nm] for nm in small_names]) for src in (weights, grads, moments_m, moments_v)]
    offs_s = packs[0][1]
    outs = _adamw(*[pk[0] for pk in packs], "adamw_small")
    for o, dst in zip(outs, (delta, new_m, new_v)):
        o = o.reshape(-1)
        for nm, off in zip(small_names, offs_s):
            dst[nm] = _take(o, off, weights[nm].shape)
    r_in, = exchanged(3, ex3, outs[0])
    adamw_big("w_in", [r_in], True, (w_in_t, m_w_in_t, v_w_in_t))

    return (loss, dh0[None], *[grads[n] for n in names], *[delta[n] for n in names],
            *[new_m[n] for n in names], *[new_v[n] for n in names])
```
